```python
import jax, jax.numpy as jnp
from jax import lax
import numpy as np

D_MODEL = 1024
BATCH = 8
SEQ = 2048
DEPTH = 2

CHUNK = 64
N_A_LAYERS = DEPTH // 2
N_B_LAYERS = DEPTH - N_A_LAYERS
POOL_WINDOWS = (2, 4, 8, 16)
N_POOL_GROUPS = len(POOL_WINDOWS)
POOL_GROUP_DIM = D_MODEL // N_POOL_GROUPS
HEAD_DIM = 64
N_HEADS = D_MODEL // HEAD_DIM
LEFT_CHUNKS = 8
BAND = (LEFT_CHUNKS + 1) * CHUNK
MAX_REL = 2 * CHUNK
N_REL = 2 * MAX_REL + 1
D_FF = ((8 * D_MODEL + 3 * 256 - 1) // (3 * 256)) * 256
EPS = 1e-6
NEG_INF = -1e30

kernel_name = "yoco_pool_chunked_relbias_attention_trunk"


def rmsnorm(x, g):
    xf = x.astype(jnp.float32)
    y = xf * lax.rsqrt(jnp.mean(xf * xf, axis=-1, keepdims=True) + EPS)
    return (y * g.astype(jnp.float32)).astype(x.dtype)


def swiglu(h, w_gate, w_up, w_down):
    return (jax.nn.silu(h @ w_gate) * (h @ w_up)) @ w_down


def pool_mixer(h, w, b, scale):
    bsz, slen, _ = h.shape
    hf = h.astype(jnp.float32)
    cs = jnp.concatenate([jnp.zeros((bsz, 1, D_MODEL), jnp.float32),
                          jnp.cumsum(hf, axis=1)], axis=1)
    t = jnp.arange(slen)
    pooled = []
    for g, win in enumerate(POOL_WINDOWS):
        csg = cs[:, :, g * POOL_GROUP_DIM:(g + 1) * POOL_GROUP_DIM]
        start = jnp.concatenate([jnp.zeros((bsz, win - 1, POOL_GROUP_DIM), jnp.float32),
                                 csg[:, :slen + 1 - win]], axis=1)
        cnt = jnp.minimum(t + 1, win).astype(jnp.float32)[None, :, None]
        pooled.append((csg[:, 1:] - start) / cnt)
    pooled = jnp.stack(pooled, axis=2)
    diff = (pooled - hf.reshape(bsz, slen, N_POOL_GROUPS, POOL_GROUP_DIM)).astype(h.dtype)
    y = jnp.einsum('bsgc,gcd->bsgd', diff, w) + b
    return y.reshape(bsz, slen, D_MODEL) * scale


def rel_bias_band(table):
    rel = jnp.arange(CHUNK)[:, None] + LEFT_CHUNKS * CHUNK - jnp.arange(BAND)[None, :]
    idx = jnp.clip(rel, -MAX_REL, MAX_REL) + MAX_REL
    return table[:, idx].astype(jnp.float32)


def chunk_band_attention(q, kp, vp, bias):
    bsz, slen = q.shape[0], q.shape[1]
    n_chunks = slen // CHUNK
    scale = HEAD_DIM ** -0.5
    key_idx = jnp.arange(BAND)

    def one_chunk(c):
        qc = lax.dynamic_slice_in_dim(q, c * CHUNK, CHUNK, axis=1)
        kc = lax.dynamic_slice_in_dim(kp, c * CHUNK, BAND, axis=1)
        vc = lax.dynamic_slice_in_dim(vp, c * CHUNK, BAND, axis=1)
        s = jnp.einsum('bqhd,bkhd->bhqk', qc, kc).astype(jnp.float32) * scale + bias
        valid = key_idx >= (LEFT_CHUNKS - c) * CHUNK
        s = jnp.where(valid, s, NEG_INF)
        p = jax.nn.softmax(s, axis=-1).astype(vc.dtype)
        return jnp.einsum('bhqk,bkhd->bqhd', p, vc)

    out = lax.map(one_chunk, jnp.arange(n_chunks))
    return jnp.moveaxis(out, 0, 1).reshape(bsz, slen, N_HEADS * HEAD_DIM)


def _fwd_setup_inputs(seed: int = 0) -> dict:
    key = jax.random.key(seed)
    ks = jax.random.split(key, 20)
    D, F, G, Cg, H, Dh = D_MODEL, D_FF, N_POOL_GROUPS, POOL_GROUP_DIM, N_HEADS, HEAD_DIM
    nrm = jax.random.normal
    f32 = jnp.float32
    return {
        "x": nrm(ks[0], (BATCH, SEQ, D), f32),
        "norm_mix_g": 1.0 + 0.02 * nrm(ks[1], (DEPTH, D), f32),
        "norm_ffn_g": 1.0 + 0.02 * nrm(ks[2], (DEPTH, D), f32),
        "pool_w": nrm(ks[3], (N_A_LAYERS, G, Cg, Cg), f32) * Cg ** -0.5,
        "pool_b": 0.02 * nrm(ks[4], (N_A_LAYERS, G, Cg), f32),
        "pool_scale": 1.0 + 0.02 * nrm(ks[5], (N_A_LAYERS, D), f32),
        "kv_norm_g": 1.0 + 0.02 * nrm(ks[6], (D,), f32),
        "w_k": nrm(ks[7], (D, H * Dh), f32) * D ** -0.5,
        "w_v": nrm(ks[8], (D, H * Dh), f32) * D ** -0.5,
        "k_norm_g": 1.0 + 0.02 * nrm(ks[9], (Dh,), f32),
        "w_q": nrm(ks[10], (N_B_LAYERS, D, H * Dh), f32) * D ** -0.5,
        "q_norm_g": 1.0 + 0.02 * nrm(ks[11], (N_B_LAYERS, Dh), f32),
        "rel_bias": 0.1 * nrm(ks[12], (N_B_LAYERS, H, N_REL), f32),
        "w_o": nrm(ks[13], (N_B_LAYERS, H * Dh, D), f32) * (H * Dh) ** -0.5,
        "w_gate": nrm(ks[14], (DEPTH, D, F), f32) * D ** -0.5,
        "w_up": nrm(ks[15], (DEPTH, D, F), f32) * D ** -0.5,
        "w_down": nrm(ks[16], (DEPTH, F, D), f32) * F ** -0.5,
    }


def _fwd_reference(x, norm_mix_g, norm_ffn_g, pool_w, pool_b, pool_scale, kv_norm_g,
              w_k, w_v, k_norm_g, w_q, q_norm_g, rel_bias, w_o,
              w_gate, w_up, w_down):
    bsz, slen, _ = x.shape
    pad = ((0, 0), (LEFT_CHUNKS * CHUNK, 0), (0, 0), (0, 0))
    kp = vp = None
    for l in range(DEPTH):
        h = rmsnorm(x, norm_mix_g[l])
        if l < N_A_LAYERS:
            x = x + pool_mixer(h, pool_w[l], pool_b[l], pool_scale[l])
        else:
            j = l - N_A_LAYERS
            if j == 0:
                hkv = rmsnorm(x, kv_norm_g)
                k = rmsnorm((hkv @ w_k).reshape(bsz, slen, N_HEADS, HEAD_DIM), k_norm_g)
                v = (hkv @ w_v).reshape(bsz, slen, N_HEADS, HEAD_DIM)
                kp = jnp.pad(k, pad)
                vp = jnp.pad(v, pad)
            q = rmsnorm((h @ w_q[j]).reshape(bsz, slen, N_HEADS, HEAD_DIM), q_norm_g[j])
            attn = chunk_band_attention(q, kp, vp, rel_bias_band(rel_bias[j]))
            x = x + attn @ w_o[j]
        x = x + swiglu(rmsnorm(x, norm_ffn_g[l]), w_gate[l], w_up[l], w_down[l])
    return x


import jax as _jax
import jax.numpy as _jnp

TWIN_FORMAT = 'train_step'
FWD_PARAMS = ['x', 'norm_mix_g', 'norm_ffn_g', 'pool_w', 'pool_b', 'pool_scale', 'kv_norm_g', 'w_k', 'w_v', 'k_norm_g', 'w_q', 'q_norm_g', 'rel_bias', 'w_o', 'w_gate', 'w_up', 'w_down']
TWIN_WEIGHTS = ['norm_mix_g', 'norm_ffn_g', 'pool_w', 'pool_b', 'pool_scale', 'kv_norm_g', 'w_k', 'w_v', 'k_norm_g', 'w_q', 'q_norm_g', 'rel_bias', 'w_o', 'w_gate', 'w_up', 'w_down']
TWIN_DIFF_INPUT = 'x'
TWIN_INPUTS = ['x', 'norm_mix_g', 'norm_ffn_g', 'pool_w', 'pool_b', 'pool_scale', 'kv_norm_g', 'w_k', 'w_v', 'k_norm_g', 'w_q', 'q_norm_g', 'rel_bias', 'w_o', 'w_gate', 'w_up', 'w_down', 'loss_target', 'm_norm_mix_g', 'm_norm_ffn_g', 'm_pool_w', 'm_pool_b', 'm_pool_scale', 'm_kv_norm_g', 'm_w_k', 'm_w_v', 'm_k_norm_g', 'm_w_q', 'm_q_norm_g', 'm_rel_bias', 'm_w_o', 'm_w_gate', 'm_w_up', 'm_w_down', 'v_norm_mix_g', 'v_norm_ffn_g', 'v_pool_w', 'v_pool_b', 'v_pool_scale', 'v_kv_norm_g', 'v_w_k', 'v_w_v', 'v_k_norm_g', 'v_w_q', 'v_q_norm_g', 'v_rel_bias', 'v_w_o', 'v_w_gate', 'v_w_up', 'v_w_down']
TWIN_OUTPUTS = ['loss', 'grad_x', 'grad_norm_mix_g', 'grad_norm_ffn_g', 'grad_pool_w', 'grad_pool_b', 'grad_pool_scale', 'grad_kv_norm_g', 'grad_w_k', 'grad_w_v', 'grad_k_norm_g', 'grad_w_q', 'grad_q_norm_g', 'grad_rel_bias', 'grad_w_o', 'grad_w_gate', 'grad_w_up', 'grad_w_down', 'delta_norm_mix_g', 'delta_norm_ffn_g', 'delta_pool_w', 'delta_pool_b', 'delta_pool_scale', 'delta_kv_norm_g', 'delta_w_k', 'delta_w_v', 'delta_k_norm_g', 'delta_w_q', 'delta_q_norm_g', 'delta_rel_bias', 'delta_w_o', 'delta_w_gate', 'delta_w_up', 'delta_w_down', 'new_m_norm_mix_g', 'new_m_norm_ffn_g', 'new_m_pool_w', 'new_m_pool_b', 'new_m_pool_scale', 'new_m_kv_norm_g', 'new_m_w_k', 'new_m_w_v', 'new_m_k_norm_g', 'new_m_w_q', 'new_m_q_norm_g', 'new_m_rel_bias', 'new_m_w_o', 'new_m_w_gate', 'new_m_w_up', 'new_m_w_down', 'new_v_norm_mix_g', 'new_v_norm_ffn_g', 'new_v_pool_w', 'new_v_pool_b', 'new_v_pool_scale', 'new_v_kv_norm_g', 'new_v_w_k', 'new_v_w_v', 'new_v_k_norm_g', 'new_v_w_q', 'new_v_q_norm_g', 'new_v_rel_bias', 'new_v_w_o', 'new_v_w_gate', 'new_v_w_up', 'new_v_w_down']
TWIN_LEAF_KINDS = {'loss': 'loss', 'grad_x': 'grad_x', 'grad_norm_mix_g': 'grad_w', 'grad_norm_ffn_g': 'grad_w', 'grad_pool_w': 'grad_w', 'grad_pool_b': 'grad_w', 'grad_pool_scale': 'grad_w', 'grad_kv_norm_g': 'grad_w', 'grad_w_k': 'grad_w', 'grad_w_v': 'grad_w', 'grad_k_norm_g': 'grad_w', 'grad_w_q': 'grad_w', 'grad_q_norm_g': 'grad_w', 'grad_rel_bias': 'grad_w', 'grad_w_o': 'grad_w', 'grad_w_gate': 'grad_w', 'grad_w_up': 'grad_w', 'grad_w_down': 'grad_w', 'delta_norm_mix_g': 'delta_w', 'delta_norm_ffn_g': 'delta_w', 'delta_pool_w': 'delta_w', 'delta_pool_b': 'delta_w', 'delta_pool_scale': 'delta_w', 'delta_kv_norm_g': 'delta_w', 'delta_w_k': 'delta_w', 'delta_w_v': 'delta_w', 'delta_k_norm_g': 'delta_w', 'delta_w_q': 'delta_w', 'delta_q_norm_g': 'delta_w', 'delta_rel_bias': 'delta_w', 'delta_w_o': 'delta_w', 'delta_w_gate': 'delta_w', 'delta_w_up': 'delta_w', 'delta_w_down': 'delta_w', 'new_m_norm_mix_g': 'new_m', 'new_m_norm_ffn_g': 'new_m', 'new_m_pool_w': 'new_m', 'new_m_pool_b': 'new_m', 'new_m_pool_scale': 'new_m', 'new_m_kv_norm_g': 'new_m', 'new_m_w_k': 'new_m', 'new_m_w_v': 'new_m', 'new_m_k_norm_g': 'new_m', 'new_m_w_q': 'new_m', 'new_m_q_norm_g': 'new_m', 'new_m_rel_bias': 'new_m', 'new_m_w_o': 'new_m', 'new_m_w_gate': 'new_m', 'new_m_w_up': 'new_m', 'new_m_w_down': 'new_m', 'new_v_norm_mix_g': 'new_v', 'new_v_norm_ffn_g': 'new_v', 'new_v_pool_w': 'new_v', 'new_v_pool_b': 'new_v', 'new_v_pool_scale': 'new_v', 'new_v_kv_norm_g': 'new_v', 'new_v_w_k': 'new_v', 'new_v_w_v': 'new_v', 'new_v_k_norm_g': 'new_v', 'new_v_w_q': 'new_v', 'new_v_q_norm_g': 'new_v', 'new_v_rel_bias': 'new_v', 'new_v_w_o': 'new_v', 'new_v_w_gate': 'new_v', 'new_v_w_up': 'new_v', 'new_v_w_down': 'new_v'}


def _forward(args):
    return _fwd_reference(*[args[k] for k in FWD_PARAMS])


def _output_shape():
    out = _jax.eval_shape(lambda: _forward(_fwd_setup_inputs(0)))
    return out.shape, out.dtype

N_MICROBATCH = 1
ADAM_LR = 0.001
ADAM_B1 = 0.9
ADAM_B2 = 0.999
ADAM_EPS = 1e-08
ADAM_WD = 0.01
ADAM_STEP = 10
PER_EXAMPLE_BATCH_AXIS = {'x': 0, 'loss_target': 0}
SHARED_INPUTS = []
_WEIGHT_DTYPES = {'norm_mix_g': _jnp.float32, 'norm_ffn_g': _jnp.float32, 'pool_w': _jnp.float32, 'pool_b': _jnp.float32, 'pool_scale': _jnp.float32, 'kv_norm_g': _jnp.float32, 'w_k': _jnp.float32, 'w_v': _jnp.float32, 'k_norm_g': _jnp.float32, 'w_q': _jnp.float32, 'q_norm_g': _jnp.float32, 'rel_bias': _jnp.float32, 'w_o': _jnp.float32, 'w_gate': _jnp.float32, 'w_up': _jnp.float32, 'w_down': _jnp.float32}
MOMENT_SCALE = {'norm_mix_g': 8.864821e+00, 'norm_ffn_g': 1.246021e+01, 'pool_w': 9.920038e-01, 'pool_b': 3.074491e+00, 'pool_scale': 1.254945e+01, 'kv_norm_g': 9.994373e-02, 'w_k': 4.166307e-02, 'w_v': 4.587483e-02, 'k_norm_g': 1.126789e+00, 'w_q': 4.169601e-02, 'q_norm_g': 1.131257e+00, 'rel_bias': 2.002926e-02, 'w_o': 4.509008e-02, 'w_gate': 1.284850e-01, 'w_up': 1.423315e-01, 'w_down': 2.182829e-01}


def _to_microbatches(a, axis):
    t = _jnp.moveaxis(a, axis, 0)
    t = t.reshape((N_MICROBATCH, t.shape[0] // N_MICROBATCH) + t.shape[1:])
    return _jnp.moveaxis(t, 1, axis + 1)


def setup_inputs(seed: int = 0) -> dict:
    inp = _fwd_setup_inputs(seed)
    key = _jax.random.fold_in(_jax.random.key(seed), 7919)
    shape, _ = _output_shape()
    out = dict(inp)
    out["loss_target"] = _jax.random.normal(_jax.random.fold_in(key, 0), shape, _jnp.float32)
    for i, name in enumerate(TWIN_WEIGHTS):
        w = inp[name].astype(_jnp.float32)
        if MOMENT_SCALE is None:
            s = _jnp.sqrt(_jnp.mean(_jnp.square(w)) + 1e-30)
        else:
            s = MOMENT_SCALE[name]
        km, kv = _jax.random.split(_jax.random.fold_in(key, i + 1))
        out[name] = w
        out["m_" + name] = s * _jax.random.normal(km, w.shape, _jnp.float32)
        out["v_" + name] = (s * s) * _jax.random.uniform(kv, w.shape, _jnp.float32, 0.5, 1.5)
    if N_MICROBATCH > 1:
        for name, axis in PER_EXAMPLE_BATCH_AXIS.items():
            out[name] = _to_microbatches(out[name], axis)
    return {'x': out['x'], 'norm_mix_g': out['norm_mix_g'], 'norm_ffn_g': out['norm_ffn_g'], 'pool_w': out['pool_w'], 'pool_b': out['pool_b'], 'pool_scale': out['pool_scale'], 'kv_norm_g': out['kv_norm_g'], 'w_k': out['w_k'], 'w_v': out['w_v'], 'k_norm_g': out['k_norm_g'], 'w_q': out['w_q'], 'q_norm_g': out['q_norm_g'], 'rel_bias': out['rel_bias'], 'w_o': out['w_o'], 'w_gate': out['w_gate'], 'w_up': out['w_up'], 'w_down': out['w_down'], 'loss_target': out['loss_target'], 'm_norm_mix_g': out['m_norm_mix_g'], 'm_norm_ffn_g': out['m_norm_ffn_g'], 'm_pool_w': out['m_pool_w'], 'm_pool_b': out['m_pool_b'], 'm_pool_scale': out['m_pool_scale'], 'm_kv_norm_g': out['m_kv_norm_g'], 'm_w_k': out['m_w_k'], 'm_w_v': out['m_w_v'], 'm_k_norm_g': out['m_k_norm_g'], 'm_w_q': out['m_w_q'], 'm_q_norm_g': out['m_q_norm_g'], 'm_rel_bias': out['m_rel_bias'], 'm_w_o': out['m_w_o'], 'm_w_gate': out['m_w_gate'], 'm_w_up': out['m_w_up'], 'm_w_down': out['m_w_down'], 'v_norm_mix_g': out['v_norm_mix_g'], 'v_norm_ffn_g': out['v_norm_ffn_g'], 'v_pool_w': out['v_pool_w'], 'v_pool_b': out['v_pool_b'], 'v_pool_scale': out['v_pool_scale'], 'v_kv_norm_g': out['v_kv_norm_g'], 'v_w_k': out['v_w_k'], 'v_w_v': out['v_w_v'], 'v_k_norm_g': out['v_k_norm_g'], 'v_w_q': out['v_w_q'], 'v_q_norm_g': out['v_q_norm_g'], 'v_rel_bias': out['v_rel_bias'], 'v_w_o': out['v_w_o'], 'v_w_gate': out['v_w_gate'], 'v_w_up': out['v_w_up'], 'v_w_down': out['v_w_down']}


def _loss(weights, diff, rest, loss_target):
    with _jax.named_scope("forward"):
        args = {**rest, TWIN_DIFF_INPUT: diff, **{k: w.astype(_WEIGHT_DTYPES[k]) for k, w in weights.items()}}
        y = _forward(args)
    with _jax.named_scope("loss_head"):
        err = _jnp.square(y.astype(_jnp.float32) - loss_target)
        return 0.5 * _jnp.sum(_jnp.mean(err, axis=-1)) if err.ndim else 0.5 * err


def _adamw(w, g, m, v):
    m = ADAM_B1 * m + (1.0 - ADAM_B1) * g
    v = ADAM_B2 * v + (1.0 - ADAM_B2) * _jnp.square(g)
    m_hat = m / (1.0 - ADAM_B1 ** ADAM_STEP)
    v_hat = v / (1.0 - ADAM_B2 ** ADAM_STEP)
    delta = -ADAM_LR * (m_hat / (_jnp.sqrt(v_hat) + ADAM_EPS) + ADAM_WD * w)
    return delta, m, v


def reference(x, norm_mix_g, norm_ffn_g, pool_w, pool_b, pool_scale, kv_norm_g, w_k, w_v, k_norm_g, w_q, q_norm_g, rel_bias, w_o, w_gate, w_up, w_down, loss_target, m_norm_mix_g, m_norm_ffn_g, m_pool_w, m_pool_b, m_pool_scale, m_kv_norm_g, m_w_k, m_w_v, m_k_norm_g, m_w_q, m_q_norm_g, m_rel_bias, m_w_o, m_w_gate, m_w_up, m_w_down, v_norm_mix_g, v_norm_ffn_g, v_pool_w, v_pool_b, v_pool_scale, v_kv_norm_g, v_w_k, v_w_v, v_k_norm_g, v_w_q, v_q_norm_g, v_rel_bias, v_w_o, v_w_gate, v_w_up, v_w_down):
    given = dict(x=x, norm_mix_g=norm_mix_g, norm_ffn_g=norm_ffn_g, pool_w=pool_w, pool_b=pool_b, pool_scale=pool_scale, kv_norm_g=kv_norm_g, w_k=w_k, w_v=w_v, k_norm_g=k_norm_g, w_q=w_q, q_norm_g=q_norm_g, rel_bias=rel_bias, w_o=w_o, w_gate=w_gate, w_up=w_up, w_down=w_down, loss_target=loss_target, m_norm_mix_g=m_norm_mix_g, m_norm_ffn_g=m_norm_ffn_g, m_pool_w=m_pool_w, m_pool_b=m_pool_b, m_pool_scale=m_pool_scale, m_kv_norm_g=m_kv_norm_g, m_w_k=m_w_k, m_w_v=m_w_v, m_k_norm_g=m_k_norm_g, m_w_q=m_w_q, m_q_norm_g=m_q_norm_g, m_rel_bias=m_rel_bias, m_w_o=m_w_o, m_w_gate=m_w_gate, m_w_up=m_w_up, m_w_down=m_w_down, v_norm_mix_g=v_norm_mix_g, v_norm_ffn_g=v_norm_ffn_g, v_pool_w=v_pool_w, v_pool_b=v_pool_b, v_pool_scale=v_pool_scale, v_kv_norm_g=v_kv_norm_g, v_w_k=v_w_k, v_w_v=v_w_v, v_k_norm_g=v_k_norm_g, v_w_q=v_w_q, v_q_norm_g=v_q_norm_g, v_rel_bias=v_rel_bias, v_w_o=v_w_o, v_w_gate=v_w_gate, v_w_up=v_w_up, v_w_down=v_w_down)
    weights = {n: given[n] for n in TWIN_WEIGHTS}
    shared = {n: given[n] for n in SHARED_INPUTS}
    per_example = {n: given[n] for n in ['x']}
    grad_fn = _jax.value_and_grad(_loss, argnums=(0, 1))

    def one_microbatch(ex, loss_target):
        ex = dict(ex)
        diff = ex.pop(TWIN_DIFF_INPUT)
        return grad_fn(weights, diff, {**shared, **ex}, loss_target)

    if N_MICROBATCH == 1:
        loss, (grad_w, grad_x) = one_microbatch(per_example, given["loss_target"])
    else:
        def body(carry, xs):
            loss_sum, grad_sum = carry
            l_k, (gw_k, gx_k) = one_microbatch(xs[0], xs[1])
            with _jax.named_scope("update"):
                return (loss_sum + l_k, _jax.tree.map(_jnp.add, grad_sum, gw_k)), gx_k

        init = (_jnp.zeros((), _jnp.float32), _jax.tree.map(_jnp.zeros_like, weights))
        (loss, grad_w), grad_x = _jax.lax.scan(body, init, (per_example, given["loss_target"]))
    with _jax.named_scope("update"):
        delta_w, new_m, new_v = {}, {}, {}
        for n in TWIN_WEIGHTS:
            delta_w[n], new_m[n], new_v[n] = _adamw(weights[n], grad_w[n], given["m_" + n], given["v_" + n])
    return (loss, grad_x, *[grad_w[n] for n in TWIN_WEIGHTS], *[delta_w[n] for n in TWIN_WEIGHTS],
            *[new_m[n] for n in TWIN_WEIGHTS], *[new_v[n] for n in TWIN_WEIGHTS])
```

```python
import functools

import jax
import jax.numpy as jnp
from jax import lax
from jax.experimental import pallas as pl
from jax.experimental.pallas import tpu as pltpu

F32 = jnp.float32
BF16 = jnp.bfloat16
MESH_ID = pl.DeviceIdType.MESH

N_DEV = 8
S = 2048
D = 1024
F = 2816
F_SHARD = F // N_DEV
D_SHARD = D // N_DEV
N_GROUPS = 4
GROUP = D // N_GROUPS
POOL_SHARD = GROUP // N_DEV
MAX_WIN = 16
HEAD = 64
N_HEADS = D // HEAD
CHUNK = 64
LEFT = 8
QB = 4 * CHUNK
KB = QB + LEFT * CHUNK
PADK = LEFT * CHUNK
TOEP = 1024
N_REL = 257
MAX_REL = 128
EPS = 1e-6
NEG_INF = -1e30
ATTN_SCALE = HEAD ** -0.5

ADAM_LR = 0.001
ADAM_B1 = 0.9
ADAM_B2 = 0.999
ADAM_EPS = 1e-08
ADAM_WD = 0.01
ADAM_STEP = 10

VMEM_LIMIT = 52 * 1024 * 1024

ANY = pl.BlockSpec(memory_space=pl.ANY)
VMEM = pl.BlockSpec(memory_space=pltpu.VMEM)


def _call(body, **kw):
    return pl.pallas_call(body, **kw)


def _params(*sem):
    return pltpu.CompilerParams(dimension_semantics=sem, vmem_limit_bytes=VMEM_LIMIT)


def _dot(a, b, dims):
    return lax.dot_general(a, b, (dims, ((), ())), preferred_element_type=F32)


def _nn(a, b):
    return _dot(a, b, ((1,), (0,)))


def _nt(a, b):
    return _dot(a, b, ((1,), (1,)))


def _tn(a, b):
    return _dot(a, b, ((0,), (0,)))


def _rstd(x):
    return lax.rsqrt(jnp.mean(x * x, axis=-1, keepdims=True) + EPS)


def _rms_bwd(dh, x, r, g):
    gd = dh * g
    return r * gd - x * (r * r * r) * jnp.mean(gd * x, axis=-1, keepdims=True)


def _colsum8(v):
    return jnp.broadcast_to(jnp.sum(v, axis=0, keepdims=True), (8, v.shape[1]))


def _seg_sum(v):
    lane = lax.broadcasted_iota(jnp.int32, v.shape, 1)
    k = 1
    while k < HEAD:
        up = pltpu.roll(v, v.shape[1] - k, axis=1)
        dn = pltpu.roll(v, k, axis=1)
        v = v + jnp.where((lane & k) == 0, up, dn)
        k *= 2
    return v


def _place():
    return lax.axis_index("x"), lax.axis_index("y"), lax.axis_index("c")


def _all_gather(shards, name):
    n = len(shards)

    def body(*refs):
        ins, outs = refs[:n], refs[n:2 * n]
        send_sems, recv_sems, local_sems = refs[2 * n:]
        x, y, c = _place()
        me, sibling = (x, y, c), (x, y, 1 - c)
        chips = [(1 - x, y), (x, 1 - y), (1 - x, 1 - y)]

        def slot(k, px, py, pc):
            return outs[k].at[4 * px + 2 * py + pc]

        def copy(k, s, block, to, src=None):
            return pltpu.make_async_remote_copy(
                src_ref=slot(k, *block) if src is None else src, dst_ref=slot(k, *block),
                send_sem=send_sems.at[k, s], recv_sem=recv_sems.at[k, s], device_id=to, device_id_type=MESH_ID)

        mine = [pltpu.make_async_copy(ins[k], slot(k, *me), local_sems.at[k]) for k in range(n)]
        for cp in mine:
            cp.start()
        first = []
        for k in range(n):
            first.append(copy(k, 0, me, sibling, src=ins[k]))
            first += [copy(k, 1 + j, me, (*chip, c), src=ins[k]) for j, chip in enumerate(chips)]
        for cp in first:
            cp.start()
        passed = []
        for j, chip in enumerate(chips):
            for k in range(n):
                copy(k, 1 + j, (*chip, c), me).wait_recv()
                fwd = copy(k, 4 + j, (*chip, c), sibling)
                fwd.start()
                passed.append(fwd)
        for k in range(n):
            copy(k, 0, sibling, me).wait_recv()
            for j, chip in enumerate(chips):
                copy(k, 4 + j, (*chip, 1 - c), me).wait_recv()
        for cp in first + passed:
            cp.wait_send()
        for cp in mine:
            cp.wait()

    return _call(
        body, name=name,
        out_shape=[jax.ShapeDtypeStruct((N_DEV,) + a.shape, a.dtype) for a in shards],
        in_specs=[ANY] * n, out_specs=[ANY] * n,
        scratch_shapes=[pltpu.SemaphoreType.DMA((n, 7)), pltpu.SemaphoreType.DMA((n, 7)),
                        pltpu.SemaphoreType.DMA((n,))],
    )(*shards)


def _peer(x, y, c, m):
    px = 1 - x if m & 4 else x
    py = 1 - y if m & 2 else y
    pc = 1 - c if m & 1 else c
    return px, py, pc


def _exchange_blocks(parts, name):
    n = len(parts)

    def body(*refs):
        ins, outs = refs[:n], refs[n:2 * n]
        send_sems, recv_sems, local_sems = refs[2 * n:]
        x, y, c = _place()
        me = 4 * x + 2 * y + c
        mine = [pltpu.make_async_copy(ins[k].at[me], outs[k].at[me], local_sems.at[k]) for k in range(n)]
        for cp in mine:
            cp.start()
        sent = []
        for m in range(1, N_DEV):
            px, py, pc = _peer(x, y, c, m)
            peer = 4 * px + 2 * py + pc
            for k in range(n):
                cp = pltpu.make_async_remote_copy(
                    src_ref=ins[k].at[peer], dst_ref=outs[k].at[me], send_sem=send_sems.at[k, m - 1],
                    recv_sem=recv_sems.at[k, m - 1], device_id=(px, py, pc), device_id_type=MESH_ID)
                cp.start()
                sent.append(cp)
        for m in range(1, N_DEV):
            px, py, pc = _peer(x, y, c, m)
            peer = 4 * px + 2 * py + pc
            for k in range(n):
                pltpu.make_async_remote_copy(
                    src_ref=ins[k].at[peer], dst_ref=outs[k].at[peer], send_sem=send_sems.at[k, m - 1],
                    recv_sem=recv_sems.at[k, m - 1], device_id=(px, py, pc), device_id_type=MESH_ID).wait_recv()
        for cp in sent:
            cp.wait_send()
        for cp in mine:
            cp.wait()

    return _call(
        body, name=name,
        out_shape=[jax.ShapeDtypeStruct(a.shape, a.dtype) for a in parts],
        in_specs=[ANY] * n, out_specs=[ANY] * n,
        scratch_shapes=[pltpu.SemaphoreType.DMA((n, 7)), pltpu.SemaphoreType.DMA((n, 7)),
                        pltpu.SemaphoreType.DMA((n,))],
    )(*parts)


def _all_reduce_small(pack, name):
    rows = pack.shape[0]

    def body(in_ref, out_ref, recv, send_sems, recv_sems):
        x, y, c = _place()
        me = 4 * x + 2 * y + c
        recv[me] = in_ref[...]
        sent = []
        for m in range(1, N_DEV):
            px, py, pc = _peer(x, y, c, m)
            cp = pltpu.make_async_remote_copy(
                src_ref=in_ref, dst_ref=recv.at[me], send_sem=send_sems.at[m - 1], recv_sem=recv_sems.at[m - 1],
                device_id=(px, py, pc), device_id_type=MESH_ID)
            cp.start()
            sent.append(cp)
        for m in range(1, N_DEV):
            px, py, pc = _peer(x, y, c, m)
            peer = 4 * px + 2 * py + pc
            pltpu.make_async_remote_copy(
                src_ref=in_ref, dst_ref=recv.at[peer], send_sem=send_sems.at[m - 1], recv_sem=recv_sems.at[m - 1],
                device_id=(px, py, pc), device_id_type=MESH_ID).wait_recv()
        acc = recv[0]
        for d in range(1, N_DEV):
            acc = acc + recv[d]
        out_ref[...] = acc
        for cp in sent:
            cp.wait_send()

    return _call(
        body, name=name, out_shape=jax.ShapeDtypeStruct(pack.shape, F32), in_specs=[VMEM], out_specs=VMEM,
        scratch_shapes=[pltpu.VMEM((N_DEV, rows, pack.shape[1]), F32), pltpu.SemaphoreType.DMA((7,)),
                        pltpu.SemaphoreType.DMA((7,))],
        compiler_params=pltpu.CompilerParams(vmem_limit_bytes=VMEM_LIMIT),
    )(pack)


POOL_TS = 256


def _pool_counts(first_row, rows, win):
    t = first_row + lax.broadcasted_iota(jnp.int32, (rows, 1), 0)
    return jnp.minimum(t + 1, win).astype(F32)


def _pool_fwd(x, g, w, b, scale):
    nt = S // POOL_TS

    def body(x_ref, g_ref, w_ref, b_ref, s_ref, out_ref, diff_ref, ext):
        i = pl.program_id(0)

        @pl.when(i == 0)
        def _():
            ext[0:MAX_WIN, :] = jnp.zeros((MAX_WIN, D), F32)

        @pl.when(i > 0)
        def _():
            ext[0:MAX_WIN, :] = ext[POOL_TS:POOL_TS + MAX_WIN, :]

        xv = x_ref[...]
        h = xv * _rstd(xv) * g_ref[...]
        ext[MAX_WIN:, :] = h
        for gi in range(N_GROUPS):
            win = 2 << gi
            cols = slice(gi * GROUP, (gi + 1) * GROUP)
            sm = ext[:, cols]
            k = 1
            while k < win:
                sm = sm + pltpu.roll(sm, k, axis=0)
                k *= 2
            pooled = sm[MAX_WIN:, :] / _pool_counts(i * POOL_TS, POOL_TS, win)
            diff = (pooled - h[:, cols]).astype(BF16)
            yv = (_nn(diff, w_ref[gi]) + b_ref[:, cols]) * s_ref[:, cols]
            out_ref[:, cols] = xv[:, cols] + yv
            diff_ref[:, cols] = diff

    row = pl.BlockSpec((1, D), lambda i: (0, 0))
    tile = pl.BlockSpec((POOL_TS, D), lambda i: (i, 0))
    return _call(
        body, name="pool_fwd", grid=(nt,),
        in_specs=[tile, row, pl.BlockSpec((N_GROUPS, GROUP, GROUP), lambda i: (0, 0, 0)), row, row],
        out_specs=[tile, tile],
        out_shape=[jax.ShapeDtypeStruct((S, D), F32), jax.ShapeDtypeStruct((S, D), BF16)],
        scratch_shapes=[pltpu.VMEM((POOL_TS + MAX_WIN, D), F32)],
        compiler_params=_params("arbitrary"),
    )(x, g, w, b, scale)


def _pool_bwd(dy, x, diff, g, w, b, scale):
    nt = S // POOL_TS

    def body(dy_ref, x_ref, diff_ref, g_ref, w_ref, b_ref, s_ref, gx_ref, dw_ref, db_ref, ds_ref, dg_ref, ext, dh):
        i = pl.program_id(0)
        first_row = (nt - 1 - i) * POOL_TS

        @pl.when(i == 0)
        def _():
            ext[POOL_TS:, :] = jnp.zeros((MAX_WIN, D), F32)
            dw_ref[...] = jnp.zeros(dw_ref.shape, F32)
            db_ref[...] = jnp.zeros(db_ref.shape, F32)
            ds_ref[...] = jnp.zeros(ds_ref.shape, F32)
            dg_ref[...] = jnp.zeros(dg_ref.shape, F32)

        @pl.when(i > 0)
        def _():
            ext[POOL_TS:, :] = ext[0:MAX_WIN, :]

        dyv = dy_ref[...]
        for gi in range(N_GROUPS):
            win = 2 << gi
            cols = slice(gi * GROUP, (gi + 1) * GROUP)
            dfb = diff_ref[:, cols]
            z = _nn(dfb, w_ref[gi]) + b_ref[:, cols]
            dyg = dyv[:, cols]
            ds_ref[:, cols] += _colsum8(dyg * z)
            dz = dyg * s_ref[:, cols]
            db_ref[:, cols] += _colsum8(dz)
            dzb = dz.astype(BF16)
            dw_ref[gi] += _tn(dfb, dzb)
            ddiff = _nt(dzb, w_ref[gi])
            ext[0:POOL_TS, cols] = ddiff / _pool_counts(first_row, POOL_TS, win)
            sm = ext[:, cols]
            k = 1
            while k < win:
                sm = sm + pltpu.roll(sm, POOL_TS + MAX_WIN - k, axis=0)
                k *= 2
            dh[:, cols] = sm[0:POOL_TS, :] - ddiff
        xv = x_ref[...]
        r = _rstd(xv)
        gv = g_ref[...]
        dhv = dh[...]
        dg_ref[...] += _colsum8(dhv * xv * r)
        gx_ref[...] = dyv + _rms_bwd(dhv, xv, r, gv)

    row = pl.BlockSpec((1, D), lambda i: (0, 0))
    tile = pl.BlockSpec((POOL_TS, D), lambda i: (nt - 1 - i, 0))
    acc = pl.BlockSpec((8, D), lambda i: (0, 0))
    wspec = pl.BlockSpec((N_GROUPS, GROUP, GROUP), lambda i: (0, 0, 0))
    return _call(
        body, name="pool_bwd", grid=(nt,),
        in_specs=[tile, tile, tile, row, wspec, row, row],
        out_specs=[tile, wspec, acc, acc, acc],
        out_shape=[jax.ShapeDtypeStruct((S, D), F32), jax.ShapeDtypeStruct((N_GROUPS, GROUP, GROUP), F32),
                   jax.ShapeDtypeStruct((8, D), F32), jax.ShapeDtypeStruct((8, D), F32),
                   jax.ShapeDtypeStruct((8, D), F32)],
        scratch_shapes=[pltpu.VMEM((POOL_TS + MAX_WIN, D), F32), pltpu.VMEM((POOL_TS, D), F32)],
        compiler_params=_params("arbitrary"),
    )(dy, x, diff, g, w, b, scale)


FFN_TS = 512
FFN_TF = 256


def _ffn_fwd(x, g, wg_t, wu_t, wd, target=None):
    ni, nj = S // FFN_TS, F // FFN_TF
    with_loss = target is not None

    def body(*refs):
        if with_loss:
            x_ref, g_ref, wg_ref, wu_ref, wd_ref, t_ref, out_ref, loss_ref, h_ref, gg_ref, uu_ref, hs, acc = refs
        else:
            x_ref, g_ref, wg_ref, wu_ref, wd_ref, out_ref, h_ref, gg_ref, uu_ref, hs, acc = refs
        i, j = pl.program_id(0), pl.program_id(1)

        @pl.when(j == 0)
        def _():
            xv = x_ref[...]
            hb = (xv * _rstd(xv) * g_ref[...]).astype(BF16)
            hs[...] = hb
            h_ref[...] = hb
            acc[...] = jnp.zeros(acc.shape, F32)

        hb = hs[...]
        gg = _nt(hb, wg_ref[...])
        uu = _nt(hb, wu_ref[...])
        gg_ref[...] = gg
        uu_ref[...] = uu
        a = (gg * jax.nn.sigmoid(gg) * uu).astype(BF16)
        acc[...] += _nn(a, wd_ref[...])

        @pl.when(j == nj - 1)
        def _():
            yv = x_ref[...] + acc[...]
            if with_loss:
                err = yv - t_ref[...]
                out_ref[...] = err * (1.0 / D)
                part = jnp.sum(err * err) * (0.5 / D)

                @pl.when(i == 0)
                def _():
                    loss_ref[...] = jnp.zeros(loss_ref.shape, F32)

                loss_ref[...] += jnp.broadcast_to(part, loss_ref.shape)
            else:
                out_ref[...] = yv

    xt = pl.BlockSpec((FFN_TS, D), lambda i, j: (i, 0))
    row = pl.BlockSpec((1, D), lambda i, j: (0, 0))
    wt = pl.BlockSpec((FFN_TF, D), lambda i, j: (j, 0))
    gt = pl.BlockSpec((FFN_TS, FFN_TF), lambda i, j: (i, j))
    in_specs = [xt, row, wt, wt, wt] + ([xt] if with_loss else [])
    out_specs = [xt] + ([pl.BlockSpec((8, 128), lambda i, j: (0, 0))] if with_loss else []) + [xt, gt, gt]
    out_shape = ([jax.ShapeDtypeStruct((S, D), F32)] + ([jax.ShapeDtypeStruct((8, 128), F32)] if with_loss else [])
                 + [jax.ShapeDtypeStruct((S, D), BF16), jax.ShapeDtypeStruct((S, F), F32),
                    jax.ShapeDtypeStruct((S, F), F32)])
    args = (x, g, wg_t, wu_t, wd) + ((target,) if with_loss else ())
    return _call(
        body, name="ffn_fwd_loss" if with_loss else "ffn_fwd", grid=(ni, nj),
        in_specs=in_specs, out_specs=out_specs, out_shape=out_shape,
        scratch_shapes=[pltpu.VMEM((FFN_TS, D), BF16), pltpu.VMEM((FFN_TS, D), F32)],
        compiler_params=_params("arbitrary", "arbitrary"),
    )(*args)


def _ffn_bwd(dout, h, gg, uu, wg_t, wu_t, wd, name):
    ni, nj = S // FFN_TS, F // FFN_TF

    def body(do_ref, h_ref, gg_ref, uu_ref, wg_ref, wu_ref, wd_ref, dh_ref, dwg_ref, dwu_ref, dwd_ref, acc):
        j, i = pl.program_id(0), pl.program_id(1)
        dob = do_ref[...].astype(BF16)
        hb = h_ref[...]
        gv, uv = gg_ref[...], uu_ref[...]
        da = _nt(dob, wd_ref[...])
        sg = jax.nn.sigmoid(gv)
        sl = gv * sg
        ab = (sl * uv).astype(BF16)
        dub = (da * sl).astype(BF16)
        dgb = (da * uv * (sg * (1.0 + gv * (1.0 - sg)))).astype(BF16)

        @pl.when(i == 0)
        def _():
            acc[...] = jnp.zeros(acc.shape, F32)

        acc[0] += _tn(dgb, hb)
        acc[1] += _tn(dub, hb)
        acc[2] += _tn(ab, dob)
        contrib = _nn(dgb, wg_ref[...]) + _nn(dub, wu_ref[...])
        rows = pl.ds(pl.multiple_of(i * FFN_TS, FFN_TS), FFN_TS)

        @pl.when(j == 0)
        def _():
            dh_ref[rows, :] = contrib

        @pl.when(j > 0)
        def _():
            dh_ref[rows, :] += contrib

        @pl.when(i == ni - 1)
        def _():
            dwg_ref[...] = acc[0].astype(BF16)
            dwu_ref[...] = acc[1].astype(BF16)
            dwd_ref[...] = acc[2].astype(BF16)

    xt = pl.BlockSpec((FFN_TS, D), lambda j, i: (i, 0))
    wt = pl.BlockSpec((FFN_TF, D), lambda j, i: (j, 0))
    gt = pl.BlockSpec((FFN_TS, FFN_TF), lambda j, i: (i, j))
    return _call(
        body, name=name, grid=(nj, ni),
        in_specs=[xt, xt, gt, gt, wt, wt, wt],
        out_specs=[pl.BlockSpec((S, D), lambda j, i: (0, 0)), wt, wt, wt],
        out_shape=[jax.ShapeDtypeStruct((S, D), F32)] + [jax.ShapeDtypeStruct((F, D), BF16)] * 3,
        scratch_shapes=[pltpu.VMEM((3, FFN_TF, D), F32)],
        compiler_params=_params("arbitrary", "arbitrary"),
    )(dout, h, gg, uu, wg_t, wu_t, wd)


NORM_TS = 256


def _norm_bwd(dres, x, dhs, gs, name):
    n = len(dhs)
    nt = S // NORM_TS

    def body(*refs):
        dres_ref, x_ref = refs[:2]
        dh_refs, g_refs = refs[2:2 + n], refs[2 + n:2 + 2 * n]
        dx_ref, dg_refs = refs[2 + 2 * n], refs[3 + 2 * n:]
        i = pl.program_id(0)
        xv = x_ref[...]
        r = _rstd(xv)
        dx = dres_ref[...]
        for k in range(n):
            dhv = dh_refs[k][...]
            dx = dx + _rms_bwd(dhv, xv, r, g_refs[k][...])

            @pl.when(i == 0)
            def _():
                dg_refs[k][...] = jnp.zeros((8, D), F32)

            dg_refs[k][...] += _colsum8(dhv * xv * r)
        dx_ref[...] = dx

    tile = pl.BlockSpec((NORM_TS, D), lambda i: (i, 0))
    row = pl.BlockSpec((1, D), lambda i: (0, 0))
    acc = pl.BlockSpec((8, D), lambda i: (0, 0))
    return _call(
        body, name=name, grid=(nt,),
        in_specs=[tile, tile] + [tile] * n + [row] * n,
        out_specs=[tile] + [acc] * n,
        out_shape=[jax.ShapeDtypeStruct((S, D), F32)] + [jax.ShapeDtypeStruct((8, D), F32)] * n,
        compiler_params=_params("arbitrary"),
    )(dres, x, *dhs, *gs)


def _mm(a, b, mode, out_dtype, name, add=None):
    if mode == "nn":
        (m, kd), n = a.shape, b.shape[1]
    elif mode == "nt":
        (m, kd), n = a.shape, b.shape[0]
    else:
        (kd, m), n = a.shape, b.shape[1]
    tm, tn, tk = min(m, 512), min(n, 512), min(kd, 1024)
    nk = kd // tk
    dot = {"nn": _nn, "nt": _nt, "tn": _tn}[mode]

    def body(*refs):
        if add is None:
            a_ref, b_ref, o_ref, acc = refs
        else:
            a_ref, b_ref, add_ref, o_ref, acc = refs
        k = pl.program_id(2)

        @pl.when(k == 0)
        def _():
            acc[...] = jnp.zeros(acc.shape, F32)

        acc[...] += dot(a_ref[...].astype(BF16), b_ref[...].astype(BF16))

        @pl.when(k == nk - 1)
        def _():
            res = acc[...]
            if add is not None:
                res = res + add_ref[...]
            o_ref[...] = res.astype(out_dtype)

    if mode == "tn":
        a_spec = pl.BlockSpec((tk, tm), lambda i, j, k: (k, i))
        b_spec = pl.BlockSpec((tk, tn), lambda i, j, k: (k, j))
    else:
        a_spec = pl.BlockSpec((tm, tk), lambda i, j, k: (i, k))
        b_spec = (pl.BlockSpec((tk, tn), lambda i, j, k: (k, j)) if mode == "nn"
                  else pl.BlockSpec((tn, tk), lambda i, j, k: (j, k)))
    o_spec = pl.BlockSpec((tm, tn), lambda i, j, k: (i, j))
    in_specs = [a_spec, b_spec] + ([o_spec] if add is not None else [])
    args = (a, b) + ((add,) if add is not None else ())
    return _call(
        body, name=name, grid=(m // tm, n // tn, nk), in_specs=in_specs, out_specs=o_spec,
        out_shape=jax.ShapeDtypeStruct((m, n), out_dtype), scratch_shapes=[pltpu.VMEM((tm, tn), F32)],
        compiler_params=_params("parallel", "parallel", "arbitrary"),
    )(*args)


PROJ_TS = 256


def _kvq_proj(x, g_kv, g_mix, wk, wv, wq, gk, gq):
    def body(x_ref, gkv_ref, gmix_ref, wk_ref, wv_ref, wq_ref, gk_ref, gq_ref,
             hkv_ref, h1_ref, kpre_ref, qpre_ref, k_ref, v_ref, q_ref):
        xv = x_ref[...]
        xr = xv * _rstd(xv)
        hkv = (xr * gkv_ref[...]).astype(BF16)
        h1 = (xr * gmix_ref[...]).astype(BF16)
        hkv_ref[...] = hkv
        h1_ref[...] = h1
        kpre = _nn(hkv, wk_ref[...])
        qpre = _nn(h1, wq_ref[...])
        kpre_ref[...] = kpre
        qpre_ref[...] = qpre
        v_ref[...] = _nn(hkv, wv_ref[...]).astype(BF16)
        rk = lax.rsqrt(_seg_sum(kpre * kpre) * (1.0 / HEAD) + EPS)
        k_ref[...] = (kpre * rk * gk_ref[...]).astype(BF16)
        rq = lax.rsqrt(_seg_sum(qpre * qpre) * (1.0 / HEAD) + EPS)
        q_ref[...] = (qpre * rq * gq_ref[...]).astype(BF16)

    tile = pl.BlockSpec((PROJ_TS, D), lambda i: (i, 0))
    row = pl.BlockSpec((1, D), lambda i: (0, 0))
    wspec = pl.BlockSpec((D, D), lambda i: (0, 0))
    bf = jax.ShapeDtypeStruct((S, D), BF16)
    ff = jax.ShapeDtypeStruct((S, D), F32)
    return _call(
        body, name="kvq_proj", grid=(S // PROJ_TS,),
        in_specs=[tile, row, row, wspec, wspec, wspec, row, row],
        out_specs=[tile] * 7, out_shape=[bf, bf, ff, ff, bf, bf, bf],
        compiler_params=_params("arbitrary"),
    )(x, g_kv, g_mix, wk, wv, wq, gk, gq)


def _head_norm_bwd(dout, pre, hg, name, row_off=0):
    nt = S // PROJ_TS

    def body(do_ref, pre_ref, hg_ref, dpre_ref, dhg_ref, acc):
        i = pl.program_id(0)
        dov, pv, hgv = do_ref[...], pre_ref[...], hg_ref[...]
        r = lax.rsqrt(_seg_sum(pv * pv) * (1.0 / HEAD) + EPS)
        gd = dov * hgv
        dpre = r * gd - pv * (r * r * r) * (_seg_sum(gd * pv) * (1.0 / HEAD))
        dpre_ref[...] = dpre.astype(BF16)

        @pl.when(i == 0)
        def _():
            acc[...] = jnp.zeros(acc.shape, F32)

        acc[...] += _colsum8(dov * pv * r)

        @pl.when(i == nt - 1)
        def _():
            full = acc[...]
            fold = full[:, 0:128]
            for blk in range(1, D // 128):
                fold = fold + full[:, blk * 128:(blk + 1) * 128]
            dhg_ref[...] = fold + pltpu.roll(fold, HEAD, axis=1)

    tile = pl.BlockSpec((PROJ_TS, D), lambda i: (i, 0))
    return _call(
        body, name=name, grid=(nt,),
        in_specs=[pl.BlockSpec((PROJ_TS, D), lambda i: (i + row_off, 0)), tile, pl.BlockSpec((1, D), lambda i: (0, 0))],
        out_specs=[tile, pl.BlockSpec((8, 128), lambda i: (0, 0))],
        out_shape=[jax.ShapeDtypeStruct((S, D), BF16), jax.ShapeDtypeStruct((8, 128), F32)],
        scratch_shapes=[pltpu.VMEM((8, D), F32)],
        compiler_params=_params("arbitrary"),
    )(dout, pre, hg)


def _toeplitz_from_table(table):
    far = jnp.broadcast_to(table[:, N_REL - 1:], (N_HEADS, PADK - MAX_REL + 1))
    near = table[:, N_REL - 2::-1]
    past = jnp.broadcast_to(table[:, 0:1], (N_HEADS, MAX_REL))
    wrap = jnp.broadcast_to(table[:, N_REL - 1:], (N_HEADS, QB - 1))
    return jnp.concatenate([far, near, past, wrap], axis=1).reshape(N_HEADS, 1, TOEP)


def _table_grad_from_toeplitz(dtp, seg):
    lo = PADK - MAX_REL + 1
    near = dtp[:, lo + N_REL - 3:lo - 1:-1]
    return jnp.concatenate([seg[:, 1:2], near, seg[:, 0:1]], axis=1)


def _bias_band(tp):
    def body(tp_ref, out_ref):
        bv = jnp.broadcast_to(tp_ref[0], (QB, TOEP))
        row = lax.broadcasted_iota(jnp.int32, (QB, TOEP), 0)
        k = 1
        while k < QB:
            bv = jnp.where((row & k) != 0, pltpu.roll(bv, k, axis=1), bv)
            k *= 2
        out_ref[0] = bv[:, 0:KB]

    return _call(
        body, name="bias_band", grid=(N_HEADS,),
        in_specs=[pl.BlockSpec((1, 1, TOEP), lambda h: (h, 0, 0))],
        out_specs=pl.BlockSpec((1, QB, KB), lambda h: (h, 0, 0)),
        out_shape=jax.ShapeDtypeStruct((N_HEADS, QB, KB), F32),
        compiler_params=_params("parallel"),
    )(tp)


def _bias_grad(dband):
    lo, hi = PADK - MAX_REL + 1, PADK + MAX_REL

    def body(db_ref, dtp_ref, seg_ref):
        bv = jnp.concatenate([db_ref[0], jnp.zeros((QB, TOEP - KB), F32)], axis=1)
        row = lax.broadcasted_iota(jnp.int32, (QB, TOEP), 0)
        k = 1
        while k < QB:
            bv = jnp.where((row & k) != 0, pltpu.roll(bv, TOEP - k, axis=1), bv)
            k *= 2
        col = jnp.sum(bv, axis=0, keepdims=True)
        dtp_ref[0] = col
        u = lax.broadcasted_iota(jnp.int32, (1, TOEP), 1)
        far = jnp.sum(jnp.where((u < lo) | (u > hi + MAX_REL), col, 0.0))
        past = jnp.sum(jnp.where((u >= hi) & (u <= hi + MAX_REL), col, 0.0))
        lane = lax.broadcasted_iota(jnp.int32, (1, 128), 1)
        seg_ref[0] = jnp.where(lane == 0, far, jnp.where(lane == 1, past, 0.0))

    return _call(
        body, name="bias_grad", grid=(N_HEADS,),
        in_specs=[pl.BlockSpec((1, QB, KB), lambda h: (h, 0, 0))],
        out_specs=[pl.BlockSpec((1, 1, TOEP), lambda h: (h, 0, 0)), pl.BlockSpec((1, 1, 128), lambda h: (h, 0, 0))],
        out_shape=[jax.ShapeDtypeStruct((N_HEADS, 1, TOEP), F32), jax.ShapeDtypeStruct((N_HEADS, 1, 128), F32)],
        compiler_params=_params("parallel"),
    )(dband)


N_QB = S // QB
N_HP = D // 128


def _band_mask(cb):
    qc = lax.broadcasted_iota(jnp.int32, (QB, KB), 0) >> 6
    p = lax.broadcasted_iota(jnp.int32, (QB, KB), 1)
    kc = p >> 6
    return (kc >= qc) & (kc <= qc + LEFT) & (p + cb * QB >= PADK)


def _half_mask(hh):
    lane = lax.broadcasted_iota(jnp.int32, (1, 128), 1)
    return jnp.where((lane < HEAD) == (hh == 0), 1.0, 0.0).astype(BF16)


def _probs(qh, kb, bias, mask):
    sc = _nt(qh, kb) * ATTN_SCALE + bias
    sc = jnp.where(mask, sc, NEG_INF)
    e = jnp.exp(sc - jnp.max(sc, axis=-1, keepdims=True))
    return e / jnp.sum(e, axis=-1, keepdims=True)


def _attn_fwd(q, kp, vp, bias):
    def body(q_ref, k_ref, v_ref, b_ref, o_ref):
        cb = pl.program_id(1)
        band = pl.ds(pl.multiple_of(cb * QB, QB), KB)
        kb, vb = k_ref[band, :], v_ref[band, :]
        qv = q_ref[...]
        mask = _band_mask(cb)
        low = lax.broadcasted_iota(jnp.int32, (QB, 128), 1) < HEAD
        outs = []
        for hh in range(2):
            pb = _probs(qv * _half_mask(hh), kb, b_ref[hh], mask).astype(BF16)
            outs.append(_nn(pb, vb))
        o_ref[...] = jnp.where(low, outs[0], outs[1]).astype(BF16)

    qspec = pl.BlockSpec((QB, 128), lambda hp, cb: (cb, hp))
    kspec = pl.BlockSpec((PADK + S, 128), lambda hp, cb: (0, hp))
    return _call(
        body, name="attn_fwd", grid=(N_HP, N_QB),
        in_specs=[qspec, kspec, kspec, pl.BlockSpec((2, QB, KB), lambda hp, cb: (hp, 0, 0))],
        out_specs=qspec, out_shape=jax.ShapeDtypeStruct((S, D), BF16),
        compiler_params=_params("parallel", "arbitrary"),
    )(q, kp, vp, bias)


def _attn_bwd(q, kp, vp, bias, do):
    def body(q_ref, k_ref, v_ref, b_ref, do_ref, dq_ref, dk_ref, dv_ref, db_ref):
        cb = pl.program_id(1)

        @pl.when(cb == 0)
        def _():
            dk_ref[...] = jnp.zeros(dk_ref.shape, F32)
            dv_ref[...] = jnp.zeros(dv_ref.shape, F32)
            db_ref[...] = jnp.zeros(db_ref.shape, F32)

        band = pl.ds(pl.multiple_of(cb * QB, QB), KB)
        kb, vb = k_ref[band, :], v_ref[band, :]
        qv, dov = q_ref[...], do_ref[...]
        mask = _band_mask(cb)
        low = lax.broadcasted_iota(jnp.int32, (QB, 128), 1) < HEAD
        dq = jnp.zeros((QB, 128), F32)
        dkb = jnp.zeros((KB, 128), F32)
        dvb = jnp.zeros((KB, 128), F32)
        for hh in range(2):
            sel = low if hh == 0 else jnp.logical_not(low)
            qh = qv * _half_mask(hh)
            doh = dov * _half_mask(hh)
            p = _probs(qh, kb, b_ref[hh], mask)
            dp = _nt(doh, vb)
            dvb = dvb + _tn(p.astype(BF16), doh)
            ds = p * (dp - jnp.sum(dp * p, axis=-1, keepdims=True))
            db_ref[hh] += ds
            dsb = (ds * ATTN_SCALE).astype(BF16)
            dq = dq + jnp.where(sel, _nn(dsb, kb), 0.0)
            dkb = dkb + _tn(dsb, qh)
        dq_ref[...] = dq
        dk_ref[band, :] += dkb
        dv_ref[band, :] += dvb

    qspec = pl.BlockSpec((QB, 128), lambda hp, cb: (cb, hp))
    kspec = pl.BlockSpec((PADK + S, 128), lambda hp, cb: (0, hp))
    bspec = pl.BlockSpec((2, QB, KB), lambda hp, cb: (hp, 0, 0))
    kf = jax.ShapeDtypeStruct((PADK + S, D), F32)
    return _call(
        body, name="attn_bwd", grid=(N_HP, N_QB),
        in_specs=[qspec, kspec, kspec, bspec, qspec],
        out_specs=[qspec, kspec, kspec, bspec],
        out_shape=[jax.ShapeDtypeStruct((S, D), F32), kf, kf, jax.ShapeDtypeStruct((N_HEADS, QB, KB), F32)],
        compiler_params=_params("parallel", "arbitrary"),
    )(q, kp, vp, bias, do)


def _adam_math(w, g, m, v):
    m = ADAM_B1 * m + (1.0 - ADAM_B1) * g
    v = ADAM_B2 * v + (1.0 - ADAM_B2) * (g * g)
    m_hat = m / (1.0 - ADAM_B1 ** ADAM_STEP)
    v_hat = v / (1.0 - ADAM_B2 ** ADAM_STEP)
    delta = -ADAM_LR * (m_hat / (jnp.sqrt(v_hat) + ADAM_EPS) + ADAM_WD * w)
    return delta, m, v


def _sum_parts(parts, name):
    _, r, c = parts.shape
    tr = r // 2 if r % 16 == 0 and r > 64 else r

    def body(p_ref, o_ref):
        acc = p_ref[0].astype(F32)
        for d in range(1, N_DEV):
            acc = acc + p_ref[d].astype(F32)
        o_ref[...] = acc

    return _call(
        body, name=name, grid=(r // tr,),
        in_specs=[pl.BlockSpec((N_DEV, tr, c), lambda i: (0, i, 0))],
        out_specs=pl.BlockSpec((tr, c), lambda i: (i, 0)),
        out_shape=jax.ShapeDtypeStruct((r, c), F32), compiler_params=_params("parallel"),
    )(parts)


def _adam(w, g, m, v, name):
    r, c = w.shape
    tr = r
    for cand in (256, 176, 128, 64, 32, 16, 8):
        if r % cand == 0:
            tr = cand
            break

    def body(w_ref, g_ref, m_ref, v_ref, d_ref, nm_ref, nv_ref):
        delta, nm, nv = _adam_math(w_ref[...], g_ref[...], m_ref[...], v_ref[...])
        d_ref[...] = delta
        nm_ref[...] = nm
        nv_ref[...] = nv

    spec = pl.BlockSpec((tr, c), lambda i: (i, 0))
    sd = jax.ShapeDtypeStruct((r, c), F32)
    return _call(
        body, name=name, grid=(r // tr,), in_specs=[spec] * 4, out_specs=[spec] * 3, out_shape=[sd] * 3,
        compiler_params=_params("parallel"),
    )(w, g, m, v)


def _pad8(rows):
    return jnp.pad(rows, ((0, 8 - rows.shape[0]), (0, 0)))


def kernel(x, norm_mix_g, norm_ffn_g, pool_w, pool_b, pool_scale, kv_norm_g, w_k, w_v, k_norm_g, w_q, q_norm_g, rel_bias, w_o, w_gate, w_up, w_down, loss_target, m_norm_mix_g, m_norm_ffn_g, m_pool_w, m_pool_b, m_pool_scale, m_kv_norm_g, m_w_k, m_w_v, m_k_norm_g, m_w_q, m_q_norm_g, m_rel_bias, m_w_o, m_w_gate, m_w_up, m_w_down, v_norm_mix_g, v_norm_ffn_g, v_pool_w, v_pool_b, v_pool_scale, v_kv_norm_g, v_w_k, v_w_v, v_k_norm_g, v_w_q, v_q_norm_g, v_rel_bias, v_w_o, v_w_gate, v_w_up, v_w_down):
    assert x.shape == (1, S, D) and w_gate.shape == (2, D, F_SHARD) and w_k.shape == (D_SHARD, D)
    xin, target = x[0], loss_target[0]

    shards = []
    for layer in range(2):
        shards += [w_gate[layer].T.astype(BF16), w_up[layer].T.astype(BF16), w_down[layer].astype(BF16)]
    shards += [w_k.astype(BF16), w_v.astype(BF16), w_q[0].astype(BF16), w_o[0].astype(BF16)]
    shards.append(pool_w[0].astype(BF16).reshape(N_GROUPS * POOL_SHARD, GROUP))
    small = jnp.concatenate([pool_b[0].reshape(1, N_GROUPS * POOL_SHARD), pool_scale], axis=1)
    shards.append(_pad8(small))
    full = _all_gather(shards, "gather_weights")
    ffn_w = [[full[3 * layer + k].reshape(F, D) for k in range(3)] for layer in range(2)]
    wk_f, wv_f, wq_f, wo_f = [full[6 + k].reshape(D, D) for k in range(4)]
    pw_f = full[10].reshape(N_DEV, N_GROUPS, POOL_SHARD, GROUP).transpose(1, 0, 2, 3).reshape(N_GROUPS, GROUP, GROUP)
    small_f = full[11][:, 0, :]
    pb_f = small_f[:, :N_GROUPS * POOL_SHARD].reshape(N_DEV, N_GROUPS, POOL_SHARD).transpose(1, 0, 2).reshape(1, D)
    ps_f = small_f[:, N_GROUPS * POOL_SHARD:].reshape(1, D)

    g_mix0, g_mix1 = norm_mix_g[0:1], norm_mix_g[1:2]
    g_ffn0, g_ffn1 = norm_ffn_g[0:1], norm_ffn_g[1:2]
    g_kv = kv_norm_g.reshape(1, D)
    gk_t = jnp.tile(k_norm_g.reshape(1, HEAD), (1, N_HEADS))
    gq_t = jnp.tile(q_norm_g.reshape(1, HEAD), (1, N_HEADS))

    x1, diff = _pool_fwd(xin, g_mix0, pw_f, pb_f, ps_f)
    x2, hf0, gg0, uu0 = _ffn_fwd(x1, g_ffn0, *ffn_w[0])
    hkv, h1, kpre, qpre, kk, vv, qq = _kvq_proj(x2, g_kv, g_mix1, wk_f, wv_f, wq_f, gk_t, gq_t)
    kp = jnp.pad(kk, ((PADK, 0), (0, 0)))
    vp = jnp.pad(vv, ((PADK, 0), (0, 0)))
    bias = _bias_band(_toeplitz_from_table(rel_bias[0]))
    att = _attn_fwd(qq, kp, vp, bias)
    x3 = _mm(att, wo_f, "nn", F32, "attn_out", add=x2)
    dx4, loss_rows, hf1, gg1, uu1 = _ffn_fwd(x3, g_ffn1, *ffn_w[1], target=target)

    dhf1, dwg1, dwu1, dwd1 = _ffn_bwd(dx4, hf1, gg1, uu1, *ffn_w[1], name="ffn_bwd1")
    dx3, dg_ffn1 = _norm_bwd(dx4, x3, [dhf1], [g_ffn1], "norm_bwd_ffn1")
    datt = _mm(dx3, wo_f, "nt", BF16, "d_attn")
    dwo = _mm(att, dx3, "tn", BF16, "d_wo")
    dq, dkp, dvp, dband = _attn_bwd(qq, kp, vp, bias, datt)
    dtp, seg = _bias_grad(dband)
    dqpre, dgq = _head_norm_bwd(dq, qpre, gq_t, "q_norm_bwd")
    dkpre, dgk = _head_norm_bwd(dkp, kpre, gk_t, "k_norm_bwd", row_off=PADK // PROJ_TS)
    dh1 = _mm(dqpre, wq_f, "nt", F32, "d_h1")
    dwq = _mm(h1, dqpre, "tn", BF16, "d_wq")
    dhkv = _mm(dkpre, wk_f, "nt", F32, "d_hkv_k")
    dv = dvp[PADK:]
    dhkv = _mm(dv, wv_f, "nt", F32, "d_hkv_v", add=dhkv)
    dwk = _mm(hkv, dkpre, "tn", BF16, "d_wk")
    dwv = _mm(hkv, dv, "tn", BF16, "d_wv")
    dx2, dg_mix1, dg_kv = _norm_bwd(dx3, x2, [dh1, dhkv], [g_mix1, g_kv], "norm_bwd_mix1")
    dhf0, dwg0, dwu0, dwd0 = _ffn_bwd(dx2, hf0, gg0, uu0, *ffn_w[0], name="ffn_bwd0")
    dx1, dg_ffn0 = _norm_bwd(dx2, x1, [dhf0], [g_ffn0], "norm_bwd_ffn0")
    grad_x, dpw, db_rows, ds_rows, dg_mix0 = _pool_bwd(dx1, xin, diff, g_mix0, pw_f, pb_f, ps_f)

    dpw_blocks = dpw.reshape(N_GROUPS, N_DEV, POOL_SHARD, GROUP).transpose(1, 0, 2, 3)
    parts = [dw.reshape(N_DEV, F_SHARD, D) for dw in (dwg0, dwu0, dwd0, dwg1, dwu1, dwd1)]
    parts += [dw.reshape(N_DEV, D_SHARD, D) for dw in (dwk, dwv, dwq, dwo)]
    parts.append(dpw_blocks.reshape(N_DEV, N_GROUPS * POOL_SHARD, GROUP).astype(BF16))
    recv = _exchange_blocks(parts, "scatter_grads")

    misc = jnp.concatenate([dgk[0:1, 0:HEAD], dgq[0:1, 0:HEAD], loss_rows[0:1, 0:1],
                            seg[:, 0, 0].reshape(1, N_HEADS), seg[:, 0, 1].reshape(1, N_HEADS)], axis=1)
    misc = jnp.pad(misc, ((0, 0), (0, D - misc.shape[1])))
    vec_rows = jnp.concatenate([dg_mix0[0:1], dg_mix1[0:1], dg_ffn0[0:1], dg_ffn1[0:1], dg_kv[0:1],
                                db_rows[0:1], ds_rows[0:1], misc], axis=0)
    pack = jnp.concatenate([vec_rows, dtp.reshape(N_HEADS, TOEP)], axis=0)
    tot = _all_reduce_small(pack, "reduce_small")

    loss = tot[7, 2 * HEAD]
    g_norm_mix = tot[0:2]
    g_norm_ffn = tot[2:4]
    g_kv_norm = tot[4]
    g_k_norm = tot[7, 0:HEAD]
    g_q_norm = tot[7, HEAD:2 * HEAD].reshape(1, HEAD)
    seg_tot = jnp.stack([tot[7, 2 * HEAD + 1:2 * HEAD + 1 + N_HEADS],
                         tot[7, 2 * HEAD + 1 + N_HEADS:2 * HEAD + 1 + 2 * N_HEADS]], axis=1)
    g_rel = _table_grad_from_toeplitz(tot[8:8 + N_HEADS], seg_tot).reshape(1, N_HEADS, N_REL)
    me = 4 * lax.axis_index("x") + 2 * lax.axis_index("y") + lax.axis_index("c")
    g_pool_b = lax.dynamic_slice_in_dim(tot[5].reshape(N_GROUPS, GROUP), me * POOL_SHARD, POOL_SHARD, axis=1)
    g_pool_b = g_pool_b.reshape(1, N_GROUPS, POOL_SHARD)
    g_pool_scale = lax.dynamic_slice_in_dim(tot[6:7], me * D_SHARD, D_SHARD, axis=1)

    sums = [_sum_parts(recv[k], "sum_parts_%d" % k) for k in range(len(recv))]
    g_gate = jnp.stack([sums[0].T, sums[3].T])
    g_up = jnp.stack([sums[1].T, sums[4].T])
    g_down = jnp.stack([sums[2], sums[5]])
    g_wk, g_wv = sums[6], sums[7]
    g_wq, g_wo = sums[8][None], sums[9][None]
    g_pool_w = sums[10].reshape(1, N_GROUPS, POOL_SHARD, GROUP)

    grads = dict(norm_mix_g=g_norm_mix, norm_ffn_g=g_norm_ffn, pool_w=g_pool_w, pool_b=g_pool_b,
                 pool_scale=g_pool_scale, kv_norm_g=g_kv_norm, w_k=g_wk, w_v=g_wv, k_norm_g=g_k_norm, w_q=g_wq,
                 q_norm_g=g_q_norm, rel_bias=g_rel, w_o=g_wo, w_gate=g_gate, w_up=g_up, w_down=g_down)
    weights = dict(norm_mix_g=norm_mix_g, norm_ffn_g=norm_ffn_g, pool_w=pool_w, pool_b=pool_b,
                   pool_scale=pool_scale, kv_norm_g=kv_norm_g, w_k=w_k, w_v=w_v, k_norm_g=k_norm_g, w_q=w_q,
                   q_norm_g=q_norm_g, rel_bias=rel_bias, w_o=w_o, w_gate=w_gate, w_up=w_up, w_down=w_down)
    mom1 = dict(norm_mix_g=m_norm_mix_g, norm_ffn_g=m_norm_ffn_g, pool_w=m_pool_w, pool_b=m_pool_b,
                pool_scale=m_pool_scale, kv_norm_g=m_kv_norm_g, w_k=m_w_k, w_v=m_w_v, k_norm_g=m_k_norm_g,
                w_q=m_w_q, q_norm_g=m_q_norm_g, rel_bias=m_rel_bias, w_o=m_w_o, w_gate=m_w_gate, w_up=m_w_up,
                w_down=m_w_down)
    mom2 = dict(norm_mix_g=v_norm_mix_g, norm_ffn_g=v_norm_ffn_g, pool_w=v_pool_w, pool_b=v_pool_b,
                pool_scale=v_pool_scale, kv_norm_g=v_kv_norm_g, w_k=v_w_k, w_v=v_w_v, k_norm_g=v_k_norm_g,
                w_q=v_w_q, q_norm_g=v_q_norm_g, rel_bias=v_rel_bias, w_o=v_w_o, w_gate=v_w_gate, w_up=v_w_up,
                w_down=v_w_down)
    names = list(grads)

    large = ("w_gate", "w_up", "w_down", "w_k", "w_v", "w_q", "w_o", "pool_w")
    deltas, new_m, new_v = {}, {}, {}
    for nm in large:
        shape = weights[nm].shape
        flat = lambda a: a.reshape(-1, shape[-1])
        dl, m1, m2 = _adam(flat(weights[nm]), flat(grads[nm]), flat(mom1[nm]), flat(mom2[nm]), "adam_" + nm)
        deltas[nm], new_m[nm], new_v[nm] = dl.reshape(shape), m1.reshape(shape), m2.reshape(shape)
    small_names = [nm for nm in names if nm not in large]

    def pack_small(tree):
        cols = []
        for nm in small_names:
            flat = tree[nm].reshape(-1)
            cols.append(jnp.pad(flat, (0, -flat.shape[0] % 1024)))
        return jnp.concatenate(cols).reshape(-1, 128)

    dl, m1, m2 = _adam(pack_small(weights), pack_small(grads), pack_small(mom1), pack_small(mom2), "adam_small")

    def unpack_small(packed, out):
        flat, off = packed.reshape(-1), 0
        for nm in small_names:
            size = weights[nm].size
            out[nm] = flat[off:off + size].reshape(weights[nm].shape)
            off += size + (-size % 1024)

    unpack_small(dl, deltas)
    unpack_small(m1, new_m)
    unpack_small(m2, new_v)

    return (loss, grad_x[None], *[grads[nm] for nm in names], *[deltas[nm] for nm in names],
            *[new_m[nm] for nm in names], *[new_v[nm] for nm in names])
```

```python
import functools

import jax
import jax.numpy as jnp
from jax import lax
from jax.experimental import pallas as pl
from jax.experimental.pallas import tpu as pltpu

F32 = jnp.float32
BF16 = jnp.bfloat16
MESH_ID = pl.DeviceIdType.MESH

N_DEV = 8
S = 2048
D = 1024
F = 2816
F_SHARD = F // N_DEV
D_SHARD = D // N_DEV
N_GROUPS = 4
GROUP = D // N_GROUPS
POOL_SHARD = GROUP // N_DEV
MAX_WIN = 16
HEAD = 64
N_HEADS = D // HEAD
CHUNK = 64
LEFT = 8
QB = 4 * CHUNK
KB = QB + LEFT * CHUNK
PADK = LEFT * CHUNK
TOEP = 1024
N_REL = 257
MAX_REL = 128
EPS = 1e-6
NEG_INF = -1e30
ATTN_SCALE = HEAD ** -0.5

ADAM_LR = 0.001
ADAM_B1 = 0.9
ADAM_B2 = 0.999
ADAM_EPS = 1e-08
ADAM_WD = 0.01
ADAM_STEP = 10

VMEM_LIMIT = 52 * 1024 * 1024

ANY = pl.BlockSpec(memory_space=pl.ANY)
VMEM = pl.BlockSpec(memory_space=pltpu.VMEM)


def _call(body, **kw):
    return pl.pallas_call(body, **kw)


def _params(*sem):
    return pltpu.CompilerParams(dimension_semantics=sem, vmem_limit_bytes=VMEM_LIMIT)


def _dot(a, b, dims):
    return lax.dot_general(a, b, (dims, ((), ())), preferred_element_type=F32)


def _nn(a, b):
    return _dot(a, b, ((1,), (0,)))


def _nt(a, b):
    return _dot(a, b, ((1,), (1,)))


def _tn(a, b):
    return _dot(a, b, ((0,), (0,)))


def _rstd(x):
    return lax.rsqrt(jnp.mean(x * x, axis=-1, keepdims=True) + EPS)


def _rms_bwd(dh, x, r, g):
    gd = dh * g
    return r * gd - x * (r * r * r) * jnp.mean(gd * x, axis=-1, keepdims=True)


def _colsum8(v):
    return jnp.broadcast_to(jnp.sum(v, axis=0, keepdims=True), (8, v.shape[1]))


def _seg_sum(v):
    r = lax.broadcasted_iota(jnp.int32, (128, 128), 0) // HEAD
    c = lax.broadcasted_iota(jnp.int32, (128, 128), 1) // HEAD
    ones = jnp.where(r == c, 1.0, 0.0).astype(BF16)
    out = []
    for blk in range(v.shape[1] // 128):
        part = v[:, blk * 128:(blk + 1) * 128]
        hi = part.astype(BF16)
        rest = part - hi.astype(F32)
        mid = rest.astype(BF16)
        lo = (rest - mid.astype(F32)).astype(BF16)
        out.append(_nn(hi, ones) + _nn(mid, ones) + _nn(lo, ones))
    return jnp.concatenate(out, axis=1)


def _place():
    return lax.axis_index("x"), lax.axis_index("y"), lax.axis_index("c")


class _Exchange:
    def __init__(self, ins, out_shape, sems, start, mid, finish):
        self.ins, self.out_shape, self.sems = list(ins), list(out_shape), list(sems)
        self.start, self.mid, self.finish = start, mid, finish


def _gather_exchange(shards):
    n = len(shards)

    def tools(ins, outs, sems):
        send_sems, recv_sems, local_sems = sems
        x, y, c = _place()
        me, sibling = (x, y, c), (x, y, 1 - c)
        chips = [(1 - x, y), (x, 1 - y), (1 - x, 1 - y)]

        def slot(k, px, py, pc):
            return outs[k].at[4 * px + 2 * py + pc]

        def copy(k, s, block, to, own=False):
            return pltpu.make_async_remote_copy(
                src_ref=ins[k] if own else slot(k, *block), dst_ref=slot(k, *block),
                send_sem=send_sems.at[k, s], recv_sem=recv_sems.at[k, s], device_id=to, device_id_type=MESH_ID)

        mine = [pltpu.make_async_copy(ins[k], slot(k, *me), local_sems.at[k]) for k in range(n)]
        first, passed = [], []
        for k in range(n):
            first.append(copy(k, 0, me, sibling, own=True))
            first += [copy(k, 1 + j, me, (*chip, c), own=True) for j, chip in enumerate(chips)]
        landed = [[copy(k, 1 + j, (*chip, c), me) for k in range(n)] for j, chip in enumerate(chips)]
        passed = [[copy(k, 4 + j, (*chip, c), sibling) for k in range(n)] for j, chip in enumerate(chips)]
        from_sibling = []
        for k in range(n):
            from_sibling.append(copy(k, 0, sibling, me))
            from_sibling += [copy(k, 4 + j, (*chip, 1 - c), me) for j, chip in enumerate(chips)]
        return mine, first, landed, passed, from_sibling

    def start(ins, outs, sems):
        mine, first, _, _, _ = tools(ins, outs, sems)
        for cp in mine + first:
            cp.start()

    def mid(ins, outs, sems):
        _, _, landed, passed, _ = tools(ins, outs, sems)
        for j in range(3):
            for k in range(n):
                landed[j][k].wait_recv()
                passed[j][k].start()

    def finish(ins, outs, sems):
        mine, first, _, passed, from_sibling = tools(ins, outs, sems)
        for cp in from_sibling:
            cp.wait_recv()
        for cp in first + passed[0] + passed[1] + passed[2]:
            cp.wait_send()
        for cp in mine:
            cp.wait()

    return _Exchange(
        shards, [jax.ShapeDtypeStruct((N_DEV,) + a.shape, a.dtype) for a in shards],
        [pltpu.SemaphoreType.DMA((n, 7)), pltpu.SemaphoreType.DMA((n, 7)), pltpu.SemaphoreType.DMA((n,))],
        start, mid, finish)


def _peer(x, y, c, m):
    px = 1 - x if m & 4 else x
    py = 1 - y if m & 2 else y
    pc = 1 - c if m & 1 else c
    return px, py, pc


def _scatter_exchange(parts):
    n = len(parts)

    def tools(ins, outs, sems):
        send_sems, recv_sems, local_sems = sems
        x, y, c = _place()
        me = 4 * x + 2 * y + c
        mine = [pltpu.make_async_copy(ins[k].at[me], outs[k].at[me], local_sems.at[k]) for k in range(n)]
        sent, landed = [], []
        for m in range(1, N_DEV):
            px, py, pc = _peer(x, y, c, m)
            peer = 4 * px + 2 * py + pc
            for k in range(n):
                sent.append(pltpu.make_async_remote_copy(
                    src_ref=ins[k].at[peer], dst_ref=outs[k].at[me], send_sem=send_sems.at[k, m - 1],
                    recv_sem=recv_sems.at[k, m - 1], device_id=(px, py, pc), device_id_type=MESH_ID))
                landed.append(pltpu.make_async_remote_copy(
                    src_ref=ins[k].at[peer], dst_ref=outs[k].at[peer], send_sem=send_sems.at[k, m - 1],
                    recv_sem=recv_sems.at[k, m - 1], device_id=(px, py, pc), device_id_type=MESH_ID))
        return mine, sent, landed

    def start(ins, outs, sems):
        mine, sent, _ = tools(ins, outs, sems)
        for cp in mine + sent:
            cp.start()

    def mid(ins, outs, sems):
        pass

    def finish(ins, outs, sems):
        mine, sent, landed = tools(ins, outs, sems)
        for cp in landed:
            cp.wait_recv()
        for cp in sent:
            cp.wait_send()
        for cp in mine:
            cp.wait()

    return _Exchange(
        parts, [jax.ShapeDtypeStruct(a.shape, a.dtype) for a in parts],
        [pltpu.SemaphoreType.DMA((n, 7)), pltpu.SemaphoreType.DMA((n, 7)), pltpu.SemaphoreType.DMA((n,))],
        start, mid, finish)


def _run_exchange(ex, name):
    n_in, n_out = len(ex.ins), len(ex.out_shape)

    def body(*refs):
        ins, outs, sems = refs[:n_in], refs[n_in:n_in + n_out], refs[n_in + n_out:]
        ex.start(ins, outs, sems)
        ex.mid(ins, outs, sems)
        ex.finish(ins, outs, sems)

    return list(_call(body, name=name, out_shape=ex.out_shape, in_specs=[ANY] * n_in, out_specs=[ANY] * n_out,
                      scratch_shapes=ex.sems)(*ex.ins))


def _call_hosting(body, ex, phases, *, in_specs, out_specs, out_shape, scratch_shapes, args, **kw):
    n_in, n_out, n_scr = len(in_specs), len(out_specs), len(scratch_shapes)
    if ex is None:
        res = _call(body, in_specs=in_specs, out_specs=out_specs, out_shape=out_shape,
                    scratch_shapes=scratch_shapes, **kw)(*args)
        return list(res), []
    n_xin, n_xout = len(ex.ins), len(ex.out_shape)

    def hosting(*refs):
        a, b = n_in, n_in + n_xin
        c, d = b + n_out, b + n_out + n_xout
        ins, xins, outs, xouts = refs[:a], refs[a:b], refs[b:c], refs[c:d]
        scr, sems = refs[d:d + n_scr], refs[d + n_scr:]
        first, mid, last = phases()

        @pl.when(first)
        def _():
            ex.start(xins, xouts, sems)

        body(*ins, *outs, *scr)

        @pl.when(mid)
        def _():
            ex.mid(xins, xouts, sems)

        @pl.when(last)
        def _():
            ex.finish(xins, xouts, sems)

    res = _call(hosting, in_specs=list(in_specs) + [ANY] * n_xin, out_specs=list(out_specs) + [ANY] * n_xout,
                out_shape=list(out_shape) + ex.out_shape, scratch_shapes=list(scratch_shapes) + ex.sems,
                **kw)(*args, *ex.ins)
    return list(res[:n_out]), list(res[n_out:])


def _grid_phases(n0, n1, mid0):
    def phases():
        i, j = pl.program_id(0), pl.program_id(1)
        return (i == 0) & (j == 0), (i == mid0) & (j == 0), (i == n0 - 1) & (j == n1 - 1)
    return phases


def _all_reduce_small(pack, name):
    rows = pack.shape[0]

    def body(in_ref, out_ref, recv, send_sems, recv_sems):
        x, y, c = _place()
        me = 4 * x + 2 * y + c
        recv[me] = in_ref[...]
        sent = []
        for m in range(1, N_DEV):
            px, py, pc = _peer(x, y, c, m)
            cp = pltpu.make_async_remote_copy(
                src_ref=in_ref, dst_ref=recv.at[me], send_sem=send_sems.at[m - 1], recv_sem=recv_sems.at[m - 1],
                device_id=(px, py, pc), device_id_type=MESH_ID)
            cp.start()
            sent.append(cp)
        for m in range(1, N_DEV):
            px, py, pc = _peer(x, y, c, m)
            peer = 4 * px + 2 * py + pc
            pltpu.make_async_remote_copy(
                src_ref=in_ref, dst_ref=recv.at[peer], send_sem=send_sems.at[m - 1], recv_sem=recv_sems.at[m - 1],
                device_id=(px, py, pc), device_id_type=MESH_ID).wait_recv()
        acc = recv[0]
        for d in range(1, N_DEV):
            acc = acc + recv[d]
        out_ref[...] = acc
        for cp in sent:
            cp.wait_send()

    return _call(
        body, name=name, out_shape=jax.ShapeDtypeStruct(pack.shape, F32), in_specs=[VMEM], out_specs=VMEM,
        scratch_shapes=[pltpu.VMEM((N_DEV, rows, pack.shape[1]), F32), pltpu.SemaphoreType.DMA((7,)),
                        pltpu.SemaphoreType.DMA((7,))],
        compiler_params=pltpu.CompilerParams(vmem_limit_bytes=VMEM_LIMIT),
    )(pack)


POOL_TS = 256


def _pool_counts(first_row, rows, win):
    t = first_row + lax.broadcasted_iota(jnp.int32, (rows, 1), 0)
    return jnp.minimum(t + 1, win).astype(F32)


def _pool_fwd(x, g, w, b, scale):
    nt = S // POOL_TS

    def body(x_ref, g_ref, w_ref, b_ref, s_ref, out_ref, diff_ref, ext):
        i = pl.program_id(0)

        @pl.when(i == 0)
        def _():
            ext[0:MAX_WIN, :] = jnp.zeros((MAX_WIN, D), F32)

        @pl.when(i > 0)
        def _():
            ext[0:MAX_WIN, :] = ext[POOL_TS:POOL_TS + MAX_WIN, :]

        xv = x_ref[...]
        h = xv * _rstd(xv) * g_ref[...]
        ext[MAX_WIN:, :] = h
        for gi in range(N_GROUPS):
            win = 2 << gi
            cols = slice(gi * GROUP, (gi + 1) * GROUP)
            sm = ext[:, cols]
            k = 1
            while k < win:
                sm = sm + pltpu.roll(sm, k, axis=0)
                k *= 2
            pooled = sm[MAX_WIN:, :] / _pool_counts(i * POOL_TS, POOL_TS, win)
            diff = (pooled - h[:, cols]).astype(BF16)
            yv = (_nn(diff, w_ref[gi]) + b_ref[:, cols]) * s_ref[:, cols]
            out_ref[:, cols] = xv[:, cols] + yv
            diff_ref[:, cols] = diff

    row = pl.BlockSpec((1, D), lambda i: (0, 0))
    tile = pl.BlockSpec((POOL_TS, D), lambda i: (i, 0))
    return _call(
        body, name="pool_fwd", grid=(nt,),
        in_specs=[tile, row, pl.BlockSpec((N_GROUPS, GROUP, GROUP), lambda i: (0, 0, 0)), row, row],
        out_specs=[tile, tile],
        out_shape=[jax.ShapeDtypeStruct((S, D), F32), jax.ShapeDtypeStruct((S, D), BF16)],
        scratch_shapes=[pltpu.VMEM((POOL_TS + MAX_WIN, D), F32)],
        compiler_params=_params("arbitrary"),
    )(x, g, w, b, scale)


def _pool_bwd(dy, x, diff, g, w, b, scale):
    nt = S // POOL_TS

    def body(dy_ref, x_ref, diff_ref, g_ref, w_ref, b_ref, s_ref, gx_ref, dw_ref, db_ref, ds_ref, dg_ref, ext, dh):
        i = pl.program_id(0)
        first_row = (nt - 1 - i) * POOL_TS

        @pl.when(i == 0)
        def _():
            ext[POOL_TS:, :] = jnp.zeros((MAX_WIN, D), F32)
            dw_ref[...] = jnp.zeros(dw_ref.shape, F32)
            db_ref[...] = jnp.zeros(db_ref.shape, F32)
            ds_ref[...] = jnp.zeros(ds_ref.shape, F32)
            dg_ref[...] = jnp.zeros(dg_ref.shape, F32)

        @pl.when(i > 0)
        def _():
            ext[POOL_TS:, :] = ext[0:MAX_WIN, :]

        dyv = dy_ref[...]
        for gi in range(N_GROUPS):
            win = 2 << gi
            cols = slice(gi * GROUP, (gi + 1) * GROUP)
            dfb = diff_ref[:, cols]
            z = _nn(dfb, w_ref[gi]) + b_ref[:, cols]
            dyg = dyv[:, cols]
            ds_ref[:, cols] += _colsum8(dyg * z)
            dz = dyg * s_ref[:, cols]
            db_ref[:, cols] += _colsum8(dz)
            dzb = dz.astype(BF16)
            dw_ref[gi] += _tn(dfb, dzb)
            ddiff = _nt(dzb, w_ref[gi])
            ext[0:POOL_TS, cols] = ddiff / _pool_counts(first_row, POOL_TS, win)
            sm = ext[:, cols]
            k = 1
            while k < win:
                sm = sm + pltpu.roll(sm, POOL_TS + MAX_WIN - k, axis=0)
                k *= 2
            dh[:, cols] = sm[0:POOL_TS, :] - ddiff
        xv = x_ref[...]
        r = _rstd(xv)
        gv = g_ref[...]
        dhv = dh[...]
        dg_ref[...] += _colsum8(dhv * xv * r)
        gx_ref[...] = dyv + _rms_bwd(dhv, xv, r, gv)

    row = pl.BlockSpec((1, D), lambda i: (0, 0))
    tile = pl.BlockSpec((POOL_TS, D), lambda i: (nt - 1 - i, 0))
    acc = pl.BlockSpec((8, D), lambda i: (0, 0))
    wspec = pl.BlockSpec((N_GROUPS, GROUP, GROUP), lambda i: (0, 0, 0))
    return _call(
        body, name="pool_bwd", grid=(nt,),
        in_specs=[tile, tile, tile, row, wspec, row, row],
        out_specs=[tile, wspec, acc, acc, acc],
        out_shape=[jax.ShapeDtypeStruct((S, D), F32), jax.ShapeDtypeStruct((N_GROUPS, GROUP, GROUP), F32),
                   jax.ShapeDtypeStruct((8, D), F32), jax.ShapeDtypeStruct((8, D), F32),
                   jax.ShapeDtypeStruct((8, D), F32)],
        scratch_shapes=[pltpu.VMEM((POOL_TS + MAX_WIN, D), F32), pltpu.VMEM((POOL_TS, D), F32)],
        compiler_params=_params("arbitrary"),
    )(dy, x, diff, g, w, b, scale)


FFN_TS = 512
FFN_TF = 256


def _ffn_fwd(x, g, wg_t, wu_t, wd, target=None, ex=None):
    ni, nj = S // FFN_TS, F // FFN_TF
    with_loss = target is not None

    def body(*refs):
        if with_loss:
            x_ref, g_ref, wg_ref, wu_ref, wd_ref, t_ref, out_ref, loss_ref, h_ref, gg_ref, uu_ref, hs, acc = refs
        else:
            x_ref, g_ref, wg_ref, wu_ref, wd_ref, out_ref, h_ref, gg_ref, uu_ref, hs, acc = refs
        i, j = pl.program_id(0), pl.program_id(1)

        @pl.when(j == 0)
        def _():
            xv = x_ref[...]
            hb = (xv * _rstd(xv) * g_ref[...]).astype(BF16)
            hs[...] = hb
            h_ref[...] = hb
            acc[...] = jnp.zeros(acc.shape, F32)

        hb = hs[...]
        gg = _nt(hb, wg_ref[...])
        uu = _nt(hb, wu_ref[...])
        gg_ref[...] = gg
        uu_ref[...] = uu
        a = (gg * jax.nn.sigmoid(gg) * uu).astype(BF16)
        acc[...] += _nn(a, wd_ref[...])

        @pl.when(j == nj - 1)
        def _():
            yv = x_ref[...] + acc[...]
            if with_loss:
                err = yv - t_ref[...]
                out_ref[...] = err * (1.0 / D)
                part = jnp.sum(err * err) * (0.5 / D)

                @pl.when(i == 0)
                def _():
                    loss_ref[...] = jnp.zeros(loss_ref.shape, F32)

                loss_ref[...] += jnp.broadcast_to(part, loss_ref.shape)
            else:
                out_ref[...] = yv

    xt = pl.BlockSpec((FFN_TS, D), lambda i, j: (i, 0))
    row = pl.BlockSpec((1, D), lambda i, j: (0, 0))
    wt = pl.BlockSpec((FFN_TF, D), lambda i, j: (j, 0))
    gt = pl.BlockSpec((FFN_TS, FFN_TF), lambda i, j: (i, j))
    in_specs = [xt, row, wt, wt, wt] + ([xt] if with_loss else [])
    out_specs = [xt] + ([pl.BlockSpec((8, 128), lambda i, j: (0, 0))] if with_loss else []) + [xt, gt, gt]
    out_shape = ([jax.ShapeDtypeStruct((S, D), F32)] + ([jax.ShapeDtypeStruct((8, 128), F32)] if with_loss else [])
                 + [jax.ShapeDtypeStruct((S, D), BF16), jax.ShapeDtypeStruct((S, F), F32),
                    jax.ShapeDtypeStruct((S, F), F32)])
    args = (x, g, wg_t, wu_t, wd) + ((target,) if with_loss else ())
    return _call_hosting(
        body, ex, _grid_phases(ni, nj, ni - 1), name="ffn_fwd_loss" if with_loss else "ffn_fwd", grid=(ni, nj),
        in_specs=in_specs, out_specs=out_specs, out_shape=out_shape,
        scratch_shapes=[pltpu.VMEM((FFN_TS, D), BF16), pltpu.VMEM((FFN_TS, D), F32)], args=args,
        compiler_params=_params("arbitrary", "arbitrary"),
    )


def _ffn_bwd(dout, h, gg, uu, wg_t, wu_t, wd, name, ex=None):
    ni, nj = S // FFN_TS, F // FFN_TF

    def body(do_ref, h_ref, gg_ref, uu_ref, wg_ref, wu_ref, wd_ref, dh_ref, dwg_ref, dwu_ref, dwd_ref, acc):
        j, i = pl.program_id(0), pl.program_id(1)
        dob = do_ref[...].astype(BF16)
        hb = h_ref[...]
        gv, uv = gg_ref[...], uu_ref[...]
        da = _nt(dob, wd_ref[...])
        sg = jax.nn.sigmoid(gv)
        sl = gv * sg
        ab = (sl * uv).astype(BF16)
        dub = (da * sl).astype(BF16)
        dgb = (da * uv * (sg * (1.0 + gv * (1.0 - sg)))).astype(BF16)

        @pl.when(i == 0)
        def _():
            acc[...] = jnp.zeros(acc.shape, F32)

        acc[0] += _tn(dgb, hb)
        acc[1] += _tn(dub, hb)
        acc[2] += _tn(ab, dob)
        contrib = _nn(dgb, wg_ref[...]) + _nn(dub, wu_ref[...])
        rows = pl.ds(pl.multiple_of(i * FFN_TS, FFN_TS), FFN_TS)

        @pl.when(j == 0)
        def _():
            dh_ref[rows, :] = contrib

        @pl.when(j > 0)
        def _():
            dh_ref[rows, :] += contrib

        @pl.when(i == ni - 1)
        def _():
            dwg_ref[...] = acc[0].astype(BF16)
            dwu_ref[...] = acc[1].astype(BF16)
            dwd_ref[...] = acc[2].astype(BF16)

    xt = pl.BlockSpec((FFN_TS, D), lambda j, i: (i, 0))
    wt = pl.BlockSpec((FFN_TF, D), lambda j, i: (j, 0))
    gt = pl.BlockSpec((FFN_TS, FFN_TF), lambda j, i: (i, j))
    return _call_hosting(
        body, ex, _grid_phases(nj, ni, nj // 2), name=name, grid=(nj, ni),
        in_specs=[xt, xt, gt, gt, wt, wt, wt],
        out_specs=[pl.BlockSpec((S, D), lambda j, i: (0, 0)), wt, wt, wt],
        out_shape=[jax.ShapeDtypeStruct((S, D), F32)] + [jax.ShapeDtypeStruct((F, D), BF16)] * 3,
        scratch_shapes=[pltpu.VMEM((3, FFN_TF, D), F32)], args=(dout, h, gg, uu, wg_t, wu_t, wd),
        compiler_params=_params("arbitrary", "arbitrary"),
    )


NORM_TS = 256


def _norm_bwd(dres, x, dhs, gs, name):
    n = len(dhs)
    nt = S // NORM_TS

    def body(*refs):
        dres_ref, x_ref = refs[:2]
        dh_refs, g_refs = refs[2:2 + n], refs[2 + n:2 + 2 * n]
        dx_ref, dg_refs = refs[2 + 2 * n], refs[3 + 2 * n:]
        i = pl.program_id(0)
        xv = x_ref[...]
        r = _rstd(xv)
        dx = dres_ref[...]
        for k in range(n):
            dhv = dh_refs[k][...]
            dx = dx + _rms_bwd(dhv, xv, r, g_refs[k][...])

            @pl.when(i == 0)
            def _():
                dg_refs[k][...] = jnp.zeros((8, D), F32)

            dg_refs[k][...] += _colsum8(dhv * xv * r)
        dx_ref[...] = dx

    tile = pl.BlockSpec((NORM_TS, D), lambda i: (i, 0))
    row = pl.BlockSpec((1, D), lambda i: (0, 0))
    acc = pl.BlockSpec((8, D), lambda i: (0, 0))
    return _call(
        body, name=name, grid=(nt,),
        in_specs=[tile, tile] + [tile] * n + [row] * n,
        out_specs=[tile] + [acc] * n,
        out_shape=[jax.ShapeDtypeStruct((S, D), F32)] + [jax.ShapeDtypeStruct((8, D), F32)] * n,
        compiler_params=_params("arbitrary"),
    )(dres, x, *dhs, *gs)


def _mm(a, b, mode, out_dtype, name, add=None):
    if mode == "nn":
        (m, kd), n = a.shape, b.shape[1]
    elif mode == "nt":
        (m, kd), n = a.shape, b.shape[0]
    else:
        (kd, m), n = a.shape, b.shape[1]
    tm, tn, tk = min(m, 512), min(n, 512), min(kd, 1024)
    nk = kd // tk
    dot = {"nn": _nn, "nt": _nt, "tn": _tn}[mode]

    def body(*refs):
        if add is None:
            a_ref, b_ref, o_ref, acc = refs
        else:
            a_ref, b_ref, add_ref, o_ref, acc = refs
        k = pl.program_id(2)

        @pl.when(k == 0)
        def _():
            acc[...] = jnp.zeros(acc.shape, F32)

        acc[...] += dot(a_ref[...].astype(BF16), b_ref[...].astype(BF16))

        @pl.when(k == nk - 1)
        def _():
            res = acc[...]
            if add is not None:
                res = res + add_ref[...]
            o_ref[...] = res.astype(out_dtype)

    if mode == "tn":
        a_spec = pl.BlockSpec((tk, tm), lambda i, j, k: (k, i))
        b_spec = pl.BlockSpec((tk, tn), lambda i, j, k: (k, j))
    else:
        a_spec = pl.BlockSpec((tm, tk), lambda i, j, k: (i, k))
        b_spec = (pl.BlockSpec((tk, tn), lambda i, j, k: (k, j)) if mode == "nn"
                  else pl.BlockSpec((tn, tk), lambda i, j, k: (j, k)))
    o_spec = pl.BlockSpec((tm, tn), lambda i, j, k: (i, j))
    in_specs = [a_spec, b_spec] + ([o_spec] if add is not None else [])
    args = (a, b) + ((add,) if add is not None else ())
    return _call(
        body, name=name, grid=(m // tm, n // tn, nk), in_specs=in_specs, out_specs=o_spec,
        out_shape=jax.ShapeDtypeStruct((m, n), out_dtype), scratch_shapes=[pltpu.VMEM((tm, tn), F32)],
        compiler_params=_params("parallel", "parallel", "arbitrary"),
    )(*args)


PROJ_TS = 256


def _kvq_proj(x, g_kv, g_mix, wk, wv, wq, gk, gq):
    def body(x_ref, gkv_ref, gmix_ref, wk_ref, wv_ref, wq_ref, gk_ref, gq_ref,
             hkv_ref, h1_ref, kpre_ref, qpre_ref, k_ref, v_ref, q_ref):
        xv = x_ref[...]
        xr = xv * _rstd(xv)
        hkv = (xr * gkv_ref[...]).astype(BF16)
        h1 = (xr * gmix_ref[...]).astype(BF16)
        hkv_ref[...] = hkv
        h1_ref[...] = h1
        kpre = _nn(hkv, wk_ref[...])
        qpre = _nn(h1, wq_ref[...])
        kpre_ref[...] = kpre
        qpre_ref[...] = qpre
        v_ref[...] = _nn(hkv, wv_ref[...]).astype(BF16)
        rk = lax.rsqrt(_seg_sum(kpre * kpre) * (1.0 / HEAD) + EPS)
        k_ref[...] = (kpre * rk * gk_ref[...]).astype(BF16)
        rq = lax.rsqrt(_seg_sum(qpre * qpre) * (1.0 / HEAD) + EPS)
        q_ref[...] = (qpre * rq * gq_ref[...]).astype(BF16)

    tile = pl.BlockSpec((PROJ_TS, D), lambda i: (i, 0))
    row = pl.BlockSpec((1, D), lambda i: (0, 0))
    wspec = pl.BlockSpec((D, D), lambda i: (0, 0))
    bf = jax.ShapeDtypeStruct((S, D), BF16)
    ff = jax.ShapeDtypeStruct((S, D), F32)
    return _call(
        body, name="kvq_proj", grid=(S // PROJ_TS,),
        in_specs=[tile, row, row, wspec, wspec, wspec, row, row],
        out_specs=[tile] * 7, out_shape=[bf, bf, ff, ff, bf, bf, bf],
        compiler_params=_params("arbitrary"),
    )(x, g_kv, g_mix, wk, wv, wq, gk, gq)


def _head_norm_bwd(dout, pre, hg, name, row_off=0):
    nt = S // PROJ_TS

    def body(do_ref, pre_ref, hg_ref, dpre_ref, dhg_ref, acc):
        i = pl.program_id(0)
        dov, pv, hgv = do_ref[...], pre_ref[...], hg_ref[...]
        r = lax.rsqrt(_seg_sum(pv * pv) * (1.0 / HEAD) + EPS)
        gd = dov * hgv
        dpre = r * gd - pv * (r * r * r) * (_seg_sum(gd * pv) * (1.0 / HEAD))
        dpre_ref[...] = dpre.astype(BF16)

        @pl.when(i == 0)
        def _():
            acc[...] = jnp.zeros(acc.shape, F32)

        acc[...] += _colsum8(dov * pv * r)

        @pl.when(i == nt - 1)
        def _():
            full = acc[...]
            fold = full[:, 0:128]
            for blk in range(1, D // 128):
                fold = fold + full[:, blk * 128:(blk + 1) * 128]
            dhg_ref[...] = fold + pltpu.roll(fold, HEAD, axis=1)

    tile = pl.BlockSpec((PROJ_TS, D), lambda i: (i, 0))
    return _call(
        body, name=name, grid=(nt,),
        in_specs=[pl.BlockSpec((PROJ_TS, D), lambda i: (i + row_off, 0)), tile, pl.BlockSpec((1, D), lambda i: (0, 0))],
        out_specs=[tile, pl.BlockSpec((8, 128), lambda i: (0, 0))],
        out_shape=[jax.ShapeDtypeStruct((S, D), BF16), jax.ShapeDtypeStruct((8, 128), F32)],
        scratch_shapes=[pltpu.VMEM((8, D), F32)],
        compiler_params=_params("arbitrary"),
    )(dout, pre, hg)


def _toeplitz_from_table(table):
    far = jnp.broadcast_to(table[:, N_REL - 1:], (N_HEADS, PADK - MAX_REL + 1))
    near = table[:, N_REL - 2::-1]
    past = jnp.broadcast_to(table[:, 0:1], (N_HEADS, MAX_REL))
    wrap = jnp.broadcast_to(table[:, N_REL - 1:], (N_HEADS, QB - 1))
    return jnp.concatenate([far, near, past, wrap], axis=1).reshape(N_HEADS, 1, TOEP)


def _table_grad_from_toeplitz(dtp, seg):
    lo = PADK - MAX_REL + 1
    near = dtp[:, lo + N_REL - 3:lo - 1:-1]
    return jnp.concatenate([seg[:, 1:2], near, seg[:, 0:1]], axis=1)


def _bias_band(tp):
    def body(tp_ref, out_ref):
        bv = jnp.broadcast_to(tp_ref[0], (QB, TOEP))
        row = lax.broadcasted_iota(jnp.int32, (QB, TOEP), 0)
        k = 1
        while k < QB:
            bv = jnp.where((row & k) != 0, pltpu.roll(bv, k, axis=1), bv)
            k *= 2
        out_ref[0] = bv[:, 0:KB]

    return _call(
        body, name="bias_band", grid=(N_HEADS,),
        in_specs=[pl.BlockSpec((1, 1, TOEP), lambda h: (h, 0, 0))],
        out_specs=pl.BlockSpec((1, QB, KB), lambda h: (h, 0, 0)),
        out_shape=jax.ShapeDtypeStruct((N_HEADS, QB, KB), F32),
        compiler_params=_params("parallel"),
    )(tp)


def _bias_grad(dband):
    lo, hi = PADK - MAX_REL + 1, PADK + MAX_REL

    def body(db_ref, dtp_ref, seg_ref):
        bv = jnp.concatenate([db_ref[0], jnp.zeros((QB, TOEP - KB), F32)], axis=1)
        row = lax.broadcasted_iota(jnp.int32, (QB, TOEP), 0)
        k = 1
        while k < QB:
            bv = jnp.where((row & k) != 0, pltpu.roll(bv, TOEP - k, axis=1), bv)
            k *= 2
        col = jnp.sum(bv, axis=0, keepdims=True)
        dtp_ref[0] = col
        u = lax.broadcasted_iota(jnp.int32, (1, TOEP), 1)
        far = jnp.sum(jnp.where((u < lo) | (u > hi + MAX_REL), col, 0.0))
        past = jnp.sum(jnp.where((u >= hi) & (u <= hi + MAX_REL), col, 0.0))
        lane = lax.broadcasted_iota(jnp.int32, (1, 128), 1)
        seg_ref[0] = jnp.where(lane == 0, far, jnp.where(lane == 1, past, 0.0))

    return _call(
        body, name="bias_grad", grid=(N_HEADS,),
        in_specs=[pl.BlockSpec((1, QB, KB), lambda h: (h, 0, 0))],
        out_specs=[pl.BlockSpec((1, 1, TOEP), lambda h: (h, 0, 0)), pl.BlockSpec((1, 1, 128), lambda h: (h, 0, 0))],
        out_shape=[jax.ShapeDtypeStruct((N_HEADS, 1, TOEP), F32), jax.ShapeDtypeStruct((N_HEADS, 1, 128), F32)],
        compiler_params=_params("parallel"),
    )(dband)


N_QB = S // QB
N_HP = D // 128


def _band_mask(cb):
    qc = lax.broadcasted_iota(jnp.int32, (QB, KB), 0) >> 6
    p = lax.broadcasted_iota(jnp.int32, (QB, KB), 1)
    kc = p >> 6
    return (kc >= qc) & (kc <= qc + LEFT) & (p + cb * QB >= PADK)


def _half_mask(hh):
    lane = lax.broadcasted_iota(jnp.int32, (1, 128), 1)
    return jnp.where((lane < HEAD) == (hh == 0), 1.0, 0.0).astype(BF16)


def _probs(qh, kb, bias, mask):
    sc = _nt(qh, kb) * ATTN_SCALE + bias
    sc = jnp.where(mask, sc, NEG_INF)
    e = jnp.exp(sc - jnp.max(sc, axis=-1, keepdims=True))
    return e / jnp.sum(e, axis=-1, keepdims=True)


def _attn_fwd(q, kp, vp, bias, ex=None):
    def body(q_ref, k_ref, v_ref, b_ref, o_ref):
        cb = pl.program_id(1)
        band = pl.ds(pl.multiple_of(cb * QB, QB), KB)
        kb, vb = k_ref[band, :], v_ref[band, :]
        qv = q_ref[...]
        mask = _band_mask(cb)
        low = lax.broadcasted_iota(jnp.int32, (QB, 128), 1) < HEAD
        outs = []
        for hh in range(2):
            pb = _probs(qv * _half_mask(hh), kb, b_ref[hh], mask).astype(BF16)
            outs.append(_nn(pb, vb))
        o_ref[...] = jnp.where(low, outs[0], outs[1]).astype(BF16)

    qspec = pl.BlockSpec((QB, 128), lambda hp, cb: (cb, hp))
    kspec = pl.BlockSpec((PADK + S, 128), lambda hp, cb: (0, hp))
    return _call_hosting(
        body, ex, _grid_phases(N_HP, N_QB, N_HP - 1), name="attn_fwd", grid=(N_HP, N_QB),
        in_specs=[qspec, kspec, kspec, pl.BlockSpec((2, QB, KB), lambda hp, cb: (hp, 0, 0))],
        out_specs=[qspec], out_shape=[jax.ShapeDtypeStruct((S, D), BF16)], scratch_shapes=[],
        args=(q, kp, vp, bias), compiler_params=_params("arbitrary", "arbitrary"),
    )


def _attn_bwd(q, kp, vp, bias, do, ex=None):
    def body(q_ref, k_ref, v_ref, b_ref, do_ref, dq_ref, dk_ref, dv_ref, db_ref):
        cb = pl.program_id(1)

        @pl.when(cb == 0)
        def _():
            dk_ref[...] = jnp.zeros(dk_ref.shape, F32)
            dv_ref[...] = jnp.zeros(dv_ref.shape, F32)
            db_ref[...] = jnp.zeros(db_ref.shape, F32)

        band = pl.ds(pl.multiple_of(cb * QB, QB), KB)
        kb, vb = k_ref[band, :], v_ref[band, :]
        qv, dov = q_ref[...], do_ref[...]
        mask = _band_mask(cb)
        low = lax.broadcasted_iota(jnp.int32, (QB, 128), 1) < HEAD
        dq = jnp.zeros((QB, 128), F32)
        dkb = jnp.zeros((KB, 128), F32)
        dvb = jnp.zeros((KB, 128), F32)
        for hh in range(2):
            sel = low if hh == 0 else jnp.logical_not(low)
            qh = qv * _half_mask(hh)
            doh = dov * _half_mask(hh)
            p = _probs(qh, kb, b_ref[hh], mask)
            dp = _nt(doh, vb)
            dvb = dvb + _tn(p.astype(BF16), doh)
            ds = p * (dp - jnp.sum(dp * p, axis=-1, keepdims=True))
            db_ref[hh] += ds
            dsb = (ds * ATTN_SCALE).astype(BF16)
            dq = dq + jnp.where(sel, _nn(dsb, kb), 0.0)
            dkb = dkb + _tn(dsb, qh)
        dq_ref[...] = dq
        dk_ref[band, :] += dkb
        dv_ref[band, :] += dvb

    qspec = pl.BlockSpec((QB, 128), lambda hp, cb: (cb, hp))
    kspec = pl.BlockSpec((PADK + S, 128), lambda hp, cb: (0, hp))
    bspec = pl.BlockSpec((2, QB, KB), lambda hp, cb: (hp, 0, 0))
    kf = jax.ShapeDtypeStruct((PADK + S, D), F32)
    return _call_hosting(
        body, ex, _grid_phases(N_HP, N_QB, N_HP // 2), name="attn_bwd", grid=(N_HP, N_QB),
        in_specs=[qspec, kspec, kspec, bspec, qspec],
        out_specs=[qspec, kspec, kspec, bspec],
        out_shape=[jax.ShapeDtypeStruct((S, D), F32), kf, kf, jax.ShapeDtypeStruct((N_HEADS, QB, KB), F32)],
        scratch_shapes=[], args=(q, kp, vp, bias, do), compiler_params=_params("arbitrary", "arbitrary"),
    )


def _adam_math(w, g, m, v):
    m = ADAM_B1 * m + (1.0 - ADAM_B1) * g
    v = ADAM_B2 * v + (1.0 - ADAM_B2) * (g * g)
    m_hat = m / (1.0 - ADAM_B1 ** ADAM_STEP)
    v_hat = v / (1.0 - ADAM_B2 ** ADAM_STEP)
    delta = -ADAM_LR * (m_hat / (jnp.sqrt(v_hat) + ADAM_EPS) + ADAM_WD * w)
    return delta, m, v


def _sum_parts(parts, name):
    _, r, c = parts.shape
    tr = r // 2 if r % 16 == 0 and r > 64 else r

    def body(p_ref, o_ref):
        acc = p_ref[0].astype(F32)
        for d in range(1, N_DEV):
            acc = acc + p_ref[d].astype(F32)
        o_ref[...] = acc

    return _call(
        body, name=name, grid=(r // tr,),
        in_specs=[pl.BlockSpec((N_DEV, tr, c), lambda i: (0, i, 0))],
        out_specs=pl.BlockSpec((tr, c), lambda i: (i, 0)),
        out_shape=jax.ShapeDtypeStruct((r, c), F32), compiler_params=_params("parallel"),
    )(parts)


def _adam(w, g, m, v, name):
    r, c = w.shape
    tr = r
    for cand in (256, 176, 128, 64, 32, 16, 8):
        if r % cand == 0:
            tr = cand
            break

    def body(w_ref, g_ref, m_ref, v_ref, d_ref, nm_ref, nv_ref):
        delta, nm, nv = _adam_math(w_ref[...], g_ref[...], m_ref[...], v_ref[...])
        d_ref[...] = delta
        nm_ref[...] = nm
        nv_ref[...] = nv

    spec = pl.BlockSpec((tr, c), lambda i: (i, 0))
    sd = jax.ShapeDtypeStruct((r, c), F32)
    return _call(
        body, name=name, grid=(r // tr,), in_specs=[spec] * 4, out_specs=[spec] * 3, out_shape=[sd] * 3,
        compiler_params=_params("parallel"),
    )(w, g, m, v)


def _pad8(rows):
    return jnp.pad(rows, ((0, 8 - rows.shape[0]), (0, 0)))


def kernel(x, norm_mix_g, norm_ffn_g, pool_w, pool_b, pool_scale, kv_norm_g, w_k, w_v, k_norm_g, w_q, q_norm_g, rel_bias, w_o, w_gate, w_up, w_down, loss_target, m_norm_mix_g, m_norm_ffn_g, m_pool_w, m_pool_b, m_pool_scale, m_kv_norm_g, m_w_k, m_w_v, m_k_norm_g, m_w_q, m_q_norm_g, m_rel_bias, m_w_o, m_w_gate, m_w_up, m_w_down, v_norm_mix_g, v_norm_ffn_g, v_pool_w, v_pool_b, v_pool_scale, v_kv_norm_g, v_w_k, v_w_v, v_k_norm_g, v_w_q, v_q_norm_g, v_rel_bias, v_w_o, v_w_gate, v_w_up, v_w_down):
    assert x.shape == (1, S, D) and w_gate.shape == (2, D, F_SHARD) and w_k.shape == (D_SHARD, D)
    xin, target = x[0], loss_target[0]

    ffn_shards = [[w_gate[layer].T.astype(BF16), w_up[layer].T.astype(BF16), w_down[layer].astype(BF16)]
                  for layer in range(2)]
    att_shards = [w_k.astype(BF16), w_v.astype(BF16), w_q[0].astype(BF16), w_o[0].astype(BF16)]
    pool_shard = pool_w[0].astype(BF16).reshape(N_GROUPS * POOL_SHARD, GROUP)
    small = jnp.concatenate([pool_b[0].reshape(1, N_GROUPS * POOL_SHARD), pool_scale], axis=1)

    full0 = _run_exchange(_gather_exchange(ffn_shards[0] + [pool_shard, _pad8(small)]), "gather_layer0")
    ffn_w0 = [a.reshape(F, D) for a in full0[:3]]
    pw_f = full0[3].reshape(N_DEV, N_GROUPS, POOL_SHARD, GROUP).transpose(1, 0, 2, 3).reshape(N_GROUPS, GROUP, GROUP)
    small_f = full0[4][:, 0, :]
    pb_f = small_f[:, :N_GROUPS * POOL_SHARD].reshape(N_DEV, N_GROUPS, POOL_SHARD).transpose(1, 0, 2).reshape(1, D)
    ps_f = small_f[:, N_GROUPS * POOL_SHARD:].reshape(1, D)

    g_mix0, g_mix1 = norm_mix_g[0:1], norm_mix_g[1:2]
    g_ffn0, g_ffn1 = norm_ffn_g[0:1], norm_ffn_g[1:2]
    g_kv = kv_norm_g.reshape(1, D)
    gk_t = jnp.tile(k_norm_g.reshape(1, HEAD), (1, N_HEADS))
    gq_t = jnp.tile(q_norm_g.reshape(1, HEAD), (1, N_HEADS))

    x1, diff = _pool_fwd(xin, g_mix0, pw_f, pb_f, ps_f)
    (x2, hf0, gg0, uu0), full_att = _ffn_fwd(x1, g_ffn0, *ffn_w0, ex=_gather_exchange(att_shards))
    wk_f, wv_f, wq_f, wo_f = [a.reshape(D, D) for a in full_att]
    hkv, h1, kpre, qpre, kk, vv, qq = _kvq_proj(x2, g_kv, g_mix1, wk_f, wv_f, wq_f, gk_t, gq_t)
    kp = jnp.pad(kk, ((PADK, 0), (0, 0)))
    vp = jnp.pad(vv, ((PADK, 0), (0, 0)))
    bias = _bias_band(_toeplitz_from_table(rel_bias[0]))
    (att,), full1 = _attn_fwd(qq, kp, vp, bias, ex=_gather_exchange(ffn_shards[1]))
    ffn_w1 = [a.reshape(F, D) for a in full1]
    x3 = _mm(att, wo_f, "nn", F32, "attn_out", add=x2)
    (dx4, loss_rows, hf1, gg1, uu1), _ = _ffn_fwd(x3, g_ffn1, *ffn_w1, target=target)

    def blocks(dw):
        return dw.reshape(N_DEV, dw.shape[0] // N_DEV, dw.shape[1])

    (dhf1, dwg1, dwu1, dwd1), _ = _ffn_bwd(dx4, hf1, gg1, uu1, *ffn_w1, name="ffn_bwd1")
    dx3, dg_ffn1 = _norm_bwd(dx4, x3, [dhf1], [g_ffn1], "norm_bwd_ffn1")
    datt = _mm(dx3, wo_f, "nt", BF16, "d_attn")
    dwo = _mm(att, dx3, "tn", BF16, "d_wo")
    (dq, dkp, dvp, dband), recv1 = _attn_bwd(
        qq, kp, vp, bias, datt, ex=_scatter_exchange([blocks(dw) for dw in (dwg1, dwu1, dwd1)]))
    dtp, seg = _bias_grad(dband)
    dqpre, dgq = _head_norm_bwd(dq, qpre, gq_t, "q_norm_bwd")
    dkpre, dgk = _head_norm_bwd(dkp, kpre, gk_t, "k_norm_bwd", row_off=PADK // PROJ_TS)
    dh1 = _mm(dqpre, wq_f, "nt", F32, "d_h1")
    dwq = _mm(h1, dqpre, "tn", BF16, "d_wq")
    dhkv = _mm(dkpre, wk_f, "nt", F32, "d_hkv_k")
    dv = dvp[PADK:]
    dhkv = _mm(dv, wv_f, "nt", F32, "d_hkv_v", add=dhkv)
    dwk = _mm(hkv, dkpre, "tn", BF16, "d_wk")
    dwv = _mm(hkv, dv, "tn", BF16, "d_wv")
    dx2, dg_mix1, dg_kv = _norm_bwd(dx3, x2, [dh1, dhkv], [g_mix1, g_kv], "norm_bwd_mix1")
    (dhf0, dwg0, dwu0, dwd0), recv_att = _ffn_bwd(
        dx2, hf0, gg0, uu0, *ffn_w0, name="ffn_bwd0",
        ex=_scatter_exchange([blocks(dw) for dw in (dwk, dwv, dwq, dwo)]))
    dx1, dg_ffn0 = _norm_bwd(dx2, x1, [dhf0], [g_ffn0], "norm_bwd_ffn0")
    grad_x, dpw, db_rows, ds_rows, dg_mix0 = _pool_bwd(dx1, xin, diff, g_mix0, pw_f, pb_f, ps_f)
    dpw_blocks = dpw.reshape(N_GROUPS, N_DEV, POOL_SHARD, GROUP).transpose(1, 0, 2, 3)
    dpw_blocks = dpw_blocks.reshape(N_DEV, N_GROUPS * POOL_SHARD, GROUP).astype(BF16)
    recv0 = _run_exchange(_scatter_exchange([blocks(dw) for dw in (dwg0, dwu0, dwd0)] + [dpw_blocks]),
                          "scatter_layer0")
    recv = recv0[:3] + recv1 + recv_att + recv0[3:]


    misc = jnp.concatenate([dgk[0:1, 0:HEAD], dgq[0:1, 0:HEAD], loss_rows[0:1, 0:1],
                            seg[:, 0, 0].reshape(1, N_HEADS), seg[:, 0, 1].reshape(1, N_HEADS)], axis=1)
    misc = jnp.pad(misc, ((0, 0), (0, D - misc.shape[1])))
    vec_rows = jnp.concatenate([dg_mix0[0:1], dg_mix1[0:1], dg_ffn0[0:1], dg_ffn1[0:1], dg_kv[0:1],
                                db_rows[0:1], ds_rows[0:1], misc], axis=0)
    pack = jnp.concatenate([vec_rows, dtp.reshape(N_HEADS, TOEP)], axis=0)
    tot = _all_reduce_small(pack, "reduce_small")

    loss = tot[7, 2 * HEAD]
    g_norm_mix = tot[0:2]
    g_norm_ffn = tot[2:4]
    g_kv_norm = tot[4]
    g_k_norm = tot[7, 0:HEAD]
    g_q_norm = tot[7, HEAD:2 * HEAD].reshape(1, HEAD)
    seg_tot = jnp.stack([tot[7, 2 * HEAD + 1:2 * HEAD + 1 + N_HEADS],
                         tot[7, 2 * HEAD + 1 + N_HEADS:2 * HEAD + 1 + 2 * N_HEADS]], axis=1)
    g_rel = _table_grad_from_toeplitz(tot[8:8 + N_HEADS], seg_tot).reshape(1, N_HEADS, N_REL)
    me = 4 * lax.axis_index("x") + 2 * lax.axis_index("y") + lax.axis_index("c")
    g_pool_b = lax.dynamic_slice_in_dim(tot[5].reshape(N_GROUPS, GROUP), me * POOL_SHARD, POOL_SHARD, axis=1)
    g_pool_b = g_pool_b.reshape(1, N_GROUPS, POOL_SHARD)
    g_pool_scale = lax.dynamic_slice_in_dim(tot[6:7], me * D_SHARD, D_SHARD, axis=1)

    sums = [_sum_parts(recv[k], "sum_parts_%d" % k) for k in range(len(recv))]
    g_gate = jnp.stack([sums[0].T, sums[3].T])
    g_up = jnp.stack([sums[1].T, sums[4].T])
    g_down = jnp.stack([sums[2], sums[5]])
    g_wk, g_wv = sums[6], sums[7]
    g_wq, g_wo = sums[8][None], sums[9][None]
    g_pool_w = sums[10].reshape(1, N_GROUPS, POOL_SHARD, GROUP)

    grads = dict(norm_mix_g=g_norm_mix, norm_ffn_g=g_norm_ffn, pool_w=g_pool_w, pool_b=g_pool_b,
                 pool_scale=g_pool_scale, kv_norm_g=g_kv_norm, w_k=g_wk, w_v=g_wv, k_norm_g=g_k_norm, w_q=g_wq,
                 q_norm_g=g_q_norm, rel_bias=g_rel, w_o=g_wo, w_gate=g_gate, w_up=g_up, w_down=g_down)
    weights = dict(norm_mix_g=norm_mix_g, norm_ffn_g=norm_ffn_g, pool_w=pool_w, pool_b=pool_b,
                   pool_scale=pool_scale, kv_norm_g=kv_norm_g, w_k=w_k, w_v=w_v, k_norm_g=k_norm_g, w_q=w_q,
                   q_norm_g=q_norm_g, rel_bias=rel_bias, w_o=w_o, w_gate=w_gate, w_up=w_up, w_down=w_down)
    mom1 = dict(norm_mix_g=m_norm_mix_g, norm_ffn_g=m_norm_ffn_g, pool_w=m_pool_w, pool_b=m_pool_b,
                pool_scale=m_pool_scale, kv_norm_g=m_kv_norm_g, w_k=m_w_k, w_v=m_w_v, k_norm_g=m_k_norm_g,
                w_q=m_w_q, q_norm_g=m_q_norm_g, rel_bias=m_rel_bias, w_o=m_w_o, w_gate=m_w_gate, w_up=m_w_up,
                w_down=m_w_down)
    mom2 = dict(norm_mix_g=v_norm_mix_g, norm_ffn_g=v_norm_ffn_g, pool_w=v_pool_w, pool_b=v_pool_b,
                pool_scale=v_pool_scale, kv_norm_g=v_kv_norm_g, w_k=v_w_k, w_v=v_w_v, k_norm_g=v_k_norm_g,
                w_q=v_w_q, q_norm_g=v_q_norm_g, rel_bias=v_rel_bias, w_o=v_w_o, w_gate=v_w_gate, w_up=v_w_up,
                w_down=v_w_down)
    names = list(grads)

    large = ("w_gate", "w_up", "w_down", "w_k", "w_v", "w_q", "w_o", "pool_w")
    deltas, new_m, new_v = {}, {}, {}
    for nm in large:
        shape = weights[nm].shape
        flat = lambda a: a.reshape(-1, shape[-1])
        dl, m1, m2 = _adam(flat(weights[nm]), flat(grads[nm]), flat(mom1[nm]), flat(mom2[nm]), "adam_" + nm)
        deltas[nm], new_m[nm], new_v[nm] = dl.reshape(shape), m1.reshape(shape), m2.reshape(shape)
    small_names = [nm for nm in names if nm not in large]

    def pack_small(tree):
        cols = []
        for nm in small_names:
            flat = tree[nm].reshape(-1)
            cols.append(jnp.pad(flat, (0, -flat.shape[0] % 1024)))
        return jnp.concatenate(cols).reshape(-1, 128)

    dl, m1, m2 = _adam(pack_small(weights), pack_small(grads), pack_small(mom1), pack_small(mom2), "adam_small")

    def unpack_small(packed, out):
        flat, off = packed.reshape(-1), 0
        for nm in small_names:
            size = weights[nm].size
            out[nm] = flat[off:off + size].reshape(weights[nm].shape)
            off += size + (-size % 1024)

    unpack_small(dl, deltas)
    unpack_small(m1, new_m)
    unpack_small(m2, new_v)

    return (loss, grad_x[None], *[grads[nm] for nm in names], *[deltas[nm] for nm in names],
            *[new_m[nm] for nm in names], *[new_v[nm] for nm in names])
```

```python
import functools

import jax
import jax.numpy as jnp
from jax import lax
from jax.experimental import pallas as pl
from jax.experimental.pallas import tpu as pltpu

F32 = jnp.float32
BF16 = jnp.bfloat16
MESH_ID = pl.DeviceIdType.MESH

N_DEV = 8
S = 2048
D = 1024
F = 2816
F_SHARD = F // N_DEV
D_SHARD = D // N_DEV
N_GROUPS = 4
GROUP = D // N_GROUPS
POOL_SHARD = GROUP // N_DEV
MAX_WIN = 16
HEAD = 64
N_HEADS = D // HEAD
CHUNK = 64
LEFT = 8
QB = 4 * CHUNK
KB = QB + LEFT * CHUNK
PADK = LEFT * CHUNK
TOEP = 1024
N_REL = 257
MAX_REL = 128
EPS = 1e-6
NEG_INF = -1e30
ATTN_SCALE = HEAD ** -0.5

ADAM_LR = 0.001
ADAM_B1 = 0.9
ADAM_B2 = 0.999
ADAM_EPS = 1e-08
ADAM_WD = 0.01
ADAM_STEP = 10

VMEM_LIMIT = 52 * 1024 * 1024

ANY = pl.BlockSpec(memory_space=pl.ANY)
VMEM = pl.BlockSpec(memory_space=pltpu.VMEM)


def _call(body, **kw):
    return pl.pallas_call(body, **kw)


def _params(*sem):
    return pltpu.CompilerParams(dimension_semantics=sem, vmem_limit_bytes=VMEM_LIMIT)


def _dot(a, b, dims):
    return lax.dot_general(a, b, (dims, ((), ())), preferred_element_type=F32)


def _nn(a, b):
    return _dot(a, b, ((1,), (0,)))


def _nt(a, b):
    return _dot(a, b, ((1,), (1,)))


def _tn(a, b):
    return _dot(a, b, ((0,), (0,)))


def _rstd(x):
    return lax.rsqrt(jnp.mean(x * x, axis=-1, keepdims=True) + EPS)


def _rms_bwd(dh, x, r, g):
    gd = dh * g
    return r * gd - x * (r * r * r) * jnp.mean(gd * x, axis=-1, keepdims=True)


def _colsum8(v):
    return jnp.broadcast_to(jnp.sum(v, axis=0, keepdims=True), (8, v.shape[1]))


def _seg_sum(v):
    r = lax.broadcasted_iota(jnp.int32, (128, 128), 0) // HEAD
    c = lax.broadcasted_iota(jnp.int32, (128, 128), 1) // HEAD
    ones = jnp.where(r == c, 1.0, 0.0).astype(BF16)
    out = []
    for blk in range(v.shape[1] // 128):
        part = v[:, blk * 128:(blk + 1) * 128]
        hi = part.astype(BF16)
        rest = part - hi.astype(F32)
        mid = rest.astype(BF16)
        lo = (rest - mid.astype(F32)).astype(BF16)
        out.append(_nn(hi, ones) + _nn(mid, ones) + _nn(lo, ones))
    return jnp.concatenate(out, axis=1)


def _place():
    return lax.axis_index("x"), lax.axis_index("y"), lax.axis_index("c")


class _Exchange:
    def __init__(self, ins, out_shape, sems, start, mid, finish):
        self.ins, self.out_shape, self.sems = list(ins), list(out_shape), list(sems)
        self.start, self.mid, self.finish = start, mid, finish


def _gather_exchange(shards):
    n = len(shards)

    def tools(ins, outs, sems):
        send_sems, recv_sems, local_sems = sems
        x, y, c = _place()
        me, sibling = (x, y, c), (x, y, 1 - c)
        chips = [(1 - x, y), (x, 1 - y), (1 - x, 1 - y)]

        def slot(k, px, py, pc):
            return outs[k].at[4 * px + 2 * py + pc]

        def copy(k, s, block, to, own=False):
            return pltpu.make_async_remote_copy(
                src_ref=ins[k] if own else slot(k, *block), dst_ref=slot(k, *block),
                send_sem=send_sems.at[k, s], recv_sem=recv_sems.at[k, s], device_id=to, device_id_type=MESH_ID)

        def mine():
            return [pltpu.make_async_copy(ins[k], slot(k, *me), local_sems.at[k]) for k in range(n)]

        def first():
            out = []
            for k in range(n):
                out.append(copy(k, 0, me, sibling, own=True))
                out += [copy(k, 1 + j, me, (*chip, c), own=True) for j, chip in enumerate(chips)]
            return out

        def landed(j):
            return [copy(k, 1 + j, (*chips[j], c), me) for k in range(n)]

        def passed(j):
            return [copy(k, 4 + j, (*chips[j], c), sibling) for k in range(n)]

        def from_sibling():
            out = []
            for k in range(n):
                out.append(copy(k, 0, sibling, me))
                out += [copy(k, 4 + j, (*chip, 1 - c), me) for j, chip in enumerate(chips)]
            return out

        return mine, first, landed, passed, from_sibling

    def start(ins, outs, sems):
        mine, first, _, _, _ = tools(ins, outs, sems)
        for cp in mine() + first():
            cp.start()

    def mid(ins, outs, sems):
        _, _, landed, passed, _ = tools(ins, outs, sems)
        for j in range(3):
            for arrived, onward in zip(landed(j), passed(j)):
                arrived.wait_recv()
                onward.start()

    def finish(ins, outs, sems):
        mine, first, _, passed, from_sibling = tools(ins, outs, sems)
        for cp in from_sibling():
            cp.wait_recv()
        for cp in first() + passed(0) + passed(1) + passed(2):
            cp.wait_send()
        for cp in mine():
            cp.wait()

    return _Exchange(
        shards, [jax.ShapeDtypeStruct((N_DEV,) + a.shape, a.dtype) for a in shards],
        [pltpu.SemaphoreType.DMA((n, 7)), pltpu.SemaphoreType.DMA((n, 7)), pltpu.SemaphoreType.DMA((n,))],
        start, mid, finish)


def _peer(x, y, c, m):
    px = 1 - x if m & 4 else x
    py = 1 - y if m & 2 else y
    pc = 1 - c if m & 1 else c
    return px, py, pc


def _scatter_exchange(parts):
    n = len(parts)

    def tools(ins, outs, sems):
        send_sems, recv_sems, local_sems = sems
        x, y, c = _place()
        me = 4 * x + 2 * y + c
        def mine():
            return [pltpu.make_async_copy(ins[k].at[me], outs[k].at[me], local_sems.at[k]) for k in range(n)]

        def remote(dst_is_mine):
            out = []
            for m in range(1, N_DEV):
                px, py, pc = _peer(x, y, c, m)
                peer = 4 * px + 2 * py + pc
                for k in range(n):
                    out.append(pltpu.make_async_remote_copy(
                        src_ref=ins[k].at[peer], dst_ref=outs[k].at[me if dst_is_mine else peer],
                        send_sem=send_sems.at[k, m - 1], recv_sem=recv_sems.at[k, m - 1],
                        device_id=(px, py, pc), device_id_type=MESH_ID))
            return out

        return mine, remote

    def start(ins, outs, sems):
        mine, remote = tools(ins, outs, sems)
        for cp in mine() + remote(True):
            cp.start()

    def mid(ins, outs, sems):
        pass

    def finish(ins, outs, sems):
        mine, remote = tools(ins, outs, sems)
        for cp in remote(False):
            cp.wait_recv()
        for cp in remote(True):
            cp.wait_send()
        for cp in mine():
            cp.wait()

    return _Exchange(
        parts, [jax.ShapeDtypeStruct(a.shape, a.dtype) for a in parts],
        [pltpu.SemaphoreType.DMA((n, 7)), pltpu.SemaphoreType.DMA((n, 7)), pltpu.SemaphoreType.DMA((n,))],
        start, mid, finish)


def _run_exchange(ex, name):
    n_in, n_out = len(ex.ins), len(ex.out_shape)

    def body(*refs):
        ins, outs, sems = refs[:n_in], refs[n_in:n_in + n_out], refs[n_in + n_out:]
        ex.start(ins, outs, sems)
        ex.mid(ins, outs, sems)
        ex.finish(ins, outs, sems)

    return list(_call(body, name=name, out_shape=ex.out_shape, in_specs=[ANY] * n_in, out_specs=[ANY] * n_out,
                      scratch_shapes=ex.sems)(*ex.ins))


def _call_hosting(body, ex, phases, *, in_specs, out_specs, out_shape, scratch_shapes, args, **kw):
    n_in, n_out, n_scr = len(in_specs), len(out_specs), len(scratch_shapes)
    if ex is None:
        res = _call(body, in_specs=in_specs, out_specs=out_specs, out_shape=out_shape,
                    scratch_shapes=scratch_shapes, **kw)(*args)
        return list(res), []
    n_xin, n_xout = len(ex.ins), len(ex.out_shape)

    def hosting(*refs):
        a, b = n_in, n_in + n_xin
        c, d = b + n_out, b + n_out + n_xout
        ins, xins, outs, xouts = refs[:a], refs[a:b], refs[b:c], refs[c:d]
        scr, sems = refs[d:d + n_scr], refs[d + n_scr:]
        first, mid, last = phases()

        @pl.when(first)
        def _():
            ex.start(xins, xouts, sems)

        body(*ins, *outs, *scr)

        @pl.when(mid)
        def _():
            ex.mid(xins, xouts, sems)

        @pl.when(last)
        def _():
            ex.finish(xins, xouts, sems)

    res = _call(hosting, in_specs=list(in_specs) + [ANY] * n_xin, out_specs=list(out_specs) + [ANY] * n_xout,
                out_shape=list(out_shape) + ex.out_shape, scratch_shapes=list(scratch_shapes) + ex.sems,
                **kw)(*args, *ex.ins)
    return list(res[:n_out]), list(res[n_out:])


def _grid_phases(n0, n1, mid0):
    def phases():
        i, j = pl.program_id(0), pl.program_id(1)
        return (i == 0) & (j == 0), (i == mid0) & (j == 0), (i == n0 - 1) & (j == n1 - 1)
    return phases


def _all_reduce_small(pack, name):
    rows = pack.shape[0]

    def body(in_ref, out_ref, recv, send_sems, recv_sems):
        x, y, c = _place()
        me = 4 * x + 2 * y + c
        recv[me] = in_ref[...]
        sent = []
        for m in range(1, N_DEV):
            px, py, pc = _peer(x, y, c, m)
            cp = pltpu.make_async_remote_copy(
                src_ref=in_ref, dst_ref=recv.at[me], send_sem=send_sems.at[m - 1], recv_sem=recv_sems.at[m - 1],
                device_id=(px, py, pc), device_id_type=MESH_ID)
            cp.start()
            sent.append(cp)
        for m in range(1, N_DEV):
            px, py, pc = _peer(x, y, c, m)
            peer = 4 * px + 2 * py + pc
            pltpu.make_async_remote_copy(
                src_ref=in_ref, dst_ref=recv.at[peer], send_sem=send_sems.at[m - 1], recv_sem=recv_sems.at[m - 1],
                device_id=(px, py, pc), device_id_type=MESH_ID).wait_recv()
        acc = recv[0]
        for d in range(1, N_DEV):
            acc = acc + recv[d]
        out_ref[...] = acc
        for cp in sent:
            cp.wait_send()

    return _call(
        body, name=name, out_shape=jax.ShapeDtypeStruct(pack.shape, F32), in_specs=[VMEM], out_specs=VMEM,
        scratch_shapes=[pltpu.VMEM((N_DEV, rows, pack.shape[1]), F32), pltpu.SemaphoreType.DMA((7,)),
                        pltpu.SemaphoreType.DMA((7,))],
        compiler_params=pltpu.CompilerParams(vmem_limit_bytes=VMEM_LIMIT),
    )(pack)


POOL_TS = 256


def _pool_counts(first_row, rows, win):
    t = first_row + lax.broadcasted_iota(jnp.int32, (rows, 1), 0)
    return jnp.minimum(t + 1, win).astype(F32)


def _pool_fwd(x, g, w, b, scale):
    nt = S // POOL_TS

    def body(x_ref, g_ref, w_ref, b_ref, s_ref, out_ref, diff_ref, ext):
        i = pl.program_id(0)

        @pl.when(i == 0)
        def _():
            ext[0:MAX_WIN, :] = jnp.zeros((MAX_WIN, D), F32)

        @pl.when(i > 0)
        def _():
            ext[0:MAX_WIN, :] = ext[POOL_TS:POOL_TS + MAX_WIN, :]

        xv = x_ref[...]
        h = xv * _rstd(xv) * g_ref[...]
        ext[MAX_WIN:, :] = h
        for gi in range(N_GROUPS):
            win = 2 << gi
            cols = slice(gi * GROUP, (gi + 1) * GROUP)
            sm = ext[:, cols]
            k = 1
            while k < win:
                sm = sm + pltpu.roll(sm, k, axis=0)
                k *= 2
            pooled = sm[MAX_WIN:, :] / _pool_counts(i * POOL_TS, POOL_TS, win)
            diff = (pooled - h[:, cols]).astype(BF16)
            yv = (_nn(diff, w_ref[gi]) + b_ref[:, cols]) * s_ref[:, cols]
            out_ref[:, cols] = xv[:, cols] + yv
            diff_ref[:, cols] = diff

    row = pl.BlockSpec((1, D), lambda i: (0, 0))
    tile = pl.BlockSpec((POOL_TS, D), lambda i: (i, 0))
    return _call(
        body, name="pool_fwd", grid=(nt,),
        in_specs=[tile, row, pl.BlockSpec((N_GROUPS, GROUP, GROUP), lambda i: (0, 0, 0)), row, row],
        out_specs=[tile, tile],
        out_shape=[jax.ShapeDtypeStruct((S, D), F32), jax.ShapeDtypeStruct((S, D), BF16)],
        scratch_shapes=[pltpu.VMEM((POOL_TS + MAX_WIN, D), F32)],
        compiler_params=_params("arbitrary"),
    )(x, g, w, b, scale)


def _pool_bwd(dy, x, diff, g, w, b, scale):
    nt = S // POOL_TS

    def body(dy_ref, x_ref, diff_ref, g_ref, w_ref, b_ref, s_ref, gx_ref, dw_ref, db_ref, ds_ref, dg_ref, ext, dh):
        i = pl.program_id(0)
        first_row = (nt - 1 - i) * POOL_TS

        @pl.when(i == 0)
        def _():
            ext[POOL_TS:, :] = jnp.zeros((MAX_WIN, D), F32)
            dw_ref[...] = jnp.zeros(dw_ref.shape, F32)
            db_ref[...] = jnp.zeros(db_ref.shape, F32)
            ds_ref[...] = jnp.zeros(ds_ref.shape, F32)
            dg_ref[...] = jnp.zeros(dg_ref.shape, F32)

        @pl.when(i > 0)
        def _():
            ext[POOL_TS:, :] = ext[0:MAX_WIN, :]

        dyv = dy_ref[...]
        for gi in range(N_GROUPS):
            win = 2 << gi
            cols = slice(gi * GROUP, (gi + 1) * GROUP)
            dfb = diff_ref[:, cols]
            z = _nn(dfb, w_ref[gi]) + b_ref[:, cols]
            dyg = dyv[:, cols]
            ds_ref[:, cols] += _colsum8(dyg * z)
            dz = dyg * s_ref[:, cols]
            db_ref[:, cols] += _colsum8(dz)
            dzb = dz.astype(BF16)
            dw_ref[gi] += _tn(dfb, dzb)
            ddiff = _nt(dzb, w_ref[gi])
            ext[0:POOL_TS, cols] = ddiff / _pool_counts(first_row, POOL_TS, win)
            sm = ext[:, cols]
            k = 1
            while k < win:
                sm = sm + pltpu.roll(sm, POOL_TS + MAX_WIN - k, axis=0)
                k *= 2
            dh[:, cols] = sm[0:POOL_TS, :] - ddiff
        xv = x_ref[...]
        r = _rstd(xv)
        gv = g_ref[...]
        dhv = dh[...]
        dg_ref[...] += _colsum8(dhv * xv * r)
        gx_ref[...] = dyv + _rms_bwd(dhv, xv, r, gv)

    row = pl.BlockSpec((1, D), lambda i: (0, 0))
    tile = pl.BlockSpec((POOL_TS, D), lambda i: (nt - 1 - i, 0))
    acc = pl.BlockSpec((8, D), lambda i: (0, 0))
    wspec = pl.BlockSpec((N_GROUPS, GROUP, GROUP), lambda i: (0, 0, 0))
    return _call(
        body, name="pool_bwd", grid=(nt,),
        in_specs=[tile, tile, tile, row, wspec, row, row],
        out_specs=[tile, wspec, acc, acc, acc],
        out_shape=[jax.ShapeDtypeStruct((S, D), F32), jax.ShapeDtypeStruct((N_GROUPS, GROUP, GROUP), F32),
                   jax.ShapeDtypeStruct((8, D), F32), jax.ShapeDtypeStruct((8, D), F32),
                   jax.ShapeDtypeStruct((8, D), F32)],
        scratch_shapes=[pltpu.VMEM((POOL_TS + MAX_WIN, D), F32), pltpu.VMEM((POOL_TS, D), F32)],
        compiler_params=_params("arbitrary"),
    )(dy, x, diff, g, w, b, scale)


FFN_TS = min(S, 1024)
FFN_TF = 256


def _ffn_fwd(x, g, wg_t, wu_t, wd, target=None, ex=None):
    ni, nj = S // FFN_TS, F // FFN_TF
    with_loss = target is not None

    def body(*refs):
        if with_loss:
            x_ref, g_ref, wg_ref, wu_ref, wd_ref, t_ref, out_ref, loss_ref, h_ref, gg_ref, uu_ref, hs, acc = refs
        else:
            x_ref, g_ref, wg_ref, wu_ref, wd_ref, out_ref, h_ref, gg_ref, uu_ref, hs, acc = refs
        i, j = pl.program_id(0), pl.program_id(1)

        @pl.when(j == 0)
        def _():
            xv = x_ref[...]
            hb = (xv * _rstd(xv) * g_ref[...]).astype(BF16)
            hs[...] = hb
            h_ref[...] = hb
            acc[...] = jnp.zeros(acc.shape, F32)

        hb = hs[...]
        gg = _nt(hb, wg_ref[...])
        uu = _nt(hb, wu_ref[...])
        gg_ref[...] = gg
        uu_ref[...] = uu
        a = (gg * jax.nn.sigmoid(gg) * uu).astype(BF16)
        acc[...] += _nn(a, wd_ref[...])

        @pl.when(j == nj - 1)
        def _():
            yv = x_ref[...] + acc[...]
            if with_loss:
                err = yv - t_ref[...]
                out_ref[...] = err * (1.0 / D)
                part = jnp.sum(err * err) * (0.5 / D)

                @pl.when(i == 0)
                def _():
                    loss_ref[...] = jnp.zeros(loss_ref.shape, F32)

                loss_ref[...] += jnp.broadcast_to(part, loss_ref.shape)
            else:
                out_ref[...] = yv

    xt = pl.BlockSpec((FFN_TS, D), lambda i, j: (i, 0))
    row = pl.BlockSpec((1, D), lambda i, j: (0, 0))
    wt = pl.BlockSpec((FFN_TF, D), lambda i, j: (j, 0))
    gt = pl.BlockSpec((FFN_TS, FFN_TF), lambda i, j: (i, j))
    in_specs = [xt, row, wt, wt, wt] + ([xt] if with_loss else [])
    out_specs = [xt] + ([pl.BlockSpec((8, 128), lambda i, j: (0, 0))] if with_loss else []) + [xt, gt, gt]
    out_shape = ([jax.ShapeDtypeStruct((S, D), F32)] + ([jax.ShapeDtypeStruct((8, 128), F32)] if with_loss else [])
                 + [jax.ShapeDtypeStruct((S, D), BF16), jax.ShapeDtypeStruct((S, F), F32),
                    jax.ShapeDtypeStruct((S, F), F32)])
    args = (x, g, wg_t, wu_t, wd) + ((target,) if with_loss else ())
    return _call_hosting(
        body, ex, _grid_phases(ni, nj, ni - 1), name="ffn_fwd_loss" if with_loss else "ffn_fwd", grid=(ni, nj),
        in_specs=in_specs, out_specs=out_specs, out_shape=out_shape,
        scratch_shapes=[pltpu.VMEM((FFN_TS, D), BF16), pltpu.VMEM((FFN_TS, D), F32)], args=args,
        compiler_params=_params("arbitrary", "arbitrary"),
    )


def _ffn_bwd_weights(dout, h, gg, uu, wd, name, ex=None):
    nj = F // FFN_TF

    def body(do_ref, h_ref, gg_ref, uu_ref, wd_ref, dg_ref, du_ref, dwg_ref, dwu_ref, dwd_ref, dob):
        @pl.when(pl.program_id(0) == 0)
        def _():
            dob[...] = do_ref[...].astype(BF16)

        hb, dov = h_ref[...], dob[...]
        gv, uv = gg_ref[...], uu_ref[...]
        da = _nt(dov, wd_ref[...])
        sg = jax.nn.sigmoid(gv)
        sl = gv * sg
        ab = (sl * uv).astype(BF16)
        dub = (da * sl).astype(BF16)
        dgb = (da * uv * (sg * (1.0 + gv * (1.0 - sg)))).astype(BF16)
        dg_ref[...] = dgb
        du_ref[...] = dub
        dwg_ref[...] = _tn(dgb, hb).astype(BF16)
        dwu_ref[...] = _tn(dub, hb).astype(BF16)
        dwd_ref[...] = _tn(ab, dov).astype(BF16)

    once = pl.Buffered(1)
    whole = lambda: pl.BlockSpec((S, D), lambda j, _: (0, 0), pipeline_mode=once)
    wt = pl.BlockSpec((FFN_TF, D), lambda j, _: (j, 0))
    gt = pl.BlockSpec((S, FFN_TF), lambda j, _: (0, j))
    return _call_hosting(
        body, ex, _grid_phases(nj, 1, nj // 2), name=name, grid=(nj, 1),
        in_specs=[whole(), whole(), gt, gt, wt], out_specs=[gt, gt, wt, wt, wt],
        out_shape=[jax.ShapeDtypeStruct((S, F), BF16)] * 2 + [jax.ShapeDtypeStruct((F, D), BF16)] * 3,
        scratch_shapes=[pltpu.VMEM((S, D), BF16)], args=(dout, h, gg, uu, wd),
        compiler_params=_params("arbitrary", "arbitrary"),
    )


BWD_TS = 512


def _ffn_bwd_input(dres, x, g, dg, du, wg_t, wu_t, name):
    nt = S // BWD_TS

    def body(dres_ref, x_ref, g_ref, dg_ref, du_ref, wg_ref, wu_ref, dx_ref, dgam_ref):
        dh = _nn(dg_ref[...], wg_ref[...]) + _nn(du_ref[...], wu_ref[...])
        xv = x_ref[...]
        r = _rstd(xv)
        dx_ref[...] = dres_ref[...] + _rms_bwd(dh, xv, r, g_ref[...])

        @pl.when(pl.program_id(0) == 0)
        def _():
            dgam_ref[...] = jnp.zeros(dgam_ref.shape, F32)

        dgam_ref[...] += _colsum8(dh * xv * r)

    tile = pl.BlockSpec((BWD_TS, D), lambda i: (i, 0))
    ftile = pl.BlockSpec((BWD_TS, F), lambda i: (i, 0))
    wspec = lambda: pl.BlockSpec((F, D), lambda i: (0, 0), pipeline_mode=pl.Buffered(1))
    return _call(
        body, name=name, grid=(nt,),
        in_specs=[tile, tile, pl.BlockSpec((1, D), lambda i: (0, 0)), ftile, ftile, wspec(), wspec()],
        out_specs=[tile, pl.BlockSpec((8, D), lambda i: (0, 0))],
        out_shape=[jax.ShapeDtypeStruct((S, D), F32), jax.ShapeDtypeStruct((8, D), F32)],
        compiler_params=_params("arbitrary"),
    )(dres, x, g, dg, du, wg_t, wu_t)


NORM_TS = 256


def _norm_bwd(dres, x, dhs, gs, name):
    n = len(dhs)
    nt = S // NORM_TS

    def body(*refs):
        dres_ref, x_ref = refs[:2]
        dh_refs, g_refs = refs[2:2 + n], refs[2 + n:2 + 2 * n]
        dx_ref, dg_refs = refs[2 + 2 * n], refs[3 + 2 * n:]
        i = pl.program_id(0)
        xv = x_ref[...]
        r = _rstd(xv)
        dx = dres_ref[...]
        for k in range(n):
            dhv = dh_refs[k][...]
            dx = dx + _rms_bwd(dhv, xv, r, g_refs[k][...])

            @pl.when(i == 0)
            def _():
                dg_refs[k][...] = jnp.zeros((8, D), F32)

            dg_refs[k][...] += _colsum8(dhv * xv * r)
        dx_ref[...] = dx

    tile = pl.BlockSpec((NORM_TS, D), lambda i: (i, 0))
    row = pl.BlockSpec((1, D), lambda i: (0, 0))
    acc = pl.BlockSpec((8, D), lambda i: (0, 0))
    return _call(
        body, name=name, grid=(nt,),
        in_specs=[tile, tile] + [tile] * n + [row] * n,
        out_specs=[tile] + [acc] * n,
        out_shape=[jax.ShapeDtypeStruct((S, D), F32)] + [jax.ShapeDtypeStruct((8, D), F32)] * n,
        compiler_params=_params("arbitrary"),
    )(dres, x, *dhs, *gs)


def _mm(a, b, mode, out_dtype, name, add=None):
    if mode == "nn":
        (m, kd), n = a.shape, b.shape[1]
    elif mode == "nt":
        (m, kd), n = a.shape, b.shape[0]
    else:
        (kd, m), n = a.shape, b.shape[1]
    tm, tn, tk = min(m, 1024), min(n, 1024), min(kd, 1024)
    nk = kd // tk
    dot = {"nn": _nn, "nt": _nt, "tn": _tn}[mode]

    def body(*refs):
        if add is None:
            a_ref, b_ref, o_ref, acc = refs
        else:
            a_ref, b_ref, add_ref, o_ref, acc = refs
        k = pl.program_id(2)

        @pl.when(k == 0)
        def _():
            acc[...] = jnp.zeros(acc.shape, F32)

        acc[...] += dot(a_ref[...].astype(BF16), b_ref[...].astype(BF16))

        @pl.when(k == nk - 1)
        def _():
            res = acc[...]
            if add is not None:
                res = res + add_ref[...]
            o_ref[...] = res.astype(out_dtype)

    if mode == "tn":
        a_spec = pl.BlockSpec((tk, tm), lambda i, j, k: (k, i))
        b_spec = pl.BlockSpec((tk, tn), lambda i, j, k: (k, j))
    else:
        a_spec = pl.BlockSpec((tm, tk), lambda i, j, k: (i, k))
        b_spec = (pl.BlockSpec((tk, tn), lambda i, j, k: (k, j)) if mode == "nn"
                  else pl.BlockSpec((tn, tk), lambda i, j, k: (j, k)))
    o_spec = pl.BlockSpec((tm, tn), lambda i, j, k: (i, j))
    in_specs = [a_spec, b_spec] + ([o_spec] if add is not None else [])
    args = (a, b) + ((add,) if add is not None else ())
    return _call(
        body, name=name, grid=(m // tm, n // tn, nk), in_specs=in_specs, out_specs=o_spec,
        out_shape=jax.ShapeDtypeStruct((m, n), out_dtype), scratch_shapes=[pltpu.VMEM((tm, tn), F32)],
        compiler_params=_params("parallel", "parallel", "arbitrary"),
    )(*args)


PROJ_TS = 256


def _kvq_proj(x, g_kv, g_mix, wk, wv, wq, gk, gq):
    def body(x_ref, gkv_ref, gmix_ref, wk_ref, wv_ref, wq_ref, gk_ref, gq_ref,
             hkv_ref, h1_ref, kpre_ref, qpre_ref, k_ref, v_ref, q_ref):
        xv = x_ref[...]
        xr = xv * _rstd(xv)
        hkv = (xr * gkv_ref[...]).astype(BF16)
        h1 = (xr * gmix_ref[...]).astype(BF16)
        hkv_ref[...] = hkv
        h1_ref[...] = h1
        kpre = _nn(hkv, wk_ref[...])
        qpre = _nn(h1, wq_ref[...])
        kpre_ref[...] = kpre
        qpre_ref[...] = qpre
        v_ref[...] = _nn(hkv, wv_ref[...]).astype(BF16)
        rk = lax.rsqrt(_seg_sum(kpre * kpre) * (1.0 / HEAD) + EPS)
        k_ref[...] = (kpre * rk * gk_ref[...]).astype(BF16)
        rq = lax.rsqrt(_seg_sum(qpre * qpre) * (1.0 / HEAD) + EPS)
        q_ref[...] = (qpre * rq * gq_ref[...]).astype(BF16)

    tile = pl.BlockSpec((PROJ_TS, D), lambda i: (i, 0))
    row = pl.BlockSpec((1, D), lambda i: (0, 0))
    wspec = pl.BlockSpec((D, D), lambda i: (0, 0))
    bf = jax.ShapeDtypeStruct((S, D), BF16)
    ff = jax.ShapeDtypeStruct((S, D), F32)
    return _call(
        body, name="kvq_proj", grid=(S // PROJ_TS,),
        in_specs=[tile, row, row, wspec, wspec, wspec, row, row],
        out_specs=[tile] * 7, out_shape=[bf, bf, ff, ff, bf, bf, bf],
        compiler_params=_params("arbitrary"),
    )(x, g_kv, g_mix, wk, wv, wq, gk, gq)


def _head_norm_bwd(dout, pre, hg, name, row_off=0):
    nt = S // PROJ_TS

    def body(do_ref, pre_ref, hg_ref, dpre_ref, dhg_ref, acc):
        i = pl.program_id(0)
        dov, pv, hgv = do_ref[...], pre_ref[...], hg_ref[...]
        r = lax.rsqrt(_seg_sum(pv * pv) * (1.0 / HEAD) + EPS)
        gd = dov * hgv
        dpre = r * gd - pv * (r * r * r) * (_seg_sum(gd * pv) * (1.0 / HEAD))
        dpre_ref[...] = dpre.astype(BF16)

        @pl.when(i == 0)
        def _():
            acc[...] = jnp.zeros(acc.shape, F32)

        acc[...] += _colsum8(dov * pv * r)

        @pl.when(i == nt - 1)
        def _():
            full = acc[...]
            fold = full[:, 0:128]
            for blk in range(1, D // 128):
                fold = fold + full[:, blk * 128:(blk + 1) * 128]
            dhg_ref[...] = fold + pltpu.roll(fold, HEAD, axis=1)

    tile = pl.BlockSpec((PROJ_TS, D), lambda i: (i, 0))
    return _call(
        body, name=name, grid=(nt,),
        in_specs=[pl.BlockSpec((PROJ_TS, D), lambda i: (i + row_off, 0)), tile, pl.BlockSpec((1, D), lambda i: (0, 0))],
        out_specs=[tile, pl.BlockSpec((8, 128), lambda i: (0, 0))],
        out_shape=[jax.ShapeDtypeStruct((S, D), BF16), jax.ShapeDtypeStruct((8, 128), F32)],
        scratch_shapes=[pltpu.VMEM((8, D), F32)],
        compiler_params=_params("arbitrary"),
    )(dout, pre, hg)


def _toeplitz_from_table(table):
    far = jnp.broadcast_to(table[:, N_REL - 1:], (N_HEADS, PADK - MAX_REL + 1))
    near = table[:, N_REL - 2::-1]
    past = jnp.broadcast_to(table[:, 0:1], (N_HEADS, MAX_REL))
    wrap = jnp.broadcast_to(table[:, N_REL - 1:], (N_HEADS, QB - 1))
    return jnp.concatenate([far, near, past, wrap], axis=1).reshape(N_HEADS, 1, TOEP)


def _table_grad_from_toeplitz(dtp, seg):
    lo = PADK - MAX_REL + 1
    near = dtp[:, lo + N_REL - 3:lo - 1:-1]
    return jnp.concatenate([seg[:, 1:2], near, seg[:, 0:1]], axis=1)


def _bias_band(tp):
    def body(tp_ref, out_ref):
        bv = jnp.broadcast_to(tp_ref[0], (QB, TOEP))
        row = lax.broadcasted_iota(jnp.int32, (QB, TOEP), 0)
        k = 1
        while k < QB:
            bv = jnp.where((row & k) != 0, pltpu.roll(bv, k, axis=1), bv)
            k *= 2
        out_ref[0] = bv[:, 0:KB]

    return _call(
        body, name="bias_band", grid=(N_HEADS,),
        in_specs=[pl.BlockSpec((1, 1, TOEP), lambda h: (h, 0, 0))],
        out_specs=pl.BlockSpec((1, QB, KB), lambda h: (h, 0, 0)),
        out_shape=jax.ShapeDtypeStruct((N_HEADS, QB, KB), F32),
        compiler_params=_params("parallel"),
    )(tp)


def _bias_grad(dband):
    lo, hi = PADK - MAX_REL + 1, PADK + MAX_REL

    def body(db_ref, dtp_ref, seg_ref):
        bv = jnp.concatenate([db_ref[0], jnp.zeros((QB, TOEP - KB), F32)], axis=1)
        row = lax.broadcasted_iota(jnp.int32, (QB, TOEP), 0)
        k = 1
        while k < QB:
            bv = jnp.where((row & k) != 0, pltpu.roll(bv, TOEP - k, axis=1), bv)
            k *= 2
        col = jnp.sum(bv, axis=0, keepdims=True)
        dtp_ref[0] = col
        u = lax.broadcasted_iota(jnp.int32, (1, TOEP), 1)
        far = jnp.sum(jnp.where((u < lo) | (u > hi + MAX_REL), col, 0.0))
        past = jnp.sum(jnp.where((u >= hi) & (u <= hi + MAX_REL), col, 0.0))
        lane = lax.broadcasted_iota(jnp.int32, (1, 128), 1)
        seg_ref[0] = jnp.where(lane == 0, far, jnp.where(lane == 1, past, 0.0))

    return _call(
        body, name="bias_grad", grid=(N_HEADS,),
        in_specs=[pl.BlockSpec((1, QB, KB), lambda h: (h, 0, 0))],
        out_specs=[pl.BlockSpec((1, 1, TOEP), lambda h: (h, 0, 0)), pl.BlockSpec((1, 1, 128), lambda h: (h, 0, 0))],
        out_shape=[jax.ShapeDtypeStruct((N_HEADS, 1, TOEP), F32), jax.ShapeDtypeStruct((N_HEADS, 1, 128), F32)],
        compiler_params=_params("parallel"),
    )(dband)


N_QB = S // QB
N_HP = D // 128


def _band_mask(cb):
    qc = lax.broadcasted_iota(jnp.int32, (QB, KB), 0) >> 6
    p = lax.broadcasted_iota(jnp.int32, (QB, KB), 1)
    kc = p >> 6
    return (kc >= qc) & (kc <= qc + LEFT) & (p + cb * QB >= PADK)


def _half_mask(hh):
    lane = lax.broadcasted_iota(jnp.int32, (1, 128), 1)
    return jnp.where((lane < HEAD) == (hh == 0), 1.0, 0.0).astype(BF16)


def _probs(qh, kb, bias, mask):
    sc = _nt(qh, kb) * ATTN_SCALE + bias
    sc = jnp.where(mask, sc, NEG_INF)
    e = jnp.exp(sc - jnp.max(sc, axis=-1, keepdims=True))
    return e / jnp.sum(e, axis=-1, keepdims=True)


def _attn_fwd(q, kp, vp, bias, ex=None):
    def body(q_ref, k_ref, v_ref, b_ref, o_ref):
        cb = pl.program_id(1)
        band = pl.ds(pl.multiple_of(cb * QB, QB), KB)
        kb, vb = k_ref[band, :], v_ref[band, :]
        qv = q_ref[...]
        mask = _band_mask(cb)
        low = lax.broadcasted_iota(jnp.int32, (QB, 128), 1) < HEAD
        outs = []
        for hh in range(2):
            pb = _probs(qv * _half_mask(hh), kb, b_ref[hh], mask).astype(BF16)
            outs.append(_nn(pb, vb))
        o_ref[...] = jnp.where(low, outs[0], outs[1]).astype(BF16)

    qspec = pl.BlockSpec((QB, 128), lambda hp, cb: (cb, hp))
    kspec = pl.BlockSpec((PADK + S, 128), lambda hp, cb: (0, hp))
    return _call_hosting(
        body, ex, _grid_phases(N_HP, N_QB, N_HP - 1), name="attn_fwd", grid=(N_HP, N_QB),
        in_specs=[qspec, kspec, kspec, pl.BlockSpec((2, QB, KB), lambda hp, cb: (hp, 0, 0))],
        out_specs=[qspec], out_shape=[jax.ShapeDtypeStruct((S, D), BF16)], scratch_shapes=[],
        args=(q, kp, vp, bias), compiler_params=_params("arbitrary", "arbitrary"),
    )


def _attn_bwd(q, kp, vp, bias, do, ex=None):
    def body(q_ref, k_ref, v_ref, b_ref, do_ref, dq_ref, dk_ref, dv_ref, db_ref):
        cb = pl.program_id(1)

        @pl.when(cb == 0)
        def _():
            dk_ref[...] = jnp.zeros(dk_ref.shape, F32)
            dv_ref[...] = jnp.zeros(dv_ref.shape, F32)
            db_ref[...] = jnp.zeros(db_ref.shape, F32)

        band = pl.ds(pl.multiple_of(cb * QB, QB), KB)
        kb, vb = k_ref[band, :], v_ref[band, :]
        qv, dov = q_ref[...], do_ref[...]
        mask = _band_mask(cb)
        low = lax.broadcasted_iota(jnp.int32, (QB, 128), 1) < HEAD
        dq = jnp.zeros((QB, 128), F32)
        dkb = jnp.zeros((KB, 128), F32)
        dvb = jnp.zeros((KB, 128), F32)
        for hh in range(2):
            sel = low if hh == 0 else jnp.logical_not(low)
            qh = qv * _half_mask(hh)
            doh = dov * _half_mask(hh)
            p = _probs(qh, kb, b_ref[hh], mask)
            dp = _nt(doh, vb)
            dvb = dvb + _tn(p.astype(BF16), doh)
            ds = p * (dp - jnp.sum(dp * p, axis=-1, keepdims=True))
            db_ref[hh] += ds
            dsb = (ds * ATTN_SCALE).astype(BF16)
            dq = dq + jnp.where(sel, _nn(dsb, kb), 0.0)
            dkb = dkb + _tn(dsb, qh)
        dq_ref[...] = dq
        dk_ref[band, :] += dkb
        dv_ref[band, :] += dvb

    qspec = pl.BlockSpec((QB, 128), lambda hp, cb: (cb, hp))
    kspec = pl.BlockSpec((PADK + S, 128), lambda hp, cb: (0, hp))
    bspec = pl.BlockSpec((2, QB, KB), lambda hp, cb: (hp, 0, 0))
    kf = jax.ShapeDtypeStruct((PADK + S, D), F32)
    return _call_hosting(
        body, ex, _grid_phases(N_HP, N_QB, N_HP // 2), name="attn_bwd", grid=(N_HP, N_QB),
        in_specs=[qspec, kspec, kspec, bspec, qspec],
        out_specs=[qspec, kspec, kspec, bspec],
        out_shape=[jax.ShapeDtypeStruct((S, D), F32), kf, kf, jax.ShapeDtypeStruct((N_HEADS, QB, KB), F32)],
        scratch_shapes=[], args=(q, kp, vp, bias, do), compiler_params=_params("arbitrary", "arbitrary"),
    )


def _adam_math(w, g, m, v):
    m = ADAM_B1 * m + (1.0 - ADAM_B1) * g
    v = ADAM_B2 * v + (1.0 - ADAM_B2) * (g * g)
    m_hat = m / (1.0 - ADAM_B1 ** ADAM_STEP)
    v_hat = v / (1.0 - ADAM_B2 ** ADAM_STEP)
    delta = -ADAM_LR * (m_hat / (jnp.sqrt(v_hat) + ADAM_EPS) + ADAM_WD * w)
    return delta, m, v


def _sum_parts(parts, name):
    _, r, c = parts.shape
    tr = r // 2 if r % 16 == 0 and r > 64 else r

    def body(p_ref, o_ref):
        acc = p_ref[0].astype(F32)
        for d in range(1, N_DEV):
            acc = acc + p_ref[d].astype(F32)
        o_ref[...] = acc

    return _call(
        body, name=name, grid=(r // tr,),
        in_specs=[pl.BlockSpec((N_DEV, tr, c), lambda i: (0, i, 0))],
        out_specs=pl.BlockSpec((tr, c), lambda i: (i, 0)),
        out_shape=jax.ShapeDtypeStruct((r, c), F32), compiler_params=_params("parallel"),
    )(parts)


def _adam(w, g, m, v, name):
    r, c = w.shape
    tr = r
    for cand in (256, 176, 128, 64, 32, 16, 8):
        if r % cand == 0:
            tr = cand
            break

    def body(w_ref, g_ref, m_ref, v_ref, d_ref, nm_ref, nv_ref):
        delta, nm, nv = _adam_math(w_ref[...], g_ref[...], m_ref[...], v_ref[...])
        d_ref[...] = delta
        nm_ref[...] = nm
        nv_ref[...] = nv

    spec = pl.BlockSpec((tr, c), lambda i: (i, 0))
    sd = jax.ShapeDtypeStruct((r, c), F32)
    return _call(
        body, name=name, grid=(r // tr,), in_specs=[spec] * 4, out_specs=[spec] * 3, out_shape=[sd] * 3,
        compiler_params=_params("parallel"),
    )(w, g, m, v)


def _pad8(rows):
    return jnp.pad(rows, ((0, 8 - rows.shape[0]), (0, 0)))


def kernel(x, norm_mix_g, norm_ffn_g, pool_w, pool_b, pool_scale, kv_norm_g, w_k, w_v, k_norm_g, w_q, q_norm_g, rel_bias, w_o, w_gate, w_up, w_down, loss_target, m_norm_mix_g, m_norm_ffn_g, m_pool_w, m_pool_b, m_pool_scale, m_kv_norm_g, m_w_k, m_w_v, m_k_norm_g, m_w_q, m_q_norm_g, m_rel_bias, m_w_o, m_w_gate, m_w_up, m_w_down, v_norm_mix_g, v_norm_ffn_g, v_pool_w, v_pool_b, v_pool_scale, v_kv_norm_g, v_w_k, v_w_v, v_k_norm_g, v_w_q, v_q_norm_g, v_rel_bias, v_w_o, v_w_gate, v_w_up, v_w_down):
    assert x.shape == (1, S, D) and w_gate.shape == (2, D, F_SHARD) and w_k.shape == (D_SHARD, D)
    xin, target = x[0], loss_target[0]

    ffn_shards = [[w_gate[layer].T.astype(BF16), w_up[layer].T.astype(BF16), w_down[layer].astype(BF16)]
                  for layer in range(2)]
    att_shards = [w_k.astype(BF16), w_v.astype(BF16), w_q[0].astype(BF16), w_o[0].astype(BF16)]
    pool_shard = pool_w[0].astype(BF16).reshape(N_GROUPS * POOL_SHARD, GROUP)
    small = jnp.concatenate([pool_b[0].reshape(1, N_GROUPS * POOL_SHARD), pool_scale], axis=1)

    full0 = _run_exchange(_gather_exchange(ffn_shards[0] + [pool_shard, _pad8(small)]), "gather_layer0")
    ffn_w0 = [a.reshape(F, D) for a in full0[:3]]
    pw_f = full0[3].reshape(N_DEV, N_GROUPS, POOL_SHARD, GROUP).transpose(1, 0, 2, 3).reshape(N_GROUPS, GROUP, GROUP)
    small_f = full0[4][:, 0, :]
    pb_f = small_f[:, :N_GROUPS * POOL_SHARD].reshape(N_DEV, N_GROUPS, POOL_SHARD).transpose(1, 0, 2).reshape(1, D)
    ps_f = small_f[:, N_GROUPS * POOL_SHARD:].reshape(1, D)

    g_mix0, g_mix1 = norm_mix_g[0:1], norm_mix_g[1:2]
    g_ffn0, g_ffn1 = norm_ffn_g[0:1], norm_ffn_g[1:2]
    g_kv = kv_norm_g.reshape(1, D)
    gk_t = jnp.tile(k_norm_g.reshape(1, HEAD), (1, N_HEADS))
    gq_t = jnp.tile(q_norm_g.reshape(1, HEAD), (1, N_HEADS))

    x1, diff = _pool_fwd(xin, g_mix0, pw_f, pb_f, ps_f)
    (x2, hf0, gg0, uu0), full_att = _ffn_fwd(x1, g_ffn0, *ffn_w0, ex=_gather_exchange(att_shards))
    wk_f, wv_f, wq_f, wo_f = [a.reshape(D, D) for a in full_att]
    hkv, h1, kpre, qpre, kk, vv, qq = _kvq_proj(x2, g_kv, g_mix1, wk_f, wv_f, wq_f, gk_t, gq_t)
    kp = jnp.pad(kk, ((PADK, 0), (0, 0)))
    vp = jnp.pad(vv, ((PADK, 0), (0, 0)))
    bias = _bias_band(_toeplitz_from_table(rel_bias[0]))
    (att,), full1 = _attn_fwd(qq, kp, vp, bias, ex=_gather_exchange(ffn_shards[1]))
    ffn_w1 = [a.reshape(F, D) for a in full1]
    x3 = _mm(att, wo_f, "nn", F32, "attn_out", add=x2)
    (dx4, loss_rows, hf1, gg1, uu1), _ = _ffn_fwd(x3, g_ffn1, *ffn_w1, target=target)

    def blocks(dw):
        return dw.reshape(N_DEV, dw.shape[0] // N_DEV, dw.shape[1])

    (dgg1, duu1, dwg1, dwu1, dwd1), _ = _ffn_bwd_weights(dx4, hf1, gg1, uu1, ffn_w1[2], name="ffn_bwd1")
    dx3, dg_ffn1 = _ffn_bwd_input(dx4, x3, g_ffn1, dgg1, duu1, ffn_w1[0], ffn_w1[1], "ffn_dx1")
    datt = _mm(dx3, wo_f, "nt", BF16, "d_attn")
    dwo = _mm(att, dx3, "tn", BF16, "d_wo")
    (dq, dkp, dvp, dband), recv1 = _attn_bwd(
        qq, kp, vp, bias, datt, ex=_scatter_exchange([blocks(dw) for dw in (dwg1, dwu1, dwd1)]))
    dtp, seg = _bias_grad(dband)
    dqpre, dgq = _head_norm_bwd(dq, qpre, gq_t, "q_norm_bwd")
    dkpre, dgk = _head_norm_bwd(dkp, kpre, gk_t, "k_norm_bwd", row_off=PADK // PROJ_TS)
    dh1 = _mm(dqpre, wq_f, "nt", F32, "d_h1")
    dwq = _mm(h1, dqpre, "tn", BF16, "d_wq")
    dhkv = _mm(dkpre, wk_f, "nt", F32, "d_hkv_k")
    dv = dvp[PADK:]
    dhkv = _mm(dv, wv_f, "nt", F32, "d_hkv_v", add=dhkv)
    dwk = _mm(hkv, dkpre, "tn", BF16, "d_wk")
    dwv = _mm(hkv, dv, "tn", BF16, "d_wv")
    dx2, dg_mix1, dg_kv = _norm_bwd(dx3, x2, [dh1, dhkv], [g_mix1, g_kv], "norm_bwd_mix1")
    (dgg0, duu0, dwg0, dwu0, dwd0), recv_att = _ffn_bwd_weights(
        dx2, hf0, gg0, uu0, ffn_w0[2], name="ffn_bwd0",
        ex=_scatter_exchange([blocks(dw) for dw in (dwk, dwv, dwq, dwo)]))
    dx1, dg_ffn0 = _ffn_bwd_input(dx2, x1, g_ffn0, dgg0, duu0, ffn_w0[0], ffn_w0[1], "ffn_dx0")
    grad_x, dpw, db_rows, ds_rows, dg_mix0 = _pool_bwd(dx1, xin, diff, g_mix0, pw_f, pb_f, ps_f)
    dpw_blocks = dpw.reshape(N_GROUPS, N_DEV, POOL_SHARD, GROUP).transpose(1, 0, 2, 3)
    dpw_blocks = dpw_blocks.reshape(N_DEV, N_GROUPS * POOL_SHARD, GROUP).astype(BF16)
    recv0 = _run_exchange(_scatter_exchange([blocks(dw) for dw in (dwg0, dwu0, dwd0)] + [dpw_blocks]),
                          "scatter_layer0")
    recv = recv0[:3] + recv1 + recv_att + recv0[3:]


    misc = jnp.concatenate([dgk[0:1, 0:HEAD], dgq[0:1, 0:HEAD], loss_rows[0:1, 0:1],
                            seg[:, 0, 0].reshape(1, N_HEADS), seg[:, 0, 1].reshape(1, N_HEADS)], axis=1)
    misc = jnp.pad(misc, ((0, 0), (0, D - misc.shape[1])))
    vec_rows = jnp.concatenate([dg_mix0[0:1], dg_mix1[0:1], dg_ffn0[0:1], dg_ffn1[0:1], dg_kv[0:1],
                                db_rows[0:1], ds_rows[0:1], misc], axis=0)
    pack = jnp.concatenate([vec_rows, dtp.reshape(N_HEADS, TOEP)], axis=0)
    tot = _all_reduce_small(pack, "reduce_small")

    loss = tot[7, 2 * HEAD]
    g_norm_mix = tot[0:2]
    g_norm_ffn = tot[2:4]
    g_kv_norm = tot[4]
    g_k_norm = tot[7, 0:HEAD]
    g_q_norm = tot[7, HEAD:2 * HEAD].reshape(1, HEAD)
    seg_tot = jnp.stack([tot[7, 2 * HEAD + 1:2 * HEAD + 1 + N_HEADS],
                         tot[7, 2 * HEAD + 1 + N_HEADS:2 * HEAD + 1 + 2 * N_HEADS]], axis=1)
    g_rel = _table_grad_from_toeplitz(tot[8:8 + N_HEADS], seg_tot).reshape(1, N_HEADS, N_REL)
    me = 4 * lax.axis_index("x") + 2 * lax.axis_index("y") + lax.axis_index("c")
    g_pool_b = lax.dynamic_slice_in_dim(tot[5].reshape(N_GROUPS, GROUP), me * POOL_SHARD, POOL_SHARD, axis=1)
    g_pool_b = g_pool_b.reshape(1, N_GROUPS, POOL_SHARD)
    g_pool_scale = lax.dynamic_slice_in_dim(tot[6:7], me * D_SHARD, D_SHARD, axis=1)

    sums = [_sum_parts(recv[k], "sum_parts_%d" % k) for k in range(len(recv))]
    g_gate = jnp.stack([sums[0].T, sums[3].T])
    g_up = jnp.stack([sums[1].T, sums[4].T])
    g_down = jnp.stack([sums[2], sums[5]])
    g_wk, g_wv = sums[6], sums[7]
    g_wq, g_wo = sums[8][None], sums[9][None]
    g_pool_w = sums[10].reshape(1, N_GROUPS, POOL_SHARD, GROUP)

    grads = dict(norm_mix_g=g_norm_mix, norm_ffn_g=g_norm_ffn, pool_w=g_pool_w, pool_b=g_pool_b,
                 pool_scale=g_pool_scale, kv_norm_g=g_kv_norm, w_k=g_wk, w_v=g_wv, k_norm_g=g_k_norm, w_q=g_wq,
                 q_norm_g=g_q_norm, rel_bias=g_rel, w_o=g_wo, w_gate=g_gate, w_up=g_up, w_down=g_down)
    weights = dict(norm_mix_g=norm_mix_g, norm_ffn_g=norm_ffn_g, pool_w=pool_w, pool_b=pool_b,
                   pool_scale=pool_scale, kv_norm_g=kv_norm_g, w_k=w_k, w_v=w_v, k_norm_g=k_norm_g, w_q=w_q,
                   q_norm_g=q_norm_g, rel_bias=rel_bias, w_o=w_o, w_gate=w_gate, w_up=w_up, w_down=w_down)
    mom1 = dict(norm_mix_g=m_norm_mix_g, norm_ffn_g=m_norm_ffn_g, pool_w=m_pool_w, pool_b=m_pool_b,
                pool_scale=m_pool_scale, kv_norm_g=m_kv_norm_g, w_k=m_w_k, w_v=m_w_v, k_norm_g=m_k_norm_g,
                w_q=m_w_q, q_norm_g=m_q_norm_g, rel_bias=m_rel_bias, w_o=m_w_o, w_gate=m_w_gate, w_up=m_w_up,
                w_down=m_w_down)
    mom2 = dict(norm_mix_g=v_norm_mix_g, norm_ffn_g=v_norm_ffn_g, pool_w=v_pool_w, pool_b=v_pool_b,
                pool_scale=v_pool_scale, kv_norm_g=v_kv_norm_g, w_k=v_w_k, w_v=v_w_v, k_norm_g=v_k_norm_g,
                w_q=v_w_q, q_norm_g=v_q_norm_g, rel_bias=v_rel_bias, w_o=v_w_o, w_gate=v_w_gate, w_up=v_w_up,
                w_down=v_w_down)
    names = list(grads)

    large = ("w_gate", "w_up", "w_down", "w_k", "w_v", "w_q", "w_o", "pool_w")
    deltas, new_m, new_v = {}, {}, {}
    for nm in large:
        shape = weights[nm].shape
        flat = lambda a: a.reshape(-1, shape[-1])
        dl, m1, m2 = _adam(flat(weights[nm]), flat(grads[nm]), flat(mom1[nm]), flat(mom2[nm]), "adam_" + nm)
        deltas[nm], new_m[nm], new_v[nm] = dl.reshape(shape), m1.reshape(shape), m2.reshape(shape)
    small_names = [nm for nm in names if nm not in large]

    def pack_small(tree):
        cols = []
        for nm in small_names:
            flat = tree[nm].reshape(-1)
            cols.append(jnp.pad(flat, (0, -flat.shape[0] % 1024)))
        return jnp.concatenate(cols).reshape(-1, 128)

    dl, m1, m2 = _adam(pack_small(weights), pack_small(grads), pack_small(mom1), pack_small(mom2), "adam_small")

    def unpack_small(packed, out):
        flat, off = packed.reshape(-1), 0
        for nm in small_names:
            size = weights[nm].size
            out[nm] = flat[off:off + size].reshape(weights[nm].shape)
            off += size + (-size % 1024)

    unpack_small(dl, deltas)
    unpack_small(m1, new_m)
    unpack_small(m2, new_v)

    return (loss, grad_x[None], *[grads[nm] for nm in names], *[deltas[nm] for nm in names],
            *[new_m[nm] for nm in names], *[new_v[nm] for nm in names])
```

```python
import functools

import jax
import jax.numpy as jnp
from jax import lax
from jax.experimental import pallas as pl
from jax.experimental.pallas import tpu as pltpu

F32 = jnp.float32
BF16 = jnp.bfloat16
MESH_ID = pl.DeviceIdType.MESH

N_DEV = 8
S = 2048
D = 1024
F = 2816
F_SHARD = F // N_DEV
D_SHARD = D // N_DEV
N_GROUPS = 4
GROUP = D // N_GROUPS
POOL_SHARD = GROUP // N_DEV
MAX_WIN = 16
HEAD = 64
N_HEADS = D // HEAD
CHUNK = 64
LEFT = 8
QB = 4 * CHUNK
KB = QB + LEFT * CHUNK
PADK = LEFT * CHUNK
TOEP = 1024
N_REL = 257
MAX_REL = 128
EPS = 1e-6
NEG_INF = -1e30
ATTN_SCALE = HEAD ** -0.5

ADAM_LR = 0.001
ADAM_B1 = 0.9
ADAM_B2 = 0.999
ADAM_EPS = 1e-08
ADAM_WD = 0.01
ADAM_STEP = 10

VMEM_LIMIT = 52 * 1024 * 1024

ANY = pl.BlockSpec(memory_space=pl.ANY)
VMEM = pl.BlockSpec(memory_space=pltpu.VMEM)


def _call(body, **kw):
    return pl.pallas_call(body, **kw)


def _params(*sem):
    return pltpu.CompilerParams(dimension_semantics=sem, vmem_limit_bytes=VMEM_LIMIT)


def _dot(a, b, dims):
    return lax.dot_general(a, b, (dims, ((), ())), preferred_element_type=F32)


def _nn(a, b):
    return _dot(a, b, ((1,), (0,)))


def _nt(a, b):
    return _dot(a, b, ((1,), (1,)))


def _tn(a, b):
    return _dot(a, b, ((0,), (0,)))


def _rstd(x):
    return lax.rsqrt(jnp.mean(x * x, axis=-1, keepdims=True) + EPS)


def _rms_bwd(dh, x, r, g):
    gd = dh * g
    return r * gd - x * (r * r * r) * jnp.mean(gd * x, axis=-1, keepdims=True)


def _colsum8(v):
    return jnp.broadcast_to(jnp.sum(v, axis=0, keepdims=True), (8, v.shape[1]))


def _seg_sum(v):
    r = lax.broadcasted_iota(jnp.int32, (128, 128), 0) // HEAD
    c = lax.broadcasted_iota(jnp.int32, (128, 128), 1) // HEAD
    ones = jnp.where(r == c, 1.0, 0.0).astype(BF16)
    out = []
    for blk in range(v.shape[1] // 128):
        part = v[:, blk * 128:(blk + 1) * 128]
        hi = part.astype(BF16)
        rest = part - hi.astype(F32)
        mid = rest.astype(BF16)
        lo = (rest - mid.astype(F32)).astype(BF16)
        out.append(_nn(hi, ones) + _nn(mid, ones) + _nn(lo, ones))
    return jnp.concatenate(out, axis=1)


def _place():
    return lax.axis_index("x"), lax.axis_index("y"), lax.axis_index("c")


class _Exchange:
    def __init__(self, ins, out_shape, sems, start, mid, finish):
        self.ins, self.out_shape, self.sems = list(ins), list(out_shape), list(sems)
        self.start, self.mid, self.finish = start, mid, finish


def _gather_exchange(shards):
    n = len(shards)

    def tools(ins, outs, sems):
        send_sems, recv_sems, local_sems = sems
        x, y, c = _place()
        me, sibling = (x, y, c), (x, y, 1 - c)
        chips = [(1 - x, y), (x, 1 - y), (1 - x, 1 - y)]

        def slot(k, px, py, pc):
            return outs[k].at[4 * px + 2 * py + pc]

        def copy(k, s, block, to, own=False):
            return pltpu.make_async_remote_copy(
                src_ref=ins[k] if own else slot(k, *block), dst_ref=slot(k, *block),
                send_sem=send_sems.at[k, s], recv_sem=recv_sems.at[k, s], device_id=to, device_id_type=MESH_ID)

        def mine():
            return [pltpu.make_async_copy(ins[k], slot(k, *me), local_sems.at[k]) for k in range(n)]

        def first():
            out = []
            for k in range(n):
                out.append(copy(k, 0, me, sibling, own=True))
                out += [copy(k, 1 + j, me, (*chip, c), own=True) for j, chip in enumerate(chips)]
            return out

        def landed(j):
            return [copy(k, 1 + j, (*chips[j], c), me) for k in range(n)]

        def passed(j):
            return [copy(k, 4 + j, (*chips[j], c), sibling) for k in range(n)]

        def from_sibling():
            out = []
            for k in range(n):
                out.append(copy(k, 0, sibling, me))
                out += [copy(k, 4 + j, (*chip, 1 - c), me) for j, chip in enumerate(chips)]
            return out

        return mine, first, landed, passed, from_sibling

    def start(ins, outs, sems):
        mine, first, _, _, _ = tools(ins, outs, sems)
        for cp in mine() + first():
            cp.start()

    def mid(ins, outs, sems):
        _, _, landed, passed, _ = tools(ins, outs, sems)
        for j in range(3):
            for arrived, onward in zip(landed(j), passed(j)):
                arrived.wait_recv()
                onward.start()

    def finish(ins, outs, sems):
        mine, first, _, passed, from_sibling = tools(ins, outs, sems)
        for cp in from_sibling():
            cp.wait_recv()
        for cp in first() + passed(0) + passed(1) + passed(2):
            cp.wait_send()
        for cp in mine():
            cp.wait()

    return _Exchange(
        shards, [jax.ShapeDtypeStruct((N_DEV,) + a.shape, a.dtype) for a in shards],
        [pltpu.SemaphoreType.DMA((n, 7)), pltpu.SemaphoreType.DMA((n, 7)), pltpu.SemaphoreType.DMA((n,))],
        start, mid, finish)


def _peer(x, y, c, m):
    px = 1 - x if m & 4 else x
    py = 1 - y if m & 2 else y
    pc = 1 - c if m & 1 else c
    return px, py, pc


def _scatter_exchange(parts):
    n = len(parts)

    def tools(ins, outs, sems):
        send_sems, recv_sems, local_sems = sems
        x, y, c = _place()
        me = 4 * x + 2 * y + c
        def mine():
            return [pltpu.make_async_copy(ins[k].at[me], outs[k].at[me], local_sems.at[k]) for k in range(n)]

        def remote(dst_is_mine):
            out = []
            for m in range(1, N_DEV):
                px, py, pc = _peer(x, y, c, m)
                peer = 4 * px + 2 * py + pc
                for k in range(n):
                    out.append(pltpu.make_async_remote_copy(
                        src_ref=ins[k].at[peer], dst_ref=outs[k].at[me if dst_is_mine else peer],
                        send_sem=send_sems.at[k, m - 1], recv_sem=recv_sems.at[k, m - 1],
                        device_id=(px, py, pc), device_id_type=MESH_ID))
            return out

        return mine, remote

    def start(ins, outs, sems):
        mine, remote = tools(ins, outs, sems)
        for cp in mine() + remote(True):
            cp.start()

    def mid(ins, outs, sems):
        pass

    def finish(ins, outs, sems):
        mine, remote = tools(ins, outs, sems)
        for cp in remote(False):
            cp.wait_recv()
        for cp in remote(True):
            cp.wait_send()
        for cp in mine():
            cp.wait()

    return _Exchange(
        parts, [jax.ShapeDtypeStruct(a.shape, a.dtype) for a in parts],
        [pltpu.SemaphoreType.DMA((n, 7)), pltpu.SemaphoreType.DMA((n, 7)), pltpu.SemaphoreType.DMA((n,))],
        start, mid, finish)


N_CHIPS = N_DEV // 2


def _pair_exchange(parts):
    n = len(parts)

    def copies(ins, outs, sems):
        send_sems, recv_sems = sems
        x, y, c = _place()
        return [pltpu.make_async_remote_copy(
            src_ref=ins[k].at[2 * q + 1 - c], dst_ref=outs[k].at[q], send_sem=send_sems.at[k, q],
            recv_sem=recv_sems.at[k, q], device_id=(x, y, 1 - c), device_id_type=MESH_ID)
            for k in range(n) for q in range(N_CHIPS)]

    def start(ins, outs, sems):
        for cp in copies(ins, outs, sems):
            cp.start()

    def mid(ins, outs, sems):
        pass

    def finish(ins, outs, sems):
        for cp in copies(ins, outs, sems):
            cp.wait_recv()
        for cp in copies(ins, outs, sems):
            cp.wait_send()

    return _Exchange(
        parts, [jax.ShapeDtypeStruct((N_CHIPS,) + a.shape[1:], a.dtype) for a in parts],
        [pltpu.SemaphoreType.DMA((n, N_CHIPS)), pltpu.SemaphoreType.DMA((n, N_CHIPS))], start, mid, finish)


def _pair_add(parts, stage, name):
    n = len(parts)
    core = lax.axis_index("c").reshape(1)

    def body(core_ref, *refs):
        for k in range(n):
            mine, theirs, out = refs[k], refs[n + k], refs[2 * n + k]
            out[0] = (mine[0, 0].astype(F32) + theirs[0].astype(F32)).astype(BF16)

    in_specs, out_specs = [], []
    for a in parts:
        _, r, cdim = a.shape
        in_specs.append(pl.BlockSpec((1, 1, r // 2, cdim), lambda q, i, core_ref: (q, core_ref[0], i, 0)))
    for a in parts:
        _, r, cdim = a.shape
        in_specs.append(pl.BlockSpec((1, r // 2, cdim), lambda q, i, core_ref: (q, i, 0)))
        out_specs.append(pl.BlockSpec((1, r // 2, cdim), lambda q, i, core_ref: (q, i, 0)))
    return list(_call(
        body, name=name,
        grid_spec=pltpu.PrefetchScalarGridSpec(num_scalar_prefetch=1, grid=(N_CHIPS, 2), in_specs=in_specs,
                                               out_specs=out_specs),
        out_shape=[jax.ShapeDtypeStruct(s.shape, BF16) for s in stage],
        compiler_params=_params("arbitrary", "arbitrary"),
    )(core, *[a.reshape((N_CHIPS, 2) + a.shape[1:]) for a in parts], *stage))


def _chip_exchange(chip_parts):
    n = len(chip_parts)

    def tools(ins, outs, sems):
        send_sems, recv_sems, local_sems = sems
        x, y, c = _place()
        me = 2 * x + y

        def mine():
            return [pltpu.make_async_copy(ins[k].at[me], outs[k].at[me], local_sems.at[k]) for k in range(n)]

        def remote(dst_is_mine):
            out = []
            for m in range(1, N_CHIPS):
                px, py, _ = _peer(x, y, c, 2 * m)
                peer = 2 * px + py
                for k in range(n):
                    out.append(pltpu.make_async_remote_copy(
                        src_ref=ins[k].at[peer], dst_ref=outs[k].at[me if dst_is_mine else peer],
                        send_sem=send_sems.at[k, m - 1], recv_sem=recv_sems.at[k, m - 1],
                        device_id=(px, py, c), device_id_type=MESH_ID))
            return out

        return mine, remote

    def start(ins, outs, sems):
        mine, remote = tools(ins, outs, sems)
        for cp in mine() + remote(True):
            cp.start()

    def mid(ins, outs, sems):
        pass

    def finish(ins, outs, sems):
        mine, remote = tools(ins, outs, sems)
        for cp in remote(False):
            cp.wait_recv()
        for cp in remote(True):
            cp.wait_send()
        for cp in mine():
            cp.wait()

    return _Exchange(
        chip_parts, [jax.ShapeDtypeStruct(a.shape, a.dtype) for a in chip_parts],
        [pltpu.SemaphoreType.DMA((n, N_CHIPS - 1)), pltpu.SemaphoreType.DMA((n, N_CHIPS - 1)),
         pltpu.SemaphoreType.DMA((n,))], start, mid, finish)


def _run_exchange(ex, name):
    n_in, n_out = len(ex.ins), len(ex.out_shape)

    def body(*refs):
        ins, outs, sems = refs[:n_in], refs[n_in:n_in + n_out], refs[n_in + n_out:]
        ex.start(ins, outs, sems)
        ex.mid(ins, outs, sems)
        ex.finish(ins, outs, sems)

    return list(_call(body, name=name, out_shape=ex.out_shape, in_specs=[ANY] * n_in, out_specs=[ANY] * n_out,
                      scratch_shapes=ex.sems)(*ex.ins))


def _call_hosting(body, ex, phases, *, in_specs, out_specs, out_shape, scratch_shapes, args, **kw):
    n_in, n_out, n_scr = len(in_specs), len(out_specs), len(scratch_shapes)
    if ex is None:
        res = _call(body, in_specs=in_specs, out_specs=out_specs, out_shape=out_shape,
                    scratch_shapes=scratch_shapes, **kw)(*args)
        return list(res), []
    n_xin, n_xout = len(ex.ins), len(ex.out_shape)

    def hosting(*refs):
        a, b = n_in, n_in + n_xin
        c, d = b + n_out, b + n_out + n_xout
        ins, xins, outs, xouts = refs[:a], refs[a:b], refs[b:c], refs[c:d]
        scr, sems = refs[d:d + n_scr], refs[d + n_scr:]
        first, mid, last = phases()

        @pl.when(first)
        def _():
            ex.start(xins, xouts, sems)

        body(*ins, *outs, *scr)

        @pl.when(mid)
        def _():
            ex.mid(xins, xouts, sems)

        @pl.when(last)
        def _():
            ex.finish(xins, xouts, sems)

    res = _call(hosting, in_specs=list(in_specs) + [ANY] * n_xin, out_specs=list(out_specs) + [ANY] * n_xout,
                out_shape=list(out_shape) + ex.out_shape, scratch_shapes=list(scratch_shapes) + ex.sems,
                **kw)(*args, *ex.ins)
    return list(res[:n_out]), list(res[n_out:])


def _grid_phases(dims, mid_fraction=0.8):
    total = 1
    for d in dims:
        total *= d
    mid = min(max(int(total * mid_fraction), 1), total - 1)

    def phases():
        step = pl.program_id(0)
        for axis in range(1, len(dims)):
            step = step * dims[axis] + pl.program_id(axis)
        return step == 0, step == mid, step == total - 1
    return phases


def _all_reduce_small(pack, name):
    rows = pack.shape[0]

    def body(in_ref, out_ref, recv, send_sems, recv_sems):
        x, y, c = _place()
        me = 4 * x + 2 * y + c
        recv[me] = in_ref[...]
        sent = []
        for m in range(1, N_DEV):
            px, py, pc = _peer(x, y, c, m)
            cp = pltpu.make_async_remote_copy(
                src_ref=in_ref, dst_ref=recv.at[me], send_sem=send_sems.at[m - 1], recv_sem=recv_sems.at[m - 1],
                device_id=(px, py, pc), device_id_type=MESH_ID)
            cp.start()
            sent.append(cp)
        for m in range(1, N_DEV):
            px, py, pc = _peer(x, y, c, m)
            peer = 4 * px + 2 * py + pc
            pltpu.make_async_remote_copy(
                src_ref=in_ref, dst_ref=recv.at[peer], send_sem=send_sems.at[m - 1], recv_sem=recv_sems.at[m - 1],
                device_id=(px, py, pc), device_id_type=MESH_ID).wait_recv()
        acc = recv[0]
        for d in range(1, N_DEV):
            acc = acc + recv[d]
        out_ref[...] = acc
        for cp in sent:
            cp.wait_send()

    return _call(
        body, name=name, out_shape=jax.ShapeDtypeStruct(pack.shape, F32), in_specs=[VMEM], out_specs=VMEM,
        scratch_shapes=[pltpu.VMEM((N_DEV, rows, pack.shape[1]), F32), pltpu.SemaphoreType.DMA((7,)),
                        pltpu.SemaphoreType.DMA((7,))],
        compiler_params=pltpu.CompilerParams(vmem_limit_bytes=VMEM_LIMIT),
    )(pack)


POOL_TS = 256


def _pool_counts(first_row, rows, win):
    t = first_row + lax.broadcasted_iota(jnp.int32, (rows, 1), 0)
    return jnp.minimum(t + 1, win).astype(F32)


def _pool_fwd(x, g, w, b, scale):
    nt = S // POOL_TS

    def body(x_ref, g_ref, w_ref, b_ref, s_ref, out_ref, diff_ref, ext):
        i = pl.program_id(0)

        @pl.when(i == 0)
        def _():
            ext[0:MAX_WIN, :] = jnp.zeros((MAX_WIN, D), F32)

        @pl.when(i > 0)
        def _():
            ext[0:MAX_WIN, :] = ext[POOL_TS:POOL_TS + MAX_WIN, :]

        xv = x_ref[...]
        h = xv * _rstd(xv) * g_ref[...]
        ext[MAX_WIN:, :] = h
        for gi in range(N_GROUPS):
            win = 2 << gi
            cols = slice(gi * GROUP, (gi + 1) * GROUP)
            sm = ext[:, cols]
            k = 1
            while k < win:
                sm = sm + pltpu.roll(sm, k, axis=0)
                k *= 2
            pooled = sm[MAX_WIN:, :] / _pool_counts(i * POOL_TS, POOL_TS, win)
            diff = (pooled - h[:, cols]).astype(BF16)
            yv = (_nn(diff, w_ref[gi]) + b_ref[:, cols]) * s_ref[:, cols]
            out_ref[:, cols] = xv[:, cols] + yv
            diff_ref[:, cols] = diff

    row = pl.BlockSpec((1, D), lambda i: (0, 0))
    tile = pl.BlockSpec((POOL_TS, D), lambda i: (i, 0))
    return _call(
        body, name="pool_fwd", grid=(nt,),
        in_specs=[tile, row, pl.BlockSpec((N_GROUPS, GROUP, GROUP), lambda i: (0, 0, 0)), row, row],
        out_specs=[tile, tile],
        out_shape=[jax.ShapeDtypeStruct((S, D), F32), jax.ShapeDtypeStruct((S, D), BF16)],
        scratch_shapes=[pltpu.VMEM((POOL_TS + MAX_WIN, D), F32)],
        compiler_params=_params("arbitrary"),
    )(x, g, w, b, scale)


def _pool_bwd(dy, x, diff, g, w, b, scale, ex=None):
    nt = S // POOL_TS

    def body(dy_ref, x_ref, diff_ref, g_ref, w_ref, b_ref, s_ref, gx_ref, dw_ref, db_ref, ds_ref, dg_ref, ext, dh):
        i = pl.program_id(0)
        first_row = (nt - 1 - i) * POOL_TS

        @pl.when(i == 0)
        def _():
            ext[POOL_TS:, :] = jnp.zeros((MAX_WIN, D), F32)
            dw_ref[...] = jnp.zeros(dw_ref.shape, F32)
            db_ref[...] = jnp.zeros(db_ref.shape, F32)
            ds_ref[...] = jnp.zeros(ds_ref.shape, F32)
            dg_ref[...] = jnp.zeros(dg_ref.shape, F32)

        @pl.when(i > 0)
        def _():
            ext[POOL_TS:, :] = ext[0:MAX_WIN, :]

        dyv = dy_ref[...]
        for gi in range(N_GROUPS):
            win = 2 << gi
            cols = slice(gi * GROUP, (gi + 1) * GROUP)
            dfb = diff_ref[:, cols]
            z = _nn(dfb, w_ref[gi]) + b_ref[:, cols]
            dyg = dyv[:, cols]
            ds_ref[:, cols] += _colsum8(dyg * z)
            dz = dyg * s_ref[:, cols]
            db_ref[:, cols] += _colsum8(dz)
            dzb = dz.astype(BF16)
            dw_ref[gi] += _tn(dfb, dzb)
            ddiff = _nt(dzb, w_ref[gi])
            ext[0:POOL_TS, cols] = ddiff / _pool_counts(first_row, POOL_TS, win)
            sm = ext[:, cols]
            k = 1
            while k < win:
                sm = sm + pltpu.roll(sm, POOL_TS + MAX_WIN - k, axis=0)
                k *= 2
            dh[:, cols] = sm[0:POOL_TS, :] - ddiff
        xv = x_ref[...]
        r = _rstd(xv)
        gv = g_ref[...]
        dhv = dh[...]
        dg_ref[...] += _colsum8(dhv * xv * r)
        gx_ref[...] = dyv + _rms_bwd(dhv, xv, r, gv)

    row = pl.BlockSpec((1, D), lambda i: (0, 0))
    tile = pl.BlockSpec((POOL_TS, D), lambda i: (nt - 1 - i, 0))
    acc = pl.BlockSpec((8, D), lambda i: (0, 0))
    wspec = pl.BlockSpec((N_GROUPS, GROUP, GROUP), lambda i: (0, 0, 0))
    return _call_hosting(
        body, ex, _grid_phases((nt,)), name="pool_bwd", grid=(nt,),
        in_specs=[tile, tile, tile, row, wspec, row, row],
        out_specs=[tile, wspec, acc, acc, acc],
        out_shape=[jax.ShapeDtypeStruct((S, D), F32), jax.ShapeDtypeStruct((N_GROUPS, GROUP, GROUP), F32),
                   jax.ShapeDtypeStruct((8, D), F32), jax.ShapeDtypeStruct((8, D), F32),
                   jax.ShapeDtypeStruct((8, D), F32)],
        scratch_shapes=[pltpu.VMEM((POOL_TS + MAX_WIN, D), F32), pltpu.VMEM((POOL_TS, D), F32)],
        args=(dy, x, diff, g, w, b, scale), compiler_params=_params("arbitrary"),
    )


FFN_TS = min(S, 1024)
FFN_TF = 256


def _ffn_fwd(x, g, wg_t, wu_t, wd, target=None, ex=None):
    ni, nj = S // FFN_TS, F // FFN_TF
    with_loss = target is not None

    def body(*refs):
        if with_loss:
            x_ref, g_ref, wg_ref, wu_ref, wd_ref, t_ref, out_ref, loss_ref, h_ref, gg_ref, uu_ref, hs, acc = refs
        else:
            x_ref, g_ref, wg_ref, wu_ref, wd_ref, out_ref, h_ref, gg_ref, uu_ref, hs, acc = refs
        i, j = pl.program_id(0), pl.program_id(1)

        @pl.when(j == 0)
        def _():
            xv = x_ref[...]
            hb = (xv * _rstd(xv) * g_ref[...]).astype(BF16)
            hs[...] = hb
            h_ref[...] = hb
            acc[...] = jnp.zeros(acc.shape, F32)

        hb = hs[...]
        gg = _nt(hb, wg_ref[...])
        uu = _nt(hb, wu_ref[...])
        gg_ref[...] = gg
        uu_ref[...] = uu
        a = (gg * jax.nn.sigmoid(gg) * uu).astype(BF16)
        acc[...] += _nn(a, wd_ref[...])

        @pl.when(j == nj - 1)
        def _():
            yv = x_ref[...] + acc[...]
            if with_loss:
                err = yv - t_ref[...]
                out_ref[...] = err * (1.0 / D)
                part = jnp.sum(err * err) * (0.5 / D)

                @pl.when(i == 0)
                def _():
                    loss_ref[...] = jnp.zeros(loss_ref.shape, F32)

                loss_ref[...] += jnp.broadcast_to(part, loss_ref.shape)
            else:
                out_ref[...] = yv

    xt = pl.BlockSpec((FFN_TS, D), lambda i, j: (i, 0))
    row = pl.BlockSpec((1, D), lambda i, j: (0, 0))
    wt = pl.BlockSpec((FFN_TF, D), lambda i, j: (j, 0))
    gt = pl.BlockSpec((FFN_TS, FFN_TF), lambda i, j: (i, j))
    in_specs = [xt, row, wt, wt, wt] + ([xt] if with_loss else [])
    out_specs = [xt] + ([pl.BlockSpec((8, 128), lambda i, j: (0, 0))] if with_loss else []) + [xt, gt, gt]
    out_shape = ([jax.ShapeDtypeStruct((S, D), F32)] + ([jax.ShapeDtypeStruct((8, 128), F32)] if with_loss else [])
                 + [jax.ShapeDtypeStruct((S, D), BF16), jax.ShapeDtypeStruct((S, F), F32),
                    jax.ShapeDtypeStruct((S, F), F32)])
    args = (x, g, wg_t, wu_t, wd) + ((target,) if with_loss else ())
    return _call_hosting(
        body, ex, _grid_phases((ni, nj), 0.8), name="ffn_fwd_loss" if with_loss else "ffn_fwd", grid=(ni, nj),
        in_specs=in_specs, out_specs=out_specs, out_shape=out_shape,
        scratch_shapes=[pltpu.VMEM((FFN_TS, D), BF16), pltpu.VMEM((FFN_TS, D), F32)], args=args,
        compiler_params=_params("arbitrary", "arbitrary"),
    )


def _ffn_bwd_weights(dout, h, gg, uu, wd, name, ex=None):
    nj = F // FFN_TF

    def body(do_ref, h_ref, gg_ref, uu_ref, wd_ref, dg_ref, du_ref, dwg_ref, dwu_ref, dwd_ref, dob):
        @pl.when(pl.program_id(0) == 0)
        def _():
            dob[...] = do_ref[...].astype(BF16)

        hb, dov = h_ref[...], dob[...]
        gv, uv = gg_ref[...], uu_ref[...]
        da = _nt(dov, wd_ref[...])
        sg = jax.nn.sigmoid(gv)
        sl = gv * sg
        ab = (sl * uv).astype(BF16)
        dub = (da * sl).astype(BF16)
        dgb = (da * uv * (sg * (1.0 + gv * (1.0 - sg)))).astype(BF16)
        dg_ref[...] = dgb
        du_ref[...] = dub
        dwg_ref[...] = _tn(dgb, hb).astype(BF16)
        dwu_ref[...] = _tn(dub, hb).astype(BF16)
        dwd_ref[...] = _tn(ab, dov).astype(BF16)

    once = pl.Buffered(1)
    whole = lambda: pl.BlockSpec((S, D), lambda j, _: (0, 0), pipeline_mode=once)
    wt = pl.BlockSpec((FFN_TF, D), lambda j, _: (j, 0))
    gt = pl.BlockSpec((S, FFN_TF), lambda j, _: (0, j))
    return _call_hosting(
        body, ex, _grid_phases((nj, 1)), name=name, grid=(nj, 1),
        in_specs=[whole(), whole(), gt, gt, wt], out_specs=[gt, gt, wt, wt, wt],
        out_shape=[jax.ShapeDtypeStruct((S, F), BF16)] * 2 + [jax.ShapeDtypeStruct((F, D), BF16)] * 3,
        scratch_shapes=[pltpu.VMEM((S, D), BF16)], args=(dout, h, gg, uu, wd),
        compiler_params=_params("arbitrary", "arbitrary"),
    )


BWD_TS = 512


def _ffn_bwd_input(dres, x, g, dg, du, wg_t, wu_t, name, ex=None):
    nt = S // BWD_TS

    def body(dres_ref, x_ref, g_ref, dg_ref, du_ref, wg_ref, wu_ref, dx_ref, dgam_ref):
        dh = _nn(dg_ref[...], wg_ref[...]) + _nn(du_ref[...], wu_ref[...])
        xv = x_ref[...]
        r = _rstd(xv)
        dx_ref[...] = dres_ref[...] + _rms_bwd(dh, xv, r, g_ref[...])

        @pl.when(pl.program_id(0) == 0)
        def _():
            dgam_ref[...] = jnp.zeros(dgam_ref.shape, F32)

        dgam_ref[...] += _colsum8(dh * xv * r)

    tile = pl.BlockSpec((BWD_TS, D), lambda i: (i, 0))
    ftile = pl.BlockSpec((BWD_TS, F), lambda i: (i, 0))
    wspec = lambda: pl.BlockSpec((F, D), lambda i: (0, 0), pipeline_mode=pl.Buffered(1))
    return _call_hosting(
        body, ex, _grid_phases((nt,)), name=name, grid=(nt,),
        in_specs=[tile, tile, pl.BlockSpec((1, D), lambda i: (0, 0)), ftile, ftile, wspec(), wspec()],
        out_specs=[tile, pl.BlockSpec((8, D), lambda i: (0, 0))],
        out_shape=[jax.ShapeDtypeStruct((S, D), F32), jax.ShapeDtypeStruct((8, D), F32)],
        scratch_shapes=[], args=(dres, x, g, dg, du, wg_t, wu_t), compiler_params=_params("arbitrary"),
    )


NORM_TS = 256


def _norm_bwd(dres, x, dhs, gs, name, ex=None):
    n = len(dhs)
    nt = S // NORM_TS

    def body(*refs):
        dres_ref, x_ref = refs[:2]
        dh_refs, g_refs = refs[2:2 + n], refs[2 + n:2 + 2 * n]
        dx_ref, dg_refs = refs[2 + 2 * n], refs[3 + 2 * n:]
        i = pl.program_id(0)
        xv = x_ref[...]
        r = _rstd(xv)
        dx = dres_ref[...]
        for k in range(n):
            dhv = dh_refs[k][...]
            dx = dx + _rms_bwd(dhv, xv, r, g_refs[k][...])

            @pl.when(i == 0)
            def _():
                dg_refs[k][...] = jnp.zeros((8, D), F32)

            dg_refs[k][...] += _colsum8(dhv * xv * r)
        dx_ref[...] = dx

    tile = pl.BlockSpec((NORM_TS, D), lambda i: (i, 0))
    row = pl.BlockSpec((1, D), lambda i: (0, 0))
    acc = pl.BlockSpec((8, D), lambda i: (0, 0))
    return _call_hosting(
        body, ex, _grid_phases((nt,)), name=name, grid=(nt,),
        in_specs=[tile, tile] + [tile] * n + [row] * n,
        out_specs=[tile] + [acc] * n,
        out_shape=[jax.ShapeDtypeStruct((S, D), F32)] + [jax.ShapeDtypeStruct((8, D), F32)] * n,
        scratch_shapes=[], args=(dres, x, *dhs, *gs), compiler_params=_params("arbitrary"),
    )


def _mm(a, b, mode, out_dtype, name, add=None):
    if mode == "nn":
        (m, kd), n = a.shape, b.shape[1]
    elif mode == "nt":
        (m, kd), n = a.shape, b.shape[0]
    else:
        (kd, m), n = a.shape, b.shape[1]
    tm, tn, tk = min(m, 1024), min(n, 1024), min(kd, 1024)
    nk = kd // tk
    dot = {"nn": _nn, "nt": _nt, "tn": _tn}[mode]

    def body(*refs):
        if add is None:
            a_ref, b_ref, o_ref, acc = refs
        else:
            a_ref, b_ref, add_ref, o_ref, acc = refs
        k = pl.program_id(2)

        @pl.when(k == 0)
        def _():
            acc[...] = jnp.zeros(acc.shape, F32)

        acc[...] += dot(a_ref[...].astype(BF16), b_ref[...].astype(BF16))

        @pl.when(k == nk - 1)
        def _():
            res = acc[...]
            if add is not None:
                res = res + add_ref[...]
            o_ref[...] = res.astype(out_dtype)

    if mode == "tn":
        a_spec = pl.BlockSpec((tk, tm), lambda i, j, k: (k, i))
        b_spec = pl.BlockSpec((tk, tn), lambda i, j, k: (k, j))
    else:
        a_spec = pl.BlockSpec((tm, tk), lambda i, j, k: (i, k))
        b_spec = (pl.BlockSpec((tk, tn), lambda i, j, k: (k, j)) if mode == "nn"
                  else pl.BlockSpec((tn, tk), lambda i, j, k: (j, k)))
    o_spec = pl.BlockSpec((tm, tn), lambda i, j, k: (i, j))
    in_specs = [a_spec, b_spec] + ([o_spec] if add is not None else [])
    args = (a, b) + ((add,) if add is not None else ())
    return _call(
        body, name=name, grid=(m // tm, n // tn, nk), in_specs=in_specs, out_specs=o_spec,
        out_shape=jax.ShapeDtypeStruct((m, n), out_dtype), scratch_shapes=[pltpu.VMEM((tm, tn), F32)],
        compiler_params=_params("parallel", "parallel", "arbitrary"),
    )(*args)


PROJ_TS = 256


def _kvq_proj(x, g_kv, g_mix, wk, wv, wq, gk, gq):
    def body(x_ref, gkv_ref, gmix_ref, wk_ref, wv_ref, wq_ref, gk_ref, gq_ref,
             hkv_ref, h1_ref, kpre_ref, qpre_ref, k_ref, v_ref, q_ref):
        xv = x_ref[...]
        xr = xv * _rstd(xv)
        hkv = (xr * gkv_ref[...]).astype(BF16)
        h1 = (xr * gmix_ref[...]).astype(BF16)
        hkv_ref[...] = hkv
        h1_ref[...] = h1
        kpre = _nn(hkv, wk_ref[...])
        qpre = _nn(h1, wq_ref[...])
        kpre_ref[...] = kpre
        qpre_ref[...] = qpre
        v_ref[...] = _nn(hkv, wv_ref[...]).astype(BF16)
        rk = lax.rsqrt(_seg_sum(kpre * kpre) * (1.0 / HEAD) + EPS)
        k_ref[...] = (kpre * rk * gk_ref[...]).astype(BF16)
        rq = lax.rsqrt(_seg_sum(qpre * qpre) * (1.0 / HEAD) + EPS)
        q_ref[...] = (qpre * rq * gq_ref[...]).astype(BF16)

    tile = pl.BlockSpec((PROJ_TS, D), lambda i: (i, 0))
    row = pl.BlockSpec((1, D), lambda i: (0, 0))
    wspec = pl.BlockSpec((D, D), lambda i: (0, 0))
    bf = jax.ShapeDtypeStruct((S, D), BF16)
    ff = jax.ShapeDtypeStruct((S, D), F32)
    return _call(
        body, name="kvq_proj", grid=(S // PROJ_TS,),
        in_specs=[tile, row, row, wspec, wspec, wspec, row, row],
        out_specs=[tile] * 7, out_shape=[bf, bf, ff, ff, bf, bf, bf],
        compiler_params=_params("arbitrary"),
    )(x, g_kv, g_mix, wk, wv, wq, gk, gq)


def _head_norm_bwd(dout, pre, hg, name, row_off=0):
    nt = S // PROJ_TS

    def body(do_ref, pre_ref, hg_ref, dpre_ref, dhg_ref, acc):
        i = pl.program_id(0)
        dov, pv, hgv = do_ref[...], pre_ref[...], hg_ref[...]
        r = lax.rsqrt(_seg_sum(pv * pv) * (1.0 / HEAD) + EPS)
        gd = dov * hgv
        dpre = r * gd - pv * (r * r * r) * (_seg_sum(gd * pv) * (1.0 / HEAD))
        dpre_ref[...] = dpre.astype(BF16)

        @pl.when(i == 0)
        def _():
            acc[...] = jnp.zeros(acc.shape, F32)

        acc[...] += _colsum8(dov * pv * r)

        @pl.when(i == nt - 1)
        def _():
            full = acc[...]
            fold = full[:, 0:128]
            for blk in range(1, D // 128):
                fold = fold + full[:, blk * 128:(blk + 1) * 128]
            dhg_ref[...] = fold + pltpu.roll(fold, HEAD, axis=1)

    tile = pl.BlockSpec((PROJ_TS, D), lambda i: (i, 0))
    return _call(
        body, name=name, grid=(nt,),
        in_specs=[pl.BlockSpec((PROJ_TS, D), lambda i: (i + row_off, 0)), tile, pl.BlockSpec((1, D), lambda i: (0, 0))],
        out_specs=[tile, pl.BlockSpec((8, 128), lambda i: (0, 0))],
        out_shape=[jax.ShapeDtypeStruct((S, D), BF16), jax.ShapeDtypeStruct((8, 128), F32)],
        scratch_shapes=[pltpu.VMEM((8, D), F32)],
        compiler_params=_params("arbitrary"),
    )(dout, pre, hg)


def _toeplitz_from_table(table):
    far = jnp.broadcast_to(table[:, N_REL - 1:], (N_HEADS, PADK - MAX_REL + 1))
    near = table[:, N_REL - 2::-1]
    past = jnp.broadcast_to(table[:, 0:1], (N_HEADS, MAX_REL))
    wrap = jnp.broadcast_to(table[:, N_REL - 1:], (N_HEADS, QB - 1))
    return jnp.concatenate([far, near, past, wrap], axis=1).reshape(N_HEADS, 1, TOEP)


def _table_grad_from_toeplitz(dtp, seg):
    lo = PADK - MAX_REL + 1
    near = dtp[:, lo + N_REL - 3:lo - 1:-1]
    return jnp.concatenate([seg[:, 1:2], near, seg[:, 0:1]], axis=1)


def _bias_band(tp):
    def body(tp_ref, out_ref):
        bv = jnp.broadcast_to(tp_ref[0], (QB, TOEP))
        row = lax.broadcasted_iota(jnp.int32, (QB, TOEP), 0)
        k = 1
        while k < QB:
            bv = jnp.where((row & k) != 0, pltpu.roll(bv, k, axis=1), bv)
            k *= 2
        out_ref[0] = bv[:, 0:KB]

    return _call(
        body, name="bias_band", grid=(N_HEADS,),
        in_specs=[pl.BlockSpec((1, 1, TOEP), lambda h: (h, 0, 0))],
        out_specs=pl.BlockSpec((1, QB, KB), lambda h: (h, 0, 0)),
        out_shape=jax.ShapeDtypeStruct((N_HEADS, QB, KB), F32),
        compiler_params=_params("parallel"),
    )(tp)


def _bias_grad(dband):
    lo, hi = PADK - MAX_REL + 1, PADK + MAX_REL

    def body(db_ref, dtp_ref, seg_ref):
        bv = jnp.concatenate([db_ref[0], jnp.zeros((QB, TOEP - KB), F32)], axis=1)
        row = lax.broadcasted_iota(jnp.int32, (QB, TOEP), 0)
        k = 1
        while k < QB:
            bv = jnp.where((row & k) != 0, pltpu.roll(bv, TOEP - k, axis=1), bv)
            k *= 2
        col = jnp.sum(bv, axis=0, keepdims=True)
        dtp_ref[0] = col
        u = lax.broadcasted_iota(jnp.int32, (1, TOEP), 1)
        far = jnp.sum(jnp.where((u < lo) | (u > hi + MAX_REL), col, 0.0))
        past = jnp.sum(jnp.where((u >= hi) & (u <= hi + MAX_REL), col, 0.0))
        lane = lax.broadcasted_iota(jnp.int32, (1, 128), 1)
        seg_ref[0] = jnp.where(lane == 0, far, jnp.where(lane == 1, past, 0.0))

    return _call(
        body, name="bias_grad", grid=(N_HEADS,),
        in_specs=[pl.BlockSpec((1, QB, KB), lambda h: (h, 0, 0))],
        out_specs=[pl.BlockSpec((1, 1, TOEP), lambda h: (h, 0, 0)), pl.BlockSpec((1, 1, 128), lambda h: (h, 0, 0))],
        out_shape=[jax.ShapeDtypeStruct((N_HEADS, 1, TOEP), F32), jax.ShapeDtypeStruct((N_HEADS, 1, 128), F32)],
        compiler_params=_params("parallel"),
    )(dband)


N_QB = S // QB
N_HP = D // 128


def _band_mask(cb):
    qc = lax.broadcasted_iota(jnp.int32, (QB, KB), 0) >> 6
    p = lax.broadcasted_iota(jnp.int32, (QB, KB), 1)
    kc = p >> 6
    return (kc >= qc) & (kc <= qc + LEFT) & (p + cb * QB >= PADK)


def _half_mask(hh):
    lane = lax.broadcasted_iota(jnp.int32, (1, 128), 1)
    return jnp.where((lane < HEAD) == (hh == 0), 1.0, 0.0).astype(BF16)


def _probs(qh, kb, bias, mask):
    sc = _nt(qh, kb) * ATTN_SCALE + bias
    sc = jnp.where(mask, sc, NEG_INF)
    e = jnp.exp(sc - jnp.max(sc, axis=-1, keepdims=True))
    return e / jnp.sum(e, axis=-1, keepdims=True)


def _attn_fwd(q, kp, vp, bias, ex=None):
    def body(q_ref, k_ref, v_ref, b_ref, o_ref):
        cb = pl.program_id(1)
        band = pl.ds(pl.multiple_of(cb * QB, QB), KB)
        kb, vb = k_ref[band, :], v_ref[band, :]
        qv = q_ref[...]
        mask = _band_mask(cb)
        low = lax.broadcasted_iota(jnp.int32, (QB, 128), 1) < HEAD
        outs = []
        for hh in range(2):
            pb = _probs(qv * _half_mask(hh), kb, b_ref[hh], mask).astype(BF16)
            outs.append(_nn(pb, vb))
        o_ref[...] = jnp.where(low, outs[0], outs[1]).astype(BF16)

    qspec = pl.BlockSpec((QB, 128), lambda hp, cb: (cb, hp))
    kspec = pl.BlockSpec((PADK + S, 128), lambda hp, cb: (0, hp))
    return _call_hosting(
        body, ex, _grid_phases((N_HP, N_QB), 0.85), name="attn_fwd", grid=(N_HP, N_QB),
        in_specs=[qspec, kspec, kspec, pl.BlockSpec((2, QB, KB), lambda hp, cb: (hp, 0, 0))],
        out_specs=[qspec], out_shape=[jax.ShapeDtypeStruct((S, D), BF16)], scratch_shapes=[],
        args=(q, kp, vp, bias), compiler_params=_params("arbitrary", "arbitrary"),
    )


def _attn_bwd(q, kp, vp, bias, do, ex=None):
    def body(q_ref, k_ref, v_ref, b_ref, do_ref, dq_ref, dk_ref, dv_ref, db_ref):
        cb = pl.program_id(1)

        @pl.when(cb == 0)
        def _():
            dk_ref[...] = jnp.zeros(dk_ref.shape, F32)
            dv_ref[...] = jnp.zeros(dv_ref.shape, F32)
            db_ref[...] = jnp.zeros(db_ref.shape, F32)

        band = pl.ds(pl.multiple_of(cb * QB, QB), KB)
        kb, vb = k_ref[band, :], v_ref[band, :]
        qv, dov = q_ref[...], do_ref[...]
        mask = _band_mask(cb)
        low = lax.broadcasted_iota(jnp.int32, (QB, 128), 1) < HEAD
        dq = jnp.zeros((QB, 128), F32)
        dkb = jnp.zeros((KB, 128), F32)
        dvb = jnp.zeros((KB, 128), F32)
        for hh in range(2):
            sel = low if hh == 0 else jnp.logical_not(low)
            qh = qv * _half_mask(hh)
            doh = dov * _half_mask(hh)
            p = _probs(qh, kb, b_ref[hh], mask)
            dp = _nt(doh, vb)
            dvb = dvb + _tn(p.astype(BF16), doh)
            ds = p * (dp - jnp.sum(dp * p, axis=-1, keepdims=True))
            db_ref[hh] += ds
            dsb = (ds * ATTN_SCALE).astype(BF16)
            dq = dq + jnp.where(sel, _nn(dsb, kb), 0.0)
            dkb = dkb + _tn(dsb, qh)
        dq_ref[...] = dq
        dk_ref[band, :] += dkb
        dv_ref[band, :] += dvb

    qspec = pl.BlockSpec((QB, 128), lambda hp, cb: (cb, hp))
    kspec = pl.BlockSpec((PADK + S, 128), lambda hp, cb: (0, hp))
    bspec = pl.BlockSpec((2, QB, KB), lambda hp, cb: (hp, 0, 0))
    kf = jax.ShapeDtypeStruct((PADK + S, D), F32)
    return _call_hosting(
        body, ex, _grid_phases((N_HP, N_QB)), name="attn_bwd", grid=(N_HP, N_QB),
        in_specs=[qspec, kspec, kspec, bspec, qspec],
        out_specs=[qspec, kspec, kspec, bspec],
        out_shape=[jax.ShapeDtypeStruct((S, D), F32), kf, kf, jax.ShapeDtypeStruct((N_HEADS, QB, KB), F32)],
        scratch_shapes=[], args=(q, kp, vp, bias, do), compiler_params=_params("arbitrary", "arbitrary"),
    )


def _adam_math(w, g, m, v):
    m = ADAM_B1 * m + (1.0 - ADAM_B1) * g
    v = ADAM_B2 * v + (1.0 - ADAM_B2) * (g * g)
    m_hat = m / (1.0 - ADAM_B1 ** ADAM_STEP)
    v_hat = v / (1.0 - ADAM_B2 ** ADAM_STEP)
    delta = -ADAM_LR * (m_hat / (jnp.sqrt(v_hat) + ADAM_EPS) + ADAM_WD * w)
    return delta, m, v


def _sum_parts(parts, name):
    count, r, c = parts.shape
    tr = r // 2 if r % 16 == 0 and r > 64 else r

    def body(p_ref, o_ref):
        acc = p_ref[0].astype(F32)
        for d in range(1, count):
            acc = acc + p_ref[d].astype(F32)
        o_ref[...] = acc

    return _call(
        body, name=name, grid=(r // tr,),
        in_specs=[pl.BlockSpec((count, tr, c), lambda i: (0, i, 0))],
        out_specs=pl.BlockSpec((tr, c), lambda i: (i, 0)),
        out_shape=jax.ShapeDtypeStruct((r, c), F32), compiler_params=_params("parallel"),
    )(parts)


def _adam(w, g, m, v, name):
    r, c = w.shape
    tr = r
    for cand in (256, 176, 128, 64, 32, 16, 8):
        if r % cand == 0:
            tr = cand
            break

    def body(w_ref, g_ref, m_ref, v_ref, d_ref, nm_ref, nv_ref):
        delta, nm, nv = _adam_math(w_ref[...], g_ref[...], m_ref[...], v_ref[...])
        d_ref[...] = delta
        nm_ref[...] = nm
        nv_ref[...] = nv

    spec = pl.BlockSpec((tr, c), lambda i: (i, 0))
    sd = jax.ShapeDtypeStruct((r, c), F32)
    return _call(
        body, name=name, grid=(r // tr,), in_specs=[spec] * 4, out_specs=[spec] * 3, out_shape=[sd] * 3,
        compiler_params=_params("parallel"),
    )(w, g, m, v)


def _pad8(rows):
    return jnp.pad(rows, ((0, 8 - rows.shape[0]), (0, 0)))


def kernel(x, norm_mix_g, norm_ffn_g, pool_w, pool_b, pool_scale, kv_norm_g, w_k, w_v, k_norm_g, w_q, q_norm_g, rel_bias, w_o, w_gate, w_up, w_down, loss_target, m_norm_mix_g, m_norm_ffn_g, m_pool_w, m_pool_b, m_pool_scale, m_kv_norm_g, m_w_k, m_w_v, m_k_norm_g, m_w_q, m_q_norm_g, m_rel_bias, m_w_o, m_w_gate, m_w_up, m_w_down, v_norm_mix_g, v_norm_ffn_g, v_pool_w, v_pool_b, v_pool_scale, v_kv_norm_g, v_w_k, v_w_v, v_k_norm_g, v_w_q, v_q_norm_g, v_rel_bias, v_w_o, v_w_gate, v_w_up, v_w_down):
    assert x.shape == (1, S, D) and w_gate.shape == (2, D, F_SHARD) and w_k.shape == (D_SHARD, D)
    xin, target = x[0], loss_target[0]

    ffn_shards = [[w_gate[layer].T.astype(BF16), w_up[layer].T.astype(BF16), w_down[layer].astype(BF16)]
                  for layer in range(2)]
    att_shards = [w_k.astype(BF16), w_v.astype(BF16), w_q[0].astype(BF16), w_o[0].astype(BF16)]
    pool_shard = pool_w[0].astype(BF16).reshape(N_GROUPS * POOL_SHARD, GROUP)
    small = jnp.concatenate([pool_b[0].reshape(1, N_GROUPS * POOL_SHARD), pool_scale], axis=1)

    full0 = _run_exchange(_gather_exchange(ffn_shards[0] + [pool_shard, _pad8(small)]), "gather_layer0")
    ffn_w0 = [a.reshape(F, D) for a in full0[:3]]
    pw_f = full0[3].reshape(N_DEV, N_GROUPS, POOL_SHARD, GROUP).transpose(1, 0, 2, 3).reshape(N_GROUPS, GROUP, GROUP)
    small_f = full0[4][:, 0, :]
    pb_f = small_f[:, :N_GROUPS * POOL_SHARD].reshape(N_DEV, N_GROUPS, POOL_SHARD).transpose(1, 0, 2).reshape(1, D)
    ps_f = small_f[:, N_GROUPS * POOL_SHARD:].reshape(1, D)

    g_mix0, g_mix1 = norm_mix_g[0:1], norm_mix_g[1:2]
    g_ffn0, g_ffn1 = norm_ffn_g[0:1], norm_ffn_g[1:2]
    g_kv = kv_norm_g.reshape(1, D)
    gk_t = jnp.tile(k_norm_g.reshape(1, HEAD), (1, N_HEADS))
    gq_t = jnp.tile(q_norm_g.reshape(1, HEAD), (1, N_HEADS))

    x1, diff = _pool_fwd(xin, g_mix0, pw_f, pb_f, ps_f)
    (x2, hf0, gg0, uu0), full_att = _ffn_fwd(x1, g_ffn0, *ffn_w0, ex=_gather_exchange(att_shards))
    wk_f, wv_f, wq_f, wo_f = [a.reshape(D, D) for a in full_att]
    hkv, h1, kpre, qpre, kk, vv, qq = _kvq_proj(x2, g_kv, g_mix1, wk_f, wv_f, wq_f, gk_t, gq_t)
    kp = jnp.pad(kk, ((PADK, 0), (0, 0)))
    vp = jnp.pad(vv, ((PADK, 0), (0, 0)))
    bias = _bias_band(_toeplitz_from_table(rel_bias[0]))
    (att,), full1 = _attn_fwd(qq, kp, vp, bias, ex=_gather_exchange(ffn_shards[1]))
    ffn_w1 = [a.reshape(F, D) for a in full1]
    x3 = _mm(att, wo_f, "nn", F32, "attn_out", add=x2)
    (dx4, loss_rows, hf1, gg1, uu1), _ = _ffn_fwd(x3, g_ffn1, *ffn_w1, target=target)

    def blocks(dw):
        return dw.reshape(N_DEV, dw.shape[0] // N_DEV, dw.shape[1])

    (dgg1, duu1, dwg1, dwu1, dwd1), _ = _ffn_bwd_weights(dx4, hf1, gg1, uu1, ffn_w1[2], name="ffn_bwd1")
    parts1 = [blocks(dw) for dw in (dwg1, dwu1, dwd1)]
    (dx3, dg_ffn1), stage1 = _ffn_bwd_input(dx4, x3, g_ffn1, dgg1, duu1, ffn_w1[0], ffn_w1[1], "ffn_dx1",
                                            ex=_pair_exchange(parts1))
    chip1 = _pair_add(parts1, stage1, "pair_add_ffn1")
    datt = _mm(dx3, wo_f, "nt", BF16, "d_attn")
    dwo = _mm(att, dx3, "tn", BF16, "d_wo")
    (dq, dkp, dvp, dband), recv1 = _attn_bwd(qq, kp, vp, bias, datt, ex=_chip_exchange(chip1))
    dtp, seg = _bias_grad(dband)
    dqpre, dgq = _head_norm_bwd(dq, qpre, gq_t, "q_norm_bwd")
    dkpre, dgk = _head_norm_bwd(dkp, kpre, gk_t, "k_norm_bwd", row_off=PADK // PROJ_TS)
    dh1 = _mm(dqpre, wq_f, "nt", F32, "d_h1")
    dwq = _mm(h1, dqpre, "tn", BF16, "d_wq")
    dhkv = _mm(dkpre, wk_f, "nt", F32, "d_hkv_k")
    dv = dvp[PADK:]
    dhkv = _mm(dv, wv_f, "nt", F32, "d_hkv_v", add=dhkv)
    dwk = _mm(hkv, dkpre, "tn", BF16, "d_wk")
    dwv = _mm(hkv, dv, "tn", BF16, "d_wv")
    parts_att = [blocks(dw) for dw in (dwk, dwv, dwq, dwo)]
    (dx2, dg_mix1, dg_kv), stage_att = _norm_bwd(dx3, x2, [dh1, dhkv], [g_mix1, g_kv], "norm_bwd_mix1",
                                                 ex=_pair_exchange(parts_att))
    chip_att = _pair_add(parts_att, stage_att, "pair_add_att")
    (dgg0, duu0, dwg0, dwu0, dwd0), recv_att = _ffn_bwd_weights(
        dx2, hf0, gg0, uu0, ffn_w0[2], name="ffn_bwd0", ex=_chip_exchange(chip_att))
    parts0 = [blocks(dw) for dw in (dwg0, dwu0, dwd0)]
    (dx1, dg_ffn0), stage0 = _ffn_bwd_input(dx2, x1, g_ffn0, dgg0, duu0, ffn_w0[0], ffn_w0[1], "ffn_dx0",
                                            ex=_pair_exchange(parts0))
    chip0 = _pair_add(parts0, stage0, "pair_add_ffn0")
    (grad_x, dpw, db_rows, ds_rows, dg_mix0), recv0 = _pool_bwd(dx1, xin, diff, g_mix0, pw_f, pb_f, ps_f,
                                                                ex=_chip_exchange(chip0))
    dpw_blocks = dpw.reshape(N_GROUPS, N_DEV, POOL_SHARD, GROUP).transpose(1, 0, 2, 3)
    dpw_blocks = dpw_blocks.reshape(N_DEV, N_GROUPS * POOL_SHARD, GROUP).astype(BF16)
    recv_pool = _run_exchange(_scatter_exchange([dpw_blocks]), "scatter_pool")
    recv = recv0 + recv1 + recv_att + recv_pool


    misc = jnp.concatenate([dgk[0:1, 0:HEAD], dgq[0:1, 0:HEAD], loss_rows[0:1, 0:1],
                            seg[:, 0, 0].reshape(1, N_HEADS), seg[:, 0, 1].reshape(1, N_HEADS)], axis=1)
    misc = jnp.pad(misc, ((0, 0), (0, D - misc.shape[1])))
    vec_rows = jnp.concatenate([dg_mix0[0:1], dg_mix1[0:1], dg_ffn0[0:1], dg_ffn1[0:1], dg_kv[0:1],
                                db_rows[0:1], ds_rows[0:1], misc], axis=0)
    pack = jnp.concatenate([vec_rows, dtp.reshape(N_HEADS, TOEP)], axis=0)
    tot = _all_reduce_small(pack, "reduce_small")

    loss = tot[7, 2 * HEAD]
    g_norm_mix = tot[0:2]
    g_norm_ffn = tot[2:4]
    g_kv_norm = tot[4]
    g_k_norm = tot[7, 0:HEAD]
    g_q_norm = tot[7, HEAD:2 * HEAD].reshape(1, HEAD)
    seg_tot = jnp.stack([tot[7, 2 * HEAD + 1:2 * HEAD + 1 + N_HEADS],
                         tot[7, 2 * HEAD + 1 + N_HEADS:2 * HEAD + 1 + 2 * N_HEADS]], axis=1)
    g_rel = _table_grad_from_toeplitz(tot[8:8 + N_HEADS], seg_tot).reshape(1, N_HEADS, N_REL)
    me = 4 * lax.axis_index("x") + 2 * lax.axis_index("y") + lax.axis_index("c")
    g_pool_b = lax.dynamic_slice_in_dim(tot[5].reshape(N_GROUPS, GROUP), me * POOL_SHARD, POOL_SHARD, axis=1)
    g_pool_b = g_pool_b.reshape(1, N_GROUPS, POOL_SHARD)
    g_pool_scale = lax.dynamic_slice_in_dim(tot[6:7], me * D_SHARD, D_SHARD, axis=1)

    sums = [_sum_parts(recv[k], "sum_parts_%d" % k) for k in range(len(recv))]
    g_gate = jnp.stack([sums[0].T, sums[3].T])
    g_up = jnp.stack([sums[1].T, sums[4].T])
    g_down = jnp.stack([sums[2], sums[5]])
    g_wk, g_wv = sums[6], sums[7]
    g_wq, g_wo = sums[8][None], sums[9][None]
    g_pool_w = sums[10].reshape(1, N_GROUPS, POOL_SHARD, GROUP)

    grads = dict(norm_mix_g=g_norm_mix, norm_ffn_g=g_norm_ffn, pool_w=g_pool_w, pool_b=g_pool_b,
                 pool_scale=g_pool_scale, kv_norm_g=g_kv_norm, w_k=g_wk, w_v=g_wv, k_norm_g=g_k_norm, w_q=g_wq,
                 q_norm_g=g_q_norm, rel_bias=g_rel, w_o=g_wo, w_gate=g_gate, w_up=g_up, w_down=g_down)
    weights = dict(norm_mix_g=norm_mix_g, norm_ffn_g=norm_ffn_g, pool_w=pool_w, pool_b=pool_b,
                   pool_scale=pool_scale, kv_norm_g=kv_norm_g, w_k=w_k, w_v=w_v, k_norm_g=k_norm_g, w_q=w_q,
                   q_norm_g=q_norm_g, rel_bias=rel_bias, w_o=w_o, w_gate=w_gate, w_up=w_up, w_down=w_down)
    mom1 = dict(norm_mix_g=m_norm_mix_g, norm_ffn_g=m_norm_ffn_g, pool_w=m_pool_w, pool_b=m_pool_b,
                pool_scale=m_pool_scale, kv_norm_g=m_kv_norm_g, w_k=m_w_k, w_v=m_w_v, k_norm_g=m_k_norm_g,
                w_q=m_w_q, q_norm_g=m_q_norm_g, rel_bias=m_rel_bias, w_o=m_w_o, w_gate=m_w_gate, w_up=m_w_up,
                w_down=m_w_down)
    mom2 = dict(norm_mix_g=v_norm_mix_g, norm_ffn_g=v_norm_ffn_g, pool_w=v_pool_w, pool_b=v_pool_b,
                pool_scale=v_pool_scale, kv_norm_g=v_kv_norm_g, w_k=v_w_k, w_v=v_w_v, k_norm_g=v_k_norm_g,
                w_q=v_w_q, q_norm_g=v_q_norm_g, rel_bias=v_rel_bias, w_o=v_w_o, w_gate=v_w_gate, w_up=v_w_up,
                w_down=v_w_down)
    names = list(grads)

    large = ("w_gate", "w_up", "w_down", "w_k", "w_v", "w_q", "w_o", "pool_w")
    deltas, new_m, new_v = {}, {}, {}
    for nm in large:
        shape = weights[nm].shape
        flat = lambda a: a.reshape(-1, shape[-1])
        dl, m1, m2 = _adam(flat(weights[nm]), flat(grads[nm]), flat(mom1[nm]), flat(mom2[nm]), "adam_" + nm)
        deltas[nm], new_m[nm], new_v[nm] = dl.reshape(shape), m1.reshape(shape), m2.reshape(shape)
    small_names = [nm for nm in names if nm not in large]

    def pack_small(tree):
        cols = []
        for nm in small_names:
            flat = tree[nm].reshape(-1)
            cols.append(jnp.pad(flat, (0, -flat.shape[0] % 1024)))
        return jnp.concatenate(cols).reshape(-1, 128)

    dl, m1, m2 = _adam(pack_small(weights), pack_small(grads), pack_small(mom1), pack_small(mom2), "adam_small")

    def unpack_small(packed, out):
        flat, off = packed.reshape(-1), 0
        for nm in small_names:
            size = weights[nm].size
            out[nm] = flat[off:off + size].reshape(weights[nm].shape)
            off += size + (-size % 1024)

    unpack_small(dl, deltas)
    unpack_small(m1, new_m)
    unpack_small(m2, new_v)

    return (loss, grad_x[None], *[grads[nm] for nm in names], *[deltas[nm] for nm in names],
            *[new_m[nm] for nm in names], *[new_v[nm] for nm in names])
```

```python
import functools

import jax
import jax.numpy as jnp
from jax import lax
from jax.experimental import pallas as pl
from jax.experimental.pallas import tpu as pltpu

F32 = jnp.float32
BF16 = jnp.bfloat16
MESH_ID = pl.DeviceIdType.MESH

N_DEV = 8
S = 2048
D = 1024
F = 2816
F_SHARD = F // N_DEV
D_SHARD = D // N_DEV
N_GROUPS = 4
GROUP = D // N_GROUPS
POOL_SHARD = GROUP // N_DEV
MAX_WIN = 16
HEAD = 64
N_HEADS = D // HEAD
CHUNK = 64
LEFT = 8
QB = 4 * CHUNK
KB = QB + LEFT * CHUNK
PADK = LEFT * CHUNK
TOEP = 1024
N_REL = 257
MAX_REL = 128
EPS = 1e-6
NEG_INF = -1e30
ATTN_SCALE = HEAD ** -0.5

ADAM_LR = 0.001
ADAM_B1 = 0.9
ADAM_B2 = 0.999
ADAM_EPS = 1e-08
ADAM_WD = 0.01
ADAM_STEP = 10

VMEM_LIMIT = 52 * 1024 * 1024

ANY = pl.BlockSpec(memory_space=pl.ANY)
VMEM = pl.BlockSpec(memory_space=pltpu.VMEM)


def _call(body, **kw):
    return pl.pallas_call(body, **kw)


def _params(*sem):
    return pltpu.CompilerParams(dimension_semantics=sem, vmem_limit_bytes=VMEM_LIMIT)


def _dot(a, b, dims):
    return lax.dot_general(a, b, (dims, ((), ())), preferred_element_type=F32)


def _nn(a, b):
    return _dot(a, b, ((1,), (0,)))


def _nt(a, b):
    return _dot(a, b, ((1,), (1,)))


def _tn(a, b):
    return _dot(a, b, ((0,), (0,)))


def _rstd(x):
    return lax.rsqrt(jnp.mean(x * x, axis=-1, keepdims=True) + EPS)


def _rms_bwd(dh, x, r, g):
    gd = dh * g
    return r * gd - x * (r * r * r) * jnp.mean(gd * x, axis=-1, keepdims=True)


def _colsum8(v):
    return jnp.broadcast_to(jnp.sum(v, axis=0, keepdims=True), (8, v.shape[1]))


def _seg_sum(v):
    r = lax.broadcasted_iota(jnp.int32, (128, 128), 0) // HEAD
    c = lax.broadcasted_iota(jnp.int32, (128, 128), 1) // HEAD
    ones = jnp.where(r == c, 1.0, 0.0).astype(BF16)
    out = []
    for blk in range(v.shape[1] // 128):
        part = v[:, blk * 128:(blk + 1) * 128]
        hi = part.astype(BF16)
        rest = part - hi.astype(F32)
        mid = rest.astype(BF16)
        lo = (rest - mid.astype(F32)).astype(BF16)
        out.append(_nn(hi, ones) + _nn(mid, ones) + _nn(lo, ones))
    return jnp.concatenate(out, axis=1)


def _place():
    return lax.axis_index("x"), lax.axis_index("y"), lax.axis_index("c")


class _Exchange:
    def __init__(self, ins, out_shape, sems, start, mid, finish):
        self.ins, self.out_shape, self.sems = list(ins), list(out_shape), list(sems)
        self.start, self.mid, self.finish = start, mid, finish


def _gather_exchange(shards):
    n = len(shards)

    def tools(ins, outs, sems):
        send_sems, recv_sems, local_sems = sems
        x, y, c = _place()
        me, sibling = (x, y, c), (x, y, 1 - c)
        chips = [(1 - x, y), (x, 1 - y), (1 - x, 1 - y)]

        def slot(k, px, py, pc):
            return outs[k].at[4 * px + 2 * py + pc]

        def copy(k, s, block, to, own=False):
            return pltpu.make_async_remote_copy(
                src_ref=ins[k] if own else slot(k, *block), dst_ref=slot(k, *block),
                send_sem=send_sems.at[k, s], recv_sem=recv_sems.at[k, s], device_id=to, device_id_type=MESH_ID)

        def mine():
            return [pltpu.make_async_copy(ins[k], slot(k, *me), local_sems.at[k]) for k in range(n)]

        def first():
            out = []
            for k in range(n):
                out.append(copy(k, 0, me, sibling, own=True))
                out += [copy(k, 1 + j, me, (*chip, c), own=True) for j, chip in enumerate(chips)]
            return out

        def landed(j):
            return [copy(k, 1 + j, (*chips[j], c), me) for k in range(n)]

        def passed(j):
            return [copy(k, 4 + j, (*chips[j], c), sibling) for k in range(n)]

        def from_sibling():
            out = []
            for k in range(n):
                out.append(copy(k, 0, sibling, me))
                out += [copy(k, 4 + j, (*chip, 1 - c), me) for j, chip in enumerate(chips)]
            return out

        return mine, first, landed, passed, from_sibling

    def start(ins, outs, sems):
        mine, first, _, _, _ = tools(ins, outs, sems)
        for cp in mine() + first():
            cp.start()

    def mid(ins, outs, sems):
        _, _, landed, passed, _ = tools(ins, outs, sems)
        for j in range(3):
            for arrived, onward in zip(landed(j), passed(j)):
                arrived.wait_recv()
                onward.start()

    def finish(ins, outs, sems):
        mine, first, _, passed, from_sibling = tools(ins, outs, sems)
        for cp in from_sibling():
            cp.wait_recv()
        for cp in first() + passed(0) + passed(1) + passed(2):
            cp.wait_send()
        for cp in mine():
            cp.wait()

    return _Exchange(
        shards, [jax.ShapeDtypeStruct((N_DEV,) + a.shape, a.dtype) for a in shards],
        [pltpu.SemaphoreType.DMA((n, 7)), pltpu.SemaphoreType.DMA((n, 7)), pltpu.SemaphoreType.DMA((n,))],
        start, mid, finish)


def _peer(x, y, c, m):
    px = 1 - x if m & 4 else x
    py = 1 - y if m & 2 else y
    pc = 1 - c if m & 1 else c
    return px, py, pc


def _scatter_exchange(parts):
    n = len(parts)

    def tools(ins, outs, sems):
        send_sems, recv_sems, local_sems = sems
        x, y, c = _place()
        me = 4 * x + 2 * y + c
        def mine():
            return [pltpu.make_async_copy(ins[k].at[me], outs[k].at[me], local_sems.at[k]) for k in range(n)]

        def remote(dst_is_mine):
            out = []
            for m in range(1, N_DEV):
                px, py, pc = _peer(x, y, c, m)
                peer = 4 * px + 2 * py + pc
                for k in range(n):
                    out.append(pltpu.make_async_remote_copy(
                        src_ref=ins[k].at[peer], dst_ref=outs[k].at[me if dst_is_mine else peer],
                        send_sem=send_sems.at[k, m - 1], recv_sem=recv_sems.at[k, m - 1],
                        device_id=(px, py, pc), device_id_type=MESH_ID))
            return out

        return mine, remote

    def start(ins, outs, sems):
        mine, remote = tools(ins, outs, sems)
        for cp in mine() + remote(True):
            cp.start()

    def mid(ins, outs, sems):
        pass

    def finish(ins, outs, sems):
        mine, remote = tools(ins, outs, sems)
        for cp in remote(False):
            cp.wait_recv()
        for cp in remote(True):
            cp.wait_send()
        for cp in mine():
            cp.wait()

    return _Exchange(
        parts, [jax.ShapeDtypeStruct(a.shape, a.dtype) for a in parts],
        [pltpu.SemaphoreType.DMA((n, 7)), pltpu.SemaphoreType.DMA((n, 7)), pltpu.SemaphoreType.DMA((n,))],
        start, mid, finish)


N_CHIPS = N_DEV // 2


def _pair_exchange(parts):
    n = len(parts)

    def copies(ins, outs, sems):
        send_sems, recv_sems = sems
        x, y, c = _place()
        return [pltpu.make_async_remote_copy(
            src_ref=ins[k].at[2 * q + 1 - c], dst_ref=outs[k].at[q], send_sem=send_sems.at[k, q],
            recv_sem=recv_sems.at[k, q], device_id=(x, y, 1 - c), device_id_type=MESH_ID)
            for k in range(n) for q in range(N_CHIPS)]

    def start(ins, outs, sems):
        for cp in copies(ins, outs, sems):
            cp.start()

    def mid(ins, outs, sems):
        pass

    def finish(ins, outs, sems):
        for cp in copies(ins, outs, sems):
            cp.wait_recv()
        for cp in copies(ins, outs, sems):
            cp.wait_send()

    return _Exchange(
        parts, [jax.ShapeDtypeStruct((N_CHIPS,) + a.shape[1:], a.dtype) for a in parts],
        [pltpu.SemaphoreType.DMA((n, N_CHIPS)), pltpu.SemaphoreType.DMA((n, N_CHIPS))], start, mid, finish)


def _pair_add(parts, stage, name, copies=1):
    n = len(parts)
    core = lax.axis_index("c").reshape(1)

    def body(core_ref, *refs):
        for k in range(n):
            mine, theirs = refs[k], refs[n + k]
            total = (mine[0, 0].astype(F32) + theirs[0].astype(F32)).astype(BF16)
            for rep in range(copies):
                refs[(2 + rep) * n + k][0] = total

    in_specs, out_specs = [], []
    for a in parts:
        _, r, cdim = a.shape
        in_specs.append(pl.BlockSpec((1, 1, r // 2, cdim), lambda q, i, core_ref: (q, core_ref[0], i, 0)))
    for a in parts:
        _, r, cdim = a.shape
        in_specs.append(pl.BlockSpec((1, r // 2, cdim), lambda q, i, core_ref: (q, i, 0)))
        out_specs.append(pl.BlockSpec((1, r // 2, cdim), lambda q, i, core_ref: (q, i, 0)))
    return list(_call(
        body, name=name,
        grid_spec=pltpu.PrefetchScalarGridSpec(num_scalar_prefetch=1, grid=(N_CHIPS, 2), in_specs=in_specs,
                                               out_specs=out_specs * copies),
        out_shape=[jax.ShapeDtypeStruct(s.shape, BF16) for s in stage] * copies,
        compiler_params=_params("arbitrary", "arbitrary"),
    )(core, *[a.reshape((N_CHIPS, 2) + a.shape[1:]) for a in parts], *stage))


HBM = pl.BlockSpec(memory_space=pltpu.HBM)
SEMAPHORES = pl.BlockSpec(memory_space=pltpu.SEMAPHORE)


def _chip_copies(srcs, lands, send_sems, recv_sems, mine_is_dst):
    x, y, c = _place()
    me = 2 * x + y
    out = []
    for m in range(1, N_CHIPS):
        px, py, _ = _peer(x, y, c, 2 * m)
        peer = 2 * px + py
        for k in range(len(srcs)):
            pair = k * (N_CHIPS - 1) + m - 1
            out.append(pltpu.make_async_remote_copy(
                src_ref=srcs[k].at[peer], dst_ref=lands[k].at[me if mine_is_dst else peer],
                send_sem=send_sems[pair], recv_sem=recv_sems[pair],
                device_id=(px, py, c), device_id_type=MESH_ID))
    return out


def _chip_exchange_start(chip_parts, lands, name):
    n = len(chip_parts)
    pairs = n * (N_CHIPS - 1)

    def body(*refs):
        srcs, zones = refs[:n], refs[n:2 * n]
        sems = refs[2 * n:2 * n + 2 * pairs]
        token = refs[-1]
        for cp in _chip_copies(srcs, zones, sems[:pairs], sems[pairs:], True):
            cp.start()
        token[...] = jnp.zeros(token.shape, F32)

    thru = [pltpu.HBM(a.shape, a.dtype) for a in chip_parts + lands]
    hbm = [pltpu.with_memory_space_constraint(a, pltpu.HBM) for a in chip_parts + lands]
    res = _call(
        body, name=name,
        out_shape=[pltpu.SemaphoreType.DMA(())] * (2 * pairs) + thru + [jax.ShapeDtypeStruct((8, 128), F32)],
        in_specs=[HBM] * (2 * n), out_specs=[SEMAPHORES] * (2 * pairs) + [HBM] * (2 * n) + [VMEM],
        input_output_aliases={i: 2 * pairs + i for i in range(2 * n)},
        compiler_params=pltpu.CompilerParams(has_side_effects=pltpu.SideEffectType.DATAFLOW_SIDE_EFFECTING),
    )(*hbm)
    sems, rest = list(res[:2 * pairs]), res[2 * pairs:]
    return sems, list(rest[:n]), list(rest[n:2 * n]), rest[-1]


def _chip_exchange_wait(sems, chip_parts, lands, after, name):
    n = len(chip_parts)
    pairs = n * (N_CHIPS - 1)

    def body(*refs):
        srcs, zones = refs[:n], refs[n:2 * n]
        sem_refs = refs[2 * n:2 * n + 2 * pairs]
        for cp in _chip_copies(srcs, zones, sem_refs[:pairs], sem_refs[pairs:], False):
            cp.wait_send()
            cp.wait_recv()

    thru = [pltpu.HBM(a.shape, a.dtype) for a in chip_parts + lands]
    res = _call(
        body, name=name, out_shape=thru,
        in_specs=[HBM] * (2 * n) + [SEMAPHORES] * (2 * pairs) + [ANY], out_specs=[HBM] * (2 * n),
        input_output_aliases={i: i for i in range(2 * n)},
        compiler_params=pltpu.CompilerParams(has_side_effects=pltpu.SideEffectType.DATAFLOW_SIDE_EFFECTING),
    )(*chip_parts, *lands, *sems, after)
    return list(res[n:])


def _chip_exchange(chip_parts):
    n = len(chip_parts)

    def tools(ins, outs, sems):
        send_sems, recv_sems, local_sems = sems
        x, y, c = _place()
        me = 2 * x + y

        def mine():
            return [pltpu.make_async_copy(ins[k].at[me], outs[k].at[me], local_sems.at[k]) for k in range(n)]

        def remote(dst_is_mine):
            out = []
            for m in range(1, N_CHIPS):
                px, py, _ = _peer(x, y, c, 2 * m)
                peer = 2 * px + py
                for k in range(n):
                    out.append(pltpu.make_async_remote_copy(
                        src_ref=ins[k].at[peer], dst_ref=outs[k].at[me if dst_is_mine else peer],
                        send_sem=send_sems.at[k, m - 1], recv_sem=recv_sems.at[k, m - 1],
                        device_id=(px, py, c), device_id_type=MESH_ID))
            return out

        return mine, remote

    def start(ins, outs, sems):
        mine, remote = tools(ins, outs, sems)
        for cp in mine() + remote(True):
            cp.start()

    def mid(ins, outs, sems):
        pass

    def finish(ins, outs, sems):
        mine, remote = tools(ins, outs, sems)
        for cp in remote(False):
            cp.wait_recv()
        for cp in remote(True):
            cp.wait_send()
        for cp in mine():
            cp.wait()

    return _Exchange(
        chip_parts, [jax.ShapeDtypeStruct(a.shape, a.dtype) for a in chip_parts],
        [pltpu.SemaphoreType.DMA((n, N_CHIPS - 1)), pltpu.SemaphoreType.DMA((n, N_CHIPS - 1)),
         pltpu.SemaphoreType.DMA((n,))], start, mid, finish)


def _run_exchange(ex, name):
    n_in, n_out = len(ex.ins), len(ex.out_shape)

    def body(*refs):
        ins, outs, sems = refs[:n_in], refs[n_in:n_in + n_out], refs[n_in + n_out:]
        ex.start(ins, outs, sems)
        ex.mid(ins, outs, sems)
        ex.finish(ins, outs, sems)

    return list(_call(body, name=name, out_shape=ex.out_shape, in_specs=[ANY] * n_in, out_specs=[ANY] * n_out,
                      scratch_shapes=ex.sems)(*ex.ins))


def _call_hosting(body, ex, phases, *, in_specs, out_specs, out_shape, scratch_shapes, args, **kw):
    n_in, n_out, n_scr = len(in_specs), len(out_specs), len(scratch_shapes)
    if ex is None:
        res = _call(body, in_specs=in_specs, out_specs=out_specs, out_shape=out_shape,
                    scratch_shapes=scratch_shapes, **kw)(*args)
        return list(res), []
    n_xin, n_xout = len(ex.ins), len(ex.out_shape)

    def hosting(*refs):
        a, b = n_in, n_in + n_xin
        c, d = b + n_out, b + n_out + n_xout
        ins, xins, outs, xouts = refs[:a], refs[a:b], refs[b:c], refs[c:d]
        scr, sems = refs[d:d + n_scr], refs[d + n_scr:]
        first, mid, last = phases()

        @pl.when(first)
        def _():
            ex.start(xins, xouts, sems)

        body(*ins, *outs, *scr)

        @pl.when(mid)
        def _():
            ex.mid(xins, xouts, sems)

        @pl.when(last)
        def _():
            ex.finish(xins, xouts, sems)

    res = _call(hosting, in_specs=list(in_specs) + [ANY] * n_xin, out_specs=list(out_specs) + [ANY] * n_xout,
                out_shape=list(out_shape) + ex.out_shape, scratch_shapes=list(scratch_shapes) + ex.sems,
                **kw)(*args, *ex.ins)
    return list(res[:n_out]), list(res[n_out:])


def _grid_phases(dims, mid_fraction=0.8):
    total = 1
    for d in dims:
        total *= d
    mid = min(max(int(total * mid_fraction), 1), total - 1)

    def phases():
        step = pl.program_id(0)
        for axis in range(1, len(dims)):
            step = step * dims[axis] + pl.program_id(axis)
        return step == 0, step == mid, step == total - 1
    return phases


def _all_reduce_small(pack, name):
    rows = pack.shape[0]

    def body(in_ref, out_ref, recv, send_sems, recv_sems):
        x, y, c = _place()
        me = 4 * x + 2 * y + c
        recv[me] = in_ref[...]
        sent = []
        for m in range(1, N_DEV):
            px, py, pc = _peer(x, y, c, m)
            cp = pltpu.make_async_remote_copy(
                src_ref=in_ref, dst_ref=recv.at[me], send_sem=send_sems.at[m - 1], recv_sem=recv_sems.at[m - 1],
                device_id=(px, py, pc), device_id_type=MESH_ID)
            cp.start()
            sent.append(cp)
        for m in range(1, N_DEV):
            px, py, pc = _peer(x, y, c, m)
            peer = 4 * px + 2 * py + pc
            pltpu.make_async_remote_copy(
                src_ref=in_ref, dst_ref=recv.at[peer], send_sem=send_sems.at[m - 1], recv_sem=recv_sems.at[m - 1],
                device_id=(px, py, pc), device_id_type=MESH_ID).wait_recv()
        acc = recv[0]
        for d in range(1, N_DEV):
            acc = acc + recv[d]
        out_ref[...] = acc
        for cp in sent:
            cp.wait_send()

    return _call(
        body, name=name, out_shape=jax.ShapeDtypeStruct(pack.shape, F32), in_specs=[VMEM], out_specs=VMEM,
        scratch_shapes=[pltpu.VMEM((N_DEV, rows, pack.shape[1]), F32), pltpu.SemaphoreType.DMA((7,)),
                        pltpu.SemaphoreType.DMA((7,))],
        compiler_params=pltpu.CompilerParams(vmem_limit_bytes=VMEM_LIMIT),
    )(pack)


POOL_TS = 256


def _pool_counts(first_row, rows, win):
    t = first_row + lax.broadcasted_iota(jnp.int32, (rows, 1), 0)
    return jnp.minimum(t + 1, win).astype(F32)


def _pool_fwd(x, g, w, b, scale):
    nt = S // POOL_TS

    def body(x_ref, g_ref, w_ref, b_ref, s_ref, out_ref, diff_ref, ext):
        i = pl.program_id(0)

        @pl.when(i == 0)
        def _():
            ext[0:MAX_WIN, :] = jnp.zeros((MAX_WIN, D), F32)

        @pl.when(i > 0)
        def _():
            ext[0:MAX_WIN, :] = ext[POOL_TS:POOL_TS + MAX_WIN, :]

        xv = x_ref[...]
        h = xv * _rstd(xv) * g_ref[...]
        ext[MAX_WIN:, :] = h
        for gi in range(N_GROUPS):
            win = 2 << gi
            cols = slice(gi * GROUP, (gi + 1) * GROUP)
            sm = ext[:, cols]
            k = 1
            while k < win:
                sm = sm + pltpu.roll(sm, k, axis=0)
                k *= 2
            pooled = sm[MAX_WIN:, :] / _pool_counts(i * POOL_TS, POOL_TS, win)
            diff = (pooled - h[:, cols]).astype(BF16)
            yv = (_nn(diff, w_ref[gi]) + b_ref[:, cols]) * s_ref[:, cols]
            out_ref[:, cols] = xv[:, cols] + yv
            diff_ref[:, cols] = diff

    row = pl.BlockSpec((1, D), lambda i: (0, 0))
    tile = pl.BlockSpec((POOL_TS, D), lambda i: (i, 0))
    return _call(
        body, name="pool_fwd", grid=(nt,),
        in_specs=[tile, row, pl.BlockSpec((N_GROUPS, GROUP, GROUP), lambda i: (0, 0, 0)), row, row],
        out_specs=[tile, tile],
        out_shape=[jax.ShapeDtypeStruct((S, D), F32), jax.ShapeDtypeStruct((S, D), BF16)],
        scratch_shapes=[pltpu.VMEM((POOL_TS + MAX_WIN, D), F32)],
        compiler_params=_params("arbitrary"),
    )(x, g, w, b, scale)


def _pool_bwd(dy, x, diff, g, w, b, scale, ex=None):
    nt = S // POOL_TS

    def body(dy_ref, x_ref, diff_ref, g_ref, w_ref, b_ref, s_ref, gx_ref, dw_ref, db_ref, ds_ref, dg_ref, ext, dh):
        i = pl.program_id(0)
        first_row = (nt - 1 - i) * POOL_TS

        @pl.when(i == 0)
        def _():
            ext[POOL_TS:, :] = jnp.zeros((MAX_WIN, D), F32)
            dw_ref[...] = jnp.zeros(dw_ref.shape, F32)
            db_ref[...] = jnp.zeros(db_ref.shape, F32)
            ds_ref[...] = jnp.zeros(ds_ref.shape, F32)
            dg_ref[...] = jnp.zeros(dg_ref.shape, F32)

        @pl.when(i > 0)
        def _():
            ext[POOL_TS:, :] = ext[0:MAX_WIN, :]

        dyv = dy_ref[...]
        for gi in range(N_GROUPS):
            win = 2 << gi
            cols = slice(gi * GROUP, (gi + 1) * GROUP)
            dfb = diff_ref[:, cols]
            z = _nn(dfb, w_ref[gi]) + b_ref[:, cols]
            dyg = dyv[:, cols]
            ds_ref[:, cols] += _colsum8(dyg * z)
            dz = dyg * s_ref[:, cols]
            db_ref[:, cols] += _colsum8(dz)
            dzb = dz.astype(BF16)
            dw_ref[gi] += _tn(dfb, dzb)
            ddiff = _nt(dzb, w_ref[gi])
            ext[0:POOL_TS, cols] = ddiff / _pool_counts(first_row, POOL_TS, win)
            sm = ext[:, cols]
            k = 1
            while k < win:
                sm = sm + pltpu.roll(sm, POOL_TS + MAX_WIN - k, axis=0)
                k *= 2
            dh[:, cols] = sm[0:POOL_TS, :] - ddiff
        xv = x_ref[...]
        r = _rstd(xv)
        gv = g_ref[...]
        dhv = dh[...]
        dg_ref[...] += _colsum8(dhv * xv * r)
        gx_ref[...] = dyv + _rms_bwd(dhv, xv, r, gv)

    row = pl.BlockSpec((1, D), lambda i: (0, 0))
    tile = pl.BlockSpec((POOL_TS, D), lambda i: (nt - 1 - i, 0))
    acc = pl.BlockSpec((8, D), lambda i: (0, 0))
    wspec = pl.BlockSpec((N_GROUPS, GROUP, GROUP), lambda i: (0, 0, 0))
    return _call_hosting(
        body, ex, _grid_phases((nt,)), name="pool_bwd", grid=(nt,),
        in_specs=[tile, tile, tile, row, wspec, row, row],
        out_specs=[tile, wspec, acc, acc, acc],
        out_shape=[jax.ShapeDtypeStruct((S, D), F32), jax.ShapeDtypeStruct((N_GROUPS, GROUP, GROUP), F32),
                   jax.ShapeDtypeStruct((8, D), F32), jax.ShapeDtypeStruct((8, D), F32),
                   jax.ShapeDtypeStruct((8, D), F32)],
        scratch_shapes=[pltpu.VMEM((POOL_TS + MAX_WIN, D), F32), pltpu.VMEM((POOL_TS, D), F32)],
        args=(dy, x, diff, g, w, b, scale), compiler_params=_params("arbitrary"),
    )


FFN_TS = min(S, 1024)
FFN_TF = 256


def _ffn_fwd(x, g, wg_t, wu_t, wd, target=None, ex=None):
    ni, nj = S // FFN_TS, F // FFN_TF
    with_loss = target is not None

    def body(*refs):
        if with_loss:
            x_ref, g_ref, wg_ref, wu_ref, wd_ref, t_ref, out_ref, loss_ref, h_ref, gg_ref, uu_ref, hs, acc = refs
        else:
            x_ref, g_ref, wg_ref, wu_ref, wd_ref, out_ref, h_ref, gg_ref, uu_ref, hs, acc = refs
        i, j = pl.program_id(0), pl.program_id(1)

        @pl.when(j == 0)
        def _():
            xv = x_ref[...]
            hb = (xv * _rstd(xv) * g_ref[...]).astype(BF16)
            hs[...] = hb
            h_ref[...] = hb
            acc[...] = jnp.zeros(acc.shape, F32)

        hb = hs[...]
        gg = _nt(hb, wg_ref[...])
        uu = _nt(hb, wu_ref[...])
        gg_ref[...] = gg
        uu_ref[...] = uu
        a = (gg * jax.nn.sigmoid(gg) * uu).astype(BF16)
        acc[...] += _nn(a, wd_ref[...])

        @pl.when(j == nj - 1)
        def _():
            yv = x_ref[...] + acc[...]
            if with_loss:
                err = yv - t_ref[...]
                out_ref[...] = err * (1.0 / D)
                part = jnp.sum(err * err) * (0.5 / D)

                @pl.when(i == 0)
                def _():
                    loss_ref[...] = jnp.zeros(loss_ref.shape, F32)

                loss_ref[...] += jnp.broadcast_to(part, loss_ref.shape)
            else:
                out_ref[...] = yv

    xt = pl.BlockSpec((FFN_TS, D), lambda i, j: (i, 0))
    row = pl.BlockSpec((1, D), lambda i, j: (0, 0))
    wt = pl.BlockSpec((FFN_TF, D), lambda i, j: (j, 0))
    gt = pl.BlockSpec((FFN_TS, FFN_TF), lambda i, j: (i, j))
    in_specs = [xt, row, wt, wt, wt] + ([xt] if with_loss else [])
    out_specs = [xt] + ([pl.BlockSpec((8, 128), lambda i, j: (0, 0))] if with_loss else []) + [xt, gt, gt]
    out_shape = ([jax.ShapeDtypeStruct((S, D), F32)] + ([jax.ShapeDtypeStruct((8, 128), F32)] if with_loss else [])
                 + [jax.ShapeDtypeStruct((S, D), BF16), jax.ShapeDtypeStruct((S, F), F32),
                    jax.ShapeDtypeStruct((S, F), F32)])
    args = (x, g, wg_t, wu_t, wd) + ((target,) if with_loss else ())
    return _call_hosting(
        body, ex, _grid_phases((ni, nj), 0.8), name="ffn_fwd_loss" if with_loss else "ffn_fwd", grid=(ni, nj),
        in_specs=in_specs, out_specs=out_specs, out_shape=out_shape,
        scratch_shapes=[pltpu.VMEM((FFN_TS, D), BF16), pltpu.VMEM((FFN_TS, D), F32)], args=args,
        compiler_params=_params("arbitrary", "arbitrary"),
    )


def _ffn_bwd_weights(dout, h, gg, uu, wd, name, ex=None):
    nj = F // FFN_TF

    def body(do_ref, h_ref, gg_ref, uu_ref, wd_ref, dg_ref, du_ref, dwg_ref, dwu_ref, dwd_ref, dob):
        @pl.when(pl.program_id(0) == 0)
        def _():
            dob[...] = do_ref[...].astype(BF16)

        hb, dov = h_ref[...], dob[...]
        gv, uv = gg_ref[...], uu_ref[...]
        da = _nt(dov, wd_ref[...])
        sg = jax.nn.sigmoid(gv)
        sl = gv * sg
        ab = (sl * uv).astype(BF16)
        dub = (da * sl).astype(BF16)
        dgb = (da * uv * (sg * (1.0 + gv * (1.0 - sg)))).astype(BF16)
        dg_ref[...] = dgb
        du_ref[...] = dub
        dwg_ref[...] = _tn(dgb, hb).astype(BF16)
        dwu_ref[...] = _tn(dub, hb).astype(BF16)
        dwd_ref[...] = _tn(ab, dov).astype(BF16)

    once = pl.Buffered(1)
    whole = lambda: pl.BlockSpec((S, D), lambda j, _: (0, 0), pipeline_mode=once)
    wt = pl.BlockSpec((FFN_TF, D), lambda j, _: (j, 0))
    gt = pl.BlockSpec((S, FFN_TF), lambda j, _: (0, j))
    return _call_hosting(
        body, ex, _grid_phases((nj, 1)), name=name, grid=(nj, 1),
        in_specs=[whole(), whole(), gt, gt, wt], out_specs=[gt, gt, wt, wt, wt],
        out_shape=[jax.ShapeDtypeStruct((S, F), BF16)] * 2 + [jax.ShapeDtypeStruct((F, D), BF16)] * 3,
        scratch_shapes=[pltpu.VMEM((S, D), BF16)], args=(dout, h, gg, uu, wd),
        compiler_params=_params("arbitrary", "arbitrary"),
    )


BWD_TS = 512


def _ffn_bwd_input(dres, x, g, dg, du, wg_t, wu_t, name, ex=None):
    nt = S // BWD_TS

    def body(dres_ref, x_ref, g_ref, dg_ref, du_ref, wg_ref, wu_ref, dx_ref, dgam_ref):
        dh = _nn(dg_ref[...], wg_ref[...]) + _nn(du_ref[...], wu_ref[...])
        xv = x_ref[...]
        r = _rstd(xv)
        dx_ref[...] = dres_ref[...] + _rms_bwd(dh, xv, r, g_ref[...])

        @pl.when(pl.program_id(0) == 0)
        def _():
            dgam_ref[...] = jnp.zeros(dgam_ref.shape, F32)

        dgam_ref[...] += _colsum8(dh * xv * r)

    tile = pl.BlockSpec((BWD_TS, D), lambda i: (i, 0))
    ftile = pl.BlockSpec((BWD_TS, F), lambda i: (i, 0))
    wspec = lambda: pl.BlockSpec((F, D), lambda i: (0, 0), pipeline_mode=pl.Buffered(1))
    return _call_hosting(
        body, ex, _grid_phases((nt,)), name=name, grid=(nt,),
        in_specs=[tile, tile, pl.BlockSpec((1, D), lambda i: (0, 0)), ftile, ftile, wspec(), wspec()],
        out_specs=[tile, pl.BlockSpec((8, D), lambda i: (0, 0))],
        out_shape=[jax.ShapeDtypeStruct((S, D), F32), jax.ShapeDtypeStruct((8, D), F32)],
        scratch_shapes=[], args=(dres, x, g, dg, du, wg_t, wu_t), compiler_params=_params("arbitrary"),
    )


NORM_TS = 256


def _norm_bwd(dres, x, dhs, gs, name, ex=None):
    n = len(dhs)
    nt = S // NORM_TS

    def body(*refs):
        dres_ref, x_ref = refs[:2]
        dh_refs, g_refs = refs[2:2 + n], refs[2 + n:2 + 2 * n]
        dx_ref, dg_refs = refs[2 + 2 * n], refs[3 + 2 * n:]
        i = pl.program_id(0)
        xv = x_ref[...]
        r = _rstd(xv)
        dx = dres_ref[...]
        for k in range(n):
            dhv = dh_refs[k][...]
            dx = dx + _rms_bwd(dhv, xv, r, g_refs[k][...])

            @pl.when(i == 0)
            def _():
                dg_refs[k][...] = jnp.zeros((8, D), F32)

            dg_refs[k][...] += _colsum8(dhv * xv * r)
        dx_ref[...] = dx

    tile = pl.BlockSpec((NORM_TS, D), lambda i: (i, 0))
    row = pl.BlockSpec((1, D), lambda i: (0, 0))
    acc = pl.BlockSpec((8, D), lambda i: (0, 0))
    return _call_hosting(
        body, ex, _grid_phases((nt,)), name=name, grid=(nt,),
        in_specs=[tile, tile] + [tile] * n + [row] * n,
        out_specs=[tile] + [acc] * n,
        out_shape=[jax.ShapeDtypeStruct((S, D), F32)] + [jax.ShapeDtypeStruct((8, D), F32)] * n,
        scratch_shapes=[], args=(dres, x, *dhs, *gs), compiler_params=_params("arbitrary"),
    )


def _mm(a, b, mode, out_dtype, name, add=None):
    if mode == "nn":
        (m, kd), n = a.shape, b.shape[1]
    elif mode == "nt":
        (m, kd), n = a.shape, b.shape[0]
    else:
        (kd, m), n = a.shape, b.shape[1]
    tm, tn, tk = min(m, 1024), min(n, 1024), min(kd, 1024)
    nk = kd // tk
    dot = {"nn": _nn, "nt": _nt, "tn": _tn}[mode]

    def body(*refs):
        if add is None:
            a_ref, b_ref, o_ref, acc = refs
        else:
            a_ref, b_ref, add_ref, o_ref, acc = refs
        k = pl.program_id(2)

        @pl.when(k == 0)
        def _():
            acc[...] = jnp.zeros(acc.shape, F32)

        acc[...] += dot(a_ref[...].astype(BF16), b_ref[...].astype(BF16))

        @pl.when(k == nk - 1)
        def _():
            res = acc[...]
            if add is not None:
                res = res + add_ref[...]
            o_ref[...] = res.astype(out_dtype)

    if mode == "tn":
        a_spec = pl.BlockSpec((tk, tm), lambda i, j, k: (k, i))
        b_spec = pl.BlockSpec((tk, tn), lambda i, j, k: (k, j))
    else:
        a_spec = pl.BlockSpec((tm, tk), lambda i, j, k: (i, k))
        b_spec = (pl.BlockSpec((tk, tn), lambda i, j, k: (k, j)) if mode == "nn"
                  else pl.BlockSpec((tn, tk), lambda i, j, k: (j, k)))
    o_spec = pl.BlockSpec((tm, tn), lambda i, j, k: (i, j))
    in_specs = [a_spec, b_spec] + ([o_spec] if add is not None else [])
    args = (a, b) + ((add,) if add is not None else ())
    return _call(
        body, name=name, grid=(m // tm, n // tn, nk), in_specs=in_specs, out_specs=o_spec,
        out_shape=jax.ShapeDtypeStruct((m, n), out_dtype), scratch_shapes=[pltpu.VMEM((tm, tn), F32)],
        compiler_params=_params("parallel", "parallel", "arbitrary"),
    )(*args)


PROJ_TS = 256


def _kvq_proj(x, g_kv, g_mix, wk, wv, wq, gk, gq):
    def body(x_ref, gkv_ref, gmix_ref, wk_ref, wv_ref, wq_ref, gk_ref, gq_ref,
             hkv_ref, h1_ref, kpre_ref, qpre_ref, k_ref, v_ref, q_ref):
        xv = x_ref[...]
        xr = xv * _rstd(xv)
        hkv = (xr * gkv_ref[...]).astype(BF16)
        h1 = (xr * gmix_ref[...]).astype(BF16)
        hkv_ref[...] = hkv
        h1_ref[...] = h1
        kpre = _nn(hkv, wk_ref[...])
        qpre = _nn(h1, wq_ref[...])
        kpre_ref[...] = kpre
        qpre_ref[...] = qpre
        v_ref[...] = _nn(hkv, wv_ref[...]).astype(BF16)
        rk = lax.rsqrt(_seg_sum(kpre * kpre) * (1.0 / HEAD) + EPS)
        k_ref[...] = (kpre * rk * gk_ref[...]).astype(BF16)
        rq = lax.rsqrt(_seg_sum(qpre * qpre) * (1.0 / HEAD) + EPS)
        q_ref[...] = (qpre * rq * gq_ref[...]).astype(BF16)

    tile = pl.BlockSpec((PROJ_TS, D), lambda i: (i, 0))
    row = pl.BlockSpec((1, D), lambda i: (0, 0))
    wspec = pl.BlockSpec((D, D), lambda i: (0, 0))
    bf = jax.ShapeDtypeStruct((S, D), BF16)
    ff = jax.ShapeDtypeStruct((S, D), F32)
    return _call(
        body, name="kvq_proj", grid=(S // PROJ_TS,),
        in_specs=[tile, row, row, wspec, wspec, wspec, row, row],
        out_specs=[tile] * 7, out_shape=[bf, bf, ff, ff, bf, bf, bf],
        compiler_params=_params("arbitrary"),
    )(x, g_kv, g_mix, wk, wv, wq, gk, gq)


def _head_norm_bwd(dout, pre, hg, name, row_off=0):
    nt = S // PROJ_TS

    def body(do_ref, pre_ref, hg_ref, dpre_ref, dhg_ref, acc):
        i = pl.program_id(0)
        dov, pv, hgv = do_ref[...], pre_ref[...], hg_ref[...]
        r = lax.rsqrt(_seg_sum(pv * pv) * (1.0 / HEAD) + EPS)
        gd = dov * hgv
        dpre = r * gd - pv * (r * r * r) * (_seg_sum(gd * pv) * (1.0 / HEAD))
        dpre_ref[...] = dpre.astype(BF16)

        @pl.when(i == 0)
        def _():
            acc[...] = jnp.zeros(acc.shape, F32)

        acc[...] += _colsum8(dov * pv * r)

        @pl.when(i == nt - 1)
        def _():
            full = acc[...]
            fold = full[:, 0:128]
            for blk in range(1, D // 128):
                fold = fold + full[:, blk * 128:(blk + 1) * 128]
            dhg_ref[...] = fold + pltpu.roll(fold, HEAD, axis=1)

    tile = pl.BlockSpec((PROJ_TS, D), lambda i: (i, 0))
    return _call(
        body, name=name, grid=(nt,),
        in_specs=[pl.BlockSpec((PROJ_TS, D), lambda i: (i + row_off, 0)), tile, pl.BlockSpec((1, D), lambda i: (0, 0))],
        out_specs=[tile, pl.BlockSpec((8, 128), lambda i: (0, 0))],
        out_shape=[jax.ShapeDtypeStruct((S, D), BF16), jax.ShapeDtypeStruct((8, 128), F32)],
        scratch_shapes=[pltpu.VMEM((8, D), F32)],
        compiler_params=_params("arbitrary"),
    )(dout, pre, hg)


def _toeplitz_from_table(table):
    far = jnp.broadcast_to(table[:, N_REL - 1:], (N_HEADS, PADK - MAX_REL + 1))
    near = table[:, N_REL - 2::-1]
    past = jnp.broadcast_to(table[:, 0:1], (N_HEADS, MAX_REL))
    wrap = jnp.broadcast_to(table[:, N_REL - 1:], (N_HEADS, QB - 1))
    return jnp.concatenate([far, near, past, wrap], axis=1).reshape(N_HEADS, 1, TOEP)


def _table_grad_from_toeplitz(dtp, seg):
    lo = PADK - MAX_REL + 1
    near = dtp[:, lo + N_REL - 3:lo - 1:-1]
    return jnp.concatenate([seg[:, 1:2], near, seg[:, 0:1]], axis=1)


def _bias_band(tp):
    def body(tp_ref, out_ref):
        bv = pltpu.roll(jnp.broadcast_to(tp_ref[0], (QB, TOEP)), 0, axis=1, stride=1, stride_axis=0)
        out_ref[0] = bv[:, 0:KB]

    return _call(
        body, name="bias_band", grid=(N_HEADS,),
        in_specs=[pl.BlockSpec((1, 1, TOEP), lambda h: (h, 0, 0))],
        out_specs=pl.BlockSpec((1, QB, KB), lambda h: (h, 0, 0)),
        out_shape=jax.ShapeDtypeStruct((N_HEADS, QB, KB), F32),
        compiler_params=_params("parallel"),
    )(tp)


def _bias_grad(dband):
    lo, hi = PADK - MAX_REL + 1, PADK + MAX_REL

    def body(db_ref, dtp_ref, seg_ref):
        bv = jnp.concatenate([db_ref[0], jnp.zeros((QB, TOEP - KB), F32)], axis=1)
        row = lax.broadcasted_iota(jnp.int32, (QB, TOEP), 0)
        k = 1
        while k < QB:
            bv = jnp.where((row & k) != 0, pltpu.roll(bv, TOEP - k, axis=1), bv)
            k *= 2
        col = jnp.sum(bv, axis=0, keepdims=True)
        dtp_ref[0] = col
        u = lax.broadcasted_iota(jnp.int32, (1, TOEP), 1)
        far = jnp.sum(jnp.where((u < lo) | (u > hi + MAX_REL), col, 0.0))
        past = jnp.sum(jnp.where((u >= hi) & (u <= hi + MAX_REL), col, 0.0))
        lane = lax.broadcasted_iota(jnp.int32, (1, 128), 1)
        seg_ref[0] = jnp.where(lane == 0, far, jnp.where(lane == 1, past, 0.0))

    return _call(
        body, name="bias_grad", grid=(N_HEADS,),
        in_specs=[pl.BlockSpec((1, QB, KB), lambda h: (h, 0, 0))],
        out_specs=[pl.BlockSpec((1, 1, TOEP), lambda h: (h, 0, 0)), pl.BlockSpec((1, 1, 128), lambda h: (h, 0, 0))],
        out_shape=[jax.ShapeDtypeStruct((N_HEADS, 1, TOEP), F32), jax.ShapeDtypeStruct((N_HEADS, 1, 128), F32)],
        compiler_params=_params("parallel"),
    )(dband)


N_QB = S // QB
N_HP = D // 128


def _band_mask(cb):
    qc = lax.broadcasted_iota(jnp.int32, (QB, KB), 0) >> 6
    p = lax.broadcasted_iota(jnp.int32, (QB, KB), 1)
    kc = p >> 6
    return (kc >= qc) & (kc <= qc + LEFT) & (p + cb * QB >= PADK)


def _half_mask(hh):
    lane = lax.broadcasted_iota(jnp.int32, (1, 128), 1)
    return jnp.where((lane < HEAD) == (hh == 0), 1.0, 0.0).astype(BF16)


def _probs(qh, kb, bias, mask):
    sc = _nt(qh, kb) * ATTN_SCALE + bias
    sc = jnp.where(mask, sc, NEG_INF)
    e = jnp.exp(sc - jnp.max(sc, axis=-1, keepdims=True))
    return e / jnp.sum(e, axis=-1, keepdims=True)


def _attn_fwd(q, kp, vp, bias, ex=None):
    def body(q_ref, k_ref, v_ref, b_ref, o_ref):
        cb = pl.program_id(1)
        band = pl.ds(pl.multiple_of(cb * QB, QB), KB)
        kb, vb = k_ref[band, :], v_ref[band, :]
        qv = q_ref[...]
        mask = _band_mask(cb)
        low = lax.broadcasted_iota(jnp.int32, (QB, 128), 1) < HEAD
        outs = []
        for hh in range(2):
            pb = _probs(qv * _half_mask(hh), kb, b_ref[hh], mask).astype(BF16)
            outs.append(_nn(pb, vb))
        o_ref[...] = jnp.where(low, outs[0], outs[1]).astype(BF16)

    qspec = pl.BlockSpec((QB, 128), lambda hp, cb: (cb, hp))
    kspec = pl.BlockSpec((PADK + S, 128), lambda hp, cb: (0, hp))
    return _call_hosting(
        body, ex, _grid_phases((N_HP, N_QB), 0.85), name="attn_fwd", grid=(N_HP, N_QB),
        in_specs=[qspec, kspec, kspec, pl.BlockSpec((2, QB, KB), lambda hp, cb: (hp, 0, 0))],
        out_specs=[qspec], out_shape=[jax.ShapeDtypeStruct((S, D), BF16)], scratch_shapes=[],
        args=(q, kp, vp, bias), compiler_params=_params("arbitrary", "arbitrary"),
    )


def _attn_bwd(q, kp, vp, bias, do, ex=None):
    def body(q_ref, k_ref, v_ref, b_ref, do_ref, dq_ref, dk_ref, dv_ref, db_ref):
        cb = pl.program_id(1)

        @pl.when(cb == 0)
        def _():
            dk_ref[...] = jnp.zeros(dk_ref.shape, F32)
            dv_ref[...] = jnp.zeros(dv_ref.shape, F32)
            db_ref[...] = jnp.zeros(db_ref.shape, F32)

        band = pl.ds(pl.multiple_of(cb * QB, QB), KB)
        kb, vb = k_ref[band, :], v_ref[band, :]
        qv, dov = q_ref[...], do_ref[...]
        mask = _band_mask(cb)
        low = lax.broadcasted_iota(jnp.int32, (QB, 128), 1) < HEAD
        dq = jnp.zeros((QB, 128), F32)
        dkb = jnp.zeros((KB, 128), F32)
        dvb = jnp.zeros((KB, 128), F32)
        for hh in range(2):
            sel = low if hh == 0 else jnp.logical_not(low)
            qh = qv * _half_mask(hh)
            doh = dov * _half_mask(hh)
            p = _probs(qh, kb, b_ref[hh], mask)
            dp = _nt(doh, vb)
            dvb = dvb + _tn(p.astype(BF16), doh)
            ds = p * (dp - jnp.sum(dp * p, axis=-1, keepdims=True))
            db_ref[hh] += ds
            dsb = (ds * ATTN_SCALE).astype(BF16)
            dq = dq + jnp.where(sel, _nn(dsb, kb), 0.0)
            dkb = dkb + _tn(dsb, qh)
        dq_ref[...] = dq
        dk_ref[band, :] += dkb
        dv_ref[band, :] += dvb

    qspec = pl.BlockSpec((QB, 128), lambda hp, cb: (cb, hp))
    kspec = pl.BlockSpec((PADK + S, 128), lambda hp, cb: (0, hp))
    bspec = pl.BlockSpec((2, QB, KB), lambda hp, cb: (hp, 0, 0))
    kf = jax.ShapeDtypeStruct((PADK + S, D), F32)
    return _call_hosting(
        body, ex, _grid_phases((N_HP, N_QB)), name="attn_bwd", grid=(N_HP, N_QB),
        in_specs=[qspec, kspec, kspec, bspec, qspec],
        out_specs=[qspec, kspec, kspec, bspec],
        out_shape=[jax.ShapeDtypeStruct((S, D), F32), kf, kf, jax.ShapeDtypeStruct((N_HEADS, QB, KB), F32)],
        scratch_shapes=[], args=(q, kp, vp, bias, do), compiler_params=_params("arbitrary", "arbitrary"),
    )


def _adam_math(w, g, m, v):
    m = ADAM_B1 * m + (1.0 - ADAM_B1) * g
    v = ADAM_B2 * v + (1.0 - ADAM_B2) * (g * g)
    m_hat = m / (1.0 - ADAM_B1 ** ADAM_STEP)
    v_hat = v / (1.0 - ADAM_B2 ** ADAM_STEP)
    delta = -ADAM_LR * (m_hat / (jnp.sqrt(v_hat) + ADAM_EPS) + ADAM_WD * w)
    return delta, m, v


def _sum_parts(parts, name):
    count, r, c = parts.shape
    tr = r // 2 if r % 16 == 0 and r > 64 else r

    def body(p_ref, o_ref):
        acc = p_ref[0].astype(F32)
        for d in range(1, count):
            acc = acc + p_ref[d].astype(F32)
        o_ref[...] = acc

    return _call(
        body, name=name, grid=(r // tr,),
        in_specs=[pl.BlockSpec((count, tr, c), lambda i: (0, i, 0))],
        out_specs=pl.BlockSpec((tr, c), lambda i: (i, 0)),
        out_shape=jax.ShapeDtypeStruct((r, c), F32), compiler_params=_params("parallel"),
    )(parts)


def _row_tile(r):
    for cand in (256, 176, 128, 64, 32, 16, 8):
        if r % cand == 0:
            return cand
    return r


def _adam_layer(w, g, m, v, layer, name, prev=None):
    nl, r, c = w.shape
    tr = _row_tile(r)

    def body(w_ref, g_ref, m_ref, v_ref, *rest):
        go_ref, d_ref, nm_ref, nv_ref = rest[-4:]
        gv = g_ref[...]
        delta, nm, nv = _adam_math(w_ref[0], gv, m_ref[0], v_ref[0])
        go_ref[0] = gv
        d_ref[0] = delta
        nm_ref[0] = nm
        nv_ref[0] = nv

    lspec = pl.BlockSpec((1, tr, c), lambda i: (layer, i, 0))
    sd = jax.ShapeDtypeStruct((nl, r, c), F32)
    extra = list(prev) if prev is not None else []
    return _call(
        body, name=name, grid=(r // tr,),
        in_specs=[lspec, pl.BlockSpec((tr, c), lambda i: (i, 0)), lspec, lspec] + [ANY] * len(extra),
        out_specs=[lspec] * 4, out_shape=[sd] * 4,
        input_output_aliases={4 + t: t for t in range(len(extra))},
        compiler_params=_params("parallel"),
    )(w, g, m, v, *extra)


def _adam(w, g, m, v, name):
    r, c = w.shape
    tr = _row_tile(r)

    def body(w_ref, g_ref, m_ref, v_ref, d_ref, nm_ref, nv_ref):
        delta, nm, nv = _adam_math(w_ref[...], g_ref[...], m_ref[...], v_ref[...])
        d_ref[...] = delta
        nm_ref[...] = nm
        nv_ref[...] = nv

    spec = pl.BlockSpec((tr, c), lambda i: (i, 0))
    sd = jax.ShapeDtypeStruct((r, c), F32)
    return _call(
        body, name=name, grid=(r // tr,), in_specs=[spec] * 4, out_specs=[spec] * 3, out_shape=[sd] * 3,
        compiler_params=_params("parallel"),
    )(w, g, m, v)


def _pad8(rows):
    return jnp.pad(rows, ((0, 8 - rows.shape[0]), (0, 0)))


def kernel(x, norm_mix_g, norm_ffn_g, pool_w, pool_b, pool_scale, kv_norm_g, w_k, w_v, k_norm_g, w_q, q_norm_g, rel_bias, w_o, w_gate, w_up, w_down, loss_target, m_norm_mix_g, m_norm_ffn_g, m_pool_w, m_pool_b, m_pool_scale, m_kv_norm_g, m_w_k, m_w_v, m_k_norm_g, m_w_q, m_q_norm_g, m_rel_bias, m_w_o, m_w_gate, m_w_up, m_w_down, v_norm_mix_g, v_norm_ffn_g, v_pool_w, v_pool_b, v_pool_scale, v_kv_norm_g, v_w_k, v_w_v, v_k_norm_g, v_w_q, v_q_norm_g, v_rel_bias, v_w_o, v_w_gate, v_w_up, v_w_down):
    assert x.shape == (1, S, D) and w_gate.shape == (2, D, F_SHARD) and w_k.shape == (D_SHARD, D)
    xin, target = x[0], loss_target[0]

    ffn_shards = [[w_gate[layer].T.astype(BF16), w_up[layer].T.astype(BF16), w_down[layer].astype(BF16)]
                  for layer in range(2)]
    att_shards = [w_k.astype(BF16), w_v.astype(BF16), w_q[0].astype(BF16), w_o[0].astype(BF16)]
    pool_shard = pool_w[0].astype(BF16).reshape(N_GROUPS * POOL_SHARD, GROUP)
    small = jnp.concatenate([pool_b[0].reshape(1, N_GROUPS * POOL_SHARD), pool_scale], axis=1)

    full0 = _run_exchange(_gather_exchange(ffn_shards[0] + [pool_shard, _pad8(small)]), "gather_layer0")
    ffn_w0 = [a.reshape(F, D) for a in full0[:3]]
    pw_f = full0[3].reshape(N_DEV, N_GROUPS, POOL_SHARD, GROUP).transpose(1, 0, 2, 3).reshape(N_GROUPS, GROUP, GROUP)
    small_f = full0[4][:, 0, :]
    pb_f = small_f[:, :N_GROUPS * POOL_SHARD].reshape(N_DEV, N_GROUPS, POOL_SHARD).transpose(1, 0, 2).reshape(1, D)
    ps_f = small_f[:, N_GROUPS * POOL_SHARD:].reshape(1, D)

    g_mix0, g_mix1 = norm_mix_g[0:1], norm_mix_g[1:2]
    g_ffn0, g_ffn1 = norm_ffn_g[0:1], norm_ffn_g[1:2]
    g_kv = kv_norm_g.reshape(1, D)
    gk_t = jnp.tile(k_norm_g.reshape(1, HEAD), (1, N_HEADS))
    gq_t = jnp.tile(q_norm_g.reshape(1, HEAD), (1, N_HEADS))

    x1, diff = _pool_fwd(xin, g_mix0, pw_f, pb_f, ps_f)
    (x2, hf0, gg0, uu0), full_att = _ffn_fwd(x1, g_ffn0, *ffn_w0, ex=_gather_exchange(att_shards))
    wk_f, wv_f, wq_f, wo_f = [a.reshape(D, D) for a in full_att]
    hkv, h1, kpre, qpre, kk, vv, qq = _kvq_proj(x2, g_kv, g_mix1, wk_f, wv_f, wq_f, gk_t, gq_t)
    kp = jnp.pad(kk, ((PADK, 0), (0, 0)))
    vp = jnp.pad(vv, ((PADK, 0), (0, 0)))
    bias = _bias_band(_toeplitz_from_table(rel_bias[0]))
    (att,), full1 = _attn_fwd(qq, kp, vp, bias, ex=_gather_exchange(ffn_shards[1]))
    ffn_w1 = [a.reshape(F, D) for a in full1]
    x3 = _mm(att, wo_f, "nn", F32, "attn_out", add=x2)
    (dx4, loss_rows, hf1, gg1, uu1), _ = _ffn_fwd(x3, g_ffn1, *ffn_w1, target=target)

    def blocks(dw):
        return dw.reshape(N_DEV, dw.shape[0] // N_DEV, dw.shape[1])

    (dgg1, duu1, dwg1, dwu1, dwd1), _ = _ffn_bwd_weights(dx4, hf1, gg1, uu1, ffn_w1[2], name="ffn_bwd1")
    parts1 = [blocks(dw) for dw in (dwg1, dwu1, dwd1)]
    (dx3, dg_ffn1), stage1 = _ffn_bwd_input(dx4, x3, g_ffn1, dgg1, duu1, ffn_w1[0], ffn_w1[1], "ffn_dx1",
                                            ex=_pair_exchange(parts1))
    chip1 = _pair_add(parts1, stage1, "pair_add_ffn1")
    datt = _mm(dx3, wo_f, "nt", BF16, "d_attn")
    dwo = _mm(att, dx3, "tn", BF16, "d_wo")
    (dq, dkp, dvp, dband), recv1 = _attn_bwd(qq, kp, vp, bias, datt, ex=_chip_exchange(chip1))
    dtp, seg = _bias_grad(dband)
    dqpre, dgq = _head_norm_bwd(dq, qpre, gq_t, "q_norm_bwd")
    dkpre, dgk = _head_norm_bwd(dkp, kpre, gk_t, "k_norm_bwd", row_off=PADK // PROJ_TS)
    dh1 = _mm(dqpre, wq_f, "nt", F32, "d_h1")
    dwq = _mm(h1, dqpre, "tn", BF16, "d_wq")
    dhkv = _mm(dkpre, wk_f, "nt", F32, "d_hkv_k")
    dv = dvp[PADK:]
    dhkv = _mm(dv, wv_f, "nt", F32, "d_hkv_v", add=dhkv)
    dwk = _mm(hkv, dkpre, "tn", BF16, "d_wk")
    dwv = _mm(hkv, dv, "tn", BF16, "d_wv")
    parts_att = [blocks(dw) for dw in (dwk, dwv, dwq, dwo)]
    (dx2, dg_mix1, dg_kv), stage_att = _norm_bwd(dx3, x2, [dh1, dhkv], [g_mix1, g_kv], "norm_bwd_mix1",
                                                 ex=_pair_exchange(parts_att))
    chip_att = _pair_add(parts_att, stage_att, "pair_add_att")
    (dgg0, duu0, dwg0, dwu0, dwd0), recv_att = _ffn_bwd_weights(
        dx2, hf0, gg0, uu0, ffn_w0[2], name="ffn_bwd0", ex=_chip_exchange(chip_att))
    parts0 = [blocks(dw) for dw in (dwg0, dwu0, dwd0)]
    (dx1, dg_ffn0), stage0 = _ffn_bwd_input(dx2, x1, g_ffn0, dgg0, duu0, ffn_w0[0], ffn_w0[1], "ffn_dx0",
                                            ex=_pair_exchange(parts0))
    both = _pair_add(parts0, stage0, "pair_add_ffn0", copies=2)
    flight, chip0, land0, token = _chip_exchange_start(both[:3], both[3:], "scatter_ffn0_start")
    (grad_x, dpw, db_rows, ds_rows, dg_mix0), _ = _pool_bwd(dx1, xin, diff, g_mix0 + token[0:1, 0:1], pw_f, pb_f, ps_f)
    dpw_blocks = dpw.reshape(N_GROUPS, N_DEV, POOL_SHARD, GROUP).transpose(1, 0, 2, 3)
    dpw_blocks = dpw_blocks.reshape(N_DEV, N_GROUPS * POOL_SHARD, GROUP).astype(BF16)
    recv_pool = _run_exchange(_scatter_exchange([dpw_blocks]), "scatter_pool")


    misc = jnp.concatenate([dgk[0:1, 0:HEAD], dgq[0:1, 0:HEAD], loss_rows[0:1, 0:1],
                            seg[:, 0, 0].reshape(1, N_HEADS), seg[:, 0, 1].reshape(1, N_HEADS)], axis=1)
    misc = jnp.pad(misc, ((0, 0), (0, D - misc.shape[1])))
    vec_rows = jnp.concatenate([dg_mix0[0:1], dg_mix1[0:1], dg_ffn0[0:1], dg_ffn1[0:1], dg_kv[0:1],
                                db_rows[0:1], ds_rows[0:1], misc], axis=0)
    pack = jnp.concatenate([vec_rows, dtp.reshape(N_HEADS, TOEP)], axis=0)
    tot = _all_reduce_small(pack, "reduce_small")

    loss = tot[7, 2 * HEAD]
    g_norm_mix = tot[0:2]
    g_norm_ffn = tot[2:4]
    g_kv_norm = tot[4]
    g_k_norm = tot[7, 0:HEAD]
    g_q_norm = tot[7, HEAD:2 * HEAD].reshape(1, HEAD)
    seg_tot = jnp.stack([tot[7, 2 * HEAD + 1:2 * HEAD + 1 + N_HEADS],
                         tot[7, 2 * HEAD + 1 + N_HEADS:2 * HEAD + 1 + 2 * N_HEADS]], axis=1)
    g_rel = _table_grad_from_toeplitz(tot[8:8 + N_HEADS], seg_tot).reshape(1, N_HEADS, N_REL)
    me = 4 * lax.axis_index("x") + 2 * lax.axis_index("y") + lax.axis_index("c")
    g_pool_b = lax.dynamic_slice_in_dim(tot[5].reshape(N_GROUPS, GROUP), me * POOL_SHARD, POOL_SHARD, axis=1)
    g_pool_b = g_pool_b.reshape(1, N_GROUPS, POOL_SHARD)
    g_pool_scale = lax.dynamic_slice_in_dim(tot[6:7], me * D_SHARD, D_SHARD, axis=1)

    def summed(arrs, tag):
        return [_sum_parts(a, "sum_parts_%s%d" % (tag, k)) for k, a in enumerate(arrs)]

    sums1 = summed(recv1, "ffn1_")
    g_wk, g_wv, g_wq, g_wo = summed(recv_att, "att_")
    g_pool_w = summed(recv_pool, "pool_")[0].reshape(1, N_GROUPS, POOL_SHARD, GROUP)

    grads = dict(norm_mix_g=g_norm_mix, norm_ffn_g=g_norm_ffn, pool_w=g_pool_w, pool_b=g_pool_b,
                 pool_scale=g_pool_scale, kv_norm_g=g_kv_norm, w_k=g_wk, w_v=g_wv, k_norm_g=g_k_norm, w_q=g_wq[None],
                 q_norm_g=g_q_norm, rel_bias=g_rel, w_o=g_wo[None])
    weights = dict(norm_mix_g=norm_mix_g, norm_ffn_g=norm_ffn_g, pool_w=pool_w, pool_b=pool_b,
                   pool_scale=pool_scale, kv_norm_g=kv_norm_g, w_k=w_k, w_v=w_v, k_norm_g=k_norm_g, w_q=w_q,
                   q_norm_g=q_norm_g, rel_bias=rel_bias, w_o=w_o, w_gate=w_gate, w_up=w_up, w_down=w_down)
    mom1 = dict(norm_mix_g=m_norm_mix_g, norm_ffn_g=m_norm_ffn_g, pool_w=m_pool_w, pool_b=m_pool_b,
                pool_scale=m_pool_scale, kv_norm_g=m_kv_norm_g, w_k=m_w_k, w_v=m_w_v, k_norm_g=m_k_norm_g,
                w_q=m_w_q, q_norm_g=m_q_norm_g, rel_bias=m_rel_bias, w_o=m_w_o, w_gate=m_w_gate, w_up=m_w_up,
                w_down=m_w_down)
    mom2 = dict(norm_mix_g=v_norm_mix_g, norm_ffn_g=v_norm_ffn_g, pool_w=v_pool_w, pool_b=v_pool_b,
                pool_scale=v_pool_scale, kv_norm_g=v_kv_norm_g, w_k=v_w_k, w_v=v_w_v, k_norm_g=v_k_norm_g,
                w_q=v_w_q, q_norm_g=v_q_norm_g, rel_bias=v_rel_bias, w_o=v_w_o, w_gate=v_w_gate, w_up=v_w_up,
                w_down=v_w_down)
    names = list(weights)

    ffn_names = ("w_gate", "w_up", "w_down")
    layer1 = {nm: _adam_layer(weights[nm], g, mom1[nm], mom2[nm], 1, "adam1_" + nm)
              for nm, g in zip(ffn_names, (sums1[0].T, sums1[1].T, sums1[2]))}
    large = ("w_k", "w_v", "w_q", "w_o", "pool_w")
    deltas, new_m, new_v = {}, {}, {}
    for nm in large:
        shape = weights[nm].shape
        flat = lambda a: a.reshape(-1, shape[-1])
        dl, m1, m2 = _adam(flat(weights[nm]), flat(grads[nm]), flat(mom1[nm]), flat(mom2[nm]), "adam_" + nm)
        deltas[nm], new_m[nm], new_v[nm] = dl.reshape(shape), m1.reshape(shape), m2.reshape(shape)
    small_names = [nm for nm in names if nm not in large + ffn_names]

    def pack_small(tree):
        cols = []
        for nm in small_names:
            flat = tree[nm].reshape(-1)
            cols.append(jnp.pad(flat, (0, -flat.shape[0] % 1024)))
        return jnp.concatenate(cols).reshape(-1, 128)

    dl, m1, m2 = _adam(pack_small(weights), pack_small(grads), pack_small(mom1), pack_small(mom2), "adam_small")

    def unpack_small(packed, out):
        flat, off = packed.reshape(-1), 0
        for nm in small_names:
            size = weights[nm].size
            out[nm] = flat[off:off + size].reshape(weights[nm].shape)
            off += size + (-size % 1024)

    unpack_small(dl, deltas)
    unpack_small(m1, new_m)
    unpack_small(m2, new_v)

    recv0 = _chip_exchange_wait(flight, chip0, land0, dl, "scatter_ffn0_wait")
    sums0 = summed(recv0, "ffn0_")
    for nm, g in zip(ffn_names, (sums0[0].T, sums0[1].T, sums0[2])):
        grads[nm], deltas[nm], new_m[nm], new_v[nm] = _adam_layer(
            weights[nm], g, mom1[nm], mom2[nm], 0, "adam0_" + nm, prev=layer1[nm])

    return (loss, grad_x[None], *[grads[nm] for nm in names], *[deltas[nm] for nm in names],
            *[new_m[nm] for nm in names], *[new_v[nm] for nm in names])
```

```python
import functools

import jax
import jax.numpy as jnp
from jax import lax
from jax.experimental import pallas as pl
from jax.experimental.pallas import tpu as pltpu

F32 = jnp.float32
BF16 = jnp.bfloat16
MESH_ID = pl.DeviceIdType.MESH

N_DEV = 8
S = 2048
D = 1024
F = 2816
F_SHARD = F // N_DEV
D_SHARD = D // N_DEV
N_GROUPS = 4
GROUP = D // N_GROUPS
POOL_SHARD = GROUP // N_DEV
MAX_WIN = 16
HEAD = 64
N_HEADS = D // HEAD
CHUNK = 64
LEFT = 8
QB = 4 * CHUNK
KB = QB + LEFT * CHUNK
PADK = LEFT * CHUNK
TOEP = 1024
N_REL = 257
MAX_REL = 128
EPS = 1e-6
NEG_INF = -1e30
ATTN_SCALE = HEAD ** -0.5

ADAM_LR = 0.001
ADAM_B1 = 0.9
ADAM_B2 = 0.999
ADAM_EPS = 1e-08
ADAM_WD = 0.01
ADAM_STEP = 10

VMEM_LIMIT = 52 * 1024 * 1024

ANY = pl.BlockSpec(memory_space=pl.ANY)
VMEM = pl.BlockSpec(memory_space=pltpu.VMEM)


def _call(body, **kw):
    return pl.pallas_call(body, **kw)


def _params(*sem):
    return pltpu.CompilerParams(dimension_semantics=sem, vmem_limit_bytes=VMEM_LIMIT)


def _dot(a, b, dims):
    return lax.dot_general(a, b, (dims, ((), ())), preferred_element_type=F32)


def _nn(a, b):
    return _dot(a, b, ((1,), (0,)))


def _nt(a, b):
    return _dot(a, b, ((1,), (1,)))


def _tn(a, b):
    return _dot(a, b, ((0,), (0,)))


def _rstd(x):
    return lax.rsqrt(jnp.mean(x * x, axis=-1, keepdims=True) + EPS)


def _rms_bwd(dh, x, r, g):
    gd = dh * g
    return r * gd - x * (r * r * r) * jnp.mean(gd * x, axis=-1, keepdims=True)


def _colsum8(v):
    return jnp.broadcast_to(jnp.sum(v, axis=0, keepdims=True), (8, v.shape[1]))


def _seg_sum(v):
    r = lax.broadcasted_iota(jnp.int32, (128, 128), 0) // HEAD
    c = lax.broadcasted_iota(jnp.int32, (128, 128), 1) // HEAD
    ones = jnp.where(r == c, 1.0, 0.0).astype(BF16)
    out = []
    for blk in range(v.shape[1] // 128):
        part = v[:, blk * 128:(blk + 1) * 128]
        hi = part.astype(BF16)
        rest = part - hi.astype(F32)
        mid = rest.astype(BF16)
        lo = (rest - mid.astype(F32)).astype(BF16)
        out.append(_nn(hi, ones) + _nn(mid, ones) + _nn(lo, ones))
    return jnp.concatenate(out, axis=1)


def _place():
    return lax.axis_index("x"), lax.axis_index("y"), lax.axis_index("c")


class _Exchange:
    def __init__(self, ins, out_shape, sems, start, mid, finish):
        self.ins, self.out_shape, self.sems = list(ins), list(out_shape), list(sems)
        self.start, self.mid, self.finish = start, mid, finish


def _gather_exchange(shards):
    n = len(shards)

    def tools(ins, outs, sems):
        send_sems, recv_sems, local_sems = sems
        x, y, c = _place()
        me, sibling = (x, y, c), (x, y, 1 - c)
        chips = [(1 - x, y), (x, 1 - y), (1 - x, 1 - y)]

        def slot(k, px, py, pc):
            return outs[k].at[4 * px + 2 * py + pc]

        def copy(k, s, block, to, own=False):
            return pltpu.make_async_remote_copy(
                src_ref=ins[k] if own else slot(k, *block), dst_ref=slot(k, *block),
                send_sem=send_sems.at[k, s], recv_sem=recv_sems.at[k, s], device_id=to, device_id_type=MESH_ID)

        def mine():
            return [pltpu.make_async_copy(ins[k], slot(k, *me), local_sems.at[k]) for k in range(n)]

        def first():
            out = []
            for k in range(n):
                out.append(copy(k, 0, me, sibling, own=True))
                out += [copy(k, 1 + j, me, (*chip, c), own=True) for j, chip in enumerate(chips)]
            return out

        def landed(j):
            return [copy(k, 1 + j, (*chips[j], c), me) for k in range(n)]

        def passed(j):
            return [copy(k, 4 + j, (*chips[j], c), sibling) for k in range(n)]

        def from_sibling():
            out = []
            for k in range(n):
                out.append(copy(k, 0, sibling, me))
                out += [copy(k, 4 + j, (*chip, 1 - c), me) for j, chip in enumerate(chips)]
            return out

        return mine, first, landed, passed, from_sibling

    def start(ins, outs, sems):
        mine, first, _, _, _ = tools(ins, outs, sems)
        for cp in mine() + first():
            cp.start()

    def mid(ins, outs, sems):
        _, _, landed, passed, _ = tools(ins, outs, sems)
        for j in range(3):
            for arrived, onward in zip(landed(j), passed(j)):
                arrived.wait_recv()
                onward.start()

    def finish(ins, outs, sems):
        mine, first, _, passed, from_sibling = tools(ins, outs, sems)
        for cp in from_sibling():
            cp.wait_recv()
        for cp in first() + passed(0) + passed(1) + passed(2):
            cp.wait_send()
        for cp in mine():
            cp.wait()

    return _Exchange(
        shards, [jax.ShapeDtypeStruct((N_DEV,) + a.shape, a.dtype) for a in shards],
        [pltpu.SemaphoreType.DMA((n, 7)), pltpu.SemaphoreType.DMA((n, 7)), pltpu.SemaphoreType.DMA((n,))],
        start, mid, finish)


def _peer(x, y, c, m):
    px = 1 - x if m & 4 else x
    py = 1 - y if m & 2 else y
    pc = 1 - c if m & 1 else c
    return px, py, pc


def _scatter_exchange(parts):
    n = len(parts)

    def tools(ins, outs, sems):
        send_sems, recv_sems, local_sems = sems
        x, y, c = _place()
        me = 4 * x + 2 * y + c
        def mine():
            return [pltpu.make_async_copy(ins[k].at[me], outs[k].at[me], local_sems.at[k]) for k in range(n)]

        def remote(dst_is_mine):
            out = []
            for m in range(1, N_DEV):
                px, py, pc = _peer(x, y, c, m)
                peer = 4 * px + 2 * py + pc
                for k in range(n):
                    out.append(pltpu.make_async_remote_copy(
                        src_ref=ins[k].at[peer], dst_ref=outs[k].at[me if dst_is_mine else peer],
                        send_sem=send_sems.at[k, m - 1], recv_sem=recv_sems.at[k, m - 1],
                        device_id=(px, py, pc), device_id_type=MESH_ID))
            return out

        return mine, remote

    def start(ins, outs, sems):
        mine, remote = tools(ins, outs, sems)
        for cp in mine() + remote(True):
            cp.start()

    def mid(ins, outs, sems):
        pass

    def finish(ins, outs, sems):
        mine, remote = tools(ins, outs, sems)
        for cp in remote(False):
            cp.wait_recv()
        for cp in remote(True):
            cp.wait_send()
        for cp in mine():
            cp.wait()

    return _Exchange(
        parts, [jax.ShapeDtypeStruct(a.shape, a.dtype) for a in parts],
        [pltpu.SemaphoreType.DMA((n, 7)), pltpu.SemaphoreType.DMA((n, 7)), pltpu.SemaphoreType.DMA((n,))],
        start, mid, finish)


N_CHIPS = N_DEV // 2


def _pair_exchange(parts):
    n = len(parts)

    def copies(ins, outs, sems):
        send_sems, recv_sems = sems
        x, y, c = _place()
        return [pltpu.make_async_remote_copy(
            src_ref=ins[k].at[2 * q + 1 - c], dst_ref=outs[k].at[q], send_sem=send_sems.at[k, q],
            recv_sem=recv_sems.at[k, q], device_id=(x, y, 1 - c), device_id_type=MESH_ID)
            for k in range(n) for q in range(N_CHIPS)]

    def start(ins, outs, sems):
        for cp in copies(ins, outs, sems):
            cp.start()

    def mid(ins, outs, sems):
        pass

    def finish(ins, outs, sems):
        for cp in copies(ins, outs, sems):
            cp.wait_recv()
        for cp in copies(ins, outs, sems):
            cp.wait_send()

    return _Exchange(
        parts, [jax.ShapeDtypeStruct((N_CHIPS,) + a.shape[1:], a.dtype) for a in parts],
        [pltpu.SemaphoreType.DMA((n, N_CHIPS)), pltpu.SemaphoreType.DMA((n, N_CHIPS))], start, mid, finish)


def _pair_add(parts, stage, name, copies=1):
    n = len(parts)
    core = lax.axis_index("c").reshape(1)

    def body(core_ref, *refs):
        for k in range(n):
            mine, theirs = refs[k], refs[n + k]
            total = (mine[0, 0].astype(F32) + theirs[0].astype(F32)).astype(BF16)
            for rep in range(copies):
                refs[(2 + rep) * n + k][0] = total

    in_specs, out_specs = [], []
    for a in parts:
        _, r, cdim = a.shape
        in_specs.append(pl.BlockSpec((1, 1, r // 2, cdim), lambda q, i, core_ref: (q, core_ref[0], i, 0)))
    for a in parts:
        _, r, cdim = a.shape
        in_specs.append(pl.BlockSpec((1, r // 2, cdim), lambda q, i, core_ref: (q, i, 0)))
        out_specs.append(pl.BlockSpec((1, r // 2, cdim), lambda q, i, core_ref: (q, i, 0)))
    return list(_call(
        body, name=name,
        grid_spec=pltpu.PrefetchScalarGridSpec(num_scalar_prefetch=1, grid=(N_CHIPS, 2), in_specs=in_specs,
                                               out_specs=out_specs * copies),
        out_shape=[jax.ShapeDtypeStruct(s.shape, BF16) for s in stage] * copies,
        compiler_params=_params("arbitrary", "arbitrary"),
    )(core, *[a.reshape((N_CHIPS, 2) + a.shape[1:]) for a in parts], *stage))


HBM = pl.BlockSpec(memory_space=pltpu.HBM)
SEMAPHORES = pl.BlockSpec(memory_space=pltpu.SEMAPHORE)


def _chip_copies(srcs, lands, send_sems, recv_sems, mine_is_dst):
    x, y, c = _place()
    me = 2 * x + y
    out = []
    for m in range(1, N_CHIPS):
        px, py, _ = _peer(x, y, c, 2 * m)
        peer = 2 * px + py
        for k in range(len(srcs)):
            pair = k * (N_CHIPS - 1) + m - 1
            out.append(pltpu.make_async_remote_copy(
                src_ref=srcs[k].at[peer], dst_ref=lands[k].at[me if mine_is_dst else peer],
                send_sem=send_sems[pair], recv_sem=recv_sems[pair],
                device_id=(px, py, c), device_id_type=MESH_ID))
    return out


def _chip_exchange_start(chip_parts, lands, after, name):
    n = len(chip_parts)
    pairs = n * (N_CHIPS - 1)

    def body(*refs):
        srcs, zones = refs[:n], refs[n:2 * n]
        sems = refs[2 * n + 1:2 * n + 1 + 2 * pairs]
        token = refs[-1]
        for cp in _chip_copies(srcs, zones, sems[:pairs], sems[pairs:], True):
            cp.start()
        token[...] = jnp.zeros(token.shape, F32)

    thru = [pltpu.HBM(a.shape, a.dtype) for a in chip_parts + lands]
    hbm = [pltpu.with_memory_space_constraint(a, pltpu.HBM) for a in chip_parts + lands]
    res = _call(
        body, name=name,
        out_shape=[pltpu.SemaphoreType.DMA(())] * (2 * pairs) + thru + [jax.ShapeDtypeStruct((8, 128), F32)],
        in_specs=[HBM] * (2 * n) + [ANY], out_specs=[SEMAPHORES] * (2 * pairs) + [HBM] * (2 * n) + [VMEM],
        input_output_aliases={i: 2 * pairs + i for i in range(2 * n)},
        compiler_params=pltpu.CompilerParams(has_side_effects=pltpu.SideEffectType.DATAFLOW_SIDE_EFFECTING),
    )(*hbm, after)
    sems, rest = list(res[:2 * pairs]), res[2 * pairs:]
    return sems, list(rest[:n]), list(rest[n:2 * n]), rest[-1]


def _chip_exchange_wait(sems, chip_parts, lands, after, name):
    n = len(chip_parts)
    pairs = n * (N_CHIPS - 1)

    def body(*refs):
        srcs, zones = refs[:n], refs[n:2 * n]
        sem_refs = refs[2 * n:2 * n + 2 * pairs]
        for cp in _chip_copies(srcs, zones, sem_refs[:pairs], sem_refs[pairs:], False):
            cp.wait_send()
            cp.wait_recv()

    thru = [pltpu.HBM(a.shape, a.dtype) for a in chip_parts + lands]
    res = _call(
        body, name=name, out_shape=thru,
        in_specs=[HBM] * (2 * n) + [SEMAPHORES] * (2 * pairs) + [ANY], out_specs=[HBM] * (2 * n),
        input_output_aliases={i: i for i in range(2 * n)},
        compiler_params=pltpu.CompilerParams(has_side_effects=pltpu.SideEffectType.DATAFLOW_SIDE_EFFECTING),
    )(*chip_parts, *lands, *sems, after)
    return list(res[n:])


def _chip_exchange(chip_parts):
    n = len(chip_parts)

    def tools(ins, outs, sems):
        send_sems, recv_sems, local_sems = sems
        x, y, c = _place()
        me = 2 * x + y

        def mine():
            return [pltpu.make_async_copy(ins[k].at[me], outs[k].at[me], local_sems.at[k]) for k in range(n)]

        def remote(dst_is_mine):
            out = []
            for m in range(1, N_CHIPS):
                px, py, _ = _peer(x, y, c, 2 * m)
                peer = 2 * px + py
                for k in range(n):
                    out.append(pltpu.make_async_remote_copy(
                        src_ref=ins[k].at[peer], dst_ref=outs[k].at[me if dst_is_mine else peer],
                        send_sem=send_sems.at[k, m - 1], recv_sem=recv_sems.at[k, m - 1],
                        device_id=(px, py, c), device_id_type=MESH_ID))
            return out

        return mine, remote

    def start(ins, outs, sems):
        mine, remote = tools(ins, outs, sems)
        for cp in mine() + remote(True):
            cp.start()

    def mid(ins, outs, sems):
        pass

    def finish(ins, outs, sems):
        mine, remote = tools(ins, outs, sems)
        for cp in remote(False):
            cp.wait_recv()
        for cp in remote(True):
            cp.wait_send()
        for cp in mine():
            cp.wait()

    return _Exchange(
        chip_parts, [jax.ShapeDtypeStruct(a.shape, a.dtype) for a in chip_parts],
        [pltpu.SemaphoreType.DMA((n, N_CHIPS - 1)), pltpu.SemaphoreType.DMA((n, N_CHIPS - 1)),
         pltpu.SemaphoreType.DMA((n,))], start, mid, finish)


def _run_exchange(ex, name):
    n_in, n_out = len(ex.ins), len(ex.out_shape)

    def body(*refs):
        ins, outs, sems = refs[:n_in], refs[n_in:n_in + n_out], refs[n_in + n_out:]
        ex.start(ins, outs, sems)
        ex.mid(ins, outs, sems)
        ex.finish(ins, outs, sems)

    return list(_call(body, name=name, out_shape=ex.out_shape, in_specs=[ANY] * n_in, out_specs=[ANY] * n_out,
                      scratch_shapes=ex.sems)(*ex.ins))


def _call_hosting(body, ex, phases, *, in_specs, out_specs, out_shape, scratch_shapes, args, **kw):
    n_in, n_out, n_scr = len(in_specs), len(out_specs), len(scratch_shapes)
    if ex is None:
        res = _call(body, in_specs=in_specs, out_specs=out_specs, out_shape=out_shape,
                    scratch_shapes=scratch_shapes, **kw)(*args)
        return list(res), []
    n_xin, n_xout = len(ex.ins), len(ex.out_shape)

    def hosting(*refs):
        a, b = n_in, n_in + n_xin
        c, d = b + n_out, b + n_out + n_xout
        ins, xins, outs, xouts = refs[:a], refs[a:b], refs[b:c], refs[c:d]
        scr, sems = refs[d:d + n_scr], refs[d + n_scr:]
        first, mid, last = phases()

        @pl.when(first)
        def _():
            ex.start(xins, xouts, sems)

        body(*ins, *outs, *scr)

        @pl.when(mid)
        def _():
            ex.mid(xins, xouts, sems)

        @pl.when(last)
        def _():
            ex.finish(xins, xouts, sems)

    res = _call(hosting, in_specs=list(in_specs) + [ANY] * n_xin, out_specs=list(out_specs) + [ANY] * n_xout,
                out_shape=list(out_shape) + ex.out_shape, scratch_shapes=list(scratch_shapes) + ex.sems,
                **kw)(*args, *ex.ins)
    return list(res[:n_out]), list(res[n_out:])


def _grid_phases(dims, mid_fraction=0.8):
    total = 1
    for d in dims:
        total *= d
    mid = min(max(int(total * mid_fraction), 1), total - 1)

    def phases():
        step = pl.program_id(0)
        for axis in range(1, len(dims)):
            step = step * dims[axis] + pl.program_id(axis)
        return step == 0, step == mid, step == total - 1
    return phases


def _all_reduce_small(pack, name):
    rows = pack.shape[0]

    def body(in_ref, out_ref, recv, send_sems, recv_sems):
        x, y, c = _place()
        me = 4 * x + 2 * y + c
        recv[me] = in_ref[...]
        sent = []
        for m in range(1, N_DEV):
            px, py, pc = _peer(x, y, c, m)
            cp = pltpu.make_async_remote_copy(
                src_ref=in_ref, dst_ref=recv.at[me], send_sem=send_sems.at[m - 1], recv_sem=recv_sems.at[m - 1],
                device_id=(px, py, pc), device_id_type=MESH_ID)
            cp.start()
            sent.append(cp)
        for m in range(1, N_DEV):
            px, py, pc = _peer(x, y, c, m)
            peer = 4 * px + 2 * py + pc
            pltpu.make_async_remote_copy(
                src_ref=in_ref, dst_ref=recv.at[peer], send_sem=send_sems.at[m - 1], recv_sem=recv_sems.at[m - 1],
                device_id=(px, py, pc), device_id_type=MESH_ID).wait_recv()
        acc = recv[0]
        for d in range(1, N_DEV):
            acc = acc + recv[d]
        out_ref[...] = acc
        for cp in sent:
            cp.wait_send()

    return _call(
        body, name=name, out_shape=jax.ShapeDtypeStruct(pack.shape, F32), in_specs=[VMEM], out_specs=VMEM,
        scratch_shapes=[pltpu.VMEM((N_DEV, rows, pack.shape[1]), F32), pltpu.SemaphoreType.DMA((7,)),
                        pltpu.SemaphoreType.DMA((7,))],
        compiler_params=pltpu.CompilerParams(vmem_limit_bytes=VMEM_LIMIT),
    )(pack)


POOL_TS = 256


def _pool_counts(first_row, rows, win):
    t = first_row + lax.broadcasted_iota(jnp.int32, (rows, 1), 0)
    return jnp.minimum(t + 1, win).astype(F32)


def _pool_fwd(x, g, w, b, scale):
    nt = S // POOL_TS

    def body(x_ref, g_ref, w_ref, b_ref, s_ref, out_ref, diff_ref, ext):
        i = pl.program_id(0)

        @pl.when(i == 0)
        def _():
            ext[0:MAX_WIN, :] = jnp.zeros((MAX_WIN, D), F32)

        @pl.when(i > 0)
        def _():
            ext[0:MAX_WIN, :] = ext[POOL_TS:POOL_TS + MAX_WIN, :]

        xv = x_ref[...]
        h = xv * _rstd(xv) * g_ref[...]
        ext[MAX_WIN:, :] = h
        for gi in range(N_GROUPS):
            win = 2 << gi
            cols = slice(gi * GROUP, (gi + 1) * GROUP)
            sm = ext[:, cols]
            k = 1
            while k < win:
                sm = sm + pltpu.roll(sm, k, axis=0)
                k *= 2
            pooled = sm[MAX_WIN:, :] / _pool_counts(i * POOL_TS, POOL_TS, win)
            diff = (pooled - h[:, cols]).astype(BF16)
            yv = (_nn(diff, w_ref[gi]) + b_ref[:, cols]) * s_ref[:, cols]
            out_ref[:, cols] = xv[:, cols] + yv
            diff_ref[:, cols] = diff

    row = pl.BlockSpec((1, D), lambda i: (0, 0))
    tile = pl.BlockSpec((POOL_TS, D), lambda i: (i, 0))
    return _call(
        body, name="pool_fwd", grid=(nt,),
        in_specs=[tile, row, pl.BlockSpec((N_GROUPS, GROUP, GROUP), lambda i: (0, 0, 0)), row, row],
        out_specs=[tile, tile],
        out_shape=[jax.ShapeDtypeStruct((S, D), F32), jax.ShapeDtypeStruct((S, D), BF16)],
        scratch_shapes=[pltpu.VMEM((POOL_TS + MAX_WIN, D), F32)],
        compiler_params=_params("arbitrary"),
    )(x, g, w, b, scale)


def _pool_bwd(dy, x, diff, g, w, b, scale, ex=None):
    nt = S // POOL_TS

    def body(dy_ref, x_ref, diff_ref, g_ref, w_ref, b_ref, s_ref, gx_ref, dw_ref, db_ref, ds_ref, dg_ref, ext, dh):
        i = pl.program_id(0)
        first_row = (nt - 1 - i) * POOL_TS

        @pl.when(i == 0)
        def _():
            ext[POOL_TS:, :] = jnp.zeros((MAX_WIN, D), F32)
            dw_ref[...] = jnp.zeros(dw_ref.shape, F32)
            db_ref[...] = jnp.zeros(db_ref.shape, F32)
            ds_ref[...] = jnp.zeros(ds_ref.shape, F32)
            dg_ref[...] = jnp.zeros(dg_ref.shape, F32)

        @pl.when(i > 0)
        def _():
            ext[POOL_TS:, :] = ext[0:MAX_WIN, :]

        dyv = dy_ref[...]
        for gi in range(N_GROUPS):
            win = 2 << gi
            cols = slice(gi * GROUP, (gi + 1) * GROUP)
            dfb = diff_ref[:, cols]
            z = _nn(dfb, w_ref[gi]) + b_ref[:, cols]
            dyg = dyv[:, cols]
            ds_ref[:, cols] += _colsum8(dyg * z)
            dz = dyg * s_ref[:, cols]
            db_ref[:, cols] += _colsum8(dz)
            dzb = dz.astype(BF16)
            dw_ref[gi] += _tn(dfb, dzb)
            ddiff = _nt(dzb, w_ref[gi])
            ext[0:POOL_TS, cols] = ddiff / _pool_counts(first_row, POOL_TS, win)
            sm = ext[:, cols]
            k = 1
            while k < win:
                sm = sm + pltpu.roll(sm, POOL_TS + MAX_WIN - k, axis=0)
                k *= 2
            dh[:, cols] = sm[0:POOL_TS, :] - ddiff
        xv = x_ref[...]
        r = _rstd(xv)
        gv = g_ref[...]
        dhv = dh[...]
        dg_ref[...] += _colsum8(dhv * xv * r)
        gx_ref[...] = dyv + _rms_bwd(dhv, xv, r, gv)

    row = pl.BlockSpec((1, D), lambda i: (0, 0))
    tile = pl.BlockSpec((POOL_TS, D), lambda i: (nt - 1 - i, 0))
    acc = pl.BlockSpec((8, D), lambda i: (0, 0))
    wspec = pl.BlockSpec((N_GROUPS, GROUP, GROUP), lambda i: (0, 0, 0))
    return _call_hosting(
        body, ex, _grid_phases((nt,)), name="pool_bwd", grid=(nt,),
        in_specs=[tile, tile, tile, row, wspec, row, row],
        out_specs=[tile, wspec, acc, acc, acc],
        out_shape=[jax.ShapeDtypeStruct((S, D), F32), jax.ShapeDtypeStruct((N_GROUPS, GROUP, GROUP), F32),
                   jax.ShapeDtypeStruct((8, D), F32), jax.ShapeDtypeStruct((8, D), F32),
                   jax.ShapeDtypeStruct((8, D), F32)],
        scratch_shapes=[pltpu.VMEM((POOL_TS + MAX_WIN, D), F32), pltpu.VMEM((POOL_TS, D), F32)],
        args=(dy, x, diff, g, w, b, scale), compiler_params=_params("arbitrary"),
    )


FFN_TS = min(S, 1024)
FFN_TF = 256


def _ffn_fwd(x, g, wg_t, wu_t, wd, target=None, ex=None):
    ni, nj = S // FFN_TS, F // FFN_TF
    with_loss = target is not None

    def body(*refs):
        if with_loss:
            x_ref, g_ref, wg_ref, wu_ref, wd_ref, t_ref, out_ref, loss_ref, h_ref, gg_ref, uu_ref, hs, acc = refs
        else:
            x_ref, g_ref, wg_ref, wu_ref, wd_ref, out_ref, h_ref, gg_ref, uu_ref, hs, acc = refs
        i, j = pl.program_id(0), pl.program_id(1)

        @pl.when(j == 0)
        def _():
            xv = x_ref[...]
            hb = (xv * _rstd(xv) * g_ref[...]).astype(BF16)
            hs[...] = hb
            h_ref[...] = hb
            acc[...] = jnp.zeros(acc.shape, F32)

        hb = hs[...]
        gg = _nt(hb, wg_ref[...])
        uu = _nt(hb, wu_ref[...])
        gg_ref[...] = gg
        uu_ref[...] = uu
        a = (gg * jax.nn.sigmoid(gg) * uu).astype(BF16)
        acc[...] += _nn(a, wd_ref[...])

        @pl.when(j == nj - 1)
        def _():
            yv = x_ref[...] + acc[...]
            if with_loss:
                err = yv - t_ref[...]
                out_ref[...] = err * (1.0 / D)
                part = jnp.sum(err * err) * (0.5 / D)

                @pl.when(i == 0)
                def _():
                    loss_ref[...] = jnp.zeros(loss_ref.shape, F32)

                loss_ref[...] += jnp.broadcast_to(part, loss_ref.shape)
            else:
                out_ref[...] = yv

    xt = pl.BlockSpec((FFN_TS, D), lambda i, j: (i, 0))
    row = pl.BlockSpec((1, D), lambda i, j: (0, 0))
    wt = pl.BlockSpec((FFN_TF, D), lambda i, j: (j, 0))
    gt = pl.BlockSpec((FFN_TS, FFN_TF), lambda i, j: (i, j))
    in_specs = [xt, row, wt, wt, wt] + ([xt] if with_loss else [])
    out_specs = [xt] + ([pl.BlockSpec((8, 128), lambda i, j: (0, 0))] if with_loss else []) + [xt, gt, gt]
    out_shape = ([jax.ShapeDtypeStruct((S, D), F32)] + ([jax.ShapeDtypeStruct((8, 128), F32)] if with_loss else [])
                 + [jax.ShapeDtypeStruct((S, D), BF16), jax.ShapeDtypeStruct((S, F), F32),
                    jax.ShapeDtypeStruct((S, F), F32)])
    args = (x, g, wg_t, wu_t, wd) + ((target,) if with_loss else ())
    return _call_hosting(
        body, ex, _grid_phases((ni, nj), 0.8), name="ffn_fwd_loss" if with_loss else "ffn_fwd", grid=(ni, nj),
        in_specs=in_specs, out_specs=out_specs, out_shape=out_shape,
        scratch_shapes=[pltpu.VMEM((FFN_TS, D), BF16), pltpu.VMEM((FFN_TS, D), F32)], args=args,
        compiler_params=_params("arbitrary", "arbitrary"),
    )


def _ffn_bwd_weights(dout, h, gg, uu, wd, name, ex=None):
    nj = F // FFN_TF

    def body(do_ref, h_ref, gg_ref, uu_ref, wd_ref, dg_ref, du_ref, dwg_ref, dwu_ref, dwd_ref, dob):
        @pl.when(pl.program_id(0) == 0)
        def _():
            dob[...] = do_ref[...].astype(BF16)

        hb, dov = h_ref[...], dob[...]
        gv, uv = gg_ref[...], uu_ref[...]
        da = _nt(dov, wd_ref[...])
        sg = jax.nn.sigmoid(gv)
        sl = gv * sg
        ab = (sl * uv).astype(BF16)
        dub = (da * sl).astype(BF16)
        dgb = (da * uv * (sg * (1.0 + gv * (1.0 - sg)))).astype(BF16)
        dg_ref[...] = dgb
        du_ref[...] = dub
        dwg_ref[...] = _tn(dgb, hb).astype(BF16)
        dwu_ref[...] = _tn(dub, hb).astype(BF16)
        dwd_ref[...] = _tn(ab, dov).astype(BF16)

    once = pl.Buffered(1)
    whole = lambda: pl.BlockSpec((S, D), lambda j, _: (0, 0), pipeline_mode=once)
    wt = pl.BlockSpec((FFN_TF, D), lambda j, _: (j, 0))
    gt = pl.BlockSpec((S, FFN_TF), lambda j, _: (0, j))
    return _call_hosting(
        body, ex, _grid_phases((nj, 1)), name=name, grid=(nj, 1),
        in_specs=[whole(), whole(), gt, gt, wt], out_specs=[gt, gt, wt, wt, wt],
        out_shape=[jax.ShapeDtypeStruct((S, F), BF16)] * 2 + [jax.ShapeDtypeStruct((F, D), BF16)] * 3,
        scratch_shapes=[pltpu.VMEM((S, D), BF16)], args=(dout, h, gg, uu, wd),
        compiler_params=_params("arbitrary", "arbitrary"),
    )


BWD_TS = 512


def _ffn_bwd_input(dres, x, g, dg, du, wg_t, wu_t, name, ex=None):
    nt = S // BWD_TS

    def body(dres_ref, x_ref, g_ref, dg_ref, du_ref, wg_ref, wu_ref, dx_ref, dgam_ref):
        dh = _nn(dg_ref[...], wg_ref[...]) + _nn(du_ref[...], wu_ref[...])
        xv = x_ref[...]
        r = _rstd(xv)
        dx_ref[...] = dres_ref[...] + _rms_bwd(dh, xv, r, g_ref[...])

        @pl.when(pl.program_id(0) == 0)
        def _():
            dgam_ref[...] = jnp.zeros(dgam_ref.shape, F32)

        dgam_ref[...] += _colsum8(dh * xv * r)

    tile = pl.BlockSpec((BWD_TS, D), lambda i: (i, 0))
    ftile = pl.BlockSpec((BWD_TS, F), lambda i: (i, 0))
    wspec = lambda: pl.BlockSpec((F, D), lambda i: (0, 0), pipeline_mode=pl.Buffered(1))
    return _call_hosting(
        body, ex, _grid_phases((nt,)), name=name, grid=(nt,),
        in_specs=[tile, tile, pl.BlockSpec((1, D), lambda i: (0, 0)), ftile, ftile, wspec(), wspec()],
        out_specs=[tile, pl.BlockSpec((8, D), lambda i: (0, 0))],
        out_shape=[jax.ShapeDtypeStruct((S, D), F32), jax.ShapeDtypeStruct((8, D), F32)],
        scratch_shapes=[], args=(dres, x, g, dg, du, wg_t, wu_t), compiler_params=_params("arbitrary"),
    )


NORM_TS = 256


def _norm_bwd(dres, x, dhs, gs, name, ex=None):
    n = len(dhs)
    nt = S // NORM_TS

    def body(*refs):
        dres_ref, x_ref = refs[:2]
        dh_refs, g_refs = refs[2:2 + n], refs[2 + n:2 + 2 * n]
        dx_ref, dg_refs = refs[2 + 2 * n], refs[3 + 2 * n:]
        i = pl.program_id(0)
        xv = x_ref[...]
        r = _rstd(xv)
        dx = dres_ref[...]
        for k in range(n):
            dhv = dh_refs[k][...]
            dx = dx + _rms_bwd(dhv, xv, r, g_refs[k][...])

            @pl.when(i == 0)
            def _():
                dg_refs[k][...] = jnp.zeros((8, D), F32)

            dg_refs[k][...] += _colsum8(dhv * xv * r)
        dx_ref[...] = dx

    tile = pl.BlockSpec((NORM_TS, D), lambda i: (i, 0))
    row = pl.BlockSpec((1, D), lambda i: (0, 0))
    acc = pl.BlockSpec((8, D), lambda i: (0, 0))
    return _call_hosting(
        body, ex, _grid_phases((nt,)), name=name, grid=(nt,),
        in_specs=[tile, tile] + [tile] * n + [row] * n,
        out_specs=[tile] + [acc] * n,
        out_shape=[jax.ShapeDtypeStruct((S, D), F32)] + [jax.ShapeDtypeStruct((8, D), F32)] * n,
        scratch_shapes=[], args=(dres, x, *dhs, *gs), compiler_params=_params("arbitrary"),
    )


def _mm(a, b, mode, out_dtype, name, add=None):
    if mode == "nn":
        (m, kd), n = a.shape, b.shape[1]
    elif mode == "nt":
        (m, kd), n = a.shape, b.shape[0]
    else:
        (kd, m), n = a.shape, b.shape[1]
    tm, tn, tk = min(m, 1024), min(n, 1024), min(kd, 1024)
    nk = kd // tk
    dot = {"nn": _nn, "nt": _nt, "tn": _tn}[mode]

    def body(*refs):
        if add is None:
            a_ref, b_ref, o_ref, acc = refs
        else:
            a_ref, b_ref, add_ref, o_ref, acc = refs
        k = pl.program_id(2)

        @pl.when(k == 0)
        def _():
            acc[...] = jnp.zeros(acc.shape, F32)

        acc[...] += dot(a_ref[...].astype(BF16), b_ref[...].astype(BF16))

        @pl.when(k == nk - 1)
        def _():
            res = acc[...]
            if add is not None:
                res = res + add_ref[...]
            o_ref[...] = res.astype(out_dtype)

    if mode == "tn":
        a_spec = pl.BlockSpec((tk, tm), lambda i, j, k: (k, i))
        b_spec = pl.BlockSpec((tk, tn), lambda i, j, k: (k, j))
    else:
        a_spec = pl.BlockSpec((tm, tk), lambda i, j, k: (i, k))
        b_spec = (pl.BlockSpec((tk, tn), lambda i, j, k: (k, j)) if mode == "nn"
                  else pl.BlockSpec((tn, tk), lambda i, j, k: (j, k)))
    o_spec = pl.BlockSpec((tm, tn), lambda i, j, k: (i, j))
    in_specs = [a_spec, b_spec] + ([o_spec] if add is not None else [])
    args = (a, b) + ((add,) if add is not None else ())
    return _call(
        body, name=name, grid=(m // tm, n // tn, nk), in_specs=in_specs, out_specs=o_spec,
        out_shape=jax.ShapeDtypeStruct((m, n), out_dtype), scratch_shapes=[pltpu.VMEM((tm, tn), F32)],
        compiler_params=_params("parallel", "parallel", "arbitrary"),
    )(*args)


PROJ_TS = 256


def _kvq_proj(x, g_kv, g_mix, wk, wv, wq, gk, gq):
    def body(x_ref, gkv_ref, gmix_ref, wk_ref, wv_ref, wq_ref, gk_ref, gq_ref,
             hkv_ref, h1_ref, kpre_ref, qpre_ref, k_ref, v_ref, q_ref):
        xv = x_ref[...]
        xr = xv * _rstd(xv)
        hkv = (xr * gkv_ref[...]).astype(BF16)
        h1 = (xr * gmix_ref[...]).astype(BF16)
        hkv_ref[...] = hkv
        h1_ref[...] = h1
        kpre = _nn(hkv, wk_ref[...])
        qpre = _nn(h1, wq_ref[...])
        kpre_ref[...] = kpre
        qpre_ref[...] = qpre
        v_ref[...] = _nn(hkv, wv_ref[...]).astype(BF16)
        rk = lax.rsqrt(_seg_sum(kpre * kpre) * (1.0 / HEAD) + EPS)
        k_ref[...] = (kpre * rk * gk_ref[...]).astype(BF16)
        rq = lax.rsqrt(_seg_sum(qpre * qpre) * (1.0 / HEAD) + EPS)
        q_ref[...] = (qpre * rq * gq_ref[...]).astype(BF16)

    tile = pl.BlockSpec((PROJ_TS, D), lambda i: (i, 0))
    row = pl.BlockSpec((1, D), lambda i: (0, 0))
    wspec = pl.BlockSpec((D, D), lambda i: (0, 0))
    bf = jax.ShapeDtypeStruct((S, D), BF16)
    ff = jax.ShapeDtypeStruct((S, D), F32)
    return _call(
        body, name="kvq_proj", grid=(S // PROJ_TS,),
        in_specs=[tile, row, row, wspec, wspec, wspec, row, row],
        out_specs=[tile] * 7, out_shape=[bf, bf, ff, ff, bf, bf, bf],
        compiler_params=_params("arbitrary"),
    )(x, g_kv, g_mix, wk, wv, wq, gk, gq)


def _head_norm_bwd(dout, pre, hg, name, row_off=0):
    nt = S // PROJ_TS

    def body(do_ref, pre_ref, hg_ref, dpre_ref, dhg_ref, acc):
        i = pl.program_id(0)
        dov, pv, hgv = do_ref[...], pre_ref[...], hg_ref[...]
        r = lax.rsqrt(_seg_sum(pv * pv) * (1.0 / HEAD) + EPS)
        gd = dov * hgv
        dpre = r * gd - pv * (r * r * r) * (_seg_sum(gd * pv) * (1.0 / HEAD))
        dpre_ref[...] = dpre.astype(BF16)

        @pl.when(i == 0)
        def _():
            acc[...] = jnp.zeros(acc.shape, F32)

        acc[...] += _colsum8(dov * pv * r)

        @pl.when(i == nt - 1)
        def _():
            full = acc[...]
            fold = full[:, 0:128]
            for blk in range(1, D // 128):
                fold = fold + full[:, blk * 128:(blk + 1) * 128]
            dhg_ref[...] = fold + pltpu.roll(fold, HEAD, axis=1)

    tile = pl.BlockSpec((PROJ_TS, D), lambda i: (i, 0))
    return _call(
        body, name=name, grid=(nt,),
        in_specs=[pl.BlockSpec((PROJ_TS, D), lambda i: (i + row_off, 0)), tile, pl.BlockSpec((1, D), lambda i: (0, 0))],
        out_specs=[tile, pl.BlockSpec((8, 128), lambda i: (0, 0))],
        out_shape=[jax.ShapeDtypeStruct((S, D), BF16), jax.ShapeDtypeStruct((8, 128), F32)],
        scratch_shapes=[pltpu.VMEM((8, D), F32)],
        compiler_params=_params("arbitrary"),
    )(dout, pre, hg)


def _toeplitz_from_table(table):
    far = jnp.broadcast_to(table[:, N_REL - 1:], (N_HEADS, PADK - MAX_REL + 1))
    near = table[:, N_REL - 2::-1]
    past = jnp.broadcast_to(table[:, 0:1], (N_HEADS, MAX_REL))
    wrap = jnp.broadcast_to(table[:, N_REL - 1:], (N_HEADS, QB - 1))
    return jnp.concatenate([far, near, past, wrap], axis=1).reshape(N_HEADS, 1, TOEP)


def _table_grad_from_toeplitz(dtp, seg):
    lo = PADK - MAX_REL + 1
    near = dtp[:, lo + N_REL - 3:lo - 1:-1]
    return jnp.concatenate([seg[:, 1:2], near, seg[:, 0:1]], axis=1)


def _bias_band(tp):
    def body(tp_ref, out_ref):
        bv = pltpu.roll(jnp.broadcast_to(tp_ref[0], (QB, TOEP)), 0, axis=1, stride=1, stride_axis=0)
        out_ref[0] = bv[:, 0:KB]

    return _call(
        body, name="bias_band", grid=(N_HEADS,),
        in_specs=[pl.BlockSpec((1, 1, TOEP), lambda h: (h, 0, 0))],
        out_specs=pl.BlockSpec((1, QB, KB), lambda h: (h, 0, 0)),
        out_shape=jax.ShapeDtypeStruct((N_HEADS, QB, KB), F32),
        compiler_params=_params("parallel"),
    )(tp)


def _bias_grad(dband):
    lo, hi = PADK - MAX_REL + 1, PADK + MAX_REL

    def body(db_ref, dtp_ref, seg_ref):
        bv = jnp.concatenate([db_ref[0], jnp.zeros((QB, TOEP - KB), F32)], axis=1)
        row = lax.broadcasted_iota(jnp.int32, (QB, TOEP), 0)
        k = 1
        while k < QB:
            bv = jnp.where((row & k) != 0, pltpu.roll(bv, TOEP - k, axis=1), bv)
            k *= 2
        col = jnp.sum(bv, axis=0, keepdims=True)
        dtp_ref[0] = col
        u = lax.broadcasted_iota(jnp.int32, (1, TOEP), 1)
        far = jnp.sum(jnp.where((u < lo) | (u > hi + MAX_REL), col, 0.0))
        past = jnp.sum(jnp.where((u >= hi) & (u <= hi + MAX_REL), col, 0.0))
        lane = lax.broadcasted_iota(jnp.int32, (1, 128), 1)
        seg_ref[0] = jnp.where(lane == 0, far, jnp.where(lane == 1, past, 0.0))

    return _call(
        body, name="bias_grad", grid=(N_HEADS,),
        in_specs=[pl.BlockSpec((1, QB, KB), lambda h: (h, 0, 0))],
        out_specs=[pl.BlockSpec((1, 1, TOEP), lambda h: (h, 0, 0)), pl.BlockSpec((1, 1, 128), lambda h: (h, 0, 0))],
        out_shape=[jax.ShapeDtypeStruct((N_HEADS, 1, TOEP), F32), jax.ShapeDtypeStruct((N_HEADS, 1, 128), F32)],
        compiler_params=_params("parallel"),
    )(dband)


N_QB = S // QB
N_HP = D // 128


def _band_mask(cb):
    qc = lax.broadcasted_iota(jnp.int32, (QB, KB), 0) >> 6
    p = lax.broadcasted_iota(jnp.int32, (QB, KB), 1)
    kc = p >> 6
    return (kc >= qc) & (kc <= qc + LEFT) & (p + cb * QB >= PADK)


def _half_mask(hh):
    lane = lax.broadcasted_iota(jnp.int32, (1, 128), 1)
    return jnp.where((lane < HEAD) == (hh == 0), 1.0, 0.0).astype(BF16)


def _probs(qh, kb, bias, mask):
    sc = _nt(qh, kb) * ATTN_SCALE + bias
    sc = jnp.where(mask, sc, NEG_INF)
    e = jnp.exp(sc - jnp.max(sc, axis=-1, keepdims=True))
    return e / jnp.sum(e, axis=-1, keepdims=True)


def _attn_fwd(q, kp, vp, bias, ex=None):
    def body(q_ref, k_ref, v_ref, b_ref, o_ref):
        cb = pl.program_id(1)
        band = pl.ds(pl.multiple_of(cb * QB, QB), KB)
        kb, vb = k_ref[band, :], v_ref[band, :]
        qv = q_ref[...]
        mask = _band_mask(cb)
        low = lax.broadcasted_iota(jnp.int32, (QB, 128), 1) < HEAD
        outs = []
        for hh in range(2):
            pb = _probs(qv * _half_mask(hh), kb, b_ref[hh], mask).astype(BF16)
            outs.append(_nn(pb, vb))
        o_ref[...] = jnp.where(low, outs[0], outs[1]).astype(BF16)

    qspec = pl.BlockSpec((QB, 128), lambda hp, cb: (cb, hp))
    kspec = pl.BlockSpec((PADK + S, 128), lambda hp, cb: (0, hp))
    return _call_hosting(
        body, ex, _grid_phases((N_HP, N_QB), 0.85), name="attn_fwd", grid=(N_HP, N_QB),
        in_specs=[qspec, kspec, kspec, pl.BlockSpec((2, QB, KB), lambda hp, cb: (hp, 0, 0))],
        out_specs=[qspec], out_shape=[jax.ShapeDtypeStruct((S, D), BF16)], scratch_shapes=[],
        args=(q, kp, vp, bias), compiler_params=_params("arbitrary", "arbitrary"),
    )


def _attn_bwd(q, kp, vp, bias, do, ex=None):
    def body(q_ref, k_ref, v_ref, b_ref, do_ref, dq_ref, dk_ref, dv_ref, db_ref):
        cb = pl.program_id(1)

        @pl.when(cb == 0)
        def _():
            dk_ref[...] = jnp.zeros(dk_ref.shape, F32)
            dv_ref[...] = jnp.zeros(dv_ref.shape, F32)
            db_ref[...] = jnp.zeros(db_ref.shape, F32)

        band = pl.ds(pl.multiple_of(cb * QB, QB), KB)
        kb, vb = k_ref[band, :], v_ref[band, :]
        qv, dov = q_ref[...], do_ref[...]
        mask = _band_mask(cb)
        low = lax.broadcasted_iota(jnp.int32, (QB, 128), 1) < HEAD
        dq = jnp.zeros((QB, 128), F32)
        dkb = jnp.zeros((KB, 128), F32)
        dvb = jnp.zeros((KB, 128), F32)
        for hh in range(2):
            sel = low if hh == 0 else jnp.logical_not(low)
            qh = qv * _half_mask(hh)
            doh = dov * _half_mask(hh)
            p = _probs(qh, kb, b_ref[hh], mask)
            dp = _nt(doh, vb)
            dvb = dvb + _tn(p.astype(BF16), doh)
            ds = p * (dp - jnp.sum(dp * p, axis=-1, keepdims=True))
            db_ref[hh] += ds
            dsb = (ds * ATTN_SCALE).astype(BF16)
            dq = dq + jnp.where(sel, _nn(dsb, kb), 0.0)
            dkb = dkb + _tn(dsb, qh)
        dq_ref[...] = dq
        dk_ref[band, :] += dkb
        dv_ref[band, :] += dvb

    qspec = pl.BlockSpec((QB, 128), lambda hp, cb: (cb, hp))
    kspec = pl.BlockSpec((PADK + S, 128), lambda hp, cb: (0, hp))
    bspec = pl.BlockSpec((2, QB, KB), lambda hp, cb: (hp, 0, 0))
    kf = jax.ShapeDtypeStruct((PADK + S, D), F32)
    return _call_hosting(
        body, ex, _grid_phases((N_HP, N_QB)), name="attn_bwd", grid=(N_HP, N_QB),
        in_specs=[qspec, kspec, kspec, bspec, qspec],
        out_specs=[qspec, kspec, kspec, bspec],
        out_shape=[jax.ShapeDtypeStruct((S, D), F32), kf, kf, jax.ShapeDtypeStruct((N_HEADS, QB, KB), F32)],
        scratch_shapes=[], args=(q, kp, vp, bias, do), compiler_params=_params("arbitrary", "arbitrary"),
    )


def _adam_math(w, g, m, v):
    m = ADAM_B1 * m + (1.0 - ADAM_B1) * g
    v = ADAM_B2 * v + (1.0 - ADAM_B2) * (g * g)
    m_hat = m / (1.0 - ADAM_B1 ** ADAM_STEP)
    v_hat = v / (1.0 - ADAM_B2 ** ADAM_STEP)
    delta = -ADAM_LR * (m_hat / (jnp.sqrt(v_hat) + ADAM_EPS) + ADAM_WD * w)
    return delta, m, v


def _sum_parts(parts, name, after=None):
    count, r, c = parts.shape
    tr = r // 2 if r % 16 == 0 and r > 64 else r

    def body(p_ref, *rest):
        o_ref = rest[-1]
        acc = p_ref[0].astype(F32)
        for d in range(1, count):
            acc = acc + p_ref[d].astype(F32)
        o_ref[...] = acc

    return _call(
        body, name=name, grid=(r // tr,),
        in_specs=[pl.BlockSpec((count, tr, c), lambda i: (0, i, 0))] + ([ANY] if after is not None else []),
        out_specs=pl.BlockSpec((tr, c), lambda i: (i, 0)),
        out_shape=jax.ShapeDtypeStruct((r, c), F32), compiler_params=_params("parallel"),
    )(parts, *([after] if after is not None else []))


def _row_tile(r):
    for cand in (256, 176, 128, 64, 32, 16, 8):
        if r % cand == 0:
            return cand
    return r


def _adam_layer(w, g, m, v, layer, name, prev=None):
    nl, r, c = w.shape
    tr = _row_tile(r)

    def body(w_ref, g_ref, m_ref, v_ref, *rest):
        go_ref, d_ref, nm_ref, nv_ref = rest[-4:]
        gv = g_ref[...]
        delta, nm, nv = _adam_math(w_ref[0], gv, m_ref[0], v_ref[0])
        go_ref[0] = gv
        d_ref[0] = delta
        nm_ref[0] = nm
        nv_ref[0] = nv

    lspec = pl.BlockSpec((1, tr, c), lambda i: (layer, i, 0))
    sd = jax.ShapeDtypeStruct((nl, r, c), F32)
    extra = list(prev) if prev is not None else []
    return _call(
        body, name=name, grid=(r // tr,),
        in_specs=[lspec, pl.BlockSpec((tr, c), lambda i: (i, 0)), lspec, lspec] + [ANY] * len(extra),
        out_specs=[lspec] * 4, out_shape=[sd] * 4,
        input_output_aliases={4 + t: t for t in range(len(extra))},
        compiler_params=_params("parallel"),
    )(w, g, m, v, *extra)


def _adam(w, g, m, v, name):
    r, c = w.shape
    tr = _row_tile(r)

    def body(w_ref, g_ref, m_ref, v_ref, d_ref, nm_ref, nv_ref):
        delta, nm, nv = _adam_math(w_ref[...], g_ref[...], m_ref[...], v_ref[...])
        d_ref[...] = delta
        nm_ref[...] = nm
        nv_ref[...] = nv

    spec = pl.BlockSpec((tr, c), lambda i: (i, 0))
    sd = jax.ShapeDtypeStruct((r, c), F32)
    return _call(
        body, name=name, grid=(r // tr,), in_specs=[spec] * 4, out_specs=[spec] * 3, out_shape=[sd] * 3,
        compiler_params=_params("parallel"),
    )(w, g, m, v)


def _pad8(rows):
    return jnp.pad(rows, ((0, 8 - rows.shape[0]), (0, 0)))


def kernel(x, norm_mix_g, norm_ffn_g, pool_w, pool_b, pool_scale, kv_norm_g, w_k, w_v, k_norm_g, w_q, q_norm_g, rel_bias, w_o, w_gate, w_up, w_down, loss_target, m_norm_mix_g, m_norm_ffn_g, m_pool_w, m_pool_b, m_pool_scale, m_kv_norm_g, m_w_k, m_w_v, m_k_norm_g, m_w_q, m_q_norm_g, m_rel_bias, m_w_o, m_w_gate, m_w_up, m_w_down, v_norm_mix_g, v_norm_ffn_g, v_pool_w, v_pool_b, v_pool_scale, v_kv_norm_g, v_w_k, v_w_v, v_k_norm_g, v_w_q, v_q_norm_g, v_rel_bias, v_w_o, v_w_gate, v_w_up, v_w_down):
    assert x.shape == (1, S, D) and w_gate.shape == (2, D, F_SHARD) and w_k.shape == (D_SHARD, D)
    xin, target = x[0], loss_target[0]

    ffn_shards = [[w_gate[layer].T.astype(BF16), w_up[layer].T.astype(BF16), w_down[layer].astype(BF16)]
                  for layer in range(2)]
    att_shards = [w_k.astype(BF16), w_v.astype(BF16), w_q[0].astype(BF16), w_o[0].astype(BF16)]
    pool_shard = pool_w[0].astype(BF16).reshape(N_GROUPS * POOL_SHARD, GROUP)
    small = jnp.concatenate([pool_b[0].reshape(1, N_GROUPS * POOL_SHARD), pool_scale], axis=1)

    full0 = _run_exchange(_gather_exchange(ffn_shards[0] + [pool_shard, _pad8(small)]), "gather_layer0")
    ffn_w0 = [a.reshape(F, D) for a in full0[:3]]
    pw_f = full0[3].reshape(N_DEV, N_GROUPS, POOL_SHARD, GROUP).transpose(1, 0, 2, 3).reshape(N_GROUPS, GROUP, GROUP)
    small_f = full0[4][:, 0, :]
    pb_f = small_f[:, :N_GROUPS * POOL_SHARD].reshape(N_DEV, N_GROUPS, POOL_SHARD).transpose(1, 0, 2).reshape(1, D)
    ps_f = small_f[:, N_GROUPS * POOL_SHARD:].reshape(1, D)

    g_mix0, g_mix1 = norm_mix_g[0:1], norm_mix_g[1:2]
    g_ffn0, g_ffn1 = norm_ffn_g[0:1], norm_ffn_g[1:2]
    g_kv = kv_norm_g.reshape(1, D)
    gk_t = jnp.tile(k_norm_g.reshape(1, HEAD), (1, N_HEADS))
    gq_t = jnp.tile(q_norm_g.reshape(1, HEAD), (1, N_HEADS))

    x1, diff = _pool_fwd(xin, g_mix0, pw_f, pb_f, ps_f)
    (x2, hf0, gg0, uu0), full_att = _ffn_fwd(x1, g_ffn0, *ffn_w0, ex=_gather_exchange(att_shards))
    wk_f, wv_f, wq_f, wo_f = [a.reshape(D, D) for a in full_att]
    hkv, h1, kpre, qpre, kk, vv, qq = _kvq_proj(x2, g_kv, g_mix1, wk_f, wv_f, wq_f, gk_t, gq_t)
    kp = jnp.pad(kk, ((PADK, 0), (0, 0)))
    vp = jnp.pad(vv, ((PADK, 0), (0, 0)))
    bias = _bias_band(_toeplitz_from_table(rel_bias[0]))
    (att,), full1 = _attn_fwd(qq, kp, vp, bias, ex=_gather_exchange(ffn_shards[1]))
    ffn_w1 = [a.reshape(F, D) for a in full1]
    x3 = _mm(att, wo_f, "nn", F32, "attn_out", add=x2)
    (dx4, loss_rows, hf1, gg1, uu1), _ = _ffn_fwd(x3, g_ffn1, *ffn_w1, target=target)

    def blocks(dw):
        return dw.reshape(N_DEV, dw.shape[0] // N_DEV, dw.shape[1])

    (dgg1, duu1, dwg1, dwu1, dwd1), _ = _ffn_bwd_weights(dx4, hf1, gg1, uu1, ffn_w1[2], name="ffn_bwd1")
    parts1 = [blocks(dw) for dw in (dwg1, dwu1, dwd1)]
    (dx3, dg_ffn1), stage1 = _ffn_bwd_input(dx4, x3, g_ffn1, dgg1, duu1, ffn_w1[0], ffn_w1[1], "ffn_dx1",
                                            ex=_pair_exchange(parts1))
    chip1 = _pair_add(parts1, stage1, "pair_add_ffn1")
    datt = _mm(dx3, wo_f, "nt", BF16, "d_attn")
    dwo = _mm(att, dx3, "tn", BF16, "d_wo")
    (dq, dkp, dvp, dband), recv1 = _attn_bwd(qq, kp, vp, bias, datt, ex=_chip_exchange(chip1))
    dtp, seg = _bias_grad(dband)
    dqpre, dgq = _head_norm_bwd(dq, qpre, gq_t, "q_norm_bwd")
    dkpre, dgk = _head_norm_bwd(dkp, kpre, gk_t, "k_norm_bwd", row_off=PADK // PROJ_TS)
    dh1 = _mm(dqpre, wq_f, "nt", F32, "d_h1")
    dwq = _mm(h1, dqpre, "tn", BF16, "d_wq")
    dhkv = _mm(dkpre, wk_f, "nt", F32, "d_hkv_k")
    dv = dvp[PADK:]
    dhkv = _mm(dv, wv_f, "nt", F32, "d_hkv_v", add=dhkv)
    dwk = _mm(hkv, dkpre, "tn", BF16, "d_wk")
    dwv = _mm(hkv, dv, "tn", BF16, "d_wv")
    parts_att = [blocks(dw) for dw in (dwk, dwv, dwq, dwo)]
    (dx2, dg_mix1, dg_kv), stage_att = _norm_bwd(dx3, x2, [dh1, dhkv], [g_mix1, g_kv], "norm_bwd_mix1",
                                                 ex=_pair_exchange(parts_att))
    chip_att = _pair_add(parts_att, stage_att, "pair_add_att")
    (dgg0, duu0, dwg0, dwu0, dwd0), recv_att = _ffn_bwd_weights(
        dx2, hf0, gg0, uu0, ffn_w0[2], name="ffn_bwd0", ex=_chip_exchange(chip_att))
    parts0 = [blocks(dw) for dw in (dwg0, dwu0, dwd0)]
    (dx1, dg_ffn0), stage0 = _ffn_bwd_input(dx2, x1, g_ffn0, dgg0, duu0, ffn_w0[0], ffn_w0[1], "ffn_dx0",
                                            ex=_pair_exchange(parts0))
    both = _pair_add(parts0, stage0, "pair_add_ffn0", copies=2)
    (grad_x, dpw, db_rows, ds_rows, dg_mix0), _ = _pool_bwd(dx1, xin, diff, g_mix0, pw_f, pb_f, ps_f)
    dpw_blocks = dpw.reshape(N_GROUPS, N_DEV, POOL_SHARD, GROUP).transpose(1, 0, 2, 3)
    dpw_blocks = dpw_blocks.reshape(N_DEV, N_GROUPS * POOL_SHARD, GROUP).astype(BF16)
    recv_pool = _run_exchange(_scatter_exchange([dpw_blocks]), "scatter_pool")


    misc = jnp.concatenate([dgk[0:1, 0:HEAD], dgq[0:1, 0:HEAD], loss_rows[0:1, 0:1],
                            seg[:, 0, 0].reshape(1, N_HEADS), seg[:, 0, 1].reshape(1, N_HEADS)], axis=1)
    misc = jnp.pad(misc, ((0, 0), (0, D - misc.shape[1])))
    vec_rows = jnp.concatenate([dg_mix0[0:1], dg_mix1[0:1], dg_ffn0[0:1], dg_ffn1[0:1], dg_kv[0:1],
                                db_rows[0:1], ds_rows[0:1], misc], axis=0)
    pack = jnp.concatenate([vec_rows, dtp.reshape(N_HEADS, TOEP)], axis=0)
    tot = _all_reduce_small(pack, "reduce_small")
    flight, chip0, land0, token = _chip_exchange_start(both[:3], both[3:], tot, "scatter_ffn0_start")

    loss = tot[7, 2 * HEAD]
    g_norm_mix = tot[0:2]
    g_norm_ffn = tot[2:4]
    g_kv_norm = tot[4]
    g_k_norm = tot[7, 0:HEAD]
    g_q_norm = tot[7, HEAD:2 * HEAD].reshape(1, HEAD)
    seg_tot = jnp.stack([tot[7, 2 * HEAD + 1:2 * HEAD + 1 + N_HEADS],
                         tot[7, 2 * HEAD + 1 + N_HEADS:2 * HEAD + 1 + 2 * N_HEADS]], axis=1)
    g_rel = _table_grad_from_toeplitz(tot[8:8 + N_HEADS], seg_tot).reshape(1, N_HEADS, N_REL)
    me = 4 * lax.axis_index("x") + 2 * lax.axis_index("y") + lax.axis_index("c")
    g_pool_b = lax.dynamic_slice_in_dim(tot[5].reshape(N_GROUPS, GROUP), me * POOL_SHARD, POOL_SHARD, axis=1)
    g_pool_b = g_pool_b.reshape(1, N_GROUPS, POOL_SHARD)
    g_pool_scale = lax.dynamic_slice_in_dim(tot[6:7], me * D_SHARD, D_SHARD, axis=1)

    def summed(arrs, tag, after):
        return [_sum_parts(a, "sum_parts_%s%d" % (tag, k), after) for k, a in enumerate(arrs)]

    sums1 = summed(recv1, "ffn1_", token)
    g_wk, g_wv, g_wq, g_wo = summed(recv_att, "att_", token)
    g_pool_w = summed(recv_pool, "pool_", token)[0].reshape(1, N_GROUPS, POOL_SHARD, GROUP)

    grads = dict(norm_mix_g=g_norm_mix, norm_ffn_g=g_norm_ffn, pool_w=g_pool_w, pool_b=g_pool_b,
                 pool_scale=g_pool_scale, kv_norm_g=g_kv_norm, w_k=g_wk, w_v=g_wv, k_norm_g=g_k_norm, w_q=g_wq[None],
                 q_norm_g=g_q_norm, rel_bias=g_rel, w_o=g_wo[None])
    weights = dict(norm_mix_g=norm_mix_g, norm_ffn_g=norm_ffn_g, pool_w=pool_w, pool_b=pool_b,
                   pool_scale=pool_scale, kv_norm_g=kv_norm_g, w_k=w_k, w_v=w_v, k_norm_g=k_norm_g, w_q=w_q,
                   q_norm_g=q_norm_g, rel_bias=rel_bias, w_o=w_o, w_gate=w_gate, w_up=w_up, w_down=w_down)
    mom1 = dict(norm_mix_g=m_norm_mix_g, norm_ffn_g=m_norm_ffn_g, pool_w=m_pool_w, pool_b=m_pool_b,
                pool_scale=m_pool_scale, kv_norm_g=m_kv_norm_g, w_k=m_w_k, w_v=m_w_v, k_norm_g=m_k_norm_g,
                w_q=m_w_q, q_norm_g=m_q_norm_g, rel_bias=m_rel_bias, w_o=m_w_o, w_gate=m_w_gate, w_up=m_w_up,
                w_down=m_w_down)
    mom2 = dict(norm_mix_g=v_norm_mix_g, norm_ffn_g=v_norm_ffn_g, pool_w=v_pool_w, pool_b=v_pool_b,
                pool_scale=v_pool_scale, kv_norm_g=v_kv_norm_g, w_k=v_w_k, w_v=v_w_v, k_norm_g=v_k_norm_g,
                w_q=v_w_q, q_norm_g=v_q_norm_g, rel_bias=v_rel_bias, w_o=v_w_o, w_gate=v_w_gate, w_up=v_w_up,
                w_down=v_w_down)
    names = list(weights)

    ffn_names = ("w_gate", "w_up", "w_down")
    layer1 = {nm: _adam_layer(weights[nm], g, mom1[nm], mom2[nm], 1, "adam1_" + nm)
              for nm, g in zip(ffn_names, (sums1[0].T, sums1[1].T, sums1[2]))}
    large = ("w_k", "w_v", "w_q", "w_o", "pool_w")
    deltas, new_m, new_v = {}, {}, {}
    for nm in large:
        shape = weights[nm].shape
        flat = lambda a: a.reshape(-1, shape[-1])
        dl, m1, m2 = _adam(flat(weights[nm]), flat(grads[nm]), flat(mom1[nm]), flat(mom2[nm]), "adam_" + nm)
        deltas[nm], new_m[nm], new_v[nm] = dl.reshape(shape), m1.reshape(shape), m2.reshape(shape)
    small_names = [nm for nm in names if nm not in large + ffn_names]

    def pack_small(tree):
        cols = []
        for nm in small_names:
            flat = tree[nm].reshape(-1)
            cols.append(jnp.pad(flat, (0, -flat.shape[0] % 1024)))
        return jnp.concatenate(cols).reshape(-1, 128)

    dl, m1, m2 = _adam(pack_small(weights), pack_small(grads), pack_small(mom1), pack_small(mom2), "adam_small")

    def unpack_small(packed, out):
        flat, off = packed.reshape(-1), 0
        for nm in small_names:
            size = weights[nm].size
            out[nm] = flat[off:off + size].reshape(weights[nm].shape)
            off += size + (-size % 1024)

    unpack_small(dl, deltas)
    unpack_small(m1, new_m)
    unpack_small(m2, new_v)

    recv0 = _chip_exchange_wait(flight, chip0, land0, dl, "scatter_ffn0_wait")
    sums0 = summed(recv0, "ffn0_", None)
    for nm, g in zip(ffn_names, (sums0[0].T, sums0[1].T, sums0[2])):
        grads[nm], deltas[nm], new_m[nm], new_v[nm] = _adam_layer(
            weights[nm], g, mom1[nm], mom2[nm], 0, "adam0_" + nm, prev=layer1[nm])

    return (loss, grad_x[None], *[grads[nm] for nm in names], *[deltas[nm] for nm in names],
            *[new_m[nm] for nm in names], *[new_v[nm] for nm in names])
```

```python
import functools

import jax
import jax.numpy as jnp
from jax import lax
from jax.experimental import pallas as pl
from jax.experimental.pallas import tpu as pltpu

F32 = jnp.float32
BF16 = jnp.bfloat16
MESH_ID = pl.DeviceIdType.MESH

N_DEV = 8
S = 2048
D = 1024
F = 2816
F_SHARD = F // N_DEV
D_SHARD = D // N_DEV
N_GROUPS = 4
GROUP = D // N_GROUPS
POOL_SHARD = GROUP // N_DEV
MAX_WIN = 16
HEAD = 64
N_HEADS = D // HEAD
CHUNK = 64
LEFT = 8
QB = 4 * CHUNK
KB = QB + LEFT * CHUNK
PADK = LEFT * CHUNK
TOEP = 1024
N_REL = 257
MAX_REL = 128
EPS = 1e-6
NEG_INF = -1e30
ATTN_SCALE = HEAD ** -0.5

ADAM_LR = 0.001
ADAM_B1 = 0.9
ADAM_B2 = 0.999
ADAM_EPS = 1e-08
ADAM_WD = 0.01
ADAM_STEP = 10

VMEM_LIMIT = 52 * 1024 * 1024

ANY = pl.BlockSpec(memory_space=pl.ANY)
VMEM = pl.BlockSpec(memory_space=pltpu.VMEM)


def _call(body, **kw):
    return pl.pallas_call(body, **kw)


def _params(*sem):
    return pltpu.CompilerParams(dimension_semantics=sem, vmem_limit_bytes=VMEM_LIMIT)


def _dot(a, b, dims):
    return lax.dot_general(a, b, (dims, ((), ())), preferred_element_type=F32)


def _nn(a, b):
    return _dot(a, b, ((1,), (0,)))


def _nt(a, b):
    return _dot(a, b, ((1,), (1,)))


def _tn(a, b):
    return _dot(a, b, ((0,), (0,)))


def _rstd(x):
    return lax.rsqrt(jnp.mean(x * x, axis=-1, keepdims=True) + EPS)


def _rms_bwd(dh, x, r, g):
    gd = dh * g
    return r * gd - x * (r * r * r) * jnp.mean(gd * x, axis=-1, keepdims=True)


def _colsum8(v):
    return jnp.broadcast_to(jnp.sum(v, axis=0, keepdims=True), (8, v.shape[1]))


def _seg_sum(v):
    r = lax.broadcasted_iota(jnp.int32, (128, 128), 0) // HEAD
    c = lax.broadcasted_iota(jnp.int32, (128, 128), 1) // HEAD
    ones = jnp.where(r == c, 1.0, 0.0).astype(BF16)
    out = []
    for blk in range(v.shape[1] // 128):
        part = v[:, blk * 128:(blk + 1) * 128]
        hi = part.astype(BF16)
        rest = part - hi.astype(F32)
        mid = rest.astype(BF16)
        lo = (rest - mid.astype(F32)).astype(BF16)
        out.append(_nn(hi, ones) + _nn(mid, ones) + _nn(lo, ones))
    return jnp.concatenate(out, axis=1)


def _place():
    return lax.axis_index("x"), lax.axis_index("y"), lax.axis_index("c")


class _Exchange:
    def __init__(self, ins, out_shape, sems, start, mid, finish):
        self.ins, self.out_shape, self.sems = list(ins), list(out_shape), list(sems)
        self.start, self.mid, self.finish = start, mid, finish


def _gather_exchange(shards):
    n = len(shards)

    def tools(ins, outs, sems):
        send_sems, recv_sems, local_sems = sems
        x, y, c = _place()
        me, sibling = (x, y, c), (x, y, 1 - c)
        chips = [(1 - x, y), (x, 1 - y), (1 - x, 1 - y)]

        def slot(k, px, py, pc):
            return outs[k].at[4 * px + 2 * py + pc]

        def copy(k, s, block, to, own=False):
            return pltpu.make_async_remote_copy(
                src_ref=ins[k] if own else slot(k, *block), dst_ref=slot(k, *block),
                send_sem=send_sems.at[k, s], recv_sem=recv_sems.at[k, s], device_id=to, device_id_type=MESH_ID)

        def mine():
            return [pltpu.make_async_copy(ins[k], slot(k, *me), local_sems.at[k]) for k in range(n)]

        def first():
            out = []
            for k in range(n):
                out.append(copy(k, 0, me, sibling, own=True))
                out += [copy(k, 1 + j, me, (*chip, c), own=True) for j, chip in enumerate(chips)]
            return out

        def landed(j):
            return [copy(k, 1 + j, (*chips[j], c), me) for k in range(n)]

        def passed(j):
            return [copy(k, 4 + j, (*chips[j], c), sibling) for k in range(n)]

        def from_sibling():
            out = []
            for k in range(n):
                out.append(copy(k, 0, sibling, me))
                out += [copy(k, 4 + j, (*chip, 1 - c), me) for j, chip in enumerate(chips)]
            return out

        return mine, first, landed, passed, from_sibling

    def start(ins, outs, sems):
        mine, first, _, _, _ = tools(ins, outs, sems)
        for cp in mine() + first():
            cp.start()

    def mid(ins, outs, sems):
        _, _, landed, passed, _ = tools(ins, outs, sems)
        for j in range(3):
            for arrived, onward in zip(landed(j), passed(j)):
                arrived.wait_recv()
                onward.start()

    def finish(ins, outs, sems):
        mine, first, _, passed, from_sibling = tools(ins, outs, sems)
        for cp in from_sibling():
            cp.wait_recv()
        for cp in first() + passed(0) + passed(1) + passed(2):
            cp.wait_send()
        for cp in mine():
            cp.wait()

    return _Exchange(
        shards, [jax.ShapeDtypeStruct((N_DEV,) + a.shape, a.dtype) for a in shards],
        [pltpu.SemaphoreType.DMA((n, 7)), pltpu.SemaphoreType.DMA((n, 7)), pltpu.SemaphoreType.DMA((n,))],
        start, mid, finish)


def _peer(x, y, c, m):
    px = 1 - x if m & 4 else x
    py = 1 - y if m & 2 else y
    pc = 1 - c if m & 1 else c
    return px, py, pc


def _scatter_exchange(parts):
    n = len(parts)

    def tools(ins, outs, sems):
        send_sems, recv_sems, local_sems = sems
        x, y, c = _place()
        me = 4 * x + 2 * y + c
        def mine():
            return [pltpu.make_async_copy(ins[k].at[me], outs[k].at[me], local_sems.at[k]) for k in range(n)]

        def remote(dst_is_mine):
            out = []
            for m in range(1, N_DEV):
                px, py, pc = _peer(x, y, c, m)
                peer = 4 * px + 2 * py + pc
                for k in range(n):
                    out.append(pltpu.make_async_remote_copy(
                        src_ref=ins[k].at[peer], dst_ref=outs[k].at[me if dst_is_mine else peer],
                        send_sem=send_sems.at[k, m - 1], recv_sem=recv_sems.at[k, m - 1],
                        device_id=(px, py, pc), device_id_type=MESH_ID))
            return out

        return mine, remote

    def start(ins, outs, sems):
        mine, remote = tools(ins, outs, sems)
        for cp in mine() + remote(True):
            cp.start()

    def mid(ins, outs, sems):
        pass

    def finish(ins, outs, sems):
        mine, remote = tools(ins, outs, sems)
        for cp in remote(False):
            cp.wait_recv()
        for cp in remote(True):
            cp.wait_send()
        for cp in mine():
            cp.wait()

    return _Exchange(
        parts, [jax.ShapeDtypeStruct(a.shape, a.dtype) for a in parts],
        [pltpu.SemaphoreType.DMA((n, 7)), pltpu.SemaphoreType.DMA((n, 7)), pltpu.SemaphoreType.DMA((n,))],
        start, mid, finish)


N_CHIPS = N_DEV // 2


def _pair_exchange(parts):
    n = len(parts)

    def copies(ins, outs, sems):
        send_sems, recv_sems = sems
        x, y, c = _place()
        return [pltpu.make_async_remote_copy(
            src_ref=ins[k].at[2 * q + 1 - c], dst_ref=outs[k].at[q], send_sem=send_sems.at[k, q],
            recv_sem=recv_sems.at[k, q], device_id=(x, y, 1 - c), device_id_type=MESH_ID)
            for k in range(n) for q in range(N_CHIPS)]

    def start(ins, outs, sems):
        for cp in copies(ins, outs, sems):
            cp.start()

    def mid(ins, outs, sems):
        pass

    def finish(ins, outs, sems):
        for cp in copies(ins, outs, sems):
            cp.wait_recv()
        for cp in copies(ins, outs, sems):
            cp.wait_send()

    return _Exchange(
        parts, [jax.ShapeDtypeStruct((N_CHIPS,) + a.shape[1:], a.dtype) for a in parts],
        [pltpu.SemaphoreType.DMA((n, N_CHIPS)), pltpu.SemaphoreType.DMA((n, N_CHIPS))], start, mid, finish)


def _pair_add(parts, stage, name, copies=1):
    n = len(parts)
    core = lax.axis_index("c").reshape(1)

    def body(core_ref, *refs):
        for k in range(n):
            mine, theirs = refs[k], refs[n + k]
            total = (mine[0, 0].astype(F32) + theirs[0].astype(F32)).astype(BF16)
            for rep in range(copies):
                refs[(2 + rep) * n + k][0] = total

    in_specs, out_specs = [], []
    for a in parts:
        _, r, cdim = a.shape
        in_specs.append(pl.BlockSpec((1, 1, r // 2, cdim), lambda q, i, core_ref: (q, core_ref[0], i, 0)))
    for a in parts:
        _, r, cdim = a.shape
        in_specs.append(pl.BlockSpec((1, r // 2, cdim), lambda q, i, core_ref: (q, i, 0)))
        out_specs.append(pl.BlockSpec((1, r // 2, cdim), lambda q, i, core_ref: (q, i, 0)))
    return list(_call(
        body, name=name,
        grid_spec=pltpu.PrefetchScalarGridSpec(num_scalar_prefetch=1, grid=(N_CHIPS, 2), in_specs=in_specs,
                                               out_specs=out_specs * copies),
        out_shape=[jax.ShapeDtypeStruct(s.shape, BF16) for s in stage] * copies,
        compiler_params=_params("arbitrary", "arbitrary"),
    )(core, *[a.reshape((N_CHIPS, 2) + a.shape[1:]) for a in parts], *stage))


HBM = pl.BlockSpec(memory_space=pltpu.HBM)
SEMAPHORES = pl.BlockSpec(memory_space=pltpu.SEMAPHORE)


def _chip_copies(srcs, lands, send_sems, recv_sems, mine_is_dst):
    x, y, c = _place()
    me = 2 * x + y
    out = []
    for m in range(1, N_CHIPS):
        px, py, _ = _peer(x, y, c, 2 * m)
        peer = 2 * px + py
        for k in range(len(srcs)):
            pair = k * (N_CHIPS - 1) + m - 1
            out.append(pltpu.make_async_remote_copy(
                src_ref=srcs[k].at[peer], dst_ref=lands[k].at[me if mine_is_dst else peer],
                send_sem=send_sems[pair], recv_sem=recv_sems[pair],
                device_id=(px, py, c), device_id_type=MESH_ID))
    return out


def _chip_exchange_start(chip_parts, lands, after, name):
    n = len(chip_parts)
    pairs = n * (N_CHIPS - 1)

    def body(*refs):
        srcs, zones = refs[:n], refs[n:2 * n]
        sems = refs[2 * n + 1:2 * n + 1 + 2 * pairs]
        token = refs[-1]
        for cp in _chip_copies(srcs, zones, sems[:pairs], sems[pairs:], True):
            cp.start()
        token[...] = jnp.zeros(token.shape, F32)

    thru = [pltpu.HBM(a.shape, a.dtype) for a in chip_parts + lands]
    hbm = [pltpu.with_memory_space_constraint(a, pltpu.HBM) for a in chip_parts + lands]
    res = _call(
        body, name=name,
        out_shape=[pltpu.SemaphoreType.DMA(())] * (2 * pairs) + thru + [jax.ShapeDtypeStruct((8, 128), F32)],
        in_specs=[HBM] * (2 * n) + [ANY], out_specs=[SEMAPHORES] * (2 * pairs) + [HBM] * (2 * n) + [VMEM],
        input_output_aliases={i: 2 * pairs + i for i in range(2 * n)},
        compiler_params=pltpu.CompilerParams(has_side_effects=pltpu.SideEffectType.DATAFLOW_SIDE_EFFECTING),
    )(*hbm, after)
    sems, rest = list(res[:2 * pairs]), res[2 * pairs:]
    return sems, list(rest[:n]), list(rest[n:2 * n]), rest[-1]


def _chip_exchange_wait(sems, chip_parts, lands, after, name):
    n = len(chip_parts)
    pairs = n * (N_CHIPS - 1)

    def body(*refs):
        srcs, zones = refs[:n], refs[n:2 * n]
        sem_refs = refs[2 * n:2 * n + 2 * pairs]
        for cp in _chip_copies(srcs, zones, sem_refs[:pairs], sem_refs[pairs:], False):
            cp.wait_send()
            cp.wait_recv()

    thru = [pltpu.HBM(a.shape, a.dtype) for a in chip_parts + lands]
    res = _call(
        body, name=name, out_shape=thru,
        in_specs=[HBM] * (2 * n) + [SEMAPHORES] * (2 * pairs) + [ANY], out_specs=[HBM] * (2 * n),
        input_output_aliases={i: i for i in range(2 * n)},
        compiler_params=pltpu.CompilerParams(has_side_effects=pltpu.SideEffectType.DATAFLOW_SIDE_EFFECTING),
    )(*chip_parts, *lands, *sems, after)
    return list(res[n:])


def _chip_exchange(chip_parts):
    n = len(chip_parts)

    def tools(ins, outs, sems):
        send_sems, recv_sems, local_sems = sems
        x, y, c = _place()
        me = 2 * x + y

        def mine():
            return [pltpu.make_async_copy(ins[k].at[me], outs[k].at[me], local_sems.at[k]) for k in range(n)]

        def remote(dst_is_mine):
            out = []
            for m in range(1, N_CHIPS):
                px, py, _ = _peer(x, y, c, 2 * m)
                peer = 2 * px + py
                for k in range(n):
                    out.append(pltpu.make_async_remote_copy(
                        src_ref=ins[k].at[peer], dst_ref=outs[k].at[me if dst_is_mine else peer],
                        send_sem=send_sems.at[k, m - 1], recv_sem=recv_sems.at[k, m - 1],
                        device_id=(px, py, c), device_id_type=MESH_ID))
            return out

        return mine, remote

    def start(ins, outs, sems):
        mine, remote = tools(ins, outs, sems)
        for cp in mine() + remote(True):
            cp.start()

    def mid(ins, outs, sems):
        pass

    def finish(ins, outs, sems):
        mine, remote = tools(ins, outs, sems)
        for cp in remote(False):
            cp.wait_recv()
        for cp in remote(True):
            cp.wait_send()
        for cp in mine():
            cp.wait()

    return _Exchange(
        chip_parts, [jax.ShapeDtypeStruct(a.shape, a.dtype) for a in chip_parts],
        [pltpu.SemaphoreType.DMA((n, N_CHIPS - 1)), pltpu.SemaphoreType.DMA((n, N_CHIPS - 1)),
         pltpu.SemaphoreType.DMA((n,))], start, mid, finish)


def _run_exchange(ex, name, after=None):
    n_in, n_out = len(ex.ins), len(ex.out_shape)
    order = [] if after is None else [after]

    def body(*refs):
        ins, outs = refs[:n_in], refs[n_in + len(order):n_in + len(order) + n_out]
        sems = refs[n_in + len(order) + n_out:]
        ex.start(ins, outs, sems)
        ex.mid(ins, outs, sems)
        ex.finish(ins, outs, sems)

    return list(_call(body, name=name, out_shape=ex.out_shape, in_specs=[ANY] * (n_in + len(order)),
                      out_specs=[ANY] * n_out, scratch_shapes=ex.sems)(*ex.ins, *order))


def _call_hosting(body, ex, phases, *, in_specs, out_specs, out_shape, scratch_shapes, args, **kw):
    n_in, n_out, n_scr = len(in_specs), len(out_specs), len(scratch_shapes)
    if ex is None:
        res = _call(body, in_specs=in_specs, out_specs=out_specs, out_shape=out_shape,
                    scratch_shapes=scratch_shapes, **kw)(*args)
        return list(res), []
    n_xin, n_xout = len(ex.ins), len(ex.out_shape)

    def hosting(*refs):
        a, b = n_in, n_in + n_xin
        c, d = b + n_out, b + n_out + n_xout
        ins, xins, outs, xouts = refs[:a], refs[a:b], refs[b:c], refs[c:d]
        scr, sems = refs[d:d + n_scr], refs[d + n_scr:]
        first, mid, last = phases()

        @pl.when(first)
        def _():
            ex.start(xins, xouts, sems)

        body(*ins, *outs, *scr)

        @pl.when(mid)
        def _():
            ex.mid(xins, xouts, sems)

        @pl.when(last)
        def _():
            ex.finish(xins, xouts, sems)

    res = _call(hosting, in_specs=list(in_specs) + [ANY] * n_xin, out_specs=list(out_specs) + [ANY] * n_xout,
                out_shape=list(out_shape) + ex.out_shape, scratch_shapes=list(scratch_shapes) + ex.sems,
                **kw)(*args, *ex.ins)
    return list(res[:n_out]), list(res[n_out:])


def _grid_phases(dims, mid_fraction=0.8):
    total = 1
    for d in dims:
        total *= d
    mid = min(max(int(total * mid_fraction), 1), total - 1)

    def phases():
        step = pl.program_id(0)
        for axis in range(1, len(dims)):
            step = step * dims[axis] + pl.program_id(axis)
        return step == 0, step == mid, step == total - 1
    return phases


def _all_reduce_small(pack, name, after):
    rows = pack.shape[0]

    def body(in_ref, _, out_ref, recv, send_sems, recv_sems):
        x, y, c = _place()
        me = 4 * x + 2 * y + c
        recv[me] = in_ref[...]
        sent = []
        for m in range(1, N_DEV):
            px, py, pc = _peer(x, y, c, m)
            cp = pltpu.make_async_remote_copy(
                src_ref=in_ref, dst_ref=recv.at[me], send_sem=send_sems.at[m - 1], recv_sem=recv_sems.at[m - 1],
                device_id=(px, py, pc), device_id_type=MESH_ID)
            cp.start()
            sent.append(cp)
        for m in range(1, N_DEV):
            px, py, pc = _peer(x, y, c, m)
            peer = 4 * px + 2 * py + pc
            pltpu.make_async_remote_copy(
                src_ref=in_ref, dst_ref=recv.at[peer], send_sem=send_sems.at[m - 1], recv_sem=recv_sems.at[m - 1],
                device_id=(px, py, pc), device_id_type=MESH_ID).wait_recv()
        acc = recv[0]
        for d in range(1, N_DEV):
            acc = acc + recv[d]
        out_ref[...] = acc
        for cp in sent:
            cp.wait_send()

    return _call(
        body, name=name, out_shape=jax.ShapeDtypeStruct(pack.shape, F32), in_specs=[VMEM, ANY], out_specs=VMEM,
        scratch_shapes=[pltpu.VMEM((N_DEV, rows, pack.shape[1]), F32), pltpu.SemaphoreType.DMA((7,)),
                        pltpu.SemaphoreType.DMA((7,))],
        compiler_params=pltpu.CompilerParams(vmem_limit_bytes=VMEM_LIMIT),
    )(pack, after)


POOL_TS = 256


def _pool_counts(first_row, rows, win):
    t = first_row + lax.broadcasted_iota(jnp.int32, (rows, 1), 0)
    return jnp.minimum(t + 1, win).astype(F32)


def _pool_fwd(x, g, w, b, scale):
    nt = S // POOL_TS

    def body(x_ref, g_ref, w_ref, b_ref, s_ref, out_ref, diff_ref, ext):
        i = pl.program_id(0)

        @pl.when(i == 0)
        def _():
            ext[0:MAX_WIN, :] = jnp.zeros((MAX_WIN, D), F32)

        @pl.when(i > 0)
        def _():
            ext[0:MAX_WIN, :] = ext[POOL_TS:POOL_TS + MAX_WIN, :]

        xv = x_ref[...]
        h = xv * _rstd(xv) * g_ref[...]
        ext[MAX_WIN:, :] = h
        for gi in range(N_GROUPS):
            win = 2 << gi
            cols = slice(gi * GROUP, (gi + 1) * GROUP)
            sm = ext[:, cols]
            k = 1
            while k < win:
                sm = sm + pltpu.roll(sm, k, axis=0)
                k *= 2
            pooled = sm[MAX_WIN:, :] / _pool_counts(i * POOL_TS, POOL_TS, win)
            diff = (pooled - h[:, cols]).astype(BF16)
            yv = (_nn(diff, w_ref[gi]) + b_ref[:, cols]) * s_ref[:, cols]
            out_ref[:, cols] = xv[:, cols] + yv
            diff_ref[:, cols] = diff

    row = pl.BlockSpec((1, D), lambda i: (0, 0))
    tile = pl.BlockSpec((POOL_TS, D), lambda i: (i, 0))
    return _call(
        body, name="pool_fwd", grid=(nt,),
        in_specs=[tile, row, pl.BlockSpec((N_GROUPS, GROUP, GROUP), lambda i: (0, 0, 0)), row, row],
        out_specs=[tile, tile],
        out_shape=[jax.ShapeDtypeStruct((S, D), F32), jax.ShapeDtypeStruct((S, D), BF16)],
        scratch_shapes=[pltpu.VMEM((POOL_TS + MAX_WIN, D), F32)],
        compiler_params=_params("arbitrary"),
    )(x, g, w, b, scale)


def _pool_bwd(dy, x, diff, g, w, b, scale, ex=None):
    nt = S // POOL_TS

    def body(dy_ref, x_ref, diff_ref, g_ref, w_ref, b_ref, s_ref, gx_ref, dw_ref, db_ref, ds_ref, dg_ref, ext, dh):
        i = pl.program_id(0)
        first_row = (nt - 1 - i) * POOL_TS

        @pl.when(i == 0)
        def _():
            ext[POOL_TS:, :] = jnp.zeros((MAX_WIN, D), F32)
            dw_ref[...] = jnp.zeros(dw_ref.shape, F32)
            db_ref[...] = jnp.zeros(db_ref.shape, F32)
            ds_ref[...] = jnp.zeros(ds_ref.shape, F32)
            dg_ref[...] = jnp.zeros(dg_ref.shape, F32)

        @pl.when(i > 0)
        def _():
            ext[POOL_TS:, :] = ext[0:MAX_WIN, :]

        dyv = dy_ref[...]
        for gi in range(N_GROUPS):
            win = 2 << gi
            cols = slice(gi * GROUP, (gi + 1) * GROUP)
            dfb = diff_ref[:, cols]
            z = _nn(dfb, w_ref[gi]) + b_ref[:, cols]
            dyg = dyv[:, cols]
            ds_ref[:, cols] += _colsum8(dyg * z)
            dz = dyg * s_ref[:, cols]
            db_ref[:, cols] += _colsum8(dz)
            dzb = dz.astype(BF16)
            dw_ref[gi] += _tn(dfb, dzb)
            ddiff = _nt(dzb, w_ref[gi])
            ext[0:POOL_TS, cols] = ddiff / _pool_counts(first_row, POOL_TS, win)
            sm = ext[:, cols]
            k = 1
            while k < win:
                sm = sm + pltpu.roll(sm, POOL_TS + MAX_WIN - k, axis=0)
                k *= 2
            dh[:, cols] = sm[0:POOL_TS, :] - ddiff
        xv = x_ref[...]
        r = _rstd(xv)
        gv = g_ref[...]
        dhv = dh[...]
        dg_ref[...] += _colsum8(dhv * xv * r)
        gx_ref[...] = dyv + _rms_bwd(dhv, xv, r, gv)

    row = pl.BlockSpec((1, D), lambda i: (0, 0))
    tile = pl.BlockSpec((POOL_TS, D), lambda i: (nt - 1 - i, 0))
    acc = pl.BlockSpec((8, D), lambda i: (0, 0))
    wspec = pl.BlockSpec((N_GROUPS, GROUP, GROUP), lambda i: (0, 0, 0))
    return _call_hosting(
        body, ex, _grid_phases((nt,)), name="pool_bwd", grid=(nt,),
        in_specs=[tile, tile, tile, row, wspec, row, row],
        out_specs=[tile, wspec, acc, acc, acc],
        out_shape=[jax.ShapeDtypeStruct((S, D), F32), jax.ShapeDtypeStruct((N_GROUPS, GROUP, GROUP), F32),
                   jax.ShapeDtypeStruct((8, D), F32), jax.ShapeDtypeStruct((8, D), F32),
                   jax.ShapeDtypeStruct((8, D), F32)],
        scratch_shapes=[pltpu.VMEM((POOL_TS + MAX_WIN, D), F32), pltpu.VMEM((POOL_TS, D), F32)],
        args=(dy, x, diff, g, w, b, scale), compiler_params=_params("arbitrary"),
    )


FFN_TS = min(S, 1024)
FFN_TF = 256


def _ffn_fwd(x, g, wg_t, wu_t, wd, target=None, ex=None):
    ni, nj = S // FFN_TS, F // FFN_TF
    with_loss = target is not None

    def body(*refs):
        if with_loss:
            x_ref, g_ref, wg_ref, wu_ref, wd_ref, t_ref, out_ref, loss_ref, h_ref, gg_ref, uu_ref, hs, acc = refs
        else:
            x_ref, g_ref, wg_ref, wu_ref, wd_ref, out_ref, h_ref, gg_ref, uu_ref, hs, acc = refs
        i, j = pl.program_id(0), pl.program_id(1)

        @pl.when(j == 0)
        def _():
            xv = x_ref[...]
            hb = (xv * _rstd(xv) * g_ref[...]).astype(BF16)
            hs[...] = hb
            h_ref[...] = hb
            acc[...] = jnp.zeros(acc.shape, F32)

        hb = hs[...]
        gg = _nt(hb, wg_ref[...])
        uu = _nt(hb, wu_ref[...])
        gg_ref[...] = gg
        uu_ref[...] = uu
        a = (gg * jax.nn.sigmoid(gg) * uu).astype(BF16)
        acc[...] += _nn(a, wd_ref[...])

        @pl.when(j == nj - 1)
        def _():
            yv = x_ref[...] + acc[...]
            if with_loss:
                err = yv - t_ref[...]
                out_ref[...] = err * (1.0 / D)
                part = jnp.sum(err * err) * (0.5 / D)

                @pl.when(i == 0)
                def _():
                    loss_ref[...] = jnp.zeros(loss_ref.shape, F32)

                loss_ref[...] += jnp.broadcast_to(part, loss_ref.shape)
            else:
                out_ref[...] = yv

    xt = pl.BlockSpec((FFN_TS, D), lambda i, j: (i, 0))
    row = pl.BlockSpec((1, D), lambda i, j: (0, 0))
    wt = pl.BlockSpec((FFN_TF, D), lambda i, j: (j, 0))
    gt = pl.BlockSpec((FFN_TS, FFN_TF), lambda i, j: (i, j))
    in_specs = [xt, row, wt, wt, wt] + ([xt] if with_loss else [])
    out_specs = [xt] + ([pl.BlockSpec((8, 128), lambda i, j: (0, 0))] if with_loss else []) + [xt, gt, gt]
    out_shape = ([jax.ShapeDtypeStruct((S, D), F32)] + ([jax.ShapeDtypeStruct((8, 128), F32)] if with_loss else [])
                 + [jax.ShapeDtypeStruct((S, D), BF16), jax.ShapeDtypeStruct((S, F), F32),
                    jax.ShapeDtypeStruct((S, F), F32)])
    args = (x, g, wg_t, wu_t, wd) + ((target,) if with_loss else ())
    return _call_hosting(
        body, ex, _grid_phases((ni, nj), 0.8), name="ffn_fwd_loss" if with_loss else "ffn_fwd", grid=(ni, nj),
        in_specs=in_specs, out_specs=out_specs, out_shape=out_shape,
        scratch_shapes=[pltpu.VMEM((FFN_TS, D), BF16), pltpu.VMEM((FFN_TS, D), F32)], args=args,
        compiler_params=_params("arbitrary", "arbitrary"),
    )


def _ffn_bwd_weights(dout, h, gg, uu, wd, name, ex=None):
    nj = F // FFN_TF

    def body(do_ref, h_ref, gg_ref, uu_ref, wd_ref, dg_ref, du_ref, dwg_ref, dwu_ref, dwd_ref, dob):
        @pl.when(pl.program_id(0) == 0)
        def _():
            dob[...] = do_ref[...].astype(BF16)

        hb, dov = h_ref[...], dob[...]
        gv, uv = gg_ref[...], uu_ref[...]
        da = _nt(dov, wd_ref[...])
        sg = jax.nn.sigmoid(gv)
        sl = gv * sg
        ab = (sl * uv).astype(BF16)
        dub = (da * sl).astype(BF16)
        dgb = (da * uv * (sg * (1.0 + gv * (1.0 - sg)))).astype(BF16)
        dg_ref[...] = dgb
        du_ref[...] = dub
        dwg_ref[...] = _tn(dgb, hb).astype(BF16)
        dwu_ref[...] = _tn(dub, hb).astype(BF16)
        dwd_ref[...] = _tn(ab, dov).astype(BF16)

    once = pl.Buffered(1)
    whole = lambda: pl.BlockSpec((S, D), lambda j, _: (0, 0), pipeline_mode=once)
    wt = pl.BlockSpec((FFN_TF, D), lambda j, _: (j, 0))
    gt = pl.BlockSpec((S, FFN_TF), lambda j, _: (0, j))
    return _call_hosting(
        body, ex, _grid_phases((nj, 1)), name=name, grid=(nj, 1),
        in_specs=[whole(), whole(), gt, gt, wt], out_specs=[gt, gt, wt, wt, wt],
        out_shape=[jax.ShapeDtypeStruct((S, F), BF16)] * 2 + [jax.ShapeDtypeStruct((F, D), BF16)] * 3,
        scratch_shapes=[pltpu.VMEM((S, D), BF16)], args=(dout, h, gg, uu, wd),
        compiler_params=_params("arbitrary", "arbitrary"),
    )


BWD_TS = 512


def _ffn_bwd_input(dres, x, g, dg, du, wg_t, wu_t, name, ex=None):
    nt = S // BWD_TS

    def body(dres_ref, x_ref, g_ref, dg_ref, du_ref, wg_ref, wu_ref, dx_ref, dgam_ref):
        dh = _nn(dg_ref[...], wg_ref[...]) + _nn(du_ref[...], wu_ref[...])
        xv = x_ref[...]
        r = _rstd(xv)
        dx_ref[...] = dres_ref[...] + _rms_bwd(dh, xv, r, g_ref[...])

        @pl.when(pl.program_id(0) == 0)
        def _():
            dgam_ref[...] = jnp.zeros(dgam_ref.shape, F32)

        dgam_ref[...] += _colsum8(dh * xv * r)

    tile = pl.BlockSpec((BWD_TS, D), lambda i: (i, 0))
    ftile = pl.BlockSpec((BWD_TS, F), lambda i: (i, 0))
    wspec = lambda: pl.BlockSpec((F, D), lambda i: (0, 0), pipeline_mode=pl.Buffered(1))
    return _call_hosting(
        body, ex, _grid_phases((nt,)), name=name, grid=(nt,),
        in_specs=[tile, tile, pl.BlockSpec((1, D), lambda i: (0, 0)), ftile, ftile, wspec(), wspec()],
        out_specs=[tile, pl.BlockSpec((8, D), lambda i: (0, 0))],
        out_shape=[jax.ShapeDtypeStruct((S, D), F32), jax.ShapeDtypeStruct((8, D), F32)],
        scratch_shapes=[], args=(dres, x, g, dg, du, wg_t, wu_t), compiler_params=_params("arbitrary"),
    )


NORM_TS = 256


def _norm_bwd(dres, x, dhs, gs, name, ex=None):
    n = len(dhs)
    nt = S // NORM_TS

    def body(*refs):
        dres_ref, x_ref = refs[:2]
        dh_refs, g_refs = refs[2:2 + n], refs[2 + n:2 + 2 * n]
        dx_ref, dg_refs = refs[2 + 2 * n], refs[3 + 2 * n:]
        i = pl.program_id(0)
        xv = x_ref[...]
        r = _rstd(xv)
        dx = dres_ref[...]
        for k in range(n):
            dhv = dh_refs[k][...]
            dx = dx + _rms_bwd(dhv, xv, r, g_refs[k][...])

            @pl.when(i == 0)
            def _():
                dg_refs[k][...] = jnp.zeros((8, D), F32)

            dg_refs[k][...] += _colsum8(dhv * xv * r)
        dx_ref[...] = dx

    tile = pl.BlockSpec((NORM_TS, D), lambda i: (i, 0))
    row = pl.BlockSpec((1, D), lambda i: (0, 0))
    acc = pl.BlockSpec((8, D), lambda i: (0, 0))
    return _call_hosting(
        body, ex, _grid_phases((nt,)), name=name, grid=(nt,),
        in_specs=[tile, tile] + [tile] * n + [row] * n,
        out_specs=[tile] + [acc] * n,
        out_shape=[jax.ShapeDtypeStruct((S, D), F32)] + [jax.ShapeDtypeStruct((8, D), F32)] * n,
        scratch_shapes=[], args=(dres, x, *dhs, *gs), compiler_params=_params("arbitrary"),
    )


def _mm(a, b, mode, out_dtype, name, add=None):
    if mode == "nn":
        (m, kd), n = a.shape, b.shape[1]
    elif mode == "nt":
        (m, kd), n = a.shape, b.shape[0]
    else:
        (kd, m), n = a.shape, b.shape[1]
    tm, tn, tk = min(m, 1024), min(n, 1024), min(kd, 1024)
    nk = kd // tk
    dot = {"nn": _nn, "nt": _nt, "tn": _tn}[mode]

    def body(*refs):
        if add is None:
            a_ref, b_ref, o_ref, acc = refs
        else:
            a_ref, b_ref, add_ref, o_ref, acc = refs
        k = pl.program_id(2)

        @pl.when(k == 0)
        def _():
            acc[...] = jnp.zeros(acc.shape, F32)

        acc[...] += dot(a_ref[...].astype(BF16), b_ref[...].astype(BF16))

        @pl.when(k == nk - 1)
        def _():
            res = acc[...]
            if add is not None:
                res = res + add_ref[...]
            o_ref[...] = res.astype(out_dtype)

    if mode == "tn":
        a_spec = pl.BlockSpec((tk, tm), lambda i, j, k: (k, i))
        b_spec = pl.BlockSpec((tk, tn), lambda i, j, k: (k, j))
    else:
        a_spec = pl.BlockSpec((tm, tk), lambda i, j, k: (i, k))
        b_spec = (pl.BlockSpec((tk, tn), lambda i, j, k: (k, j)) if mode == "nn"
                  else pl.BlockSpec((tn, tk), lambda i, j, k: (j, k)))
    o_spec = pl.BlockSpec((tm, tn), lambda i, j, k: (i, j))
    in_specs = [a_spec, b_spec] + ([o_spec] if add is not None else [])
    args = (a, b) + ((add,) if add is not None else ())
    return _call(
        body, name=name, grid=(m // tm, n // tn, nk), in_specs=in_specs, out_specs=o_spec,
        out_shape=jax.ShapeDtypeStruct((m, n), out_dtype), scratch_shapes=[pltpu.VMEM((tm, tn), F32)],
        compiler_params=_params("parallel", "parallel", "arbitrary"),
    )(*args)


PROJ_TS = 256


def _kvq_proj(x, g_kv, g_mix, wk, wv, wq, gk, gq):
    def body(x_ref, gkv_ref, gmix_ref, wk_ref, wv_ref, wq_ref, gk_ref, gq_ref,
             hkv_ref, h1_ref, kpre_ref, qpre_ref, k_ref, v_ref, q_ref):
        xv = x_ref[...]
        xr = xv * _rstd(xv)
        hkv = (xr * gkv_ref[...]).astype(BF16)
        h1 = (xr * gmix_ref[...]).astype(BF16)
        hkv_ref[...] = hkv
        h1_ref[...] = h1
        kpre = _nn(hkv, wk_ref[...])
        qpre = _nn(h1, wq_ref[...])
        kpre_ref[...] = kpre
        qpre_ref[...] = qpre
        v_ref[...] = _nn(hkv, wv_ref[...]).astype(BF16)
        rk = lax.rsqrt(_seg_sum(kpre * kpre) * (1.0 / HEAD) + EPS)
        k_ref[...] = (kpre * rk * gk_ref[...]).astype(BF16)
        rq = lax.rsqrt(_seg_sum(qpre * qpre) * (1.0 / HEAD) + EPS)
        q_ref[...] = (qpre * rq * gq_ref[...]).astype(BF16)

    tile = pl.BlockSpec((PROJ_TS, D), lambda i: (i, 0))
    row = pl.BlockSpec((1, D), lambda i: (0, 0))
    wspec = pl.BlockSpec((D, D), lambda i: (0, 0))
    bf = jax.ShapeDtypeStruct((S, D), BF16)
    ff = jax.ShapeDtypeStruct((S, D), F32)
    return _call(
        body, name="kvq_proj", grid=(S // PROJ_TS,),
        in_specs=[tile, row, row, wspec, wspec, wspec, row, row],
        out_specs=[tile] * 7, out_shape=[bf, bf, ff, ff, bf, bf, bf],
        compiler_params=_params("arbitrary"),
    )(x, g_kv, g_mix, wk, wv, wq, gk, gq)


def _head_norm_bwd(dout, pre, hg, name, row_off=0):
    nt = S // PROJ_TS

    def body(do_ref, pre_ref, hg_ref, dpre_ref, dhg_ref, acc):
        i = pl.program_id(0)
        dov, pv, hgv = do_ref[...], pre_ref[...], hg_ref[...]
        r = lax.rsqrt(_seg_sum(pv * pv) * (1.0 / HEAD) + EPS)
        gd = dov * hgv
        dpre = r * gd - pv * (r * r * r) * (_seg_sum(gd * pv) * (1.0 / HEAD))
        dpre_ref[...] = dpre.astype(BF16)

        @pl.when(i == 0)
        def _():
            acc[...] = jnp.zeros(acc.shape, F32)

        acc[...] += _colsum8(dov * pv * r)

        @pl.when(i == nt - 1)
        def _():
            full = acc[...]
            fold = full[:, 0:128]
            for blk in range(1, D // 128):
                fold = fold + full[:, blk * 128:(blk + 1) * 128]
            dhg_ref[...] = fold + pltpu.roll(fold, HEAD, axis=1)

    tile = pl.BlockSpec((PROJ_TS, D), lambda i: (i, 0))
    return _call(
        body, name=name, grid=(nt,),
        in_specs=[pl.BlockSpec((PROJ_TS, D), lambda i: (i + row_off, 0)), tile, pl.BlockSpec((1, D), lambda i: (0, 0))],
        out_specs=[tile, pl.BlockSpec((8, 128), lambda i: (0, 0))],
        out_shape=[jax.ShapeDtypeStruct((S, D), BF16), jax.ShapeDtypeStruct((8, 128), F32)],
        scratch_shapes=[pltpu.VMEM((8, D), F32)],
        compiler_params=_params("arbitrary"),
    )(dout, pre, hg)


def _toeplitz_from_table(table):
    far = jnp.broadcast_to(table[:, N_REL - 1:], (N_HEADS, PADK - MAX_REL + 1))
    near = table[:, N_REL - 2::-1]
    past = jnp.broadcast_to(table[:, 0:1], (N_HEADS, MAX_REL))
    wrap = jnp.broadcast_to(table[:, N_REL - 1:], (N_HEADS, QB - 1))
    return jnp.concatenate([far, near, past, wrap], axis=1).reshape(N_HEADS, 1, TOEP)


def _table_grad_from_toeplitz(dtp, seg):
    lo = PADK - MAX_REL + 1
    near = dtp[:, lo + N_REL - 3:lo - 1:-1]
    return jnp.concatenate([seg[:, 1:2], near, seg[:, 0:1]], axis=1)


def _bias_band(tp):
    def body(tp_ref, out_ref):
        bv = pltpu.roll(jnp.broadcast_to(tp_ref[0], (QB, TOEP)), 0, axis=1, stride=1, stride_axis=0)
        out_ref[0] = bv[:, 0:KB]

    return _call(
        body, name="bias_band", grid=(N_HEADS,),
        in_specs=[pl.BlockSpec((1, 1, TOEP), lambda h: (h, 0, 0))],
        out_specs=pl.BlockSpec((1, QB, KB), lambda h: (h, 0, 0)),
        out_shape=jax.ShapeDtypeStruct((N_HEADS, QB, KB), F32),
        compiler_params=_params("parallel"),
    )(tp)


def _bias_grad(dband, after):
    lo, hi = PADK - MAX_REL + 1, PADK + MAX_REL

    def body(db_ref, _, dtp_ref, seg_ref):
        bv = jnp.concatenate([db_ref[0], jnp.zeros((QB, TOEP - KB), F32)], axis=1)
        row = lax.broadcasted_iota(jnp.int32, (QB, TOEP), 0)
        k = 1
        while k < QB:
            bv = jnp.where((row & k) != 0, pltpu.roll(bv, TOEP - k, axis=1), bv)
            k *= 2
        col = jnp.sum(bv, axis=0, keepdims=True)
        dtp_ref[0] = col
        u = lax.broadcasted_iota(jnp.int32, (1, TOEP), 1)
        far = jnp.sum(jnp.where((u < lo) | (u > hi + MAX_REL), col, 0.0))
        past = jnp.sum(jnp.where((u >= hi) & (u <= hi + MAX_REL), col, 0.0))
        lane = lax.broadcasted_iota(jnp.int32, (1, 128), 1)
        seg_ref[0] = jnp.where(lane == 0, far, jnp.where(lane == 1, past, 0.0))

    return _call(
        body, name="bias_grad", grid=(N_HEADS,),
        in_specs=[pl.BlockSpec((1, QB, KB), lambda h: (h, 0, 0)), ANY],
        out_specs=[pl.BlockSpec((1, 1, TOEP), lambda h: (h, 0, 0)), pl.BlockSpec((1, 1, 128), lambda h: (h, 0, 0))],
        out_shape=[jax.ShapeDtypeStruct((N_HEADS, 1, TOEP), F32), jax.ShapeDtypeStruct((N_HEADS, 1, 128), F32)],
        compiler_params=_params("parallel"),
    )(dband, after)


N_QB = S // QB
N_HP = D // 128


def _band_mask(cb):
    qc = lax.broadcasted_iota(jnp.int32, (QB, KB), 0) >> 6
    p = lax.broadcasted_iota(jnp.int32, (QB, KB), 1)
    kc = p >> 6
    return (kc >= qc) & (kc <= qc + LEFT) & (p + cb * QB >= PADK)


def _half_mask(hh):
    lane = lax.broadcasted_iota(jnp.int32, (1, 128), 1)
    return jnp.where((lane < HEAD) == (hh == 0), 1.0, 0.0).astype(BF16)


def _probs(qh, kb, bias, mask):
    sc = _nt(qh, kb) * ATTN_SCALE + bias
    sc = jnp.where(mask, sc, NEG_INF)
    e = jnp.exp(sc - jnp.max(sc, axis=-1, keepdims=True))
    return e / jnp.sum(e, axis=-1, keepdims=True)


def _attn_fwd(q, kp, vp, bias, ex=None):
    def body(q_ref, k_ref, v_ref, b_ref, o_ref):
        cb = pl.program_id(1)
        band = pl.ds(pl.multiple_of(cb * QB, QB), KB)
        kb, vb = k_ref[band, :], v_ref[band, :]
        qv = q_ref[...]
        mask = _band_mask(cb)
        low = lax.broadcasted_iota(jnp.int32, (QB, 128), 1) < HEAD
        outs = []
        for hh in range(2):
            pb = _probs(qv * _half_mask(hh), kb, b_ref[hh], mask).astype(BF16)
            outs.append(_nn(pb, vb))
        o_ref[...] = jnp.where(low, outs[0], outs[1]).astype(BF16)

    qspec = pl.BlockSpec((QB, 128), lambda hp, cb: (cb, hp))
    kspec = pl.BlockSpec((PADK + S, 128), lambda hp, cb: (0, hp))
    return _call_hosting(
        body, ex, _grid_phases((N_HP, N_QB), 0.85), name="attn_fwd", grid=(N_HP, N_QB),
        in_specs=[qspec, kspec, kspec, pl.BlockSpec((2, QB, KB), lambda hp, cb: (hp, 0, 0))],
        out_specs=[qspec], out_shape=[jax.ShapeDtypeStruct((S, D), BF16)], scratch_shapes=[],
        args=(q, kp, vp, bias), compiler_params=_params("arbitrary", "arbitrary"),
    )


def _attn_bwd(q, kp, vp, bias, do, ex=None):
    def body(q_ref, k_ref, v_ref, b_ref, do_ref, dq_ref, dk_ref, dv_ref, db_ref):
        cb = pl.program_id(1)

        @pl.when(cb == 0)
        def _():
            dk_ref[...] = jnp.zeros(dk_ref.shape, F32)
            dv_ref[...] = jnp.zeros(dv_ref.shape, F32)
            db_ref[...] = jnp.zeros(db_ref.shape, F32)

        band = pl.ds(pl.multiple_of(cb * QB, QB), KB)
        kb, vb = k_ref[band, :], v_ref[band, :]
        qv, dov = q_ref[...], do_ref[...]
        mask = _band_mask(cb)
        low = lax.broadcasted_iota(jnp.int32, (QB, 128), 1) < HEAD
        dq = jnp.zeros((QB, 128), F32)
        dkb = jnp.zeros((KB, 128), F32)
        dvb = jnp.zeros((KB, 128), F32)
        for hh in range(2):
            sel = low if hh == 0 else jnp.logical_not(low)
            qh = qv * _half_mask(hh)
            doh = dov * _half_mask(hh)
            p = _probs(qh, kb, b_ref[hh], mask)
            dp = _nt(doh, vb)
            dvb = dvb + _tn(p.astype(BF16), doh)
            ds = p * (dp - jnp.sum(dp * p, axis=-1, keepdims=True))
            db_ref[hh] += ds
            dsb = (ds * ATTN_SCALE).astype(BF16)
            dq = dq + jnp.where(sel, _nn(dsb, kb), 0.0)
            dkb = dkb + _tn(dsb, qh)
        dq_ref[...] = dq
        dk_ref[band, :] += dkb
        dv_ref[band, :] += dvb

    qspec = pl.BlockSpec((QB, 128), lambda hp, cb: (cb, hp))
    kspec = pl.BlockSpec((PADK + S, 128), lambda hp, cb: (0, hp))
    bspec = pl.BlockSpec((2, QB, KB), lambda hp, cb: (hp, 0, 0))
    kf = jax.ShapeDtypeStruct((PADK + S, D), F32)
    return _call_hosting(
        body, ex, _grid_phases((N_HP, N_QB)), name="attn_bwd", grid=(N_HP, N_QB),
        in_specs=[qspec, kspec, kspec, bspec, qspec],
        out_specs=[qspec, kspec, kspec, bspec],
        out_shape=[jax.ShapeDtypeStruct((S, D), F32), kf, kf, jax.ShapeDtypeStruct((N_HEADS, QB, KB), F32)],
        scratch_shapes=[], args=(q, kp, vp, bias, do), compiler_params=_params("arbitrary", "arbitrary"),
    )


def _adam_math(w, g, m, v):
    m = ADAM_B1 * m + (1.0 - ADAM_B1) * g
    v = ADAM_B2 * v + (1.0 - ADAM_B2) * (g * g)
    m_hat = m / (1.0 - ADAM_B1 ** ADAM_STEP)
    v_hat = v / (1.0 - ADAM_B2 ** ADAM_STEP)
    delta = -ADAM_LR * (m_hat / (jnp.sqrt(v_hat) + ADAM_EPS) + ADAM_WD * w)
    return delta, m, v


def _sum_parts(parts, name, after=None):
    count, r, c = parts.shape
    tr = r // 2 if r % 16 == 0 and r > 64 else r

    def body(p_ref, *rest):
        o_ref = rest[-1]
        acc = p_ref[0].astype(F32)
        for d in range(1, count):
            acc = acc + p_ref[d].astype(F32)
        o_ref[...] = acc

    return _call(
        body, name=name, grid=(r // tr,),
        in_specs=[pl.BlockSpec((count, tr, c), lambda i: (0, i, 0))] + ([ANY] if after is not None else []),
        out_specs=pl.BlockSpec((tr, c), lambda i: (i, 0)),
        out_shape=jax.ShapeDtypeStruct((r, c), F32), compiler_params=_params("parallel"),
    )(parts, *([after] if after is not None else []))


def _row_tile(r):
    for cand in (256, 176, 128, 64, 32, 16, 8):
        if r % cand == 0:
            return cand
    return r


def _adam_layer(w, g, m, v, layer, name, prev=None):
    nl, r, c = w.shape
    tr = _row_tile(r)

    def body(w_ref, g_ref, m_ref, v_ref, *rest):
        go_ref, d_ref, nm_ref, nv_ref = rest[-4:]
        gv = g_ref[...]
        delta, nm, nv = _adam_math(w_ref[0], gv, m_ref[0], v_ref[0])
        go_ref[0] = gv
        d_ref[0] = delta
        nm_ref[0] = nm
        nv_ref[0] = nv

    lspec = pl.BlockSpec((1, tr, c), lambda i: (layer, i, 0))
    sd = jax.ShapeDtypeStruct((nl, r, c), F32)
    extra = list(prev) if prev is not None else []
    return _call(
        body, name=name, grid=(r // tr,),
        in_specs=[lspec, pl.BlockSpec((tr, c), lambda i: (i, 0)), lspec, lspec] + [ANY] * len(extra),
        out_specs=[lspec] * 4, out_shape=[sd] * 4,
        input_output_aliases={4 + t: t for t in range(len(extra))},
        compiler_params=_params("parallel"),
    )(w, g, m, v, *extra)


def _adam(w, g, m, v, name):
    r, c = w.shape
    tr = _row_tile(r)

    def body(w_ref, g_ref, m_ref, v_ref, d_ref, nm_ref, nv_ref):
        delta, nm, nv = _adam_math(w_ref[...], g_ref[...], m_ref[...], v_ref[...])
        d_ref[...] = delta
        nm_ref[...] = nm
        nv_ref[...] = nv

    spec = pl.BlockSpec((tr, c), lambda i: (i, 0))
    sd = jax.ShapeDtypeStruct((r, c), F32)
    return _call(
        body, name=name, grid=(r // tr,), in_specs=[spec] * 4, out_specs=[spec] * 3, out_shape=[sd] * 3,
        compiler_params=_params("parallel"),
    )(w, g, m, v)


def _pad8(rows):
    return jnp.pad(rows, ((0, 8 - rows.shape[0]), (0, 0)))


def kernel(x, norm_mix_g, norm_ffn_g, pool_w, pool_b, pool_scale, kv_norm_g, w_k, w_v, k_norm_g, w_q, q_norm_g, rel_bias, w_o, w_gate, w_up, w_down, loss_target, m_norm_mix_g, m_norm_ffn_g, m_pool_w, m_pool_b, m_pool_scale, m_kv_norm_g, m_w_k, m_w_v, m_k_norm_g, m_w_q, m_q_norm_g, m_rel_bias, m_w_o, m_w_gate, m_w_up, m_w_down, v_norm_mix_g, v_norm_ffn_g, v_pool_w, v_pool_b, v_pool_scale, v_kv_norm_g, v_w_k, v_w_v, v_k_norm_g, v_w_q, v_q_norm_g, v_rel_bias, v_w_o, v_w_gate, v_w_up, v_w_down):
    assert x.shape == (1, S, D) and w_gate.shape == (2, D, F_SHARD) and w_k.shape == (D_SHARD, D)
    xin, target = x[0], loss_target[0]

    ffn_shards = [[w_gate[layer].T.astype(BF16), w_up[layer].T.astype(BF16), w_down[layer].astype(BF16)]
                  for layer in range(2)]
    att_shards = [w_k.astype(BF16), w_v.astype(BF16), w_q[0].astype(BF16), w_o[0].astype(BF16)]
    pool_shard = pool_w[0].astype(BF16).reshape(N_GROUPS * POOL_SHARD, GROUP)
    small = jnp.concatenate([pool_b[0].reshape(1, N_GROUPS * POOL_SHARD), pool_scale], axis=1)

    full0 = _run_exchange(_gather_exchange(ffn_shards[0] + [pool_shard, _pad8(small)]), "gather_layer0")
    ffn_w0 = [a.reshape(F, D) for a in full0[:3]]
    pw_f = full0[3].reshape(N_DEV, N_GROUPS, POOL_SHARD, GROUP).transpose(1, 0, 2, 3).reshape(N_GROUPS, GROUP, GROUP)
    small_f = full0[4][:, 0, :]
    pb_f = small_f[:, :N_GROUPS * POOL_SHARD].reshape(N_DEV, N_GROUPS, POOL_SHARD).transpose(1, 0, 2).reshape(1, D)
    ps_f = small_f[:, N_GROUPS * POOL_SHARD:].reshape(1, D)

    g_mix0, g_mix1 = norm_mix_g[0:1], norm_mix_g[1:2]
    g_ffn0, g_ffn1 = norm_ffn_g[0:1], norm_ffn_g[1:2]
    g_kv = kv_norm_g.reshape(1, D)
    gk_t = jnp.tile(k_norm_g.reshape(1, HEAD), (1, N_HEADS))
    gq_t = jnp.tile(q_norm_g.reshape(1, HEAD), (1, N_HEADS))

    x1, diff = _pool_fwd(xin, g_mix0, pw_f, pb_f, ps_f)
    (x2, hf0, gg0, uu0), full_att = _ffn_fwd(x1, g_ffn0, *ffn_w0, ex=_gather_exchange(att_shards))
    wk_f, wv_f, wq_f, wo_f = [a.reshape(D, D) for a in full_att]
    hkv, h1, kpre, qpre, kk, vv, qq = _kvq_proj(x2, g_kv, g_mix1, wk_f, wv_f, wq_f, gk_t, gq_t)
    kp = jnp.pad(kk, ((PADK, 0), (0, 0)))
    vp = jnp.pad(vv, ((PADK, 0), (0, 0)))
    bias = _bias_band(_toeplitz_from_table(rel_bias[0]))
    (att,), full1 = _attn_fwd(qq, kp, vp, bias, ex=_gather_exchange(ffn_shards[1]))
    ffn_w1 = [a.reshape(F, D) for a in full1]
    x3 = _mm(att, wo_f, "nn", F32, "attn_out", add=x2)
    (dx4, loss_rows, hf1, gg1, uu1), _ = _ffn_fwd(x3, g_ffn1, *ffn_w1, target=target)

    def blocks(dw):
        return dw.reshape(N_DEV, dw.shape[0] // N_DEV, dw.shape[1])

    (dgg1, duu1, dwg1, dwu1, dwd1), _ = _ffn_bwd_weights(dx4, hf1, gg1, uu1, ffn_w1[2], name="ffn_bwd1")
    parts1 = [blocks(dw) for dw in (dwg1, dwu1, dwd1)]
    (dx3, dg_ffn1), stage1 = _ffn_bwd_input(dx4, x3, g_ffn1, dgg1, duu1, ffn_w1[0], ffn_w1[1], "ffn_dx1",
                                            ex=_pair_exchange(parts1))
    chip1 = _pair_add(parts1, stage1, "pair_add_ffn1")
    datt = _mm(dx3, wo_f, "nt", BF16, "d_attn")
    dwo = _mm(att, dx3, "tn", BF16, "d_wo")
    (dq, dkp, dvp, dband), recv1 = _attn_bwd(qq, kp, vp, bias, datt, ex=_chip_exchange(chip1))
    dqpre, dgq = _head_norm_bwd(dq, qpre, gq_t, "q_norm_bwd")
    dkpre, dgk = _head_norm_bwd(dkp, kpre, gk_t, "k_norm_bwd", row_off=PADK // PROJ_TS)
    dh1 = _mm(dqpre, wq_f, "nt", F32, "d_h1")
    dwq = _mm(h1, dqpre, "tn", BF16, "d_wq")
    dhkv = _mm(dkpre, wk_f, "nt", F32, "d_hkv_k")
    dv = dvp[PADK:]
    dhkv = _mm(dv, wv_f, "nt", F32, "d_hkv_v", add=dhkv)
    dwk = _mm(hkv, dkpre, "tn", BF16, "d_wk")
    dwv = _mm(hkv, dv, "tn", BF16, "d_wv")
    parts_att = [blocks(dw) for dw in (dwk, dwv, dwq, dwo)]
    (dx2, dg_mix1, dg_kv), stage_att = _norm_bwd(dx3, x2, [dh1, dhkv], [g_mix1, g_kv], "norm_bwd_mix1",
                                                 ex=_pair_exchange(parts_att))
    chip_att = _pair_add(parts_att, stage_att, "pair_add_att")
    (dgg0, duu0, dwg0, dwu0, dwd0), recv_att = _ffn_bwd_weights(
        dx2, hf0, gg0, uu0, ffn_w0[2], name="ffn_bwd0", ex=_chip_exchange(chip_att))
    parts0 = [blocks(dw) for dw in (dwg0, dwu0, dwd0)]
    (dx1, dg_ffn0), stage0 = _ffn_bwd_input(dx2, x1, g_ffn0, dgg0, duu0, ffn_w0[0], ffn_w0[1], "ffn_dx0",
                                            ex=_pair_exchange(parts0))
    both = _pair_add(parts0, stage0, "pair_add_ffn0", copies=2)
    flight, chip0, land0, token = _chip_exchange_start(both[:3], both[3:], dx1, "scatter_ffn0_start")
    (grad_x, dpw, db_rows, ds_rows, dg_mix0), _ = _pool_bwd(dx1, xin, diff, g_mix0 + token[0:1, 0:1], pw_f, pb_f, ps_f)
    dtp, seg = _bias_grad(dband, token)

    weights = dict(norm_mix_g=norm_mix_g, norm_ffn_g=norm_ffn_g, pool_w=pool_w, pool_b=pool_b,
                   pool_scale=pool_scale, kv_norm_g=kv_norm_g, w_k=w_k, w_v=w_v, k_norm_g=k_norm_g, w_q=w_q,
                   q_norm_g=q_norm_g, rel_bias=rel_bias, w_o=w_o, w_gate=w_gate, w_up=w_up, w_down=w_down)
    mom1 = dict(norm_mix_g=m_norm_mix_g, norm_ffn_g=m_norm_ffn_g, pool_w=m_pool_w, pool_b=m_pool_b,
                pool_scale=m_pool_scale, kv_norm_g=m_kv_norm_g, w_k=m_w_k, w_v=m_w_v, k_norm_g=m_k_norm_g,
                w_q=m_w_q, q_norm_g=m_q_norm_g, rel_bias=m_rel_bias, w_o=m_w_o, w_gate=m_w_gate, w_up=m_w_up,
                w_down=m_w_down)
    mom2 = dict(norm_mix_g=v_norm_mix_g, norm_ffn_g=v_norm_ffn_g, pool_w=v_pool_w, pool_b=v_pool_b,
                pool_scale=v_pool_scale, kv_norm_g=v_kv_norm_g, w_k=v_w_k, w_v=v_w_v, k_norm_g=v_k_norm_g,
                w_q=v_w_q, q_norm_g=v_q_norm_g, rel_bias=v_rel_bias, w_o=v_w_o, w_gate=v_w_gate, w_up=v_w_up,
                w_down=v_w_down)
    names = list(weights)
    grads, deltas, new_m, new_v = {}, {}, {}, {}

    def summed(arrs, tag, after):
        return [_sum_parts(a, "sum_parts_%s%d" % (tag, k), after) for k, a in enumerate(arrs)]

    def adam_flat(nm):
        shape = weights[nm].shape
        flat = lambda a: a.reshape(-1, shape[-1])
        dl, m1, m2 = _adam(flat(weights[nm]), flat(grads[nm]), flat(mom1[nm]), flat(mom2[nm]), "adam_" + nm)
        deltas[nm], new_m[nm], new_v[nm] = dl.reshape(shape), m1.reshape(shape), m2.reshape(shape)

    ffn_names = ("w_gate", "w_up", "w_down")
    sums1 = summed(recv1, "ffn1_", token)
    layer1 = {nm: _adam_layer(weights[nm], g, mom1[nm], mom2[nm], 1, "adam1_" + nm)
              for nm, g in zip(ffn_names, (sums1[0].T, sums1[1].T, sums1[2]))}
    g_wk, g_wv, g_wq, g_wo = summed(recv_att, "att_", token)
    grads.update(w_k=g_wk, w_v=g_wv, w_q=g_wq[None], w_o=g_wo[None])
    for nm in ("w_k", "w_v", "w_q", "w_o"):
        adam_flat(nm)

    dpw_blocks = dpw.reshape(N_GROUPS, N_DEV, POOL_SHARD, GROUP).transpose(1, 0, 2, 3)
    dpw_blocks = dpw_blocks.reshape(N_DEV, N_GROUPS * POOL_SHARD, GROUP).astype(BF16)
    recv_pool = _run_exchange(_scatter_exchange([dpw_blocks]), "scatter_pool", after=deltas["w_o"])
    misc = jnp.concatenate([dgk[0:1, 0:HEAD], dgq[0:1, 0:HEAD], loss_rows[0:1, 0:1],
                            seg[:, 0, 0].reshape(1, N_HEADS), seg[:, 0, 1].reshape(1, N_HEADS)], axis=1)
    misc = jnp.pad(misc, ((0, 0), (0, D - misc.shape[1])))
    vec_rows = jnp.concatenate([dg_mix0[0:1], dg_mix1[0:1], dg_ffn0[0:1], dg_ffn1[0:1], dg_kv[0:1],
                                db_rows[0:1], ds_rows[0:1], misc], axis=0)
    pack = jnp.concatenate([vec_rows, dtp.reshape(N_HEADS, TOEP)], axis=0)
    tot = _all_reduce_small(pack, "reduce_small", after=recv_pool[0])

    loss = tot[7, 2 * HEAD]
    seg_tot = jnp.stack([tot[7, 2 * HEAD + 1:2 * HEAD + 1 + N_HEADS],
                         tot[7, 2 * HEAD + 1 + N_HEADS:2 * HEAD + 1 + 2 * N_HEADS]], axis=1)
    me = 4 * lax.axis_index("x") + 2 * lax.axis_index("y") + lax.axis_index("c")
    g_pool_b = lax.dynamic_slice_in_dim(tot[5].reshape(N_GROUPS, GROUP), me * POOL_SHARD, POOL_SHARD, axis=1)
    grads.update(
        norm_mix_g=tot[0:2], norm_ffn_g=tot[2:4], kv_norm_g=tot[4], k_norm_g=tot[7, 0:HEAD],
        q_norm_g=tot[7, HEAD:2 * HEAD].reshape(1, HEAD),
        rel_bias=_table_grad_from_toeplitz(tot[8:8 + N_HEADS], seg_tot).reshape(1, N_HEADS, N_REL),
        pool_b=g_pool_b.reshape(1, N_GROUPS, POOL_SHARD),
        pool_scale=lax.dynamic_slice_in_dim(tot[6:7], me * D_SHARD, D_SHARD, axis=1),
        pool_w=summed(recv_pool, "pool_", None)[0].reshape(1, N_GROUPS, POOL_SHARD, GROUP))
    adam_flat("pool_w")
    small_names = [nm for nm in names if nm not in ffn_names + ("w_k", "w_v", "w_q", "w_o", "pool_w")]

    def pack_small(tree):
        cols = []
        for nm in small_names:
            flat = tree[nm].reshape(-1)
            cols.append(jnp.pad(flat, (0, -flat.shape[0] % 1024)))
        return jnp.concatenate(cols).reshape(-1, 128)

    dl, m1, m2 = _adam(pack_small(weights), pack_small(grads), pack_small(mom1), pack_small(mom2), "adam_small")

    def unpack_small(packed, out):
        flat, off = packed.reshape(-1), 0
        for nm in small_names:
            size = weights[nm].size
            out[nm] = flat[off:off + size].reshape(weights[nm].shape)
            off += size + (-size % 1024)

    unpack_small(dl, deltas)
    unpack_small(m1, new_m)
    unpack_small(m2, new_v)

    recv0 = _chip_exchange_wait(flight, chip0, land0, dl, "scatter_ffn0_wait")
    sums0 = summed(recv0, "ffn0_", None)
    for nm, g in zip(ffn_names, (sums0[0].T, sums0[1].T, sums0[2])):
        grads[nm], deltas[nm], new_m[nm], new_v[nm] = _adam_layer(
            weights[nm], g, mom1[nm], mom2[nm], 0, "adam0_" + nm, prev=layer1[nm])

    return (loss, grad_x[None], *[grads[nm] for nm in names], *[deltas[nm] for nm in names],
            *[new_m[nm] for nm in names], *[new_v[nm] for nm in names])
```

```python
import functools

import jax
import jax.numpy as jnp
from jax import lax
from jax.experimental import pallas as pl
from jax.experimental.pallas import tpu as pltpu

F32 = jnp.float32
BF16 = jnp.bfloat16
MESH_ID = pl.DeviceIdType.MESH

N_DEV = 8
S = 2048
D = 1024
F = 2816
F_SHARD = F // N_DEV
D_SHARD = D // N_DEV
N_GROUPS = 4
GROUP = D // N_GROUPS
POOL_SHARD = GROUP // N_DEV
MAX_WIN = 16
HEAD = 64
N_HEADS = D // HEAD
CHUNK = 64
LEFT = 8
QB = 4 * CHUNK
KB = QB + LEFT * CHUNK
PADK = LEFT * CHUNK
TOEP = 1024
N_REL = 257
MAX_REL = 128
EPS = 1e-6
NEG_INF = -1e30
ATTN_SCALE = HEAD ** -0.5

ADAM_LR = 0.001
ADAM_B1 = 0.9
ADAM_B2 = 0.999
ADAM_EPS = 1e-08
ADAM_WD = 0.01
ADAM_STEP = 10

VMEM_LIMIT = 52 * 1024 * 1024

ANY = pl.BlockSpec(memory_space=pl.ANY)
VMEM = pl.BlockSpec(memory_space=pltpu.VMEM)


def _call(body, **kw):
    return pl.pallas_call(body, **kw)


def _params(*sem):
    return pltpu.CompilerParams(dimension_semantics=sem, vmem_limit_bytes=VMEM_LIMIT)


def _dot(a, b, dims):
    return lax.dot_general(a, b, (dims, ((), ())), preferred_element_type=F32)


def _nn(a, b):
    return _dot(a, b, ((1,), (0,)))


def _nt(a, b):
    return _dot(a, b, ((1,), (1,)))


def _tn(a, b):
    return _dot(a, b, ((0,), (0,)))


def _rstd(x):
    return lax.rsqrt(jnp.mean(x * x, axis=-1, keepdims=True) + EPS)


def _rms_bwd(dh, x, r, g):
    gd = dh * g
    return r * gd - x * (r * r * r) * jnp.mean(gd * x, axis=-1, keepdims=True)


def _colsum8(v):
    return jnp.broadcast_to(jnp.sum(v, axis=0, keepdims=True), (8, v.shape[1]))


def _seg_sum(v):
    r = lax.broadcasted_iota(jnp.int32, (128, 128), 0) // HEAD
    c = lax.broadcasted_iota(jnp.int32, (128, 128), 1) // HEAD
    ones = jnp.where(r == c, 1.0, 0.0).astype(BF16)
    out = []
    for blk in range(v.shape[1] // 128):
        part = v[:, blk * 128:(blk + 1) * 128]
        hi = part.astype(BF16)
        rest = part - hi.astype(F32)
        mid = rest.astype(BF16)
        lo = (rest - mid.astype(F32)).astype(BF16)
        out.append(_nn(hi, ones) + _nn(mid, ones) + _nn(lo, ones))
    return jnp.concatenate(out, axis=1)


def _place():
    return lax.axis_index("x"), lax.axis_index("y"), lax.axis_index("c")


class _Exchange:
    def __init__(self, ins, out_shape, sems, start, mid, finish):
        self.ins, self.out_shape, self.sems = list(ins), list(out_shape), list(sems)
        self.start, self.mid, self.finish = start, mid, finish


def _gather_exchange(shards):
    n = len(shards)

    def tools(ins, outs, sems):
        send_sems, recv_sems, local_sems = sems
        x, y, c = _place()
        me, sibling = (x, y, c), (x, y, 1 - c)
        chips = [(1 - x, y), (x, 1 - y), (1 - x, 1 - y)]

        def slot(k, px, py, pc):
            return outs[k].at[4 * px + 2 * py + pc]

        def copy(k, s, block, to, own=False):
            return pltpu.make_async_remote_copy(
                src_ref=ins[k] if own else slot(k, *block), dst_ref=slot(k, *block),
                send_sem=send_sems.at[k, s], recv_sem=recv_sems.at[k, s], device_id=to, device_id_type=MESH_ID)

        def mine():
            return [pltpu.make_async_copy(ins[k], slot(k, *me), local_sems.at[k]) for k in range(n)]

        def first():
            out = []
            for k in range(n):
                out.append(copy(k, 0, me, sibling, own=True))
                out += [copy(k, 1 + j, me, (*chip, c), own=True) for j, chip in enumerate(chips)]
            return out

        def landed(j):
            return [copy(k, 1 + j, (*chips[j], c), me) for k in range(n)]

        def passed(j):
            return [copy(k, 4 + j, (*chips[j], c), sibling) for k in range(n)]

        def from_sibling():
            out = []
            for k in range(n):
                out.append(copy(k, 0, sibling, me))
                out += [copy(k, 4 + j, (*chip, 1 - c), me) for j, chip in enumerate(chips)]
            return out

        return mine, first, landed, passed, from_sibling

    def start(ins, outs, sems):
        mine, first, _, _, _ = tools(ins, outs, sems)
        for cp in mine() + first():
            cp.start()

    def mid(ins, outs, sems):
        _, _, landed, passed, _ = tools(ins, outs, sems)
        for j in range(3):
            for arrived, onward in zip(landed(j), passed(j)):
                arrived.wait_recv()
                onward.start()

    def finish(ins, outs, sems):
        mine, first, _, passed, from_sibling = tools(ins, outs, sems)
        for cp in from_sibling():
            cp.wait_recv()
        for cp in first() + passed(0) + passed(1) + passed(2):
            cp.wait_send()
        for cp in mine():
            cp.wait()

    return _Exchange(
        shards, [jax.ShapeDtypeStruct((N_DEV,) + a.shape, a.dtype) for a in shards],
        [pltpu.SemaphoreType.DMA((n, 7)), pltpu.SemaphoreType.DMA((n, 7)), pltpu.SemaphoreType.DMA((n,))],
        start, mid, finish)


def _peer(x, y, c, m):
    px = 1 - x if m & 4 else x
    py = 1 - y if m & 2 else y
    pc = 1 - c if m & 1 else c
    return px, py, pc


def _scatter_exchange(parts):
    n = len(parts)

    def tools(ins, outs, sems):
        send_sems, recv_sems, local_sems = sems
        x, y, c = _place()
        me = 4 * x + 2 * y + c
        def mine():
            return [pltpu.make_async_copy(ins[k].at[me], outs[k].at[me], local_sems.at[k]) for k in range(n)]

        def remote(dst_is_mine):
            out = []
            for m in range(1, N_DEV):
                px, py, pc = _peer(x, y, c, m)
                peer = 4 * px + 2 * py + pc
                for k in range(n):
                    out.append(pltpu.make_async_remote_copy(
                        src_ref=ins[k].at[peer], dst_ref=outs[k].at[me if dst_is_mine else peer],
                        send_sem=send_sems.at[k, m - 1], recv_sem=recv_sems.at[k, m - 1],
                        device_id=(px, py, pc), device_id_type=MESH_ID))
            return out

        return mine, remote

    def start(ins, outs, sems):
        mine, remote = tools(ins, outs, sems)
        for cp in mine() + remote(True):
            cp.start()

    def mid(ins, outs, sems):
        pass

    def finish(ins, outs, sems):
        mine, remote = tools(ins, outs, sems)
        for cp in remote(False):
            cp.wait_recv()
        for cp in remote(True):
            cp.wait_send()
        for cp in mine():
            cp.wait()

    return _Exchange(
        parts, [jax.ShapeDtypeStruct(a.shape, a.dtype) for a in parts],
        [pltpu.SemaphoreType.DMA((n, 7)), pltpu.SemaphoreType.DMA((n, 7)), pltpu.SemaphoreType.DMA((n,))],
        start, mid, finish)


N_CHIPS = N_DEV // 2


def _pair_exchange(parts):
    n = len(parts)

    def copies(ins, outs, sems):
        send_sems, recv_sems = sems
        x, y, c = _place()
        return [pltpu.make_async_remote_copy(
            src_ref=ins[k].at[2 * q + 1 - c], dst_ref=outs[k].at[q], send_sem=send_sems.at[k, q],
            recv_sem=recv_sems.at[k, q], device_id=(x, y, 1 - c), device_id_type=MESH_ID)
            for k in range(n) for q in range(N_CHIPS)]

    def start(ins, outs, sems):
        for cp in copies(ins, outs, sems):
            cp.start()

    def mid(ins, outs, sems):
        pass

    def finish(ins, outs, sems):
        for cp in copies(ins, outs, sems):
            cp.wait_recv()
        for cp in copies(ins, outs, sems):
            cp.wait_send()

    return _Exchange(
        parts, [jax.ShapeDtypeStruct((N_CHIPS,) + a.shape[1:], a.dtype) for a in parts],
        [pltpu.SemaphoreType.DMA((n, N_CHIPS)), pltpu.SemaphoreType.DMA((n, N_CHIPS))], start, mid, finish)


def _pair_add(parts, stage, name, copies=1):
    n = len(parts)
    core = lax.axis_index("c").reshape(1)

    def body(core_ref, *refs):
        for k in range(n):
            mine, theirs = refs[k], refs[n + k]
            total = (mine[0, 0].astype(F32) + theirs[0].astype(F32)).astype(BF16)
            for rep in range(copies):
                refs[(2 + rep) * n + k][0] = total

    in_specs, out_specs = [], []
    for a in parts:
        _, r, cdim = a.shape
        in_specs.append(pl.BlockSpec((1, 1, r // 2, cdim), lambda q, i, core_ref: (q, core_ref[0], i, 0)))
    for a in parts:
        _, r, cdim = a.shape
        in_specs.append(pl.BlockSpec((1, r // 2, cdim), lambda q, i, core_ref: (q, i, 0)))
        out_specs.append(pl.BlockSpec((1, r // 2, cdim), lambda q, i, core_ref: (q, i, 0)))
    return list(_call(
        body, name=name,
        grid_spec=pltpu.PrefetchScalarGridSpec(num_scalar_prefetch=1, grid=(N_CHIPS, 2), in_specs=in_specs,
                                               out_specs=out_specs * copies),
        out_shape=[jax.ShapeDtypeStruct(s.shape, BF16) for s in stage] * copies,
        compiler_params=_params("arbitrary", "arbitrary"),
    )(core, *[a.reshape((N_CHIPS, 2) + a.shape[1:]) for a in parts], *stage))


HBM = pl.BlockSpec(memory_space=pltpu.HBM)
SEMAPHORES = pl.BlockSpec(memory_space=pltpu.SEMAPHORE)


def _chip_copies(srcs, lands, send_sems, recv_sems, mine_is_dst):
    x, y, c = _place()
    me = 2 * x + y
    out = []
    for m in range(1, N_CHIPS):
        px, py, _ = _peer(x, y, c, 2 * m)
        peer = 2 * px + py
        for k in range(len(srcs)):
            pair = k * (N_CHIPS - 1) + m - 1
            out.append(pltpu.make_async_remote_copy(
                src_ref=srcs[k].at[peer], dst_ref=lands[k].at[me if mine_is_dst else peer],
                send_sem=send_sems[pair], recv_sem=recv_sems[pair],
                device_id=(px, py, c), device_id_type=MESH_ID))
    return out


def _chip_exchange_start(chip_parts, lands, after, name):
    n = len(chip_parts)
    pairs = n * (N_CHIPS - 1)

    def body(*refs):
        srcs, zones = refs[:n], refs[n:2 * n]
        sems = refs[2 * n + 1:2 * n + 1 + 2 * pairs]
        token = refs[-1]
        for cp in _chip_copies(srcs, zones, sems[:pairs], sems[pairs:], True):
            cp.start()
        token[...] = jnp.zeros(token.shape, F32)

    thru = [pltpu.HBM(a.shape, a.dtype) for a in chip_parts + lands]
    hbm = [pltpu.with_memory_space_constraint(a, pltpu.HBM) for a in chip_parts + lands]
    res = _call(
        body, name=name,
        out_shape=[pltpu.SemaphoreType.DMA(())] * (2 * pairs) + thru + [jax.ShapeDtypeStruct((8, 128), F32)],
        in_specs=[HBM] * (2 * n) + [ANY], out_specs=[SEMAPHORES] * (2 * pairs) + [HBM] * (2 * n) + [VMEM],
        input_output_aliases={i: 2 * pairs + i for i in range(2 * n)},
        compiler_params=pltpu.CompilerParams(has_side_effects=pltpu.SideEffectType.DATAFLOW_SIDE_EFFECTING),
    )(*hbm, after)
    sems, rest = list(res[:2 * pairs]), res[2 * pairs:]
    return sems, list(rest[:n]), list(rest[n:2 * n]), rest[-1]


def _chip_exchange_wait(sems, chip_parts, lands, after, name):
    n = len(chip_parts)
    pairs = n * (N_CHIPS - 1)

    def body(*refs):
        srcs, zones = refs[:n], refs[n:2 * n]
        sem_refs = refs[2 * n:2 * n + 2 * pairs]
        for cp in _chip_copies(srcs, zones, sem_refs[:pairs], sem_refs[pairs:], False):
            cp.wait_send()
            cp.wait_recv()

    thru = [pltpu.HBM(a.shape, a.dtype) for a in chip_parts + lands]
    res = _call(
        body, name=name, out_shape=thru,
        in_specs=[HBM] * (2 * n) + [SEMAPHORES] * (2 * pairs) + [ANY], out_specs=[HBM] * (2 * n),
        input_output_aliases={i: i for i in range(2 * n)},
        compiler_params=pltpu.CompilerParams(has_side_effects=pltpu.SideEffectType.DATAFLOW_SIDE_EFFECTING),
    )(*chip_parts, *lands, *sems, after)
    return list(res[n:])


def _chip_exchange(chip_parts):
    n = len(chip_parts)

    def tools(ins, outs, sems):
        send_sems, recv_sems, local_sems = sems
        x, y, c = _place()
        me = 2 * x + y

        def mine():
            return [pltpu.make_async_copy(ins[k].at[me], outs[k].at[me], local_sems.at[k]) for k in range(n)]

        def remote(dst_is_mine):
            out = []
            for m in range(1, N_CHIPS):
                px, py, _ = _peer(x, y, c, 2 * m)
                peer = 2 * px + py
                for k in range(n):
                    out.append(pltpu.make_async_remote_copy(
                        src_ref=ins[k].at[peer], dst_ref=outs[k].at[me if dst_is_mine else peer],
                        send_sem=send_sems.at[k, m - 1], recv_sem=recv_sems.at[k, m - 1],
                        device_id=(px, py, c), device_id_type=MESH_ID))
            return out

        return mine, remote

    def start(ins, outs, sems):
        mine, remote = tools(ins, outs, sems)
        for cp in mine() + remote(True):
            cp.start()

    def mid(ins, outs, sems):
        pass

    def finish(ins, outs, sems):
        mine, remote = tools(ins, outs, sems)
        for cp in remote(False):
            cp.wait_recv()
        for cp in remote(True):
            cp.wait_send()
        for cp in mine():
            cp.wait()

    return _Exchange(
        chip_parts, [jax.ShapeDtypeStruct(a.shape, a.dtype) for a in chip_parts],
        [pltpu.SemaphoreType.DMA((n, N_CHIPS - 1)), pltpu.SemaphoreType.DMA((n, N_CHIPS - 1)),
         pltpu.SemaphoreType.DMA((n,))], start, mid, finish)


def _run_exchange(ex, name, after=None):
    n_in, n_out = len(ex.ins), len(ex.out_shape)
    order = [] if after is None else [after]

    def body(*refs):
        ins, outs = refs[:n_in], refs[n_in + len(order):n_in + len(order) + n_out]
        sems = refs[n_in + len(order) + n_out:]
        ex.start(ins, outs, sems)
        ex.mid(ins, outs, sems)
        ex.finish(ins, outs, sems)

    return list(_call(body, name=name, out_shape=ex.out_shape, in_specs=[ANY] * (n_in + len(order)),
                      out_specs=[ANY] * n_out, scratch_shapes=ex.sems)(*ex.ins, *order))


def _call_hosting(body, ex, phases, *, in_specs, out_specs, out_shape, scratch_shapes, args, **kw):
    n_in, n_out, n_scr = len(in_specs), len(out_specs), len(scratch_shapes)
    if ex is None:
        res = _call(body, in_specs=in_specs, out_specs=out_specs, out_shape=out_shape,
                    scratch_shapes=scratch_shapes, **kw)(*args)
        return list(res), []
    n_xin, n_xout = len(ex.ins), len(ex.out_shape)

    def hosting(*refs):
        a, b = n_in, n_in + n_xin
        c, d = b + n_out, b + n_out + n_xout
        ins, xins, outs, xouts = refs[:a], refs[a:b], refs[b:c], refs[c:d]
        scr, sems = refs[d:d + n_scr], refs[d + n_scr:]
        first, mid, last = phases()

        @pl.when(first)
        def _():
            ex.start(xins, xouts, sems)

        body(*ins, *outs, *scr)

        @pl.when(mid)
        def _():
            ex.mid(xins, xouts, sems)

        @pl.when(last)
        def _():
            ex.finish(xins, xouts, sems)

    res = _call(hosting, in_specs=list(in_specs) + [ANY] * n_xin, out_specs=list(out_specs) + [ANY] * n_xout,
                out_shape=list(out_shape) + ex.out_shape, scratch_shapes=list(scratch_shapes) + ex.sems,
                **kw)(*args, *ex.ins)
    return list(res[:n_out]), list(res[n_out:])


def _grid_phases(dims, mid_fraction=0.8):
    total = 1
    for d in dims:
        total *= d
    mid = min(max(int(total * mid_fraction), 1), total - 1)

    def phases():
        step = pl.program_id(0)
        for axis in range(1, len(dims)):
            step = step * dims[axis] + pl.program_id(axis)
        return step == 0, step == mid, step == total - 1
    return phases


def _all_reduce_small(pack, name, after):
    rows = pack.shape[0]

    def body(in_ref, _, out_ref, recv, send_sems, recv_sems):
        x, y, c = _place()
        me = 4 * x + 2 * y + c
        recv[me] = in_ref[...]
        sent = []
        for m in range(1, N_DEV):
            px, py, pc = _peer(x, y, c, m)
            cp = pltpu.make_async_remote_copy(
                src_ref=in_ref, dst_ref=recv.at[me], send_sem=send_sems.at[m - 1], recv_sem=recv_sems.at[m - 1],
                device_id=(px, py, pc), device_id_type=MESH_ID)
            cp.start()
            sent.append(cp)
        for m in range(1, N_DEV):
            px, py, pc = _peer(x, y, c, m)
            peer = 4 * px + 2 * py + pc
            pltpu.make_async_remote_copy(
                src_ref=in_ref, dst_ref=recv.at[peer], send_sem=send_sems.at[m - 1], recv_sem=recv_sems.at[m - 1],
                device_id=(px, py, pc), device_id_type=MESH_ID).wait_recv()
        acc = recv[0]
        for d in range(1, N_DEV):
            acc = acc + recv[d]
        out_ref[...] = acc
        for cp in sent:
            cp.wait_send()

    return _call(
        body, name=name, out_shape=jax.ShapeDtypeStruct(pack.shape, F32), in_specs=[VMEM, ANY], out_specs=VMEM,
        scratch_shapes=[pltpu.VMEM((N_DEV, rows, pack.shape[1]), F32), pltpu.SemaphoreType.DMA((7,)),
                        pltpu.SemaphoreType.DMA((7,))],
        compiler_params=pltpu.CompilerParams(vmem_limit_bytes=VMEM_LIMIT),
    )(pack, after)


POOL_TS = 256


def _pool_counts(first_row, rows, win):
    t = first_row + lax.broadcasted_iota(jnp.int32, (rows, 1), 0)
    return jnp.minimum(t + 1, win).astype(F32)


def _pool_fwd(x, g, w, b, scale):
    nt = S // POOL_TS

    def body(x_ref, g_ref, w_ref, b_ref, s_ref, out_ref, diff_ref, ext):
        i = pl.program_id(0)

        @pl.when(i == 0)
        def _():
            ext[0:MAX_WIN, :] = jnp.zeros((MAX_WIN, D), F32)

        @pl.when(i > 0)
        def _():
            ext[0:MAX_WIN, :] = ext[POOL_TS:POOL_TS + MAX_WIN, :]

        xv = x_ref[...]
        h = xv * _rstd(xv) * g_ref[...]
        ext[MAX_WIN:, :] = h
        for gi in range(N_GROUPS):
            win = 2 << gi
            cols = slice(gi * GROUP, (gi + 1) * GROUP)
            sm = ext[:, cols]
            k = 1
            while k < win:
                sm = sm + pltpu.roll(sm, k, axis=0)
                k *= 2
            pooled = sm[MAX_WIN:, :] / _pool_counts(i * POOL_TS, POOL_TS, win)
            diff = (pooled - h[:, cols]).astype(BF16)
            yv = (_nn(diff, w_ref[gi]) + b_ref[:, cols]) * s_ref[:, cols]
            out_ref[:, cols] = xv[:, cols] + yv
            diff_ref[:, cols] = diff

    row = pl.BlockSpec((1, D), lambda i: (0, 0))
    tile = pl.BlockSpec((POOL_TS, D), lambda i: (i, 0))
    return _call(
        body, name="pool_fwd", grid=(nt,),
        in_specs=[tile, row, pl.BlockSpec((N_GROUPS, GROUP, GROUP), lambda i: (0, 0, 0)), row, row],
        out_specs=[tile, tile],
        out_shape=[jax.ShapeDtypeStruct((S, D), F32), jax.ShapeDtypeStruct((S, D), BF16)],
        scratch_shapes=[pltpu.VMEM((POOL_TS + MAX_WIN, D), F32)],
        compiler_params=_params("arbitrary"),
    )(x, g, w, b, scale)


def _pool_bwd(dy, x, diff, g, w, b, scale, ex=None):
    nt = S // POOL_TS

    def body(dy_ref, x_ref, diff_ref, g_ref, w_ref, b_ref, s_ref, gx_ref, dw_ref, db_ref, ds_ref, dg_ref, ext, dh):
        i = pl.program_id(0)
        first_row = (nt - 1 - i) * POOL_TS

        @pl.when(i == 0)
        def _():
            ext[POOL_TS:, :] = jnp.zeros((MAX_WIN, D), F32)
            dw_ref[...] = jnp.zeros(dw_ref.shape, F32)
            db_ref[...] = jnp.zeros(db_ref.shape, F32)
            ds_ref[...] = jnp.zeros(ds_ref.shape, F32)
            dg_ref[...] = jnp.zeros(dg_ref.shape, F32)

        @pl.when(i > 0)
        def _():
            ext[POOL_TS:, :] = ext[0:MAX_WIN, :]

        dyv = dy_ref[...]
        for gi in range(N_GROUPS):
            win = 2 << gi
            cols = slice(gi * GROUP, (gi + 1) * GROUP)
            dfb = diff_ref[:, cols]
            z = _nn(dfb, w_ref[gi]) + b_ref[:, cols]
            dyg = dyv[:, cols]
            ds_ref[:, cols] += _colsum8(dyg * z)
            dz = dyg * s_ref[:, cols]
            db_ref[:, cols] += _colsum8(dz)
            dzb = dz.astype(BF16)
            dw_ref[gi] += _tn(dfb, dzb)
            ddiff = _nt(dzb, w_ref[gi])
            ext[0:POOL_TS, cols] = ddiff / _pool_counts(first_row, POOL_TS, win)
            sm = ext[:, cols]
            k = 1
            while k < win:
                sm = sm + pltpu.roll(sm, POOL_TS + MAX_WIN - k, axis=0)
                k *= 2
            dh[:, cols] = sm[0:POOL_TS, :] - ddiff
        xv = x_ref[...]
        r = _rstd(xv)
        gv = g_ref[...]
        dhv = dh[...]
        dg_ref[...] += _colsum8(dhv * xv * r)
        gx_ref[...] = dyv + _rms_bwd(dhv, xv, r, gv)

    row = pl.BlockSpec((1, D), lambda i: (0, 0))
    tile = pl.BlockSpec((POOL_TS, D), lambda i: (nt - 1 - i, 0))
    acc = pl.BlockSpec((8, D), lambda i: (0, 0))
    wspec = pl.BlockSpec((N_GROUPS, GROUP, GROUP), lambda i: (0, 0, 0))
    return _call_hosting(
        body, ex, _grid_phases((nt,)), name="pool_bwd", grid=(nt,),
        in_specs=[tile, tile, tile, row, wspec, row, row],
        out_specs=[tile, wspec, acc, acc, acc],
        out_shape=[jax.ShapeDtypeStruct((S, D), F32), jax.ShapeDtypeStruct((N_GROUPS, GROUP, GROUP), F32),
                   jax.ShapeDtypeStruct((8, D), F32), jax.ShapeDtypeStruct((8, D), F32),
                   jax.ShapeDtypeStruct((8, D), F32)],
        scratch_shapes=[pltpu.VMEM((POOL_TS + MAX_WIN, D), F32), pltpu.VMEM((POOL_TS, D), F32)],
        args=(dy, x, diff, g, w, b, scale), compiler_params=_params("arbitrary"),
    )


FFN_TS = min(S, 1024)
FFN_TF = 256


def _ffn_fwd(x, g, wg_t, wu_t, wd, target=None, ex=None):
    ni, nj = S // FFN_TS, F // FFN_TF
    with_loss = target is not None

    def body(*refs):
        if with_loss:
            x_ref, g_ref, wg_ref, wu_ref, wd_ref, t_ref, out_ref, loss_ref, h_ref, gg_ref, uu_ref, hs, acc = refs
        else:
            x_ref, g_ref, wg_ref, wu_ref, wd_ref, out_ref, h_ref, gg_ref, uu_ref, hs, acc = refs
        i, j = pl.program_id(0), pl.program_id(1)

        @pl.when(j == 0)
        def _():
            xv = x_ref[...]
            hb = (xv * _rstd(xv) * g_ref[...]).astype(BF16)
            hs[...] = hb
            h_ref[...] = hb
            acc[...] = jnp.zeros(acc.shape, F32)

        hb = hs[...]
        gg = _nt(hb, wg_ref[...])
        uu = _nt(hb, wu_ref[...])
        gg_ref[...] = gg
        uu_ref[...] = uu
        a = (gg * jax.nn.sigmoid(gg) * uu).astype(BF16)
        acc[...] += _nn(a, wd_ref[...])

        @pl.when(j == nj - 1)
        def _():
            yv = x_ref[...] + acc[...]
            if with_loss:
                err = yv - t_ref[...]
                out_ref[...] = err * (1.0 / D)
                part = jnp.sum(err * err) * (0.5 / D)

                @pl.when(i == 0)
                def _():
                    loss_ref[...] = jnp.zeros(loss_ref.shape, F32)

                loss_ref[...] += jnp.broadcast_to(part, loss_ref.shape)
            else:
                out_ref[...] = yv

    xt = pl.BlockSpec((FFN_TS, D), lambda i, j: (i, 0))
    row = pl.BlockSpec((1, D), lambda i, j: (0, 0))
    wt = pl.BlockSpec((FFN_TF, D), lambda i, j: (j, 0))
    gt = pl.BlockSpec((FFN_TS, FFN_TF), lambda i, j: (i, j))
    in_specs = [xt, row, wt, wt, wt] + ([xt] if with_loss else [])
    out_specs = [xt] + ([pl.BlockSpec((8, 128), lambda i, j: (0, 0))] if with_loss else []) + [xt, gt, gt]
    out_shape = ([jax.ShapeDtypeStruct((S, D), F32)] + ([jax.ShapeDtypeStruct((8, 128), F32)] if with_loss else [])
                 + [jax.ShapeDtypeStruct((S, D), BF16), jax.ShapeDtypeStruct((S, F), F32),
                    jax.ShapeDtypeStruct((S, F), F32)])
    args = (x, g, wg_t, wu_t, wd) + ((target,) if with_loss else ())
    return _call_hosting(
        body, ex, _grid_phases((ni, nj), 0.8), name="ffn_fwd_loss" if with_loss else "ffn_fwd", grid=(ni, nj),
        in_specs=in_specs, out_specs=out_specs, out_shape=out_shape,
        scratch_shapes=[pltpu.VMEM((FFN_TS, D), BF16), pltpu.VMEM((FFN_TS, D), F32)], args=args,
        compiler_params=_params("arbitrary", "arbitrary"),
    )


def _ffn_bwd_weights(dout, h, gg, uu, wd, name, ex=None):
    nj = F // FFN_TF

    def body(do_ref, h_ref, gg_ref, uu_ref, wd_ref, dg_ref, du_ref, dwg_ref, dwu_ref, dwd_ref, dob):
        @pl.when(pl.program_id(0) == 0)
        def _():
            dob[...] = do_ref[...].astype(BF16)

        hb, dov = h_ref[...], dob[...]
        gv, uv = gg_ref[...], uu_ref[...]
        da = _nt(dov, wd_ref[...])
        sg = jax.nn.sigmoid(gv)
        sl = gv * sg
        ab = (sl * uv).astype(BF16)
        dub = (da * sl).astype(BF16)
        dgb = (da * uv * (sg * (1.0 + gv * (1.0 - sg)))).astype(BF16)
        dg_ref[...] = dgb
        du_ref[...] = dub
        dwg_ref[...] = _tn(dgb, hb).astype(BF16)
        dwu_ref[...] = _tn(dub, hb).astype(BF16)
        dwd_ref[...] = _tn(ab, dov).astype(BF16)

    once = pl.Buffered(1)
    whole = lambda: pl.BlockSpec((S, D), lambda j, _: (0, 0), pipeline_mode=once)
    wt = pl.BlockSpec((FFN_TF, D), lambda j, _: (j, 0))
    gt = pl.BlockSpec((S, FFN_TF), lambda j, _: (0, j))
    return _call_hosting(
        body, ex, _grid_phases((nj, 1)), name=name, grid=(nj, 1),
        in_specs=[whole(), whole(), gt, gt, wt], out_specs=[gt, gt, wt, wt, wt],
        out_shape=[jax.ShapeDtypeStruct((S, F), BF16)] * 2 + [jax.ShapeDtypeStruct((F, D), BF16)] * 3,
        scratch_shapes=[pltpu.VMEM((S, D), BF16)], args=(dout, h, gg, uu, wd),
        compiler_params=_params("arbitrary", "arbitrary"),
    )


BWD_TS = 512


def _ffn_bwd_input(dres, x, g, dg, du, wg_t, wu_t, name, ex=None):
    nt = S // BWD_TS

    def body(dres_ref, x_ref, g_ref, dg_ref, du_ref, wg_ref, wu_ref, dx_ref, dgam_ref):
        dh = _nn(dg_ref[...], wg_ref[...]) + _nn(du_ref[...], wu_ref[...])
        xv = x_ref[...]
        r = _rstd(xv)
        dx_ref[...] = dres_ref[...] + _rms_bwd(dh, xv, r, g_ref[...])

        @pl.when(pl.program_id(0) == 0)
        def _():
            dgam_ref[...] = jnp.zeros(dgam_ref.shape, F32)

        dgam_ref[...] += _colsum8(dh * xv * r)

    tile = pl.BlockSpec((BWD_TS, D), lambda i: (i, 0))
    ftile = pl.BlockSpec((BWD_TS, F), lambda i: (i, 0))
    wspec = lambda: pl.BlockSpec((F, D), lambda i: (0, 0), pipeline_mode=pl.Buffered(1))
    return _call_hosting(
        body, ex, _grid_phases((nt,)), name=name, grid=(nt,),
        in_specs=[tile, tile, pl.BlockSpec((1, D), lambda i: (0, 0)), ftile, ftile, wspec(), wspec()],
        out_specs=[tile, pl.BlockSpec((8, D), lambda i: (0, 0))],
        out_shape=[jax.ShapeDtypeStruct((S, D), F32), jax.ShapeDtypeStruct((8, D), F32)],
        scratch_shapes=[], args=(dres, x, g, dg, du, wg_t, wu_t), compiler_params=_params("arbitrary"),
    )


NORM_TS = 256


def _norm_bwd(dres, x, dhs, gs, name, ex=None):
    n = len(dhs)
    nt = S // NORM_TS

    def body(*refs):
        dres_ref, x_ref = refs[:2]
        dh_refs, g_refs = refs[2:2 + n], refs[2 + n:2 + 2 * n]
        dx_ref, dg_refs = refs[2 + 2 * n], refs[3 + 2 * n:]
        i = pl.program_id(0)
        xv = x_ref[...]
        r = _rstd(xv)
        dx = dres_ref[...]
        for k in range(n):
            dhv = dh_refs[k][...]
            dx = dx + _rms_bwd(dhv, xv, r, g_refs[k][...])

            @pl.when(i == 0)
            def _():
                dg_refs[k][...] = jnp.zeros((8, D), F32)

            dg_refs[k][...] += _colsum8(dhv * xv * r)
        dx_ref[...] = dx

    tile = pl.BlockSpec((NORM_TS, D), lambda i: (i, 0))
    row = pl.BlockSpec((1, D), lambda i: (0, 0))
    acc = pl.BlockSpec((8, D), lambda i: (0, 0))
    return _call_hosting(
        body, ex, _grid_phases((nt,)), name=name, grid=(nt,),
        in_specs=[tile, tile] + [tile] * n + [row] * n,
        out_specs=[tile] + [acc] * n,
        out_shape=[jax.ShapeDtypeStruct((S, D), F32)] + [jax.ShapeDtypeStruct((8, D), F32)] * n,
        scratch_shapes=[], args=(dres, x, *dhs, *gs), compiler_params=_params("arbitrary"),
    )


def _mm(a, b, mode, out_dtype, name, add=None, skip_rows=0):
    if mode == "nn":
        (m, kd), n = (a.shape[0] - skip_rows, a.shape[1]), b.shape[1]
    elif mode == "nt":
        (m, kd), n = (a.shape[0] - skip_rows, a.shape[1]), b.shape[0]
    else:
        (kd, m), n = a.shape, b.shape[1]
    tm, tn, tk = min(m, 1024), min(n, 1024), min(kd, 1024)
    if skip_rows and mode == "tn":
        tk = min(tk, skip_rows)
    elif skip_rows:
        tm = min(tm, skip_rows)
    skip_a = skip_rows // tm if mode != "tn" else 0
    skip_b = skip_rows // tk if mode == "tn" else 0
    assert skip_rows == skip_a * tm + skip_b * tk
    nk = kd // tk
    dot = {"nn": _nn, "nt": _nt, "tn": _tn}[mode]

    def body(*refs):
        if add is None:
            a_ref, b_ref, o_ref, acc = refs
        else:
            a_ref, b_ref, add_ref, o_ref, acc = refs
        k = pl.program_id(2)

        @pl.when(k == 0)
        def _():
            acc[...] = jnp.zeros(acc.shape, F32)

        acc[...] += dot(a_ref[...].astype(BF16), b_ref[...].astype(BF16))

        @pl.when(k == nk - 1)
        def _():
            res = acc[...]
            if add is not None:
                res = res + add_ref[...]
            o_ref[...] = res.astype(out_dtype)

    if mode == "tn":
        a_spec = pl.BlockSpec((tk, tm), lambda i, j, k: (k, i))
        b_spec = pl.BlockSpec((tk, tn), lambda i, j, k: (k + skip_b, j))
    else:
        a_spec = pl.BlockSpec((tm, tk), lambda i, j, k: (i + skip_a, k))
        b_spec = (pl.BlockSpec((tk, tn), lambda i, j, k: (k, j)) if mode == "nn"
                  else pl.BlockSpec((tn, tk), lambda i, j, k: (j, k)))
    o_spec = pl.BlockSpec((tm, tn), lambda i, j, k: (i, j))
    in_specs = [a_spec, b_spec] + ([o_spec] if add is not None else [])
    args = (a, b) + ((add,) if add is not None else ())
    return _call(
        body, name=name, grid=(m // tm, n // tn, nk), in_specs=in_specs, out_specs=o_spec,
        out_shape=jax.ShapeDtypeStruct((m, n), out_dtype), scratch_shapes=[pltpu.VMEM((tm, tn), F32)],
        compiler_params=_params("parallel", "parallel", "arbitrary"),
    )(*args)


PROJ_TS = 256


def _kvq_proj(x, g_kv, g_mix, wk, wv, wq, gk, gq):
    lead = PADK // PROJ_TS

    def body(x_ref, gkv_ref, gmix_ref, wk_ref, wv_ref, wq_ref, gk_ref, gq_ref,
             hkv_ref, h1_ref, kpre_ref, qpre_ref, k_ref, v_ref, q_ref):
        i = pl.program_id(0)

        @pl.when(i < lead)
        def _():
            k_ref[...] = jnp.zeros(k_ref.shape, BF16)
            v_ref[...] = jnp.zeros(v_ref.shape, BF16)

        @pl.when(i >= lead)
        def _():
            xv = x_ref[...]
            xr = xv * _rstd(xv)
            hkv = (xr * gkv_ref[...]).astype(BF16)
            h1 = (xr * gmix_ref[...]).astype(BF16)
            hkv_ref[...] = hkv
            h1_ref[...] = h1
            kpre = _nn(hkv, wk_ref[...])
            qpre = _nn(h1, wq_ref[...])
            kpre_ref[...] = kpre
            qpre_ref[...] = qpre
            v_ref[...] = _nn(hkv, wv_ref[...]).astype(BF16)
            rk = lax.rsqrt(_seg_sum(kpre * kpre) * (1.0 / HEAD) + EPS)
            k_ref[...] = (kpre * rk * gk_ref[...]).astype(BF16)
            rq = lax.rsqrt(_seg_sum(qpre * qpre) * (1.0 / HEAD) + EPS)
            q_ref[...] = (qpre * rq * gq_ref[...]).astype(BF16)

    tile = pl.BlockSpec((PROJ_TS, D), lambda i: (jnp.maximum(i - lead, 0), 0))
    padded = pl.BlockSpec((PROJ_TS, D), lambda i: (i, 0))
    row = pl.BlockSpec((1, D), lambda i: (0, 0))
    wspec = pl.BlockSpec((D, D), lambda i: (0, 0))
    bf = jax.ShapeDtypeStruct((S, D), BF16)
    ff = jax.ShapeDtypeStruct((S, D), F32)
    bp = jax.ShapeDtypeStruct((PADK + S, D), BF16)
    return _call(
        body, name="kvq_proj", grid=(lead + S // PROJ_TS,),
        in_specs=[tile, row, row, wspec, wspec, wspec, row, row],
        out_specs=[tile, tile, tile, tile, padded, padded, tile], out_shape=[bf, bf, ff, ff, bp, bp, bf],
        compiler_params=_params("arbitrary"),
    )(x, g_kv, g_mix, wk, wv, wq, gk, gq)


def _head_norm_bwd(dout, pre, hg, name, row_off=0):
    nt = S // PROJ_TS

    def body(do_ref, pre_ref, hg_ref, dpre_ref, dhg_ref, acc):
        i = pl.program_id(0)
        dov, pv, hgv = do_ref[...], pre_ref[...], hg_ref[...]
        r = lax.rsqrt(_seg_sum(pv * pv) * (1.0 / HEAD) + EPS)
        gd = dov * hgv
        dpre = r * gd - pv * (r * r * r) * (_seg_sum(gd * pv) * (1.0 / HEAD))
        dpre_ref[...] = dpre.astype(BF16)

        @pl.when(i == 0)
        def _():
            acc[...] = jnp.zeros(acc.shape, F32)

        acc[...] += _colsum8(dov * pv * r)

        @pl.when(i == nt - 1)
        def _():
            full = acc[...]
            fold = full[:, 0:128]
            for blk in range(1, D // 128):
                fold = fold + full[:, blk * 128:(blk + 1) * 128]
            dhg_ref[...] = fold + pltpu.roll(fold, HEAD, axis=1)

    tile = pl.BlockSpec((PROJ_TS, D), lambda i: (i, 0))
    return _call(
        body, name=name, grid=(nt,),
        in_specs=[pl.BlockSpec((PROJ_TS, D), lambda i: (i + row_off, 0)), tile, pl.BlockSpec((1, D), lambda i: (0, 0))],
        out_specs=[tile, pl.BlockSpec((8, 128), lambda i: (0, 0))],
        out_shape=[jax.ShapeDtypeStruct((S, D), BF16), jax.ShapeDtypeStruct((8, 128), F32)],
        scratch_shapes=[pltpu.VMEM((8, D), F32)],
        compiler_params=_params("arbitrary"),
    )(dout, pre, hg)


def _toeplitz_from_table(table):
    far = jnp.broadcast_to(table[:, N_REL - 1:], (N_HEADS, PADK - MAX_REL + 1))
    near = table[:, N_REL - 2::-1]
    past = jnp.broadcast_to(table[:, 0:1], (N_HEADS, MAX_REL))
    wrap = jnp.broadcast_to(table[:, N_REL - 1:], (N_HEADS, QB - 1))
    return jnp.concatenate([far, near, past, wrap], axis=1).reshape(N_HEADS, 1, TOEP)


def _table_grad_from_toeplitz(dtp, seg):
    lo = PADK - MAX_REL + 1
    near = dtp[:, lo + N_REL - 3:lo - 1:-1]
    return jnp.concatenate([seg[:, 1:2], near, seg[:, 0:1]], axis=1)


def _bias_band(tp):
    def body(tp_ref, out_ref):
        bv = pltpu.roll(jnp.broadcast_to(tp_ref[0], (QB, TOEP)), 0, axis=1, stride=1, stride_axis=0)
        out_ref[0] = bv[:, 0:KB]

    return _call(
        body, name="bias_band", grid=(N_HEADS,),
        in_specs=[pl.BlockSpec((1, 1, TOEP), lambda h: (h, 0, 0))],
        out_specs=pl.BlockSpec((1, QB, KB), lambda h: (h, 0, 0)),
        out_shape=jax.ShapeDtypeStruct((N_HEADS, QB, KB), F32),
        compiler_params=_params("parallel"),
    )(tp)


def _bias_grad(dband, after):
    lo, hi = PADK - MAX_REL + 1, PADK + MAX_REL

    def body(db_ref, _, dtp_ref, seg_ref):
        bv = jnp.concatenate([db_ref[0], jnp.zeros((QB, TOEP - KB), F32)], axis=1)
        row = lax.broadcasted_iota(jnp.int32, (QB, TOEP), 0)
        k = 1
        while k < QB:
            bv = jnp.where((row & k) != 0, pltpu.roll(bv, TOEP - k, axis=1), bv)
            k *= 2
        col = jnp.sum(bv, axis=0, keepdims=True)
        dtp_ref[0] = col
        u = lax.broadcasted_iota(jnp.int32, (1, TOEP), 1)
        far = jnp.sum(jnp.where((u < lo) | (u > hi + MAX_REL), col, 0.0))
        past = jnp.sum(jnp.where((u >= hi) & (u <= hi + MAX_REL), col, 0.0))
        lane = lax.broadcasted_iota(jnp.int32, (1, 128), 1)
        seg_ref[0] = jnp.where(lane == 0, far, jnp.where(lane == 1, past, 0.0))

    return _call(
        body, name="bias_grad", grid=(N_HEADS,),
        in_specs=[pl.BlockSpec((1, QB, KB), lambda h: (h, 0, 0)), ANY],
        out_specs=[pl.BlockSpec((1, 1, TOEP), lambda h: (h, 0, 0)), pl.BlockSpec((1, 1, 128), lambda h: (h, 0, 0))],
        out_shape=[jax.ShapeDtypeStruct((N_HEADS, 1, TOEP), F32), jax.ShapeDtypeStruct((N_HEADS, 1, 128), F32)],
        compiler_params=_params("parallel"),
    )(dband, after)


N_QB = S // QB
N_HP = D // 128


def _band_mask(cb):
    qc = lax.broadcasted_iota(jnp.int32, (QB, KB), 0) >> 6
    p = lax.broadcasted_iota(jnp.int32, (QB, KB), 1)
    kc = p >> 6
    return (kc >= qc) & (kc <= qc + LEFT) & (p + cb * QB >= PADK)


def _half_mask(hh):
    lane = lax.broadcasted_iota(jnp.int32, (1, 128), 1)
    return jnp.where((lane < HEAD) == (hh == 0), 1.0, 0.0).astype(BF16)


def _probs(qh, kb, bias, mask):
    sc = _nt(qh, kb) * ATTN_SCALE + bias
    sc = jnp.where(mask, sc, NEG_INF)
    e = jnp.exp(sc - jnp.max(sc, axis=-1, keepdims=True))
    return e / jnp.sum(e, axis=-1, keepdims=True)


def _attn_fwd(q, kp, vp, bias, ex=None):
    def body(q_ref, k_ref, v_ref, b_ref, o_ref):
        cb = pl.program_id(1)
        band = pl.ds(pl.multiple_of(cb * QB, QB), KB)
        kb, vb = k_ref[band, :], v_ref[band, :]
        qv = q_ref[...]
        mask = _band_mask(cb)
        low = lax.broadcasted_iota(jnp.int32, (QB, 128), 1) < HEAD
        outs = []
        for hh in range(2):
            pb = _probs(qv * _half_mask(hh), kb, b_ref[hh], mask).astype(BF16)
            outs.append(_nn(pb, vb))
        o_ref[...] = jnp.where(low, outs[0], outs[1]).astype(BF16)

    qspec = pl.BlockSpec((QB, 128), lambda hp, cb: (cb, hp))
    kspec = pl.BlockSpec((PADK + S, 128), lambda hp, cb: (0, hp))
    return _call_hosting(
        body, ex, _grid_phases((N_HP, N_QB), 0.85), name="attn_fwd", grid=(N_HP, N_QB),
        in_specs=[qspec, kspec, kspec, pl.BlockSpec((2, QB, KB), lambda hp, cb: (hp, 0, 0))],
        out_specs=[qspec], out_shape=[jax.ShapeDtypeStruct((S, D), BF16)], scratch_shapes=[],
        args=(q, kp, vp, bias), compiler_params=_params("arbitrary", "arbitrary"),
    )


def _attn_bwd(q, kp, vp, bias, do, ex=None):
    def body(q_ref, k_ref, v_ref, b_ref, do_ref, dq_ref, dk_ref, dv_ref, db_ref):
        cb = pl.program_id(1)

        @pl.when(cb == 0)
        def _():
            dk_ref[...] = jnp.zeros(dk_ref.shape, F32)
            dv_ref[...] = jnp.zeros(dv_ref.shape, F32)
            db_ref[...] = jnp.zeros(db_ref.shape, F32)

        band = pl.ds(pl.multiple_of(cb * QB, QB), KB)
        kb, vb = k_ref[band, :], v_ref[band, :]
        qv, dov = q_ref[...], do_ref[...]
        mask = _band_mask(cb)
        low = lax.broadcasted_iota(jnp.int32, (QB, 128), 1) < HEAD
        dq = jnp.zeros((QB, 128), F32)
        dkb = jnp.zeros((KB, 128), F32)
        dvb = jnp.zeros((KB, 128), F32)
        for hh in range(2):
            sel = low if hh == 0 else jnp.logical_not(low)
            qh = qv * _half_mask(hh)
            doh = dov * _half_mask(hh)
            p = _probs(qh, kb, b_ref[hh], mask)
            dp = _nt(doh, vb)
            dvb = dvb + _tn(p.astype(BF16), doh)
            ds = p * (dp - jnp.sum(dp * p, axis=-1, keepdims=True))
            db_ref[hh] += ds
            dsb = (ds * ATTN_SCALE).astype(BF16)
            dq = dq + jnp.where(sel, _nn(dsb, kb), 0.0)
            dkb = dkb + _tn(dsb, qh)
        dq_ref[...] = dq
        dk_ref[band, :] += dkb
        dv_ref[band, :] += dvb

    qspec = pl.BlockSpec((QB, 128), lambda hp, cb: (cb, hp))
    kspec = pl.BlockSpec((PADK + S, 128), lambda hp, cb: (0, hp))
    bspec = pl.BlockSpec((2, QB, KB), lambda hp, cb: (hp, 0, 0))
    kf = jax.ShapeDtypeStruct((PADK + S, D), F32)
    return _call_hosting(
        body, ex, _grid_phases((N_HP, N_QB)), name="attn_bwd", grid=(N_HP, N_QB),
        in_specs=[qspec, kspec, kspec, bspec, qspec],
        out_specs=[qspec, kspec, kspec, bspec],
        out_shape=[jax.ShapeDtypeStruct((S, D), F32), kf, kf, jax.ShapeDtypeStruct((N_HEADS, QB, KB), F32)],
        scratch_shapes=[], args=(q, kp, vp, bias, do), compiler_params=_params("arbitrary", "arbitrary"),
    )


def _adam_math(w, g, m, v):
    m = ADAM_B1 * m + (1.0 - ADAM_B1) * g
    v = ADAM_B2 * v + (1.0 - ADAM_B2) * (g * g)
    m_hat = m / (1.0 - ADAM_B1 ** ADAM_STEP)
    v_hat = v / (1.0 - ADAM_B2 ** ADAM_STEP)
    delta = -ADAM_LR * (m_hat / (jnp.sqrt(v_hat) + ADAM_EPS) + ADAM_WD * w)
    return delta, m, v


def _sum_parts(parts, name, after=None):
    count, r, c = parts.shape
    tr = r // 2 if r % 16 == 0 and r > 64 else r

    def body(p_ref, *rest):
        o_ref = rest[-1]
        acc = p_ref[0].astype(F32)
        for d in range(1, count):
            acc = acc + p_ref[d].astype(F32)
        o_ref[...] = acc

    return _call(
        body, name=name, grid=(r // tr,),
        in_specs=[pl.BlockSpec((count, tr, c), lambda i: (0, i, 0))] + ([ANY] if after is not None else []),
        out_specs=pl.BlockSpec((tr, c), lambda i: (i, 0)),
        out_shape=jax.ShapeDtypeStruct((r, c), F32), compiler_params=_params("parallel"),
    )(parts, *([after] if after is not None else []))


def _row_tile(r):
    for cand in (256, 176, 128, 64, 32, 16, 8):
        if r % cand == 0:
            return cand
    return r


def _adam_layer(w, g, m, v, layer, name, prev=None):
    nl, r, c = w.shape
    tr = _row_tile(r)

    def body(w_ref, g_ref, m_ref, v_ref, *rest):
        go_ref, d_ref, nm_ref, nv_ref = rest[-4:]
        gv = g_ref[...]
        delta, nm, nv = _adam_math(w_ref[0], gv, m_ref[0], v_ref[0])
        go_ref[0] = gv
        d_ref[0] = delta
        nm_ref[0] = nm
        nv_ref[0] = nv

    lspec = pl.BlockSpec((1, tr, c), lambda i: (layer, i, 0))
    sd = jax.ShapeDtypeStruct((nl, r, c), F32)
    extra = list(prev) if prev is not None else []
    return _call(
        body, name=name, grid=(r // tr,),
        in_specs=[lspec, pl.BlockSpec((tr, c), lambda i: (i, 0)), lspec, lspec] + [ANY] * len(extra),
        out_specs=[lspec] * 4, out_shape=[sd] * 4,
        input_output_aliases={4 + t: t for t in range(len(extra))},
        compiler_params=_params("parallel"),
    )(w, g, m, v, *extra)


def _adam(w, g, m, v, name):
    r, c = w.shape
    tr = _row_tile(r)

    def body(w_ref, g_ref, m_ref, v_ref, d_ref, nm_ref, nv_ref):
        delta, nm, nv = _adam_math(w_ref[...], g_ref[...], m_ref[...], v_ref[...])
        d_ref[...] = delta
        nm_ref[...] = nm
        nv_ref[...] = nv

    spec = pl.BlockSpec((tr, c), lambda i: (i, 0))
    sd = jax.ShapeDtypeStruct((r, c), F32)
    return _call(
        body, name=name, grid=(r // tr,), in_specs=[spec] * 4, out_specs=[spec] * 3, out_shape=[sd] * 3,
        compiler_params=_params("parallel"),
    )(w, g, m, v)


def _pad8(rows):
    return jnp.pad(rows, ((0, 8 - rows.shape[0]), (0, 0)))


def kernel(x, norm_mix_g, norm_ffn_g, pool_w, pool_b, pool_scale, kv_norm_g, w_k, w_v, k_norm_g, w_q, q_norm_g, rel_bias, w_o, w_gate, w_up, w_down, loss_target, m_norm_mix_g, m_norm_ffn_g, m_pool_w, m_pool_b, m_pool_scale, m_kv_norm_g, m_w_k, m_w_v, m_k_norm_g, m_w_q, m_q_norm_g, m_rel_bias, m_w_o, m_w_gate, m_w_up, m_w_down, v_norm_mix_g, v_norm_ffn_g, v_pool_w, v_pool_b, v_pool_scale, v_kv_norm_g, v_w_k, v_w_v, v_k_norm_g, v_w_q, v_q_norm_g, v_rel_bias, v_w_o, v_w_gate, v_w_up, v_w_down):
    assert x.shape == (1, S, D) and w_gate.shape == (2, D, F_SHARD) and w_k.shape == (D_SHARD, D)
    xin, target = x[0], loss_target[0]

    ffn_shards = [[w_gate[layer].T.astype(BF16), w_up[layer].T.astype(BF16), w_down[layer].astype(BF16)]
                  for layer in range(2)]
    att_shards = [w_k.astype(BF16), w_v.astype(BF16), w_q[0].astype(BF16), w_o[0].astype(BF16)]
    pool_shard = pool_w[0].astype(BF16).reshape(N_GROUPS * POOL_SHARD, GROUP)
    small = jnp.concatenate([pool_b[0].reshape(1, N_GROUPS * POOL_SHARD), pool_scale], axis=1)

    full0 = _run_exchange(_gather_exchange(ffn_shards[0] + [pool_shard, _pad8(small)]), "gather_layer0")
    ffn_w0 = [a.reshape(F, D) for a in full0[:3]]
    pw_f = full0[3].reshape(N_DEV, N_GROUPS, POOL_SHARD, GROUP).transpose(1, 0, 2, 3).reshape(N_GROUPS, GROUP, GROUP)
    small_f = full0[4][:, 0, :]
    pb_f = small_f[:, :N_GROUPS * POOL_SHARD].reshape(N_DEV, N_GROUPS, POOL_SHARD).transpose(1, 0, 2).reshape(1, D)
    ps_f = small_f[:, N_GROUPS * POOL_SHARD:].reshape(1, D)

    g_mix0, g_mix1 = norm_mix_g[0:1], norm_mix_g[1:2]
    g_ffn0, g_ffn1 = norm_ffn_g[0:1], norm_ffn_g[1:2]
    g_kv = kv_norm_g.reshape(1, D)
    gk_t = jnp.tile(k_norm_g.reshape(1, HEAD), (1, N_HEADS))
    gq_t = jnp.tile(q_norm_g.reshape(1, HEAD), (1, N_HEADS))

    x1, diff = _pool_fwd(xin, g_mix0, pw_f, pb_f, ps_f)
    (x2, hf0, gg0, uu0), full_att = _ffn_fwd(x1, g_ffn0, *ffn_w0, ex=_gather_exchange(att_shards))
    wk_f, wv_f, wq_f, wo_f = [a.reshape(D, D) for a in full_att]
    hkv, h1, kpre, qpre, kp, vp, qq = _kvq_proj(x2, g_kv, g_mix1, wk_f, wv_f, wq_f, gk_t, gq_t)
    bias = _bias_band(_toeplitz_from_table(rel_bias[0]))
    (att,), full1 = _attn_fwd(qq, kp, vp, bias, ex=_gather_exchange(ffn_shards[1]))
    ffn_w1 = [a.reshape(F, D) for a in full1]
    x3 = _mm(att, wo_f, "nn", F32, "attn_out", add=x2)
    (dx4, loss_rows, hf1, gg1, uu1), _ = _ffn_fwd(x3, g_ffn1, *ffn_w1, target=target)

    def blocks(dw):
        return dw.reshape(N_DEV, dw.shape[0] // N_DEV, dw.shape[1])

    (dgg1, duu1, dwg1, dwu1, dwd1), _ = _ffn_bwd_weights(dx4, hf1, gg1, uu1, ffn_w1[2], name="ffn_bwd1")
    parts1 = [blocks(dw) for dw in (dwg1, dwu1, dwd1)]
    (dx3, dg_ffn1), stage1 = _ffn_bwd_input(dx4, x3, g_ffn1, dgg1, duu1, ffn_w1[0], ffn_w1[1], "ffn_dx1",
                                            ex=_pair_exchange(parts1))
    chip1 = _pair_add(parts1, stage1, "pair_add_ffn1")
    datt = _mm(dx3, wo_f, "nt", BF16, "d_attn")
    dwo = _mm(att, dx3, "tn", BF16, "d_wo")
    (dq, dkp, dvp, dband), recv1 = _attn_bwd(qq, kp, vp, bias, datt, ex=_chip_exchange(chip1))
    dqpre, dgq = _head_norm_bwd(dq, qpre, gq_t, "q_norm_bwd")
    dkpre, dgk = _head_norm_bwd(dkp, kpre, gk_t, "k_norm_bwd", row_off=PADK // PROJ_TS)
    dh1 = _mm(dqpre, wq_f, "nt", F32, "d_h1")
    dwq = _mm(h1, dqpre, "tn", BF16, "d_wq")
    dhkv = _mm(dkpre, wk_f, "nt", F32, "d_hkv_k")
    dhkv = _mm(dvp, wv_f, "nt", F32, "d_hkv_v", add=dhkv, skip_rows=PADK)
    dwk = _mm(hkv, dkpre, "tn", BF16, "d_wk")
    dwv = _mm(hkv, dvp, "tn", BF16, "d_wv", skip_rows=PADK)
    parts_att = [blocks(dw) for dw in (dwk, dwv, dwq, dwo)]
    (dx2, dg_mix1, dg_kv), stage_att = _norm_bwd(dx3, x2, [dh1, dhkv], [g_mix1, g_kv], "norm_bwd_mix1",
                                                 ex=_pair_exchange(parts_att))
    chip_att = _pair_add(parts_att, stage_att, "pair_add_att")
    (dgg0, duu0, dwg0, dwu0, dwd0), recv_att = _ffn_bwd_weights(
        dx2, hf0, gg0, uu0, ffn_w0[2], name="ffn_bwd0", ex=_chip_exchange(chip_att))
    parts0 = [blocks(dw) for dw in (dwg0, dwu0, dwd0)]
    (dx1, dg_ffn0), stage0 = _ffn_bwd_input(dx2, x1, g_ffn0, dgg0, duu0, ffn_w0[0], ffn_w0[1], "ffn_dx0",
                                            ex=_pair_exchange(parts0))
    both = _pair_add(parts0, stage0, "pair_add_ffn0", copies=2)
    flight, chip0, land0, token = _chip_exchange_start(both[:3], both[3:], dx1, "scatter_ffn0_start")
    (grad_x, dpw, db_rows, ds_rows, dg_mix0), _ = _pool_bwd(dx1, xin, diff, g_mix0 + token[0:1, 0:1], pw_f, pb_f, ps_f)
    dtp, seg = _bias_grad(dband, token)

    weights = dict(norm_mix_g=norm_mix_g, norm_ffn_g=norm_ffn_g, pool_w=pool_w, pool_b=pool_b,
                   pool_scale=pool_scale, kv_norm_g=kv_norm_g, w_k=w_k, w_v=w_v, k_norm_g=k_norm_g, w_q=w_q,
                   q_norm_g=q_norm_g, rel_bias=rel_bias, w_o=w_o, w_gate=w_gate, w_up=w_up, w_down=w_down)
    mom1 = dict(norm_mix_g=m_norm_mix_g, norm_ffn_g=m_norm_ffn_g, pool_w=m_pool_w, pool_b=m_pool_b,
                pool_scale=m_pool_scale, kv_norm_g=m_kv_norm_g, w_k=m_w_k, w_v=m_w_v, k_norm_g=m_k_norm_g,
                w_q=m_w_q, q_norm_g=m_q_norm_g, rel_bias=m_rel_bias, w_o=m_w_o, w_gate=m_w_gate, w_up=m_w_up,
                w_down=m_w_down)
    mom2 = dict(norm_mix_g=v_norm_mix_g, norm_ffn_g=v_norm_ffn_g, pool_w=v_pool_w, pool_b=v_pool_b,
                pool_scale=v_pool_scale, kv_norm_g=v_kv_norm_g, w_k=v_w_k, w_v=v_w_v, k_norm_g=v_k_norm_g,
                w_q=v_w_q, q_norm_g=v_q_norm_g, rel_bias=v_rel_bias, w_o=v_w_o, w_gate=v_w_gate, w_up=v_w_up,
                w_down=v_w_down)
    names = list(weights)
    grads, deltas, new_m, new_v = {}, {}, {}, {}

    def summed(arrs, tag, after):
        return [_sum_parts(a, "sum_parts_%s%d" % (tag, k), after) for k, a in enumerate(arrs)]

    def adam_flat(nm):
        shape = weights[nm].shape
        flat = lambda a: a.reshape(-1, shape[-1])
        dl, m1, m2 = _adam(flat(weights[nm]), flat(grads[nm]), flat(mom1[nm]), flat(mom2[nm]), "adam_" + nm)
        deltas[nm], new_m[nm], new_v[nm] = dl.reshape(shape), m1.reshape(shape), m2.reshape(shape)

    ffn_names = ("w_gate", "w_up", "w_down")
    def ffn_view(nm, a):
        return a if nm == "w_down" else a.transpose(0, 2, 1)

    sums1 = summed(recv1, "ffn1_", token)
    layer1 = {nm: _adam_layer(ffn_view(nm, weights[nm]), g, ffn_view(nm, mom1[nm]), ffn_view(nm, mom2[nm]), 1,
                              "adam1_" + nm)
              for nm, g in zip(ffn_names, sums1)}
    g_wk, g_wv, g_wq, g_wo = summed(recv_att, "att_", token)
    grads.update(w_k=g_wk, w_v=g_wv, w_q=g_wq[None], w_o=g_wo[None])
    for nm in ("w_k", "w_v", "w_q", "w_o"):
        adam_flat(nm)

    dpw_blocks = dpw.reshape(N_GROUPS, N_DEV, POOL_SHARD, GROUP).transpose(1, 0, 2, 3)
    dpw_blocks = dpw_blocks.reshape(N_DEV, N_GROUPS * POOL_SHARD, GROUP).astype(BF16)
    recv_pool = _run_exchange(_scatter_exchange([dpw_blocks]), "scatter_pool", after=deltas["w_o"])
    misc = jnp.concatenate([dgk[0:1, 0:HEAD], dgq[0:1, 0:HEAD], loss_rows[0:1, 0:1],
                            seg[:, 0, 0].reshape(1, N_HEADS), seg[:, 0, 1].reshape(1, N_HEADS)], axis=1)
    misc = jnp.pad(misc, ((0, 0), (0, D - misc.shape[1])))
    vec_rows = jnp.concatenate([dg_mix0[0:1], dg_mix1[0:1], dg_ffn0[0:1], dg_ffn1[0:1], dg_kv[0:1],
                                db_rows[0:1], ds_rows[0:1], misc], axis=0)
    pack = jnp.concatenate([vec_rows, dtp.reshape(N_HEADS, TOEP)], axis=0)
    tot = _all_reduce_small(pack, "reduce_small", after=recv_pool[0])

    loss = tot[7, 2 * HEAD]
    seg_tot = jnp.stack([tot[7, 2 * HEAD + 1:2 * HEAD + 1 + N_HEADS],
                         tot[7, 2 * HEAD + 1 + N_HEADS:2 * HEAD + 1 + 2 * N_HEADS]], axis=1)
    me = 4 * lax.axis_index("x") + 2 * lax.axis_index("y") + lax.axis_index("c")
    g_pool_b = lax.dynamic_slice_in_dim(tot[5].reshape(N_GROUPS, GROUP), me * POOL_SHARD, POOL_SHARD, axis=1)
    grads.update(
        norm_mix_g=tot[0:2], norm_ffn_g=tot[2:4], kv_norm_g=tot[4], k_norm_g=tot[7, 0:HEAD],
        q_norm_g=tot[7, HEAD:2 * HEAD].reshape(1, HEAD),
        rel_bias=_table_grad_from_toeplitz(tot[8:8 + N_HEADS], seg_tot).reshape(1, N_HEADS, N_REL),
        pool_b=g_pool_b.reshape(1, N_GROUPS, POOL_SHARD),
        pool_scale=lax.dynamic_slice_in_dim(tot[6:7], me * D_SHARD, D_SHARD, axis=1),
        pool_w=summed(recv_pool, "pool_", None)[0].reshape(1, N_GROUPS, POOL_SHARD, GROUP))
    adam_flat("pool_w")
    small_names = [nm for nm in names if nm not in ffn_names + ("w_k", "w_v", "w_q", "w_o", "pool_w")]

    def pack_small(tree):
        cols = []
        for nm in small_names:
            flat = tree[nm].reshape(-1)
            cols.append(jnp.pad(flat, (0, -flat.shape[0] % 1024)))
        return jnp.concatenate(cols).reshape(-1, 128)

    dl, m1, m2 = _adam(pack_small(weights), pack_small(grads), pack_small(mom1), pack_small(mom2), "adam_small")

    def unpack_small(packed, out):
        flat, off = packed.reshape(-1), 0
        for nm in small_names:
            size = weights[nm].size
            out[nm] = flat[off:off + size].reshape(weights[nm].shape)
            off += size + (-size % 1024)

    unpack_small(dl, deltas)
    unpack_small(m1, new_m)
    unpack_small(m2, new_v)

    recv0 = _chip_exchange_wait(flight, chip0, land0, dl, "scatter_ffn0_wait")
    sums0 = summed(recv0, "ffn0_", None)
    for nm, g in zip(ffn_names, sums0):
        done = _adam_layer(ffn_view(nm, weights[nm]), g, ffn_view(nm, mom1[nm]), ffn_view(nm, mom2[nm]), 0,
                           "adam0_" + nm, prev=layer1[nm])
        grads[nm], deltas[nm], new_m[nm], new_v[nm] = [ffn_view(nm, a) for a in done]

    return (loss, grad_x[None], *[grads[nm] for nm in names], *[deltas[nm] for nm in names],
            *[new_m[nm] for nm in names], *[new_v[nm] for nm in names])
```

```python
import functools

import jax
import jax.numpy as jnp
from jax import lax
from jax.experimental import pallas as pl
from jax.experimental.pallas import tpu as pltpu

F32 = jnp.float32
BF16 = jnp.bfloat16
MESH_ID = pl.DeviceIdType.MESH

N_DEV = 8
S = 2048
D = 1024
F = 2816
F_SHARD = F // N_DEV
D_SHARD = D // N_DEV
N_GROUPS = 4
GROUP = D // N_GROUPS
POOL_SHARD = GROUP // N_DEV
MAX_WIN = 16
HEAD = 64
N_HEADS = D // HEAD
CHUNK = 64
LEFT = 8
QB = 4 * CHUNK
KB = QB + LEFT * CHUNK
PADK = LEFT * CHUNK
TOEP = 1024
N_REL = 257
MAX_REL = 128
EPS = 1e-6
NEG_INF = -1e30
ATTN_SCALE = HEAD ** -0.5

ADAM_LR = 0.001
ADAM_B1 = 0.9
ADAM_B2 = 0.999
ADAM_EPS = 1e-08
ADAM_WD = 0.01
ADAM_STEP = 10

VMEM_LIMIT = 52 * 1024 * 1024

ANY = pl.BlockSpec(memory_space=pl.ANY)
VMEM = pl.BlockSpec(memory_space=pltpu.VMEM)


def _call(body, **kw):
    return pl.pallas_call(body, **kw)


def _params(*sem):
    return pltpu.CompilerParams(dimension_semantics=sem, vmem_limit_bytes=VMEM_LIMIT)


def _dot(a, b, dims):
    return lax.dot_general(a, b, (dims, ((), ())), preferred_element_type=F32)


def _nn(a, b):
    return _dot(a, b, ((1,), (0,)))


def _nt(a, b):
    return _dot(a, b, ((1,), (1,)))


def _tn(a, b):
    return _dot(a, b, ((0,), (0,)))


def _rstd(x):
    return lax.rsqrt(jnp.mean(x * x, axis=-1, keepdims=True) + EPS)


def _rms_bwd(dh, x, r, g):
    gd = dh * g
    return r * gd - x * (r * r * r) * jnp.mean(gd * x, axis=-1, keepdims=True)


def _colsum8(v):
    return jnp.broadcast_to(jnp.sum(v, axis=0, keepdims=True), (8, v.shape[1]))


def _seg_sum(v):
    r = lax.broadcasted_iota(jnp.int32, (128, 128), 0) // HEAD
    c = lax.broadcasted_iota(jnp.int32, (128, 128), 1) // HEAD
    ones = jnp.where(r == c, 1.0, 0.0).astype(BF16)
    out = []
    for blk in range(v.shape[1] // 128):
        part = v[:, blk * 128:(blk + 1) * 128]
        hi = part.astype(BF16)
        rest = part - hi.astype(F32)
        mid = rest.astype(BF16)
        lo = (rest - mid.astype(F32)).astype(BF16)
        out.append(_nn(hi, ones) + _nn(mid, ones) + _nn(lo, ones))
    return jnp.concatenate(out, axis=1)


def _place():
    return lax.axis_index("x"), lax.axis_index("y"), lax.axis_index("c")


class _Exchange:
    def __init__(self, ins, out_shape, sems, start, mid, finish):
        self.ins, self.out_shape, self.sems = list(ins), list(out_shape), list(sems)
        self.start, self.mid, self.finish = start, mid, finish


def _gather_exchange(shards):
    n = len(shards)

    def tools(ins, outs, sems):
        send_sems, recv_sems, local_sems = sems
        x, y, c = _place()
        me, sibling = (x, y, c), (x, y, 1 - c)
        chips = [(1 - x, y), (x, 1 - y), (1 - x, 1 - y)]

        def slot(k, px, py, pc):
            return outs[k].at[4 * px + 2 * py + pc]

        def copy(k, s, block, to, own=False):
            return pltpu.make_async_remote_copy(
                src_ref=ins[k] if own else slot(k, *block), dst_ref=slot(k, *block),
                send_sem=send_sems.at[k, s], recv_sem=recv_sems.at[k, s], device_id=to, device_id_type=MESH_ID)

        def mine():
            return [pltpu.make_async_copy(ins[k], slot(k, *me), local_sems.at[k]) for k in range(n)]

        def first():
            out = []
            for k in range(n):
                out.append(copy(k, 0, me, sibling, own=True))
                out += [copy(k, 1 + j, me, (*chip, c), own=True) for j, chip in enumerate(chips)]
            return out

        def landed(j):
            return [copy(k, 1 + j, (*chips[j], c), me) for k in range(n)]

        def passed(j):
            return [copy(k, 4 + j, (*chips[j], c), sibling) for k in range(n)]

        def from_sibling():
            out = []
            for k in range(n):
                out.append(copy(k, 0, sibling, me))
                out += [copy(k, 4 + j, (*chip, 1 - c), me) for j, chip in enumerate(chips)]
            return out

        return mine, first, landed, passed, from_sibling

    def start(ins, outs, sems):
        mine, first, _, _, _ = tools(ins, outs, sems)
        for cp in mine() + first():
            cp.start()

    def mid(ins, outs, sems):
        _, _, landed, passed, _ = tools(ins, outs, sems)
        for j in range(3):
            for arrived, onward in zip(landed(j), passed(j)):
                arrived.wait_recv()
                onward.start()

    def finish(ins, outs, sems):
        mine, first, _, passed, from_sibling = tools(ins, outs, sems)
        for cp in from_sibling():
            cp.wait_recv()
        for cp in first() + passed(0) + passed(1) + passed(2):
            cp.wait_send()
        for cp in mine():
            cp.wait()

    return _Exchange(
        shards, [jax.ShapeDtypeStruct((N_DEV,) + a.shape, a.dtype) for a in shards],
        [pltpu.SemaphoreType.DMA((n, 7)), pltpu.SemaphoreType.DMA((n, 7)), pltpu.SemaphoreType.DMA((n,))],
        start, mid, finish)


def _peer(x, y, c, m):
    px = 1 - x if m & 4 else x
    py = 1 - y if m & 2 else y
    pc = 1 - c if m & 1 else c
    return px, py, pc


def _scatter_exchange(parts):
    n = len(parts)

    def tools(ins, outs, sems):
        send_sems, recv_sems, local_sems = sems
        x, y, c = _place()
        me = 4 * x + 2 * y + c
        def mine():
            return [pltpu.make_async_copy(ins[k].at[me], outs[k].at[me], local_sems.at[k]) for k in range(n)]

        def remote(dst_is_mine):
            out = []
            for m in range(1, N_DEV):
                px, py, pc = _peer(x, y, c, m)
                peer = 4 * px + 2 * py + pc
                for k in range(n):
                    out.append(pltpu.make_async_remote_copy(
                        src_ref=ins[k].at[peer], dst_ref=outs[k].at[me if dst_is_mine else peer],
                        send_sem=send_sems.at[k, m - 1], recv_sem=recv_sems.at[k, m - 1],
                        device_id=(px, py, pc), device_id_type=MESH_ID))
            return out

        return mine, remote

    def start(ins, outs, sems):
        mine, remote = tools(ins, outs, sems)
        for cp in mine() + remote(True):
            cp.start()

    def mid(ins, outs, sems):
        pass

    def finish(ins, outs, sems):
        mine, remote = tools(ins, outs, sems)
        for cp in remote(False):
            cp.wait_recv()
        for cp in remote(True):
            cp.wait_send()
        for cp in mine():
            cp.wait()

    return _Exchange(
        parts, [jax.ShapeDtypeStruct(a.shape, a.dtype) for a in parts],
        [pltpu.SemaphoreType.DMA((n, 7)), pltpu.SemaphoreType.DMA((n, 7)), pltpu.SemaphoreType.DMA((n,))],
        start, mid, finish)


N_CHIPS = N_DEV // 2


def _pair_exchange(parts):
    n = len(parts)

    def copies(ins, outs, sems):
        send_sems, recv_sems = sems
        x, y, c = _place()
        return [pltpu.make_async_remote_copy(
            src_ref=ins[k].at[2 * q + 1 - c], dst_ref=outs[k].at[q], send_sem=send_sems.at[k, q],
            recv_sem=recv_sems.at[k, q], device_id=(x, y, 1 - c), device_id_type=MESH_ID)
            for k in range(n) for q in range(N_CHIPS)]

    def start(ins, outs, sems):
        for cp in copies(ins, outs, sems):
            cp.start()

    def mid(ins, outs, sems):
        pass

    def finish(ins, outs, sems):
        for cp in copies(ins, outs, sems):
            cp.wait_recv()
        for cp in copies(ins, outs, sems):
            cp.wait_send()

    return _Exchange(
        parts, [jax.ShapeDtypeStruct((N_CHIPS,) + a.shape[1:], a.dtype) for a in parts],
        [pltpu.SemaphoreType.DMA((n, N_CHIPS)), pltpu.SemaphoreType.DMA((n, N_CHIPS))], start, mid, finish)


def _pair_add(parts, stage, name, copies=1):
    n = len(parts)
    core = lax.axis_index("c").reshape(1)

    def body(core_ref, *refs):
        for k in range(n):
            mine, theirs = refs[k], refs[n + k]
            total = (mine[0, 0].astype(F32) + theirs[0].astype(F32)).astype(BF16)
            for rep in range(copies):
                refs[(2 + rep) * n + k][0] = total

    in_specs, out_specs = [], []
    for a in parts:
        _, r, cdim = a.shape
        in_specs.append(pl.BlockSpec((1, 1, r // 2, cdim), lambda q, i, core_ref: (q, core_ref[0], i, 0)))
    for a in parts:
        _, r, cdim = a.shape
        in_specs.append(pl.BlockSpec((1, r // 2, cdim), lambda q, i, core_ref: (q, i, 0)))
        out_specs.append(pl.BlockSpec((1, r // 2, cdim), lambda q, i, core_ref: (q, i, 0)))
    return list(_call(
        body, name=name,
        grid_spec=pltpu.PrefetchScalarGridSpec(num_scalar_prefetch=1, grid=(N_CHIPS, 2), in_specs=in_specs,
                                               out_specs=out_specs * copies),
        out_shape=[jax.ShapeDtypeStruct(s.shape, BF16) for s in stage] * copies,
        compiler_params=_params("arbitrary", "arbitrary"),
    )(core, *[a.reshape((N_CHIPS, 2) + a.shape[1:]) for a in parts], *stage))


HBM = pl.BlockSpec(memory_space=pltpu.HBM)
SEMAPHORES = pl.BlockSpec(memory_space=pltpu.SEMAPHORE)


def _chip_copies(srcs, lands, send_sems, recv_sems, mine_is_dst):
    x, y, c = _place()
    me = 2 * x + y
    out = []
    for m in range(1, N_CHIPS):
        px, py, _ = _peer(x, y, c, 2 * m)
        peer = 2 * px + py
        for k in range(len(srcs)):
            pair = k * (N_CHIPS - 1) + m - 1
            out.append(pltpu.make_async_remote_copy(
                src_ref=srcs[k].at[peer], dst_ref=lands[k].at[me if mine_is_dst else peer],
                send_sem=send_sems[pair], recv_sem=recv_sems[pair],
                device_id=(px, py, c), device_id_type=MESH_ID))
    return out


def _chip_exchange_start(chip_parts, lands, after, name):
    n = len(chip_parts)
    pairs = n * (N_CHIPS - 1)

    def body(*refs):
        srcs, zones = refs[:n], refs[n:2 * n]
        sems = refs[2 * n + 1:2 * n + 1 + 2 * pairs]
        token = refs[-1]
        for cp in _chip_copies(srcs, zones, sems[:pairs], sems[pairs:], True):
            cp.start()
        token[...] = jnp.zeros(token.shape, F32)

    thru = [pltpu.HBM(a.shape, a.dtype) for a in chip_parts + lands]
    hbm = [pltpu.with_memory_space_constraint(a, pltpu.HBM) for a in chip_parts + lands]
    res = _call(
        body, name=name,
        out_shape=[pltpu.SemaphoreType.DMA(())] * (2 * pairs) + thru + [jax.ShapeDtypeStruct((8, 128), F32)],
        in_specs=[HBM] * (2 * n) + [ANY], out_specs=[SEMAPHORES] * (2 * pairs) + [HBM] * (2 * n) + [VMEM],
        input_output_aliases={i: 2 * pairs + i for i in range(2 * n)},
        compiler_params=pltpu.CompilerParams(has_side_effects=pltpu.SideEffectType.DATAFLOW_SIDE_EFFECTING),
    )(*hbm, after)
    sems, rest = list(res[:2 * pairs]), res[2 * pairs:]
    return sems, list(rest[:n]), list(rest[n:2 * n]), rest[-1]


def _chip_exchange_wait(sems, chip_parts, lands, after, name):
    n = len(chip_parts)
    pairs = n * (N_CHIPS - 1)

    def body(*refs):
        srcs, zones = refs[:n], refs[n:2 * n]
        sem_refs = refs[2 * n:2 * n + 2 * pairs]
        for cp in _chip_copies(srcs, zones, sem_refs[:pairs], sem_refs[pairs:], False):
            cp.wait_send()
            cp.wait_recv()

    thru = [pltpu.HBM(a.shape, a.dtype) for a in chip_parts + lands]
    res = _call(
        body, name=name, out_shape=thru,
        in_specs=[HBM] * (2 * n) + [SEMAPHORES] * (2 * pairs) + [ANY], out_specs=[HBM] * (2 * n),
        input_output_aliases={i: i for i in range(2 * n)},
        compiler_params=pltpu.CompilerParams(has_side_effects=pltpu.SideEffectType.DATAFLOW_SIDE_EFFECTING),
    )(*chip_parts, *lands, *sems, after)
    return list(res[n:])


def _chip_exchange(chip_parts):
    n = len(chip_parts)

    def tools(ins, outs, sems):
        send_sems, recv_sems, local_sems = sems
        x, y, c = _place()
        me = 2 * x + y

        def mine():
            return [pltpu.make_async_copy(ins[k].at[me], outs[k].at[me], local_sems.at[k]) for k in range(n)]

        def remote(dst_is_mine):
            out = []
            for m in range(1, N_CHIPS):
                px, py, _ = _peer(x, y, c, 2 * m)
                peer = 2 * px + py
                for k in range(n):
                    out.append(pltpu.make_async_remote_copy(
                        src_ref=ins[k].at[peer], dst_ref=outs[k].at[me if dst_is_mine else peer],
                        send_sem=send_sems.at[k, m - 1], recv_sem=recv_sems.at[k, m - 1],
                        device_id=(px, py, c), device_id_type=MESH_ID))
            return out

        return mine, remote

    def start(ins, outs, sems):
        mine, remote = tools(ins, outs, sems)
        for cp in mine() + remote(True):
            cp.start()

    def mid(ins, outs, sems):
        pass

    def finish(ins, outs, sems):
        mine, remote = tools(ins, outs, sems)
        for cp in remote(False):
            cp.wait_recv()
        for cp in remote(True):
            cp.wait_send()
        for cp in mine():
            cp.wait()

    return _Exchange(
        chip_parts, [jax.ShapeDtypeStruct(a.shape, a.dtype) for a in chip_parts],
        [pltpu.SemaphoreType.DMA((n, N_CHIPS - 1)), pltpu.SemaphoreType.DMA((n, N_CHIPS - 1)),
         pltpu.SemaphoreType.DMA((n,))], start, mid, finish)


def _run_exchange(ex, name, after=None):
    n_in, n_out = len(ex.ins), len(ex.out_shape)
    order = [] if after is None else [after]

    def body(*refs):
        ins, outs = refs[:n_in], refs[n_in + len(order):n_in + len(order) + n_out]
        sems = refs[n_in + len(order) + n_out:]
        ex.start(ins, outs, sems)
        ex.mid(ins, outs, sems)
        ex.finish(ins, outs, sems)

    return list(_call(body, name=name, out_shape=ex.out_shape, in_specs=[ANY] * (n_in + len(order)),
                      out_specs=[ANY] * n_out, scratch_shapes=ex.sems)(*ex.ins, *order))


def _call_hosting(body, ex, phases, *, in_specs, out_specs, out_shape, scratch_shapes, args, **kw):
    n_in, n_out, n_scr = len(in_specs), len(out_specs), len(scratch_shapes)
    if ex is None:
        res = _call(body, in_specs=in_specs, out_specs=out_specs, out_shape=out_shape,
                    scratch_shapes=scratch_shapes, **kw)(*args)
        return list(res), []
    n_xin, n_xout = len(ex.ins), len(ex.out_shape)

    def hosting(*refs):
        a, b = n_in, n_in + n_xin
        c, d = b + n_out, b + n_out + n_xout
        ins, xins, outs, xouts = refs[:a], refs[a:b], refs[b:c], refs[c:d]
        scr, sems = refs[d:d + n_scr], refs[d + n_scr:]
        first, mid, last = phases()

        @pl.when(first)
        def _():
            ex.start(xins, xouts, sems)

        body(*ins, *outs, *scr)

        @pl.when(mid)
        def _():
            ex.mid(xins, xouts, sems)

        @pl.when(last)
        def _():
            ex.finish(xins, xouts, sems)

    res = _call(hosting, in_specs=list(in_specs) + [ANY] * n_xin, out_specs=list(out_specs) + [ANY] * n_xout,
                out_shape=list(out_shape) + ex.out_shape, scratch_shapes=list(scratch_shapes) + ex.sems,
                **kw)(*args, *ex.ins)
    return list(res[:n_out]), list(res[n_out:])


def _grid_phases(dims, mid_fraction=0.8):
    total = 1
    for d in dims:
        total *= d
    mid = min(max(int(total * mid_fraction), 1), total - 1)

    def phases():
        step = pl.program_id(0)
        for axis in range(1, len(dims)):
            step = step * dims[axis] + pl.program_id(axis)
        return step == 0, step == mid, step == total - 1
    return phases


def _all_reduce_small(pack, name, after):
    rows = pack.shape[0]

    def body(in_ref, _, out_ref, recv, send_sems, recv_sems):
        x, y, c = _place()
        me = 4 * x + 2 * y + c
        recv[me] = in_ref[...]
        sent = []
        for m in range(1, N_DEV):
            px, py, pc = _peer(x, y, c, m)
            cp = pltpu.make_async_remote_copy(
                src_ref=in_ref, dst_ref=recv.at[me], send_sem=send_sems.at[m - 1], recv_sem=recv_sems.at[m - 1],
                device_id=(px, py, pc), device_id_type=MESH_ID)
            cp.start()
            sent.append(cp)
        for m in range(1, N_DEV):
            px, py, pc = _peer(x, y, c, m)
            peer = 4 * px + 2 * py + pc
            pltpu.make_async_remote_copy(
                src_ref=in_ref, dst_ref=recv.at[peer], send_sem=send_sems.at[m - 1], recv_sem=recv_sems.at[m - 1],
                device_id=(px, py, pc), device_id_type=MESH_ID).wait_recv()
        acc = recv[0]
        for d in range(1, N_DEV):
            acc = acc + recv[d]
        out_ref[...] = acc
        for cp in sent:
            cp.wait_send()

    return _call(
        body, name=name, out_shape=jax.ShapeDtypeStruct(pack.shape, F32), in_specs=[VMEM, ANY], out_specs=VMEM,
        scratch_shapes=[pltpu.VMEM((N_DEV, rows, pack.shape[1]), F32), pltpu.SemaphoreType.DMA((7,)),
                        pltpu.SemaphoreType.DMA((7,))],
        compiler_params=pltpu.CompilerParams(vmem_limit_bytes=VMEM_LIMIT),
    )(pack, after)


POOL_TS = 256


def _pool_counts(first_row, rows, win):
    t = first_row + lax.broadcasted_iota(jnp.int32, (rows, 1), 0)
    return jnp.minimum(t + 1, win).astype(F32)


def _pool_fwd(x, g, w, b, scale):
    nt = S // POOL_TS

    def body(x_ref, g_ref, w_ref, b_ref, s_ref, out_ref, diff_ref, ext):
        i = pl.program_id(0)

        @pl.when(i == 0)
        def _():
            ext[0:MAX_WIN, :] = jnp.zeros((MAX_WIN, D), F32)

        @pl.when(i > 0)
        def _():
            ext[0:MAX_WIN, :] = ext[POOL_TS:POOL_TS + MAX_WIN, :]

        xv = x_ref[...]
        h = xv * _rstd(xv) * g_ref[...]
        ext[MAX_WIN:, :] = h
        for gi in range(N_GROUPS):
            win = 2 << gi
            cols = slice(gi * GROUP, (gi + 1) * GROUP)
            sm = ext[:, cols]
            k = 1
            while k < win:
                sm = sm + pltpu.roll(sm, k, axis=0)
                k *= 2
            pooled = sm[MAX_WIN:, :] / _pool_counts(i * POOL_TS, POOL_TS, win)
            diff = (pooled - h[:, cols]).astype(BF16)
            yv = (_nn(diff, w_ref[gi]) + b_ref[:, cols]) * s_ref[:, cols]
            out_ref[:, cols] = xv[:, cols] + yv
            diff_ref[:, cols] = diff

    row = pl.BlockSpec((1, D), lambda i: (0, 0))
    tile = pl.BlockSpec((POOL_TS, D), lambda i: (i, 0))
    return _call(
        body, name="pool_fwd", grid=(nt,),
        in_specs=[tile, row, pl.BlockSpec((N_GROUPS, GROUP, GROUP), lambda i: (0, 0, 0)), row, row],
        out_specs=[tile, tile],
        out_shape=[jax.ShapeDtypeStruct((S, D), F32), jax.ShapeDtypeStruct((S, D), BF16)],
        scratch_shapes=[pltpu.VMEM((POOL_TS + MAX_WIN, D), F32)],
        compiler_params=_params("arbitrary"),
    )(x, g, w, b, scale)


def _pool_bwd(dy, x, diff, g, w, b, scale, ex=None):
    nt = S // POOL_TS

    def body(dy_ref, x_ref, diff_ref, g_ref, w_ref, b_ref, s_ref, gx_ref, dw_ref, db_ref, ds_ref, dg_ref, ext, dh):
        i = pl.program_id(0)
        first_row = (nt - 1 - i) * POOL_TS

        @pl.when(i == 0)
        def _():
            ext[POOL_TS:, :] = jnp.zeros((MAX_WIN, D), F32)
            dw_ref[...] = jnp.zeros(dw_ref.shape, F32)
            db_ref[...] = jnp.zeros(db_ref.shape, F32)
            ds_ref[...] = jnp.zeros(ds_ref.shape, F32)
            dg_ref[...] = jnp.zeros(dg_ref.shape, F32)

        @pl.when(i > 0)
        def _():
            ext[POOL_TS:, :] = ext[0:MAX_WIN, :]

        dyv = dy_ref[...]
        for gi in range(N_GROUPS):
            win = 2 << gi
            cols = slice(gi * GROUP, (gi + 1) * GROUP)
            dfb = diff_ref[:, cols]
            z = _nn(dfb, w_ref[gi]) + b_ref[:, cols]
            dyg = dyv[:, cols]
            ds_ref[:, cols] += _colsum8(dyg * z)
            dz = dyg * s_ref[:, cols]
            db_ref[:, cols] += _colsum8(dz)
            dzb = dz.astype(BF16)
            dw_ref[gi] += _tn(dfb, dzb)
            ddiff = _nt(dzb, w_ref[gi])
            ext[0:POOL_TS, cols] = ddiff / _pool_counts(first_row, POOL_TS, win)
            sm = ext[:, cols]
            k = 1
            while k < win:
                sm = sm + pltpu.roll(sm, POOL_TS + MAX_WIN - k, axis=0)
                k *= 2
            dh[:, cols] = sm[0:POOL_TS, :] - ddiff
        xv = x_ref[...]
        r = _rstd(xv)
        gv = g_ref[...]
        dhv = dh[...]
        dg_ref[...] += _colsum8(dhv * xv * r)
        gx_ref[...] = dyv + _rms_bwd(dhv, xv, r, gv)

    row = pl.BlockSpec((1, D), lambda i: (0, 0))
    tile = pl.BlockSpec((POOL_TS, D), lambda i: (nt - 1 - i, 0))
    acc = pl.BlockSpec((8, D), lambda i: (0, 0))
    wspec = pl.BlockSpec((N_GROUPS, GROUP, GROUP), lambda i: (0, 0, 0))
    return _call_hosting(
        body, ex, _grid_phases((nt,)), name="pool_bwd", grid=(nt,),
        in_specs=[tile, tile, tile, row, wspec, row, row],
        out_specs=[tile, wspec, acc, acc, acc],
        out_shape=[jax.ShapeDtypeStruct((S, D), F32), jax.ShapeDtypeStruct((N_GROUPS, GROUP, GROUP), F32),
                   jax.ShapeDtypeStruct((8, D), F32), jax.ShapeDtypeStruct((8, D), F32),
                   jax.ShapeDtypeStruct((8, D), F32)],
        scratch_shapes=[pltpu.VMEM((POOL_TS + MAX_WIN, D), F32), pltpu.VMEM((POOL_TS, D), F32)],
        args=(dy, x, diff, g, w, b, scale), compiler_params=_params("arbitrary"),
    )


FFN_TS = min(S, 1024)
FFN_TF = 256


def _ffn_fwd(x, g, wg_t, wu_t, wd, target=None, ex=None):
    ni, nj = S // FFN_TS, F // FFN_TF
    with_loss = target is not None

    def body(*refs):
        if with_loss:
            x_ref, g_ref, wg_ref, wu_ref, wd_ref, t_ref, out_ref, loss_ref, h_ref, gg_ref, uu_ref, hs, acc = refs
        else:
            x_ref, g_ref, wg_ref, wu_ref, wd_ref, out_ref, h_ref, gg_ref, uu_ref, hs, acc = refs
        i, j = pl.program_id(0), pl.program_id(1)

        @pl.when(j == 0)
        def _():
            xv = x_ref[...]
            hb = (xv * _rstd(xv) * g_ref[...]).astype(BF16)
            hs[...] = hb
            h_ref[...] = hb
            acc[...] = jnp.zeros(acc.shape, F32)

        hb = hs[...]
        gg = _nt(hb, wg_ref[...])
        uu = _nt(hb, wu_ref[...])
        gg_ref[...] = gg
        uu_ref[...] = uu
        a = (gg * jax.nn.sigmoid(gg) * uu).astype(BF16)
        acc[...] += _nn(a, wd_ref[...])

        @pl.when(j == nj - 1)
        def _():
            yv = x_ref[...] + acc[...]
            if with_loss:
                err = yv - t_ref[...]
                out_ref[...] = err * (1.0 / D)
                part = jnp.sum(err * err) * (0.5 / D)

                @pl.when(i == 0)
                def _():
                    loss_ref[...] = jnp.zeros(loss_ref.shape, F32)

                loss_ref[...] += jnp.broadcast_to(part, loss_ref.shape)
            else:
                out_ref[...] = yv

    xt = pl.BlockSpec((FFN_TS, D), lambda i, j: (i, 0))
    row = pl.BlockSpec((1, D), lambda i, j: (0, 0))
    wt = pl.BlockSpec((FFN_TF, D), lambda i, j: (j, 0))
    gt = pl.BlockSpec((FFN_TS, FFN_TF), lambda i, j: (i, j))
    in_specs = [xt, row, wt, wt, wt] + ([xt] if with_loss else [])
    out_specs = [xt] + ([pl.BlockSpec((8, 128), lambda i, j: (0, 0))] if with_loss else []) + [xt, gt, gt]
    out_shape = ([jax.ShapeDtypeStruct((S, D), F32)] + ([jax.ShapeDtypeStruct((8, 128), F32)] if with_loss else [])
                 + [jax.ShapeDtypeStruct((S, D), BF16), jax.ShapeDtypeStruct((S, F), F32),
                    jax.ShapeDtypeStruct((S, F), F32)])
    args = (x, g, wg_t, wu_t, wd) + ((target,) if with_loss else ())
    return _call_hosting(
        body, ex, _grid_phases((ni, nj), 0.8), name="ffn_fwd_loss" if with_loss else "ffn_fwd", grid=(ni, nj),
        in_specs=in_specs, out_specs=out_specs, out_shape=out_shape,
        scratch_shapes=[pltpu.VMEM((FFN_TS, D), BF16), pltpu.VMEM((FFN_TS, D), F32)], args=args,
        compiler_params=_params("arbitrary", "arbitrary"),
    )


def _ffn_bwd_weights(dout, h, gg, uu, wd, name, ex=None):
    nj = F // FFN_TF

    def body(do_ref, h_ref, gg_ref, uu_ref, wd_ref, dg_ref, du_ref, dwg_ref, dwu_ref, dwd_ref, dob):
        @pl.when(pl.program_id(0) == 0)
        def _():
            dob[...] = do_ref[...].astype(BF16)

        hb, dov = h_ref[...], dob[...]
        gv, uv = gg_ref[...], uu_ref[...]
        da = _nt(dov, wd_ref[...])
        sg = jax.nn.sigmoid(gv)
        sl = gv * sg
        ab = (sl * uv).astype(BF16)
        dub = (da * sl).astype(BF16)
        dgb = (da * uv * (sg * (1.0 + gv * (1.0 - sg)))).astype(BF16)
        dg_ref[...] = dgb
        du_ref[...] = dub
        dwg_ref[...] = _tn(dgb, hb).astype(BF16)
        dwu_ref[...] = _tn(dub, hb).astype(BF16)
        dwd_ref[...] = _tn(ab, dov).astype(BF16)

    once = pl.Buffered(1)
    whole = lambda: pl.BlockSpec((S, D), lambda j, _: (0, 0), pipeline_mode=once)
    wt = pl.BlockSpec((FFN_TF, D), lambda j, _: (j, 0))
    gt = pl.BlockSpec((S, FFN_TF), lambda j, _: (0, j))
    return _call_hosting(
        body, ex, _grid_phases((nj, 1)), name=name, grid=(nj, 1),
        in_specs=[whole(), whole(), gt, gt, wt], out_specs=[gt, gt, wt, wt, wt],
        out_shape=[jax.ShapeDtypeStruct((S, F), BF16)] * 2 + [jax.ShapeDtypeStruct((F, D), BF16)] * 3,
        scratch_shapes=[pltpu.VMEM((S, D), BF16)], args=(dout, h, gg, uu, wd),
        compiler_params=_params("arbitrary", "arbitrary"),
    )


BWD_TS = 512


def _ffn_bwd_input(dres, x, g, dg, du, wg_t, wu_t, name, ex=None):
    nt = S // BWD_TS

    def body(dres_ref, x_ref, g_ref, dg_ref, du_ref, wg_ref, wu_ref, dx_ref, dgam_ref):
        dh = _nn(dg_ref[...], wg_ref[...]) + _nn(du_ref[...], wu_ref[...])
        xv = x_ref[...]
        r = _rstd(xv)
        dx_ref[...] = dres_ref[...] + _rms_bwd(dh, xv, r, g_ref[...])

        @pl.when(pl.program_id(0) == 0)
        def _():
            dgam_ref[...] = jnp.zeros(dgam_ref.shape, F32)

        dgam_ref[...] += _colsum8(dh * xv * r)

    tile = pl.BlockSpec((BWD_TS, D), lambda i: (i, 0))
    ftile = pl.BlockSpec((BWD_TS, F), lambda i: (i, 0))
    wspec = lambda: pl.BlockSpec((F, D), lambda i: (0, 0), pipeline_mode=pl.Buffered(1))
    return _call_hosting(
        body, ex, _grid_phases((nt,)), name=name, grid=(nt,),
        in_specs=[tile, tile, pl.BlockSpec((1, D), lambda i: (0, 0)), ftile, ftile, wspec(), wspec()],
        out_specs=[tile, pl.BlockSpec((8, D), lambda i: (0, 0))],
        out_shape=[jax.ShapeDtypeStruct((S, D), F32), jax.ShapeDtypeStruct((8, D), F32)],
        scratch_shapes=[], args=(dres, x, g, dg, du, wg_t, wu_t), compiler_params=_params("arbitrary"),
    )


NORM_TS = 256


def _norm_bwd(dres, x, dhs, gs, name, ex=None):
    n = len(dhs)
    nt = S // NORM_TS

    def body(*refs):
        dres_ref, x_ref = refs[:2]
        dh_refs, g_refs = refs[2:2 + n], refs[2 + n:2 + 2 * n]
        dx_ref, dg_refs = refs[2 + 2 * n], refs[3 + 2 * n:]
        i = pl.program_id(0)
        xv = x_ref[...]
        r = _rstd(xv)
        dx = dres_ref[...]
        for k in range(n):
            dhv = dh_refs[k][...]
            dx = dx + _rms_bwd(dhv, xv, r, g_refs[k][...])

            @pl.when(i == 0)
            def _():
                dg_refs[k][...] = jnp.zeros((8, D), F32)

            dg_refs[k][...] += _colsum8(dhv * xv * r)
        dx_ref[...] = dx

    tile = pl.BlockSpec((NORM_TS, D), lambda i: (i, 0))
    row = pl.BlockSpec((1, D), lambda i: (0, 0))
    acc = pl.BlockSpec((8, D), lambda i: (0, 0))
    return _call_hosting(
        body, ex, _grid_phases((nt,)), name=name, grid=(nt,),
        in_specs=[tile, tile] + [tile] * n + [row] * n,
        out_specs=[tile] + [acc] * n,
        out_shape=[jax.ShapeDtypeStruct((S, D), F32)] + [jax.ShapeDtypeStruct((8, D), F32)] * n,
        scratch_shapes=[], args=(dres, x, *dhs, *gs), compiler_params=_params("arbitrary"),
    )


def _mm(a, b, mode, out_dtype, name, add=None, skip_rows=0):
    if mode == "nn":
        (m, kd), n = (a.shape[0] - skip_rows, a.shape[1]), b.shape[1]
    elif mode == "nt":
        (m, kd), n = (a.shape[0] - skip_rows, a.shape[1]), b.shape[0]
    else:
        (kd, m), n = a.shape, b.shape[1]
    tm, tn, tk = min(m, 1024), min(n, 1024), min(kd, 1024)
    if skip_rows and mode == "tn":
        tk = min(tk, skip_rows)
    elif skip_rows:
        tm = min(tm, skip_rows)
    skip_a = skip_rows // tm if mode != "tn" else 0
    skip_b = skip_rows // tk if mode == "tn" else 0
    assert skip_rows == skip_a * tm + skip_b * tk
    nk = kd // tk
    dot = {"nn": _nn, "nt": _nt, "tn": _tn}[mode]

    def body(*refs):
        if add is None:
            a_ref, b_ref, o_ref, acc = refs
        else:
            a_ref, b_ref, add_ref, o_ref, acc = refs
        k = pl.program_id(2)

        @pl.when(k == 0)
        def _():
            acc[...] = jnp.zeros(acc.shape, F32)

        acc[...] += dot(a_ref[...].astype(BF16), b_ref[...].astype(BF16))

        @pl.when(k == nk - 1)
        def _():
            res = acc[...]
            if add is not None:
                res = res + add_ref[...]
            o_ref[...] = res.astype(out_dtype)

    if mode == "tn":
        a_spec = pl.BlockSpec((tk, tm), lambda i, j, k: (k, i))
        b_spec = pl.BlockSpec((tk, tn), lambda i, j, k: (k + skip_b, j))
    else:
        a_spec = pl.BlockSpec((tm, tk), lambda i, j, k: (i + skip_a, k))
        b_spec = (pl.BlockSpec((tk, tn), lambda i, j, k: (k, j)) if mode == "nn"
                  else pl.BlockSpec((tn, tk), lambda i, j, k: (j, k)))
    o_spec = pl.BlockSpec((tm, tn), lambda i, j, k: (i, j))
    in_specs = [a_spec, b_spec] + ([o_spec] if add is not None else [])
    args = (a, b) + ((add,) if add is not None else ())
    return _call(
        body, name=name, grid=(m // tm, n // tn, nk), in_specs=in_specs, out_specs=o_spec,
        out_shape=jax.ShapeDtypeStruct((m, n), out_dtype), scratch_shapes=[pltpu.VMEM((tm, tn), F32)],
        compiler_params=_params("parallel", "parallel", "arbitrary"),
    )(*args)


PROJ_TS = 256


def _kvq_proj(x, g_kv, g_mix, wk, wv, wq, gk, gq, ex=None):
    lead = PADK // PROJ_TS

    def body(x_ref, gkv_ref, gmix_ref, wk_ref, wv_ref, wq_ref, gk_ref, gq_ref,
             hkv_ref, h1_ref, kpre_ref, qpre_ref, k_ref, v_ref, q_ref):
        i = pl.program_id(0)

        @pl.when(i < lead)
        def _():
            k_ref[...] = jnp.zeros(k_ref.shape, BF16)
            v_ref[...] = jnp.zeros(v_ref.shape, BF16)

        @pl.when(i >= lead)
        def _():
            xv = x_ref[...]
            xr = xv * _rstd(xv)
            hkv = (xr * gkv_ref[...]).astype(BF16)
            h1 = (xr * gmix_ref[...]).astype(BF16)
            hkv_ref[...] = hkv
            h1_ref[...] = h1
            kpre = _nn(hkv, wk_ref[...])
            qpre = _nn(h1, wq_ref[...])
            kpre_ref[...] = kpre
            qpre_ref[...] = qpre
            v_ref[...] = _nn(hkv, wv_ref[...]).astype(BF16)
            rk = lax.rsqrt(_seg_sum(kpre * kpre) * (1.0 / HEAD) + EPS)
            k_ref[...] = (kpre * rk * gk_ref[...]).astype(BF16)
            rq = lax.rsqrt(_seg_sum(qpre * qpre) * (1.0 / HEAD) + EPS)
            q_ref[...] = (qpre * rq * gq_ref[...]).astype(BF16)

    tile = pl.BlockSpec((PROJ_TS, D), lambda i: (jnp.maximum(i - lead, 0), 0))
    padded = pl.BlockSpec((PROJ_TS, D), lambda i: (i, 0))
    row = pl.BlockSpec((1, D), lambda i: (0, 0))
    wspec = pl.BlockSpec((D, D), lambda i: (0, 0))
    bf = jax.ShapeDtypeStruct((S, D), BF16)
    ff = jax.ShapeDtypeStruct((S, D), F32)
    bp = jax.ShapeDtypeStruct((PADK + S, D), BF16)
    return _call_hosting(
        body, ex, _grid_phases((lead + S // PROJ_TS,), 0.85), name="kvq_proj", grid=(lead + S // PROJ_TS,),
        in_specs=[tile, row, row, wspec, wspec, wspec, row, row],
        out_specs=[tile, tile, tile, tile, padded, padded, tile], out_shape=[bf, bf, ff, ff, bp, bp, bf],
        scratch_shapes=[], args=(x, g_kv, g_mix, wk, wv, wq, gk, gq), compiler_params=_params("arbitrary"),
    )


def _head_norm_bwd(dout, pre, hg, name, row_off=0):
    nt = S // PROJ_TS

    def body(do_ref, pre_ref, hg_ref, dpre_ref, dhg_ref, acc):
        i = pl.program_id(0)
        dov, pv, hgv = do_ref[...], pre_ref[...], hg_ref[...]
        r = lax.rsqrt(_seg_sum(pv * pv) * (1.0 / HEAD) + EPS)
        gd = dov * hgv
        dpre = r * gd - pv * (r * r * r) * (_seg_sum(gd * pv) * (1.0 / HEAD))
        dpre_ref[...] = dpre.astype(BF16)

        @pl.when(i == 0)
        def _():
            acc[...] = jnp.zeros(acc.shape, F32)

        acc[...] += _colsum8(dov * pv * r)

        @pl.when(i == nt - 1)
        def _():
            full = acc[...]
            fold = full[:, 0:128]
            for blk in range(1, D // 128):
                fold = fold + full[:, blk * 128:(blk + 1) * 128]
            dhg_ref[...] = fold + pltpu.roll(fold, HEAD, axis=1)

    tile = pl.BlockSpec((PROJ_TS, D), lambda i: (i, 0))
    return _call(
        body, name=name, grid=(nt,),
        in_specs=[pl.BlockSpec((PROJ_TS, D), lambda i: (i + row_off, 0)), tile, pl.BlockSpec((1, D), lambda i: (0, 0))],
        out_specs=[tile, pl.BlockSpec((8, 128), lambda i: (0, 0))],
        out_shape=[jax.ShapeDtypeStruct((S, D), BF16), jax.ShapeDtypeStruct((8, 128), F32)],
        scratch_shapes=[pltpu.VMEM((8, D), F32)],
        compiler_params=_params("arbitrary"),
    )(dout, pre, hg)


def _toeplitz_from_table(table):
    far = jnp.broadcast_to(table[:, N_REL - 1:], (N_HEADS, PADK - MAX_REL + 1))
    near = table[:, N_REL - 2::-1]
    past = jnp.broadcast_to(table[:, 0:1], (N_HEADS, MAX_REL))
    wrap = jnp.broadcast_to(table[:, N_REL - 1:], (N_HEADS, QB - 1))
    return jnp.concatenate([far, near, past, wrap], axis=1).reshape(N_HEADS, 1, TOEP)


def _table_grad_from_toeplitz(dtp, seg):
    lo = PADK - MAX_REL + 1
    near = dtp[:, lo + N_REL - 3:lo - 1:-1]
    return jnp.concatenate([seg[:, 1:2], near, seg[:, 0:1]], axis=1)


def _bias_band(tp):
    def body(tp_ref, out_ref):
        bv = pltpu.roll(jnp.broadcast_to(tp_ref[0], (QB, TOEP)), 0, axis=1, stride=1, stride_axis=0)
        out_ref[0] = jnp.where(_band_mask(), bv[:, 0:KB], NEG_INF)

    return _call(
        body, name="bias_band", grid=(N_HEADS,),
        in_specs=[pl.BlockSpec((1, 1, TOEP), lambda h: (h, 0, 0))],
        out_specs=pl.BlockSpec((1, QB, KB), lambda h: (h, 0, 0)),
        out_shape=jax.ShapeDtypeStruct((N_HEADS, QB, KB), F32),
        compiler_params=_params("parallel"),
    )(tp)


def _bias_grad(dband, after):
    lo, hi = PADK - MAX_REL + 1, PADK + MAX_REL

    def body(db_ref, _, dtp_ref, seg_ref):
        bv = jnp.concatenate([db_ref[0], jnp.zeros((QB, TOEP - KB), F32)], axis=1)
        row = lax.broadcasted_iota(jnp.int32, (QB, TOEP), 0)
        k = 1
        while k < QB:
            bv = jnp.where((row & k) != 0, pltpu.roll(bv, TOEP - k, axis=1), bv)
            k *= 2
        col = jnp.sum(bv, axis=0, keepdims=True)
        dtp_ref[0] = col
        u = lax.broadcasted_iota(jnp.int32, (1, TOEP), 1)
        far = jnp.sum(jnp.where((u < lo) | (u > hi + MAX_REL), col, 0.0))
        past = jnp.sum(jnp.where((u >= hi) & (u <= hi + MAX_REL), col, 0.0))
        lane = lax.broadcasted_iota(jnp.int32, (1, 128), 1)
        seg_ref[0] = jnp.where(lane == 0, far, jnp.where(lane == 1, past, 0.0))

    return _call(
        body, name="bias_grad", grid=(N_HEADS,),
        in_specs=[pl.BlockSpec((1, QB, KB), lambda h: (h, 0, 0)), ANY],
        out_specs=[pl.BlockSpec((1, 1, TOEP), lambda h: (h, 0, 0)), pl.BlockSpec((1, 1, 128), lambda h: (h, 0, 0))],
        out_shape=[jax.ShapeDtypeStruct((N_HEADS, 1, TOEP), F32), jax.ShapeDtypeStruct((N_HEADS, 1, 128), F32)],
        compiler_params=_params("parallel"),
    )(dband, after)


N_QB = S // QB
HEADS_PER_STEP = 4
ATT_LANES = HEADS_PER_STEP * HEAD
N_HG = D // ATT_LANES


def _band_mask():
    qc = lax.broadcasted_iota(jnp.int32, (QB, KB), 0) // CHUNK
    kc = lax.broadcasted_iota(jnp.int32, (QB, KB), 1) // CHUNK
    return (kc >= qc) & (kc <= qc + LEFT)


def _half_scale(hh, scale):
    lane = lax.broadcasted_iota(jnp.int32, (1, 128), 1)
    return jnp.where((lane < HEAD) == (hh == 0), scale, 0.0).astype(BF16)


def _probs(qh, kb, bias, first_key):
    sc = _nt(qh, kb) + bias
    if first_key is not None:
        sc = jnp.where(lax.broadcasted_iota(jnp.int32, (QB, KB), 1) >= first_key, sc, NEG_INF)
    e = jnp.exp(sc - jnp.max(sc, axis=-1, keepdims=True))
    return e * (1.0 / jnp.sum(e, axis=-1, keepdims=True))


def _by_padding(cb, compute):
    @pl.when(cb < PADK // QB)
    def _():
        compute(PADK - cb * QB)

    @pl.when(cb >= PADK // QB)
    def _():
        compute(None)


def _attn_fwd(q, kp, vp, bias, ex=None):
    def body(q_ref, k_ref, v_ref, b_ref, o_ref):
        cb = pl.program_id(1)
        band = pl.ds(pl.multiple_of(cb * QB, QB), KB)
        low = lax.broadcasted_iota(jnp.int32, (QB, 128), 1) < HEAD

        def compute(first_key):
            for pair in range(HEADS_PER_STEP // 2):
                lanes = pl.ds(pair * 128, 128)
                kb, vb, qv = k_ref[band, lanes], v_ref[band, lanes], q_ref[:, lanes]
                outs = []
                for hh in range(2):
                    pb = _probs(qv * _half_scale(hh, ATTN_SCALE), kb, b_ref[2 * pair + hh], first_key).astype(BF16)
                    outs.append(_nn(pb, vb))
                o_ref[:, lanes] = jnp.where(low, outs[0], outs[1]).astype(BF16)

        _by_padding(cb, compute)

    qspec = pl.BlockSpec((QB, ATT_LANES), lambda hg, cb: (cb, hg))
    kspec = pl.BlockSpec((PADK + S, ATT_LANES), lambda hg, cb: (0, hg))
    return _call_hosting(
        body, ex, _grid_phases((N_HG, N_QB), 0.85), name="attn_fwd", grid=(N_HG, N_QB),
        in_specs=[qspec, kspec, kspec, pl.BlockSpec((HEADS_PER_STEP, QB, KB), lambda hg, cb: (hg, 0, 0))],
        out_specs=[qspec], out_shape=[jax.ShapeDtypeStruct((S, D), BF16)], scratch_shapes=[],
        args=(q, kp, vp, bias), compiler_params=_params("arbitrary", "arbitrary"),
    )


def _attn_bwd(q, kp, vp, bias, do, ex=None):
    def body(q_ref, k_ref, v_ref, b_ref, do_ref, dq_ref, dk_ref, dv_ref, db_ref):
        cb = pl.program_id(1)

        @pl.when(cb == 0)
        def _():
            dk_ref[...] = jnp.zeros(dk_ref.shape, F32)
            dv_ref[...] = jnp.zeros(dv_ref.shape, F32)
            db_ref[...] = jnp.zeros(db_ref.shape, F32)

        band = pl.ds(pl.multiple_of(cb * QB, QB), KB)
        low = lax.broadcasted_iota(jnp.int32, (QB, 128), 1) < HEAD

        def compute(first_key):
            for pair in range(HEADS_PER_STEP // 2):
                lanes = pl.ds(pair * 128, 128)
                kb, vb = k_ref[band, lanes], v_ref[band, lanes]
                qv, dov = q_ref[:, lanes], do_ref[:, lanes]
                dq = jnp.zeros((QB, 128), F32)
                dkb = jnp.zeros((KB, 128), F32)
                dvb = jnp.zeros((KB, 128), F32)
                for hh in range(2):
                    sel = low if hh == 0 else jnp.logical_not(low)
                    doh = dov * _half_scale(hh, 1.0)
                    p = _probs(qv * _half_scale(hh, ATTN_SCALE), kb, b_ref[2 * pair + hh], first_key)
                    dp = _nt(doh, vb)
                    dvb = dvb + _tn(p.astype(BF16), doh)
                    ds = p * (dp - jnp.sum(dp * p, axis=-1, keepdims=True))
                    db_ref[2 * pair + hh] += ds
                    dsb = (ds * ATTN_SCALE).astype(BF16)
                    dq = dq + jnp.where(sel, _nn(dsb, kb), 0.0)
                    dkb = dkb + _tn(dsb, qv * _half_scale(hh, 1.0))
                dq_ref[:, lanes] = dq
                dk_ref[band, lanes] += dkb
                dv_ref[band, lanes] += dvb

        _by_padding(cb, compute)

    qspec = pl.BlockSpec((QB, ATT_LANES), lambda hg, cb: (cb, hg))
    kspec = pl.BlockSpec((PADK + S, ATT_LANES), lambda hg, cb: (0, hg))
    bspec = pl.BlockSpec((HEADS_PER_STEP, QB, KB), lambda hg, cb: (hg, 0, 0))
    kf = jax.ShapeDtypeStruct((PADK + S, D), F32)
    return _call_hosting(
        body, ex, _grid_phases((N_HG, N_QB)), name="attn_bwd", grid=(N_HG, N_QB),
        in_specs=[qspec, kspec, kspec, bspec, qspec],
        out_specs=[qspec, kspec, kspec, bspec],
        out_shape=[jax.ShapeDtypeStruct((S, D), F32), kf, kf, jax.ShapeDtypeStruct((N_HEADS, QB, KB), F32)],
        scratch_shapes=[], args=(q, kp, vp, bias, do), compiler_params=_params("arbitrary", "arbitrary"),
    )


def _adam_math(w, g, m, v):
    m = ADAM_B1 * m + (1.0 - ADAM_B1) * g
    v = ADAM_B2 * v + (1.0 - ADAM_B2) * (g * g)
    m_hat = m / (1.0 - ADAM_B1 ** ADAM_STEP)
    v_hat = v / (1.0 - ADAM_B2 ** ADAM_STEP)
    delta = -ADAM_LR * (m_hat / (jnp.sqrt(v_hat) + ADAM_EPS) + ADAM_WD * w)
    return delta, m, v


def _sum_parts(parts, name, after=None):
    count, r, c = parts.shape
    tr = r // 2 if r % 16 == 0 and r > 64 else r

    def body(p_ref, *rest):
        o_ref = rest[-1]
        acc = p_ref[0].astype(F32)
        for d in range(1, count):
            acc = acc + p_ref[d].astype(F32)
        o_ref[...] = acc

    return _call(
        body, name=name, grid=(r // tr,),
        in_specs=[pl.BlockSpec((count, tr, c), lambda i: (0, i, 0))] + ([ANY] if after is not None else []),
        out_specs=pl.BlockSpec((tr, c), lambda i: (i, 0)),
        out_shape=jax.ShapeDtypeStruct((r, c), F32), compiler_params=_params("parallel"),
    )(parts, *([after] if after is not None else []))


def _row_tile(r):
    for cand in (256, 176, 128, 64, 32, 16, 8):
        if r % cand == 0:
            return cand
    return r


def _adam_layer(w, g, m, v, layer, name, prev=None):
    nl, r, c = w.shape
    tr = _row_tile(r)

    def body(w_ref, g_ref, m_ref, v_ref, *rest):
        go_ref, d_ref, nm_ref, nv_ref = rest[-4:]
        gv = g_ref[...]
        delta, nm, nv = _adam_math(w_ref[0], gv, m_ref[0], v_ref[0])
        go_ref[0] = gv
        d_ref[0] = delta
        nm_ref[0] = nm
        nv_ref[0] = nv

    lspec = pl.BlockSpec((1, tr, c), lambda i: (layer, i, 0))
    sd = jax.ShapeDtypeStruct((nl, r, c), F32)
    extra = list(prev) if prev is not None else []
    return _call(
        body, name=name, grid=(r // tr,),
        in_specs=[lspec, pl.BlockSpec((tr, c), lambda i: (i, 0)), lspec, lspec] + [ANY] * len(extra),
        out_specs=[lspec] * 4, out_shape=[sd] * 4,
        input_output_aliases={4 + t: t for t in range(len(extra))},
        compiler_params=_params("parallel"),
    )(w, g, m, v, *extra)


def _adam(w, g, m, v, name):
    r, c = w.shape
    tr = _row_tile(r)

    def body(w_ref, g_ref, m_ref, v_ref, d_ref, nm_ref, nv_ref):
        delta, nm, nv = _adam_math(w_ref[...], g_ref[...], m_ref[...], v_ref[...])
        d_ref[...] = delta
        nm_ref[...] = nm
        nv_ref[...] = nv

    spec = pl.BlockSpec((tr, c), lambda i: (i, 0))
    sd = jax.ShapeDtypeStruct((r, c), F32)
    return _call(
        body, name=name, grid=(r // tr,), in_specs=[spec] * 4, out_specs=[spec] * 3, out_shape=[sd] * 3,
        compiler_params=_params("parallel"),
    )(w, g, m, v)


def _pad8(rows):
    return jnp.pad(rows, ((0, 8 - rows.shape[0]), (0, 0)))


def kernel(x, norm_mix_g, norm_ffn_g, pool_w, pool_b, pool_scale, kv_norm_g, w_k, w_v, k_norm_g, w_q, q_norm_g, rel_bias, w_o, w_gate, w_up, w_down, loss_target, m_norm_mix_g, m_norm_ffn_g, m_pool_w, m_pool_b, m_pool_scale, m_kv_norm_g, m_w_k, m_w_v, m_k_norm_g, m_w_q, m_q_norm_g, m_rel_bias, m_w_o, m_w_gate, m_w_up, m_w_down, v_norm_mix_g, v_norm_ffn_g, v_pool_w, v_pool_b, v_pool_scale, v_kv_norm_g, v_w_k, v_w_v, v_k_norm_g, v_w_q, v_q_norm_g, v_rel_bias, v_w_o, v_w_gate, v_w_up, v_w_down):
    assert x.shape == (1, S, D) and w_gate.shape == (2, D, F_SHARD) and w_k.shape == (D_SHARD, D)
    xin, target = x[0], loss_target[0]

    ffn_shards = [[w_gate[layer].T.astype(BF16), w_up[layer].T.astype(BF16), w_down[layer].astype(BF16)]
                  for layer in range(2)]
    att_shards = [w_k.astype(BF16), w_v.astype(BF16), w_q[0].astype(BF16), w_o[0].astype(BF16)]
    pool_shard = pool_w[0].astype(BF16).reshape(N_GROUPS * POOL_SHARD, GROUP)
    small = jnp.concatenate([pool_b[0].reshape(1, N_GROUPS * POOL_SHARD), pool_scale], axis=1)

    full0 = _run_exchange(_gather_exchange(ffn_shards[0] + [pool_shard, _pad8(small)]), "gather_layer0")
    ffn_w0 = [a.reshape(F, D) for a in full0[:3]]
    pw_f = full0[3].reshape(N_DEV, N_GROUPS, POOL_SHARD, GROUP).transpose(1, 0, 2, 3).reshape(N_GROUPS, GROUP, GROUP)
    small_f = full0[4][:, 0, :]
    pb_f = small_f[:, :N_GROUPS * POOL_SHARD].reshape(N_DEV, N_GROUPS, POOL_SHARD).transpose(1, 0, 2).reshape(1, D)
    ps_f = small_f[:, N_GROUPS * POOL_SHARD:].reshape(1, D)

    g_mix0, g_mix1 = norm_mix_g[0:1], norm_mix_g[1:2]
    g_ffn0, g_ffn1 = norm_ffn_g[0:1], norm_ffn_g[1:2]
    g_kv = kv_norm_g.reshape(1, D)
    gk_t = jnp.tile(k_norm_g.reshape(1, HEAD), (1, N_HEADS))
    gq_t = jnp.tile(q_norm_g.reshape(1, HEAD), (1, N_HEADS))

    x1, diff = _pool_fwd(xin, g_mix0, pw_f, pb_f, ps_f)
    (x2, hf0, gg0, uu0), full_att = _ffn_fwd(x1, g_ffn0, *ffn_w0, ex=_gather_exchange(att_shards))
    wk_f, wv_f, wq_f, wo_f = [a.reshape(D, D) for a in full_att]
    (hkv, h1, kpre, qpre, kp, vp, qq), full1_gate = _kvq_proj(x2, g_kv, g_mix1, wk_f, wv_f, wq_f, gk_t, gq_t,
                                                                ex=_gather_exchange(ffn_shards[1][:1]))
    bias = _bias_band(_toeplitz_from_table(rel_bias[0]))
    (att,), full1_rest = _attn_fwd(qq, kp, vp, bias, ex=_gather_exchange(ffn_shards[1][1:]))
    ffn_w1 = [a.reshape(F, D) for a in full1_gate + full1_rest]
    x3 = _mm(att, wo_f, "nn", F32, "attn_out", add=x2)
    (dx4, loss_rows, hf1, gg1, uu1), _ = _ffn_fwd(x3, g_ffn1, *ffn_w1, target=target)

    def blocks(dw):
        return dw.reshape(N_DEV, dw.shape[0] // N_DEV, dw.shape[1])

    (dgg1, duu1, dwg1, dwu1, dwd1), _ = _ffn_bwd_weights(dx4, hf1, gg1, uu1, ffn_w1[2], name="ffn_bwd1")
    parts1 = [blocks(dw) for dw in (dwg1, dwu1, dwd1)]
    (dx3, dg_ffn1), stage1 = _ffn_bwd_input(dx4, x3, g_ffn1, dgg1, duu1, ffn_w1[0], ffn_w1[1], "ffn_dx1",
                                            ex=_pair_exchange(parts1))
    chip1 = _pair_add(parts1, stage1, "pair_add_ffn1")
    datt = _mm(dx3, wo_f, "nt", BF16, "d_attn")
    dwo = _mm(att, dx3, "tn", BF16, "d_wo")
    (dq, dkp, dvp, dband), recv1 = _attn_bwd(qq, kp, vp, bias, datt, ex=_chip_exchange(chip1))
    dqpre, dgq = _head_norm_bwd(dq, qpre, gq_t, "q_norm_bwd")
    dkpre, dgk = _head_norm_bwd(dkp, kpre, gk_t, "k_norm_bwd", row_off=PADK // PROJ_TS)
    dh1 = _mm(dqpre, wq_f, "nt", F32, "d_h1")
    dwq = _mm(h1, dqpre, "tn", BF16, "d_wq")
    dhkv = _mm(dkpre, wk_f, "nt", F32, "d_hkv_k")
    dhkv = _mm(dvp, wv_f, "nt", F32, "d_hkv_v", add=dhkv, skip_rows=PADK)
    dwk = _mm(hkv, dkpre, "tn", BF16, "d_wk")
    dwv = _mm(hkv, dvp, "tn", BF16, "d_wv", skip_rows=PADK)
    parts_att = [blocks(dw) for dw in (dwk, dwv, dwq, dwo)]
    (dx2, dg_mix1, dg_kv), stage_att = _norm_bwd(dx3, x2, [dh1, dhkv], [g_mix1, g_kv], "norm_bwd_mix1",
                                                 ex=_pair_exchange(parts_att))
    chip_att = _pair_add(parts_att, stage_att, "pair_add_att")
    (dgg0, duu0, dwg0, dwu0, dwd0), recv_att = _ffn_bwd_weights(
        dx2, hf0, gg0, uu0, ffn_w0[2], name="ffn_bwd0", ex=_chip_exchange(chip_att))
    parts0 = [blocks(dw) for dw in (dwg0, dwu0, dwd0)]
    (dx1, dg_ffn0), stage0 = _ffn_bwd_input(dx2, x1, g_ffn0, dgg0, duu0, ffn_w0[0], ffn_w0[1], "ffn_dx0",
                                            ex=_pair_exchange(parts0))
    both = _pair_add(parts0, stage0, "pair_add_ffn0", copies=2)
    flight, chip0, land0, token = _chip_exchange_start(both[:3], both[3:], dx1, "scatter_ffn0_start")
    (grad_x, dpw, db_rows, ds_rows, dg_mix0), _ = _pool_bwd(dx1, xin, diff, g_mix0 + token[0:1, 0:1], pw_f, pb_f, ps_f)
    dtp, seg = _bias_grad(dband, token)

    weights = dict(norm_mix_g=norm_mix_g, norm_ffn_g=norm_ffn_g, pool_w=pool_w, pool_b=pool_b,
                   pool_scale=pool_scale, kv_norm_g=kv_norm_g, w_k=w_k, w_v=w_v, k_norm_g=k_norm_g, w_q=w_q,
                   q_norm_g=q_norm_g, rel_bias=rel_bias, w_o=w_o, w_gate=w_gate, w_up=w_up, w_down=w_down)
    mom1 = dict(norm_mix_g=m_norm_mix_g, norm_ffn_g=m_norm_ffn_g, pool_w=m_pool_w, pool_b=m_pool_b,
                pool_scale=m_pool_scale, kv_norm_g=m_kv_norm_g, w_k=m_w_k, w_v=m_w_v, k_norm_g=m_k_norm_g,
                w_q=m_w_q, q_norm_g=m_q_norm_g, rel_bias=m_rel_bias, w_o=m_w_o, w_gate=m_w_gate, w_up=m_w_up,
                w_down=m_w_down)
    mom2 = dict(norm_mix_g=v_norm_mix_g, norm_ffn_g=v_norm_ffn_g, pool_w=v_pool_w, pool_b=v_pool_b,
                pool_scale=v_pool_scale, kv_norm_g=v_kv_norm_g, w_k=v_w_k, w_v=v_w_v, k_norm_g=v_k_norm_g,
                w_q=v_w_q, q_norm_g=v_q_norm_g, rel_bias=v_rel_bias, w_o=v_w_o, w_gate=v_w_gate, w_up=v_w_up,
                w_down=v_w_down)
    names = list(weights)
    grads, deltas, new_m, new_v = {}, {}, {}, {}

    def summed(arrs, tag, after):
        return [_sum_parts(a, "sum_parts_%s%d" % (tag, k), after) for k, a in enumerate(arrs)]

    def adam_flat(nm):
        shape = weights[nm].shape
        flat = lambda a: a.reshape(-1, shape[-1])
        dl, m1, m2 = _adam(flat(weights[nm]), flat(grads[nm]), flat(mom1[nm]), flat(mom2[nm]), "adam_" + nm)
        deltas[nm], new_m[nm], new_v[nm] = dl.reshape(shape), m1.reshape(shape), m2.reshape(shape)

    ffn_names = ("w_gate", "w_up", "w_down")
    def ffn_view(nm, a):
        return a if nm == "w_down" else a.transpose(0, 2, 1)

    sums1 = summed(recv1, "ffn1_", token)
    layer1 = {nm: _adam_layer(ffn_view(nm, weights[nm]), g, ffn_view(nm, mom1[nm]), ffn_view(nm, mom2[nm]), 1,
                              "adam1_" + nm)
              for nm, g in zip(ffn_names, sums1)}
    g_wk, g_wv, g_wq, g_wo = summed(recv_att, "att_", token)
    grads.update(w_k=g_wk, w_v=g_wv, w_q=g_wq[None], w_o=g_wo[None])
    for nm in ("w_k", "w_v", "w_q", "w_o"):
        adam_flat(nm)

    dpw_blocks = dpw.reshape(N_GROUPS, N_DEV, POOL_SHARD, GROUP).transpose(1, 0, 2, 3)
    dpw_blocks = dpw_blocks.reshape(N_DEV, N_GROUPS * POOL_SHARD, GROUP).astype(BF16)
    recv_pool = _run_exchange(_scatter_exchange([dpw_blocks]), "scatter_pool", after=deltas["w_o"])
    misc = jnp.concatenate([dgk[0:1, 0:HEAD], dgq[0:1, 0:HEAD], loss_rows[0:1, 0:1],
                            seg[:, 0, 0].reshape(1, N_HEADS), seg[:, 0, 1].reshape(1, N_HEADS)], axis=1)
    misc = jnp.pad(misc, ((0, 0), (0, D - misc.shape[1])))
    vec_rows = jnp.concatenate([dg_mix0[0:1], dg_mix1[0:1], dg_ffn0[0:1], dg_ffn1[0:1], dg_kv[0:1],
                                db_rows[0:1], ds_rows[0:1], misc], axis=0)
    pack = jnp.concatenate([vec_rows, dtp.reshape(N_HEADS, TOEP)], axis=0)
    tot = _all_reduce_small(pack, "reduce_small", after=recv_pool[0])

    loss = tot[7, 2 * HEAD]
    seg_tot = jnp.stack([tot[7, 2 * HEAD + 1:2 * HEAD + 1 + N_HEADS],
                         tot[7, 2 * HEAD + 1 + N_HEADS:2 * HEAD + 1 + 2 * N_HEADS]], axis=1)
    me = 4 * lax.axis_index("x") + 2 * lax.axis_index("y") + lax.axis_index("c")
    g_pool_b = lax.dynamic_slice_in_dim(tot[5].reshape(N_GROUPS, GROUP), me * POOL_SHARD, POOL_SHARD, axis=1)
    grads.update(
        norm_mix_g=tot[0:2], norm_ffn_g=tot[2:4], kv_norm_g=tot[4], k_norm_g=tot[7, 0:HEAD],
        q_norm_g=tot[7, HEAD:2 * HEAD].reshape(1, HEAD),
        rel_bias=_table_grad_from_toeplitz(tot[8:8 + N_HEADS], seg_tot).reshape(1, N_HEADS, N_REL),
        pool_b=g_pool_b.reshape(1, N_GROUPS, POOL_SHARD),
        pool_scale=lax.dynamic_slice_in_dim(tot[6:7], me * D_SHARD, D_SHARD, axis=1),
        pool_w=summed(recv_pool, "pool_", None)[0].reshape(1, N_GROUPS, POOL_SHARD, GROUP))
    adam_flat("pool_w")
    small_names = [nm for nm in names if nm not in ffn_names + ("w_k", "w_v", "w_q", "w_o", "pool_w")]

    def pack_small(tree):
        cols = []
        for nm in small_names:
            flat = tree[nm].reshape(-1)
            cols.append(jnp.pad(flat, (0, -flat.shape[0] % 1024)))
        return jnp.concatenate(cols).reshape(-1, 128)

    dl, m1, m2 = _adam(pack_small(weights), pack_small(grads), pack_small(mom1), pack_small(mom2), "adam_small")

    def unpack_small(packed, out):
        flat, off = packed.reshape(-1), 0
        for nm in small_names:
            size = weights[nm].size
            out[nm] = flat[off:off + size].reshape(weights[nm].shape)
            off += size + (-size % 1024)

    unpack_small(dl, deltas)
    unpack_small(m1, new_m)
    unpack_small(m2, new_v)

    recv0 = _chip_exchange_wait(flight, chip0, land0, dl, "scatter_ffn0_wait")
    sums0 = summed(recv0, "ffn0_", None)
    for nm, g in zip(ffn_names, sums0):
        done = _adam_layer(ffn_view(nm, weights[nm]), g, ffn_view(nm, mom1[nm]), ffn_view(nm, mom2[nm]), 0,
                           "adam0_" + nm, prev=layer1[nm])
        grads[nm], deltas[nm], new_m[nm], new_v[nm] = [ffn_view(nm, a) for a in done]

    return (loss, grad_x[None], *[grads[nm] for nm in names], *[deltas[nm] for nm in names],
            *[new_m[nm] for nm in names], *[new_v[nm] for nm in names])
```

```python
import functools

import jax
import jax.numpy as jnp
from jax import lax
from jax.experimental import pallas as pl
from jax.experimental.pallas import tpu as pltpu

F32 = jnp.float32
BF16 = jnp.bfloat16
MESH_ID = pl.DeviceIdType.MESH

N_DEV = 8
S = 2048
D = 1024
F = 2816
F_SHARD = F // N_DEV
D_SHARD = D // N_DEV
N_GROUPS = 4
GROUP = D // N_GROUPS
POOL_SHARD = GROUP // N_DEV
MAX_WIN = 16
HEAD = 64
N_HEADS = D // HEAD
CHUNK = 64
LEFT = 8
QB = 4 * CHUNK
KB = QB + LEFT * CHUNK
PADK = LEFT * CHUNK
TOEP = 1024
N_REL = 257
MAX_REL = 128
EPS = 1e-6
NEG_INF = -1e30
ATTN_SCALE = HEAD ** -0.5

ADAM_LR = 0.001
ADAM_B1 = 0.9
ADAM_B2 = 0.999
ADAM_EPS = 1e-08
ADAM_WD = 0.01
ADAM_STEP = 10

VMEM_LIMIT = 52 * 1024 * 1024

ANY = pl.BlockSpec(memory_space=pl.ANY)
VMEM = pl.BlockSpec(memory_space=pltpu.VMEM)


def _call(body, **kw):
    return pl.pallas_call(body, **kw)


def _params(*sem):
    return pltpu.CompilerParams(dimension_semantics=sem, vmem_limit_bytes=VMEM_LIMIT)


def _dot(a, b, dims):
    return lax.dot_general(a, b, (dims, ((), ())), preferred_element_type=F32)


def _nn(a, b):
    return _dot(a, b, ((1,), (0,)))


def _nt(a, b):
    return _dot(a, b, ((1,), (1,)))


def _tn(a, b):
    return _dot(a, b, ((0,), (0,)))


def _rstd(x):
    return lax.rsqrt(jnp.mean(x * x, axis=-1, keepdims=True) + EPS)


def _rms_bwd(dh, x, r, g):
    gd = dh * g
    return r * gd - x * (r * r * r) * jnp.mean(gd * x, axis=-1, keepdims=True)


def _colsum8(v):
    return jnp.broadcast_to(jnp.sum(v, axis=0, keepdims=True), (8, v.shape[1]))


def _seg_sum(v):
    r = lax.broadcasted_iota(jnp.int32, (128, 128), 0) // HEAD
    c = lax.broadcasted_iota(jnp.int32, (128, 128), 1) // HEAD
    ones = jnp.where(r == c, 1.0, 0.0).astype(BF16)
    out = []
    for blk in range(v.shape[1] // 128):
        part = v[:, blk * 128:(blk + 1) * 128]
        hi = part.astype(BF16)
        rest = part - hi.astype(F32)
        mid = rest.astype(BF16)
        lo = (rest - mid.astype(F32)).astype(BF16)
        out.append(_nn(hi, ones) + _nn(mid, ones) + _nn(lo, ones))
    return jnp.concatenate(out, axis=1)


def _place():
    return lax.axis_index("x"), lax.axis_index("y"), lax.axis_index("c")


class _Exchange:
    def __init__(self, ins, out_shape, sems, start, mid, finish):
        self.ins, self.out_shape, self.sems = list(ins), list(out_shape), list(sems)
        self.start, self.mid, self.finish = start, mid, finish


def _gather_exchange(shards):
    n = len(shards)

    def tools(ins, outs, sems):
        send_sems, recv_sems, local_sems = sems
        x, y, c = _place()
        me, sibling = (x, y, c), (x, y, 1 - c)
        chips = [(1 - x, y), (x, 1 - y), (1 - x, 1 - y)]

        def slot(k, px, py, pc):
            return outs[k].at[4 * px + 2 * py + pc]

        def copy(k, s, block, to, own=False):
            return pltpu.make_async_remote_copy(
                src_ref=ins[k] if own else slot(k, *block), dst_ref=slot(k, *block),
                send_sem=send_sems.at[k, s], recv_sem=recv_sems.at[k, s], device_id=to, device_id_type=MESH_ID)

        def mine():
            return [pltpu.make_async_copy(ins[k], slot(k, *me), local_sems.at[k]) for k in range(n)]

        def first():
            out = []
            for k in range(n):
                out.append(copy(k, 0, me, sibling, own=True))
                out += [copy(k, 1 + j, me, (*chip, c), own=True) for j, chip in enumerate(chips)]
            return out

        def landed(j):
            return [copy(k, 1 + j, (*chips[j], c), me) for k in range(n)]

        def passed(j):
            return [copy(k, 4 + j, (*chips[j], c), sibling) for k in range(n)]

        def from_sibling():
            out = []
            for k in range(n):
                out.append(copy(k, 0, sibling, me))
                out += [copy(k, 4 + j, (*chip, 1 - c), me) for j, chip in enumerate(chips)]
            return out

        return mine, first, landed, passed, from_sibling

    def start(ins, outs, sems):
        mine, first, _, _, _ = tools(ins, outs, sems)
        for cp in mine() + first():
            cp.start()

    def mid(ins, outs, sems):
        _, _, landed, passed, _ = tools(ins, outs, sems)
        for j in range(3):
            for arrived, onward in zip(landed(j), passed(j)):
                arrived.wait_recv()
                onward.start()

    def finish(ins, outs, sems):
        mine, first, _, passed, from_sibling = tools(ins, outs, sems)
        for cp in from_sibling():
            cp.wait_recv()
        for cp in first() + passed(0) + passed(1) + passed(2):
            cp.wait_send()
        for cp in mine():
            cp.wait()

    return _Exchange(
        shards, [jax.ShapeDtypeStruct((N_DEV,) + a.shape, a.dtype) for a in shards],
        [pltpu.SemaphoreType.DMA((n, 7)), pltpu.SemaphoreType.DMA((n, 7)), pltpu.SemaphoreType.DMA((n,))],
        start, mid, finish)


def _peer(x, y, c, m):
    px = 1 - x if m & 4 else x
    py = 1 - y if m & 2 else y
    pc = 1 - c if m & 1 else c
    return px, py, pc


def _scatter_exchange(parts):
    n = len(parts)

    def tools(ins, outs, sems):
        send_sems, recv_sems, local_sems = sems
        x, y, c = _place()
        me = 4 * x + 2 * y + c
        def mine():
            return [pltpu.make_async_copy(ins[k].at[me], outs[k].at[me], local_sems.at[k]) for k in range(n)]

        def remote(dst_is_mine):
            out = []
            for m in range(1, N_DEV):
                px, py, pc = _peer(x, y, c, m)
                peer = 4 * px + 2 * py + pc
                for k in range(n):
                    out.append(pltpu.make_async_remote_copy(
                        src_ref=ins[k].at[peer], dst_ref=outs[k].at[me if dst_is_mine else peer],
                        send_sem=send_sems.at[k, m - 1], recv_sem=recv_sems.at[k, m - 1],
                        device_id=(px, py, pc), device_id_type=MESH_ID))
            return out

        return mine, remote

    def start(ins, outs, sems):
        mine, remote = tools(ins, outs, sems)
        for cp in mine() + remote(True):
            cp.start()

    def mid(ins, outs, sems):
        pass

    def finish(ins, outs, sems):
        mine, remote = tools(ins, outs, sems)
        for cp in remote(False):
            cp.wait_recv()
        for cp in remote(True):
            cp.wait_send()
        for cp in mine():
            cp.wait()

    return _Exchange(
        parts, [jax.ShapeDtypeStruct(a.shape, a.dtype) for a in parts],
        [pltpu.SemaphoreType.DMA((n, 7)), pltpu.SemaphoreType.DMA((n, 7)), pltpu.SemaphoreType.DMA((n,))],
        start, mid, finish)


N_CHIPS = N_DEV // 2


def _pair_exchange(parts):
    n = len(parts)

    def copies(ins, outs, sems):
        send_sems, recv_sems = sems
        x, y, c = _place()
        return [pltpu.make_async_remote_copy(
            src_ref=ins[k].at[2 * q + 1 - c], dst_ref=outs[k].at[q], send_sem=send_sems.at[k, q],
            recv_sem=recv_sems.at[k, q], device_id=(x, y, 1 - c), device_id_type=MESH_ID)
            for k in range(n) for q in range(N_CHIPS)]

    def start(ins, outs, sems):
        for cp in copies(ins, outs, sems):
            cp.start()

    def mid(ins, outs, sems):
        pass

    def finish(ins, outs, sems):
        for cp in copies(ins, outs, sems):
            cp.wait_recv()
        for cp in copies(ins, outs, sems):
            cp.wait_send()

    return _Exchange(
        parts, [jax.ShapeDtypeStruct((N_CHIPS,) + a.shape[1:], a.dtype) for a in parts],
        [pltpu.SemaphoreType.DMA((n, N_CHIPS)), pltpu.SemaphoreType.DMA((n, N_CHIPS))], start, mid, finish)


def _pair_add(parts, stage, name, copies=1):
    n = len(parts)
    core = lax.axis_index("c").reshape(1)

    def body(core_ref, *refs):
        for k in range(n):
            mine, theirs = refs[k], refs[n + k]
            total = (mine[0, 0].astype(F32) + theirs[0].astype(F32)).astype(BF16)
            for rep in range(copies):
                refs[(2 + rep) * n + k][0] = total

    in_specs, out_specs = [], []
    for a in parts:
        _, r, cdim = a.shape
        in_specs.append(pl.BlockSpec((1, 1, r // 2, cdim), lambda q, i, core_ref: (q, core_ref[0], i, 0)))
    for a in parts:
        _, r, cdim = a.shape
        in_specs.append(pl.BlockSpec((1, r // 2, cdim), lambda q, i, core_ref: (q, i, 0)))
        out_specs.append(pl.BlockSpec((1, r // 2, cdim), lambda q, i, core_ref: (q, i, 0)))
    return list(_call(
        body, name=name,
        grid_spec=pltpu.PrefetchScalarGridSpec(num_scalar_prefetch=1, grid=(N_CHIPS, 2), in_specs=in_specs,
                                               out_specs=out_specs * copies),
        out_shape=[jax.ShapeDtypeStruct(s.shape, BF16) for s in stage] * copies,
        compiler_params=_params("arbitrary", "arbitrary"),
    )(core, *[a.reshape((N_CHIPS, 2) + a.shape[1:]) for a in parts], *stage))


HBM = pl.BlockSpec(memory_space=pltpu.HBM)
SEMAPHORES = pl.BlockSpec(memory_space=pltpu.SEMAPHORE)


def _chip_copies(srcs, lands, send_sems, recv_sems, mine_is_dst):
    x, y, c = _place()
    me = 2 * x + y
    out = []
    for m in range(1, N_CHIPS):
        px, py, _ = _peer(x, y, c, 2 * m)
        peer = 2 * px + py
        for k in range(len(srcs)):
            pair = k * (N_CHIPS - 1) + m - 1
            out.append(pltpu.make_async_remote_copy(
                src_ref=srcs[k].at[peer], dst_ref=lands[k].at[me if mine_is_dst else peer],
                send_sem=send_sems[pair], recv_sem=recv_sems[pair],
                device_id=(px, py, c), device_id_type=MESH_ID))
    return out


def _chip_exchange_start(chip_parts, lands, after, name):
    n = len(chip_parts)
    pairs = n * (N_CHIPS - 1)

    def body(*refs):
        srcs, zones = refs[:n], refs[n:2 * n]
        sems = refs[2 * n + 1:2 * n + 1 + 2 * pairs]
        token = refs[-1]
        for cp in _chip_copies(srcs, zones, sems[:pairs], sems[pairs:], True):
            cp.start()
        token[...] = jnp.zeros(token.shape, F32)

    thru = [pltpu.HBM(a.shape, a.dtype) for a in chip_parts + lands]
    hbm = [pltpu.with_memory_space_constraint(a, pltpu.HBM) for a in chip_parts + lands]
    res = _call(
        body, name=name,
        out_shape=[pltpu.SemaphoreType.DMA(())] * (2 * pairs) + thru + [jax.ShapeDtypeStruct((8, 128), F32)],
        in_specs=[HBM] * (2 * n) + [ANY], out_specs=[SEMAPHORES] * (2 * pairs) + [HBM] * (2 * n) + [VMEM],
        input_output_aliases={i: 2 * pairs + i for i in range(2 * n)},
        compiler_params=pltpu.CompilerParams(has_side_effects=pltpu.SideEffectType.DATAFLOW_SIDE_EFFECTING),
    )(*hbm, after)
    sems, rest = list(res[:2 * pairs]), res[2 * pairs:]
    return sems, list(rest[:n]), list(rest[n:2 * n]), rest[-1]


def _chip_exchange_wait(sems, chip_parts, lands, after, name):
    n = len(chip_parts)
    pairs = n * (N_CHIPS - 1)

    def body(*refs):
        srcs, zones = refs[:n], refs[n:2 * n]
        sem_refs = refs[2 * n:2 * n + 2 * pairs]
        for cp in _chip_copies(srcs, zones, sem_refs[:pairs], sem_refs[pairs:], False):
            cp.wait_send()
            cp.wait_recv()

    thru = [pltpu.HBM(a.shape, a.dtype) for a in chip_parts + lands]
    res = _call(
        body, name=name, out_shape=thru,
        in_specs=[HBM] * (2 * n) + [SEMAPHORES] * (2 * pairs) + [ANY], out_specs=[HBM] * (2 * n),
        input_output_aliases={i: i for i in range(2 * n)},
        compiler_params=pltpu.CompilerParams(has_side_effects=pltpu.SideEffectType.DATAFLOW_SIDE_EFFECTING),
    )(*chip_parts, *lands, *sems, after)
    return list(res[n:])


def _chip_exchange(chip_parts):
    n = len(chip_parts)

    def tools(ins, outs, sems):
        send_sems, recv_sems, local_sems = sems
        x, y, c = _place()
        me = 2 * x + y

        def mine():
            return [pltpu.make_async_copy(ins[k].at[me], outs[k].at[me], local_sems.at[k]) for k in range(n)]

        def remote(dst_is_mine):
            out = []
            for m in range(1, N_CHIPS):
                px, py, _ = _peer(x, y, c, 2 * m)
                peer = 2 * px + py
                for k in range(n):
                    out.append(pltpu.make_async_remote_copy(
                        src_ref=ins[k].at[peer], dst_ref=outs[k].at[me if dst_is_mine else peer],
                        send_sem=send_sems.at[k, m - 1], recv_sem=recv_sems.at[k, m - 1],
                        device_id=(px, py, c), device_id_type=MESH_ID))
            return out

        return mine, remote

    def start(ins, outs, sems):
        mine, remote = tools(ins, outs, sems)
        for cp in mine() + remote(True):
            cp.start()

    def mid(ins, outs, sems):
        pass

    def finish(ins, outs, sems):
        mine, remote = tools(ins, outs, sems)
        for cp in remote(False):
            cp.wait_recv()
        for cp in remote(True):
            cp.wait_send()
        for cp in mine():
            cp.wait()

    return _Exchange(
        chip_parts, [jax.ShapeDtypeStruct(a.shape, a.dtype) for a in chip_parts],
        [pltpu.SemaphoreType.DMA((n, N_CHIPS - 1)), pltpu.SemaphoreType.DMA((n, N_CHIPS - 1)),
         pltpu.SemaphoreType.DMA((n,))], start, mid, finish)


def _both_exchanges(first, second):
    n_in, n_out, n_sem = len(first.ins), len(first.out_shape), len(first.sems)

    def phase(which):
        def run(ins, outs, sems):
            getattr(first, which)(ins[:n_in], outs[:n_out], sems[:n_sem])
            getattr(second, which)(ins[n_in:], outs[n_out:], sems[n_sem:])
        return run

    return _Exchange(first.ins + second.ins, first.out_shape + second.out_shape, first.sems + second.sems,
                     phase("start"), phase("mid"), phase("finish"))


def _run_exchange(ex, name, after=None):
    n_in, n_out = len(ex.ins), len(ex.out_shape)
    order = [] if after is None else [after]

    def body(*refs):
        ins, outs = refs[:n_in], refs[n_in + len(order):n_in + len(order) + n_out]
        sems = refs[n_in + len(order) + n_out:]
        ex.start(ins, outs, sems)
        ex.mid(ins, outs, sems)
        ex.finish(ins, outs, sems)

    return list(_call(body, name=name, out_shape=ex.out_shape, in_specs=[ANY] * (n_in + len(order)),
                      out_specs=[ANY] * n_out, scratch_shapes=ex.sems)(*ex.ins, *order))


def _call_hosting(body, ex, phases, *, in_specs, out_specs, out_shape, scratch_shapes, args, **kw):
    n_in, n_out, n_scr = len(in_specs), len(out_specs), len(scratch_shapes)
    if ex is None:
        res = _call(body, in_specs=in_specs, out_specs=out_specs, out_shape=out_shape,
                    scratch_shapes=scratch_shapes, **kw)(*args)
        return list(res), []
    n_xin, n_xout = len(ex.ins), len(ex.out_shape)

    def hosting(*refs):
        a, b = n_in, n_in + n_xin
        c, d = b + n_out, b + n_out + n_xout
        ins, xins, outs, xouts = refs[:a], refs[a:b], refs[b:c], refs[c:d]
        scr, sems = refs[d:d + n_scr], refs[d + n_scr:]
        first, mid, last = phases()

        @pl.when(first)
        def _():
            ex.start(xins, xouts, sems)

        body(*ins, *outs, *scr)

        @pl.when(mid)
        def _():
            ex.mid(xins, xouts, sems)

        @pl.when(last)
        def _():
            ex.finish(xins, xouts, sems)

    res = _call(hosting, in_specs=list(in_specs) + [ANY] * n_xin, out_specs=list(out_specs) + [ANY] * n_xout,
                out_shape=list(out_shape) + ex.out_shape, scratch_shapes=list(scratch_shapes) + ex.sems,
                **kw)(*args, *ex.ins)
    return list(res[:n_out]), list(res[n_out:])


def _grid_phases(dims, mid_fraction=0.8):
    total = 1
    for d in dims:
        total *= d
    mid = min(max(int(total * mid_fraction), 1), total - 1)

    def phases():
        step = pl.program_id(0)
        for axis in range(1, len(dims)):
            step = step * dims[axis] + pl.program_id(axis)
        return step == 0, step == mid, step == total - 1
    return phases


def _all_reduce_small(pack, name, after):
    rows = pack.shape[0]

    def body(in_ref, _, out_ref, recv, send_sems, recv_sems):
        x, y, c = _place()
        me = 4 * x + 2 * y + c
        recv[me] = in_ref[...]
        sent = []
        for m in range(1, N_DEV):
            px, py, pc = _peer(x, y, c, m)
            cp = pltpu.make_async_remote_copy(
                src_ref=in_ref, dst_ref=recv.at[me], send_sem=send_sems.at[m - 1], recv_sem=recv_sems.at[m - 1],
                device_id=(px, py, pc), device_id_type=MESH_ID)
            cp.start()
            sent.append(cp)
        for m in range(1, N_DEV):
            px, py, pc = _peer(x, y, c, m)
            peer = 4 * px + 2 * py + pc
            pltpu.make_async_remote_copy(
                src_ref=in_ref, dst_ref=recv.at[peer], send_sem=send_sems.at[m - 1], recv_sem=recv_sems.at[m - 1],
                device_id=(px, py, pc), device_id_type=MESH_ID).wait_recv()
        acc = recv[0]
        for d in range(1, N_DEV):
            acc = acc + recv[d]
        out_ref[...] = acc
        for cp in sent:
            cp.wait_send()

    return _call(
        body, name=name, out_shape=jax.ShapeDtypeStruct(pack.shape, F32), in_specs=[VMEM, ANY], out_specs=VMEM,
        scratch_shapes=[pltpu.VMEM((N_DEV, rows, pack.shape[1]), F32), pltpu.SemaphoreType.DMA((7,)),
                        pltpu.SemaphoreType.DMA((7,))],
        compiler_params=pltpu.CompilerParams(vmem_limit_bytes=VMEM_LIMIT),
    )(pack, after)


POOL_TS = 256


def _pool_counts(first_row, rows, win):
    t = first_row + lax.broadcasted_iota(jnp.int32, (rows, 1), 0)
    return jnp.minimum(t + 1, win).astype(F32)


def _pool_fwd(x, g, w, b, scale):
    nt = S // POOL_TS

    def body(x_ref, g_ref, w_ref, b_ref, s_ref, out_ref, diff_ref, ext):
        i = pl.program_id(0)

        @pl.when(i == 0)
        def _():
            ext[0:MAX_WIN, :] = jnp.zeros((MAX_WIN, D), F32)

        @pl.when(i > 0)
        def _():
            ext[0:MAX_WIN, :] = ext[POOL_TS:POOL_TS + MAX_WIN, :]

        xv = x_ref[...]
        h = xv * _rstd(xv) * g_ref[...]
        ext[MAX_WIN:, :] = h
        for gi in range(N_GROUPS):
            win = 2 << gi
            cols = slice(gi * GROUP, (gi + 1) * GROUP)
            sm = ext[:, cols]
            k = 1
            while k < win:
                sm = sm + pltpu.roll(sm, k, axis=0)
                k *= 2
            pooled = sm[MAX_WIN:, :] / _pool_counts(i * POOL_TS, POOL_TS, win)
            diff = (pooled - h[:, cols]).astype(BF16)
            yv = (_nn(diff, w_ref[gi]) + b_ref[:, cols]) * s_ref[:, cols]
            out_ref[:, cols] = xv[:, cols] + yv
            diff_ref[:, cols] = diff

    row = pl.BlockSpec((1, D), lambda i: (0, 0))
    tile = pl.BlockSpec((POOL_TS, D), lambda i: (i, 0))
    return _call(
        body, name="pool_fwd", grid=(nt,),
        in_specs=[tile, row, pl.BlockSpec((N_GROUPS, GROUP, GROUP), lambda i: (0, 0, 0)), row, row],
        out_specs=[tile, tile],
        out_shape=[jax.ShapeDtypeStruct((S, D), F32), jax.ShapeDtypeStruct((S, D), BF16)],
        scratch_shapes=[pltpu.VMEM((POOL_TS + MAX_WIN, D), F32)],
        compiler_params=_params("arbitrary"),
    )(x, g, w, b, scale)


def _pool_bwd(dy, x, diff, g, w, b, scale, ex=None):
    nt = S // POOL_TS

    def body(dy_ref, x_ref, diff_ref, g_ref, w_ref, b_ref, s_ref, gx_ref, dw_ref, db_ref, ds_ref, dg_ref, ext, dh):
        i = pl.program_id(0)
        first_row = (nt - 1 - i) * POOL_TS

        @pl.when(i == 0)
        def _():
            ext[POOL_TS:, :] = jnp.zeros((MAX_WIN, D), F32)
            dw_ref[...] = jnp.zeros(dw_ref.shape, F32)
            db_ref[...] = jnp.zeros(db_ref.shape, F32)
            ds_ref[...] = jnp.zeros(ds_ref.shape, F32)
            dg_ref[...] = jnp.zeros(dg_ref.shape, F32)

        @pl.when(i > 0)
        def _():
            ext[POOL_TS:, :] = ext[0:MAX_WIN, :]

        dyv = dy_ref[...]
        for gi in range(N_GROUPS):
            win = 2 << gi
            cols = slice(gi * GROUP, (gi + 1) * GROUP)
            dfb = diff_ref[:, cols]
            z = _nn(dfb, w_ref[gi]) + b_ref[:, cols]
            dyg = dyv[:, cols]
            ds_ref[:, cols] += _colsum8(dyg * z)
            dz = dyg * s_ref[:, cols]
            db_ref[:, cols] += _colsum8(dz)
            dzb = dz.astype(BF16)
            dw_ref[gi] += _tn(dfb, dzb)
            ddiff = _nt(dzb, w_ref[gi])
            ext[0:POOL_TS, cols] = ddiff / _pool_counts(first_row, POOL_TS, win)
            sm = ext[:, cols]
            k = 1
            while k < win:
                sm = sm + pltpu.roll(sm, POOL_TS + MAX_WIN - k, axis=0)
                k *= 2
            dh[:, cols] = sm[0:POOL_TS, :] - ddiff
        xv = x_ref[...]
        r = _rstd(xv)
        gv = g_ref[...]
        dhv = dh[...]
        dg_ref[...] += _colsum8(dhv * xv * r)
        gx_ref[...] = dyv + _rms_bwd(dhv, xv, r, gv)

    row = pl.BlockSpec((1, D), lambda i: (0, 0))
    tile = pl.BlockSpec((POOL_TS, D), lambda i: (nt - 1 - i, 0))
    acc = pl.BlockSpec((8, D), lambda i: (0, 0))
    wspec = pl.BlockSpec((N_GROUPS, GROUP, GROUP), lambda i: (0, 0, 0))
    return _call_hosting(
        body, ex, _grid_phases((nt,)), name="pool_bwd", grid=(nt,),
        in_specs=[tile, tile, tile, row, wspec, row, row],
        out_specs=[tile, wspec, acc, acc, acc],
        out_shape=[jax.ShapeDtypeStruct((S, D), F32), jax.ShapeDtypeStruct((N_GROUPS, GROUP, GROUP), F32),
                   jax.ShapeDtypeStruct((8, D), F32), jax.ShapeDtypeStruct((8, D), F32),
                   jax.ShapeDtypeStruct((8, D), F32)],
        scratch_shapes=[pltpu.VMEM((POOL_TS + MAX_WIN, D), F32), pltpu.VMEM((POOL_TS, D), F32)],
        args=(dy, x, diff, g, w, b, scale), compiler_params=_params("arbitrary"),
    )


FFN_TS = min(S, 1024)
FFN_TF = 256


def _ffn_fwd(x, g, wg_t, wu_t, wd, target=None, ex=None):
    ni, nj = S // FFN_TS, F // FFN_TF
    with_loss = target is not None

    def body(*refs):
        if with_loss:
            x_ref, g_ref, wg_ref, wu_ref, wd_ref, t_ref, out_ref, loss_ref, h_ref, gg_ref, uu_ref, hs, acc = refs
        else:
            x_ref, g_ref, wg_ref, wu_ref, wd_ref, out_ref, h_ref, gg_ref, uu_ref, hs, acc = refs
        i, j = pl.program_id(0), pl.program_id(1)

        @pl.when(j == 0)
        def _():
            xv = x_ref[...]
            hb = (xv * _rstd(xv) * g_ref[...]).astype(BF16)
            hs[...] = hb
            h_ref[...] = hb
            acc[...] = jnp.zeros(acc.shape, F32)

        hb = hs[...]
        gg = _nt(hb, wg_ref[...])
        uu = _nt(hb, wu_ref[...])
        gg_ref[...] = gg
        uu_ref[...] = uu
        a = (gg * jax.nn.sigmoid(gg) * uu).astype(BF16)
        acc[...] += _nn(a, wd_ref[...])

        @pl.when(j == nj - 1)
        def _():
            yv = x_ref[...] + acc[...]
            if with_loss:
                err = yv - t_ref[...]
                out_ref[...] = err * (1.0 / D)
                part = jnp.sum(err * err) * (0.5 / D)

                @pl.when(i == 0)
                def _():
                    loss_ref[...] = jnp.zeros(loss_ref.shape, F32)

                loss_ref[...] += jnp.broadcast_to(part, loss_ref.shape)
            else:
                out_ref[...] = yv

    xt = pl.BlockSpec((FFN_TS, D), lambda i, j: (i, 0))
    row = pl.BlockSpec((1, D), lambda i, j: (0, 0))
    wt = pl.BlockSpec((FFN_TF, D), lambda i, j: (j, 0))
    gt = pl.BlockSpec((FFN_TS, FFN_TF), lambda i, j: (i, j))
    in_specs = [xt, row, wt, wt, wt] + ([xt] if with_loss else [])
    out_specs = [xt] + ([pl.BlockSpec((8, 128), lambda i, j: (0, 0))] if with_loss else []) + [xt, gt, gt]
    out_shape = ([jax.ShapeDtypeStruct((S, D), F32)] + ([jax.ShapeDtypeStruct((8, 128), F32)] if with_loss else [])
                 + [jax.ShapeDtypeStruct((S, D), BF16), jax.ShapeDtypeStruct((S, F), F32),
                    jax.ShapeDtypeStruct((S, F), F32)])
    args = (x, g, wg_t, wu_t, wd) + ((target,) if with_loss else ())
    return _call_hosting(
        body, ex, _grid_phases((ni, nj), 0.8), name="ffn_fwd_loss" if with_loss else "ffn_fwd", grid=(ni, nj),
        in_specs=in_specs, out_specs=out_specs, out_shape=out_shape,
        scratch_shapes=[pltpu.VMEM((FFN_TS, D), BF16), pltpu.VMEM((FFN_TS, D), F32)], args=args,
        compiler_params=_params("arbitrary", "arbitrary"),
    )


def _ffn_bwd_weights(dout, h, gg, uu, wd, name, ex=None):
    nj = F // FFN_TF

    def body(do_ref, h_ref, gg_ref, uu_ref, wd_ref, dg_ref, du_ref, dwg_ref, dwu_ref, dwd_ref, dob):
        @pl.when(pl.program_id(0) == 0)
        def _():
            dob[...] = do_ref[...].astype(BF16)

        hb, dov = h_ref[...], dob[...]
        gv, uv = gg_ref[...], uu_ref[...]
        da = _nt(dov, wd_ref[...])
        sg = jax.nn.sigmoid(gv)
        sl = gv * sg
        ab = (sl * uv).astype(BF16)
        dub = (da * sl).astype(BF16)
        dgb = (da * uv * (sg * (1.0 + gv * (1.0 - sg)))).astype(BF16)
        dg_ref[...] = dgb
        du_ref[...] = dub
        dwg_ref[...] = _tn(dgb, hb).astype(BF16)
        dwu_ref[...] = _tn(dub, hb).astype(BF16)
        dwd_ref[...] = _tn(ab, dov).astype(BF16)

    once = pl.Buffered(1)
    whole = lambda: pl.BlockSpec((S, D), lambda j, _: (0, 0), pipeline_mode=once)
    wt = pl.BlockSpec((FFN_TF, D), lambda j, _: (j, 0))
    gt = pl.BlockSpec((S, FFN_TF), lambda j, _: (0, j))
    return _call_hosting(
        body, ex, _grid_phases((nj, 1)), name=name, grid=(nj, 1),
        in_specs=[whole(), whole(), gt, gt, wt], out_specs=[gt, gt, wt, wt, wt],
        out_shape=[jax.ShapeDtypeStruct((S, F), BF16)] * 2 + [jax.ShapeDtypeStruct((F, D), BF16)] * 3,
        scratch_shapes=[pltpu.VMEM((S, D), BF16)], args=(dout, h, gg, uu, wd),
        compiler_params=_params("arbitrary", "arbitrary"),
    )


BWD_TS = 512


def _ffn_bwd_input(dres, x, g, dg, du, wg_t, wu_t, name, ex=None):
    nt = S // BWD_TS

    def body(dres_ref, x_ref, g_ref, dg_ref, du_ref, wg_ref, wu_ref, dx_ref, dgam_ref):
        dh = _nn(dg_ref[...], wg_ref[...]) + _nn(du_ref[...], wu_ref[...])
        xv = x_ref[...]
        r = _rstd(xv)
        dx_ref[...] = dres_ref[...] + _rms_bwd(dh, xv, r, g_ref[...])

        @pl.when(pl.program_id(0) == 0)
        def _():
            dgam_ref[...] = jnp.zeros(dgam_ref.shape, F32)

        dgam_ref[...] += _colsum8(dh * xv * r)

    tile = pl.BlockSpec((BWD_TS, D), lambda i: (i, 0))
    ftile = pl.BlockSpec((BWD_TS, F), lambda i: (i, 0))
    wspec = lambda: pl.BlockSpec((F, D), lambda i: (0, 0), pipeline_mode=pl.Buffered(1))
    return _call_hosting(
        body, ex, _grid_phases((nt,)), name=name, grid=(nt,),
        in_specs=[tile, tile, pl.BlockSpec((1, D), lambda i: (0, 0)), ftile, ftile, wspec(), wspec()],
        out_specs=[tile, pl.BlockSpec((8, D), lambda i: (0, 0))],
        out_shape=[jax.ShapeDtypeStruct((S, D), F32), jax.ShapeDtypeStruct((8, D), F32)],
        scratch_shapes=[], args=(dres, x, g, dg, du, wg_t, wu_t), compiler_params=_params("arbitrary"),
    )


def _mm(a, b, mode, out_dtype, name, add=None, skip_rows=0):
    if mode == "nn":
        (m, kd), n = (a.shape[0] - skip_rows, a.shape[1]), b.shape[1]
    elif mode == "nt":
        (m, kd), n = (a.shape[0] - skip_rows, a.shape[1]), b.shape[0]
    else:
        (kd, m), n = a.shape, b.shape[1]
    tm, tn, tk = min(m, 1024), min(n, 1024), min(kd, 1024)
    if skip_rows and mode == "tn":
        tk = min(tk, skip_rows)
    elif skip_rows:
        tm = min(tm, skip_rows)
    skip_a = skip_rows // tm if mode != "tn" else 0
    skip_b = skip_rows // tk if mode == "tn" else 0
    assert skip_rows == skip_a * tm + skip_b * tk
    nk = kd // tk
    dot = {"nn": _nn, "nt": _nt, "tn": _tn}[mode]

    def body(*refs):
        if add is None:
            a_ref, b_ref, o_ref, acc = refs
        else:
            a_ref, b_ref, add_ref, o_ref, acc = refs
        k = pl.program_id(2)

        @pl.when(k == 0)
        def _():
            acc[...] = jnp.zeros(acc.shape, F32)

        acc[...] += dot(a_ref[...].astype(BF16), b_ref[...].astype(BF16))

        @pl.when(k == nk - 1)
        def _():
            res = acc[...]
            if add is not None:
                res = res + add_ref[...]
            o_ref[...] = res.astype(out_dtype)

    if mode == "tn":
        a_spec = pl.BlockSpec((tk, tm), lambda i, j, k: (k, i))
        b_spec = pl.BlockSpec((tk, tn), lambda i, j, k: (k + skip_b, j))
    else:
        a_spec = pl.BlockSpec((tm, tk), lambda i, j, k: (i + skip_a, k))
        b_spec = (pl.BlockSpec((tk, tn), lambda i, j, k: (k, j)) if mode == "nn"
                  else pl.BlockSpec((tn, tk), lambda i, j, k: (j, k)))
    o_spec = pl.BlockSpec((tm, tn), lambda i, j, k: (i, j))
    in_specs = [a_spec, b_spec] + ([o_spec] if add is not None else [])
    args = (a, b) + ((add,) if add is not None else ())
    return _call(
        body, name=name, grid=(m // tm, n // tn, nk), in_specs=in_specs, out_specs=o_spec,
        out_shape=jax.ShapeDtypeStruct((m, n), out_dtype), scratch_shapes=[pltpu.VMEM((tm, tn), F32)],
        compiler_params=_params("parallel", "parallel", "arbitrary"),
    )(*args)


PROJ_TS = 256


def _kvq_proj(x, g_kv, g_mix, wk, wv, wq, gk, gq, ex=None):
    lead = PADK // PROJ_TS

    def body(x_ref, gkv_ref, gmix_ref, wk_ref, wv_ref, wq_ref, gk_ref, gq_ref,
             hkv_ref, h1_ref, kpre_ref, qpre_ref, k_ref, v_ref, q_ref):
        i = pl.program_id(0)

        @pl.when(i < lead)
        def _():
            k_ref[...] = jnp.zeros(k_ref.shape, BF16)
            v_ref[...] = jnp.zeros(v_ref.shape, BF16)

        @pl.when(i >= lead)
        def _():
            xv = x_ref[...]
            xr = xv * _rstd(xv)
            hkv = (xr * gkv_ref[...]).astype(BF16)
            h1 = (xr * gmix_ref[...]).astype(BF16)
            hkv_ref[...] = hkv
            h1_ref[...] = h1
            kpre = _nn(hkv, wk_ref[...])
            qpre = _nn(h1, wq_ref[...])
            kpre_ref[...] = kpre
            qpre_ref[...] = qpre
            v_ref[...] = _nn(hkv, wv_ref[...]).astype(BF16)
            rk = lax.rsqrt(_seg_sum(kpre * kpre) * (1.0 / HEAD) + EPS)
            k_ref[...] = (kpre * rk * gk_ref[...]).astype(BF16)
            rq = lax.rsqrt(_seg_sum(qpre * qpre) * (1.0 / HEAD) + EPS)
            q_ref[...] = (qpre * rq * gq_ref[...]).astype(BF16)

    tile = pl.BlockSpec((PROJ_TS, D), lambda i: (jnp.maximum(i - lead, 0), 0))
    padded = pl.BlockSpec((PROJ_TS, D), lambda i: (i, 0))
    row = pl.BlockSpec((1, D), lambda i: (0, 0))
    wspec = pl.BlockSpec((D, D), lambda i: (0, 0))
    bf = jax.ShapeDtypeStruct((S, D), BF16)
    ff = jax.ShapeDtypeStruct((S, D), F32)
    bp = jax.ShapeDtypeStruct((PADK + S, D), BF16)
    return _call_hosting(
        body, ex, _grid_phases((lead + S // PROJ_TS,), 0.85), name="kvq_proj", grid=(lead + S // PROJ_TS,),
        in_specs=[tile, row, row, wspec, wspec, wspec, row, row],
        out_specs=[tile, tile, tile, tile, padded, padded, tile], out_shape=[bf, bf, ff, ff, bp, bp, bf],
        scratch_shapes=[], args=(x, g_kv, g_mix, wk, wv, wq, gk, gq), compiler_params=_params("arbitrary"),
    )


def _qkv_input_bwd(dq, qpre, gq, dkp, kpre, gk, dvp, wq, wk, wv, dres, x, g_mix, g_kv, ex=None):
    nt = S // PROJ_TS
    lead = PADK // PROJ_TS

    def head_bwd(dov, pv, hgv):
        r = lax.rsqrt(_seg_sum(pv * pv) * (1.0 / HEAD) + EPS)
        gd = dov * hgv
        dpre = r * gd - pv * (r * r * r) * (_seg_sum(gd * pv) * (1.0 / HEAD))
        return dpre.astype(BF16), _colsum8(dov * pv * r)

    def fold_heads(full):
        fold = full[:, 0:128]
        for blk in range(1, D // 128):
            fold = fold + full[:, blk * 128:(blk + 1) * 128]
        return fold + pltpu.roll(fold, HEAD, axis=1)

    def body(dq_ref, qpre_ref, gq_ref, dk_ref, kpre_ref, gk_ref, dv_ref, wq_ref, wk_ref, wv_ref, dres_ref, x_ref,
             gmix_ref, gkv_ref, dx_ref, dqpre_ref, dkpre_ref, dgmix_ref, dgkv_ref, dgq_ref, dgk_ref, accq, acck):
        i = pl.program_id(0)

        @pl.when(i == 0)
        def _():
            accq[...] = jnp.zeros(accq.shape, F32)
            acck[...] = jnp.zeros(acck.shape, F32)
            dgmix_ref[...] = jnp.zeros(dgmix_ref.shape, F32)
            dgkv_ref[...] = jnp.zeros(dgkv_ref.shape, F32)

        dqb, cq = head_bwd(dq_ref[...], qpre_ref[...], gq_ref[...])
        dkb, ck = head_bwd(dk_ref[...], kpre_ref[...], gk_ref[...])
        dqpre_ref[...] = dqb
        dkpre_ref[...] = dkb
        accq[...] += cq
        acck[...] += ck
        dh1 = _nt(dqb, wq_ref[...])
        dhkv = _nt(dkb, wk_ref[...]) + _nt(dv_ref[...].astype(BF16), wv_ref[...])
        xv = x_ref[...]
        r = _rstd(xv)
        dx_ref[...] = dres_ref[...] + _rms_bwd(dh1, xv, r, gmix_ref[...]) + _rms_bwd(dhkv, xv, r, gkv_ref[...])
        dgmix_ref[...] += _colsum8(dh1 * xv * r)
        dgkv_ref[...] += _colsum8(dhkv * xv * r)

        @pl.when(i == nt - 1)
        def _():
            dgq_ref[...] = fold_heads(accq[...])
            dgk_ref[...] = fold_heads(acck[...])

    tile = pl.BlockSpec((PROJ_TS, D), lambda i: (i, 0))
    behind = pl.BlockSpec((PROJ_TS, D), lambda i: (i + lead, 0))
    row = pl.BlockSpec((1, D), lambda i: (0, 0))
    wspec = lambda: pl.BlockSpec((D, D), lambda i: (0, 0), pipeline_mode=pl.Buffered(1))
    acc = pl.BlockSpec((8, D), lambda i: (0, 0))
    small = pl.BlockSpec((8, 128), lambda i: (0, 0))
    bf = jax.ShapeDtypeStruct((S, D), BF16)
    return _call_hosting(
        body, ex, _grid_phases((nt,)), name="qkv_input_bwd", grid=(nt,),
        in_specs=[tile, tile, row, behind, tile, row, behind, wspec(), wspec(), wspec(), tile, tile, row, row],
        out_specs=[tile, tile, tile, acc, acc, small, small],
        out_shape=[jax.ShapeDtypeStruct((S, D), F32), bf, bf, jax.ShapeDtypeStruct((8, D), F32),
                   jax.ShapeDtypeStruct((8, D), F32), jax.ShapeDtypeStruct((8, 128), F32),
                   jax.ShapeDtypeStruct((8, 128), F32)],
        scratch_shapes=[pltpu.VMEM((8, D), F32), pltpu.VMEM((8, D), F32)],
        args=(dq, qpre, gq, dkp, kpre, gk, dvp, wq, wk, wv, dres, x, g_mix, g_kv),
        compiler_params=_params("arbitrary"),
    )


def _toeplitz_from_table(table):
    far = jnp.broadcast_to(table[:, N_REL - 1:], (N_HEADS, PADK - MAX_REL + 1))
    near = table[:, N_REL - 2::-1]
    past = jnp.broadcast_to(table[:, 0:1], (N_HEADS, MAX_REL))
    wrap = jnp.broadcast_to(table[:, N_REL - 1:], (N_HEADS, QB - 1))
    return jnp.concatenate([far, near, past, wrap], axis=1).reshape(N_HEADS, 1, TOEP)


def _table_grad_from_toeplitz(dtp, seg):
    lo = PADK - MAX_REL + 1
    near = dtp[:, lo + N_REL - 3:lo - 1:-1]
    return jnp.concatenate([seg[:, 1:2], near, seg[:, 0:1]], axis=1)


def _bias_band(tp):
    def body(tp_ref, out_ref):
        bv = pltpu.roll(jnp.broadcast_to(tp_ref[0], (QB, TOEP)), 0, axis=1, stride=1, stride_axis=0)
        out_ref[0] = jnp.where(_band_mask(), bv[:, 0:KB], NEG_INF)

    return _call(
        body, name="bias_band", grid=(N_HEADS,),
        in_specs=[pl.BlockSpec((1, 1, TOEP), lambda h: (h, 0, 0))],
        out_specs=pl.BlockSpec((1, QB, KB), lambda h: (h, 0, 0)),
        out_shape=jax.ShapeDtypeStruct((N_HEADS, QB, KB), F32),
        compiler_params=_params("parallel"),
    )(tp)


def _bias_grad(dband, after):
    lo, hi = PADK - MAX_REL + 1, PADK + MAX_REL

    def body(db_ref, _, dtp_ref, seg_ref):
        bv = jnp.concatenate([db_ref[0], jnp.zeros((QB, TOEP - KB), F32)], axis=1)
        row = lax.broadcasted_iota(jnp.int32, (QB, TOEP), 0)
        k = 1
        while k < QB:
            bv = jnp.where((row & k) != 0, pltpu.roll(bv, TOEP - k, axis=1), bv)
            k *= 2
        col = jnp.sum(bv, axis=0, keepdims=True)
        dtp_ref[0] = col
        u = lax.broadcasted_iota(jnp.int32, (1, TOEP), 1)
        far = jnp.sum(jnp.where((u < lo) | (u > hi + MAX_REL), col, 0.0))
        past = jnp.sum(jnp.where((u >= hi) & (u <= hi + MAX_REL), col, 0.0))
        lane = lax.broadcasted_iota(jnp.int32, (1, 128), 1)
        seg_ref[0] = jnp.where(lane == 0, far, jnp.where(lane == 1, past, 0.0))

    return _call(
        body, name="bias_grad", grid=(N_HEADS,),
        in_specs=[pl.BlockSpec((1, QB, KB), lambda h: (h, 0, 0)), ANY],
        out_specs=[pl.BlockSpec((1, 1, TOEP), lambda h: (h, 0, 0)), pl.BlockSpec((1, 1, 128), lambda h: (h, 0, 0))],
        out_shape=[jax.ShapeDtypeStruct((N_HEADS, 1, TOEP), F32), jax.ShapeDtypeStruct((N_HEADS, 1, 128), F32)],
        compiler_params=_params("parallel"),
    )(dband, after)


N_QB = S // QB
HEADS_PER_STEP = 4
ATT_LANES = HEADS_PER_STEP * HEAD
N_HG = D // ATT_LANES


def _band_mask():
    qc = lax.broadcasted_iota(jnp.int32, (QB, KB), 0) // CHUNK
    kc = lax.broadcasted_iota(jnp.int32, (QB, KB), 1) // CHUNK
    return (kc >= qc) & (kc <= qc + LEFT)


def _half_scale(hh, scale):
    lane = lax.broadcasted_iota(jnp.int32, (1, 128), 1)
    return jnp.where((lane < HEAD) == (hh == 0), scale, 0.0).astype(BF16)


def _probs(qh, kb, bias, first_key):
    sc = _nt(qh, kb) + bias
    if first_key is not None:
        sc = jnp.where(lax.broadcasted_iota(jnp.int32, (QB, KB), 1) >= first_key, sc, NEG_INF)
    e = jnp.exp(sc - jnp.max(sc, axis=-1, keepdims=True))
    return e * (1.0 / jnp.sum(e, axis=-1, keepdims=True))


def _by_padding(cb, compute):
    @pl.when(cb < PADK // QB)
    def _():
        compute(PADK - cb * QB)

    @pl.when(cb >= PADK // QB)
    def _():
        compute(None)


def _attn_fwd(q, kp, vp, bias, ex=None):
    def body(q_ref, k_ref, v_ref, b_ref, o_ref):
        cb = pl.program_id(1)
        band = pl.ds(pl.multiple_of(cb * QB, QB), KB)
        low = lax.broadcasted_iota(jnp.int32, (QB, 128), 1) < HEAD

        def compute(first_key):
            for pair in range(HEADS_PER_STEP // 2):
                lanes = pl.ds(pair * 128, 128)
                kb, vb, qv = k_ref[band, lanes], v_ref[band, lanes], q_ref[:, lanes]
                outs = []
                for hh in range(2):
                    pb = _probs(qv * _half_scale(hh, ATTN_SCALE), kb, b_ref[2 * pair + hh], first_key).astype(BF16)
                    outs.append(_nn(pb, vb))
                o_ref[:, lanes] = jnp.where(low, outs[0], outs[1]).astype(BF16)

        _by_padding(cb, compute)

    qspec = pl.BlockSpec((QB, ATT_LANES), lambda hg, cb: (cb, hg))
    kspec = pl.BlockSpec((PADK + S, ATT_LANES), lambda hg, cb: (0, hg))
    return _call_hosting(
        body, ex, _grid_phases((N_HG, N_QB), 0.85), name="attn_fwd", grid=(N_HG, N_QB),
        in_specs=[qspec, kspec, kspec, pl.BlockSpec((HEADS_PER_STEP, QB, KB), lambda hg, cb: (hg, 0, 0))],
        out_specs=[qspec], out_shape=[jax.ShapeDtypeStruct((S, D), BF16)], scratch_shapes=[],
        args=(q, kp, vp, bias), compiler_params=_params("arbitrary", "arbitrary"),
    )


def _attn_bwd(q, kp, vp, bias, do, ex=None):
    def body(q_ref, k_ref, v_ref, b_ref, do_ref, dq_ref, dk_ref, dv_ref, db_ref):
        cb = pl.program_id(1)

        @pl.when(cb == 0)
        def _():
            dk_ref[...] = jnp.zeros(dk_ref.shape, F32)
            dv_ref[...] = jnp.zeros(dv_ref.shape, F32)
            db_ref[...] = jnp.zeros(db_ref.shape, F32)

        band = pl.ds(pl.multiple_of(cb * QB, QB), KB)
        low = lax.broadcasted_iota(jnp.int32, (QB, 128), 1) < HEAD

        def compute(first_key):
            for pair in range(HEADS_PER_STEP // 2):
                lanes = pl.ds(pair * 128, 128)
                kb, vb = k_ref[band, lanes], v_ref[band, lanes]
                qv, dov = q_ref[:, lanes], do_ref[:, lanes]
                dq = jnp.zeros((QB, 128), F32)
                dkb = jnp.zeros((KB, 128), F32)
                dvb = jnp.zeros((KB, 128), F32)
                for hh in range(2):
                    sel = low if hh == 0 else jnp.logical_not(low)
                    doh = dov * _half_scale(hh, 1.0)
                    p = _probs(qv * _half_scale(hh, ATTN_SCALE), kb, b_ref[2 * pair + hh], first_key)
                    dp = _nt(doh, vb)
                    dvb = dvb + _tn(p.astype(BF16), doh)
                    ds = p * (dp - jnp.sum(dp * p, axis=-1, keepdims=True))
                    db_ref[2 * pair + hh] += ds
                    dsb = (ds * ATTN_SCALE).astype(BF16)
                    dq = dq + jnp.where(sel, _nn(dsb, kb), 0.0)
                    dkb = dkb + _tn(dsb, qv * _half_scale(hh, 1.0))
                dq_ref[:, lanes] = dq
                dk_ref[band, lanes] += dkb
                dv_ref[band, lanes] += dvb

        _by_padding(cb, compute)

    qspec = pl.BlockSpec((QB, ATT_LANES), lambda hg, cb: (cb, hg))
    kspec = pl.BlockSpec((PADK + S, ATT_LANES), lambda hg, cb: (0, hg))
    bspec = pl.BlockSpec((HEADS_PER_STEP, QB, KB), lambda hg, cb: (hg, 0, 0))
    kf = jax.ShapeDtypeStruct((PADK + S, D), F32)
    return _call_hosting(
        body, ex, _grid_phases((N_HG, N_QB)), name="attn_bwd", grid=(N_HG, N_QB),
        in_specs=[qspec, kspec, kspec, bspec, qspec],
        out_specs=[qspec, kspec, kspec, bspec],
        out_shape=[jax.ShapeDtypeStruct((S, D), F32), kf, kf, jax.ShapeDtypeStruct((N_HEADS, QB, KB), F32)],
        scratch_shapes=[], args=(q, kp, vp, bias, do), compiler_params=_params("arbitrary", "arbitrary"),
    )


def _adam_math(w, g, m, v):
    m = ADAM_B1 * m + (1.0 - ADAM_B1) * g
    v = ADAM_B2 * v + (1.0 - ADAM_B2) * (g * g)
    m_hat = m / (1.0 - ADAM_B1 ** ADAM_STEP)
    v_hat = v / (1.0 - ADAM_B2 ** ADAM_STEP)
    delta = -ADAM_LR * (m_hat / (jnp.sqrt(v_hat) + ADAM_EPS) + ADAM_WD * w)
    return delta, m, v


def _sum_parts(parts, name, after=None):
    count, r, c = parts.shape
    tr = r // 2 if r % 16 == 0 and r > 64 else r

    def body(p_ref, *rest):
        o_ref = rest[-1]
        acc = p_ref[0].astype(F32)
        for d in range(1, count):
            acc = acc + p_ref[d].astype(F32)
        o_ref[...] = acc

    return _call(
        body, name=name, grid=(r // tr,),
        in_specs=[pl.BlockSpec((count, tr, c), lambda i: (0, i, 0))] + ([ANY] if after is not None else []),
        out_specs=pl.BlockSpec((tr, c), lambda i: (i, 0)),
        out_shape=jax.ShapeDtypeStruct((r, c), F32), compiler_params=_params("parallel"),
    )(parts, *([after] if after is not None else []))


def _row_tile(r):
    for cand in (256, 176, 128, 64, 32, 16, 8):
        if r % cand == 0:
            return cand
    return r


def _adam_layer(w, g, m, v, layer, name, prev=None):
    nl, r, c = w.shape
    tr = _row_tile(r)

    def body(w_ref, g_ref, m_ref, v_ref, *rest):
        go_ref, d_ref, nm_ref, nv_ref = rest[-4:]
        gv = g_ref[...]
        delta, nm, nv = _adam_math(w_ref[0], gv, m_ref[0], v_ref[0])
        go_ref[0] = gv
        d_ref[0] = delta
        nm_ref[0] = nm
        nv_ref[0] = nv

    lspec = pl.BlockSpec((1, tr, c), lambda i: (layer, i, 0))
    sd = jax.ShapeDtypeStruct((nl, r, c), F32)
    extra = list(prev) if prev is not None else []
    return _call(
        body, name=name, grid=(r // tr,),
        in_specs=[lspec, pl.BlockSpec((tr, c), lambda i: (i, 0)), lspec, lspec] + [ANY] * len(extra),
        out_specs=[lspec] * 4, out_shape=[sd] * 4,
        input_output_aliases={4 + t: t for t in range(len(extra))},
        compiler_params=_params("parallel"),
    )(w, g, m, v, *extra)


def _adam(w, g, m, v, name):
    r, c = w.shape
    tr = _row_tile(r)

    def body(w_ref, g_ref, m_ref, v_ref, d_ref, nm_ref, nv_ref):
        delta, nm, nv = _adam_math(w_ref[...], g_ref[...], m_ref[...], v_ref[...])
        d_ref[...] = delta
        nm_ref[...] = nm
        nv_ref[...] = nv

    spec = pl.BlockSpec((tr, c), lambda i: (i, 0))
    sd = jax.ShapeDtypeStruct((r, c), F32)
    return _call(
        body, name=name, grid=(r // tr,), in_specs=[spec] * 4, out_specs=[spec] * 3, out_shape=[sd] * 3,
        compiler_params=_params("parallel"),
    )(w, g, m, v)


def _pad8(rows):
    return jnp.pad(rows, ((0, 8 - rows.shape[0]), (0, 0)))


def kernel(x, norm_mix_g, norm_ffn_g, pool_w, pool_b, pool_scale, kv_norm_g, w_k, w_v, k_norm_g, w_q, q_norm_g, rel_bias, w_o, w_gate, w_up, w_down, loss_target, m_norm_mix_g, m_norm_ffn_g, m_pool_w, m_pool_b, m_pool_scale, m_kv_norm_g, m_w_k, m_w_v, m_k_norm_g, m_w_q, m_q_norm_g, m_rel_bias, m_w_o, m_w_gate, m_w_up, m_w_down, v_norm_mix_g, v_norm_ffn_g, v_pool_w, v_pool_b, v_pool_scale, v_kv_norm_g, v_w_k, v_w_v, v_k_norm_g, v_w_q, v_q_norm_g, v_rel_bias, v_w_o, v_w_gate, v_w_up, v_w_down):
    assert x.shape == (1, S, D) and w_gate.shape == (2, D, F_SHARD) and w_k.shape == (D_SHARD, D)
    xin, target = x[0], loss_target[0]

    ffn_shards = [[w_gate[layer].T.astype(BF16), w_up[layer].T.astype(BF16), w_down[layer].astype(BF16)]
                  for layer in range(2)]
    att_shards = [w_k.astype(BF16), w_v.astype(BF16), w_q[0].astype(BF16), w_o[0].astype(BF16)]
    pool_shard = pool_w[0].astype(BF16).reshape(N_GROUPS * POOL_SHARD, GROUP)
    small = jnp.concatenate([pool_b[0].reshape(1, N_GROUPS * POOL_SHARD), pool_scale], axis=1)

    full0 = _run_exchange(_gather_exchange(ffn_shards[0] + [pool_shard, _pad8(small)]), "gather_layer0")
    ffn_w0 = [a.reshape(F, D) for a in full0[:3]]
    pw_f = full0[3].reshape(N_DEV, N_GROUPS, POOL_SHARD, GROUP).transpose(1, 0, 2, 3).reshape(N_GROUPS, GROUP, GROUP)
    small_f = full0[4][:, 0, :]
    pb_f = small_f[:, :N_GROUPS * POOL_SHARD].reshape(N_DEV, N_GROUPS, POOL_SHARD).transpose(1, 0, 2).reshape(1, D)
    ps_f = small_f[:, N_GROUPS * POOL_SHARD:].reshape(1, D)

    g_mix0, g_mix1 = norm_mix_g[0:1], norm_mix_g[1:2]
    g_ffn0, g_ffn1 = norm_ffn_g[0:1], norm_ffn_g[1:2]
    g_kv = kv_norm_g.reshape(1, D)
    gk_t = jnp.tile(k_norm_g.reshape(1, HEAD), (1, N_HEADS))
    gq_t = jnp.tile(q_norm_g.reshape(1, HEAD), (1, N_HEADS))

    x1, diff = _pool_fwd(xin, g_mix0, pw_f, pb_f, ps_f)
    (x2, hf0, gg0, uu0), full_att = _ffn_fwd(x1, g_ffn0, *ffn_w0, ex=_gather_exchange(att_shards))
    wk_f, wv_f, wq_f, wo_f = [a.reshape(D, D) for a in full_att]
    (hkv, h1, kpre, qpre, kp, vp, qq), full1_gate = _kvq_proj(x2, g_kv, g_mix1, wk_f, wv_f, wq_f, gk_t, gq_t,
                                                                ex=_gather_exchange(ffn_shards[1][:1]))
    bias = _bias_band(_toeplitz_from_table(rel_bias[0]))
    (att,), full1_rest = _attn_fwd(qq, kp, vp, bias, ex=_gather_exchange(ffn_shards[1][1:]))
    ffn_w1 = [a.reshape(F, D) for a in full1_gate + full1_rest]
    x3 = _mm(att, wo_f, "nn", F32, "attn_out", add=x2)
    (dx4, loss_rows, hf1, gg1, uu1), _ = _ffn_fwd(x3, g_ffn1, *ffn_w1, target=target)

    def blocks(dw):
        return dw.reshape(N_DEV, dw.shape[0] // N_DEV, dw.shape[1])

    (dgg1, duu1, dwg1, dwu1, dwd1), _ = _ffn_bwd_weights(dx4, hf1, gg1, uu1, ffn_w1[2], name="ffn_bwd1")
    parts1 = [blocks(dw) for dw in (dwg1, dwu1, dwd1)]
    (dx3, dg_ffn1), stage1 = _ffn_bwd_input(dx4, x3, g_ffn1, dgg1, duu1, ffn_w1[0], ffn_w1[1], "ffn_dx1",
                                            ex=_pair_exchange(parts1))
    chip1 = _pair_add(parts1, stage1, "pair_add_ffn1")
    datt = _mm(dx3, wo_f, "nt", BF16, "d_attn")
    dwo = _mm(att, dx3, "tn", BF16, "d_wo")
    (dq, dkp, dvp, dband), recv1 = _attn_bwd(qq, kp, vp, bias, datt, ex=_chip_exchange(chip1))
    (dx2, dqpre, dkpre, dg_mix1, dg_kv, dgq, dgk), _ = _qkv_input_bwd(
        dq, qpre, gq_t, dkp, kpre, gk_t, dvp, wq_f, wk_f, wv_f, dx3, x2, g_mix1, g_kv)
    dwq = _mm(h1, dqpre, "tn", BF16, "d_wq")
    dwk = _mm(hkv, dkpre, "tn", BF16, "d_wk")
    dwv = _mm(hkv, dvp, "tn", BF16, "d_wv", skip_rows=PADK)
    parts_att = [blocks(dw) for dw in (dwk, dwv, dwq, dwo)]
    (dgg0, duu0, dwg0, dwu0, dwd0), stage_att = _ffn_bwd_weights(
        dx2, hf0, gg0, uu0, ffn_w0[2], name="ffn_bwd0", ex=_pair_exchange(parts_att))
    chip_att = _pair_add(parts_att, stage_att, "pair_add_att")
    parts0 = [blocks(dw) for dw in (dwg0, dwu0, dwd0)]
    (dx1, dg_ffn0), landed = _ffn_bwd_input(
        dx2, x1, g_ffn0, dgg0, duu0, ffn_w0[0], ffn_w0[1], "ffn_dx0",
        ex=_both_exchanges(_chip_exchange(chip_att), _pair_exchange(parts0)))
    recv_att, stage0 = landed[:len(chip_att)], landed[len(chip_att):]
    both = _pair_add(parts0, stage0, "pair_add_ffn0", copies=2)
    flight, chip0, land0, token = _chip_exchange_start(both[:3], both[3:], dx1, "scatter_ffn0_start")
    (grad_x, dpw, db_rows, ds_rows, dg_mix0), _ = _pool_bwd(dx1, xin, diff, g_mix0 + token[0:1, 0:1], pw_f, pb_f, ps_f)
    dtp, seg = _bias_grad(dband, token)

    weights = dict(norm_mix_g=norm_mix_g, norm_ffn_g=norm_ffn_g, pool_w=pool_w, pool_b=pool_b,
                   pool_scale=pool_scale, kv_norm_g=kv_norm_g, w_k=w_k, w_v=w_v, k_norm_g=k_norm_g, w_q=w_q,
                   q_norm_g=q_norm_g, rel_bias=rel_bias, w_o=w_o, w_gate=w_gate, w_up=w_up, w_down=w_down)
    mom1 = dict(norm_mix_g=m_norm_mix_g, norm_ffn_g=m_norm_ffn_g, pool_w=m_pool_w, pool_b=m_pool_b,
                pool_scale=m_pool_scale, kv_norm_g=m_kv_norm_g, w_k=m_w_k, w_v=m_w_v, k_norm_g=m_k_norm_g,
                w_q=m_w_q, q_norm_g=m_q_norm_g, rel_bias=m_rel_bias, w_o=m_w_o, w_gate=m_w_gate, w_up=m_w_up,
                w_down=m_w_down)
    mom2 = dict(norm_mix_g=v_norm_mix_g, norm_ffn_g=v_norm_ffn_g, pool_w=v_pool_w, pool_b=v_pool_b,
                pool_scale=v_pool_scale, kv_norm_g=v_kv_norm_g, w_k=v_w_k, w_v=v_w_v, k_norm_g=v_k_norm_g,
                w_q=v_w_q, q_norm_g=v_q_norm_g, rel_bias=v_rel_bias, w_o=v_w_o, w_gate=v_w_gate, w_up=v_w_up,
                w_down=v_w_down)
    names = list(weights)
    grads, deltas, new_m, new_v = {}, {}, {}, {}

    def summed(arrs, tag, after):
        return [_sum_parts(a, "sum_parts_%s%d" % (tag, k), after) for k, a in enumerate(arrs)]

    def adam_flat(nm):
        shape = weights[nm].shape
        flat = lambda a: a.reshape(-1, shape[-1])
        dl, m1, m2 = _adam(flat(weights[nm]), flat(grads[nm]), flat(mom1[nm]), flat(mom2[nm]), "adam_" + nm)
        deltas[nm], new_m[nm], new_v[nm] = dl.reshape(shape), m1.reshape(shape), m2.reshape(shape)

    ffn_names = ("w_gate", "w_up", "w_down")
    def ffn_view(nm, a):
        return a if nm == "w_down" else a.transpose(0, 2, 1)

    sums1 = summed(recv1, "ffn1_", token)
    layer1 = {nm: _adam_layer(ffn_view(nm, weights[nm]), g, ffn_view(nm, mom1[nm]), ffn_view(nm, mom2[nm]), 1,
                              "adam1_" + nm)
              for nm, g in zip(ffn_names, sums1)}
    g_wk, g_wv, g_wq, g_wo = summed(recv_att, "att_", token)
    grads.update(w_k=g_wk, w_v=g_wv, w_q=g_wq[None], w_o=g_wo[None])
    for nm in ("w_k", "w_v", "w_q", "w_o"):
        adam_flat(nm)

    dpw_blocks = dpw.reshape(N_GROUPS, N_DEV, POOL_SHARD, GROUP).transpose(1, 0, 2, 3)
    dpw_blocks = dpw_blocks.reshape(N_DEV, N_GROUPS * POOL_SHARD, GROUP).astype(BF16)
    recv_pool = _run_exchange(_scatter_exchange([dpw_blocks]), "scatter_pool", after=deltas["w_o"])
    misc = jnp.concatenate([dgk[0:1, 0:HEAD], dgq[0:1, 0:HEAD], loss_rows[0:1, 0:1],
                            seg[:, 0, 0].reshape(1, N_HEADS), seg[:, 0, 1].reshape(1, N_HEADS)], axis=1)
    misc = jnp.pad(misc, ((0, 0), (0, D - misc.shape[1])))
    vec_rows = jnp.concatenate([dg_mix0[0:1], dg_mix1[0:1], dg_ffn0[0:1], dg_ffn1[0:1], dg_kv[0:1],
                                db_rows[0:1], ds_rows[0:1], misc], axis=0)
    pack = jnp.concatenate([vec_rows, dtp.reshape(N_HEADS, TOEP)], axis=0)
    tot = _all_reduce_small(pack, "reduce_small", after=recv_pool[0])

    loss = tot[7, 2 * HEAD]
    seg_tot = jnp.stack([tot[7, 2 * HEAD + 1:2 * HEAD + 1 + N_HEADS],
                         tot[7, 2 * HEAD + 1 + N_HEADS:2 * HEAD + 1 + 2 * N_HEADS]], axis=1)
    me = 4 * lax.axis_index("x") + 2 * lax.axis_index("y") + lax.axis_index("c")
    g_pool_b = lax.dynamic_slice_in_dim(tot[5].reshape(N_GROUPS, GROUP), me * POOL_SHARD, POOL_SHARD, axis=1)
    grads.update(
        norm_mix_g=tot[0:2], norm_ffn_g=tot[2:4], kv_norm_g=tot[4], k_norm_g=tot[7, 0:HEAD],
        q_norm_g=tot[7, HEAD:2 * HEAD].reshape(1, HEAD),
        rel_bias=_table_grad_from_toeplitz(tot[8:8 + N_HEADS], seg_tot).reshape(1, N_HEADS, N_REL),
        pool_b=g_pool_b.reshape(1, N_GROUPS, POOL_SHARD),
        pool_scale=lax.dynamic_slice_in_dim(tot[6:7], me * D_SHARD, D_SHARD, axis=1),
        pool_w=summed(recv_pool, "pool_", None)[0].reshape(1, N_GROUPS, POOL_SHARD, GROUP))
    adam_flat("pool_w")
    small_names = [nm for nm in names if nm not in ffn_names + ("w_k", "w_v", "w_q", "w_o", "pool_w")]

    def pack_small(tree):
        cols = []
        for nm in small_names:
            flat = tree[nm].reshape(-1)
            cols.append(jnp.pad(flat, (0, -flat.shape[0] % 1024)))
        return jnp.concatenate(cols).reshape(-1, 128)

    dl, m1, m2 = _adam(pack_small(weights), pack_small(grads), pack_small(mom1), pack_small(mom2), "adam_small")

    def unpack_small(packed, out):
        flat, off = packed.reshape(-1), 0
        for nm in small_names:
            size = weights[nm].size
            out[nm] = flat[off:off + size].reshape(weights[nm].shape)
            off += size + (-size % 1024)

    unpack_small(dl, deltas)
    unpack_small(m1, new_m)
    unpack_small(m2, new_v)

    recv0 = _chip_exchange_wait(flight, chip0, land0, dl, "scatter_ffn0_wait")
    sums0 = summed(recv0, "ffn0_", None)
    for nm, g in zip(ffn_names, sums0):
        done = _adam_layer(ffn_view(nm, weights[nm]), g, ffn_view(nm, mom1[nm]), ffn_view(nm, mom2[nm]), 0,
                           "adam0_" + nm, prev=layer1[nm])
        grads[nm], deltas[nm], new_m[nm], new_v[nm] = [ffn_view(nm, a) for a in done]

    return (loss, grad_x[None], *[grads[nm] for nm in names], *[deltas[nm] for nm in names],
            *[new_m[nm] for nm in names], *[new_v[nm] for nm in names])
```

```python
import functools

import jax
import jax.numpy as jnp
from jax import lax
from jax.experimental import pallas as pl
from jax.experimental.pallas import tpu as pltpu

F32 = jnp.float32
BF16 = jnp.bfloat16
MESH_ID = pl.DeviceIdType.MESH

N_DEV = 8
S = 2048
D = 1024
F = 2816
F_SHARD = F // N_DEV
D_SHARD = D // N_DEV
N_GROUPS = 4
GROUP = D // N_GROUPS
POOL_SHARD = GROUP // N_DEV
MAX_WIN = 16
HEAD = 64
N_HEADS = D // HEAD
CHUNK = 64
LEFT = 8
QB = 4 * CHUNK
KB = QB + LEFT * CHUNK
PADK = LEFT * CHUNK
TOEP = 1024
N_REL = 257
MAX_REL = 128
EPS = 1e-6
NEG_INF = -1e30
ATTN_SCALE = HEAD ** -0.5

ADAM_LR = 0.001
ADAM_B1 = 0.9
ADAM_B2 = 0.999
ADAM_EPS = 1e-08
ADAM_WD = 0.01
ADAM_STEP = 10

VMEM_LIMIT = 52 * 1024 * 1024

ANY = pl.BlockSpec(memory_space=pl.ANY)
VMEM = pl.BlockSpec(memory_space=pltpu.VMEM)


def _call(body, **kw):
    return pl.pallas_call(body, **kw)


def _params(*sem):
    return pltpu.CompilerParams(dimension_semantics=sem, vmem_limit_bytes=VMEM_LIMIT)


def _dot(a, b, dims):
    return lax.dot_general(a, b, (dims, ((), ())), preferred_element_type=F32)


def _nn(a, b):
    return _dot(a, b, ((1,), (0,)))


def _nt(a, b):
    return _dot(a, b, ((1,), (1,)))


def _tn(a, b):
    return _dot(a, b, ((0,), (0,)))


def _rstd(x):
    return lax.rsqrt(jnp.mean(x * x, axis=-1, keepdims=True) + EPS)


def _rms_bwd(dh, x, r, g):
    gd = dh * g
    return r * gd - x * (r * r * r) * jnp.mean(gd * x, axis=-1, keepdims=True)


def _colsum8(v):
    return jnp.broadcast_to(jnp.sum(v, axis=0, keepdims=True), (8, v.shape[1]))


def _seg_sum(v):
    r = lax.broadcasted_iota(jnp.int32, (128, 128), 0) // HEAD
    c = lax.broadcasted_iota(jnp.int32, (128, 128), 1) // HEAD
    ones = jnp.where(r == c, 1.0, 0.0).astype(BF16)
    out = []
    for blk in range(v.shape[1] // 128):
        part = v[:, blk * 128:(blk + 1) * 128]
        hi = part.astype(BF16)
        rest = part - hi.astype(F32)
        mid = rest.astype(BF16)
        lo = (rest - mid.astype(F32)).astype(BF16)
        out.append(_nn(hi, ones) + _nn(mid, ones) + _nn(lo, ones))
    return jnp.concatenate(out, axis=1)


def _place():
    return lax.axis_index("x"), lax.axis_index("y"), lax.axis_index("c")


class _Exchange:
    def __init__(self, ins, out_shape, sems, start, mid, finish):
        self.ins, self.out_shape, self.sems = list(ins), list(out_shape), list(sems)
        self.start, self.mid, self.finish = start, mid, finish


def _gather_exchange(shards):
    n = len(shards)

    def tools(ins, outs, sems):
        send_sems, recv_sems, local_sems = sems
        x, y, c = _place()
        me, sibling = (x, y, c), (x, y, 1 - c)
        chips = [(1 - x, y), (x, 1 - y), (1 - x, 1 - y)]

        def slot(k, px, py, pc):
            return outs[k].at[4 * px + 2 * py + pc]

        def copy(k, s, block, to, own=False):
            return pltpu.make_async_remote_copy(
                src_ref=ins[k] if own else slot(k, *block), dst_ref=slot(k, *block),
                send_sem=send_sems.at[k, s], recv_sem=recv_sems.at[k, s], device_id=to, device_id_type=MESH_ID)

        def mine():
            return [pltpu.make_async_copy(ins[k], slot(k, *me), local_sems.at[k]) for k in range(n)]

        def first():
            out = []
            for k in range(n):
                out.append(copy(k, 0, me, sibling, own=True))
                out += [copy(k, 1 + j, me, (*chip, c), own=True) for j, chip in enumerate(chips)]
            return out

        def landed(j):
            return [copy(k, 1 + j, (*chips[j], c), me) for k in range(n)]

        def passed(j):
            return [copy(k, 4 + j, (*chips[j], c), sibling) for k in range(n)]

        def from_sibling():
            out = []
            for k in range(n):
                out.append(copy(k, 0, sibling, me))
                out += [copy(k, 4 + j, (*chip, 1 - c), me) for j, chip in enumerate(chips)]
            return out

        return mine, first, landed, passed, from_sibling

    def start(ins, outs, sems):
        mine, first, _, _, _ = tools(ins, outs, sems)
        for cp in mine() + first():
            cp.start()

    def mid(ins, outs, sems):
        _, _, landed, passed, _ = tools(ins, outs, sems)
        for j in range(3):
            for arrived, onward in zip(landed(j), passed(j)):
                arrived.wait_recv()
                onward.start()

    def finish(ins, outs, sems):
        mine, first, _, passed, from_sibling = tools(ins, outs, sems)
        for cp in from_sibling():
            cp.wait_recv()
        for cp in first() + passed(0) + passed(1) + passed(2):
            cp.wait_send()
        for cp in mine():
            cp.wait()

    return _Exchange(
        shards, [jax.ShapeDtypeStruct((N_DEV,) + a.shape, a.dtype) for a in shards],
        [pltpu.SemaphoreType.DMA((n, 7)), pltpu.SemaphoreType.DMA((n, 7)), pltpu.SemaphoreType.DMA((n,))],
        start, mid, finish)


def _peer(x, y, c, m):
    px = 1 - x if m & 4 else x
    py = 1 - y if m & 2 else y
    pc = 1 - c if m & 1 else c
    return px, py, pc


def _scatter_exchange(parts):
    n = len(parts)

    def tools(ins, outs, sems):
        send_sems, recv_sems, local_sems = sems
        x, y, c = _place()
        me = 4 * x + 2 * y + c
        def mine():
            return [pltpu.make_async_copy(ins[k].at[me], outs[k].at[me], local_sems.at[k]) for k in range(n)]

        def remote(dst_is_mine):
            out = []
            for m in range(1, N_DEV):
                px, py, pc = _peer(x, y, c, m)
                peer = 4 * px + 2 * py + pc
                for k in range(n):
                    out.append(pltpu.make_async_remote_copy(
                        src_ref=ins[k].at[peer], dst_ref=outs[k].at[me if dst_is_mine else peer],
                        send_sem=send_sems.at[k, m - 1], recv_sem=recv_sems.at[k, m - 1],
                        device_id=(px, py, pc), device_id_type=MESH_ID))
            return out

        return mine, remote

    def start(ins, outs, sems):
        mine, remote = tools(ins, outs, sems)
        for cp in mine() + remote(True):
            cp.start()

    def mid(ins, outs, sems):
        pass

    def finish(ins, outs, sems):
        mine, remote = tools(ins, outs, sems)
        for cp in remote(False):
            cp.wait_recv()
        for cp in remote(True):
            cp.wait_send()
        for cp in mine():
            cp.wait()

    return _Exchange(
        parts, [jax.ShapeDtypeStruct(a.shape, a.dtype) for a in parts],
        [pltpu.SemaphoreType.DMA((n, 7)), pltpu.SemaphoreType.DMA((n, 7)), pltpu.SemaphoreType.DMA((n,))],
        start, mid, finish)


N_CHIPS = N_DEV // 2


def _pair_exchange(parts):
    n = len(parts)

    def copies(ins, outs, sems):
        send_sems, recv_sems = sems
        x, y, c = _place()
        return [pltpu.make_async_remote_copy(
            src_ref=ins[k].at[2 * q + 1 - c], dst_ref=outs[k].at[q], send_sem=send_sems.at[k, q],
            recv_sem=recv_sems.at[k, q], device_id=(x, y, 1 - c), device_id_type=MESH_ID)
            for k in range(n) for q in range(N_CHIPS)]

    def start(ins, outs, sems):
        for cp in copies(ins, outs, sems):
            cp.start()

    def mid(ins, outs, sems):
        pass

    def finish(ins, outs, sems):
        for cp in copies(ins, outs, sems):
            cp.wait_recv()
        for cp in copies(ins, outs, sems):
            cp.wait_send()

    return _Exchange(
        parts, [jax.ShapeDtypeStruct((N_CHIPS,) + a.shape[1:], a.dtype) for a in parts],
        [pltpu.SemaphoreType.DMA((n, N_CHIPS)), pltpu.SemaphoreType.DMA((n, N_CHIPS))], start, mid, finish)


def _pair_add(parts, stage, name, copies=1):
    n = len(parts)
    core = lax.axis_index("c").reshape(1)

    def body(core_ref, *refs):
        for k in range(n):
            mine, theirs = refs[k], refs[n + k]
            total = (mine[0, 0].astype(F32) + theirs[0].astype(F32)).astype(BF16)
            for rep in range(copies):
                refs[(2 + rep) * n + k][0] = total

    in_specs, out_specs = [], []
    for a in parts:
        _, r, cdim = a.shape
        in_specs.append(pl.BlockSpec((1, 1, r // 2, cdim), lambda q, i, core_ref: (q, core_ref[0], i, 0)))
    for a in parts:
        _, r, cdim = a.shape
        in_specs.append(pl.BlockSpec((1, r // 2, cdim), lambda q, i, core_ref: (q, i, 0)))
        out_specs.append(pl.BlockSpec((1, r // 2, cdim), lambda q, i, core_ref: (q, i, 0)))
    return list(_call(
        body, name=name,
        grid_spec=pltpu.PrefetchScalarGridSpec(num_scalar_prefetch=1, grid=(N_CHIPS, 2), in_specs=in_specs,
                                               out_specs=out_specs * copies),
        out_shape=[jax.ShapeDtypeStruct(s.shape, BF16) for s in stage] * copies,
        compiler_params=_params("arbitrary", "arbitrary"),
    )(core, *[a.reshape((N_CHIPS, 2) + a.shape[1:]) for a in parts], *stage))


HBM = pl.BlockSpec(memory_space=pltpu.HBM)
SEMAPHORES = pl.BlockSpec(memory_space=pltpu.SEMAPHORE)


def _chip_copies(srcs, lands, send_sems, recv_sems, mine_is_dst):
    x, y, c = _place()
    me = 2 * x + y
    out = []
    for m in range(1, N_CHIPS):
        px, py, _ = _peer(x, y, c, 2 * m)
        peer = 2 * px + py
        for k in range(len(srcs)):
            pair = k * (N_CHIPS - 1) + m - 1
            out.append(pltpu.make_async_remote_copy(
                src_ref=srcs[k].at[peer], dst_ref=lands[k].at[me if mine_is_dst else peer],
                send_sem=send_sems[pair], recv_sem=recv_sems[pair],
                device_id=(px, py, c), device_id_type=MESH_ID))
    return out


def _chip_exchange_start(chip_parts, lands, after, name):
    n = len(chip_parts)
    pairs = n * (N_CHIPS - 1)

    def body(*refs):
        srcs, zones = refs[:n], refs[n:2 * n]
        sems = refs[2 * n + 1:2 * n + 1 + 2 * pairs]
        token = refs[-1]
        for cp in _chip_copies(srcs, zones, sems[:pairs], sems[pairs:], True):
            cp.start()
        token[...] = jnp.zeros(token.shape, F32)

    thru = [pltpu.HBM(a.shape, a.dtype) for a in chip_parts + lands]
    hbm = [pltpu.with_memory_space_constraint(a, pltpu.HBM) for a in chip_parts + lands]
    res = _call(
        body, name=name,
        out_shape=[pltpu.SemaphoreType.DMA(())] * (2 * pairs) + thru + [jax.ShapeDtypeStruct((8, 128), F32)],
        in_specs=[HBM] * (2 * n) + [ANY], out_specs=[SEMAPHORES] * (2 * pairs) + [HBM] * (2 * n) + [VMEM],
        input_output_aliases={i: 2 * pairs + i for i in range(2 * n)},
        compiler_params=pltpu.CompilerParams(has_side_effects=pltpu.SideEffectType.DATAFLOW_SIDE_EFFECTING),
    )(*hbm, after)
    sems, rest = list(res[:2 * pairs]), res[2 * pairs:]
    return sems, list(rest[:n]), list(rest[n:2 * n]), rest[-1]


def _chip_exchange_wait(sems, chip_parts, lands, after, name):
    n = len(chip_parts)
    pairs = n * (N_CHIPS - 1)

    def body(*refs):
        srcs, zones = refs[:n], refs[n:2 * n]
        sem_refs = refs[2 * n:2 * n + 2 * pairs]
        for cp in _chip_copies(srcs, zones, sem_refs[:pairs], sem_refs[pairs:], False):
            cp.wait_send()
            cp.wait_recv()

    thru = [pltpu.HBM(a.shape, a.dtype) for a in chip_parts + lands]
    res = _call(
        body, name=name, out_shape=thru,
        in_specs=[HBM] * (2 * n) + [SEMAPHORES] * (2 * pairs) + [ANY], out_specs=[HBM] * (2 * n),
        input_output_aliases={i: i for i in range(2 * n)},
        compiler_params=pltpu.CompilerParams(has_side_effects=pltpu.SideEffectType.DATAFLOW_SIDE_EFFECTING),
    )(*chip_parts, *lands, *sems, after)
    return list(res[n:])


def _run_exchange(ex, name, after=None):
    n_in, n_out = len(ex.ins), len(ex.out_shape)
    order = [] if after is None else [after]

    def body(*refs):
        ins, outs = refs[:n_in], refs[n_in + len(order):n_in + len(order) + n_out]
        sems = refs[n_in + len(order) + n_out:]
        ex.start(ins, outs, sems)
        ex.mid(ins, outs, sems)
        ex.finish(ins, outs, sems)

    return list(_call(body, name=name, out_shape=ex.out_shape, in_specs=[ANY] * (n_in + len(order)),
                      out_specs=[ANY] * n_out, scratch_shapes=ex.sems)(*ex.ins, *order))


def _call_hosting(body, ex, phases, *, in_specs, out_specs, out_shape, scratch_shapes, args, **kw):
    n_in, n_out, n_scr = len(in_specs), len(out_specs), len(scratch_shapes)
    if ex is None:
        res = _call(body, in_specs=in_specs, out_specs=out_specs, out_shape=out_shape,
                    scratch_shapes=scratch_shapes, **kw)(*args)
        return list(res), []
    n_xin, n_xout = len(ex.ins), len(ex.out_shape)

    def hosting(*refs):
        a, b = n_in, n_in + n_xin
        c, d = b + n_out, b + n_out + n_xout
        ins, xins, outs, xouts = refs[:a], refs[a:b], refs[b:c], refs[c:d]
        scr, sems = refs[d:d + n_scr], refs[d + n_scr:]
        first, mid, last = phases()

        @pl.when(first)
        def _():
            ex.start(xins, xouts, sems)

        body(*ins, *outs, *scr)

        @pl.when(mid)
        def _():
            ex.mid(xins, xouts, sems)

        @pl.when(last)
        def _():
            ex.finish(xins, xouts, sems)

    res = _call(hosting, in_specs=list(in_specs) + [ANY] * n_xin, out_specs=list(out_specs) + [ANY] * n_xout,
                out_shape=list(out_shape) + ex.out_shape, scratch_shapes=list(scratch_shapes) + ex.sems,
                **kw)(*args, *ex.ins)
    return list(res[:n_out]), list(res[n_out:])


def _grid_phases(dims, mid_fraction=0.8):
    total = 1
    for d in dims:
        total *= d
    mid = min(max(int(total * mid_fraction), 1), total - 1)

    def phases():
        step = pl.program_id(0)
        for axis in range(1, len(dims)):
            step = step * dims[axis] + pl.program_id(axis)
        return step == 0, step == mid, step == total - 1
    return phases


def _all_reduce_small(pack, name, after):
    rows = pack.shape[0]

    def body(in_ref, _, out_ref, recv, send_sems, recv_sems):
        x, y, c = _place()
        me = 4 * x + 2 * y + c
        recv[me] = in_ref[...]
        sent = []
        for m in range(1, N_DEV):
            px, py, pc = _peer(x, y, c, m)
            cp = pltpu.make_async_remote_copy(
                src_ref=in_ref, dst_ref=recv.at[me], send_sem=send_sems.at[m - 1], recv_sem=recv_sems.at[m - 1],
                device_id=(px, py, pc), device_id_type=MESH_ID)
            cp.start()
            sent.append(cp)
        for m in range(1, N_DEV):
            px, py, pc = _peer(x, y, c, m)
            peer = 4 * px + 2 * py + pc
            pltpu.make_async_remote_copy(
                src_ref=in_ref, dst_ref=recv.at[peer], send_sem=send_sems.at[m - 1], recv_sem=recv_sems.at[m - 1],
                device_id=(px, py, pc), device_id_type=MESH_ID).wait_recv()
        acc = recv[0]
        for d in range(1, N_DEV):
            acc = acc + recv[d]
        out_ref[...] = acc
        for cp in sent:
            cp.wait_send()

    return _call(
        body, name=name, out_shape=jax.ShapeDtypeStruct(pack.shape, F32), in_specs=[VMEM, ANY], out_specs=VMEM,
        scratch_shapes=[pltpu.VMEM((N_DEV, rows, pack.shape[1]), F32), pltpu.SemaphoreType.DMA((7,)),
                        pltpu.SemaphoreType.DMA((7,))],
        compiler_params=pltpu.CompilerParams(vmem_limit_bytes=VMEM_LIMIT),
    )(pack, after)


POOL_TS = 256


def _pool_counts(first_row, rows, win):
    t = first_row + lax.broadcasted_iota(jnp.int32, (rows, 1), 0)
    return jnp.minimum(t + 1, win).astype(F32)


def _pool_fwd(x, g, w, b, scale):
    nt = S // POOL_TS

    def body(x_ref, g_ref, w_ref, b_ref, s_ref, out_ref, diff_ref, ext):
        i = pl.program_id(0)

        @pl.when(i == 0)
        def _():
            ext[0:MAX_WIN, :] = jnp.zeros((MAX_WIN, D), F32)

        @pl.when(i > 0)
        def _():
            ext[0:MAX_WIN, :] = ext[POOL_TS:POOL_TS + MAX_WIN, :]

        xv = x_ref[...]
        h = xv * _rstd(xv) * g_ref[...]
        ext[MAX_WIN:, :] = h
        for gi in range(N_GROUPS):
            win = 2 << gi
            cols = slice(gi * GROUP, (gi + 1) * GROUP)
            sm = ext[:, cols]
            k = 1
            while k < win:
                sm = sm + pltpu.roll(sm, k, axis=0)
                k *= 2
            pooled = sm[MAX_WIN:, :] / _pool_counts(i * POOL_TS, POOL_TS, win)
            diff = (pooled - h[:, cols]).astype(BF16)
            yv = (_nn(diff, w_ref[gi]) + b_ref[:, cols]) * s_ref[:, cols]
            out_ref[:, cols] = xv[:, cols] + yv
            diff_ref[:, cols] = diff

    row = pl.BlockSpec((1, D), lambda i: (0, 0))
    tile = pl.BlockSpec((POOL_TS, D), lambda i: (i, 0))
    return _call(
        body, name="pool_fwd", grid=(nt,),
        in_specs=[tile, row, pl.BlockSpec((N_GROUPS, GROUP, GROUP), lambda i: (0, 0, 0)), row, row],
        out_specs=[tile, tile],
        out_shape=[jax.ShapeDtypeStruct((S, D), F32), jax.ShapeDtypeStruct((S, D), BF16)],
        scratch_shapes=[pltpu.VMEM((POOL_TS + MAX_WIN, D), F32)],
        compiler_params=_params("arbitrary"),
    )(x, g, w, b, scale)


def _pool_bwd(dy, x, diff, g, w, b, scale, ex=None):
    nt = S // POOL_TS

    def body(dy_ref, x_ref, diff_ref, g_ref, w_ref, b_ref, s_ref, gx_ref, dw_ref, db_ref, ds_ref, dg_ref, ext, dh):
        i = pl.program_id(0)
        first_row = (nt - 1 - i) * POOL_TS

        @pl.when(i == 0)
        def _():
            ext[POOL_TS:, :] = jnp.zeros((MAX_WIN, D), F32)
            dw_ref[...] = jnp.zeros(dw_ref.shape, F32)
            db_ref[...] = jnp.zeros(db_ref.shape, F32)
            ds_ref[...] = jnp.zeros(ds_ref.shape, F32)
            dg_ref[...] = jnp.zeros(dg_ref.shape, F32)

        @pl.when(i > 0)
        def _():
            ext[POOL_TS:, :] = ext[0:MAX_WIN, :]

        dyv = dy_ref[...]
        for gi in range(N_GROUPS):
            win = 2 << gi
            cols = slice(gi * GROUP, (gi + 1) * GROUP)
            dfb = diff_ref[:, cols]
            z = _nn(dfb, w_ref[gi]) + b_ref[:, cols]
            dyg = dyv[:, cols]
            ds_ref[:, cols] += _colsum8(dyg * z)
            dz = dyg * s_ref[:, cols]
            db_ref[:, cols] += _colsum8(dz)
            dzb = dz.astype(BF16)
            dw_ref[gi] += _tn(dfb, dzb)
            ddiff = _nt(dzb, w_ref[gi])
            ext[0:POOL_TS, cols] = ddiff / _pool_counts(first_row, POOL_TS, win)
            sm = ext[:, cols]
            k = 1
            while k < win:
                sm = sm + pltpu.roll(sm, POOL_TS + MAX_WIN - k, axis=0)
                k *= 2
            dh[:, cols] = sm[0:POOL_TS, :] - ddiff
        xv = x_ref[...]
        r = _rstd(xv)
        gv = g_ref[...]
        dhv = dh[...]
        dg_ref[...] += _colsum8(dhv * xv * r)
        gx_ref[...] = dyv + _rms_bwd(dhv, xv, r, gv)

    row = pl.BlockSpec((1, D), lambda i: (0, 0))
    tile = pl.BlockSpec((POOL_TS, D), lambda i: (nt - 1 - i, 0))
    acc = pl.BlockSpec((8, D), lambda i: (0, 0))
    wspec = pl.BlockSpec((N_GROUPS, GROUP, GROUP), lambda i: (0, 0, 0))
    return _call_hosting(
        body, ex, _grid_phases((nt,)), name="pool_bwd", grid=(nt,),
        in_specs=[tile, tile, tile, row, wspec, row, row],
        out_specs=[tile, wspec, acc, acc, acc],
        out_shape=[jax.ShapeDtypeStruct((S, D), F32), jax.ShapeDtypeStruct((N_GROUPS, GROUP, GROUP), F32),
                   jax.ShapeDtypeStruct((8, D), F32), jax.ShapeDtypeStruct((8, D), F32),
                   jax.ShapeDtypeStruct((8, D), F32)],
        scratch_shapes=[pltpu.VMEM((POOL_TS + MAX_WIN, D), F32), pltpu.VMEM((POOL_TS, D), F32)],
        args=(dy, x, diff, g, w, b, scale), compiler_params=_params("arbitrary"),
    )


FFN_TS = min(S, 1024)
FFN_TF = 256


def _ffn_fwd(x, g, wg_t, wu_t, wd, target=None, ex=None):
    ni, nj = S // FFN_TS, F // FFN_TF
    with_loss = target is not None

    def body(*refs):
        if with_loss:
            x_ref, g_ref, wg_ref, wu_ref, wd_ref, t_ref, out_ref, loss_ref, h_ref, gg_ref, uu_ref, hs, acc = refs
        else:
            x_ref, g_ref, wg_ref, wu_ref, wd_ref, out_ref, h_ref, gg_ref, uu_ref, hs, acc = refs
        i, j = pl.program_id(0), pl.program_id(1)

        @pl.when(j == 0)
        def _():
            xv = x_ref[...]
            hb = (xv * _rstd(xv) * g_ref[...]).astype(BF16)
            hs[...] = hb
            h_ref[...] = hb
            acc[...] = jnp.zeros(acc.shape, F32)

        hb = hs[...]
        gg = _nt(hb, wg_ref[...])
        uu = _nt(hb, wu_ref[...])
        gg_ref[...] = gg
        uu_ref[...] = uu
        a = (gg * jax.nn.sigmoid(gg) * uu).astype(BF16)
        acc[...] += _nn(a, wd_ref[...])

        @pl.when(j == nj - 1)
        def _():
            yv = x_ref[...] + acc[...]
            if with_loss:
                err = yv - t_ref[...]
                out_ref[...] = err * (1.0 / D)
                part = jnp.sum(err * err) * (0.5 / D)

                @pl.when(i == 0)
                def _():
                    loss_ref[...] = jnp.zeros(loss_ref.shape, F32)

                loss_ref[...] += jnp.broadcast_to(part, loss_ref.shape)
            else:
                out_ref[...] = yv

    xt = pl.BlockSpec((FFN_TS, D), lambda i, j: (i, 0))
    row = pl.BlockSpec((1, D), lambda i, j: (0, 0))
    wt = pl.BlockSpec((FFN_TF, D), lambda i, j: (j, 0))
    gt = pl.BlockSpec((FFN_TS, FFN_TF), lambda i, j: (i, j))
    in_specs = [xt, row, wt, wt, wt] + ([xt] if with_loss else [])
    out_specs = [xt] + ([pl.BlockSpec((8, 128), lambda i, j: (0, 0))] if with_loss else []) + [xt, gt, gt]
    out_shape = ([jax.ShapeDtypeStruct((S, D), F32)] + ([jax.ShapeDtypeStruct((8, 128), F32)] if with_loss else [])
                 + [jax.ShapeDtypeStruct((S, D), BF16), jax.ShapeDtypeStruct((S, F), F32),
                    jax.ShapeDtypeStruct((S, F), F32)])
    args = (x, g, wg_t, wu_t, wd) + ((target,) if with_loss else ())
    return _call_hosting(
        body, ex, _grid_phases((ni, nj), 0.8), name="ffn_fwd_loss" if with_loss else "ffn_fwd", grid=(ni, nj),
        in_specs=in_specs, out_specs=out_specs, out_shape=out_shape,
        scratch_shapes=[pltpu.VMEM((FFN_TS, D), BF16), pltpu.VMEM((FFN_TS, D), F32)], args=args,
        compiler_params=_params("arbitrary", "arbitrary"),
    )


def _ffn_bwd_weights(dout, h, gg, uu, wd, name, ex=None):
    nj = F // FFN_TF

    def body(do_ref, h_ref, gg_ref, uu_ref, wd_ref, dg_ref, du_ref, dwg_ref, dwu_ref, dwd_ref, dob):
        @pl.when(pl.program_id(0) == 0)
        def _():
            dob[...] = do_ref[...].astype(BF16)

        hb, dov = h_ref[...], dob[...]
        gv, uv = gg_ref[...], uu_ref[...]
        da = _nt(dov, wd_ref[...])
        sg = jax.nn.sigmoid(gv)
        sl = gv * sg
        ab = (sl * uv).astype(BF16)
        dub = (da * sl).astype(BF16)
        dgb = (da * uv * (sg * (1.0 + gv * (1.0 - sg)))).astype(BF16)
        dg_ref[...] = dgb
        du_ref[...] = dub
        dwg_ref[...] = _tn(dgb, hb).astype(BF16)
        dwu_ref[...] = _tn(dub, hb).astype(BF16)
        dwd_ref[...] = _tn(ab, dov).astype(BF16)

    once = pl.Buffered(1)
    whole = lambda: pl.BlockSpec((S, D), lambda j, _: (0, 0), pipeline_mode=once)
    wt = pl.BlockSpec((FFN_TF, D), lambda j, _: (j, 0))
    gt = pl.BlockSpec((S, FFN_TF), lambda j, _: (0, j))
    return _call_hosting(
        body, ex, _grid_phases((nj, 1)), name=name, grid=(nj, 1),
        in_specs=[whole(), whole(), gt, gt, wt], out_specs=[gt, gt, wt, wt, wt],
        out_shape=[jax.ShapeDtypeStruct((S, F), BF16)] * 2 + [jax.ShapeDtypeStruct((F, D), BF16)] * 3,
        scratch_shapes=[pltpu.VMEM((S, D), BF16)], args=(dout, h, gg, uu, wd),
        compiler_params=_params("arbitrary", "arbitrary"),
    )


BWD_TS = 512


def _ffn_bwd_input(dres, x, g, dg, du, wg_t, wu_t, name, ex=None):
    nt = S // BWD_TS

    def body(dres_ref, x_ref, g_ref, dg_ref, du_ref, wg_ref, wu_ref, dx_ref, dgam_ref):
        dh = _nn(dg_ref[...], wg_ref[...]) + _nn(du_ref[...], wu_ref[...])
        xv = x_ref[...]
        r = _rstd(xv)
        dx_ref[...] = dres_ref[...] + _rms_bwd(dh, xv, r, g_ref[...])

        @pl.when(pl.program_id(0) == 0)
        def _():
            dgam_ref[...] = jnp.zeros(dgam_ref.shape, F32)

        dgam_ref[...] += _colsum8(dh * xv * r)

    tile = pl.BlockSpec((BWD_TS, D), lambda i: (i, 0))
    ftile = pl.BlockSpec((BWD_TS, F), lambda i: (i, 0))
    wspec = lambda: pl.BlockSpec((F, D), lambda i: (0, 0), pipeline_mode=pl.Buffered(1))
    return _call_hosting(
        body, ex, _grid_phases((nt,)), name=name, grid=(nt,),
        in_specs=[tile, tile, pl.BlockSpec((1, D), lambda i: (0, 0)), ftile, ftile, wspec(), wspec()],
        out_specs=[tile, pl.BlockSpec((8, D), lambda i: (0, 0))],
        out_shape=[jax.ShapeDtypeStruct((S, D), F32), jax.ShapeDtypeStruct((8, D), F32)],
        scratch_shapes=[], args=(dres, x, g, dg, du, wg_t, wu_t), compiler_params=_params("arbitrary"),
    )


def _mm(a, b, mode, out_dtype, name, add=None, skip_rows=0):
    if mode == "nn":
        (m, kd), n = (a.shape[0] - skip_rows, a.shape[1]), b.shape[1]
    elif mode == "nt":
        (m, kd), n = (a.shape[0] - skip_rows, a.shape[1]), b.shape[0]
    else:
        (kd, m), n = a.shape, b.shape[1]
    tm, tn, tk = min(m, 1024), min(n, 1024), min(kd, 1024)
    if skip_rows and mode == "tn":
        tk = min(tk, skip_rows)
    elif skip_rows:
        tm = min(tm, skip_rows)
    skip_a = skip_rows // tm if mode != "tn" else 0
    skip_b = skip_rows // tk if mode == "tn" else 0
    assert skip_rows == skip_a * tm + skip_b * tk
    nk = kd // tk
    dot = {"nn": _nn, "nt": _nt, "tn": _tn}[mode]

    def body(*refs):
        if add is None:
            a_ref, b_ref, o_ref, acc = refs
        else:
            a_ref, b_ref, add_ref, o_ref, acc = refs
        k = pl.program_id(2)

        @pl.when(k == 0)
        def _():
            acc[...] = jnp.zeros(acc.shape, F32)

        acc[...] += dot(a_ref[...].astype(BF16), b_ref[...].astype(BF16))

        @pl.when(k == nk - 1)
        def _():
            res = acc[...]
            if add is not None:
                res = res + add_ref[...]
            o_ref[...] = res.astype(out_dtype)

    if mode == "tn":
        a_spec = pl.BlockSpec((tk, tm), lambda i, j, k: (k, i))
        b_spec = pl.BlockSpec((tk, tn), lambda i, j, k: (k + skip_b, j))
    else:
        a_spec = pl.BlockSpec((tm, tk), lambda i, j, k: (i + skip_a, k))
        b_spec = (pl.BlockSpec((tk, tn), lambda i, j, k: (k, j)) if mode == "nn"
                  else pl.BlockSpec((tn, tk), lambda i, j, k: (j, k)))
    o_spec = pl.BlockSpec((tm, tn), lambda i, j, k: (i, j))
    in_specs = [a_spec, b_spec] + ([o_spec] if add is not None else [])
    args = (a, b) + ((add,) if add is not None else ())
    return _call(
        body, name=name, grid=(m // tm, n // tn, nk), in_specs=in_specs, out_specs=o_spec,
        out_shape=jax.ShapeDtypeStruct((m, n), out_dtype), scratch_shapes=[pltpu.VMEM((tm, tn), F32)],
        compiler_params=_params("parallel", "parallel", "arbitrary"),
    )(*args)


PROJ_TS = 256


def _kvq_proj(x, g_kv, g_mix, wk, wv, wq, gk, gq, ex=None):
    lead = PADK // PROJ_TS

    def body(x_ref, gkv_ref, gmix_ref, wk_ref, wv_ref, wq_ref, gk_ref, gq_ref,
             hkv_ref, h1_ref, kpre_ref, qpre_ref, k_ref, v_ref, q_ref):
        i = pl.program_id(0)

        @pl.when(i < lead)
        def _():
            k_ref[...] = jnp.zeros(k_ref.shape, BF16)
            v_ref[...] = jnp.zeros(v_ref.shape, BF16)

        @pl.when(i >= lead)
        def _():
            xv = x_ref[...]
            xr = xv * _rstd(xv)
            hkv = (xr * gkv_ref[...]).astype(BF16)
            h1 = (xr * gmix_ref[...]).astype(BF16)
            hkv_ref[...] = hkv
            h1_ref[...] = h1
            kpre = _nn(hkv, wk_ref[...])
            qpre = _nn(h1, wq_ref[...])
            kpre_ref[...] = kpre
            qpre_ref[...] = qpre
            v_ref[...] = _nn(hkv, wv_ref[...]).astype(BF16)
            rk = lax.rsqrt(_seg_sum(kpre * kpre) * (1.0 / HEAD) + EPS)
            k_ref[...] = (kpre * rk * gk_ref[...]).astype(BF16)
            rq = lax.rsqrt(_seg_sum(qpre * qpre) * (1.0 / HEAD) + EPS)
            q_ref[...] = (qpre * rq * gq_ref[...]).astype(BF16)

    tile = pl.BlockSpec((PROJ_TS, D), lambda i: (jnp.maximum(i - lead, 0), 0))
    padded = pl.BlockSpec((PROJ_TS, D), lambda i: (i, 0))
    row = pl.BlockSpec((1, D), lambda i: (0, 0))
    wspec = pl.BlockSpec((D, D), lambda i: (0, 0))
    bf = jax.ShapeDtypeStruct((S, D), BF16)
    ff = jax.ShapeDtypeStruct((S, D), F32)
    bp = jax.ShapeDtypeStruct((PADK + S, D), BF16)
    return _call_hosting(
        body, ex, _grid_phases((lead + S // PROJ_TS,), 0.85), name="kvq_proj", grid=(lead + S // PROJ_TS,),
        in_specs=[tile, row, row, wspec, wspec, wspec, row, row],
        out_specs=[tile, tile, tile, tile, padded, padded, tile], out_shape=[bf, bf, ff, ff, bp, bp, bf],
        scratch_shapes=[], args=(x, g_kv, g_mix, wk, wv, wq, gk, gq), compiler_params=_params("arbitrary"),
    )


def _qkv_input_bwd(dq, qpre, gq, dkp, kpre, gk, dvp, wq, wk, wv, dres, x, g_mix, g_kv, ex=None):
    nt = S // PROJ_TS
    lead = PADK // PROJ_TS

    def head_bwd(dov, pv, hgv):
        r = lax.rsqrt(_seg_sum(pv * pv) * (1.0 / HEAD) + EPS)
        gd = dov * hgv
        dpre = r * gd - pv * (r * r * r) * (_seg_sum(gd * pv) * (1.0 / HEAD))
        return dpre.astype(BF16), _colsum8(dov * pv * r)

    def fold_heads(full):
        fold = full[:, 0:128]
        for blk in range(1, D // 128):
            fold = fold + full[:, blk * 128:(blk + 1) * 128]
        return fold + pltpu.roll(fold, HEAD, axis=1)

    def body(dq_ref, qpre_ref, gq_ref, dk_ref, kpre_ref, gk_ref, dv_ref, wq_ref, wk_ref, wv_ref, dres_ref, x_ref,
             gmix_ref, gkv_ref, dx_ref, dqpre_ref, dkpre_ref, dgmix_ref, dgkv_ref, dgq_ref, dgk_ref, accq, acck):
        i = pl.program_id(0)

        @pl.when(i == 0)
        def _():
            accq[...] = jnp.zeros(accq.shape, F32)
            acck[...] = jnp.zeros(acck.shape, F32)
            dgmix_ref[...] = jnp.zeros(dgmix_ref.shape, F32)
            dgkv_ref[...] = jnp.zeros(dgkv_ref.shape, F32)

        dqb, cq = head_bwd(dq_ref[...], qpre_ref[...], gq_ref[...])
        dkb, ck = head_bwd(dk_ref[...], kpre_ref[...], gk_ref[...])
        dqpre_ref[...] = dqb
        dkpre_ref[...] = dkb
        accq[...] += cq
        acck[...] += ck
        dh1 = _nt(dqb, wq_ref[...])
        dhkv = _nt(dkb, wk_ref[...]) + _nt(dv_ref[...].astype(BF16), wv_ref[...])
        xv = x_ref[...]
        r = _rstd(xv)
        dx_ref[...] = dres_ref[...] + _rms_bwd(dh1, xv, r, gmix_ref[...]) + _rms_bwd(dhkv, xv, r, gkv_ref[...])
        dgmix_ref[...] += _colsum8(dh1 * xv * r)
        dgkv_ref[...] += _colsum8(dhkv * xv * r)

        @pl.when(i == nt - 1)
        def _():
            dgq_ref[...] = fold_heads(accq[...])
            dgk_ref[...] = fold_heads(acck[...])

    tile = pl.BlockSpec((PROJ_TS, D), lambda i: (i, 0))
    behind = pl.BlockSpec((PROJ_TS, D), lambda i: (i + lead, 0))
    row = pl.BlockSpec((1, D), lambda i: (0, 0))
    wspec = lambda: pl.BlockSpec((D, D), lambda i: (0, 0), pipeline_mode=pl.Buffered(1))
    acc = pl.BlockSpec((8, D), lambda i: (0, 0))
    small = pl.BlockSpec((8, 128), lambda i: (0, 0))
    bf = jax.ShapeDtypeStruct((S, D), BF16)
    return _call_hosting(
        body, ex, _grid_phases((nt,)), name="qkv_input_bwd", grid=(nt,),
        in_specs=[tile, tile, row, behind, tile, row, behind, wspec(), wspec(), wspec(), tile, tile, row, row],
        out_specs=[tile, tile, tile, acc, acc, small, small],
        out_shape=[jax.ShapeDtypeStruct((S, D), F32), bf, bf, jax.ShapeDtypeStruct((8, D), F32),
                   jax.ShapeDtypeStruct((8, D), F32), jax.ShapeDtypeStruct((8, 128), F32),
                   jax.ShapeDtypeStruct((8, 128), F32)],
        scratch_shapes=[pltpu.VMEM((8, D), F32), pltpu.VMEM((8, D), F32)],
        args=(dq, qpre, gq, dkp, kpre, gk, dvp, wq, wk, wv, dres, x, g_mix, g_kv),
        compiler_params=_params("arbitrary"),
    )


def _toeplitz_from_table(table):
    far = jnp.broadcast_to(table[:, N_REL - 1:], (N_HEADS, PADK - MAX_REL + 1))
    near = table[:, N_REL - 2::-1]
    past = jnp.broadcast_to(table[:, 0:1], (N_HEADS, MAX_REL))
    wrap = jnp.broadcast_to(table[:, N_REL - 1:], (N_HEADS, QB - 1))
    return jnp.concatenate([far, near, past, wrap], axis=1).reshape(N_HEADS, 1, TOEP)


def _table_grad_from_toeplitz(dtp, seg):
    lo = PADK - MAX_REL + 1
    near = dtp[:, lo + N_REL - 3:lo - 1:-1]
    return jnp.concatenate([seg[:, 1:2], near, seg[:, 0:1]], axis=1)


def _bias_band(tp):
    def body(tp_ref, out_ref):
        bv = pltpu.roll(jnp.broadcast_to(tp_ref[0], (QB, TOEP)), 0, axis=1, stride=1, stride_axis=0)
        out_ref[0] = jnp.where(_band_mask(), bv[:, 0:KB], NEG_INF)

    return _call(
        body, name="bias_band", grid=(N_HEADS,),
        in_specs=[pl.BlockSpec((1, 1, TOEP), lambda h: (h, 0, 0))],
        out_specs=pl.BlockSpec((1, QB, KB), lambda h: (h, 0, 0)),
        out_shape=jax.ShapeDtypeStruct((N_HEADS, QB, KB), F32),
        compiler_params=_params("parallel"),
    )(tp)


def _bias_grad(dband, after):
    lo, hi = PADK - MAX_REL + 1, PADK + MAX_REL

    def body(db_ref, _, dtp_ref, seg_ref):
        bv = jnp.concatenate([db_ref[0], jnp.zeros((QB, TOEP - KB), F32)], axis=1)
        row = lax.broadcasted_iota(jnp.int32, (QB, TOEP), 0)
        k = 1
        while k < QB:
            bv = jnp.where((row & k) != 0, pltpu.roll(bv, TOEP - k, axis=1), bv)
            k *= 2
        col = jnp.sum(bv, axis=0, keepdims=True)
        dtp_ref[0] = col
        u = lax.broadcasted_iota(jnp.int32, (1, TOEP), 1)
        far = jnp.sum(jnp.where((u < lo) | (u > hi + MAX_REL), col, 0.0))
        past = jnp.sum(jnp.where((u >= hi) & (u <= hi + MAX_REL), col, 0.0))
        lane = lax.broadcasted_iota(jnp.int32, (1, 128), 1)
        seg_ref[0] = jnp.where(lane == 0, far, jnp.where(lane == 1, past, 0.0))

    return _call(
        body, name="bias_grad", grid=(N_HEADS,),
        in_specs=[pl.BlockSpec((1, QB, KB), lambda h: (h, 0, 0)), ANY],
        out_specs=[pl.BlockSpec((1, 1, TOEP), lambda h: (h, 0, 0)), pl.BlockSpec((1, 1, 128), lambda h: (h, 0, 0))],
        out_shape=[jax.ShapeDtypeStruct((N_HEADS, 1, TOEP), F32), jax.ShapeDtypeStruct((N_HEADS, 1, 128), F32)],
        compiler_params=_params("parallel"),
    )(dband, after)


N_QB = S // QB
HEADS_PER_STEP = 4
ATT_LANES = HEADS_PER_STEP * HEAD
N_HG = D // ATT_LANES


def _band_mask():
    qc = lax.broadcasted_iota(jnp.int32, (QB, KB), 0) // CHUNK
    kc = lax.broadcasted_iota(jnp.int32, (QB, KB), 1) // CHUNK
    return (kc >= qc) & (kc <= qc + LEFT)


def _half_scale(hh, scale):
    lane = lax.broadcasted_iota(jnp.int32, (1, 128), 1)
    return jnp.where((lane < HEAD) == (hh == 0), scale, 0.0).astype(BF16)


def _probs(qh, kb, bias, first_key):
    sc = _nt(qh, kb) + bias
    if first_key is not None:
        sc = jnp.where(lax.broadcasted_iota(jnp.int32, (QB, KB), 1) >= first_key, sc, NEG_INF)
    e = jnp.exp(sc - jnp.max(sc, axis=-1, keepdims=True))
    return e * (1.0 / jnp.sum(e, axis=-1, keepdims=True))


def _by_padding(cb, compute):
    @pl.when(cb < PADK // QB)
    def _():
        compute(PADK - cb * QB)

    @pl.when(cb >= PADK // QB)
    def _():
        compute(None)


def _attn_fwd(q, kp, vp, bias, ex=None):
    def body(q_ref, k_ref, v_ref, b_ref, o_ref):
        cb = pl.program_id(1)
        band = pl.ds(pl.multiple_of(cb * QB, QB), KB)
        low = lax.broadcasted_iota(jnp.int32, (QB, 128), 1) < HEAD

        def compute(first_key):
            for pair in range(HEADS_PER_STEP // 2):
                lanes = pl.ds(pair * 128, 128)
                kb, vb, qv = k_ref[band, lanes], v_ref[band, lanes], q_ref[:, lanes]
                outs = []
                for hh in range(2):
                    pb = _probs(qv * _half_scale(hh, ATTN_SCALE), kb, b_ref[2 * pair + hh], first_key).astype(BF16)
                    outs.append(_nn(pb, vb))
                o_ref[:, lanes] = jnp.where(low, outs[0], outs[1]).astype(BF16)

        _by_padding(cb, compute)

    qspec = pl.BlockSpec((QB, ATT_LANES), lambda hg, cb: (cb, hg))
    kspec = pl.BlockSpec((PADK + S, ATT_LANES), lambda hg, cb: (0, hg))
    return _call_hosting(
        body, ex, _grid_phases((N_HG, N_QB), 0.85), name="attn_fwd", grid=(N_HG, N_QB),
        in_specs=[qspec, kspec, kspec, pl.BlockSpec((HEADS_PER_STEP, QB, KB), lambda hg, cb: (hg, 0, 0))],
        out_specs=[qspec], out_shape=[jax.ShapeDtypeStruct((S, D), BF16)], scratch_shapes=[],
        args=(q, kp, vp, bias), compiler_params=_params("arbitrary", "arbitrary"),
    )


def _attn_bwd(q, kp, vp, bias, do, ex=None):
    def body(q_ref, k_ref, v_ref, b_ref, do_ref, dq_ref, dk_ref, dv_ref, db_ref):
        cb = pl.program_id(1)

        @pl.when(cb == 0)
        def _():
            dk_ref[...] = jnp.zeros(dk_ref.shape, F32)
            dv_ref[...] = jnp.zeros(dv_ref.shape, F32)
            db_ref[...] = jnp.zeros(db_ref.shape, F32)

        band = pl.ds(pl.multiple_of(cb * QB, QB), KB)
        low = lax.broadcasted_iota(jnp.int32, (QB, 128), 1) < HEAD

        def compute(first_key):
            for pair in range(HEADS_PER_STEP // 2):
                lanes = pl.ds(pair * 128, 128)
                kb, vb = k_ref[band, lanes], v_ref[band, lanes]
                qv, dov = q_ref[:, lanes], do_ref[:, lanes]
                dq = jnp.zeros((QB, 128), F32)
                dkb = jnp.zeros((KB, 128), F32)
                dvb = jnp.zeros((KB, 128), F32)
                for hh in range(2):
                    sel = low if hh == 0 else jnp.logical_not(low)
                    doh = dov * _half_scale(hh, 1.0)
                    p = _probs(qv * _half_scale(hh, ATTN_SCALE), kb, b_ref[2 * pair + hh], first_key)
                    dp = _nt(doh, vb)
                    dvb = dvb + _tn(p.astype(BF16), doh)
                    ds = p * (dp - jnp.sum(dp * p, axis=-1, keepdims=True))
                    db_ref[2 * pair + hh] += ds
                    dsb = (ds * ATTN_SCALE).astype(BF16)
                    dq = dq + jnp.where(sel, _nn(dsb, kb), 0.0)
                    dkb = dkb + _tn(dsb, qv * _half_scale(hh, 1.0))
                dq_ref[:, lanes] = dq
                dk_ref[band, lanes] += dkb
                dv_ref[band, lanes] += dvb

        _by_padding(cb, compute)

    qspec = pl.BlockSpec((QB, ATT_LANES), lambda hg, cb: (cb, hg))
    kspec = pl.BlockSpec((PADK + S, ATT_LANES), lambda hg, cb: (0, hg))
    bspec = pl.BlockSpec((HEADS_PER_STEP, QB, KB), lambda hg, cb: (hg, 0, 0))
    kf = jax.ShapeDtypeStruct((PADK + S, D), F32)
    return _call_hosting(
        body, ex, _grid_phases((N_HG, N_QB)), name="attn_bwd", grid=(N_HG, N_QB),
        in_specs=[qspec, kspec, kspec, bspec, qspec],
        out_specs=[qspec, kspec, kspec, bspec],
        out_shape=[jax.ShapeDtypeStruct((S, D), F32), kf, kf, jax.ShapeDtypeStruct((N_HEADS, QB, KB), F32)],
        scratch_shapes=[], args=(q, kp, vp, bias, do), compiler_params=_params("arbitrary", "arbitrary"),
    )


def _adam_math(w, g, m, v):
    m = ADAM_B1 * m + (1.0 - ADAM_B1) * g
    v = ADAM_B2 * v + (1.0 - ADAM_B2) * (g * g)
    m_hat = m / (1.0 - ADAM_B1 ** ADAM_STEP)
    v_hat = v / (1.0 - ADAM_B2 ** ADAM_STEP)
    delta = -ADAM_LR * (m_hat / (jnp.sqrt(v_hat) + ADAM_EPS) + ADAM_WD * w)
    return delta, m, v


def _sum_parts(parts, name, after=None):
    count, r, c = parts.shape
    tr = r // 2 if r % 16 == 0 and r > 64 else r

    def body(p_ref, *rest):
        o_ref = rest[-1]
        acc = p_ref[0].astype(F32)
        for d in range(1, count):
            acc = acc + p_ref[d].astype(F32)
        o_ref[...] = acc

    return _call(
        body, name=name, grid=(r // tr,),
        in_specs=[pl.BlockSpec((count, tr, c), lambda i: (0, i, 0))] + ([ANY] if after is not None else []),
        out_specs=pl.BlockSpec((tr, c), lambda i: (i, 0)),
        out_shape=jax.ShapeDtypeStruct((r, c), F32), compiler_params=_params("parallel"),
    )(parts, *([after] if after is not None else []))


def _row_tile(r):
    for cand in (256, 176, 128, 64, 32, 16, 8):
        if r % cand == 0:
            return cand
    return r


def _adam_layer(w, g, m, v, layer, name, prev=None):
    nl, r, c = w.shape
    tr = _row_tile(r)

    def body(w_ref, g_ref, m_ref, v_ref, *rest):
        go_ref, d_ref, nm_ref, nv_ref = rest[-4:]
        gv = g_ref[...]
        delta, nm, nv = _adam_math(w_ref[0], gv, m_ref[0], v_ref[0])
        go_ref[0] = gv
        d_ref[0] = delta
        nm_ref[0] = nm
        nv_ref[0] = nv

    lspec = pl.BlockSpec((1, tr, c), lambda i: (layer, i, 0))
    sd = jax.ShapeDtypeStruct((nl, r, c), F32)
    extra = list(prev) if prev is not None else []
    return _call(
        body, name=name, grid=(r // tr,),
        in_specs=[lspec, pl.BlockSpec((tr, c), lambda i: (i, 0)), lspec, lspec] + [ANY] * len(extra),
        out_specs=[lspec] * 4, out_shape=[sd] * 4,
        input_output_aliases={4 + t: t for t in range(len(extra))},
        compiler_params=_params("parallel"),
    )(w, g, m, v, *extra)


def _adam(w, g, m, v, name):
    r, c = w.shape
    tr = _row_tile(r)

    def body(w_ref, g_ref, m_ref, v_ref, d_ref, nm_ref, nv_ref):
        delta, nm, nv = _adam_math(w_ref[...], g_ref[...], m_ref[...], v_ref[...])
        d_ref[...] = delta
        nm_ref[...] = nm
        nv_ref[...] = nv

    spec = pl.BlockSpec((tr, c), lambda i: (i, 0))
    sd = jax.ShapeDtypeStruct((r, c), F32)
    return _call(
        body, name=name, grid=(r // tr,), in_specs=[spec] * 4, out_specs=[spec] * 3, out_shape=[sd] * 3,
        compiler_params=_params("parallel"),
    )(w, g, m, v)


def _pad8(rows):
    return jnp.pad(rows, ((0, 8 - rows.shape[0]), (0, 0)))


def kernel(x, norm_mix_g, norm_ffn_g, pool_w, pool_b, pool_scale, kv_norm_g, w_k, w_v, k_norm_g, w_q, q_norm_g, rel_bias, w_o, w_gate, w_up, w_down, loss_target, m_norm_mix_g, m_norm_ffn_g, m_pool_w, m_pool_b, m_pool_scale, m_kv_norm_g, m_w_k, m_w_v, m_k_norm_g, m_w_q, m_q_norm_g, m_rel_bias, m_w_o, m_w_gate, m_w_up, m_w_down, v_norm_mix_g, v_norm_ffn_g, v_pool_w, v_pool_b, v_pool_scale, v_kv_norm_g, v_w_k, v_w_v, v_k_norm_g, v_w_q, v_q_norm_g, v_rel_bias, v_w_o, v_w_gate, v_w_up, v_w_down):
    assert x.shape == (1, S, D) and w_gate.shape == (2, D, F_SHARD) and w_k.shape == (D_SHARD, D)
    xin, target = x[0], loss_target[0]

    ffn_shards = [[w_gate[layer].T.astype(BF16), w_up[layer].T.astype(BF16), w_down[layer].astype(BF16)]
                  for layer in range(2)]
    att_shards = [w_k.astype(BF16), w_v.astype(BF16), w_q[0].astype(BF16), w_o[0].astype(BF16)]
    pool_shard = pool_w[0].astype(BF16).reshape(N_GROUPS * POOL_SHARD, GROUP)
    small = jnp.concatenate([pool_b[0].reshape(1, N_GROUPS * POOL_SHARD), pool_scale], axis=1)

    full0 = _run_exchange(_gather_exchange(ffn_shards[0] + [pool_shard, _pad8(small)]), "gather_layer0")
    ffn_w0 = [a.reshape(F, D) for a in full0[:3]]
    pw_f = full0[3].reshape(N_DEV, N_GROUPS, POOL_SHARD, GROUP).transpose(1, 0, 2, 3).reshape(N_GROUPS, GROUP, GROUP)
    small_f = full0[4][:, 0, :]
    pb_f = small_f[:, :N_GROUPS * POOL_SHARD].reshape(N_DEV, N_GROUPS, POOL_SHARD).transpose(1, 0, 2).reshape(1, D)
    ps_f = small_f[:, N_GROUPS * POOL_SHARD:].reshape(1, D)

    g_mix0, g_mix1 = norm_mix_g[0:1], norm_mix_g[1:2]
    g_ffn0, g_ffn1 = norm_ffn_g[0:1], norm_ffn_g[1:2]
    g_kv = kv_norm_g.reshape(1, D)
    gk_t = jnp.tile(k_norm_g.reshape(1, HEAD), (1, N_HEADS))
    gq_t = jnp.tile(q_norm_g.reshape(1, HEAD), (1, N_HEADS))

    x1, diff = _pool_fwd(xin, g_mix0, pw_f, pb_f, ps_f)
    (x2, hf0, gg0, uu0), full_att = _ffn_fwd(x1, g_ffn0, *ffn_w0, ex=_gather_exchange(att_shards))
    wk_f, wv_f, wq_f, wo_f = [a.reshape(D, D) for a in full_att]
    (hkv, h1, kpre, qpre, kp, vp, qq), full1_gate = _kvq_proj(x2, g_kv, g_mix1, wk_f, wv_f, wq_f, gk_t, gq_t,
                                                                ex=_gather_exchange(ffn_shards[1][:1]))
    bias = _bias_band(_toeplitz_from_table(rel_bias[0]))
    (att,), full1_rest = _attn_fwd(qq, kp, vp, bias, ex=_gather_exchange(ffn_shards[1][1:]))
    ffn_w1 = [a.reshape(F, D) for a in full1_gate + full1_rest]
    x3 = _mm(att, wo_f, "nn", F32, "attn_out", add=x2)
    (dx4, loss_rows, hf1, gg1, uu1), _ = _ffn_fwd(x3, g_ffn1, *ffn_w1, target=target)

    def blocks(dw):
        return dw.reshape(N_DEV, dw.shape[0] // N_DEV, dw.shape[1])

    (dgg1, duu1, dwg1, dwu1, dwd1), _ = _ffn_bwd_weights(dx4, hf1, gg1, uu1, ffn_w1[2], name="ffn_bwd1")
    parts1 = [blocks(dw) for dw in (dwg1, dwu1, dwd1)]
    (dx3, dg_ffn1), stage1 = _ffn_bwd_input(dx4, x3, g_ffn1, dgg1, duu1, ffn_w1[0], ffn_w1[1], "ffn_dx1",
                                            ex=_pair_exchange(parts1))
    both1 = _pair_add(parts1, stage1, "pair_add_ffn1", copies=2)
    flight1, chip1, land1, _ = _chip_exchange_start(both1[:3], both1[3:], dx3, "scatter_ffn1_start")
    datt = _mm(dx3, wo_f, "nt", BF16, "d_attn")
    dwo = _mm(att, dx3, "tn", BF16, "d_wo")
    (dq, dkp, dvp, dband), _ = _attn_bwd(qq, kp, vp, bias, datt)
    (dx2, dqpre, dkpre, dg_mix1, dg_kv, dgq, dgk), _ = _qkv_input_bwd(
        dq, qpre, gq_t, dkp, kpre, gk_t, dvp, wq_f, wk_f, wv_f, dx3, x2, g_mix1, g_kv)
    dwq = _mm(h1, dqpre, "tn", BF16, "d_wq")
    dwk = _mm(hkv, dkpre, "tn", BF16, "d_wk")
    dwv = _mm(hkv, dvp, "tn", BF16, "d_wv", skip_rows=PADK)
    parts_att = [blocks(dw) for dw in (dwk, dwv, dwq, dwo)]
    (dgg0, duu0, dwg0, dwu0, dwd0), stage_att = _ffn_bwd_weights(
        dx2, hf0, gg0, uu0, ffn_w0[2], name="ffn_bwd0", ex=_pair_exchange(parts_att))
    both_att = _pair_add(parts_att, stage_att, "pair_add_att", copies=2)
    flight_att, chip_att, land_att, _ = _chip_exchange_start(both_att[:4], both_att[4:], dx2, "scatter_att_start")
    parts0 = [blocks(dw) for dw in (dwg0, dwu0, dwd0)]
    (dx1, dg_ffn0), stage0 = _ffn_bwd_input(dx2, x1, g_ffn0, dgg0, duu0, ffn_w0[0], ffn_w0[1], "ffn_dx0",
                                            ex=_pair_exchange(parts0))
    both = _pair_add(parts0, stage0, "pair_add_ffn0", copies=2)
    flight, chip0, land0, token = _chip_exchange_start(both[:3], both[3:], dx1, "scatter_ffn0_start")
    (grad_x, dpw, db_rows, ds_rows, dg_mix0), _ = _pool_bwd(dx1, xin, diff, g_mix0 + token[0:1, 0:1], pw_f, pb_f, ps_f)
    dtp, seg = _bias_grad(dband, token)

    weights = dict(norm_mix_g=norm_mix_g, norm_ffn_g=norm_ffn_g, pool_w=pool_w, pool_b=pool_b,
                   pool_scale=pool_scale, kv_norm_g=kv_norm_g, w_k=w_k, w_v=w_v, k_norm_g=k_norm_g, w_q=w_q,
                   q_norm_g=q_norm_g, rel_bias=rel_bias, w_o=w_o, w_gate=w_gate, w_up=w_up, w_down=w_down)
    mom1 = dict(norm_mix_g=m_norm_mix_g, norm_ffn_g=m_norm_ffn_g, pool_w=m_pool_w, pool_b=m_pool_b,
                pool_scale=m_pool_scale, kv_norm_g=m_kv_norm_g, w_k=m_w_k, w_v=m_w_v, k_norm_g=m_k_norm_g,
                w_q=m_w_q, q_norm_g=m_q_norm_g, rel_bias=m_rel_bias, w_o=m_w_o, w_gate=m_w_gate, w_up=m_w_up,
                w_down=m_w_down)
    mom2 = dict(norm_mix_g=v_norm_mix_g, norm_ffn_g=v_norm_ffn_g, pool_w=v_pool_w, pool_b=v_pool_b,
                pool_scale=v_pool_scale, kv_norm_g=v_kv_norm_g, w_k=v_w_k, w_v=v_w_v, k_norm_g=v_k_norm_g,
                w_q=v_w_q, q_norm_g=v_q_norm_g, rel_bias=v_rel_bias, w_o=v_w_o, w_gate=v_w_gate, w_up=v_w_up,
                w_down=v_w_down)
    names = list(weights)
    grads, deltas, new_m, new_v = {}, {}, {}, {}

    def summed(arrs, tag, after):
        return [_sum_parts(a, "sum_parts_%s%d" % (tag, k), after) for k, a in enumerate(arrs)]

    def adam_flat(nm):
        shape = weights[nm].shape
        flat = lambda a: a.reshape(-1, shape[-1])
        dl, m1, m2 = _adam(flat(weights[nm]), flat(grads[nm]), flat(mom1[nm]), flat(mom2[nm]), "adam_" + nm)
        deltas[nm], new_m[nm], new_v[nm] = dl.reshape(shape), m1.reshape(shape), m2.reshape(shape)

    ffn_names = ("w_gate", "w_up", "w_down")
    def ffn_view(nm, a):
        return a if nm == "w_down" else a.transpose(0, 2, 1)

    recv1 = _chip_exchange_wait(flight1, chip1, land1, token, "scatter_ffn1_wait")
    recv_att = _chip_exchange_wait(flight_att, chip_att, land_att, token, "scatter_att_wait")
    sums1 = summed(recv1, "ffn1_", token)
    layer1 = {nm: _adam_layer(ffn_view(nm, weights[nm]), g, ffn_view(nm, mom1[nm]), ffn_view(nm, mom2[nm]), 1,
                              "adam1_" + nm)
              for nm, g in zip(ffn_names, sums1)}
    g_wk, g_wv, g_wq, g_wo = summed(recv_att, "att_", token)
    grads.update(w_k=g_wk, w_v=g_wv, w_q=g_wq[None], w_o=g_wo[None])
    for nm in ("w_k", "w_v", "w_q", "w_o"):
        adam_flat(nm)

    dpw_blocks = dpw.reshape(N_GROUPS, N_DEV, POOL_SHARD, GROUP).transpose(1, 0, 2, 3)
    dpw_blocks = dpw_blocks.reshape(N_DEV, N_GROUPS * POOL_SHARD, GROUP).astype(BF16)
    recv_pool = _run_exchange(_scatter_exchange([dpw_blocks]), "scatter_pool", after=deltas["w_o"])
    misc = jnp.concatenate([dgk[0:1, 0:HEAD], dgq[0:1, 0:HEAD], loss_rows[0:1, 0:1],
                            seg[:, 0, 0].reshape(1, N_HEADS), seg[:, 0, 1].reshape(1, N_HEADS)], axis=1)
    misc = jnp.pad(misc, ((0, 0), (0, D - misc.shape[1])))
    vec_rows = jnp.concatenate([dg_mix0[0:1], dg_mix1[0:1], dg_ffn0[0:1], dg_ffn1[0:1], dg_kv[0:1],
                                db_rows[0:1], ds_rows[0:1], misc], axis=0)
    pack = jnp.concatenate([vec_rows, dtp.reshape(N_HEADS, TOEP)], axis=0)
    tot = _all_reduce_small(pack, "reduce_small", after=recv_pool[0])

    loss = tot[7, 2 * HEAD]
    seg_tot = jnp.stack([tot[7, 2 * HEAD + 1:2 * HEAD + 1 + N_HEADS],
                         tot[7, 2 * HEAD + 1 + N_HEADS:2 * HEAD + 1 + 2 * N_HEADS]], axis=1)
    me = 4 * lax.axis_index("x") + 2 * lax.axis_index("y") + lax.axis_index("c")
    g_pool_b = lax.dynamic_slice_in_dim(tot[5].reshape(N_GROUPS, GROUP), me * POOL_SHARD, POOL_SHARD, axis=1)
    grads.update(
        norm_mix_g=tot[0:2], norm_ffn_g=tot[2:4], kv_norm_g=tot[4], k_norm_g=tot[7, 0:HEAD],
        q_norm_g=tot[7, HEAD:2 * HEAD].reshape(1, HEAD),
        rel_bias=_table_grad_from_toeplitz(tot[8:8 + N_HEADS], seg_tot).reshape(1, N_HEADS, N_REL),
        pool_b=g_pool_b.reshape(1, N_GROUPS, POOL_SHARD),
        pool_scale=lax.dynamic_slice_in_dim(tot[6:7], me * D_SHARD, D_SHARD, axis=1),
        pool_w=summed(recv_pool, "pool_", None)[0].reshape(1, N_GROUPS, POOL_SHARD, GROUP))
    adam_flat("pool_w")
    small_names = [nm for nm in names if nm not in ffn_names + ("w_k", "w_v", "w_q", "w_o", "pool_w")]

    def pack_small(tree):
        cols = []
        for nm in small_names:
            flat = tree[nm].reshape(-1)
            cols.append(jnp.pad(flat, (0, -flat.shape[0] % 1024)))
        return jnp.concatenate(cols).reshape(-1, 128)

    dl, m1, m2 = _adam(pack_small(weights), pack_small(grads), pack_small(mom1), pack_small(mom2), "adam_small")

    def unpack_small(packed, out):
        flat, off = packed.reshape(-1), 0
        for nm in small_names:
            size = weights[nm].size
            out[nm] = flat[off:off + size].reshape(weights[nm].shape)
            off += size + (-size % 1024)

    unpack_small(dl, deltas)
    unpack_small(m1, new_m)
    unpack_small(m2, new_v)

    recv0 = _chip_exchange_wait(flight, chip0, land0, dl, "scatter_ffn0_wait")
    sums0 = summed(recv0, "ffn0_", None)
    for nm, g in zip(ffn_names, sums0):
        done = _adam_layer(ffn_view(nm, weights[nm]), g, ffn_view(nm, mom1[nm]), ffn_view(nm, mom2[nm]), 0,
                           "adam0_" + nm, prev=layer1[nm])
        grads[nm], deltas[nm], new_m[nm], new_v[nm] = [ffn_view(nm, a) for a in done]

    return (loss, grad_x[None], *[grads[nm] for nm in names], *[deltas[nm] for nm in names],
            *[new_m[nm] for nm in names], *[new_v[nm] for nm in names])
```

```python
import functools

import jax
import jax.numpy as jnp
from jax import lax
from jax.experimental import pallas as pl
from jax.experimental.pallas import tpu as pltpu

F32 = jnp.float32
BF16 = jnp.bfloat16
MESH_ID = pl.DeviceIdType.MESH

N_DEV = 8
S = 2048
D = 1024
F = 2816
F_SHARD = F // N_DEV
D_SHARD = D // N_DEV
N_GROUPS = 4
GROUP = D // N_GROUPS
POOL_SHARD = GROUP // N_DEV
MAX_WIN = 16
HEAD = 64
N_HEADS = D // HEAD
CHUNK = 64
LEFT = 8
QB = 4 * CHUNK
KB = QB + LEFT * CHUNK
PADK = LEFT * CHUNK
TOEP = 1024
N_REL = 257
MAX_REL = 128
EPS = 1e-6
NEG_INF = -1e30
ATTN_SCALE = HEAD ** -0.5

ADAM_LR = 0.001
ADAM_B1 = 0.9
ADAM_B2 = 0.999
ADAM_EPS = 1e-08
ADAM_WD = 0.01
ADAM_STEP = 10

VMEM_LIMIT = 52 * 1024 * 1024

ANY = pl.BlockSpec(memory_space=pl.ANY)
VMEM = pl.BlockSpec(memory_space=pltpu.VMEM)


def _call(body, **kw):
    return pl.pallas_call(body, **kw)


def _params(*sem):
    return pltpu.CompilerParams(dimension_semantics=sem, vmem_limit_bytes=VMEM_LIMIT)


def _dot(a, b, dims):
    return lax.dot_general(a, b, (dims, ((), ())), preferred_element_type=F32)


def _nn(a, b):
    return _dot(a, b, ((1,), (0,)))


def _nt(a, b):
    return _dot(a, b, ((1,), (1,)))


def _tn(a, b):
    return _dot(a, b, ((0,), (0,)))


def _rstd(x):
    return lax.rsqrt(jnp.mean(x * x, axis=-1, keepdims=True) + EPS)


def _rms_bwd(dh, x, r, g):
    gd = dh * g
    return r * gd - x * (r * r * r) * jnp.mean(gd * x, axis=-1, keepdims=True)


def _colsum8(v):
    return jnp.broadcast_to(jnp.sum(v, axis=0, keepdims=True), (8, v.shape[1]))


def _seg_sum(v):
    r = lax.broadcasted_iota(jnp.int32, (128, 128), 0) // HEAD
    c = lax.broadcasted_iota(jnp.int32, (128, 128), 1) // HEAD
    ones = jnp.where(r == c, 1.0, 0.0).astype(BF16)
    out = []
    for blk in range(v.shape[1] // 128):
        part = v[:, blk * 128:(blk + 1) * 128]
        hi = part.astype(BF16)
        rest = part - hi.astype(F32)
        mid = rest.astype(BF16)
        lo = (rest - mid.astype(F32)).astype(BF16)
        out.append(_nn(hi, ones) + _nn(mid, ones) + _nn(lo, ones))
    return jnp.concatenate(out, axis=1)


def _place():
    return lax.axis_index("x"), lax.axis_index("y"), lax.axis_index("c")


class _Exchange:
    def __init__(self, ins, out_shape, sems, start, mid, finish):
        self.ins, self.out_shape, self.sems = list(ins), list(out_shape), list(sems)
        self.start, self.mid, self.finish = start, mid, finish


def _gather_exchange(shards):
    n = len(shards)

    def tools(ins, outs, sems):
        send_sems, recv_sems, local_sems = sems
        x, y, c = _place()
        me, sibling = (x, y, c), (x, y, 1 - c)
        chips = [(1 - x, y), (x, 1 - y), (1 - x, 1 - y)]

        def slot(k, px, py, pc):
            return outs[k].at[4 * px + 2 * py + pc]

        def copy(k, s, block, to, own=False):
            return pltpu.make_async_remote_copy(
                src_ref=ins[k] if own else slot(k, *block), dst_ref=slot(k, *block),
                send_sem=send_sems.at[k, s], recv_sem=recv_sems.at[k, s], device_id=to, device_id_type=MESH_ID)

        def mine():
            return [pltpu.make_async_copy(ins[k], slot(k, *me), local_sems.at[k]) for k in range(n)]

        def first():
            out = []
            for k in range(n):
                out.append(copy(k, 0, me, sibling, own=True))
                out += [copy(k, 1 + j, me, (*chip, c), own=True) for j, chip in enumerate(chips)]
            return out

        def landed(j):
            return [copy(k, 1 + j, (*chips[j], c), me) for k in range(n)]

        def passed(j):
            return [copy(k, 4 + j, (*chips[j], c), sibling) for k in range(n)]

        def from_sibling():
            out = []
            for k in range(n):
                out.append(copy(k, 0, sibling, me))
                out += [copy(k, 4 + j, (*chip, 1 - c), me) for j, chip in enumerate(chips)]
            return out

        return mine, first, landed, passed, from_sibling

    def start(ins, outs, sems):
        mine, first, _, _, _ = tools(ins, outs, sems)
        for cp in mine() + first():
            cp.start()

    def mid(ins, outs, sems):
        _, _, landed, passed, _ = tools(ins, outs, sems)
        for j in range(3):
            for arrived, onward in zip(landed(j), passed(j)):
                arrived.wait_recv()
                onward.start()

    def finish(ins, outs, sems):
        mine, first, _, passed, from_sibling = tools(ins, outs, sems)
        for cp in from_sibling():
            cp.wait_recv()
        for cp in first() + passed(0) + passed(1) + passed(2):
            cp.wait_send()
        for cp in mine():
            cp.wait()

    return _Exchange(
        shards, [jax.ShapeDtypeStruct((N_DEV,) + a.shape, a.dtype) for a in shards],
        [pltpu.SemaphoreType.DMA((n, 7)), pltpu.SemaphoreType.DMA((n, 7)), pltpu.SemaphoreType.DMA((n,))],
        start, mid, finish)


def _peer(x, y, c, m):
    px = 1 - x if m & 4 else x
    py = 1 - y if m & 2 else y
    pc = 1 - c if m & 1 else c
    return px, py, pc


def _scatter_exchange(parts):
    n = len(parts)

    def tools(ins, outs, sems):
        send_sems, recv_sems, local_sems = sems
        x, y, c = _place()
        me = 4 * x + 2 * y + c
        def mine():
            return [pltpu.make_async_copy(ins[k].at[me], outs[k].at[me], local_sems.at[k]) for k in range(n)]

        def remote(dst_is_mine):
            out = []
            for m in range(1, N_DEV):
                px, py, pc = _peer(x, y, c, m)
                peer = 4 * px + 2 * py + pc
                for k in range(n):
                    out.append(pltpu.make_async_remote_copy(
                        src_ref=ins[k].at[peer], dst_ref=outs[k].at[me if dst_is_mine else peer],
                        send_sem=send_sems.at[k, m - 1], recv_sem=recv_sems.at[k, m - 1],
                        device_id=(px, py, pc), device_id_type=MESH_ID))
            return out

        return mine, remote

    def start(ins, outs, sems):
        mine, remote = tools(ins, outs, sems)
        for cp in mine() + remote(True):
            cp.start()

    def mid(ins, outs, sems):
        pass

    def finish(ins, outs, sems):
        mine, remote = tools(ins, outs, sems)
        for cp in remote(False):
            cp.wait_recv()
        for cp in remote(True):
            cp.wait_send()
        for cp in mine():
            cp.wait()

    return _Exchange(
        parts, [jax.ShapeDtypeStruct(a.shape, a.dtype) for a in parts],
        [pltpu.SemaphoreType.DMA((n, 7)), pltpu.SemaphoreType.DMA((n, 7)), pltpu.SemaphoreType.DMA((n,))],
        start, mid, finish)


N_CHIPS = N_DEV // 2


def _pair_exchange(parts):
    n = len(parts)

    def copies(ins, outs, sems):
        send_sems, recv_sems = sems
        x, y, c = _place()
        return [pltpu.make_async_remote_copy(
            src_ref=ins[k].at[2 * q + 1 - c], dst_ref=outs[k].at[q], send_sem=send_sems.at[k, q],
            recv_sem=recv_sems.at[k, q], device_id=(x, y, 1 - c), device_id_type=MESH_ID)
            for k in range(n) for q in range(N_CHIPS)]

    def start(ins, outs, sems):
        for cp in copies(ins, outs, sems):
            cp.start()

    def mid(ins, outs, sems):
        pass

    def finish(ins, outs, sems):
        for cp in copies(ins, outs, sems):
            cp.wait_recv()
        for cp in copies(ins, outs, sems):
            cp.wait_send()

    return _Exchange(
        parts, [jax.ShapeDtypeStruct((N_CHIPS,) + a.shape[1:], a.dtype) for a in parts],
        [pltpu.SemaphoreType.DMA((n, N_CHIPS)), pltpu.SemaphoreType.DMA((n, N_CHIPS))], start, mid, finish)


def _pair_add(parts, stage, name, copies=1):
    n = len(parts)
    core = lax.axis_index("c").reshape(1)

    def body(core_ref, *refs):
        for k in range(n):
            mine, theirs = refs[k], refs[n + k]
            total = (mine[0, 0].astype(F32) + theirs[0].astype(F32)).astype(BF16)
            for rep in range(copies):
                refs[(2 + rep) * n + k][0] = total

    in_specs, out_specs = [], []
    for a in parts:
        _, r, cdim = a.shape
        in_specs.append(pl.BlockSpec((1, 1, r // 2, cdim), lambda q, i, core_ref: (q, core_ref[0], i, 0)))
    for a in parts:
        _, r, cdim = a.shape
        in_specs.append(pl.BlockSpec((1, r // 2, cdim), lambda q, i, core_ref: (q, i, 0)))
        out_specs.append(pl.BlockSpec((1, r // 2, cdim), lambda q, i, core_ref: (q, i, 0)))
    return list(_call(
        body, name=name,
        grid_spec=pltpu.PrefetchScalarGridSpec(num_scalar_prefetch=1, grid=(N_CHIPS, 2), in_specs=in_specs,
                                               out_specs=out_specs * copies),
        out_shape=[jax.ShapeDtypeStruct(s.shape, BF16) for s in stage] * copies,
        compiler_params=_params("arbitrary", "arbitrary"),
    )(core, *[a.reshape((N_CHIPS, 2) + a.shape[1:]) for a in parts], *stage))


HBM = pl.BlockSpec(memory_space=pltpu.HBM)
SEMAPHORES = pl.BlockSpec(memory_space=pltpu.SEMAPHORE)


def _chip_copies(srcs, lands, send_sems, recv_sems, mine_is_dst):
    x, y, c = _place()
    me = 2 * x + y
    out = []
    for m in range(1, N_CHIPS):
        px, py, _ = _peer(x, y, c, 2 * m)
        peer = 2 * px + py
        for k in range(len(srcs)):
            pair = k * (N_CHIPS - 1) + m - 1
            out.append(pltpu.make_async_remote_copy(
                src_ref=srcs[k].at[peer], dst_ref=lands[k].at[me if mine_is_dst else peer],
                send_sem=send_sems[pair], recv_sem=recv_sems[pair],
                device_id=(px, py, c), device_id_type=MESH_ID))
    return out


def _chip_exchange_start(chip_parts, lands, after, name):
    n = len(chip_parts)
    pairs = n * (N_CHIPS - 1)

    def body(*refs):
        srcs, zones = refs[:n], refs[n:2 * n]
        sems = refs[2 * n + 1:2 * n + 1 + 2 * pairs]
        token = refs[-1]
        for cp in _chip_copies(srcs, zones, sems[:pairs], sems[pairs:], True):
            cp.start()
        token[...] = jnp.zeros(token.shape, F32)

    thru = [pltpu.HBM(a.shape, a.dtype) for a in chip_parts + lands]
    hbm = [pltpu.with_memory_space_constraint(a, pltpu.HBM) for a in chip_parts + lands]
    res = _call(
        body, name=name,
        out_shape=[pltpu.SemaphoreType.DMA(())] * (2 * pairs) + thru + [jax.ShapeDtypeStruct((8, 128), F32)],
        in_specs=[HBM] * (2 * n) + [ANY], out_specs=[SEMAPHORES] * (2 * pairs) + [HBM] * (2 * n) + [VMEM],
        input_output_aliases={i: 2 * pairs + i for i in range(2 * n)},
        compiler_params=pltpu.CompilerParams(has_side_effects=pltpu.SideEffectType.DATAFLOW_SIDE_EFFECTING),
    )(*hbm, after)
    sems, rest = list(res[:2 * pairs]), res[2 * pairs:]
    return sems, list(rest[:n]), list(rest[n:2 * n]), rest[-1]


def _chip_exchange_wait(sems, chip_parts, lands, after, name):
    n = len(chip_parts)
    pairs = n * (N_CHIPS - 1)

    def body(*refs):
        srcs, zones = refs[:n], refs[n:2 * n]
        sem_refs = refs[2 * n:2 * n + 2 * pairs]
        for cp in _chip_copies(srcs, zones, sem_refs[:pairs], sem_refs[pairs:], False):
            cp.wait_send()
            cp.wait_recv()

    thru = [pltpu.HBM(a.shape, a.dtype) for a in chip_parts + lands]
    res = _call(
        body, name=name, out_shape=thru,
        in_specs=[HBM] * (2 * n) + [SEMAPHORES] * (2 * pairs) + [ANY], out_specs=[HBM] * (2 * n),
        input_output_aliases={i: i for i in range(2 * n)},
        compiler_params=pltpu.CompilerParams(has_side_effects=pltpu.SideEffectType.DATAFLOW_SIDE_EFFECTING),
    )(*chip_parts, *lands, *sems, after)
    return list(res[n:])


def _run_exchange(ex, name, after=None):
    n_in, n_out = len(ex.ins), len(ex.out_shape)
    order = [] if after is None else [after]

    def body(*refs):
        ins, outs = refs[:n_in], refs[n_in + len(order):n_in + len(order) + n_out]
        sems = refs[n_in + len(order) + n_out:]
        ex.start(ins, outs, sems)
        ex.mid(ins, outs, sems)
        ex.finish(ins, outs, sems)

    return list(_call(body, name=name, out_shape=ex.out_shape, in_specs=[ANY] * (n_in + len(order)),
                      out_specs=[ANY] * n_out, scratch_shapes=ex.sems)(*ex.ins, *order))


def _call_hosting(body, ex, phases, *, in_specs, out_specs, out_shape, scratch_shapes, args, **kw):
    n_in, n_out, n_scr = len(in_specs), len(out_specs), len(scratch_shapes)
    if ex is None:
        res = _call(body, in_specs=in_specs, out_specs=out_specs, out_shape=out_shape,
                    scratch_shapes=scratch_shapes, **kw)(*args)
        return list(res), []
    n_xin, n_xout = len(ex.ins), len(ex.out_shape)

    def hosting(*refs):
        a, b = n_in, n_in + n_xin
        c, d = b + n_out, b + n_out + n_xout
        ins, xins, outs, xouts = refs[:a], refs[a:b], refs[b:c], refs[c:d]
        scr, sems = refs[d:d + n_scr], refs[d + n_scr:]
        first, mid, last = phases()

        @pl.when(first)
        def _():
            ex.start(xins, xouts, sems)

        body(*ins, *outs, *scr)

        @pl.when(mid)
        def _():
            ex.mid(xins, xouts, sems)

        @pl.when(last)
        def _():
            ex.finish(xins, xouts, sems)

    res = _call(hosting, in_specs=list(in_specs) + [ANY] * n_xin, out_specs=list(out_specs) + [ANY] * n_xout,
                out_shape=list(out_shape) + ex.out_shape, scratch_shapes=list(scratch_shapes) + ex.sems,
                **kw)(*args, *ex.ins)
    return list(res[:n_out]), list(res[n_out:])


def _grid_phases(dims, mid_fraction=0.8):
    total = 1
    for d in dims:
        total *= d
    mid = min(max(int(total * mid_fraction), 1), total - 1)

    def phases():
        step = pl.program_id(0)
        for axis in range(1, len(dims)):
            step = step * dims[axis] + pl.program_id(axis)
        return step == 0, step == mid, step == total - 1
    return phases


def _all_reduce_small(pack, name, after):
    rows = pack.shape[0]

    def body(in_ref, _, out_ref, recv, send_sems, recv_sems):
        x, y, c = _place()
        me = 4 * x + 2 * y + c
        recv[me] = in_ref[...]
        sent = []
        for m in range(1, N_DEV):
            px, py, pc = _peer(x, y, c, m)
            cp = pltpu.make_async_remote_copy(
                src_ref=in_ref, dst_ref=recv.at[me], send_sem=send_sems.at[m - 1], recv_sem=recv_sems.at[m - 1],
                device_id=(px, py, pc), device_id_type=MESH_ID)
            cp.start()
            sent.append(cp)
        for m in range(1, N_DEV):
            px, py, pc = _peer(x, y, c, m)
            peer = 4 * px + 2 * py + pc
            pltpu.make_async_remote_copy(
                src_ref=in_ref, dst_ref=recv.at[peer], send_sem=send_sems.at[m - 1], recv_sem=recv_sems.at[m - 1],
                device_id=(px, py, pc), device_id_type=MESH_ID).wait_recv()
        acc = recv[0]
        for d in range(1, N_DEV):
            acc = acc + recv[d]
        out_ref[...] = acc
        for cp in sent:
            cp.wait_send()

    return _call(
        body, name=name, out_shape=jax.ShapeDtypeStruct(pack.shape, F32), in_specs=[VMEM, ANY], out_specs=VMEM,
        scratch_shapes=[pltpu.VMEM((N_DEV, rows, pack.shape[1]), F32), pltpu.SemaphoreType.DMA((7,)),
                        pltpu.SemaphoreType.DMA((7,))],
        compiler_params=pltpu.CompilerParams(vmem_limit_bytes=VMEM_LIMIT),
    )(pack, after)


POOL_TS = 256


def _pool_counts(first_row, rows, win):
    t = first_row + lax.broadcasted_iota(jnp.int32, (rows, 1), 0)
    return jnp.minimum(t + 1, win).astype(F32)


def _pool_fwd(x, g, w, b, scale):
    nt = S // POOL_TS

    def body(x_ref, g_ref, w_ref, b_ref, s_ref, out_ref, diff_ref, ext):
        i = pl.program_id(0)

        @pl.when(i == 0)
        def _():
            ext[0:MAX_WIN, :] = jnp.zeros((MAX_WIN, D), F32)

        @pl.when(i > 0)
        def _():
            ext[0:MAX_WIN, :] = ext[POOL_TS:POOL_TS + MAX_WIN, :]

        xv = x_ref[...]
        h = xv * _rstd(xv) * g_ref[...]
        ext[MAX_WIN:, :] = h
        for gi in range(N_GROUPS):
            win = 2 << gi
            cols = slice(gi * GROUP, (gi + 1) * GROUP)
            sm = ext[:, cols]
            k = 1
            while k < win:
                sm = sm + pltpu.roll(sm, k, axis=0)
                k *= 2
            pooled = sm[MAX_WIN:, :] / _pool_counts(i * POOL_TS, POOL_TS, win)
            diff = (pooled - h[:, cols]).astype(BF16)
            yv = (_nn(diff, w_ref[gi]) + b_ref[:, cols]) * s_ref[:, cols]
            out_ref[:, cols] = xv[:, cols] + yv
            diff_ref[:, cols] = diff

    row = pl.BlockSpec((1, D), lambda i: (0, 0))
    tile = pl.BlockSpec((POOL_TS, D), lambda i: (i, 0))
    return _call(
        body, name="pool_fwd", grid=(nt,),
        in_specs=[tile, row, pl.BlockSpec((N_GROUPS, GROUP, GROUP), lambda i: (0, 0, 0)), row, row],
        out_specs=[tile, tile],
        out_shape=[jax.ShapeDtypeStruct((S, D), F32), jax.ShapeDtypeStruct((S, D), BF16)],
        scratch_shapes=[pltpu.VMEM((POOL_TS + MAX_WIN, D), F32)],
        compiler_params=_params("arbitrary"),
    )(x, g, w, b, scale)


def _pool_bwd(dy, x, diff, g, w, b, scale, ex=None):
    nt = S // POOL_TS

    def body(dy_ref, x_ref, diff_ref, g_ref, w_ref, b_ref, s_ref, gx_ref, dw_ref, db_ref, ds_ref, dg_ref, ext, dh):
        i = pl.program_id(0)
        first_row = (nt - 1 - i) * POOL_TS

        @pl.when(i == 0)
        def _():
            ext[POOL_TS:, :] = jnp.zeros((MAX_WIN, D), F32)
            dw_ref[...] = jnp.zeros(dw_ref.shape, F32)
            db_ref[...] = jnp.zeros(db_ref.shape, F32)
            ds_ref[...] = jnp.zeros(ds_ref.shape, F32)
            dg_ref[...] = jnp.zeros(dg_ref.shape, F32)

        @pl.when(i > 0)
        def _():
            ext[POOL_TS:, :] = ext[0:MAX_WIN, :]

        dyv = dy_ref[...]
        for gi in range(N_GROUPS):
            win = 2 << gi
            cols = slice(gi * GROUP, (gi + 1) * GROUP)
            dfb = diff_ref[:, cols]
            z = _nn(dfb, w_ref[gi]) + b_ref[:, cols]
            dyg = dyv[:, cols]
            ds_ref[:, cols] += _colsum8(dyg * z)
            dz = dyg * s_ref[:, cols]
            db_ref[:, cols] += _colsum8(dz)
            dzb = dz.astype(BF16)
            dw_ref[gi] += _tn(dfb, dzb)
            ddiff = _nt(dzb, w_ref[gi])
            ext[0:POOL_TS, cols] = ddiff / _pool_counts(first_row, POOL_TS, win)
            sm = ext[:, cols]
            k = 1
            while k < win:
                sm = sm + pltpu.roll(sm, POOL_TS + MAX_WIN - k, axis=0)
                k *= 2
            dh[:, cols] = sm[0:POOL_TS, :] - ddiff
        xv = x_ref[...]
        r = _rstd(xv)
        gv = g_ref[...]
        dhv = dh[...]
        dg_ref[...] += _colsum8(dhv * xv * r)
        gx_ref[...] = dyv + _rms_bwd(dhv, xv, r, gv)

    row = pl.BlockSpec((1, D), lambda i: (0, 0))
    tile = pl.BlockSpec((POOL_TS, D), lambda i: (nt - 1 - i, 0))
    acc = pl.BlockSpec((8, D), lambda i: (0, 0))
    wspec = pl.BlockSpec((N_GROUPS, GROUP, GROUP), lambda i: (0, 0, 0))
    return _call_hosting(
        body, ex, _grid_phases((nt,)), name="pool_bwd", grid=(nt,),
        in_specs=[tile, tile, tile, row, wspec, row, row],
        out_specs=[tile, wspec, acc, acc, acc],
        out_shape=[jax.ShapeDtypeStruct((S, D), F32), jax.ShapeDtypeStruct((N_GROUPS, GROUP, GROUP), F32),
                   jax.ShapeDtypeStruct((8, D), F32), jax.ShapeDtypeStruct((8, D), F32),
                   jax.ShapeDtypeStruct((8, D), F32)],
        scratch_shapes=[pltpu.VMEM((POOL_TS + MAX_WIN, D), F32), pltpu.VMEM((POOL_TS, D), F32)],
        args=(dy, x, diff, g, w, b, scale), compiler_params=_params("arbitrary"),
    )


FFN_TS = min(S, 1024)
FFN_TF = 256


def _ffn_fwd(x, g, wg_t, wu_t, wd, target=None, ex=None):
    ni, nj = S // FFN_TS, F // FFN_TF
    with_loss = target is not None

    def body(*refs):
        if with_loss:
            x_ref, g_ref, wg_ref, wu_ref, wd_ref, t_ref, out_ref, loss_ref, h_ref, gg_ref, uu_ref, hs, acc = refs
        else:
            x_ref, g_ref, wg_ref, wu_ref, wd_ref, out_ref, h_ref, gg_ref, uu_ref, hs, acc = refs
        i, j = pl.program_id(0), pl.program_id(1)

        @pl.when(j == 0)
        def _():
            xv = x_ref[...]
            hb = (xv * _rstd(xv) * g_ref[...]).astype(BF16)
            hs[...] = hb
            h_ref[...] = hb
            acc[...] = jnp.zeros(acc.shape, F32)

        hb = hs[...]
        gg = _nt(hb, wg_ref[...])
        uu = _nt(hb, wu_ref[...])
        gg_ref[...] = gg
        uu_ref[...] = uu
        a = (gg * jax.nn.sigmoid(gg) * uu).astype(BF16)
        acc[...] += _nn(a, wd_ref[...])

        @pl.when(j == nj - 1)
        def _():
            yv = x_ref[...] + acc[...]
            if with_loss:
                err = yv - t_ref[...]
                out_ref[...] = err * (1.0 / D)
                part = jnp.sum(err * err) * (0.5 / D)

                @pl.when(i == 0)
                def _():
                    loss_ref[...] = jnp.zeros(loss_ref.shape, F32)

                loss_ref[...] += jnp.broadcast_to(part, loss_ref.shape)
            else:
                out_ref[...] = yv

    xt = pl.BlockSpec((FFN_TS, D), lambda i, j: (i, 0))
    row = pl.BlockSpec((1, D), lambda i, j: (0, 0))
    wt = pl.BlockSpec((FFN_TF, D), lambda i, j: (j, 0))
    gt = pl.BlockSpec((FFN_TS, FFN_TF), lambda i, j: (i, j))
    in_specs = [xt, row, wt, wt, wt] + ([xt] if with_loss else [])
    out_specs = [xt] + ([pl.BlockSpec((8, 128), lambda i, j: (0, 0))] if with_loss else []) + [xt, gt, gt]
    out_shape = ([jax.ShapeDtypeStruct((S, D), F32)] + ([jax.ShapeDtypeStruct((8, 128), F32)] if with_loss else [])
                 + [jax.ShapeDtypeStruct((S, D), BF16), jax.ShapeDtypeStruct((S, F), F32),
                    jax.ShapeDtypeStruct((S, F), F32)])
    args = (x, g, wg_t, wu_t, wd) + ((target,) if with_loss else ())
    return _call_hosting(
        body, ex, _grid_phases((ni, nj), 0.8), name="ffn_fwd_loss" if with_loss else "ffn_fwd", grid=(ni, nj),
        in_specs=in_specs, out_specs=out_specs, out_shape=out_shape,
        scratch_shapes=[pltpu.VMEM((FFN_TS, D), BF16), pltpu.VMEM((FFN_TS, D), F32)], args=args,
        compiler_params=_params("arbitrary", "arbitrary"),
    )


def _ffn_bwd_weights(dout, h, gg, uu, wd, name, ex=None):
    nj = F // FFN_TF

    def body(do_ref, h_ref, gg_ref, uu_ref, wd_ref, dg_ref, du_ref, dwg_ref, dwu_ref, dwd_ref, dob):
        @pl.when(pl.program_id(0) == 0)
        def _():
            dob[...] = do_ref[...].astype(BF16)

        hb, dov = h_ref[...], dob[...]
        gv, uv = gg_ref[...], uu_ref[...]
        da = _nt(dov, wd_ref[...])
        sg = jax.nn.sigmoid(gv)
        sl = gv * sg
        ab = (sl * uv).astype(BF16)
        dub = (da * sl).astype(BF16)
        dgb = (da * uv * (sg * (1.0 + gv * (1.0 - sg)))).astype(BF16)
        dg_ref[...] = dgb
        du_ref[...] = dub
        dwg_ref[...] = _tn(dgb, hb).astype(BF16)
        dwu_ref[...] = _tn(dub, hb).astype(BF16)
        dwd_ref[...] = _tn(ab, dov).astype(BF16)

    once = pl.Buffered(1)
    whole = lambda: pl.BlockSpec((S, D), lambda j, _: (0, 0), pipeline_mode=once)
    wt = pl.BlockSpec((FFN_TF, D), lambda j, _: (j, 0))
    gt = pl.BlockSpec((S, FFN_TF), lambda j, _: (0, j))
    return _call_hosting(
        body, ex, _grid_phases((nj, 1)), name=name, grid=(nj, 1),
        in_specs=[whole(), whole(), gt, gt, wt], out_specs=[gt, gt, wt, wt, wt],
        out_shape=[jax.ShapeDtypeStruct((S, F), BF16)] * 2 + [jax.ShapeDtypeStruct((F, D), BF16)] * 3,
        scratch_shapes=[pltpu.VMEM((S, D), BF16)], args=(dout, h, gg, uu, wd),
        compiler_params=_params("arbitrary", "arbitrary"),
    )


BWD_TS = 512


def _ffn_bwd_input(dres, x, g, dg, du, wg_t, wu_t, name, ex=None):
    nt = S // BWD_TS

    def body(dres_ref, x_ref, g_ref, dg_ref, du_ref, wg_ref, wu_ref, dx_ref, dgam_ref):
        dh = _nn(dg_ref[...], wg_ref[...]) + _nn(du_ref[...], wu_ref[...])
        xv = x_ref[...]
        r = _rstd(xv)
        dx_ref[...] = dres_ref[...] + _rms_bwd(dh, xv, r, g_ref[...])

        @pl.when(pl.program_id(0) == 0)
        def _():
            dgam_ref[...] = jnp.zeros(dgam_ref.shape, F32)

        dgam_ref[...] += _colsum8(dh * xv * r)

    tile = pl.BlockSpec((BWD_TS, D), lambda i: (i, 0))
    ftile = pl.BlockSpec((BWD_TS, F), lambda i: (i, 0))
    wspec = lambda: pl.BlockSpec((F, D), lambda i: (0, 0), pipeline_mode=pl.Buffered(1))
    return _call_hosting(
        body, ex, _grid_phases((nt,)), name=name, grid=(nt,),
        in_specs=[tile, tile, pl.BlockSpec((1, D), lambda i: (0, 0)), ftile, ftile, wspec(), wspec()],
        out_specs=[tile, pl.BlockSpec((8, D), lambda i: (0, 0))],
        out_shape=[jax.ShapeDtypeStruct((S, D), F32), jax.ShapeDtypeStruct((8, D), F32)],
        scratch_shapes=[], args=(dres, x, g, dg, du, wg_t, wu_t), compiler_params=_params("arbitrary"),
    )


def _mm(a, b, mode, out_dtype, name, add=None, skip_rows=0):
    if mode == "nn":
        (m, kd), n = (a.shape[0] - skip_rows, a.shape[1]), b.shape[1]
    elif mode == "nt":
        (m, kd), n = (a.shape[0] - skip_rows, a.shape[1]), b.shape[0]
    else:
        (kd, m), n = a.shape, b.shape[1]
    tm, tn, tk = min(m, 1024), min(n, 1024), min(kd, 1024)
    if skip_rows and mode == "tn":
        tk = min(tk, skip_rows)
    elif skip_rows:
        tm = min(tm, skip_rows)
    skip_a = skip_rows // tm if mode != "tn" else 0
    skip_b = skip_rows // tk if mode == "tn" else 0
    assert skip_rows == skip_a * tm + skip_b * tk
    nk = kd // tk
    dot = {"nn": _nn, "nt": _nt, "tn": _tn}[mode]

    def body(*refs):
        if add is None:
            a_ref, b_ref, o_ref, acc = refs
        else:
            a_ref, b_ref, add_ref, o_ref, acc = refs
        k = pl.program_id(2)

        @pl.when(k == 0)
        def _():
            acc[...] = jnp.zeros(acc.shape, F32)

        acc[...] += dot(a_ref[...].astype(BF16), b_ref[...].astype(BF16))

        @pl.when(k == nk - 1)
        def _():
            res = acc[...]
            if add is not None:
                res = res + add_ref[...]
            o_ref[...] = res.astype(out_dtype)

    if mode == "tn":
        a_spec = pl.BlockSpec((tk, tm), lambda i, j, k: (k, i))
        b_spec = pl.BlockSpec((tk, tn), lambda i, j, k: (k + skip_b, j))
    else:
        a_spec = pl.BlockSpec((tm, tk), lambda i, j, k: (i + skip_a, k))
        b_spec = (pl.BlockSpec((tk, tn), lambda i, j, k: (k, j)) if mode == "nn"
                  else pl.BlockSpec((tn, tk), lambda i, j, k: (j, k)))
    o_spec = pl.BlockSpec((tm, tn), lambda i, j, k: (i, j))
    in_specs = [a_spec, b_spec] + ([o_spec] if add is not None else [])
    args = (a, b) + ((add,) if add is not None else ())
    return _call(
        body, name=name, grid=(m // tm, n // tn, nk), in_specs=in_specs, out_specs=o_spec,
        out_shape=jax.ShapeDtypeStruct((m, n), out_dtype), scratch_shapes=[pltpu.VMEM((tm, tn), F32)],
        compiler_params=_params("parallel", "parallel", "arbitrary"),
    )(*args)


PROJ_TS = 256


def _kvq_proj(x, g_kv, g_mix, wk, wv, wq, gk, gq, ex=None):
    lead = PADK // PROJ_TS

    def body(x_ref, gkv_ref, gmix_ref, wk_ref, wv_ref, wq_ref, gk_ref, gq_ref,
             hkv_ref, h1_ref, kpre_ref, qpre_ref, k_ref, v_ref, q_ref):
        i = pl.program_id(0)

        @pl.when(i < lead)
        def _():
            k_ref[...] = jnp.zeros(k_ref.shape, BF16)
            v_ref[...] = jnp.zeros(v_ref.shape, BF16)

        @pl.when(i >= lead)
        def _():
            xv = x_ref[...]
            xr = xv * _rstd(xv)
            hkv = (xr * gkv_ref[...]).astype(BF16)
            h1 = (xr * gmix_ref[...]).astype(BF16)
            hkv_ref[...] = hkv
            h1_ref[...] = h1
            kpre = _nn(hkv, wk_ref[...])
            qpre = _nn(h1, wq_ref[...])
            kpre_ref[...] = kpre
            qpre_ref[...] = qpre
            v_ref[...] = _nn(hkv, wv_ref[...]).astype(BF16)
            rk = lax.rsqrt(_seg_sum(kpre * kpre) * (1.0 / HEAD) + EPS)
            k_ref[...] = (kpre * rk * gk_ref[...]).astype(BF16)
            rq = lax.rsqrt(_seg_sum(qpre * qpre) * (1.0 / HEAD) + EPS)
            q_ref[...] = (qpre * rq * gq_ref[...]).astype(BF16)

    tile = pl.BlockSpec((PROJ_TS, D), lambda i: (jnp.maximum(i - lead, 0), 0))
    padded = pl.BlockSpec((PROJ_TS, D), lambda i: (i, 0))
    row = pl.BlockSpec((1, D), lambda i: (0, 0))
    wspec = pl.BlockSpec((D, D), lambda i: (0, 0))
    bf = jax.ShapeDtypeStruct((S, D), BF16)
    ff = jax.ShapeDtypeStruct((S, D), F32)
    bp = jax.ShapeDtypeStruct((PADK + S, D), BF16)
    return _call_hosting(
        body, ex, _grid_phases((lead + S // PROJ_TS,), 0.85), name="kvq_proj", grid=(lead + S // PROJ_TS,),
        in_specs=[tile, row, row, wspec, wspec, wspec, row, row],
        out_specs=[tile, tile, tile, tile, padded, padded, tile], out_shape=[bf, bf, ff, ff, bp, bp, bf],
        scratch_shapes=[], args=(x, g_kv, g_mix, wk, wv, wq, gk, gq), compiler_params=_params("arbitrary"),
    )


def _qkv_input_bwd(dq, qpre, gq, dkp, kpre, gk, dvp, wq, wk, wv, dres, x, g_mix, g_kv, ex=None):
    nt = S // PROJ_TS
    lead = PADK // PROJ_TS

    def head_bwd(dov, pv, hgv):
        r = lax.rsqrt(_seg_sum(pv * pv) * (1.0 / HEAD) + EPS)
        gd = dov * hgv
        dpre = r * gd - pv * (r * r * r) * (_seg_sum(gd * pv) * (1.0 / HEAD))
        return dpre.astype(BF16), _colsum8(dov * pv * r)

    def fold_heads(full):
        fold = full[:, 0:128]
        for blk in range(1, D // 128):
            fold = fold + full[:, blk * 128:(blk + 1) * 128]
        return fold + pltpu.roll(fold, HEAD, axis=1)

    def body(dq_ref, qpre_ref, gq_ref, dk_ref, kpre_ref, gk_ref, dv_ref, wq_ref, wk_ref, wv_ref, dres_ref, x_ref,
             gmix_ref, gkv_ref, dx_ref, dqpre_ref, dkpre_ref, dgmix_ref, dgkv_ref, dgq_ref, dgk_ref, accq, acck):
        i = pl.program_id(0)

        @pl.when(i == 0)
        def _():
            accq[...] = jnp.zeros(accq.shape, F32)
            acck[...] = jnp.zeros(acck.shape, F32)
            dgmix_ref[...] = jnp.zeros(dgmix_ref.shape, F32)
            dgkv_ref[...] = jnp.zeros(dgkv_ref.shape, F32)

        dqb, cq = head_bwd(dq_ref[...], qpre_ref[...], gq_ref[...])
        dkb, ck = head_bwd(dk_ref[...], kpre_ref[...], gk_ref[...])
        dqpre_ref[...] = dqb
        dkpre_ref[...] = dkb
        accq[...] += cq
        acck[...] += ck
        dh1 = _nt(dqb, wq_ref[...])
        dhkv = _nt(dkb, wk_ref[...]) + _nt(dv_ref[...].astype(BF16), wv_ref[...])
        xv = x_ref[...]
        r = _rstd(xv)
        dx_ref[...] = dres_ref[...] + _rms_bwd(dh1, xv, r, gmix_ref[...]) + _rms_bwd(dhkv, xv, r, gkv_ref[...])
        dgmix_ref[...] += _colsum8(dh1 * xv * r)
        dgkv_ref[...] += _colsum8(dhkv * xv * r)

        @pl.when(i == nt - 1)
        def _():
            dgq_ref[...] = fold_heads(accq[...])
            dgk_ref[...] = fold_heads(acck[...])

    tile = pl.BlockSpec((PROJ_TS, D), lambda i: (i, 0))
    behind = pl.BlockSpec((PROJ_TS, D), lambda i: (i + lead, 0))
    row = pl.BlockSpec((1, D), lambda i: (0, 0))
    wspec = lambda: pl.BlockSpec((D, D), lambda i: (0, 0), pipeline_mode=pl.Buffered(1))
    acc = pl.BlockSpec((8, D), lambda i: (0, 0))
    small = pl.BlockSpec((8, 128), lambda i: (0, 0))
    bf = jax.ShapeDtypeStruct((S, D), BF16)
    return _call_hosting(
        body, ex, _grid_phases((nt,)), name="qkv_input_bwd", grid=(nt,),
        in_specs=[tile, tile, row, behind, tile, row, behind, wspec(), wspec(), wspec(), tile, tile, row, row],
        out_specs=[tile, tile, tile, acc, acc, small, small],
        out_shape=[jax.ShapeDtypeStruct((S, D), F32), bf, bf, jax.ShapeDtypeStruct((8, D), F32),
                   jax.ShapeDtypeStruct((8, D), F32), jax.ShapeDtypeStruct((8, 128), F32),
                   jax.ShapeDtypeStruct((8, 128), F32)],
        scratch_shapes=[pltpu.VMEM((8, D), F32), pltpu.VMEM((8, D), F32)],
        args=(dq, qpre, gq, dkp, kpre, gk, dvp, wq, wk, wv, dres, x, g_mix, g_kv),
        compiler_params=_params("arbitrary"),
    )


def _toeplitz_from_table(table):
    far = jnp.broadcast_to(table[:, N_REL - 1:], (N_HEADS, PADK - MAX_REL + 1))
    near = table[:, N_REL - 2::-1]
    past = jnp.broadcast_to(table[:, 0:1], (N_HEADS, MAX_REL))
    wrap = jnp.broadcast_to(table[:, N_REL - 1:], (N_HEADS, QB - 1))
    return jnp.concatenate([far, near, past, wrap], axis=1).reshape(N_HEADS, 1, TOEP)


def _table_grad_from_toeplitz(dtp, seg):
    lo = PADK - MAX_REL + 1
    near = dtp[:, lo + N_REL - 3:lo - 1:-1]
    return jnp.concatenate([seg[:, 1:2], near, seg[:, 0:1]], axis=1)


def _bias_band(tp):
    def body(tp_ref, out_ref):
        bv = pltpu.roll(jnp.broadcast_to(tp_ref[0], (QB, TOEP)), 0, axis=1, stride=1, stride_axis=0)
        out_ref[0] = jnp.where(_band_mask(), bv[:, 0:KB], NEG_INF)

    return _call(
        body, name="bias_band", grid=(N_HEADS,),
        in_specs=[pl.BlockSpec((1, 1, TOEP), lambda h: (h, 0, 0))],
        out_specs=pl.BlockSpec((1, QB, KB), lambda h: (h, 0, 0)),
        out_shape=jax.ShapeDtypeStruct((N_HEADS, QB, KB), F32),
        compiler_params=_params("parallel"),
    )(tp)


def _bias_grad(dband, after):
    lo, hi = PADK - MAX_REL + 1, PADK + MAX_REL

    def body(db_ref, _, dtp_ref, seg_ref):
        bv = jnp.concatenate([db_ref[0], jnp.zeros((QB, TOEP - KB), F32)], axis=1)
        row = lax.broadcasted_iota(jnp.int32, (QB, TOEP), 0)
        k = 1
        while k < QB:
            bv = jnp.where((row & k) != 0, pltpu.roll(bv, TOEP - k, axis=1), bv)
            k *= 2
        col = jnp.sum(bv, axis=0, keepdims=True)
        dtp_ref[0] = col
        u = lax.broadcasted_iota(jnp.int32, (1, TOEP), 1)
        far = jnp.sum(jnp.where((u < lo) | (u > hi + MAX_REL), col, 0.0))
        past = jnp.sum(jnp.where((u >= hi) & (u <= hi + MAX_REL), col, 0.0))
        lane = lax.broadcasted_iota(jnp.int32, (1, 128), 1)
        seg_ref[0] = jnp.where(lane == 0, far, jnp.where(lane == 1, past, 0.0))

    return _call(
        body, name="bias_grad", grid=(N_HEADS,),
        in_specs=[pl.BlockSpec((1, QB, KB), lambda h: (h, 0, 0)), ANY],
        out_specs=[pl.BlockSpec((1, 1, TOEP), lambda h: (h, 0, 0)), pl.BlockSpec((1, 1, 128), lambda h: (h, 0, 0))],
        out_shape=[jax.ShapeDtypeStruct((N_HEADS, 1, TOEP), F32), jax.ShapeDtypeStruct((N_HEADS, 1, 128), F32)],
        compiler_params=_params("parallel"),
    )(dband, after)


N_QB = S // QB
HEADS_PER_STEP = 4
ATT_LANES = HEADS_PER_STEP * HEAD
N_HG = D // ATT_LANES


def _band_mask():
    qc = lax.broadcasted_iota(jnp.int32, (QB, KB), 0) // CHUNK
    kc = lax.broadcasted_iota(jnp.int32, (QB, KB), 1) // CHUNK
    return (kc >= qc) & (kc <= qc + LEFT)


def _half_scale(hh, scale):
    lane = lax.broadcasted_iota(jnp.int32, (1, 128), 1)
    return jnp.where((lane < HEAD) == (hh == 0), scale, 0.0).astype(BF16)


def _probs(qh, kb, bias, first_key):
    sc = _nt(qh, kb) + bias
    if first_key is not None:
        sc = jnp.where(lax.broadcasted_iota(jnp.int32, (QB, KB), 1) >= first_key, sc, NEG_INF)
    e = jnp.exp(sc - jnp.max(sc, axis=-1, keepdims=True))
    return e * (1.0 / jnp.sum(e, axis=-1, keepdims=True))


def _by_padding(cb, compute):
    @pl.when(cb < PADK // QB)
    def _():
        compute(PADK - cb * QB)

    @pl.when(cb >= PADK // QB)
    def _():
        compute(None)


def _attn_fwd(q, kp, vp, bias, ex=None):
    def body(q_ref, k_ref, v_ref, b_ref, o_ref):
        cb = pl.program_id(1)
        band = pl.ds(pl.multiple_of(cb * QB, QB), KB)
        low = lax.broadcasted_iota(jnp.int32, (QB, 128), 1) < HEAD

        def compute(first_key):
            for pair in range(HEADS_PER_STEP // 2):
                lanes = pl.ds(pair * 128, 128)
                kb, vb, qv = k_ref[band, lanes], v_ref[band, lanes], q_ref[:, lanes]
                outs = []
                for hh in range(2):
                    pb = _probs(qv * _half_scale(hh, ATTN_SCALE), kb, b_ref[2 * pair + hh], first_key).astype(BF16)
                    outs.append(_nn(pb, vb))
                o_ref[:, lanes] = jnp.where(low, outs[0], outs[1]).astype(BF16)

        _by_padding(cb, compute)

    qspec = pl.BlockSpec((QB, ATT_LANES), lambda hg, cb: (cb, hg))
    kspec = pl.BlockSpec((PADK + S, ATT_LANES), lambda hg, cb: (0, hg))
    return _call_hosting(
        body, ex, _grid_phases((N_HG, N_QB), 0.85), name="attn_fwd", grid=(N_HG, N_QB),
        in_specs=[qspec, kspec, kspec, pl.BlockSpec((HEADS_PER_STEP, QB, KB), lambda hg, cb: (hg, 0, 0))],
        out_specs=[qspec], out_shape=[jax.ShapeDtypeStruct((S, D), BF16)], scratch_shapes=[],
        args=(q, kp, vp, bias), compiler_params=_params("arbitrary", "arbitrary"),
    )


def _attn_bwd(q, kp, vp, bias, do, ex=None):
    def body(q_ref, k_ref, v_ref, b_ref, do_ref, dq_ref, dk_ref, dv_ref, db_ref):
        cb = pl.program_id(1)

        @pl.when(cb == 0)
        def _():
            dk_ref[...] = jnp.zeros(dk_ref.shape, F32)
            dv_ref[...] = jnp.zeros(dv_ref.shape, F32)
            db_ref[...] = jnp.zeros(db_ref.shape, F32)

        band = pl.ds(pl.multiple_of(cb * QB, QB), KB)
        low = lax.broadcasted_iota(jnp.int32, (QB, 128), 1) < HEAD

        def compute(first_key):
            for pair in range(HEADS_PER_STEP // 2):
                lanes = pl.ds(pair * 128, 128)
                kb, vb = k_ref[band, lanes], v_ref[band, lanes]
                qv, dov = q_ref[:, lanes], do_ref[:, lanes]
                dq = jnp.zeros((QB, 128), F32)
                dkb = jnp.zeros((KB, 128), F32)
                dvb = jnp.zeros((KB, 128), F32)
                for hh in range(2):
                    sel = low if hh == 0 else jnp.logical_not(low)
                    doh = dov * _half_scale(hh, 1.0)
                    p = _probs(qv * _half_scale(hh, ATTN_SCALE), kb, b_ref[2 * pair + hh], first_key)
                    dp = _nt(doh, vb)
                    dvb = dvb + _tn(p.astype(BF16), doh)
                    ds = p * (dp - jnp.sum(dp * p, axis=-1, keepdims=True))
                    db_ref[2 * pair + hh] += ds
                    dsb = (ds * ATTN_SCALE).astype(BF16)
                    dq = dq + jnp.where(sel, _nn(dsb, kb), 0.0)
                    dkb = dkb + _tn(dsb, qv * _half_scale(hh, 1.0))
                dq_ref[:, lanes] = dq
                dk_ref[band, lanes] += dkb
                dv_ref[band, lanes] += dvb

        _by_padding(cb, compute)

    qspec = pl.BlockSpec((QB, ATT_LANES), lambda hg, cb: (cb, hg))
    kspec = pl.BlockSpec((PADK + S, ATT_LANES), lambda hg, cb: (0, hg))
    bspec = pl.BlockSpec((HEADS_PER_STEP, QB, KB), lambda hg, cb: (hg, 0, 0))
    kf = jax.ShapeDtypeStruct((PADK + S, D), F32)
    return _call_hosting(
        body, ex, _grid_phases((N_HG, N_QB)), name="attn_bwd", grid=(N_HG, N_QB),
        in_specs=[qspec, kspec, kspec, bspec, qspec],
        out_specs=[qspec, kspec, kspec, bspec],
        out_shape=[jax.ShapeDtypeStruct((S, D), F32), kf, kf, jax.ShapeDtypeStruct((N_HEADS, QB, KB), F32)],
        scratch_shapes=[], args=(q, kp, vp, bias, do), compiler_params=_params("arbitrary", "arbitrary"),
    )


def _adam_math(w, g, m, v):
    m = ADAM_B1 * m + (1.0 - ADAM_B1) * g
    v = ADAM_B2 * v + (1.0 - ADAM_B2) * (g * g)
    m_hat = m / (1.0 - ADAM_B1 ** ADAM_STEP)
    v_hat = v / (1.0 - ADAM_B2 ** ADAM_STEP)
    delta = -ADAM_LR * (m_hat / (jnp.sqrt(v_hat) + ADAM_EPS) + ADAM_WD * w)
    return delta, m, v


def _sum_parts(parts, name, after=None):
    count, r, c = parts.shape
    tr = r // 2 if r % 16 == 0 and r > 64 else r

    def body(p_ref, *rest):
        o_ref = rest[-1]
        acc = p_ref[0].astype(F32)
        for d in range(1, count):
            acc = acc + p_ref[d].astype(F32)
        o_ref[...] = acc

    return _call(
        body, name=name, grid=(r // tr,),
        in_specs=[pl.BlockSpec((count, tr, c), lambda i: (0, i, 0))] + ([ANY] if after is not None else []),
        out_specs=pl.BlockSpec((tr, c), lambda i: (i, 0)),
        out_shape=jax.ShapeDtypeStruct((r, c), F32), compiler_params=_params("parallel"),
    )(parts, *([after] if after is not None else []))


def _row_tile(r):
    for cand in (256, 176, 128, 64, 32, 16, 8):
        if r % cand == 0:
            return cand
    return r


def _adam_layer(w, g, m, v, layer, name, prev=None):
    nl, r, c = w.shape
    tr = _row_tile(r)

    def body(w_ref, g_ref, m_ref, v_ref, *rest):
        go_ref, d_ref, nm_ref, nv_ref = rest[-4:]
        gv = g_ref[...]
        delta, nm, nv = _adam_math(w_ref[0], gv, m_ref[0], v_ref[0])
        go_ref[0] = gv
        d_ref[0] = delta
        nm_ref[0] = nm
        nv_ref[0] = nv

    lspec = pl.BlockSpec((1, tr, c), lambda i: (layer, i, 0))
    sd = jax.ShapeDtypeStruct((nl, r, c), F32)
    extra = list(prev) if prev is not None else []
    return _call(
        body, name=name, grid=(r // tr,),
        in_specs=[lspec, pl.BlockSpec((tr, c), lambda i: (i, 0)), lspec, lspec] + [ANY] * len(extra),
        out_specs=[lspec] * 4, out_shape=[sd] * 4,
        input_output_aliases={4 + t: t for t in range(len(extra))},
        compiler_params=_params("parallel"),
    )(w, g, m, v, *extra)


def _adam(w, g, m, v, name):
    r, c = w.shape
    tr = _row_tile(r)

    def body(w_ref, g_ref, m_ref, v_ref, d_ref, nm_ref, nv_ref):
        delta, nm, nv = _adam_math(w_ref[...], g_ref[...], m_ref[...], v_ref[...])
        d_ref[...] = delta
        nm_ref[...] = nm
        nv_ref[...] = nv

    spec = pl.BlockSpec((tr, c), lambda i: (i, 0))
    sd = jax.ShapeDtypeStruct((r, c), F32)
    return _call(
        body, name=name, grid=(r // tr,), in_specs=[spec] * 4, out_specs=[spec] * 3, out_shape=[sd] * 3,
        compiler_params=_params("parallel"),
    )(w, g, m, v)


def _pad8(rows):
    return jnp.pad(rows, ((0, 8 - rows.shape[0]), (0, 0)))


def kernel(x, norm_mix_g, norm_ffn_g, pool_w, pool_b, pool_scale, kv_norm_g, w_k, w_v, k_norm_g, w_q, q_norm_g, rel_bias, w_o, w_gate, w_up, w_down, loss_target, m_norm_mix_g, m_norm_ffn_g, m_pool_w, m_pool_b, m_pool_scale, m_kv_norm_g, m_w_k, m_w_v, m_k_norm_g, m_w_q, m_q_norm_g, m_rel_bias, m_w_o, m_w_gate, m_w_up, m_w_down, v_norm_mix_g, v_norm_ffn_g, v_pool_w, v_pool_b, v_pool_scale, v_kv_norm_g, v_w_k, v_w_v, v_k_norm_g, v_w_q, v_q_norm_g, v_rel_bias, v_w_o, v_w_gate, v_w_up, v_w_down):
    assert x.shape == (1, S, D) and w_gate.shape == (2, D, F_SHARD) and w_k.shape == (D_SHARD, D)
    xin, target = x[0], loss_target[0]

    ffn_shards = [[w_gate[layer].T.astype(BF16), w_up[layer].T.astype(BF16), w_down[layer].astype(BF16)]
                  for layer in range(2)]
    att_shards = [w_k.astype(BF16), w_v.astype(BF16), w_q[0].astype(BF16), w_o[0].astype(BF16)]
    pool_shard = pool_w[0].astype(BF16).reshape(N_GROUPS * POOL_SHARD, GROUP)
    small = jnp.concatenate([pool_b[0].reshape(1, N_GROUPS * POOL_SHARD), pool_scale], axis=1)

    full0 = _run_exchange(_gather_exchange(ffn_shards[0] + [pool_shard, _pad8(small)]), "gather_layer0")
    ffn_w0 = [a.reshape(F, D) for a in full0[:3]]
    pw_f = full0[3].reshape(N_DEV, N_GROUPS, POOL_SHARD, GROUP).transpose(1, 0, 2, 3).reshape(N_GROUPS, GROUP, GROUP)
    small_f = full0[4][:, 0, :]
    pb_f = small_f[:, :N_GROUPS * POOL_SHARD].reshape(N_DEV, N_GROUPS, POOL_SHARD).transpose(1, 0, 2).reshape(1, D)
    ps_f = small_f[:, N_GROUPS * POOL_SHARD:].reshape(1, D)

    g_mix0, g_mix1 = norm_mix_g[0:1], norm_mix_g[1:2]
    g_ffn0, g_ffn1 = norm_ffn_g[0:1], norm_ffn_g[1:2]
    g_kv = kv_norm_g.reshape(1, D)
    gk_t = jnp.tile(k_norm_g.reshape(1, HEAD), (1, N_HEADS))
    gq_t = jnp.tile(q_norm_g.reshape(1, HEAD), (1, N_HEADS))

    x1, diff = _pool_fwd(xin, g_mix0, pw_f, pb_f, ps_f)
    (x2, hf0, gg0, uu0), full_att = _ffn_fwd(x1, g_ffn0, *ffn_w0, ex=_gather_exchange(att_shards))
    wk_f, wv_f, wq_f, wo_f = [a.reshape(D, D) for a in full_att]
    (hkv, h1, kpre, qpre, kp, vp, qq), full1_gate = _kvq_proj(x2, g_kv, g_mix1, wk_f, wv_f, wq_f, gk_t, gq_t,
                                                                ex=_gather_exchange(ffn_shards[1][:1]))
    bias = _bias_band(_toeplitz_from_table(rel_bias[0]))
    (att,), full1_rest = _attn_fwd(qq, kp, vp, bias, ex=_gather_exchange(ffn_shards[1][1:]))
    ffn_w1 = [a.reshape(F, D) for a in full1_gate + full1_rest]
    x3 = _mm(att, wo_f, "nn", F32, "attn_out", add=x2)
    (dx4, loss_rows, hf1, gg1, uu1), _ = _ffn_fwd(x3, g_ffn1, *ffn_w1, target=target)

    def blocks(dw):
        return dw.reshape(N_DEV, dw.shape[0] // N_DEV, dw.shape[1])

    (dgg1, duu1, dwg1, dwu1, dwd1), _ = _ffn_bwd_weights(dx4, hf1, gg1, uu1, ffn_w1[2], name="ffn_bwd1")
    parts1 = [blocks(dw) for dw in (dwg1, dwu1, dwd1)]
    (dx3, dg_ffn1), stage1 = _ffn_bwd_input(dx4, x3, g_ffn1, dgg1, duu1, ffn_w1[0], ffn_w1[1], "ffn_dx1",
                                            ex=_pair_exchange(parts1))
    both1 = _pair_add(parts1, stage1, "pair_add_ffn1", copies=2)
    flight1, chip1, land1, _ = _chip_exchange_start(both1[:3], both1[3:], dx3, "scatter_ffn1_start")
    datt = _mm(dx3, wo_f, "nt", BF16, "d_attn")
    dwo = _mm(att, dx3, "tn", BF16, "d_wo")
    (dq, dkp, dvp, dband), _ = _attn_bwd(qq, kp, vp, bias, datt)
    (dx2, dqpre, dkpre, dg_mix1, dg_kv, dgq, dgk), _ = _qkv_input_bwd(
        dq, qpre, gq_t, dkp, kpre, gk_t, dvp, wq_f, wk_f, wv_f, dx3, x2, g_mix1, g_kv)
    dwq = _mm(h1, dqpre, "tn", BF16, "d_wq")
    dwk = _mm(hkv, dkpre, "tn", BF16, "d_wk")
    dwv = _mm(hkv, dvp, "tn", BF16, "d_wv", skip_rows=PADK)
    parts_att = [blocks(dw) for dw in (dwk, dwv, dwq, dwo)]
    (dgg0, duu0, dwg0, dwu0, dwd0), stage_att = _ffn_bwd_weights(
        dx2, hf0, gg0, uu0, ffn_w0[2], name="ffn_bwd0", ex=_pair_exchange(parts_att))
    both_att = _pair_add(parts_att, stage_att, "pair_add_att", copies=2)
    flight_att, chip_att, land_att, _ = _chip_exchange_start(both_att[:4], both_att[4:], dx2, "scatter_att_start")
    parts0 = [blocks(dw) for dw in (dwg0, dwu0, dwd0)]
    (dx1, dg_ffn0), stage0 = _ffn_bwd_input(dx2, x1, g_ffn0, dgg0, duu0, ffn_w0[0], ffn_w0[1], "ffn_dx0",
                                            ex=_pair_exchange(parts0))
    both = _pair_add(parts0, stage0, "pair_add_ffn0", copies=2)
    flight, chip0, land0, token = _chip_exchange_start(both[:3], both[3:], dx1, "scatter_ffn0_start")
    (grad_x, dpw, db_rows, ds_rows, dg_mix0), _ = _pool_bwd(dx1, xin, diff, g_mix0 + token[0:1, 0:1], pw_f, pb_f, ps_f)
    dtp, seg = _bias_grad(dband, dpw)

    weights = dict(norm_mix_g=norm_mix_g, norm_ffn_g=norm_ffn_g, pool_w=pool_w, pool_b=pool_b,
                   pool_scale=pool_scale, kv_norm_g=kv_norm_g, w_k=w_k, w_v=w_v, k_norm_g=k_norm_g, w_q=w_q,
                   q_norm_g=q_norm_g, rel_bias=rel_bias, w_o=w_o, w_gate=w_gate, w_up=w_up, w_down=w_down)
    mom1 = dict(norm_mix_g=m_norm_mix_g, norm_ffn_g=m_norm_ffn_g, pool_w=m_pool_w, pool_b=m_pool_b,
                pool_scale=m_pool_scale, kv_norm_g=m_kv_norm_g, w_k=m_w_k, w_v=m_w_v, k_norm_g=m_k_norm_g,
                w_q=m_w_q, q_norm_g=m_q_norm_g, rel_bias=m_rel_bias, w_o=m_w_o, w_gate=m_w_gate, w_up=m_w_up,
                w_down=m_w_down)
    mom2 = dict(norm_mix_g=v_norm_mix_g, norm_ffn_g=v_norm_ffn_g, pool_w=v_pool_w, pool_b=v_pool_b,
                pool_scale=v_pool_scale, kv_norm_g=v_kv_norm_g, w_k=v_w_k, w_v=v_w_v, k_norm_g=v_k_norm_g,
                w_q=v_w_q, q_norm_g=v_q_norm_g, rel_bias=v_rel_bias, w_o=v_w_o, w_gate=v_w_gate, w_up=v_w_up,
                w_down=v_w_down)
    names = list(weights)
    grads, deltas, new_m, new_v = {}, {}, {}, {}

    def summed(arrs, tag, after):
        return [_sum_parts(a, "sum_parts_%s%d" % (tag, k), after) for k, a in enumerate(arrs)]

    def adam_flat(nm):
        shape = weights[nm].shape
        flat = lambda a: a.reshape(-1, shape[-1])
        dl, m1, m2 = _adam(flat(weights[nm]), flat(grads[nm]), flat(mom1[nm]), flat(mom2[nm]), "adam_" + nm)
        deltas[nm], new_m[nm], new_v[nm] = dl.reshape(shape), m1.reshape(shape), m2.reshape(shape)

    ffn_names = ("w_gate", "w_up", "w_down")
    def ffn_view(nm, a):
        return a if nm == "w_down" else a.transpose(0, 2, 1)

    recv1 = _chip_exchange_wait(flight1, chip1, land1, dtp, "scatter_ffn1_wait")
    sums1 = summed(recv1, "ffn1_", token)
    layer1 = {nm: _adam_layer(ffn_view(nm, weights[nm]), g, ffn_view(nm, mom1[nm]), ffn_view(nm, mom2[nm]), 1,
                              "adam1_" + nm)
              for nm, g in zip(ffn_names, sums1)}
    recv_att = _chip_exchange_wait(flight_att, chip_att, land_att, layer1["w_down"][1], "scatter_att_wait")
    g_wk, g_wv, g_wq, g_wo = summed(recv_att, "att_", token)
    grads.update(w_k=g_wk, w_v=g_wv, w_q=g_wq[None], w_o=g_wo[None])
    for nm in ("w_k", "w_v", "w_q", "w_o"):
        adam_flat(nm)

    dpw_blocks = dpw.reshape(N_GROUPS, N_DEV, POOL_SHARD, GROUP).transpose(1, 0, 2, 3)
    dpw_blocks = dpw_blocks.reshape(N_DEV, N_GROUPS * POOL_SHARD, GROUP).astype(BF16)
    recv_pool = _run_exchange(_scatter_exchange([dpw_blocks]), "scatter_pool", after=deltas["w_o"])
    misc = jnp.concatenate([dgk[0:1, 0:HEAD], dgq[0:1, 0:HEAD], loss_rows[0:1, 0:1],
                            seg[:, 0, 0].reshape(1, N_HEADS), seg[:, 0, 1].reshape(1, N_HEADS)], axis=1)
    misc = jnp.pad(misc, ((0, 0), (0, D - misc.shape[1])))
    vec_rows = jnp.concatenate([dg_mix0[0:1], dg_mix1[0:1], dg_ffn0[0:1], dg_ffn1[0:1], dg_kv[0:1],
                                db_rows[0:1], ds_rows[0:1], misc], axis=0)
    pack = jnp.concatenate([vec_rows, dtp.reshape(N_HEADS, TOEP)], axis=0)
    tot = _all_reduce_small(pack, "reduce_small", after=recv_pool[0])

    loss = tot[7, 2 * HEAD]
    seg_tot = jnp.stack([tot[7, 2 * HEAD + 1:2 * HEAD + 1 + N_HEADS],
                         tot[7, 2 * HEAD + 1 + N_HEADS:2 * HEAD + 1 + 2 * N_HEADS]], axis=1)
    me = 4 * lax.axis_index("x") + 2 * lax.axis_index("y") + lax.axis_index("c")
    g_pool_b = lax.dynamic_slice_in_dim(tot[5].reshape(N_GROUPS, GROUP), me * POOL_SHARD, POOL_SHARD, axis=1)
    grads.update(
        norm_mix_g=tot[0:2], norm_ffn_g=tot[2:4], kv_norm_g=tot[4], k_norm_g=tot[7, 0:HEAD],
        q_norm_g=tot[7, HEAD:2 * HEAD].reshape(1, HEAD),
        rel_bias=_table_grad_from_toeplitz(tot[8:8 + N_HEADS], seg_tot).reshape(1, N_HEADS, N_REL),
        pool_b=g_pool_b.reshape(1, N_GROUPS, POOL_SHARD),
        pool_scale=lax.dynamic_slice_in_dim(tot[6:7], me * D_SHARD, D_SHARD, axis=1),
        pool_w=summed(recv_pool, "pool_", None)[0].reshape(1, N_GROUPS, POOL_SHARD, GROUP))
    adam_flat("pool_w")
    small_names = [nm for nm in names if nm not in ffn_names + ("w_k", "w_v", "w_q", "w_o", "pool_w")]

    def pack_small(tree):
        cols = []
        for nm in small_names:
            flat = tree[nm].reshape(-1)
            cols.append(jnp.pad(flat, (0, -flat.shape[0] % 1024)))
        return jnp.concatenate(cols).reshape(-1, 128)

    dl, m1, m2 = _adam(pack_small(weights), pack_small(grads), pack_small(mom1), pack_small(mom2), "adam_small")

    def unpack_small(packed, out):
        flat, off = packed.reshape(-1), 0
        for nm in small_names:
            size = weights[nm].size
            out[nm] = flat[off:off + size].reshape(weights[nm].shape)
            off += size + (-size % 1024)

    unpack_small(dl, deltas)
    unpack_small(m1, new_m)
    unpack_small(m2, new_v)

    recv0 = _chip_exchange_wait(flight, chip0, land0, dl, "scatter_ffn0_wait")
    sums0 = summed(recv0, "ffn0_", None)
    for nm, g in zip(ffn_names, sums0):
        done = _adam_layer(ffn_view(nm, weights[nm]), g, ffn_view(nm, mom1[nm]), ffn_view(nm, mom2[nm]), 0,
                           "adam0_" + nm, prev=layer1[nm])
        grads[nm], deltas[nm], new_m[nm], new_v[nm] = [ffn_view(nm, a) for a in done]

    return (loss, grad_x[None], *[grads[nm] for nm in names], *[deltas[nm] for nm in names],
            *[new_m[nm] for nm in names], *[new_v[nm] for nm in names])
```

```python
import functools

import jax
import jax.numpy as jnp
from jax import lax
from jax.experimental import pallas as pl
from jax.experimental.pallas import tpu as pltpu

F32 = jnp.float32
BF16 = jnp.bfloat16
MESH_ID = pl.DeviceIdType.MESH

N_DEV = 8
S = 2048
D = 1024
F = 2816
F_SHARD = F // N_DEV
D_SHARD = D // N_DEV
N_GROUPS = 4
GROUP = D // N_GROUPS
POOL_SHARD = GROUP // N_DEV
MAX_WIN = 16
HEAD = 64
N_HEADS = D // HEAD
CHUNK = 64
LEFT = 8
QB = 4 * CHUNK
KB = QB + LEFT * CHUNK
PADK = LEFT * CHUNK
TOEP = 1024
N_REL = 257
MAX_REL = 128
EPS = 1e-6
NEG_INF = -1e30
ATTN_SCALE = HEAD ** -0.5

ADAM_LR = 0.001
ADAM_B1 = 0.9
ADAM_B2 = 0.999
ADAM_EPS = 1e-08
ADAM_WD = 0.01
ADAM_STEP = 10

VMEM_LIMIT = 52 * 1024 * 1024

ANY = pl.BlockSpec(memory_space=pl.ANY)
VMEM = pl.BlockSpec(memory_space=pltpu.VMEM)


def _call(body, **kw):
    return pl.pallas_call(body, **kw)


def _params(*sem):
    return pltpu.CompilerParams(dimension_semantics=sem, vmem_limit_bytes=VMEM_LIMIT)


def _dot(a, b, dims):
    return lax.dot_general(a, b, (dims, ((), ())), preferred_element_type=F32)


def _nn(a, b):
    return _dot(a, b, ((1,), (0,)))


def _nt(a, b):
    return _dot(a, b, ((1,), (1,)))


def _tn(a, b):
    return _dot(a, b, ((0,), (0,)))


def _rstd(x):
    return lax.rsqrt(jnp.mean(x * x, axis=-1, keepdims=True) + EPS)


def _rms_bwd(dh, x, r, g):
    gd = dh * g
    return r * gd - x * (r * r * r) * jnp.mean(gd * x, axis=-1, keepdims=True)


def _colsum8(v):
    return jnp.broadcast_to(jnp.sum(v, axis=0, keepdims=True), (8, v.shape[1]))


def _seg_sum(v):
    r = lax.broadcasted_iota(jnp.int32, (128, 128), 0) // HEAD
    c = lax.broadcasted_iota(jnp.int32, (128, 128), 1) // HEAD
    ones = jnp.where(r == c, 1.0, 0.0).astype(BF16)
    out = []
    for blk in range(v.shape[1] // 128):
        part = v[:, blk * 128:(blk + 1) * 128]
        hi = part.astype(BF16)
        rest = part - hi.astype(F32)
        mid = rest.astype(BF16)
        lo = (rest - mid.astype(F32)).astype(BF16)
        out.append(_nn(hi, ones) + _nn(mid, ones) + _nn(lo, ones))
    return jnp.concatenate(out, axis=1)


def _place():
    return lax.axis_index("x"), lax.axis_index("y"), lax.axis_index("c")


class _Exchange:
    def __init__(self, ins, out_shape, sems, start, mid, finish):
        self.ins, self.out_shape, self.sems = list(ins), list(out_shape), list(sems)
        self.start, self.mid, self.finish = start, mid, finish


def _gather_exchange(shards):
    n = len(shards)

    def tools(ins, outs, sems):
        send_sems, recv_sems, local_sems = sems
        x, y, c = _place()
        me, sibling = (x, y, c), (x, y, 1 - c)
        chips = [(1 - x, y), (x, 1 - y), (1 - x, 1 - y)]

        def slot(k, px, py, pc):
            return outs[k].at[4 * px + 2 * py + pc]

        def copy(k, s, block, to, own=False):
            return pltpu.make_async_remote_copy(
                src_ref=ins[k] if own else slot(k, *block), dst_ref=slot(k, *block),
                send_sem=send_sems.at[k, s], recv_sem=recv_sems.at[k, s], device_id=to, device_id_type=MESH_ID)

        def mine():
            return [pltpu.make_async_copy(ins[k], slot(k, *me), local_sems.at[k]) for k in range(n)]

        def first():
            out = []
            for k in range(n):
                out.append(copy(k, 0, me, sibling, own=True))
                out += [copy(k, 1 + j, me, (*chip, c), own=True) for j, chip in enumerate(chips)]
            return out

        def landed(j):
            return [copy(k, 1 + j, (*chips[j], c), me) for k in range(n)]

        def passed(j):
            return [copy(k, 4 + j, (*chips[j], c), sibling) for k in range(n)]

        def from_sibling():
            out = []
            for k in range(n):
                out.append(copy(k, 0, sibling, me))
                out += [copy(k, 4 + j, (*chip, 1 - c), me) for j, chip in enumerate(chips)]
            return out

        return mine, first, landed, passed, from_sibling

    def start(ins, outs, sems):
        mine, first, _, _, _ = tools(ins, outs, sems)
        for cp in mine() + first():
            cp.start()

    def mid(ins, outs, sems):
        _, _, landed, passed, _ = tools(ins, outs, sems)
        for j in range(3):
            for arrived, onward in zip(landed(j), passed(j)):
                arrived.wait_recv()
                onward.start()

    def finish(ins, outs, sems):
        mine, first, _, passed, from_sibling = tools(ins, outs, sems)
        for cp in from_sibling():
            cp.wait_recv()
        for cp in first() + passed(0) + passed(1) + passed(2):
            cp.wait_send()
        for cp in mine():
            cp.wait()

    return _Exchange(
        shards, [jax.ShapeDtypeStruct((N_DEV,) + a.shape, a.dtype) for a in shards],
        [pltpu.SemaphoreType.DMA((n, 7)), pltpu.SemaphoreType.DMA((n, 7)), pltpu.SemaphoreType.DMA((n,))],
        start, mid, finish)


def _peer(x, y, c, m):
    px = 1 - x if m & 4 else x
    py = 1 - y if m & 2 else y
    pc = 1 - c if m & 1 else c
    return px, py, pc


def _scatter_exchange(parts):
    n = len(parts)

    def tools(ins, outs, sems):
        send_sems, recv_sems, local_sems = sems
        x, y, c = _place()
        me = 4 * x + 2 * y + c
        def mine():
            return [pltpu.make_async_copy(ins[k].at[me], outs[k].at[me], local_sems.at[k]) for k in range(n)]

        def remote(dst_is_mine):
            out = []
            for m in range(1, N_DEV):
                px, py, pc = _peer(x, y, c, m)
                peer = 4 * px + 2 * py + pc
                for k in range(n):
                    out.append(pltpu.make_async_remote_copy(
                        src_ref=ins[k].at[peer], dst_ref=outs[k].at[me if dst_is_mine else peer],
                        send_sem=send_sems.at[k, m - 1], recv_sem=recv_sems.at[k, m - 1],
                        device_id=(px, py, pc), device_id_type=MESH_ID))
            return out

        return mine, remote

    def start(ins, outs, sems):
        mine, remote = tools(ins, outs, sems)
        for cp in mine() + remote(True):
            cp.start()

    def mid(ins, outs, sems):
        pass

    def finish(ins, outs, sems):
        mine, remote = tools(ins, outs, sems)
        for cp in remote(False):
            cp.wait_recv()
        for cp in remote(True):
            cp.wait_send()
        for cp in mine():
            cp.wait()

    return _Exchange(
        parts, [jax.ShapeDtypeStruct(a.shape, a.dtype) for a in parts],
        [pltpu.SemaphoreType.DMA((n, 7)), pltpu.SemaphoreType.DMA((n, 7)), pltpu.SemaphoreType.DMA((n,))],
        start, mid, finish)


N_CHIPS = N_DEV // 2


def _pair_exchange(parts):
    n = len(parts)

    def copies(ins, outs, sems):
        send_sems, recv_sems = sems
        x, y, c = _place()
        return [pltpu.make_async_remote_copy(
            src_ref=ins[k].at[2 * q + 1 - c], dst_ref=outs[k].at[q], send_sem=send_sems.at[k, q],
            recv_sem=recv_sems.at[k, q], device_id=(x, y, 1 - c), device_id_type=MESH_ID)
            for k in range(n) for q in range(N_CHIPS)]

    def start(ins, outs, sems):
        for cp in copies(ins, outs, sems):
            cp.start()

    def mid(ins, outs, sems):
        pass

    def finish(ins, outs, sems):
        for cp in copies(ins, outs, sems):
            cp.wait_recv()
        for cp in copies(ins, outs, sems):
            cp.wait_send()

    return _Exchange(
        parts, [jax.ShapeDtypeStruct((N_CHIPS,) + a.shape[1:], a.dtype) for a in parts],
        [pltpu.SemaphoreType.DMA((n, N_CHIPS)), pltpu.SemaphoreType.DMA((n, N_CHIPS))], start, mid, finish)


def _pair_add(parts, stage, name, copies=1):
    n = len(parts)
    core = lax.axis_index("c").reshape(1)

    def body(core_ref, *refs):
        for k in range(n):
            mine, theirs = refs[k], refs[n + k]
            total = (mine[0, 0].astype(F32) + theirs[0].astype(F32)).astype(BF16)
            for rep in range(copies):
                refs[(2 + rep) * n + k][0] = total

    in_specs, out_specs = [], []
    for a in parts:
        _, r, cdim = a.shape
        in_specs.append(pl.BlockSpec((1, 1, r // 2, cdim), lambda q, i, core_ref: (q, core_ref[0], i, 0)))
    for a in parts:
        _, r, cdim = a.shape
        in_specs.append(pl.BlockSpec((1, r // 2, cdim), lambda q, i, core_ref: (q, i, 0)))
        out_specs.append(pl.BlockSpec((1, r // 2, cdim), lambda q, i, core_ref: (q, i, 0)))
    return list(_call(
        body, name=name,
        grid_spec=pltpu.PrefetchScalarGridSpec(num_scalar_prefetch=1, grid=(N_CHIPS, 2), in_specs=in_specs,
                                               out_specs=out_specs * copies),
        out_shape=[jax.ShapeDtypeStruct(s.shape, BF16) for s in stage] * copies,
        compiler_params=_params("arbitrary", "arbitrary"),
    )(core, *[a.reshape((N_CHIPS, 2) + a.shape[1:]) for a in parts], *stage))


HBM = pl.BlockSpec(memory_space=pltpu.HBM)
SEMAPHORES = pl.BlockSpec(memory_space=pltpu.SEMAPHORE)


def _chip_copies(srcs, lands, send_sems, recv_sems, mine_is_dst):
    x, y, c = _place()
    me = 2 * x + y
    out = []
    for m in range(1, N_CHIPS):
        px, py, _ = _peer(x, y, c, 2 * m)
        peer = 2 * px + py
        for k in range(len(srcs)):
            pair = k * (N_CHIPS - 1) + m - 1
            out.append(pltpu.make_async_remote_copy(
                src_ref=srcs[k].at[peer], dst_ref=lands[k].at[me if mine_is_dst else peer],
                send_sem=send_sems[pair], recv_sem=recv_sems[pair],
                device_id=(px, py, c), device_id_type=MESH_ID))
    return out


def _chip_exchange_start(chip_parts, lands, after, name):
    n = len(chip_parts)
    pairs = n * (N_CHIPS - 1)

    def body(*refs):
        srcs, zones = refs[:n], refs[n:2 * n]
        sems = refs[2 * n + 1:2 * n + 1 + 2 * pairs]
        token = refs[-1]
        for cp in _chip_copies(srcs, zones, sems[:pairs], sems[pairs:], True):
            cp.start()
        token[...] = jnp.zeros(token.shape, F32)

    thru = [pltpu.HBM(a.shape, a.dtype) for a in chip_parts + lands]
    hbm = [pltpu.with_memory_space_constraint(a, pltpu.HBM) for a in chip_parts + lands]
    res = _call(
        body, name=name,
        out_shape=[pltpu.SemaphoreType.DMA(())] * (2 * pairs) + thru + [jax.ShapeDtypeStruct((8, 128), F32)],
        in_specs=[HBM] * (2 * n) + [ANY], out_specs=[SEMAPHORES] * (2 * pairs) + [HBM] * (2 * n) + [VMEM],
        input_output_aliases={i: 2 * pairs + i for i in range(2 * n)},
        compiler_params=pltpu.CompilerParams(has_side_effects=pltpu.SideEffectType.DATAFLOW_SIDE_EFFECTING),
    )(*hbm, after)
    sems, rest = list(res[:2 * pairs]), res[2 * pairs:]
    return sems, list(rest[:n]), list(rest[n:2 * n]), rest[-1]


def _chip_exchange_wait(sems, chip_parts, lands, after, name):
    n = len(chip_parts)
    pairs = n * (N_CHIPS - 1)

    def body(*refs):
        srcs, zones = refs[:n], refs[n:2 * n]
        sem_refs = refs[2 * n:2 * n + 2 * pairs]
        for cp in _chip_copies(srcs, zones, sem_refs[:pairs], sem_refs[pairs:], False):
            cp.wait_send()
            cp.wait_recv()

    thru = [pltpu.HBM(a.shape, a.dtype) for a in chip_parts + lands]
    res = _call(
        body, name=name, out_shape=thru,
        in_specs=[HBM] * (2 * n) + [SEMAPHORES] * (2 * pairs) + [ANY], out_specs=[HBM] * (2 * n),
        input_output_aliases={i: i for i in range(2 * n)},
        compiler_params=pltpu.CompilerParams(has_side_effects=pltpu.SideEffectType.DATAFLOW_SIDE_EFFECTING),
    )(*chip_parts, *lands, *sems, after)
    return list(res[n:])


def _run_exchange(ex, name, after=None):
    n_in, n_out = len(ex.ins), len(ex.out_shape)
    order = [] if after is None else [after]

    def body(*refs):
        ins, outs = refs[:n_in], refs[n_in + len(order):n_in + len(order) + n_out]
        sems = refs[n_in + len(order) + n_out:]
        ex.start(ins, outs, sems)
        ex.mid(ins, outs, sems)
        ex.finish(ins, outs, sems)

    return list(_call(body, name=name, out_shape=ex.out_shape, in_specs=[ANY] * (n_in + len(order)),
                      out_specs=[ANY] * n_out, scratch_shapes=ex.sems)(*ex.ins, *order))


def _call_hosting(body, ex, phases, *, in_specs, out_specs, out_shape, scratch_shapes, args, **kw):
    n_in, n_out, n_scr = len(in_specs), len(out_specs), len(scratch_shapes)
    if ex is None:
        res = _call(body, in_specs=in_specs, out_specs=out_specs, out_shape=out_shape,
                    scratch_shapes=scratch_shapes, **kw)(*args)
        return list(res), []
    n_xin, n_xout = len(ex.ins), len(ex.out_shape)

    def hosting(*refs):
        a, b = n_in, n_in + n_xin
        c, d = b + n_out, b + n_out + n_xout
        ins, xins, outs, xouts = refs[:a], refs[a:b], refs[b:c], refs[c:d]
        scr, sems = refs[d:d + n_scr], refs[d + n_scr:]
        first, mid, last = phases()

        @pl.when(first)
        def _():
            ex.start(xins, xouts, sems)

        body(*ins, *outs, *scr)

        @pl.when(mid)
        def _():
            ex.mid(xins, xouts, sems)

        @pl.when(last)
        def _():
            ex.finish(xins, xouts, sems)

    res = _call(hosting, in_specs=list(in_specs) + [ANY] * n_xin, out_specs=list(out_specs) + [ANY] * n_xout,
                out_shape=list(out_shape) + ex.out_shape, scratch_shapes=list(scratch_shapes) + ex.sems,
                **kw)(*args, *ex.ins)
    return list(res[:n_out]), list(res[n_out:])


def _grid_phases(dims, mid_fraction=0.8):
    total = 1
    for d in dims:
        total *= d
    mid = min(max(int(total * mid_fraction), 1), total - 1)

    def phases():
        step = pl.program_id(0)
        for axis in range(1, len(dims)):
            step = step * dims[axis] + pl.program_id(axis)
        return step == 0, step == mid, step == total - 1
    return phases


def _all_reduce_small(pack, name, after):
    rows = pack.shape[0]

    def body(in_ref, _, out_ref, recv, send_sems, recv_sems):
        x, y, c = _place()
        me = 4 * x + 2 * y + c
        recv[me] = in_ref[...]
        sent = []
        for m in range(1, N_DEV):
            px, py, pc = _peer(x, y, c, m)
            cp = pltpu.make_async_remote_copy(
                src_ref=in_ref, dst_ref=recv.at[me], send_sem=send_sems.at[m - 1], recv_sem=recv_sems.at[m - 1],
                device_id=(px, py, pc), device_id_type=MESH_ID)
            cp.start()
            sent.append(cp)
        for m in range(1, N_DEV):
            px, py, pc = _peer(x, y, c, m)
            peer = 4 * px + 2 * py + pc
            pltpu.make_async_remote_copy(
                src_ref=in_ref, dst_ref=recv.at[peer], send_sem=send_sems.at[m - 1], recv_sem=recv_sems.at[m - 1],
                device_id=(px, py, pc), device_id_type=MESH_ID).wait_recv()
        acc = recv[0]
        for d in range(1, N_DEV):
            acc = acc + recv[d]
        out_ref[...] = acc
        for cp in sent:
            cp.wait_send()

    return _call(
        body, name=name, out_shape=jax.ShapeDtypeStruct(pack.shape, F32), in_specs=[VMEM, ANY], out_specs=VMEM,
        scratch_shapes=[pltpu.VMEM((N_DEV, rows, pack.shape[1]), F32), pltpu.SemaphoreType.DMA((7,)),
                        pltpu.SemaphoreType.DMA((7,))],
        compiler_params=pltpu.CompilerParams(vmem_limit_bytes=VMEM_LIMIT),
    )(pack, after)


POOL_TS = 256


def _pool_counts(first_row, rows, win):
    t = first_row + lax.broadcasted_iota(jnp.int32, (rows, 1), 0)
    return jnp.minimum(t + 1, win).astype(F32)


def _pool_fwd(x, g, w, b, scale):
    nt = S // POOL_TS

    def body(x_ref, g_ref, w_ref, b_ref, s_ref, out_ref, diff_ref, ext):
        i = pl.program_id(0)

        @pl.when(i == 0)
        def _():
            ext[0:MAX_WIN, :] = jnp.zeros((MAX_WIN, D), F32)

        @pl.when(i > 0)
        def _():
            ext[0:MAX_WIN, :] = ext[POOL_TS:POOL_TS + MAX_WIN, :]

        xv = x_ref[...]
        h = xv * _rstd(xv) * g_ref[...]
        ext[MAX_WIN:, :] = h
        for gi in range(N_GROUPS):
            win = 2 << gi
            cols = slice(gi * GROUP, (gi + 1) * GROUP)
            sm = ext[:, cols]
            k = 1
            while k < win:
                sm = sm + pltpu.roll(sm, k, axis=0)
                k *= 2
            pooled = sm[MAX_WIN:, :] / _pool_counts(i * POOL_TS, POOL_TS, win)
            diff = (pooled - h[:, cols]).astype(BF16)
            yv = (_nn(diff, w_ref[gi]) + b_ref[:, cols]) * s_ref[:, cols]
            out_ref[:, cols] = xv[:, cols] + yv
            diff_ref[:, cols] = diff

    row = pl.BlockSpec((1, D), lambda i: (0, 0))
    tile = pl.BlockSpec((POOL_TS, D), lambda i: (i, 0))
    return _call(
        body, name="pool_fwd", grid=(nt,),
        in_specs=[tile, row, pl.BlockSpec((N_GROUPS, GROUP, GROUP), lambda i: (0, 0, 0)), row, row],
        out_specs=[tile, tile],
        out_shape=[jax.ShapeDtypeStruct((S, D), F32), jax.ShapeDtypeStruct((S, D), BF16)],
        scratch_shapes=[pltpu.VMEM((POOL_TS + MAX_WIN, D), F32)],
        compiler_params=_params("arbitrary"),
    )(x, g, w, b, scale)


def _pool_bwd(dy, x, diff, g, w, b, scale, ex=None):
    nt = S // POOL_TS

    def body(dy_ref, x_ref, diff_ref, g_ref, w_ref, b_ref, s_ref, gx_ref, dw_ref, db_ref, ds_ref, dg_ref, ext, dh):
        i = pl.program_id(0)
        first_row = (nt - 1 - i) * POOL_TS

        @pl.when(i == 0)
        def _():
            ext[POOL_TS:, :] = jnp.zeros((MAX_WIN, D), F32)
            dw_ref[...] = jnp.zeros(dw_ref.shape, F32)
            db_ref[...] = jnp.zeros(db_ref.shape, F32)
            ds_ref[...] = jnp.zeros(ds_ref.shape, F32)
            dg_ref[...] = jnp.zeros(dg_ref.shape, F32)

        @pl.when(i > 0)
        def _():
            ext[POOL_TS:, :] = ext[0:MAX_WIN, :]

        dyv = dy_ref[...]
        for gi in range(N_GROUPS):
            win = 2 << gi
            cols = slice(gi * GROUP, (gi + 1) * GROUP)
            dfb = diff_ref[:, cols]
            z = _nn(dfb, w_ref[gi]) + b_ref[:, cols]
            dyg = dyv[:, cols]
            ds_ref[:, cols] += _colsum8(dyg * z)
            dz = dyg * s_ref[:, cols]
            db_ref[:, cols] += _colsum8(dz)
            dzb = dz.astype(BF16)
            dw_ref[gi] += _tn(dfb, dzb)
            ddiff = _nt(dzb, w_ref[gi])
            ext[0:POOL_TS, cols] = ddiff / _pool_counts(first_row, POOL_TS, win)
            sm = ext[:, cols]
            k = 1
            while k < win:
                sm = sm + pltpu.roll(sm, POOL_TS + MAX_WIN - k, axis=0)
                k *= 2
            dh[:, cols] = sm[0:POOL_TS, :] - ddiff
        xv = x_ref[...]
        r = _rstd(xv)
        gv = g_ref[...]
        dhv = dh[...]
        dg_ref[...] += _colsum8(dhv * xv * r)
        gx_ref[...] = dyv + _rms_bwd(dhv, xv, r, gv)

    row = pl.BlockSpec((1, D), lambda i: (0, 0))
    tile = pl.BlockSpec((POOL_TS, D), lambda i: (nt - 1 - i, 0))
    acc = pl.BlockSpec((8, D), lambda i: (0, 0))
    wspec = pl.BlockSpec((N_GROUPS, GROUP, GROUP), lambda i: (0, 0, 0))
    return _call_hosting(
        body, ex, _grid_phases((nt,)), name="pool_bwd", grid=(nt,),
        in_specs=[tile, tile, tile, row, wspec, row, row],
        out_specs=[tile, wspec, acc, acc, acc],
        out_shape=[jax.ShapeDtypeStruct((S, D), F32), jax.ShapeDtypeStruct((N_GROUPS, GROUP, GROUP), F32),
                   jax.ShapeDtypeStruct((8, D), F32), jax.ShapeDtypeStruct((8, D), F32),
                   jax.ShapeDtypeStruct((8, D), F32)],
        scratch_shapes=[pltpu.VMEM((POOL_TS + MAX_WIN, D), F32), pltpu.VMEM((POOL_TS, D), F32)],
        args=(dy, x, diff, g, w, b, scale), compiler_params=_params("arbitrary"),
    )


FFN_TS = min(S, 1024)
FFN_TF = 256
SPLIT0 = 32 * ((F_SHARD // 32 + 1) // 2)


def _ffn_fwd(x, g, wg_t, wu_t, wd, name, target=None, ex=None, res=None):
    width = wd.shape[0]
    ni, nj = S // FFN_TS, width // FFN_TF
    with_loss = target is not None
    extra = [a for a in (res, target) if a is not None]

    def body(*refs):
        x_ref, g_ref, wg_ref, wu_ref, wd_ref = refs[:5]
        more = list(refs[5:5 + len(extra)])
        res_ref = more.pop(0) if res is not None else x_ref
        t_ref = more.pop(0) if with_loss else None
        outs = refs[5 + len(extra):]
        if with_loss:
            out_ref, loss_ref, h_ref, gg_ref, uu_ref, hs, acc = outs
        else:
            out_ref, h_ref, gg_ref, uu_ref, hs, acc = outs
        i, j = pl.program_id(0), pl.program_id(1)

        @pl.when(j == 0)
        def _():
            xv = x_ref[...]
            hb = (xv * _rstd(xv) * g_ref[...]).astype(BF16)
            hs[...] = hb
            h_ref[...] = hb
            acc[...] = jnp.zeros(acc.shape, F32)

        hb = hs[...]
        gg = _nt(hb, wg_ref[...])
        uu = _nt(hb, wu_ref[...])
        gg_ref[...] = gg
        uu_ref[...] = uu
        a = (gg * jax.nn.sigmoid(gg) * uu).astype(BF16)
        acc[...] += _nn(a, wd_ref[...])

        @pl.when(j == nj - 1)
        def _():
            yv = res_ref[...] + acc[...]
            if with_loss:
                err = yv - t_ref[...]
                out_ref[...] = err * (1.0 / D)
                part = jnp.sum(err * err) * (0.5 / D)

                @pl.when(i == 0)
                def _():
                    loss_ref[...] = jnp.zeros(loss_ref.shape, F32)

                loss_ref[...] += jnp.broadcast_to(part, loss_ref.shape)
            else:
                out_ref[...] = yv

    xt = pl.BlockSpec((FFN_TS, D), lambda i, j: (i, 0))
    row = pl.BlockSpec((1, D), lambda i, j: (0, 0))
    wt = pl.BlockSpec((FFN_TF, D), lambda i, j: (j, 0))
    gt = pl.BlockSpec((FFN_TS, FFN_TF), lambda i, j: (i, j))
    in_specs = [xt, row, wt, wt, wt] + [xt] * len(extra)
    out_specs = [xt] + ([pl.BlockSpec((8, 128), lambda i, j: (0, 0))] if with_loss else []) + [xt, gt, gt]
    out_shape = ([jax.ShapeDtypeStruct((S, D), F32)] + ([jax.ShapeDtypeStruct((8, 128), F32)] if with_loss else [])
                 + [jax.ShapeDtypeStruct((S, D), BF16), jax.ShapeDtypeStruct((S, width), F32),
                    jax.ShapeDtypeStruct((S, width), F32)])
    args = (x, g, wg_t, wu_t, wd, *extra)
    return _call_hosting(
        body, ex, _grid_phases((ni, nj), 0.8), name=name, grid=(ni, nj),
        in_specs=in_specs, out_specs=out_specs, out_shape=out_shape,
        scratch_shapes=[pltpu.VMEM((FFN_TS, D), BF16), pltpu.VMEM((FFN_TS, D), F32)], args=args,
        compiler_params=_params("arbitrary", "arbitrary"),
    )


def _ffn_bwd_weights(dout, h, gg, uu, wd, name, ex=None):
    width = wd.shape[0]
    nj = width // FFN_TF

    def body(do_ref, h_ref, gg_ref, uu_ref, wd_ref, dg_ref, du_ref, dwg_ref, dwu_ref, dwd_ref, dob):
        @pl.when(pl.program_id(0) == 0)
        def _():
            dob[...] = do_ref[...].astype(BF16)

        hb, dov = h_ref[...], dob[...]
        gv, uv = gg_ref[...], uu_ref[...]
        da = _nt(dov, wd_ref[...])
        sg = jax.nn.sigmoid(gv)
        sl = gv * sg
        ab = (sl * uv).astype(BF16)
        dub = (da * sl).astype(BF16)
        dgb = (da * uv * (sg * (1.0 + gv * (1.0 - sg)))).astype(BF16)
        dg_ref[...] = dgb
        du_ref[...] = dub
        dwg_ref[...] = _tn(dgb, hb).astype(BF16)
        dwu_ref[...] = _tn(dub, hb).astype(BF16)
        dwd_ref[...] = _tn(ab, dov).astype(BF16)

    once = pl.Buffered(1)
    whole = lambda: pl.BlockSpec((S, D), lambda j, _: (0, 0), pipeline_mode=once)
    wt = pl.BlockSpec((FFN_TF, D), lambda j, _: (j, 0))
    gt = pl.BlockSpec((S, FFN_TF), lambda j, _: (0, j))
    return _call_hosting(
        body, ex, _grid_phases((nj, 1)), name=name, grid=(nj, 1),
        in_specs=[whole(), whole(), gt, gt, wt], out_specs=[gt, gt, wt, wt, wt],
        out_shape=[jax.ShapeDtypeStruct((S, width), BF16)] * 2 + [jax.ShapeDtypeStruct((width, D), BF16)] * 3,
        scratch_shapes=[pltpu.VMEM((S, D), BF16)], args=(dout, h, gg, uu, wd),
        compiler_params=_params("arbitrary", "arbitrary"),
    )


BWD_TS = 512


def _ffn_bwd_input(dres, x, g, grads, weights, name, ex=None):
    nt = S // BWD_TS
    n = len(grads)

    def body(*refs):
        dres_ref, x_ref, g_ref = refs[:3]
        grad_refs, w_refs = refs[3:3 + n], refs[3 + n:3 + 2 * n]
        dx_ref, dgam_ref = refs[3 + 2 * n:]
        dh = _nn(grad_refs[0][...], w_refs[0][...])
        for k in range(1, n):
            dh = dh + _nn(grad_refs[k][...], w_refs[k][...])
        xv = x_ref[...]
        r = _rstd(xv)
        dx_ref[...] = dres_ref[...] + _rms_bwd(dh, xv, r, g_ref[...])

        @pl.when(pl.program_id(0) == 0)
        def _():
            dgam_ref[...] = jnp.zeros(dgam_ref.shape, F32)

        dgam_ref[...] += _colsum8(dh * xv * r)

    tile = pl.BlockSpec((BWD_TS, D), lambda i: (i, 0))
    ftiles = [pl.BlockSpec((BWD_TS, a.shape[1]), lambda i: (i, 0)) for a in grads]
    wspecs = [pl.BlockSpec(w.shape, lambda i: (0, 0), pipeline_mode=pl.Buffered(1)) for w in weights]
    return _call_hosting(
        body, ex, _grid_phases((nt,)), name=name, grid=(nt,),
        in_specs=[tile, tile, pl.BlockSpec((1, D), lambda i: (0, 0))] + ftiles + wspecs,
        out_specs=[tile, pl.BlockSpec((8, D), lambda i: (0, 0))],
        out_shape=[jax.ShapeDtypeStruct((S, D), F32), jax.ShapeDtypeStruct((8, D), F32)],
        scratch_shapes=[], args=(dres, x, g, *grads, *weights), compiler_params=_params("arbitrary"),
    )


def _mm(a, b, mode, out_dtype, name, add=None, skip_rows=0):
    if mode == "nn":
        (m, kd), n = (a.shape[0] - skip_rows, a.shape[1]), b.shape[1]
    elif mode == "nt":
        (m, kd), n = (a.shape[0] - skip_rows, a.shape[1]), b.shape[0]
    else:
        (kd, m), n = a.shape, b.shape[1]
    tm, tn, tk = min(m, 1024), min(n, 1024), min(kd, 1024)
    if skip_rows and mode == "tn":
        tk = min(tk, skip_rows)
    elif skip_rows:
        tm = min(tm, skip_rows)
    skip_a = skip_rows // tm if mode != "tn" else 0
    skip_b = skip_rows // tk if mode == "tn" else 0
    assert skip_rows == skip_a * tm + skip_b * tk
    nk = kd // tk
    dot = {"nn": _nn, "nt": _nt, "tn": _tn}[mode]

    def body(*refs):
        if add is None:
            a_ref, b_ref, o_ref, acc = refs
        else:
            a_ref, b_ref, add_ref, o_ref, acc = refs
        k = pl.program_id(2)

        @pl.when(k == 0)
        def _():
            acc[...] = jnp.zeros(acc.shape, F32)

        acc[...] += dot(a_ref[...].astype(BF16), b_ref[...].astype(BF16))

        @pl.when(k == nk - 1)
        def _():
            res = acc[...]
            if add is not None:
                res = res + add_ref[...]
            o_ref[...] = res.astype(out_dtype)

    if mode == "tn":
        a_spec = pl.BlockSpec((tk, tm), lambda i, j, k: (k, i))
        b_spec = pl.BlockSpec((tk, tn), lambda i, j, k: (k + skip_b, j))
    else:
        a_spec = pl.BlockSpec((tm, tk), lambda i, j, k: (i + skip_a, k))
        b_spec = (pl.BlockSpec((tk, tn), lambda i, j, k: (k, j)) if mode == "nn"
                  else pl.BlockSpec((tn, tk), lambda i, j, k: (j, k)))
    o_spec = pl.BlockSpec((tm, tn), lambda i, j, k: (i, j))
    in_specs = [a_spec, b_spec] + ([o_spec] if add is not None else [])
    args = (a, b) + ((add,) if add is not None else ())
    return _call(
        body, name=name, grid=(m // tm, n // tn, nk), in_specs=in_specs, out_specs=o_spec,
        out_shape=jax.ShapeDtypeStruct((m, n), out_dtype), scratch_shapes=[pltpu.VMEM((tm, tn), F32)],
        compiler_params=_params("parallel", "parallel", "arbitrary"),
    )(*args)


PROJ_TS = 256


def _kvq_proj(x, g_kv, g_mix, wk, wv, wq, gk, gq, ex=None):
    lead = PADK // PROJ_TS

    def body(x_ref, gkv_ref, gmix_ref, wk_ref, wv_ref, wq_ref, gk_ref, gq_ref,
             hkv_ref, h1_ref, kpre_ref, qpre_ref, k_ref, v_ref, q_ref):
        i = pl.program_id(0)

        @pl.when(i < lead)
        def _():
            k_ref[...] = jnp.zeros(k_ref.shape, BF16)
            v_ref[...] = jnp.zeros(v_ref.shape, BF16)

        @pl.when(i >= lead)
        def _():
            xv = x_ref[...]
            xr = xv * _rstd(xv)
            hkv = (xr * gkv_ref[...]).astype(BF16)
            h1 = (xr * gmix_ref[...]).astype(BF16)
            hkv_ref[...] = hkv
            h1_ref[...] = h1
            kpre = _nn(hkv, wk_ref[...])
            qpre = _nn(h1, wq_ref[...])
            kpre_ref[...] = kpre
            qpre_ref[...] = qpre
            v_ref[...] = _nn(hkv, wv_ref[...]).astype(BF16)
            rk = lax.rsqrt(_seg_sum(kpre * kpre) * (1.0 / HEAD) + EPS)
            k_ref[...] = (kpre * rk * gk_ref[...]).astype(BF16)
            rq = lax.rsqrt(_seg_sum(qpre * qpre) * (1.0 / HEAD) + EPS)
            q_ref[...] = (qpre * rq * gq_ref[...]).astype(BF16)

    tile = pl.BlockSpec((PROJ_TS, D), lambda i: (jnp.maximum(i - lead, 0), 0))
    padded = pl.BlockSpec((PROJ_TS, D), lambda i: (i, 0))
    row = pl.BlockSpec((1, D), lambda i: (0, 0))
    wspec = pl.BlockSpec((D, D), lambda i: (0, 0))
    bf = jax.ShapeDtypeStruct((S, D), BF16)
    ff = jax.ShapeDtypeStruct((S, D), F32)
    bp = jax.ShapeDtypeStruct((PADK + S, D), BF16)
    return _call_hosting(
        body, ex, _grid_phases((lead + S // PROJ_TS,), 0.85), name="kvq_proj", grid=(lead + S // PROJ_TS,),
        in_specs=[tile, row, row, wspec, wspec, wspec, row, row],
        out_specs=[tile, tile, tile, tile, padded, padded, tile], out_shape=[bf, bf, ff, ff, bp, bp, bf],
        scratch_shapes=[], args=(x, g_kv, g_mix, wk, wv, wq, gk, gq), compiler_params=_params("arbitrary"),
    )


def _qkv_input_bwd(dq, qpre, gq, dkp, kpre, gk, dvp, wq, wk, wv, dres, x, g_mix, g_kv, ex=None):
    nt = S // PROJ_TS
    lead = PADK // PROJ_TS

    def head_bwd(dov, pv, hgv):
        r = lax.rsqrt(_seg_sum(pv * pv) * (1.0 / HEAD) + EPS)
        gd = dov * hgv
        dpre = r * gd - pv * (r * r * r) * (_seg_sum(gd * pv) * (1.0 / HEAD))
        return dpre.astype(BF16), _colsum8(dov * pv * r)

    def fold_heads(full):
        fold = full[:, 0:128]
        for blk in range(1, D // 128):
            fold = fold + full[:, blk * 128:(blk + 1) * 128]
        return fold + pltpu.roll(fold, HEAD, axis=1)

    def body(dq_ref, qpre_ref, gq_ref, dk_ref, kpre_ref, gk_ref, dv_ref, wq_ref, wk_ref, wv_ref, dres_ref, x_ref,
             gmix_ref, gkv_ref, dx_ref, dqpre_ref, dkpre_ref, dgmix_ref, dgkv_ref, dgq_ref, dgk_ref, accq, acck):
        i = pl.program_id(0)

        @pl.when(i == 0)
        def _():
            accq[...] = jnp.zeros(accq.shape, F32)
            acck[...] = jnp.zeros(acck.shape, F32)
            dgmix_ref[...] = jnp.zeros(dgmix_ref.shape, F32)
            dgkv_ref[...] = jnp.zeros(dgkv_ref.shape, F32)

        dqb, cq = head_bwd(dq_ref[...], qpre_ref[...], gq_ref[...])
        dkb, ck = head_bwd(dk_ref[...], kpre_ref[...], gk_ref[...])
        dqpre_ref[...] = dqb
        dkpre_ref[...] = dkb
        accq[...] += cq
        acck[...] += ck
        dh1 = _nt(dqb, wq_ref[...])
        dhkv = _nt(dkb, wk_ref[...]) + _nt(dv_ref[...].astype(BF16), wv_ref[...])
        xv = x_ref[...]
        r = _rstd(xv)
        dx_ref[...] = dres_ref[...] + _rms_bwd(dh1, xv, r, gmix_ref[...]) + _rms_bwd(dhkv, xv, r, gkv_ref[...])
        dgmix_ref[...] += _colsum8(dh1 * xv * r)
        dgkv_ref[...] += _colsum8(dhkv * xv * r)

        @pl.when(i == nt - 1)
        def _():
            dgq_ref[...] = fold_heads(accq[...])
            dgk_ref[...] = fold_heads(acck[...])

    tile = pl.BlockSpec((PROJ_TS, D), lambda i: (i, 0))
    behind = pl.BlockSpec((PROJ_TS, D), lambda i: (i + lead, 0))
    row = pl.BlockSpec((1, D), lambda i: (0, 0))
    wspec = lambda: pl.BlockSpec((D, D), lambda i: (0, 0), pipeline_mode=pl.Buffered(1))
    acc = pl.BlockSpec((8, D), lambda i: (0, 0))
    small = pl.BlockSpec((8, 128), lambda i: (0, 0))
    bf = jax.ShapeDtypeStruct((S, D), BF16)
    return _call_hosting(
        body, ex, _grid_phases((nt,)), name="qkv_input_bwd", grid=(nt,),
        in_specs=[tile, tile, row, behind, tile, row, behind, wspec(), wspec(), wspec(), tile, tile, row, row],
        out_specs=[tile, tile, tile, acc, acc, small, small],
        out_shape=[jax.ShapeDtypeStruct((S, D), F32), bf, bf, jax.ShapeDtypeStruct((8, D), F32),
                   jax.ShapeDtypeStruct((8, D), F32), jax.ShapeDtypeStruct((8, 128), F32),
                   jax.ShapeDtypeStruct((8, 128), F32)],
        scratch_shapes=[pltpu.VMEM((8, D), F32), pltpu.VMEM((8, D), F32)],
        args=(dq, qpre, gq, dkp, kpre, gk, dvp, wq, wk, wv, dres, x, g_mix, g_kv),
        compiler_params=_params("arbitrary"),
    )


def _toeplitz_from_table(table):
    far = jnp.broadcast_to(table[:, N_REL - 1:], (N_HEADS, PADK - MAX_REL + 1))
    near = table[:, N_REL - 2::-1]
    past = jnp.broadcast_to(table[:, 0:1], (N_HEADS, MAX_REL))
    wrap = jnp.broadcast_to(table[:, N_REL - 1:], (N_HEADS, QB - 1))
    return jnp.concatenate([far, near, past, wrap], axis=1).reshape(N_HEADS, 1, TOEP)


def _table_grad_from_toeplitz(dtp, seg):
    lo = PADK - MAX_REL + 1
    near = dtp[:, lo + N_REL - 3:lo - 1:-1]
    return jnp.concatenate([seg[:, 1:2], near, seg[:, 0:1]], axis=1)


def _bias_band(tp):
    def body(tp_ref, out_ref):
        bv = pltpu.roll(jnp.broadcast_to(tp_ref[0], (QB, TOEP)), 0, axis=1, stride=1, stride_axis=0)
        out_ref[0] = jnp.where(_band_mask(), bv[:, 0:KB], NEG_INF)

    return _call(
        body, name="bias_band", grid=(N_HEADS,),
        in_specs=[pl.BlockSpec((1, 1, TOEP), lambda h: (h, 0, 0))],
        out_specs=pl.BlockSpec((1, QB, KB), lambda h: (h, 0, 0)),
        out_shape=jax.ShapeDtypeStruct((N_HEADS, QB, KB), F32),
        compiler_params=_params("parallel"),
    )(tp)


def _bias_grad(dband, after):
    lo, hi = PADK - MAX_REL + 1, PADK + MAX_REL

    def body(db_ref, _, dtp_ref, seg_ref):
        bv = jnp.concatenate([db_ref[0], jnp.zeros((QB, TOEP - KB), F32)], axis=1)
        row = lax.broadcasted_iota(jnp.int32, (QB, TOEP), 0)
        k = 1
        while k < QB:
            bv = jnp.where((row & k) != 0, pltpu.roll(bv, TOEP - k, axis=1), bv)
            k *= 2
        col = jnp.sum(bv, axis=0, keepdims=True)
        dtp_ref[0] = col
        u = lax.broadcasted_iota(jnp.int32, (1, TOEP), 1)
        far = jnp.sum(jnp.where((u < lo) | (u > hi + MAX_REL), col, 0.0))
        past = jnp.sum(jnp.where((u >= hi) & (u <= hi + MAX_REL), col, 0.0))
        lane = lax.broadcasted_iota(jnp.int32, (1, 128), 1)
        seg_ref[0] = jnp.where(lane == 0, far, jnp.where(lane == 1, past, 0.0))

    return _call(
        body, name="bias_grad", grid=(N_HEADS,),
        in_specs=[pl.BlockSpec((1, QB, KB), lambda h: (h, 0, 0)), ANY],
        out_specs=[pl.BlockSpec((1, 1, TOEP), lambda h: (h, 0, 0)), pl.BlockSpec((1, 1, 128), lambda h: (h, 0, 0))],
        out_shape=[jax.ShapeDtypeStruct((N_HEADS, 1, TOEP), F32), jax.ShapeDtypeStruct((N_HEADS, 1, 128), F32)],
        compiler_params=_params("parallel"),
    )(dband, after)


N_QB = S // QB
HEADS_PER_STEP = 4
ATT_LANES = HEADS_PER_STEP * HEAD
N_HG = D // ATT_LANES


def _band_mask():
    qc = lax.broadcasted_iota(jnp.int32, (QB, KB), 0) // CHUNK
    kc = lax.broadcasted_iota(jnp.int32, (QB, KB), 1) // CHUNK
    return (kc >= qc) & (kc <= qc + LEFT)


def _half_scale(hh, scale):
    lane = lax.broadcasted_iota(jnp.int32, (1, 128), 1)
    return jnp.where((lane < HEAD) == (hh == 0), scale, 0.0).astype(BF16)


def _probs(qh, kb, bias, first_key):
    sc = _nt(qh, kb) + bias
    if first_key is not None:
        sc = jnp.where(lax.broadcasted_iota(jnp.int32, (QB, KB), 1) >= first_key, sc, NEG_INF)
    e = jnp.exp(sc - jnp.max(sc, axis=-1, keepdims=True))
    return e * (1.0 / jnp.sum(e, axis=-1, keepdims=True))


def _by_padding(cb, compute):
    @pl.when(cb < PADK // QB)
    def _():
        compute(PADK - cb * QB)

    @pl.when(cb >= PADK // QB)
    def _():
        compute(None)


def _attn_fwd(q, kp, vp, bias, ex=None):
    def body(q_ref, k_ref, v_ref, b_ref, o_ref):
        cb = pl.program_id(1)
        band = pl.ds(pl.multiple_of(cb * QB, QB), KB)
        low = lax.broadcasted_iota(jnp.int32, (QB, 128), 1) < HEAD

        def compute(first_key):
            for pair in range(HEADS_PER_STEP // 2):
                lanes = pl.ds(pair * 128, 128)
                kb, vb, qv = k_ref[band, lanes], v_ref[band, lanes], q_ref[:, lanes]
                outs = []
                for hh in range(2):
                    pb = _probs(qv * _half_scale(hh, ATTN_SCALE), kb, b_ref[2 * pair + hh], first_key).astype(BF16)
                    outs.append(_nn(pb, vb))
                o_ref[:, lanes] = jnp.where(low, outs[0], outs[1]).astype(BF16)

        _by_padding(cb, compute)

    qspec = pl.BlockSpec((QB, ATT_LANES), lambda hg, cb: (cb, hg))
    kspec = pl.BlockSpec((PADK + S, ATT_LANES), lambda hg, cb: (0, hg))
    return _call_hosting(
        body, ex, _grid_phases((N_HG, N_QB), 0.85), name="attn_fwd", grid=(N_HG, N_QB),
        in_specs=[qspec, kspec, kspec, pl.BlockSpec((HEADS_PER_STEP, QB, KB), lambda hg, cb: (hg, 0, 0))],
        out_specs=[qspec], out_shape=[jax.ShapeDtypeStruct((S, D), BF16)], scratch_shapes=[],
        args=(q, kp, vp, bias), compiler_params=_params("arbitrary", "arbitrary"),
    )


def _attn_bwd(q, kp, vp, bias, do, ex=None):
    def body(q_ref, k_ref, v_ref, b_ref, do_ref, dq_ref, dk_ref, dv_ref, db_ref):
        cb = pl.program_id(1)

        @pl.when(cb == 0)
        def _():
            dk_ref[...] = jnp.zeros(dk_ref.shape, F32)
            dv_ref[...] = jnp.zeros(dv_ref.shape, F32)
            db_ref[...] = jnp.zeros(db_ref.shape, F32)

        band = pl.ds(pl.multiple_of(cb * QB, QB), KB)
        low = lax.broadcasted_iota(jnp.int32, (QB, 128), 1) < HEAD

        def compute(first_key):
            for pair in range(HEADS_PER_STEP // 2):
                lanes = pl.ds(pair * 128, 128)
                kb, vb = k_ref[band, lanes], v_ref[band, lanes]
                qv, dov = q_ref[:, lanes], do_ref[:, lanes]
                dq = jnp.zeros((QB, 128), F32)
                dkb = jnp.zeros((KB, 128), F32)
                dvb = jnp.zeros((KB, 128), F32)
                for hh in range(2):
                    sel = low if hh == 0 else jnp.logical_not(low)
                    doh = dov * _half_scale(hh, 1.0)
                    p = _probs(qv * _half_scale(hh, ATTN_SCALE), kb, b_ref[2 * pair + hh], first_key)
                    dp = _nt(doh, vb)
                    dvb = dvb + _tn(p.astype(BF16), doh)
                    ds = p * (dp - jnp.sum(dp * p, axis=-1, keepdims=True))
                    db_ref[2 * pair + hh] += ds
                    dsb = (ds * ATTN_SCALE).astype(BF16)
                    dq = dq + jnp.where(sel, _nn(dsb, kb), 0.0)
                    dkb = dkb + _tn(dsb, qv * _half_scale(hh, 1.0))
                dq_ref[:, lanes] = dq
                dk_ref[band, lanes] += dkb
                dv_ref[band, lanes] += dvb

        _by_padding(cb, compute)

    qspec = pl.BlockSpec((QB, ATT_LANES), lambda hg, cb: (cb, hg))
    kspec = pl.BlockSpec((PADK + S, ATT_LANES), lambda hg, cb: (0, hg))
    bspec = pl.BlockSpec((HEADS_PER_STEP, QB, KB), lambda hg, cb: (hg, 0, 0))
    kf = jax.ShapeDtypeStruct((PADK + S, D), F32)
    return _call_hosting(
        body, ex, _grid_phases((N_HG, N_QB)), name="attn_bwd", grid=(N_HG, N_QB),
        in_specs=[qspec, kspec, kspec, bspec, qspec],
        out_specs=[qspec, kspec, kspec, bspec],
        out_shape=[jax.ShapeDtypeStruct((S, D), F32), kf, kf, jax.ShapeDtypeStruct((N_HEADS, QB, KB), F32)],
        scratch_shapes=[], args=(q, kp, vp, bias, do), compiler_params=_params("arbitrary", "arbitrary"),
    )


def _adam_math(w, g, m, v):
    m = ADAM_B1 * m + (1.0 - ADAM_B1) * g
    v = ADAM_B2 * v + (1.0 - ADAM_B2) * (g * g)
    m_hat = m / (1.0 - ADAM_B1 ** ADAM_STEP)
    v_hat = v / (1.0 - ADAM_B2 ** ADAM_STEP)
    delta = -ADAM_LR * (m_hat / (jnp.sqrt(v_hat) + ADAM_EPS) + ADAM_WD * w)
    return delta, m, v


def _sum_parts(parts, name, after=None):
    count, r, c = parts.shape
    tr = r // 2 if r % 16 == 0 and r > 64 else r

    def body(p_ref, *rest):
        o_ref = rest[-1]
        acc = p_ref[0].astype(F32)
        for d in range(1, count):
            acc = acc + p_ref[d].astype(F32)
        o_ref[...] = acc

    return _call(
        body, name=name, grid=(r // tr,),
        in_specs=[pl.BlockSpec((count, tr, c), lambda i: (0, i, 0))] + ([ANY] if after is not None else []),
        out_specs=pl.BlockSpec((tr, c), lambda i: (i, 0)),
        out_shape=jax.ShapeDtypeStruct((r, c), F32), compiler_params=_params("parallel"),
    )(parts, *([after] if after is not None else []))


def _row_tile(r):
    for cand in (256, 176, 128, 64, 32, 16, 8):
        if r % cand == 0:
            return cand
    return r


def _adam_layer(w, g, m, v, layer, name, prev=None):
    nl, r, c = w.shape
    tr = _row_tile(r)

    def body(w_ref, g_ref, m_ref, v_ref, *rest):
        go_ref, d_ref, nm_ref, nv_ref = rest[-4:]
        gv = g_ref[...]
        delta, nm, nv = _adam_math(w_ref[0], gv, m_ref[0], v_ref[0])
        go_ref[0] = gv
        d_ref[0] = delta
        nm_ref[0] = nm
        nv_ref[0] = nv

    lspec = pl.BlockSpec((1, tr, c), lambda i: (layer, i, 0))
    sd = jax.ShapeDtypeStruct((nl, r, c), F32)
    extra = list(prev) if prev is not None else []
    return _call(
        body, name=name, grid=(r // tr,),
        in_specs=[lspec, pl.BlockSpec((tr, c), lambda i: (i, 0)), lspec, lspec] + [ANY] * len(extra),
        out_specs=[lspec] * 4, out_shape=[sd] * 4,
        input_output_aliases={4 + t: t for t in range(len(extra))},
        compiler_params=_params("parallel"),
    )(w, g, m, v, *extra)


def _adam(w, g, m, v, name):
    r, c = w.shape
    tr = _row_tile(r)

    def body(w_ref, g_ref, m_ref, v_ref, d_ref, nm_ref, nv_ref):
        delta, nm, nv = _adam_math(w_ref[...], g_ref[...], m_ref[...], v_ref[...])
        d_ref[...] = delta
        nm_ref[...] = nm
        nv_ref[...] = nv

    spec = pl.BlockSpec((tr, c), lambda i: (i, 0))
    sd = jax.ShapeDtypeStruct((r, c), F32)
    return _call(
        body, name=name, grid=(r // tr,), in_specs=[spec] * 4, out_specs=[spec] * 3, out_shape=[sd] * 3,
        compiler_params=_params("parallel"),
    )(w, g, m, v)


def _pad8(rows):
    return jnp.pad(rows, ((0, 8 - rows.shape[0]), (0, 0)))


def kernel(x, norm_mix_g, norm_ffn_g, pool_w, pool_b, pool_scale, kv_norm_g, w_k, w_v, k_norm_g, w_q, q_norm_g, rel_bias, w_o, w_gate, w_up, w_down, loss_target, m_norm_mix_g, m_norm_ffn_g, m_pool_w, m_pool_b, m_pool_scale, m_kv_norm_g, m_w_k, m_w_v, m_k_norm_g, m_w_q, m_q_norm_g, m_rel_bias, m_w_o, m_w_gate, m_w_up, m_w_down, v_norm_mix_g, v_norm_ffn_g, v_pool_w, v_pool_b, v_pool_scale, v_kv_norm_g, v_w_k, v_w_v, v_k_norm_g, v_w_q, v_q_norm_g, v_rel_bias, v_w_o, v_w_gate, v_w_up, v_w_down):
    assert x.shape == (1, S, D) and w_gate.shape == (2, D, F_SHARD) and w_k.shape == (D_SHARD, D)
    xin, target = x[0], loss_target[0]

    ffn_shards = [[w_gate[layer].T.astype(BF16), w_up[layer].T.astype(BF16), w_down[layer].astype(BF16)]
                  for layer in range(2)]
    att_shards = [w_k.astype(BF16), w_v.astype(BF16), w_q[0].astype(BF16), w_o[0].astype(BF16)]
    pool_shard = pool_w[0].astype(BF16).reshape(N_GROUPS * POOL_SHARD, GROUP)
    small = jnp.concatenate([pool_b[0].reshape(1, N_GROUPS * POOL_SHARD), pool_scale], axis=1)

    first0 = [a[:SPLIT0] for a in ffn_shards[0]]
    rest0 = [a[SPLIT0:] for a in ffn_shards[0]]
    full0 = _run_exchange(_gather_exchange(first0 + [pool_shard, _pad8(small)]), "gather_layer0")
    w0a = [a.reshape(N_DEV * SPLIT0, D) for a in full0[:3]]
    pw_f = full0[3].reshape(N_DEV, N_GROUPS, POOL_SHARD, GROUP).transpose(1, 0, 2, 3).reshape(N_GROUPS, GROUP, GROUP)
    small_f = full0[4][:, 0, :]
    pb_f = small_f[:, :N_GROUPS * POOL_SHARD].reshape(N_DEV, N_GROUPS, POOL_SHARD).transpose(1, 0, 2).reshape(1, D)
    ps_f = small_f[:, N_GROUPS * POOL_SHARD:].reshape(1, D)

    g_mix0, g_mix1 = norm_mix_g[0:1], norm_mix_g[1:2]
    g_ffn0, g_ffn1 = norm_ffn_g[0:1], norm_ffn_g[1:2]
    g_kv = kv_norm_g.reshape(1, D)
    gk_t = jnp.tile(k_norm_g.reshape(1, HEAD), (1, N_HEADS))
    gq_t = jnp.tile(q_norm_g.reshape(1, HEAD), (1, N_HEADS))

    x1, diff = _pool_fwd(xin, g_mix0, pw_f, pb_f, ps_f)
    (x2a, hf0, gg0a, uu0a), full0b = _ffn_fwd(x1, g_ffn0, *w0a, name="ffn_fwd0a", ex=_gather_exchange(rest0))
    w0b = [a.reshape(N_DEV * (F_SHARD - SPLIT0), D) for a in full0b]
    (x2, _, gg0b, uu0b), full_kvq = _ffn_fwd(x1, g_ffn0, *w0b, name="ffn_fwd0b", res=x2a,
                                             ex=_gather_exchange(att_shards[:3]))
    wk_f, wv_f, wq_f = [a.reshape(D, D) for a in full_kvq]
    (hkv, h1, kpre, qpre, kp, vp, qq), full1_gate = _kvq_proj(x2, g_kv, g_mix1, wk_f, wv_f, wq_f, gk_t, gq_t,
                                                                ex=_gather_exchange(ffn_shards[1][:1]))
    bias = _bias_band(_toeplitz_from_table(rel_bias[0]))
    (att,), full1_rest = _attn_fwd(qq, kp, vp, bias, ex=_gather_exchange(ffn_shards[1][1:] + att_shards[3:]))
    ffn_w1 = [a.reshape(F, D) for a in full1_gate + full1_rest[:2]]
    wo_f = full1_rest[2].reshape(D, D)
    x3 = _mm(att, wo_f, "nn", F32, "attn_out", add=x2)
    (dx4, loss_rows, hf1, gg1, uu1), _ = _ffn_fwd(x3, g_ffn1, *ffn_w1, name="ffn_fwd_loss", target=target)

    def blocks(dw):
        return dw.reshape(N_DEV, dw.shape[0] // N_DEV, dw.shape[1])

    (dgg1, duu1, dwg1, dwu1, dwd1), _ = _ffn_bwd_weights(dx4, hf1, gg1, uu1, ffn_w1[2], name="ffn_bwd1")
    parts1 = [blocks(dw) for dw in (dwg1, dwu1, dwd1)]
    (dx3, dg_ffn1), stage1 = _ffn_bwd_input(dx4, x3, g_ffn1, [dgg1, duu1], ffn_w1[:2], "ffn_dx1",
                                            ex=_pair_exchange(parts1))
    both1 = _pair_add(parts1, stage1, "pair_add_ffn1", copies=2)
    flight1, chip1, land1, _ = _chip_exchange_start(both1[:3], both1[3:], dx3, "scatter_ffn1_start")
    datt = _mm(dx3, wo_f, "nt", BF16, "d_attn")
    dwo = _mm(att, dx3, "tn", BF16, "d_wo")
    (dq, dkp, dvp, dband), _ = _attn_bwd(qq, kp, vp, bias, datt)
    (dx2, dqpre, dkpre, dg_mix1, dg_kv, dgq, dgk), _ = _qkv_input_bwd(
        dq, qpre, gq_t, dkp, kpre, gk_t, dvp, wq_f, wk_f, wv_f, dx3, x2, g_mix1, g_kv)
    dwq = _mm(h1, dqpre, "tn", BF16, "d_wq")
    dwk = _mm(hkv, dkpre, "tn", BF16, "d_wk")
    dwv = _mm(hkv, dvp, "tn", BF16, "d_wv", skip_rows=PADK)
    parts_att = [blocks(dw) for dw in (dwk, dwv, dwq, dwo)]
    (dgg0a, duu0a, *dw0a), stage_att = _ffn_bwd_weights(
        dx2, hf0, gg0a, uu0a, w0a[2], name="ffn_bwd0a", ex=_pair_exchange(parts_att))
    both_att = _pair_add(parts_att, stage_att, "pair_add_att", copies=2)
    flight_att, chip_att, land_att, _ = _chip_exchange_start(both_att[:4], both_att[4:], dx2, "scatter_att_start")
    parts0a = [blocks(dw) for dw in dw0a]
    (dgg0b, duu0b, *dw0b), stage0a = _ffn_bwd_weights(
        dx2, hf0, gg0b, uu0b, w0b[2], name="ffn_bwd0b", ex=_pair_exchange(parts0a))
    both0a = _pair_add(parts0a, stage0a, "pair_add_ffn0a", copies=2)
    flight0a, chip0a, land0a, _ = _chip_exchange_start(both0a[:3], both0a[3:], dgg0b, "scatter_ffn0a_start")
    parts0 = [blocks(dw) for dw in dw0b]
    (dx1, dg_ffn0), stage0 = _ffn_bwd_input(dx2, x1, g_ffn0, [dgg0a, duu0a, dgg0b, duu0b],
                                            [w0a[0], w0a[1], w0b[0], w0b[1]], "ffn_dx0", ex=_pair_exchange(parts0))
    both = _pair_add(parts0, stage0, "pair_add_ffn0", copies=2)
    flight, chip0, land0, token = _chip_exchange_start(both[:3], both[3:], dx1, "scatter_ffn0_start")
    (grad_x, dpw, db_rows, ds_rows, dg_mix0), _ = _pool_bwd(dx1, xin, diff, g_mix0 + token[0:1, 0:1], pw_f, pb_f, ps_f)
    dtp, seg = _bias_grad(dband, dpw)

    weights = dict(norm_mix_g=norm_mix_g, norm_ffn_g=norm_ffn_g, pool_w=pool_w, pool_b=pool_b,
                   pool_scale=pool_scale, kv_norm_g=kv_norm_g, w_k=w_k, w_v=w_v, k_norm_g=k_norm_g, w_q=w_q,
                   q_norm_g=q_norm_g, rel_bias=rel_bias, w_o=w_o, w_gate=w_gate, w_up=w_up, w_down=w_down)
    mom1 = dict(norm_mix_g=m_norm_mix_g, norm_ffn_g=m_norm_ffn_g, pool_w=m_pool_w, pool_b=m_pool_b,
                pool_scale=m_pool_scale, kv_norm_g=m_kv_norm_g, w_k=m_w_k, w_v=m_w_v, k_norm_g=m_k_norm_g,
                w_q=m_w_q, q_norm_g=m_q_norm_g, rel_bias=m_rel_bias, w_o=m_w_o, w_gate=m_w_gate, w_up=m_w_up,
                w_down=m_w_down)
    mom2 = dict(norm_mix_g=v_norm_mix_g, norm_ffn_g=v_norm_ffn_g, pool_w=v_pool_w, pool_b=v_pool_b,
                pool_scale=v_pool_scale, kv_norm_g=v_kv_norm_g, w_k=v_w_k, w_v=v_w_v, k_norm_g=v_k_norm_g,
                w_q=v_w_q, q_norm_g=v_q_norm_g, rel_bias=v_rel_bias, w_o=v_w_o, w_gate=v_w_gate, w_up=v_w_up,
                w_down=v_w_down)
    names = list(weights)
    grads, deltas, new_m, new_v = {}, {}, {}, {}

    def summed(arrs, tag, after):
        return [_sum_parts(a, "sum_parts_%s%d" % (tag, k), after) for k, a in enumerate(arrs)]

    def adam_flat(nm):
        shape = weights[nm].shape
        flat = lambda a: a.reshape(-1, shape[-1])
        dl, m1, m2 = _adam(flat(weights[nm]), flat(grads[nm]), flat(mom1[nm]), flat(mom2[nm]), "adam_" + nm)
        deltas[nm], new_m[nm], new_v[nm] = dl.reshape(shape), m1.reshape(shape), m2.reshape(shape)

    ffn_names = ("w_gate", "w_up", "w_down")
    def ffn_view(nm, a):
        return a if nm == "w_down" else a.transpose(0, 2, 1)

    recv1 = _chip_exchange_wait(flight1, chip1, land1, dtp, "scatter_ffn1_wait")
    sums1 = summed(recv1, "ffn1_", token)
    layer1 = {nm: _adam_layer(ffn_view(nm, weights[nm]), g, ffn_view(nm, mom1[nm]), ffn_view(nm, mom2[nm]), 1,
                              "adam1_" + nm)
              for nm, g in zip(ffn_names, sums1)}
    recv_att = _chip_exchange_wait(flight_att, chip_att, land_att, layer1["w_down"][1], "scatter_att_wait")
    g_wk, g_wv, g_wq, g_wo = summed(recv_att, "att_", token)
    grads.update(w_k=g_wk, w_v=g_wv, w_q=g_wq[None], w_o=g_wo[None])
    for nm in ("w_k", "w_v", "w_q", "w_o"):
        adam_flat(nm)

    dpw_blocks = dpw.reshape(N_GROUPS, N_DEV, POOL_SHARD, GROUP).transpose(1, 0, 2, 3)
    dpw_blocks = dpw_blocks.reshape(N_DEV, N_GROUPS * POOL_SHARD, GROUP).astype(BF16)
    recv_pool = _run_exchange(_scatter_exchange([dpw_blocks]), "scatter_pool", after=deltas["w_o"])
    misc = jnp.concatenate([dgk[0:1, 0:HEAD], dgq[0:1, 0:HEAD], loss_rows[0:1, 0:1],
                            seg[:, 0, 0].reshape(1, N_HEADS), seg[:, 0, 1].reshape(1, N_HEADS)], axis=1)
    misc = jnp.pad(misc, ((0, 0), (0, D - misc.shape[1])))
    vec_rows = jnp.concatenate([dg_mix0[0:1], dg_mix1[0:1], dg_ffn0[0:1], dg_ffn1[0:1], dg_kv[0:1],
                                db_rows[0:1], ds_rows[0:1], misc], axis=0)
    pack = jnp.concatenate([vec_rows, dtp.reshape(N_HEADS, TOEP)], axis=0)
    tot = _all_reduce_small(pack, "reduce_small", after=recv_pool[0])

    loss = tot[7, 2 * HEAD]
    seg_tot = jnp.stack([tot[7, 2 * HEAD + 1:2 * HEAD + 1 + N_HEADS],
                         tot[7, 2 * HEAD + 1 + N_HEADS:2 * HEAD + 1 + 2 * N_HEADS]], axis=1)
    me = 4 * lax.axis_index("x") + 2 * lax.axis_index("y") + lax.axis_index("c")
    g_pool_b = lax.dynamic_slice_in_dim(tot[5].reshape(N_GROUPS, GROUP), me * POOL_SHARD, POOL_SHARD, axis=1)
    grads.update(
        norm_mix_g=tot[0:2], norm_ffn_g=tot[2:4], kv_norm_g=tot[4], k_norm_g=tot[7, 0:HEAD],
        q_norm_g=tot[7, HEAD:2 * HEAD].reshape(1, HEAD),
        rel_bias=_table_grad_from_toeplitz(tot[8:8 + N_HEADS], seg_tot).reshape(1, N_HEADS, N_REL),
        pool_b=g_pool_b.reshape(1, N_GROUPS, POOL_SHARD),
        pool_scale=lax.dynamic_slice_in_dim(tot[6:7], me * D_SHARD, D_SHARD, axis=1),
        pool_w=summed(recv_pool, "pool_", None)[0].reshape(1, N_GROUPS, POOL_SHARD, GROUP))
    adam_flat("pool_w")
    small_names = [nm for nm in names if nm not in ffn_names + ("w_k", "w_v", "w_q", "w_o", "pool_w")]

    def pack_small(tree):
        cols = []
        for nm in small_names:
            flat = tree[nm].reshape(-1)
            cols.append(jnp.pad(flat, (0, -flat.shape[0] % 1024)))
        return jnp.concatenate(cols).reshape(-1, 128)

    dl, m1, m2 = _adam(pack_small(weights), pack_small(grads), pack_small(mom1), pack_small(mom2), "adam_small")

    def unpack_small(packed, out):
        flat, off = packed.reshape(-1), 0
        for nm in small_names:
            size = weights[nm].size
            out[nm] = flat[off:off + size].reshape(weights[nm].shape)
            off += size + (-size % 1024)

    unpack_small(dl, deltas)
    unpack_small(m1, new_m)
    unpack_small(m2, new_v)

    recv0a = _chip_exchange_wait(flight0a, chip0a, land0a, dl, "scatter_ffn0a_wait")
    recv0b = _chip_exchange_wait(flight, chip0, land0, dl, "scatter_ffn0_wait")
    sums0 = [jnp.concatenate(pair, axis=0)
             for pair in zip(summed(recv0a, "ffn0a_", None), summed(recv0b, "ffn0b_", None))]
    for nm, g in zip(ffn_names, sums0):
        done = _adam_layer(ffn_view(nm, weights[nm]), g, ffn_view(nm, mom1[nm]), ffn_view(nm, mom2[nm]), 0,
                           "adam0_" + nm, prev=layer1[nm])
        grads[nm], deltas[nm], new_m[nm], new_v[nm] = [ffn_view(nm, a) for a in done]

    return (loss, grad_x[None], *[grads[nm] for nm in names], *[deltas[nm] for nm in names],
            *[new_m[nm] for nm in names], *[new_v[nm] for nm in names])
```

```python
import functools

import jax
import jax.numpy as jnp
from jax import lax
from jax.experimental import pallas as pl
from jax.experimental.pallas import tpu as pltpu

F32 = jnp.float32
BF16 = jnp.bfloat16
MESH_ID = pl.DeviceIdType.MESH

N_DEV = 8
S = 2048
D = 1024
F = 2816
F_SHARD = F // N_DEV
D_SHARD = D // N_DEV
N_GROUPS = 4
GROUP = D // N_GROUPS
POOL_SHARD = GROUP // N_DEV
MAX_WIN = 16
HEAD = 64
N_HEADS = D // HEAD
CHUNK = 64
LEFT = 8
QB = 4 * CHUNK
KB = QB + LEFT * CHUNK
PADK = LEFT * CHUNK
TOEP = 1024
N_REL = 257
MAX_REL = 128
EPS = 1e-6
NEG_INF = -1e30
ATTN_SCALE = HEAD ** -0.5

ADAM_LR = 0.001
ADAM_B1 = 0.9
ADAM_B2 = 0.999
ADAM_EPS = 1e-08
ADAM_WD = 0.01
ADAM_STEP = 10

VMEM_LIMIT = 52 * 1024 * 1024

ANY = pl.BlockSpec(memory_space=pl.ANY)
VMEM = pl.BlockSpec(memory_space=pltpu.VMEM)


def _call(body, **kw):
    return pl.pallas_call(body, **kw)


def _params(*sem):
    return pltpu.CompilerParams(dimension_semantics=sem, vmem_limit_bytes=VMEM_LIMIT)


def _dot(a, b, dims):
    return lax.dot_general(a, b, (dims, ((), ())), preferred_element_type=F32)


def _nn(a, b):
    return _dot(a, b, ((1,), (0,)))


def _nt(a, b):
    return _dot(a, b, ((1,), (1,)))


def _tn(a, b):
    return _dot(a, b, ((0,), (0,)))


def _rstd(x):
    return lax.rsqrt(jnp.mean(x * x, axis=-1, keepdims=True) + EPS)


def _rms_bwd(dh, x, r, g):
    gd = dh * g
    return r * gd - x * (r * r * r) * jnp.mean(gd * x, axis=-1, keepdims=True)


def _colsum8(v):
    return jnp.broadcast_to(jnp.sum(v, axis=0, keepdims=True), (8, v.shape[1]))


def _seg_sum(v):
    r = lax.broadcasted_iota(jnp.int32, (128, 128), 0) // HEAD
    c = lax.broadcasted_iota(jnp.int32, (128, 128), 1) // HEAD
    ones = jnp.where(r == c, 1.0, 0.0).astype(BF16)
    out = []
    for blk in range(v.shape[1] // 128):
        part = v[:, blk * 128:(blk + 1) * 128]
        hi = part.astype(BF16)
        rest = part - hi.astype(F32)
        mid = rest.astype(BF16)
        lo = (rest - mid.astype(F32)).astype(BF16)
        out.append(_nn(hi, ones) + _nn(mid, ones) + _nn(lo, ones))
    return jnp.concatenate(out, axis=1)


def _place():
    return lax.axis_index("x"), lax.axis_index("y"), lax.axis_index("c")


class _Exchange:
    def __init__(self, ins, out_shape, sems, start, mid, finish):
        self.ins, self.out_shape, self.sems = list(ins), list(out_shape), list(sems)
        self.start, self.mid, self.finish = start, mid, finish


def _gather_exchange(shards):
    n = len(shards)

    def tools(ins, outs, sems):
        send_sems, recv_sems, local_sems = sems
        x, y, c = _place()
        me, sibling = (x, y, c), (x, y, 1 - c)
        chips = [(1 - x, y), (x, 1 - y), (1 - x, 1 - y)]

        def slot(k, px, py, pc):
            return outs[k].at[4 * px + 2 * py + pc]

        def copy(k, s, block, to, own=False):
            return pltpu.make_async_remote_copy(
                src_ref=ins[k] if own else slot(k, *block), dst_ref=slot(k, *block),
                send_sem=send_sems.at[k, s], recv_sem=recv_sems.at[k, s], device_id=to, device_id_type=MESH_ID)

        def mine():
            return [pltpu.make_async_copy(ins[k], slot(k, *me), local_sems.at[k]) for k in range(n)]

        def first():
            out = []
            for k in range(n):
                out.append(copy(k, 0, me, sibling, own=True))
                out += [copy(k, 1 + j, me, (*chip, c), own=True) for j, chip in enumerate(chips)]
            return out

        def landed(j):
            return [copy(k, 1 + j, (*chips[j], c), me) for k in range(n)]

        def passed(j):
            return [copy(k, 4 + j, (*chips[j], c), sibling) for k in range(n)]

        def from_sibling():
            out = []
            for k in range(n):
                out.append(copy(k, 0, sibling, me))
                out += [copy(k, 4 + j, (*chip, 1 - c), me) for j, chip in enumerate(chips)]
            return out

        return mine, first, landed, passed, from_sibling

    def start(ins, outs, sems):
        mine, first, _, _, _ = tools(ins, outs, sems)
        for cp in mine() + first():
            cp.start()

    def mid(ins, outs, sems):
        _, _, landed, passed, _ = tools(ins, outs, sems)
        arrived, onward = [landed(j) for j in range(3)], [passed(j) for j in range(3)]
        for k in range(n):
            for j in range(3):
                arrived[j][k].wait_recv()
                onward[j][k].start()

    def finish(ins, outs, sems):
        mine, first, _, passed, from_sibling = tools(ins, outs, sems)
        for cp in from_sibling():
            cp.wait_recv()
        for cp in first() + passed(0) + passed(1) + passed(2):
            cp.wait_send()
        for cp in mine():
            cp.wait()

    return _Exchange(
        shards, [jax.ShapeDtypeStruct((N_DEV,) + a.shape, a.dtype) for a in shards],
        [pltpu.SemaphoreType.DMA((n, 7)), pltpu.SemaphoreType.DMA((n, 7)), pltpu.SemaphoreType.DMA((n,))],
        start, mid, finish)


def _peer(x, y, c, m):
    px = 1 - x if m & 4 else x
    py = 1 - y if m & 2 else y
    pc = 1 - c if m & 1 else c
    return px, py, pc


def _scatter_exchange(parts):
    n = len(parts)

    def tools(ins, outs, sems):
        send_sems, recv_sems, local_sems = sems
        x, y, c = _place()
        me = 4 * x + 2 * y + c
        def mine():
            return [pltpu.make_async_copy(ins[k].at[me], outs[k].at[me], local_sems.at[k]) for k in range(n)]

        def remote(dst_is_mine):
            out = []
            for m in range(1, N_DEV):
                px, py, pc = _peer(x, y, c, m)
                peer = 4 * px + 2 * py + pc
                for k in range(n):
                    out.append(pltpu.make_async_remote_copy(
                        src_ref=ins[k].at[peer], dst_ref=outs[k].at[me if dst_is_mine else peer],
                        send_sem=send_sems.at[k, m - 1], recv_sem=recv_sems.at[k, m - 1],
                        device_id=(px, py, pc), device_id_type=MESH_ID))
            return out

        return mine, remote

    def start(ins, outs, sems):
        mine, remote = tools(ins, outs, sems)
        for cp in mine() + remote(True):
            cp.start()

    def mid(ins, outs, sems):
        pass

    def finish(ins, outs, sems):
        mine, remote = tools(ins, outs, sems)
        for cp in remote(False):
            cp.wait_recv()
        for cp in remote(True):
            cp.wait_send()
        for cp in mine():
            cp.wait()

    return _Exchange(
        parts, [jax.ShapeDtypeStruct(a.shape, a.dtype) for a in parts],
        [pltpu.SemaphoreType.DMA((n, 7)), pltpu.SemaphoreType.DMA((n, 7)), pltpu.SemaphoreType.DMA((n,))],
        start, mid, finish)


N_CHIPS = N_DEV // 2


def _pair_exchange(parts):
    n = len(parts)

    def copies(ins, outs, sems):
        send_sems, recv_sems = sems
        x, y, c = _place()
        return [pltpu.make_async_remote_copy(
            src_ref=ins[k].at[2 * q + 1 - c], dst_ref=outs[k].at[q], send_sem=send_sems.at[k, q],
            recv_sem=recv_sems.at[k, q], device_id=(x, y, 1 - c), device_id_type=MESH_ID)
            for k in range(n) for q in range(N_CHIPS)]

    def start(ins, outs, sems):
        for cp in copies(ins, outs, sems):
            cp.start()

    def mid(ins, outs, sems):
        pass

    def finish(ins, outs, sems):
        for cp in copies(ins, outs, sems):
            cp.wait_recv()
        for cp in copies(ins, outs, sems):
            cp.wait_send()

    return _Exchange(
        parts, [jax.ShapeDtypeStruct((N_CHIPS,) + a.shape[1:], a.dtype) for a in parts],
        [pltpu.SemaphoreType.DMA((n, N_CHIPS)), pltpu.SemaphoreType.DMA((n, N_CHIPS))], start, mid, finish)


def _pair_add(parts, stage, name, copies=1):
    n = len(parts)
    core = lax.axis_index("c").reshape(1)

    def body(core_ref, *refs):
        for k in range(n):
            mine, theirs = refs[k], refs[n + k]
            total = (mine[0, 0].astype(F32) + theirs[0].astype(F32)).astype(BF16)
            for rep in range(copies):
                refs[(2 + rep) * n + k][0] = total

    in_specs, out_specs = [], []
    for a in parts:
        _, r, cdim = a.shape
        in_specs.append(pl.BlockSpec((1, 1, r // 2, cdim), lambda q, i, core_ref: (q, core_ref[0], i, 0)))
    for a in parts:
        _, r, cdim = a.shape
        in_specs.append(pl.BlockSpec((1, r // 2, cdim), lambda q, i, core_ref: (q, i, 0)))
        out_specs.append(pl.BlockSpec((1, r // 2, cdim), lambda q, i, core_ref: (q, i, 0)))
    return list(_call(
        body, name=name,
        grid_spec=pltpu.PrefetchScalarGridSpec(num_scalar_prefetch=1, grid=(N_CHIPS, 2), in_specs=in_specs,
                                               out_specs=out_specs * copies),
        out_shape=[jax.ShapeDtypeStruct(s.shape, BF16) for s in stage] * copies,
        compiler_params=_params("arbitrary", "arbitrary"),
    )(core, *[a.reshape((N_CHIPS, 2) + a.shape[1:]) for a in parts], *stage))


HBM = pl.BlockSpec(memory_space=pltpu.HBM)
SEMAPHORES = pl.BlockSpec(memory_space=pltpu.SEMAPHORE)


def _chip_copies(srcs, lands, send_sems, recv_sems, mine_is_dst):
    x, y, c = _place()
    me = 2 * x + y
    out = []
    for m in range(1, N_CHIPS):
        px, py, _ = _peer(x, y, c, 2 * m)
        peer = 2 * px + py
        for k in range(len(srcs)):
            pair = k * (N_CHIPS - 1) + m - 1
            out.append(pltpu.make_async_remote_copy(
                src_ref=srcs[k].at[peer], dst_ref=lands[k].at[me if mine_is_dst else peer],
                send_sem=send_sems[pair], recv_sem=recv_sems[pair],
                device_id=(px, py, c), device_id_type=MESH_ID))
    return out


def _chip_exchange_start(chip_parts, lands, after, name):
    n = len(chip_parts)
    pairs = n * (N_CHIPS - 1)

    def body(*refs):
        srcs, zones = refs[:n], refs[n:2 * n]
        sems = refs[2 * n + 1:2 * n + 1 + 2 * pairs]
        token = refs[-1]
        for cp in _chip_copies(srcs, zones, sems[:pairs], sems[pairs:], True):
            cp.start()
        token[...] = jnp.zeros(token.shape, F32)

    thru = [pltpu.HBM(a.shape, a.dtype) for a in chip_parts + lands]
    hbm = [pltpu.with_memory_space_constraint(a, pltpu.HBM) for a in chip_parts + lands]
    res = _call(
        body, name=name,
        out_shape=[pltpu.SemaphoreType.DMA(())] * (2 * pairs) + thru + [jax.ShapeDtypeStruct((8, 128), F32)],
        in_specs=[HBM] * (2 * n) + [ANY], out_specs=[SEMAPHORES] * (2 * pairs) + [HBM] * (2 * n) + [VMEM],
        input_output_aliases={i: 2 * pairs + i for i in range(2 * n)},
        compiler_params=pltpu.CompilerParams(has_side_effects=pltpu.SideEffectType.DATAFLOW_SIDE_EFFECTING),
    )(*hbm, after)
    sems, rest = list(res[:2 * pairs]), res[2 * pairs:]
    return sems, list(rest[:n]), list(rest[n:2 * n]), rest[-1]


def _chip_exchange_wait(sems, chip_parts, lands, after, name):
    n = len(chip_parts)
    pairs = n * (N_CHIPS - 1)

    def body(*refs):
        srcs, zones = refs[:n], refs[n:2 * n]
        sem_refs = refs[2 * n:2 * n + 2 * pairs]
        for cp in _chip_copies(srcs, zones, sem_refs[:pairs], sem_refs[pairs:], False):
            cp.wait_send()
            cp.wait_recv()

    thru = [pltpu.HBM(a.shape, a.dtype) for a in chip_parts + lands]
    res = _call(
        body, name=name, out_shape=thru,
        in_specs=[HBM] * (2 * n) + [SEMAPHORES] * (2 * pairs) + [ANY], out_specs=[HBM] * (2 * n),
        input_output_aliases={i: i for i in range(2 * n)},
        compiler_params=pltpu.CompilerParams(has_side_effects=pltpu.SideEffectType.DATAFLOW_SIDE_EFFECTING),
    )(*chip_parts, *lands, *sems, after)
    return list(res[n:])


def _run_exchange(ex, name, after=None):
    n_in, n_out = len(ex.ins), len(ex.out_shape)
    order = [] if after is None else [after]

    def body(*refs):
        ins, outs = refs[:n_in], refs[n_in + len(order):n_in + len(order) + n_out]
        sems = refs[n_in + len(order) + n_out:]
        ex.start(ins, outs, sems)
        ex.mid(ins, outs, sems)
        ex.finish(ins, outs, sems)

    return list(_call(body, name=name, out_shape=ex.out_shape, in_specs=[ANY] * (n_in + len(order)),
                      out_specs=[ANY] * n_out, scratch_shapes=ex.sems)(*ex.ins, *order))


def _call_hosting(body, ex, phases, *, in_specs, out_specs, out_shape, scratch_shapes, args, **kw):
    n_in, n_out, n_scr = len(in_specs), len(out_specs), len(scratch_shapes)
    if ex is None:
        res = _call(body, in_specs=in_specs, out_specs=out_specs, out_shape=out_shape,
                    scratch_shapes=scratch_shapes, **kw)(*args)
        return list(res), []
    n_xin, n_xout = len(ex.ins), len(ex.out_shape)

    def hosting(*refs):
        a, b = n_in, n_in + n_xin
        c, d = b + n_out, b + n_out + n_xout
        ins, xins, outs, xouts = refs[:a], refs[a:b], refs[b:c], refs[c:d]
        scr, sems = refs[d:d + n_scr], refs[d + n_scr:]
        first, mid, last = phases()

        @pl.when(first)
        def _():
            ex.start(xins, xouts, sems)

        body(*ins, *outs, *scr)

        @pl.when(mid)
        def _():
            ex.mid(xins, xouts, sems)

        @pl.when(last)
        def _():
            ex.finish(xins, xouts, sems)

    res = _call(hosting, in_specs=list(in_specs) + [ANY] * n_xin, out_specs=list(out_specs) + [ANY] * n_xout,
                out_shape=list(out_shape) + ex.out_shape, scratch_shapes=list(scratch_shapes) + ex.sems,
                **kw)(*args, *ex.ins)
    return list(res[:n_out]), list(res[n_out:])


def _grid_phases(dims, mid_fraction=0.8):
    total = 1
    for d in dims:
        total *= d
    mid = min(max(int(total * mid_fraction), 1), total - 1)

    def phases():
        step = pl.program_id(0)
        for axis in range(1, len(dims)):
            step = step * dims[axis] + pl.program_id(axis)
        return step == 0, step == mid, step == total - 1
    return phases


def _all_reduce_small(pack, name, after):
    rows = pack.shape[0]

    def body(in_ref, _, out_ref, recv, send_sems, recv_sems):
        x, y, c = _place()
        me = 4 * x + 2 * y + c
        recv[me] = in_ref[...]
        sent = []
        for m in range(1, N_DEV):
            px, py, pc = _peer(x, y, c, m)
            cp = pltpu.make_async_remote_copy(
                src_ref=in_ref, dst_ref=recv.at[me], send_sem=send_sems.at[m - 1], recv_sem=recv_sems.at[m - 1],
                device_id=(px, py, pc), device_id_type=MESH_ID)
            cp.start()
            sent.append(cp)
        for m in range(1, N_DEV):
            px, py, pc = _peer(x, y, c, m)
            peer = 4 * px + 2 * py + pc
            pltpu.make_async_remote_copy(
                src_ref=in_ref, dst_ref=recv.at[peer], send_sem=send_sems.at[m - 1], recv_sem=recv_sems.at[m - 1],
                device_id=(px, py, pc), device_id_type=MESH_ID).wait_recv()
        acc = recv[0]
        for d in range(1, N_DEV):
            acc = acc + recv[d]
        out_ref[...] = acc
        for cp in sent:
            cp.wait_send()

    return _call(
        body, name=name, out_shape=jax.ShapeDtypeStruct(pack.shape, F32), in_specs=[VMEM, ANY], out_specs=VMEM,
        scratch_shapes=[pltpu.VMEM((N_DEV, rows, pack.shape[1]), F32), pltpu.SemaphoreType.DMA((7,)),
                        pltpu.SemaphoreType.DMA((7,))],
        compiler_params=pltpu.CompilerParams(vmem_limit_bytes=VMEM_LIMIT),
    )(pack, after)


POOL_TS = 256


def _pool_counts(first_row, rows, win):
    t = first_row + lax.broadcasted_iota(jnp.int32, (rows, 1), 0)
    return jnp.minimum(t + 1, win).astype(F32)


def _pool_fwd(x, g, w, b, scale):
    nt = S // POOL_TS

    def body(x_ref, g_ref, w_ref, b_ref, s_ref, out_ref, diff_ref, ext):
        i = pl.program_id(0)

        @pl.when(i == 0)
        def _():
            ext[0:MAX_WIN, :] = jnp.zeros((MAX_WIN, D), F32)

        @pl.when(i > 0)
        def _():
            ext[0:MAX_WIN, :] = ext[POOL_TS:POOL_TS + MAX_WIN, :]

        xv = x_ref[...]
        h = xv * _rstd(xv) * g_ref[...]
        ext[MAX_WIN:, :] = h
        for gi in range(N_GROUPS):
            win = 2 << gi
            cols = slice(gi * GROUP, (gi + 1) * GROUP)
            sm = ext[:, cols]
            k = 1
            while k < win:
                sm = sm + pltpu.roll(sm, k, axis=0)
                k *= 2
            pooled = sm[MAX_WIN:, :] / _pool_counts(i * POOL_TS, POOL_TS, win)
            diff = (pooled - h[:, cols]).astype(BF16)
            yv = (_nn(diff, w_ref[gi]) + b_ref[:, cols]) * s_ref[:, cols]
            out_ref[:, cols] = xv[:, cols] + yv
            diff_ref[:, cols] = diff

    row = pl.BlockSpec((1, D), lambda i: (0, 0))
    tile = pl.BlockSpec((POOL_TS, D), lambda i: (i, 0))
    return _call(
        body, name="pool_fwd", grid=(nt,),
        in_specs=[tile, row, pl.BlockSpec((N_GROUPS, GROUP, GROUP), lambda i: (0, 0, 0)), row, row],
        out_specs=[tile, tile],
        out_shape=[jax.ShapeDtypeStruct((S, D), F32), jax.ShapeDtypeStruct((S, D), BF16)],
        scratch_shapes=[pltpu.VMEM((POOL_TS + MAX_WIN, D), F32)],
        compiler_params=_params("arbitrary"),
    )(x, g, w, b, scale)


def _pool_bwd(dy, x, diff, g, w, b, scale, ex=None):
    nt = S // POOL_TS

    def body(dy_ref, x_ref, diff_ref, g_ref, w_ref, b_ref, s_ref, gx_ref, dw_ref, db_ref, ds_ref, dg_ref, ext, dh):
        i = pl.program_id(0)
        first_row = (nt - 1 - i) * POOL_TS

        @pl.when(i == 0)
        def _():
            ext[POOL_TS:, :] = jnp.zeros((MAX_WIN, D), F32)
            dw_ref[...] = jnp.zeros(dw_ref.shape, F32)
            db_ref[...] = jnp.zeros(db_ref.shape, F32)
            ds_ref[...] = jnp.zeros(ds_ref.shape, F32)
            dg_ref[...] = jnp.zeros(dg_ref.shape, F32)

        @pl.when(i > 0)
        def _():
            ext[POOL_TS:, :] = ext[0:MAX_WIN, :]

        dyv = dy_ref[...]
        for gi in range(N_GROUPS):
            win = 2 << gi
            cols = slice(gi * GROUP, (gi + 1) * GROUP)
            dfb = diff_ref[:, cols]
            z = _nn(dfb, w_ref[gi]) + b_ref[:, cols]
            dyg = dyv[:, cols]
            ds_ref[:, cols] += _colsum8(dyg * z)
            dz = dyg * s_ref[:, cols]
            db_ref[:, cols] += _colsum8(dz)
            dzb = dz.astype(BF16)
            dw_ref[gi] += _tn(dfb, dzb)
            ddiff = _nt(dzb, w_ref[gi])
            ext[0:POOL_TS, cols] = ddiff / _pool_counts(first_row, POOL_TS, win)
            sm = ext[:, cols]
            k = 1
            while k < win:
                sm = sm + pltpu.roll(sm, POOL_TS + MAX_WIN - k, axis=0)
                k *= 2
            dh[:, cols] = sm[0:POOL_TS, :] - ddiff
        xv = x_ref[...]
        r = _rstd(xv)
        gv = g_ref[...]
        dhv = dh[...]
        dg_ref[...] += _colsum8(dhv * xv * r)
        gx_ref[...] = dyv + _rms_bwd(dhv, xv, r, gv)

    row = pl.BlockSpec((1, D), lambda i: (0, 0))
    tile = pl.BlockSpec((POOL_TS, D), lambda i: (nt - 1 - i, 0))
    acc = pl.BlockSpec((8, D), lambda i: (0, 0))
    wspec = pl.BlockSpec((N_GROUPS, GROUP, GROUP), lambda i: (0, 0, 0))
    return _call_hosting(
        body, ex, _grid_phases((nt,)), name="pool_bwd", grid=(nt,),
        in_specs=[tile, tile, tile, row, wspec, row, row],
        out_specs=[tile, wspec, acc, acc, acc],
        out_shape=[jax.ShapeDtypeStruct((S, D), F32), jax.ShapeDtypeStruct((N_GROUPS, GROUP, GROUP), F32),
                   jax.ShapeDtypeStruct((8, D), F32), jax.ShapeDtypeStruct((8, D), F32),
                   jax.ShapeDtypeStruct((8, D), F32)],
        scratch_shapes=[pltpu.VMEM((POOL_TS + MAX_WIN, D), F32), pltpu.VMEM((POOL_TS, D), F32)],
        args=(dy, x, diff, g, w, b, scale), compiler_params=_params("arbitrary"),
    )


FFN_TS = min(S, 1024)
FFN_TF = 256


def _ffn_fwd(x, g, wg_t, wu_t, wd, name, target=None, ex=None):
    width = wd.shape[0]
    ni, nj = S // FFN_TS, width // FFN_TF
    with_loss = target is not None
    extra = [target] if with_loss else []

    def body(*refs):
        x_ref, g_ref, wg_ref, wu_ref, wd_ref = refs[:5]
        if with_loss:
            t_ref, out_ref, loss_ref, h_ref, gg_ref, uu_ref, hs, acc = refs[5:]
        else:
            out_ref, h_ref, gg_ref, uu_ref, hs, acc = refs[5:]
        i, j = pl.program_id(0), pl.program_id(1)

        @pl.when(j == 0)
        def _():
            xv = x_ref[...]
            hb = (xv * _rstd(xv) * g_ref[...]).astype(BF16)
            hs[...] = hb
            h_ref[...] = hb
            acc[...] = jnp.zeros(acc.shape, F32)

        hb = hs[...]
        gg = _nt(hb, wg_ref[...])
        uu = _nt(hb, wu_ref[...])
        gg_ref[...] = gg
        uu_ref[...] = uu
        a = (gg * jax.nn.sigmoid(gg) * uu).astype(BF16)
        acc[...] += _nn(a, wd_ref[...])

        @pl.when(j == nj - 1)
        def _():
            yv = x_ref[...] + acc[...]
            if with_loss:
                err = yv - t_ref[...]
                out_ref[...] = err * (1.0 / D)
                part = jnp.sum(err * err) * (0.5 / D)

                @pl.when(i == 0)
                def _():
                    loss_ref[...] = jnp.zeros(loss_ref.shape, F32)

                loss_ref[...] += jnp.broadcast_to(part, loss_ref.shape)
            else:
                out_ref[...] = yv

    xt = pl.BlockSpec((FFN_TS, D), lambda i, j: (i, 0))
    row = pl.BlockSpec((1, D), lambda i, j: (0, 0))
    wt = pl.BlockSpec((FFN_TF, D), lambda i, j: (j, 0))
    gt = pl.BlockSpec((FFN_TS, FFN_TF), lambda i, j: (i, j))
    in_specs = [xt, row, wt, wt, wt] + [xt] * len(extra)
    out_specs = [xt] + ([pl.BlockSpec((8, 128), lambda i, j: (0, 0))] if with_loss else []) + [xt, gt, gt]
    out_shape = ([jax.ShapeDtypeStruct((S, D), F32)] + ([jax.ShapeDtypeStruct((8, 128), F32)] if with_loss else [])
                 + [jax.ShapeDtypeStruct((S, D), BF16), jax.ShapeDtypeStruct((S, width), F32),
                    jax.ShapeDtypeStruct((S, width), F32)])
    args = (x, g, wg_t, wu_t, wd, *extra)
    return _call_hosting(
        body, ex, _grid_phases((ni, nj), 0.8), name=name, grid=(ni, nj),
        in_specs=in_specs, out_specs=out_specs, out_shape=out_shape,
        scratch_shapes=[pltpu.VMEM((FFN_TS, D), BF16), pltpu.VMEM((FFN_TS, D), F32)], args=args,
        compiler_params=_params("arbitrary", "arbitrary"),
    )


def _ffn_bwd_weights(dout, h, gg, uu, wd, name, ex=None):
    width = wd.shape[0]
    nj = width // FFN_TF

    def body(do_ref, h_ref, gg_ref, uu_ref, wd_ref, dg_ref, du_ref, dwg_ref, dwu_ref, dwd_ref, dob):
        @pl.when(pl.program_id(0) == 0)
        def _():
            dob[...] = do_ref[...].astype(BF16)

        hb, dov = h_ref[...], dob[...]
        gv, uv = gg_ref[...], uu_ref[...]
        da = _nt(dov, wd_ref[...])
        sg = jax.nn.sigmoid(gv)
        sl = gv * sg
        ab = (sl * uv).astype(BF16)
        dub = (da * sl).astype(BF16)
        dgb = (da * uv * (sg * (1.0 + gv * (1.0 - sg)))).astype(BF16)
        dg_ref[...] = dgb
        du_ref[...] = dub
        dwg_ref[...] = _tn(dgb, hb).astype(BF16)
        dwu_ref[...] = _tn(dub, hb).astype(BF16)
        dwd_ref[...] = _tn(ab, dov).astype(BF16)

    once = pl.Buffered(1)
    whole = lambda: pl.BlockSpec((S, D), lambda j, _: (0, 0), pipeline_mode=once)
    wt = pl.BlockSpec((FFN_TF, D), lambda j, _: (j, 0))
    gt = pl.BlockSpec((S, FFN_TF), lambda j, _: (0, j))
    return _call_hosting(
        body, ex, _grid_phases((nj, 1)), name=name, grid=(nj, 1),
        in_specs=[whole(), whole(), gt, gt, wt], out_specs=[gt, gt, wt, wt, wt],
        out_shape=[jax.ShapeDtypeStruct((S, width), BF16)] * 2 + [jax.ShapeDtypeStruct((width, D), BF16)] * 3,
        scratch_shapes=[pltpu.VMEM((S, D), BF16)], args=(dout, h, gg, uu, wd),
        compiler_params=_params("arbitrary", "arbitrary"),
    )


BWD_TS = 512


def _ffn_bwd_input(dres, x, g, grads, weights, name, ex=None):
    nt = S // BWD_TS
    n = len(grads)

    def body(*refs):
        dres_ref, x_ref, g_ref = refs[:3]
        grad_refs, w_refs = refs[3:3 + n], refs[3 + n:3 + 2 * n]
        dx_ref, dgam_ref = refs[3 + 2 * n:]
        dh = _nn(grad_refs[0][...], w_refs[0][...])
        for k in range(1, n):
            dh = dh + _nn(grad_refs[k][...], w_refs[k][...])
        xv = x_ref[...]
        r = _rstd(xv)
        dx_ref[...] = dres_ref[...] + _rms_bwd(dh, xv, r, g_ref[...])

        @pl.when(pl.program_id(0) == 0)
        def _():
            dgam_ref[...] = jnp.zeros(dgam_ref.shape, F32)

        dgam_ref[...] += _colsum8(dh * xv * r)

    tile = pl.BlockSpec((BWD_TS, D), lambda i: (i, 0))
    ftiles = [pl.BlockSpec((BWD_TS, a.shape[1]), lambda i: (i, 0)) for a in grads]
    wspecs = [pl.BlockSpec(w.shape, lambda i: (0, 0), pipeline_mode=pl.Buffered(1)) for w in weights]
    return _call_hosting(
        body, ex, _grid_phases((nt,)), name=name, grid=(nt,),
        in_specs=[tile, tile, pl.BlockSpec((1, D), lambda i: (0, 0))] + ftiles + wspecs,
        out_specs=[tile, pl.BlockSpec((8, D), lambda i: (0, 0))],
        out_shape=[jax.ShapeDtypeStruct((S, D), F32), jax.ShapeDtypeStruct((8, D), F32)],
        scratch_shapes=[], args=(dres, x, g, *grads, *weights), compiler_params=_params("arbitrary"),
    )


def _mm(a, b, mode, out_dtype, name, add=None, skip_rows=0):
    if mode == "nn":
        (m, kd), n = (a.shape[0] - skip_rows, a.shape[1]), b.shape[1]
    elif mode == "nt":
        (m, kd), n = (a.shape[0] - skip_rows, a.shape[1]), b.shape[0]
    else:
        (kd, m), n = a.shape, b.shape[1]
    tm, tn, tk = min(m, 1024), min(n, 1024), min(kd, 1024)
    if skip_rows and mode == "tn":
        tk = min(tk, skip_rows)
    elif skip_rows:
        tm = min(tm, skip_rows)
    skip_a = skip_rows // tm if mode != "tn" else 0
    skip_b = skip_rows // tk if mode == "tn" else 0
    assert skip_rows == skip_a * tm + skip_b * tk
    nk = kd // tk
    dot = {"nn": _nn, "nt": _nt, "tn": _tn}[mode]

    def body(*refs):
        if add is None:
            a_ref, b_ref, o_ref, acc = refs
        else:
            a_ref, b_ref, add_ref, o_ref, acc = refs
        k = pl.program_id(2)

        @pl.when(k == 0)
        def _():
            acc[...] = jnp.zeros(acc.shape, F32)

        acc[...] += dot(a_ref[...].astype(BF16), b_ref[...].astype(BF16))

        @pl.when(k == nk - 1)
        def _():
            res = acc[...]
            if add is not None:
                res = res + add_ref[...]
            o_ref[...] = res.astype(out_dtype)

    if mode == "tn":
        a_spec = pl.BlockSpec((tk, tm), lambda i, j, k: (k, i))
        b_spec = pl.BlockSpec((tk, tn), lambda i, j, k: (k + skip_b, j))
    else:
        a_spec = pl.BlockSpec((tm, tk), lambda i, j, k: (i + skip_a, k))
        b_spec = (pl.BlockSpec((tk, tn), lambda i, j, k: (k, j)) if mode == "nn"
                  else pl.BlockSpec((tn, tk), lambda i, j, k: (j, k)))
    o_spec = pl.BlockSpec((tm, tn), lambda i, j, k: (i, j))
    in_specs = [a_spec, b_spec] + ([o_spec] if add is not None else [])
    args = (a, b) + ((add,) if add is not None else ())
    return _call(
        body, name=name, grid=(m // tm, n // tn, nk), in_specs=in_specs, out_specs=o_spec,
        out_shape=jax.ShapeDtypeStruct((m, n), out_dtype), scratch_shapes=[pltpu.VMEM((tm, tn), F32)],
        compiler_params=_params("parallel", "parallel", "arbitrary"),
    )(*args)


PROJ_TS = 256


def _kvq_proj(x, g_kv, g_mix, wk, wv, wq, gk, gq, ex=None):
    lead = PADK // PROJ_TS

    def body(x_ref, gkv_ref, gmix_ref, wk_ref, wv_ref, wq_ref, gk_ref, gq_ref,
             hkv_ref, h1_ref, kpre_ref, qpre_ref, k_ref, v_ref, q_ref):
        i = pl.program_id(0)

        @pl.when(i < lead)
        def _():
            k_ref[...] = jnp.zeros(k_ref.shape, BF16)
            v_ref[...] = jnp.zeros(v_ref.shape, BF16)

        @pl.when(i >= lead)
        def _():
            xv = x_ref[...]
            xr = xv * _rstd(xv)
            hkv = (xr * gkv_ref[...]).astype(BF16)
            h1 = (xr * gmix_ref[...]).astype(BF16)
            hkv_ref[...] = hkv
            h1_ref[...] = h1
            kpre = _nn(hkv, wk_ref[...])
            qpre = _nn(h1, wq_ref[...])
            kpre_ref[...] = kpre
            qpre_ref[...] = qpre
            v_ref[...] = _nn(hkv, wv_ref[...]).astype(BF16)
            rk = lax.rsqrt(_seg_sum(kpre * kpre) * (1.0 / HEAD) + EPS)
            k_ref[...] = (kpre * rk * gk_ref[...]).astype(BF16)
            rq = lax.rsqrt(_seg_sum(qpre * qpre) * (1.0 / HEAD) + EPS)
            q_ref[...] = (qpre * rq * gq_ref[...]).astype(BF16)

    tile = pl.BlockSpec((PROJ_TS, D), lambda i: (jnp.maximum(i - lead, 0), 0))
    padded = pl.BlockSpec((PROJ_TS, D), lambda i: (i, 0))
    row = pl.BlockSpec((1, D), lambda i: (0, 0))
    wspec = pl.BlockSpec((D, D), lambda i: (0, 0))
    bf = jax.ShapeDtypeStruct((S, D), BF16)
    ff = jax.ShapeDtypeStruct((S, D), F32)
    bp = jax.ShapeDtypeStruct((PADK + S, D), BF16)
    return _call_hosting(
        body, ex, _grid_phases((lead + S // PROJ_TS,), 0.85), name="kvq_proj", grid=(lead + S // PROJ_TS,),
        in_specs=[tile, row, row, wspec, wspec, wspec, row, row],
        out_specs=[tile, tile, tile, tile, padded, padded, tile], out_shape=[bf, bf, ff, ff, bp, bp, bf],
        scratch_shapes=[], args=(x, g_kv, g_mix, wk, wv, wq, gk, gq), compiler_params=_params("arbitrary"),
    )


def _qkv_input_bwd(dq, qpre, gq, dkp, kpre, gk, dvp, wq, wk, wv, dres, x, g_mix, g_kv, ex=None):
    nt = S // PROJ_TS
    lead = PADK // PROJ_TS

    def head_bwd(dov, pv, hgv):
        r = lax.rsqrt(_seg_sum(pv * pv) * (1.0 / HEAD) + EPS)
        gd = dov * hgv
        dpre = r * gd - pv * (r * r * r) * (_seg_sum(gd * pv) * (1.0 / HEAD))
        return dpre.astype(BF16), _colsum8(dov * pv * r)

    def fold_heads(full):
        fold = full[:, 0:128]
        for blk in range(1, D // 128):
            fold = fold + full[:, blk * 128:(blk + 1) * 128]
        return fold + pltpu.roll(fold, HEAD, axis=1)

    def body(dq_ref, qpre_ref, gq_ref, dk_ref, kpre_ref, gk_ref, dv_ref, wq_ref, wk_ref, wv_ref, dres_ref, x_ref,
             gmix_ref, gkv_ref, dx_ref, dqpre_ref, dkpre_ref, dgmix_ref, dgkv_ref, dgq_ref, dgk_ref, accq, acck):
        i = pl.program_id(0)

        @pl.when(i == 0)
        def _():
            accq[...] = jnp.zeros(accq.shape, F32)
            acck[...] = jnp.zeros(acck.shape, F32)
            dgmix_ref[...] = jnp.zeros(dgmix_ref.shape, F32)
            dgkv_ref[...] = jnp.zeros(dgkv_ref.shape, F32)

        dqb, cq = head_bwd(dq_ref[...], qpre_ref[...], gq_ref[...])
        dkb, ck = head_bwd(dk_ref[...], kpre_ref[...], gk_ref[...])
        dqpre_ref[...] = dqb
        dkpre_ref[...] = dkb
        accq[...] += cq
        acck[...] += ck
        dh1 = _nt(dqb, wq_ref[...])
        dhkv = _nt(dkb, wk_ref[...]) + _nt(dv_ref[...].astype(BF16), wv_ref[...])
        xv = x_ref[...]
        r = _rstd(xv)
        dx_ref[...] = dres_ref[...] + _rms_bwd(dh1, xv, r, gmix_ref[...]) + _rms_bwd(dhkv, xv, r, gkv_ref[...])
        dgmix_ref[...] += _colsum8(dh1 * xv * r)
        dgkv_ref[...] += _colsum8(dhkv * xv * r)

        @pl.when(i == nt - 1)
        def _():
            dgq_ref[...] = fold_heads(accq[...])
            dgk_ref[...] = fold_heads(acck[...])

    tile = pl.BlockSpec((PROJ_TS, D), lambda i: (i, 0))
    behind = pl.BlockSpec((PROJ_TS, D), lambda i: (i + lead, 0))
    row = pl.BlockSpec((1, D), lambda i: (0, 0))
    wspec = lambda: pl.BlockSpec((D, D), lambda i: (0, 0), pipeline_mode=pl.Buffered(1))
    acc = pl.BlockSpec((8, D), lambda i: (0, 0))
    small = pl.BlockSpec((8, 128), lambda i: (0, 0))
    bf = jax.ShapeDtypeStruct((S, D), BF16)
    return _call_hosting(
        body, ex, _grid_phases((nt,)), name="qkv_input_bwd", grid=(nt,),
        in_specs=[tile, tile, row, behind, tile, row, behind, wspec(), wspec(), wspec(), tile, tile, row, row],
        out_specs=[tile, tile, tile, acc, acc, small, small],
        out_shape=[jax.ShapeDtypeStruct((S, D), F32), bf, bf, jax.ShapeDtypeStruct((8, D), F32),
                   jax.ShapeDtypeStruct((8, D), F32), jax.ShapeDtypeStruct((8, 128), F32),
                   jax.ShapeDtypeStruct((8, 128), F32)],
        scratch_shapes=[pltpu.VMEM((8, D), F32), pltpu.VMEM((8, D), F32)],
        args=(dq, qpre, gq, dkp, kpre, gk, dvp, wq, wk, wv, dres, x, g_mix, g_kv),
        compiler_params=_params("arbitrary"),
    )


def _toeplitz_from_table(table):
    far = jnp.broadcast_to(table[:, N_REL - 1:], (N_HEADS, PADK - MAX_REL + 1))
    near = table[:, N_REL - 2::-1]
    past = jnp.broadcast_to(table[:, 0:1], (N_HEADS, MAX_REL))
    wrap = jnp.broadcast_to(table[:, N_REL - 1:], (N_HEADS, TOEP - (PADK + 2 * MAX_REL + 1)))
    return jnp.concatenate([far, near, past, wrap], axis=1).reshape(N_HEADS, 1, TOEP)


def _table_grad_from_toeplitz(dtp, seg):
    lo = PADK - MAX_REL + 1
    near = dtp[:, lo + N_REL - 3:lo - 1:-1]
    return jnp.concatenate([seg[:, 1:2], near, seg[:, 0:1]], axis=1)


def _bias_band(tp):
    def body(tp_ref, out_ref):
        bv = pltpu.roll(jnp.broadcast_to(tp_ref[0], (QB, TOEP)), 0, axis=1, stride=1, stride_axis=0)
        out_ref[0] = jnp.where(_band_mask(), bv[:, 0:KB], NEG_INF)

    return _call(
        body, name="bias_band", grid=(N_HEADS,),
        in_specs=[pl.BlockSpec((1, 1, TOEP), lambda h: (h, 0, 0))],
        out_specs=pl.BlockSpec((1, QB, KB), lambda h: (h, 0, 0)),
        out_shape=jax.ShapeDtypeStruct((N_HEADS, QB, KB), F32),
        compiler_params=_params("parallel"),
    )(tp)


def _bias_grad(dband, after):
    lo, hi = PADK - MAX_REL + 1, PADK + MAX_REL

    def body(db_ref, _, dtp_ref, seg_ref):
        bv = jnp.concatenate([db_ref[0], jnp.zeros((QB, TOEP - KB), F32)], axis=1)
        row = lax.broadcasted_iota(jnp.int32, (QB, TOEP), 0)
        k = 1
        while k < QB:
            bv = jnp.where((row & k) != 0, pltpu.roll(bv, TOEP - k, axis=1), bv)
            k *= 2
        col = jnp.sum(bv, axis=0, keepdims=True)
        dtp_ref[0] = col
        u = lax.broadcasted_iota(jnp.int32, (1, TOEP), 1)
        far = jnp.sum(jnp.where((u < lo) | (u > hi + MAX_REL), col, 0.0))
        past = jnp.sum(jnp.where((u >= hi) & (u <= hi + MAX_REL), col, 0.0))
        lane = lax.broadcasted_iota(jnp.int32, (1, 128), 1)
        seg_ref[0] = jnp.where(lane == 0, far, jnp.where(lane == 1, past, 0.0))

    return _call(
        body, name="bias_grad", grid=(N_HEADS,),
        in_specs=[pl.BlockSpec((1, QB, KB), lambda h: (h, 0, 0)), ANY],
        out_specs=[pl.BlockSpec((1, 1, TOEP), lambda h: (h, 0, 0)), pl.BlockSpec((1, 1, 128), lambda h: (h, 0, 0))],
        out_shape=[jax.ShapeDtypeStruct((N_HEADS, 1, TOEP), F32), jax.ShapeDtypeStruct((N_HEADS, 1, 128), F32)],
        compiler_params=_params("parallel"),
    )(dband, after)


N_QB = S // QB
HEADS_PER_STEP = 4
ATT_LANES = HEADS_PER_STEP * HEAD
N_HG = D // ATT_LANES


def _band_mask():
    qc = lax.broadcasted_iota(jnp.int32, (QB, KB), 0) // CHUNK
    kc = lax.broadcasted_iota(jnp.int32, (QB, KB), 1) // CHUNK
    return (kc >= qc) & (kc <= qc + LEFT)


def _half_scale(hh, scale):
    lane = lax.broadcasted_iota(jnp.int32, (1, 128), 1)
    return jnp.where((lane < HEAD) == (hh == 0), scale, 0.0).astype(BF16)


def _probs(qh, kb, bias, first_key):
    sc = _nt(qh, kb) + bias
    if first_key is not None:
        sc = jnp.where(lax.broadcasted_iota(jnp.int32, (QB, KB), 1) >= first_key, sc, NEG_INF)
    e = jnp.exp(sc - jnp.max(sc, axis=-1, keepdims=True))
    return e * (1.0 / jnp.sum(e, axis=-1, keepdims=True))


def _by_padding(cb, compute):
    @pl.when(cb < PADK // QB)
    def _():
        compute(PADK - cb * QB)

    @pl.when(cb >= PADK // QB)
    def _():
        compute(None)


def _attn_fwd(q, kp, vp, bias, ex=None):
    def body(q_ref, k_ref, v_ref, b_ref, o_ref):
        cb = pl.program_id(1)
        band = pl.ds(pl.multiple_of(cb * QB, QB), KB)
        low = lax.broadcasted_iota(jnp.int32, (QB, 128), 1) < HEAD

        def compute(first_key):
            for pair in range(HEADS_PER_STEP // 2):
                lanes = pl.ds(pair * 128, 128)
                kb, vb, qv = k_ref[band, lanes], v_ref[band, lanes], q_ref[:, lanes]
                outs = []
                for hh in range(2):
                    pb = _probs(qv * _half_scale(hh, ATTN_SCALE), kb, b_ref[2 * pair + hh], first_key).astype(BF16)
                    outs.append(_nn(pb, vb))
                o_ref[:, lanes] = jnp.where(low, outs[0], outs[1]).astype(BF16)

        _by_padding(cb, compute)

    qspec = pl.BlockSpec((QB, ATT_LANES), lambda hg, cb: (cb, hg))
    kspec = pl.BlockSpec((PADK + S, ATT_LANES), lambda hg, cb: (0, hg))
    return _call_hosting(
        body, ex, _grid_phases((N_HG, N_QB), 0.85), name="attn_fwd", grid=(N_HG, N_QB),
        in_specs=[qspec, kspec, kspec, pl.BlockSpec((HEADS_PER_STEP, QB, KB), lambda hg, cb: (hg, 0, 0))],
        out_specs=[qspec], out_shape=[jax.ShapeDtypeStruct((S, D), BF16)], scratch_shapes=[],
        args=(q, kp, vp, bias), compiler_params=_params("arbitrary", "arbitrary"),
    )


def _attn_bwd(q, kp, vp, bias, do, ex=None):
    def body(q_ref, k_ref, v_ref, b_ref, do_ref, dq_ref, dk_ref, dv_ref, db_ref):
        cb = pl.program_id(1)

        @pl.when(cb == 0)
        def _():
            dk_ref[...] = jnp.zeros(dk_ref.shape, F32)
            dv_ref[...] = jnp.zeros(dv_ref.shape, F32)
            db_ref[...] = jnp.zeros(db_ref.shape, F32)

        band = pl.ds(pl.multiple_of(cb * QB, QB), KB)
        low = lax.broadcasted_iota(jnp.int32, (QB, 128), 1) < HEAD

        def compute(first_key):
            for pair in range(HEADS_PER_STEP // 2):
                lanes = pl.ds(pair * 128, 128)
                kb, vb = k_ref[band, lanes], v_ref[band, lanes]
                qv, dov = q_ref[:, lanes], do_ref[:, lanes]
                dq = jnp.zeros((QB, 128), F32)
                dkb = jnp.zeros((KB, 128), F32)
                dvb = jnp.zeros((KB, 128), F32)
                for hh in range(2):
                    sel = low if hh == 0 else jnp.logical_not(low)
                    doh = dov * _half_scale(hh, 1.0)
                    p = _probs(qv * _half_scale(hh, ATTN_SCALE), kb, b_ref[2 * pair + hh], first_key)
                    dp = _nt(doh, vb)
                    dvb = dvb + _tn(p.astype(BF16), doh)
                    ds = p * (dp - jnp.sum(dp * p, axis=-1, keepdims=True))
                    db_ref[2 * pair + hh] += ds
                    dsb = (ds * ATTN_SCALE).astype(BF16)
                    dq = dq + jnp.where(sel, _nn(dsb, kb), 0.0)
                    dkb = dkb + _tn(dsb, qv * _half_scale(hh, 1.0))
                dq_ref[:, lanes] = dq
                dk_ref[band, lanes] += dkb
                dv_ref[band, lanes] += dvb

        _by_padding(cb, compute)

    qspec = pl.BlockSpec((QB, ATT_LANES), lambda hg, cb: (cb, hg))
    kspec = pl.BlockSpec((PADK + S, ATT_LANES), lambda hg, cb: (0, hg))
    bspec = pl.BlockSpec((HEADS_PER_STEP, QB, KB), lambda hg, cb: (hg, 0, 0))
    kf = jax.ShapeDtypeStruct((PADK + S, D), F32)
    return _call_hosting(
        body, ex, _grid_phases((N_HG, N_QB)), name="attn_bwd", grid=(N_HG, N_QB),
        in_specs=[qspec, kspec, kspec, bspec, qspec],
        out_specs=[qspec, kspec, kspec, bspec],
        out_shape=[jax.ShapeDtypeStruct((S, D), F32), kf, kf, jax.ShapeDtypeStruct((N_HEADS, QB, KB), F32)],
        scratch_shapes=[], args=(q, kp, vp, bias, do), compiler_params=_params("arbitrary", "arbitrary"),
    )


def _adam_math(w, g, m, v):
    m = ADAM_B1 * m + (1.0 - ADAM_B1) * g
    v = ADAM_B2 * v + (1.0 - ADAM_B2) * (g * g)
    m_hat = m / (1.0 - ADAM_B1 ** ADAM_STEP)
    v_hat = v / (1.0 - ADAM_B2 ** ADAM_STEP)
    delta = -ADAM_LR * (m_hat / (jnp.sqrt(v_hat) + ADAM_EPS) + ADAM_WD * w)
    return delta, m, v


def _sum_parts(parts, name, after=None):
    count, r, c = parts.shape
    tr = r // 2 if r % 16 == 0 and r > 64 else r

    def body(p_ref, *rest):
        o_ref = rest[-1]
        acc = p_ref[0].astype(F32)
        for d in range(1, count):
            acc = acc + p_ref[d].astype(F32)
        o_ref[...] = acc

    return _call(
        body, name=name, grid=(r // tr,),
        in_specs=[pl.BlockSpec((count, tr, c), lambda i: (0, i, 0))] + ([ANY] if after is not None else []),
        out_specs=pl.BlockSpec((tr, c), lambda i: (i, 0)),
        out_shape=jax.ShapeDtypeStruct((r, c), F32), compiler_params=_params("parallel"),
    )(parts, *([after] if after is not None else []))


def _row_tile(r):
    for cand in (256, 176, 128, 64, 32, 16, 8):
        if r % cand == 0:
            return cand
    return r


def _adam_layer(w, g, m, v, layer, name, prev=None):
    nl, r, c = w.shape
    tr = _row_tile(r)

    def body(w_ref, g_ref, m_ref, v_ref, *rest):
        go_ref, d_ref, nm_ref, nv_ref = rest[-4:]
        gv = g_ref[...]
        delta, nm, nv = _adam_math(w_ref[0], gv, m_ref[0], v_ref[0])
        go_ref[0] = gv
        d_ref[0] = delta
        nm_ref[0] = nm
        nv_ref[0] = nv

    lspec = pl.BlockSpec((1, tr, c), lambda i: (layer, i, 0))
    sd = jax.ShapeDtypeStruct((nl, r, c), F32)
    extra = list(prev) if prev is not None else []
    return _call(
        body, name=name, grid=(r // tr,),
        in_specs=[lspec, pl.BlockSpec((tr, c), lambda i: (i, 0)), lspec, lspec] + [ANY] * len(extra),
        out_specs=[lspec] * 4, out_shape=[sd] * 4,
        input_output_aliases={4 + t: t for t in range(len(extra))},
        compiler_params=_params("parallel"),
    )(w, g, m, v, *extra)


def _adam(w, g, m, v, name):
    r, c = w.shape
    tr = _row_tile(r)

    def body(w_ref, g_ref, m_ref, v_ref, d_ref, nm_ref, nv_ref):
        delta, nm, nv = _adam_math(w_ref[...], g_ref[...], m_ref[...], v_ref[...])
        d_ref[...] = delta
        nm_ref[...] = nm
        nv_ref[...] = nv

    spec = pl.BlockSpec((tr, c), lambda i: (i, 0))
    sd = jax.ShapeDtypeStruct((r, c), F32)
    return _call(
        body, name=name, grid=(r // tr,), in_specs=[spec] * 4, out_specs=[spec] * 3, out_shape=[sd] * 3,
        compiler_params=_params("parallel"),
    )(w, g, m, v)


def _pad8(rows):
    return jnp.pad(rows, ((0, 8 - rows.shape[0]), (0, 0)))


def kernel(x, norm_mix_g, norm_ffn_g, pool_w, pool_b, pool_scale, kv_norm_g, w_k, w_v, k_norm_g, w_q, q_norm_g, rel_bias, w_o, w_gate, w_up, w_down, loss_target, m_norm_mix_g, m_norm_ffn_g, m_pool_w, m_pool_b, m_pool_scale, m_kv_norm_g, m_w_k, m_w_v, m_k_norm_g, m_w_q, m_q_norm_g, m_rel_bias, m_w_o, m_w_gate, m_w_up, m_w_down, v_norm_mix_g, v_norm_ffn_g, v_pool_w, v_pool_b, v_pool_scale, v_kv_norm_g, v_w_k, v_w_v, v_k_norm_g, v_w_q, v_q_norm_g, v_rel_bias, v_w_o, v_w_gate, v_w_up, v_w_down):
    assert x.shape == (1, S, D) and w_gate.shape == (2, D, F_SHARD) and w_k.shape == (D_SHARD, D)
    xin, target = x[0], loss_target[0]

    ffn_shards = [[w_gate[layer].T.astype(BF16), w_up[layer].T.astype(BF16), w_down[layer].astype(BF16)]
                  for layer in range(2)]
    att_shards = [w_k.astype(BF16), w_v.astype(BF16), w_q[0].astype(BF16), w_o[0].astype(BF16)]
    pool_shard = pool_w[0].astype(BF16).reshape(N_GROUPS * POOL_SHARD, GROUP)
    small = jnp.concatenate([pool_b[0].reshape(1, N_GROUPS * POOL_SHARD), pool_scale], axis=1)

    full0 = _run_exchange(_gather_exchange(ffn_shards[0] + [pool_shard, _pad8(small)]), "gather_layer0")
    ffn_w0 = [a.reshape(F, D) for a in full0[:3]]
    pw_f = full0[3].reshape(N_DEV, N_GROUPS, POOL_SHARD, GROUP).transpose(1, 0, 2, 3).reshape(N_GROUPS, GROUP, GROUP)
    small_f = full0[4][:, 0, :]
    pb_f = small_f[:, :N_GROUPS * POOL_SHARD].reshape(N_DEV, N_GROUPS, POOL_SHARD).transpose(1, 0, 2).reshape(1, D)
    ps_f = small_f[:, N_GROUPS * POOL_SHARD:].reshape(1, D)

    g_mix0, g_mix1 = norm_mix_g[0:1], norm_mix_g[1:2]
    g_ffn0, g_ffn1 = norm_ffn_g[0:1], norm_ffn_g[1:2]
    g_kv = kv_norm_g.reshape(1, D)
    gk_t = jnp.tile(k_norm_g.reshape(1, HEAD), (1, N_HEADS))
    gq_t = jnp.tile(q_norm_g.reshape(1, HEAD), (1, N_HEADS))

    x1, diff = _pool_fwd(xin, g_mix0, pw_f, pb_f, ps_f)
    (x2, hf0, gg0, uu0), full_att = _ffn_fwd(x1, g_ffn0, *ffn_w0, name="ffn_fwd", ex=_gather_exchange(att_shards))
    wk_f, wv_f, wq_f, wo_f = [a.reshape(D, D) for a in full_att]
    (hkv, h1, kpre, qpre, kp, vp, qq), full1_gate = _kvq_proj(x2, g_kv, g_mix1, wk_f, wv_f, wq_f, gk_t, gq_t,
                                                                ex=_gather_exchange(ffn_shards[1][:1]))
    bias = _bias_band(_toeplitz_from_table(rel_bias[0]))
    (att,), full1_rest = _attn_fwd(qq, kp, vp, bias, ex=_gather_exchange(ffn_shards[1][1:]))
    ffn_w1 = [a.reshape(F, D) for a in full1_gate + full1_rest]
    x3 = _mm(att, wo_f, "nn", F32, "attn_out", add=x2)
    (dx4, loss_rows, hf1, gg1, uu1), _ = _ffn_fwd(x3, g_ffn1, *ffn_w1, name="ffn_fwd_loss", target=target)

    def blocks(dw):
        return dw.reshape(N_DEV, dw.shape[0] // N_DEV, dw.shape[1])

    (dgg1, duu1, dwg1, dwu1, dwd1), _ = _ffn_bwd_weights(dx4, hf1, gg1, uu1, ffn_w1[2], name="ffn_bwd1")
    parts1 = [blocks(dw) for dw in (dwg1, dwu1, dwd1)]
    (dx3, dg_ffn1), stage1 = _ffn_bwd_input(dx4, x3, g_ffn1, [dgg1, duu1], ffn_w1[:2], "ffn_dx1",
                                            ex=_pair_exchange(parts1))
    both1 = _pair_add(parts1, stage1, "pair_add_ffn1", copies=2)
    flight1, chip1, land1, _ = _chip_exchange_start(both1[:3], both1[3:], dx3, "scatter_ffn1_start")
    datt = _mm(dx3, wo_f, "nt", BF16, "d_attn")
    dwo = _mm(att, dx3, "tn", BF16, "d_wo")
    (dq, dkp, dvp, dband), _ = _attn_bwd(qq, kp, vp, bias, datt)
    (dx2, dqpre, dkpre, dg_mix1, dg_kv, dgq, dgk), _ = _qkv_input_bwd(
        dq, qpre, gq_t, dkp, kpre, gk_t, dvp, wq_f, wk_f, wv_f, dx3, x2, g_mix1, g_kv)
    dwq = _mm(h1, dqpre, "tn", BF16, "d_wq")
    dwk = _mm(hkv, dkpre, "tn", BF16, "d_wk")
    dwv = _mm(hkv, dvp, "tn", BF16, "d_wv", skip_rows=PADK)
    parts_att = [blocks(dw) for dw in (dwk, dwv, dwq, dwo)]
    (dgg0, duu0, *dw0), stage_att = _ffn_bwd_weights(
        dx2, hf0, gg0, uu0, ffn_w0[2], name="ffn_bwd0", ex=_pair_exchange(parts_att))
    both_att = _pair_add(parts_att, stage_att, "pair_add_att", copies=2)
    flight_att, chip_att, land_att, _ = _chip_exchange_start(both_att[:4], both_att[4:], dx2, "scatter_att_start")
    parts0 = [blocks(dw) for dw in dw0]
    (dx1, dg_ffn0), stage0 = _ffn_bwd_input(dx2, x1, g_ffn0, [dgg0, duu0], ffn_w0[:2], "ffn_dx0",
                                            ex=_pair_exchange(parts0))
    both = _pair_add(parts0, stage0, "pair_add_ffn0", copies=2)
    flight, chip0, land0, token = _chip_exchange_start(both[:3], both[3:], dx1, "scatter_ffn0_start")
    (grad_x, dpw, db_rows, ds_rows, dg_mix0), _ = _pool_bwd(dx1, xin, diff, g_mix0 + token[0:1, 0:1], pw_f, pb_f, ps_f)
    dtp, seg = _bias_grad(dband, dpw)

    weights = dict(norm_mix_g=norm_mix_g, norm_ffn_g=norm_ffn_g, pool_w=pool_w, pool_b=pool_b,
                   pool_scale=pool_scale, kv_norm_g=kv_norm_g, w_k=w_k, w_v=w_v, k_norm_g=k_norm_g, w_q=w_q,
                   q_norm_g=q_norm_g, rel_bias=rel_bias, w_o=w_o, w_gate=w_gate, w_up=w_up, w_down=w_down)
    mom1 = dict(norm_mix_g=m_norm_mix_g, norm_ffn_g=m_norm_ffn_g, pool_w=m_pool_w, pool_b=m_pool_b,
                pool_scale=m_pool_scale, kv_norm_g=m_kv_norm_g, w_k=m_w_k, w_v=m_w_v, k_norm_g=m_k_norm_g,
                w_q=m_w_q, q_norm_g=m_q_norm_g, rel_bias=m_rel_bias, w_o=m_w_o, w_gate=m_w_gate, w_up=m_w_up,
                w_down=m_w_down)
    mom2 = dict(norm_mix_g=v_norm_mix_g, norm_ffn_g=v_norm_ffn_g, pool_w=v_pool_w, pool_b=v_pool_b,
                pool_scale=v_pool_scale, kv_norm_g=v_kv_norm_g, w_k=v_w_k, w_v=v_w_v, k_norm_g=v_k_norm_g,
                w_q=v_w_q, q_norm_g=v_q_norm_g, rel_bias=v_rel_bias, w_o=v_w_o, w_gate=v_w_gate, w_up=v_w_up,
                w_down=v_w_down)
    names = list(weights)
    grads, deltas, new_m, new_v = {}, {}, {}, {}

    def summed(arrs, tag, after):
        return [_sum_parts(a, "sum_parts_%s%d" % (tag, k), after) for k, a in enumerate(arrs)]

    def adam_flat(nm):
        shape = weights[nm].shape
        flat = lambda a: a.reshape(-1, shape[-1])
        dl, m1, m2 = _adam(flat(weights[nm]), flat(grads[nm]), flat(mom1[nm]), flat(mom2[nm]), "adam_" + nm)
        deltas[nm], new_m[nm], new_v[nm] = dl.reshape(shape), m1.reshape(shape), m2.reshape(shape)

    ffn_names = ("w_gate", "w_up", "w_down")
    def ffn_view(nm, a):
        return a if nm == "w_down" else a.transpose(0, 2, 1)

    recv1 = _chip_exchange_wait(flight1, chip1, land1, dtp, "scatter_ffn1_wait")
    sums1 = summed(recv1, "ffn1_", token)
    layer1 = {nm: _adam_layer(ffn_view(nm, weights[nm]), g, ffn_view(nm, mom1[nm]), ffn_view(nm, mom2[nm]), 1,
                              "adam1_" + nm)
              for nm, g in zip(ffn_names, sums1)}
    recv_att = _chip_exchange_wait(flight_att, chip_att, land_att, layer1["w_down"][1], "scatter_att_wait")
    g_wk, g_wv, g_wq, g_wo = summed(recv_att, "att_", token)
    grads.update(w_k=g_wk, w_v=g_wv, w_q=g_wq[None], w_o=g_wo[None])
    for nm in ("w_k", "w_v", "w_q", "w_o"):
        adam_flat(nm)

    dpw_blocks = dpw.reshape(N_GROUPS, N_DEV, POOL_SHARD, GROUP).transpose(1, 0, 2, 3)
    dpw_blocks = dpw_blocks.reshape(N_DEV, N_GROUPS * POOL_SHARD, GROUP).astype(BF16)
    recv_pool = _run_exchange(_scatter_exchange([dpw_blocks]), "scatter_pool", after=deltas["w_o"])
    misc = jnp.concatenate([dgk[0:1, 0:HEAD], dgq[0:1, 0:HEAD], loss_rows[0:1, 0:1],
                            seg[:, 0, 0].reshape(1, N_HEADS), seg[:, 0, 1].reshape(1, N_HEADS)], axis=1)
    misc = jnp.pad(misc, ((0, 0), (0, D - misc.shape[1])))
    vec_rows = jnp.concatenate([dg_mix0[0:1], dg_mix1[0:1], dg_ffn0[0:1], dg_ffn1[0:1], dg_kv[0:1],
                                db_rows[0:1], ds_rows[0:1], misc], axis=0)
    pack = jnp.concatenate([vec_rows, dtp.reshape(N_HEADS, TOEP)], axis=0)
    tot = _all_reduce_small(pack, "reduce_small", after=recv_pool[0])

    loss = tot[7, 2 * HEAD]
    seg_tot = jnp.stack([tot[7, 2 * HEAD + 1:2 * HEAD + 1 + N_HEADS],
                         tot[7, 2 * HEAD + 1 + N_HEADS:2 * HEAD + 1 + 2 * N_HEADS]], axis=1)
    me = 4 * lax.axis_index("x") + 2 * lax.axis_index("y") + lax.axis_index("c")
    g_pool_b = lax.dynamic_slice_in_dim(tot[5].reshape(N_GROUPS, GROUP), me * POOL_SHARD, POOL_SHARD, axis=1)
    grads.update(
        norm_mix_g=tot[0:2], norm_ffn_g=tot[2:4], kv_norm_g=tot[4], k_norm_g=tot[7, 0:HEAD],
        q_norm_g=tot[7, HEAD:2 * HEAD].reshape(1, HEAD),
        rel_bias=_table_grad_from_toeplitz(tot[8:8 + N_HEADS], seg_tot).reshape(1, N_HEADS, N_REL),
        pool_b=g_pool_b.reshape(1, N_GROUPS, POOL_SHARD),
        pool_scale=lax.dynamic_slice_in_dim(tot[6:7], me * D_SHARD, D_SHARD, axis=1),
        pool_w=summed(recv_pool, "pool_", None)[0].reshape(1, N_GROUPS, POOL_SHARD, GROUP))
    adam_flat("pool_w")
    small_names = [nm for nm in names if nm not in ffn_names + ("w_k", "w_v", "w_q", "w_o", "pool_w")]

    def pack_small(tree):
        cols = []
        for nm in small_names:
            flat = tree[nm].reshape(-1)
            cols.append(jnp.pad(flat, (0, -flat.shape[0] % 1024)))
        return jnp.concatenate(cols).reshape(-1, 128)

    dl, m1, m2 = _adam(pack_small(weights), pack_small(grads), pack_small(mom1), pack_small(mom2), "adam_small")

    def unpack_small(packed, out):
        flat, off = packed.reshape(-1), 0
        for nm in small_names:
            size = weights[nm].size
            out[nm] = flat[off:off + size].reshape(weights[nm].shape)
            off += size + (-size % 1024)

    unpack_small(dl, deltas)
    unpack_small(m1, new_m)
    unpack_small(m2, new_v)

    recv0 = _chip_exchange_wait(flight, chip0, land0, dl, "scatter_ffn0_wait")
    sums0 = summed(recv0, "ffn0_", None)
    for nm, g in zip(ffn_names, sums0):
        done = _adam_layer(ffn_view(nm, weights[nm]), g, ffn_view(nm, mom1[nm]), ffn_view(nm, mom2[nm]), 0,
                           "adam0_" + nm, prev=layer1[nm])
        grads[nm], deltas[nm], new_m[nm], new_v[nm] = [ffn_view(nm, a) for a in done]

    return (loss, grad_x[None], *[grads[nm] for nm in names], *[deltas[nm] for nm in names],
            *[new_m[nm] for nm in names], *[new_v[nm] for nm in names])
```

```python
import functools

import jax
import jax.numpy as jnp
from jax import lax
from jax.experimental import pallas as pl
from jax.experimental.pallas import tpu as pltpu

F32 = jnp.float32
BF16 = jnp.bfloat16
MESH_ID = pl.DeviceIdType.MESH

N_DEV = 8
S = 2048
D = 1024
F = 2816
F_SHARD = F // N_DEV
D_SHARD = D // N_DEV
N_GROUPS = 4
GROUP = D // N_GROUPS
POOL_SHARD = GROUP // N_DEV
MAX_WIN = 16
HEAD = 64
N_HEADS = D // HEAD
CHUNK = 64
LEFT = 8
QB = 4 * CHUNK
KB = QB + LEFT * CHUNK
PADK = LEFT * CHUNK
TOEP = 1024
N_REL = 257
MAX_REL = 128
EPS = 1e-6
NEG_INF = -1e30
ATTN_SCALE = HEAD ** -0.5

ADAM_LR = 0.001
ADAM_B1 = 0.9
ADAM_B2 = 0.999
ADAM_EPS = 1e-08
ADAM_WD = 0.01
ADAM_STEP = 10

VMEM_LIMIT = 52 * 1024 * 1024

ANY = pl.BlockSpec(memory_space=pl.ANY)
VMEM = pl.BlockSpec(memory_space=pltpu.VMEM)


def _call(body, **kw):
    return pl.pallas_call(body, **kw)


def _params(*sem):
    return pltpu.CompilerParams(dimension_semantics=sem, vmem_limit_bytes=VMEM_LIMIT)


def _dot(a, b, dims):
    return lax.dot_general(a, b, (dims, ((), ())), preferred_element_type=F32)


def _nn(a, b):
    return _dot(a, b, ((1,), (0,)))


def _nt(a, b):
    return _dot(a, b, ((1,), (1,)))


def _tn(a, b):
    return _dot(a, b, ((0,), (0,)))


def _rstd(x):
    return lax.rsqrt(jnp.mean(x * x, axis=-1, keepdims=True) + EPS)


def _rms_bwd(dh, x, r, g):
    gd = dh * g
    return r * gd - x * (r * r * r) * jnp.mean(gd * x, axis=-1, keepdims=True)


def _colsum8(v):
    return jnp.broadcast_to(jnp.sum(v, axis=0, keepdims=True), (8, v.shape[1]))


def _seg_sum(v):
    r = lax.broadcasted_iota(jnp.int32, (128, 128), 0) // HEAD
    c = lax.broadcasted_iota(jnp.int32, (128, 128), 1) // HEAD
    ones = jnp.where(r == c, 1.0, 0.0).astype(BF16)
    out = []
    for blk in range(v.shape[1] // 128):
        part = v[:, blk * 128:(blk + 1) * 128]
        hi = part.astype(BF16)
        rest = part - hi.astype(F32)
        mid = rest.astype(BF16)
        lo = (rest - mid.astype(F32)).astype(BF16)
        out.append(_nn(hi, ones) + _nn(mid, ones) + _nn(lo, ones))
    return jnp.concatenate(out, axis=1)


def _place():
    return lax.axis_index("x"), lax.axis_index("y"), lax.axis_index("c")


class _Exchange:
    def __init__(self, ins, out_shape, sems, start, mid, finish):
        self.ins, self.out_shape, self.sems = list(ins), list(out_shape), list(sems)
        self.start, self.mid, self.finish = start, mid, finish


def _gather_exchange(shards):
    n = len(shards)

    def tools(ins, outs, sems):
        send_sems, recv_sems, local_sems = sems
        x, y, c = _place()
        me, sibling = (x, y, c), (x, y, 1 - c)
        chips = [(1 - x, y), (x, 1 - y), (1 - x, 1 - y)]

        def slot(k, px, py, pc):
            return outs[k].at[4 * px + 2 * py + pc]

        def copy(k, s, block, to, own=False):
            return pltpu.make_async_remote_copy(
                src_ref=ins[k] if own else slot(k, *block), dst_ref=slot(k, *block),
                send_sem=send_sems.at[k, s], recv_sem=recv_sems.at[k, s], device_id=to, device_id_type=MESH_ID)

        def mine():
            return [pltpu.make_async_copy(ins[k], slot(k, *me), local_sems.at[k]) for k in range(n)]

        def first():
            out = []
            for k in range(n):
                out.append(copy(k, 0, me, sibling, own=True))
                out += [copy(k, 1 + j, me, (*chip, c), own=True) for j, chip in enumerate(chips)]
            return out

        def landed(j):
            return [copy(k, 1 + j, (*chips[j], c), me) for k in range(n)]

        def passed(j):
            return [copy(k, 4 + j, (*chips[j], c), sibling) for k in range(n)]

        def from_sibling():
            out = []
            for k in range(n):
                out.append(copy(k, 0, sibling, me))
                out += [copy(k, 4 + j, (*chip, 1 - c), me) for j, chip in enumerate(chips)]
            return out

        return mine, first, landed, passed, from_sibling

    def start(ins, outs, sems):
        mine, first, _, _, _ = tools(ins, outs, sems)
        for cp in mine() + first():
            cp.start()

    def mid(ins, outs, sems):
        _, _, landed, passed, _ = tools(ins, outs, sems)
        arrived, onward = [landed(j) for j in range(3)], [passed(j) for j in range(3)]
        for k in range(n):
            for j in range(3):
                arrived[j][k].wait_recv()
                onward[j][k].start()

    def finish(ins, outs, sems):
        mine, first, _, passed, from_sibling = tools(ins, outs, sems)
        for cp in from_sibling():
            cp.wait_recv()
        for cp in first() + passed(0) + passed(1) + passed(2):
            cp.wait_send()
        for cp in mine():
            cp.wait()

    return _Exchange(
        shards, [jax.ShapeDtypeStruct((N_DEV,) + a.shape, a.dtype) for a in shards],
        [pltpu.SemaphoreType.DMA((n, 7)), pltpu.SemaphoreType.DMA((n, 7)), pltpu.SemaphoreType.DMA((n,))],
        start, mid, finish)


def _relay_gather_exchange(shards):
    n = len(shards)

    def tools(ins, outs, sems):
        send_sems, recv_sems, local_sems = sems
        x, y, c = _place()
        me, sib = (x, y, c), (x, y, 1 - c)
        xn, yn, dg = (1 - x, y, c), (x, 1 - y, c), (1 - x, 1 - y, c)
        other = lambda dev: (dev[0], dev[1], 1 - c)

        def piece(k, ref, half):
            rows = shards[k].shape[0] // 2
            return ref if half is None else ref.at[pl.ds(half * rows, rows)]

        def held(k, dev, half):
            return piece(k, outs[k].at[4 * dev[0] + 2 * dev[1] + dev[2]], half)

        def copy(k, s, dev, half, to, own=False):
            return pltpu.make_async_remote_copy(
                src_ref=piece(k, ins[k], half) if own else held(k, dev, half), dst_ref=held(k, dev, half),
                send_sem=send_sems.at[k, s], recv_sem=recv_sems.at[k, s], device_id=to, device_id_type=MESH_ID)

        def mine(k):
            return pltpu.make_async_copy(ins[k], held(k, me, None), local_sems.at[k])

        def sent(k):
            return [copy(k, 0, me, None, sib, own=True), copy(k, 1, me, 0, xn, own=True), copy(k, 2, me, 1, yn, own=True),
                    copy(k, 3, me, 0, yn, own=True), copy(k, 4, me, 1, xn, own=True)]

        relays = [(1, xn, 0, [(5, yn), (7, sib)]), (2, yn, 1, [(6, xn), (9, sib)]), (3, yn, 0, [(10, sib)]),
                  (4, xn, 1, [(8, sib)]), (6, dg, 1, [(11, sib)]), (5, dg, 0, [(12, sib)])]
        from_sibling = [(0, other(me), None), (7, other(xn), 0), (8, other(xn), 1), (9, other(yn), 1),
                        (10, other(yn), 0), (11, other(dg), 1), (12, other(dg), 0)]
        return mine, sent, copy, relays, from_sibling, me

    def start(ins, outs, sems):
        mine, sent, _, _, _, _ = tools(ins, outs, sems)
        for k in range(n):
            mine(k).start()
        for k in range(n):
            for cp in sent(k):
                cp.start()

    def mid(ins, outs, sems):
        _, _, copy, relays, _, me = tools(ins, outs, sems)
        for k in range(n):
            for s, dev, half, onward in relays:
                copy(k, s, dev, half, me).wait_recv()
                for s2, to in onward:
                    copy(k, s2, dev, half, to).start()

    def finish(ins, outs, sems):
        mine, sent, copy, relays, from_sibling, me = tools(ins, outs, sems)
        for k in range(n):
            for s, dev, half in from_sibling:
                copy(k, s, dev, half, me).wait_recv()
        for k in range(n):
            for cp in sent(k):
                cp.wait_send()
            for _, dev, half, onward in relays:
                for s2, to in onward:
                    copy(k, s2, dev, half, to).wait_send()
            mine(k).wait()

    return _Exchange(
        shards, [jax.ShapeDtypeStruct((N_DEV,) + a.shape, a.dtype) for a in shards],
        [pltpu.SemaphoreType.DMA((n, 13)), pltpu.SemaphoreType.DMA((n, 13)), pltpu.SemaphoreType.DMA((n,))],
        start, mid, finish)


def _peer(x, y, c, m):
    px = 1 - x if m & 4 else x
    py = 1 - y if m & 2 else y
    pc = 1 - c if m & 1 else c
    return px, py, pc


def _scatter_exchange(parts):
    n = len(parts)

    def tools(ins, outs, sems):
        send_sems, recv_sems, local_sems = sems
        x, y, c = _place()
        me = 4 * x + 2 * y + c
        def mine():
            return [pltpu.make_async_copy(ins[k].at[me], outs[k].at[me], local_sems.at[k]) for k in range(n)]

        def remote(dst_is_mine):
            out = []
            for m in range(1, N_DEV):
                px, py, pc = _peer(x, y, c, m)
                peer = 4 * px + 2 * py + pc
                for k in range(n):
                    out.append(pltpu.make_async_remote_copy(
                        src_ref=ins[k].at[peer], dst_ref=outs[k].at[me if dst_is_mine else peer],
                        send_sem=send_sems.at[k, m - 1], recv_sem=recv_sems.at[k, m - 1],
                        device_id=(px, py, pc), device_id_type=MESH_ID))
            return out

        return mine, remote

    def start(ins, outs, sems):
        mine, remote = tools(ins, outs, sems)
        for cp in mine() + remote(True):
            cp.start()

    def mid(ins, outs, sems):
        pass

    def finish(ins, outs, sems):
        mine, remote = tools(ins, outs, sems)
        for cp in remote(False):
            cp.wait_recv()
        for cp in remote(True):
            cp.wait_send()
        for cp in mine():
            cp.wait()

    return _Exchange(
        parts, [jax.ShapeDtypeStruct(a.shape, a.dtype) for a in parts],
        [pltpu.SemaphoreType.DMA((n, 7)), pltpu.SemaphoreType.DMA((n, 7)), pltpu.SemaphoreType.DMA((n,))],
        start, mid, finish)


N_CHIPS = N_DEV // 2


def _pair_exchange(parts):
    n = len(parts)

    def copies(ins, outs, sems):
        send_sems, recv_sems = sems
        x, y, c = _place()
        return [pltpu.make_async_remote_copy(
            src_ref=ins[k].at[2 * q + 1 - c], dst_ref=outs[k].at[q], send_sem=send_sems.at[k, q],
            recv_sem=recv_sems.at[k, q], device_id=(x, y, 1 - c), device_id_type=MESH_ID)
            for k in range(n) for q in range(N_CHIPS)]

    def start(ins, outs, sems):
        for cp in copies(ins, outs, sems):
            cp.start()

    def mid(ins, outs, sems):
        pass

    def finish(ins, outs, sems):
        for cp in copies(ins, outs, sems):
            cp.wait_recv()
        for cp in copies(ins, outs, sems):
            cp.wait_send()

    return _Exchange(
        parts, [jax.ShapeDtypeStruct((N_CHIPS,) + a.shape[1:], a.dtype) for a in parts],
        [pltpu.SemaphoreType.DMA((n, N_CHIPS)), pltpu.SemaphoreType.DMA((n, N_CHIPS))], start, mid, finish)


def _pair_add(parts, stage, name, copies=1):
    n = len(parts)
    core = lax.axis_index("c").reshape(1)

    def body(core_ref, *refs):
        for k in range(n):
            mine, theirs = refs[k], refs[n + k]
            total = (mine[0, 0].astype(F32) + theirs[0].astype(F32)).astype(BF16)
            for rep in range(copies):
                refs[(2 + rep) * n + k][0] = total

    in_specs, out_specs = [], []
    for a in parts:
        _, r, cdim = a.shape
        in_specs.append(pl.BlockSpec((1, 1, r // 2, cdim), lambda q, i, core_ref: (q, core_ref[0], i, 0)))
    for a in parts:
        _, r, cdim = a.shape
        in_specs.append(pl.BlockSpec((1, r // 2, cdim), lambda q, i, core_ref: (q, i, 0)))
        out_specs.append(pl.BlockSpec((1, r // 2, cdim), lambda q, i, core_ref: (q, i, 0)))
    return list(_call(
        body, name=name,
        grid_spec=pltpu.PrefetchScalarGridSpec(num_scalar_prefetch=1, grid=(N_CHIPS, 2), in_specs=in_specs,
                                               out_specs=out_specs * copies),
        out_shape=[jax.ShapeDtypeStruct(s.shape, BF16) for s in stage] * copies,
        compiler_params=_params("arbitrary", "arbitrary"),
    )(core, *[a.reshape((N_CHIPS, 2) + a.shape[1:]) for a in parts], *stage))


HBM = pl.BlockSpec(memory_space=pltpu.HBM)
SEMAPHORES = pl.BlockSpec(memory_space=pltpu.SEMAPHORE)


def _chip_copies(srcs, lands, send_sems, recv_sems, mine_is_dst):
    x, y, c = _place()
    me = 2 * x + y
    out = []
    for m in range(1, N_CHIPS):
        px, py, _ = _peer(x, y, c, 2 * m)
        peer = 2 * px + py
        for k in range(len(srcs)):
            pair = k * (N_CHIPS - 1) + m - 1
            out.append(pltpu.make_async_remote_copy(
                src_ref=srcs[k].at[peer], dst_ref=lands[k].at[me if mine_is_dst else peer],
                send_sem=send_sems[pair], recv_sem=recv_sems[pair],
                device_id=(px, py, c), device_id_type=MESH_ID))
    return out


def _chip_exchange_start(chip_parts, lands, after, name):
    n = len(chip_parts)
    pairs = n * (N_CHIPS - 1)

    def body(*refs):
        srcs, zones = refs[:n], refs[n:2 * n]
        sems = refs[2 * n + 1:2 * n + 1 + 2 * pairs]
        token = refs[-1]
        for cp in _chip_copies(srcs, zones, sems[:pairs], sems[pairs:], True):
            cp.start()
        token[...] = jnp.zeros(token.shape, F32)

    thru = [pltpu.HBM(a.shape, a.dtype) for a in chip_parts + lands]
    hbm = [pltpu.with_memory_space_constraint(a, pltpu.HBM) for a in chip_parts + lands]
    res = _call(
        body, name=name,
        out_shape=[pltpu.SemaphoreType.DMA(())] * (2 * pairs) + thru + [jax.ShapeDtypeStruct((8, 128), F32)],
        in_specs=[HBM] * (2 * n) + [ANY], out_specs=[SEMAPHORES] * (2 * pairs) + [HBM] * (2 * n) + [VMEM],
        input_output_aliases={i: 2 * pairs + i for i in range(2 * n)},
        compiler_params=pltpu.CompilerParams(has_side_effects=pltpu.SideEffectType.DATAFLOW_SIDE_EFFECTING),
    )(*hbm, after)
    sems, rest = list(res[:2 * pairs]), res[2 * pairs:]
    return sems, list(rest[:n]), list(rest[n:2 * n]), rest[-1]


def _chip_exchange_wait(sems, chip_parts, lands, after, name):
    n = len(chip_parts)
    pairs = n * (N_CHIPS - 1)

    def body(*refs):
        srcs, zones = refs[:n], refs[n:2 * n]
        sem_refs = refs[2 * n:2 * n + 2 * pairs]
        for cp in _chip_copies(srcs, zones, sem_refs[:pairs], sem_refs[pairs:], False):
            cp.wait_send()
            cp.wait_recv()

    thru = [pltpu.HBM(a.shape, a.dtype) for a in chip_parts + lands]
    res = _call(
        body, name=name, out_shape=thru,
        in_specs=[HBM] * (2 * n) + [SEMAPHORES] * (2 * pairs) + [ANY], out_specs=[HBM] * (2 * n),
        input_output_aliases={i: i for i in range(2 * n)},
        compiler_params=pltpu.CompilerParams(has_side_effects=pltpu.SideEffectType.DATAFLOW_SIDE_EFFECTING),
    )(*chip_parts, *lands, *sems, after)
    return list(res[n:])


def _run_exchange(ex, name, after=None):
    n_in, n_out = len(ex.ins), len(ex.out_shape)
    order = [] if after is None else [after]

    def body(*refs):
        ins, outs = refs[:n_in], refs[n_in + len(order):n_in + len(order) + n_out]
        sems = refs[n_in + len(order) + n_out:]
        ex.start(ins, outs, sems)
        ex.mid(ins, outs, sems)
        ex.finish(ins, outs, sems)

    return list(_call(body, name=name, out_shape=ex.out_shape, in_specs=[ANY] * (n_in + len(order)),
                      out_specs=[ANY] * n_out, scratch_shapes=ex.sems)(*ex.ins, *order))


def _call_hosting(body, ex, phases, *, in_specs, out_specs, out_shape, scratch_shapes, args, **kw):
    n_in, n_out, n_scr = len(in_specs), len(out_specs), len(scratch_shapes)
    if ex is None:
        res = _call(body, in_specs=in_specs, out_specs=out_specs, out_shape=out_shape,
                    scratch_shapes=scratch_shapes, **kw)(*args)
        return list(res), []
    n_xin, n_xout = len(ex.ins), len(ex.out_shape)

    def hosting(*refs):
        a, b = n_in, n_in + n_xin
        c, d = b + n_out, b + n_out + n_xout
        ins, xins, outs, xouts = refs[:a], refs[a:b], refs[b:c], refs[c:d]
        scr, sems = refs[d:d + n_scr], refs[d + n_scr:]
        first, mid, last = phases()

        @pl.when(first)
        def _():
            ex.start(xins, xouts, sems)

        body(*ins, *outs, *scr)

        @pl.when(mid)
        def _():
            ex.mid(xins, xouts, sems)

        @pl.when(last)
        def _():
            ex.finish(xins, xouts, sems)

    res = _call(hosting, in_specs=list(in_specs) + [ANY] * n_xin, out_specs=list(out_specs) + [ANY] * n_xout,
                out_shape=list(out_shape) + ex.out_shape, scratch_shapes=list(scratch_shapes) + ex.sems,
                **kw)(*args, *ex.ins)
    return list(res[:n_out]), list(res[n_out:])


def _grid_phases(dims, mid_fraction=0.8):
    total = 1
    for d in dims:
        total *= d
    mid = min(max(int(total * mid_fraction), 1), total - 1)

    def phases():
        step = pl.program_id(0)
        for axis in range(1, len(dims)):
            step = step * dims[axis] + pl.program_id(axis)
        return step == 0, step == mid, step == total - 1
    return phases


def _all_reduce_small(pack, name, after):
    rows = pack.shape[0]

    def body(in_ref, _, out_ref, recv, send_sems, recv_sems):
        x, y, c = _place()
        me = 4 * x + 2 * y + c
        recv[me] = in_ref[...]
        sent = []
        for m in range(1, N_DEV):
            px, py, pc = _peer(x, y, c, m)
            cp = pltpu.make_async_remote_copy(
                src_ref=in_ref, dst_ref=recv.at[me], send_sem=send_sems.at[m - 1], recv_sem=recv_sems.at[m - 1],
                device_id=(px, py, pc), device_id_type=MESH_ID)
            cp.start()
            sent.append(cp)
        for m in range(1, N_DEV):
            px, py, pc = _peer(x, y, c, m)
            peer = 4 * px + 2 * py + pc
            pltpu.make_async_remote_copy(
                src_ref=in_ref, dst_ref=recv.at[peer], send_sem=send_sems.at[m - 1], recv_sem=recv_sems.at[m - 1],
                device_id=(px, py, pc), device_id_type=MESH_ID).wait_recv()
        acc = recv[0]
        for d in range(1, N_DEV):
            acc = acc + recv[d]
        out_ref[...] = acc
        for cp in sent:
            cp.wait_send()

    return _call(
        body, name=name, out_shape=jax.ShapeDtypeStruct(pack.shape, F32), in_specs=[VMEM, ANY], out_specs=VMEM,
        scratch_shapes=[pltpu.VMEM((N_DEV, rows, pack.shape[1]), F32), pltpu.SemaphoreType.DMA((7,)),
                        pltpu.SemaphoreType.DMA((7,))],
        compiler_params=pltpu.CompilerParams(vmem_limit_bytes=VMEM_LIMIT),
    )(pack, after)


POOL_TS = 256


def _pool_counts(first_row, rows, win):
    t = first_row + lax.broadcasted_iota(jnp.int32, (rows, 1), 0)
    return jnp.minimum(t + 1, win).astype(F32)


def _pool_fwd(x, g, w, b, scale):
    nt = S // POOL_TS

    def body(x_ref, g_ref, w_ref, b_ref, s_ref, out_ref, diff_ref, ext):
        i = pl.program_id(0)

        @pl.when(i == 0)
        def _():
            ext[0:MAX_WIN, :] = jnp.zeros((MAX_WIN, D), F32)

        @pl.when(i > 0)
        def _():
            ext[0:MAX_WIN, :] = ext[POOL_TS:POOL_TS + MAX_WIN, :]

        xv = x_ref[...]
        h = xv * _rstd(xv) * g_ref[...]
        ext[MAX_WIN:, :] = h
        for gi in range(N_GROUPS):
            win = 2 << gi
            cols = slice(gi * GROUP, (gi + 1) * GROUP)
            sm = ext[:, cols]
            k = 1
            while k < win:
                sm = sm + pltpu.roll(sm, k, axis=0)
                k *= 2
            pooled = sm[MAX_WIN:, :] / _pool_counts(i * POOL_TS, POOL_TS, win)
            diff = (pooled - h[:, cols]).astype(BF16)
            yv = (_nn(diff, w_ref[gi]) + b_ref[:, cols]) * s_ref[:, cols]
            out_ref[:, cols] = xv[:, cols] + yv
            diff_ref[:, cols] = diff

    row = pl.BlockSpec((1, D), lambda i: (0, 0))
    tile = pl.BlockSpec((POOL_TS, D), lambda i: (i, 0))
    return _call(
        body, name="pool_fwd", grid=(nt,),
        in_specs=[tile, row, pl.BlockSpec((N_GROUPS, GROUP, GROUP), lambda i: (0, 0, 0)), row, row],
        out_specs=[tile, tile],
        out_shape=[jax.ShapeDtypeStruct((S, D), F32), jax.ShapeDtypeStruct((S, D), BF16)],
        scratch_shapes=[pltpu.VMEM((POOL_TS + MAX_WIN, D), F32)],
        compiler_params=_params("arbitrary"),
    )(x, g, w, b, scale)


def _pool_bwd(dy, x, diff, g, w, b, scale, ex=None):
    nt = S // POOL_TS

    def body(dy_ref, x_ref, diff_ref, g_ref, w_ref, b_ref, s_ref, gx_ref, dw_ref, db_ref, ds_ref, dg_ref, ext, dh):
        i = pl.program_id(0)
        first_row = (nt - 1 - i) * POOL_TS

        @pl.when(i == 0)
        def _():
            ext[POOL_TS:, :] = jnp.zeros((MAX_WIN, D), F32)
            dw_ref[...] = jnp.zeros(dw_ref.shape, F32)
            db_ref[...] = jnp.zeros(db_ref.shape, F32)
            ds_ref[...] = jnp.zeros(ds_ref.shape, F32)
            dg_ref[...] = jnp.zeros(dg_ref.shape, F32)

        @pl.when(i > 0)
        def _():
            ext[POOL_TS:, :] = ext[0:MAX_WIN, :]

        dyv = dy_ref[...]
        for gi in range(N_GROUPS):
            win = 2 << gi
            cols = slice(gi * GROUP, (gi + 1) * GROUP)
            dfb = diff_ref[:, cols]
            z = _nn(dfb, w_ref[gi]) + b_ref[:, cols]
            dyg = dyv[:, cols]
            ds_ref[:, cols] += _colsum8(dyg * z)
            dz = dyg * s_ref[:, cols]
            db_ref[:, cols] += _colsum8(dz)
            dzb = dz.astype(BF16)
            dw_ref[gi] += _tn(dfb, dzb)
            ddiff = _nt(dzb, w_ref[gi])
            ext[0:POOL_TS, cols] = ddiff / _pool_counts(first_row, POOL_TS, win)
            sm = ext[:, cols]
            k = 1
            while k < win:
                sm = sm + pltpu.roll(sm, POOL_TS + MAX_WIN - k, axis=0)
                k *= 2
            dh[:, cols] = sm[0:POOL_TS, :] - ddiff
        xv = x_ref[...]
        r = _rstd(xv)
        gv = g_ref[...]
        dhv = dh[...]
        dg_ref[...] += _colsum8(dhv * xv * r)
        gx_ref[...] = dyv + _rms_bwd(dhv, xv, r, gv)

    row = pl.BlockSpec((1, D), lambda i: (0, 0))
    tile = pl.BlockSpec((POOL_TS, D), lambda i: (nt - 1 - i, 0))
    acc = pl.BlockSpec((8, D), lambda i: (0, 0))
    wspec = pl.BlockSpec((N_GROUPS, GROUP, GROUP), lambda i: (0, 0, 0))
    return _call_hosting(
        body, ex, _grid_phases((nt,)), name="pool_bwd", grid=(nt,),
        in_specs=[tile, tile, tile, row, wspec, row, row],
        out_specs=[tile, wspec, acc, acc, acc],
        out_shape=[jax.ShapeDtypeStruct((S, D), F32), jax.ShapeDtypeStruct((N_GROUPS, GROUP, GROUP), F32),
                   jax.ShapeDtypeStruct((8, D), F32), jax.ShapeDtypeStruct((8, D), F32),
                   jax.ShapeDtypeStruct((8, D), F32)],
        scratch_shapes=[pltpu.VMEM((POOL_TS + MAX_WIN, D), F32), pltpu.VMEM((POOL_TS, D), F32)],
        args=(dy, x, diff, g, w, b, scale), compiler_params=_params("arbitrary"),
    )


FFN_TS = min(S, 1024)
FFN_TF = 256


def _ffn_fwd(x, g, wg_t, wu_t, wd, name, target=None, ex=None):
    width = wd.shape[0]
    ni, nj = S // FFN_TS, width // FFN_TF
    with_loss = target is not None
    extra = [target] if with_loss else []

    def body(*refs):
        x_ref, g_ref, wg_ref, wu_ref, wd_ref = refs[:5]
        if with_loss:
            t_ref, out_ref, loss_ref, h_ref, gg_ref, uu_ref, hs, acc = refs[5:]
        else:
            out_ref, h_ref, gg_ref, uu_ref, hs, acc = refs[5:]
        i, j = pl.program_id(0), pl.program_id(1)

        @pl.when(j == 0)
        def _():
            xv = x_ref[...]
            hb = (xv * _rstd(xv) * g_ref[...]).astype(BF16)
            hs[...] = hb
            h_ref[...] = hb
            acc[...] = jnp.zeros(acc.shape, F32)

        hb = hs[...]
        gg = _nt(hb, wg_ref[...])
        uu = _nt(hb, wu_ref[...])
        gg_ref[...] = gg
        uu_ref[...] = uu
        a = (gg * jax.nn.sigmoid(gg) * uu).astype(BF16)
        acc[...] += _nn(a, wd_ref[...])

        @pl.when(j == nj - 1)
        def _():
            yv = x_ref[...] + acc[...]
            if with_loss:
                err = yv - t_ref[...]
                out_ref[...] = err * (1.0 / D)
                part = jnp.sum(err * err) * (0.5 / D)

                @pl.when(i == 0)
                def _():
                    loss_ref[...] = jnp.zeros(loss_ref.shape, F32)

                loss_ref[...] += jnp.broadcast_to(part, loss_ref.shape)
            else:
                out_ref[...] = yv

    xt = pl.BlockSpec((FFN_TS, D), lambda i, j: (i, 0))
    row = pl.BlockSpec((1, D), lambda i, j: (0, 0))
    wt = pl.BlockSpec((FFN_TF, D), lambda i, j: (j, 0))
    gt = pl.BlockSpec((FFN_TS, FFN_TF), lambda i, j: (i, j))
    in_specs = [xt, row, wt, wt, wt] + [xt] * len(extra)
    out_specs = [xt] + ([pl.BlockSpec((8, 128), lambda i, j: (0, 0))] if with_loss else []) + [xt, gt, gt]
    out_shape = ([jax.ShapeDtypeStruct((S, D), F32)] + ([jax.ShapeDtypeStruct((8, 128), F32)] if with_loss else [])
                 + [jax.ShapeDtypeStruct((S, D), BF16), jax.ShapeDtypeStruct((S, width), F32),
                    jax.ShapeDtypeStruct((S, width), F32)])
    args = (x, g, wg_t, wu_t, wd, *extra)
    return _call_hosting(
        body, ex, _grid_phases((ni, nj), 0.8), name=name, grid=(ni, nj),
        in_specs=in_specs, out_specs=out_specs, out_shape=out_shape,
        scratch_shapes=[pltpu.VMEM((FFN_TS, D), BF16), pltpu.VMEM((FFN_TS, D), F32)], args=args,
        compiler_params=_params("arbitrary", "arbitrary"),
    )


def _ffn_bwd_weights(dout, h, gg, uu, wd, name, ex=None):
    width = wd.shape[0]
    nj = width // FFN_TF

    def body(do_ref, h_ref, gg_ref, uu_ref, wd_ref, dg_ref, du_ref, dwg_ref, dwu_ref, dwd_ref, dob):
        @pl.when(pl.program_id(0) == 0)
        def _():
            dob[...] = do_ref[...].astype(BF16)

        hb, dov = h_ref[...], dob[...]
        gv, uv = gg_ref[...], uu_ref[...]
        da = _nt(dov, wd_ref[...])
        sg = jax.nn.sigmoid(gv)
        sl = gv * sg
        ab = (sl * uv).astype(BF16)
        dub = (da * sl).astype(BF16)
        dgb = (da * uv * (sg * (1.0 + gv * (1.0 - sg)))).astype(BF16)
        dg_ref[...] = dgb
        du_ref[...] = dub
        dwg_ref[...] = _tn(dgb, hb).astype(BF16)
        dwu_ref[...] = _tn(dub, hb).astype(BF16)
        dwd_ref[...] = _tn(ab, dov).astype(BF16)

    once = pl.Buffered(1)
    whole = lambda: pl.BlockSpec((S, D), lambda j, _: (0, 0), pipeline_mode=once)
    wt = pl.BlockSpec((FFN_TF, D), lambda j, _: (j, 0))
    gt = pl.BlockSpec((S, FFN_TF), lambda j, _: (0, j))
    return _call_hosting(
        body, ex, _grid_phases((nj, 1)), name=name, grid=(nj, 1),
        in_specs=[whole(), whole(), gt, gt, wt], out_specs=[gt, gt, wt, wt, wt],
        out_shape=[jax.ShapeDtypeStruct((S, width), BF16)] * 2 + [jax.ShapeDtypeStruct((width, D), BF16)] * 3,
        scratch_shapes=[pltpu.VMEM((S, D), BF16)], args=(dout, h, gg, uu, wd),
        compiler_params=_params("arbitrary", "arbitrary"),
    )


BWD_TS = 512


def _ffn_bwd_input(dres, x, g, grads, weights, name, ex=None):
    nt = S // BWD_TS
    n = len(grads)

    def body(*refs):
        dres_ref, x_ref, g_ref = refs[:3]
        grad_refs, w_refs = refs[3:3 + n], refs[3 + n:3 + 2 * n]
        dx_ref, dgam_ref = refs[3 + 2 * n:]
        dh = _nn(grad_refs[0][...], w_refs[0][...])
        for k in range(1, n):
            dh = dh + _nn(grad_refs[k][...], w_refs[k][...])
        xv = x_ref[...]
        r = _rstd(xv)
        dx_ref[...] = dres_ref[...] + _rms_bwd(dh, xv, r, g_ref[...])

        @pl.when(pl.program_id(0) == 0)
        def _():
            dgam_ref[...] = jnp.zeros(dgam_ref.shape, F32)

        dgam_ref[...] += _colsum8(dh * xv * r)

    tile = pl.BlockSpec((BWD_TS, D), lambda i: (i, 0))
    ftiles = [pl.BlockSpec((BWD_TS, a.shape[1]), lambda i: (i, 0)) for a in grads]
    wspecs = [pl.BlockSpec(w.shape, lambda i: (0, 0), pipeline_mode=pl.Buffered(1)) for w in weights]
    return _call_hosting(
        body, ex, _grid_phases((nt,)), name=name, grid=(nt,),
        in_specs=[tile, tile, pl.BlockSpec((1, D), lambda i: (0, 0))] + ftiles + wspecs,
        out_specs=[tile, pl.BlockSpec((8, D), lambda i: (0, 0))],
        out_shape=[jax.ShapeDtypeStruct((S, D), F32), jax.ShapeDtypeStruct((8, D), F32)],
        scratch_shapes=[], args=(dres, x, g, *grads, *weights), compiler_params=_params("arbitrary"),
    )


def _mm(a, b, mode, out_dtype, name, add=None, skip_rows=0):
    if mode == "nn":
        (m, kd), n = (a.shape[0] - skip_rows, a.shape[1]), b.shape[1]
    elif mode == "nt":
        (m, kd), n = (a.shape[0] - skip_rows, a.shape[1]), b.shape[0]
    else:
        (kd, m), n = a.shape, b.shape[1]
    tm, tn, tk = min(m, 1024), min(n, 1024), min(kd, 1024)
    if skip_rows and mode == "tn":
        tk = min(tk, skip_rows)
    elif skip_rows:
        tm = min(tm, skip_rows)
    skip_a = skip_rows // tm if mode != "tn" else 0
    skip_b = skip_rows // tk if mode == "tn" else 0
    assert skip_rows == skip_a * tm + skip_b * tk
    nk = kd // tk
    dot = {"nn": _nn, "nt": _nt, "tn": _tn}[mode]

    def body(*refs):
        if add is None:
            a_ref, b_ref, o_ref, acc = refs
        else:
            a_ref, b_ref, add_ref, o_ref, acc = refs
        k = pl.program_id(2)

        @pl.when(k == 0)
        def _():
            acc[...] = jnp.zeros(acc.shape, F32)

        acc[...] += dot(a_ref[...].astype(BF16), b_ref[...].astype(BF16))

        @pl.when(k == nk - 1)
        def _():
            res = acc[...]
            if add is not None:
                res = res + add_ref[...]
            o_ref[...] = res.astype(out_dtype)

    if mode == "tn":
        a_spec = pl.BlockSpec((tk, tm), lambda i, j, k: (k, i))
        b_spec = pl.BlockSpec((tk, tn), lambda i, j, k: (k + skip_b, j))
    else:
        a_spec = pl.BlockSpec((tm, tk), lambda i, j, k: (i + skip_a, k))
        b_spec = (pl.BlockSpec((tk, tn), lambda i, j, k: (k, j)) if mode == "nn"
                  else pl.BlockSpec((tn, tk), lambda i, j, k: (j, k)))
    o_spec = pl.BlockSpec((tm, tn), lambda i, j, k: (i, j))
    in_specs = [a_spec, b_spec] + ([o_spec] if add is not None else [])
    args = (a, b) + ((add,) if add is not None else ())
    return _call(
        body, name=name, grid=(m // tm, n // tn, nk), in_specs=in_specs, out_specs=o_spec,
        out_shape=jax.ShapeDtypeStruct((m, n), out_dtype), scratch_shapes=[pltpu.VMEM((tm, tn), F32)],
        compiler_params=_params("parallel", "parallel", "arbitrary"),
    )(*args)


PROJ_TS = 256


def _kvq_proj(x, g_kv, g_mix, wk, wv, wq, gk, gq, ex=None):
    lead = PADK // PROJ_TS

    def body(x_ref, gkv_ref, gmix_ref, wk_ref, wv_ref, wq_ref, gk_ref, gq_ref,
             hkv_ref, h1_ref, kpre_ref, qpre_ref, k_ref, v_ref, q_ref):
        i = pl.program_id(0)

        @pl.when(i < lead)
        def _():
            k_ref[...] = jnp.zeros(k_ref.shape, BF16)
            v_ref[...] = jnp.zeros(v_ref.shape, BF16)

        @pl.when(i >= lead)
        def _():
            xv = x_ref[...]
            xr = xv * _rstd(xv)
            hkv = (xr * gkv_ref[...]).astype(BF16)
            h1 = (xr * gmix_ref[...]).astype(BF16)
            hkv_ref[...] = hkv
            h1_ref[...] = h1
            kpre = _nn(hkv, wk_ref[...])
            qpre = _nn(h1, wq_ref[...])
            kpre_ref[...] = kpre
            qpre_ref[...] = qpre
            v_ref[...] = _nn(hkv, wv_ref[...]).astype(BF16)
            rk = lax.rsqrt(_seg_sum(kpre * kpre) * (1.0 / HEAD) + EPS)
            k_ref[...] = (kpre * rk * gk_ref[...]).astype(BF16)
            rq = lax.rsqrt(_seg_sum(qpre * qpre) * (1.0 / HEAD) + EPS)
            q_ref[...] = (qpre * rq * gq_ref[...]).astype(BF16)

    tile = pl.BlockSpec((PROJ_TS, D), lambda i: (jnp.maximum(i - lead, 0), 0))
    padded = pl.BlockSpec((PROJ_TS, D), lambda i: (i, 0))
    row = pl.BlockSpec((1, D), lambda i: (0, 0))
    wspec = pl.BlockSpec((D, D), lambda i: (0, 0))
    bf = jax.ShapeDtypeStruct((S, D), BF16)
    ff = jax.ShapeDtypeStruct((S, D), F32)
    bp = jax.ShapeDtypeStruct((PADK + S, D), BF16)
    return _call_hosting(
        body, ex, _grid_phases((lead + S // PROJ_TS,), 0.85), name="kvq_proj", grid=(lead + S // PROJ_TS,),
        in_specs=[tile, row, row, wspec, wspec, wspec, row, row],
        out_specs=[tile, tile, tile, tile, padded, padded, tile], out_shape=[bf, bf, ff, ff, bp, bp, bf],
        scratch_shapes=[], args=(x, g_kv, g_mix, wk, wv, wq, gk, gq), compiler_params=_params("arbitrary"),
    )


def _qkv_input_bwd(dq, qpre, gq, dkp, kpre, gk, dvp, wq, wk, wv, dres, x, g_mix, g_kv, ex=None):
    nt = S // PROJ_TS
    lead = PADK // PROJ_TS

    def head_bwd(dov, pv, hgv):
        r = lax.rsqrt(_seg_sum(pv * pv) * (1.0 / HEAD) + EPS)
        gd = dov * hgv
        dpre = r * gd - pv * (r * r * r) * (_seg_sum(gd * pv) * (1.0 / HEAD))
        return dpre.astype(BF16), _colsum8(dov * pv * r)

    def fold_heads(full):
        fold = full[:, 0:128]
        for blk in range(1, D // 128):
            fold = fold + full[:, blk * 128:(blk + 1) * 128]
        return fold + pltpu.roll(fold, HEAD, axis=1)

    def body(dq_ref, qpre_ref, gq_ref, dk_ref, kpre_ref, gk_ref, dv_ref, wq_ref, wk_ref, wv_ref, dres_ref, x_ref,
             gmix_ref, gkv_ref, dx_ref, dqpre_ref, dkpre_ref, dgmix_ref, dgkv_ref, dgq_ref, dgk_ref, accq, acck):
        i = pl.program_id(0)

        @pl.when(i == 0)
        def _():
            accq[...] = jnp.zeros(accq.shape, F32)
            acck[...] = jnp.zeros(acck.shape, F32)
            dgmix_ref[...] = jnp.zeros(dgmix_ref.shape, F32)
            dgkv_ref[...] = jnp.zeros(dgkv_ref.shape, F32)

        dqb, cq = head_bwd(dq_ref[...], qpre_ref[...], gq_ref[...])
        dkb, ck = head_bwd(dk_ref[...], kpre_ref[...], gk_ref[...])
        dqpre_ref[...] = dqb
        dkpre_ref[...] = dkb
        accq[...] += cq
        acck[...] += ck
        dh1 = _nt(dqb, wq_ref[...])
        dhkv = _nt(dkb, wk_ref[...]) + _nt(dv_ref[...].astype(BF16), wv_ref[...])
        xv = x_ref[...]
        r = _rstd(xv)
        dx_ref[...] = dres_ref[...] + _rms_bwd(dh1, xv, r, gmix_ref[...]) + _rms_bwd(dhkv, xv, r, gkv_ref[...])
        dgmix_ref[...] += _colsum8(dh1 * xv * r)
        dgkv_ref[...] += _colsum8(dhkv * xv * r)

        @pl.when(i == nt - 1)
        def _():
            dgq_ref[...] = fold_heads(accq[...])
            dgk_ref[...] = fold_heads(acck[...])

    tile = pl.BlockSpec((PROJ_TS, D), lambda i: (i, 0))
    behind = pl.BlockSpec((PROJ_TS, D), lambda i: (i + lead, 0))
    row = pl.BlockSpec((1, D), lambda i: (0, 0))
    wspec = lambda: pl.BlockSpec((D, D), lambda i: (0, 0), pipeline_mode=pl.Buffered(1))
    acc = pl.BlockSpec((8, D), lambda i: (0, 0))
    small = pl.BlockSpec((8, 128), lambda i: (0, 0))
    bf = jax.ShapeDtypeStruct((S, D), BF16)
    return _call_hosting(
        body, ex, _grid_phases((nt,)), name="qkv_input_bwd", grid=(nt,),
        in_specs=[tile, tile, row, behind, tile, row, behind, wspec(), wspec(), wspec(), tile, tile, row, row],
        out_specs=[tile, tile, tile, acc, acc, small, small],
        out_shape=[jax.ShapeDtypeStruct((S, D), F32), bf, bf, jax.ShapeDtypeStruct((8, D), F32),
                   jax.ShapeDtypeStruct((8, D), F32), jax.ShapeDtypeStruct((8, 128), F32),
                   jax.ShapeDtypeStruct((8, 128), F32)],
        scratch_shapes=[pltpu.VMEM((8, D), F32), pltpu.VMEM((8, D), F32)],
        args=(dq, qpre, gq, dkp, kpre, gk, dvp, wq, wk, wv, dres, x, g_mix, g_kv),
        compiler_params=_params("arbitrary"),
    )


def _toeplitz_from_table(table):
    far = jnp.broadcast_to(table[:, N_REL - 1:], (N_HEADS, PADK - MAX_REL + 1))
    near = table[:, N_REL - 2::-1]
    past = jnp.broadcast_to(table[:, 0:1], (N_HEADS, MAX_REL))
    wrap = jnp.broadcast_to(table[:, N_REL - 1:], (N_HEADS, TOEP - (PADK + 2 * MAX_REL + 1)))
    return jnp.concatenate([far, near, past, wrap], axis=1).reshape(N_HEADS, 1, TOEP)


def _table_grad_from_toeplitz(dtp, seg):
    lo = PADK - MAX_REL + 1
    near = dtp[:, lo + N_REL - 3:lo - 1:-1]
    return jnp.concatenate([seg[:, 1:2], near, seg[:, 0:1]], axis=1)


def _bias_band(tp):
    def body(tp_ref, out_ref):
        bv = pltpu.roll(jnp.broadcast_to(tp_ref[0], (QB, TOEP)), 0, axis=1, stride=1, stride_axis=0)
        out_ref[0] = jnp.where(_band_mask(), bv[:, 0:KB], NEG_INF)

    return _call(
        body, name="bias_band", grid=(N_HEADS,),
        in_specs=[pl.BlockSpec((1, 1, TOEP), lambda h: (h, 0, 0))],
        out_specs=pl.BlockSpec((1, QB, KB), lambda h: (h, 0, 0)),
        out_shape=jax.ShapeDtypeStruct((N_HEADS, QB, KB), F32),
        compiler_params=_params("parallel"),
    )(tp)


def _bias_grad(dband, after):
    lo, hi = PADK - MAX_REL + 1, PADK + MAX_REL

    def body(db_ref, _, dtp_ref, seg_ref):
        bv = jnp.concatenate([db_ref[0], jnp.zeros((QB, TOEP - KB), F32)], axis=1)
        row = lax.broadcasted_iota(jnp.int32, (QB, TOEP), 0)
        k = 1
        while k < QB:
            bv = jnp.where((row & k) != 0, pltpu.roll(bv, TOEP - k, axis=1), bv)
            k *= 2
        col = jnp.sum(bv, axis=0, keepdims=True)
        dtp_ref[0] = col
        u = lax.broadcasted_iota(jnp.int32, (1, TOEP), 1)
        far = jnp.sum(jnp.where((u < lo) | (u > hi + MAX_REL), col, 0.0))
        past = jnp.sum(jnp.where((u >= hi) & (u <= hi + MAX_REL), col, 0.0))
        lane = lax.broadcasted_iota(jnp.int32, (1, 128), 1)
        seg_ref[0] = jnp.where(lane == 0, far, jnp.where(lane == 1, past, 0.0))

    return _call(
        body, name="bias_grad", grid=(N_HEADS,),
        in_specs=[pl.BlockSpec((1, QB, KB), lambda h: (h, 0, 0)), ANY],
        out_specs=[pl.BlockSpec((1, 1, TOEP), lambda h: (h, 0, 0)), pl.BlockSpec((1, 1, 128), lambda h: (h, 0, 0))],
        out_shape=[jax.ShapeDtypeStruct((N_HEADS, 1, TOEP), F32), jax.ShapeDtypeStruct((N_HEADS, 1, 128), F32)],
        compiler_params=_params("parallel"),
    )(dband, after)


N_QB = S // QB
HEADS_PER_STEP = 4
ATT_LANES = HEADS_PER_STEP * HEAD
N_HG = D // ATT_LANES


def _band_mask():
    qc = lax.broadcasted_iota(jnp.int32, (QB, KB), 0) // CHUNK
    kc = lax.broadcasted_iota(jnp.int32, (QB, KB), 1) // CHUNK
    return (kc >= qc) & (kc <= qc + LEFT)


def _half_scale(hh, scale):
    lane = lax.broadcasted_iota(jnp.int32, (1, 128), 1)
    return jnp.where((lane < HEAD) == (hh == 0), scale, 0.0).astype(BF16)


def _probs(qh, kb, bias, first_key):
    sc = _nt(qh, kb) + bias
    if first_key is not None:
        sc = jnp.where(lax.broadcasted_iota(jnp.int32, (QB, KB), 1) >= first_key, sc, NEG_INF)
    e = jnp.exp(sc - jnp.max(sc, axis=-1, keepdims=True))
    return e * (1.0 / jnp.sum(e, axis=-1, keepdims=True))


def _by_padding(cb, compute):
    @pl.when(cb < PADK // QB)
    def _():
        compute(PADK - cb * QB)

    @pl.when(cb >= PADK // QB)
    def _():
        compute(None)


def _attn_fwd(q, kp, vp, bias, ex=None):
    def body(q_ref, k_ref, v_ref, b_ref, o_ref):
        cb = pl.program_id(1)
        band = pl.ds(pl.multiple_of(cb * QB, QB), KB)
        low = lax.broadcasted_iota(jnp.int32, (QB, 128), 1) < HEAD

        def compute(first_key):
            for pair in range(HEADS_PER_STEP // 2):
                lanes = pl.ds(pair * 128, 128)
                kb, vb, qv = k_ref[band, lanes], v_ref[band, lanes], q_ref[:, lanes]
                outs = []
                for hh in range(2):
                    pb = _probs(qv * _half_scale(hh, ATTN_SCALE), kb, b_ref[2 * pair + hh], first_key).astype(BF16)
                    outs.append(_nn(pb, vb))
                o_ref[:, lanes] = jnp.where(low, outs[0], outs[1]).astype(BF16)

        _by_padding(cb, compute)

    qspec = pl.BlockSpec((QB, ATT_LANES), lambda hg, cb: (cb, hg))
    kspec = pl.BlockSpec((PADK + S, ATT_LANES), lambda hg, cb: (0, hg))
    return _call_hosting(
        body, ex, _grid_phases((N_HG, N_QB), 0.85), name="attn_fwd", grid=(N_HG, N_QB),
        in_specs=[qspec, kspec, kspec, pl.BlockSpec((HEADS_PER_STEP, QB, KB), lambda hg, cb: (hg, 0, 0))],
        out_specs=[qspec], out_shape=[jax.ShapeDtypeStruct((S, D), BF16)], scratch_shapes=[],
        args=(q, kp, vp, bias), compiler_params=_params("arbitrary", "arbitrary"),
    )


def _attn_bwd(q, kp, vp, bias, do, ex=None):
    def body(q_ref, k_ref, v_ref, b_ref, do_ref, dq_ref, dk_ref, dv_ref, db_ref):
        cb = pl.program_id(1)

        @pl.when(cb == 0)
        def _():
            dk_ref[...] = jnp.zeros(dk_ref.shape, F32)
            dv_ref[...] = jnp.zeros(dv_ref.shape, F32)
            db_ref[...] = jnp.zeros(db_ref.shape, F32)

        band = pl.ds(pl.multiple_of(cb * QB, QB), KB)
        low = lax.broadcasted_iota(jnp.int32, (QB, 128), 1) < HEAD

        def compute(first_key):
            for pair in range(HEADS_PER_STEP // 2):
                lanes = pl.ds(pair * 128, 128)
                kb, vb = k_ref[band, lanes], v_ref[band, lanes]
                qv, dov = q_ref[:, lanes], do_ref[:, lanes]
                dq = jnp.zeros((QB, 128), F32)
                dkb = jnp.zeros((KB, 128), F32)
                dvb = jnp.zeros((KB, 128), F32)
                for hh in range(2):
                    sel = low if hh == 0 else jnp.logical_not(low)
                    doh = dov * _half_scale(hh, 1.0)
                    p = _probs(qv * _half_scale(hh, ATTN_SCALE), kb, b_ref[2 * pair + hh], first_key)
                    dp = _nt(doh, vb)
                    dvb = dvb + _tn(p.astype(BF16), doh)
                    ds = p * (dp - jnp.sum(dp * p, axis=-1, keepdims=True))
                    db_ref[2 * pair + hh] += ds
                    dsb = (ds * ATTN_SCALE).astype(BF16)
                    dq = dq + jnp.where(sel, _nn(dsb, kb), 0.0)
                    dkb = dkb + _tn(dsb, qv * _half_scale(hh, 1.0))
                dq_ref[:, lanes] = dq
                dk_ref[band, lanes] += dkb
                dv_ref[band, lanes] += dvb

        _by_padding(cb, compute)

    qspec = pl.BlockSpec((QB, ATT_LANES), lambda hg, cb: (cb, hg))
    kspec = pl.BlockSpec((PADK + S, ATT_LANES), lambda hg, cb: (0, hg))
    bspec = pl.BlockSpec((HEADS_PER_STEP, QB, KB), lambda hg, cb: (hg, 0, 0))
    kf = jax.ShapeDtypeStruct((PADK + S, D), F32)
    return _call_hosting(
        body, ex, _grid_phases((N_HG, N_QB)), name="attn_bwd", grid=(N_HG, N_QB),
        in_specs=[qspec, kspec, kspec, bspec, qspec],
        out_specs=[qspec, kspec, kspec, bspec],
        out_shape=[jax.ShapeDtypeStruct((S, D), F32), kf, kf, jax.ShapeDtypeStruct((N_HEADS, QB, KB), F32)],
        scratch_shapes=[], args=(q, kp, vp, bias, do), compiler_params=_params("arbitrary", "arbitrary"),
    )


def _adam_math(w, g, m, v):
    m = ADAM_B1 * m + (1.0 - ADAM_B1) * g
    v = ADAM_B2 * v + (1.0 - ADAM_B2) * (g * g)
    m_hat = m / (1.0 - ADAM_B1 ** ADAM_STEP)
    v_hat = v / (1.0 - ADAM_B2 ** ADAM_STEP)
    delta = -ADAM_LR * (m_hat / (jnp.sqrt(v_hat) + ADAM_EPS) + ADAM_WD * w)
    return delta, m, v


def _sum_parts(parts, name, after=None):
    count, r, c = parts.shape
    tr = r // 2 if r % 16 == 0 and r > 64 else r

    def body(p_ref, *rest):
        o_ref = rest[-1]
        acc = p_ref[0].astype(F32)
        for d in range(1, count):
            acc = acc + p_ref[d].astype(F32)
        o_ref[...] = acc

    return _call(
        body, name=name, grid=(r // tr,),
        in_specs=[pl.BlockSpec((count, tr, c), lambda i: (0, i, 0))] + ([ANY] if after is not None else []),
        out_specs=pl.BlockSpec((tr, c), lambda i: (i, 0)),
        out_shape=jax.ShapeDtypeStruct((r, c), F32), compiler_params=_params("parallel"),
    )(parts, *([after] if after is not None else []))


def _row_tile(r):
    for cand in (256, 176, 128, 64, 32, 16, 8):
        if r % cand == 0:
            return cand
    return r


def _adam_layer(w, g, m, v, layer, name, prev=None):
    nl, r, c = w.shape
    tr = _row_tile(r)

    def body(w_ref, g_ref, m_ref, v_ref, *rest):
        go_ref, d_ref, nm_ref, nv_ref = rest[-4:]
        gv = g_ref[...]
        delta, nm, nv = _adam_math(w_ref[0], gv, m_ref[0], v_ref[0])
        go_ref[0] = gv
        d_ref[0] = delta
        nm_ref[0] = nm
        nv_ref[0] = nv

    lspec = pl.BlockSpec((1, tr, c), lambda i: (layer, i, 0))
    sd = jax.ShapeDtypeStruct((nl, r, c), F32)
    extra = list(prev) if prev is not None else []
    return _call(
        body, name=name, grid=(r // tr,),
        in_specs=[lspec, pl.BlockSpec((tr, c), lambda i: (i, 0)), lspec, lspec] + [ANY] * len(extra),
        out_specs=[lspec] * 4, out_shape=[sd] * 4,
        input_output_aliases={4 + t: t for t in range(len(extra))},
        compiler_params=_params("parallel"),
    )(w, g, m, v, *extra)


def _adam(w, g, m, v, name):
    r, c = w.shape
    tr = _row_tile(r)

    def body(w_ref, g_ref, m_ref, v_ref, d_ref, nm_ref, nv_ref):
        delta, nm, nv = _adam_math(w_ref[...], g_ref[...], m_ref[...], v_ref[...])
        d_ref[...] = delta
        nm_ref[...] = nm
        nv_ref[...] = nv

    spec = pl.BlockSpec((tr, c), lambda i: (i, 0))
    sd = jax.ShapeDtypeStruct((r, c), F32)
    return _call(
        body, name=name, grid=(r // tr,), in_specs=[spec] * 4, out_specs=[spec] * 3, out_shape=[sd] * 3,
        compiler_params=_params("parallel"),
    )(w, g, m, v)


def _pad16(rows):
    return jnp.pad(rows, ((0, 16 - rows.shape[0]), (0, 0)))


def kernel(x, norm_mix_g, norm_ffn_g, pool_w, pool_b, pool_scale, kv_norm_g, w_k, w_v, k_norm_g, w_q, q_norm_g, rel_bias, w_o, w_gate, w_up, w_down, loss_target, m_norm_mix_g, m_norm_ffn_g, m_pool_w, m_pool_b, m_pool_scale, m_kv_norm_g, m_w_k, m_w_v, m_k_norm_g, m_w_q, m_q_norm_g, m_rel_bias, m_w_o, m_w_gate, m_w_up, m_w_down, v_norm_mix_g, v_norm_ffn_g, v_pool_w, v_pool_b, v_pool_scale, v_kv_norm_g, v_w_k, v_w_v, v_k_norm_g, v_w_q, v_q_norm_g, v_rel_bias, v_w_o, v_w_gate, v_w_up, v_w_down):
    assert x.shape == (1, S, D) and w_gate.shape == (2, D, F_SHARD) and w_k.shape == (D_SHARD, D)
    xin, target = x[0], loss_target[0]

    ffn_shards = [[w_gate[layer].T.astype(BF16), w_up[layer].T.astype(BF16), w_down[layer].astype(BF16)]
                  for layer in range(2)]
    att_shards = [w_k.astype(BF16), w_v.astype(BF16), w_q[0].astype(BF16), w_o[0].astype(BF16)]
    pool_shard = pool_w[0].astype(BF16).reshape(N_GROUPS * POOL_SHARD, GROUP)
    small = jnp.concatenate([pool_b[0].reshape(1, N_GROUPS * POOL_SHARD), pool_scale], axis=1)

    full0 = _run_exchange(_relay_gather_exchange(ffn_shards[0] + [pool_shard, _pad16(small)]), "gather_layer0")
    ffn_w0 = [a.reshape(F, D) for a in full0[:3]]
    pw_f = full0[3].reshape(N_DEV, N_GROUPS, POOL_SHARD, GROUP).transpose(1, 0, 2, 3).reshape(N_GROUPS, GROUP, GROUP)
    small_f = full0[4][:, 0, :]
    pb_f = small_f[:, :N_GROUPS * POOL_SHARD].reshape(N_DEV, N_GROUPS, POOL_SHARD).transpose(1, 0, 2).reshape(1, D)
    ps_f = small_f[:, N_GROUPS * POOL_SHARD:].reshape(1, D)

    g_mix0, g_mix1 = norm_mix_g[0:1], norm_mix_g[1:2]
    g_ffn0, g_ffn1 = norm_ffn_g[0:1], norm_ffn_g[1:2]
    g_kv = kv_norm_g.reshape(1, D)
    gk_t = jnp.tile(k_norm_g.reshape(1, HEAD), (1, N_HEADS))
    gq_t = jnp.tile(q_norm_g.reshape(1, HEAD), (1, N_HEADS))

    x1, diff = _pool_fwd(xin, g_mix0, pw_f, pb_f, ps_f)
    (x2, hf0, gg0, uu0), full_att = _ffn_fwd(x1, g_ffn0, *ffn_w0, name="ffn_fwd", ex=_gather_exchange(att_shards))
    wk_f, wv_f, wq_f, wo_f = [a.reshape(D, D) for a in full_att]
    (hkv, h1, kpre, qpre, kp, vp, qq), full1_gate = _kvq_proj(x2, g_kv, g_mix1, wk_f, wv_f, wq_f, gk_t, gq_t,
                                                                ex=_gather_exchange(ffn_shards[1][:1]))
    bias = _bias_band(_toeplitz_from_table(rel_bias[0]))
    (att,), full1_rest = _attn_fwd(qq, kp, vp, bias, ex=_gather_exchange(ffn_shards[1][1:]))
    ffn_w1 = [a.reshape(F, D) for a in full1_gate + full1_rest]
    x3 = _mm(att, wo_f, "nn", F32, "attn_out", add=x2)
    (dx4, loss_rows, hf1, gg1, uu1), _ = _ffn_fwd(x3, g_ffn1, *ffn_w1, name="ffn_fwd_loss", target=target)

    def blocks(dw):
        return dw.reshape(N_DEV, dw.shape[0] // N_DEV, dw.shape[1])

    (dgg1, duu1, dwg1, dwu1, dwd1), _ = _ffn_bwd_weights(dx4, hf1, gg1, uu1, ffn_w1[2], name="ffn_bwd1")
    parts1 = [blocks(dw) for dw in (dwg1, dwu1, dwd1)]
    (dx3, dg_ffn1), stage1 = _ffn_bwd_input(dx4, x3, g_ffn1, [dgg1, duu1], ffn_w1[:2], "ffn_dx1",
                                            ex=_pair_exchange(parts1))
    both1 = _pair_add(parts1, stage1, "pair_add_ffn1", copies=2)
    flight1, chip1, land1, _ = _chip_exchange_start(both1[:3], both1[3:], dx3, "scatter_ffn1_start")
    datt = _mm(dx3, wo_f, "nt", BF16, "d_attn")
    dwo = _mm(att, dx3, "tn", BF16, "d_wo")
    (dq, dkp, dvp, dband), _ = _attn_bwd(qq, kp, vp, bias, datt)
    (dx2, dqpre, dkpre, dg_mix1, dg_kv, dgq, dgk), _ = _qkv_input_bwd(
        dq, qpre, gq_t, dkp, kpre, gk_t, dvp, wq_f, wk_f, wv_f, dx3, x2, g_mix1, g_kv)
    dwq = _mm(h1, dqpre, "tn", BF16, "d_wq")
    dwk = _mm(hkv, dkpre, "tn", BF16, "d_wk")
    dwv = _mm(hkv, dvp, "tn", BF16, "d_wv", skip_rows=PADK)
    parts_att = [blocks(dw) for dw in (dwk, dwv, dwq, dwo)]
    (dgg0, duu0, *dw0), stage_att = _ffn_bwd_weights(
        dx2, hf0, gg0, uu0, ffn_w0[2], name="ffn_bwd0", ex=_pair_exchange(parts_att))
    both_att = _pair_add(parts_att, stage_att, "pair_add_att", copies=2)
    flight_att, chip_att, land_att, _ = _chip_exchange_start(both_att[:4], both_att[4:], dx2, "scatter_att_start")
    parts0 = [blocks(dw) for dw in dw0]
    (dx1, dg_ffn0), stage0 = _ffn_bwd_input(dx2, x1, g_ffn0, [dgg0, duu0], ffn_w0[:2], "ffn_dx0",
                                            ex=_pair_exchange(parts0))
    both = _pair_add(parts0, stage0, "pair_add_ffn0", copies=2)
    flight, chip0, land0, token = _chip_exchange_start(both[:3], both[3:], dx1, "scatter_ffn0_start")
    (grad_x, dpw, db_rows, ds_rows, dg_mix0), _ = _pool_bwd(dx1, xin, diff, g_mix0 + token[0:1, 0:1], pw_f, pb_f, ps_f)
    dtp, seg = _bias_grad(dband, dpw)

    weights = dict(norm_mix_g=norm_mix_g, norm_ffn_g=norm_ffn_g, pool_w=pool_w, pool_b=pool_b,
                   pool_scale=pool_scale, kv_norm_g=kv_norm_g, w_k=w_k, w_v=w_v, k_norm_g=k_norm_g, w_q=w_q,
                   q_norm_g=q_norm_g, rel_bias=rel_bias, w_o=w_o, w_gate=w_gate, w_up=w_up, w_down=w_down)
    mom1 = dict(norm_mix_g=m_norm_mix_g, norm_ffn_g=m_norm_ffn_g, pool_w=m_pool_w, pool_b=m_pool_b,
                pool_scale=m_pool_scale, kv_norm_g=m_kv_norm_g, w_k=m_w_k, w_v=m_w_v, k_norm_g=m_k_norm_g,
                w_q=m_w_q, q_norm_g=m_q_norm_g, rel_bias=m_rel_bias, w_o=m_w_o, w_gate=m_w_gate, w_up=m_w_up,
                w_down=m_w_down)
    mom2 = dict(norm_mix_g=v_norm_mix_g, norm_ffn_g=v_norm_ffn_g, pool_w=v_pool_w, pool_b=v_pool_b,
                pool_scale=v_pool_scale, kv_norm_g=v_kv_norm_g, w_k=v_w_k, w_v=v_w_v, k_norm_g=v_k_norm_g,
                w_q=v_w_q, q_norm_g=v_q_norm_g, rel_bias=v_rel_bias, w_o=v_w_o, w_gate=v_w_gate, w_up=v_w_up,
                w_down=v_w_down)
    names = list(weights)
    grads, deltas, new_m, new_v = {}, {}, {}, {}

    def summed(arrs, tag, after):
        return [_sum_parts(a, "sum_parts_%s%d" % (tag, k), after) for k, a in enumerate(arrs)]

    def adam_flat(nm):
        shape = weights[nm].shape
        flat = lambda a: a.reshape(-1, shape[-1])
        dl, m1, m2 = _adam(flat(weights[nm]), flat(grads[nm]), flat(mom1[nm]), flat(mom2[nm]), "adam_" + nm)
        deltas[nm], new_m[nm], new_v[nm] = dl.reshape(shape), m1.reshape(shape), m2.reshape(shape)

    ffn_names = ("w_gate", "w_up", "w_down")
    def ffn_view(nm, a):
        return a if nm == "w_down" else a.transpose(0, 2, 1)

    recv1 = _chip_exchange_wait(flight1, chip1, land1, dtp, "scatter_ffn1_wait")
    sums1 = summed(recv1, "ffn1_", token)
    layer1 = {nm: _adam_layer(ffn_view(nm, weights[nm]), g, ffn_view(nm, mom1[nm]), ffn_view(nm, mom2[nm]), 1,
                              "adam1_" + nm)
              for nm, g in zip(ffn_names, sums1)}
    recv_att = _chip_exchange_wait(flight_att, chip_att, land_att, layer1["w_down"][1], "scatter_att_wait")
    g_wk, g_wv, g_wq, g_wo = summed(recv_att, "att_", token)
    grads.update(w_k=g_wk, w_v=g_wv, w_q=g_wq[None], w_o=g_wo[None])
    for nm in ("w_k", "w_v", "w_q", "w_o"):
        adam_flat(nm)

    dpw_blocks = dpw.reshape(N_GROUPS, N_DEV, POOL_SHARD, GROUP).transpose(1, 0, 2, 3)
    dpw_blocks = dpw_blocks.reshape(N_DEV, N_GROUPS * POOL_SHARD, GROUP).astype(BF16)
    recv_pool = _run_exchange(_scatter_exchange([dpw_blocks]), "scatter_pool", after=deltas["w_o"])
    misc = jnp.concatenate([dgk[0:1, 0:HEAD], dgq[0:1, 0:HEAD], loss_rows[0:1, 0:1],
                            seg[:, 0, 0].reshape(1, N_HEADS), seg[:, 0, 1].reshape(1, N_HEADS)], axis=1)
    misc = jnp.pad(misc, ((0, 0), (0, D - misc.shape[1])))
    vec_rows = jnp.concatenate([dg_mix0[0:1], dg_mix1[0:1], dg_ffn0[0:1], dg_ffn1[0:1], dg_kv[0:1],
                                db_rows[0:1], ds_rows[0:1], misc], axis=0)
    pack = jnp.concatenate([vec_rows, dtp.reshape(N_HEADS, TOEP)], axis=0)
    tot = _all_reduce_small(pack, "reduce_small", after=recv_pool[0])

    loss = tot[7, 2 * HEAD]
    seg_tot = jnp.stack([tot[7, 2 * HEAD + 1:2 * HEAD + 1 + N_HEADS],
                         tot[7, 2 * HEAD + 1 + N_HEADS:2 * HEAD + 1 + 2 * N_HEADS]], axis=1)
    me = 4 * lax.axis_index("x") + 2 * lax.axis_index("y") + lax.axis_index("c")
    g_pool_b = lax.dynamic_slice_in_dim(tot[5].reshape(N_GROUPS, GROUP), me * POOL_SHARD, POOL_SHARD, axis=1)
    grads.update(
        norm_mix_g=tot[0:2], norm_ffn_g=tot[2:4], kv_norm_g=tot[4], k_norm_g=tot[7, 0:HEAD],
        q_norm_g=tot[7, HEAD:2 * HEAD].reshape(1, HEAD),
        rel_bias=_table_grad_from_toeplitz(tot[8:8 + N_HEADS], seg_tot).reshape(1, N_HEADS, N_REL),
        pool_b=g_pool_b.reshape(1, N_GROUPS, POOL_SHARD),
        pool_scale=lax.dynamic_slice_in_dim(tot[6:7], me * D_SHARD, D_SHARD, axis=1),
        pool_w=summed(recv_pool, "pool_", None)[0].reshape(1, N_GROUPS, POOL_SHARD, GROUP))
    adam_flat("pool_w")
    small_names = [nm for nm in names if nm not in ffn_names + ("w_k", "w_v", "w_q", "w_o", "pool_w")]

    def pack_small(tree):
        cols = []
        for nm in small_names:
            flat = tree[nm].reshape(-1)
            cols.append(jnp.pad(flat, (0, -flat.shape[0] % 1024)))
        return jnp.concatenate(cols).reshape(-1, 128)

    dl, m1, m2 = _adam(pack_small(weights), pack_small(grads), pack_small(mom1), pack_small(mom2), "adam_small")

    def unpack_small(packed, out):
        flat, off = packed.reshape(-1), 0
        for nm in small_names:
            size = weights[nm].size
            out[nm] = flat[off:off + size].reshape(weights[nm].shape)
            off += size + (-size % 1024)

    unpack_small(dl, deltas)
    unpack_small(m1, new_m)
    unpack_small(m2, new_v)

    recv0 = _chip_exchange_wait(flight, chip0, land0, dl, "scatter_ffn0_wait")
    sums0 = summed(recv0, "ffn0_", None)
    for nm, g in zip(ffn_names, sums0):
        done = _adam_layer(ffn_view(nm, weights[nm]), g, ffn_view(nm, mom1[nm]), ffn_view(nm, mom2[nm]), 0,
                           "adam0_" + nm, prev=layer1[nm])
        grads[nm], deltas[nm], new_m[nm], new_v[nm] = [ffn_view(nm, a) for a in done]

    return (loss, grad_x[None], *[grads[nm] for nm in names], *[deltas[nm] for nm in names],
            *[new_m[nm] for nm in names], *[new_v[nm] for nm in names])
```

```python
import functools

import jax
import jax.numpy as jnp
from jax import lax
from jax.experimental import pallas as pl
from jax.experimental.pallas import tpu as pltpu

F32 = jnp.float32
BF16 = jnp.bfloat16
MESH_ID = pl.DeviceIdType.MESH

N_DEV = 8
S = 2048
D = 1024
F = 2816
F_SHARD = F // N_DEV
D_SHARD = D // N_DEV
N_GROUPS = 4
GROUP = D // N_GROUPS
POOL_SHARD = GROUP // N_DEV
MAX_WIN = 16
HEAD = 64
N_HEADS = D // HEAD
CHUNK = 64
LEFT = 8
QB = 4 * CHUNK
KB = QB + LEFT * CHUNK
PADK = LEFT * CHUNK
TOEP = 1024
N_REL = 257
MAX_REL = 128
EPS = 1e-6
NEG_INF = -1e30
ATTN_SCALE = HEAD ** -0.5

ADAM_LR = 0.001
ADAM_B1 = 0.9
ADAM_B2 = 0.999
ADAM_EPS = 1e-08
ADAM_WD = 0.01
ADAM_STEP = 10

VMEM_LIMIT = 52 * 1024 * 1024

ANY = pl.BlockSpec(memory_space=pl.ANY)
VMEM = pl.BlockSpec(memory_space=pltpu.VMEM)


def _call(body, **kw):
    return pl.pallas_call(body, **kw)


def _params(*sem):
    return pltpu.CompilerParams(dimension_semantics=sem, vmem_limit_bytes=VMEM_LIMIT)


def _dot(a, b, dims):
    return lax.dot_general(a, b, (dims, ((), ())), preferred_element_type=F32)


def _nn(a, b):
    return _dot(a, b, ((1,), (0,)))


def _nt(a, b):
    return _dot(a, b, ((1,), (1,)))


def _tn(a, b):
    return _dot(a, b, ((0,), (0,)))


def _rstd(x):
    return lax.rsqrt(jnp.mean(x * x, axis=-1, keepdims=True) + EPS)


def _rms_bwd(dh, x, r, g):
    gd = dh * g
    return r * gd - x * (r * r * r) * jnp.mean(gd * x, axis=-1, keepdims=True)


def _colsum8(v):
    return jnp.broadcast_to(jnp.sum(v, axis=0, keepdims=True), (8, v.shape[1]))


def _seg_sum(v):
    r = lax.broadcasted_iota(jnp.int32, (128, 128), 0) // HEAD
    c = lax.broadcasted_iota(jnp.int32, (128, 128), 1) // HEAD
    ones = jnp.where(r == c, 1.0, 0.0).astype(BF16)
    out = []
    for blk in range(v.shape[1] // 128):
        part = v[:, blk * 128:(blk + 1) * 128]
        hi = part.astype(BF16)
        rest = part - hi.astype(F32)
        mid = rest.astype(BF16)
        lo = (rest - mid.astype(F32)).astype(BF16)
        out.append(_nn(hi, ones) + _nn(mid, ones) + _nn(lo, ones))
    return jnp.concatenate(out, axis=1)


def _place():
    return lax.axis_index("x"), lax.axis_index("y"), lax.axis_index("c")


class _Exchange:
    def __init__(self, ins, out_shape, sems, start, mid, finish, early=None):
        self.ins, self.out_shape, self.sems = list(ins), list(out_shape), list(sems)
        self.start, self.mid, self.finish = start, mid, finish
        self.early = early if early is not None else (lambda ins, outs, sems: None)


def _relay_gather_exchange(shards):
    n = len(shards)

    def tools(ins, outs, sems):
        send_sems, recv_sems, local_sems = sems
        x, y, c = _place()
        me, sib = (x, y, c), (x, y, 1 - c)
        xn, yn, dg = (1 - x, y, c), (x, 1 - y, c), (1 - x, 1 - y, c)
        other = lambda dev: (dev[0], dev[1], 1 - c)

        def piece(k, ref, half):
            rows = shards[k].shape[0] // 2
            return ref if half is None else ref.at[pl.ds(half * rows, rows)]

        def held(k, dev, half):
            return piece(k, outs[k].at[4 * dev[0] + 2 * dev[1] + dev[2]], half)

        def copy(k, s, dev, half, to, own=False):
            return pltpu.make_async_remote_copy(
                src_ref=piece(k, ins[k], half) if own else held(k, dev, half), dst_ref=held(k, dev, half),
                send_sem=send_sems.at[k, s], recv_sem=recv_sems.at[k, s], device_id=to, device_id_type=MESH_ID)

        def mine(k):
            return pltpu.make_async_copy(ins[k], held(k, me, None), local_sems.at[k])

        def sent(k):
            return [copy(k, 0, me, None, sib, own=True), copy(k, 1, me, 0, xn, own=True), copy(k, 2, me, 1, yn, own=True),
                    copy(k, 3, me, 0, yn, own=True), copy(k, 4, me, 1, xn, own=True)]

        relays = [(1, xn, 0, [(5, yn), (7, sib)]), (2, yn, 1, [(6, xn), (9, sib)]), (3, yn, 0, [(10, sib)]),
                  (4, xn, 1, [(8, sib)]), (6, dg, 1, [(11, sib)]), (5, dg, 0, [(12, sib)])]
        from_sibling = [(0, other(me), None), (7, other(xn), 0), (8, other(xn), 1), (9, other(yn), 1),
                        (10, other(yn), 0), (11, other(dg), 1), (12, other(dg), 0)]
        return mine, sent, copy, relays, from_sibling, me

    def start(ins, outs, sems):
        mine, sent, _, _, _, _ = tools(ins, outs, sems)
        for k in range(n):
            mine(k).start()
        for k in range(n):
            for cp in sent(k):
                cp.start()

    def pass_on(ins, outs, sems, which):
        _, _, copy, relays, _, me = tools(ins, outs, sems)
        for k in range(n):
            for s, dev, half, onward in relays[which]:
                copy(k, s, dev, half, me).wait_recv()
                for s2, to in onward:
                    copy(k, s2, dev, half, to).start()

    def early(ins, outs, sems):
        pass_on(ins, outs, sems, slice(0, 2))

    def mid(ins, outs, sems):
        pass_on(ins, outs, sems, slice(2, None))

    def finish(ins, outs, sems):
        mine, sent, copy, relays, from_sibling, me = tools(ins, outs, sems)
        for k in range(n):
            for s, dev, half in from_sibling:
                copy(k, s, dev, half, me).wait_recv()
        for k in range(n):
            for cp in sent(k):
                cp.wait_send()
            for _, dev, half, onward in relays:
                for s2, to in onward:
                    copy(k, s2, dev, half, to).wait_send()
            mine(k).wait()

    return _Exchange(
        shards, [jax.ShapeDtypeStruct((N_DEV,) + a.shape, a.dtype) for a in shards],
        [pltpu.SemaphoreType.DMA((n, 13)), pltpu.SemaphoreType.DMA((n, 13)), pltpu.SemaphoreType.DMA((n,))],
        start, mid, finish, early)


def _peer(x, y, c, m):
    px = 1 - x if m & 4 else x
    py = 1 - y if m & 2 else y
    pc = 1 - c if m & 1 else c
    return px, py, pc


def _scatter_exchange(parts):
    n = len(parts)

    def tools(ins, outs, sems):
        send_sems, recv_sems, local_sems = sems
        x, y, c = _place()
        me = 4 * x + 2 * y + c
        def mine():
            return [pltpu.make_async_copy(ins[k].at[me], outs[k].at[me], local_sems.at[k]) for k in range(n)]

        def remote(dst_is_mine):
            out = []
            for m in range(1, N_DEV):
                px, py, pc = _peer(x, y, c, m)
                peer = 4 * px + 2 * py + pc
                for k in range(n):
                    out.append(pltpu.make_async_remote_copy(
                        src_ref=ins[k].at[peer], dst_ref=outs[k].at[me if dst_is_mine else peer],
                        send_sem=send_sems.at[k, m - 1], recv_sem=recv_sems.at[k, m - 1],
                        device_id=(px, py, pc), device_id_type=MESH_ID))
            return out

        return mine, remote

    def start(ins, outs, sems):
        mine, remote = tools(ins, outs, sems)
        for cp in mine() + remote(True):
            cp.start()

    def mid(ins, outs, sems):
        pass

    def finish(ins, outs, sems):
        mine, remote = tools(ins, outs, sems)
        for cp in remote(False):
            cp.wait_recv()
        for cp in remote(True):
            cp.wait_send()
        for cp in mine():
            cp.wait()

    return _Exchange(
        parts, [jax.ShapeDtypeStruct(a.shape, a.dtype) for a in parts],
        [pltpu.SemaphoreType.DMA((n, 7)), pltpu.SemaphoreType.DMA((n, 7)), pltpu.SemaphoreType.DMA((n,))],
        start, mid, finish)


N_CHIPS = N_DEV // 2


def _pair_exchange(parts):
    n = len(parts)

    def copies(ins, outs, sems):
        send_sems, recv_sems = sems
        x, y, c = _place()
        return [pltpu.make_async_remote_copy(
            src_ref=ins[k].at[2 * q + 1 - c], dst_ref=outs[k].at[q], send_sem=send_sems.at[k, q],
            recv_sem=recv_sems.at[k, q], device_id=(x, y, 1 - c), device_id_type=MESH_ID)
            for k in range(n) for q in range(N_CHIPS)]

    def start(ins, outs, sems):
        for cp in copies(ins, outs, sems):
            cp.start()

    def mid(ins, outs, sems):
        pass

    def finish(ins, outs, sems):
        for cp in copies(ins, outs, sems):
            cp.wait_recv()
        for cp in copies(ins, outs, sems):
            cp.wait_send()

    return _Exchange(
        parts, [jax.ShapeDtypeStruct((N_CHIPS,) + a.shape[1:], a.dtype) for a in parts],
        [pltpu.SemaphoreType.DMA((n, N_CHIPS)), pltpu.SemaphoreType.DMA((n, N_CHIPS))], start, mid, finish)


def _pair_add(parts, stage, name, copies=1):
    n = len(parts)
    core = lax.axis_index("c").reshape(1)

    def body(core_ref, *refs):
        for k in range(n):
            mine, theirs = refs[k], refs[n + k]
            total = (mine[0, 0].astype(F32) + theirs[0].astype(F32)).astype(BF16)
            for rep in range(copies):
                refs[(2 + rep) * n + k][0] = total

    in_specs, out_specs = [], []
    for a in parts:
        _, r, cdim = a.shape
        in_specs.append(pl.BlockSpec((1, 1, r // 2, cdim), lambda q, i, core_ref: (q, core_ref[0], i, 0)))
    for a in parts:
        _, r, cdim = a.shape
        in_specs.append(pl.BlockSpec((1, r // 2, cdim), lambda q, i, core_ref: (q, i, 0)))
        out_specs.append(pl.BlockSpec((1, r // 2, cdim), lambda q, i, core_ref: (q, i, 0)))
    return list(_call(
        body, name=name,
        grid_spec=pltpu.PrefetchScalarGridSpec(num_scalar_prefetch=1, grid=(N_CHIPS, 2), in_specs=in_specs,
                                               out_specs=out_specs * copies),
        out_shape=[jax.ShapeDtypeStruct(s.shape, BF16) for s in stage] * copies,
        compiler_params=_params("arbitrary", "arbitrary"),
    )(core, *[a.reshape((N_CHIPS, 2) + a.shape[1:]) for a in parts], *stage))


HBM = pl.BlockSpec(memory_space=pltpu.HBM)
SEMAPHORES = pl.BlockSpec(memory_space=pltpu.SEMAPHORE)


def _chip_copies(srcs, lands, send_sems, recv_sems, mine_is_dst):
    x, y, c = _place()
    me = 2 * x + y
    out = []
    for m in range(1, N_CHIPS):
        px, py, _ = _peer(x, y, c, 2 * m)
        peer = 2 * px + py
        for k in range(len(srcs)):
            pair = k * (N_CHIPS - 1) + m - 1
            out.append(pltpu.make_async_remote_copy(
                src_ref=srcs[k].at[peer], dst_ref=lands[k].at[me if mine_is_dst else peer],
                send_sem=send_sems[pair], recv_sem=recv_sems[pair],
                device_id=(px, py, c), device_id_type=MESH_ID))
    return out


def _chip_exchange_start(chip_parts, lands, after, name):
    n = len(chip_parts)
    pairs = n * (N_CHIPS - 1)

    def body(*refs):
        srcs, zones = refs[:n], refs[n:2 * n]
        sems = refs[2 * n + 1:2 * n + 1 + 2 * pairs]
        token = refs[-1]
        for cp in _chip_copies(srcs, zones, sems[:pairs], sems[pairs:], True):
            cp.start()
        token[...] = jnp.zeros(token.shape, F32)

    thru = [pltpu.HBM(a.shape, a.dtype) for a in chip_parts + lands]
    hbm = [pltpu.with_memory_space_constraint(a, pltpu.HBM) for a in chip_parts + lands]
    res = _call(
        body, name=name,
        out_shape=[pltpu.SemaphoreType.DMA(())] * (2 * pairs) + thru + [jax.ShapeDtypeStruct((8, 128), F32)],
        in_specs=[HBM] * (2 * n) + [ANY], out_specs=[SEMAPHORES] * (2 * pairs) + [HBM] * (2 * n) + [VMEM],
        input_output_aliases={i: 2 * pairs + i for i in range(2 * n)},
        compiler_params=pltpu.CompilerParams(has_side_effects=pltpu.SideEffectType.DATAFLOW_SIDE_EFFECTING),
    )(*hbm, after)
    sems, rest = list(res[:2 * pairs]), res[2 * pairs:]
    return sems, list(rest[:n]), list(rest[n:2 * n]), rest[-1]


def _chip_exchange_wait(sems, chip_parts, lands, after, name):
    n = len(chip_parts)
    pairs = n * (N_CHIPS - 1)

    def body(*refs):
        srcs, zones = refs[:n], refs[n:2 * n]
        sem_refs = refs[2 * n:2 * n + 2 * pairs]
        for cp in _chip_copies(srcs, zones, sem_refs[:pairs], sem_refs[pairs:], False):
            cp.wait_send()
            cp.wait_recv()

    thru = [pltpu.HBM(a.shape, a.dtype) for a in chip_parts + lands]
    res = _call(
        body, name=name, out_shape=thru,
        in_specs=[HBM] * (2 * n) + [SEMAPHORES] * (2 * pairs) + [ANY], out_specs=[HBM] * (2 * n),
        input_output_aliases={i: i for i in range(2 * n)},
        compiler_params=pltpu.CompilerParams(has_side_effects=pltpu.SideEffectType.DATAFLOW_SIDE_EFFECTING),
    )(*chip_parts, *lands, *sems, after)
    return list(res[n:])


def _run_exchange(ex, name, after=None):
    n_in, n_out = len(ex.ins), len(ex.out_shape)
    order = [] if after is None else [after]

    def body(*refs):
        ins, outs = refs[:n_in], refs[n_in + len(order):n_in + len(order) + n_out]
        sems = refs[n_in + len(order) + n_out:]
        ex.start(ins, outs, sems)
        ex.early(ins, outs, sems)
        ex.mid(ins, outs, sems)
        ex.finish(ins, outs, sems)

    return list(_call(body, name=name, out_shape=ex.out_shape, in_specs=[ANY] * (n_in + len(order)),
                      out_specs=[ANY] * n_out, scratch_shapes=ex.sems)(*ex.ins, *order))


def _call_hosting(body, ex, phases, *, in_specs, out_specs, out_shape, scratch_shapes, args, **kw):
    n_in, n_out, n_scr = len(in_specs), len(out_specs), len(scratch_shapes)
    if ex is None:
        res = _call(body, in_specs=in_specs, out_specs=out_specs, out_shape=out_shape,
                    scratch_shapes=scratch_shapes, **kw)(*args)
        return list(res), []
    n_xin, n_xout = len(ex.ins), len(ex.out_shape)

    def hosting(*refs):
        a, b = n_in, n_in + n_xin
        c, d = b + n_out, b + n_out + n_xout
        ins, xins, outs, xouts = refs[:a], refs[a:b], refs[b:c], refs[c:d]
        scr, sems = refs[d:d + n_scr], refs[d + n_scr:]
        first, early, mid, last = phases()

        @pl.when(first)
        def _():
            ex.start(xins, xouts, sems)

        body(*ins, *outs, *scr)

        @pl.when(early)
        def _():
            ex.early(xins, xouts, sems)

        @pl.when(mid)
        def _():
            ex.mid(xins, xouts, sems)

        @pl.when(last)
        def _():
            ex.finish(xins, xouts, sems)

    res = _call(hosting, in_specs=list(in_specs) + [ANY] * n_xin, out_specs=list(out_specs) + [ANY] * n_xout,
                out_shape=list(out_shape) + ex.out_shape, scratch_shapes=list(scratch_shapes) + ex.sems,
                **kw)(*args, *ex.ins)
    return list(res[:n_out]), list(res[n_out:])


def _grid_phases(dims, mid_fraction=0.8, early_fraction=0.4):
    total = 1
    for d in dims:
        total *= d
    mid = min(max(int(total * mid_fraction), 1), total - 1)
    early = min(int(total * early_fraction), mid)

    def phases():
        step = pl.program_id(0)
        for axis in range(1, len(dims)):
            step = step * dims[axis] + pl.program_id(axis)
        return step == 0, step == early, step == mid, step == total - 1
    return phases


def _all_reduce_small(pack, name, after):
    rows = pack.shape[0]

    def body(in_ref, _, out_ref, recv, send_sems, recv_sems):
        x, y, c = _place()
        me = 4 * x + 2 * y + c
        recv[me] = in_ref[...]
        sent = []
        for m in range(1, N_DEV):
            px, py, pc = _peer(x, y, c, m)
            cp = pltpu.make_async_remote_copy(
                src_ref=in_ref, dst_ref=recv.at[me], send_sem=send_sems.at[m - 1], recv_sem=recv_sems.at[m - 1],
                device_id=(px, py, pc), device_id_type=MESH_ID)
            cp.start()
            sent.append(cp)
        for m in range(1, N_DEV):
            px, py, pc = _peer(x, y, c, m)
            peer = 4 * px + 2 * py + pc
            pltpu.make_async_remote_copy(
                src_ref=in_ref, dst_ref=recv.at[peer], send_sem=send_sems.at[m - 1], recv_sem=recv_sems.at[m - 1],
                device_id=(px, py, pc), device_id_type=MESH_ID).wait_recv()
        acc = recv[0]
        for d in range(1, N_DEV):
            acc = acc + recv[d]
        out_ref[...] = acc
        for cp in sent:
            cp.wait_send()

    return _call(
        body, name=name, out_shape=jax.ShapeDtypeStruct(pack.shape, F32), in_specs=[VMEM, ANY], out_specs=VMEM,
        scratch_shapes=[pltpu.VMEM((N_DEV, rows, pack.shape[1]), F32), pltpu.SemaphoreType.DMA((7,)),
                        pltpu.SemaphoreType.DMA((7,))],
        compiler_params=pltpu.CompilerParams(vmem_limit_bytes=VMEM_LIMIT),
    )(pack, after)


POOL_TS = 256


def _pool_counts(first_row, rows, win):
    t = first_row + lax.broadcasted_iota(jnp.int32, (rows, 1), 0)
    return jnp.minimum(t + 1, win).astype(F32)


def _pool_fwd(x, g, w, b, scale):
    nt = S // POOL_TS

    def body(x_ref, g_ref, w_ref, b_ref, s_ref, out_ref, diff_ref, ext):
        i = pl.program_id(0)

        @pl.when(i == 0)
        def _():
            ext[0:MAX_WIN, :] = jnp.zeros((MAX_WIN, D), F32)

        @pl.when(i > 0)
        def _():
            ext[0:MAX_WIN, :] = ext[POOL_TS:POOL_TS + MAX_WIN, :]

        xv = x_ref[...]
        h = xv * _rstd(xv) * g_ref[...]
        ext[MAX_WIN:, :] = h
        for gi in range(N_GROUPS):
            win = 2 << gi
            cols = slice(gi * GROUP, (gi + 1) * GROUP)
            sm = ext[:, cols]
            k = 1
            while k < win:
                sm = sm + pltpu.roll(sm, k, axis=0)
                k *= 2
            pooled = sm[MAX_WIN:, :] / _pool_counts(i * POOL_TS, POOL_TS, win)
            diff = (pooled - h[:, cols]).astype(BF16)
            yv = (_nn(diff, w_ref[gi]) + b_ref[:, cols]) * s_ref[:, cols]
            out_ref[:, cols] = xv[:, cols] + yv
            diff_ref[:, cols] = diff

    row = pl.BlockSpec((1, D), lambda i: (0, 0))
    tile = pl.BlockSpec((POOL_TS, D), lambda i: (i, 0))
    return _call(
        body, name="pool_fwd", grid=(nt,),
        in_specs=[tile, row, pl.BlockSpec((N_GROUPS, GROUP, GROUP), lambda i: (0, 0, 0)), row, row],
        out_specs=[tile, tile],
        out_shape=[jax.ShapeDtypeStruct((S, D), F32), jax.ShapeDtypeStruct((S, D), BF16)],
        scratch_shapes=[pltpu.VMEM((POOL_TS + MAX_WIN, D), F32)],
        compiler_params=_params("arbitrary"),
    )(x, g, w, b, scale)


def _pool_bwd(dy, x, diff, g, w, b, scale, ex=None):
    nt = S // POOL_TS

    def body(dy_ref, x_ref, diff_ref, g_ref, w_ref, b_ref, s_ref, gx_ref, dw_ref, db_ref, ds_ref, dg_ref, ext, dh):
        i = pl.program_id(0)
        first_row = (nt - 1 - i) * POOL_TS

        @pl.when(i == 0)
        def _():
            ext[POOL_TS:, :] = jnp.zeros((MAX_WIN, D), F32)
            dw_ref[...] = jnp.zeros(dw_ref.shape, F32)
            db_ref[...] = jnp.zeros(db_ref.shape, F32)
            ds_ref[...] = jnp.zeros(ds_ref.shape, F32)
            dg_ref[...] = jnp.zeros(dg_ref.shape, F32)

        @pl.when(i > 0)
        def _():
            ext[POOL_TS:, :] = ext[0:MAX_WIN, :]

        dyv = dy_ref[...]
        for gi in range(N_GROUPS):
            win = 2 << gi
            cols = slice(gi * GROUP, (gi + 1) * GROUP)
            dfb = diff_ref[:, cols]
            z = _nn(dfb, w_ref[gi]) + b_ref[:, cols]
            dyg = dyv[:, cols]
            ds_ref[:, cols] += _colsum8(dyg * z)
            dz = dyg * s_ref[:, cols]
            db_ref[:, cols] += _colsum8(dz)
            dzb = dz.astype(BF16)
            dw_ref[gi] += _tn(dfb, dzb)
            ddiff = _nt(dzb, w_ref[gi])
            ext[0:POOL_TS, cols] = ddiff / _pool_counts(first_row, POOL_TS, win)
            sm = ext[:, cols]
            k = 1
            while k < win:
                sm = sm + pltpu.roll(sm, POOL_TS + MAX_WIN - k, axis=0)
                k *= 2
            dh[:, cols] = sm[0:POOL_TS, :] - ddiff
        xv = x_ref[...]
        r = _rstd(xv)
        gv = g_ref[...]
        dhv = dh[...]
        dg_ref[...] += _colsum8(dhv * xv * r)
        gx_ref[...] = dyv + _rms_bwd(dhv, xv, r, gv)

    row = pl.BlockSpec((1, D), lambda i: (0, 0))
    tile = pl.BlockSpec((POOL_TS, D), lambda i: (nt - 1 - i, 0))
    acc = pl.BlockSpec((8, D), lambda i: (0, 0))
    wspec = pl.BlockSpec((N_GROUPS, GROUP, GROUP), lambda i: (0, 0, 0))
    return _call_hosting(
        body, ex, _grid_phases((nt,)), name="pool_bwd", grid=(nt,),
        in_specs=[tile, tile, tile, row, wspec, row, row],
        out_specs=[tile, wspec, acc, acc, acc],
        out_shape=[jax.ShapeDtypeStruct((S, D), F32), jax.ShapeDtypeStruct((N_GROUPS, GROUP, GROUP), F32),
                   jax.ShapeDtypeStruct((8, D), F32), jax.ShapeDtypeStruct((8, D), F32),
                   jax.ShapeDtypeStruct((8, D), F32)],
        scratch_shapes=[pltpu.VMEM((POOL_TS + MAX_WIN, D), F32), pltpu.VMEM((POOL_TS, D), F32)],
        args=(dy, x, diff, g, w, b, scale), compiler_params=_params("arbitrary"),
    )


FFN_TS = min(S, 1024)
FFN_TF = 256


def _ffn_fwd(x, g, wg_t, wu_t, wd, name, target=None, ex=None):
    width = wd.shape[0]
    ni, nj = S // FFN_TS, width // FFN_TF
    with_loss = target is not None
    extra = [target] if with_loss else []

    def body(*refs):
        x_ref, g_ref, wg_ref, wu_ref, wd_ref = refs[:5]
        if with_loss:
            t_ref, out_ref, loss_ref, h_ref, gg_ref, uu_ref, hs, acc = refs[5:]
        else:
            out_ref, h_ref, gg_ref, uu_ref, hs, acc = refs[5:]
        i, j = pl.program_id(0), pl.program_id(1)

        @pl.when(j == 0)
        def _():
            xv = x_ref[...]
            hb = (xv * _rstd(xv) * g_ref[...]).astype(BF16)
            hs[...] = hb
            h_ref[...] = hb
            acc[...] = jnp.zeros(acc.shape, F32)

        hb = hs[...]
        gg = _nt(hb, wg_ref[...])
        uu = _nt(hb, wu_ref[...])
        gg_ref[...] = gg
        uu_ref[...] = uu
        a = (gg * jax.nn.sigmoid(gg) * uu).astype(BF16)
        acc[...] += _nn(a, wd_ref[...])

        @pl.when(j == nj - 1)
        def _():
            yv = x_ref[...] + acc[...]
            if with_loss:
                err = yv - t_ref[...]
                out_ref[...] = err * (1.0 / D)
                part = jnp.sum(err * err) * (0.5 / D)

                @pl.when(i == 0)
                def _():
                    loss_ref[...] = jnp.zeros(loss_ref.shape, F32)

                loss_ref[...] += jnp.broadcast_to(part, loss_ref.shape)
            else:
                out_ref[...] = yv

    xt = pl.BlockSpec((FFN_TS, D), lambda i, j: (i, 0))
    row = pl.BlockSpec((1, D), lambda i, j: (0, 0))
    wt = pl.BlockSpec((FFN_TF, D), lambda i, j: (j, 0))
    gt = pl.BlockSpec((FFN_TS, FFN_TF), lambda i, j: (i, j))
    in_specs = [xt, row, wt, wt, wt] + [xt] * len(extra)
    out_specs = [xt] + ([pl.BlockSpec((8, 128), lambda i, j: (0, 0))] if with_loss else []) + [xt, gt, gt]
    out_shape = ([jax.ShapeDtypeStruct((S, D), F32)] + ([jax.ShapeDtypeStruct((8, 128), F32)] if with_loss else [])
                 + [jax.ShapeDtypeStruct((S, D), BF16), jax.ShapeDtypeStruct((S, width), F32),
                    jax.ShapeDtypeStruct((S, width), F32)])
    args = (x, g, wg_t, wu_t, wd, *extra)
    return _call_hosting(
        body, ex, _grid_phases((ni, nj), 0.8), name=name, grid=(ni, nj),
        in_specs=in_specs, out_specs=out_specs, out_shape=out_shape,
        scratch_shapes=[pltpu.VMEM((FFN_TS, D), BF16), pltpu.VMEM((FFN_TS, D), F32)], args=args,
        compiler_params=_params("arbitrary", "arbitrary"),
    )


def _ffn_bwd_weights(dout, h, gg, uu, wd, name, ex=None):
    width = wd.shape[0]
    nj = width // FFN_TF

    def body(do_ref, h_ref, gg_ref, uu_ref, wd_ref, dg_ref, du_ref, dwg_ref, dwu_ref, dwd_ref, dob):
        @pl.when(pl.program_id(0) == 0)
        def _():
            dob[...] = do_ref[...].astype(BF16)

        hb, dov = h_ref[...], dob[...]
        gv, uv = gg_ref[...], uu_ref[...]
        da = _nt(dov, wd_ref[...])
        sg = jax.nn.sigmoid(gv)
        sl = gv * sg
        ab = (sl * uv).astype(BF16)
        dub = (da * sl).astype(BF16)
        dgb = (da * uv * (sg * (1.0 + gv * (1.0 - sg)))).astype(BF16)
        dg_ref[...] = dgb
        du_ref[...] = dub
        dwg_ref[...] = _tn(dgb, hb).astype(BF16)
        dwu_ref[...] = _tn(dub, hb).astype(BF16)
        dwd_ref[...] = _tn(ab, dov).astype(BF16)

    once = pl.Buffered(1)
    whole = lambda: pl.BlockSpec((S, D), lambda j, _: (0, 0), pipeline_mode=once)
    wt = pl.BlockSpec((FFN_TF, D), lambda j, _: (j, 0))
    gt = pl.BlockSpec((S, FFN_TF), lambda j, _: (0, j))
    return _call_hosting(
        body, ex, _grid_phases((nj, 1)), name=name, grid=(nj, 1),
        in_specs=[whole(), whole(), gt, gt, wt], out_specs=[gt, gt, wt, wt, wt],
        out_shape=[jax.ShapeDtypeStruct((S, width), BF16)] * 2 + [jax.ShapeDtypeStruct((width, D), BF16)] * 3,
        scratch_shapes=[pltpu.VMEM((S, D), BF16)], args=(dout, h, gg, uu, wd),
        compiler_params=_params("arbitrary", "arbitrary"),
    )


BWD_TS = 512


def _ffn_bwd_input(dres, x, g, grads, weights, name, ex=None):
    nt = S // BWD_TS
    n = len(grads)

    def body(*refs):
        dres_ref, x_ref, g_ref = refs[:3]
        grad_refs, w_refs = refs[3:3 + n], refs[3 + n:3 + 2 * n]
        dx_ref, dgam_ref = refs[3 + 2 * n:]
        dh = _nn(grad_refs[0][...], w_refs[0][...])
        for k in range(1, n):
            dh = dh + _nn(grad_refs[k][...], w_refs[k][...])
        xv = x_ref[...]
        r = _rstd(xv)
        dx_ref[...] = dres_ref[...] + _rms_bwd(dh, xv, r, g_ref[...])

        @pl.when(pl.program_id(0) == 0)
        def _():
            dgam_ref[...] = jnp.zeros(dgam_ref.shape, F32)

        dgam_ref[...] += _colsum8(dh * xv * r)

    tile = pl.BlockSpec((BWD_TS, D), lambda i: (i, 0))
    ftiles = [pl.BlockSpec((BWD_TS, a.shape[1]), lambda i: (i, 0)) for a in grads]
    wspecs = [pl.BlockSpec(w.shape, lambda i: (0, 0), pipeline_mode=pl.Buffered(1)) for w in weights]
    return _call_hosting(
        body, ex, _grid_phases((nt,)), name=name, grid=(nt,),
        in_specs=[tile, tile, pl.BlockSpec((1, D), lambda i: (0, 0))] + ftiles + wspecs,
        out_specs=[tile, pl.BlockSpec((8, D), lambda i: (0, 0))],
        out_shape=[jax.ShapeDtypeStruct((S, D), F32), jax.ShapeDtypeStruct((8, D), F32)],
        scratch_shapes=[], args=(dres, x, g, *grads, *weights), compiler_params=_params("arbitrary"),
    )


def _mm(a, b, mode, out_dtype, name, add=None, skip_rows=0):
    if mode == "nn":
        (m, kd), n = (a.shape[0] - skip_rows, a.shape[1]), b.shape[1]
    elif mode == "nt":
        (m, kd), n = (a.shape[0] - skip_rows, a.shape[1]), b.shape[0]
    else:
        (kd, m), n = a.shape, b.shape[1]
    tm, tn, tk = min(m, 1024), min(n, 1024), min(kd, 1024)
    if skip_rows and mode == "tn":
        tk = min(tk, skip_rows)
    elif skip_rows:
        tm = min(tm, skip_rows)
    skip_a = skip_rows // tm if mode != "tn" else 0
    skip_b = skip_rows // tk if mode == "tn" else 0
    assert skip_rows == skip_a * tm + skip_b * tk
    nk = kd // tk
    dot = {"nn": _nn, "nt": _nt, "tn": _tn}[mode]

    def body(*refs):
        if add is None:
            a_ref, b_ref, o_ref, acc = refs
        else:
            a_ref, b_ref, add_ref, o_ref, acc = refs
        k = pl.program_id(2)

        @pl.when(k == 0)
        def _():
            acc[...] = jnp.zeros(acc.shape, F32)

        acc[...] += dot(a_ref[...].astype(BF16), b_ref[...].astype(BF16))

        @pl.when(k == nk - 1)
        def _():
            res = acc[...]
            if add is not None:
                res = res + add_ref[...]
            o_ref[...] = res.astype(out_dtype)

    if mode == "tn":
        a_spec = pl.BlockSpec((tk, tm), lambda i, j, k: (k, i))
        b_spec = pl.BlockSpec((tk, tn), lambda i, j, k: (k + skip_b, j))
    else:
        a_spec = pl.BlockSpec((tm, tk), lambda i, j, k: (i + skip_a, k))
        b_spec = (pl.BlockSpec((tk, tn), lambda i, j, k: (k, j)) if mode == "nn"
                  else pl.BlockSpec((tn, tk), lambda i, j, k: (j, k)))
    o_spec = pl.BlockSpec((tm, tn), lambda i, j, k: (i, j))
    in_specs = [a_spec, b_spec] + ([o_spec] if add is not None else [])
    args = (a, b) + ((add,) if add is not None else ())
    return _call(
        body, name=name, grid=(m // tm, n // tn, nk), in_specs=in_specs, out_specs=o_spec,
        out_shape=jax.ShapeDtypeStruct((m, n), out_dtype), scratch_shapes=[pltpu.VMEM((tm, tn), F32)],
        compiler_params=_params("parallel", "parallel", "arbitrary"),
    )(*args)


PROJ_TS = 256


def _kvq_proj(x, g_kv, g_mix, wk, wv, wq, gk, gq, ex=None):
    lead = PADK // PROJ_TS

    def body(x_ref, gkv_ref, gmix_ref, wk_ref, wv_ref, wq_ref, gk_ref, gq_ref,
             hkv_ref, h1_ref, kpre_ref, qpre_ref, k_ref, v_ref, q_ref):
        i = pl.program_id(0)

        @pl.when(i < lead)
        def _():
            k_ref[...] = jnp.zeros(k_ref.shape, BF16)
            v_ref[...] = jnp.zeros(v_ref.shape, BF16)

        @pl.when(i >= lead)
        def _():
            xv = x_ref[...]
            xr = xv * _rstd(xv)
            hkv = (xr * gkv_ref[...]).astype(BF16)
            h1 = (xr * gmix_ref[...]).astype(BF16)
            hkv_ref[...] = hkv
            h1_ref[...] = h1
            kpre = _nn(hkv, wk_ref[...])
            qpre = _nn(h1, wq_ref[...])
            kpre_ref[...] = kpre
            qpre_ref[...] = qpre
            v_ref[...] = _nn(hkv, wv_ref[...]).astype(BF16)
            rk = lax.rsqrt(_seg_sum(kpre * kpre) * (1.0 / HEAD) + EPS)
            k_ref[...] = (kpre * rk * gk_ref[...]).astype(BF16)
            rq = lax.rsqrt(_seg_sum(qpre * qpre) * (1.0 / HEAD) + EPS)
            q_ref[...] = (qpre * rq * gq_ref[...]).astype(BF16)

    tile = pl.BlockSpec((PROJ_TS, D), lambda i: (jnp.maximum(i - lead, 0), 0))
    padded = pl.BlockSpec((PROJ_TS, D), lambda i: (i, 0))
    row = pl.BlockSpec((1, D), lambda i: (0, 0))
    wspec = pl.BlockSpec((D, D), lambda i: (0, 0))
    bf = jax.ShapeDtypeStruct((S, D), BF16)
    ff = jax.ShapeDtypeStruct((S, D), F32)
    bp = jax.ShapeDtypeStruct((PADK + S, D), BF16)
    return _call_hosting(
        body, ex, _grid_phases((lead + S // PROJ_TS,), 0.85), name="kvq_proj", grid=(lead + S // PROJ_TS,),
        in_specs=[tile, row, row, wspec, wspec, wspec, row, row],
        out_specs=[tile, tile, tile, tile, padded, padded, tile], out_shape=[bf, bf, ff, ff, bp, bp, bf],
        scratch_shapes=[], args=(x, g_kv, g_mix, wk, wv, wq, gk, gq), compiler_params=_params("arbitrary"),
    )


def _qkv_input_bwd(dq, qpre, gq, dkp, kpre, gk, dvp, wq, wk, wv, dres, x, g_mix, g_kv, ex=None):
    nt = S // PROJ_TS
    lead = PADK // PROJ_TS

    def head_bwd(dov, pv, hgv):
        r = lax.rsqrt(_seg_sum(pv * pv) * (1.0 / HEAD) + EPS)
        gd = dov * hgv
        dpre = r * gd - pv * (r * r * r) * (_seg_sum(gd * pv) * (1.0 / HEAD))
        return dpre.astype(BF16), _colsum8(dov * pv * r)

    def fold_heads(full):
        fold = full[:, 0:128]
        for blk in range(1, D // 128):
            fold = fold + full[:, blk * 128:(blk + 1) * 128]
        return fold + pltpu.roll(fold, HEAD, axis=1)

    def body(dq_ref, qpre_ref, gq_ref, dk_ref, kpre_ref, gk_ref, dv_ref, wq_ref, wk_ref, wv_ref, dres_ref, x_ref,
             gmix_ref, gkv_ref, dx_ref, dqpre_ref, dkpre_ref, dgmix_ref, dgkv_ref, dgq_ref, dgk_ref, accq, acck):
        i = pl.program_id(0)

        @pl.when(i == 0)
        def _():
            accq[...] = jnp.zeros(accq.shape, F32)
            acck[...] = jnp.zeros(acck.shape, F32)
            dgmix_ref[...] = jnp.zeros(dgmix_ref.shape, F32)
            dgkv_ref[...] = jnp.zeros(dgkv_ref.shape, F32)

        dqb, cq = head_bwd(dq_ref[...], qpre_ref[...], gq_ref[...])
        dkb, ck = head_bwd(dk_ref[...], kpre_ref[...], gk_ref[...])
        dqpre_ref[...] = dqb
        dkpre_ref[...] = dkb
        accq[...] += cq
        acck[...] += ck
        dh1 = _nt(dqb, wq_ref[...])
        dhkv = _nt(dkb, wk_ref[...]) + _nt(dv_ref[...].astype(BF16), wv_ref[...])
        xv = x_ref[...]
        r = _rstd(xv)
        dx_ref[...] = dres_ref[...] + _rms_bwd(dh1, xv, r, gmix_ref[...]) + _rms_bwd(dhkv, xv, r, gkv_ref[...])
        dgmix_ref[...] += _colsum8(dh1 * xv * r)
        dgkv_ref[...] += _colsum8(dhkv * xv * r)

        @pl.when(i == nt - 1)
        def _():
            dgq_ref[...] = fold_heads(accq[...])
            dgk_ref[...] = fold_heads(acck[...])

    tile = pl.BlockSpec((PROJ_TS, D), lambda i: (i, 0))
    behind = pl.BlockSpec((PROJ_TS, D), lambda i: (i + lead, 0))
    row = pl.BlockSpec((1, D), lambda i: (0, 0))
    wspec = lambda: pl.BlockSpec((D, D), lambda i: (0, 0), pipeline_mode=pl.Buffered(1))
    acc = pl.BlockSpec((8, D), lambda i: (0, 0))
    small = pl.BlockSpec((8, 128), lambda i: (0, 0))
    bf = jax.ShapeDtypeStruct((S, D), BF16)
    return _call_hosting(
        body, ex, _grid_phases((nt,)), name="qkv_input_bwd", grid=(nt,),
        in_specs=[tile, tile, row, behind, tile, row, behind, wspec(), wspec(), wspec(), tile, tile, row, row],
        out_specs=[tile, tile, tile, acc, acc, small, small],
        out_shape=[jax.ShapeDtypeStruct((S, D), F32), bf, bf, jax.ShapeDtypeStruct((8, D), F32),
                   jax.ShapeDtypeStruct((8, D), F32), jax.ShapeDtypeStruct((8, 128), F32),
                   jax.ShapeDtypeStruct((8, 128), F32)],
        scratch_shapes=[pltpu.VMEM((8, D), F32), pltpu.VMEM((8, D), F32)],
        args=(dq, qpre, gq, dkp, kpre, gk, dvp, wq, wk, wv, dres, x, g_mix, g_kv),
        compiler_params=_params("arbitrary"),
    )


def _toeplitz_from_table(table):
    far = jnp.broadcast_to(table[:, N_REL - 1:], (N_HEADS, PADK - MAX_REL + 1))
    near = table[:, N_REL - 2::-1]
    past = jnp.broadcast_to(table[:, 0:1], (N_HEADS, MAX_REL))
    wrap = jnp.broadcast_to(table[:, N_REL - 1:], (N_HEADS, TOEP - (PADK + 2 * MAX_REL + 1)))
    return jnp.concatenate([far, near, past, wrap], axis=1).reshape(N_HEADS, 1, TOEP)


def _table_grad_from_toeplitz(dtp, seg):
    lo = PADK - MAX_REL + 1
    near = dtp[:, lo + N_REL - 3:lo - 1:-1]
    return jnp.concatenate([seg[:, 1:2], near, seg[:, 0:1]], axis=1)


def _bias_band(tp):
    def body(tp_ref, out_ref):
        bv = pltpu.roll(jnp.broadcast_to(tp_ref[0], (QB, TOEP)), 0, axis=1, stride=1, stride_axis=0)
        out_ref[0] = jnp.where(_band_mask(), bv[:, 0:KB], NEG_INF)

    return _call(
        body, name="bias_band", grid=(N_HEADS,),
        in_specs=[pl.BlockSpec((1, 1, TOEP), lambda h: (h, 0, 0))],
        out_specs=pl.BlockSpec((1, QB, KB), lambda h: (h, 0, 0)),
        out_shape=jax.ShapeDtypeStruct((N_HEADS, QB, KB), F32),
        compiler_params=_params("parallel"),
    )(tp)


def _bias_grad(dband, after):
    lo, hi = PADK - MAX_REL + 1, PADK + MAX_REL

    def body(db_ref, _, dtp_ref, seg_ref):
        bv = jnp.concatenate([db_ref[0], jnp.zeros((QB, TOEP - KB), F32)], axis=1)
        row = lax.broadcasted_iota(jnp.int32, (QB, TOEP), 0)
        k = 1
        while k < QB:
            bv = jnp.where((row & k) != 0, pltpu.roll(bv, TOEP - k, axis=1), bv)
            k *= 2
        col = jnp.sum(bv, axis=0, keepdims=True)
        dtp_ref[0] = col
        u = lax.broadcasted_iota(jnp.int32, (1, TOEP), 1)
        far = jnp.sum(jnp.where((u < lo) | (u > hi + MAX_REL), col, 0.0))
        past = jnp.sum(jnp.where((u >= hi) & (u <= hi + MAX_REL), col, 0.0))
        lane = lax.broadcasted_iota(jnp.int32, (1, 128), 1)
        seg_ref[0] = jnp.where(lane == 0, far, jnp.where(lane == 1, past, 0.0))

    return _call(
        body, name="bias_grad", grid=(N_HEADS,),
        in_specs=[pl.BlockSpec((1, QB, KB), lambda h: (h, 0, 0)), ANY],
        out_specs=[pl.BlockSpec((1, 1, TOEP), lambda h: (h, 0, 0)), pl.BlockSpec((1, 1, 128), lambda h: (h, 0, 0))],
        out_shape=[jax.ShapeDtypeStruct((N_HEADS, 1, TOEP), F32), jax.ShapeDtypeStruct((N_HEADS, 1, 128), F32)],
        compiler_params=_params("parallel"),
    )(dband, after)


N_QB = S // QB
HEADS_PER_STEP = 4
ATT_LANES = HEADS_PER_STEP * HEAD
N_HG = D // ATT_LANES


def _band_mask():
    qc = lax.broadcasted_iota(jnp.int32, (QB, KB), 0) // CHUNK
    kc = lax.broadcasted_iota(jnp.int32, (QB, KB), 1) // CHUNK
    return (kc >= qc) & (kc <= qc + LEFT)


def _half_scale(hh, scale):
    lane = lax.broadcasted_iota(jnp.int32, (1, 128), 1)
    return jnp.where((lane < HEAD) == (hh == 0), scale, 0.0).astype(BF16)


def _probs(qh, kb, bias, first_key):
    sc = _nt(qh, kb) + bias
    if first_key is not None:
        sc = jnp.where(lax.broadcasted_iota(jnp.int32, (QB, KB), 1) >= first_key, sc, NEG_INF)
    e = jnp.exp(sc - jnp.max(sc, axis=-1, keepdims=True))
    return e * (1.0 / jnp.sum(e, axis=-1, keepdims=True))


def _by_padding(cb, compute):
    @pl.when(cb < PADK // QB)
    def _():
        compute(PADK - cb * QB)

    @pl.when(cb >= PADK // QB)
    def _():
        compute(None)


def _attn_fwd(q, kp, vp, bias, ex=None):
    def body(q_ref, k_ref, v_ref, b_ref, o_ref):
        cb = pl.program_id(1)
        band = pl.ds(pl.multiple_of(cb * QB, QB), KB)
        low = lax.broadcasted_iota(jnp.int32, (QB, 128), 1) < HEAD

        def compute(first_key):
            for pair in range(HEADS_PER_STEP // 2):
                lanes = pl.ds(pair * 128, 128)
                kb, vb, qv = k_ref[band, lanes], v_ref[band, lanes], q_ref[:, lanes]
                outs = []
                for hh in range(2):
                    pb = _probs(qv * _half_scale(hh, ATTN_SCALE), kb, b_ref[2 * pair + hh], first_key).astype(BF16)
                    outs.append(_nn(pb, vb))
                o_ref[:, lanes] = jnp.where(low, outs[0], outs[1]).astype(BF16)

        _by_padding(cb, compute)

    qspec = pl.BlockSpec((QB, ATT_LANES), lambda hg, cb: (cb, hg))
    kspec = pl.BlockSpec((PADK + S, ATT_LANES), lambda hg, cb: (0, hg))
    return _call_hosting(
        body, ex, _grid_phases((N_HG, N_QB), 0.85), name="attn_fwd", grid=(N_HG, N_QB),
        in_specs=[qspec, kspec, kspec, pl.BlockSpec((HEADS_PER_STEP, QB, KB), lambda hg, cb: (hg, 0, 0))],
        out_specs=[qspec], out_shape=[jax.ShapeDtypeStruct((S, D), BF16)], scratch_shapes=[],
        args=(q, kp, vp, bias), compiler_params=_params("arbitrary", "arbitrary"),
    )


def _attn_bwd(q, kp, vp, bias, do, ex=None):
    def body(q_ref, k_ref, v_ref, b_ref, do_ref, dq_ref, dk_ref, dv_ref, db_ref):
        cb = pl.program_id(1)

        @pl.when(cb == 0)
        def _():
            dk_ref[...] = jnp.zeros(dk_ref.shape, F32)
            dv_ref[...] = jnp.zeros(dv_ref.shape, F32)
            db_ref[...] = jnp.zeros(db_ref.shape, F32)

        band = pl.ds(pl.multiple_of(cb * QB, QB), KB)
        low = lax.broadcasted_iota(jnp.int32, (QB, 128), 1) < HEAD

        def compute(first_key):
            for pair in range(HEADS_PER_STEP // 2):
                lanes = pl.ds(pair * 128, 128)
                kb, vb = k_ref[band, lanes], v_ref[band, lanes]
                qv, dov = q_ref[:, lanes], do_ref[:, lanes]
                dq = jnp.zeros((QB, 128), F32)
                dkb = jnp.zeros((KB, 128), F32)
                dvb = jnp.zeros((KB, 128), F32)
                for hh in range(2):
                    sel = low if hh == 0 else jnp.logical_not(low)
                    doh = dov * _half_scale(hh, 1.0)
                    p = _probs(qv * _half_scale(hh, ATTN_SCALE), kb, b_ref[2 * pair + hh], first_key)
                    dp = _nt(doh, vb)
                    dvb = dvb + _tn(p.astype(BF16), doh)
                    ds = p * (dp - jnp.sum(dp * p, axis=-1, keepdims=True))
                    db_ref[2 * pair + hh] += ds
                    dsb = (ds * ATTN_SCALE).astype(BF16)
                    dq = dq + jnp.where(sel, _nn(dsb, kb), 0.0)
                    dkb = dkb + _tn(dsb, qv * _half_scale(hh, 1.0))
                dq_ref[:, lanes] = dq
                dk_ref[band, lanes] += dkb
                dv_ref[band, lanes] += dvb

        _by_padding(cb, compute)

    qspec = pl.BlockSpec((QB, ATT_LANES), lambda hg, cb: (cb, hg))
    kspec = pl.BlockSpec((PADK + S, ATT_LANES), lambda hg, cb: (0, hg))
    bspec = pl.BlockSpec((HEADS_PER_STEP, QB, KB), lambda hg, cb: (hg, 0, 0))
    kf = jax.ShapeDtypeStruct((PADK + S, D), F32)
    return _call_hosting(
        body, ex, _grid_phases((N_HG, N_QB)), name="attn_bwd", grid=(N_HG, N_QB),
        in_specs=[qspec, kspec, kspec, bspec, qspec],
        out_specs=[qspec, kspec, kspec, bspec],
        out_shape=[jax.ShapeDtypeStruct((S, D), F32), kf, kf, jax.ShapeDtypeStruct((N_HEADS, QB, KB), F32)],
        scratch_shapes=[], args=(q, kp, vp, bias, do), compiler_params=_params("arbitrary", "arbitrary"),
    )


def _adam_math(w, g, m, v):
    m = ADAM_B1 * m + (1.0 - ADAM_B1) * g
    v = ADAM_B2 * v + (1.0 - ADAM_B2) * (g * g)
    m_hat = m / (1.0 - ADAM_B1 ** ADAM_STEP)
    v_hat = v / (1.0 - ADAM_B2 ** ADAM_STEP)
    delta = -ADAM_LR * (m_hat / (jnp.sqrt(v_hat) + ADAM_EPS) + ADAM_WD * w)
    return delta, m, v


def _sum_parts(parts, name, after=None):
    count, r, c = parts.shape
    tr = r // 2 if r % 16 == 0 and r > 64 else r

    def body(p_ref, *rest):
        o_ref = rest[-1]
        acc = p_ref[0].astype(F32)
        for d in range(1, count):
            acc = acc + p_ref[d].astype(F32)
        o_ref[...] = acc

    return _call(
        body, name=name, grid=(r // tr,),
        in_specs=[pl.BlockSpec((count, tr, c), lambda i: (0, i, 0))] + ([ANY] if after is not None else []),
        out_specs=pl.BlockSpec((tr, c), lambda i: (i, 0)),
        out_shape=jax.ShapeDtypeStruct((r, c), F32), compiler_params=_params("parallel"),
    )(parts, *([after] if after is not None else []))


def _row_tile(r):
    for cand in (256, 176, 128, 64, 32, 16, 8):
        if r % cand == 0:
            return cand
    return r


def _adam_layer(w, g, m, v, layer, name, prev=None):
    nl, r, c = w.shape
    tr = _row_tile(r)

    def body(w_ref, g_ref, m_ref, v_ref, *rest):
        go_ref, d_ref, nm_ref, nv_ref = rest[-4:]
        gv = g_ref[...]
        delta, nm, nv = _adam_math(w_ref[0], gv, m_ref[0], v_ref[0])
        go_ref[0] = gv
        d_ref[0] = delta
        nm_ref[0] = nm
        nv_ref[0] = nv

    lspec = pl.BlockSpec((1, tr, c), lambda i: (layer, i, 0))
    sd = jax.ShapeDtypeStruct((nl, r, c), F32)
    extra = list(prev) if prev is not None else []
    return _call(
        body, name=name, grid=(r // tr,),
        in_specs=[lspec, pl.BlockSpec((tr, c), lambda i: (i, 0)), lspec, lspec] + [ANY] * len(extra),
        out_specs=[lspec] * 4, out_shape=[sd] * 4,
        input_output_aliases={4 + t: t for t in range(len(extra))},
        compiler_params=_params("parallel"),
    )(w, g, m, v, *extra)


def _adam(w, g, m, v, name):
    r, c = w.shape
    tr = _row_tile(r)

    def body(w_ref, g_ref, m_ref, v_ref, d_ref, nm_ref, nv_ref):
        delta, nm, nv = _adam_math(w_ref[...], g_ref[...], m_ref[...], v_ref[...])
        d_ref[...] = delta
        nm_ref[...] = nm
        nv_ref[...] = nv

    spec = pl.BlockSpec((tr, c), lambda i: (i, 0))
    sd = jax.ShapeDtypeStruct((r, c), F32)
    return _call(
        body, name=name, grid=(r // tr,), in_specs=[spec] * 4, out_specs=[spec] * 3, out_shape=[sd] * 3,
        compiler_params=_params("parallel"),
    )(w, g, m, v)


def _pad16(rows):
    return jnp.pad(rows, ((0, 16 - rows.shape[0]), (0, 0)))


def kernel(x, norm_mix_g, norm_ffn_g, pool_w, pool_b, pool_scale, kv_norm_g, w_k, w_v, k_norm_g, w_q, q_norm_g, rel_bias, w_o, w_gate, w_up, w_down, loss_target, m_norm_mix_g, m_norm_ffn_g, m_pool_w, m_pool_b, m_pool_scale, m_kv_norm_g, m_w_k, m_w_v, m_k_norm_g, m_w_q, m_q_norm_g, m_rel_bias, m_w_o, m_w_gate, m_w_up, m_w_down, v_norm_mix_g, v_norm_ffn_g, v_pool_w, v_pool_b, v_pool_scale, v_kv_norm_g, v_w_k, v_w_v, v_k_norm_g, v_w_q, v_q_norm_g, v_rel_bias, v_w_o, v_w_gate, v_w_up, v_w_down):
    assert x.shape == (1, S, D) and w_gate.shape == (2, D, F_SHARD) and w_k.shape == (D_SHARD, D)
    xin, target = x[0], loss_target[0]

    ffn_shards = [[w_gate[layer].T.astype(BF16), w_up[layer].T.astype(BF16), w_down[layer].astype(BF16)]
                  for layer in range(2)]
    att_shards = [w_k.astype(BF16), w_v.astype(BF16), w_q[0].astype(BF16), w_o[0].astype(BF16)]
    pool_shard = pool_w[0].astype(BF16).reshape(N_GROUPS * POOL_SHARD, GROUP)
    small = jnp.concatenate([pool_b[0].reshape(1, N_GROUPS * POOL_SHARD), pool_scale], axis=1)

    full0 = _run_exchange(_relay_gather_exchange(ffn_shards[0] + [pool_shard, _pad16(small)]), "gather_layer0")
    ffn_w0 = [a.reshape(F, D) for a in full0[:3]]
    pw_f = full0[3].reshape(N_DEV, N_GROUPS, POOL_SHARD, GROUP).transpose(1, 0, 2, 3).reshape(N_GROUPS, GROUP, GROUP)
    small_f = full0[4][:, 0, :]
    pb_f = small_f[:, :N_GROUPS * POOL_SHARD].reshape(N_DEV, N_GROUPS, POOL_SHARD).transpose(1, 0, 2).reshape(1, D)
    ps_f = small_f[:, N_GROUPS * POOL_SHARD:].reshape(1, D)

    g_mix0, g_mix1 = norm_mix_g[0:1], norm_mix_g[1:2]
    g_ffn0, g_ffn1 = norm_ffn_g[0:1], norm_ffn_g[1:2]
    g_kv = kv_norm_g.reshape(1, D)
    gk_t = jnp.tile(k_norm_g.reshape(1, HEAD), (1, N_HEADS))
    gq_t = jnp.tile(q_norm_g.reshape(1, HEAD), (1, N_HEADS))

    x1, diff = _pool_fwd(xin, g_mix0, pw_f, pb_f, ps_f)
    (x2, hf0, gg0, uu0), full_att = _ffn_fwd(x1, g_ffn0, *ffn_w0, name="ffn_fwd", ex=_relay_gather_exchange(att_shards))
    wk_f, wv_f, wq_f, wo_f = [a.reshape(D, D) for a in full_att]
    (hkv, h1, kpre, qpre, kp, vp, qq), full1_gate = _kvq_proj(x2, g_kv, g_mix1, wk_f, wv_f, wq_f, gk_t, gq_t,
                                                                ex=_relay_gather_exchange(ffn_shards[1][:1]))
    bias = _bias_band(_toeplitz_from_table(rel_bias[0]))
    (att,), full1_rest = _attn_fwd(qq, kp, vp, bias, ex=_relay_gather_exchange(ffn_shards[1][1:]))
    ffn_w1 = [a.reshape(F, D) for a in full1_gate + full1_rest]
    x3 = _mm(att, wo_f, "nn", F32, "attn_out", add=x2)
    (dx4, loss_rows, hf1, gg1, uu1), _ = _ffn_fwd(x3, g_ffn1, *ffn_w1, name="ffn_fwd_loss", target=target)

    def blocks(dw):
        return dw.reshape(N_DEV, dw.shape[0] // N_DEV, dw.shape[1])

    (dgg1, duu1, dwg1, dwu1, dwd1), _ = _ffn_bwd_weights(dx4, hf1, gg1, uu1, ffn_w1[2], name="ffn_bwd1")
    parts1 = [blocks(dw) for dw in (dwg1, dwu1, dwd1)]
    (dx3, dg_ffn1), stage1 = _ffn_bwd_input(dx4, x3, g_ffn1, [dgg1, duu1], ffn_w1[:2], "ffn_dx1",
                                            ex=_pair_exchange(parts1))
    both1 = _pair_add(parts1, stage1, "pair_add_ffn1", copies=2)
    flight1, chip1, land1, _ = _chip_exchange_start(both1[:3], both1[3:], dx3, "scatter_ffn1_start")
    datt = _mm(dx3, wo_f, "nt", BF16, "d_attn")
    dwo = _mm(att, dx3, "tn", BF16, "d_wo")
    (dq, dkp, dvp, dband), _ = _attn_bwd(qq, kp, vp, bias, datt)
    (dx2, dqpre, dkpre, dg_mix1, dg_kv, dgq, dgk), _ = _qkv_input_bwd(
        dq, qpre, gq_t, dkp, kpre, gk_t, dvp, wq_f, wk_f, wv_f, dx3, x2, g_mix1, g_kv)
    dwq = _mm(h1, dqpre, "tn", BF16, "d_wq")
    dwk = _mm(hkv, dkpre, "tn", BF16, "d_wk")
    dwv = _mm(hkv, dvp, "tn", BF16, "d_wv", skip_rows=PADK)
    parts_att = [blocks(dw) for dw in (dwk, dwv, dwq, dwo)]
    (dgg0, duu0, *dw0), stage_att = _ffn_bwd_weights(
        dx2, hf0, gg0, uu0, ffn_w0[2], name="ffn_bwd0", ex=_pair_exchange(parts_att))
    both_att = _pair_add(parts_att, stage_att, "pair_add_att", copies=2)
    flight_att, chip_att, land_att, _ = _chip_exchange_start(both_att[:4], both_att[4:], dx2, "scatter_att_start")
    parts0 = [blocks(dw) for dw in dw0]
    (dx1, dg_ffn0), stage0 = _ffn_bwd_input(dx2, x1, g_ffn0, [dgg0, duu0], ffn_w0[:2], "ffn_dx0",
                                            ex=_pair_exchange(parts0))
    both = _pair_add(parts0, stage0, "pair_add_ffn0", copies=2)
    flight, chip0, land0, token = _chip_exchange_start(both[:3], both[3:], dx1, "scatter_ffn0_start")
    (grad_x, dpw, db_rows, ds_rows, dg_mix0), _ = _pool_bwd(dx1, xin, diff, g_mix0 + token[0:1, 0:1], pw_f, pb_f, ps_f)
    dtp, seg = _bias_grad(dband, dpw)

    weights = dict(norm_mix_g=norm_mix_g, norm_ffn_g=norm_ffn_g, pool_w=pool_w, pool_b=pool_b,
                   pool_scale=pool_scale, kv_norm_g=kv_norm_g, w_k=w_k, w_v=w_v, k_norm_g=k_norm_g, w_q=w_q,
                   q_norm_g=q_norm_g, rel_bias=rel_bias, w_o=w_o, w_gate=w_gate, w_up=w_up, w_down=w_down)
    mom1 = dict(norm_mix_g=m_norm_mix_g, norm_ffn_g=m_norm_ffn_g, pool_w=m_pool_w, pool_b=m_pool_b,
                pool_scale=m_pool_scale, kv_norm_g=m_kv_norm_g, w_k=m_w_k, w_v=m_w_v, k_norm_g=m_k_norm_g,
                w_q=m_w_q, q_norm_g=m_q_norm_g, rel_bias=m_rel_bias, w_o=m_w_o, w_gate=m_w_gate, w_up=m_w_up,
                w_down=m_w_down)
    mom2 = dict(norm_mix_g=v_norm_mix_g, norm_ffn_g=v_norm_ffn_g, pool_w=v_pool_w, pool_b=v_pool_b,
                pool_scale=v_pool_scale, kv_norm_g=v_kv_norm_g, w_k=v_w_k, w_v=v_w_v, k_norm_g=v_k_norm_g,
                w_q=v_w_q, q_norm_g=v_q_norm_g, rel_bias=v_rel_bias, w_o=v_w_o, w_gate=v_w_gate, w_up=v_w_up,
                w_down=v_w_down)
    names = list(weights)
    grads, deltas, new_m, new_v = {}, {}, {}, {}

    def summed(arrs, tag, after):
        return [_sum_parts(a, "sum_parts_%s%d" % (tag, k), after) for k, a in enumerate(arrs)]

    def adam_flat(nm):
        shape = weights[nm].shape
        flat = lambda a: a.reshape(-1, shape[-1])
        dl, m1, m2 = _adam(flat(weights[nm]), flat(grads[nm]), flat(mom1[nm]), flat(mom2[nm]), "adam_" + nm)
        deltas[nm], new_m[nm], new_v[nm] = dl.reshape(shape), m1.reshape(shape), m2.reshape(shape)

    ffn_names = ("w_gate", "w_up", "w_down")
    def ffn_view(nm, a):
        return a if nm == "w_down" else a.transpose(0, 2, 1)

    recv1 = _chip_exchange_wait(flight1, chip1, land1, dtp, "scatter_ffn1_wait")
    sums1 = summed(recv1, "ffn1_", token)
    layer1 = {nm: _adam_layer(ffn_view(nm, weights[nm]), g, ffn_view(nm, mom1[nm]), ffn_view(nm, mom2[nm]), 1,
                              "adam1_" + nm)
              for nm, g in zip(ffn_names, sums1)}
    recv_att = _chip_exchange_wait(flight_att, chip_att, land_att, layer1["w_down"][1], "scatter_att_wait")
    g_wk, g_wv, g_wq, g_wo = summed(recv_att, "att_", token)
    grads.update(w_k=g_wk, w_v=g_wv, w_q=g_wq[None], w_o=g_wo[None])
    for nm in ("w_k", "w_v", "w_q", "w_o"):
        adam_flat(nm)

    dpw_blocks = dpw.reshape(N_GROUPS, N_DEV, POOL_SHARD, GROUP).transpose(1, 0, 2, 3)
    dpw_blocks = dpw_blocks.reshape(N_DEV, N_GROUPS * POOL_SHARD, GROUP).astype(BF16)
    recv_pool = _run_exchange(_scatter_exchange([dpw_blocks]), "scatter_pool", after=deltas["w_o"])
    misc = jnp.concatenate([dgk[0:1, 0:HEAD], dgq[0:1, 0:HEAD], loss_rows[0:1, 0:1],
                            seg[:, 0, 0].reshape(1, N_HEADS), seg[:, 0, 1].reshape(1, N_HEADS)], axis=1)
    misc = jnp.pad(misc, ((0, 0), (0, D - misc.shape[1])))
    vec_rows = jnp.concatenate([dg_mix0[0:1], dg_mix1[0:1], dg_ffn0[0:1], dg_ffn1[0:1], dg_kv[0:1],
                                db_rows[0:1], ds_rows[0:1], misc], axis=0)
    pack = jnp.concatenate([vec_rows, dtp.reshape(N_HEADS, TOEP)], axis=0)
    tot = _all_reduce_small(pack, "reduce_small", after=recv_pool[0])

    loss = tot[7, 2 * HEAD]
    seg_tot = jnp.stack([tot[7, 2 * HEAD + 1:2 * HEAD + 1 + N_HEADS],
                         tot[7, 2 * HEAD + 1 + N_HEADS:2 * HEAD + 1 + 2 * N_HEADS]], axis=1)
    me = 4 * lax.axis_index("x") + 2 * lax.axis_index("y") + lax.axis_index("c")
    g_pool_b = lax.dynamic_slice_in_dim(tot[5].reshape(N_GROUPS, GROUP), me * POOL_SHARD, POOL_SHARD, axis=1)
    grads.update(
        norm_mix_g=tot[0:2], norm_ffn_g=tot[2:4], kv_norm_g=tot[4], k_norm_g=tot[7, 0:HEAD],
        q_norm_g=tot[7, HEAD:2 * HEAD].reshape(1, HEAD),
        rel_bias=_table_grad_from_toeplitz(tot[8:8 + N_HEADS], seg_tot).reshape(1, N_HEADS, N_REL),
        pool_b=g_pool_b.reshape(1, N_GROUPS, POOL_SHARD),
        pool_scale=lax.dynamic_slice_in_dim(tot[6:7], me * D_SHARD, D_SHARD, axis=1),
        pool_w=summed(recv_pool, "pool_", None)[0].reshape(1, N_GROUPS, POOL_SHARD, GROUP))
    adam_flat("pool_w")
    small_names = [nm for nm in names if nm not in ffn_names + ("w_k", "w_v", "w_q", "w_o", "pool_w")]

    def pack_small(tree):
        cols = []
        for nm in small_names:
            flat = tree[nm].reshape(-1)
            cols.append(jnp.pad(flat, (0, -flat.shape[0] % 1024)))
        return jnp.concatenate(cols).reshape(-1, 128)

    dl, m1, m2 = _adam(pack_small(weights), pack_small(grads), pack_small(mom1), pack_small(mom2), "adam_small")

    def unpack_small(packed, out):
        flat, off = packed.reshape(-1), 0
        for nm in small_names:
            size = weights[nm].size
            out[nm] = flat[off:off + size].reshape(weights[nm].shape)
            off += size + (-size % 1024)

    unpack_small(dl, deltas)
    unpack_small(m1, new_m)
    unpack_small(m2, new_v)

    recv0 = _chip_exchange_wait(flight, chip0, land0, dl, "scatter_ffn0_wait")
    sums0 = summed(recv0, "ffn0_", None)
    for nm, g in zip(ffn_names, sums0):
        done = _adam_layer(ffn_view(nm, weights[nm]), g, ffn_view(nm, mom1[nm]), ffn_view(nm, mom2[nm]), 0,
                           "adam0_" + nm, prev=layer1[nm])
        grads[nm], deltas[nm], new_m[nm], new_v[nm] = [ffn_view(nm, a) for a in done]

    return (loss, grad_x[None], *[grads[nm] for nm in names], *[deltas[nm] for nm in names],
            *[new_m[nm] for nm in names], *[new_v[nm] for nm in names])
```

```python
import functools

import jax
import jax.numpy as jnp
from jax import lax
from jax.experimental import pallas as pl
from jax.experimental.pallas import tpu as pltpu

F32 = jnp.float32
BF16 = jnp.bfloat16
MESH_ID = pl.DeviceIdType.MESH

N_DEV = 8
S = 2048
D = 1024
F = 2816
F_SHARD = F // N_DEV
D_SHARD = D // N_DEV
N_GROUPS = 4
GROUP = D // N_GROUPS
POOL_SHARD = GROUP // N_DEV
MAX_WIN = 16
HEAD = 64
N_HEADS = D // HEAD
CHUNK = 64
LEFT = 8
QB = 4 * CHUNK
KB = QB + LEFT * CHUNK
PADK = LEFT * CHUNK
TOEP = 1024
N_REL = 257
MAX_REL = 128
EPS = 1e-6
NEG_INF = -1e30
ATTN_SCALE = HEAD ** -0.5

ADAM_LR = 0.001
ADAM_B1 = 0.9
ADAM_B2 = 0.999
ADAM_EPS = 1e-08
ADAM_WD = 0.01
ADAM_STEP = 10

VMEM_LIMIT = 52 * 1024 * 1024

ANY = pl.BlockSpec(memory_space=pl.ANY)
VMEM = pl.BlockSpec(memory_space=pltpu.VMEM)


def _call(body, **kw):
    return pl.pallas_call(body, **kw)


def _params(*sem):
    return pltpu.CompilerParams(dimension_semantics=sem, vmem_limit_bytes=VMEM_LIMIT)


def _dot(a, b, dims):
    return lax.dot_general(a, b, (dims, ((), ())), preferred_element_type=F32)


def _nn(a, b):
    return _dot(a, b, ((1,), (0,)))


def _nt(a, b):
    return _dot(a, b, ((1,), (1,)))


def _tn(a, b):
    return _dot(a, b, ((0,), (0,)))


def _rstd(x):
    return lax.rsqrt(jnp.mean(x * x, axis=-1, keepdims=True) + EPS)


def _rms_bwd(dh, x, r, g):
    gd = dh * g
    return r * gd - x * (r * r * r) * jnp.mean(gd * x, axis=-1, keepdims=True)


def _colsum8(v):
    return jnp.broadcast_to(jnp.sum(v, axis=0, keepdims=True), (8, v.shape[1]))


def _seg_sum(v):
    r = lax.broadcasted_iota(jnp.int32, (128, 128), 0) // HEAD
    c = lax.broadcasted_iota(jnp.int32, (128, 128), 1) // HEAD
    ones = jnp.where(r == c, 1.0, 0.0).astype(BF16)
    out = []
    for blk in range(v.shape[1] // 128):
        part = v[:, blk * 128:(blk + 1) * 128]
        hi = part.astype(BF16)
        rest = part - hi.astype(F32)
        mid = rest.astype(BF16)
        lo = (rest - mid.astype(F32)).astype(BF16)
        out.append(_nn(hi, ones) + _nn(mid, ones) + _nn(lo, ones))
    return jnp.concatenate(out, axis=1)


def _place():
    return lax.axis_index("x"), lax.axis_index("y"), lax.axis_index("c")


class _Exchange:
    def __init__(self, ins, out_shape, sems, start, mid, finish, early=None):
        self.ins, self.out_shape, self.sems = list(ins), list(out_shape), list(sems)
        self.start, self.mid, self.finish = start, mid, finish
        self.early = early if early is not None else (lambda ins, outs, sems: None)


def _relay_gather_exchange(shards):
    n = len(shards)

    def tools(ins, outs, sems):
        send_sems, recv_sems, local_sems = sems
        x, y, c = _place()
        me, sib = (x, y, c), (x, y, 1 - c)
        xn, yn, dg = (1 - x, y, c), (x, 1 - y, c), (1 - x, 1 - y, c)
        other = lambda dev: (dev[0], dev[1], 1 - c)

        def piece(k, ref, half):
            rows = shards[k].shape[0] // 2
            return ref if half is None else ref.at[pl.ds(half * rows, rows)]

        def held(k, dev, half):
            return piece(k, outs[k].at[4 * dev[0] + 2 * dev[1] + dev[2]], half)

        def copy(k, s, dev, half, to, own=False):
            return pltpu.make_async_remote_copy(
                src_ref=piece(k, ins[k], half) if own else held(k, dev, half), dst_ref=held(k, dev, half),
                send_sem=send_sems.at[k, s], recv_sem=recv_sems.at[k, s], device_id=to, device_id_type=MESH_ID)

        def mine(k):
            return pltpu.make_async_copy(ins[k], held(k, me, None), local_sems.at[k])

        def sent(k):
            return [copy(k, 0, me, None, sib, own=True), copy(k, 1, me, 0, xn, own=True), copy(k, 2, me, 1, yn, own=True),
                    copy(k, 3, me, 0, yn, own=True), copy(k, 4, me, 1, xn, own=True)]

        relays = [(1, xn, 0, [(5, yn), (7, sib)]), (2, yn, 1, [(6, xn), (9, sib)]), (3, yn, 0, [(10, sib)]),
                  (4, xn, 1, [(8, sib)]), (6, dg, 1, [(11, sib)]), (5, dg, 0, [(12, sib)])]
        from_sibling = [(0, other(me), None), (7, other(xn), 0), (8, other(xn), 1), (9, other(yn), 1),
                        (10, other(yn), 0), (11, other(dg), 1), (12, other(dg), 0)]
        return mine, sent, copy, relays, from_sibling, me

    def start(ins, outs, sems):
        mine, sent, _, _, _, _ = tools(ins, outs, sems)
        for k in range(n):
            mine(k).start()
        for k in range(n):
            for cp in sent(k):
                cp.start()

    def pass_on(ins, outs, sems, which):
        _, _, copy, relays, _, me = tools(ins, outs, sems)
        for k in range(n):
            for s, dev, half, onward in relays[which]:
                copy(k, s, dev, half, me).wait_recv()
                for s2, to in onward:
                    copy(k, s2, dev, half, to).start()

    def early(ins, outs, sems):
        pass_on(ins, outs, sems, slice(0, 2))

    def mid(ins, outs, sems):
        pass_on(ins, outs, sems, slice(2, None))

    def finish(ins, outs, sems):
        mine, sent, copy, relays, from_sibling, me = tools(ins, outs, sems)
        for k in range(n):
            for s, dev, half in from_sibling:
                copy(k, s, dev, half, me).wait_recv()
        for k in range(n):
            for cp in sent(k):
                cp.wait_send()
            for _, dev, half, onward in relays:
                for s2, to in onward:
                    copy(k, s2, dev, half, to).wait_send()
            mine(k).wait()

    return _Exchange(
        shards, [jax.ShapeDtypeStruct((N_DEV,) + a.shape, a.dtype) for a in shards],
        [pltpu.SemaphoreType.DMA((n, 13)), pltpu.SemaphoreType.DMA((n, 13)), pltpu.SemaphoreType.DMA((n,))],
        start, mid, finish, early)


def _peer(x, y, c, m):
    px = 1 - x if m & 4 else x
    py = 1 - y if m & 2 else y
    pc = 1 - c if m & 1 else c
    return px, py, pc


def _scatter_exchange(parts):
    n = len(parts)

    def tools(ins, outs, sems):
        send_sems, recv_sems, local_sems = sems
        x, y, c = _place()
        me = 4 * x + 2 * y + c
        def mine():
            return [pltpu.make_async_copy(ins[k].at[me], outs[k].at[me], local_sems.at[k]) for k in range(n)]

        def remote(dst_is_mine):
            out = []
            for m in range(1, N_DEV):
                px, py, pc = _peer(x, y, c, m)
                peer = 4 * px + 2 * py + pc
                for k in range(n):
                    out.append(pltpu.make_async_remote_copy(
                        src_ref=ins[k].at[peer], dst_ref=outs[k].at[me if dst_is_mine else peer],
                        send_sem=send_sems.at[k, m - 1], recv_sem=recv_sems.at[k, m - 1],
                        device_id=(px, py, pc), device_id_type=MESH_ID))
            return out

        return mine, remote

    def start(ins, outs, sems):
        mine, remote = tools(ins, outs, sems)
        for cp in mine() + remote(True):
            cp.start()

    def mid(ins, outs, sems):
        pass

    def finish(ins, outs, sems):
        mine, remote = tools(ins, outs, sems)
        for cp in remote(False):
            cp.wait_recv()
        for cp in remote(True):
            cp.wait_send()
        for cp in mine():
            cp.wait()

    return _Exchange(
        parts, [jax.ShapeDtypeStruct(a.shape, a.dtype) for a in parts],
        [pltpu.SemaphoreType.DMA((n, 7)), pltpu.SemaphoreType.DMA((n, 7)), pltpu.SemaphoreType.DMA((n,))],
        start, mid, finish)


N_CHIPS = N_DEV // 2


def _pair_exchange(parts):
    n = len(parts)

    def copies(ins, outs, sems):
        send_sems, recv_sems = sems
        x, y, c = _place()
        return [pltpu.make_async_remote_copy(
            src_ref=ins[k].at[2 * q + 1 - c], dst_ref=outs[k].at[q], send_sem=send_sems.at[k, q],
            recv_sem=recv_sems.at[k, q], device_id=(x, y, 1 - c), device_id_type=MESH_ID)
            for k in range(n) for q in range(N_CHIPS)]

    def start(ins, outs, sems):
        for cp in copies(ins, outs, sems):
            cp.start()

    def mid(ins, outs, sems):
        pass

    def finish(ins, outs, sems):
        for cp in copies(ins, outs, sems):
            cp.wait_recv()
        for cp in copies(ins, outs, sems):
            cp.wait_send()

    return _Exchange(
        parts, [jax.ShapeDtypeStruct((N_CHIPS,) + a.shape[1:], a.dtype) for a in parts],
        [pltpu.SemaphoreType.DMA((n, N_CHIPS)), pltpu.SemaphoreType.DMA((n, N_CHIPS))], start, mid, finish)


def _pair_add(parts, stage, name, copies=1):
    n = len(parts)
    core = lax.axis_index("c").reshape(1)

    def body(core_ref, *refs):
        for k in range(n):
            mine, theirs = refs[k], refs[n + k]
            total = (mine[0, 0].astype(F32) + theirs[0].astype(F32)).astype(BF16)
            for rep in range(copies):
                refs[(2 + rep) * n + k][0] = total

    in_specs, out_specs = [], []
    for a in parts:
        _, r, cdim = a.shape
        in_specs.append(pl.BlockSpec((1, 1, r // 2, cdim), lambda q, i, core_ref: (q, core_ref[0], i, 0)))
    for a in parts:
        _, r, cdim = a.shape
        in_specs.append(pl.BlockSpec((1, r // 2, cdim), lambda q, i, core_ref: (q, i, 0)))
        out_specs.append(pl.BlockSpec((1, r // 2, cdim), lambda q, i, core_ref: (q, i, 0)))
    return list(_call(
        body, name=name,
        grid_spec=pltpu.PrefetchScalarGridSpec(num_scalar_prefetch=1, grid=(N_CHIPS, 2), in_specs=in_specs,
                                               out_specs=out_specs * copies),
        out_shape=[jax.ShapeDtypeStruct(s.shape, BF16) for s in stage] * copies,
        compiler_params=_params("arbitrary", "arbitrary"),
    )(core, *[a.reshape((N_CHIPS, 2) + a.shape[1:]) for a in parts], *stage))


HBM = pl.BlockSpec(memory_space=pltpu.HBM)
SEMAPHORES = pl.BlockSpec(memory_space=pltpu.SEMAPHORE)


def _chip_copies(srcs, lands, send_sems, recv_sems, mine_is_dst):
    x, y, c = _place()
    me = 2 * x + y
    out = []
    for m in range(1, N_CHIPS):
        px, py, _ = _peer(x, y, c, 2 * m)
        peer = 2 * px + py
        for k in range(len(srcs)):
            pair = k * (N_CHIPS - 1) + m - 1
            out.append(pltpu.make_async_remote_copy(
                src_ref=srcs[k].at[peer], dst_ref=lands[k].at[me if mine_is_dst else peer],
                send_sem=send_sems[pair], recv_sem=recv_sems[pair],
                device_id=(px, py, c), device_id_type=MESH_ID))
    return out


def _chip_exchange_start(chip_parts, lands, after, name):
    n = len(chip_parts)
    pairs = n * (N_CHIPS - 1)

    def body(*refs):
        srcs, zones = refs[:n], refs[n:2 * n]
        sems = refs[2 * n + 1:2 * n + 1 + 2 * pairs]
        token = refs[-1]
        for cp in _chip_copies(srcs, zones, sems[:pairs], sems[pairs:], True):
            cp.start()
        token[...] = jnp.zeros(token.shape, F32)

    thru = [pltpu.HBM(a.shape, a.dtype) for a in chip_parts + lands]
    hbm = [pltpu.with_memory_space_constraint(a, pltpu.HBM) for a in chip_parts + lands]
    res = _call(
        body, name=name,
        out_shape=[pltpu.SemaphoreType.DMA(())] * (2 * pairs) + thru + [jax.ShapeDtypeStruct((8, 128), F32)],
        in_specs=[HBM] * (2 * n) + [ANY], out_specs=[SEMAPHORES] * (2 * pairs) + [HBM] * (2 * n) + [VMEM],
        input_output_aliases={i: 2 * pairs + i for i in range(2 * n)},
        compiler_params=pltpu.CompilerParams(has_side_effects=pltpu.SideEffectType.DATAFLOW_SIDE_EFFECTING),
    )(*hbm, after)
    sems, rest = list(res[:2 * pairs]), res[2 * pairs:]
    return sems, list(rest[:n]), list(rest[n:2 * n]), rest[-1]


def _chip_exchange_wait(sems, chip_parts, lands, after, name):
    n = len(chip_parts)
    pairs = n * (N_CHIPS - 1)

    def body(*refs):
        srcs, zones = refs[:n], refs[n:2 * n]
        sem_refs = refs[2 * n:2 * n + 2 * pairs]
        for cp in _chip_copies(srcs, zones, sem_refs[:pairs], sem_refs[pairs:], False):
            cp.wait_send()
            cp.wait_recv()

    thru = [pltpu.HBM(a.shape, a.dtype) for a in chip_parts + lands]
    res = _call(
        body, name=name, out_shape=thru,
        in_specs=[HBM] * (2 * n) + [SEMAPHORES] * (2 * pairs) + [ANY], out_specs=[HBM] * (2 * n),
        input_output_aliases={i: i for i in range(2 * n)},
        compiler_params=pltpu.CompilerParams(has_side_effects=pltpu.SideEffectType.DATAFLOW_SIDE_EFFECTING),
    )(*chip_parts, *lands, *sems, after)
    return list(res[n:])


def _run_exchange(ex, name, after=None):
    n_in, n_out = len(ex.ins), len(ex.out_shape)
    order = [] if after is None else [after]

    def body(*refs):
        ins, outs = refs[:n_in], refs[n_in + len(order):n_in + len(order) + n_out]
        sems = refs[n_in + len(order) + n_out:]
        ex.start(ins, outs, sems)
        ex.early(ins, outs, sems)
        ex.mid(ins, outs, sems)
        ex.finish(ins, outs, sems)

    return list(_call(body, name=name, out_shape=ex.out_shape, in_specs=[ANY] * (n_in + len(order)),
                      out_specs=[ANY] * n_out, scratch_shapes=ex.sems)(*ex.ins, *order))


def _call_hosting(body, ex, phases, *, in_specs, out_specs, out_shape, scratch_shapes, args, **kw):
    n_in, n_out, n_scr = len(in_specs), len(out_specs), len(scratch_shapes)
    if ex is None:
        res = _call(body, in_specs=in_specs, out_specs=out_specs, out_shape=out_shape,
                    scratch_shapes=scratch_shapes, **kw)(*args)
        return list(res), []
    n_xin, n_xout = len(ex.ins), len(ex.out_shape)

    def hosting(*refs):
        a, b = n_in, n_in + n_xin
        c, d = b + n_out, b + n_out + n_xout
        ins, xins, outs, xouts = refs[:a], refs[a:b], refs[b:c], refs[c:d]
        scr, sems = refs[d:d + n_scr], refs[d + n_scr:]
        first, early, mid, last = phases()

        @pl.when(first)
        def _():
            ex.start(xins, xouts, sems)

        body(*ins, *outs, *scr)

        @pl.when(early)
        def _():
            ex.early(xins, xouts, sems)

        @pl.when(mid)
        def _():
            ex.mid(xins, xouts, sems)

        @pl.when(last)
        def _():
            ex.finish(xins, xouts, sems)

    res = _call(hosting, in_specs=list(in_specs) + [ANY] * n_xin, out_specs=list(out_specs) + [ANY] * n_xout,
                out_shape=list(out_shape) + ex.out_shape, scratch_shapes=list(scratch_shapes) + ex.sems,
                **kw)(*args, *ex.ins)
    return list(res[:n_out]), list(res[n_out:])


def _grid_phases(dims, mid_fraction=0.8, early_fraction=0.4):
    total = 1
    for d in dims:
        total *= d
    mid = min(max(int(total * mid_fraction), 1), total - 1)
    early = min(int(total * early_fraction), mid)

    def phases():
        step = pl.program_id(0)
        for axis in range(1, len(dims)):
            step = step * dims[axis] + pl.program_id(axis)
        return step == 0, step == early, step == mid, step == total - 1
    return phases


def _all_reduce_small(pack, name, after):
    rows = pack.shape[0]

    def body(in_ref, _, out_ref, recv, send_sems, recv_sems):
        x, y, c = _place()
        me = 4 * x + 2 * y + c
        recv[me] = in_ref[...]
        sent = []
        for m in range(1, N_DEV):
            px, py, pc = _peer(x, y, c, m)
            cp = pltpu.make_async_remote_copy(
                src_ref=in_ref, dst_ref=recv.at[me], send_sem=send_sems.at[m - 1], recv_sem=recv_sems.at[m - 1],
                device_id=(px, py, pc), device_id_type=MESH_ID)
            cp.start()
            sent.append(cp)
        for m in range(1, N_DEV):
            px, py, pc = _peer(x, y, c, m)
            peer = 4 * px + 2 * py + pc
            pltpu.make_async_remote_copy(
                src_ref=in_ref, dst_ref=recv.at[peer], send_sem=send_sems.at[m - 1], recv_sem=recv_sems.at[m - 1],
                device_id=(px, py, pc), device_id_type=MESH_ID).wait_recv()
        acc = recv[0]
        for d in range(1, N_DEV):
            acc = acc + recv[d]
        out_ref[...] = acc
        for cp in sent:
            cp.wait_send()

    return _call(
        body, name=name, out_shape=jax.ShapeDtypeStruct(pack.shape, F32), in_specs=[VMEM, ANY], out_specs=VMEM,
        scratch_shapes=[pltpu.VMEM((N_DEV, rows, pack.shape[1]), F32), pltpu.SemaphoreType.DMA((7,)),
                        pltpu.SemaphoreType.DMA((7,))],
        compiler_params=pltpu.CompilerParams(vmem_limit_bytes=VMEM_LIMIT),
    )(pack, after)


POOL_TS = 256


def _pool_counts(first_row, rows, win):
    t = first_row + lax.broadcasted_iota(jnp.int32, (rows, 1), 0)
    return jnp.minimum(t + 1, win).astype(F32)


def _pool_fwd(x, g, w, b, scale):
    nt = S // POOL_TS

    def body(x_ref, g_ref, w_ref, b_ref, s_ref, out_ref, diff_ref, ext):
        i = pl.program_id(0)

        @pl.when(i == 0)
        def _():
            ext[0:MAX_WIN, :] = jnp.zeros((MAX_WIN, D), F32)

        @pl.when(i > 0)
        def _():
            ext[0:MAX_WIN, :] = ext[POOL_TS:POOL_TS + MAX_WIN, :]

        xv = x_ref[...]
        h = xv * _rstd(xv) * g_ref[...]
        ext[MAX_WIN:, :] = h
        for gi in range(N_GROUPS):
            win = 2 << gi
            cols = slice(gi * GROUP, (gi + 1) * GROUP)
            sm = ext[:, cols]
            k = 1
            while k < win:
                sm = sm + pltpu.roll(sm, k, axis=0)
                k *= 2
            pooled = sm[MAX_WIN:, :] / _pool_counts(i * POOL_TS, POOL_TS, win)
            diff = (pooled - h[:, cols]).astype(BF16)
            yv = (_nn(diff, w_ref[gi]) + b_ref[:, cols]) * s_ref[:, cols]
            out_ref[:, cols] = xv[:, cols] + yv
            diff_ref[:, cols] = diff

    row = pl.BlockSpec((1, D), lambda i: (0, 0))
    tile = pl.BlockSpec((POOL_TS, D), lambda i: (i, 0))
    return _call(
        body, name="pool_fwd", grid=(nt,),
        in_specs=[tile, row, pl.BlockSpec((N_GROUPS, GROUP, GROUP), lambda i: (0, 0, 0)), row, row],
        out_specs=[tile, tile],
        out_shape=[jax.ShapeDtypeStruct((S, D), F32), jax.ShapeDtypeStruct((S, D), BF16)],
        scratch_shapes=[pltpu.VMEM((POOL_TS + MAX_WIN, D), F32)],
        compiler_params=_params("arbitrary"),
    )(x, g, w, b, scale)


def _pool_bwd(dy, x, diff, g, w, b, scale, ex=None):
    nt = S // POOL_TS

    def body(dy_ref, x_ref, diff_ref, g_ref, w_ref, b_ref, s_ref, gx_ref, dw_ref, db_ref, ds_ref, dg_ref, ext, dh):
        i = pl.program_id(0)
        first_row = (nt - 1 - i) * POOL_TS

        @pl.when(i == 0)
        def _():
            ext[POOL_TS:, :] = jnp.zeros((MAX_WIN, D), F32)
            dw_ref[...] = jnp.zeros(dw_ref.shape, F32)
            db_ref[...] = jnp.zeros(db_ref.shape, F32)
            ds_ref[...] = jnp.zeros(ds_ref.shape, F32)
            dg_ref[...] = jnp.zeros(dg_ref.shape, F32)

        @pl.when(i > 0)
        def _():
            ext[POOL_TS:, :] = ext[0:MAX_WIN, :]

        dyv = dy_ref[...]
        for gi in range(N_GROUPS):
            win = 2 << gi
            cols = slice(gi * GROUP, (gi + 1) * GROUP)
            dfb = diff_ref[:, cols]
            z = _nn(dfb, w_ref[gi]) + b_ref[:, cols]
            dyg = dyv[:, cols]
            ds_ref[:, cols] += _colsum8(dyg * z)
            dz = dyg * s_ref[:, cols]
            db_ref[:, cols] += _colsum8(dz)
            dzb = dz.astype(BF16)
            dw_ref[gi] += _tn(dfb, dzb)
            ddiff = _nt(dzb, w_ref[gi])
            ext[0:POOL_TS, cols] = ddiff / _pool_counts(first_row, POOL_TS, win)
            sm = ext[:, cols]
            k = 1
            while k < win:
                sm = sm + pltpu.roll(sm, POOL_TS + MAX_WIN - k, axis=0)
                k *= 2
            dh[:, cols] = sm[0:POOL_TS, :] - ddiff
        xv = x_ref[...]
        r = _rstd(xv)
        gv = g_ref[...]
        dhv = dh[...]
        dg_ref[...] += _colsum8(dhv * xv * r)
        gx_ref[...] = dyv + _rms_bwd(dhv, xv, r, gv)

    row = pl.BlockSpec((1, D), lambda i: (0, 0))
    tile = pl.BlockSpec((POOL_TS, D), lambda i: (nt - 1 - i, 0))
    acc = pl.BlockSpec((8, D), lambda i: (0, 0))
    wspec = pl.BlockSpec((N_GROUPS, GROUP, GROUP), lambda i: (0, 0, 0))
    return _call_hosting(
        body, ex, _grid_phases((nt,)), name="pool_bwd", grid=(nt,),
        in_specs=[tile, tile, tile, row, wspec, row, row],
        out_specs=[tile, wspec, acc, acc, acc],
        out_shape=[jax.ShapeDtypeStruct((S, D), F32), jax.ShapeDtypeStruct((N_GROUPS, GROUP, GROUP), F32),
                   jax.ShapeDtypeStruct((8, D), F32), jax.ShapeDtypeStruct((8, D), F32),
                   jax.ShapeDtypeStruct((8, D), F32)],
        scratch_shapes=[pltpu.VMEM((POOL_TS + MAX_WIN, D), F32), pltpu.VMEM((POOL_TS, D), F32)],
        args=(dy, x, diff, g, w, b, scale), compiler_params=_params("arbitrary"),
    )


FFN_TS = min(S, 1024)
FFN_TF = 256


def _ffn_fwd(x, g, wg_t, wu_t, wd, name, target=None, ex=None):
    width = wd.shape[0]
    ni, nj = S // FFN_TS, width // FFN_TF
    with_loss = target is not None
    extra = [target] if with_loss else []

    def body(*refs):
        x_ref, g_ref, wg_ref, wu_ref, wd_ref = refs[:5]
        if with_loss:
            t_ref, out_ref, loss_ref, h_ref, gg_ref, uu_ref, hs, acc = refs[5:]
        else:
            out_ref, h_ref, gg_ref, uu_ref, hs, acc = refs[5:]
        i, j = pl.program_id(0), pl.program_id(1)

        @pl.when(j == 0)
        def _():
            xv = x_ref[...]
            hb = (xv * _rstd(xv) * g_ref[...]).astype(BF16)
            hs[...] = hb
            h_ref[...] = hb
            acc[...] = jnp.zeros(acc.shape, F32)

        hb = hs[...]
        gg = _nt(hb, wg_ref[...])
        uu = _nt(hb, wu_ref[...])
        gg_ref[...] = gg
        uu_ref[...] = uu
        a = (gg * jax.nn.sigmoid(gg) * uu).astype(BF16)
        acc[...] += _nn(a, wd_ref[...])

        @pl.when(j == nj - 1)
        def _():
            yv = x_ref[...] + acc[...]
            if with_loss:
                err = yv - t_ref[...]
                out_ref[...] = err * (1.0 / D)
                part = jnp.sum(err * err) * (0.5 / D)

                @pl.when(i == 0)
                def _():
                    loss_ref[...] = jnp.zeros(loss_ref.shape, F32)

                loss_ref[...] += jnp.broadcast_to(part, loss_ref.shape)
            else:
                out_ref[...] = yv

    xt = pl.BlockSpec((FFN_TS, D), lambda i, j: (i, 0))
    row = pl.BlockSpec((1, D), lambda i, j: (0, 0))
    wt = pl.BlockSpec((FFN_TF, D), lambda i, j: (j, 0))
    gt = pl.BlockSpec((FFN_TS, FFN_TF), lambda i, j: (i, j))
    in_specs = [xt, row, wt, wt, wt] + [xt] * len(extra)
    out_specs = [xt] + ([pl.BlockSpec((8, 128), lambda i, j: (0, 0))] if with_loss else []) + [xt, gt, gt]
    out_shape = ([jax.ShapeDtypeStruct((S, D), F32)] + ([jax.ShapeDtypeStruct((8, 128), F32)] if with_loss else [])
                 + [jax.ShapeDtypeStruct((S, D), BF16), jax.ShapeDtypeStruct((S, width), F32),
                    jax.ShapeDtypeStruct((S, width), F32)])
    args = (x, g, wg_t, wu_t, wd, *extra)
    return _call_hosting(
        body, ex, _grid_phases((ni, nj), 0.8), name=name, grid=(ni, nj),
        in_specs=in_specs, out_specs=out_specs, out_shape=out_shape,
        scratch_shapes=[pltpu.VMEM((FFN_TS, D), BF16), pltpu.VMEM((FFN_TS, D), F32)], args=args,
        compiler_params=_params("arbitrary", "arbitrary"),
    )


def _ffn_bwd_weights(dout, h, gg, uu, wd, name, ex=None):
    width = wd.shape[0]
    nj = width // FFN_TF

    def body(do_ref, h_ref, gg_ref, uu_ref, wd_ref, dg_ref, du_ref, dwg_ref, dwu_ref, dwd_ref, dob):
        @pl.when(pl.program_id(0) == 0)
        def _():
            dob[...] = do_ref[...].astype(BF16)

        hb, dov = h_ref[...], dob[...]
        gv, uv = gg_ref[...], uu_ref[...]
        da = _nt(dov, wd_ref[...])
        sg = jax.nn.sigmoid(gv)
        sl = gv * sg
        ab = (sl * uv).astype(BF16)
        dub = (da * sl).astype(BF16)
        dgb = (da * uv * (sg * (1.0 + gv * (1.0 - sg)))).astype(BF16)
        dg_ref[...] = dgb
        du_ref[...] = dub
        dwg_ref[...] = _tn(dgb, hb).astype(BF16)
        dwu_ref[...] = _tn(dub, hb).astype(BF16)
        dwd_ref[...] = _tn(ab, dov).astype(BF16)

    once = pl.Buffered(1)
    whole = lambda: pl.BlockSpec((S, D), lambda j, _: (0, 0), pipeline_mode=once)
    wt = pl.BlockSpec((FFN_TF, D), lambda j, _: (j, 0))
    gt = pl.BlockSpec((S, FFN_TF), lambda j, _: (0, j))
    return _call_hosting(
        body, ex, _grid_phases((nj, 1)), name=name, grid=(nj, 1),
        in_specs=[whole(), whole(), gt, gt, wt], out_specs=[gt, gt, wt, wt, wt],
        out_shape=[jax.ShapeDtypeStruct((S, width), BF16)] * 2 + [jax.ShapeDtypeStruct((width, D), BF16)] * 3,
        scratch_shapes=[pltpu.VMEM((S, D), BF16)], args=(dout, h, gg, uu, wd),
        compiler_params=_params("arbitrary", "arbitrary"),
    )


BWD_TS = 512


def _ffn_bwd_input(dres, x, g, grads, weights, name, ex=None):
    nt = S // BWD_TS
    n = len(grads)

    def body(*refs):
        dres_ref, x_ref, g_ref = refs[:3]
        grad_refs, w_refs = refs[3:3 + n], refs[3 + n:3 + 2 * n]
        dx_ref, dgam_ref = refs[3 + 2 * n:]
        dh = _nn(grad_refs[0][...], w_refs[0][...])
        for k in range(1, n):
            dh = dh + _nn(grad_refs[k][...], w_refs[k][...])
        xv = x_ref[...]
        r = _rstd(xv)
        dx_ref[...] = dres_ref[...] + _rms_bwd(dh, xv, r, g_ref[...])

        @pl.when(pl.program_id(0) == 0)
        def _():
            dgam_ref[...] = jnp.zeros(dgam_ref.shape, F32)

        dgam_ref[...] += _colsum8(dh * xv * r)

    tile = pl.BlockSpec((BWD_TS, D), lambda i: (i, 0))
    ftiles = [pl.BlockSpec((BWD_TS, a.shape[1]), lambda i: (i, 0)) for a in grads]
    wspecs = [pl.BlockSpec(w.shape, lambda i: (0, 0), pipeline_mode=pl.Buffered(1)) for w in weights]
    return _call_hosting(
        body, ex, _grid_phases((nt,)), name=name, grid=(nt,),
        in_specs=[tile, tile, pl.BlockSpec((1, D), lambda i: (0, 0))] + ftiles + wspecs,
        out_specs=[tile, pl.BlockSpec((8, D), lambda i: (0, 0))],
        out_shape=[jax.ShapeDtypeStruct((S, D), F32), jax.ShapeDtypeStruct((8, D), F32)],
        scratch_shapes=[], args=(dres, x, g, *grads, *weights), compiler_params=_params("arbitrary"),
    )


def _mm(a, b, mode, out_dtype, name, add=None, skip_rows=0):
    if mode == "nn":
        (m, kd), n = (a.shape[0] - skip_rows, a.shape[1]), b.shape[1]
    elif mode == "nt":
        (m, kd), n = (a.shape[0] - skip_rows, a.shape[1]), b.shape[0]
    else:
        (kd, m), n = a.shape, b.shape[1]
    tm, tn, tk = min(m, 1024), min(n, 1024), min(kd, 1024)
    if skip_rows and mode == "tn":
        tk = min(tk, skip_rows)
    elif skip_rows:
        tm = min(tm, skip_rows)
    skip_a = skip_rows // tm if mode != "tn" else 0
    skip_b = skip_rows // tk if mode == "tn" else 0
    assert skip_rows == skip_a * tm + skip_b * tk
    nk = kd // tk
    dot = {"nn": _nn, "nt": _nt, "tn": _tn}[mode]

    def body(*refs):
        if add is None:
            a_ref, b_ref, o_ref, acc = refs
        else:
            a_ref, b_ref, add_ref, o_ref, acc = refs
        k = pl.program_id(2)

        @pl.when(k == 0)
        def _():
            acc[...] = jnp.zeros(acc.shape, F32)

        acc[...] += dot(a_ref[...].astype(BF16), b_ref[...].astype(BF16))

        @pl.when(k == nk - 1)
        def _():
            res = acc[...]
            if add is not None:
                res = res + add_ref[...]
            o_ref[...] = res.astype(out_dtype)

    if mode == "tn":
        a_spec = pl.BlockSpec((tk, tm), lambda i, j, k: (k, i))
        b_spec = pl.BlockSpec((tk, tn), lambda i, j, k: (k + skip_b, j))
    else:
        a_spec = pl.BlockSpec((tm, tk), lambda i, j, k: (i + skip_a, k))
        b_spec = (pl.BlockSpec((tk, tn), lambda i, j, k: (k, j)) if mode == "nn"
                  else pl.BlockSpec((tn, tk), lambda i, j, k: (j, k)))
    o_spec = pl.BlockSpec((tm, tn), lambda i, j, k: (i, j))
    in_specs = [a_spec, b_spec] + ([o_spec] if add is not None else [])
    args = (a, b) + ((add,) if add is not None else ())
    return _call(
        body, name=name, grid=(m // tm, n // tn, nk), in_specs=in_specs, out_specs=o_spec,
        out_shape=jax.ShapeDtypeStruct((m, n), out_dtype), scratch_shapes=[pltpu.VMEM((tm, tn), F32)],
        compiler_params=_params("parallel", "parallel", "arbitrary"),
    )(*args)


PROJ_TS = 256


def _kvq_proj(x, g_kv, g_mix, wk, wv, wq, gk, gq, ex=None):
    lead = PADK // PROJ_TS

    def body(x_ref, gkv_ref, gmix_ref, wk_ref, wv_ref, wq_ref, gk_ref, gq_ref,
             hkv_ref, h1_ref, kpre_ref, qpre_ref, k_ref, v_ref, q_ref):
        i = pl.program_id(0)

        @pl.when(i < lead)
        def _():
            k_ref[...] = jnp.zeros(k_ref.shape, BF16)
            v_ref[...] = jnp.zeros(v_ref.shape, BF16)

        @pl.when(i >= lead)
        def _():
            xv = x_ref[...]
            xr = xv * _rstd(xv)
            hkv = (xr * gkv_ref[...]).astype(BF16)
            h1 = (xr * gmix_ref[...]).astype(BF16)
            hkv_ref[...] = hkv
            h1_ref[...] = h1
            kpre = _nn(hkv, wk_ref[...])
            qpre = _nn(h1, wq_ref[...])
            kpre_ref[...] = kpre
            qpre_ref[...] = qpre
            v_ref[...] = _nn(hkv, wv_ref[...]).astype(BF16)
            rk = lax.rsqrt(_seg_sum(kpre * kpre) * (1.0 / HEAD) + EPS)
            k_ref[...] = (kpre * rk * gk_ref[...]).astype(BF16)
            rq = lax.rsqrt(_seg_sum(qpre * qpre) * (1.0 / HEAD) + EPS)
            q_ref[...] = (qpre * rq * gq_ref[...]).astype(BF16)

    tile = pl.BlockSpec((PROJ_TS, D), lambda i: (jnp.maximum(i - lead, 0), 0))
    padded = pl.BlockSpec((PROJ_TS, D), lambda i: (i, 0))
    row = pl.BlockSpec((1, D), lambda i: (0, 0))
    wspec = pl.BlockSpec((D, D), lambda i: (0, 0))
    bf = jax.ShapeDtypeStruct((S, D), BF16)
    ff = jax.ShapeDtypeStruct((S, D), F32)
    bp = jax.ShapeDtypeStruct((PADK + S, D), BF16)
    return _call_hosting(
        body, ex, _grid_phases((lead + S // PROJ_TS,), 0.85), name="kvq_proj", grid=(lead + S // PROJ_TS,),
        in_specs=[tile, row, row, wspec, wspec, wspec, row, row],
        out_specs=[tile, tile, tile, tile, padded, padded, tile], out_shape=[bf, bf, ff, ff, bp, bp, bf],
        scratch_shapes=[], args=(x, g_kv, g_mix, wk, wv, wq, gk, gq), compiler_params=_params("arbitrary"),
    )


def _qkv_input_bwd(dq, qpre, gq, dkp, kpre, gk, dvp, wq, wk, wv, dres, x, g_mix, g_kv, ex=None):
    nt = S // PROJ_TS
    lead = PADK // PROJ_TS

    def head_bwd(dov, pv, hgv):
        r = lax.rsqrt(_seg_sum(pv * pv) * (1.0 / HEAD) + EPS)
        gd = dov * hgv
        dpre = r * gd - pv * (r * r * r) * (_seg_sum(gd * pv) * (1.0 / HEAD))
        return dpre.astype(BF16), _colsum8(dov * pv * r)

    def fold_heads(full):
        fold = full[:, 0:128]
        for blk in range(1, D // 128):
            fold = fold + full[:, blk * 128:(blk + 1) * 128]
        return fold + pltpu.roll(fold, HEAD, axis=1)

    def body(dq_ref, qpre_ref, gq_ref, dk_ref, kpre_ref, gk_ref, dv_ref, wq_ref, wk_ref, wv_ref, dres_ref, x_ref,
             gmix_ref, gkv_ref, dx_ref, dqpre_ref, dkpre_ref, dgmix_ref, dgkv_ref, dgq_ref, dgk_ref, accq, acck):
        i = pl.program_id(0)

        @pl.when(i == 0)
        def _():
            accq[...] = jnp.zeros(accq.shape, F32)
            acck[...] = jnp.zeros(acck.shape, F32)
            dgmix_ref[...] = jnp.zeros(dgmix_ref.shape, F32)
            dgkv_ref[...] = jnp.zeros(dgkv_ref.shape, F32)

        dqb, cq = head_bwd(dq_ref[...], qpre_ref[...], gq_ref[...])
        dkb, ck = head_bwd(dk_ref[...], kpre_ref[...], gk_ref[...])
        dqpre_ref[...] = dqb
        dkpre_ref[...] = dkb
        accq[...] += cq
        acck[...] += ck
        dh1 = _nt(dqb, wq_ref[...])
        dhkv = _nt(dkb, wk_ref[...]) + _nt(dv_ref[...].astype(BF16), wv_ref[...])
        xv = x_ref[...]
        r = _rstd(xv)
        dx_ref[...] = dres_ref[...] + _rms_bwd(dh1, xv, r, gmix_ref[...]) + _rms_bwd(dhkv, xv, r, gkv_ref[...])
        dgmix_ref[...] += _colsum8(dh1 * xv * r)
        dgkv_ref[...] += _colsum8(dhkv * xv * r)

        @pl.when(i == nt - 1)
        def _():
            dgq_ref[...] = fold_heads(accq[...])
            dgk_ref[...] = fold_heads(acck[...])

    tile = pl.BlockSpec((PROJ_TS, D), lambda i: (i, 0))
    behind = pl.BlockSpec((PROJ_TS, D), lambda i: (i + lead, 0))
    row = pl.BlockSpec((1, D), lambda i: (0, 0))
    wspec = lambda: pl.BlockSpec((D, D), lambda i: (0, 0), pipeline_mode=pl.Buffered(1))
    acc = pl.BlockSpec((8, D), lambda i: (0, 0))
    small = pl.BlockSpec((8, 128), lambda i: (0, 0))
    bf = jax.ShapeDtypeStruct((S, D), BF16)
    return _call_hosting(
        body, ex, _grid_phases((nt,)), name="qkv_input_bwd", grid=(nt,),
        in_specs=[tile, tile, row, behind, tile, row, behind, wspec(), wspec(), wspec(), tile, tile, row, row],
        out_specs=[tile, tile, tile, acc, acc, small, small],
        out_shape=[jax.ShapeDtypeStruct((S, D), F32), bf, bf, jax.ShapeDtypeStruct((8, D), F32),
                   jax.ShapeDtypeStruct((8, D), F32), jax.ShapeDtypeStruct((8, 128), F32),
                   jax.ShapeDtypeStruct((8, 128), F32)],
        scratch_shapes=[pltpu.VMEM((8, D), F32), pltpu.VMEM((8, D), F32)],
        args=(dq, qpre, gq, dkp, kpre, gk, dvp, wq, wk, wv, dres, x, g_mix, g_kv),
        compiler_params=_params("arbitrary"),
    )


def _toeplitz_from_table(table):
    far = jnp.broadcast_to(table[:, N_REL - 1:], (N_HEADS, PADK - MAX_REL + 1))
    near = table[:, N_REL - 2::-1]
    past = jnp.broadcast_to(table[:, 0:1], (N_HEADS, MAX_REL))
    wrap = jnp.broadcast_to(table[:, N_REL - 1:], (N_HEADS, TOEP - (PADK + 2 * MAX_REL + 1)))
    return jnp.concatenate([far, near, past, wrap], axis=1).reshape(N_HEADS, 1, TOEP)


def _table_grad_from_toeplitz(dtp, seg):
    lo = PADK - MAX_REL + 1
    near = dtp[:, lo + N_REL - 3:lo - 1:-1]
    return jnp.concatenate([seg[:, 1:2], near, seg[:, 0:1]], axis=1)


def _bias_band(tp):
    def body(tp_ref, out_ref):
        bv = pltpu.roll(jnp.broadcast_to(tp_ref[0], (QB, TOEP)), 0, axis=1, stride=1, stride_axis=0)
        out_ref[0] = jnp.where(_band_mask(), bv[:, 0:KB], NEG_INF)

    return _call(
        body, name="bias_band", grid=(N_HEADS,),
        in_specs=[pl.BlockSpec((1, 1, TOEP), lambda h: (h, 0, 0))],
        out_specs=pl.BlockSpec((1, QB, KB), lambda h: (h, 0, 0)),
        out_shape=jax.ShapeDtypeStruct((N_HEADS, QB, KB), F32),
        compiler_params=_params("parallel"),
    )(tp)


def _bias_grad(dband, after):
    lo, hi = PADK - MAX_REL + 1, PADK + MAX_REL

    def body(db_ref, _, dtp_ref, seg_ref):
        bv = jnp.concatenate([db_ref[0], jnp.zeros((QB, TOEP - KB), F32)], axis=1)
        row = lax.broadcasted_iota(jnp.int32, (QB, TOEP), 0)
        k = 1
        while k < QB:
            bv = jnp.where((row & k) != 0, pltpu.roll(bv, TOEP - k, axis=1), bv)
            k *= 2
        col = jnp.sum(bv, axis=0, keepdims=True)
        dtp_ref[0] = col
        u = lax.broadcasted_iota(jnp.int32, (1, TOEP), 1)
        far = jnp.sum(jnp.where((u < lo) | (u > hi + MAX_REL), col, 0.0))
        past = jnp.sum(jnp.where((u >= hi) & (u <= hi + MAX_REL), col, 0.0))
        lane = lax.broadcasted_iota(jnp.int32, (1, 128), 1)
        seg_ref[0] = jnp.where(lane == 0, far, jnp.where(lane == 1, past, 0.0))

    return _call(
        body, name="bias_grad", grid=(N_HEADS,),
        in_specs=[pl.BlockSpec((1, QB, KB), lambda h: (h, 0, 0)), ANY],
        out_specs=[pl.BlockSpec((1, 1, TOEP), lambda h: (h, 0, 0)), pl.BlockSpec((1, 1, 128), lambda h: (h, 0, 0))],
        out_shape=[jax.ShapeDtypeStruct((N_HEADS, 1, TOEP), F32), jax.ShapeDtypeStruct((N_HEADS, 1, 128), F32)],
        compiler_params=_params("parallel"),
    )(dband, after)


N_QB = S // QB
HEADS_PER_STEP = 4
ATT_LANES = HEADS_PER_STEP * HEAD
N_HG = D // ATT_LANES


def _band_mask():
    qc = lax.broadcasted_iota(jnp.int32, (QB, KB), 0) // CHUNK
    kc = lax.broadcasted_iota(jnp.int32, (QB, KB), 1) // CHUNK
    return (kc >= qc) & (kc <= qc + LEFT)


def _half_scale(hh, scale):
    lane = lax.broadcasted_iota(jnp.int32, (1, 128), 1)
    return jnp.where((lane < HEAD) == (hh == 0), scale, 0.0).astype(BF16)


def _probs(qh, kb, bias, first_key):
    sc = _nt(qh, kb) + bias
    if first_key is not None:
        sc = jnp.where(lax.broadcasted_iota(jnp.int32, (QB, KB), 1) >= first_key, sc, NEG_INF)
    e = jnp.exp(sc - jnp.max(sc, axis=-1, keepdims=True))
    return e * (1.0 / jnp.sum(e, axis=-1, keepdims=True))


def _by_padding(cb, compute):
    @pl.when(cb < PADK // QB)
    def _():
        compute(PADK - cb * QB)

    @pl.when(cb >= PADK // QB)
    def _():
        compute(None)


def _attn_fwd(q, kp, vp, bias, ex=None):
    def body(q_ref, k_ref, v_ref, b_ref, o_ref):
        cb = pl.program_id(1)
        band = pl.ds(pl.multiple_of(cb * QB, QB), KB)
        low = lax.broadcasted_iota(jnp.int32, (QB, 128), 1) < HEAD

        def compute(first_key):
            for pair in range(HEADS_PER_STEP // 2):
                lanes = pl.ds(pair * 128, 128)
                kb, vb, qv = k_ref[band, lanes], v_ref[band, lanes], q_ref[:, lanes]
                outs = []
                for hh in range(2):
                    pb = _probs(qv * _half_scale(hh, ATTN_SCALE), kb, b_ref[2 * pair + hh], first_key).astype(BF16)
                    outs.append(_nn(pb, vb))
                o_ref[:, lanes] = jnp.where(low, outs[0], outs[1]).astype(BF16)

        _by_padding(cb, compute)

    qspec = pl.BlockSpec((QB, ATT_LANES), lambda hg, cb: (cb, hg))
    kspec = pl.BlockSpec((PADK + S, ATT_LANES), lambda hg, cb: (0, hg))
    return _call_hosting(
        body, ex, _grid_phases((N_HG, N_QB), 0.85), name="attn_fwd", grid=(N_HG, N_QB),
        in_specs=[qspec, kspec, kspec, pl.BlockSpec((HEADS_PER_STEP, QB, KB), lambda hg, cb: (hg, 0, 0))],
        out_specs=[qspec], out_shape=[jax.ShapeDtypeStruct((S, D), BF16)], scratch_shapes=[],
        args=(q, kp, vp, bias), compiler_params=_params("arbitrary", "arbitrary"),
    )


def _attn_bwd(q, kp, vp, bias, do, ex=None):
    def body(q_ref, k_ref, v_ref, b_ref, do_ref, dq_ref, dk_ref, dv_ref, db_ref):
        cb = pl.program_id(1)

        @pl.when(cb == 0)
        def _():
            dk_ref[...] = jnp.zeros(dk_ref.shape, F32)
            dv_ref[...] = jnp.zeros(dv_ref.shape, F32)
            db_ref[...] = jnp.zeros(db_ref.shape, F32)

        band = pl.ds(pl.multiple_of(cb * QB, QB), KB)
        low = lax.broadcasted_iota(jnp.int32, (QB, 128), 1) < HEAD

        def compute(first_key):
            for pair in range(HEADS_PER_STEP // 2):
                lanes = pl.ds(pair * 128, 128)
                kb, vb = k_ref[band, lanes], v_ref[band, lanes]
                qv, dov = q_ref[:, lanes], do_ref[:, lanes]
                dq = jnp.zeros((QB, 128), F32)
                dkb = jnp.zeros((KB, 128), F32)
                dvb = jnp.zeros((KB, 128), F32)
                for hh in range(2):
                    sel = low if hh == 0 else jnp.logical_not(low)
                    doh = dov * _half_scale(hh, 1.0)
                    p = _probs(qv * _half_scale(hh, ATTN_SCALE), kb, b_ref[2 * pair + hh], first_key)
                    dp = _nt(doh, vb)
                    dvb = dvb + _tn(p.astype(BF16), doh)
                    ds = p * (dp - jnp.sum(dp * p, axis=-1, keepdims=True))
                    db_ref[2 * pair + hh] += ds
                    dsb = (ds * ATTN_SCALE).astype(BF16)
                    dq = dq + jnp.where(sel, _nn(dsb, kb), 0.0)
                    dkb = dkb + _tn(dsb, qv * _half_scale(hh, 1.0))
                dq_ref[:, lanes] = dq
                dk_ref[band, lanes] += dkb
                dv_ref[band, lanes] += dvb

        _by_padding(cb, compute)

    qspec = pl.BlockSpec((QB, ATT_LANES), lambda hg, cb: (cb, hg))
    kspec = pl.BlockSpec((PADK + S, ATT_LANES), lambda hg, cb: (0, hg))
    bspec = pl.BlockSpec((HEADS_PER_STEP, QB, KB), lambda hg, cb: (hg, 0, 0))
    kf = jax.ShapeDtypeStruct((PADK + S, D), F32)
    return _call_hosting(
        body, ex, _grid_phases((N_HG, N_QB)), name="attn_bwd", grid=(N_HG, N_QB),
        in_specs=[qspec, kspec, kspec, bspec, qspec],
        out_specs=[qspec, kspec, kspec, bspec],
        out_shape=[jax.ShapeDtypeStruct((S, D), F32), kf, kf, jax.ShapeDtypeStruct((N_HEADS, QB, KB), F32)],
        scratch_shapes=[], args=(q, kp, vp, bias, do), compiler_params=_params("arbitrary", "arbitrary"),
    )


def _adam_math(w, g, m, v):
    m = ADAM_B1 * m + (1.0 - ADAM_B1) * g
    v = ADAM_B2 * v + (1.0 - ADAM_B2) * (g * g)
    m_hat = m / (1.0 - ADAM_B1 ** ADAM_STEP)
    v_hat = v / (1.0 - ADAM_B2 ** ADAM_STEP)
    delta = -ADAM_LR * (m_hat / (jnp.sqrt(v_hat) + ADAM_EPS) + ADAM_WD * w)
    return delta, m, v


def _sum_parts(parts, name, after=None):
    count, r, c = parts.shape
    tr = r // 2 if r % 16 == 0 and r > 64 else r

    def body(p_ref, *rest):
        o_ref = rest[-1]
        acc = p_ref[0].astype(F32)
        for d in range(1, count):
            acc = acc + p_ref[d].astype(F32)
        o_ref[...] = acc

    return _call(
        body, name=name, grid=(r // tr,),
        in_specs=[pl.BlockSpec((count, tr, c), lambda i: (0, i, 0))] + ([ANY] if after is not None else []),
        out_specs=pl.BlockSpec((tr, c), lambda i: (i, 0)),
        out_shape=jax.ShapeDtypeStruct((r, c), F32), compiler_params=_params("parallel"),
    )(parts, *([after] if after is not None else []))


def _row_tile(r):
    for cand in (256, 176, 128, 64, 32, 16, 8):
        if r % cand == 0:
            return cand
    return r


def _adam_layer(w, g, m, v, layer, name, prev=None):
    nl, r, c = w.shape
    tr = _row_tile(r)

    def body(w_ref, g_ref, m_ref, v_ref, *rest):
        go_ref, d_ref, nm_ref, nv_ref = rest[-4:]
        gv = g_ref[...]
        delta, nm, nv = _adam_math(w_ref[0], gv, m_ref[0], v_ref[0])
        go_ref[0] = gv
        d_ref[0] = delta
        nm_ref[0] = nm
        nv_ref[0] = nv

    lspec = pl.BlockSpec((1, tr, c), lambda i: (layer, i, 0))
    sd = jax.ShapeDtypeStruct((nl, r, c), F32)
    extra = list(prev) if prev is not None else []
    return _call(
        body, name=name, grid=(r // tr,),
        in_specs=[lspec, pl.BlockSpec((tr, c), lambda i: (i, 0)), lspec, lspec] + [ANY] * len(extra),
        out_specs=[lspec] * 4, out_shape=[sd] * 4,
        input_output_aliases={4 + t: t for t in range(len(extra))},
        compiler_params=_params("parallel"),
    )(w, g, m, v, *extra)


def _adam(w, g, m, v, name):
    r, c = w.shape
    tr = _row_tile(r)

    def body(w_ref, g_ref, m_ref, v_ref, d_ref, nm_ref, nv_ref):
        delta, nm, nv = _adam_math(w_ref[...], g_ref[...], m_ref[...], v_ref[...])
        d_ref[...] = delta
        nm_ref[...] = nm
        nv_ref[...] = nv

    spec = pl.BlockSpec((tr, c), lambda i: (i, 0))
    sd = jax.ShapeDtypeStruct((r, c), F32)
    return _call(
        body, name=name, grid=(r // tr,), in_specs=[spec] * 4, out_specs=[spec] * 3, out_shape=[sd] * 3,
        compiler_params=_params("parallel"),
    )(w, g, m, v)


def _pad16(rows):
    return jnp.pad(rows, ((0, 16 - rows.shape[0]), (0, 0)))


def kernel(x, norm_mix_g, norm_ffn_g, pool_w, pool_b, pool_scale, kv_norm_g, w_k, w_v, k_norm_g, w_q, q_norm_g, rel_bias, w_o, w_gate, w_up, w_down, loss_target, m_norm_mix_g, m_norm_ffn_g, m_pool_w, m_pool_b, m_pool_scale, m_kv_norm_g, m_w_k, m_w_v, m_k_norm_g, m_w_q, m_q_norm_g, m_rel_bias, m_w_o, m_w_gate, m_w_up, m_w_down, v_norm_mix_g, v_norm_ffn_g, v_pool_w, v_pool_b, v_pool_scale, v_kv_norm_g, v_w_k, v_w_v, v_k_norm_g, v_w_q, v_q_norm_g, v_rel_bias, v_w_o, v_w_gate, v_w_up, v_w_down):
    assert x.shape == (1, S, D) and w_gate.shape == (2, D, F_SHARD) and w_k.shape == (D_SHARD, D)
    xin, target = x[0], loss_target[0]

    ffn_shards = [[w_gate[layer].T.astype(BF16), w_up[layer].T.astype(BF16), w_down[layer].astype(BF16)]
                  for layer in range(2)]
    att_shards = [w_k.astype(BF16), w_v.astype(BF16), w_q[0].astype(BF16), w_o[0].astype(BF16)]
    pool_shard = pool_w[0].astype(BF16).reshape(N_GROUPS * POOL_SHARD, GROUP)
    small = jnp.concatenate([pool_b[0].reshape(1, N_GROUPS * POOL_SHARD), pool_scale], axis=1)

    full0 = _run_exchange(_relay_gather_exchange(ffn_shards[0] + [pool_shard, _pad16(small)]), "gather_layer0")
    ffn_w0 = [a.reshape(F, D) for a in full0[:3]]
    pw_f = full0[3].reshape(N_DEV, N_GROUPS, POOL_SHARD, GROUP).transpose(1, 0, 2, 3).reshape(N_GROUPS, GROUP, GROUP)
    small_f = full0[4][:, 0, :]
    pb_f = small_f[:, :N_GROUPS * POOL_SHARD].reshape(N_DEV, N_GROUPS, POOL_SHARD).transpose(1, 0, 2).reshape(1, D)
    ps_f = small_f[:, N_GROUPS * POOL_SHARD:].reshape(1, D)

    g_mix0, g_mix1 = norm_mix_g[0:1], norm_mix_g[1:2]
    g_ffn0, g_ffn1 = norm_ffn_g[0:1], norm_ffn_g[1:2]
    g_kv = kv_norm_g.reshape(1, D)
    gk_t = jnp.tile(k_norm_g.reshape(1, HEAD), (1, N_HEADS))
    gq_t = jnp.tile(q_norm_g.reshape(1, HEAD), (1, N_HEADS))

    x1, diff = _pool_fwd(xin, g_mix0, pw_f, pb_f, ps_f)
    (x2, hf0, gg0, uu0), full_att = _ffn_fwd(x1, g_ffn0, *ffn_w0, name="ffn_fwd", ex=_relay_gather_exchange(att_shards))
    wk_f, wv_f, wq_f, wo_f = [a.reshape(D, D) for a in full_att]
    (hkv, h1, kpre, qpre, kp, vp, qq), full1_gate = _kvq_proj(x2, g_kv, g_mix1, wk_f, wv_f, wq_f, gk_t, gq_t,
                                                                ex=_relay_gather_exchange(ffn_shards[1][:1]))
    bias = _bias_band(_toeplitz_from_table(rel_bias[0]))
    (att,), full1_rest = _attn_fwd(qq, kp, vp, bias, ex=_relay_gather_exchange(ffn_shards[1][1:]))
    ffn_w1 = [a.reshape(F, D) for a in full1_gate + full1_rest]
    x3 = _mm(att, wo_f, "nn", F32, "attn_out", add=x2)
    (dx4, loss_rows, hf1, gg1, uu1), _ = _ffn_fwd(x3, g_ffn1, *ffn_w1, name="ffn_fwd_loss", target=target)

    def blocks(dw):
        return dw.reshape(N_DEV, dw.shape[0] // N_DEV, dw.shape[1])

    (dgg1, duu1, dwg1, dwu1, dwd1), _ = _ffn_bwd_weights(dx4, hf1, gg1, uu1, ffn_w1[2], name="ffn_bwd1")
    parts1 = [blocks(dw) for dw in (dwg1, dwu1, dwd1)]
    (dx3, dg_ffn1), stage1 = _ffn_bwd_input(dx4, x3, g_ffn1, [dgg1, duu1], ffn_w1[:2], "ffn_dx1",
                                            ex=_pair_exchange(parts1))
    both1 = _pair_add(parts1, stage1, "pair_add_ffn1", copies=2)
    flight1, chip1, land1, _ = _chip_exchange_start(both1[:3], both1[3:], dx3, "scatter_ffn1_start")
    datt = _mm(dx3, wo_f, "nt", BF16, "d_attn")
    dwo = _mm(att, dx3, "tn", BF16, "d_wo")
    (dq, dkp, dvp, dband), _ = _attn_bwd(qq, kp, vp, bias, datt)
    (dx2, dqpre, dkpre, dg_mix1, dg_kv, dgq, dgk), _ = _qkv_input_bwd(
        dq, qpre, gq_t, dkp, kpre, gk_t, dvp, wq_f, wk_f, wv_f, dx3, x2, g_mix1, g_kv)
    dwq = _mm(h1, dqpre, "tn", BF16, "d_wq")
    dwk = _mm(hkv, dkpre, "tn", BF16, "d_wk")
    dwv = _mm(hkv, dvp, "tn", BF16, "d_wv", skip_rows=PADK)
    parts_att = [blocks(dw) for dw in (dwk, dwv, dwq, dwo)]
    (dgg0, duu0, *dw0), stage_att = _ffn_bwd_weights(
        dx2, hf0, gg0, uu0, ffn_w0[2], name="ffn_bwd0", ex=_pair_exchange(parts_att))
    both_att = _pair_add(parts_att, stage_att, "pair_add_att", copies=2)
    flight_att, chip_att, land_att, _ = _chip_exchange_start(both_att[:4], both_att[4:], dx2, "scatter_att_start")
    parts0 = [blocks(dw) for dw in dw0]
    (dx1, dg_ffn0), stage0 = _ffn_bwd_input(dx2, x1, g_ffn0, [dgg0, duu0], ffn_w0[:2], "ffn_dx0",
                                            ex=_pair_exchange(parts0))
    both = _pair_add(parts0, stage0, "pair_add_ffn0", copies=2)
    flight, chip0, land0, token = _chip_exchange_start(both[:3], both[3:], dx1, "scatter_ffn0_start")
    (grad_x, dpw, db_rows, ds_rows, dg_mix0), _ = _pool_bwd(dx1, xin, diff, g_mix0 + token[0:1, 0:1], pw_f, pb_f, ps_f)
    dtp, seg = _bias_grad(dband, dpw)

    weights = dict(norm_mix_g=norm_mix_g, norm_ffn_g=norm_ffn_g, pool_w=pool_w, pool_b=pool_b,
                   pool_scale=pool_scale, kv_norm_g=kv_norm_g, w_k=w_k, w_v=w_v, k_norm_g=k_norm_g, w_q=w_q,
                   q_norm_g=q_norm_g, rel_bias=rel_bias, w_o=w_o, w_gate=w_gate, w_up=w_up, w_down=w_down)
    mom1 = dict(norm_mix_g=m_norm_mix_g, norm_ffn_g=m_norm_ffn_g, pool_w=m_pool_w, pool_b=m_pool_b,
                pool_scale=m_pool_scale, kv_norm_g=m_kv_norm_g, w_k=m_w_k, w_v=m_w_v, k_norm_g=m_k_norm_g,
                w_q=m_w_q, q_norm_g=m_q_norm_g, rel_bias=m_rel_bias, w_o=m_w_o, w_gate=m_w_gate, w_up=m_w_up,
                w_down=m_w_down)
    mom2 = dict(norm_mix_g=v_norm_mix_g, norm_ffn_g=v_norm_ffn_g, pool_w=v_pool_w, pool_b=v_pool_b,
                pool_scale=v_pool_scale, kv_norm_g=v_kv_norm_g, w_k=v_w_k, w_v=v_w_v, k_norm_g=v_k_norm_g,
                w_q=v_w_q, q_norm_g=v_q_norm_g, rel_bias=v_rel_bias, w_o=v_w_o, w_gate=v_w_gate, w_up=v_w_up,
                w_down=v_w_down)
    names = list(weights)
    grads, deltas, new_m, new_v = {}, {}, {}, {}

    def summed(arrs, tag, after):
        return [_sum_parts(a, "sum_parts_%s%d" % (tag, k), after) for k, a in enumerate(arrs)]

    def adam_flat(nm):
        shape = weights[nm].shape
        flat = lambda a: a.reshape(-1, shape[-1])
        dl, m1, m2 = _adam(flat(weights[nm]), flat(grads[nm]), flat(mom1[nm]), flat(mom2[nm]), "adam_" + nm)
        deltas[nm], new_m[nm], new_v[nm] = dl.reshape(shape), m1.reshape(shape), m2.reshape(shape)

    ffn_names = ("w_gate", "w_up", "w_down")
    def ffn_view(nm, a):
        return a if nm == "w_down" else a.transpose(0, 2, 1)

    recv1 = _chip_exchange_wait(flight1, chip1, land1, dtp, "scatter_ffn1_wait")
    sums1 = summed(recv1, "ffn1_", token)
    layer1 = {nm: _adam_layer(ffn_view(nm, weights[nm]), g, ffn_view(nm, mom1[nm]), ffn_view(nm, mom2[nm]), 1,
                              "adam1_" + nm)
              for nm, g in zip(ffn_names, sums1)}

    dpw_blocks = dpw.reshape(N_GROUPS, N_DEV, POOL_SHARD, GROUP).transpose(1, 0, 2, 3)
    dpw_blocks = dpw_blocks.reshape(N_DEV, N_GROUPS * POOL_SHARD, GROUP).astype(BF16)
    recv_pool = _run_exchange(_scatter_exchange([dpw_blocks]), "scatter_pool", after=layer1["w_down"][1])
    misc = jnp.concatenate([dgk[0:1, 0:HEAD], dgq[0:1, 0:HEAD], loss_rows[0:1, 0:1],
                            seg[:, 0, 0].reshape(1, N_HEADS), seg[:, 0, 1].reshape(1, N_HEADS)], axis=1)
    misc = jnp.pad(misc, ((0, 0), (0, D - misc.shape[1])))
    vec_rows = jnp.concatenate([dg_mix0[0:1], dg_mix1[0:1], dg_ffn0[0:1], dg_ffn1[0:1], dg_kv[0:1],
                                db_rows[0:1], ds_rows[0:1], misc], axis=0)
    pack = jnp.concatenate([vec_rows, dtp.reshape(N_HEADS, TOEP)], axis=0)
    tot = _all_reduce_small(pack, "reduce_small", after=recv_pool[0])

    loss = tot[7, 2 * HEAD]
    seg_tot = jnp.stack([tot[7, 2 * HEAD + 1:2 * HEAD + 1 + N_HEADS],
                         tot[7, 2 * HEAD + 1 + N_HEADS:2 * HEAD + 1 + 2 * N_HEADS]], axis=1)
    me = 4 * lax.axis_index("x") + 2 * lax.axis_index("y") + lax.axis_index("c")
    g_pool_b = lax.dynamic_slice_in_dim(tot[5].reshape(N_GROUPS, GROUP), me * POOL_SHARD, POOL_SHARD, axis=1)
    grads.update(
        norm_mix_g=tot[0:2], norm_ffn_g=tot[2:4], kv_norm_g=tot[4], k_norm_g=tot[7, 0:HEAD],
        q_norm_g=tot[7, HEAD:2 * HEAD].reshape(1, HEAD),
        rel_bias=_table_grad_from_toeplitz(tot[8:8 + N_HEADS], seg_tot).reshape(1, N_HEADS, N_REL),
        pool_b=g_pool_b.reshape(1, N_GROUPS, POOL_SHARD),
        pool_scale=lax.dynamic_slice_in_dim(tot[6:7], me * D_SHARD, D_SHARD, axis=1),
        pool_w=summed(recv_pool, "pool_", None)[0].reshape(1, N_GROUPS, POOL_SHARD, GROUP))
    adam_flat("pool_w")
    small_names = [nm for nm in names if nm not in ffn_names + ("w_k", "w_v", "w_q", "w_o", "pool_w")]

    def pack_small(tree):
        cols = []
        for nm in small_names:
            flat = tree[nm].reshape(-1)
            cols.append(jnp.pad(flat, (0, -flat.shape[0] % 1024)))
        return jnp.concatenate(cols).reshape(-1, 128)

    dl, m1, m2 = _adam(pack_small(weights), pack_small(grads), pack_small(mom1), pack_small(mom2), "adam_small")

    def unpack_small(packed, out):
        flat, off = packed.reshape(-1), 0
        for nm in small_names:
            size = weights[nm].size
            out[nm] = flat[off:off + size].reshape(weights[nm].shape)
            off += size + (-size % 1024)

    unpack_small(dl, deltas)
    unpack_small(m1, new_m)
    unpack_small(m2, new_v)

    recv_att = _chip_exchange_wait(flight_att, chip_att, land_att, dl, "scatter_att_wait")
    g_wk, g_wv, g_wq, g_wo = summed(recv_att, "att_", None)
    grads.update(w_k=g_wk, w_v=g_wv, w_q=g_wq[None], w_o=g_wo[None])
    for nm in ("w_k", "w_v", "w_q", "w_o"):
        adam_flat(nm)
    recv0 = _chip_exchange_wait(flight, chip0, land0, deltas["w_o"], "scatter_ffn0_wait")
    sums0 = summed(recv0, "ffn0_", None)
    for nm, g in zip(ffn_names, sums0):
        done = _adam_layer(ffn_view(nm, weights[nm]), g, ffn_view(nm, mom1[nm]), ffn_view(nm, mom2[nm]), 0,
                           "adam0_" + nm, prev=layer1[nm])
        grads[nm], deltas[nm], new_m[nm], new_v[nm] = [ffn_view(nm, a) for a in done]

    return (loss, grad_x[None], *[grads[nm] for nm in names], *[deltas[nm] for nm in names],
            *[new_m[nm] for nm in names], *[new_v[nm] for nm in names])
```

```python
import functools

import jax
import jax.numpy as jnp
from jax import lax
from jax.experimental import pallas as pl
from jax.experimental.pallas import tpu as pltpu

F32 = jnp.float32
BF16 = jnp.bfloat16
MESH_ID = pl.DeviceIdType.MESH

N_DEV = 8
S = 2048
D = 1024
F = 2816
F_SHARD = F // N_DEV
D_SHARD = D // N_DEV
N_GROUPS = 4
GROUP = D // N_GROUPS
POOL_SHARD = GROUP // N_DEV
MAX_WIN = 16
HEAD = 64
N_HEADS = D // HEAD
CHUNK = 64
LEFT = 8
QB = 4 * CHUNK
KB = QB + LEFT * CHUNK
PADK = LEFT * CHUNK
TOEP = 1024
N_REL = 257
MAX_REL = 128
EPS = 1e-6
NEG_INF = -1e30
ATTN_SCALE = HEAD ** -0.5

ADAM_LR = 0.001
ADAM_B1 = 0.9
ADAM_B2 = 0.999
ADAM_EPS = 1e-08
ADAM_WD = 0.01
ADAM_STEP = 10

VMEM_LIMIT = 52 * 1024 * 1024

ANY = pl.BlockSpec(memory_space=pl.ANY)
VMEM = pl.BlockSpec(memory_space=pltpu.VMEM)


def _call(body, **kw):
    return pl.pallas_call(body, **kw)


def _params(*sem):
    return pltpu.CompilerParams(dimension_semantics=sem, vmem_limit_bytes=VMEM_LIMIT)


def _dot(a, b, dims):
    return lax.dot_general(a, b, (dims, ((), ())), preferred_element_type=F32)


def _nn(a, b):
    return _dot(a, b, ((1,), (0,)))


def _nt(a, b):
    return _dot(a, b, ((1,), (1,)))


def _tn(a, b):
    return _dot(a, b, ((0,), (0,)))


def _rstd(x):
    return lax.rsqrt(jnp.mean(x * x, axis=-1, keepdims=True) + EPS)


def _rms_bwd(dh, x, r, g):
    gd = dh * g
    return r * gd - x * (r * r * r) * jnp.mean(gd * x, axis=-1, keepdims=True)


def _colsum8(v):
    return jnp.broadcast_to(jnp.sum(v, axis=0, keepdims=True), (8, v.shape[1]))


def _seg_sum(v):
    r = lax.broadcasted_iota(jnp.int32, (128, 128), 0) // HEAD
    c = lax.broadcasted_iota(jnp.int32, (128, 128), 1) // HEAD
    ones = jnp.where(r == c, 1.0, 0.0).astype(BF16)
    out = []
    for blk in range(v.shape[1] // 128):
        part = v[:, blk * 128:(blk + 1) * 128]
        hi = part.astype(BF16)
        rest = part - hi.astype(F32)
        mid = rest.astype(BF16)
        lo = (rest - mid.astype(F32)).astype(BF16)
        out.append(_nn(hi, ones) + _nn(mid, ones) + _nn(lo, ones))
    return jnp.concatenate(out, axis=1)


def _place():
    return lax.axis_index("x"), lax.axis_index("y"), lax.axis_index("c")


class _Exchange:
    def __init__(self, ins, out_shape, sems, start, mid, finish, early=None):
        self.ins, self.out_shape, self.sems = list(ins), list(out_shape), list(sems)
        self.start, self.mid, self.finish = start, mid, finish
        self.early = early if early is not None else (lambda ins, outs, sems: None)


def _relay_gather_exchange(shards):
    n = len(shards)

    def tools(ins, outs, sems):
        send_sems, recv_sems, local_sems = sems
        x, y, c = _place()
        me, sib = (x, y, c), (x, y, 1 - c)
        xn, yn, dg = (1 - x, y, c), (x, 1 - y, c), (1 - x, 1 - y, c)
        other = lambda dev: (dev[0], dev[1], 1 - c)

        def piece(k, ref, half):
            rows = shards[k].shape[0] // 2
            return ref if half is None else ref.at[pl.ds(half * rows, rows)]

        def held(k, dev, half):
            return piece(k, outs[k].at[4 * dev[0] + 2 * dev[1] + dev[2]], half)

        def copy(k, s, dev, half, to, own=False):
            return pltpu.make_async_remote_copy(
                src_ref=piece(k, ins[k], half) if own else held(k, dev, half), dst_ref=held(k, dev, half),
                send_sem=send_sems.at[k, s], recv_sem=recv_sems.at[k, s], device_id=to, device_id_type=MESH_ID)

        def mine(k):
            return pltpu.make_async_copy(ins[k], held(k, me, None), local_sems.at[k])

        def sent(k):
            return [copy(k, 0, me, None, sib, own=True), copy(k, 1, me, 0, xn, own=True), copy(k, 2, me, 1, yn, own=True),
                    copy(k, 3, me, 0, yn, own=True), copy(k, 4, me, 1, xn, own=True)]

        relays = [(1, xn, 0, [(5, yn), (7, sib)]), (2, yn, 1, [(6, xn), (9, sib)]), (3, yn, 0, [(10, sib)]),
                  (4, xn, 1, [(8, sib)]), (6, dg, 1, [(11, sib)]), (5, dg, 0, [(12, sib)])]
        from_sibling = [(0, other(me), None), (7, other(xn), 0), (8, other(xn), 1), (9, other(yn), 1),
                        (10, other(yn), 0), (11, other(dg), 1), (12, other(dg), 0)]
        return mine, sent, copy, relays, from_sibling, me

    def start(ins, outs, sems):
        mine, sent, _, _, _, _ = tools(ins, outs, sems)
        for k in range(n):
            mine(k).start()
        for k in range(n):
            for cp in sent(k):
                cp.start()

    def pass_on(ins, outs, sems, which):
        _, _, copy, relays, _, me = tools(ins, outs, sems)
        for k in range(n):
            for s, dev, half, onward in relays[which]:
                copy(k, s, dev, half, me).wait_recv()
                for s2, to in onward:
                    copy(k, s2, dev, half, to).start()

    def early(ins, outs, sems):
        pass_on(ins, outs, sems, slice(0, 2))

    def mid(ins, outs, sems):
        pass_on(ins, outs, sems, slice(2, None))

    def finish(ins, outs, sems):
        mine, sent, copy, relays, from_sibling, me = tools(ins, outs, sems)
        for k in range(n):
            for s, dev, half in from_sibling:
                copy(k, s, dev, half, me).wait_recv()
        for k in range(n):
            for cp in sent(k):
                cp.wait_send()
            for _, dev, half, onward in relays:
                for s2, to in onward:
                    copy(k, s2, dev, half, to).wait_send()
            mine(k).wait()

    return _Exchange(
        shards, [jax.ShapeDtypeStruct((N_DEV,) + a.shape, a.dtype) for a in shards],
        [pltpu.SemaphoreType.DMA((n, 13)), pltpu.SemaphoreType.DMA((n, 13)), pltpu.SemaphoreType.DMA((n,))],
        start, mid, finish, early)


def _peer(x, y, c, m):
    px = 1 - x if m & 4 else x
    py = 1 - y if m & 2 else y
    pc = 1 - c if m & 1 else c
    return px, py, pc


def _scatter_exchange(parts):
    n = len(parts)

    def tools(ins, outs, sems):
        send_sems, recv_sems, local_sems = sems
        x, y, c = _place()
        me = 4 * x + 2 * y + c
        def mine():
            return [pltpu.make_async_copy(ins[k].at[me], outs[k].at[me], local_sems.at[k]) for k in range(n)]

        def remote(dst_is_mine):
            out = []
            for m in range(1, N_DEV):
                px, py, pc = _peer(x, y, c, m)
                peer = 4 * px + 2 * py + pc
                for k in range(n):
                    out.append(pltpu.make_async_remote_copy(
                        src_ref=ins[k].at[peer], dst_ref=outs[k].at[me if dst_is_mine else peer],
                        send_sem=send_sems.at[k, m - 1], recv_sem=recv_sems.at[k, m - 1],
                        device_id=(px, py, pc), device_id_type=MESH_ID))
            return out

        return mine, remote

    def start(ins, outs, sems):
        mine, remote = tools(ins, outs, sems)
        for cp in mine() + remote(True):
            cp.start()

    def mid(ins, outs, sems):
        pass

    def finish(ins, outs, sems):
        mine, remote = tools(ins, outs, sems)
        for cp in remote(False):
            cp.wait_recv()
        for cp in remote(True):
            cp.wait_send()
        for cp in mine():
            cp.wait()

    return _Exchange(
        parts, [jax.ShapeDtypeStruct(a.shape, a.dtype) for a in parts],
        [pltpu.SemaphoreType.DMA((n, 7)), pltpu.SemaphoreType.DMA((n, 7)), pltpu.SemaphoreType.DMA((n,))],
        start, mid, finish)


N_CHIPS = N_DEV // 2


def _pair_exchange(parts):
    n = len(parts)

    def copies(ins, outs, sems):
        send_sems, recv_sems = sems
        x, y, c = _place()
        return [pltpu.make_async_remote_copy(
            src_ref=ins[k].at[2 * q + 1 - c], dst_ref=outs[k].at[q], send_sem=send_sems.at[k, q],
            recv_sem=recv_sems.at[k, q], device_id=(x, y, 1 - c), device_id_type=MESH_ID)
            for k in range(n) for q in range(N_CHIPS)]

    def start(ins, outs, sems):
        for cp in copies(ins, outs, sems):
            cp.start()

    def mid(ins, outs, sems):
        pass

    def finish(ins, outs, sems):
        for cp in copies(ins, outs, sems):
            cp.wait_recv()
        for cp in copies(ins, outs, sems):
            cp.wait_send()

    return _Exchange(
        parts, [jax.ShapeDtypeStruct((N_CHIPS,) + a.shape[1:], a.dtype) for a in parts],
        [pltpu.SemaphoreType.DMA((n, N_CHIPS)), pltpu.SemaphoreType.DMA((n, N_CHIPS))], start, mid, finish)


def _pair_add(parts, stage, name, copies=1):
    n = len(parts)
    core = lax.axis_index("c").reshape(1)

    def body(core_ref, *refs):
        for k in range(n):
            mine, theirs = refs[k], refs[n + k]
            total = (mine[0, 0].astype(F32) + theirs[0].astype(F32)).astype(BF16)
            for rep in range(copies):
                refs[(2 + rep) * n + k][0] = total

    in_specs, out_specs = [], []
    for a in parts:
        _, r, cdim = a.shape
        in_specs.append(pl.BlockSpec((1, 1, r // 2, cdim), lambda q, i, core_ref: (q, core_ref[0], i, 0)))
    for a in parts:
        _, r, cdim = a.shape
        in_specs.append(pl.BlockSpec((1, r // 2, cdim), lambda q, i, core_ref: (q, i, 0)))
        out_specs.append(pl.BlockSpec((1, r // 2, cdim), lambda q, i, core_ref: (q, i, 0)))
    return list(_call(
        body, name=name,
        grid_spec=pltpu.PrefetchScalarGridSpec(num_scalar_prefetch=1, grid=(N_CHIPS, 2), in_specs=in_specs,
                                               out_specs=out_specs * copies),
        out_shape=[jax.ShapeDtypeStruct(s.shape, BF16) for s in stage] * copies,
        compiler_params=_params("arbitrary", "arbitrary"),
    )(core, *[a.reshape((N_CHIPS, 2) + a.shape[1:]) for a in parts], *stage))


HBM = pl.BlockSpec(memory_space=pltpu.HBM)
SEMAPHORES = pl.BlockSpec(memory_space=pltpu.SEMAPHORE)


def _chip_copies(srcs, lands, send_sems, recv_sems, mine_is_dst):
    x, y, c = _place()
    me = 2 * x + y
    out = []
    for m in range(1, N_CHIPS):
        px, py, _ = _peer(x, y, c, 2 * m)
        peer = 2 * px + py
        for k in range(len(srcs)):
            pair = k * (N_CHIPS - 1) + m - 1
            out.append(pltpu.make_async_remote_copy(
                src_ref=srcs[k].at[peer], dst_ref=lands[k].at[me if mine_is_dst else peer],
                send_sem=send_sems[pair], recv_sem=recv_sems[pair],
                device_id=(px, py, c), device_id_type=MESH_ID))
    return out


def _chip_exchange_start(chip_parts, lands, after, name):
    n = len(chip_parts)
    pairs = n * (N_CHIPS - 1)

    def body(*refs):
        srcs, zones = refs[:n], refs[n:2 * n]
        sems = refs[2 * n + 1:2 * n + 1 + 2 * pairs]
        token = refs[-1]
        for cp in _chip_copies(srcs, zones, sems[:pairs], sems[pairs:], True):
            cp.start()
        token[...] = jnp.zeros(token.shape, F32)

    thru = [pltpu.HBM(a.shape, a.dtype) for a in chip_parts + lands]
    hbm = [pltpu.with_memory_space_constraint(a, pltpu.HBM) for a in chip_parts + lands]
    res = _call(
        body, name=name,
        out_shape=[pltpu.SemaphoreType.DMA(())] * (2 * pairs) + thru + [jax.ShapeDtypeStruct((8, 128), F32)],
        in_specs=[HBM] * (2 * n) + [ANY], out_specs=[SEMAPHORES] * (2 * pairs) + [HBM] * (2 * n) + [VMEM],
        input_output_aliases={i: 2 * pairs + i for i in range(2 * n)},
        compiler_params=pltpu.CompilerParams(has_side_effects=pltpu.SideEffectType.DATAFLOW_SIDE_EFFECTING),
    )(*hbm, after)
    sems, rest = list(res[:2 * pairs]), res[2 * pairs:]
    return sems, list(rest[:n]), list(rest[n:2 * n]), rest[-1]


def _chip_exchange_wait(sems, chip_parts, lands, after, name):
    n = len(chip_parts)
    pairs = n * (N_CHIPS - 1)

    def body(*refs):
        srcs, zones = refs[:n], refs[n:2 * n]
        sem_refs = refs[2 * n:2 * n + 2 * pairs]
        for cp in _chip_copies(srcs, zones, sem_refs[:pairs], sem_refs[pairs:], False):
            cp.wait_send()
            cp.wait_recv()

    thru = [pltpu.HBM(a.shape, a.dtype) for a in chip_parts + lands]
    res = _call(
        body, name=name, out_shape=thru,
        in_specs=[HBM] * (2 * n) + [SEMAPHORES] * (2 * pairs) + [ANY], out_specs=[HBM] * (2 * n),
        input_output_aliases={i: i for i in range(2 * n)},
        compiler_params=pltpu.CompilerParams(has_side_effects=pltpu.SideEffectType.DATAFLOW_SIDE_EFFECTING),
    )(*chip_parts, *lands, *sems, after)
    return list(res[n:])


def _run_exchange(ex, name, after=None):
    n_in, n_out = len(ex.ins), len(ex.out_shape)
    order = [] if after is None else [after]

    def body(*refs):
        ins, outs = refs[:n_in], refs[n_in + len(order):n_in + len(order) + n_out]
        sems = refs[n_in + len(order) + n_out:]
        ex.start(ins, outs, sems)
        ex.early(ins, outs, sems)
        ex.mid(ins, outs, sems)
        ex.finish(ins, outs, sems)

    return list(_call(body, name=name, out_shape=ex.out_shape, in_specs=[ANY] * (n_in + len(order)),
                      out_specs=[ANY] * n_out, scratch_shapes=ex.sems)(*ex.ins, *order))


def _call_hosting(body, ex, phases, *, in_specs, out_specs, out_shape, scratch_shapes, args, **kw):
    n_in, n_out, n_scr = len(in_specs), len(out_specs), len(scratch_shapes)
    if ex is None:
        res = _call(body, in_specs=in_specs, out_specs=out_specs, out_shape=out_shape,
                    scratch_shapes=scratch_shapes, **kw)(*args)
        return list(res), []
    n_xin, n_xout = len(ex.ins), len(ex.out_shape)

    def hosting(*refs):
        a, b = n_in, n_in + n_xin
        c, d = b + n_out, b + n_out + n_xout
        ins, xins, outs, xouts = refs[:a], refs[a:b], refs[b:c], refs[c:d]
        scr, sems = refs[d:d + n_scr], refs[d + n_scr:]
        first, early, mid, last = phases()

        @pl.when(first)
        def _():
            ex.start(xins, xouts, sems)

        body(*ins, *outs, *scr)

        @pl.when(early)
        def _():
            ex.early(xins, xouts, sems)

        @pl.when(mid)
        def _():
            ex.mid(xins, xouts, sems)

        @pl.when(last)
        def _():
            ex.finish(xins, xouts, sems)

    res = _call(hosting, in_specs=list(in_specs) + [ANY] * n_xin, out_specs=list(out_specs) + [ANY] * n_xout,
                out_shape=list(out_shape) + ex.out_shape, scratch_shapes=list(scratch_shapes) + ex.sems,
                **kw)(*args, *ex.ins)
    return list(res[:n_out]), list(res[n_out:])


def _grid_phases(dims, mid_fraction=0.8, early_fraction=0.4):
    total = 1
    for d in dims:
        total *= d
    mid = min(max(int(total * mid_fraction), 1), total - 1)
    early = min(int(total * early_fraction), mid)

    def phases():
        step = pl.program_id(0)
        for axis in range(1, len(dims)):
            step = step * dims[axis] + pl.program_id(axis)
        return step == 0, step == early, step == mid, step == total - 1
    return phases


def _all_reduce_small(pack, name, after):
    rows = pack.shape[0]

    def body(in_ref, _, out_ref, recv, send_sems, recv_sems):
        x, y, c = _place()
        me = 4 * x + 2 * y + c
        recv[me] = in_ref[...]
        sent = []
        for m in range(1, N_DEV):
            px, py, pc = _peer(x, y, c, m)
            cp = pltpu.make_async_remote_copy(
                src_ref=in_ref, dst_ref=recv.at[me], send_sem=send_sems.at[m - 1], recv_sem=recv_sems.at[m - 1],
                device_id=(px, py, pc), device_id_type=MESH_ID)
            cp.start()
            sent.append(cp)
        for m in range(1, N_DEV):
            px, py, pc = _peer(x, y, c, m)
            peer = 4 * px + 2 * py + pc
            pltpu.make_async_remote_copy(
                src_ref=in_ref, dst_ref=recv.at[peer], send_sem=send_sems.at[m - 1], recv_sem=recv_sems.at[m - 1],
                device_id=(px, py, pc), device_id_type=MESH_ID).wait_recv()
        acc = recv[0]
        for d in range(1, N_DEV):
            acc = acc + recv[d]
        out_ref[...] = acc
        for cp in sent:
            cp.wait_send()

    return _call(
        body, name=name, out_shape=jax.ShapeDtypeStruct(pack.shape, F32), in_specs=[VMEM, ANY], out_specs=VMEM,
        scratch_shapes=[pltpu.VMEM((N_DEV, rows, pack.shape[1]), F32), pltpu.SemaphoreType.DMA((7,)),
                        pltpu.SemaphoreType.DMA((7,))],
        compiler_params=pltpu.CompilerParams(vmem_limit_bytes=VMEM_LIMIT),
    )(pack, after)


POOL_TS = 256


def _pool_counts(first_row, rows, win):
    t = first_row + lax.broadcasted_iota(jnp.int32, (rows, 1), 0)
    return jnp.minimum(t + 1, win).astype(F32)


def _pool_fwd(x, g, w, b, scale):
    nt = S // POOL_TS

    def body(x_ref, g_ref, w_ref, b_ref, s_ref, out_ref, diff_ref, ext):
        i = pl.program_id(0)

        @pl.when(i == 0)
        def _():
            ext[0:MAX_WIN, :] = jnp.zeros((MAX_WIN, D), F32)

        @pl.when(i > 0)
        def _():
            ext[0:MAX_WIN, :] = ext[POOL_TS:POOL_TS + MAX_WIN, :]

        xv = x_ref[...]
        h = xv * _rstd(xv) * g_ref[...]
        ext[MAX_WIN:, :] = h
        for gi in range(N_GROUPS):
            win = 2 << gi
            cols = slice(gi * GROUP, (gi + 1) * GROUP)
            sm = ext[:, cols]
            k = 1
            while k < win:
                sm = sm + pltpu.roll(sm, k, axis=0)
                k *= 2
            pooled = sm[MAX_WIN:, :] / _pool_counts(i * POOL_TS, POOL_TS, win)
            diff = (pooled - h[:, cols]).astype(BF16)
            yv = (_nn(diff, w_ref[gi]) + b_ref[:, cols]) * s_ref[:, cols]
            out_ref[:, cols] = xv[:, cols] + yv
            diff_ref[:, cols] = diff

    row = pl.BlockSpec((1, D), lambda i: (0, 0))
    tile = pl.BlockSpec((POOL_TS, D), lambda i: (i, 0))
    return _call(
        body, name="pool_fwd", grid=(nt,),
        in_specs=[tile, row, pl.BlockSpec((N_GROUPS, GROUP, GROUP), lambda i: (0, 0, 0)), row, row],
        out_specs=[tile, tile],
        out_shape=[jax.ShapeDtypeStruct((S, D), F32), jax.ShapeDtypeStruct((S, D), BF16)],
        scratch_shapes=[pltpu.VMEM((POOL_TS + MAX_WIN, D), F32)],
        compiler_params=_params("arbitrary"),
    )(x, g, w, b, scale)


def _pool_bwd(dy, x, diff, g, w, b, scale, ex=None):
    nt = S // POOL_TS

    def body(dy_ref, x_ref, diff_ref, g_ref, w_ref, b_ref, s_ref, gx_ref, dw_ref, db_ref, ds_ref, dg_ref, ext, dh):
        i = pl.program_id(0)
        first_row = (nt - 1 - i) * POOL_TS

        @pl.when(i == 0)
        def _():
            ext[POOL_TS:, :] = jnp.zeros((MAX_WIN, D), F32)
            dw_ref[...] = jnp.zeros(dw_ref.shape, F32)
            db_ref[...] = jnp.zeros(db_ref.shape, F32)
            ds_ref[...] = jnp.zeros(ds_ref.shape, F32)
            dg_ref[...] = jnp.zeros(dg_ref.shape, F32)

        @pl.when(i > 0)
        def _():
            ext[POOL_TS:, :] = ext[0:MAX_WIN, :]

        dyv = dy_ref[...]
        for gi in range(N_GROUPS):
            win = 2 << gi
            cols = slice(gi * GROUP, (gi + 1) * GROUP)
            dfb = diff_ref[:, cols]
            z = _nn(dfb, w_ref[gi]) + b_ref[:, cols]
            dyg = dyv[:, cols]
            ds_ref[:, cols] += _colsum8(dyg * z)
            dz = dyg * s_ref[:, cols]
            db_ref[:, cols] += _colsum8(dz)
            dzb = dz.astype(BF16)
            dw_ref[gi] += _tn(dfb, dzb)
            ddiff = _nt(dzb, w_ref[gi])
            ext[0:POOL_TS, cols] = ddiff / _pool_counts(first_row, POOL_TS, win)
            sm = ext[:, cols]
            k = 1
            while k < win:
                sm = sm + pltpu.roll(sm, POOL_TS + MAX_WIN - k, axis=0)
                k *= 2
            dh[:, cols] = sm[0:POOL_TS, :] - ddiff
        xv = x_ref[...]
        r = _rstd(xv)
        gv = g_ref[...]
        dhv = dh[...]
        dg_ref[...] += _colsum8(dhv * xv * r)
        gx_ref[...] = dyv + _rms_bwd(dhv, xv, r, gv)

    row = pl.BlockSpec((1, D), lambda i: (0, 0))
    tile = pl.BlockSpec((POOL_TS, D), lambda i: (nt - 1 - i, 0))
    acc = pl.BlockSpec((8, D), lambda i: (0, 0))
    wspec = pl.BlockSpec((N_GROUPS, GROUP, GROUP), lambda i: (0, 0, 0))
    return _call_hosting(
        body, ex, _grid_phases((nt,)), name="pool_bwd", grid=(nt,),
        in_specs=[tile, tile, tile, row, wspec, row, row],
        out_specs=[tile, wspec, acc, acc, acc],
        out_shape=[jax.ShapeDtypeStruct((S, D), F32), jax.ShapeDtypeStruct((N_GROUPS, GROUP, GROUP), F32),
                   jax.ShapeDtypeStruct((8, D), F32), jax.ShapeDtypeStruct((8, D), F32),
                   jax.ShapeDtypeStruct((8, D), F32)],
        scratch_shapes=[pltpu.VMEM((POOL_TS + MAX_WIN, D), F32), pltpu.VMEM((POOL_TS, D), F32)],
        args=(dy, x, diff, g, w, b, scale), compiler_params=_params("arbitrary"),
    )


FFN_TS = min(S, 1024)
FFN_TF = 256


def _ffn_fwd(x, g, wg_t, wu_t, wd, name, target=None, ex=None):
    width = wd.shape[0]
    ni, nj = S // FFN_TS, width // FFN_TF
    with_loss = target is not None
    extra = [target] if with_loss else []

    def body(*refs):
        x_ref, g_ref, wg_ref, wu_ref, wd_ref = refs[:5]
        if with_loss:
            t_ref, out_ref, loss_ref, h_ref, gg_ref, uu_ref, hs, acc = refs[5:]
        else:
            out_ref, h_ref, gg_ref, uu_ref, hs, acc = refs[5:]
        i, j = pl.program_id(0), pl.program_id(1)

        @pl.when(j == 0)
        def _():
            xv = x_ref[...]
            hb = (xv * _rstd(xv) * g_ref[...]).astype(BF16)
            hs[...] = hb
            h_ref[...] = hb
            acc[...] = jnp.zeros(acc.shape, F32)

        hb = hs[...]
        gg = _nt(hb, wg_ref[...])
        uu = _nt(hb, wu_ref[...])
        gg_ref[...] = gg
        uu_ref[...] = uu
        a = (gg * jax.nn.sigmoid(gg) * uu).astype(BF16)
        acc[...] += _nn(a, wd_ref[...])

        @pl.when(j == nj - 1)
        def _():
            yv = x_ref[...] + acc[...]
            if with_loss:
                err = yv - t_ref[...]
                out_ref[...] = err * (1.0 / D)
                part = jnp.sum(err * err) * (0.5 / D)

                @pl.when(i == 0)
                def _():
                    loss_ref[...] = jnp.zeros(loss_ref.shape, F32)

                loss_ref[...] += jnp.broadcast_to(part, loss_ref.shape)
            else:
                out_ref[...] = yv

    xt = pl.BlockSpec((FFN_TS, D), lambda i, j: (i, 0))
    row = pl.BlockSpec((1, D), lambda i, j: (0, 0))
    wt = pl.BlockSpec((FFN_TF, D), lambda i, j: (j, 0))
    gt = pl.BlockSpec((FFN_TS, FFN_TF), lambda i, j: (i, j))
    in_specs = [xt, row, wt, wt, wt] + [xt] * len(extra)
    out_specs = [xt] + ([pl.BlockSpec((8, 128), lambda i, j: (0, 0))] if with_loss else []) + [xt, gt, gt]
    out_shape = ([jax.ShapeDtypeStruct((S, D), F32)] + ([jax.ShapeDtypeStruct((8, 128), F32)] if with_loss else [])
                 + [jax.ShapeDtypeStruct((S, D), BF16), jax.ShapeDtypeStruct((S, width), F32),
                    jax.ShapeDtypeStruct((S, width), F32)])
    args = (x, g, wg_t, wu_t, wd, *extra)
    return _call_hosting(
        body, ex, _grid_phases((ni, nj), 0.8), name=name, grid=(ni, nj),
        in_specs=in_specs, out_specs=out_specs, out_shape=out_shape,
        scratch_shapes=[pltpu.VMEM((FFN_TS, D), BF16), pltpu.VMEM((FFN_TS, D), F32)], args=args,
        compiler_params=_params("arbitrary", "arbitrary"),
    )


def _ffn_bwd_weights(dout, h, gg, uu, wd, name, ex=None):
    width = wd.shape[0]
    nj = width // FFN_TF

    def body(do_ref, h_ref, gg_ref, uu_ref, wd_ref, dg_ref, du_ref, dwg_ref, dwu_ref, dwd_ref, dob, kept):
        j = pl.program_id(0)

        @pl.when(j == 0)
        def _():
            dob[...] = do_ref[...].astype(BF16)
            kept[1] = jnp.zeros(kept.shape[1:], BF16)

        prev, cur = (j + 1) % 2, j % 2
        hb, dov = h_ref[...], dob[...]
        dwg_ref[...] = _tn(kept[prev, 0], hb).astype(BF16)
        dwu_ref[...] = _tn(kept[prev, 1], hb).astype(BF16)
        dwd_ref[...] = _tn(kept[prev, 2], dov).astype(BF16)
        gv, uv = gg_ref[...], uu_ref[...]
        da = _nt(dov, wd_ref[...])
        sg = jax.nn.sigmoid(gv)
        sl = gv * sg
        dub = (da * sl).astype(BF16)
        dgb = (da * uv * (sg * (1.0 + gv * (1.0 - sg)))).astype(BF16)
        dg_ref[...] = dgb
        du_ref[...] = dub
        kept[cur, 0] = dgb
        kept[cur, 1] = dub
        kept[cur, 2] = (sl * uv).astype(BF16)

    once = pl.Buffered(1)
    whole = lambda: pl.BlockSpec((S, D), lambda j, _: (0, 0), pipeline_mode=once)
    this = lambda j: jnp.minimum(j, nj - 1)
    last = lambda j: jnp.maximum(j - 1, 0)
    wt_in = pl.BlockSpec((FFN_TF, D), lambda j, _: (this(j), 0))
    wt_out = pl.BlockSpec((FFN_TF, D), lambda j, _: (last(j), 0))
    gt = pl.BlockSpec((S, FFN_TF), lambda j, _: (0, this(j)))
    return _call_hosting(
        body, ex, _grid_phases((nj + 1, 1)), name=name, grid=(nj + 1, 1),
        in_specs=[whole(), whole(), gt, gt, wt_in], out_specs=[gt, gt, wt_out, wt_out, wt_out],
        out_shape=[jax.ShapeDtypeStruct((S, width), BF16)] * 2 + [jax.ShapeDtypeStruct((width, D), BF16)] * 3,
        scratch_shapes=[pltpu.VMEM((S, D), BF16), pltpu.VMEM((2, 3, S, FFN_TF), BF16)], args=(dout, h, gg, uu, wd),
        compiler_params=_params("arbitrary", "arbitrary"),
    )


BWD_TS = 512


def _ffn_bwd_input(dres, x, g, grads, weights, name, ex=None):
    nt = S // BWD_TS
    n = len(grads)

    def body(*refs):
        dres_ref, x_ref, g_ref = refs[:3]
        grad_refs, w_refs = refs[3:3 + n], refs[3 + n:3 + 2 * n]
        dx_ref, dgam_ref = refs[3 + 2 * n:]
        dh = _nn(grad_refs[0][...], w_refs[0][...])
        for k in range(1, n):
            dh = dh + _nn(grad_refs[k][...], w_refs[k][...])
        xv = x_ref[...]
        r = _rstd(xv)
        dx_ref[...] = dres_ref[...] + _rms_bwd(dh, xv, r, g_ref[...])

        @pl.when(pl.program_id(0) == 0)
        def _():
            dgam_ref[...] = jnp.zeros(dgam_ref.shape, F32)

        dgam_ref[...] += _colsum8(dh * xv * r)

    tile = pl.BlockSpec((BWD_TS, D), lambda i: (i, 0))
    ftiles = [pl.BlockSpec((BWD_TS, a.shape[1]), lambda i: (i, 0)) for a in grads]
    wspecs = [pl.BlockSpec(w.shape, lambda i: (0, 0), pipeline_mode=pl.Buffered(1)) for w in weights]
    return _call_hosting(
        body, ex, _grid_phases((nt,)), name=name, grid=(nt,),
        in_specs=[tile, tile, pl.BlockSpec((1, D), lambda i: (0, 0))] + ftiles + wspecs,
        out_specs=[tile, pl.BlockSpec((8, D), lambda i: (0, 0))],
        out_shape=[jax.ShapeDtypeStruct((S, D), F32), jax.ShapeDtypeStruct((8, D), F32)],
        scratch_shapes=[], args=(dres, x, g, *grads, *weights), compiler_params=_params("arbitrary"),
    )


def _mm(a, b, mode, out_dtype, name, add=None, skip_rows=0):
    if mode == "nn":
        (m, kd), n = (a.shape[0] - skip_rows, a.shape[1]), b.shape[1]
    elif mode == "nt":
        (m, kd), n = (a.shape[0] - skip_rows, a.shape[1]), b.shape[0]
    else:
        (kd, m), n = a.shape, b.shape[1]
    tm, tn, tk = min(m, 1024), min(n, 1024), min(kd, 1024)
    if skip_rows and mode == "tn":
        tk = min(tk, skip_rows)
    elif skip_rows:
        tm = min(tm, skip_rows)
    skip_a = skip_rows // tm if mode != "tn" else 0
    skip_b = skip_rows // tk if mode == "tn" else 0
    assert skip_rows == skip_a * tm + skip_b * tk
    nk = kd // tk
    dot = {"nn": _nn, "nt": _nt, "tn": _tn}[mode]

    def body(*refs):
        if add is None:
            a_ref, b_ref, o_ref, acc = refs
        else:
            a_ref, b_ref, add_ref, o_ref, acc = refs
        k = pl.program_id(2)

        @pl.when(k == 0)
        def _():
            acc[...] = jnp.zeros(acc.shape, F32)

        acc[...] += dot(a_ref[...].astype(BF16), b_ref[...].astype(BF16))

        @pl.when(k == nk - 1)
        def _():
            res = acc[...]
            if add is not None:
                res = res + add_ref[...]
            o_ref[...] = res.astype(out_dtype)

    if mode == "tn":
        a_spec = pl.BlockSpec((tk, tm), lambda i, j, k: (k, i))
        b_spec = pl.BlockSpec((tk, tn), lambda i, j, k: (k + skip_b, j))
    else:
        a_spec = pl.BlockSpec((tm, tk), lambda i, j, k: (i + skip_a, k))
        b_spec = (pl.BlockSpec((tk, tn), lambda i, j, k: (k, j)) if mode == "nn"
                  else pl.BlockSpec((tn, tk), lambda i, j, k: (j, k)))
    o_spec = pl.BlockSpec((tm, tn), lambda i, j, k: (i, j))
    in_specs = [a_spec, b_spec] + ([o_spec] if add is not None else [])
    args = (a, b) + ((add,) if add is not None else ())
    return _call(
        body, name=name, grid=(m // tm, n // tn, nk), in_specs=in_specs, out_specs=o_spec,
        out_shape=jax.ShapeDtypeStruct((m, n), out_dtype), scratch_shapes=[pltpu.VMEM((tm, tn), F32)],
        compiler_params=_params("parallel", "parallel", "arbitrary"),
    )(*args)


PROJ_TS = 256


def _kvq_proj(x, g_kv, g_mix, wk, wv, wq, gk, gq, ex=None):
    lead = PADK // PROJ_TS

    def body(x_ref, gkv_ref, gmix_ref, wk_ref, wv_ref, wq_ref, gk_ref, gq_ref,
             hkv_ref, h1_ref, kpre_ref, qpre_ref, k_ref, v_ref, q_ref):
        i = pl.program_id(0)

        @pl.when(i < lead)
        def _():
            k_ref[...] = jnp.zeros(k_ref.shape, BF16)
            v_ref[...] = jnp.zeros(v_ref.shape, BF16)

        @pl.when(i >= lead)
        def _():
            xv = x_ref[...]
            xr = xv * _rstd(xv)
            hkv = (xr * gkv_ref[...]).astype(BF16)
            h1 = (xr * gmix_ref[...]).astype(BF16)
            hkv_ref[...] = hkv
            h1_ref[...] = h1
            kpre = _nn(hkv, wk_ref[...])
            qpre = _nn(h1, wq_ref[...])
            kpre_ref[...] = kpre
            qpre_ref[...] = qpre
            v_ref[...] = _nn(hkv, wv_ref[...]).astype(BF16)
            rk = lax.rsqrt(_seg_sum(kpre * kpre) * (1.0 / HEAD) + EPS)
            k_ref[...] = (kpre * rk * gk_ref[...]).astype(BF16)
            rq = lax.rsqrt(_seg_sum(qpre * qpre) * (1.0 / HEAD) + EPS)
            q_ref[...] = (qpre * rq * gq_ref[...]).astype(BF16)

    tile = pl.BlockSpec((PROJ_TS, D), lambda i: (jnp.maximum(i - lead, 0), 0))
    padded = pl.BlockSpec((PROJ_TS, D), lambda i: (i, 0))
    row = pl.BlockSpec((1, D), lambda i: (0, 0))
    wspec = pl.BlockSpec((D, D), lambda i: (0, 0))
    bf = jax.ShapeDtypeStruct((S, D), BF16)
    ff = jax.ShapeDtypeStruct((S, D), F32)
    bp = jax.ShapeDtypeStruct((PADK + S, D), BF16)
    return _call_hosting(
        body, ex, _grid_phases((lead + S // PROJ_TS,), 0.85), name="kvq_proj", grid=(lead + S // PROJ_TS,),
        in_specs=[tile, row, row, wspec, wspec, wspec, row, row],
        out_specs=[tile, tile, tile, tile, padded, padded, tile], out_shape=[bf, bf, ff, ff, bp, bp, bf],
        scratch_shapes=[], args=(x, g_kv, g_mix, wk, wv, wq, gk, gq), compiler_params=_params("arbitrary"),
    )


def _qkv_input_bwd(dq, qpre, gq, dkp, kpre, gk, dvp, wq, wk, wv, dres, x, g_mix, g_kv, ex=None):
    nt = S // PROJ_TS
    lead = PADK // PROJ_TS

    def head_bwd(dov, pv, hgv):
        r = lax.rsqrt(_seg_sum(pv * pv) * (1.0 / HEAD) + EPS)
        gd = dov * hgv
        dpre = r * gd - pv * (r * r * r) * (_seg_sum(gd * pv) * (1.0 / HEAD))
        return dpre.astype(BF16), _colsum8(dov * pv * r)

    def fold_heads(full):
        fold = full[:, 0:128]
        for blk in range(1, D // 128):
            fold = fold + full[:, blk * 128:(blk + 1) * 128]
        return fold + pltpu.roll(fold, HEAD, axis=1)

    def body(dq_ref, qpre_ref, gq_ref, dk_ref, kpre_ref, gk_ref, dv_ref, wq_ref, wk_ref, wv_ref, dres_ref, x_ref,
             gmix_ref, gkv_ref, dx_ref, dqpre_ref, dkpre_ref, dgmix_ref, dgkv_ref, dgq_ref, dgk_ref, accq, acck):
        i = pl.program_id(0)

        @pl.when(i == 0)
        def _():
            accq[...] = jnp.zeros(accq.shape, F32)
            acck[...] = jnp.zeros(acck.shape, F32)
            dgmix_ref[...] = jnp.zeros(dgmix_ref.shape, F32)
            dgkv_ref[...] = jnp.zeros(dgkv_ref.shape, F32)

        dqb, cq = head_bwd(dq_ref[...], qpre_ref[...], gq_ref[...])
        dkb, ck = head_bwd(dk_ref[...], kpre_ref[...], gk_ref[...])
        dqpre_ref[...] = dqb
        dkpre_ref[...] = dkb
        accq[...] += cq
        acck[...] += ck
        dh1 = _nt(dqb, wq_ref[...])
        dhkv = _nt(dkb, wk_ref[...]) + _nt(dv_ref[...].astype(BF16), wv_ref[...])
        xv = x_ref[...]
        r = _rstd(xv)
        dx_ref[...] = dres_ref[...] + _rms_bwd(dh1, xv, r, gmix_ref[...]) + _rms_bwd(dhkv, xv, r, gkv_ref[...])
        dgmix_ref[...] += _colsum8(dh1 * xv * r)
        dgkv_ref[...] += _colsum8(dhkv * xv * r)

        @pl.when(i == nt - 1)
        def _():
            dgq_ref[...] = fold_heads(accq[...])
            dgk_ref[...] = fold_heads(acck[...])

    tile = pl.BlockSpec((PROJ_TS, D), lambda i: (i, 0))
    behind = pl.BlockSpec((PROJ_TS, D), lambda i: (i + lead, 0))
    row = pl.BlockSpec((1, D), lambda i: (0, 0))
    wspec = lambda: pl.BlockSpec((D, D), lambda i: (0, 0), pipeline_mode=pl.Buffered(1))
    acc = pl.BlockSpec((8, D), lambda i: (0, 0))
    small = pl.BlockSpec((8, 128), lambda i: (0, 0))
    bf = jax.ShapeDtypeStruct((S, D), BF16)
    return _call_hosting(
        body, ex, _grid_phases((nt,)), name="qkv_input_bwd", grid=(nt,),
        in_specs=[tile, tile, row, behind, tile, row, behind, wspec(), wspec(), wspec(), tile, tile, row, row],
        out_specs=[tile, tile, tile, acc, acc, small, small],
        out_shape=[jax.ShapeDtypeStruct((S, D), F32), bf, bf, jax.ShapeDtypeStruct((8, D), F32),
                   jax.ShapeDtypeStruct((8, D), F32), jax.ShapeDtypeStruct((8, 128), F32),
                   jax.ShapeDtypeStruct((8, 128), F32)],
        scratch_shapes=[pltpu.VMEM((8, D), F32), pltpu.VMEM((8, D), F32)],
        args=(dq, qpre, gq, dkp, kpre, gk, dvp, wq, wk, wv, dres, x, g_mix, g_kv),
        compiler_params=_params("arbitrary"),
    )


def _toeplitz_from_table(table):
    far = jnp.broadcast_to(table[:, N_REL - 1:], (N_HEADS, PADK - MAX_REL + 1))
    near = table[:, N_REL - 2::-1]
    past = jnp.broadcast_to(table[:, 0:1], (N_HEADS, MAX_REL))
    wrap = jnp.broadcast_to(table[:, N_REL - 1:], (N_HEADS, TOEP - (PADK + 2 * MAX_REL + 1)))
    return jnp.concatenate([far, near, past, wrap], axis=1).reshape(N_HEADS, 1, TOEP)


def _table_grad_from_toeplitz(dtp, seg):
    lo = PADK - MAX_REL + 1
    near = dtp[:, lo + N_REL - 3:lo - 1:-1]
    return jnp.concatenate([seg[:, 1:2], near, seg[:, 0:1]], axis=1)


def _bias_band(tp):
    def body(tp_ref, out_ref):
        bv = pltpu.roll(jnp.broadcast_to(tp_ref[0], (QB, TOEP)), 0, axis=1, stride=1, stride_axis=0)
        out_ref[0] = jnp.where(_band_mask(), bv[:, 0:KB], NEG_INF)

    return _call(
        body, name="bias_band", grid=(N_HEADS,),
        in_specs=[pl.BlockSpec((1, 1, TOEP), lambda h: (h, 0, 0))],
        out_specs=pl.BlockSpec((1, QB, KB), lambda h: (h, 0, 0)),
        out_shape=jax.ShapeDtypeStruct((N_HEADS, QB, KB), F32),
        compiler_params=_params("parallel"),
    )(tp)


def _bias_grad(dband, after):
    lo, hi = PADK - MAX_REL + 1, PADK + MAX_REL

    def body(db_ref, _, dtp_ref, seg_ref):
        bv = jnp.concatenate([db_ref[0], jnp.zeros((QB, TOEP - KB), F32)], axis=1)
        row = lax.broadcasted_iota(jnp.int32, (QB, TOEP), 0)
        k = 1
        while k < QB:
            bv = jnp.where((row & k) != 0, pltpu.roll(bv, TOEP - k, axis=1), bv)
            k *= 2
        col = jnp.sum(bv, axis=0, keepdims=True)
        dtp_ref[0] = col
        u = lax.broadcasted_iota(jnp.int32, (1, TOEP), 1)
        far = jnp.sum(jnp.where((u < lo) | (u > hi + MAX_REL), col, 0.0))
        past = jnp.sum(jnp.where((u >= hi) & (u <= hi + MAX_REL), col, 0.0))
        lane = lax.broadcasted_iota(jnp.int32, (1, 128), 1)
        seg_ref[0] = jnp.where(lane == 0, far, jnp.where(lane == 1, past, 0.0))

    return _call(
        body, name="bias_grad", grid=(N_HEADS,),
        in_specs=[pl.BlockSpec((1, QB, KB), lambda h: (h, 0, 0)), ANY],
        out_specs=[pl.BlockSpec((1, 1, TOEP), lambda h: (h, 0, 0)), pl.BlockSpec((1, 1, 128), lambda h: (h, 0, 0))],
        out_shape=[jax.ShapeDtypeStruct((N_HEADS, 1, TOEP), F32), jax.ShapeDtypeStruct((N_HEADS, 1, 128), F32)],
        compiler_params=_params("parallel"),
    )(dband, after)


N_QB = S // QB
HEADS_PER_STEP = 4
ATT_LANES = HEADS_PER_STEP * HEAD
N_HG = D // ATT_LANES


def _band_mask():
    qc = lax.broadcasted_iota(jnp.int32, (QB, KB), 0) // CHUNK
    kc = lax.broadcasted_iota(jnp.int32, (QB, KB), 1) // CHUNK
    return (kc >= qc) & (kc <= qc + LEFT)


def _half_scale(hh, scale):
    lane = lax.broadcasted_iota(jnp.int32, (1, 128), 1)
    return jnp.where((lane < HEAD) == (hh == 0), scale, 0.0).astype(BF16)


def _probs(qh, kb, bias, first_key):
    sc = _nt(qh, kb) + bias
    if first_key is not None:
        sc = jnp.where(lax.broadcasted_iota(jnp.int32, (QB, KB), 1) >= first_key, sc, NEG_INF)
    e = jnp.exp(sc - jnp.max(sc, axis=-1, keepdims=True))
    return e * (1.0 / jnp.sum(e, axis=-1, keepdims=True))


def _by_padding(cb, compute):
    @pl.when(cb < PADK // QB)
    def _():
        compute(PADK - cb * QB)

    @pl.when(cb >= PADK // QB)
    def _():
        compute(None)


def _attn_fwd(q, kp, vp, bias, ex=None):
    def body(q_ref, k_ref, v_ref, b_ref, o_ref):
        cb = pl.program_id(1)
        band = pl.ds(pl.multiple_of(cb * QB, QB), KB)
        low = lax.broadcasted_iota(jnp.int32, (QB, 128), 1) < HEAD

        def compute(first_key):
            for pair in range(HEADS_PER_STEP // 2):
                lanes = pl.ds(pair * 128, 128)
                kb, vb, qv = k_ref[band, lanes], v_ref[band, lanes], q_ref[:, lanes]
                outs = []
                for hh in range(2):
                    pb = _probs(qv * _half_scale(hh, ATTN_SCALE), kb, b_ref[2 * pair + hh], first_key).astype(BF16)
                    outs.append(_nn(pb, vb))
                o_ref[:, lanes] = jnp.where(low, outs[0], outs[1]).astype(BF16)

        _by_padding(cb, compute)

    qspec = pl.BlockSpec((QB, ATT_LANES), lambda hg, cb: (cb, hg))
    kspec = pl.BlockSpec((PADK + S, ATT_LANES), lambda hg, cb: (0, hg))
    return _call_hosting(
        body, ex, _grid_phases((N_HG, N_QB), 0.85), name="attn_fwd", grid=(N_HG, N_QB),
        in_specs=[qspec, kspec, kspec, pl.BlockSpec((HEADS_PER_STEP, QB, KB), lambda hg, cb: (hg, 0, 0))],
        out_specs=[qspec], out_shape=[jax.ShapeDtypeStruct((S, D), BF16)], scratch_shapes=[],
        args=(q, kp, vp, bias), compiler_params=_params("arbitrary", "arbitrary"),
    )


def _attn_bwd(q, kp, vp, bias, do, ex=None):
    def body(q_ref, k_ref, v_ref, b_ref, do_ref, dq_ref, dk_ref, dv_ref, db_ref):
        cb = pl.program_id(1)

        @pl.when(cb == 0)
        def _():
            dk_ref[...] = jnp.zeros(dk_ref.shape, F32)
            dv_ref[...] = jnp.zeros(dv_ref.shape, F32)
            db_ref[...] = jnp.zeros(db_ref.shape, F32)

        band = pl.ds(pl.multiple_of(cb * QB, QB), KB)
        low = lax.broadcasted_iota(jnp.int32, (QB, 128), 1) < HEAD

        def compute(first_key):
            for pair in range(HEADS_PER_STEP // 2):
                lanes = pl.ds(pair * 128, 128)
                kb, vb = k_ref[band, lanes], v_ref[band, lanes]
                qv, dov = q_ref[:, lanes], do_ref[:, lanes]
                dq = jnp.zeros((QB, 128), F32)
                dkb = jnp.zeros((KB, 128), F32)
                dvb = jnp.zeros((KB, 128), F32)
                for hh in range(2):
                    sel = low if hh == 0 else jnp.logical_not(low)
                    doh = dov * _half_scale(hh, 1.0)
                    p = _probs(qv * _half_scale(hh, ATTN_SCALE), kb, b_ref[2 * pair + hh], first_key)
                    dp = _nt(doh, vb)
                    dvb = dvb + _tn(p.astype(BF16), doh)
                    ds = p * (dp - jnp.sum(dp * p, axis=-1, keepdims=True))
                    db_ref[2 * pair + hh] += ds
                    dsb = (ds * ATTN_SCALE).astype(BF16)
                    dq = dq + jnp.where(sel, _nn(dsb, kb), 0.0)
                    dkb = dkb + _tn(dsb, qv * _half_scale(hh, 1.0))
                dq_ref[:, lanes] = dq
                dk_ref[band, lanes] += dkb
                dv_ref[band, lanes] += dvb

        _by_padding(cb, compute)

    qspec = pl.BlockSpec((QB, ATT_LANES), lambda hg, cb: (cb, hg))
    kspec = pl.BlockSpec((PADK + S, ATT_LANES), lambda hg, cb: (0, hg))
    bspec = pl.BlockSpec((HEADS_PER_STEP, QB, KB), lambda hg, cb: (hg, 0, 0))
    kf = jax.ShapeDtypeStruct((PADK + S, D), F32)
    return _call_hosting(
        body, ex, _grid_phases((N_HG, N_QB)), name="attn_bwd", grid=(N_HG, N_QB),
        in_specs=[qspec, kspec, kspec, bspec, qspec],
        out_specs=[qspec, kspec, kspec, bspec],
        out_shape=[jax.ShapeDtypeStruct((S, D), F32), kf, kf, jax.ShapeDtypeStruct((N_HEADS, QB, KB), F32)],
        scratch_shapes=[], args=(q, kp, vp, bias, do), compiler_params=_params("arbitrary", "arbitrary"),
    )


def _adam_math(w, g, m, v):
    m = ADAM_B1 * m + (1.0 - ADAM_B1) * g
    v = ADAM_B2 * v + (1.0 - ADAM_B2) * (g * g)
    m_hat = m / (1.0 - ADAM_B1 ** ADAM_STEP)
    v_hat = v / (1.0 - ADAM_B2 ** ADAM_STEP)
    delta = -ADAM_LR * (m_hat / (jnp.sqrt(v_hat) + ADAM_EPS) + ADAM_WD * w)
    return delta, m, v


def _sum_parts(parts, name, after=None):
    count, r, c = parts.shape
    tr = r // 2 if r % 16 == 0 and r > 64 else r

    def body(p_ref, *rest):
        o_ref = rest[-1]
        acc = p_ref[0].astype(F32)
        for d in range(1, count):
            acc = acc + p_ref[d].astype(F32)
        o_ref[...] = acc

    return _call(
        body, name=name, grid=(r // tr,),
        in_specs=[pl.BlockSpec((count, tr, c), lambda i: (0, i, 0))] + ([ANY] if after is not None else []),
        out_specs=pl.BlockSpec((tr, c), lambda i: (i, 0)),
        out_shape=jax.ShapeDtypeStruct((r, c), F32), compiler_params=_params("parallel"),
    )(parts, *([after] if after is not None else []))


def _row_tile(r):
    for cand in (256, 176, 128, 64, 32, 16, 8):
        if r % cand == 0:
            return cand
    return r


def _adam_layer(w, g, m, v, layer, name, prev=None):
    nl, r, c = w.shape
    tr = _row_tile(r)

    def body(w_ref, g_ref, m_ref, v_ref, *rest):
        go_ref, d_ref, nm_ref, nv_ref = rest[-4:]
        gv = g_ref[...]
        delta, nm, nv = _adam_math(w_ref[0], gv, m_ref[0], v_ref[0])
        go_ref[0] = gv
        d_ref[0] = delta
        nm_ref[0] = nm
        nv_ref[0] = nv

    lspec = pl.BlockSpec((1, tr, c), lambda i: (layer, i, 0))
    sd = jax.ShapeDtypeStruct((nl, r, c), F32)
    extra = list(prev) if prev is not None else []
    return _call(
        body, name=name, grid=(r // tr,),
        in_specs=[lspec, pl.BlockSpec((tr, c), lambda i: (i, 0)), lspec, lspec] + [ANY] * len(extra),
        out_specs=[lspec] * 4, out_shape=[sd] * 4,
        input_output_aliases={4 + t: t for t in range(len(extra))},
        compiler_params=_params("parallel"),
    )(w, g, m, v, *extra)


def _adam(w, g, m, v, name):
    r, c = w.shape
    tr = _row_tile(r)

    def body(w_ref, g_ref, m_ref, v_ref, d_ref, nm_ref, nv_ref):
        delta, nm, nv = _adam_math(w_ref[...], g_ref[...], m_ref[...], v_ref[...])
        d_ref[...] = delta
        nm_ref[...] = nm
        nv_ref[...] = nv

    spec = pl.BlockSpec((tr, c), lambda i: (i, 0))
    sd = jax.ShapeDtypeStruct((r, c), F32)
    return _call(
        body, name=name, grid=(r // tr,), in_specs=[spec] * 4, out_specs=[spec] * 3, out_shape=[sd] * 3,
        compiler_params=_params("parallel"),
    )(w, g, m, v)


def _pad16(rows):
    return jnp.pad(rows, ((0, 16 - rows.shape[0]), (0, 0)))


def kernel(x, norm_mix_g, norm_ffn_g, pool_w, pool_b, pool_scale, kv_norm_g, w_k, w_v, k_norm_g, w_q, q_norm_g, rel_bias, w_o, w_gate, w_up, w_down, loss_target, m_norm_mix_g, m_norm_ffn_g, m_pool_w, m_pool_b, m_pool_scale, m_kv_norm_g, m_w_k, m_w_v, m_k_norm_g, m_w_q, m_q_norm_g, m_rel_bias, m_w_o, m_w_gate, m_w_up, m_w_down, v_norm_mix_g, v_norm_ffn_g, v_pool_w, v_pool_b, v_pool_scale, v_kv_norm_g, v_w_k, v_w_v, v_k_norm_g, v_w_q, v_q_norm_g, v_rel_bias, v_w_o, v_w_gate, v_w_up, v_w_down):
    assert x.shape == (1, S, D) and w_gate.shape == (2, D, F_SHARD) and w_k.shape == (D_SHARD, D)
    xin, target = x[0], loss_target[0]

    ffn_shards = [[w_gate[layer].T.astype(BF16), w_up[layer].T.astype(BF16), w_down[layer].astype(BF16)]
                  for layer in range(2)]
    att_shards = [w_k.astype(BF16), w_v.astype(BF16), w_q[0].astype(BF16), w_o[0].astype(BF16)]
    pool_shard = pool_w[0].astype(BF16).reshape(N_GROUPS * POOL_SHARD, GROUP)
    small = jnp.concatenate([pool_b[0].reshape(1, N_GROUPS * POOL_SHARD), pool_scale], axis=1)

    full0 = _run_exchange(_relay_gather_exchange(ffn_shards[0] + [pool_shard, _pad16(small)]), "gather_layer0")
    ffn_w0 = [a.reshape(F, D) for a in full0[:3]]
    pw_f = full0[3].reshape(N_DEV, N_GROUPS, POOL_SHARD, GROUP).transpose(1, 0, 2, 3).reshape(N_GROUPS, GROUP, GROUP)
    small_f = full0[4][:, 0, :]
    pb_f = small_f[:, :N_GROUPS * POOL_SHARD].reshape(N_DEV, N_GROUPS, POOL_SHARD).transpose(1, 0, 2).reshape(1, D)
    ps_f = small_f[:, N_GROUPS * POOL_SHARD:].reshape(1, D)

    g_mix0, g_mix1 = norm_mix_g[0:1], norm_mix_g[1:2]
    g_ffn0, g_ffn1 = norm_ffn_g[0:1], norm_ffn_g[1:2]
    g_kv = kv_norm_g.reshape(1, D)
    gk_t = jnp.tile(k_norm_g.reshape(1, HEAD), (1, N_HEADS))
    gq_t = jnp.tile(q_norm_g.reshape(1, HEAD), (1, N_HEADS))

    x1, diff = _pool_fwd(xin, g_mix0, pw_f, pb_f, ps_f)
    (x2, hf0, gg0, uu0), full_att = _ffn_fwd(x1, g_ffn0, *ffn_w0, name="ffn_fwd", ex=_relay_gather_exchange(att_shards))
    wk_f, wv_f, wq_f, wo_f = [a.reshape(D, D) for a in full_att]
    (hkv, h1, kpre, qpre, kp, vp, qq), full1_gate = _kvq_proj(x2, g_kv, g_mix1, wk_f, wv_f, wq_f, gk_t, gq_t,
                                                                ex=_relay_gather_exchange(ffn_shards[1][:1]))
    bias = _bias_band(_toeplitz_from_table(rel_bias[0]))
    (att,), full1_rest = _attn_fwd(qq, kp, vp, bias, ex=_relay_gather_exchange(ffn_shards[1][1:]))
    ffn_w1 = [a.reshape(F, D) for a in full1_gate + full1_rest]
    x3 = _mm(att, wo_f, "nn", F32, "attn_out", add=x2)
    (dx4, loss_rows, hf1, gg1, uu1), _ = _ffn_fwd(x3, g_ffn1, *ffn_w1, name="ffn_fwd_loss", target=target)

    def blocks(dw):
        return dw.reshape(N_DEV, dw.shape[0] // N_DEV, dw.shape[1])

    (dgg1, duu1, dwg1, dwu1, dwd1), _ = _ffn_bwd_weights(dx4, hf1, gg1, uu1, ffn_w1[2], name="ffn_bwd1")
    parts1 = [blocks(dw) for dw in (dwg1, dwu1, dwd1)]
    (dx3, dg_ffn1), stage1 = _ffn_bwd_input(dx4, x3, g_ffn1, [dgg1, duu1], ffn_w1[:2], "ffn_dx1",
                                            ex=_pair_exchange(parts1))
    both1 = _pair_add(parts1, stage1, "pair_add_ffn1", copies=2)
    flight1, chip1, land1, _ = _chip_exchange_start(both1[:3], both1[3:], dx3, "scatter_ffn1_start")
    datt = _mm(dx3, wo_f, "nt", BF16, "d_attn")
    dwo = _mm(att, dx3, "tn", BF16, "d_wo")
    (dq, dkp, dvp, dband), _ = _attn_bwd(qq, kp, vp, bias, datt)
    (dx2, dqpre, dkpre, dg_mix1, dg_kv, dgq, dgk), _ = _qkv_input_bwd(
        dq, qpre, gq_t, dkp, kpre, gk_t, dvp, wq_f, wk_f, wv_f, dx3, x2, g_mix1, g_kv)
    dwq = _mm(h1, dqpre, "tn", BF16, "d_wq")
    dwk = _mm(hkv, dkpre, "tn", BF16, "d_wk")
    dwv = _mm(hkv, dvp, "tn", BF16, "d_wv", skip_rows=PADK)
    parts_att = [blocks(dw) for dw in (dwk, dwv, dwq, dwo)]
    (dgg0, duu0, *dw0), stage_att = _ffn_bwd_weights(
        dx2, hf0, gg0, uu0, ffn_w0[2], name="ffn_bwd0", ex=_pair_exchange(parts_att))
    both_att = _pair_add(parts_att, stage_att, "pair_add_att", copies=2)
    flight_att, chip_att, land_att, _ = _chip_exchange_start(both_att[:4], both_att[4:], dx2, "scatter_att_start")
    parts0 = [blocks(dw) for dw in dw0]
    (dx1, dg_ffn0), stage0 = _ffn_bwd_input(dx2, x1, g_ffn0, [dgg0, duu0], ffn_w0[:2], "ffn_dx0",
                                            ex=_pair_exchange(parts0))
    both = _pair_add(parts0, stage0, "pair_add_ffn0", copies=2)
    flight, chip0, land0, token = _chip_exchange_start(both[:3], both[3:], dx1, "scatter_ffn0_start")
    (grad_x, dpw, db_rows, ds_rows, dg_mix0), _ = _pool_bwd(dx1, xin, diff, g_mix0 + token[0:1, 0:1], pw_f, pb_f, ps_f)
    dtp, seg = _bias_grad(dband, dpw)

    weights = dict(norm_mix_g=norm_mix_g, norm_ffn_g=norm_ffn_g, pool_w=pool_w, pool_b=pool_b,
                   pool_scale=pool_scale, kv_norm_g=kv_norm_g, w_k=w_k, w_v=w_v, k_norm_g=k_norm_g, w_q=w_q,
                   q_norm_g=q_norm_g, rel_bias=rel_bias, w_o=w_o, w_gate=w_gate, w_up=w_up, w_down=w_down)
    mom1 = dict(norm_mix_g=m_norm_mix_g, norm_ffn_g=m_norm_ffn_g, pool_w=m_pool_w, pool_b=m_pool_b,
                pool_scale=m_pool_scale, kv_norm_g=m_kv_norm_g, w_k=m_w_k, w_v=m_w_v, k_norm_g=m_k_norm_g,
                w_q=m_w_q, q_norm_g=m_q_norm_g, rel_bias=m_rel_bias, w_o=m_w_o, w_gate=m_w_gate, w_up=m_w_up,
                w_down=m_w_down)
    mom2 = dict(norm_mix_g=v_norm_mix_g, norm_ffn_g=v_norm_ffn_g, pool_w=v_pool_w, pool_b=v_pool_b,
                pool_scale=v_pool_scale, kv_norm_g=v_kv_norm_g, w_k=v_w_k, w_v=v_w_v, k_norm_g=v_k_norm_g,
                w_q=v_w_q, q_norm_g=v_q_norm_g, rel_bias=v_rel_bias, w_o=v_w_o, w_gate=v_w_gate, w_up=v_w_up,
                w_down=v_w_down)
    names = list(weights)
    grads, deltas, new_m, new_v = {}, {}, {}, {}

    def summed(arrs, tag, after):
        return [_sum_parts(a, "sum_parts_%s%d" % (tag, k), after) for k, a in enumerate(arrs)]

    def adam_flat(nm):
        shape = weights[nm].shape
        flat = lambda a: a.reshape(-1, shape[-1])
        dl, m1, m2 = _adam(flat(weights[nm]), flat(grads[nm]), flat(mom1[nm]), flat(mom2[nm]), "adam_" + nm)
        deltas[nm], new_m[nm], new_v[nm] = dl.reshape(shape), m1.reshape(shape), m2.reshape(shape)

    ffn_names = ("w_gate", "w_up", "w_down")
    def ffn_view(nm, a):
        return a if nm == "w_down" else a.transpose(0, 2, 1)

    recv1 = _chip_exchange_wait(flight1, chip1, land1, dtp, "scatter_ffn1_wait")
    sums1 = summed(recv1, "ffn1_", token)
    layer1 = {nm: _adam_layer(ffn_view(nm, weights[nm]), g, ffn_view(nm, mom1[nm]), ffn_view(nm, mom2[nm]), 1,
                              "adam1_" + nm)
              for nm, g in zip(ffn_names, sums1)}
    recv_att = _chip_exchange_wait(flight_att, chip_att, land_att, layer1["w_down"][1], "scatter_att_wait")
    g_wk, g_wv, g_wq, g_wo = summed(recv_att, "att_", token)
    grads.update(w_k=g_wk, w_v=g_wv, w_q=g_wq[None], w_o=g_wo[None])
    for nm in ("w_k", "w_v", "w_q", "w_o"):
        adam_flat(nm)

    dpw_blocks = dpw.reshape(N_GROUPS, N_DEV, POOL_SHARD, GROUP).transpose(1, 0, 2, 3)
    dpw_blocks = dpw_blocks.reshape(N_DEV, N_GROUPS * POOL_SHARD, GROUP).astype(BF16)
    recv_pool = _run_exchange(_scatter_exchange([dpw_blocks]), "scatter_pool", after=deltas["w_o"])
    misc = jnp.concatenate([dgk[0:1, 0:HEAD], dgq[0:1, 0:HEAD], loss_rows[0:1, 0:1],
                            seg[:, 0, 0].reshape(1, N_HEADS), seg[:, 0, 1].reshape(1, N_HEADS)], axis=1)
    misc = jnp.pad(misc, ((0, 0), (0, D - misc.shape[1])))
    vec_rows = jnp.concatenate([dg_mix0[0:1], dg_mix1[0:1], dg_ffn0[0:1], dg_ffn1[0:1], dg_kv[0:1],
                                db_rows[0:1], ds_rows[0:1], misc], axis=0)
    pack = jnp.concatenate([vec_rows, dtp.reshape(N_HEADS, TOEP)], axis=0)
    tot = _all_reduce_small(pack, "reduce_small", after=recv_pool[0])

    loss = tot[7, 2 * HEAD]
    seg_tot = jnp.stack([tot[7, 2 * HEAD + 1:2 * HEAD + 1 + N_HEADS],
                         tot[7, 2 * HEAD + 1 + N_HEADS:2 * HEAD + 1 + 2 * N_HEADS]], axis=1)
    me = 4 * lax.axis_index("x") + 2 * lax.axis_index("y") + lax.axis_index("c")
    g_pool_b = lax.dynamic_slice_in_dim(tot[5].reshape(N_GROUPS, GROUP), me * POOL_SHARD, POOL_SHARD, axis=1)
    grads.update(
        norm_mix_g=tot[0:2], norm_ffn_g=tot[2:4], kv_norm_g=tot[4], k_norm_g=tot[7, 0:HEAD],
        q_norm_g=tot[7, HEAD:2 * HEAD].reshape(1, HEAD),
        rel_bias=_table_grad_from_toeplitz(tot[8:8 + N_HEADS], seg_tot).reshape(1, N_HEADS, N_REL),
        pool_b=g_pool_b.reshape(1, N_GROUPS, POOL_SHARD),
        pool_scale=lax.dynamic_slice_in_dim(tot[6:7], me * D_SHARD, D_SHARD, axis=1),
        pool_w=summed(recv_pool, "pool_", None)[0].reshape(1, N_GROUPS, POOL_SHARD, GROUP))
    adam_flat("pool_w")
    small_names = [nm for nm in names if nm not in ffn_names + ("w_k", "w_v", "w_q", "w_o", "pool_w")]

    def pack_small(tree):
        cols = []
        for nm in small_names:
            flat = tree[nm].reshape(-1)
            cols.append(jnp.pad(flat, (0, -flat.shape[0] % 1024)))
        return jnp.concatenate(cols).reshape(-1, 128)

    dl, m1, m2 = _adam(pack_small(weights), pack_small(grads), pack_small(mom1), pack_small(mom2), "adam_small")

    def unpack_small(packed, out):
        flat, off = packed.reshape(-1), 0
        for nm in small_names:
            size = weights[nm].size
            out[nm] = flat[off:off + size].reshape(weights[nm].shape)
            off += size + (-size % 1024)

    unpack_small(dl, deltas)
    unpack_small(m1, new_m)
    unpack_small(m2, new_v)

    recv0 = _chip_exchange_wait(flight, chip0, land0, dl, "scatter_ffn0_wait")
    sums0 = summed(recv0, "ffn0_", None)
    for nm, g in zip(ffn_names, sums0):
        done = _adam_layer(ffn_view(nm, weights[nm]), g, ffn_view(nm, mom1[nm]), ffn_view(nm, mom2[nm]), 0,
                           "adam0_" + nm, prev=layer1[nm])
        grads[nm], deltas[nm], new_m[nm], new_v[nm] = [ffn_view(nm, a) for a in done]

    return (loss, grad_x[None], *[grads[nm] for nm in names], *[deltas[nm] for nm in names],
            *[new_m[nm] for nm in names], *[new_v[nm] for nm in names])
```

```python
import functools

import jax
import jax.numpy as jnp
from jax import lax
from jax.experimental import pallas as pl
from jax.experimental.pallas import tpu as pltpu

F32 = jnp.float32
BF16 = jnp.bfloat16
MESH_ID = pl.DeviceIdType.MESH

N_DEV = 8
S = 2048
D = 1024
F = 2816
F_SHARD = F // N_DEV
D_SHARD = D // N_DEV
N_GROUPS = 4
GROUP = D // N_GROUPS
POOL_SHARD = GROUP // N_DEV
MAX_WIN = 16
HEAD = 64
N_HEADS = D // HEAD
CHUNK = 64
LEFT = 8
QB = 4 * CHUNK
KB = QB + LEFT * CHUNK
PADK = LEFT * CHUNK
TOEP = 1024
N_REL = 257
MAX_REL = 128
EPS = 1e-6
NEG_INF = -1e30
ATTN_SCALE = HEAD ** -0.5

ADAM_LR = 0.001
ADAM_B1 = 0.9
ADAM_B2 = 0.999
ADAM_EPS = 1e-08
ADAM_WD = 0.01
ADAM_STEP = 10

VMEM_LIMIT = 52 * 1024 * 1024

ANY = pl.BlockSpec(memory_space=pl.ANY)
VMEM = pl.BlockSpec(memory_space=pltpu.VMEM)


def _call(body, **kw):
    return pl.pallas_call(body, **kw)


def _params(*sem):
    return pltpu.CompilerParams(dimension_semantics=sem, vmem_limit_bytes=VMEM_LIMIT)


def _dot(a, b, dims):
    return lax.dot_general(a, b, (dims, ((), ())), preferred_element_type=F32)


def _nn(a, b):
    return _dot(a, b, ((1,), (0,)))


def _nt(a, b):
    return _dot(a, b, ((1,), (1,)))


def _tn(a, b):
    return _dot(a, b, ((0,), (0,)))


def _rstd(x):
    return lax.rsqrt(jnp.mean(x * x, axis=-1, keepdims=True) + EPS)


def _rms_bwd(dh, x, r, g):
    gd = dh * g
    return r * gd - x * (r * r * r) * jnp.mean(gd * x, axis=-1, keepdims=True)


def _colsum8(v):
    return jnp.broadcast_to(jnp.sum(v, axis=0, keepdims=True), (8, v.shape[1]))


def _seg_sum(v):
    r = lax.broadcasted_iota(jnp.int32, (128, 128), 0) // HEAD
    c = lax.broadcasted_iota(jnp.int32, (128, 128), 1) // HEAD
    ones = jnp.where(r == c, 1.0, 0.0).astype(BF16)
    out = []
    for blk in range(v.shape[1] // 128):
        part = v[:, blk * 128:(blk + 1) * 128]
        hi = part.astype(BF16)
        rest = part - hi.astype(F32)
        mid = rest.astype(BF16)
        lo = (rest - mid.astype(F32)).astype(BF16)
        out.append(_nn(hi, ones) + _nn(mid, ones) + _nn(lo, ones))
    return jnp.concatenate(out, axis=1)


def _place():
    return lax.axis_index("x"), lax.axis_index("y"), lax.axis_index("c")


class _Exchange:
    def __init__(self, ins, out_shape, sems, start, mid, finish, early=None):
        self.ins, self.out_shape, self.sems = list(ins), list(out_shape), list(sems)
        self.start, self.mid, self.finish = start, mid, finish
        self.early = early if early is not None else (lambda ins, outs, sems: None)


def _relay_gather_exchange(shards):
    n = len(shards)

    def tools(ins, outs, sems):
        send_sems, recv_sems, local_sems = sems
        x, y, c = _place()
        me, sib = (x, y, c), (x, y, 1 - c)
        xn, yn, dg = (1 - x, y, c), (x, 1 - y, c), (1 - x, 1 - y, c)
        other = lambda dev: (dev[0], dev[1], 1 - c)

        def piece(k, ref, half):
            rows = shards[k].shape[0] // 2
            return ref if half is None else ref.at[pl.ds(half * rows, rows)]

        def held(k, dev, half):
            return piece(k, outs[k].at[4 * dev[0] + 2 * dev[1] + dev[2]], half)

        def copy(k, s, dev, half, to, own=False):
            return pltpu.make_async_remote_copy(
                src_ref=piece(k, ins[k], half) if own else held(k, dev, half), dst_ref=held(k, dev, half),
                send_sem=send_sems.at[k, s], recv_sem=recv_sems.at[k, s], device_id=to, device_id_type=MESH_ID)

        def mine(k):
            return pltpu.make_async_copy(ins[k], held(k, me, None), local_sems.at[k])

        def sent(k):
            return [copy(k, 0, me, None, sib, own=True), copy(k, 1, me, 0, xn, own=True), copy(k, 2, me, 1, yn, own=True),
                    copy(k, 3, me, 0, yn, own=True), copy(k, 4, me, 1, xn, own=True)]

        relays = [(1, xn, 0, [(5, yn), (7, sib)]), (2, yn, 1, [(6, xn), (9, sib)]), (3, yn, 0, [(10, sib)]),
                  (4, xn, 1, [(8, sib)]), (6, dg, 1, [(11, sib)]), (5, dg, 0, [(12, sib)])]
        from_sibling = [(0, other(me), None), (7, other(xn), 0), (8, other(xn), 1), (9, other(yn), 1),
                        (10, other(yn), 0), (11, other(dg), 1), (12, other(dg), 0)]
        return mine, sent, copy, relays, from_sibling, me

    def start(ins, outs, sems):
        mine, sent, _, _, _, _ = tools(ins, outs, sems)
        for k in range(n):
            mine(k).start()
        for k in range(n):
            for cp in sent(k):
                cp.start()

    def pass_on(ins, outs, sems, which):
        _, _, copy, relays, _, me = tools(ins, outs, sems)
        for k in range(n):
            for s, dev, half, onward in relays[which]:
                copy(k, s, dev, half, me).wait_recv()
                for s2, to in onward:
                    copy(k, s2, dev, half, to).start()

    def early(ins, outs, sems):
        pass_on(ins, outs, sems, slice(0, 2))

    def mid(ins, outs, sems):
        pass_on(ins, outs, sems, slice(2, None))

    def finish(ins, outs, sems):
        mine, sent, copy, relays, from_sibling, me = tools(ins, outs, sems)
        for k in range(n):
            for s, dev, half in from_sibling:
                copy(k, s, dev, half, me).wait_recv()
        for k in range(n):
            for cp in sent(k):
                cp.wait_send()
            for _, dev, half, onward in relays:
                for s2, to in onward:
                    copy(k, s2, dev, half, to).wait_send()
            mine(k).wait()

    return _Exchange(
        shards, [jax.ShapeDtypeStruct((N_DEV,) + a.shape, a.dtype) for a in shards],
        [pltpu.SemaphoreType.DMA((n, 13)), pltpu.SemaphoreType.DMA((n, 13)), pltpu.SemaphoreType.DMA((n,))],
        start, mid, finish, early)


def _peer(x, y, c, m):
    px = 1 - x if m & 4 else x
    py = 1 - y if m & 2 else y
    pc = 1 - c if m & 1 else c
    return px, py, pc


def _scatter_exchange(parts):
    n = len(parts)

    def tools(ins, outs, sems):
        send_sems, recv_sems, local_sems = sems
        x, y, c = _place()
        me = 4 * x + 2 * y + c
        def mine():
            return [pltpu.make_async_copy(ins[k].at[me], outs[k].at[me], local_sems.at[k]) for k in range(n)]

        def remote(dst_is_mine):
            out = []
            for m in range(1, N_DEV):
                px, py, pc = _peer(x, y, c, m)
                peer = 4 * px + 2 * py + pc
                for k in range(n):
                    out.append(pltpu.make_async_remote_copy(
                        src_ref=ins[k].at[peer], dst_ref=outs[k].at[me if dst_is_mine else peer],
                        send_sem=send_sems.at[k, m - 1], recv_sem=recv_sems.at[k, m - 1],
                        device_id=(px, py, pc), device_id_type=MESH_ID))
            return out

        return mine, remote

    def start(ins, outs, sems):
        mine, remote = tools(ins, outs, sems)
        for cp in mine() + remote(True):
            cp.start()

    def mid(ins, outs, sems):
        pass

    def finish(ins, outs, sems):
        mine, remote = tools(ins, outs, sems)
        for cp in remote(False):
            cp.wait_recv()
        for cp in remote(True):
            cp.wait_send()
        for cp in mine():
            cp.wait()

    return _Exchange(
        parts, [jax.ShapeDtypeStruct(a.shape, a.dtype) for a in parts],
        [pltpu.SemaphoreType.DMA((n, 7)), pltpu.SemaphoreType.DMA((n, 7)), pltpu.SemaphoreType.DMA((n,))],
        start, mid, finish)


N_CHIPS = N_DEV // 2


def _pair_exchange(parts):
    n = len(parts)

    def copies(ins, outs, sems):
        send_sems, recv_sems = sems
        x, y, c = _place()
        return [pltpu.make_async_remote_copy(
            src_ref=ins[k].at[2 * q + 1 - c], dst_ref=outs[k].at[q], send_sem=send_sems.at[k, q],
            recv_sem=recv_sems.at[k, q], device_id=(x, y, 1 - c), device_id_type=MESH_ID)
            for k in range(n) for q in range(N_CHIPS)]

    def start(ins, outs, sems):
        for cp in copies(ins, outs, sems):
            cp.start()

    def mid(ins, outs, sems):
        pass

    def finish(ins, outs, sems):
        for cp in copies(ins, outs, sems):
            cp.wait_recv()
        for cp in copies(ins, outs, sems):
            cp.wait_send()

    return _Exchange(
        parts, [jax.ShapeDtypeStruct((N_CHIPS,) + a.shape[1:], a.dtype) for a in parts],
        [pltpu.SemaphoreType.DMA((n, N_CHIPS)), pltpu.SemaphoreType.DMA((n, N_CHIPS))], start, mid, finish)


def _pair_add(parts, stage, name, copies=1):
    n = len(parts)
    core = lax.axis_index("c").reshape(1)

    def body(core_ref, *refs):
        for k in range(n):
            mine, theirs = refs[k], refs[n + k]
            total = (mine[0, 0].astype(F32) + theirs[0].astype(F32)).astype(BF16)
            for rep in range(copies):
                refs[(2 + rep) * n + k][0] = total

    in_specs, out_specs = [], []
    for a in parts:
        _, r, cdim = a.shape
        in_specs.append(pl.BlockSpec((1, 1, r // 2, cdim), lambda q, i, core_ref: (q, core_ref[0], i, 0)))
    for a in parts:
        _, r, cdim = a.shape
        in_specs.append(pl.BlockSpec((1, r // 2, cdim), lambda q, i, core_ref: (q, i, 0)))
        out_specs.append(pl.BlockSpec((1, r // 2, cdim), lambda q, i, core_ref: (q, i, 0)))
    return list(_call(
        body, name=name,
        grid_spec=pltpu.PrefetchScalarGridSpec(num_scalar_prefetch=1, grid=(N_CHIPS, 2), in_specs=in_specs,
                                               out_specs=out_specs * copies),
        out_shape=[jax.ShapeDtypeStruct(s.shape, BF16) for s in stage] * copies,
        compiler_params=_params("arbitrary", "arbitrary"),
    )(core, *[a.reshape((N_CHIPS, 2) + a.shape[1:]) for a in parts], *stage))


HBM = pl.BlockSpec(memory_space=pltpu.HBM)
SEMAPHORES = pl.BlockSpec(memory_space=pltpu.SEMAPHORE)


def _chip_copies(srcs, lands, send_sems, recv_sems, mine_is_dst):
    x, y, c = _place()
    me = 2 * x + y
    out = []
    for m in range(1, N_CHIPS):
        px, py, _ = _peer(x, y, c, 2 * m)
        peer = 2 * px + py
        for k in range(len(srcs)):
            pair = k * (N_CHIPS - 1) + m - 1
            out.append(pltpu.make_async_remote_copy(
                src_ref=srcs[k].at[peer], dst_ref=lands[k].at[me if mine_is_dst else peer],
                send_sem=send_sems[pair], recv_sem=recv_sems[pair],
                device_id=(px, py, c), device_id_type=MESH_ID))
    return out


def _chip_exchange_start(chip_parts, lands, after, name):
    n = len(chip_parts)
    pairs = n * (N_CHIPS - 1)

    def body(*refs):
        srcs, zones = refs[:n], refs[n:2 * n]
        sems = refs[2 * n + 1:2 * n + 1 + 2 * pairs]
        token = refs[-1]
        for cp in _chip_copies(srcs, zones, sems[:pairs], sems[pairs:], True):
            cp.start()
        token[...] = jnp.zeros(token.shape, F32)

    thru = [pltpu.HBM(a.shape, a.dtype) for a in chip_parts + lands]
    hbm = [pltpu.with_memory_space_constraint(a, pltpu.HBM) for a in chip_parts + lands]
    res = _call(
        body, name=name,
        out_shape=[pltpu.SemaphoreType.DMA(())] * (2 * pairs) + thru + [jax.ShapeDtypeStruct((8, 128), F32)],
        in_specs=[HBM] * (2 * n) + [ANY], out_specs=[SEMAPHORES] * (2 * pairs) + [HBM] * (2 * n) + [VMEM],
        input_output_aliases={i: 2 * pairs + i for i in range(2 * n)},
        compiler_params=pltpu.CompilerParams(has_side_effects=pltpu.SideEffectType.DATAFLOW_SIDE_EFFECTING),
    )(*hbm, after)
    sems, rest = list(res[:2 * pairs]), res[2 * pairs:]
    return sems, list(rest[:n]), list(rest[n:2 * n]), rest[-1]


def _chip_exchange_wait(sems, chip_parts, lands, after, name):
    n = len(chip_parts)
    pairs = n * (N_CHIPS - 1)

    def body(*refs):
        srcs, zones = refs[:n], refs[n:2 * n]
        sem_refs = refs[2 * n:2 * n + 2 * pairs]
        for cp in _chip_copies(srcs, zones, sem_refs[:pairs], sem_refs[pairs:], False):
            cp.wait_send()
            cp.wait_recv()

    thru = [pltpu.HBM(a.shape, a.dtype) for a in chip_parts + lands]
    res = _call(
        body, name=name, out_shape=thru,
        in_specs=[HBM] * (2 * n) + [SEMAPHORES] * (2 * pairs) + [ANY], out_specs=[HBM] * (2 * n),
        input_output_aliases={i: i for i in range(2 * n)},
        compiler_params=pltpu.CompilerParams(has_side_effects=pltpu.SideEffectType.DATAFLOW_SIDE_EFFECTING),
    )(*chip_parts, *lands, *sems, after)
    return list(res[n:])


def _run_exchange(ex, name, after=None):
    n_in, n_out = len(ex.ins), len(ex.out_shape)
    order = [] if after is None else [after]

    def body(*refs):
        ins, outs = refs[:n_in], refs[n_in + len(order):n_in + len(order) + n_out]
        sems = refs[n_in + len(order) + n_out:]
        ex.start(ins, outs, sems)
        ex.early(ins, outs, sems)
        ex.mid(ins, outs, sems)
        ex.finish(ins, outs, sems)

    return list(_call(body, name=name, out_shape=ex.out_shape, in_specs=[ANY] * (n_in + len(order)),
                      out_specs=[ANY] * n_out, scratch_shapes=ex.sems)(*ex.ins, *order))


def _call_hosting(body, ex, phases, *, in_specs, out_specs, out_shape, scratch_shapes, args, **kw):
    n_in, n_out, n_scr = len(in_specs), len(out_specs), len(scratch_shapes)
    if ex is None:
        res = _call(body, in_specs=in_specs, out_specs=out_specs, out_shape=out_shape,
                    scratch_shapes=scratch_shapes, **kw)(*args)
        return list(res), []
    n_xin, n_xout = len(ex.ins), len(ex.out_shape)

    def hosting(*refs):
        a, b = n_in, n_in + n_xin
        c, d = b + n_out, b + n_out + n_xout
        ins, xins, outs, xouts = refs[:a], refs[a:b], refs[b:c], refs[c:d]
        scr, sems = refs[d:d + n_scr], refs[d + n_scr:]
        first, early, mid, last = phases()

        @pl.when(first)
        def _():
            ex.start(xins, xouts, sems)

        body(*ins, *outs, *scr)

        @pl.when(early)
        def _():
            ex.early(xins, xouts, sems)

        @pl.when(mid)
        def _():
            ex.mid(xins, xouts, sems)

        @pl.when(last)
        def _():
            ex.finish(xins, xouts, sems)

    res = _call(hosting, in_specs=list(in_specs) + [ANY] * n_xin, out_specs=list(out_specs) + [ANY] * n_xout,
                out_shape=list(out_shape) + ex.out_shape, scratch_shapes=list(scratch_shapes) + ex.sems,
                **kw)(*args, *ex.ins)
    return list(res[:n_out]), list(res[n_out:])


def _grid_phases(dims, mid_fraction=0.8, early_fraction=0.4):
    total = 1
    for d in dims:
        total *= d
    mid = min(max(int(total * mid_fraction), 1), total - 1)
    early = min(int(total * early_fraction), mid)

    def phases():
        step = pl.program_id(0)
        for axis in range(1, len(dims)):
            step = step * dims[axis] + pl.program_id(axis)
        return step == 0, step == early, step == mid, step == total - 1
    return phases


def _all_reduce_small(pack, name, after):
    rows = pack.shape[0]

    def body(in_ref, _, out_ref, recv, send_sems, recv_sems):
        x, y, c = _place()
        me = 4 * x + 2 * y + c
        recv[me] = in_ref[...]
        sent = []
        for m in range(1, N_DEV):
            px, py, pc = _peer(x, y, c, m)
            cp = pltpu.make_async_remote_copy(
                src_ref=in_ref, dst_ref=recv.at[me], send_sem=send_sems.at[m - 1], recv_sem=recv_sems.at[m - 1],
                device_id=(px, py, pc), device_id_type=MESH_ID)
            cp.start()
            sent.append(cp)
        for m in range(1, N_DEV):
            px, py, pc = _peer(x, y, c, m)
            peer = 4 * px + 2 * py + pc
            pltpu.make_async_remote_copy(
                src_ref=in_ref, dst_ref=recv.at[peer], send_sem=send_sems.at[m - 1], recv_sem=recv_sems.at[m - 1],
                device_id=(px, py, pc), device_id_type=MESH_ID).wait_recv()
        acc = recv[0]
        for d in range(1, N_DEV):
            acc = acc + recv[d]
        out_ref[...] = acc
        for cp in sent:
            cp.wait_send()

    return _call(
        body, name=name, out_shape=jax.ShapeDtypeStruct(pack.shape, F32), in_specs=[VMEM, ANY], out_specs=VMEM,
        scratch_shapes=[pltpu.VMEM((N_DEV, rows, pack.shape[1]), F32), pltpu.SemaphoreType.DMA((7,)),
                        pltpu.SemaphoreType.DMA((7,))],
        compiler_params=pltpu.CompilerParams(vmem_limit_bytes=VMEM_LIMIT),
    )(pack, after)


POOL_TS = 256


def _pool_counts(first_row, rows, win):
    t = first_row + lax.broadcasted_iota(jnp.int32, (rows, 1), 0)
    return jnp.minimum(t + 1, win).astype(F32)


def _pool_fwd(x, g, w, b, scale):
    nt = S // POOL_TS

    def body(x_ref, g_ref, w_ref, b_ref, s_ref, out_ref, diff_ref, ext):
        i = pl.program_id(0)

        @pl.when(i == 0)
        def _():
            ext[0:MAX_WIN, :] = jnp.zeros((MAX_WIN, D), F32)

        @pl.when(i > 0)
        def _():
            ext[0:MAX_WIN, :] = ext[POOL_TS:POOL_TS + MAX_WIN, :]

        xv = x_ref[...]
        h = xv * _rstd(xv) * g_ref[...]
        ext[MAX_WIN:, :] = h
        for gi in range(N_GROUPS):
            win = 2 << gi
            cols = slice(gi * GROUP, (gi + 1) * GROUP)
            sm = ext[:, cols]
            k = 1
            while k < win:
                sm = sm + pltpu.roll(sm, k, axis=0)
                k *= 2
            pooled = sm[MAX_WIN:, :] / _pool_counts(i * POOL_TS, POOL_TS, win)
            diff = (pooled - h[:, cols]).astype(BF16)
            yv = (_nn(diff, w_ref[gi]) + b_ref[:, cols]) * s_ref[:, cols]
            out_ref[:, cols] = xv[:, cols] + yv
            diff_ref[:, cols] = diff

    row = pl.BlockSpec((1, D), lambda i: (0, 0))
    tile = pl.BlockSpec((POOL_TS, D), lambda i: (i, 0))
    return _call(
        body, name="pool_fwd", grid=(nt,),
        in_specs=[tile, row, pl.BlockSpec((N_GROUPS, GROUP, GROUP), lambda i: (0, 0, 0)), row, row],
        out_specs=[tile, tile],
        out_shape=[jax.ShapeDtypeStruct((S, D), F32), jax.ShapeDtypeStruct((S, D), BF16)],
        scratch_shapes=[pltpu.VMEM((POOL_TS + MAX_WIN, D), F32)],
        compiler_params=_params("arbitrary"),
    )(x, g, w, b, scale)


def _pool_bwd(dy, x, diff, g, w, b, scale, ex=None):
    nt = S // POOL_TS

    def body(dy_ref, x_ref, diff_ref, g_ref, w_ref, b_ref, s_ref, gx_ref, dw_ref, db_ref, ds_ref, dg_ref, ext, dh):
        i = pl.program_id(0)
        first_row = (nt - 1 - i) * POOL_TS

        @pl.when(i == 0)
        def _():
            ext[POOL_TS:, :] = jnp.zeros((MAX_WIN, D), F32)
            dw_ref[...] = jnp.zeros(dw_ref.shape, F32)
            db_ref[...] = jnp.zeros(db_ref.shape, F32)
            ds_ref[...] = jnp.zeros(ds_ref.shape, F32)
            dg_ref[...] = jnp.zeros(dg_ref.shape, F32)

        @pl.when(i > 0)
        def _():
            ext[POOL_TS:, :] = ext[0:MAX_WIN, :]

        dyv = dy_ref[...]
        for gi in range(N_GROUPS):
            win = 2 << gi
            cols = slice(gi * GROUP, (gi + 1) * GROUP)
            dfb = diff_ref[:, cols]
            z = _nn(dfb, w_ref[gi]) + b_ref[:, cols]
            dyg = dyv[:, cols]
            ds_ref[:, cols] += _colsum8(dyg * z)
            dz = dyg * s_ref[:, cols]
            db_ref[:, cols] += _colsum8(dz)
            dzb = dz.astype(BF16)
            dw_ref[gi] += _tn(dfb, dzb)
            ddiff = _nt(dzb, w_ref[gi])
            ext[0:POOL_TS, cols] = ddiff / _pool_counts(first_row, POOL_TS, win)
            sm = ext[:, cols]
            k = 1
            while k < win:
                sm = sm + pltpu.roll(sm, POOL_TS + MAX_WIN - k, axis=0)
                k *= 2
            dh[:, cols] = sm[0:POOL_TS, :] - ddiff
        xv = x_ref[...]
        r = _rstd(xv)
        gv = g_ref[...]
        dhv = dh[...]
        dg_ref[...] += _colsum8(dhv * xv * r)
        gx_ref[...] = dyv + _rms_bwd(dhv, xv, r, gv)

    row = pl.BlockSpec((1, D), lambda i: (0, 0))
    tile = pl.BlockSpec((POOL_TS, D), lambda i: (nt - 1 - i, 0))
    acc = pl.BlockSpec((8, D), lambda i: (0, 0))
    wspec = pl.BlockSpec((N_GROUPS, GROUP, GROUP), lambda i: (0, 0, 0))
    return _call_hosting(
        body, ex, _grid_phases((nt,)), name="pool_bwd", grid=(nt,),
        in_specs=[tile, tile, tile, row, wspec, row, row],
        out_specs=[tile, wspec, acc, acc, acc],
        out_shape=[jax.ShapeDtypeStruct((S, D), F32), jax.ShapeDtypeStruct((N_GROUPS, GROUP, GROUP), F32),
                   jax.ShapeDtypeStruct((8, D), F32), jax.ShapeDtypeStruct((8, D), F32),
                   jax.ShapeDtypeStruct((8, D), F32)],
        scratch_shapes=[pltpu.VMEM((POOL_TS + MAX_WIN, D), F32), pltpu.VMEM((POOL_TS, D), F32)],
        args=(dy, x, diff, g, w, b, scale), compiler_params=_params("arbitrary"),
    )


FFN_TS = min(S, 1024)
FFN_TF = 256


def _ffn_fwd(x, g, wg_t, wu_t, wd, name, target=None, ex=None):
    width = wd.shape[0]
    ni, nj = S // FFN_TS, width // FFN_TF
    with_loss = target is not None
    extra = [target] if with_loss else []

    def body(*refs):
        x_ref, g_ref, wg_ref, wu_ref, wd_ref = refs[:5]
        if with_loss:
            t_ref, out_ref, loss_ref, h_ref, gg_ref, uu_ref, hs, acc, kept = refs[5:]
        else:
            out_ref, h_ref, gg_ref, uu_ref, hs, acc, kept = refs[5:]
        i, j = pl.program_id(0), pl.program_id(1)

        @pl.when(j == 0)
        def _():
            xv = x_ref[...]
            hb = (xv * _rstd(xv) * g_ref[...]).astype(BF16)
            hs[...] = hb
            h_ref[...] = hb
            acc[...] = jnp.zeros(acc.shape, F32)
            kept[1] = jnp.zeros(kept.shape[1:], BF16)

        prev, cur = (j + 1) % 2, j % 2
        acc[...] += _nn(kept[prev], wd_ref[...])
        hb = hs[...]
        gg = _nt(hb, wg_ref[...])
        uu = _nt(hb, wu_ref[...])
        gg_ref[...] = gg
        uu_ref[...] = uu
        kept[cur] = (gg * jax.nn.sigmoid(gg) * uu).astype(BF16)

        @pl.when(j == nj)
        def _():
            yv = x_ref[...] + acc[...]
            if with_loss:
                err = yv - t_ref[...]
                out_ref[...] = err * (1.0 / D)
                part = jnp.sum(err * err) * (0.5 / D)

                @pl.when(i == 0)
                def _():
                    loss_ref[...] = jnp.zeros(loss_ref.shape, F32)

                loss_ref[...] += jnp.broadcast_to(part, loss_ref.shape)
            else:
                out_ref[...] = yv

    xt = pl.BlockSpec((FFN_TS, D), lambda i, j: (i, 0))
    row = pl.BlockSpec((1, D), lambda i, j: (0, 0))
    wt = pl.BlockSpec((FFN_TF, D), lambda i, j: (jnp.minimum(j, nj - 1), 0))
    wt_kept = pl.BlockSpec((FFN_TF, D), lambda i, j: (jnp.maximum(j - 1, 0), 0))
    gt = pl.BlockSpec((FFN_TS, FFN_TF), lambda i, j: (i, jnp.minimum(j, nj - 1)))
    in_specs = [xt, row, wt, wt, wt_kept] + [xt] * len(extra)
    out_specs = [xt] + ([pl.BlockSpec((8, 128), lambda i, j: (0, 0))] if with_loss else []) + [xt, gt, gt]
    out_shape = ([jax.ShapeDtypeStruct((S, D), F32)] + ([jax.ShapeDtypeStruct((8, 128), F32)] if with_loss else [])
                 + [jax.ShapeDtypeStruct((S, D), BF16), jax.ShapeDtypeStruct((S, width), F32),
                    jax.ShapeDtypeStruct((S, width), F32)])
    args = (x, g, wg_t, wu_t, wd, *extra)
    return _call_hosting(
        body, ex, _grid_phases((ni, nj + 1), 0.8), name=name, grid=(ni, nj + 1),
        in_specs=in_specs, out_specs=out_specs, out_shape=out_shape,
        scratch_shapes=[pltpu.VMEM((FFN_TS, D), BF16), pltpu.VMEM((FFN_TS, D), F32),
                        pltpu.VMEM((2, FFN_TS, FFN_TF), BF16)], args=args,
        compiler_params=_params("arbitrary", "arbitrary"),
    )


def _ffn_bwd_weights(dout, h, gg, uu, wd, name, ex=None):
    width = wd.shape[0]
    nj = width // FFN_TF

    def body(do_ref, h_ref, gg_ref, uu_ref, wd_ref, dg_ref, du_ref, dwg_ref, dwu_ref, dwd_ref, dob, kept):
        j = pl.program_id(0)

        @pl.when(j == 0)
        def _():
            dob[...] = do_ref[...].astype(BF16)
            kept[1] = jnp.zeros(kept.shape[1:], BF16)

        prev, cur = (j + 1) % 2, j % 2
        hb, dov = h_ref[...], dob[...]
        dwg_ref[...] = _tn(kept[prev, 0], hb).astype(BF16)
        dwu_ref[...] = _tn(kept[prev, 1], hb).astype(BF16)
        dwd_ref[...] = _tn(kept[prev, 2], dov).astype(BF16)
        gv, uv = gg_ref[...], uu_ref[...]
        da = _nt(dov, wd_ref[...])
        sg = jax.nn.sigmoid(gv)
        sl = gv * sg
        dub = (da * sl).astype(BF16)
        dgb = (da * uv * (sg * (1.0 + gv * (1.0 - sg)))).astype(BF16)
        dg_ref[...] = dgb
        du_ref[...] = dub
        kept[cur, 0] = dgb
        kept[cur, 1] = dub
        kept[cur, 2] = (sl * uv).astype(BF16)

    once = pl.Buffered(1)
    whole = lambda: pl.BlockSpec((S, D), lambda j, _: (0, 0), pipeline_mode=once)
    this = lambda j: jnp.minimum(j, nj - 1)
    last = lambda j: jnp.maximum(j - 1, 0)
    wt_in = pl.BlockSpec((FFN_TF, D), lambda j, _: (this(j), 0))
    wt_out = pl.BlockSpec((FFN_TF, D), lambda j, _: (last(j), 0))
    gt = pl.BlockSpec((S, FFN_TF), lambda j, _: (0, this(j)))
    return _call_hosting(
        body, ex, _grid_phases((nj + 1, 1)), name=name, grid=(nj + 1, 1),
        in_specs=[whole(), whole(), gt, gt, wt_in], out_specs=[gt, gt, wt_out, wt_out, wt_out],
        out_shape=[jax.ShapeDtypeStruct((S, width), BF16)] * 2 + [jax.ShapeDtypeStruct((width, D), BF16)] * 3,
        scratch_shapes=[pltpu.VMEM((S, D), BF16), pltpu.VMEM((2, 3, S, FFN_TF), BF16)], args=(dout, h, gg, uu, wd),
        compiler_params=_params("arbitrary", "arbitrary"),
    )


BWD_TS = 512


def _ffn_bwd_input(dres, x, g, grads, weights, name, ex=None):
    nt = S // BWD_TS
    n = len(grads)

    def body(*refs):
        dres_ref, x_ref, g_ref = refs[:3]
        grad_refs, w_refs = refs[3:3 + n], refs[3 + n:3 + 2 * n]
        dx_ref, dgam_ref = refs[3 + 2 * n:]
        dh = _nn(grad_refs[0][...], w_refs[0][...])
        for k in range(1, n):
            dh = dh + _nn(grad_refs[k][...], w_refs[k][...])
        xv = x_ref[...]
        r = _rstd(xv)
        dx_ref[...] = dres_ref[...] + _rms_bwd(dh, xv, r, g_ref[...])

        @pl.when(pl.program_id(0) == 0)
        def _():
            dgam_ref[...] = jnp.zeros(dgam_ref.shape, F32)

        dgam_ref[...] += _colsum8(dh * xv * r)

    tile = pl.BlockSpec((BWD_TS, D), lambda i: (i, 0))
    ftiles = [pl.BlockSpec((BWD_TS, a.shape[1]), lambda i: (i, 0)) for a in grads]
    wspecs = [pl.BlockSpec(w.shape, lambda i: (0, 0), pipeline_mode=pl.Buffered(1)) for w in weights]
    return _call_hosting(
        body, ex, _grid_phases((nt,)), name=name, grid=(nt,),
        in_specs=[tile, tile, pl.BlockSpec((1, D), lambda i: (0, 0))] + ftiles + wspecs,
        out_specs=[tile, pl.BlockSpec((8, D), lambda i: (0, 0))],
        out_shape=[jax.ShapeDtypeStruct((S, D), F32), jax.ShapeDtypeStruct((8, D), F32)],
        scratch_shapes=[], args=(dres, x, g, *grads, *weights), compiler_params=_params("arbitrary"),
    )


def _mm(a, b, mode, out_dtype, name, add=None, skip_rows=0):
    if mode == "nn":
        (m, kd), n = (a.shape[0] - skip_rows, a.shape[1]), b.shape[1]
    elif mode == "nt":
        (m, kd), n = (a.shape[0] - skip_rows, a.shape[1]), b.shape[0]
    else:
        (kd, m), n = a.shape, b.shape[1]
    tm, tn, tk = min(m, 1024), min(n, 1024), min(kd, 1024)
    if skip_rows and mode == "tn":
        tk = min(tk, skip_rows)
    elif skip_rows:
        tm = min(tm, skip_rows)
    skip_a = skip_rows // tm if mode != "tn" else 0
    skip_b = skip_rows // tk if mode == "tn" else 0
    assert skip_rows == skip_a * tm + skip_b * tk
    nk = kd // tk
    dot = {"nn": _nn, "nt": _nt, "tn": _tn}[mode]

    def body(*refs):
        if add is None:
            a_ref, b_ref, o_ref, acc = refs
        else:
            a_ref, b_ref, add_ref, o_ref, acc = refs
        k = pl.program_id(2)

        @pl.when(k == 0)
        def _():
            acc[...] = jnp.zeros(acc.shape, F32)

        acc[...] += dot(a_ref[...].astype(BF16), b_ref[...].astype(BF16))

        @pl.when(k == nk - 1)
        def _():
            res = acc[...]
            if add is not None:
                res = res + add_ref[...]
            o_ref[...] = res.astype(out_dtype)

    if mode == "tn":
        a_spec = pl.BlockSpec((tk, tm), lambda i, j, k: (k, i))
        b_spec = pl.BlockSpec((tk, tn), lambda i, j, k: (k + skip_b, j))
    else:
        a_spec = pl.BlockSpec((tm, tk), lambda i, j, k: (i + skip_a, k))
        b_spec = (pl.BlockSpec((tk, tn), lambda i, j, k: (k, j)) if mode == "nn"
                  else pl.BlockSpec((tn, tk), lambda i, j, k: (j, k)))
    o_spec = pl.BlockSpec((tm, tn), lambda i, j, k: (i, j))
    in_specs = [a_spec, b_spec] + ([o_spec] if add is not None else [])
    args = (a, b) + ((add,) if add is not None else ())
    return _call(
        body, name=name, grid=(m // tm, n // tn, nk), in_specs=in_specs, out_specs=o_spec,
        out_shape=jax.ShapeDtypeStruct((m, n), out_dtype), scratch_shapes=[pltpu.VMEM((tm, tn), F32)],
        compiler_params=_params("parallel", "parallel", "arbitrary"),
    )(*args)


PROJ_TS = 256


def _kvq_proj(x, g_kv, g_mix, wk, wv, wq, gk, gq, ex=None):
    lead = PADK // PROJ_TS

    def body(x_ref, gkv_ref, gmix_ref, wk_ref, wv_ref, wq_ref, gk_ref, gq_ref,
             hkv_ref, h1_ref, kpre_ref, qpre_ref, k_ref, v_ref, q_ref):
        i = pl.program_id(0)

        @pl.when(i < lead)
        def _():
            k_ref[...] = jnp.zeros(k_ref.shape, BF16)
            v_ref[...] = jnp.zeros(v_ref.shape, BF16)

        @pl.when(i >= lead)
        def _():
            xv = x_ref[...]
            xr = xv * _rstd(xv)
            hkv = (xr * gkv_ref[...]).astype(BF16)
            h1 = (xr * gmix_ref[...]).astype(BF16)
            hkv_ref[...] = hkv
            h1_ref[...] = h1
            kpre = _nn(hkv, wk_ref[...])
            qpre = _nn(h1, wq_ref[...])
            kpre_ref[...] = kpre
            qpre_ref[...] = qpre
            v_ref[...] = _nn(hkv, wv_ref[...]).astype(BF16)
            rk = lax.rsqrt(_seg_sum(kpre * kpre) * (1.0 / HEAD) + EPS)
            k_ref[...] = (kpre * rk * gk_ref[...]).astype(BF16)
            rq = lax.rsqrt(_seg_sum(qpre * qpre) * (1.0 / HEAD) + EPS)
            q_ref[...] = (qpre * rq * gq_ref[...]).astype(BF16)

    tile = pl.BlockSpec((PROJ_TS, D), lambda i: (jnp.maximum(i - lead, 0), 0))
    padded = pl.BlockSpec((PROJ_TS, D), lambda i: (i, 0))
    row = pl.BlockSpec((1, D), lambda i: (0, 0))
    wspec = pl.BlockSpec((D, D), lambda i: (0, 0))
    bf = jax.ShapeDtypeStruct((S, D), BF16)
    ff = jax.ShapeDtypeStruct((S, D), F32)
    bp = jax.ShapeDtypeStruct((PADK + S, D), BF16)
    return _call_hosting(
        body, ex, _grid_phases((lead + S // PROJ_TS,), 0.85), name="kvq_proj", grid=(lead + S // PROJ_TS,),
        in_specs=[tile, row, row, wspec, wspec, wspec, row, row],
        out_specs=[tile, tile, tile, tile, padded, padded, tile], out_shape=[bf, bf, ff, ff, bp, bp, bf],
        scratch_shapes=[], args=(x, g_kv, g_mix, wk, wv, wq, gk, gq), compiler_params=_params("arbitrary"),
    )


def _qkv_input_bwd(dq, qpre, gq, dkp, kpre, gk, dvp, wq, wk, wv, dres, x, g_mix, g_kv, ex=None):
    nt = S // PROJ_TS
    lead = PADK // PROJ_TS

    def head_bwd(dov, pv, hgv):
        r = lax.rsqrt(_seg_sum(pv * pv) * (1.0 / HEAD) + EPS)
        gd = dov * hgv
        dpre = r * gd - pv * (r * r * r) * (_seg_sum(gd * pv) * (1.0 / HEAD))
        return dpre.astype(BF16), _colsum8(dov * pv * r)

    def fold_heads(full):
        fold = full[:, 0:128]
        for blk in range(1, D // 128):
            fold = fold + full[:, blk * 128:(blk + 1) * 128]
        return fold + pltpu.roll(fold, HEAD, axis=1)

    def body(dq_ref, qpre_ref, gq_ref, dk_ref, kpre_ref, gk_ref, dv_ref, wq_ref, wk_ref, wv_ref, dres_ref, x_ref,
             gmix_ref, gkv_ref, dx_ref, dqpre_ref, dkpre_ref, dgmix_ref, dgkv_ref, dgq_ref, dgk_ref, accq, acck):
        i = pl.program_id(0)

        @pl.when(i == 0)
        def _():
            accq[...] = jnp.zeros(accq.shape, F32)
            acck[...] = jnp.zeros(acck.shape, F32)
            dgmix_ref[...] = jnp.zeros(dgmix_ref.shape, F32)
            dgkv_ref[...] = jnp.zeros(dgkv_ref.shape, F32)

        dqb, cq = head_bwd(dq_ref[...], qpre_ref[...], gq_ref[...])
        dkb, ck = head_bwd(dk_ref[...], kpre_ref[...], gk_ref[...])
        dqpre_ref[...] = dqb
        dkpre_ref[...] = dkb
        accq[...] += cq
        acck[...] += ck
        dh1 = _nt(dqb, wq_ref[...])
        dhkv = _nt(dkb, wk_ref[...]) + _nt(dv_ref[...].astype(BF16), wv_ref[...])
        xv = x_ref[...]
        r = _rstd(xv)
        dx_ref[...] = dres_ref[...] + _rms_bwd(dh1, xv, r, gmix_ref[...]) + _rms_bwd(dhkv, xv, r, gkv_ref[...])
        dgmix_ref[...] += _colsum8(dh1 * xv * r)
        dgkv_ref[...] += _colsum8(dhkv * xv * r)

        @pl.when(i == nt - 1)
        def _():
            dgq_ref[...] = fold_heads(accq[...])
            dgk_ref[...] = fold_heads(acck[...])

    tile = pl.BlockSpec((PROJ_TS, D), lambda i: (i, 0))
    behind = pl.BlockSpec((PROJ_TS, D), lambda i: (i + lead, 0))
    row = pl.BlockSpec((1, D), lambda i: (0, 0))
    wspec = lambda: pl.BlockSpec((D, D), lambda i: (0, 0), pipeline_mode=pl.Buffered(1))
    acc = pl.BlockSpec((8, D), lambda i: (0, 0))
    small = pl.BlockSpec((8, 128), lambda i: (0, 0))
    bf = jax.ShapeDtypeStruct((S, D), BF16)
    return _call_hosting(
        body, ex, _grid_phases((nt,)), name="qkv_input_bwd", grid=(nt,),
        in_specs=[tile, tile, row, behind, tile, row, behind, wspec(), wspec(), wspec(), tile, tile, row, row],
        out_specs=[tile, tile, tile, acc, acc, small, small],
        out_shape=[jax.ShapeDtypeStruct((S, D), F32), bf, bf, jax.ShapeDtypeStruct((8, D), F32),
                   jax.ShapeDtypeStruct((8, D), F32), jax.ShapeDtypeStruct((8, 128), F32),
                   jax.ShapeDtypeStruct((8, 128), F32)],
        scratch_shapes=[pltpu.VMEM((8, D), F32), pltpu.VMEM((8, D), F32)],
        args=(dq, qpre, gq, dkp, kpre, gk, dvp, wq, wk, wv, dres, x, g_mix, g_kv),
        compiler_params=_params("arbitrary"),
    )


def _toeplitz_from_table(table):
    far = jnp.broadcast_to(table[:, N_REL - 1:], (N_HEADS, PADK - MAX_REL + 1))
    near = table[:, N_REL - 2::-1]
    past = jnp.broadcast_to(table[:, 0:1], (N_HEADS, MAX_REL))
    wrap = jnp.broadcast_to(table[:, N_REL - 1:], (N_HEADS, TOEP - (PADK + 2 * MAX_REL + 1)))
    return jnp.concatenate([far, near, past, wrap], axis=1).reshape(N_HEADS, 1, TOEP)


def _table_grad_from_toeplitz(dtp, seg):
    lo = PADK - MAX_REL + 1
    near = dtp[:, lo + N_REL - 3:lo - 1:-1]
    return jnp.concatenate([seg[:, 1:2], near, seg[:, 0:1]], axis=1)


def _bias_band(tp):
    def body(tp_ref, out_ref):
        bv = pltpu.roll(jnp.broadcast_to(tp_ref[0], (QB, TOEP)), 0, axis=1, stride=1, stride_axis=0)
        out_ref[0] = jnp.where(_band_mask(), bv[:, 0:KB], NEG_INF)

    return _call(
        body, name="bias_band", grid=(N_HEADS,),
        in_specs=[pl.BlockSpec((1, 1, TOEP), lambda h: (h, 0, 0))],
        out_specs=pl.BlockSpec((1, QB, KB), lambda h: (h, 0, 0)),
        out_shape=jax.ShapeDtypeStruct((N_HEADS, QB, KB), F32),
        compiler_params=_params("parallel"),
    )(tp)


def _bias_grad(dband, after):
    lo, hi = PADK - MAX_REL + 1, PADK + MAX_REL

    def body(db_ref, _, dtp_ref, seg_ref):
        bv = jnp.concatenate([db_ref[0], jnp.zeros((QB, TOEP - KB), F32)], axis=1)
        row = lax.broadcasted_iota(jnp.int32, (QB, TOEP), 0)
        k = 1
        while k < QB:
            bv = jnp.where((row & k) != 0, pltpu.roll(bv, TOEP - k, axis=1), bv)
            k *= 2
        col = jnp.sum(bv, axis=0, keepdims=True)
        dtp_ref[0] = col
        u = lax.broadcasted_iota(jnp.int32, (1, TOEP), 1)
        far = jnp.sum(jnp.where((u < lo) | (u > hi + MAX_REL), col, 0.0))
        past = jnp.sum(jnp.where((u >= hi) & (u <= hi + MAX_REL), col, 0.0))
        lane = lax.broadcasted_iota(jnp.int32, (1, 128), 1)
        seg_ref[0] = jnp.where(lane == 0, far, jnp.where(lane == 1, past, 0.0))

    return _call(
        body, name="bias_grad", grid=(N_HEADS,),
        in_specs=[pl.BlockSpec((1, QB, KB), lambda h: (h, 0, 0)), ANY],
        out_specs=[pl.BlockSpec((1, 1, TOEP), lambda h: (h, 0, 0)), pl.BlockSpec((1, 1, 128), lambda h: (h, 0, 0))],
        out_shape=[jax.ShapeDtypeStruct((N_HEADS, 1, TOEP), F32), jax.ShapeDtypeStruct((N_HEADS, 1, 128), F32)],
        compiler_params=_params("parallel"),
    )(dband, after)


N_QB = S // QB
HEADS_PER_STEP = 4
ATT_LANES = HEADS_PER_STEP * HEAD
N_HG = D // ATT_LANES


def _band_mask():
    qc = lax.broadcasted_iota(jnp.int32, (QB, KB), 0) // CHUNK
    kc = lax.broadcasted_iota(jnp.int32, (QB, KB), 1) // CHUNK
    return (kc >= qc) & (kc <= qc + LEFT)


def _half_scale(hh, scale):
    lane = lax.broadcasted_iota(jnp.int32, (1, 128), 1)
    return jnp.where((lane < HEAD) == (hh == 0), scale, 0.0).astype(BF16)


def _probs(qh, kb, bias, first_key):
    sc = _nt(qh, kb) + bias
    if first_key is not None:
        sc = jnp.where(lax.broadcasted_iota(jnp.int32, (QB, KB), 1) >= first_key, sc, NEG_INF)
    e = jnp.exp(sc - jnp.max(sc, axis=-1, keepdims=True))
    return e * (1.0 / jnp.sum(e, axis=-1, keepdims=True))


def _by_padding(cb, compute):
    @pl.when(cb < PADK // QB)
    def _():
        compute(PADK - cb * QB)

    @pl.when(cb >= PADK // QB)
    def _():
        compute(None)


def _attn_fwd(q, kp, vp, bias, ex=None):
    def body(q_ref, k_ref, v_ref, b_ref, o_ref):
        cb = pl.program_id(1)
        band = pl.ds(pl.multiple_of(cb * QB, QB), KB)
        low = lax.broadcasted_iota(jnp.int32, (QB, 128), 1) < HEAD

        def compute(first_key):
            for pair in range(HEADS_PER_STEP // 2):
                lanes = pl.ds(pair * 128, 128)
                kb, vb, qv = k_ref[band, lanes], v_ref[band, lanes], q_ref[:, lanes]
                outs = []
                for hh in range(2):
                    pb = _probs(qv * _half_scale(hh, ATTN_SCALE), kb, b_ref[2 * pair + hh], first_key).astype(BF16)
                    outs.append(_nn(pb, vb))
                o_ref[:, lanes] = jnp.where(low, outs[0], outs[1]).astype(BF16)

        _by_padding(cb, compute)

    qspec = pl.BlockSpec((QB, ATT_LANES), lambda hg, cb: (cb, hg))
    kspec = pl.BlockSpec((PADK + S, ATT_LANES), lambda hg, cb: (0, hg))
    return _call_hosting(
        body, ex, _grid_phases((N_HG, N_QB), 0.85), name="attn_fwd", grid=(N_HG, N_QB),
        in_specs=[qspec, kspec, kspec, pl.BlockSpec((HEADS_PER_STEP, QB, KB), lambda hg, cb: (hg, 0, 0))],
        out_specs=[qspec], out_shape=[jax.ShapeDtypeStruct((S, D), BF16)], scratch_shapes=[],
        args=(q, kp, vp, bias), compiler_params=_params("arbitrary", "arbitrary"),
    )


def _attn_bwd(q, kp, vp, bias, do, ex=None):
    def body(q_ref, k_ref, v_ref, b_ref, do_ref, dq_ref, dk_ref, dv_ref, db_ref):
        cb = pl.program_id(1)

        @pl.when(cb == 0)
        def _():
            dk_ref[...] = jnp.zeros(dk_ref.shape, F32)
            dv_ref[...] = jnp.zeros(dv_ref.shape, F32)
            db_ref[...] = jnp.zeros(db_ref.shape, F32)

        band = pl.ds(pl.multiple_of(cb * QB, QB), KB)
        low = lax.broadcasted_iota(jnp.int32, (QB, 128), 1) < HEAD

        def compute(first_key):
            for pair in range(HEADS_PER_STEP // 2):
                lanes = pl.ds(pair * 128, 128)
                kb, vb = k_ref[band, lanes], v_ref[band, lanes]
                qv, dov = q_ref[:, lanes], do_ref[:, lanes]
                dq = jnp.zeros((QB, 128), F32)
                dkb = jnp.zeros((KB, 128), F32)
                dvb = jnp.zeros((KB, 128), F32)
                for hh in range(2):
                    sel = low if hh == 0 else jnp.logical_not(low)
                    doh = dov * _half_scale(hh, 1.0)
                    p = _probs(qv * _half_scale(hh, ATTN_SCALE), kb, b_ref[2 * pair + hh], first_key)
                    dp = _nt(doh, vb)
                    dvb = dvb + _tn(p.astype(BF16), doh)
                    ds = p * (dp - jnp.sum(dp * p, axis=-1, keepdims=True))
                    db_ref[2 * pair + hh] += ds
                    dsb = (ds * ATTN_SCALE).astype(BF16)
                    dq = dq + jnp.where(sel, _nn(dsb, kb), 0.0)
                    dkb = dkb + _tn(dsb, qv * _half_scale(hh, 1.0))
                dq_ref[:, lanes] = dq
                dk_ref[band, lanes] += dkb
                dv_ref[band, lanes] += dvb

        _by_padding(cb, compute)

    qspec = pl.BlockSpec((QB, ATT_LANES), lambda hg, cb: (cb, hg))
    kspec = pl.BlockSpec((PADK + S, ATT_LANES), lambda hg, cb: (0, hg))
    bspec = pl.BlockSpec((HEADS_PER_STEP, QB, KB), lambda hg, cb: (hg, 0, 0))
    kf = jax.ShapeDtypeStruct((PADK + S, D), F32)
    return _call_hosting(
        body, ex, _grid_phases((N_HG, N_QB)), name="attn_bwd", grid=(N_HG, N_QB),
        in_specs=[qspec, kspec, kspec, bspec, qspec],
        out_specs=[qspec, kspec, kspec, bspec],
        out_shape=[jax.ShapeDtypeStruct((S, D), F32), kf, kf, jax.ShapeDtypeStruct((N_HEADS, QB, KB), F32)],
        scratch_shapes=[], args=(q, kp, vp, bias, do), compiler_params=_params("arbitrary", "arbitrary"),
    )


def _adam_math(w, g, m, v):
    m = ADAM_B1 * m + (1.0 - ADAM_B1) * g
    v = ADAM_B2 * v + (1.0 - ADAM_B2) * (g * g)
    m_hat = m / (1.0 - ADAM_B1 ** ADAM_STEP)
    v_hat = v / (1.0 - ADAM_B2 ** ADAM_STEP)
    delta = -ADAM_LR * (m_hat / (jnp.sqrt(v_hat) + ADAM_EPS) + ADAM_WD * w)
    return delta, m, v


def _sum_parts(parts, name, after=None):
    count, r, c = parts.shape
    tr = r // 2 if r % 16 == 0 and r > 64 else r

    def body(p_ref, *rest):
        o_ref = rest[-1]
        acc = p_ref[0].astype(F32)
        for d in range(1, count):
            acc = acc + p_ref[d].astype(F32)
        o_ref[...] = acc

    return _call(
        body, name=name, grid=(r // tr,),
        in_specs=[pl.BlockSpec((count, tr, c), lambda i: (0, i, 0))] + ([ANY] if after is not None else []),
        out_specs=pl.BlockSpec((tr, c), lambda i: (i, 0)),
        out_shape=jax.ShapeDtypeStruct((r, c), F32), compiler_params=_params("parallel"),
    )(parts, *([after] if after is not None else []))


def _row_tile(r):
    for cand in (256, 176, 128, 64, 32, 16, 8):
        if r % cand == 0:
            return cand
    return r


def _adam_layer(w, g, m, v, layer, name, prev=None):
    nl, r, c = w.shape
    tr = _row_tile(r)

    def body(w_ref, g_ref, m_ref, v_ref, *rest):
        go_ref, d_ref, nm_ref, nv_ref = rest[-4:]
        gv = g_ref[...]
        delta, nm, nv = _adam_math(w_ref[0], gv, m_ref[0], v_ref[0])
        go_ref[0] = gv
        d_ref[0] = delta
        nm_ref[0] = nm
        nv_ref[0] = nv

    lspec = pl.BlockSpec((1, tr, c), lambda i: (layer, i, 0))
    sd = jax.ShapeDtypeStruct((nl, r, c), F32)
    extra = list(prev) if prev is not None else []
    return _call(
        body, name=name, grid=(r // tr,),
        in_specs=[lspec, pl.BlockSpec((tr, c), lambda i: (i, 0)), lspec, lspec] + [ANY] * len(extra),
        out_specs=[lspec] * 4, out_shape=[sd] * 4,
        input_output_aliases={4 + t: t for t in range(len(extra))},
        compiler_params=_params("parallel"),
    )(w, g, m, v, *extra)


def _adam(w, g, m, v, name):
    r, c = w.shape
    tr = _row_tile(r)

    def body(w_ref, g_ref, m_ref, v_ref, d_ref, nm_ref, nv_ref):
        delta, nm, nv = _adam_math(w_ref[...], g_ref[...], m_ref[...], v_ref[...])
        d_ref[...] = delta
        nm_ref[...] = nm
        nv_ref[...] = nv

    spec = pl.BlockSpec((tr, c), lambda i: (i, 0))
    sd = jax.ShapeDtypeStruct((r, c), F32)
    return _call(
        body, name=name, grid=(r // tr,), in_specs=[spec] * 4, out_specs=[spec] * 3, out_shape=[sd] * 3,
        compiler_params=_params("parallel"),
    )(w, g, m, v)


def _pad16(rows):
    return jnp.pad(rows, ((0, 16 - rows.shape[0]), (0, 0)))


def kernel(x, norm_mix_g, norm_ffn_g, pool_w, pool_b, pool_scale, kv_norm_g, w_k, w_v, k_norm_g, w_q, q_norm_g, rel_bias, w_o, w_gate, w_up, w_down, loss_target, m_norm_mix_g, m_norm_ffn_g, m_pool_w, m_pool_b, m_pool_scale, m_kv_norm_g, m_w_k, m_w_v, m_k_norm_g, m_w_q, m_q_norm_g, m_rel_bias, m_w_o, m_w_gate, m_w_up, m_w_down, v_norm_mix_g, v_norm_ffn_g, v_pool_w, v_pool_b, v_pool_scale, v_kv_norm_g, v_w_k, v_w_v, v_k_norm_g, v_w_q, v_q_norm_g, v_rel_bias, v_w_o, v_w_gate, v_w_up, v_w_down):
    assert x.shape == (1, S, D) and w_gate.shape == (2, D, F_SHARD) and w_k.shape == (D_SHARD, D)
    xin, target = x[0], loss_target[0]

    ffn_shards = [[w_gate[layer].T.astype(BF16), w_up[layer].T.astype(BF16), w_down[layer].astype(BF16)]
                  for layer in range(2)]
    att_shards = [w_k.astype(BF16), w_v.astype(BF16), w_q[0].astype(BF16), w_o[0].astype(BF16)]
    pool_shard = pool_w[0].astype(BF16).reshape(N_GROUPS * POOL_SHARD, GROUP)
    small = jnp.concatenate([pool_b[0].reshape(1, N_GROUPS * POOL_SHARD), pool_scale], axis=1)

    full0 = _run_exchange(_relay_gather_exchange(ffn_shards[0] + [pool_shard, _pad16(small)]), "gather_layer0")
    ffn_w0 = [a.reshape(F, D) for a in full0[:3]]
    pw_f = full0[3].reshape(N_DEV, N_GROUPS, POOL_SHARD, GROUP).transpose(1, 0, 2, 3).reshape(N_GROUPS, GROUP, GROUP)
    small_f = full0[4][:, 0, :]
    pb_f = small_f[:, :N_GROUPS * POOL_SHARD].reshape(N_DEV, N_GROUPS, POOL_SHARD).transpose(1, 0, 2).reshape(1, D)
    ps_f = small_f[:, N_GROUPS * POOL_SHARD:].reshape(1, D)

    g_mix0, g_mix1 = norm_mix_g[0:1], norm_mix_g[1:2]
    g_ffn0, g_ffn1 = norm_ffn_g[0:1], norm_ffn_g[1:2]
    g_kv = kv_norm_g.reshape(1, D)
    gk_t = jnp.tile(k_norm_g.reshape(1, HEAD), (1, N_HEADS))
    gq_t = jnp.tile(q_norm_g.reshape(1, HEAD), (1, N_HEADS))

    x1, diff = _pool_fwd(xin, g_mix0, pw_f, pb_f, ps_f)
    (x2, hf0, gg0, uu0), full_att = _ffn_fwd(x1, g_ffn0, *ffn_w0, name="ffn_fwd", ex=_relay_gather_exchange(att_shards))
    wk_f, wv_f, wq_f, wo_f = [a.reshape(D, D) for a in full_att]
    (hkv, h1, kpre, qpre, kp, vp, qq), full1_gate = _kvq_proj(x2, g_kv, g_mix1, wk_f, wv_f, wq_f, gk_t, gq_t,
                                                                ex=_relay_gather_exchange(ffn_shards[1][:1]))
    bias = _bias_band(_toeplitz_from_table(rel_bias[0]))
    (att,), full1_rest = _attn_fwd(qq, kp, vp, bias, ex=_relay_gather_exchange(ffn_shards[1][1:]))
    ffn_w1 = [a.reshape(F, D) for a in full1_gate + full1_rest]
    x3 = _mm(att, wo_f, "nn", F32, "attn_out", add=x2)
    (dx4, loss_rows, hf1, gg1, uu1), _ = _ffn_fwd(x3, g_ffn1, *ffn_w1, name="ffn_fwd_loss", target=target)

    def blocks(dw):
        return dw.reshape(N_DEV, dw.shape[0] // N_DEV, dw.shape[1])

    (dgg1, duu1, dwg1, dwu1, dwd1), _ = _ffn_bwd_weights(dx4, hf1, gg1, uu1, ffn_w1[2], name="ffn_bwd1")
    parts1 = [blocks(dw) for dw in (dwg1, dwu1, dwd1)]
    (dx3, dg_ffn1), stage1 = _ffn_bwd_input(dx4, x3, g_ffn1, [dgg1, duu1], ffn_w1[:2], "ffn_dx1",
                                            ex=_pair_exchange(parts1))
    both1 = _pair_add(parts1, stage1, "pair_add_ffn1", copies=2)
    flight1, chip1, land1, _ = _chip_exchange_start(both1[:3], both1[3:], dx3, "scatter_ffn1_start")
    datt = _mm(dx3, wo_f, "nt", BF16, "d_attn")
    dwo = _mm(att, dx3, "tn", BF16, "d_wo")
    (dq, dkp, dvp, dband), _ = _attn_bwd(qq, kp, vp, bias, datt)
    (dx2, dqpre, dkpre, dg_mix1, dg_kv, dgq, dgk), _ = _qkv_input_bwd(
        dq, qpre, gq_t, dkp, kpre, gk_t, dvp, wq_f, wk_f, wv_f, dx3, x2, g_mix1, g_kv)
    dwq = _mm(h1, dqpre, "tn", BF16, "d_wq")
    dwk = _mm(hkv, dkpre, "tn", BF16, "d_wk")
    dwv = _mm(hkv, dvp, "tn", BF16, "d_wv", skip_rows=PADK)
    parts_att = [blocks(dw) for dw in (dwk, dwv, dwq, dwo)]
    (dgg0, duu0, *dw0), stage_att = _ffn_bwd_weights(
        dx2, hf0, gg0, uu0, ffn_w0[2], name="ffn_bwd0", ex=_pair_exchange(parts_att))
    both_att = _pair_add(parts_att, stage_att, "pair_add_att", copies=2)
    flight_att, chip_att, land_att, _ = _chip_exchange_start(both_att[:4], both_att[4:], dx2, "scatter_att_start")
    parts0 = [blocks(dw) for dw in dw0]
    (dx1, dg_ffn0), stage0 = _ffn_bwd_input(dx2, x1, g_ffn0, [dgg0, duu0], ffn_w0[:2], "ffn_dx0",
                                            ex=_pair_exchange(parts0))
    both = _pair_add(parts0, stage0, "pair_add_ffn0", copies=2)
    flight, chip0, land0, token = _chip_exchange_start(both[:3], both[3:], dx1, "scatter_ffn0_start")
    (grad_x, dpw, db_rows, ds_rows, dg_mix0), _ = _pool_bwd(dx1, xin, diff, g_mix0 + token[0:1, 0:1], pw_f, pb_f, ps_f)
    dtp, seg = _bias_grad(dband, dpw)

    weights = dict(norm_mix_g=norm_mix_g, norm_ffn_g=norm_ffn_g, pool_w=pool_w, pool_b=pool_b,
                   pool_scale=pool_scale, kv_norm_g=kv_norm_g, w_k=w_k, w_v=w_v, k_norm_g=k_norm_g, w_q=w_q,
                   q_norm_g=q_norm_g, rel_bias=rel_bias, w_o=w_o, w_gate=w_gate, w_up=w_up, w_down=w_down)
    mom1 = dict(norm_mix_g=m_norm_mix_g, norm_ffn_g=m_norm_ffn_g, pool_w=m_pool_w, pool_b=m_pool_b,
                pool_scale=m_pool_scale, kv_norm_g=m_kv_norm_g, w_k=m_w_k, w_v=m_w_v, k_norm_g=m_k_norm_g,
                w_q=m_w_q, q_norm_g=m_q_norm_g, rel_bias=m_rel_bias, w_o=m_w_o, w_gate=m_w_gate, w_up=m_w_up,
                w_down=m_w_down)
    mom2 = dict(norm_mix_g=v_norm_mix_g, norm_ffn_g=v_norm_ffn_g, pool_w=v_pool_w, pool_b=v_pool_b,
                pool_scale=v_pool_scale, kv_norm_g=v_kv_norm_g, w_k=v_w_k, w_v=v_w_v, k_norm_g=v_k_norm_g,
                w_q=v_w_q, q_norm_g=v_q_norm_g, rel_bias=v_rel_bias, w_o=v_w_o, w_gate=v_w_gate, w_up=v_w_up,
                w_down=v_w_down)
    names = list(weights)
    grads, deltas, new_m, new_v = {}, {}, {}, {}

    def summed(arrs, tag, after):
        return [_sum_parts(a, "sum_parts_%s%d" % (tag, k), after) for k, a in enumerate(arrs)]

    def adam_flat(nm):
        shape = weights[nm].shape
        flat = lambda a: a.reshape(-1, shape[-1])
        dl, m1, m2 = _adam(flat(weights[nm]), flat(grads[nm]), flat(mom1[nm]), flat(mom2[nm]), "adam_" + nm)
        deltas[nm], new_m[nm], new_v[nm] = dl.reshape(shape), m1.reshape(shape), m2.reshape(shape)

    ffn_names = ("w_gate", "w_up", "w_down")
    def ffn_view(nm, a):
        return a if nm == "w_down" else a.transpose(0, 2, 1)

    recv1 = _chip_exchange_wait(flight1, chip1, land1, dtp, "scatter_ffn1_wait")
    sums1 = summed(recv1, "ffn1_", token)
    layer1 = {nm: _adam_layer(ffn_view(nm, weights[nm]), g, ffn_view(nm, mom1[nm]), ffn_view(nm, mom2[nm]), 1,
                              "adam1_" + nm)
              for nm, g in zip(ffn_names, sums1)}
    recv_att = _chip_exchange_wait(flight_att, chip_att, land_att, layer1["w_down"][1], "scatter_att_wait")
    g_wk, g_wv, g_wq, g_wo = summed(recv_att, "att_", token)
    grads.update(w_k=g_wk, w_v=g_wv, w_q=g_wq[None], w_o=g_wo[None])
    for nm in ("w_k", "w_v", "w_q", "w_o"):
        adam_flat(nm)

    dpw_blocks = dpw.reshape(N_GROUPS, N_DEV, POOL_SHARD, GROUP).transpose(1, 0, 2, 3)
    dpw_blocks = dpw_blocks.reshape(N_DEV, N_GROUPS * POOL_SHARD, GROUP).astype(BF16)
    recv_pool = _run_exchange(_scatter_exchange([dpw_blocks]), "scatter_pool", after=deltas["w_o"])
    misc = jnp.concatenate([dgk[0:1, 0:HEAD], dgq[0:1, 0:HEAD], loss_rows[0:1, 0:1],
                            seg[:, 0, 0].reshape(1, N_HEADS), seg[:, 0, 1].reshape(1, N_HEADS)], axis=1)
    misc = jnp.pad(misc, ((0, 0), (0, D - misc.shape[1])))
    vec_rows = jnp.concatenate([dg_mix0[0:1], dg_mix1[0:1], dg_ffn0[0:1], dg_ffn1[0:1], dg_kv[0:1],
                                db_rows[0:1], ds_rows[0:1], misc], axis=0)
    pack = jnp.concatenate([vec_rows, dtp.reshape(N_HEADS, TOEP)], axis=0)
    tot = _all_reduce_small(pack, "reduce_small", after=recv_pool[0])

    loss = tot[7, 2 * HEAD]
    seg_tot = jnp.stack([tot[7, 2 * HEAD + 1:2 * HEAD + 1 + N_HEADS],
                         tot[7, 2 * HEAD + 1 + N_HEADS:2 * HEAD + 1 + 2 * N_HEADS]], axis=1)
    me = 4 * lax.axis_index("x") + 2 * lax.axis_index("y") + lax.axis_index("c")
    g_pool_b = lax.dynamic_slice_in_dim(tot[5].reshape(N_GROUPS, GROUP), me * POOL_SHARD, POOL_SHARD, axis=1)
    grads.update(
        norm_mix_g=tot[0:2], norm_ffn_g=tot[2:4], kv_norm_g=tot[4], k_norm_g=tot[7, 0:HEAD],
        q_norm_g=tot[7, HEAD:2 * HEAD].reshape(1, HEAD),
        rel_bias=_table_grad_from_toeplitz(tot[8:8 + N_HEADS], seg_tot).reshape(1, N_HEADS, N_REL),
        pool_b=g_pool_b.reshape(1, N_GROUPS, POOL_SHARD),
        pool_scale=lax.dynamic_slice_in_dim(tot[6:7], me * D_SHARD, D_SHARD, axis=1),
        pool_w=summed(recv_pool, "pool_", None)[0].reshape(1, N_GROUPS, POOL_SHARD, GROUP))
    adam_flat("pool_w")
    small_names = [nm for nm in names if nm not in ffn_names + ("w_k", "w_v", "w_q", "w_o", "pool_w")]

    def pack_small(tree):
        cols = []
        for nm in small_names:
            flat = tree[nm].reshape(-1)
            cols.append(jnp.pad(flat, (0, -flat.shape[0] % 1024)))
        return jnp.concatenate(cols).reshape(-1, 128)

    dl, m1, m2 = _adam(pack_small(weights), pack_small(grads), pack_small(mom1), pack_small(mom2), "adam_small")

    def unpack_small(packed, out):
        flat, off = packed.reshape(-1), 0
        for nm in small_names:
            size = weights[nm].size
            out[nm] = flat[off:off + size].reshape(weights[nm].shape)
            off += size + (-size % 1024)

    unpack_small(dl, deltas)
    unpack_small(m1, new_m)
    unpack_small(m2, new_v)

    recv0 = _chip_exchange_wait(flight, chip0, land0, dl, "scatter_ffn0_wait")
    sums0 = summed(recv0, "ffn0_", None)
    for nm, g in zip(ffn_names, sums0):
        done = _adam_layer(ffn_view(nm, weights[nm]), g, ffn_view(nm, mom1[nm]), ffn_view(nm, mom2[nm]), 0,
                           "adam0_" + nm, prev=layer1[nm])
        grads[nm], deltas[nm], new_m[nm], new_v[nm] = [ffn_view(nm, a) for a in done]

    return (loss, grad_x[None], *[grads[nm] for nm in names], *[deltas[nm] for nm in names],
            *[new_m[nm] for nm in names], *[new_v[nm] for nm in names])
```

```python
import functools

import jax
import jax.numpy as jnp
from jax import lax
from jax.experimental import pallas as pl
from jax.experimental.pallas import tpu as pltpu

F32 = jnp.float32
BF16 = jnp.bfloat16
MESH_ID = pl.DeviceIdType.MESH

N_DEV = 8
S = 2048
D = 1024
F = 2816
F_SHARD = F // N_DEV
D_SHARD = D // N_DEV
N_GROUPS = 4
GROUP = D // N_GROUPS
POOL_SHARD = GROUP // N_DEV
MAX_WIN = 16
HEAD = 64
N_HEADS = D // HEAD
CHUNK = 64
LEFT = 8
QB = 4 * CHUNK
KB = QB + LEFT * CHUNK
PADK = LEFT * CHUNK
TOEP = 1024
N_REL = 257
MAX_REL = 128
EPS = 1e-6
NEG_INF = -1e30
ATTN_SCALE = HEAD ** -0.5

ADAM_LR = 0.001
ADAM_B1 = 0.9
ADAM_B2 = 0.999
ADAM_EPS = 1e-08
ADAM_WD = 0.01
ADAM_STEP = 10

VMEM_LIMIT = 52 * 1024 * 1024

ANY = pl.BlockSpec(memory_space=pl.ANY)
VMEM = pl.BlockSpec(memory_space=pltpu.VMEM)


def _call(body, **kw):
    return pl.pallas_call(body, **kw)


def _params(*sem):
    return pltpu.CompilerParams(dimension_semantics=sem, vmem_limit_bytes=VMEM_LIMIT)


def _dot(a, b, dims):
    return lax.dot_general(a, b, (dims, ((), ())), preferred_element_type=F32)


def _nn(a, b):
    return _dot(a, b, ((1,), (0,)))


def _nt(a, b):
    return _dot(a, b, ((1,), (1,)))


def _tn(a, b):
    return _dot(a, b, ((0,), (0,)))


def _rstd(x):
    return lax.rsqrt(jnp.mean(x * x, axis=-1, keepdims=True) + EPS)


def _rms_bwd(dh, x, r, g):
    gd = dh * g
    return r * gd - x * (r * r * r) * jnp.mean(gd * x, axis=-1, keepdims=True)


def _colsum8(v):
    return jnp.broadcast_to(jnp.sum(v, axis=0, keepdims=True), (8, v.shape[1]))


def _seg_sum(v):
    r = lax.broadcasted_iota(jnp.int32, (128, 128), 0) // HEAD
    c = lax.broadcasted_iota(jnp.int32, (128, 128), 1) // HEAD
    ones = jnp.where(r == c, 1.0, 0.0).astype(BF16)
    out = []
    for blk in range(v.shape[1] // 128):
        part = v[:, blk * 128:(blk + 1) * 128]
        hi = part.astype(BF16)
        rest = part - hi.astype(F32)
        mid = rest.astype(BF16)
        lo = (rest - mid.astype(F32)).astype(BF16)
        out.append(_nn(hi, ones) + _nn(mid, ones) + _nn(lo, ones))
    return jnp.concatenate(out, axis=1)


def _place():
    return lax.axis_index("x"), lax.axis_index("y"), lax.axis_index("c")


class _Exchange:
    def __init__(self, ins, out_shape, sems, start, mid, finish, early=None):
        self.ins, self.out_shape, self.sems = list(ins), list(out_shape), list(sems)
        self.start, self.mid, self.finish = start, mid, finish
        self.early = early if early is not None else (lambda ins, outs, sems: None)


def _relay_gather_exchange(shards):
    n = len(shards)

    def tools(ins, outs, sems):
        send_sems, recv_sems, local_sems = sems
        x, y, c = _place()
        me, sib = (x, y, c), (x, y, 1 - c)
        xn, yn, dg = (1 - x, y, c), (x, 1 - y, c), (1 - x, 1 - y, c)
        other = lambda dev: (dev[0], dev[1], 1 - c)

        def piece(k, ref, half):
            rows = shards[k].shape[0] // 2
            return ref if half is None else ref.at[pl.ds(half * rows, rows)]

        def held(k, dev, half):
            return piece(k, outs[k].at[4 * dev[0] + 2 * dev[1] + dev[2]], half)

        def copy(k, s, dev, half, to, own=False):
            return pltpu.make_async_remote_copy(
                src_ref=piece(k, ins[k], half) if own else held(k, dev, half), dst_ref=held(k, dev, half),
                send_sem=send_sems.at[k, s], recv_sem=recv_sems.at[k, s], device_id=to, device_id_type=MESH_ID)

        def mine(k):
            return pltpu.make_async_copy(ins[k], held(k, me, None), local_sems.at[k])

        def sent(k):
            return [copy(k, 0, me, None, sib, own=True), copy(k, 1, me, 0, xn, own=True), copy(k, 2, me, 1, yn, own=True),
                    copy(k, 3, me, 0, yn, own=True), copy(k, 4, me, 1, xn, own=True)]

        relays = [(1, xn, 0, [(5, yn), (7, sib)]), (2, yn, 1, [(6, xn), (9, sib)]), (3, yn, 0, [(10, sib)]),
                  (4, xn, 1, [(8, sib)]), (6, dg, 1, [(11, sib)]), (5, dg, 0, [(12, sib)])]
        from_sibling = [(0, other(me), None), (7, other(xn), 0), (8, other(xn), 1), (9, other(yn), 1),
                        (10, other(yn), 0), (11, other(dg), 1), (12, other(dg), 0)]
        return mine, sent, copy, relays, from_sibling, me

    def start(ins, outs, sems):
        mine, sent, _, _, _, _ = tools(ins, outs, sems)
        for k in range(n):
            mine(k).start()
        for k in range(n):
            for cp in sent(k):
                cp.start()

    def pass_on(ins, outs, sems, which):
        _, _, copy, relays, _, me = tools(ins, outs, sems)
        for k in range(n):
            for s, dev, half, onward in relays[which]:
                copy(k, s, dev, half, me).wait_recv()
                for s2, to in onward:
                    copy(k, s2, dev, half, to).start()

    def early(ins, outs, sems):
        pass_on(ins, outs, sems, slice(0, 2))

    def mid(ins, outs, sems):
        pass_on(ins, outs, sems, slice(2, None))

    def finish(ins, outs, sems):
        mine, sent, copy, relays, from_sibling, me = tools(ins, outs, sems)
        for k in range(n):
            for s, dev, half in from_sibling:
                copy(k, s, dev, half, me).wait_recv()
        for k in range(n):
            for cp in sent(k):
                cp.wait_send()
            for _, dev, half, onward in relays:
                for s2, to in onward:
                    copy(k, s2, dev, half, to).wait_send()
            mine(k).wait()

    return _Exchange(
        shards, [jax.ShapeDtypeStruct((N_DEV,) + a.shape, a.dtype) for a in shards],
        [pltpu.SemaphoreType.DMA((n, 13)), pltpu.SemaphoreType.DMA((n, 13)), pltpu.SemaphoreType.DMA((n,))],
        start, mid, finish, early)


def _peer(x, y, c, m):
    px = 1 - x if m & 4 else x
    py = 1 - y if m & 2 else y
    pc = 1 - c if m & 1 else c
    return px, py, pc


def _scatter_exchange(parts):
    n = len(parts)

    def tools(ins, outs, sems):
        send_sems, recv_sems, local_sems = sems
        x, y, c = _place()
        me = 4 * x + 2 * y + c
        def mine():
            return [pltpu.make_async_copy(ins[k].at[me], outs[k].at[me], local_sems.at[k]) for k in range(n)]

        def remote(dst_is_mine):
            out = []
            for m in range(1, N_DEV):
                px, py, pc = _peer(x, y, c, m)
                peer = 4 * px + 2 * py + pc
                for k in range(n):
                    out.append(pltpu.make_async_remote_copy(
                        src_ref=ins[k].at[peer], dst_ref=outs[k].at[me if dst_is_mine else peer],
                        send_sem=send_sems.at[k, m - 1], recv_sem=recv_sems.at[k, m - 1],
                        device_id=(px, py, pc), device_id_type=MESH_ID))
            return out

        return mine, remote

    def start(ins, outs, sems):
        mine, remote = tools(ins, outs, sems)
        for cp in mine() + remote(True):
            cp.start()

    def mid(ins, outs, sems):
        pass

    def finish(ins, outs, sems):
        mine, remote = tools(ins, outs, sems)
        for cp in remote(False):
            cp.wait_recv()
        for cp in remote(True):
            cp.wait_send()
        for cp in mine():
            cp.wait()

    return _Exchange(
        parts, [jax.ShapeDtypeStruct(a.shape, a.dtype) for a in parts],
        [pltpu.SemaphoreType.DMA((n, 7)), pltpu.SemaphoreType.DMA((n, 7)), pltpu.SemaphoreType.DMA((n,))],
        start, mid, finish)


N_CHIPS = N_DEV // 2


def _pair_exchange(parts):
    n = len(parts)

    def copies(ins, outs, sems):
        send_sems, recv_sems = sems
        x, y, c = _place()
        return [pltpu.make_async_remote_copy(
            src_ref=ins[k].at[2 * q + 1 - c], dst_ref=outs[k].at[q], send_sem=send_sems.at[k, q],
            recv_sem=recv_sems.at[k, q], device_id=(x, y, 1 - c), device_id_type=MESH_ID)
            for k in range(n) for q in range(N_CHIPS)]

    def start(ins, outs, sems):
        for cp in copies(ins, outs, sems):
            cp.start()

    def mid(ins, outs, sems):
        pass

    def finish(ins, outs, sems):
        for cp in copies(ins, outs, sems):
            cp.wait_recv()
        for cp in copies(ins, outs, sems):
            cp.wait_send()

    return _Exchange(
        parts, [jax.ShapeDtypeStruct((N_CHIPS,) + a.shape[1:], a.dtype) for a in parts],
        [pltpu.SemaphoreType.DMA((n, N_CHIPS)), pltpu.SemaphoreType.DMA((n, N_CHIPS))], start, mid, finish)


def _pair_add(parts, stage, name, copies=1):
    n = len(parts)
    core = lax.axis_index("c").reshape(1)

    def body(core_ref, *refs):
        for k in range(n):
            mine, theirs = refs[k], refs[n + k]
            total = (mine[0, 0].astype(F32) + theirs[0].astype(F32)).astype(BF16)
            for rep in range(copies):
                refs[(2 + rep) * n + k][0] = total

    in_specs, out_specs = [], []
    for a in parts:
        _, r, cdim = a.shape
        in_specs.append(pl.BlockSpec((1, 1, r // 2, cdim), lambda q, i, core_ref: (q, core_ref[0], i, 0)))
    for a in parts:
        _, r, cdim = a.shape
        in_specs.append(pl.BlockSpec((1, r // 2, cdim), lambda q, i, core_ref: (q, i, 0)))
        out_specs.append(pl.BlockSpec((1, r // 2, cdim), lambda q, i, core_ref: (q, i, 0)))
    return list(_call(
        body, name=name,
        grid_spec=pltpu.PrefetchScalarGridSpec(num_scalar_prefetch=1, grid=(N_CHIPS, 2), in_specs=in_specs,
                                               out_specs=out_specs * copies),
        out_shape=[jax.ShapeDtypeStruct(s.shape, BF16) for s in stage] * copies,
        compiler_params=_params("arbitrary", "arbitrary"),
    )(core, *[a.reshape((N_CHIPS, 2) + a.shape[1:]) for a in parts], *stage))


HBM = pl.BlockSpec(memory_space=pltpu.HBM)
SEMAPHORES = pl.BlockSpec(memory_space=pltpu.SEMAPHORE)


def _chip_copies(srcs, lands, send_sems, recv_sems, mine_is_dst):
    x, y, c = _place()
    me = 2 * x + y
    out = []
    for m in range(1, N_CHIPS):
        px, py, _ = _peer(x, y, c, 2 * m)
        peer = 2 * px + py
        for k in range(len(srcs)):
            pair = k * (N_CHIPS - 1) + m - 1
            out.append(pltpu.make_async_remote_copy(
                src_ref=srcs[k].at[peer], dst_ref=lands[k].at[me if mine_is_dst else peer],
                send_sem=send_sems[pair], recv_sem=recv_sems[pair],
                device_id=(px, py, c), device_id_type=MESH_ID))
    return out


def _chip_exchange_start(chip_parts, lands, after, name):
    n = len(chip_parts)
    pairs = n * (N_CHIPS - 1)

    def body(*refs):
        srcs, zones = refs[:n], refs[n:2 * n]
        sems = refs[2 * n + 1:2 * n + 1 + 2 * pairs]
        token = refs[-1]
        for cp in _chip_copies(srcs, zones, sems[:pairs], sems[pairs:], True):
            cp.start()
        token[...] = jnp.zeros(token.shape, F32)

    thru = [pltpu.HBM(a.shape, a.dtype) for a in chip_parts + lands]
    hbm = [pltpu.with_memory_space_constraint(a, pltpu.HBM) for a in chip_parts + lands]
    res = _call(
        body, name=name,
        out_shape=[pltpu.SemaphoreType.DMA(())] * (2 * pairs) + thru + [jax.ShapeDtypeStruct((8, 128), F32)],
        in_specs=[HBM] * (2 * n) + [ANY], out_specs=[SEMAPHORES] * (2 * pairs) + [HBM] * (2 * n) + [VMEM],
        input_output_aliases={i: 2 * pairs + i for i in range(2 * n)},
        compiler_params=pltpu.CompilerParams(has_side_effects=pltpu.SideEffectType.DATAFLOW_SIDE_EFFECTING),
    )(*hbm, after)
    sems, rest = list(res[:2 * pairs]), res[2 * pairs:]
    return sems, list(rest[:n]), list(rest[n:2 * n]), rest[-1]


def _chip_exchange_wait(sems, chip_parts, lands, after, name):
    n = len(chip_parts)
    pairs = n * (N_CHIPS - 1)

    def body(*refs):
        srcs, zones = refs[:n], refs[n:2 * n]
        sem_refs = refs[2 * n:2 * n + 2 * pairs]
        for cp in _chip_copies(srcs, zones, sem_refs[:pairs], sem_refs[pairs:], False):
            cp.wait_send()
            cp.wait_recv()

    thru = [pltpu.HBM(a.shape, a.dtype) for a in chip_parts + lands]
    res = _call(
        body, name=name, out_shape=thru,
        in_specs=[HBM] * (2 * n) + [SEMAPHORES] * (2 * pairs) + [ANY], out_specs=[HBM] * (2 * n),
        input_output_aliases={i: i for i in range(2 * n)},
        compiler_params=pltpu.CompilerParams(has_side_effects=pltpu.SideEffectType.DATAFLOW_SIDE_EFFECTING),
    )(*chip_parts, *lands, *sems, after)
    return list(res[n:])


def _run_exchange(ex, name, after=None):
    n_in, n_out = len(ex.ins), len(ex.out_shape)
    order = [] if after is None else [after]

    def body(*refs):
        ins, outs = refs[:n_in], refs[n_in + len(order):n_in + len(order) + n_out]
        sems = refs[n_in + len(order) + n_out:]
        ex.start(ins, outs, sems)
        ex.early(ins, outs, sems)
        ex.mid(ins, outs, sems)
        ex.finish(ins, outs, sems)

    return list(_call(body, name=name, out_shape=ex.out_shape, in_specs=[ANY] * (n_in + len(order)),
                      out_specs=[ANY] * n_out, scratch_shapes=ex.sems)(*ex.ins, *order))


def _call_hosting(body, ex, phases, *, in_specs, out_specs, out_shape, scratch_shapes, args, **kw):
    n_in, n_out, n_scr = len(in_specs), len(out_specs), len(scratch_shapes)
    if ex is None:
        res = _call(body, in_specs=in_specs, out_specs=out_specs, out_shape=out_shape,
                    scratch_shapes=scratch_shapes, **kw)(*args)
        return list(res), []
    n_xin, n_xout = len(ex.ins), len(ex.out_shape)

    def hosting(*refs):
        a, b = n_in, n_in + n_xin
        c, d = b + n_out, b + n_out + n_xout
        ins, xins, outs, xouts = refs[:a], refs[a:b], refs[b:c], refs[c:d]
        scr, sems = refs[d:d + n_scr], refs[d + n_scr:]
        first, early, mid, last = phases()

        @pl.when(first)
        def _():
            ex.start(xins, xouts, sems)

        body(*ins, *outs, *scr)

        @pl.when(early)
        def _():
            ex.early(xins, xouts, sems)

        @pl.when(mid)
        def _():
            ex.mid(xins, xouts, sems)

        @pl.when(last)
        def _():
            ex.finish(xins, xouts, sems)

    res = _call(hosting, in_specs=list(in_specs) + [ANY] * n_xin, out_specs=list(out_specs) + [ANY] * n_xout,
                out_shape=list(out_shape) + ex.out_shape, scratch_shapes=list(scratch_shapes) + ex.sems,
                **kw)(*args, *ex.ins)
    return list(res[:n_out]), list(res[n_out:])


def _grid_phases(dims, mid_fraction=0.8, early_fraction=0.4):
    total = 1
    for d in dims:
        total *= d
    mid = min(max(int(total * mid_fraction), 1), total - 1)
    early = min(int(total * early_fraction), mid)

    def phases():
        step = pl.program_id(0)
        for axis in range(1, len(dims)):
            step = step * dims[axis] + pl.program_id(axis)
        return step == 0, step == early, step == mid, step == total - 1
    return phases


def _all_reduce_small(pack, name, after):
    rows = pack.shape[0]

    def body(in_ref, _, out_ref, recv, send_sems, recv_sems):
        x, y, c = _place()
        me = 4 * x + 2 * y + c
        recv[me] = in_ref[...]
        sent = []
        for m in range(1, N_DEV):
            px, py, pc = _peer(x, y, c, m)
            cp = pltpu.make_async_remote_copy(
                src_ref=in_ref, dst_ref=recv.at[me], send_sem=send_sems.at[m - 1], recv_sem=recv_sems.at[m - 1],
                device_id=(px, py, pc), device_id_type=MESH_ID)
            cp.start()
            sent.append(cp)
        for m in range(1, N_DEV):
            px, py, pc = _peer(x, y, c, m)
            peer = 4 * px + 2 * py + pc
            pltpu.make_async_remote_copy(
                src_ref=in_ref, dst_ref=recv.at[peer], send_sem=send_sems.at[m - 1], recv_sem=recv_sems.at[m - 1],
                device_id=(px, py, pc), device_id_type=MESH_ID).wait_recv()
        acc = recv[0]
        for d in range(1, N_DEV):
            acc = acc + recv[d]
        out_ref[...] = acc
        for cp in sent:
            cp.wait_send()

    return _call(
        body, name=name, out_shape=jax.ShapeDtypeStruct(pack.shape, F32), in_specs=[VMEM, ANY], out_specs=VMEM,
        scratch_shapes=[pltpu.VMEM((N_DEV, rows, pack.shape[1]), F32), pltpu.SemaphoreType.DMA((7,)),
                        pltpu.SemaphoreType.DMA((7,))],
        compiler_params=pltpu.CompilerParams(vmem_limit_bytes=VMEM_LIMIT),
    )(pack, after)


POOL_TS = 256


def _pool_counts(first_row, rows, win):
    t = first_row + lax.broadcasted_iota(jnp.int32, (rows, 1), 0)
    return jnp.minimum(t + 1, win).astype(F32)


def _pool_fwd(x, g, w, b, scale):
    nt = S // POOL_TS

    def body(x_ref, g_ref, w_ref, b_ref, s_ref, out_ref, diff_ref, ext):
        i = pl.program_id(0)

        @pl.when(i == 0)
        def _():
            ext[0:MAX_WIN, :] = jnp.zeros((MAX_WIN, D), F32)

        @pl.when(i > 0)
        def _():
            ext[0:MAX_WIN, :] = ext[POOL_TS:POOL_TS + MAX_WIN, :]

        xv = x_ref[...]
        h = xv * _rstd(xv) * g_ref[...]
        ext[MAX_WIN:, :] = h
        for gi in range(N_GROUPS):
            win = 2 << gi
            cols = slice(gi * GROUP, (gi + 1) * GROUP)
            sm = ext[:, cols]
            k = 1
            while k < win:
                sm = sm + pltpu.roll(sm, k, axis=0)
                k *= 2
            pooled = sm[MAX_WIN:, :] / _pool_counts(i * POOL_TS, POOL_TS, win)
            diff = (pooled - h[:, cols]).astype(BF16)
            yv = (_nn(diff, w_ref[gi]) + b_ref[:, cols]) * s_ref[:, cols]
            out_ref[:, cols] = xv[:, cols] + yv
            diff_ref[:, cols] = diff

    row = pl.BlockSpec((1, D), lambda i: (0, 0))
    tile = pl.BlockSpec((POOL_TS, D), lambda i: (i, 0))
    return _call(
        body, name="pool_fwd", grid=(nt,),
        in_specs=[tile, row, pl.BlockSpec((N_GROUPS, GROUP, GROUP), lambda i: (0, 0, 0)), row, row],
        out_specs=[tile, tile],
        out_shape=[jax.ShapeDtypeStruct((S, D), F32), jax.ShapeDtypeStruct((S, D), BF16)],
        scratch_shapes=[pltpu.VMEM((POOL_TS + MAX_WIN, D), F32)],
        compiler_params=_params("arbitrary"),
    )(x, g, w, b, scale)


def _pool_bwd(dy, x, diff, g, w, b, scale, ex=None):
    nt = S // POOL_TS

    def body(dy_ref, x_ref, diff_ref, g_ref, w_ref, b_ref, s_ref, gx_ref, dw_ref, db_ref, ds_ref, dg_ref, ext, dh):
        i = pl.program_id(0)
        first_row = (nt - 1 - i) * POOL_TS

        @pl.when(i == 0)
        def _():
            ext[POOL_TS:, :] = jnp.zeros((MAX_WIN, D), F32)
            dw_ref[...] = jnp.zeros(dw_ref.shape, F32)
            db_ref[...] = jnp.zeros(db_ref.shape, F32)
            ds_ref[...] = jnp.zeros(ds_ref.shape, F32)
            dg_ref[...] = jnp.zeros(dg_ref.shape, F32)

        @pl.when(i > 0)
        def _():
            ext[POOL_TS:, :] = ext[0:MAX_WIN, :]

        dyv = dy_ref[...]
        for gi in range(N_GROUPS):
            win = 2 << gi
            cols = slice(gi * GROUP, (gi + 1) * GROUP)
            dfb = diff_ref[:, cols]
            z = _nn(dfb, w_ref[gi]) + b_ref[:, cols]
            dyg = dyv[:, cols]
            ds_ref[:, cols] += _colsum8(dyg * z)
            dz = dyg * s_ref[:, cols]
            db_ref[:, cols] += _colsum8(dz)
            dzb = dz.astype(BF16)
            dw_ref[gi] += _tn(dfb, dzb)
            ddiff = _nt(dzb, w_ref[gi])
            ext[0:POOL_TS, cols] = ddiff / _pool_counts(first_row, POOL_TS, win)
            sm = ext[:, cols]
            k = 1
            while k < win:
                sm = sm + pltpu.roll(sm, POOL_TS + MAX_WIN - k, axis=0)
                k *= 2
            dh[:, cols] = sm[0:POOL_TS, :] - ddiff
        xv = x_ref[...]
        r = _rstd(xv)
        gv = g_ref[...]
        dhv = dh[...]
        dg_ref[...] += _colsum8(dhv * xv * r)
        gx_ref[...] = dyv + _rms_bwd(dhv, xv, r, gv)

    row = pl.BlockSpec((1, D), lambda i: (0, 0))
    tile = pl.BlockSpec((POOL_TS, D), lambda i: (nt - 1 - i, 0))
    acc = pl.BlockSpec((8, D), lambda i: (0, 0))
    wspec = pl.BlockSpec((N_GROUPS, GROUP, GROUP), lambda i: (0, 0, 0))
    return _call_hosting(
        body, ex, _grid_phases((nt,)), name="pool_bwd", grid=(nt,),
        in_specs=[tile, tile, tile, row, wspec, row, row],
        out_specs=[tile, wspec, acc, acc, acc],
        out_shape=[jax.ShapeDtypeStruct((S, D), F32), jax.ShapeDtypeStruct((N_GROUPS, GROUP, GROUP), F32),
                   jax.ShapeDtypeStruct((8, D), F32), jax.ShapeDtypeStruct((8, D), F32),
                   jax.ShapeDtypeStruct((8, D), F32)],
        scratch_shapes=[pltpu.VMEM((POOL_TS + MAX_WIN, D), F32), pltpu.VMEM((POOL_TS, D), F32)],
        args=(dy, x, diff, g, w, b, scale), compiler_params=_params("arbitrary"),
    )


FFN_TS = min(S, 1024)
FFN_TF = 256


def _ffn_fwd(x, g, wg_t, wu_t, wd, name, target=None, ex=None):
    width = wd.shape[0]
    ni, nj = S // FFN_TS, width // FFN_TF
    with_loss = target is not None
    extra = [target] if with_loss else []

    def body(*refs):
        x_ref, g_ref, wg_ref, wu_ref, wd_ref, wd_last_ref = refs[:6]
        if with_loss:
            t_ref, out_ref, loss_ref, h_ref, gg_ref, uu_ref, hs, acc, kept = refs[6:]
        else:
            out_ref, h_ref, gg_ref, uu_ref, hs, acc, kept = refs[6:]
        i, j = pl.program_id(0), pl.program_id(1)

        @pl.when(j == 0)
        def _():
            xv = x_ref[...]
            hb = (xv * _rstd(xv) * g_ref[...]).astype(BF16)
            hs[...] = hb
            h_ref[...] = hb
            acc[...] = jnp.zeros(acc.shape, F32)
            kept[1] = jnp.zeros(kept.shape[1:], BF16)

        prev, cur = (j + 1) % 2, j % 2
        acc[...] += _nn(kept[prev], wd_ref[...])
        hb = hs[...]
        gg = _nt(hb, wg_ref[...])
        uu = _nt(hb, wu_ref[...])
        gg_ref[...] = gg
        uu_ref[...] = uu
        kept[cur] = (gg * jax.nn.sigmoid(gg) * uu).astype(BF16)

        @pl.when(j == nj - 1)
        def _():
            yv = x_ref[...] + acc[...] + _nn(kept[(nj - 1) % 2], wd_last_ref[...])
            if with_loss:
                err = yv - t_ref[...]
                out_ref[...] = err * (1.0 / D)
                part = jnp.sum(err * err) * (0.5 / D)

                @pl.when(i == 0)
                def _():
                    loss_ref[...] = jnp.zeros(loss_ref.shape, F32)

                loss_ref[...] += jnp.broadcast_to(part, loss_ref.shape)
            else:
                out_ref[...] = yv

    xt = pl.BlockSpec((FFN_TS, D), lambda i, j: (i, 0))
    row = pl.BlockSpec((1, D), lambda i, j: (0, 0))
    wt = pl.BlockSpec((FFN_TF, D), lambda i, j: (j, 0))
    wt_kept = pl.BlockSpec((FFN_TF, D), lambda i, j: (jnp.maximum(j - 1, 0), 0))
    wt_last = pl.BlockSpec((FFN_TF, D), lambda i, j: (nj - 1, 0))
    gt = pl.BlockSpec((FFN_TS, FFN_TF), lambda i, j: (i, j))
    in_specs = [xt, row, wt, wt, wt_kept, wt_last] + [xt] * len(extra)
    out_specs = [xt] + ([pl.BlockSpec((8, 128), lambda i, j: (0, 0))] if with_loss else []) + [xt, gt, gt]
    out_shape = ([jax.ShapeDtypeStruct((S, D), F32)] + ([jax.ShapeDtypeStruct((8, 128), F32)] if with_loss else [])
                 + [jax.ShapeDtypeStruct((S, D), BF16), jax.ShapeDtypeStruct((S, width), F32),
                    jax.ShapeDtypeStruct((S, width), F32)])
    args = (x, g, wg_t, wu_t, wd, wd, *extra)
    return _call_hosting(
        body, ex, _grid_phases((ni, nj), 0.8), name=name, grid=(ni, nj),
        in_specs=in_specs, out_specs=out_specs, out_shape=out_shape,
        scratch_shapes=[pltpu.VMEM((FFN_TS, D), BF16), pltpu.VMEM((FFN_TS, D), F32),
                        pltpu.VMEM((2, FFN_TS, FFN_TF), BF16)], args=args,
        compiler_params=_params("arbitrary", "arbitrary"),
    )


def _ffn_bwd_weights(dout, h, gg, uu, wd, name, ex=None):
    width = wd.shape[0]
    nj = width // FFN_TF

    def body(do_ref, h_ref, gg_ref, uu_ref, wd_ref, dg_ref, du_ref, dwg_ref, dwu_ref, dwd_ref, dob, kept):
        j = pl.program_id(0)

        @pl.when(j == 0)
        def _():
            dob[...] = do_ref[...].astype(BF16)
            kept[1] = jnp.zeros(kept.shape[1:], BF16)

        prev, cur = (j + 1) % 2, j % 2
        hb, dov = h_ref[...], dob[...]
        dwg_ref[...] = _tn(kept[prev, 0], hb).astype(BF16)
        dwu_ref[...] = _tn(kept[prev, 1], hb).astype(BF16)
        dwd_ref[...] = _tn(kept[prev, 2], dov).astype(BF16)
        gv, uv = gg_ref[...], uu_ref[...]
        da = _nt(dov, wd_ref[...])
        sg = jax.nn.sigmoid(gv)
        sl = gv * sg
        dub = (da * sl).astype(BF16)
        dgb = (da * uv * (sg * (1.0 + gv * (1.0 - sg)))).astype(BF16)
        dg_ref[...] = dgb
        du_ref[...] = dub
        kept[cur, 0] = dgb
        kept[cur, 1] = dub
        kept[cur, 2] = (sl * uv).astype(BF16)

    once = pl.Buffered(1)
    whole = lambda: pl.BlockSpec((S, D), lambda j, _: (0, 0), pipeline_mode=once)
    this = lambda j: jnp.minimum(j, nj - 1)
    last = lambda j: jnp.maximum(j - 1, 0)
    wt_in = pl.BlockSpec((FFN_TF, D), lambda j, _: (this(j), 0))
    wt_out = pl.BlockSpec((FFN_TF, D), lambda j, _: (last(j), 0))
    gt = pl.BlockSpec((S, FFN_TF), lambda j, _: (0, this(j)))
    return _call_hosting(
        body, ex, _grid_phases((nj + 1, 1)), name=name, grid=(nj + 1, 1),
        in_specs=[whole(), whole(), gt, gt, wt_in], out_specs=[gt, gt, wt_out, wt_out, wt_out],
        out_shape=[jax.ShapeDtypeStruct((S, width), BF16)] * 2 + [jax.ShapeDtypeStruct((width, D), BF16)] * 3,
        scratch_shapes=[pltpu.VMEM((S, D), BF16), pltpu.VMEM((2, 3, S, FFN_TF), BF16)], args=(dout, h, gg, uu, wd),
        compiler_params=_params("arbitrary", "arbitrary"),
    )


BWD_TS = 512


def _ffn_bwd_input(dres, x, g, grads, weights, name, ex=None):
    nt = S // BWD_TS
    n = len(grads)

    def body(*refs):
        dres_ref, x_ref, g_ref = refs[:3]
        grad_refs, w_refs = refs[3:3 + n], refs[3 + n:3 + 2 * n]
        dx_ref, dgam_ref = refs[3 + 2 * n:]
        dh = _nn(grad_refs[0][...], w_refs[0][...])
        for k in range(1, n):
            dh = dh + _nn(grad_refs[k][...], w_refs[k][...])
        xv = x_ref[...]
        r = _rstd(xv)
        dx_ref[...] = dres_ref[...] + _rms_bwd(dh, xv, r, g_ref[...])

        @pl.when(pl.program_id(0) == 0)
        def _():
            dgam_ref[...] = jnp.zeros(dgam_ref.shape, F32)

        dgam_ref[...] += _colsum8(dh * xv * r)

    tile = pl.BlockSpec((BWD_TS, D), lambda i: (i, 0))
    ftiles = [pl.BlockSpec((BWD_TS, a.shape[1]), lambda i: (i, 0)) for a in grads]
    wspecs = [pl.BlockSpec(w.shape, lambda i: (0, 0), pipeline_mode=pl.Buffered(1)) for w in weights]
    return _call_hosting(
        body, ex, _grid_phases((nt,)), name=name, grid=(nt,),
        in_specs=[tile, tile, pl.BlockSpec((1, D), lambda i: (0, 0))] + ftiles + wspecs,
        out_specs=[tile, pl.BlockSpec((8, D), lambda i: (0, 0))],
        out_shape=[jax.ShapeDtypeStruct((S, D), F32), jax.ShapeDtypeStruct((8, D), F32)],
        scratch_shapes=[], args=(dres, x, g, *grads, *weights), compiler_params=_params("arbitrary"),
    )


def _mm(a, b, mode, out_dtype, name, add=None, skip_rows=0):
    if mode == "nn":
        (m, kd), n = (a.shape[0] - skip_rows, a.shape[1]), b.shape[1]
    elif mode == "nt":
        (m, kd), n = (a.shape[0] - skip_rows, a.shape[1]), b.shape[0]
    else:
        (kd, m), n = a.shape, b.shape[1]
    tm, tn, tk = min(m, 1024), min(n, 1024), min(kd, 1024)
    if skip_rows and mode == "tn":
        tk = min(tk, skip_rows)
    elif skip_rows:
        tm = min(tm, skip_rows)
    skip_a = skip_rows // tm if mode != "tn" else 0
    skip_b = skip_rows // tk if mode == "tn" else 0
    assert skip_rows == skip_a * tm + skip_b * tk
    nk = kd // tk
    dot = {"nn": _nn, "nt": _nt, "tn": _tn}[mode]

    def body(*refs):
        if add is None:
            a_ref, b_ref, o_ref, acc = refs
        else:
            a_ref, b_ref, add_ref, o_ref, acc = refs
        k = pl.program_id(2)

        @pl.when(k == 0)
        def _():
            acc[...] = jnp.zeros(acc.shape, F32)

        acc[...] += dot(a_ref[...].astype(BF16), b_ref[...].astype(BF16))

        @pl.when(k == nk - 1)
        def _():
            res = acc[...]
            if add is not None:
                res = res + add_ref[...]
            o_ref[...] = res.astype(out_dtype)

    if mode == "tn":
        a_spec = pl.BlockSpec((tk, tm), lambda i, j, k: (k, i))
        b_spec = pl.BlockSpec((tk, tn), lambda i, j, k: (k + skip_b, j))
    else:
        a_spec = pl.BlockSpec((tm, tk), lambda i, j, k: (i + skip_a, k))
        b_spec = (pl.BlockSpec((tk, tn), lambda i, j, k: (k, j)) if mode == "nn"
                  else pl.BlockSpec((tn, tk), lambda i, j, k: (j, k)))
    o_spec = pl.BlockSpec((tm, tn), lambda i, j, k: (i, j))
    in_specs = [a_spec, b_spec] + ([o_spec] if add is not None else [])
    args = (a, b) + ((add,) if add is not None else ())
    return _call(
        body, name=name, grid=(m // tm, n // tn, nk), in_specs=in_specs, out_specs=o_spec,
        out_shape=jax.ShapeDtypeStruct((m, n), out_dtype), scratch_shapes=[pltpu.VMEM((tm, tn), F32)],
        compiler_params=_params("parallel", "parallel", "arbitrary"),
    )(*args)


PROJ_TS = 256


def _kvq_proj(x, g_kv, g_mix, wk, wv, wq, gk, gq, ex=None):
    lead = PADK // PROJ_TS

    def body(x_ref, gkv_ref, gmix_ref, wk_ref, wv_ref, wq_ref, gk_ref, gq_ref,
             hkv_ref, h1_ref, kpre_ref, qpre_ref, k_ref, v_ref, q_ref):
        i = pl.program_id(0)

        @pl.when(i < lead)
        def _():
            k_ref[...] = jnp.zeros(k_ref.shape, BF16)
            v_ref[...] = jnp.zeros(v_ref.shape, BF16)

        @pl.when(i >= lead)
        def _():
            xv = x_ref[...]
            xr = xv * _rstd(xv)
            hkv = (xr * gkv_ref[...]).astype(BF16)
            h1 = (xr * gmix_ref[...]).astype(BF16)
            hkv_ref[...] = hkv
            h1_ref[...] = h1
            kpre = _nn(hkv, wk_ref[...])
            qpre = _nn(h1, wq_ref[...])
            kpre_ref[...] = kpre
            qpre_ref[...] = qpre
            v_ref[...] = _nn(hkv, wv_ref[...]).astype(BF16)
            rk = lax.rsqrt(_seg_sum(kpre * kpre) * (1.0 / HEAD) + EPS)
            k_ref[...] = (kpre * rk * gk_ref[...]).astype(BF16)
            rq = lax.rsqrt(_seg_sum(qpre * qpre) * (1.0 / HEAD) + EPS)
            q_ref[...] = (qpre * rq * gq_ref[...]).astype(BF16)

    tile = pl.BlockSpec((PROJ_TS, D), lambda i: (jnp.maximum(i - lead, 0), 0))
    padded = pl.BlockSpec((PROJ_TS, D), lambda i: (i, 0))
    row = pl.BlockSpec((1, D), lambda i: (0, 0))
    wspec = pl.BlockSpec((D, D), lambda i: (0, 0))
    bf = jax.ShapeDtypeStruct((S, D), BF16)
    ff = jax.ShapeDtypeStruct((S, D), F32)
    bp = jax.ShapeDtypeStruct((PADK + S, D), BF16)
    return _call_hosting(
        body, ex, _grid_phases((lead + S // PROJ_TS,), 0.85), name="kvq_proj", grid=(lead + S // PROJ_TS,),
        in_specs=[tile, row, row, wspec, wspec, wspec, row, row],
        out_specs=[tile, tile, tile, tile, padded, padded, tile], out_shape=[bf, bf, ff, ff, bp, bp, bf],
        scratch_shapes=[], args=(x, g_kv, g_mix, wk, wv, wq, gk, gq), compiler_params=_params("arbitrary"),
    )


def _qkv_input_bwd(dq, qpre, gq, dkp, kpre, gk, dvp, wq, wk, wv, dres, x, g_mix, g_kv, ex=None):
    nt = S // PROJ_TS
    lead = PADK // PROJ_TS

    def head_bwd(dov, pv, hgv):
        r = lax.rsqrt(_seg_sum(pv * pv) * (1.0 / HEAD) + EPS)
        gd = dov * hgv
        dpre = r * gd - pv * (r * r * r) * (_seg_sum(gd * pv) * (1.0 / HEAD))
        return dpre.astype(BF16), _colsum8(dov * pv * r)

    def fold_heads(full):
        fold = full[:, 0:128]
        for blk in range(1, D // 128):
            fold = fold + full[:, blk * 128:(blk + 1) * 128]
        return fold + pltpu.roll(fold, HEAD, axis=1)

    def body(dq_ref, qpre_ref, gq_ref, dk_ref, kpre_ref, gk_ref, dv_ref, wq_ref, wk_ref, wv_ref, dres_ref, x_ref,
             gmix_ref, gkv_ref, dx_ref, dqpre_ref, dkpre_ref, dgmix_ref, dgkv_ref, dgq_ref, dgk_ref, accq, acck):
        i = pl.program_id(0)

        @pl.when(i == 0)
        def _():
            accq[...] = jnp.zeros(accq.shape, F32)
            acck[...] = jnp.zeros(acck.shape, F32)
            dgmix_ref[...] = jnp.zeros(dgmix_ref.shape, F32)
            dgkv_ref[...] = jnp.zeros(dgkv_ref.shape, F32)

        dqb, cq = head_bwd(dq_ref[...], qpre_ref[...], gq_ref[...])
        dkb, ck = head_bwd(dk_ref[...], kpre_ref[...], gk_ref[...])
        dqpre_ref[...] = dqb
        dkpre_ref[...] = dkb
        accq[...] += cq
        acck[...] += ck
        dh1 = _nt(dqb, wq_ref[...])
        dhkv = _nt(dkb, wk_ref[...]) + _nt(dv_ref[...].astype(BF16), wv_ref[...])
        xv = x_ref[...]
        r = _rstd(xv)
        dx_ref[...] = dres_ref[...] + _rms_bwd(dh1, xv, r, gmix_ref[...]) + _rms_bwd(dhkv, xv, r, gkv_ref[...])
        dgmix_ref[...] += _colsum8(dh1 * xv * r)
        dgkv_ref[...] += _colsum8(dhkv * xv * r)

        @pl.when(i == nt - 1)
        def _():
            dgq_ref[...] = fold_heads(accq[...])
            dgk_ref[...] = fold_heads(acck[...])

    tile = pl.BlockSpec((PROJ_TS, D), lambda i: (i, 0))
    behind = pl.BlockSpec((PROJ_TS, D), lambda i: (i + lead, 0))
    row = pl.BlockSpec((1, D), lambda i: (0, 0))
    wspec = lambda: pl.BlockSpec((D, D), lambda i: (0, 0), pipeline_mode=pl.Buffered(1))
    acc = pl.BlockSpec((8, D), lambda i: (0, 0))
    small = pl.BlockSpec((8, 128), lambda i: (0, 0))
    bf = jax.ShapeDtypeStruct((S, D), BF16)
    return _call_hosting(
        body, ex, _grid_phases((nt,)), name="qkv_input_bwd", grid=(nt,),
        in_specs=[tile, tile, row, behind, tile, row, behind, wspec(), wspec(), wspec(), tile, tile, row, row],
        out_specs=[tile, tile, tile, acc, acc, small, small],
        out_shape=[jax.ShapeDtypeStruct((S, D), F32), bf, bf, jax.ShapeDtypeStruct((8, D), F32),
                   jax.ShapeDtypeStruct((8, D), F32), jax.ShapeDtypeStruct((8, 128), F32),
                   jax.ShapeDtypeStruct((8, 128), F32)],
        scratch_shapes=[pltpu.VMEM((8, D), F32), pltpu.VMEM((8, D), F32)],
        args=(dq, qpre, gq, dkp, kpre, gk, dvp, wq, wk, wv, dres, x, g_mix, g_kv),
        compiler_params=_params("arbitrary"),
    )


def _toeplitz_from_table(table):
    far = jnp.broadcast_to(table[:, N_REL - 1:], (N_HEADS, PADK - MAX_REL + 1))
    near = table[:, N_REL - 2::-1]
    past = jnp.broadcast_to(table[:, 0:1], (N_HEADS, MAX_REL))
    wrap = jnp.broadcast_to(table[:, N_REL - 1:], (N_HEADS, TOEP - (PADK + 2 * MAX_REL + 1)))
    return jnp.concatenate([far, near, past, wrap], axis=1).reshape(N_HEADS, 1, TOEP)


def _table_grad_from_toeplitz(dtp, seg):
    lo = PADK - MAX_REL + 1
    near = dtp[:, lo + N_REL - 3:lo - 1:-1]
    return jnp.concatenate([seg[:, 1:2], near, seg[:, 0:1]], axis=1)


def _bias_band(tp):
    def body(tp_ref, out_ref):
        bv = pltpu.roll(jnp.broadcast_to(tp_ref[0], (QB, TOEP)), 0, axis=1, stride=1, stride_axis=0)
        out_ref[0] = jnp.where(_band_mask(), bv[:, 0:KB], NEG_INF)

    return _call(
        body, name="bias_band", grid=(N_HEADS,),
        in_specs=[pl.BlockSpec((1, 1, TOEP), lambda h: (h, 0, 0))],
        out_specs=pl.BlockSpec((1, QB, KB), lambda h: (h, 0, 0)),
        out_shape=jax.ShapeDtypeStruct((N_HEADS, QB, KB), F32),
        compiler_params=_params("parallel"),
    )(tp)


def _bias_grad(dband, after):
    lo, hi = PADK - MAX_REL + 1, PADK + MAX_REL

    def body(db_ref, _, dtp_ref, seg_ref):
        bv = jnp.concatenate([db_ref[0], jnp.zeros((QB, TOEP - KB), F32)], axis=1)
        row = lax.broadcasted_iota(jnp.int32, (QB, TOEP), 0)
        k = 1
        while k < QB:
            bv = jnp.where((row & k) != 0, pltpu.roll(bv, TOEP - k, axis=1), bv)
            k *= 2
        col = jnp.sum(bv, axis=0, keepdims=True)
        dtp_ref[0] = col
        u = lax.broadcasted_iota(jnp.int32, (1, TOEP), 1)
        far = jnp.sum(jnp.where((u < lo) | (u > hi + MAX_REL), col, 0.0))
        past = jnp.sum(jnp.where((u >= hi) & (u <= hi + MAX_REL), col, 0.0))
        lane = lax.broadcasted_iota(jnp.int32, (1, 128), 1)
        seg_ref[0] = jnp.where(lane == 0, far, jnp.where(lane == 1, past, 0.0))

    return _call(
        body, name="bias_grad", grid=(N_HEADS,),
        in_specs=[pl.BlockSpec((1, QB, KB), lambda h: (h, 0, 0)), ANY],
        out_specs=[pl.BlockSpec((1, 1, TOEP), lambda h: (h, 0, 0)), pl.BlockSpec((1, 1, 128), lambda h: (h, 0, 0))],
        out_shape=[jax.ShapeDtypeStruct((N_HEADS, 1, TOEP), F32), jax.ShapeDtypeStruct((N_HEADS, 1, 128), F32)],
        compiler_params=_params("parallel"),
    )(dband, after)


N_QB = S // QB
HEADS_PER_STEP = 4
ATT_LANES = HEADS_PER_STEP * HEAD
N_HG = D // ATT_LANES


def _band_mask():
    qc = lax.broadcasted_iota(jnp.int32, (QB, KB), 0) // CHUNK
    kc = lax.broadcasted_iota(jnp.int32, (QB, KB), 1) // CHUNK
    return (kc >= qc) & (kc <= qc + LEFT)


def _half_scale(hh, scale):
    lane = lax.broadcasted_iota(jnp.int32, (1, 128), 1)
    return jnp.where((lane < HEAD) == (hh == 0), scale, 0.0).astype(BF16)


def _probs(qh, kb, bias, first_key):
    sc = _nt(qh, kb) + bias
    if first_key is not None:
        sc = jnp.where(lax.broadcasted_iota(jnp.int32, (QB, KB), 1) >= first_key, sc, NEG_INF)
    e = jnp.exp(sc - jnp.max(sc, axis=-1, keepdims=True))
    return e * (1.0 / jnp.sum(e, axis=-1, keepdims=True))


def _by_padding(cb, compute):
    @pl.when(cb < PADK // QB)
    def _():
        compute(PADK - cb * QB)

    @pl.when(cb >= PADK // QB)
    def _():
        compute(None)


def _attn_fwd(q, kp, vp, bias, ex=None):
    def body(q_ref, k_ref, v_ref, b_ref, o_ref):
        cb = pl.program_id(1)
        band = pl.ds(pl.multiple_of(cb * QB, QB), KB)
        low = lax.broadcasted_iota(jnp.int32, (QB, 128), 1) < HEAD

        def compute(first_key):
            for pair in range(HEADS_PER_STEP // 2):
                lanes = pl.ds(pair * 128, 128)
                kb, vb, qv = k_ref[band, lanes], v_ref[band, lanes], q_ref[:, lanes]
                outs = []
                for hh in range(2):
                    pb = _probs(qv * _half_scale(hh, ATTN_SCALE), kb, b_ref[2 * pair + hh], first_key).astype(BF16)
                    outs.append(_nn(pb, vb))
                o_ref[:, lanes] = jnp.where(low, outs[0], outs[1]).astype(BF16)

        _by_padding(cb, compute)

    qspec = pl.BlockSpec((QB, ATT_LANES), lambda hg, cb: (cb, hg))
    kspec = pl.BlockSpec((PADK + S, ATT_LANES), lambda hg, cb: (0, hg))
    return _call_hosting(
        body, ex, _grid_phases((N_HG, N_QB), 0.85), name="attn_fwd", grid=(N_HG, N_QB),
        in_specs=[qspec, kspec, kspec, pl.BlockSpec((HEADS_PER_STEP, QB, KB), lambda hg, cb: (hg, 0, 0))],
        out_specs=[qspec], out_shape=[jax.ShapeDtypeStruct((S, D), BF16)], scratch_shapes=[],
        args=(q, kp, vp, bias), compiler_params=_params("arbitrary", "arbitrary"),
    )


def _attn_bwd(q, kp, vp, bias, do, ex=None):
    def body(q_ref, k_ref, v_ref, b_ref, do_ref, dq_ref, dk_ref, dv_ref, db_ref):
        cb = pl.program_id(1)

        @pl.when(cb == 0)
        def _():
            dk_ref[...] = jnp.zeros(dk_ref.shape, F32)
            dv_ref[...] = jnp.zeros(dv_ref.shape, F32)
            db_ref[...] = jnp.zeros(db_ref.shape, F32)

        band = pl.ds(pl.multiple_of(cb * QB, QB), KB)
        low = lax.broadcasted_iota(jnp.int32, (QB, 128), 1) < HEAD

        def compute(first_key):
            for pair in range(HEADS_PER_STEP // 2):
                lanes = pl.ds(pair * 128, 128)
                kb, vb = k_ref[band, lanes], v_ref[band, lanes]
                qv, dov = q_ref[:, lanes], do_ref[:, lanes]
                dq = jnp.zeros((QB, 128), F32)
                dkb = jnp.zeros((KB, 128), F32)
                dvb = jnp.zeros((KB, 128), F32)
                for hh in range(2):
                    sel = low if hh == 0 else jnp.logical_not(low)
                    doh = dov * _half_scale(hh, 1.0)
                    p = _probs(qv * _half_scale(hh, ATTN_SCALE), kb, b_ref[2 * pair + hh], first_key)
                    dp = _nt(doh, vb)
                    dvb = dvb + _tn(p.astype(BF16), doh)
                    ds = p * (dp - jnp.sum(dp * p, axis=-1, keepdims=True))
                    db_ref[2 * pair + hh] += ds
                    dsb = (ds * ATTN_SCALE).astype(BF16)
                    dq = dq + jnp.where(sel, _nn(dsb, kb), 0.0)
                    dkb = dkb + _tn(dsb, qv * _half_scale(hh, 1.0))
                dq_ref[:, lanes] = dq
                dk_ref[band, lanes] += dkb
                dv_ref[band, lanes] += dvb

        _by_padding(cb, compute)

    qspec = pl.BlockSpec((QB, ATT_LANES), lambda hg, cb: (cb, hg))
    kspec = pl.BlockSpec((PADK + S, ATT_LANES), lambda hg, cb: (0, hg))
    bspec = pl.BlockSpec((HEADS_PER_STEP, QB, KB), lambda hg, cb: (hg, 0, 0))
    kf = jax.ShapeDtypeStruct((PADK + S, D), F32)
    return _call_hosting(
        body, ex, _grid_phases((N_HG, N_QB)), name="attn_bwd", grid=(N_HG, N_QB),
        in_specs=[qspec, kspec, kspec, bspec, qspec],
        out_specs=[qspec, kspec, kspec, bspec],
        out_shape=[jax.ShapeDtypeStruct((S, D), F32), kf, kf, jax.ShapeDtypeStruct((N_HEADS, QB, KB), F32)],
        scratch_shapes=[], args=(q, kp, vp, bias, do), compiler_params=_params("arbitrary", "arbitrary"),
    )


def _adam_math(w, g, m, v):
    m = ADAM_B1 * m + (1.0 - ADAM_B1) * g
    v = ADAM_B2 * v + (1.0 - ADAM_B2) * (g * g)
    m_hat = m / (1.0 - ADAM_B1 ** ADAM_STEP)
    v_hat = v / (1.0 - ADAM_B2 ** ADAM_STEP)
    delta = -ADAM_LR * (m_hat / (jnp.sqrt(v_hat) + ADAM_EPS) + ADAM_WD * w)
    return delta, m, v


def _sum_parts(parts, name, after=None):
    count, r, c = parts.shape
    tr = r // 2 if r % 16 == 0 and r > 64 else r

    def body(p_ref, *rest):
        o_ref = rest[-1]
        acc = p_ref[0].astype(F32)
        for d in range(1, count):
            acc = acc + p_ref[d].astype(F32)
        o_ref[...] = acc

    return _call(
        body, name=name, grid=(r // tr,),
        in_specs=[pl.BlockSpec((count, tr, c), lambda i: (0, i, 0))] + ([ANY] if after is not None else []),
        out_specs=pl.BlockSpec((tr, c), lambda i: (i, 0)),
        out_shape=jax.ShapeDtypeStruct((r, c), F32), compiler_params=_params("parallel"),
    )(parts, *([after] if after is not None else []))


def _row_tile(r):
    for cand in (256, 176, 128, 64, 32, 16, 8):
        if r % cand == 0:
            return cand
    return r


def _adam_layer(w, g, m, v, layer, name, prev=None):
    nl, r, c = w.shape
    tr = _row_tile(r)

    def body(w_ref, g_ref, m_ref, v_ref, *rest):
        go_ref, d_ref, nm_ref, nv_ref = rest[-4:]
        gv = g_ref[...]
        delta, nm, nv = _adam_math(w_ref[0], gv, m_ref[0], v_ref[0])
        go_ref[0] = gv
        d_ref[0] = delta
        nm_ref[0] = nm
        nv_ref[0] = nv

    lspec = pl.BlockSpec((1, tr, c), lambda i: (layer, i, 0))
    sd = jax.ShapeDtypeStruct((nl, r, c), F32)
    extra = list(prev) if prev is not None else []
    return _call(
        body, name=name, grid=(r // tr,),
        in_specs=[lspec, pl.BlockSpec((tr, c), lambda i: (i, 0)), lspec, lspec] + [ANY] * len(extra),
        out_specs=[lspec] * 4, out_shape=[sd] * 4,
        input_output_aliases={4 + t: t for t in range(len(extra))},
        compiler_params=_params("parallel"),
    )(w, g, m, v, *extra)


def _adam(w, g, m, v, name):
    r, c = w.shape
    tr = _row_tile(r)

    def body(w_ref, g_ref, m_ref, v_ref, d_ref, nm_ref, nv_ref):
        delta, nm, nv = _adam_math(w_ref[...], g_ref[...], m_ref[...], v_ref[...])
        d_ref[...] = delta
        nm_ref[...] = nm
        nv_ref[...] = nv

    spec = pl.BlockSpec((tr, c), lambda i: (i, 0))
    sd = jax.ShapeDtypeStruct((r, c), F32)
    return _call(
        body, name=name, grid=(r // tr,), in_specs=[spec] * 4, out_specs=[spec] * 3, out_shape=[sd] * 3,
        compiler_params=_params("parallel"),
    )(w, g, m, v)


def _pad16(rows):
    return jnp.pad(rows, ((0, 16 - rows.shape[0]), (0, 0)))


def kernel(x, norm_mix_g, norm_ffn_g, pool_w, pool_b, pool_scale, kv_norm_g, w_k, w_v, k_norm_g, w_q, q_norm_g, rel_bias, w_o, w_gate, w_up, w_down, loss_target, m_norm_mix_g, m_norm_ffn_g, m_pool_w, m_pool_b, m_pool_scale, m_kv_norm_g, m_w_k, m_w_v, m_k_norm_g, m_w_q, m_q_norm_g, m_rel_bias, m_w_o, m_w_gate, m_w_up, m_w_down, v_norm_mix_g, v_norm_ffn_g, v_pool_w, v_pool_b, v_pool_scale, v_kv_norm_g, v_w_k, v_w_v, v_k_norm_g, v_w_q, v_q_norm_g, v_rel_bias, v_w_o, v_w_gate, v_w_up, v_w_down):
    assert x.shape == (1, S, D) and w_gate.shape == (2, D, F_SHARD) and w_k.shape == (D_SHARD, D)
    xin, target = x[0], loss_target[0]

    ffn_shards = [[w_gate[layer].T.astype(BF16), w_up[layer].T.astype(BF16), w_down[layer].astype(BF16)]
                  for layer in range(2)]
    att_shards = [w_k.astype(BF16), w_v.astype(BF16), w_q[0].astype(BF16), w_o[0].astype(BF16)]
    pool_shard = pool_w[0].astype(BF16).reshape(N_GROUPS * POOL_SHARD, GROUP)
    small = jnp.concatenate([pool_b[0].reshape(1, N_GROUPS * POOL_SHARD), pool_scale], axis=1)

    full0 = _run_exchange(_relay_gather_exchange(ffn_shards[0] + [pool_shard, _pad16(small)]), "gather_layer0")
    ffn_w0 = [a.reshape(F, D) for a in full0[:3]]
    pw_f = full0[3].reshape(N_DEV, N_GROUPS, POOL_SHARD, GROUP).transpose(1, 0, 2, 3).reshape(N_GROUPS, GROUP, GROUP)
    small_f = full0[4][:, 0, :]
    pb_f = small_f[:, :N_GROUPS * POOL_SHARD].reshape(N_DEV, N_GROUPS, POOL_SHARD).transpose(1, 0, 2).reshape(1, D)
    ps_f = small_f[:, N_GROUPS * POOL_SHARD:].reshape(1, D)

    g_mix0, g_mix1 = norm_mix_g[0:1], norm_mix_g[1:2]
    g_ffn0, g_ffn1 = norm_ffn_g[0:1], norm_ffn_g[1:2]
    g_kv = kv_norm_g.reshape(1, D)
    gk_t = jnp.tile(k_norm_g.reshape(1, HEAD), (1, N_HEADS))
    gq_t = jnp.tile(q_norm_g.reshape(1, HEAD), (1, N_HEADS))

    x1, diff = _pool_fwd(xin, g_mix0, pw_f, pb_f, ps_f)
    (x2, hf0, gg0, uu0), full_att = _ffn_fwd(x1, g_ffn0, *ffn_w0, name="ffn_fwd", ex=_relay_gather_exchange(att_shards))
    wk_f, wv_f, wq_f, wo_f = [a.reshape(D, D) for a in full_att]
    (hkv, h1, kpre, qpre, kp, vp, qq), full1_gate = _kvq_proj(x2, g_kv, g_mix1, wk_f, wv_f, wq_f, gk_t, gq_t,
                                                                ex=_relay_gather_exchange(ffn_shards[1][:1]))
    bias = _bias_band(_toeplitz_from_table(rel_bias[0]))
    (att,), full1_rest = _attn_fwd(qq, kp, vp, bias, ex=_relay_gather_exchange(ffn_shards[1][1:]))
    ffn_w1 = [a.reshape(F, D) for a in full1_gate + full1_rest]
    x3 = _mm(att, wo_f, "nn", F32, "attn_out", add=x2)
    (dx4, loss_rows, hf1, gg1, uu1), _ = _ffn_fwd(x3, g_ffn1, *ffn_w1, name="ffn_fwd_loss", target=target)

    def blocks(dw):
        return dw.reshape(N_DEV, dw.shape[0] // N_DEV, dw.shape[1])

    (dgg1, duu1, dwg1, dwu1, dwd1), _ = _ffn_bwd_weights(dx4, hf1, gg1, uu1, ffn_w1[2], name="ffn_bwd1")
    parts1 = [blocks(dw) for dw in (dwg1, dwu1, dwd1)]
    (dx3, dg_ffn1), stage1 = _ffn_bwd_input(dx4, x3, g_ffn1, [dgg1, duu1], ffn_w1[:2], "ffn_dx1",
                                            ex=_pair_exchange(parts1))
    both1 = _pair_add(parts1, stage1, "pair_add_ffn1", copies=2)
    flight1, chip1, land1, _ = _chip_exchange_start(both1[:3], both1[3:], dx3, "scatter_ffn1_start")
    datt = _mm(dx3, wo_f, "nt", BF16, "d_attn")
    dwo = _mm(att, dx3, "tn", BF16, "d_wo")
    (dq, dkp, dvp, dband), _ = _attn_bwd(qq, kp, vp, bias, datt)
    (dx2, dqpre, dkpre, dg_mix1, dg_kv, dgq, dgk), _ = _qkv_input_bwd(
        dq, qpre, gq_t, dkp, kpre, gk_t, dvp, wq_f, wk_f, wv_f, dx3, x2, g_mix1, g_kv)
    dwq = _mm(h1, dqpre, "tn", BF16, "d_wq")
    dwk = _mm(hkv, dkpre, "tn", BF16, "d_wk")
    dwv = _mm(hkv, dvp, "tn", BF16, "d_wv", skip_rows=PADK)
    parts_att = [blocks(dw) for dw in (dwk, dwv, dwq, dwo)]
    (dgg0, duu0, *dw0), stage_att = _ffn_bwd_weights(
        dx2, hf0, gg0, uu0, ffn_w0[2], name="ffn_bwd0", ex=_pair_exchange(parts_att))
    both_att = _pair_add(parts_att, stage_att, "pair_add_att", copies=2)
    flight_att, chip_att, land_att, _ = _chip_exchange_start(both_att[:4], both_att[4:], dx2, "scatter_att_start")
    parts0 = [blocks(dw) for dw in dw0]
    (dx1, dg_ffn0), stage0 = _ffn_bwd_input(dx2, x1, g_ffn0, [dgg0, duu0], ffn_w0[:2], "ffn_dx0",
                                            ex=_pair_exchange(parts0))
    both = _pair_add(parts0, stage0, "pair_add_ffn0", copies=2)
    flight, chip0, land0, token = _chip_exchange_start(both[:3], both[3:], dx1, "scatter_ffn0_start")
    (grad_x, dpw, db_rows, ds_rows, dg_mix0), _ = _pool_bwd(dx1, xin, diff, g_mix0 + token[0:1, 0:1], pw_f, pb_f, ps_f)
    dtp, seg = _bias_grad(dband, dpw)

    weights = dict(norm_mix_g=norm_mix_g, norm_ffn_g=norm_ffn_g, pool_w=pool_w, pool_b=pool_b,
                   pool_scale=pool_scale, kv_norm_g=kv_norm_g, w_k=w_k, w_v=w_v, k_norm_g=k_norm_g, w_q=w_q,
                   q_norm_g=q_norm_g, rel_bias=rel_bias, w_o=w_o, w_gate=w_gate, w_up=w_up, w_down=w_down)
    mom1 = dict(norm_mix_g=m_norm_mix_g, norm_ffn_g=m_norm_ffn_g, pool_w=m_pool_w, pool_b=m_pool_b,
                pool_scale=m_pool_scale, kv_norm_g=m_kv_norm_g, w_k=m_w_k, w_v=m_w_v, k_norm_g=m_k_norm_g,
                w_q=m_w_q, q_norm_g=m_q_norm_g, rel_bias=m_rel_bias, w_o=m_w_o, w_gate=m_w_gate, w_up=m_w_up,
                w_down=m_w_down)
    mom2 = dict(norm_mix_g=v_norm_mix_g, norm_ffn_g=v_norm_ffn_g, pool_w=v_pool_w, pool_b=v_pool_b,
                pool_scale=v_pool_scale, kv_norm_g=v_kv_norm_g, w_k=v_w_k, w_v=v_w_v, k_norm_g=v_k_norm_g,
                w_q=v_w_q, q_norm_g=v_q_norm_g, rel_bias=v_rel_bias, w_o=v_w_o, w_gate=v_w_gate, w_up=v_w_up,
                w_down=v_w_down)
    names = list(weights)
    grads, deltas, new_m, new_v = {}, {}, {}, {}

    def summed(arrs, tag, after):
        return [_sum_parts(a, "sum_parts_%s%d" % (tag, k), after) for k, a in enumerate(arrs)]

    def adam_flat(nm):
        shape = weights[nm].shape
        flat = lambda a: a.reshape(-1, shape[-1])
        dl, m1, m2 = _adam(flat(weights[nm]), flat(grads[nm]), flat(mom1[nm]), flat(mom2[nm]), "adam_" + nm)
        deltas[nm], new_m[nm], new_v[nm] = dl.reshape(shape), m1.reshape(shape), m2.reshape(shape)

    ffn_names = ("w_gate", "w_up", "w_down")
    def ffn_view(nm, a):
        return a if nm == "w_down" else a.transpose(0, 2, 1)

    recv1 = _chip_exchange_wait(flight1, chip1, land1, dtp, "scatter_ffn1_wait")
    sums1 = summed(recv1, "ffn1_", token)
    layer1 = {nm: _adam_layer(ffn_view(nm, weights[nm]), g, ffn_view(nm, mom1[nm]), ffn_view(nm, mom2[nm]), 1,
                              "adam1_" + nm)
              for nm, g in zip(ffn_names, sums1)}
    recv_att = _chip_exchange_wait(flight_att, chip_att, land_att, layer1["w_down"][1], "scatter_att_wait")
    g_wk, g_wv, g_wq, g_wo = summed(recv_att, "att_", token)
    grads.update(w_k=g_wk, w_v=g_wv, w_q=g_wq[None], w_o=g_wo[None])
    for nm in ("w_k", "w_v", "w_q", "w_o"):
        adam_flat(nm)

    dpw_blocks = dpw.reshape(N_GROUPS, N_DEV, POOL_SHARD, GROUP).transpose(1, 0, 2, 3)
    dpw_blocks = dpw_blocks.reshape(N_DEV, N_GROUPS * POOL_SHARD, GROUP).astype(BF16)
    recv_pool = _run_exchange(_scatter_exchange([dpw_blocks]), "scatter_pool", after=deltas["w_o"])
    misc = jnp.concatenate([dgk[0:1, 0:HEAD], dgq[0:1, 0:HEAD], loss_rows[0:1, 0:1],
                            seg[:, 0, 0].reshape(1, N_HEADS), seg[:, 0, 1].reshape(1, N_HEADS)], axis=1)
    misc = jnp.pad(misc, ((0, 0), (0, D - misc.shape[1])))
    vec_rows = jnp.concatenate([dg_mix0[0:1], dg_mix1[0:1], dg_ffn0[0:1], dg_ffn1[0:1], dg_kv[0:1],
                                db_rows[0:1], ds_rows[0:1], misc], axis=0)
    pack = jnp.concatenate([vec_rows, dtp.reshape(N_HEADS, TOEP)], axis=0)
    tot = _all_reduce_small(pack, "reduce_small", after=recv_pool[0])

    loss = tot[7, 2 * HEAD]
    seg_tot = jnp.stack([tot[7, 2 * HEAD + 1:2 * HEAD + 1 + N_HEADS],
                         tot[7, 2 * HEAD + 1 + N_HEADS:2 * HEAD + 1 + 2 * N_HEADS]], axis=1)
    me = 4 * lax.axis_index("x") + 2 * lax.axis_index("y") + lax.axis_index("c")
    g_pool_b = lax.dynamic_slice_in_dim(tot[5].reshape(N_GROUPS, GROUP), me * POOL_SHARD, POOL_SHARD, axis=1)
    grads.update(
        norm_mix_g=tot[0:2], norm_ffn_g=tot[2:4], kv_norm_g=tot[4], k_norm_g=tot[7, 0:HEAD],
        q_norm_g=tot[7, HEAD:2 * HEAD].reshape(1, HEAD),
        rel_bias=_table_grad_from_toeplitz(tot[8:8 + N_HEADS], seg_tot).reshape(1, N_HEADS, N_REL),
        pool_b=g_pool_b.reshape(1, N_GROUPS, POOL_SHARD),
        pool_scale=lax.dynamic_slice_in_dim(tot[6:7], me * D_SHARD, D_SHARD, axis=1),
        pool_w=summed(recv_pool, "pool_", None)[0].reshape(1, N_GROUPS, POOL_SHARD, GROUP))
    adam_flat("pool_w")
    small_names = [nm for nm in names if nm not in ffn_names + ("w_k", "w_v", "w_q", "w_o", "pool_w")]

    def pack_small(tree):
        cols = []
        for nm in small_names:
            flat = tree[nm].reshape(-1)
            cols.append(jnp.pad(flat, (0, -flat.shape[0] % 1024)))
        return jnp.concatenate(cols).reshape(-1, 128)

    dl, m1, m2 = _adam(pack_small(weights), pack_small(grads), pack_small(mom1), pack_small(mom2), "adam_small")

    def unpack_small(packed, out):
        flat, off = packed.reshape(-1), 0
        for nm in small_names:
            size = weights[nm].size
            out[nm] = flat[off:off + size].reshape(weights[nm].shape)
            off += size + (-size % 1024)

    unpack_small(dl, deltas)
    unpack_small(m1, new_m)
    unpack_small(m2, new_v)

    recv0 = _chip_exchange_wait(flight, chip0, land0, dl, "scatter_ffn0_wait")
    sums0 = summed(recv0, "ffn0_", None)
    for nm, g in zip(ffn_names, sums0):
        done = _adam_layer(ffn_view(nm, weights[nm]), g, ffn_view(nm, mom1[nm]), ffn_view(nm, mom2[nm]), 0,
                           "adam0_" + nm, prev=layer1[nm])
        grads[nm], deltas[nm], new_m[nm], new_v[nm] = [ffn_view(nm, a) for a in done]

    return (loss, grad_x[None], *[grads[nm] for nm in names], *[deltas[nm] for nm in names],
            *[new_m[nm] for nm in names], *[new_v[nm] for nm in names])
```

```python
import functools

import jax
import jax.numpy as jnp
from jax import lax
from jax.experimental import pallas as pl
from jax.experimental.pallas import tpu as pltpu

F32 = jnp.float32
BF16 = jnp.bfloat16
MESH_ID = pl.DeviceIdType.MESH

N_DEV = 8
S = 2048
D = 1024
F = 2816
F_SHARD = F // N_DEV
D_SHARD = D // N_DEV
N_GROUPS = 4
GROUP = D // N_GROUPS
POOL_SHARD = GROUP // N_DEV
MAX_WIN = 16
HEAD = 64
N_HEADS = D // HEAD
CHUNK = 64
LEFT = 8
QB = 4 * CHUNK
KB = QB + LEFT * CHUNK
PADK = LEFT * CHUNK
TOEP = 1024
N_REL = 257
MAX_REL = 128
EPS = 1e-6
NEG_INF = -1e30
ATTN_SCALE = HEAD ** -0.5

ADAM_LR = 0.001
ADAM_B1 = 0.9
ADAM_B2 = 0.999
ADAM_EPS = 1e-08
ADAM_WD = 0.01
ADAM_STEP = 10

VMEM_LIMIT = 52 * 1024 * 1024

ANY = pl.BlockSpec(memory_space=pl.ANY)
VMEM = pl.BlockSpec(memory_space=pltpu.VMEM)


def _call(body, **kw):
    return pl.pallas_call(body, **kw)


def _params(*sem):
    return pltpu.CompilerParams(dimension_semantics=sem, vmem_limit_bytes=VMEM_LIMIT)


def _dot(a, b, dims):
    return lax.dot_general(a, b, (dims, ((), ())), preferred_element_type=F32)


def _nn(a, b):
    return _dot(a, b, ((1,), (0,)))


def _nt(a, b):
    return _dot(a, b, ((1,), (1,)))


def _tn(a, b):
    return _dot(a, b, ((0,), (0,)))


def _rstd(x):
    return lax.rsqrt(jnp.mean(x * x, axis=-1, keepdims=True) + EPS)


def _rms_bwd(dh, x, r, g):
    gd = dh * g
    return r * gd - x * (r * r * r) * jnp.mean(gd * x, axis=-1, keepdims=True)


def _colsum8(v):
    return jnp.broadcast_to(jnp.sum(v, axis=0, keepdims=True), (8, v.shape[1]))


def _seg_sum(v):
    r = lax.broadcasted_iota(jnp.int32, (128, 128), 0) // HEAD
    c = lax.broadcasted_iota(jnp.int32, (128, 128), 1) // HEAD
    ones = jnp.where(r == c, 1.0, 0.0).astype(BF16)
    out = []
    for blk in range(v.shape[1] // 128):
        part = v[:, blk * 128:(blk + 1) * 128]
        hi = part.astype(BF16)
        rest = part - hi.astype(F32)
        mid = rest.astype(BF16)
        lo = (rest - mid.astype(F32)).astype(BF16)
        out.append(_nn(hi, ones) + _nn(mid, ones) + _nn(lo, ones))
    return jnp.concatenate(out, axis=1)


def _place():
    return lax.axis_index("x"), lax.axis_index("y"), lax.axis_index("c")


class _Exchange:
    def __init__(self, ins, out_shape, sems, start, mid, finish, early=None):
        self.ins, self.out_shape, self.sems = list(ins), list(out_shape), list(sems)
        self.start, self.mid, self.finish = start, mid, finish
        self.early = early if early is not None else (lambda ins, outs, sems: None)


def _relay_gather_exchange(shards):
    n = len(shards)

    def tools(ins, outs, sems):
        send_sems, recv_sems, local_sems = sems
        x, y, c = _place()
        me, sib = (x, y, c), (x, y, 1 - c)
        xn, yn, dg = (1 - x, y, c), (x, 1 - y, c), (1 - x, 1 - y, c)
        other = lambda dev: (dev[0], dev[1], 1 - c)

        def piece(k, ref, half):
            rows = shards[k].shape[0] // 2
            return ref if half is None else ref.at[pl.ds(half * rows, rows)]

        def held(k, dev, half):
            return piece(k, outs[k].at[4 * dev[0] + 2 * dev[1] + dev[2]], half)

        def copy(k, s, dev, half, to, own=False):
            return pltpu.make_async_remote_copy(
                src_ref=piece(k, ins[k], half) if own else held(k, dev, half), dst_ref=held(k, dev, half),
                send_sem=send_sems.at[k, s], recv_sem=recv_sems.at[k, s], device_id=to, device_id_type=MESH_ID)

        def mine(k):
            return pltpu.make_async_copy(ins[k], held(k, me, None), local_sems.at[k])

        def sent(k):
            return [copy(k, 0, me, None, sib, own=True), copy(k, 1, me, 0, xn, own=True), copy(k, 2, me, 1, yn, own=True),
                    copy(k, 3, me, 0, yn, own=True), copy(k, 4, me, 1, xn, own=True)]

        relays = [(1, xn, 0, [(5, yn), (7, sib)]), (2, yn, 1, [(6, xn), (9, sib)]), (3, yn, 0, [(10, sib)]),
                  (4, xn, 1, [(8, sib)]), (6, dg, 1, [(11, sib)]), (5, dg, 0, [(12, sib)])]
        from_sibling = [(0, other(me), None), (7, other(xn), 0), (8, other(xn), 1), (9, other(yn), 1),
                        (10, other(yn), 0), (11, other(dg), 1), (12, other(dg), 0)]
        return mine, sent, copy, relays, from_sibling, me

    def start(ins, outs, sems):
        mine, sent, _, _, _, _ = tools(ins, outs, sems)
        for k in range(n):
            mine(k).start()
        for k in range(n):
            for cp in sent(k):
                cp.start()

    def pass_on(ins, outs, sems, which):
        _, _, copy, relays, _, me = tools(ins, outs, sems)
        for k in range(n):
            for s, dev, half, onward in relays[which]:
                copy(k, s, dev, half, me).wait_recv()
                for s2, to in onward:
                    copy(k, s2, dev, half, to).start()

    def early(ins, outs, sems):
        pass_on(ins, outs, sems, slice(0, 2))

    def mid(ins, outs, sems):
        pass_on(ins, outs, sems, slice(2, None))

    def finish(ins, outs, sems):
        mine, sent, copy, relays, from_sibling, me = tools(ins, outs, sems)
        for k in range(n):
            for s, dev, half in from_sibling:
                copy(k, s, dev, half, me).wait_recv()
        for k in range(n):
            for cp in sent(k):
                cp.wait_send()
            for _, dev, half, onward in relays:
                for s2, to in onward:
                    copy(k, s2, dev, half, to).wait_send()
            mine(k).wait()

    return _Exchange(
        shards, [jax.ShapeDtypeStruct((N_DEV,) + a.shape, a.dtype) for a in shards],
        [pltpu.SemaphoreType.DMA((n, 13)), pltpu.SemaphoreType.DMA((n, 13)), pltpu.SemaphoreType.DMA((n,))],
        start, mid, finish, early)


def _peer(x, y, c, m):
    px = 1 - x if m & 4 else x
    py = 1 - y if m & 2 else y
    pc = 1 - c if m & 1 else c
    return px, py, pc


N_CHIPS = N_DEV // 2


def _pair_exchange(parts):
    n = len(parts)

    def copies(ins, outs, sems):
        send_sems, recv_sems = sems
        x, y, c = _place()
        return [pltpu.make_async_remote_copy(
            src_ref=ins[k].at[2 * q + 1 - c], dst_ref=outs[k].at[q], send_sem=send_sems.at[k, q],
            recv_sem=recv_sems.at[k, q], device_id=(x, y, 1 - c), device_id_type=MESH_ID)
            for k in range(n) for q in range(N_CHIPS)]

    def start(ins, outs, sems):
        for cp in copies(ins, outs, sems):
            cp.start()

    def mid(ins, outs, sems):
        pass

    def finish(ins, outs, sems):
        for cp in copies(ins, outs, sems):
            cp.wait_recv()
        for cp in copies(ins, outs, sems):
            cp.wait_send()

    return _Exchange(
        parts, [jax.ShapeDtypeStruct((N_CHIPS,) + a.shape[1:], a.dtype) for a in parts],
        [pltpu.SemaphoreType.DMA((n, N_CHIPS)), pltpu.SemaphoreType.DMA((n, N_CHIPS))], start, mid, finish)


def _pair_add(parts, stage, name, copies=1):
    n = len(parts)
    core = lax.axis_index("c").reshape(1)

    def body(core_ref, *refs):
        for k in range(n):
            mine, theirs = refs[k], refs[n + k]
            total = (mine[0, 0].astype(F32) + theirs[0].astype(F32)).astype(BF16)
            for rep in range(copies):
                refs[(2 + rep) * n + k][0] = total

    in_specs, out_specs = [], []
    for a in parts:
        _, r, cdim = a.shape
        in_specs.append(pl.BlockSpec((1, 1, r // 2, cdim), lambda q, i, core_ref: (q, core_ref[0], i, 0)))
    for a in parts:
        _, r, cdim = a.shape
        in_specs.append(pl.BlockSpec((1, r // 2, cdim), lambda q, i, core_ref: (q, i, 0)))
        out_specs.append(pl.BlockSpec((1, r // 2, cdim), lambda q, i, core_ref: (q, i, 0)))
    return list(_call(
        body, name=name,
        grid_spec=pltpu.PrefetchScalarGridSpec(num_scalar_prefetch=1, grid=(N_CHIPS, 2), in_specs=in_specs,
                                               out_specs=out_specs * copies),
        out_shape=[jax.ShapeDtypeStruct(s.shape, BF16) for s in stage] * copies,
        compiler_params=_params("arbitrary", "arbitrary"),
    )(core, *[a.reshape((N_CHIPS, 2) + a.shape[1:]) for a in parts], *stage))


HBM = pl.BlockSpec(memory_space=pltpu.HBM)
SEMAPHORES = pl.BlockSpec(memory_space=pltpu.SEMAPHORE)


def _chip_copies(srcs, lands, send_sems, recv_sems, mine_is_dst):
    x, y, c = _place()
    me = 2 * x + y
    out = []
    for m in range(1, N_CHIPS):
        px, py, _ = _peer(x, y, c, 2 * m)
        peer = 2 * px + py
        for k in range(len(srcs)):
            pair = k * (N_CHIPS - 1) + m - 1
            out.append(pltpu.make_async_remote_copy(
                src_ref=srcs[k].at[peer], dst_ref=lands[k].at[me if mine_is_dst else peer],
                send_sem=send_sems[pair], recv_sem=recv_sems[pair],
                device_id=(px, py, c), device_id_type=MESH_ID))
    return out


def _chip_exchange_start(chip_parts, lands, after, name):
    n = len(chip_parts)
    pairs = n * (N_CHIPS - 1)

    def body(*refs):
        srcs, zones = refs[:n], refs[n:2 * n]
        sems = refs[2 * n + 1:2 * n + 1 + 2 * pairs]
        token = refs[-1]
        for cp in _chip_copies(srcs, zones, sems[:pairs], sems[pairs:], True):
            cp.start()
        token[...] = jnp.zeros(token.shape, F32)

    thru = [pltpu.HBM(a.shape, a.dtype) for a in chip_parts + lands]
    hbm = [pltpu.with_memory_space_constraint(a, pltpu.HBM) for a in chip_parts + lands]
    res = _call(
        body, name=name,
        out_shape=[pltpu.SemaphoreType.DMA(())] * (2 * pairs) + thru + [jax.ShapeDtypeStruct((8, 128), F32)],
        in_specs=[HBM] * (2 * n) + [ANY], out_specs=[SEMAPHORES] * (2 * pairs) + [HBM] * (2 * n) + [VMEM],
        input_output_aliases={i: 2 * pairs + i for i in range(2 * n)},
        compiler_params=pltpu.CompilerParams(has_side_effects=pltpu.SideEffectType.DATAFLOW_SIDE_EFFECTING),
    )(*hbm, after)
    sems, rest = list(res[:2 * pairs]), res[2 * pairs:]
    return sems, list(rest[:n]), list(rest[n:2 * n]), rest[-1]


def _chip_exchange_wait(sems, chip_parts, lands, after, name):
    n = len(chip_parts)
    pairs = n * (N_CHIPS - 1)

    def body(*refs):
        srcs, zones = refs[:n], refs[n:2 * n]
        sem_refs = refs[2 * n:2 * n + 2 * pairs]
        for cp in _chip_copies(srcs, zones, sem_refs[:pairs], sem_refs[pairs:], False):
            cp.wait_send()
            cp.wait_recv()

    thru = [pltpu.HBM(a.shape, a.dtype) for a in chip_parts + lands]
    res = _call(
        body, name=name, out_shape=thru,
        in_specs=[HBM] * (2 * n) + [SEMAPHORES] * (2 * pairs) + [ANY], out_specs=[HBM] * (2 * n),
        input_output_aliases={i: i for i in range(2 * n)},
        compiler_params=pltpu.CompilerParams(has_side_effects=pltpu.SideEffectType.DATAFLOW_SIDE_EFFECTING),
    )(*chip_parts, *lands, *sems, after)
    return list(res[n:])


def _run_exchange(ex, name, after=None):
    n_in, n_out = len(ex.ins), len(ex.out_shape)
    order = [] if after is None else [after]

    def body(*refs):
        ins, outs = refs[:n_in], refs[n_in + len(order):n_in + len(order) + n_out]
        sems = refs[n_in + len(order) + n_out:]
        ex.start(ins, outs, sems)
        ex.early(ins, outs, sems)
        ex.mid(ins, outs, sems)
        ex.finish(ins, outs, sems)

    return list(_call(body, name=name, out_shape=ex.out_shape, in_specs=[ANY] * (n_in + len(order)),
                      out_specs=[ANY] * n_out, scratch_shapes=ex.sems)(*ex.ins, *order))


def _call_hosting(body, ex, phases, *, in_specs, out_specs, out_shape, scratch_shapes, args, **kw):
    n_in, n_out, n_scr = len(in_specs), len(out_specs), len(scratch_shapes)
    if ex is None:
        res = _call(body, in_specs=in_specs, out_specs=out_specs, out_shape=out_shape,
                    scratch_shapes=scratch_shapes, **kw)(*args)
        return list(res), []
    n_xin, n_xout = len(ex.ins), len(ex.out_shape)

    def hosting(*refs):
        a, b = n_in, n_in + n_xin
        c, d = b + n_out, b + n_out + n_xout
        ins, xins, outs, xouts = refs[:a], refs[a:b], refs[b:c], refs[c:d]
        scr, sems = refs[d:d + n_scr], refs[d + n_scr:]
        first, early, mid, last = phases()

        @pl.when(first)
        def _():
            ex.start(xins, xouts, sems)

        body(*ins, *outs, *scr)

        @pl.when(early)
        def _():
            ex.early(xins, xouts, sems)

        @pl.when(mid)
        def _():
            ex.mid(xins, xouts, sems)

        @pl.when(last)
        def _():
            ex.finish(xins, xouts, sems)

    res = _call(hosting, in_specs=list(in_specs) + [ANY] * n_xin, out_specs=list(out_specs) + [ANY] * n_xout,
                out_shape=list(out_shape) + ex.out_shape, scratch_shapes=list(scratch_shapes) + ex.sems,
                **kw)(*args, *ex.ins)
    return list(res[:n_out]), list(res[n_out:])


def _grid_phases(dims, mid_fraction=0.8, early_fraction=0.4):
    total = 1
    for d in dims:
        total *= d
    mid = min(max(int(total * mid_fraction), 1), total - 1)
    early = min(int(total * early_fraction), mid)

    def phases():
        step = pl.program_id(0)
        for axis in range(1, len(dims)):
            step = step * dims[axis] + pl.program_id(axis)
        return step == 0, step == early, step == mid, step == total - 1
    return phases


def _small_exchanges(pack, blocks, name, after):
    rows = pack.shape[0]

    def body(in_ref, blk_ref, _, out_ref, got_ref, recv, send_sems, recv_sems, blk_send, blk_recv, blk_local):
        x, y, c = _place()
        me = 4 * x + 2 * y + c
        peers = [_peer(x, y, c, m) for m in range(1, N_DEV)]
        index = [4 * px + 2 * py + pc for px, py, pc in peers]

        def block_copy(m, landing):
            return pltpu.make_async_remote_copy(
                src_ref=blk_ref.at[index[m]], dst_ref=got_ref.at[index[m] if landing else me],
                send_sem=blk_send.at[m], recv_sem=blk_recv.at[m], device_id=peers[m], device_id_type=MESH_ID)

        def pack_copy(m, landing):
            return pltpu.make_async_remote_copy(
                src_ref=in_ref, dst_ref=recv.at[index[m] if landing else me], send_sem=send_sems.at[m],
                recv_sem=recv_sems.at[m], device_id=peers[m], device_id_type=MESH_ID)

        own = pltpu.make_async_copy(blk_ref.at[me], got_ref.at[me], blk_local)
        own.start()
        recv[me] = in_ref[...]
        for m in range(N_DEV - 1):
            pack_copy(m, False).start()
            block_copy(m, False).start()
        for m in range(N_DEV - 1):
            pack_copy(m, True).wait_recv()
        acc = recv[0]
        for d in range(1, N_DEV):
            acc = acc + recv[d]
        out_ref[...] = acc
        for m in range(N_DEV - 1):
            block_copy(m, True).wait_recv()
        for m in range(N_DEV - 1):
            pack_copy(m, False).wait_send()
            block_copy(m, False).wait_send()
        own.wait()

    return _call(
        body, name=name,
        out_shape=[jax.ShapeDtypeStruct(pack.shape, F32), jax.ShapeDtypeStruct(blocks.shape, blocks.dtype)],
        in_specs=[VMEM, ANY, ANY], out_specs=[VMEM, ANY],
        scratch_shapes=[pltpu.VMEM((N_DEV, rows, pack.shape[1]), F32)] + [pltpu.SemaphoreType.DMA((7,))] * 4
        + [pltpu.SemaphoreType.DMA],
        compiler_params=pltpu.CompilerParams(vmem_limit_bytes=VMEM_LIMIT),
    )(pack, blocks, after)


POOL_TS = 256


def _pool_counts(first_row, rows, win):
    t = first_row + lax.broadcasted_iota(jnp.int32, (rows, 1), 0)
    return jnp.minimum(t + 1, win).astype(F32)


def _pool_fwd(x, g, w, b, scale):
    nt = S // POOL_TS

    def body(x_ref, g_ref, w_ref, b_ref, s_ref, out_ref, diff_ref, ext):
        i = pl.program_id(0)

        @pl.when(i == 0)
        def _():
            ext[0:MAX_WIN, :] = jnp.zeros((MAX_WIN, D), F32)

        @pl.when(i > 0)
        def _():
            ext[0:MAX_WIN, :] = ext[POOL_TS:POOL_TS + MAX_WIN, :]

        xv = x_ref[...]
        h = xv * _rstd(xv) * g_ref[...]
        ext[MAX_WIN:, :] = h
        for gi in range(N_GROUPS):
            win = 2 << gi
            cols = slice(gi * GROUP, (gi + 1) * GROUP)
            sm = ext[:, cols]
            k = 1
            while k < win:
                sm = sm + pltpu.roll(sm, k, axis=0)
                k *= 2
            pooled = sm[MAX_WIN:, :] / _pool_counts(i * POOL_TS, POOL_TS, win)
            diff = (pooled - h[:, cols]).astype(BF16)
            yv = (_nn(diff, w_ref[gi]) + b_ref[:, cols]) * s_ref[:, cols]
            out_ref[:, cols] = xv[:, cols] + yv
            diff_ref[:, cols] = diff

    row = pl.BlockSpec((1, D), lambda i: (0, 0))
    tile = pl.BlockSpec((POOL_TS, D), lambda i: (i, 0))
    return _call(
        body, name="pool_fwd", grid=(nt,),
        in_specs=[tile, row, pl.BlockSpec((N_GROUPS, GROUP, GROUP), lambda i: (0, 0, 0)), row, row],
        out_specs=[tile, tile],
        out_shape=[jax.ShapeDtypeStruct((S, D), F32), jax.ShapeDtypeStruct((S, D), BF16)],
        scratch_shapes=[pltpu.VMEM((POOL_TS + MAX_WIN, D), F32)],
        compiler_params=_params("arbitrary"),
    )(x, g, w, b, scale)


def _pool_bwd(dy, x, diff, g, w, b, scale, ex=None):
    nt = S // POOL_TS

    def body(dy_ref, x_ref, diff_ref, g_ref, w_ref, b_ref, s_ref, gx_ref, dw_ref, db_ref, ds_ref, dg_ref, ext, dh):
        i = pl.program_id(0)
        first_row = (nt - 1 - i) * POOL_TS

        @pl.when(i == 0)
        def _():
            ext[POOL_TS:, :] = jnp.zeros((MAX_WIN, D), F32)
            dw_ref[...] = jnp.zeros(dw_ref.shape, F32)
            db_ref[...] = jnp.zeros(db_ref.shape, F32)
            ds_ref[...] = jnp.zeros(ds_ref.shape, F32)
            dg_ref[...] = jnp.zeros(dg_ref.shape, F32)

        @pl.when(i > 0)
        def _():
            ext[POOL_TS:, :] = ext[0:MAX_WIN, :]

        dyv = dy_ref[...]
        for gi in range(N_GROUPS):
            win = 2 << gi
            cols = slice(gi * GROUP, (gi + 1) * GROUP)
            dfb = diff_ref[:, cols]
            z = _nn(dfb, w_ref[gi]) + b_ref[:, cols]
            dyg = dyv[:, cols]
            ds_ref[:, cols] += _colsum8(dyg * z)
            dz = dyg * s_ref[:, cols]
            db_ref[:, cols] += _colsum8(dz)
            dzb = dz.astype(BF16)
            dw_ref[gi] += _tn(dfb, dzb)
            ddiff = _nt(dzb, w_ref[gi])
            ext[0:POOL_TS, cols] = ddiff / _pool_counts(first_row, POOL_TS, win)
            sm = ext[:, cols]
            k = 1
            while k < win:
                sm = sm + pltpu.roll(sm, POOL_TS + MAX_WIN - k, axis=0)
                k *= 2
            dh[:, cols] = sm[0:POOL_TS, :] - ddiff
        xv = x_ref[...]
        r = _rstd(xv)
        gv = g_ref[...]
        dhv = dh[...]
        dg_ref[...] += _colsum8(dhv * xv * r)
        gx_ref[...] = dyv + _rms_bwd(dhv, xv, r, gv)

    row = pl.BlockSpec((1, D), lambda i: (0, 0))
    tile = pl.BlockSpec((POOL_TS, D), lambda i: (nt - 1 - i, 0))
    acc = pl.BlockSpec((8, D), lambda i: (0, 0))
    wspec = pl.BlockSpec((N_GROUPS, GROUP, GROUP), lambda i: (0, 0, 0))
    return _call_hosting(
        body, ex, _grid_phases((nt,)), name="pool_bwd", grid=(nt,),
        in_specs=[tile, tile, tile, row, wspec, row, row],
        out_specs=[tile, wspec, acc, acc, acc],
        out_shape=[jax.ShapeDtypeStruct((S, D), F32), jax.ShapeDtypeStruct((N_GROUPS, GROUP, GROUP), F32),
                   jax.ShapeDtypeStruct((8, D), F32), jax.ShapeDtypeStruct((8, D), F32),
                   jax.ShapeDtypeStruct((8, D), F32)],
        scratch_shapes=[pltpu.VMEM((POOL_TS + MAX_WIN, D), F32), pltpu.VMEM((POOL_TS, D), F32)],
        args=(dy, x, diff, g, w, b, scale), compiler_params=_params("arbitrary"),
    )


FFN_TS = min(S, 1024)
FFN_TF = 256


def _ffn_fwd(x, g, wg_t, wu_t, wd, name, target=None, ex=None):
    width = wd.shape[0]
    ni, nj = S // FFN_TS, width // FFN_TF
    with_loss = target is not None
    extra = [target] if with_loss else []

    def body(*refs):
        x_ref, g_ref, wg_ref, wu_ref, wd_ref = refs[:5]
        if with_loss:
            t_ref, out_ref, loss_ref, h_ref, gg_ref, uu_ref, hs, acc = refs[5:]
        else:
            out_ref, h_ref, gg_ref, uu_ref, hs, acc = refs[5:]
        i, j = pl.program_id(0), pl.program_id(1)

        @pl.when(j == 0)
        def _():
            xv = x_ref[...]
            hb = (xv * _rstd(xv) * g_ref[...]).astype(BF16)
            hs[...] = hb
            h_ref[...] = hb
            acc[...] = jnp.zeros(acc.shape, F32)

        hb = hs[...]
        gg = _nt(hb, wg_ref[...])
        uu = _nt(hb, wu_ref[...])
        gg_ref[...] = gg
        uu_ref[...] = uu
        a = (gg * jax.nn.sigmoid(gg) * uu).astype(BF16)
        acc[...] += _nn(a, wd_ref[...])

        @pl.when(j == nj - 1)
        def _():
            yv = x_ref[...] + acc[...]
            if with_loss:
                err = yv - t_ref[...]
                out_ref[...] = err * (1.0 / D)
                part = jnp.sum(err * err) * (0.5 / D)

                @pl.when(i == 0)
                def _():
                    loss_ref[...] = jnp.zeros(loss_ref.shape, F32)

                loss_ref[...] += jnp.broadcast_to(part, loss_ref.shape)
            else:
                out_ref[...] = yv

    xt = pl.BlockSpec((FFN_TS, D), lambda i, j: (i, 0))
    row = pl.BlockSpec((1, D), lambda i, j: (0, 0))
    wt = pl.BlockSpec((FFN_TF, D), lambda i, j: (j, 0))
    gt = pl.BlockSpec((FFN_TS, FFN_TF), lambda i, j: (i, j))
    in_specs = [xt, row, wt, wt, wt] + [xt] * len(extra)
    out_specs = [xt] + ([pl.BlockSpec((8, 128), lambda i, j: (0, 0))] if with_loss else []) + [xt, gt, gt]
    out_shape = ([jax.ShapeDtypeStruct((S, D), F32)] + ([jax.ShapeDtypeStruct((8, 128), F32)] if with_loss else [])
                 + [jax.ShapeDtypeStruct((S, D), BF16), jax.ShapeDtypeStruct((S, width), F32),
                    jax.ShapeDtypeStruct((S, width), F32)])
    args = (x, g, wg_t, wu_t, wd, *extra)
    return _call_hosting(
        body, ex, _grid_phases((ni, nj), 0.8), name=name, grid=(ni, nj),
        in_specs=in_specs, out_specs=out_specs, out_shape=out_shape,
        scratch_shapes=[pltpu.VMEM((FFN_TS, D), BF16), pltpu.VMEM((FFN_TS, D), F32)], args=args,
        compiler_params=_params("arbitrary", "arbitrary"),
    )


def _ffn_bwd_weights(dout, h, gg, uu, wd, name, ex=None):
    width = wd.shape[0]
    nj = width // FFN_TF

    def body(do_ref, h_ref, gg_ref, uu_ref, wd_ref, dg_ref, du_ref, dwg_ref, dwu_ref, dwd_ref, dob, kept):
        j = pl.program_id(0)

        @pl.when(j == 0)
        def _():
            dob[...] = do_ref[...].astype(BF16)
            kept[1] = jnp.zeros(kept.shape[1:], BF16)

        prev, cur = (j + 1) % 2, j % 2
        hb, dov = h_ref[...], dob[...]
        dwg_ref[...] = _tn(kept[prev, 0], hb).astype(BF16)
        dwu_ref[...] = _tn(kept[prev, 1], hb).astype(BF16)
        dwd_ref[...] = _tn(kept[prev, 2], dov).astype(BF16)
        gv, uv = gg_ref[...], uu_ref[...]
        da = _nt(dov, wd_ref[...])
        sg = jax.nn.sigmoid(gv)
        sl = gv * sg
        dub = (da * sl).astype(BF16)
        dgb = (da * uv * (sg * (1.0 + gv * (1.0 - sg)))).astype(BF16)
        dg_ref[...] = dgb
        du_ref[...] = dub
        kept[cur, 0] = dgb
        kept[cur, 1] = dub
        kept[cur, 2] = (sl * uv).astype(BF16)

    once = pl.Buffered(1)
    whole = lambda: pl.BlockSpec((S, D), lambda j, _: (0, 0), pipeline_mode=once)
    this = lambda j: jnp.minimum(j, nj - 1)
    last = lambda j: jnp.maximum(j - 1, 0)
    wt_in = pl.BlockSpec((FFN_TF, D), lambda j, _: (this(j), 0))
    wt_out = pl.BlockSpec((FFN_TF, D), lambda j, _: (last(j), 0))
    gt = pl.BlockSpec((S, FFN_TF), lambda j, _: (0, this(j)))
    return _call_hosting(
        body, ex, _grid_phases((nj + 1, 1)), name=name, grid=(nj + 1, 1),
        in_specs=[whole(), whole(), gt, gt, wt_in], out_specs=[gt, gt, wt_out, wt_out, wt_out],
        out_shape=[jax.ShapeDtypeStruct((S, width), BF16)] * 2 + [jax.ShapeDtypeStruct((width, D), BF16)] * 3,
        scratch_shapes=[pltpu.VMEM((S, D), BF16), pltpu.VMEM((2, 3, S, FFN_TF), BF16)], args=(dout, h, gg, uu, wd),
        compiler_params=_params("arbitrary", "arbitrary"),
    )


BWD_TS = 512


def _ffn_bwd_input(dres, x, g, grads, weights, name, ex=None):
    nt = S // BWD_TS
    n = len(grads)

    def body(*refs):
        dres_ref, x_ref, g_ref = refs[:3]
        grad_refs, w_refs = refs[3:3 + n], refs[3 + n:3 + 2 * n]
        dx_ref, dgam_ref = refs[3 + 2 * n:]
        dh = _nn(grad_refs[0][...], w_refs[0][...])
        for k in range(1, n):
            dh = dh + _nn(grad_refs[k][...], w_refs[k][...])
        xv = x_ref[...]
        r = _rstd(xv)
        dx_ref[...] = dres_ref[...] + _rms_bwd(dh, xv, r, g_ref[...])

        @pl.when(pl.program_id(0) == 0)
        def _():
            dgam_ref[...] = jnp.zeros(dgam_ref.shape, F32)

        dgam_ref[...] += _colsum8(dh * xv * r)

    tile = pl.BlockSpec((BWD_TS, D), lambda i: (i, 0))
    ftiles = [pl.BlockSpec((BWD_TS, a.shape[1]), lambda i: (i, 0)) for a in grads]
    wspecs = [pl.BlockSpec(w.shape, lambda i: (0, 0), pipeline_mode=pl.Buffered(1)) for w in weights]
    return _call_hosting(
        body, ex, _grid_phases((nt,)), name=name, grid=(nt,),
        in_specs=[tile, tile, pl.BlockSpec((1, D), lambda i: (0, 0))] + ftiles + wspecs,
        out_specs=[tile, pl.BlockSpec((8, D), lambda i: (0, 0))],
        out_shape=[jax.ShapeDtypeStruct((S, D), F32), jax.ShapeDtypeStruct((8, D), F32)],
        scratch_shapes=[], args=(dres, x, g, *grads, *weights), compiler_params=_params("arbitrary"),
    )


def _mm(a, b, mode, out_dtype, name, add=None, skip_rows=0):
    if mode == "nn":
        (m, kd), n = (a.shape[0] - skip_rows, a.shape[1]), b.shape[1]
    elif mode == "nt":
        (m, kd), n = (a.shape[0] - skip_rows, a.shape[1]), b.shape[0]
    else:
        (kd, m), n = a.shape, b.shape[1]
    tm, tn, tk = min(m, 1024), min(n, 1024), min(kd, 1024)
    if skip_rows and mode == "tn":
        tk = min(tk, skip_rows)
    elif skip_rows:
        tm = min(tm, skip_rows)
    skip_a = skip_rows // tm if mode != "tn" else 0
    skip_b = skip_rows // tk if mode == "tn" else 0
    assert skip_rows == skip_a * tm + skip_b * tk
    nk = kd // tk
    dot = {"nn": _nn, "nt": _nt, "tn": _tn}[mode]

    def body(*refs):
        if add is None:
            a_ref, b_ref, o_ref, acc = refs
        else:
            a_ref, b_ref, add_ref, o_ref, acc = refs
        k = pl.program_id(2)

        @pl.when(k == 0)
        def _():
            acc[...] = jnp.zeros(acc.shape, F32)

        acc[...] += dot(a_ref[...].astype(BF16), b_ref[...].astype(BF16))

        @pl.when(k == nk - 1)
        def _():
            res = acc[...]
            if add is not None:
                res = res + add_ref[...]
            o_ref[...] = res.astype(out_dtype)

    if mode == "tn":
        a_spec = pl.BlockSpec((tk, tm), lambda i, j, k: (k, i))
        b_spec = pl.BlockSpec((tk, tn), lambda i, j, k: (k + skip_b, j))
    else:
        a_spec = pl.BlockSpec((tm, tk), lambda i, j, k: (i + skip_a, k))
        b_spec = (pl.BlockSpec((tk, tn), lambda i, j, k: (k, j)) if mode == "nn"
                  else pl.BlockSpec((tn, tk), lambda i, j, k: (j, k)))
    o_spec = pl.BlockSpec((tm, tn), lambda i, j, k: (i, j))
    in_specs = [a_spec, b_spec] + ([o_spec] if add is not None else [])
    args = (a, b) + ((add,) if add is not None else ())
    return _call(
        body, name=name, grid=(m // tm, n // tn, nk), in_specs=in_specs, out_specs=o_spec,
        out_shape=jax.ShapeDtypeStruct((m, n), out_dtype), scratch_shapes=[pltpu.VMEM((tm, tn), F32)],
        compiler_params=_params("parallel", "parallel", "arbitrary"),
    )(*args)


PROJ_TS = 256


def _kvq_proj(x, g_kv, g_mix, wk, wv, wq, gk, gq, ex=None):
    lead = PADK // PROJ_TS

    def body(x_ref, gkv_ref, gmix_ref, wk_ref, wv_ref, wq_ref, gk_ref, gq_ref,
             hkv_ref, h1_ref, kpre_ref, qpre_ref, k_ref, v_ref, q_ref):
        i = pl.program_id(0)

        @pl.when(i < lead)
        def _():
            k_ref[...] = jnp.zeros(k_ref.shape, BF16)
            v_ref[...] = jnp.zeros(v_ref.shape, BF16)

        @pl.when(i >= lead)
        def _():
            xv = x_ref[...]
            xr = xv * _rstd(xv)
            hkv = (xr * gkv_ref[...]).astype(BF16)
            h1 = (xr * gmix_ref[...]).astype(BF16)
            hkv_ref[...] = hkv
            h1_ref[...] = h1
            kpre = _nn(hkv, wk_ref[...])
            qpre = _nn(h1, wq_ref[...])
            kpre_ref[...] = kpre
            qpre_ref[...] = qpre
            v_ref[...] = _nn(hkv, wv_ref[...]).astype(BF16)
            rk = lax.rsqrt(_seg_sum(kpre * kpre) * (1.0 / HEAD) + EPS)
            k_ref[...] = (kpre * rk * gk_ref[...]).astype(BF16)
            rq = lax.rsqrt(_seg_sum(qpre * qpre) * (1.0 / HEAD) + EPS)
            q_ref[...] = (qpre * rq * gq_ref[...]).astype(BF16)

    tile = pl.BlockSpec((PROJ_TS, D), lambda i: (jnp.maximum(i - lead, 0), 0))
    padded = pl.BlockSpec((PROJ_TS, D), lambda i: (i, 0))
    row = pl.BlockSpec((1, D), lambda i: (0, 0))
    wspec = pl.BlockSpec((D, D), lambda i: (0, 0))
    bf = jax.ShapeDtypeStruct((S, D), BF16)
    ff = jax.ShapeDtypeStruct((S, D), F32)
    bp = jax.ShapeDtypeStruct((PADK + S, D), BF16)
    return _call_hosting(
        body, ex, _grid_phases((lead + S // PROJ_TS,), 0.85), name="kvq_proj", grid=(lead + S // PROJ_TS,),
        in_specs=[tile, row, row, wspec, wspec, wspec, row, row],
        out_specs=[tile, tile, tile, tile, padded, padded, tile], out_shape=[bf, bf, ff, ff, bp, bp, bf],
        scratch_shapes=[], args=(x, g_kv, g_mix, wk, wv, wq, gk, gq), compiler_params=_params("arbitrary"),
    )


def _qkv_input_bwd(dq, qpre, gq, dkp, kpre, gk, dvp, wq, wk, wv, dres, x, g_mix, g_kv, ex=None):
    nt = S // PROJ_TS
    lead = PADK // PROJ_TS

    def head_bwd(dov, pv, hgv):
        r = lax.rsqrt(_seg_sum(pv * pv) * (1.0 / HEAD) + EPS)
        gd = dov * hgv
        dpre = r * gd - pv * (r * r * r) * (_seg_sum(gd * pv) * (1.0 / HEAD))
        return dpre.astype(BF16), _colsum8(dov * pv * r)

    def fold_heads(full):
        fold = full[:, 0:128]
        for blk in range(1, D // 128):
            fold = fold + full[:, blk * 128:(blk + 1) * 128]
        return fold + pltpu.roll(fold, HEAD, axis=1)

    def body(dq_ref, qpre_ref, gq_ref, dk_ref, kpre_ref, gk_ref, dv_ref, wq_ref, wk_ref, wv_ref, dres_ref, x_ref,
             gmix_ref, gkv_ref, dx_ref, dqpre_ref, dkpre_ref, dgmix_ref, dgkv_ref, dgq_ref, dgk_ref, accq, acck):
        i = pl.program_id(0)

        @pl.when(i == 0)
        def _():
            accq[...] = jnp.zeros(accq.shape, F32)
            acck[...] = jnp.zeros(acck.shape, F32)
            dgmix_ref[...] = jnp.zeros(dgmix_ref.shape, F32)
            dgkv_ref[...] = jnp.zeros(dgkv_ref.shape, F32)

        dqb, cq = head_bwd(dq_ref[...], qpre_ref[...], gq_ref[...])
        dkb, ck = head_bwd(dk_ref[...], kpre_ref[...], gk_ref[...])
        dqpre_ref[...] = dqb
        dkpre_ref[...] = dkb
        accq[...] += cq
        acck[...] += ck
        dh1 = _nt(dqb, wq_ref[...])
        dhkv = _nt(dkb, wk_ref[...]) + _nt(dv_ref[...].astype(BF16), wv_ref[...])
        xv = x_ref[...]
        r = _rstd(xv)
        dx_ref[...] = dres_ref[...] + _rms_bwd(dh1, xv, r, gmix_ref[...]) + _rms_bwd(dhkv, xv, r, gkv_ref[...])
        dgmix_ref[...] += _colsum8(dh1 * xv * r)
        dgkv_ref[...] += _colsum8(dhkv * xv * r)

        @pl.when(i == nt - 1)
        def _():
            dgq_ref[...] = fold_heads(accq[...])
            dgk_ref[...] = fold_heads(acck[...])

    tile = pl.BlockSpec((PROJ_TS, D), lambda i: (i, 0))
    behind = pl.BlockSpec((PROJ_TS, D), lambda i: (i + lead, 0))
    row = pl.BlockSpec((1, D), lambda i: (0, 0))
    wspec = lambda: pl.BlockSpec((D, D), lambda i: (0, 0), pipeline_mode=pl.Buffered(1))
    acc = pl.BlockSpec((8, D), lambda i: (0, 0))
    small = pl.BlockSpec((8, 128), lambda i: (0, 0))
    bf = jax.ShapeDtypeStruct((S, D), BF16)
    return _call_hosting(
        body, ex, _grid_phases((nt,)), name="qkv_input_bwd", grid=(nt,),
        in_specs=[tile, tile, row, behind, tile, row, behind, wspec(), wspec(), wspec(), tile, tile, row, row],
        out_specs=[tile, tile, tile, acc, acc, small, small],
        out_shape=[jax.ShapeDtypeStruct((S, D), F32), bf, bf, jax.ShapeDtypeStruct((8, D), F32),
                   jax.ShapeDtypeStruct((8, D), F32), jax.ShapeDtypeStruct((8, 128), F32),
                   jax.ShapeDtypeStruct((8, 128), F32)],
        scratch_shapes=[pltpu.VMEM((8, D), F32), pltpu.VMEM((8, D), F32)],
        args=(dq, qpre, gq, dkp, kpre, gk, dvp, wq, wk, wv, dres, x, g_mix, g_kv),
        compiler_params=_params("arbitrary"),
    )


def _toeplitz_from_table(table):
    far = jnp.broadcast_to(table[:, N_REL - 1:], (N_HEADS, PADK - MAX_REL + 1))
    near = table[:, N_REL - 2::-1]
    past = jnp.broadcast_to(table[:, 0:1], (N_HEADS, MAX_REL))
    wrap = jnp.broadcast_to(table[:, N_REL - 1:], (N_HEADS, TOEP - (PADK + 2 * MAX_REL + 1)))
    return jnp.concatenate([far, near, past, wrap], axis=1).reshape(N_HEADS, 1, TOEP)


def _table_grad_from_toeplitz(dtp, seg):
    lo = PADK - MAX_REL + 1
    near = dtp[:, lo + N_REL - 3:lo - 1:-1]
    return jnp.concatenate([seg[:, 1:2], near, seg[:, 0:1]], axis=1)


def _bias_band(tp):
    def body(tp_ref, out_ref):
        bv = pltpu.roll(jnp.broadcast_to(tp_ref[0], (QB, TOEP)), 0, axis=1, stride=1, stride_axis=0)
        out_ref[0] = jnp.where(_band_mask(), bv[:, 0:KB], NEG_INF)

    return _call(
        body, name="bias_band", grid=(N_HEADS,),
        in_specs=[pl.BlockSpec((1, 1, TOEP), lambda h: (h, 0, 0))],
        out_specs=pl.BlockSpec((1, QB, KB), lambda h: (h, 0, 0)),
        out_shape=jax.ShapeDtypeStruct((N_HEADS, QB, KB), F32),
        compiler_params=_params("parallel"),
    )(tp)


def _bias_grad(dband, after):
    lo, hi = PADK - MAX_REL + 1, PADK + MAX_REL

    def body(db_ref, _, dtp_ref, seg_ref):
        bv = jnp.concatenate([db_ref[0], jnp.zeros((QB, TOEP - KB), F32)], axis=1)
        row = lax.broadcasted_iota(jnp.int32, (QB, TOEP), 0)
        k = 1
        while k < QB:
            bv = jnp.where((row & k) != 0, pltpu.roll(bv, TOEP - k, axis=1), bv)
            k *= 2
        col = jnp.sum(bv, axis=0, keepdims=True)
        dtp_ref[0] = col
        u = lax.broadcasted_iota(jnp.int32, (1, TOEP), 1)
        far = jnp.sum(jnp.where((u < lo) | (u > hi + MAX_REL), col, 0.0))
        past = jnp.sum(jnp.where((u >= hi) & (u <= hi + MAX_REL), col, 0.0))
        lane = lax.broadcasted_iota(jnp.int32, (1, 128), 1)
        seg_ref[0] = jnp.where(lane == 0, far, jnp.where(lane == 1, past, 0.0))

    return _call(
        body, name="bias_grad", grid=(N_HEADS,),
        in_specs=[pl.BlockSpec((1, QB, KB), lambda h: (h, 0, 0)), ANY],
        out_specs=[pl.BlockSpec((1, 1, TOEP), lambda h: (h, 0, 0)), pl.BlockSpec((1, 1, 128), lambda h: (h, 0, 0))],
        out_shape=[jax.ShapeDtypeStruct((N_HEADS, 1, TOEP), F32), jax.ShapeDtypeStruct((N_HEADS, 1, 128), F32)],
        compiler_params=_params("parallel"),
    )(dband, after)


N_QB = S // QB
HEADS_PER_STEP = 4
ATT_LANES = HEADS_PER_STEP * HEAD
N_HG = D // ATT_LANES


def _band_mask():
    qc = lax.broadcasted_iota(jnp.int32, (QB, KB), 0) // CHUNK
    kc = lax.broadcasted_iota(jnp.int32, (QB, KB), 1) // CHUNK
    return (kc >= qc) & (kc <= qc + LEFT)


def _half_scale(hh, scale):
    lane = lax.broadcasted_iota(jnp.int32, (1, 128), 1)
    return jnp.where((lane < HEAD) == (hh == 0), scale, 0.0).astype(BF16)


def _probs(qh, kb, bias, first_key):
    sc = _nt(qh, kb) + bias
    if first_key is not None:
        sc = jnp.where(lax.broadcasted_iota(jnp.int32, (QB, KB), 1) >= first_key, sc, NEG_INF)
    e = jnp.exp(sc - jnp.max(sc, axis=-1, keepdims=True))
    return e * (1.0 / jnp.sum(e, axis=-1, keepdims=True))


def _by_padding(cb, compute):
    @pl.when(cb < PADK // QB)
    def _():
        compute(PADK - cb * QB)

    @pl.when(cb >= PADK // QB)
    def _():
        compute(None)


def _attn_fwd(q, kp, vp, bias, ex=None):
    def body(q_ref, k_ref, v_ref, b_ref, o_ref):
        cb = pl.program_id(1)
        band = pl.ds(pl.multiple_of(cb * QB, QB), KB)
        low = lax.broadcasted_iota(jnp.int32, (QB, 128), 1) < HEAD

        def compute(first_key):
            for pair in range(HEADS_PER_STEP // 2):
                lanes = pl.ds(pair * 128, 128)
                kb, vb, qv = k_ref[band, lanes], v_ref[band, lanes], q_ref[:, lanes]
                outs = []
                for hh in range(2):
                    pb = _probs(qv * _half_scale(hh, ATTN_SCALE), kb, b_ref[2 * pair + hh], first_key).astype(BF16)
                    outs.append(_nn(pb, vb))
                o_ref[:, lanes] = jnp.where(low, outs[0], outs[1]).astype(BF16)

        _by_padding(cb, compute)

    qspec = pl.BlockSpec((QB, ATT_LANES), lambda hg, cb: (cb, hg))
    kspec = pl.BlockSpec((PADK + S, ATT_LANES), lambda hg, cb: (0, hg))
    return _call_hosting(
        body, ex, _grid_phases((N_HG, N_QB), 0.85), name="attn_fwd", grid=(N_HG, N_QB),
        in_specs=[qspec, kspec, kspec, pl.BlockSpec((HEADS_PER_STEP, QB, KB), lambda hg, cb: (hg, 0, 0))],
        out_specs=[qspec], out_shape=[jax.ShapeDtypeStruct((S, D), BF16)], scratch_shapes=[],
        args=(q, kp, vp, bias), compiler_params=_params("arbitrary", "arbitrary"),
    )


def _attn_bwd(q, kp, vp, bias, do, ex=None):
    def body(q_ref, k_ref, v_ref, b_ref, do_ref, dq_ref, dk_ref, dv_ref, db_ref):
        cb = pl.program_id(1)

        @pl.when(cb == 0)
        def _():
            dk_ref[...] = jnp.zeros(dk_ref.shape, F32)
            dv_ref[...] = jnp.zeros(dv_ref.shape, F32)
            db_ref[...] = jnp.zeros(db_ref.shape, F32)

        band = pl.ds(pl.multiple_of(cb * QB, QB), KB)
        low = lax.broadcasted_iota(jnp.int32, (QB, 128), 1) < HEAD

        def compute(first_key):
            for pair in range(HEADS_PER_STEP // 2):
                lanes = pl.ds(pair * 128, 128)
                kb, vb = k_ref[band, lanes], v_ref[band, lanes]
                qv, dov = q_ref[:, lanes], do_ref[:, lanes]
                dq = jnp.zeros((QB, 128), F32)
                dkb = jnp.zeros((KB, 128), F32)
                dvb = jnp.zeros((KB, 128), F32)
                for hh in range(2):
                    sel = low if hh == 0 else jnp.logical_not(low)
                    doh = dov * _half_scale(hh, 1.0)
                    p = _probs(qv * _half_scale(hh, ATTN_SCALE), kb, b_ref[2 * pair + hh], first_key)
                    dp = _nt(doh, vb)
                    dvb = dvb + _tn(p.astype(BF16), doh)
                    ds = p * (dp - jnp.sum(dp * p, axis=-1, keepdims=True))
                    db_ref[2 * pair + hh] += ds
                    dsb = (ds * ATTN_SCALE).astype(BF16)
                    dq = dq + jnp.where(sel, _nn(dsb, kb), 0.0)
                    dkb = dkb + _tn(dsb, qv * _half_scale(hh, 1.0))
                dq_ref[:, lanes] = dq
                dk_ref[band, lanes] += dkb
                dv_ref[band, lanes] += dvb

        _by_padding(cb, compute)

    qspec = pl.BlockSpec((QB, ATT_LANES), lambda hg, cb: (cb, hg))
    kspec = pl.BlockSpec((PADK + S, ATT_LANES), lambda hg, cb: (0, hg))
    bspec = pl.BlockSpec((HEADS_PER_STEP, QB, KB), lambda hg, cb: (hg, 0, 0))
    kf = jax.ShapeDtypeStruct((PADK + S, D), F32)
    return _call_hosting(
        body, ex, _grid_phases((N_HG, N_QB)), name="attn_bwd", grid=(N_HG, N_QB),
        in_specs=[qspec, kspec, kspec, bspec, qspec],
        out_specs=[qspec, kspec, kspec, bspec],
        out_shape=[jax.ShapeDtypeStruct((S, D), F32), kf, kf, jax.ShapeDtypeStruct((N_HEADS, QB, KB), F32)],
        scratch_shapes=[], args=(q, kp, vp, bias, do), compiler_params=_params("arbitrary", "arbitrary"),
    )


def _adam_math(w, g, m, v):
    m = ADAM_B1 * m + (1.0 - ADAM_B1) * g
    v = ADAM_B2 * v + (1.0 - ADAM_B2) * (g * g)
    m_hat = m / (1.0 - ADAM_B1 ** ADAM_STEP)
    v_hat = v / (1.0 - ADAM_B2 ** ADAM_STEP)
    delta = -ADAM_LR * (m_hat / (jnp.sqrt(v_hat) + ADAM_EPS) + ADAM_WD * w)
    return delta, m, v


def _sum_parts(parts, name, after=None):
    count, r, c = parts.shape
    tr = r // 2 if r % 16 == 0 and r > 64 else r

    def body(p_ref, *rest):
        o_ref = rest[-1]
        acc = p_ref[0].astype(F32)
        for d in range(1, count):
            acc = acc + p_ref[d].astype(F32)
        o_ref[...] = acc

    return _call(
        body, name=name, grid=(r // tr,),
        in_specs=[pl.BlockSpec((count, tr, c), lambda i: (0, i, 0))] + ([ANY] if after is not None else []),
        out_specs=pl.BlockSpec((tr, c), lambda i: (i, 0)),
        out_shape=jax.ShapeDtypeStruct((r, c), F32), compiler_params=_params("parallel"),
    )(parts, *([after] if after is not None else []))


def _row_tile(r):
    for cand in (256, 176, 128, 64, 32, 16, 8):
        if r % cand == 0:
            return cand
    return r


def _adam_layer(w, parts, m, v, layer, name, prev=None):
    nl, r, c = w.shape
    count = parts.shape[0]
    tr = _row_tile(r)

    def body(w_ref, g_ref, m_ref, v_ref, *rest):
        go_ref, d_ref, nm_ref, nv_ref = rest[-4:]
        gv = g_ref[0].astype(F32)
        for q in range(1, count):
            gv = gv + g_ref[q].astype(F32)
        delta, nm, nv = _adam_math(w_ref[0], gv, m_ref[0], v_ref[0])
        go_ref[0] = gv
        d_ref[0] = delta
        nm_ref[0] = nm
        nv_ref[0] = nv

    lspec = pl.BlockSpec((1, tr, c), lambda i: (layer, i, 0))
    sd = jax.ShapeDtypeStruct((nl, r, c), F32)
    extra = list(prev) if prev is not None else []
    return _call(
        body, name=name, grid=(r // tr,),
        in_specs=[lspec, pl.BlockSpec((count, tr, c), lambda i: (0, i, 0)), lspec, lspec] + [ANY] * len(extra),
        out_specs=[lspec] * 4, out_shape=[sd] * 4,
        input_output_aliases={4 + t: t for t in range(len(extra))},
        compiler_params=_params("parallel"),
    )(w, parts, m, v, *extra)


def _adam(w, g, m, v, name):
    r, c = w.shape
    tr = _row_tile(r)

    def body(w_ref, g_ref, m_ref, v_ref, d_ref, nm_ref, nv_ref):
        delta, nm, nv = _adam_math(w_ref[...], g_ref[...], m_ref[...], v_ref[...])
        d_ref[...] = delta
        nm_ref[...] = nm
        nv_ref[...] = nv

    spec = pl.BlockSpec((tr, c), lambda i: (i, 0))
    sd = jax.ShapeDtypeStruct((r, c), F32)
    return _call(
        body, name=name, grid=(r // tr,), in_specs=[spec] * 4, out_specs=[spec] * 3, out_shape=[sd] * 3,
        compiler_params=_params("parallel"),
    )(w, g, m, v)


def _pad16(rows):
    return jnp.pad(rows, ((0, 16 - rows.shape[0]), (0, 0)))


def kernel(x, norm_mix_g, norm_ffn_g, pool_w, pool_b, pool_scale, kv_norm_g, w_k, w_v, k_norm_g, w_q, q_norm_g, rel_bias, w_o, w_gate, w_up, w_down, loss_target, m_norm_mix_g, m_norm_ffn_g, m_pool_w, m_pool_b, m_pool_scale, m_kv_norm_g, m_w_k, m_w_v, m_k_norm_g, m_w_q, m_q_norm_g, m_rel_bias, m_w_o, m_w_gate, m_w_up, m_w_down, v_norm_mix_g, v_norm_ffn_g, v_pool_w, v_pool_b, v_pool_scale, v_kv_norm_g, v_w_k, v_w_v, v_k_norm_g, v_w_q, v_q_norm_g, v_rel_bias, v_w_o, v_w_gate, v_w_up, v_w_down):
    assert x.shape == (1, S, D) and w_gate.shape == (2, D, F_SHARD) and w_k.shape == (D_SHARD, D)
    xin, target = x[0], loss_target[0]

    ffn_shards = [[w_gate[layer].T.astype(BF16), w_up[layer].T.astype(BF16), w_down[layer].astype(BF16)]
                  for layer in range(2)]
    att_shards = [w_k.astype(BF16), w_v.astype(BF16), w_q[0].astype(BF16), w_o[0].astype(BF16)]
    pool_shard = pool_w[0].astype(BF16).reshape(N_GROUPS * POOL_SHARD, GROUP)
    small = jnp.concatenate([pool_b[0].reshape(1, N_GROUPS * POOL_SHARD), pool_scale], axis=1)

    full0 = _run_exchange(_relay_gather_exchange(ffn_shards[0] + [pool_shard, _pad16(small)]), "gather_layer0")
    ffn_w0 = [a.reshape(F, D) for a in full0[:3]]
    pw_f = full0[3].reshape(N_DEV, N_GROUPS, POOL_SHARD, GROUP).transpose(1, 0, 2, 3).reshape(N_GROUPS, GROUP, GROUP)
    small_f = full0[4][:, 0, :]
    pb_f = small_f[:, :N_GROUPS * POOL_SHARD].reshape(N_DEV, N_GROUPS, POOL_SHARD).transpose(1, 0, 2).reshape(1, D)
    ps_f = small_f[:, N_GROUPS * POOL_SHARD:].reshape(1, D)

    g_mix0, g_mix1 = norm_mix_g[0:1], norm_mix_g[1:2]
    g_ffn0, g_ffn1 = norm_ffn_g[0:1], norm_ffn_g[1:2]
    g_kv = kv_norm_g.reshape(1, D)
    gk_t = jnp.tile(k_norm_g.reshape(1, HEAD), (1, N_HEADS))
    gq_t = jnp.tile(q_norm_g.reshape(1, HEAD), (1, N_HEADS))

    x1, diff = _pool_fwd(xin, g_mix0, pw_f, pb_f, ps_f)
    (x2, hf0, gg0, uu0), full_att = _ffn_fwd(x1, g_ffn0, *ffn_w0, name="ffn_fwd", ex=_relay_gather_exchange(att_shards))
    wk_f, wv_f, wq_f, wo_f = [a.reshape(D, D) for a in full_att]
    (hkv, h1, kpre, qpre, kp, vp, qq), full1_gate = _kvq_proj(x2, g_kv, g_mix1, wk_f, wv_f, wq_f, gk_t, gq_t,
                                                                ex=_relay_gather_exchange(ffn_shards[1][:1]))
    bias = _bias_band(_toeplitz_from_table(rel_bias[0]))
    (att,), full1_rest = _attn_fwd(qq, kp, vp, bias, ex=_relay_gather_exchange(ffn_shards[1][1:]))
    ffn_w1 = [a.reshape(F, D) for a in full1_gate + full1_rest]
    x3 = _mm(att, wo_f, "nn", F32, "attn_out", add=x2)
    (dx4, loss_rows, hf1, gg1, uu1), _ = _ffn_fwd(x3, g_ffn1, *ffn_w1, name="ffn_fwd_loss", target=target)

    def blocks(dw):
        return dw.reshape(N_DEV, dw.shape[0] // N_DEV, dw.shape[1])

    (dgg1, duu1, dwg1, dwu1, dwd1), _ = _ffn_bwd_weights(dx4, hf1, gg1, uu1, ffn_w1[2], name="ffn_bwd1")
    parts1 = [blocks(dw) for dw in (dwg1, dwu1, dwd1)]
    (dx3, dg_ffn1), stage1 = _ffn_bwd_input(dx4, x3, g_ffn1, [dgg1, duu1], ffn_w1[:2], "ffn_dx1",
                                            ex=_pair_exchange(parts1))
    both1 = _pair_add(parts1, stage1, "pair_add_ffn1", copies=2)
    flight1, chip1, land1, _ = _chip_exchange_start(both1[:3], both1[3:], dx3, "scatter_ffn1_start")
    datt = _mm(dx3, wo_f, "nt", BF16, "d_attn")
    dwo = _mm(att, dx3, "tn", BF16, "d_wo")
    (dq, dkp, dvp, dband), _ = _attn_bwd(qq, kp, vp, bias, datt)
    (dx2, dqpre, dkpre, dg_mix1, dg_kv, dgq, dgk), _ = _qkv_input_bwd(
        dq, qpre, gq_t, dkp, kpre, gk_t, dvp, wq_f, wk_f, wv_f, dx3, x2, g_mix1, g_kv)
    dwq = _mm(h1, dqpre, "tn", BF16, "d_wq")
    dwk = _mm(hkv, dkpre, "tn", BF16, "d_wk")
    dwv = _mm(hkv, dvp, "tn", BF16, "d_wv", skip_rows=PADK)
    parts_att = [blocks(dw) for dw in (dwk, dwv, dwq, dwo)]
    (dgg0, duu0, *dw0), stage_att = _ffn_bwd_weights(
        dx2, hf0, gg0, uu0, ffn_w0[2], name="ffn_bwd0", ex=_pair_exchange(parts_att))
    both_att = _pair_add(parts_att, stage_att, "pair_add_att", copies=2)
    flight_att, chip_att, land_att, _ = _chip_exchange_start(both_att[:4], both_att[4:], dx2, "scatter_att_start")
    parts0 = [blocks(dw) for dw in dw0]
    (dx1, dg_ffn0), stage0 = _ffn_bwd_input(dx2, x1, g_ffn0, [dgg0, duu0], ffn_w0[:2], "ffn_dx0",
                                            ex=_pair_exchange(parts0))
    both = _pair_add(parts0, stage0, "pair_add_ffn0", copies=2)
    flight, chip0, land0, token = _chip_exchange_start(both[:3], both[3:], dx1, "scatter_ffn0_start")
    (grad_x, dpw, db_rows, ds_rows, dg_mix0), _ = _pool_bwd(dx1, xin, diff, g_mix0 + token[0:1, 0:1], pw_f, pb_f, ps_f)
    dtp, seg = _bias_grad(dband, dpw)

    weights = dict(norm_mix_g=norm_mix_g, norm_ffn_g=norm_ffn_g, pool_w=pool_w, pool_b=pool_b,
                   pool_scale=pool_scale, kv_norm_g=kv_norm_g, w_k=w_k, w_v=w_v, k_norm_g=k_norm_g, w_q=w_q,
                   q_norm_g=q_norm_g, rel_bias=rel_bias, w_o=w_o, w_gate=w_gate, w_up=w_up, w_down=w_down)
    mom1 = dict(norm_mix_g=m_norm_mix_g, norm_ffn_g=m_norm_ffn_g, pool_w=m_pool_w, pool_b=m_pool_b,
                pool_scale=m_pool_scale, kv_norm_g=m_kv_norm_g, w_k=m_w_k, w_v=m_w_v, k_norm_g=m_k_norm_g,
                w_q=m_w_q, q_norm_g=m_q_norm_g, rel_bias=m_rel_bias, w_o=m_w_o, w_gate=m_w_gate, w_up=m_w_up,
                w_down=m_w_down)
    mom2 = dict(norm_mix_g=v_norm_mix_g, norm_ffn_g=v_norm_ffn_g, pool_w=v_pool_w, pool_b=v_pool_b,
                pool_scale=v_pool_scale, kv_norm_g=v_kv_norm_g, w_k=v_w_k, w_v=v_w_v, k_norm_g=v_k_norm_g,
                w_q=v_w_q, q_norm_g=v_q_norm_g, rel_bias=v_rel_bias, w_o=v_w_o, w_gate=v_w_gate, w_up=v_w_up,
                w_down=v_w_down)
    names = list(weights)
    grads, deltas, new_m, new_v = {}, {}, {}, {}

    def summed(arrs, tag, after):
        return [_sum_parts(a, "sum_parts_%s%d" % (tag, k), after) for k, a in enumerate(arrs)]

    def adam_flat(nm):
        shape = weights[nm].shape
        flat = lambda a: a.reshape(-1, shape[-1])
        dl, m1, m2 = _adam(flat(weights[nm]), flat(grads[nm]), flat(mom1[nm]), flat(mom2[nm]), "adam_" + nm)
        deltas[nm], new_m[nm], new_v[nm] = dl.reshape(shape), m1.reshape(shape), m2.reshape(shape)

    ffn_names = ("w_gate", "w_up", "w_down")
    def ffn_view(nm, a):
        return a if nm == "w_down" else a.transpose(0, 2, 1)

    recv1 = _chip_exchange_wait(flight1, chip1, land1, dtp, "scatter_ffn1_wait")
    layer1 = {nm: _adam_layer(ffn_view(nm, weights[nm]), g, ffn_view(nm, mom1[nm]), ffn_view(nm, mom2[nm]), 1,
                              "adam1_" + nm)
              for nm, g in zip(ffn_names, recv1)}
    recv_att = _chip_exchange_wait(flight_att, chip_att, land_att, layer1["w_down"][1], "scatter_att_wait")
    g_wk, g_wv, g_wq, g_wo = summed(recv_att, "att_", token)
    grads.update(w_k=g_wk, w_v=g_wv, w_q=g_wq[None], w_o=g_wo[None])
    for nm in ("w_k", "w_v", "w_q", "w_o"):
        adam_flat(nm)

    dpw_blocks = dpw.reshape(N_GROUPS, N_DEV, POOL_SHARD, GROUP).transpose(1, 0, 2, 3)
    dpw_blocks = dpw_blocks.reshape(N_DEV, N_GROUPS * POOL_SHARD, GROUP).astype(BF16)
    misc = jnp.concatenate([dgk[0:1, 0:HEAD], dgq[0:1, 0:HEAD], loss_rows[0:1, 0:1],
                            seg[:, 0, 0].reshape(1, N_HEADS), seg[:, 0, 1].reshape(1, N_HEADS)], axis=1)
    misc = jnp.pad(misc, ((0, 0), (0, D - misc.shape[1])))
    vec_rows = jnp.concatenate([dg_mix0[0:1], dg_mix1[0:1], dg_ffn0[0:1], dg_ffn1[0:1], dg_kv[0:1],
                                db_rows[0:1], ds_rows[0:1], misc], axis=0)
    pack = jnp.concatenate([vec_rows, dtp.reshape(N_HEADS, TOEP)], axis=0)
    tot, recv_pool = _small_exchanges(pack, dpw_blocks, "small_exchanges", after=deltas["w_o"])

    loss = tot[7, 2 * HEAD]
    seg_tot = jnp.stack([tot[7, 2 * HEAD + 1:2 * HEAD + 1 + N_HEADS],
                         tot[7, 2 * HEAD + 1 + N_HEADS:2 * HEAD + 1 + 2 * N_HEADS]], axis=1)
    me = 4 * lax.axis_index("x") + 2 * lax.axis_index("y") + lax.axis_index("c")
    g_pool_b = lax.dynamic_slice_in_dim(tot[5].reshape(N_GROUPS, GROUP), me * POOL_SHARD, POOL_SHARD, axis=1)
    grads.update(
        norm_mix_g=tot[0:2], norm_ffn_g=tot[2:4], kv_norm_g=tot[4], k_norm_g=tot[7, 0:HEAD],
        q_norm_g=tot[7, HEAD:2 * HEAD].reshape(1, HEAD),
        rel_bias=_table_grad_from_toeplitz(tot[8:8 + N_HEADS], seg_tot).reshape(1, N_HEADS, N_REL),
        pool_b=g_pool_b.reshape(1, N_GROUPS, POOL_SHARD),
        pool_scale=lax.dynamic_slice_in_dim(tot[6:7], me * D_SHARD, D_SHARD, axis=1),
        pool_w=_sum_parts(recv_pool, "sum_parts_pool").reshape(1, N_GROUPS, POOL_SHARD, GROUP))
    adam_flat("pool_w")
    small_names = [nm for nm in names if nm not in ffn_names + ("w_k", "w_v", "w_q", "w_o", "pool_w")]

    def pack_small(tree):
        cols = []
        for nm in small_names:
            flat = tree[nm].reshape(-1)
            cols.append(jnp.pad(flat, (0, -flat.shape[0] % 1024)))
        return jnp.concatenate(cols).reshape(-1, 128)

    dl, m1, m2 = _adam(pack_small(weights), pack_small(grads), pack_small(mom1), pack_small(mom2), "adam_small")

    def unpack_small(packed, out):
        flat, off = packed.reshape(-1), 0
        for nm in small_names:
            size = weights[nm].size
            out[nm] = flat[off:off + size].reshape(weights[nm].shape)
            off += size + (-size % 1024)

    unpack_small(dl, deltas)
    unpack_small(m1, new_m)
    unpack_small(m2, new_v)

    recv0 = _chip_exchange_wait(flight, chip0, land0, dl, "scatter_ffn0_wait")
    for nm, g in zip(ffn_names, recv0):
        done = _adam_layer(ffn_view(nm, weights[nm]), g, ffn_view(nm, mom1[nm]), ffn_view(nm, mom2[nm]), 0,
                           "adam0_" + nm, prev=layer1[nm])
        grads[nm], deltas[nm], new_m[nm], new_v[nm] = [ffn_view(nm, a) for a in done]

    return (loss, grad_x[None], *[grads[nm] for nm in names], *[deltas[nm] for nm in names],
            *[new_m[nm] for nm in names], *[new_v[nm] for nm in names])
```

```python
import functools

import jax
import jax.numpy as jnp
from jax import lax
from jax.experimental import pallas as pl
from jax.experimental.pallas import tpu as pltpu

F32 = jnp.float32
BF16 = jnp.bfloat16
MESH_ID = pl.DeviceIdType.MESH

N_DEV = 8
S = 2048
D = 1024
F = 2816
F_SHARD = F // N_DEV
D_SHARD = D // N_DEV
N_GROUPS = 4
GROUP = D // N_GROUPS
POOL_SHARD = GROUP // N_DEV
MAX_WIN = 16
HEAD = 64
N_HEADS = D // HEAD
CHUNK = 64
LEFT = 8
QB = 4 * CHUNK
KB = QB + LEFT * CHUNK
PADK = LEFT * CHUNK
TOEP = 1024
N_REL = 257
MAX_REL = 128
EPS = 1e-6
NEG_INF = -1e30
ATTN_SCALE = HEAD ** -0.5

ADAM_LR = 0.001
ADAM_B1 = 0.9
ADAM_B2 = 0.999
ADAM_EPS = 1e-08
ADAM_WD = 0.01
ADAM_STEP = 10

VMEM_LIMIT = 52 * 1024 * 1024

ANY = pl.BlockSpec(memory_space=pl.ANY)
VMEM = pl.BlockSpec(memory_space=pltpu.VMEM)


def _call(body, **kw):
    return pl.pallas_call(body, **kw)


def _params(*sem):
    return pltpu.CompilerParams(dimension_semantics=sem, vmem_limit_bytes=VMEM_LIMIT)


def _dot(a, b, dims):
    return lax.dot_general(a, b, (dims, ((), ())), preferred_element_type=F32)


def _nn(a, b):
    return _dot(a, b, ((1,), (0,)))


def _nt(a, b):
    return _dot(a, b, ((1,), (1,)))


def _tn(a, b):
    return _dot(a, b, ((0,), (0,)))


def _rstd(x):
    return lax.rsqrt(jnp.mean(x * x, axis=-1, keepdims=True) + EPS)


def _rms_bwd(dh, x, r, g):
    gd = dh * g
    return r * gd - x * (r * r * r) * jnp.mean(gd * x, axis=-1, keepdims=True)


def _colsum8(v):
    return jnp.broadcast_to(jnp.sum(v, axis=0, keepdims=True), (8, v.shape[1]))


def _seg_sum(v):
    r = lax.broadcasted_iota(jnp.int32, (128, 128), 0) // HEAD
    c = lax.broadcasted_iota(jnp.int32, (128, 128), 1) // HEAD
    ones = jnp.where(r == c, 1.0, 0.0).astype(BF16)
    out = []
    for blk in range(v.shape[1] // 128):
        part = v[:, blk * 128:(blk + 1) * 128]
        hi = part.astype(BF16)
        rest = part - hi.astype(F32)
        mid = rest.astype(BF16)
        lo = (rest - mid.astype(F32)).astype(BF16)
        out.append(_nn(hi, ones) + _nn(mid, ones) + _nn(lo, ones))
    return jnp.concatenate(out, axis=1)


def _place():
    return lax.axis_index("x"), lax.axis_index("y"), lax.axis_index("c")


class _Exchange:
    def __init__(self, ins, out_shape, sems, start, mid, finish, early=None):
        self.ins, self.out_shape, self.sems = list(ins), list(out_shape), list(sems)
        self.start, self.mid, self.finish = start, mid, finish
        self.early = early if early is not None else (lambda ins, outs, sems: None)


def _relay_gather_exchange(shards):
    n = len(shards)

    def tools(ins, outs, sems):
        send_sems, recv_sems, local_sems = sems
        x, y, c = _place()
        me, sib = (x, y, c), (x, y, 1 - c)
        xn, yn, dg = (1 - x, y, c), (x, 1 - y, c), (1 - x, 1 - y, c)
        other = lambda dev: (dev[0], dev[1], 1 - c)

        def piece(k, ref, half):
            rows = shards[k].shape[0] // 2
            return ref if half is None else ref.at[pl.ds(half * rows, rows)]

        def held(k, dev, half):
            return piece(k, outs[k].at[4 * dev[0] + 2 * dev[1] + dev[2]], half)

        def copy(k, s, dev, half, to, own=False):
            return pltpu.make_async_remote_copy(
                src_ref=piece(k, ins[k], half) if own else held(k, dev, half), dst_ref=held(k, dev, half),
                send_sem=send_sems.at[k, s], recv_sem=recv_sems.at[k, s], device_id=to, device_id_type=MESH_ID)

        def mine(k):
            return pltpu.make_async_copy(ins[k], held(k, me, None), local_sems.at[k])

        def sent(k):
            return [copy(k, 0, me, None, sib, own=True), copy(k, 1, me, 0, xn, own=True), copy(k, 2, me, 1, yn, own=True),
                    copy(k, 3, me, 0, yn, own=True), copy(k, 4, me, 1, xn, own=True)]

        relays = [(1, xn, 0, [(5, yn), (7, sib)]), (2, yn, 1, [(6, xn), (9, sib)]), (3, yn, 0, [(10, sib)]),
                  (4, xn, 1, [(8, sib)]), (6, dg, 1, [(11, sib)]), (5, dg, 0, [(12, sib)])]
        from_sibling = [(0, other(me), None), (7, other(xn), 0), (8, other(xn), 1), (9, other(yn), 1),
                        (10, other(yn), 0), (11, other(dg), 1), (12, other(dg), 0)]
        return mine, sent, copy, relays, from_sibling, me

    def start(ins, outs, sems):
        mine, sent, _, _, _, _ = tools(ins, outs, sems)
        for k in range(n):
            mine(k).start()
        for k in range(n):
            for cp in sent(k):
                cp.start()

    def pass_on(ins, outs, sems, which):
        _, _, copy, relays, _, me = tools(ins, outs, sems)
        for k in range(n):
            for s, dev, half, onward in relays[which]:
                copy(k, s, dev, half, me).wait_recv()
                for s2, to in onward:
                    copy(k, s2, dev, half, to).start()

    def early(ins, outs, sems):
        pass_on(ins, outs, sems, slice(0, 2))

    def mid(ins, outs, sems):
        pass_on(ins, outs, sems, slice(2, None))

    def finish(ins, outs, sems):
        mine, sent, copy, relays, from_sibling, me = tools(ins, outs, sems)
        for k in range(n):
            for s, dev, half in from_sibling:
                copy(k, s, dev, half, me).wait_recv()
        for k in range(n):
            for cp in sent(k):
                cp.wait_send()
            for _, dev, half, onward in relays:
                for s2, to in onward:
                    copy(k, s2, dev, half, to).wait_send()
            mine(k).wait()

    return _Exchange(
        shards, [jax.ShapeDtypeStruct((N_DEV,) + a.shape, a.dtype) for a in shards],
        [pltpu.SemaphoreType.DMA((n, 13)), pltpu.SemaphoreType.DMA((n, 13)), pltpu.SemaphoreType.DMA((n,))],
        start, mid, finish, early)


def _peer(x, y, c, m):
    px = 1 - x if m & 4 else x
    py = 1 - y if m & 2 else y
    pc = 1 - c if m & 1 else c
    return px, py, pc


N_CHIPS = N_DEV // 2


def _pair_exchange(parts):
    n = len(parts)

    def copies(ins, outs, sems):
        send_sems, recv_sems = sems
        x, y, c = _place()
        return [pltpu.make_async_remote_copy(
            src_ref=ins[k].at[2 * q + 1 - c], dst_ref=outs[k].at[q], send_sem=send_sems.at[k, q],
            recv_sem=recv_sems.at[k, q], device_id=(x, y, 1 - c), device_id_type=MESH_ID)
            for k in range(n) for q in range(N_CHIPS)]

    def start(ins, outs, sems):
        for cp in copies(ins, outs, sems):
            cp.start()

    def mid(ins, outs, sems):
        pass

    def finish(ins, outs, sems):
        for cp in copies(ins, outs, sems):
            cp.wait_recv()
        for cp in copies(ins, outs, sems):
            cp.wait_send()

    return _Exchange(
        parts, [jax.ShapeDtypeStruct((N_CHIPS,) + a.shape[1:], a.dtype) for a in parts],
        [pltpu.SemaphoreType.DMA((n, N_CHIPS)), pltpu.SemaphoreType.DMA((n, N_CHIPS))], start, mid, finish)


def _pair_add(parts, stage, name, copies=1):
    n = len(parts)
    core = lax.axis_index("c").reshape(1)

    def body(core_ref, *refs):
        for k in range(n):
            mine, theirs = refs[k], refs[n + k]
            total = (mine[0, 0].astype(F32) + theirs[0].astype(F32)).astype(BF16)
            for rep in range(copies):
                refs[(2 + rep) * n + k][0] = total

    in_specs, out_specs = [], []
    for a in parts:
        _, r, cdim = a.shape
        in_specs.append(pl.BlockSpec((1, 1, r // 2, cdim), lambda q, i, core_ref: (q, core_ref[0], i, 0)))
    for a in parts:
        _, r, cdim = a.shape
        in_specs.append(pl.BlockSpec((1, r // 2, cdim), lambda q, i, core_ref: (q, i, 0)))
        out_specs.append(pl.BlockSpec((1, r // 2, cdim), lambda q, i, core_ref: (q, i, 0)))
    return list(_call(
        body, name=name,
        grid_spec=pltpu.PrefetchScalarGridSpec(num_scalar_prefetch=1, grid=(N_CHIPS, 2), in_specs=in_specs,
                                               out_specs=out_specs * copies),
        out_shape=[jax.ShapeDtypeStruct(s.shape, BF16) for s in stage] * copies,
        compiler_params=_params("arbitrary", "arbitrary"),
    )(core, *[a.reshape((N_CHIPS, 2) + a.shape[1:]) for a in parts], *stage))


HBM = pl.BlockSpec(memory_space=pltpu.HBM)
SEMAPHORES = pl.BlockSpec(memory_space=pltpu.SEMAPHORE)


def _chip_copies(srcs, lands, send_sems, recv_sems, mine_is_dst):
    x, y, c = _place()
    me = 2 * x + y
    out = []
    for m in range(1, N_CHIPS):
        px, py, _ = _peer(x, y, c, 2 * m)
        peer = 2 * px + py
        for k in range(len(srcs)):
            pair = k * (N_CHIPS - 1) + m - 1
            out.append(pltpu.make_async_remote_copy(
                src_ref=srcs[k].at[peer], dst_ref=lands[k].at[me if mine_is_dst else peer],
                send_sem=send_sems[pair], recv_sem=recv_sems[pair],
                device_id=(px, py, c), device_id_type=MESH_ID))
    return out


def _chip_exchange_start(chip_parts, lands, after, name):
    n = len(chip_parts)
    pairs = n * (N_CHIPS - 1)

    def body(*refs):
        srcs, zones = refs[:n], refs[n:2 * n]
        sems = refs[2 * n + 1:2 * n + 1 + 2 * pairs]
        token = refs[-1]
        for cp in _chip_copies(srcs, zones, sems[:pairs], sems[pairs:], True):
            cp.start()
        token[...] = jnp.zeros(token.shape, F32)

    thru = [pltpu.HBM(a.shape, a.dtype) for a in chip_parts + lands]
    hbm = [pltpu.with_memory_space_constraint(a, pltpu.HBM) for a in chip_parts + lands]
    res = _call(
        body, name=name,
        out_shape=[pltpu.SemaphoreType.DMA(())] * (2 * pairs) + thru + [jax.ShapeDtypeStruct((8, 128), F32)],
        in_specs=[HBM] * (2 * n) + [ANY], out_specs=[SEMAPHORES] * (2 * pairs) + [HBM] * (2 * n) + [VMEM],
        input_output_aliases={i: 2 * pairs + i for i in range(2 * n)},
        compiler_params=pltpu.CompilerParams(has_side_effects=pltpu.SideEffectType.DATAFLOW_SIDE_EFFECTING),
    )(*hbm, after)
    sems, rest = list(res[:2 * pairs]), res[2 * pairs:]
    return sems, list(rest[:n]), list(rest[n:2 * n]), rest[-1]


def _chip_exchange_wait(sems, chip_parts, lands, after, name):
    n = len(chip_parts)
    pairs = n * (N_CHIPS - 1)

    def body(*refs):
        srcs, zones = refs[:n], refs[n:2 * n]
        sem_refs = refs[2 * n:2 * n + 2 * pairs]
        for cp in _chip_copies(srcs, zones, sem_refs[:pairs], sem_refs[pairs:], False):
            cp.wait_send()
            cp.wait_recv()

    thru = [pltpu.HBM(a.shape, a.dtype) for a in chip_parts + lands]
    res = _call(
        body, name=name, out_shape=thru,
        in_specs=[HBM] * (2 * n) + [SEMAPHORES] * (2 * pairs) + [ANY], out_specs=[HBM] * (2 * n),
        input_output_aliases={i: i for i in range(2 * n)},
        compiler_params=pltpu.CompilerParams(has_side_effects=pltpu.SideEffectType.DATAFLOW_SIDE_EFFECTING),
    )(*chip_parts, *lands, *sems, after)
    return list(res[n:])


def _run_exchange(ex, name, after=None):
    n_in, n_out = len(ex.ins), len(ex.out_shape)
    order = [] if after is None else [after]

    def body(*refs):
        ins, outs = refs[:n_in], refs[n_in + len(order):n_in + len(order) + n_out]
        sems = refs[n_in + len(order) + n_out:]
        ex.start(ins, outs, sems)
        ex.early(ins, outs, sems)
        ex.mid(ins, outs, sems)
        ex.finish(ins, outs, sems)

    return list(_call(body, name=name, out_shape=ex.out_shape, in_specs=[ANY] * (n_in + len(order)),
                      out_specs=[ANY] * n_out, scratch_shapes=ex.sems)(*ex.ins, *order))


def _call_hosting(body, ex, phases, *, in_specs, out_specs, out_shape, scratch_shapes, args, **kw):
    n_in, n_out, n_scr = len(in_specs), len(out_specs), len(scratch_shapes)
    if ex is None:
        res = _call(body, in_specs=in_specs, out_specs=out_specs, out_shape=out_shape,
                    scratch_shapes=scratch_shapes, **kw)(*args)
        return list(res), []
    n_xin, n_xout = len(ex.ins), len(ex.out_shape)

    def hosting(*refs):
        a, b = n_in, n_in + n_xin
        c, d = b + n_out, b + n_out + n_xout
        ins, xins, outs, xouts = refs[:a], refs[a:b], refs[b:c], refs[c:d]
        scr, sems = refs[d:d + n_scr], refs[d + n_scr:]
        first, early, mid, last = phases()

        @pl.when(first)
        def _():
            ex.start(xins, xouts, sems)

        body(*ins, *outs, *scr)

        @pl.when(early)
        def _():
            ex.early(xins, xouts, sems)

        @pl.when(mid)
        def _():
            ex.mid(xins, xouts, sems)

        @pl.when(last)
        def _():
            ex.finish(xins, xouts, sems)

    res = _call(hosting, in_specs=list(in_specs) + [ANY] * n_xin, out_specs=list(out_specs) + [ANY] * n_xout,
                out_shape=list(out_shape) + ex.out_shape, scratch_shapes=list(scratch_shapes) + ex.sems,
                **kw)(*args, *ex.ins)
    return list(res[:n_out]), list(res[n_out:])


def _grid_phases(dims, mid_fraction=0.8, early_fraction=0.4):
    total = 1
    for d in dims:
        total *= d
    mid = min(max(int(total * mid_fraction), 1), total - 1)
    early = min(int(total * early_fraction), mid)

    def phases():
        step = pl.program_id(0)
        for axis in range(1, len(dims)):
            step = step * dims[axis] + pl.program_id(axis)
        return step == 0, step == early, step == mid, step == total - 1
    return phases


def _small_exchanges(pack, blocks, name, after):
    rows = pack.shape[0]

    def body(in_ref, blk_ref, _, out_ref, got_ref, recv, send_sems, recv_sems, blk_send, blk_recv, blk_local):
        x, y, c = _place()
        me = 4 * x + 2 * y + c
        peers = [_peer(x, y, c, m) for m in range(1, N_DEV)]
        index = [4 * px + 2 * py + pc for px, py, pc in peers]

        def block_copy(m, landing):
            return pltpu.make_async_remote_copy(
                src_ref=blk_ref.at[index[m]], dst_ref=got_ref.at[index[m] if landing else me],
                send_sem=blk_send.at[m], recv_sem=blk_recv.at[m], device_id=peers[m], device_id_type=MESH_ID)

        def pack_copy(m, landing):
            return pltpu.make_async_remote_copy(
                src_ref=in_ref, dst_ref=recv.at[index[m] if landing else me], send_sem=send_sems.at[m],
                recv_sem=recv_sems.at[m], device_id=peers[m], device_id_type=MESH_ID)

        own = pltpu.make_async_copy(blk_ref.at[me], got_ref.at[me], blk_local)
        own.start()
        recv[me] = in_ref[...]
        for m in range(N_DEV - 1):
            pack_copy(m, False).start()
            block_copy(m, False).start()
        for m in range(N_DEV - 1):
            pack_copy(m, True).wait_recv()
        acc = recv[0]
        for d in range(1, N_DEV):
            acc = acc + recv[d]
        out_ref[...] = acc
        for m in range(N_DEV - 1):
            block_copy(m, True).wait_recv()
        for m in range(N_DEV - 1):
            pack_copy(m, False).wait_send()
            block_copy(m, False).wait_send()
        own.wait()

    return _call(
        body, name=name,
        out_shape=[jax.ShapeDtypeStruct(pack.shape, F32), jax.ShapeDtypeStruct(blocks.shape, blocks.dtype)],
        in_specs=[VMEM, ANY, ANY], out_specs=[VMEM, ANY],
        scratch_shapes=[pltpu.VMEM((N_DEV, rows, pack.shape[1]), F32)] + [pltpu.SemaphoreType.DMA((7,))] * 4
        + [pltpu.SemaphoreType.DMA],
        compiler_params=pltpu.CompilerParams(vmem_limit_bytes=VMEM_LIMIT),
    )(pack, blocks, after)


POOL_TS = 256


def _pool_counts(first_row, rows, win):
    t = first_row + lax.broadcasted_iota(jnp.int32, (rows, 1), 0)
    return jnp.minimum(t + 1, win).astype(F32)


def _pool_fwd(x, g, w, b, scale):
    nt = S // POOL_TS

    def body(x_ref, g_ref, w_ref, b_ref, s_ref, out_ref, diff_ref, ext):
        i = pl.program_id(0)

        @pl.when(i == 0)
        def _():
            ext[0:MAX_WIN, :] = jnp.zeros((MAX_WIN, D), F32)

        @pl.when(i > 0)
        def _():
            ext[0:MAX_WIN, :] = ext[POOL_TS:POOL_TS + MAX_WIN, :]

        xv = x_ref[...]
        h = xv * _rstd(xv) * g_ref[...]
        ext[MAX_WIN:, :] = h
        for gi in range(N_GROUPS):
            win = 2 << gi
            cols = slice(gi * GROUP, (gi + 1) * GROUP)
            sm = ext[:, cols]
            k = 1
            while k < win:
                sm = sm + pltpu.roll(sm, k, axis=0)
                k *= 2
            pooled = sm[MAX_WIN:, :] / _pool_counts(i * POOL_TS, POOL_TS, win)
            diff = (pooled - h[:, cols]).astype(BF16)
            yv = (_nn(diff, w_ref[gi]) + b_ref[:, cols]) * s_ref[:, cols]
            out_ref[:, cols] = xv[:, cols] + yv
            diff_ref[:, cols] = diff

    row = pl.BlockSpec((1, D), lambda i: (0, 0))
    tile = pl.BlockSpec((POOL_TS, D), lambda i: (i, 0))
    return _call(
        body, name="pool_fwd", grid=(nt,),
        in_specs=[tile, row, pl.BlockSpec((N_GROUPS, GROUP, GROUP), lambda i: (0, 0, 0)), row, row],
        out_specs=[tile, tile],
        out_shape=[jax.ShapeDtypeStruct((S, D), F32), jax.ShapeDtypeStruct((S, D), BF16)],
        scratch_shapes=[pltpu.VMEM((POOL_TS + MAX_WIN, D), F32)],
        compiler_params=_params("arbitrary"),
    )(x, g, w, b, scale)


def _pool_bwd(dy, x, diff, g, w, b, scale, ex=None):
    nt = S // POOL_TS

    def body(dy_ref, x_ref, diff_ref, g_ref, w_ref, b_ref, s_ref, gx_ref, dw_ref, db_ref, ds_ref, dg_ref, ext, dh):
        i = pl.program_id(0)
        first_row = (nt - 1 - i) * POOL_TS

        @pl.when(i == 0)
        def _():
            ext[POOL_TS:, :] = jnp.zeros((MAX_WIN, D), F32)
            dw_ref[...] = jnp.zeros(dw_ref.shape, F32)
            db_ref[...] = jnp.zeros(db_ref.shape, F32)
            ds_ref[...] = jnp.zeros(ds_ref.shape, F32)
            dg_ref[...] = jnp.zeros(dg_ref.shape, F32)

        @pl.when(i > 0)
        def _():
            ext[POOL_TS:, :] = ext[0:MAX_WIN, :]

        dyv = dy_ref[...]
        for gi in range(N_GROUPS):
            win = 2 << gi
            cols = slice(gi * GROUP, (gi + 1) * GROUP)
            dfb = diff_ref[:, cols]
            z = _nn(dfb, w_ref[gi]) + b_ref[:, cols]
            dyg = dyv[:, cols]
            ds_ref[:, cols] += _colsum8(dyg * z)
            dz = dyg * s_ref[:, cols]
            db_ref[:, cols] += _colsum8(dz)
            dzb = dz.astype(BF16)
            dw_ref[gi] += _tn(dfb, dzb)
            ddiff = _nt(dzb, w_ref[gi])
            ext[0:POOL_TS, cols] = ddiff / _pool_counts(first_row, POOL_TS, win)
            sm = ext[:, cols]
            k = 1
            while k < win:
                sm = sm + pltpu.roll(sm, POOL_TS + MAX_WIN - k, axis=0)
                k *= 2
            dh[:, cols] = sm[0:POOL_TS, :] - ddiff
        xv = x_ref[...]
        r = _rstd(xv)
        gv = g_ref[...]
        dhv = dh[...]
        dg_ref[...] += _colsum8(dhv * xv * r)
        gx_ref[...] = dyv + _rms_bwd(dhv, xv, r, gv)

    row = pl.BlockSpec((1, D), lambda i: (0, 0))
    tile = pl.BlockSpec((POOL_TS, D), lambda i: (nt - 1 - i, 0))
    acc = pl.BlockSpec((8, D), lambda i: (0, 0))
    wspec = pl.BlockSpec((N_GROUPS, GROUP, GROUP), lambda i: (0, 0, 0))
    return _call_hosting(
        body, ex, _grid_phases((nt,)), name="pool_bwd", grid=(nt,),
        in_specs=[tile, tile, tile, row, wspec, row, row],
        out_specs=[tile, wspec, acc, acc, acc],
        out_shape=[jax.ShapeDtypeStruct((S, D), F32), jax.ShapeDtypeStruct((N_GROUPS, GROUP, GROUP), F32),
                   jax.ShapeDtypeStruct((8, D), F32), jax.ShapeDtypeStruct((8, D), F32),
                   jax.ShapeDtypeStruct((8, D), F32)],
        scratch_shapes=[pltpu.VMEM((POOL_TS + MAX_WIN, D), F32), pltpu.VMEM((POOL_TS, D), F32)],
        args=(dy, x, diff, g, w, b, scale), compiler_params=_params("arbitrary"),
    )


FFN_TS = min(S, 1024)
FFN_TF = 256


def _ffn_fwd(x, g, wg_t, wu_t, wd, name, target=None, ex=None):
    width = wd.shape[0]
    ni, nj = S // FFN_TS, width // FFN_TF
    with_loss = target is not None
    extra = [target] if with_loss else []

    def body(*refs):
        x_ref, g_ref, wg_ref, wu_ref, wd_ref = refs[:5]
        if with_loss:
            t_ref, out_ref, loss_ref, h_ref, gg_ref, uu_ref, hs, acc = refs[5:]
        else:
            out_ref, h_ref, gg_ref, uu_ref, hs, acc = refs[5:]
        i, j = pl.program_id(0), pl.program_id(1)

        @pl.when(j == 0)
        def _():
            xv = x_ref[...]
            hb = (xv * _rstd(xv) * g_ref[...]).astype(BF16)
            hs[...] = hb
            h_ref[...] = hb
            acc[...] = jnp.zeros(acc.shape, F32)

        hb = hs[...]
        gg = _nt(hb, wg_ref[...])
        uu = _nt(hb, wu_ref[...])
        gg_ref[...] = gg
        uu_ref[...] = uu
        a = (gg * jax.nn.sigmoid(gg) * uu).astype(BF16)
        acc[...] += _nn(a, wd_ref[...])

        @pl.when(j == nj - 1)
        def _():
            yv = x_ref[...] + acc[...]
            if with_loss:
                err = yv - t_ref[...]
                out_ref[...] = err * (1.0 / D)
                part = jnp.sum(err * err) * (0.5 / D)

                @pl.when(i == 0)
                def _():
                    loss_ref[...] = jnp.zeros(loss_ref.shape, F32)

                loss_ref[...] += jnp.broadcast_to(part, loss_ref.shape)
            else:
                out_ref[...] = yv

    xt = pl.BlockSpec((FFN_TS, D), lambda i, j: (i, 0))
    row = pl.BlockSpec((1, D), lambda i, j: (0, 0))
    wt = pl.BlockSpec((FFN_TF, D), lambda i, j: (j, 0))
    gt = pl.BlockSpec((FFN_TS, FFN_TF), lambda i, j: (i, j))
    in_specs = [xt, row, wt, wt, wt] + [xt] * len(extra)
    out_specs = [xt] + ([pl.BlockSpec((8, 128), lambda i, j: (0, 0))] if with_loss else []) + [xt, gt, gt]
    out_shape = ([jax.ShapeDtypeStruct((S, D), F32)] + ([jax.ShapeDtypeStruct((8, 128), F32)] if with_loss else [])
                 + [jax.ShapeDtypeStruct((S, D), BF16), jax.ShapeDtypeStruct((S, width), F32),
                    jax.ShapeDtypeStruct((S, width), F32)])
    args = (x, g, wg_t, wu_t, wd, *extra)
    return _call_hosting(
        body, ex, _grid_phases((ni, nj), 0.8), name=name, grid=(ni, nj),
        in_specs=in_specs, out_specs=out_specs, out_shape=out_shape,
        scratch_shapes=[pltpu.VMEM((FFN_TS, D), BF16), pltpu.VMEM((FFN_TS, D), F32)], args=args,
        compiler_params=_params("arbitrary", "arbitrary"),
    )


def _ffn_bwd_weights(dout, h, gg, uu, wd, name, ex=None):
    width = wd.shape[0]
    nj = width // FFN_TF

    def body(do_ref, h_ref, gg_ref, uu_ref, wd_ref, dg_ref, du_ref, dwg_ref, dwu_ref, dwd_ref, dob, kept):
        j = pl.program_id(0)

        @pl.when(j == 0)
        def _():
            dob[...] = do_ref[...].astype(BF16)
            kept[1] = jnp.zeros(kept.shape[1:], BF16)

        prev, cur = (j + 1) % 2, j % 2
        hb, dov = h_ref[...], dob[...]
        dwg_ref[...] = _tn(kept[prev, 0], hb).astype(BF16)
        dwu_ref[...] = _tn(kept[prev, 1], hb).astype(BF16)
        dwd_ref[...] = _tn(kept[prev, 2], dov).astype(BF16)
        gv, uv = gg_ref[...], uu_ref[...]
        da = _nt(dov, wd_ref[...])
        sg = jax.nn.sigmoid(gv)
        sl = gv * sg
        dub = (da * sl).astype(BF16)
        dgb = (da * uv * (sg * (1.0 + gv * (1.0 - sg)))).astype(BF16)
        dg_ref[...] = dgb
        du_ref[...] = dub
        kept[cur, 0] = dgb
        kept[cur, 1] = dub
        kept[cur, 2] = (sl * uv).astype(BF16)

    once = pl.Buffered(1)
    whole = lambda: pl.BlockSpec((S, D), lambda j, _: (0, 0), pipeline_mode=once)
    this = lambda j: jnp.minimum(j, nj - 1)
    last = lambda j: jnp.maximum(j - 1, 0)
    wt_in = pl.BlockSpec((FFN_TF, D), lambda j, _: (this(j), 0))
    wt_out = pl.BlockSpec((FFN_TF, D), lambda j, _: (last(j), 0))
    gt = pl.BlockSpec((S, FFN_TF), lambda j, _: (0, this(j)))
    return _call_hosting(
        body, ex, _grid_phases((nj + 1, 1)), name=name, grid=(nj + 1, 1),
        in_specs=[whole(), whole(), gt, gt, wt_in], out_specs=[gt, gt, wt_out, wt_out, wt_out],
        out_shape=[jax.ShapeDtypeStruct((S, width), BF16)] * 2 + [jax.ShapeDtypeStruct((width, D), BF16)] * 3,
        scratch_shapes=[pltpu.VMEM((S, D), BF16), pltpu.VMEM((2, 3, S, FFN_TF), BF16)], args=(dout, h, gg, uu, wd),
        compiler_params=_params("arbitrary", "arbitrary"),
    )


BWD_TS = 512


def _ffn_bwd_input(dres, x, g, grads, weights, name, ex=None):
    nt = S // BWD_TS
    n = len(grads)

    def body(*refs):
        dres_ref, x_ref, g_ref = refs[:3]
        grad_refs, w_refs = refs[3:3 + n], refs[3 + n:3 + 2 * n]
        dx_ref, dgam_ref = refs[3 + 2 * n:]
        dh = _nn(grad_refs[0][...], w_refs[0][...])
        for k in range(1, n):
            dh = dh + _nn(grad_refs[k][...], w_refs[k][...])
        xv = x_ref[...]
        r = _rstd(xv)
        dx_ref[...] = dres_ref[...] + _rms_bwd(dh, xv, r, g_ref[...])

        @pl.when(pl.program_id(0) == 0)
        def _():
            dgam_ref[...] = jnp.zeros(dgam_ref.shape, F32)

        dgam_ref[...] += _colsum8(dh * xv * r)

    tile = pl.BlockSpec((BWD_TS, D), lambda i: (i, 0))
    ftiles = [pl.BlockSpec((BWD_TS, a.shape[1]), lambda i: (i, 0)) for a in grads]
    wspecs = [pl.BlockSpec(w.shape, lambda i: (0, 0), pipeline_mode=pl.Buffered(1)) for w in weights]
    return _call_hosting(
        body, ex, _grid_phases((nt,)), name=name, grid=(nt,),
        in_specs=[tile, tile, pl.BlockSpec((1, D), lambda i: (0, 0))] + ftiles + wspecs,
        out_specs=[tile, pl.BlockSpec((8, D), lambda i: (0, 0))],
        out_shape=[jax.ShapeDtypeStruct((S, D), F32), jax.ShapeDtypeStruct((8, D), F32)],
        scratch_shapes=[], args=(dres, x, g, *grads, *weights), compiler_params=_params("arbitrary"),
    )


def _mm(a, b, mode, out_dtype, name, add=None, skip_rows=0):
    if mode == "nn":
        (m, kd), n = (a.shape[0] - skip_rows, a.shape[1]), b.shape[1]
    elif mode == "nt":
        (m, kd), n = (a.shape[0] - skip_rows, a.shape[1]), b.shape[0]
    else:
        (kd, m), n = a.shape, b.shape[1]
    tm, tn, tk = min(m, 1024), min(n, 1024), min(kd, 1024)
    if skip_rows and mode == "tn":
        tk = min(tk, skip_rows)
    elif skip_rows:
        tm = min(tm, skip_rows)
    skip_a = skip_rows // tm if mode != "tn" else 0
    skip_b = skip_rows // tk if mode == "tn" else 0
    assert skip_rows == skip_a * tm + skip_b * tk
    nk = kd // tk
    dot = {"nn": _nn, "nt": _nt, "tn": _tn}[mode]

    def body(*refs):
        if add is None:
            a_ref, b_ref, o_ref, acc = refs
        else:
            a_ref, b_ref, add_ref, o_ref, acc = refs
        k = pl.program_id(2)

        @pl.when(k == 0)
        def _():
            acc[...] = jnp.zeros(acc.shape, F32)

        acc[...] += dot(a_ref[...].astype(BF16), b_ref[...].astype(BF16))

        @pl.when(k == nk - 1)
        def _():
            res = acc[...]
            if add is not None:
                res = res + add_ref[...]
            o_ref[...] = res.astype(out_dtype)

    if mode == "tn":
        a_spec = pl.BlockSpec((tk, tm), lambda i, j, k: (k, i))
        b_spec = pl.BlockSpec((tk, tn), lambda i, j, k: (k + skip_b, j))
    else:
        a_spec = pl.BlockSpec((tm, tk), lambda i, j, k: (i + skip_a, k))
        b_spec = (pl.BlockSpec((tk, tn), lambda i, j, k: (k, j)) if mode == "nn"
                  else pl.BlockSpec((tn, tk), lambda i, j, k: (j, k)))
    o_spec = pl.BlockSpec((tm, tn), lambda i, j, k: (i, j))
    in_specs = [a_spec, b_spec] + ([o_spec] if add is not None else [])
    args = (a, b) + ((add,) if add is not None else ())
    return _call(
        body, name=name, grid=(m // tm, n // tn, nk), in_specs=in_specs, out_specs=o_spec,
        out_shape=jax.ShapeDtypeStruct((m, n), out_dtype), scratch_shapes=[pltpu.VMEM((tm, tn), F32)],
        compiler_params=_params("parallel", "parallel", "arbitrary"),
    )(*args)


PROJ_TS = 256


def _kvq_proj(x, g_kv, g_mix, wk, wv, wq, gk, gq, ex=None):
    lead = PADK // PROJ_TS

    def body(x_ref, gkv_ref, gmix_ref, wk_ref, wv_ref, wq_ref, gk_ref, gq_ref,
             hkv_ref, h1_ref, kpre_ref, qpre_ref, k_ref, v_ref, q_ref):
        i = pl.program_id(0)

        @pl.when(i < lead)
        def _():
            k_ref[...] = jnp.zeros(k_ref.shape, BF16)
            v_ref[...] = jnp.zeros(v_ref.shape, BF16)

        @pl.when(i >= lead)
        def _():
            xv = x_ref[...]
            xr = xv * _rstd(xv)
            hkv = (xr * gkv_ref[...]).astype(BF16)
            h1 = (xr * gmix_ref[...]).astype(BF16)
            hkv_ref[...] = hkv
            h1_ref[...] = h1
            kpre = _nn(hkv, wk_ref[...])
            qpre = _nn(h1, wq_ref[...])
            kpre_ref[...] = kpre
            qpre_ref[...] = qpre
            v_ref[...] = _nn(hkv, wv_ref[...]).astype(BF16)
            rk = lax.rsqrt(_seg_sum(kpre * kpre) * (1.0 / HEAD) + EPS)
            k_ref[...] = (kpre * rk * gk_ref[...]).astype(BF16)
            rq = lax.rsqrt(_seg_sum(qpre * qpre) * (1.0 / HEAD) + EPS)
            q_ref[...] = (qpre * rq * gq_ref[...]).astype(BF16)

    tile = pl.BlockSpec((PROJ_TS, D), lambda i: (jnp.maximum(i - lead, 0), 0))
    padded = pl.BlockSpec((PROJ_TS, D), lambda i: (i, 0))
    row = pl.BlockSpec((1, D), lambda i: (0, 0))
    wspec = pl.BlockSpec((D, D), lambda i: (0, 0))
    bf = jax.ShapeDtypeStruct((S, D), BF16)
    ff = jax.ShapeDtypeStruct((S, D), F32)
    bp = jax.ShapeDtypeStruct((PADK + S, D), BF16)
    return _call_hosting(
        body, ex, _grid_phases((lead + S // PROJ_TS,), 0.85), name="kvq_proj", grid=(lead + S // PROJ_TS,),
        in_specs=[tile, row, row, wspec, wspec, wspec, row, row],
        out_specs=[tile, tile, tile, tile, padded, padded, tile], out_shape=[bf, bf, ff, ff, bp, bp, bf],
        scratch_shapes=[], args=(x, g_kv, g_mix, wk, wv, wq, gk, gq), compiler_params=_params("arbitrary"),
    )


def _qkv_input_bwd(dq, qpre, gq, dkp, kpre, gk, dvp, wq, wk, wv, dres, x, g_mix, g_kv, ex=None):
    nt = S // PROJ_TS
    lead = PADK // PROJ_TS

    def head_bwd(dov, pv, hgv):
        r = lax.rsqrt(_seg_sum(pv * pv) * (1.0 / HEAD) + EPS)
        gd = dov * hgv
        dpre = r * gd - pv * (r * r * r) * (_seg_sum(gd * pv) * (1.0 / HEAD))
        return dpre.astype(BF16), _colsum8(dov * pv * r)

    def fold_heads(full):
        fold = full[:, 0:128]
        for blk in range(1, D // 128):
            fold = fold + full[:, blk * 128:(blk + 1) * 128]
        return fold + pltpu.roll(fold, HEAD, axis=1)

    def body(dq_ref, qpre_ref, gq_ref, dk_ref, kpre_ref, gk_ref, dv_ref, wq_ref, wk_ref, wv_ref, dres_ref, x_ref,
             gmix_ref, gkv_ref, dx_ref, dqpre_ref, dkpre_ref, dgmix_ref, dgkv_ref, dgq_ref, dgk_ref, accq, acck):
        i = pl.program_id(0)

        @pl.when(i == 0)
        def _():
            accq[...] = jnp.zeros(accq.shape, F32)
            acck[...] = jnp.zeros(acck.shape, F32)
            dgmix_ref[...] = jnp.zeros(dgmix_ref.shape, F32)
            dgkv_ref[...] = jnp.zeros(dgkv_ref.shape, F32)

        dqb, cq = head_bwd(dq_ref[...], qpre_ref[...], gq_ref[...])
        dkb, ck = head_bwd(dk_ref[...], kpre_ref[...], gk_ref[...])
        dqpre_ref[...] = dqb
        dkpre_ref[...] = dkb
        accq[...] += cq
        acck[...] += ck
        dh1 = _nt(dqb, wq_ref[...])
        dhkv = _nt(dkb, wk_ref[...]) + _nt(dv_ref[...].astype(BF16), wv_ref[...])
        xv = x_ref[...]
        r = _rstd(xv)
        dx_ref[...] = dres_ref[...] + _rms_bwd(dh1, xv, r, gmix_ref[...]) + _rms_bwd(dhkv, xv, r, gkv_ref[...])
        dgmix_ref[...] += _colsum8(dh1 * xv * r)
        dgkv_ref[...] += _colsum8(dhkv * xv * r)

        @pl.when(i == nt - 1)
        def _():
            dgq_ref[...] = fold_heads(accq[...])
            dgk_ref[...] = fold_heads(acck[...])

    tile = pl.BlockSpec((PROJ_TS, D), lambda i: (i, 0))
    behind = pl.BlockSpec((PROJ_TS, D), lambda i: (i + lead, 0))
    row = pl.BlockSpec((1, D), lambda i: (0, 0))
    wspec = lambda: pl.BlockSpec((D, D), lambda i: (0, 0), pipeline_mode=pl.Buffered(1))
    acc = pl.BlockSpec((8, D), lambda i: (0, 0))
    small = pl.BlockSpec((8, 128), lambda i: (0, 0))
    bf = jax.ShapeDtypeStruct((S, D), BF16)
    return _call_hosting(
        body, ex, _grid_phases((nt,)), name="qkv_input_bwd", grid=(nt,),
        in_specs=[tile, tile, row, behind, tile, row, behind, wspec(), wspec(), wspec(), tile, tile, row, row],
        out_specs=[tile, tile, tile, acc, acc, small, small],
        out_shape=[jax.ShapeDtypeStruct((S, D), F32), bf, bf, jax.ShapeDtypeStruct((8, D), F32),
                   jax.ShapeDtypeStruct((8, D), F32), jax.ShapeDtypeStruct((8, 128), F32),
                   jax.ShapeDtypeStruct((8, 128), F32)],
        scratch_shapes=[pltpu.VMEM((8, D), F32), pltpu.VMEM((8, D), F32)],
        args=(dq, qpre, gq, dkp, kpre, gk, dvp, wq, wk, wv, dres, x, g_mix, g_kv),
        compiler_params=_params("arbitrary"),
    )


def _toeplitz_from_table(table):
    far = jnp.broadcast_to(table[:, N_REL - 1:], (N_HEADS, PADK - MAX_REL + 1))
    near = table[:, N_REL - 2::-1]
    past = jnp.broadcast_to(table[:, 0:1], (N_HEADS, MAX_REL))
    wrap = jnp.broadcast_to(table[:, N_REL - 1:], (N_HEADS, TOEP - (PADK + 2 * MAX_REL + 1)))
    return jnp.concatenate([far, near, past, wrap], axis=1).reshape(N_HEADS, 1, TOEP)


def _table_grad_from_toeplitz(dtp, seg):
    lo = PADK - MAX_REL + 1
    near = dtp[:, lo + N_REL - 3:lo - 1:-1]
    return jnp.concatenate([seg[:, 1:2], near, seg[:, 0:1]], axis=1)


def _bias_band(tp):
    def body(tp_ref, out_ref):
        bv = pltpu.roll(jnp.broadcast_to(tp_ref[0], (QB, TOEP)), 0, axis=1, stride=1, stride_axis=0)
        out_ref[0] = jnp.where(_band_mask(), bv[:, 0:KB], NEG_INF)

    return _call(
        body, name="bias_band", grid=(N_HEADS,),
        in_specs=[pl.BlockSpec((1, 1, TOEP), lambda h: (h, 0, 0))],
        out_specs=pl.BlockSpec((1, QB, KB), lambda h: (h, 0, 0)),
        out_shape=jax.ShapeDtypeStruct((N_HEADS, QB, KB), F32),
        compiler_params=_params("parallel"),
    )(tp)


def _bias_grad(dband, after):
    lo, hi = PADK - MAX_REL + 1, PADK + MAX_REL

    def body(db_ref, _, dtp_ref, seg_ref):
        bv = jnp.concatenate([db_ref[0], jnp.zeros((QB, TOEP - KB), F32)], axis=1)
        row = lax.broadcasted_iota(jnp.int32, (QB, TOEP), 0)
        k = 1
        while k < QB:
            bv = jnp.where((row & k) != 0, pltpu.roll(bv, TOEP - k, axis=1), bv)
            k *= 2
        col = jnp.sum(bv, axis=0, keepdims=True)
        dtp_ref[0] = col
        u = lax.broadcasted_iota(jnp.int32, (1, TOEP), 1)
        far = jnp.sum(jnp.where((u < lo) | (u > hi + MAX_REL), col, 0.0))
        past = jnp.sum(jnp.where((u >= hi) & (u <= hi + MAX_REL), col, 0.0))
        lane = lax.broadcasted_iota(jnp.int32, (1, 128), 1)
        seg_ref[0] = jnp.where(lane == 0, far, jnp.where(lane == 1, past, 0.0))

    return _call(
        body, name="bias_grad", grid=(N_HEADS,),
        in_specs=[pl.BlockSpec((1, QB, KB), lambda h: (h, 0, 0)), ANY],
        out_specs=[pl.BlockSpec((1, 1, TOEP), lambda h: (h, 0, 0)), pl.BlockSpec((1, 1, 128), lambda h: (h, 0, 0))],
        out_shape=[jax.ShapeDtypeStruct((N_HEADS, 1, TOEP), F32), jax.ShapeDtypeStruct((N_HEADS, 1, 128), F32)],
        compiler_params=_params("parallel"),
    )(dband, after)


N_QB = S // QB
HEADS_PER_STEP = 4
ATT_LANES = HEADS_PER_STEP * HEAD
N_HG = D // ATT_LANES


def _band_mask():
    qc = lax.broadcasted_iota(jnp.int32, (QB, KB), 0) // CHUNK
    kc = lax.broadcasted_iota(jnp.int32, (QB, KB), 1) // CHUNK
    return (kc >= qc) & (kc <= qc + LEFT)


def _half_scale(hh, scale):
    lane = lax.broadcasted_iota(jnp.int32, (1, 128), 1)
    return jnp.where((lane < HEAD) == (hh == 0), scale, 0.0).astype(BF16)


def _probs(qh, kb, bias, first_key):
    sc = _nt(qh, kb) + bias
    if first_key is not None:
        sc = jnp.where(lax.broadcasted_iota(jnp.int32, (QB, KB), 1) >= first_key, sc, NEG_INF)
    e = jnp.exp(sc - jnp.max(sc, axis=-1, keepdims=True))
    return e * (1.0 / jnp.sum(e, axis=-1, keepdims=True))


def _by_padding(cb, compute):
    @pl.when(cb < PADK // QB)
    def _():
        compute(PADK - cb * QB)

    @pl.when(cb >= PADK // QB)
    def _():
        compute(None)


def _attn_fwd(q, kp, vp, bias, ex=None):
    def body(q_ref, k_ref, v_ref, b_ref, o_ref):
        cb = pl.program_id(1)
        band = pl.ds(pl.multiple_of(cb * QB, QB), KB)
        low = lax.broadcasted_iota(jnp.int32, (QB, 128), 1) < HEAD

        def compute(first_key):
            for pair in range(HEADS_PER_STEP // 2):
                lanes = pl.ds(pair * 128, 128)
                kb, vb, qv = k_ref[band, lanes], v_ref[band, lanes], q_ref[:, lanes]
                outs = []
                for hh in range(2):
                    pb = _probs(qv * _half_scale(hh, ATTN_SCALE), kb, b_ref[2 * pair + hh], first_key).astype(BF16)
                    outs.append(_nn(pb, vb))
                o_ref[:, lanes] = jnp.where(low, outs[0], outs[1]).astype(BF16)

        _by_padding(cb, compute)

    qspec = pl.BlockSpec((QB, ATT_LANES), lambda hg, cb: (cb, hg))
    kspec = pl.BlockSpec((PADK + S, ATT_LANES), lambda hg, cb: (0, hg))
    return _call_hosting(
        body, ex, _grid_phases((N_HG, N_QB), 0.85), name="attn_fwd", grid=(N_HG, N_QB),
        in_specs=[qspec, kspec, kspec, pl.BlockSpec((HEADS_PER_STEP, QB, KB), lambda hg, cb: (hg, 0, 0))],
        out_specs=[qspec], out_shape=[jax.ShapeDtypeStruct((S, D), BF16)], scratch_shapes=[],
        args=(q, kp, vp, bias), compiler_params=_params("arbitrary", "arbitrary"),
    )


def _attn_bwd(q, kp, vp, bias, do, ex=None):
    def body(q_ref, k_ref, v_ref, b_ref, do_ref, dq_ref, dk_ref, dv_ref, db_ref):
        cb = pl.program_id(1)

        @pl.when(cb == 0)
        def _():
            dk_ref[...] = jnp.zeros(dk_ref.shape, F32)
            dv_ref[...] = jnp.zeros(dv_ref.shape, F32)
            db_ref[...] = jnp.zeros(db_ref.shape, F32)

        band = pl.ds(pl.multiple_of(cb * QB, QB), KB)
        low = lax.broadcasted_iota(jnp.int32, (QB, 128), 1) < HEAD

        def compute(first_key):
            for pair in range(HEADS_PER_STEP // 2):
                lanes = pl.ds(pair * 128, 128)
                kb, vb = k_ref[band, lanes], v_ref[band, lanes]
                qv, dov = q_ref[:, lanes], do_ref[:, lanes]
                dq = jnp.zeros((QB, 128), F32)
                dkb = jnp.zeros((KB, 128), F32)
                dvb = jnp.zeros((KB, 128), F32)
                for hh in range(2):
                    sel = low if hh == 0 else jnp.logical_not(low)
                    doh = dov * _half_scale(hh, 1.0)
                    p = _probs(qv * _half_scale(hh, ATTN_SCALE), kb, b_ref[2 * pair + hh], first_key)
                    dp = _nt(doh, vb)
                    dvb = dvb + _tn(p.astype(BF16), doh)
                    ds = p * (dp - jnp.sum(dp * p, axis=-1, keepdims=True))
                    db_ref[2 * pair + hh] += ds
                    dsb = (ds * ATTN_SCALE).astype(BF16)
                    dq = dq + jnp.where(sel, _nn(dsb, kb), 0.0)
                    dkb = dkb + _tn(dsb, qv * _half_scale(hh, 1.0))
                dq_ref[:, lanes] = dq
                dk_ref[band, lanes] += dkb
                dv_ref[band, lanes] += dvb

        _by_padding(cb, compute)

    qspec = pl.BlockSpec((QB, ATT_LANES), lambda hg, cb: (cb, hg))
    kspec = pl.BlockSpec((PADK + S, ATT_LANES), lambda hg, cb: (0, hg))
    bspec = pl.BlockSpec((HEADS_PER_STEP, QB, KB), lambda hg, cb: (hg, 0, 0))
    kf = jax.ShapeDtypeStruct((PADK + S, D), F32)
    return _call_hosting(
        body, ex, _grid_phases((N_HG, N_QB)), name="attn_bwd", grid=(N_HG, N_QB),
        in_specs=[qspec, kspec, kspec, bspec, qspec],
        out_specs=[qspec, kspec, kspec, bspec],
        out_shape=[jax.ShapeDtypeStruct((S, D), F32), kf, kf, jax.ShapeDtypeStruct((N_HEADS, QB, KB), F32)],
        scratch_shapes=[], args=(q, kp, vp, bias, do), compiler_params=_params("arbitrary", "arbitrary"),
    )


def _adam_math(w, g, m, v):
    m = ADAM_B1 * m + (1.0 - ADAM_B1) * g
    v = ADAM_B2 * v + (1.0 - ADAM_B2) * (g * g)
    m_hat = m / (1.0 - ADAM_B1 ** ADAM_STEP)
    v_hat = v / (1.0 - ADAM_B2 ** ADAM_STEP)
    delta = -ADAM_LR * (m_hat / (jnp.sqrt(v_hat) + ADAM_EPS) + ADAM_WD * w)
    return delta, m, v


def _row_tile(r):
    for cand in (256, 176, 128, 64, 32, 16, 8):
        if r % cand == 0:
            return cand
    return r


def _adam_layer(w, parts, m, v, layer, name, prev=None):
    nl, r, c = w.shape
    count = parts.shape[0]
    tr = _row_tile(r)

    def body(w_ref, g_ref, m_ref, v_ref, *rest):
        go_ref, d_ref, nm_ref, nv_ref = rest[-4:]
        gv = g_ref[0].astype(F32)
        for q in range(1, count):
            gv = gv + g_ref[q].astype(F32)
        delta, nm, nv = _adam_math(w_ref[0], gv, m_ref[0], v_ref[0])
        go_ref[0] = gv
        d_ref[0] = delta
        nm_ref[0] = nm
        nv_ref[0] = nv

    lspec = pl.BlockSpec((1, tr, c), lambda i: (layer, i, 0))
    sd = jax.ShapeDtypeStruct((nl, r, c), F32)
    extra = list(prev) if prev is not None else []
    return _call(
        body, name=name, grid=(r // tr,),
        in_specs=[lspec, pl.BlockSpec((count, tr, c), lambda i: (0, i, 0)), lspec, lspec] + [ANY] * len(extra),
        out_specs=[lspec] * 4, out_shape=[sd] * 4,
        input_output_aliases={4 + t: t for t in range(len(extra))},
        compiler_params=_params("parallel"),
    )(w, parts, m, v, *extra)


def _adam(w, g, m, v, name):
    r, c = w.shape
    tr = _row_tile(r)
    count = g.shape[0] if g.ndim == 3 else 0

    def body(w_ref, g_ref, m_ref, v_ref, go_ref, d_ref, nm_ref, nv_ref):
        if count:
            gv = g_ref[0].astype(F32)
            for q in range(1, count):
                gv = gv + g_ref[q].astype(F32)
        else:
            gv = g_ref[...]
        delta, nm, nv = _adam_math(w_ref[...], gv, m_ref[...], v_ref[...])
        go_ref[...] = gv
        d_ref[...] = delta
        nm_ref[...] = nm
        nv_ref[...] = nv

    spec = pl.BlockSpec((tr, c), lambda i: (i, 0))
    gspec = pl.BlockSpec((count, tr, c), lambda i: (0, i, 0)) if count else spec
    sd = jax.ShapeDtypeStruct((r, c), F32)
    return _call(
        body, name=name, grid=(r // tr,), in_specs=[spec, gspec, spec, spec], out_specs=[spec] * 4,
        out_shape=[sd] * 4, compiler_params=_params("parallel"),
    )(w, g, m, v)


def _pad16(rows):
    return jnp.pad(rows, ((0, 16 - rows.shape[0]), (0, 0)))


def kernel(x, norm_mix_g, norm_ffn_g, pool_w, pool_b, pool_scale, kv_norm_g, w_k, w_v, k_norm_g, w_q, q_norm_g, rel_bias, w_o, w_gate, w_up, w_down, loss_target, m_norm_mix_g, m_norm_ffn_g, m_pool_w, m_pool_b, m_pool_scale, m_kv_norm_g, m_w_k, m_w_v, m_k_norm_g, m_w_q, m_q_norm_g, m_rel_bias, m_w_o, m_w_gate, m_w_up, m_w_down, v_norm_mix_g, v_norm_ffn_g, v_pool_w, v_pool_b, v_pool_scale, v_kv_norm_g, v_w_k, v_w_v, v_k_norm_g, v_w_q, v_q_norm_g, v_rel_bias, v_w_o, v_w_gate, v_w_up, v_w_down):
    assert x.shape == (1, S, D) and w_gate.shape == (2, D, F_SHARD) and w_k.shape == (D_SHARD, D)
    xin, target = x[0], loss_target[0]

    ffn_shards = [[w_gate[layer].T.astype(BF16), w_up[layer].T.astype(BF16), w_down[layer].astype(BF16)]
                  for layer in range(2)]
    att_shards = [w_k.astype(BF16), w_v.astype(BF16), w_q[0].astype(BF16), w_o[0].astype(BF16)]
    pool_shard = pool_w[0].astype(BF16).reshape(N_GROUPS * POOL_SHARD, GROUP)
    small = jnp.concatenate([pool_b[0].reshape(1, N_GROUPS * POOL_SHARD), pool_scale], axis=1)

    full0 = _run_exchange(_relay_gather_exchange(ffn_shards[0] + [pool_shard, _pad16(small)]), "gather_layer0")
    ffn_w0 = [a.reshape(F, D) for a in full0[:3]]
    pw_f = full0[3].reshape(N_DEV, N_GROUPS, POOL_SHARD, GROUP).transpose(1, 0, 2, 3).reshape(N_GROUPS, GROUP, GROUP)
    small_f = full0[4][:, 0, :]
    pb_f = small_f[:, :N_GROUPS * POOL_SHARD].reshape(N_DEV, N_GROUPS, POOL_SHARD).transpose(1, 0, 2).reshape(1, D)
    ps_f = small_f[:, N_GROUPS * POOL_SHARD:].reshape(1, D)

    g_mix0, g_mix1 = norm_mix_g[0:1], norm_mix_g[1:2]
    g_ffn0, g_ffn1 = norm_ffn_g[0:1], norm_ffn_g[1:2]
    g_kv = kv_norm_g.reshape(1, D)
    gk_t = jnp.tile(k_norm_g.reshape(1, HEAD), (1, N_HEADS))
    gq_t = jnp.tile(q_norm_g.reshape(1, HEAD), (1, N_HEADS))

    x1, diff = _pool_fwd(xin, g_mix0, pw_f, pb_f, ps_f)
    (x2, hf0, gg0, uu0), full_att = _ffn_fwd(x1, g_ffn0, *ffn_w0, name="ffn_fwd", ex=_relay_gather_exchange(att_shards))
    wk_f, wv_f, wq_f, wo_f = [a.reshape(D, D) for a in full_att]
    (hkv, h1, kpre, qpre, kp, vp, qq), full1_gate = _kvq_proj(x2, g_kv, g_mix1, wk_f, wv_f, wq_f, gk_t, gq_t,
                                                                ex=_relay_gather_exchange(ffn_shards[1][:1]))
    bias = _bias_band(_toeplitz_from_table(rel_bias[0]))
    (att,), full1_rest = _attn_fwd(qq, kp, vp, bias, ex=_relay_gather_exchange(ffn_shards[1][1:]))
    ffn_w1 = [a.reshape(F, D) for a in full1_gate + full1_rest]
    x3 = _mm(att, wo_f, "nn", F32, "attn_out", add=x2)
    (dx4, loss_rows, hf1, gg1, uu1), _ = _ffn_fwd(x3, g_ffn1, *ffn_w1, name="ffn_fwd_loss", target=target)

    def blocks(dw):
        return dw.reshape(N_DEV, dw.shape[0] // N_DEV, dw.shape[1])

    (dgg1, duu1, dwg1, dwu1, dwd1), _ = _ffn_bwd_weights(dx4, hf1, gg1, uu1, ffn_w1[2], name="ffn_bwd1")
    parts1 = [blocks(dw) for dw in (dwg1, dwu1, dwd1)]
    (dx3, dg_ffn1), stage1 = _ffn_bwd_input(dx4, x3, g_ffn1, [dgg1, duu1], ffn_w1[:2], "ffn_dx1",
                                            ex=_pair_exchange(parts1))
    both1 = _pair_add(parts1, stage1, "pair_add_ffn1", copies=2)
    flight1, chip1, land1, _ = _chip_exchange_start(both1[:3], both1[3:], dx3, "scatter_ffn1_start")
    datt = _mm(dx3, wo_f, "nt", BF16, "d_attn")
    dwo = _mm(att, dx3, "tn", BF16, "d_wo")
    (dq, dkp, dvp, dband), _ = _attn_bwd(qq, kp, vp, bias, datt)
    (dx2, dqpre, dkpre, dg_mix1, dg_kv, dgq, dgk), _ = _qkv_input_bwd(
        dq, qpre, gq_t, dkp, kpre, gk_t, dvp, wq_f, wk_f, wv_f, dx3, x2, g_mix1, g_kv)
    dwq = _mm(h1, dqpre, "tn", BF16, "d_wq")
    dwk = _mm(hkv, dkpre, "tn", BF16, "d_wk")
    dwv = _mm(hkv, dvp, "tn", BF16, "d_wv", skip_rows=PADK)
    parts_att = [blocks(dw) for dw in (dwk, dwv, dwq, dwo)]
    (dgg0, duu0, *dw0), stage_att = _ffn_bwd_weights(
        dx2, hf0, gg0, uu0, ffn_w0[2], name="ffn_bwd0", ex=_pair_exchange(parts_att))
    both_att = _pair_add(parts_att, stage_att, "pair_add_att", copies=2)
    flight_att, chip_att, land_att, _ = _chip_exchange_start(both_att[:4], both_att[4:], dx2, "scatter_att_start")
    parts0 = [blocks(dw) for dw in dw0]
    (dx1, dg_ffn0), stage0 = _ffn_bwd_input(dx2, x1, g_ffn0, [dgg0, duu0], ffn_w0[:2], "ffn_dx0",
                                            ex=_pair_exchange(parts0))
    both = _pair_add(parts0, stage0, "pair_add_ffn0", copies=2)
    flight, chip0, land0, token = _chip_exchange_start(both[:3], both[3:], dx1, "scatter_ffn0_start")
    (grad_x, dpw, db_rows, ds_rows, dg_mix0), _ = _pool_bwd(dx1, xin, diff, g_mix0 + token[0:1, 0:1], pw_f, pb_f, ps_f)
    dtp, seg = _bias_grad(dband, dpw)

    weights = dict(norm_mix_g=norm_mix_g, norm_ffn_g=norm_ffn_g, pool_w=pool_w, pool_b=pool_b,
                   pool_scale=pool_scale, kv_norm_g=kv_norm_g, w_k=w_k, w_v=w_v, k_norm_g=k_norm_g, w_q=w_q,
                   q_norm_g=q_norm_g, rel_bias=rel_bias, w_o=w_o, w_gate=w_gate, w_up=w_up, w_down=w_down)
    mom1 = dict(norm_mix_g=m_norm_mix_g, norm_ffn_g=m_norm_ffn_g, pool_w=m_pool_w, pool_b=m_pool_b,
                pool_scale=m_pool_scale, kv_norm_g=m_kv_norm_g, w_k=m_w_k, w_v=m_w_v, k_norm_g=m_k_norm_g,
                w_q=m_w_q, q_norm_g=m_q_norm_g, rel_bias=m_rel_bias, w_o=m_w_o, w_gate=m_w_gate, w_up=m_w_up,
                w_down=m_w_down)
    mom2 = dict(norm_mix_g=v_norm_mix_g, norm_ffn_g=v_norm_ffn_g, pool_w=v_pool_w, pool_b=v_pool_b,
                pool_scale=v_pool_scale, kv_norm_g=v_kv_norm_g, w_k=v_w_k, w_v=v_w_v, k_norm_g=v_k_norm_g,
                w_q=v_w_q, q_norm_g=v_q_norm_g, rel_bias=v_rel_bias, w_o=v_w_o, w_gate=v_w_gate, w_up=v_w_up,
                w_down=v_w_down)
    names = list(weights)
    grads, deltas, new_m, new_v = {}, {}, {}, {}

    def adam_flat(nm, parts):
        shape = weights[nm].shape
        flat = lambda a: a.reshape(-1, shape[-1])
        done = _adam(flat(weights[nm]), parts, flat(mom1[nm]), flat(mom2[nm]), "adam_" + nm)
        grads[nm], deltas[nm], new_m[nm], new_v[nm] = [a.reshape(shape) for a in done]

    ffn_names = ("w_gate", "w_up", "w_down")
    def ffn_view(nm, a):
        return a if nm == "w_down" else a.transpose(0, 2, 1)

    recv1 = _chip_exchange_wait(flight1, chip1, land1, dtp, "scatter_ffn1_wait")
    layer1 = {nm: _adam_layer(ffn_view(nm, weights[nm]), g, ffn_view(nm, mom1[nm]), ffn_view(nm, mom2[nm]), 1,
                              "adam1_" + nm)
              for nm, g in zip(ffn_names, recv1)}
    recv_att = _chip_exchange_wait(flight_att, chip_att, land_att, layer1["w_down"][1], "scatter_att_wait")
    for nm, parts in zip(("w_k", "w_v", "w_q", "w_o"), recv_att):
        adam_flat(nm, parts)

    dpw_blocks = dpw.reshape(N_GROUPS, N_DEV, POOL_SHARD, GROUP).transpose(1, 0, 2, 3)
    dpw_blocks = dpw_blocks.reshape(N_DEV, N_GROUPS * POOL_SHARD, GROUP).astype(BF16)
    misc = jnp.concatenate([dgk[0:1, 0:HEAD], dgq[0:1, 0:HEAD], loss_rows[0:1, 0:1],
                            seg[:, 0, 0].reshape(1, N_HEADS), seg[:, 0, 1].reshape(1, N_HEADS)], axis=1)
    misc = jnp.pad(misc, ((0, 0), (0, D - misc.shape[1])))
    vec_rows = jnp.concatenate([dg_mix0[0:1], dg_mix1[0:1], dg_ffn0[0:1], dg_ffn1[0:1], dg_kv[0:1],
                                db_rows[0:1], ds_rows[0:1], misc], axis=0)
    pack = jnp.concatenate([vec_rows, dtp.reshape(N_HEADS, TOEP)], axis=0)
    tot, recv_pool = _small_exchanges(pack, dpw_blocks, "small_exchanges", after=deltas["w_o"])

    loss = tot[7, 2 * HEAD]
    seg_tot = jnp.stack([tot[7, 2 * HEAD + 1:2 * HEAD + 1 + N_HEADS],
                         tot[7, 2 * HEAD + 1 + N_HEADS:2 * HEAD + 1 + 2 * N_HEADS]], axis=1)
    me = 4 * lax.axis_index("x") + 2 * lax.axis_index("y") + lax.axis_index("c")
    g_pool_b = lax.dynamic_slice_in_dim(tot[5].reshape(N_GROUPS, GROUP), me * POOL_SHARD, POOL_SHARD, axis=1)
    grads.update(
        norm_mix_g=tot[0:2], norm_ffn_g=tot[2:4], kv_norm_g=tot[4], k_norm_g=tot[7, 0:HEAD],
        q_norm_g=tot[7, HEAD:2 * HEAD].reshape(1, HEAD),
        rel_bias=_table_grad_from_toeplitz(tot[8:8 + N_HEADS], seg_tot).reshape(1, N_HEADS, N_REL),
        pool_b=g_pool_b.reshape(1, N_GROUPS, POOL_SHARD),
        pool_scale=lax.dynamic_slice_in_dim(tot[6:7], me * D_SHARD, D_SHARD, axis=1))
    adam_flat("pool_w", recv_pool)
    small_names = [nm for nm in names if nm not in ffn_names + ("w_k", "w_v", "w_q", "w_o", "pool_w")]

    def pack_small(tree):
        cols = []
        for nm in small_names:
            flat = tree[nm].reshape(-1)
            cols.append(jnp.pad(flat, (0, -flat.shape[0] % 1024)))
        return jnp.concatenate(cols).reshape(-1, 128)

    _, dl, m1, m2 = _adam(pack_small(weights), pack_small(grads), pack_small(mom1), pack_small(mom2), "adam_small")

    def unpack_small(packed, out):
        flat, off = packed.reshape(-1), 0
        for nm in small_names:
            size = weights[nm].size
            out[nm] = flat[off:off + size].reshape(weights[nm].shape)
            off += size + (-size % 1024)

    unpack_small(dl, deltas)
    unpack_small(m1, new_m)
    unpack_small(m2, new_v)

    recv0 = _chip_exchange_wait(flight, chip0, land0, dl, "scatter_ffn0_wait")
    for nm, g in zip(ffn_names, recv0):
        done = _adam_layer(ffn_view(nm, weights[nm]), g, ffn_view(nm, mom1[nm]), ffn_view(nm, mom2[nm]), 0,
                           "adam0_" + nm, prev=layer1[nm])
        grads[nm], deltas[nm], new_m[nm], new_v[nm] = [ffn_view(nm, a) for a in done]

    return (loss, grad_x[None], *[grads[nm] for nm in names], *[deltas[nm] for nm in names],
            *[new_m[nm] for nm in names], *[new_v[nm] for nm in names])
```

```python
import functools

import jax
import jax.numpy as jnp
from jax import lax
from jax.experimental import pallas as pl
from jax.experimental.pallas import tpu as pltpu

F32 = jnp.float32
BF16 = jnp.bfloat16
MESH_ID = pl.DeviceIdType.MESH

N_DEV = 8
S = 2048
D = 1024
F = 2816
F_SHARD = F // N_DEV
D_SHARD = D // N_DEV
N_GROUPS = 4
GROUP = D // N_GROUPS
POOL_SHARD = GROUP // N_DEV
MAX_WIN = 16
HEAD = 64
N_HEADS = D // HEAD
CHUNK = 64
LEFT = 8
QB = 4 * CHUNK
KB = QB + LEFT * CHUNK
PADK = LEFT * CHUNK
TOEP = 1024
N_REL = 257
MAX_REL = 128
EPS = 1e-6
NEG_INF = -1e30
ATTN_SCALE = HEAD ** -0.5

ADAM_LR = 0.001
ADAM_B1 = 0.9
ADAM_B2 = 0.999
ADAM_EPS = 1e-08
ADAM_WD = 0.01
ADAM_STEP = 10

VMEM_LIMIT = 52 * 1024 * 1024

ANY = pl.BlockSpec(memory_space=pl.ANY)
VMEM = pl.BlockSpec(memory_space=pltpu.VMEM)


def _call(body, **kw):
    return pl.pallas_call(body, **kw)


def _params(*sem):
    return pltpu.CompilerParams(dimension_semantics=sem, vmem_limit_bytes=VMEM_LIMIT)


def _dot(a, b, dims):
    return lax.dot_general(a, b, (dims, ((), ())), preferred_element_type=F32)


def _nn(a, b):
    return _dot(a, b, ((1,), (0,)))


def _nt(a, b):
    return _dot(a, b, ((1,), (1,)))


def _tn(a, b):
    return _dot(a, b, ((0,), (0,)))


def _rstd(x):
    return lax.rsqrt(jnp.mean(x * x, axis=-1, keepdims=True) + EPS)


def _rms_bwd(dh, x, r, g):
    gd = dh * g
    return r * gd - x * (r * r * r) * jnp.mean(gd * x, axis=-1, keepdims=True)


def _colsum8(v):
    return jnp.broadcast_to(jnp.sum(v, axis=0, keepdims=True), (8, v.shape[1]))


def _seg_sum(v):
    r = lax.broadcasted_iota(jnp.int32, (128, 128), 0) // HEAD
    c = lax.broadcasted_iota(jnp.int32, (128, 128), 1) // HEAD
    ones = jnp.where(r == c, 1.0, 0.0).astype(BF16)
    out = []
    for blk in range(v.shape[1] // 128):
        part = v[:, blk * 128:(blk + 1) * 128]
        hi = part.astype(BF16)
        rest = part - hi.astype(F32)
        mid = rest.astype(BF16)
        lo = (rest - mid.astype(F32)).astype(BF16)
        out.append(_nn(hi, ones) + _nn(mid, ones) + _nn(lo, ones))
    return jnp.concatenate(out, axis=1)


def _place():
    return lax.axis_index("x"), lax.axis_index("y"), lax.axis_index("c")


class _Exchange:
    def __init__(self, ins, out_shape, sems, start, mid, finish, early=None):
        self.ins, self.out_shape, self.sems = list(ins), list(out_shape), list(sems)
        self.start, self.mid, self.finish = start, mid, finish
        self.early = early if early is not None else (lambda ins, outs, sems: None)


def _relay_gather_exchange(shards):
    n = len(shards)

    def tools(ins, outs, sems):
        send_sems, recv_sems, local_sems = sems
        x, y, c = _place()
        me, sib = (x, y, c), (x, y, 1 - c)
        xn, yn, dg = (1 - x, y, c), (x, 1 - y, c), (1 - x, 1 - y, c)
        other = lambda dev: (dev[0], dev[1], 1 - c)

        def piece(k, ref, half):
            rows = shards[k].shape[0] // 2
            return ref if half is None else ref.at[pl.ds(half * rows, rows)]

        def held(k, dev, half):
            return piece(k, outs[k].at[4 * dev[0] + 2 * dev[1] + dev[2]], half)

        def copy(k, s, dev, half, to, own=False):
            return pltpu.make_async_remote_copy(
                src_ref=piece(k, ins[k], half) if own else held(k, dev, half), dst_ref=held(k, dev, half),
                send_sem=send_sems.at[k, s], recv_sem=recv_sems.at[k, s], device_id=to, device_id_type=MESH_ID)

        def mine(k):
            return pltpu.make_async_copy(ins[k], held(k, me, None), local_sems.at[k])

        def sent(k):
            return [copy(k, 0, me, None, sib, own=True), copy(k, 1, me, 0, xn, own=True), copy(k, 2, me, 1, yn, own=True),
                    copy(k, 3, me, 0, yn, own=True), copy(k, 4, me, 1, xn, own=True)]

        relays = [(1, xn, 0, [(5, yn), (7, sib)]), (2, yn, 1, [(6, xn), (9, sib)]), (3, yn, 0, [(10, sib)]),
                  (4, xn, 1, [(8, sib)]), (6, dg, 1, [(11, sib)]), (5, dg, 0, [(12, sib)])]
        from_sibling = [(0, other(me), None), (7, other(xn), 0), (8, other(xn), 1), (9, other(yn), 1),
                        (10, other(yn), 0), (11, other(dg), 1), (12, other(dg), 0)]
        return mine, sent, copy, relays, from_sibling, me

    def start(ins, outs, sems):
        mine, sent, _, _, _, _ = tools(ins, outs, sems)
        for k in range(n):
            mine(k).start()
        for k in range(n):
            for cp in sent(k):
                cp.start()

    def pass_on(ins, outs, sems, which):
        _, _, copy, relays, _, me = tools(ins, outs, sems)
        for k in range(n):
            for s, dev, half, onward in relays[which]:
                copy(k, s, dev, half, me).wait_recv()
                for s2, to in onward:
                    copy(k, s2, dev, half, to).start()

    def early(ins, outs, sems):
        pass_on(ins, outs, sems, slice(0, 2))

    def mid(ins, outs, sems):
        pass_on(ins, outs, sems, slice(2, None))

    def finish(ins, outs, sems):
        mine, sent, copy, relays, from_sibling, me = tools(ins, outs, sems)
        for k in range(n):
            for s, dev, half in from_sibling:
                copy(k, s, dev, half, me).wait_recv()
        for k in range(n):
            for cp in sent(k):
                cp.wait_send()
            for _, dev, half, onward in relays:
                for s2, to in onward:
                    copy(k, s2, dev, half, to).wait_send()
            mine(k).wait()

    return _Exchange(
        shards, [jax.ShapeDtypeStruct((N_DEV,) + a.shape, a.dtype) for a in shards],
        [pltpu.SemaphoreType.DMA((n, 13)), pltpu.SemaphoreType.DMA((n, 13)), pltpu.SemaphoreType.DMA((n,))],
        start, mid, finish, early)


def _peer(x, y, c, m):
    px = 1 - x if m & 4 else x
    py = 1 - y if m & 2 else y
    pc = 1 - c if m & 1 else c
    return px, py, pc


N_CHIPS = N_DEV // 2


def _pair_exchange(parts):
    n = len(parts)

    def copies(ins, outs, sems):
        send_sems, recv_sems = sems
        x, y, c = _place()
        return [pltpu.make_async_remote_copy(
            src_ref=ins[k].at[2 * q + 1 - c], dst_ref=outs[k].at[q], send_sem=send_sems.at[k, q],
            recv_sem=recv_sems.at[k, q], device_id=(x, y, 1 - c), device_id_type=MESH_ID)
            for k in range(n) for q in range(N_CHIPS)]

    def start(ins, outs, sems):
        for cp in copies(ins, outs, sems):
            cp.start()

    def mid(ins, outs, sems):
        pass

    def finish(ins, outs, sems):
        for cp in copies(ins, outs, sems):
            cp.wait_recv()
        for cp in copies(ins, outs, sems):
            cp.wait_send()

    return _Exchange(
        parts, [jax.ShapeDtypeStruct((N_CHIPS,) + a.shape[1:], a.dtype) for a in parts],
        [pltpu.SemaphoreType.DMA((n, N_CHIPS)), pltpu.SemaphoreType.DMA((n, N_CHIPS))], start, mid, finish)


def _pair_add(parts, stage, name, copies=1):
    n = len(parts)
    core = lax.axis_index("c").reshape(1)

    def body(core_ref, *refs):
        for k in range(n):
            mine, theirs = refs[k], refs[n + k]
            total = (mine[0, 0].astype(F32) + theirs[0].astype(F32)).astype(BF16)
            for rep in range(copies):
                refs[(2 + rep) * n + k][0] = total

    in_specs, out_specs = [], []
    for a in parts:
        _, r, cdim = a.shape
        in_specs.append(pl.BlockSpec((1, 1, r // 2, cdim), lambda q, i, core_ref: (q, core_ref[0], i, 0)))
    for a in parts:
        _, r, cdim = a.shape
        in_specs.append(pl.BlockSpec((1, r // 2, cdim), lambda q, i, core_ref: (q, i, 0)))
        out_specs.append(pl.BlockSpec((1, r // 2, cdim), lambda q, i, core_ref: (q, i, 0)))
    return list(_call(
        body, name=name,
        grid_spec=pltpu.PrefetchScalarGridSpec(num_scalar_prefetch=1, grid=(N_CHIPS, 2), in_specs=in_specs,
                                               out_specs=out_specs * copies),
        out_shape=[jax.ShapeDtypeStruct(s.shape, BF16) for s in stage] * copies,
        compiler_params=_params("arbitrary", "arbitrary"),
    )(core, *[a.reshape((N_CHIPS, 2) + a.shape[1:]) for a in parts], *stage))


HBM = pl.BlockSpec(memory_space=pltpu.HBM)
SEMAPHORES = pl.BlockSpec(memory_space=pltpu.SEMAPHORE)


def _chip_copies(srcs, lands, send_sems, recv_sems, mine_is_dst):
    x, y, c = _place()
    me = 2 * x + y
    out = []
    for m in range(1, N_CHIPS):
        px, py, _ = _peer(x, y, c, 2 * m)
        peer = 2 * px + py
        for k in range(len(srcs)):
            pair = k * (N_CHIPS - 1) + m - 1
            out.append(pltpu.make_async_remote_copy(
                src_ref=srcs[k].at[peer], dst_ref=lands[k].at[me if mine_is_dst else peer],
                send_sem=send_sems[pair], recv_sem=recv_sems[pair],
                device_id=(px, py, c), device_id_type=MESH_ID))
    return out


def _chip_exchange_start(chip_parts, lands, after, name):
    n = len(chip_parts)
    pairs = n * (N_CHIPS - 1)

    def body(*refs):
        srcs, zones = refs[:n], refs[n:2 * n]
        sems = refs[2 * n + 1:2 * n + 1 + 2 * pairs]
        token = refs[-1]
        for cp in _chip_copies(srcs, zones, sems[:pairs], sems[pairs:], True):
            cp.start()
        token[...] = jnp.zeros(token.shape, F32)

    thru = [pltpu.HBM(a.shape, a.dtype) for a in chip_parts + lands]
    hbm = [pltpu.with_memory_space_constraint(a, pltpu.HBM) for a in chip_parts + lands]
    res = _call(
        body, name=name,
        out_shape=[pltpu.SemaphoreType.DMA(())] * (2 * pairs) + thru + [jax.ShapeDtypeStruct((8, 128), F32)],
        in_specs=[HBM] * (2 * n) + [ANY], out_specs=[SEMAPHORES] * (2 * pairs) + [HBM] * (2 * n) + [VMEM],
        input_output_aliases={i: 2 * pairs + i for i in range(2 * n)},
        compiler_params=pltpu.CompilerParams(has_side_effects=pltpu.SideEffectType.DATAFLOW_SIDE_EFFECTING),
    )(*hbm, after)
    sems, rest = list(res[:2 * pairs]), res[2 * pairs:]
    return sems, list(rest[:n]), list(rest[n:2 * n]), rest[-1]


def _chip_exchange_wait(sems, chip_parts, lands, after, name):
    n = len(chip_parts)
    pairs = n * (N_CHIPS - 1)

    def body(*refs):
        srcs, zones = refs[:n], refs[n:2 * n]
        sem_refs = refs[2 * n:2 * n + 2 * pairs]
        for cp in _chip_copies(srcs, zones, sem_refs[:pairs], sem_refs[pairs:], False):
            cp.wait_send()
            cp.wait_recv()

    thru = [pltpu.HBM(a.shape, a.dtype) for a in chip_parts + lands]
    res = _call(
        body, name=name, out_shape=thru,
        in_specs=[HBM] * (2 * n) + [SEMAPHORES] * (2 * pairs) + [ANY], out_specs=[HBM] * (2 * n),
        input_output_aliases={i: i for i in range(2 * n)},
        compiler_params=pltpu.CompilerParams(has_side_effects=pltpu.SideEffectType.DATAFLOW_SIDE_EFFECTING),
    )(*chip_parts, *lands, *sems, after)
    return list(res[n:])


def _run_exchange(ex, name, after=None):
    n_in, n_out = len(ex.ins), len(ex.out_shape)
    order = [] if after is None else [after]

    def body(*refs):
        ins, outs = refs[:n_in], refs[n_in + len(order):n_in + len(order) + n_out]
        sems = refs[n_in + len(order) + n_out:]
        ex.start(ins, outs, sems)
        ex.early(ins, outs, sems)
        ex.mid(ins, outs, sems)
        ex.finish(ins, outs, sems)

    return list(_call(body, name=name, out_shape=ex.out_shape, in_specs=[ANY] * (n_in + len(order)),
                      out_specs=[ANY] * n_out, scratch_shapes=ex.sems)(*ex.ins, *order))


def _call_hosting(body, ex, phases, *, in_specs, out_specs, out_shape, scratch_shapes, args, **kw):
    n_in, n_out, n_scr = len(in_specs), len(out_specs), len(scratch_shapes)
    if ex is None:
        res = _call(body, in_specs=in_specs, out_specs=out_specs, out_shape=out_shape,
                    scratch_shapes=scratch_shapes, **kw)(*args)
        return list(res), []
    n_xin, n_xout = len(ex.ins), len(ex.out_shape)

    def hosting(*refs):
        a, b = n_in, n_in + n_xin
        c, d = b + n_out, b + n_out + n_xout
        ins, xins, outs, xouts = refs[:a], refs[a:b], refs[b:c], refs[c:d]
        scr, sems = refs[d:d + n_scr], refs[d + n_scr:]
        first, early, mid, last = phases()

        @pl.when(first)
        def _():
            ex.start(xins, xouts, sems)

        body(*ins, *outs, *scr)

        @pl.when(early)
        def _():
            ex.early(xins, xouts, sems)

        @pl.when(mid)
        def _():
            ex.mid(xins, xouts, sems)

        @pl.when(last)
        def _():
            ex.finish(xins, xouts, sems)

    res = _call(hosting, in_specs=list(in_specs) + [ANY] * n_xin, out_specs=list(out_specs) + [ANY] * n_xout,
                out_shape=list(out_shape) + ex.out_shape, scratch_shapes=list(scratch_shapes) + ex.sems,
                **kw)(*args, *ex.ins)
    return list(res[:n_out]), list(res[n_out:])


def _grid_phases(dims, mid_fraction=0.8, early_fraction=0.4):
    total = 1
    for d in dims:
        total *= d
    mid = min(max(int(total * mid_fraction), 1), total - 1)
    early = min(int(total * early_fraction), mid)

    def phases():
        step = pl.program_id(0)
        for axis in range(1, len(dims)):
            step = step * dims[axis] + pl.program_id(axis)
        return step == 0, step == early, step == mid, step == total - 1
    return phases


def _small_exchanges(pack, blocks, name, after):
    rows = pack.shape[0]

    def body(in_ref, blk_ref, _, out_ref, got_ref, recv, send_sems, recv_sems, blk_send, blk_recv, blk_local):
        x, y, c = _place()
        me = 4 * x + 2 * y + c
        peers = [_peer(x, y, c, m) for m in range(1, N_DEV)]
        index = [4 * px + 2 * py + pc for px, py, pc in peers]

        def block_copy(m, landing):
            return pltpu.make_async_remote_copy(
                src_ref=blk_ref.at[index[m]], dst_ref=got_ref.at[index[m] if landing else me],
                send_sem=blk_send.at[m], recv_sem=blk_recv.at[m], device_id=peers[m], device_id_type=MESH_ID)

        def pack_copy(m, landing):
            return pltpu.make_async_remote_copy(
                src_ref=in_ref, dst_ref=recv.at[index[m] if landing else me], send_sem=send_sems.at[m],
                recv_sem=recv_sems.at[m], device_id=peers[m], device_id_type=MESH_ID)

        own = pltpu.make_async_copy(blk_ref.at[me], got_ref.at[me], blk_local)
        own.start()
        recv[me] = in_ref[...]
        for m in range(N_DEV - 1):
            pack_copy(m, False).start()
            block_copy(m, False).start()
        for m in range(N_DEV - 1):
            pack_copy(m, True).wait_recv()
        acc = recv[0]
        for d in range(1, N_DEV):
            acc = acc + recv[d]
        out_ref[...] = acc
        for m in range(N_DEV - 1):
            block_copy(m, True).wait_recv()
        for m in range(N_DEV - 1):
            pack_copy(m, False).wait_send()
            block_copy(m, False).wait_send()
        own.wait()

    return _call(
        body, name=name,
        out_shape=[jax.ShapeDtypeStruct(pack.shape, F32), jax.ShapeDtypeStruct(blocks.shape, blocks.dtype)],
        in_specs=[VMEM, ANY, ANY], out_specs=[VMEM, ANY],
        scratch_shapes=[pltpu.VMEM((N_DEV, rows, pack.shape[1]), F32)] + [pltpu.SemaphoreType.DMA((7,))] * 4
        + [pltpu.SemaphoreType.DMA],
        compiler_params=pltpu.CompilerParams(vmem_limit_bytes=VMEM_LIMIT),
    )(pack, blocks, after)


POOL_TS = 256


def _pool_counts(first_row, rows, win):
    t = first_row + lax.broadcasted_iota(jnp.int32, (rows, 1), 0)
    return jnp.minimum(t + 1, win).astype(F32)


def _pool_fwd(x, g, w, b, scale):
    nt = S // POOL_TS

    def body(x_ref, g_ref, w_ref, b_ref, s_ref, out_ref, diff_ref, ext):
        i = pl.program_id(0)

        @pl.when(i == 0)
        def _():
            ext[0:MAX_WIN, :] = jnp.zeros((MAX_WIN, D), F32)

        @pl.when(i > 0)
        def _():
            ext[0:MAX_WIN, :] = ext[POOL_TS:POOL_TS + MAX_WIN, :]

        xv = x_ref[...]
        h = xv * _rstd(xv) * g_ref[...]
        ext[MAX_WIN:, :] = h
        for gi in range(N_GROUPS):
            win = 2 << gi
            cols = slice(gi * GROUP, (gi + 1) * GROUP)
            sm = ext[:, cols]
            k = 1
            while k < win:
                sm = sm + pltpu.roll(sm, k, axis=0)
                k *= 2
            pooled = sm[MAX_WIN:, :] / _pool_counts(i * POOL_TS, POOL_TS, win)
            diff = (pooled - h[:, cols]).astype(BF16)
            yv = (_nn(diff, w_ref[gi]) + b_ref[:, cols]) * s_ref[:, cols]
            out_ref[:, cols] = xv[:, cols] + yv
            diff_ref[:, cols] = diff

    row = pl.BlockSpec((1, D), lambda i: (0, 0))
    tile = pl.BlockSpec((POOL_TS, D), lambda i: (i, 0))
    return _call(
        body, name="pool_fwd", grid=(nt,),
        in_specs=[tile, row, pl.BlockSpec((N_GROUPS, GROUP, GROUP), lambda i: (0, 0, 0)), row, row],
        out_specs=[tile, tile],
        out_shape=[jax.ShapeDtypeStruct((S, D), F32), jax.ShapeDtypeStruct((S, D), BF16)],
        scratch_shapes=[pltpu.VMEM((POOL_TS + MAX_WIN, D), F32)],
        compiler_params=_params("arbitrary"),
    )(x, g, w, b, scale)


def _pool_bwd(dy, x, diff, g, w, b, scale, ex=None):
    nt = S // POOL_TS

    def body(dy_ref, x_ref, diff_ref, g_ref, w_ref, b_ref, s_ref, gx_ref, dw_ref, db_ref, ds_ref, dg_ref, ext, dh):
        i = pl.program_id(0)
        first_row = (nt - 1 - i) * POOL_TS

        @pl.when(i == 0)
        def _():
            ext[POOL_TS:, :] = jnp.zeros((MAX_WIN, D), F32)
            dw_ref[...] = jnp.zeros(dw_ref.shape, F32)
            db_ref[...] = jnp.zeros(db_ref.shape, F32)
            ds_ref[...] = jnp.zeros(ds_ref.shape, F32)
            dg_ref[...] = jnp.zeros(dg_ref.shape, F32)

        @pl.when(i > 0)
        def _():
            ext[POOL_TS:, :] = ext[0:MAX_WIN, :]

        dyv = dy_ref[...]
        for gi in range(N_GROUPS):
            win = 2 << gi
            cols = slice(gi * GROUP, (gi + 1) * GROUP)
            dfb = diff_ref[:, cols]
            z = _nn(dfb, w_ref[gi]) + b_ref[:, cols]
            dyg = dyv[:, cols]
            ds_ref[:, cols] += _colsum8(dyg * z)
            dz = dyg * s_ref[:, cols]
            db_ref[:, cols] += _colsum8(dz)
            dzb = dz.astype(BF16)
            dw_ref[gi] += _tn(dfb, dzb)
            ddiff = _nt(dzb, w_ref[gi])
            ext[0:POOL_TS, cols] = ddiff / _pool_counts(first_row, POOL_TS, win)
            sm = ext[:, cols]
            k = 1
            while k < win:
                sm = sm + pltpu.roll(sm, POOL_TS + MAX_WIN - k, axis=0)
                k *= 2
            dh[:, cols] = sm[0:POOL_TS, :] - ddiff
        xv = x_ref[...]
        r = _rstd(xv)
        gv = g_ref[...]
        dhv = dh[...]
        dg_ref[...] += _colsum8(dhv * xv * r)
        gx_ref[...] = dyv + _rms_bwd(dhv, xv, r, gv)

    row = pl.BlockSpec((1, D), lambda i: (0, 0))
    tile = pl.BlockSpec((POOL_TS, D), lambda i: (nt - 1 - i, 0))
    acc = pl.BlockSpec((8, D), lambda i: (0, 0))
    wspec = pl.BlockSpec((N_GROUPS, GROUP, GROUP), lambda i: (0, 0, 0))
    return _call_hosting(
        body, ex, _grid_phases((nt,)), name="pool_bwd", grid=(nt,),
        in_specs=[tile, tile, tile, row, wspec, row, row],
        out_specs=[tile, wspec, acc, acc, acc],
        out_shape=[jax.ShapeDtypeStruct((S, D), F32), jax.ShapeDtypeStruct((N_GROUPS, GROUP, GROUP), F32),
                   jax.ShapeDtypeStruct((8, D), F32), jax.ShapeDtypeStruct((8, D), F32),
                   jax.ShapeDtypeStruct((8, D), F32)],
        scratch_shapes=[pltpu.VMEM((POOL_TS + MAX_WIN, D), F32), pltpu.VMEM((POOL_TS, D), F32)],
        args=(dy, x, diff, g, w, b, scale), compiler_params=_params("arbitrary"),
    )


FFN_TS = min(S, 1024)
FFN_TF = 256


def _ffn_fwd(x, g, wg_t, wu_t, wd, name, target=None, ex=None):
    width = wd.shape[0]
    ni, nj = S // FFN_TS, width // FFN_TF
    with_loss = target is not None
    extra = [target] if with_loss else []

    def body(*refs):
        x_ref, g_ref, wg_ref, wu_ref, wd_ref = refs[:5]
        if with_loss:
            t_ref, out_ref, loss_ref, h_ref, gg_ref, uu_ref, hs, acc = refs[5:]
        else:
            out_ref, h_ref, gg_ref, uu_ref, hs, acc = refs[5:]
        i, j = pl.program_id(0), pl.program_id(1)

        @pl.when(j == 0)
        def _():
            xv = x_ref[...]
            hb = (xv * _rstd(xv) * g_ref[...]).astype(BF16)
            hs[...] = hb
            h_ref[...] = hb
            acc[...] = jnp.zeros(acc.shape, F32)

        hb = hs[...]
        gg = _nt(hb, wg_ref[...])
        uu = _nt(hb, wu_ref[...])
        gg_ref[...] = gg
        uu_ref[...] = uu
        a = (gg * jax.nn.sigmoid(gg) * uu).astype(BF16)
        acc[...] += _nn(a, wd_ref[...])

        @pl.when(j == nj - 1)
        def _():
            yv = x_ref[...] + acc[...]
            if with_loss:
                err = yv - t_ref[...]
                out_ref[...] = err * (1.0 / D)
                part = jnp.sum(err * err) * (0.5 / D)

                @pl.when(i == 0)
                def _():
                    loss_ref[...] = jnp.zeros(loss_ref.shape, F32)

                loss_ref[...] += jnp.broadcast_to(part, loss_ref.shape)
            else:
                out_ref[...] = yv

    xt = pl.BlockSpec((FFN_TS, D), lambda i, j: (i, 0))
    row = pl.BlockSpec((1, D), lambda i, j: (0, 0))
    wt = pl.BlockSpec((FFN_TF, D), lambda i, j: (j, 0))
    gt = pl.BlockSpec((FFN_TS, FFN_TF), lambda i, j: (i, j))
    in_specs = [xt, row, wt, wt, wt] + [xt] * len(extra)
    out_specs = [xt] + ([pl.BlockSpec((8, 128), lambda i, j: (0, 0))] if with_loss else []) + [xt, gt, gt]
    out_shape = ([jax.ShapeDtypeStruct((S, D), F32)] + ([jax.ShapeDtypeStruct((8, 128), F32)] if with_loss else [])
                 + [jax.ShapeDtypeStruct((S, D), BF16), jax.ShapeDtypeStruct((S, width), F32),
                    jax.ShapeDtypeStruct((S, width), F32)])
    args = (x, g, wg_t, wu_t, wd, *extra)
    return _call_hosting(
        body, ex, _grid_phases((ni, nj), 0.8), name=name, grid=(ni, nj),
        in_specs=in_specs, out_specs=out_specs, out_shape=out_shape,
        scratch_shapes=[pltpu.VMEM((FFN_TS, D), BF16), pltpu.VMEM((FFN_TS, D), F32)], args=args,
        compiler_params=_params("arbitrary", "arbitrary"),
    )


def _ffn_bwd_weights(dout, h, gg, uu, wd, name, ex=None):
    width = wd.shape[0]
    nj = width // FFN_TF

    def body(do_ref, h_ref, gg_ref, uu_ref, wd_ref, dg_ref, du_ref, dwg_ref, dwu_ref, dwd_ref, dob, kept):
        j = pl.program_id(0)

        @pl.when(j == 0)
        def _():
            dob[...] = do_ref[...].astype(BF16)
            kept[1] = jnp.zeros(kept.shape[1:], BF16)

        prev, cur = (j + 1) % 2, j % 2
        hb, dov = h_ref[...], dob[...]
        dwg_ref[...] = _tn(kept[prev, 0], hb).astype(BF16)
        dwu_ref[...] = _tn(kept[prev, 1], hb).astype(BF16)
        dwd_ref[...] = _tn(kept[prev, 2], dov).astype(BF16)
        gv, uv = gg_ref[...], uu_ref[...]
        da = _nt(dov, wd_ref[...])
        sg = jax.nn.sigmoid(gv)
        sl = gv * sg
        dub = (da * sl).astype(BF16)
        dgb = (da * uv * (sg * (1.0 + gv * (1.0 - sg)))).astype(BF16)
        dg_ref[...] = dgb
        du_ref[...] = dub
        kept[cur, 0] = dgb
        kept[cur, 1] = dub
        kept[cur, 2] = (sl * uv).astype(BF16)

    once = pl.Buffered(1)
    whole = lambda: pl.BlockSpec((S, D), lambda j, _: (0, 0), pipeline_mode=once)
    this = lambda j: jnp.minimum(j, nj - 1)
    last = lambda j: jnp.maximum(j - 1, 0)
    wt_in = pl.BlockSpec((FFN_TF, D), lambda j, _: (this(j), 0))
    wt_out = pl.BlockSpec((FFN_TF, D), lambda j, _: (last(j), 0))
    gt = pl.BlockSpec((S, FFN_TF), lambda j, _: (0, this(j)))
    return _call_hosting(
        body, ex, _grid_phases((nj + 1, 1)), name=name, grid=(nj + 1, 1),
        in_specs=[whole(), whole(), gt, gt, wt_in], out_specs=[gt, gt, wt_out, wt_out, wt_out],
        out_shape=[jax.ShapeDtypeStruct((S, width), BF16)] * 2 + [jax.ShapeDtypeStruct((width, D), BF16)] * 3,
        scratch_shapes=[pltpu.VMEM((S, D), BF16), pltpu.VMEM((2, 3, S, FFN_TF), BF16)], args=(dout, h, gg, uu, wd),
        compiler_params=_params("arbitrary", "arbitrary"),
    )


BWD_TS = 512


def _ffn_bwd_input(dres, x, g, grads, weights, name, ex=None):
    nt = S // BWD_TS
    n = len(grads)

    def body(*refs):
        dres_ref, x_ref, g_ref = refs[:3]
        grad_refs, w_refs = refs[3:3 + n], refs[3 + n:3 + 2 * n]
        dx_ref, dgam_ref = refs[3 + 2 * n:]
        dh = _nn(grad_refs[0][...], w_refs[0][...])
        for k in range(1, n):
            dh = dh + _nn(grad_refs[k][...], w_refs[k][...])
        xv = x_ref[...]
        r = _rstd(xv)
        dx_ref[...] = dres_ref[...] + _rms_bwd(dh, xv, r, g_ref[...])

        @pl.when(pl.program_id(0) == 0)
        def _():
            dgam_ref[...] = jnp.zeros(dgam_ref.shape, F32)

        dgam_ref[...] += _colsum8(dh * xv * r)

    tile = pl.BlockSpec((BWD_TS, D), lambda i: (i, 0))
    ftiles = [pl.BlockSpec((BWD_TS, a.shape[1]), lambda i: (i, 0)) for a in grads]
    wspecs = [pl.BlockSpec(w.shape, lambda i: (0, 0), pipeline_mode=pl.Buffered(1)) for w in weights]
    return _call_hosting(
        body, ex, _grid_phases((nt,)), name=name, grid=(nt,),
        in_specs=[tile, tile, pl.BlockSpec((1, D), lambda i: (0, 0))] + ftiles + wspecs,
        out_specs=[tile, pl.BlockSpec((8, D), lambda i: (0, 0))],
        out_shape=[jax.ShapeDtypeStruct((S, D), F32), jax.ShapeDtypeStruct((8, D), F32)],
        scratch_shapes=[], args=(dres, x, g, *grads, *weights), compiler_params=_params("arbitrary"),
    )


def _mm(a, b, mode, out_dtype, name, add=None, skip_rows=0):
    if mode == "nn":
        (m, kd), n = (a.shape[0] - skip_rows, a.shape[1]), b.shape[1]
    elif mode == "nt":
        (m, kd), n = (a.shape[0] - skip_rows, a.shape[1]), b.shape[0]
    else:
        (kd, m), n = a.shape, b.shape[1]
    tm, tn, tk = min(m, 1024), min(n, 1024), min(kd, 1024)
    if skip_rows and mode == "tn":
        tk = min(tk, skip_rows)
    elif skip_rows:
        tm = min(tm, skip_rows)
    skip_a = skip_rows // tm if mode != "tn" else 0
    skip_b = skip_rows // tk if mode == "tn" else 0
    assert skip_rows == skip_a * tm + skip_b * tk
    nk = kd // tk
    dot = {"nn": _nn, "nt": _nt, "tn": _tn}[mode]

    def body(*refs):
        if add is None:
            a_ref, b_ref, o_ref, acc = refs
        else:
            a_ref, b_ref, add_ref, o_ref, acc = refs
        k = pl.program_id(2)

        @pl.when(k == 0)
        def _():
            acc[...] = jnp.zeros(acc.shape, F32)

        acc[...] += dot(a_ref[...].astype(BF16), b_ref[...].astype(BF16))

        @pl.when(k == nk - 1)
        def _():
            res = acc[...]
            if add is not None:
                res = res + add_ref[...]
            o_ref[...] = res.astype(out_dtype)

    if mode == "tn":
        a_spec = pl.BlockSpec((tk, tm), lambda i, j, k: (k, i))
        b_spec = pl.BlockSpec((tk, tn), lambda i, j, k: (k + skip_b, j))
    else:
        a_spec = pl.BlockSpec((tm, tk), lambda i, j, k: (i + skip_a, k))
        b_spec = (pl.BlockSpec((tk, tn), lambda i, j, k: (k, j)) if mode == "nn"
                  else pl.BlockSpec((tn, tk), lambda i, j, k: (j, k)))
    o_spec = pl.BlockSpec((tm, tn), lambda i, j, k: (i, j))
    in_specs = [a_spec, b_spec] + ([o_spec] if add is not None else [])
    args = (a, b) + ((add,) if add is not None else ())
    return _call(
        body, name=name, grid=(m // tm, n // tn, nk), in_specs=in_specs, out_specs=o_spec,
        out_shape=jax.ShapeDtypeStruct((m, n), out_dtype), scratch_shapes=[pltpu.VMEM((tm, tn), F32)],
        compiler_params=_params("parallel", "parallel", "arbitrary"),
    )(*args)


PROJ_TS = 256


def _kvq_proj(x, g_kv, g_mix, wk, wv, wq, gk, gq, ex=None):
    lead = PADK // PROJ_TS

    def body(x_ref, gkv_ref, gmix_ref, wk_ref, wv_ref, wq_ref, gk_ref, gq_ref,
             hkv_ref, h1_ref, kpre_ref, qpre_ref, k_ref, v_ref, q_ref):
        i = pl.program_id(0)

        @pl.when(i < lead)
        def _():
            k_ref[...] = jnp.zeros(k_ref.shape, BF16)
            v_ref[...] = jnp.zeros(v_ref.shape, BF16)

        @pl.when(i >= lead)
        def _():
            xv = x_ref[...]
            xr = xv * _rstd(xv)
            hkv = (xr * gkv_ref[...]).astype(BF16)
            h1 = (xr * gmix_ref[...]).astype(BF16)
            hkv_ref[...] = hkv
            h1_ref[...] = h1
            kpre = _nn(hkv, wk_ref[...])
            qpre = _nn(h1, wq_ref[...])
            kpre_ref[...] = kpre
            qpre_ref[...] = qpre
            v_ref[...] = _nn(hkv, wv_ref[...]).astype(BF16)
            rk = lax.rsqrt(_seg_sum(kpre * kpre) * (1.0 / HEAD) + EPS)
            k_ref[...] = (kpre * rk * gk_ref[...]).astype(BF16)
            rq = lax.rsqrt(_seg_sum(qpre * qpre) * (1.0 / HEAD) + EPS)
            q_ref[...] = (qpre * rq * gq_ref[...]).astype(BF16)

    tile = pl.BlockSpec((PROJ_TS, D), lambda i: (jnp.maximum(i - lead, 0), 0))
    padded = pl.BlockSpec((PROJ_TS, D), lambda i: (i, 0))
    row = pl.BlockSpec((1, D), lambda i: (0, 0))
    wspec = pl.BlockSpec((D, D), lambda i: (0, 0))
    bf = jax.ShapeDtypeStruct((S, D), BF16)
    ff = jax.ShapeDtypeStruct((S, D), F32)
    bp = jax.ShapeDtypeStruct((PADK + S, D), BF16)
    return _call_hosting(
        body, ex, _grid_phases((lead + S // PROJ_TS,), 0.85), name="kvq_proj", grid=(lead + S // PROJ_TS,),
        in_specs=[tile, row, row, wspec, wspec, wspec, row, row],
        out_specs=[tile, tile, tile, tile, padded, padded, tile], out_shape=[bf, bf, ff, ff, bp, bp, bf],
        scratch_shapes=[], args=(x, g_kv, g_mix, wk, wv, wq, gk, gq), compiler_params=_params("arbitrary"),
    )


def _qkv_input_bwd(dq, qpre, gq, dkp, kpre, gk, dvp, wq, wk, wv, dres, x, g_mix, g_kv, ex=None):
    nt = S // PROJ_TS
    lead = PADK // PROJ_TS

    def head_bwd(dov, pv, hgv):
        r = lax.rsqrt(_seg_sum(pv * pv) * (1.0 / HEAD) + EPS)
        gd = dov * hgv
        dpre = r * gd - pv * (r * r * r) * (_seg_sum(gd * pv) * (1.0 / HEAD))
        return dpre.astype(BF16), _colsum8(dov * pv * r)

    def fold_heads(full):
        fold = full[:, 0:128]
        for blk in range(1, D // 128):
            fold = fold + full[:, blk * 128:(blk + 1) * 128]
        return fold + pltpu.roll(fold, HEAD, axis=1)

    def body(dq_ref, qpre_ref, gq_ref, dk_ref, kpre_ref, gk_ref, dv_ref, wq_ref, wk_ref, wv_ref, dres_ref, x_ref,
             gmix_ref, gkv_ref, dx_ref, dqpre_ref, dkpre_ref, dgmix_ref, dgkv_ref, dgq_ref, dgk_ref, accq, acck):
        i = pl.program_id(0)

        @pl.when(i == 0)
        def _():
            accq[...] = jnp.zeros(accq.shape, F32)
            acck[...] = jnp.zeros(acck.shape, F32)
            dgmix_ref[...] = jnp.zeros(dgmix_ref.shape, F32)
            dgkv_ref[...] = jnp.zeros(dgkv_ref.shape, F32)

        dqb, cq = head_bwd(dq_ref[...], qpre_ref[...], gq_ref[...])
        dkb, ck = head_bwd(dk_ref[...], kpre_ref[...], gk_ref[...])
        dqpre_ref[...] = dqb
        dkpre_ref[...] = dkb
        accq[...] += cq
        acck[...] += ck
        dh1 = _nt(dqb, wq_ref[...])
        dhkv = _nt(dkb, wk_ref[...]) + _nt(dv_ref[...].astype(BF16), wv_ref[...])
        xv = x_ref[...]
        r = _rstd(xv)
        dx_ref[...] = dres_ref[...] + _rms_bwd(dh1, xv, r, gmix_ref[...]) + _rms_bwd(dhkv, xv, r, gkv_ref[...])
        dgmix_ref[...] += _colsum8(dh1 * xv * r)
        dgkv_ref[...] += _colsum8(dhkv * xv * r)

        @pl.when(i == nt - 1)
        def _():
            dgq_ref[...] = fold_heads(accq[...])
            dgk_ref[...] = fold_heads(acck[...])

    tile = pl.BlockSpec((PROJ_TS, D), lambda i: (i, 0))
    behind = pl.BlockSpec((PROJ_TS, D), lambda i: (i + lead, 0))
    row = pl.BlockSpec((1, D), lambda i: (0, 0))
    wspec = lambda: pl.BlockSpec((D, D), lambda i: (0, 0), pipeline_mode=pl.Buffered(1))
    acc = pl.BlockSpec((8, D), lambda i: (0, 0))
    small = pl.BlockSpec((8, 128), lambda i: (0, 0))
    bf = jax.ShapeDtypeStruct((S, D), BF16)
    return _call_hosting(
        body, ex, _grid_phases((nt,)), name="qkv_input_bwd", grid=(nt,),
        in_specs=[tile, tile, row, behind, tile, row, behind, wspec(), wspec(), wspec(), tile, tile, row, row],
        out_specs=[tile, tile, tile, acc, acc, small, small],
        out_shape=[jax.ShapeDtypeStruct((S, D), F32), bf, bf, jax.ShapeDtypeStruct((8, D), F32),
                   jax.ShapeDtypeStruct((8, D), F32), jax.ShapeDtypeStruct((8, 128), F32),
                   jax.ShapeDtypeStruct((8, 128), F32)],
        scratch_shapes=[pltpu.VMEM((8, D), F32), pltpu.VMEM((8, D), F32)],
        args=(dq, qpre, gq, dkp, kpre, gk, dvp, wq, wk, wv, dres, x, g_mix, g_kv),
        compiler_params=_params("arbitrary"),
    )


def _toeplitz_from_table(table):
    far = jnp.broadcast_to(table[:, N_REL - 1:], (N_HEADS, PADK - MAX_REL + 1))
    near = table[:, N_REL - 2::-1]
    past = jnp.broadcast_to(table[:, 0:1], (N_HEADS, MAX_REL))
    wrap = jnp.broadcast_to(table[:, N_REL - 1:], (N_HEADS, TOEP - (PADK + 2 * MAX_REL + 1)))
    return jnp.concatenate([far, near, past, wrap], axis=1).reshape(N_HEADS, 1, TOEP)


def _table_grad_from_toeplitz(dtp, seg):
    lo = PADK - MAX_REL + 1
    near = dtp[:, lo + N_REL - 3:lo - 1:-1]
    return jnp.concatenate([seg[:, 1:2], near, seg[:, 0:1]], axis=1)


def _bias_band(tp):
    def body(tp_ref, out_ref):
        bv = pltpu.roll(jnp.broadcast_to(tp_ref[0], (QB, TOEP)), 0, axis=1, stride=1, stride_axis=0)
        out_ref[0] = jnp.where(_band_mask(), bv[:, 0:KB], NEG_INF)

    return _call(
        body, name="bias_band", grid=(N_HEADS,),
        in_specs=[pl.BlockSpec((1, 1, TOEP), lambda h: (h, 0, 0))],
        out_specs=pl.BlockSpec((1, QB, KB), lambda h: (h, 0, 0)),
        out_shape=jax.ShapeDtypeStruct((N_HEADS, QB, KB), F32),
        compiler_params=_params("parallel"),
    )(tp)


def _bias_grad(dband, after):
    lo, hi = PADK - MAX_REL + 1, PADK + MAX_REL

    def body(db_ref, _, dtp_ref, seg_ref):
        bv = jnp.concatenate([db_ref[0], jnp.zeros((QB, TOEP - KB), F32)], axis=1)
        row = lax.broadcasted_iota(jnp.int32, (QB, TOEP), 0)
        k = 1
        while k < QB:
            bv = jnp.where((row & k) != 0, pltpu.roll(bv, TOEP - k, axis=1), bv)
            k *= 2
        col = jnp.sum(bv, axis=0, keepdims=True)
        dtp_ref[0] = col
        u = lax.broadcasted_iota(jnp.int32, (1, TOEP), 1)
        far = jnp.sum(jnp.where((u < lo) | (u > hi + MAX_REL), col, 0.0))
        past = jnp.sum(jnp.where((u >= hi) & (u <= hi + MAX_REL), col, 0.0))
        lane = lax.broadcasted_iota(jnp.int32, (1, 128), 1)
        seg_ref[0] = jnp.where(lane == 0, far, jnp.where(lane == 1, past, 0.0))

    return _call(
        body, name="bias_grad", grid=(N_HEADS,),
        in_specs=[pl.BlockSpec((1, QB, KB), lambda h: (h, 0, 0)), ANY],
        out_specs=[pl.BlockSpec((1, 1, TOEP), lambda h: (h, 0, 0)), pl.BlockSpec((1, 1, 128), lambda h: (h, 0, 0))],
        out_shape=[jax.ShapeDtypeStruct((N_HEADS, 1, TOEP), F32), jax.ShapeDtypeStruct((N_HEADS, 1, 128), F32)],
        compiler_params=_params("parallel"),
    )(dband, after)


N_QB = S // QB
HEADS_PER_STEP = 4
ATT_LANES = HEADS_PER_STEP * HEAD
N_HG = D // ATT_LANES


def _band_mask():
    qc = lax.broadcasted_iota(jnp.int32, (QB, KB), 0) // CHUNK
    kc = lax.broadcasted_iota(jnp.int32, (QB, KB), 1) // CHUNK
    return (kc >= qc) & (kc <= qc + LEFT)


def _half_scale(hh, scale):
    lane = lax.broadcasted_iota(jnp.int32, (1, 128), 1)
    return jnp.where((lane < HEAD) == (hh == 0), scale, 0.0).astype(BF16)


def _probs(qh, kb, bias, first_key):
    sc = _nt(qh, kb) + bias
    if first_key is not None:
        sc = jnp.where(lax.broadcasted_iota(jnp.int32, (QB, KB), 1) >= first_key, sc, NEG_INF)
    e = jnp.exp(sc - jnp.max(sc, axis=-1, keepdims=True))
    return e * (1.0 / jnp.sum(e, axis=-1, keepdims=True))


def _by_padding(cb, compute):
    @pl.when(cb < PADK // QB)
    def _():
        compute(PADK - cb * QB)

    @pl.when(cb >= PADK // QB)
    def _():
        compute(None)


def _attn_fwd(q, kp, vp, bias, ex=None):
    def body(q_ref, k_ref, v_ref, b_ref, o_ref):
        cb = pl.program_id(1)
        band = pl.ds(pl.multiple_of(cb * QB, QB), KB)
        low = lax.broadcasted_iota(jnp.int32, (QB, 128), 1) < HEAD

        def compute(first_key):
            for pair in range(HEADS_PER_STEP // 2):
                lanes = pl.ds(pair * 128, 128)
                kb, vb, qv = k_ref[band, lanes], v_ref[band, lanes], q_ref[:, lanes]
                outs = []
                for hh in range(2):
                    pb = _probs(qv * _half_scale(hh, ATTN_SCALE), kb, b_ref[2 * pair + hh], first_key).astype(BF16)
                    outs.append(_nn(pb, vb))
                o_ref[:, lanes] = jnp.where(low, outs[0], outs[1]).astype(BF16)

        _by_padding(cb, compute)

    qspec = pl.BlockSpec((QB, ATT_LANES), lambda hg, cb: (cb, hg))
    kspec = pl.BlockSpec((PADK + S, ATT_LANES), lambda hg, cb: (0, hg))
    return _call_hosting(
        body, ex, _grid_phases((N_HG, N_QB), 0.85), name="attn_fwd", grid=(N_HG, N_QB),
        in_specs=[qspec, kspec, kspec, pl.BlockSpec((HEADS_PER_STEP, QB, KB), lambda hg, cb: (hg, 0, 0))],
        out_specs=[qspec], out_shape=[jax.ShapeDtypeStruct((S, D), BF16)], scratch_shapes=[],
        args=(q, kp, vp, bias), compiler_params=_params("arbitrary", "arbitrary"),
    )


def _attn_bwd(q, kp, vp, bias, do, ex=None):
    def body(q_ref, k_ref, v_ref, b_ref, do_ref, dq_ref, dk_ref, dv_ref, db_ref):
        cb = pl.program_id(1)

        @pl.when(cb == 0)
        def _():
            dk_ref[...] = jnp.zeros(dk_ref.shape, F32)
            dv_ref[...] = jnp.zeros(dv_ref.shape, F32)
            db_ref[...] = jnp.zeros(db_ref.shape, F32)

        band = pl.ds(pl.multiple_of(cb * QB, QB), KB)
        low = lax.broadcasted_iota(jnp.int32, (QB, 128), 1) < HEAD

        def compute(first_key):
            for pair in range(HEADS_PER_STEP // 2):
                lanes = pl.ds(pair * 128, 128)
                kb, vb = k_ref[band, lanes], v_ref[band, lanes]
                qv, dov = q_ref[:, lanes], do_ref[:, lanes]
                dq = jnp.zeros((QB, 128), F32)
                dkb = jnp.zeros((KB, 128), F32)
                dvb = jnp.zeros((KB, 128), F32)
                for hh in range(2):
                    sel = low if hh == 0 else jnp.logical_not(low)
                    doh = dov * _half_scale(hh, 1.0)
                    p = _probs(qv * _half_scale(hh, ATTN_SCALE), kb, b_ref[2 * pair + hh], first_key)
                    dp = _nt(doh, vb)
                    dvb = dvb + _tn(p.astype(BF16), doh)
                    ds = p * (dp - jnp.sum(dp * p, axis=-1, keepdims=True))
                    db_ref[2 * pair + hh] += ds
                    dsb = (ds * ATTN_SCALE).astype(BF16)
                    dq = dq + jnp.where(sel, _nn(dsb, kb), 0.0)
                    dkb = dkb + _tn(dsb, qv * _half_scale(hh, 1.0))
                dq_ref[:, lanes] = dq
                dk_ref[band, lanes] += dkb
                dv_ref[band, lanes] += dvb

        _by_padding(cb, compute)

    qspec = pl.BlockSpec((QB, ATT_LANES), lambda hg, cb: (cb, hg))
    kspec = pl.BlockSpec((PADK + S, ATT_LANES), lambda hg, cb: (0, hg))
    bspec = pl.BlockSpec((HEADS_PER_STEP, QB, KB), lambda hg, cb: (hg, 0, 0))
    kf = jax.ShapeDtypeStruct((PADK + S, D), F32)
    return _call_hosting(
        body, ex, _grid_phases((N_HG, N_QB)), name="attn_bwd", grid=(N_HG, N_QB),
        in_specs=[qspec, kspec, kspec, bspec, qspec],
        out_specs=[qspec, kspec, kspec, bspec],
        out_shape=[jax.ShapeDtypeStruct((S, D), F32), kf, kf, jax.ShapeDtypeStruct((N_HEADS, QB, KB), F32)],
        scratch_shapes=[], args=(q, kp, vp, bias, do), compiler_params=_params("arbitrary", "arbitrary"),
    )


def _adam_math(w, g, m, v):
    m = ADAM_B1 * m + (1.0 - ADAM_B1) * g
    v = ADAM_B2 * v + (1.0 - ADAM_B2) * (g * g)
    m_hat = m / (1.0 - ADAM_B1 ** ADAM_STEP)
    v_hat = v / (1.0 - ADAM_B2 ** ADAM_STEP)
    delta = -ADAM_LR * (m_hat / (jnp.sqrt(v_hat) + ADAM_EPS) + ADAM_WD * w)
    return delta, m, v


def _row_tile(r):
    for cand in (256, 176, 128, 64, 32, 16, 8):
        if r % cand == 0:
            return cand
    return r


def _adam_layer(w, parts, m, v, layer, name, prev=None):
    nl, r, c = w.shape
    count = parts.shape[0]
    tr = _row_tile(r)

    def body(w_ref, g_ref, m_ref, v_ref, *rest):
        go_ref, d_ref, nm_ref, nv_ref = rest[-4:]
        gv = g_ref[0].astype(F32)
        for q in range(1, count):
            gv = gv + g_ref[q].astype(F32)
        delta, nm, nv = _adam_math(w_ref[0], gv, m_ref[0], v_ref[0])
        go_ref[0] = gv
        d_ref[0] = delta
        nm_ref[0] = nm
        nv_ref[0] = nv

    lspec = pl.BlockSpec((1, tr, c), lambda i: (layer, i, 0))
    sd = jax.ShapeDtypeStruct((nl, r, c), F32)
    extra = list(prev) if prev is not None else []
    return _call(
        body, name=name, grid=(r // tr,),
        in_specs=[lspec, pl.BlockSpec((count, tr, c), lambda i: (0, i, 0)), lspec, lspec] + [ANY] * len(extra),
        out_specs=[lspec] * 4, out_shape=[sd] * 4,
        input_output_aliases={4 + t: t for t in range(len(extra))},
        compiler_params=_params("parallel"),
    )(w, parts, m, v, *extra)


def _adam(w, g, m, v, name):
    r, c = w.shape
    tr = _row_tile(r)
    count = g.shape[0] if g.ndim == 3 else 0

    def body(w_ref, g_ref, m_ref, v_ref, go_ref, d_ref, nm_ref, nv_ref):
        if count:
            gv = g_ref[0].astype(F32)
            for q in range(1, count):
                gv = gv + g_ref[q].astype(F32)
        else:
            gv = g_ref[...]
        delta, nm, nv = _adam_math(w_ref[...], gv, m_ref[...], v_ref[...])
        go_ref[...] = gv
        d_ref[...] = delta
        nm_ref[...] = nm
        nv_ref[...] = nv

    spec = pl.BlockSpec((tr, c), lambda i: (i, 0))
    gspec = pl.BlockSpec((count, tr, c), lambda i: (0, i, 0)) if count else spec
    sd = jax.ShapeDtypeStruct((r, c), F32)
    return _call(
        body, name=name, grid=(r // tr,), in_specs=[spec, gspec, spec, spec], out_specs=[spec] * 4,
        out_shape=[sd] * 4, compiler_params=_params("parallel"),
    )(w, g, m, v)


def _pad16(rows):
    return jnp.pad(rows, ((0, 16 - rows.shape[0]), (0, 0)))


def kernel(x, norm_mix_g, norm_ffn_g, pool_w, pool_b, pool_scale, kv_norm_g, w_k, w_v, k_norm_g, w_q, q_norm_g, rel_bias, w_o, w_gate, w_up, w_down, loss_target, m_norm_mix_g, m_norm_ffn_g, m_pool_w, m_pool_b, m_pool_scale, m_kv_norm_g, m_w_k, m_w_v, m_k_norm_g, m_w_q, m_q_norm_g, m_rel_bias, m_w_o, m_w_gate, m_w_up, m_w_down, v_norm_mix_g, v_norm_ffn_g, v_pool_w, v_pool_b, v_pool_scale, v_kv_norm_g, v_w_k, v_w_v, v_k_norm_g, v_w_q, v_q_norm_g, v_rel_bias, v_w_o, v_w_gate, v_w_up, v_w_down):
    assert x.shape == (1, S, D) and w_gate.shape == (2, D, F_SHARD) and w_k.shape == (D_SHARD, D)
    xin, target = x[0], loss_target[0]

    ffn_shards = [[w_gate[layer].T.astype(BF16), w_up[layer].T.astype(BF16), w_down[layer].astype(BF16)]
                  for layer in range(2)]
    att_shards = [w_k.astype(BF16), w_v.astype(BF16), w_q[0].astype(BF16), w_o[0].astype(BF16)]
    pool_shard = pool_w[0].astype(BF16).reshape(N_GROUPS * POOL_SHARD, GROUP)
    small = jnp.concatenate([pool_b[0].reshape(1, N_GROUPS * POOL_SHARD), pool_scale], axis=1)

    full0 = _run_exchange(_relay_gather_exchange(ffn_shards[0] + [pool_shard, _pad16(small)]), "gather_layer0")
    ffn_w0 = [a.reshape(F, D) for a in full0[:3]]
    pw_f = full0[3].reshape(N_DEV, N_GROUPS, POOL_SHARD, GROUP).transpose(1, 0, 2, 3).reshape(N_GROUPS, GROUP, GROUP)
    small_f = full0[4][:, 0, :]
    pb_f = small_f[:, :N_GROUPS * POOL_SHARD].reshape(N_DEV, N_GROUPS, POOL_SHARD).transpose(1, 0, 2).reshape(1, D)
    ps_f = small_f[:, N_GROUPS * POOL_SHARD:].reshape(1, D)

    g_mix0, g_mix1 = norm_mix_g[0:1], norm_mix_g[1:2]
    g_ffn0, g_ffn1 = norm_ffn_g[0:1], norm_ffn_g[1:2]
    g_kv = kv_norm_g.reshape(1, D)
    gk_t = jnp.tile(k_norm_g.reshape(1, HEAD), (1, N_HEADS))
    gq_t = jnp.tile(q_norm_g.reshape(1, HEAD), (1, N_HEADS))

    x1, diff = _pool_fwd(xin, g_mix0, pw_f, pb_f, ps_f)
    (x2, hf0, gg0, uu0), full_att = _ffn_fwd(x1, g_ffn0, *ffn_w0, name="ffn_fwd", ex=_relay_gather_exchange(att_shards))
    wk_f, wv_f, wq_f, wo_f = [a.reshape(D, D) for a in full_att]
    (hkv, h1, kpre, qpre, kp, vp, qq), full1_gate = _kvq_proj(x2, g_kv, g_mix1, wk_f, wv_f, wq_f, gk_t, gq_t,
                                                                ex=_relay_gather_exchange(ffn_shards[1][:1]))
    bias = _bias_band(_toeplitz_from_table(rel_bias[0]))
    (att,), full1_rest = _attn_fwd(qq, kp, vp, bias, ex=_relay_gather_exchange(ffn_shards[1][1:]))
    ffn_w1 = [a.reshape(F, D) for a in full1_gate + full1_rest]
    x3 = _mm(att, wo_f, "nn", F32, "attn_out", add=x2)
    (dx4, loss_rows, hf1, gg1, uu1), _ = _ffn_fwd(x3, g_ffn1, *ffn_w1, name="ffn_fwd_loss", target=target)

    def blocks(dw):
        return dw.reshape(N_DEV, dw.shape[0] // N_DEV, dw.shape[1])

    (dgg1, duu1, dwg1, dwu1, dwd1), _ = _ffn_bwd_weights(dx4, hf1, gg1, uu1, ffn_w1[2], name="ffn_bwd1")
    (dx3, dg_ffn1), _ = _ffn_bwd_input(dx4, x3, g_ffn1, [dgg1, duu1], ffn_w1[:2], "ffn_dx1")
    datt = _mm(dx3, wo_f, "nt", BF16, "d_attn")
    dwo = _mm(att, dx3, "tn", BF16, "d_wo")
    parts1 = [blocks(dw) for dw in (dwg1, dwu1, dwd1, dwo)]
    (dq, dkp, dvp, dband), stage1 = _attn_bwd(qq, kp, vp, bias, datt, ex=_pair_exchange(parts1))
    both1 = _pair_add(parts1, stage1, "pair_add_ffn1", copies=2)
    flight1, chip1, land1, _ = _chip_exchange_start(both1[:4], both1[4:], dq, "scatter_ffn1_start")
    (dx2, dqpre, dkpre, dg_mix1, dg_kv, dgq, dgk), _ = _qkv_input_bwd(
        dq, qpre, gq_t, dkp, kpre, gk_t, dvp, wq_f, wk_f, wv_f, dx3, x2, g_mix1, g_kv)
    dwq = _mm(h1, dqpre, "tn", BF16, "d_wq")
    dwk = _mm(hkv, dkpre, "tn", BF16, "d_wk")
    dwv = _mm(hkv, dvp, "tn", BF16, "d_wv", skip_rows=PADK)
    parts_att = [blocks(dw) for dw in (dwk, dwv, dwq)]
    (dgg0, duu0, *dw0), stage_att = _ffn_bwd_weights(
        dx2, hf0, gg0, uu0, ffn_w0[2], name="ffn_bwd0", ex=_pair_exchange(parts_att))
    both_att = _pair_add(parts_att, stage_att, "pair_add_att", copies=2)
    flight_att, chip_att, land_att, _ = _chip_exchange_start(both_att[:3], both_att[3:], dx2, "scatter_att_start")
    parts0 = [blocks(dw) for dw in dw0]
    (dx1, dg_ffn0), stage0 = _ffn_bwd_input(dx2, x1, g_ffn0, [dgg0, duu0], ffn_w0[:2], "ffn_dx0",
                                            ex=_pair_exchange(parts0))
    both = _pair_add(parts0, stage0, "pair_add_ffn0", copies=2)
    flight, chip0, land0, token = _chip_exchange_start(both[:3], both[3:], dx1, "scatter_ffn0_start")
    (grad_x, dpw, db_rows, ds_rows, dg_mix0), _ = _pool_bwd(dx1, xin, diff, g_mix0 + token[0:1, 0:1], pw_f, pb_f, ps_f)
    dtp, seg = _bias_grad(dband, dpw)

    weights = dict(norm_mix_g=norm_mix_g, norm_ffn_g=norm_ffn_g, pool_w=pool_w, pool_b=pool_b,
                   pool_scale=pool_scale, kv_norm_g=kv_norm_g, w_k=w_k, w_v=w_v, k_norm_g=k_norm_g, w_q=w_q,
                   q_norm_g=q_norm_g, rel_bias=rel_bias, w_o=w_o, w_gate=w_gate, w_up=w_up, w_down=w_down)
    mom1 = dict(norm_mix_g=m_norm_mix_g, norm_ffn_g=m_norm_ffn_g, pool_w=m_pool_w, pool_b=m_pool_b,
                pool_scale=m_pool_scale, kv_norm_g=m_kv_norm_g, w_k=m_w_k, w_v=m_w_v, k_norm_g=m_k_norm_g,
                w_q=m_w_q, q_norm_g=m_q_norm_g, rel_bias=m_rel_bias, w_o=m_w_o, w_gate=m_w_gate, w_up=m_w_up,
                w_down=m_w_down)
    mom2 = dict(norm_mix_g=v_norm_mix_g, norm_ffn_g=v_norm_ffn_g, pool_w=v_pool_w, pool_b=v_pool_b,
                pool_scale=v_pool_scale, kv_norm_g=v_kv_norm_g, w_k=v_w_k, w_v=v_w_v, k_norm_g=v_k_norm_g,
                w_q=v_w_q, q_norm_g=v_q_norm_g, rel_bias=v_rel_bias, w_o=v_w_o, w_gate=v_w_gate, w_up=v_w_up,
                w_down=v_w_down)
    names = list(weights)
    grads, deltas, new_m, new_v = {}, {}, {}, {}

    def adam_flat(nm, parts):
        shape = weights[nm].shape
        flat = lambda a: a.reshape(-1, shape[-1])
        done = _adam(flat(weights[nm]), parts, flat(mom1[nm]), flat(mom2[nm]), "adam_" + nm)
        grads[nm], deltas[nm], new_m[nm], new_v[nm] = [a.reshape(shape) for a in done]

    ffn_names = ("w_gate", "w_up", "w_down")
    def ffn_view(nm, a):
        return a if nm == "w_down" else a.transpose(0, 2, 1)

    recv1 = _chip_exchange_wait(flight1, chip1, land1, dtp, "scatter_ffn1_wait")
    layer1 = {nm: _adam_layer(ffn_view(nm, weights[nm]), g, ffn_view(nm, mom1[nm]), ffn_view(nm, mom2[nm]), 1,
                              "adam1_" + nm)
              for nm, g in zip(ffn_names, recv1)}
    adam_flat("w_o", recv1[3])
    recv_att = _chip_exchange_wait(flight_att, chip_att, land_att, deltas["w_o"], "scatter_att_wait")
    for nm, parts in zip(("w_k", "w_v", "w_q"), recv_att):
        adam_flat(nm, parts)

    dpw_blocks = dpw.reshape(N_GROUPS, N_DEV, POOL_SHARD, GROUP).transpose(1, 0, 2, 3)
    dpw_blocks = dpw_blocks.reshape(N_DEV, N_GROUPS * POOL_SHARD, GROUP).astype(BF16)
    misc = jnp.concatenate([dgk[0:1, 0:HEAD], dgq[0:1, 0:HEAD], loss_rows[0:1, 0:1],
                            seg[:, 0, 0].reshape(1, N_HEADS), seg[:, 0, 1].reshape(1, N_HEADS)], axis=1)
    misc = jnp.pad(misc, ((0, 0), (0, D - misc.shape[1])))
    vec_rows = jnp.concatenate([dg_mix0[0:1], dg_mix1[0:1], dg_ffn0[0:1], dg_ffn1[0:1], dg_kv[0:1],
                                db_rows[0:1], ds_rows[0:1], misc], axis=0)
    pack = jnp.concatenate([vec_rows, dtp.reshape(N_HEADS, TOEP)], axis=0)
    tot, recv_pool = _small_exchanges(pack, dpw_blocks, "small_exchanges", after=deltas["w_q"])

    loss = tot[7, 2 * HEAD]
    seg_tot = jnp.stack([tot[7, 2 * HEAD + 1:2 * HEAD + 1 + N_HEADS],
                         tot[7, 2 * HEAD + 1 + N_HEADS:2 * HEAD + 1 + 2 * N_HEADS]], axis=1)
    me = 4 * lax.axis_index("x") + 2 * lax.axis_index("y") + lax.axis_index("c")
    g_pool_b = lax.dynamic_slice_in_dim(tot[5].reshape(N_GROUPS, GROUP), me * POOL_SHARD, POOL_SHARD, axis=1)
    grads.update(
        norm_mix_g=tot[0:2], norm_ffn_g=tot[2:4], kv_norm_g=tot[4], k_norm_g=tot[7, 0:HEAD],
        q_norm_g=tot[7, HEAD:2 * HEAD].reshape(1, HEAD),
        rel_bias=_table_grad_from_toeplitz(tot[8:8 + N_HEADS], seg_tot).reshape(1, N_HEADS, N_REL),
        pool_b=g_pool_b.reshape(1, N_GROUPS, POOL_SHARD),
        pool_scale=lax.dynamic_slice_in_dim(tot[6:7], me * D_SHARD, D_SHARD, axis=1))
    adam_flat("pool_w", recv_pool)
    small_names = [nm for nm in names if nm not in ffn_names + ("w_k", "w_v", "w_q", "w_o", "pool_w")]

    def pack_small(tree):
        cols = []
        for nm in small_names:
            flat = tree[nm].reshape(-1)
            cols.append(jnp.pad(flat, (0, -flat.shape[0] % 1024)))
        return jnp.concatenate(cols).reshape(-1, 128)

    _, dl, m1, m2 = _adam(pack_small(weights), pack_small(grads), pack_small(mom1), pack_small(mom2), "adam_small")

    def unpack_small(packed, out):
        flat, off = packed.reshape(-1), 0
        for nm in small_names:
            size = weights[nm].size
            out[nm] = flat[off:off + size].reshape(weights[nm].shape)
            off += size + (-size % 1024)

    unpack_small(dl, deltas)
    unpack_small(m1, new_m)
    unpack_small(m2, new_v)

    recv0 = _chip_exchange_wait(flight, chip0, land0, dl, "scatter_ffn0_wait")
    for nm, g in zip(ffn_names, recv0):
        done = _adam_layer(ffn_view(nm, weights[nm]), g, ffn_view(nm, mom1[nm]), ffn_view(nm, mom2[nm]), 0,
                           "adam0_" + nm, prev=layer1[nm])
        grads[nm], deltas[nm], new_m[nm], new_v[nm] = [ffn_view(nm, a) for a in done]

    return (loss, grad_x[None], *[grads[nm] for nm in names], *[deltas[nm] for nm in names],
            *[new_m[nm] for nm in names], *[new_v[nm] for nm in names])
```

```python
import functools

import jax
import jax.numpy as jnp
from jax import lax
from jax.experimental import pallas as pl
from jax.experimental.pallas import tpu as pltpu

F32 = jnp.float32
BF16 = jnp.bfloat16
MESH_ID = pl.DeviceIdType.MESH

N_DEV = 8
S = 2048
D = 1024
F = 2816
F_SHARD = F // N_DEV
D_SHARD = D // N_DEV
N_GROUPS = 4
GROUP = D // N_GROUPS
POOL_SHARD = GROUP // N_DEV
MAX_WIN = 16
HEAD = 64
N_HEADS = D // HEAD
CHUNK = 64
LEFT = 8
QB = 4 * CHUNK
KB = QB + LEFT * CHUNK
PADK = LEFT * CHUNK
TOEP = 1024
N_REL = 257
MAX_REL = 128
EPS = 1e-6
NEG_INF = -1e30
ATTN_SCALE = HEAD ** -0.5

ADAM_LR = 0.001
ADAM_B1 = 0.9
ADAM_B2 = 0.999
ADAM_EPS = 1e-08
ADAM_WD = 0.01
ADAM_STEP = 10

VMEM_LIMIT = 52 * 1024 * 1024

ANY = pl.BlockSpec(memory_space=pl.ANY)
VMEM = pl.BlockSpec(memory_space=pltpu.VMEM)


def _call(body, **kw):
    return pl.pallas_call(body, **kw)


def _params(*sem):
    return pltpu.CompilerParams(dimension_semantics=sem, vmem_limit_bytes=VMEM_LIMIT)


def _dot(a, b, dims):
    return lax.dot_general(a, b, (dims, ((), ())), preferred_element_type=F32)


def _nn(a, b):
    return _dot(a, b, ((1,), (0,)))


def _nt(a, b):
    return _dot(a, b, ((1,), (1,)))


def _tn(a, b):
    return _dot(a, b, ((0,), (0,)))


def _rstd(x):
    return lax.rsqrt(jnp.mean(x * x, axis=-1, keepdims=True) + EPS)


def _rms_bwd(dh, x, r, g):
    gd = dh * g
    return r * gd - x * (r * r * r) * jnp.mean(gd * x, axis=-1, keepdims=True)


def _colsum8(v):
    return jnp.broadcast_to(jnp.sum(v, axis=0, keepdims=True), (8, v.shape[1]))


def _seg_sum(v):
    r = lax.broadcasted_iota(jnp.int32, (128, 128), 0) // HEAD
    c = lax.broadcasted_iota(jnp.int32, (128, 128), 1) // HEAD
    ones = jnp.where(r == c, 1.0, 0.0).astype(BF16)
    out = []
    for blk in range(v.shape[1] // 128):
        part = v[:, blk * 128:(blk + 1) * 128]
        hi = part.astype(BF16)
        rest = part - hi.astype(F32)
        mid = rest.astype(BF16)
        lo = (rest - mid.astype(F32)).astype(BF16)
        out.append(_nn(hi, ones) + _nn(mid, ones) + _nn(lo, ones))
    return jnp.concatenate(out, axis=1)


def _place():
    return lax.axis_index("x"), lax.axis_index("y"), lax.axis_index("c")


class _Exchange:
    def __init__(self, ins, out_shape, sems, start, mid, finish, early=None):
        self.ins, self.out_shape, self.sems = list(ins), list(out_shape), list(sems)
        self.start, self.mid, self.finish = start, mid, finish
        self.early = early if early is not None else (lambda ins, outs, sems: None)


def _relay_gather_exchange(shards):
    n = len(shards)

    def tools(ins, outs, sems):
        send_sems, recv_sems, local_sems = sems
        x, y, c = _place()
        me, sib = (x, y, c), (x, y, 1 - c)
        xn, yn, dg = (1 - x, y, c), (x, 1 - y, c), (1 - x, 1 - y, c)
        other = lambda dev: (dev[0], dev[1], 1 - c)

        def piece(k, ref, half):
            rows = shards[k].shape[0] // 2
            return ref if half is None else ref.at[pl.ds(half * rows, rows)]

        def held(k, dev, half):
            return piece(k, outs[k].at[4 * dev[0] + 2 * dev[1] + dev[2]], half)

        def copy(k, s, dev, half, to, own=False):
            return pltpu.make_async_remote_copy(
                src_ref=piece(k, ins[k], half) if own else held(k, dev, half), dst_ref=held(k, dev, half),
                send_sem=send_sems.at[k, s], recv_sem=recv_sems.at[k, s], device_id=to, device_id_type=MESH_ID)

        def mine(k):
            return pltpu.make_async_copy(ins[k], held(k, me, None), local_sems.at[k])

        def sent(k):
            return [copy(k, 0, me, None, sib, own=True), copy(k, 1, me, 0, xn, own=True), copy(k, 2, me, 1, yn, own=True),
                    copy(k, 3, me, 0, yn, own=True), copy(k, 4, me, 1, xn, own=True)]

        relays = [(1, xn, 0, [(5, yn), (7, sib)]), (2, yn, 1, [(6, xn), (9, sib)]), (3, yn, 0, [(10, sib)]),
                  (4, xn, 1, [(8, sib)]), (6, dg, 1, [(11, sib)]), (5, dg, 0, [(12, sib)])]
        from_sibling = [(0, other(me), None), (7, other(xn), 0), (8, other(xn), 1), (9, other(yn), 1),
                        (10, other(yn), 0), (11, other(dg), 1), (12, other(dg), 0)]
        return mine, sent, copy, relays, from_sibling, me

    def start(ins, outs, sems):
        mine, sent, _, _, _, _ = tools(ins, outs, sems)
        for k in range(n):
            mine(k).start()
        for k in range(n):
            for cp in sent(k):
                cp.start()

    def pass_on(ins, outs, sems, which):
        _, _, copy, relays, _, me = tools(ins, outs, sems)
        for k in range(n):
            for s, dev, half, onward in relays[which]:
                copy(k, s, dev, half, me).wait_recv()
                for s2, to in onward:
                    copy(k, s2, dev, half, to).start()

    def early(ins, outs, sems):
        pass_on(ins, outs, sems, slice(0, 2))

    def mid(ins, outs, sems):
        pass_on(ins, outs, sems, slice(2, None))

    def finish(ins, outs, sems):
        mine, sent, copy, relays, from_sibling, me = tools(ins, outs, sems)
        for k in range(n):
            for s, dev, half in from_sibling:
                copy(k, s, dev, half, me).wait_recv()
        for k in range(n):
            for cp in sent(k):
                cp.wait_send()
            for _, dev, half, onward in relays:
                for s2, to in onward:
                    copy(k, s2, dev, half, to).wait_send()
            mine(k).wait()

    return _Exchange(
        shards, [jax.ShapeDtypeStruct((N_DEV,) + a.shape, a.dtype) for a in shards],
        [pltpu.SemaphoreType.DMA((n, 13)), pltpu.SemaphoreType.DMA((n, 13)), pltpu.SemaphoreType.DMA((n,))],
        start, mid, finish, early)


def _peer(x, y, c, m):
    px = 1 - x if m & 4 else x
    py = 1 - y if m & 2 else y
    pc = 1 - c if m & 1 else c
    return px, py, pc


N_CHIPS = N_DEV // 2


def _pair_exchange(parts):
    n = len(parts)

    def copies(ins, outs, sems):
        send_sems, recv_sems = sems
        x, y, c = _place()
        return [pltpu.make_async_remote_copy(
            src_ref=ins[k].at[2 * q + 1 - c], dst_ref=outs[k].at[q], send_sem=send_sems.at[k, q],
            recv_sem=recv_sems.at[k, q], device_id=(x, y, 1 - c), device_id_type=MESH_ID)
            for k in range(n) for q in range(N_CHIPS)]

    def start(ins, outs, sems):
        for cp in copies(ins, outs, sems):
            cp.start()

    def mid(ins, outs, sems):
        pass

    def finish(ins, outs, sems):
        for cp in copies(ins, outs, sems):
            cp.wait_recv()
        for cp in copies(ins, outs, sems):
            cp.wait_send()

    return _Exchange(
        parts, [jax.ShapeDtypeStruct((N_CHIPS,) + a.shape[1:], a.dtype) for a in parts],
        [pltpu.SemaphoreType.DMA((n, N_CHIPS)), pltpu.SemaphoreType.DMA((n, N_CHIPS))], start, mid, finish)


def _pair_add(parts, stage, name, copies=1):
    n = len(parts)
    core = lax.axis_index("c").reshape(1)

    def body(core_ref, *refs):
        for k in range(n):
            mine, theirs = refs[k], refs[n + k]
            total = (mine[0, 0].astype(F32) + theirs[0].astype(F32)).astype(BF16)
            for rep in range(copies):
                refs[(2 + rep) * n + k][0] = total

    in_specs, out_specs = [], []
    for a in parts:
        _, r, cdim = a.shape
        in_specs.append(pl.BlockSpec((1, 1, r // 2, cdim), lambda q, i, core_ref: (q, core_ref[0], i, 0)))
    for a in parts:
        _, r, cdim = a.shape
        in_specs.append(pl.BlockSpec((1, r // 2, cdim), lambda q, i, core_ref: (q, i, 0)))
        out_specs.append(pl.BlockSpec((1, r // 2, cdim), lambda q, i, core_ref: (q, i, 0)))
    return list(_call(
        body, name=name,
        grid_spec=pltpu.PrefetchScalarGridSpec(num_scalar_prefetch=1, grid=(N_CHIPS, 2), in_specs=in_specs,
                                               out_specs=out_specs * copies),
        out_shape=[jax.ShapeDtypeStruct(s.shape, BF16) for s in stage] * copies,
        compiler_params=_params("arbitrary", "arbitrary"),
    )(core, *[a.reshape((N_CHIPS, 2) + a.shape[1:]) for a in parts], *stage))


HBM = pl.BlockSpec(memory_space=pltpu.HBM)
SEMAPHORES = pl.BlockSpec(memory_space=pltpu.SEMAPHORE)


def _chip_copies(srcs, lands, send_sems, recv_sems, mine_is_dst):
    x, y, c = _place()
    me = 2 * x + y
    out = []
    for m in range(1, N_CHIPS):
        px, py, _ = _peer(x, y, c, 2 * m)
        peer = 2 * px + py
        for k in range(len(srcs)):
            pair = k * (N_CHIPS - 1) + m - 1
            out.append(pltpu.make_async_remote_copy(
                src_ref=srcs[k].at[peer], dst_ref=lands[k].at[me if mine_is_dst else peer],
                send_sem=send_sems[pair], recv_sem=recv_sems[pair],
                device_id=(px, py, c), device_id_type=MESH_ID))
    return out


def _chip_exchange_start(chip_parts, lands, after, name):
    n = len(chip_parts)
    pairs = n * (N_CHIPS - 1)

    def body(*refs):
        srcs, zones = refs[:n], refs[n:2 * n]
        sems = refs[2 * n + 1:2 * n + 1 + 2 * pairs]
        token = refs[-1]
        for cp in _chip_copies(srcs, zones, sems[:pairs], sems[pairs:], True):
            cp.start()
        token[...] = jnp.zeros(token.shape, F32)

    thru = [pltpu.HBM(a.shape, a.dtype) for a in chip_parts + lands]
    hbm = [pltpu.with_memory_space_constraint(a, pltpu.HBM) for a in chip_parts + lands]
    res = _call(
        body, name=name,
        out_shape=[pltpu.SemaphoreType.DMA(())] * (2 * pairs) + thru + [jax.ShapeDtypeStruct((8, 128), F32)],
        in_specs=[HBM] * (2 * n) + [ANY], out_specs=[SEMAPHORES] * (2 * pairs) + [HBM] * (2 * n) + [VMEM],
        input_output_aliases={i: 2 * pairs + i for i in range(2 * n)},
        compiler_params=pltpu.CompilerParams(has_side_effects=pltpu.SideEffectType.DATAFLOW_SIDE_EFFECTING),
    )(*hbm, after)
    sems, rest = list(res[:2 * pairs]), res[2 * pairs:]
    return sems, list(rest[:n]), list(rest[n:2 * n]), rest[-1]


def _chip_exchange_wait(sems, chip_parts, lands, after, name):
    n = len(chip_parts)
    pairs = n * (N_CHIPS - 1)

    def body(*refs):
        srcs, zones = refs[:n], refs[n:2 * n]
        sem_refs = refs[2 * n:2 * n + 2 * pairs]
        for cp in _chip_copies(srcs, zones, sem_refs[:pairs], sem_refs[pairs:], False):
            cp.wait_send()
            cp.wait_recv()

    thru = [pltpu.HBM(a.shape, a.dtype) for a in chip_parts + lands]
    res = _call(
        body, name=name, out_shape=thru,
        in_specs=[HBM] * (2 * n) + [SEMAPHORES] * (2 * pairs) + [ANY], out_specs=[HBM] * (2 * n),
        input_output_aliases={i: i for i in range(2 * n)},
        compiler_params=pltpu.CompilerParams(has_side_effects=pltpu.SideEffectType.DATAFLOW_SIDE_EFFECTING),
    )(*chip_parts, *lands, *sems, after)
    return list(res[n:])


def _run_exchange(ex, name, after=None):
    n_in, n_out = len(ex.ins), len(ex.out_shape)
    order = [] if after is None else [after]

    def body(*refs):
        ins, outs = refs[:n_in], refs[n_in + len(order):n_in + len(order) + n_out]
        sems = refs[n_in + len(order) + n_out:]
        ex.start(ins, outs, sems)
        ex.early(ins, outs, sems)
        ex.mid(ins, outs, sems)
        ex.finish(ins, outs, sems)

    return list(_call(body, name=name, out_shape=ex.out_shape, in_specs=[ANY] * (n_in + len(order)),
                      out_specs=[ANY] * n_out, scratch_shapes=ex.sems)(*ex.ins, *order))


def _call_hosting(body, ex, phases, *, in_specs, out_specs, out_shape, scratch_shapes, args, **kw):
    n_in, n_out, n_scr = len(in_specs), len(out_specs), len(scratch_shapes)
    if ex is None:
        res = _call(body, in_specs=in_specs, out_specs=out_specs, out_shape=out_shape,
                    scratch_shapes=scratch_shapes, **kw)(*args)
        return list(res), []
    n_xin, n_xout = len(ex.ins), len(ex.out_shape)

    def hosting(*refs):
        a, b = n_in, n_in + n_xin
        c, d = b + n_out, b + n_out + n_xout
        ins, xins, outs, xouts = refs[:a], refs[a:b], refs[b:c], refs[c:d]
        scr, sems = refs[d:d + n_scr], refs[d + n_scr:]
        first, early, mid, last = phases()

        @pl.when(first)
        def _():
            ex.start(xins, xouts, sems)

        body(*ins, *outs, *scr)

        @pl.when(early)
        def _():
            ex.early(xins, xouts, sems)

        @pl.when(mid)
        def _():
            ex.mid(xins, xouts, sems)

        @pl.when(last)
        def _():
            ex.finish(xins, xouts, sems)

    res = _call(hosting, in_specs=list(in_specs) + [ANY] * n_xin, out_specs=list(out_specs) + [ANY] * n_xout,
                out_shape=list(out_shape) + ex.out_shape, scratch_shapes=list(scratch_shapes) + ex.sems,
                **kw)(*args, *ex.ins)
    return list(res[:n_out]), list(res[n_out:])


def _grid_phases(dims, mid_fraction=0.8, early_fraction=0.4):
    total = 1
    for d in dims:
        total *= d
    mid = min(max(int(total * mid_fraction), 1), total - 1)
    early = min(int(total * early_fraction), mid)

    def phases():
        step = pl.program_id(0)
        for axis in range(1, len(dims)):
            step = step * dims[axis] + pl.program_id(axis)
        return step == 0, step == early, step == mid, step == total - 1
    return phases


def _small_exchanges(pack, blocks, name, after):
    rows = pack.shape[0]

    def body(in_ref, blk_ref, _, out_ref, got_ref, recv, send_sems, recv_sems, blk_send, blk_recv, blk_local):
        x, y, c = _place()
        me = 4 * x + 2 * y + c
        peers = [_peer(x, y, c, m) for m in range(1, N_DEV)]
        index = [4 * px + 2 * py + pc for px, py, pc in peers]

        def block_copy(m, landing):
            return pltpu.make_async_remote_copy(
                src_ref=blk_ref.at[index[m]], dst_ref=got_ref.at[index[m] if landing else me],
                send_sem=blk_send.at[m], recv_sem=blk_recv.at[m], device_id=peers[m], device_id_type=MESH_ID)

        def pack_copy(m, landing):
            return pltpu.make_async_remote_copy(
                src_ref=in_ref, dst_ref=recv.at[index[m] if landing else me], send_sem=send_sems.at[m],
                recv_sem=recv_sems.at[m], device_id=peers[m], device_id_type=MESH_ID)

        own = pltpu.make_async_copy(blk_ref.at[me], got_ref.at[me], blk_local)
        own.start()
        recv[me] = in_ref[...]
        for m in range(N_DEV - 1):
            pack_copy(m, False).start()
            block_copy(m, False).start()
        for m in range(N_DEV - 1):
            pack_copy(m, True).wait_recv()
        acc = recv[0]
        for d in range(1, N_DEV):
            acc = acc + recv[d]
        out_ref[...] = acc
        for m in range(N_DEV - 1):
            block_copy(m, True).wait_recv()
        for m in range(N_DEV - 1):
            pack_copy(m, False).wait_send()
            block_copy(m, False).wait_send()
        own.wait()

    return _call(
        body, name=name,
        out_shape=[jax.ShapeDtypeStruct(pack.shape, F32), jax.ShapeDtypeStruct(blocks.shape, blocks.dtype)],
        in_specs=[VMEM, ANY, ANY], out_specs=[VMEM, ANY],
        scratch_shapes=[pltpu.VMEM((N_DEV, rows, pack.shape[1]), F32)] + [pltpu.SemaphoreType.DMA((7,))] * 4
        + [pltpu.SemaphoreType.DMA],
        compiler_params=pltpu.CompilerParams(vmem_limit_bytes=VMEM_LIMIT),
    )(pack, blocks, after)


POOL_TS = 256


def _pool_counts(first_row, rows, win):
    t = first_row + lax.broadcasted_iota(jnp.int32, (rows, 1), 0)
    return jnp.minimum(t + 1, win).astype(F32)


def _pool_fwd(x, g, w, b, scale):
    nt = S // POOL_TS

    def body(x_ref, g_ref, w_ref, b_ref, s_ref, out_ref, diff_ref, ext):
        i = pl.program_id(0)

        @pl.when(i == 0)
        def _():
            ext[0:MAX_WIN, :] = jnp.zeros((MAX_WIN, D), F32)

        @pl.when(i > 0)
        def _():
            ext[0:MAX_WIN, :] = ext[POOL_TS:POOL_TS + MAX_WIN, :]

        xv = x_ref[...]
        h = xv * _rstd(xv) * g_ref[...]
        ext[MAX_WIN:, :] = h
        for gi in range(N_GROUPS):
            win = 2 << gi
            cols = slice(gi * GROUP, (gi + 1) * GROUP)
            sm = ext[:, cols]
            k = 1
            while k < win:
                sm = sm + pltpu.roll(sm, k, axis=0)
                k *= 2
            pooled = sm[MAX_WIN:, :] / _pool_counts(i * POOL_TS, POOL_TS, win)
            diff = (pooled - h[:, cols]).astype(BF16)
            yv = (_nn(diff, w_ref[gi]) + b_ref[:, cols]) * s_ref[:, cols]
            out_ref[:, cols] = xv[:, cols] + yv
            diff_ref[:, cols] = diff

    row = pl.BlockSpec((1, D), lambda i: (0, 0))
    tile = pl.BlockSpec((POOL_TS, D), lambda i: (i, 0))
    return _call(
        body, name="pool_fwd", grid=(nt,),
        in_specs=[tile, row, pl.BlockSpec((N_GROUPS, GROUP, GROUP), lambda i: (0, 0, 0)), row, row],
        out_specs=[tile, tile],
        out_shape=[jax.ShapeDtypeStruct((S, D), F32), jax.ShapeDtypeStruct((S, D), BF16)],
        scratch_shapes=[pltpu.VMEM((POOL_TS + MAX_WIN, D), F32)],
        compiler_params=_params("arbitrary"),
    )(x, g, w, b, scale)


def _pool_bwd(dy, x, diff, g, w, b, scale, ex=None):
    nt = S // POOL_TS

    def body(dy_ref, x_ref, diff_ref, g_ref, w_ref, b_ref, s_ref, gx_ref, dw_ref, db_ref, ds_ref, dg_ref, ext, dh):
        i = pl.program_id(0)
        first_row = (nt - 1 - i) * POOL_TS

        @pl.when(i == 0)
        def _():
            ext[POOL_TS:, :] = jnp.zeros((MAX_WIN, D), F32)
            dw_ref[...] = jnp.zeros(dw_ref.shape, F32)
            db_ref[...] = jnp.zeros(db_ref.shape, F32)
            ds_ref[...] = jnp.zeros(ds_ref.shape, F32)
            dg_ref[...] = jnp.zeros(dg_ref.shape, F32)

        @pl.when(i > 0)
        def _():
            ext[POOL_TS:, :] = ext[0:MAX_WIN, :]

        dyv = dy_ref[...]
        for gi in range(N_GROUPS):
            win = 2 << gi
            cols = slice(gi * GROUP, (gi + 1) * GROUP)
            dfb = diff_ref[:, cols]
            z = _nn(dfb, w_ref[gi]) + b_ref[:, cols]
            dyg = dyv[:, cols]
            ds_ref[:, cols] += _colsum8(dyg * z)
            dz = dyg * s_ref[:, cols]
            db_ref[:, cols] += _colsum8(dz)
            dzb = dz.astype(BF16)
            dw_ref[gi] += _tn(dfb, dzb)
            ddiff = _nt(dzb, w_ref[gi])
            ext[0:POOL_TS, cols] = ddiff / _pool_counts(first_row, POOL_TS, win)
            sm = ext[:, cols]
            k = 1
            while k < win:
                sm = sm + pltpu.roll(sm, POOL_TS + MAX_WIN - k, axis=0)
                k *= 2
            dh[:, cols] = sm[0:POOL_TS, :] - ddiff
        xv = x_ref[...]
        r = _rstd(xv)
        gv = g_ref[...]
        dhv = dh[...]
        dg_ref[...] += _colsum8(dhv * xv * r)
        gx_ref[...] = dyv + _rms_bwd(dhv, xv, r, gv)

    row = pl.BlockSpec((1, D), lambda i: (0, 0))
    tile = pl.BlockSpec((POOL_TS, D), lambda i: (nt - 1 - i, 0))
    acc = pl.BlockSpec((8, D), lambda i: (0, 0))
    wspec = pl.BlockSpec((N_GROUPS, GROUP, GROUP), lambda i: (0, 0, 0))
    return _call_hosting(
        body, ex, _grid_phases((nt,)), name="pool_bwd", grid=(nt,),
        in_specs=[tile, tile, tile, row, wspec, row, row],
        out_specs=[tile, wspec, acc, acc, acc],
        out_shape=[jax.ShapeDtypeStruct((S, D), F32), jax.ShapeDtypeStruct((N_GROUPS, GROUP, GROUP), F32),
                   jax.ShapeDtypeStruct((8, D), F32), jax.ShapeDtypeStruct((8, D), F32),
                   jax.ShapeDtypeStruct((8, D), F32)],
        scratch_shapes=[pltpu.VMEM((POOL_TS + MAX_WIN, D), F32), pltpu.VMEM((POOL_TS, D), F32)],
        args=(dy, x, diff, g, w, b, scale), compiler_params=_params("arbitrary"),
    )


FFN_TS = min(S, 1024)
FFN_TF = 256


def _ffn_fwd(x, g, wg_t, wu_t, wd, name, target=None, ex=None):
    width = wd.shape[0]
    ni, nj = S // FFN_TS, width // FFN_TF
    with_loss = target is not None
    extra = [target] if with_loss else []

    def body(*refs):
        x_ref, g_ref, wg_ref, wu_ref, wd_ref = refs[:5]
        if with_loss:
            t_ref, out_ref, loss_ref, h_ref, gg_ref, uu_ref, hs, acc = refs[5:]
        else:
            out_ref, h_ref, gg_ref, uu_ref, hs, acc = refs[5:]
        i, j = pl.program_id(0), pl.program_id(1)

        @pl.when(j == 0)
        def _():
            xv = x_ref[...]
            hb = (xv * _rstd(xv) * g_ref[...]).astype(BF16)
            hs[...] = hb
            h_ref[...] = hb
            acc[...] = jnp.zeros(acc.shape, F32)

        hb = hs[...]
        gg = _nt(hb, wg_ref[...])
        uu = _nt(hb, wu_ref[...])
        gg_ref[...] = gg
        uu_ref[...] = uu
        a = (gg * jax.nn.sigmoid(gg) * uu).astype(BF16)
        acc[...] += _nn(a, wd_ref[...])

        @pl.when(j == nj - 1)
        def _():
            yv = x_ref[...] + acc[...]
            if with_loss:
                err = yv - t_ref[...]
                out_ref[...] = err * (1.0 / D)
                part = jnp.sum(err * err) * (0.5 / D)

                @pl.when(i == 0)
                def _():
                    loss_ref[...] = jnp.zeros(loss_ref.shape, F32)

                loss_ref[...] += jnp.broadcast_to(part, loss_ref.shape)
            else:
                out_ref[...] = yv

    xt = pl.BlockSpec((FFN_TS, D), lambda i, j: (i, 0))
    row = pl.BlockSpec((1, D), lambda i, j: (0, 0))
    wt = pl.BlockSpec((FFN_TF, D), lambda i, j: (j, 0))
    gt = pl.BlockSpec((FFN_TS, FFN_TF), lambda i, j: (i, j))
    in_specs = [xt, row, wt, wt, wt] + [xt] * len(extra)
    out_specs = [xt] + ([pl.BlockSpec((8, 128), lambda i, j: (0, 0))] if with_loss else []) + [xt, gt, gt]
    out_shape = ([jax.ShapeDtypeStruct((S, D), F32)] + ([jax.ShapeDtypeStruct((8, 128), F32)] if with_loss else [])
                 + [jax.ShapeDtypeStruct((S, D), BF16), jax.ShapeDtypeStruct((S, width), F32),
                    jax.ShapeDtypeStruct((S, width), F32)])
    args = (x, g, wg_t, wu_t, wd, *extra)
    return _call_hosting(
        body, ex, _grid_phases((ni, nj), 0.8), name=name, grid=(ni, nj),
        in_specs=in_specs, out_specs=out_specs, out_shape=out_shape,
        scratch_shapes=[pltpu.VMEM((FFN_TS, D), BF16), pltpu.VMEM((FFN_TS, D), F32)], args=args,
        compiler_params=_params("arbitrary", "arbitrary"),
    )


def _ffn_bwd_weights(dout, h, gg, uu, wd, name, ex=None):
    width = wd.shape[0]
    nj = width // FFN_TF

    def body(do_ref, h_ref, gg_ref, uu_ref, wd_ref, dg_ref, du_ref, dwg_ref, dwu_ref, dwd_ref, dob, kept):
        j = pl.program_id(0)

        @pl.when(j == 0)
        def _():
            dob[...] = do_ref[...].astype(BF16)
            kept[1] = jnp.zeros(kept.shape[1:], BF16)

        prev, cur = (j + 1) % 2, j % 2
        hb, dov = h_ref[...], dob[...]
        dwg_ref[...] = _tn(kept[prev, 0], hb).astype(BF16)
        dwu_ref[...] = _tn(kept[prev, 1], hb).astype(BF16)
        dwd_ref[...] = _tn(kept[prev, 2], dov).astype(BF16)
        gv, uv = gg_ref[...], uu_ref[...]
        da = _nt(dov, wd_ref[...])
        sg = jax.nn.sigmoid(gv)
        sl = gv * sg
        dub = (da * sl).astype(BF16)
        dgb = (da * uv * (sg * (1.0 + gv * (1.0 - sg)))).astype(BF16)
        dg_ref[...] = dgb
        du_ref[...] = dub
        kept[cur, 0] = dgb
        kept[cur, 1] = dub
        kept[cur, 2] = (sl * uv).astype(BF16)

    once = pl.Buffered(1)
    whole = lambda: pl.BlockSpec((S, D), lambda j, _: (0, 0), pipeline_mode=once)
    this = lambda j: jnp.minimum(j, nj - 1)
    last = lambda j: jnp.maximum(j - 1, 0)
    wt_in = pl.BlockSpec((FFN_TF, D), lambda j, _: (this(j), 0))
    wt_out = pl.BlockSpec((FFN_TF, D), lambda j, _: (last(j), 0))
    gt = pl.BlockSpec((S, FFN_TF), lambda j, _: (0, this(j)))
    return _call_hosting(
        body, ex, _grid_phases((nj + 1, 1)), name=name, grid=(nj + 1, 1),
        in_specs=[whole(), whole(), gt, gt, wt_in], out_specs=[gt, gt, wt_out, wt_out, wt_out],
        out_shape=[jax.ShapeDtypeStruct((S, width), BF16)] * 2 + [jax.ShapeDtypeStruct((width, D), BF16)] * 3,
        scratch_shapes=[pltpu.VMEM((S, D), BF16), pltpu.VMEM((2, 3, S, FFN_TF), BF16)], args=(dout, h, gg, uu, wd),
        compiler_params=_params("arbitrary", "arbitrary"),
    )


BWD_TS = 512


def _ffn_bwd_input(dres, x, g, grads, weights, name, ex=None):
    nt = S // BWD_TS
    n = len(grads)

    def body(*refs):
        dres_ref, x_ref, g_ref = refs[:3]
        grad_refs, w_refs = refs[3:3 + n], refs[3 + n:3 + 2 * n]
        dx_ref, dgam_ref = refs[3 + 2 * n:]
        dh = _nn(grad_refs[0][...], w_refs[0][...])
        for k in range(1, n):
            dh = dh + _nn(grad_refs[k][...], w_refs[k][...])
        xv = x_ref[...]
        r = _rstd(xv)
        dx_ref[...] = dres_ref[...] + _rms_bwd(dh, xv, r, g_ref[...])

        @pl.when(pl.program_id(0) == 0)
        def _():
            dgam_ref[...] = jnp.zeros(dgam_ref.shape, F32)

        dgam_ref[...] += _colsum8(dh * xv * r)

    tile = pl.BlockSpec((BWD_TS, D), lambda i: (i, 0))
    ftiles = [pl.BlockSpec((BWD_TS, a.shape[1]), lambda i: (i, 0)) for a in grads]
    wspecs = [pl.BlockSpec(w.shape, lambda i: (0, 0), pipeline_mode=pl.Buffered(1)) for w in weights]
    return _call_hosting(
        body, ex, _grid_phases((nt,)), name=name, grid=(nt,),
        in_specs=[tile, tile, pl.BlockSpec((1, D), lambda i: (0, 0))] + ftiles + wspecs,
        out_specs=[tile, pl.BlockSpec((8, D), lambda i: (0, 0))],
        out_shape=[jax.ShapeDtypeStruct((S, D), F32), jax.ShapeDtypeStruct((8, D), F32)],
        scratch_shapes=[], args=(dres, x, g, *grads, *weights), compiler_params=_params("arbitrary"),
    )


def _mm(a, b, mode, out_dtype, name, add=None, skip_rows=0):
    if mode == "nn":
        (m, kd), n = (a.shape[0] - skip_rows, a.shape[1]), b.shape[1]
    elif mode == "nt":
        (m, kd), n = (a.shape[0] - skip_rows, a.shape[1]), b.shape[0]
    else:
        (kd, m), n = a.shape, b.shape[1]
    tm, tn, tk = min(m, 1024), min(n, 1024), min(kd, 1024)
    if skip_rows and mode == "tn":
        tk = min(tk, skip_rows)
    elif skip_rows:
        tm = min(tm, skip_rows)
    skip_a = skip_rows // tm if mode != "tn" else 0
    skip_b = skip_rows // tk if mode == "tn" else 0
    assert skip_rows == skip_a * tm + skip_b * tk
    nk = kd // tk
    dot = {"nn": _nn, "nt": _nt, "tn": _tn}[mode]

    def body(*refs):
        if add is None:
            a_ref, b_ref, o_ref, acc = refs
        else:
            a_ref, b_ref, add_ref, o_ref, acc = refs
        k = pl.program_id(2)

        @pl.when(k == 0)
        def _():
            acc[...] = jnp.zeros(acc.shape, F32)

        acc[...] += dot(a_ref[...].astype(BF16), b_ref[...].astype(BF16))

        @pl.when(k == nk - 1)
        def _():
            res = acc[...]
            if add is not None:
                res = res + add_ref[...]
            o_ref[...] = res.astype(out_dtype)

    if mode == "tn":
        a_spec = pl.BlockSpec((tk, tm), lambda i, j, k: (k, i))
        b_spec = pl.BlockSpec((tk, tn), lambda i, j, k: (k + skip_b, j))
    else:
        a_spec = pl.BlockSpec((tm, tk), lambda i, j, k: (i + skip_a, k))
        b_spec = (pl.BlockSpec((tk, tn), lambda i, j, k: (k, j)) if mode == "nn"
                  else pl.BlockSpec((tn, tk), lambda i, j, k: (j, k)))
    o_spec = pl.BlockSpec((tm, tn), lambda i, j, k: (i, j))
    in_specs = [a_spec, b_spec] + ([o_spec] if add is not None else [])
    args = (a, b) + ((add,) if add is not None else ())
    return _call(
        body, name=name, grid=(m // tm, n // tn, nk), in_specs=in_specs, out_specs=o_spec,
        out_shape=jax.ShapeDtypeStruct((m, n), out_dtype), scratch_shapes=[pltpu.VMEM((tm, tn), F32)],
        compiler_params=_params("parallel", "parallel", "arbitrary"),
    )(*args)


PROJ_TS = 256


def _kvq_proj(x, g_kv, g_mix, wk, wv, wq, gk, gq, ex=None):
    lead = PADK // PROJ_TS

    def body(x_ref, gkv_ref, gmix_ref, wk_ref, wv_ref, wq_ref, gk_ref, gq_ref,
             hkv_ref, h1_ref, kpre_ref, qpre_ref, k_ref, v_ref, q_ref):
        i = pl.program_id(0)

        @pl.when(i < lead)
        def _():
            k_ref[...] = jnp.zeros(k_ref.shape, BF16)
            v_ref[...] = jnp.zeros(v_ref.shape, BF16)

        @pl.when(i >= lead)
        def _():
            xv = x_ref[...]
            xr = xv * _rstd(xv)
            hkv = (xr * gkv_ref[...]).astype(BF16)
            h1 = (xr * gmix_ref[...]).astype(BF16)
            hkv_ref[...] = hkv
            h1_ref[...] = h1
            kpre = _nn(hkv, wk_ref[...])
            qpre = _nn(h1, wq_ref[...])
            kpre_ref[...] = kpre
            qpre_ref[...] = qpre
            v_ref[...] = _nn(hkv, wv_ref[...]).astype(BF16)
            rk = lax.rsqrt(_seg_sum(kpre * kpre) * (1.0 / HEAD) + EPS)
            k_ref[...] = (kpre * rk * gk_ref[...]).astype(BF16)
            rq = lax.rsqrt(_seg_sum(qpre * qpre) * (1.0 / HEAD) + EPS)
            q_ref[...] = (qpre * rq * gq_ref[...]).astype(BF16)

    tile = pl.BlockSpec((PROJ_TS, D), lambda i: (jnp.maximum(i - lead, 0), 0))
    padded = pl.BlockSpec((PROJ_TS, D), lambda i: (i, 0))
    row = pl.BlockSpec((1, D), lambda i: (0, 0))
    wspec = pl.BlockSpec((D, D), lambda i: (0, 0))
    bf = jax.ShapeDtypeStruct((S, D), BF16)
    ff = jax.ShapeDtypeStruct((S, D), F32)
    bp = jax.ShapeDtypeStruct((PADK + S, D), BF16)
    return _call_hosting(
        body, ex, _grid_phases((lead + S // PROJ_TS,), 0.85), name="kvq_proj", grid=(lead + S // PROJ_TS,),
        in_specs=[tile, row, row, wspec, wspec, wspec, row, row],
        out_specs=[tile, tile, tile, tile, padded, padded, tile], out_shape=[bf, bf, ff, ff, bp, bp, bf],
        scratch_shapes=[], args=(x, g_kv, g_mix, wk, wv, wq, gk, gq), compiler_params=_params("arbitrary"),
    )


def _qkv_input_bwd(dq, qpre, gq, dkp, kpre, gk, dvp, wq, wk, wv, dres, x, g_mix, g_kv, ex=None):
    nt = S // PROJ_TS
    lead = PADK // PROJ_TS

    def head_bwd(dov, pv, hgv):
        r = lax.rsqrt(_seg_sum(pv * pv) * (1.0 / HEAD) + EPS)
        gd = dov * hgv
        dpre = r * gd - pv * (r * r * r) * (_seg_sum(gd * pv) * (1.0 / HEAD))
        return dpre.astype(BF16), _colsum8(dov * pv * r)

    def fold_heads(full):
        fold = full[:, 0:128]
        for blk in range(1, D // 128):
            fold = fold + full[:, blk * 128:(blk + 1) * 128]
        return fold + pltpu.roll(fold, HEAD, axis=1)

    def body(dq_ref, qpre_ref, gq_ref, dk_ref, kpre_ref, gk_ref, dv_ref, wq_ref, wk_ref, wv_ref, dres_ref, x_ref,
             gmix_ref, gkv_ref, dx_ref, dqpre_ref, dkpre_ref, dgmix_ref, dgkv_ref, dgq_ref, dgk_ref, accq, acck):
        i = pl.program_id(0)

        @pl.when(i == 0)
        def _():
            accq[...] = jnp.zeros(accq.shape, F32)
            acck[...] = jnp.zeros(acck.shape, F32)
            dgmix_ref[...] = jnp.zeros(dgmix_ref.shape, F32)
            dgkv_ref[...] = jnp.zeros(dgkv_ref.shape, F32)

        dqb, cq = head_bwd(dq_ref[...], qpre_ref[...], gq_ref[...])
        dkb, ck = head_bwd(dk_ref[...], kpre_ref[...], gk_ref[...])
        dqpre_ref[...] = dqb
        dkpre_ref[...] = dkb
        accq[...] += cq
        acck[...] += ck
        dh1 = _nt(dqb, wq_ref[...])
        dhkv = _nt(dkb, wk_ref[...]) + _nt(dv_ref[...].astype(BF16), wv_ref[...])
        xv = x_ref[...]
        r = _rstd(xv)
        dx_ref[...] = dres_ref[...] + _rms_bwd(dh1, xv, r, gmix_ref[...]) + _rms_bwd(dhkv, xv, r, gkv_ref[...])
        dgmix_ref[...] += _colsum8(dh1 * xv * r)
        dgkv_ref[...] += _colsum8(dhkv * xv * r)

        @pl.when(i == nt - 1)
        def _():
            dgq_ref[...] = fold_heads(accq[...])
            dgk_ref[...] = fold_heads(acck[...])

    tile = pl.BlockSpec((PROJ_TS, D), lambda i: (i, 0))
    behind = pl.BlockSpec((PROJ_TS, D), lambda i: (i + lead, 0))
    row = pl.BlockSpec((1, D), lambda i: (0, 0))
    wspec = lambda: pl.BlockSpec((D, D), lambda i: (0, 0), pipeline_mode=pl.Buffered(1))
    acc = pl.BlockSpec((8, D), lambda i: (0, 0))
    small = pl.BlockSpec((8, 128), lambda i: (0, 0))
    bf = jax.ShapeDtypeStruct((S, D), BF16)
    return _call_hosting(
        body, ex, _grid_phases((nt,)), name="qkv_input_bwd", grid=(nt,),
        in_specs=[tile, tile, row, behind, tile, row, behind, wspec(), wspec(), wspec(), tile, tile, row, row],
        out_specs=[tile, tile, tile, acc, acc, small, small],
        out_shape=[jax.ShapeDtypeStruct((S, D), F32), bf, bf, jax.ShapeDtypeStruct((8, D), F32),
                   jax.ShapeDtypeStruct((8, D), F32), jax.ShapeDtypeStruct((8, 128), F32),
                   jax.ShapeDtypeStruct((8, 128), F32)],
        scratch_shapes=[pltpu.VMEM((8, D), F32), pltpu.VMEM((8, D), F32)],
        args=(dq, qpre, gq, dkp, kpre, gk, dvp, wq, wk, wv, dres, x, g_mix, g_kv),
        compiler_params=_params("arbitrary"),
    )


def _toeplitz_from_table(table):
    far = jnp.broadcast_to(table[:, N_REL - 1:], (N_HEADS, PADK - MAX_REL + 1))
    near = table[:, N_REL - 2::-1]
    past = jnp.broadcast_to(table[:, 0:1], (N_HEADS, MAX_REL))
    wrap = jnp.broadcast_to(table[:, N_REL - 1:], (N_HEADS, TOEP - (PADK + 2 * MAX_REL + 1)))
    return jnp.concatenate([far, near, past, wrap], axis=1).reshape(N_HEADS, 1, TOEP)


def _table_grad_from_toeplitz(dtp, seg):
    lo = PADK - MAX_REL + 1
    near = dtp[:, lo + N_REL - 3:lo - 1:-1]
    return jnp.concatenate([seg[:, 1:2], near, seg[:, 0:1]], axis=1)


def _bias_band(tp):
    def body(tp_ref, out_ref):
        bv = pltpu.roll(jnp.broadcast_to(tp_ref[0], (QB, TOEP)), 0, axis=1, stride=1, stride_axis=0)
        out_ref[0] = jnp.where(_band_mask(), bv[:, 0:KB], NEG_INF)

    return _call(
        body, name="bias_band", grid=(N_HEADS,),
        in_specs=[pl.BlockSpec((1, 1, TOEP), lambda h: (h, 0, 0))],
        out_specs=pl.BlockSpec((1, QB, KB), lambda h: (h, 0, 0)),
        out_shape=jax.ShapeDtypeStruct((N_HEADS, QB, KB), F32),
        compiler_params=_params("parallel"),
    )(tp)


def _bias_grad(dband, after):
    lo, hi = PADK - MAX_REL + 1, PADK + MAX_REL

    def body(db_ref, _, dtp_ref, seg_ref):
        bv = jnp.concatenate([db_ref[0], jnp.zeros((QB, TOEP - KB), F32)], axis=1)
        row = lax.broadcasted_iota(jnp.int32, (QB, TOEP), 0)
        k = 1
        while k < QB:
            bv = jnp.where((row & k) != 0, pltpu.roll(bv, TOEP - k, axis=1), bv)
            k *= 2
        col = jnp.sum(bv, axis=0, keepdims=True)
        dtp_ref[0] = col
        u = lax.broadcasted_iota(jnp.int32, (1, TOEP), 1)
        far = jnp.sum(jnp.where((u < lo) | (u > hi + MAX_REL), col, 0.0))
        past = jnp.sum(jnp.where((u >= hi) & (u <= hi + MAX_REL), col, 0.0))
        lane = lax.broadcasted_iota(jnp.int32, (1, 128), 1)
        seg_ref[0] = jnp.where(lane == 0, far, jnp.where(lane == 1, past, 0.0))

    return _call(
        body, name="bias_grad", grid=(N_HEADS,),
        in_specs=[pl.BlockSpec((1, QB, KB), lambda h: (h, 0, 0)), ANY],
        out_specs=[pl.BlockSpec((1, 1, TOEP), lambda h: (h, 0, 0)), pl.BlockSpec((1, 1, 128), lambda h: (h, 0, 0))],
        out_shape=[jax.ShapeDtypeStruct((N_HEADS, 1, TOEP), F32), jax.ShapeDtypeStruct((N_HEADS, 1, 128), F32)],
        compiler_params=_params("parallel"),
    )(dband, after)


N_QB = S // QB
HEADS_PER_STEP = 4
ATT_LANES = HEADS_PER_STEP * HEAD
N_HG = D // ATT_LANES


def _band_mask():
    qc = lax.broadcasted_iota(jnp.int32, (QB, KB), 0) // CHUNK
    kc = lax.broadcasted_iota(jnp.int32, (QB, KB), 1) // CHUNK
    return (kc >= qc) & (kc <= qc + LEFT)


def _half_scale(hh, scale):
    lane = lax.broadcasted_iota(jnp.int32, (1, 128), 1)
    return jnp.where((lane < HEAD) == (hh == 0), scale, 0.0).astype(BF16)


def _probs(qh, kb, bias, first_key):
    sc = _nt(qh, kb) + bias
    if first_key is not None:
        sc = jnp.where(lax.broadcasted_iota(jnp.int32, (QB, KB), 1) >= first_key, sc, NEG_INF)
    e = jnp.exp(sc - jnp.max(sc, axis=-1, keepdims=True))
    return e * (1.0 / jnp.sum(e, axis=-1, keepdims=True))


def _by_padding(cb, compute):
    @pl.when(cb < PADK // QB)
    def _():
        compute(PADK - cb * QB)

    @pl.when(cb >= PADK // QB)
    def _():
        compute(None)


def _attn_fwd(q, kp, vp, bias, ex=None):
    def body(q_ref, k_ref, v_ref, b_ref, o_ref):
        cb = pl.program_id(1)
        band = pl.ds(pl.multiple_of(cb * QB, QB), KB)
        low = lax.broadcasted_iota(jnp.int32, (QB, 128), 1) < HEAD

        def compute(first_key):
            for pair in range(HEADS_PER_STEP // 2):
                lanes = pl.ds(pair * 128, 128)
                kb, vb, qv = k_ref[band, lanes], v_ref[band, lanes], q_ref[:, lanes]
                outs = []
                for hh in range(2):
                    pb = _probs(qv * _half_scale(hh, ATTN_SCALE), kb, b_ref[2 * pair + hh], first_key).astype(BF16)
                    outs.append(_nn(pb, vb))
                o_ref[:, lanes] = jnp.where(low, outs[0], outs[1]).astype(BF16)

        _by_padding(cb, compute)

    qspec = pl.BlockSpec((QB, ATT_LANES), lambda hg, cb: (cb, hg))
    kspec = pl.BlockSpec((PADK + S, ATT_LANES), lambda hg, cb: (0, hg))
    return _call_hosting(
        body, ex, _grid_phases((N_HG, N_QB), 0.85), name="attn_fwd", grid=(N_HG, N_QB),
        in_specs=[qspec, kspec, kspec, pl.BlockSpec((HEADS_PER_STEP, QB, KB), lambda hg, cb: (hg, 0, 0))],
        out_specs=[qspec], out_shape=[jax.ShapeDtypeStruct((S, D), BF16)], scratch_shapes=[],
        args=(q, kp, vp, bias), compiler_params=_params("arbitrary", "arbitrary"),
    )


def _attn_bwd(q, kp, vp, bias, do, ex=None):
    def body(q_ref, k_ref, v_ref, b_ref, do_ref, dq_ref, dk_ref, dv_ref, db_ref):
        cb = pl.program_id(1)

        @pl.when(cb == 0)
        def _():
            dk_ref[...] = jnp.zeros(dk_ref.shape, F32)
            dv_ref[...] = jnp.zeros(dv_ref.shape, F32)
            db_ref[...] = jnp.zeros(db_ref.shape, F32)

        band = pl.ds(pl.multiple_of(cb * QB, QB), KB)
        low = lax.broadcasted_iota(jnp.int32, (QB, 128), 1) < HEAD

        def compute(first_key):
            for pair in range(HEADS_PER_STEP // 2):
                lanes = pl.ds(pair * 128, 128)
                kb, vb = k_ref[band, lanes], v_ref[band, lanes]
                qv, dov = q_ref[:, lanes], do_ref[:, lanes]
                dq = jnp.zeros((QB, 128), F32)
                dkb = jnp.zeros((KB, 128), F32)
                dvb = jnp.zeros((KB, 128), F32)
                for hh in range(2):
                    sel = low if hh == 0 else jnp.logical_not(low)
                    doh = dov * _half_scale(hh, 1.0)
                    p = _probs(qv * _half_scale(hh, ATTN_SCALE), kb, b_ref[2 * pair + hh], first_key)
                    dp = _nt(doh, vb)
                    dvb = dvb + _tn(p.astype(BF16), doh)
                    ds = p * (dp - jnp.sum(dp * p, axis=-1, keepdims=True))
                    db_ref[2 * pair + hh] += ds
                    dsb = (ds * ATTN_SCALE).astype(BF16)
                    dq = dq + jnp.where(sel, _nn(dsb, kb), 0.0)
                    dkb = dkb + _tn(dsb, qv * _half_scale(hh, 1.0))
                dq_ref[:, lanes] = dq
                dk_ref[band, lanes] += dkb
                dv_ref[band, lanes] += dvb

        _by_padding(cb, compute)

    qspec = pl.BlockSpec((QB, ATT_LANES), lambda hg, cb: (cb, hg))
    kspec = pl.BlockSpec((PADK + S, ATT_LANES), lambda hg, cb: (0, hg))
    bspec = pl.BlockSpec((HEADS_PER_STEP, QB, KB), lambda hg, cb: (hg, 0, 0))
    kf = jax.ShapeDtypeStruct((PADK + S, D), F32)
    return _call_hosting(
        body, ex, _grid_phases((N_HG, N_QB)), name="attn_bwd", grid=(N_HG, N_QB),
        in_specs=[qspec, kspec, kspec, bspec, qspec],
        out_specs=[qspec, kspec, kspec, bspec],
        out_shape=[jax.ShapeDtypeStruct((S, D), F32), kf, kf, jax.ShapeDtypeStruct((N_HEADS, QB, KB), F32)],
        scratch_shapes=[], args=(q, kp, vp, bias, do), compiler_params=_params("arbitrary", "arbitrary"),
    )


def _adam_math(w, g, m, v):
    m = ADAM_B1 * m + (1.0 - ADAM_B1) * g
    v = ADAM_B2 * v + (1.0 - ADAM_B2) * (g * g)
    m_hat = m / (1.0 - ADAM_B1 ** ADAM_STEP)
    v_hat = v / (1.0 - ADAM_B2 ** ADAM_STEP)
    delta = -ADAM_LR * (m_hat / (jnp.sqrt(v_hat) + ADAM_EPS) + ADAM_WD * w)
    return delta, m, v


def _row_tile(r):
    for cand in (256, 176, 128, 64, 32, 16, 8):
        if r % cand == 0:
            return cand
    return r


def _adam_layer(w, parts, m, v, layer, name, prev=None):
    nl, r, c = w.shape
    count = parts.shape[0]
    tr = _row_tile(r)

    def body(w_ref, g_ref, m_ref, v_ref, *rest):
        go_ref, d_ref, nm_ref, nv_ref = rest[-4:]
        gv = g_ref[0].astype(F32)
        for q in range(1, count):
            gv = gv + g_ref[q].astype(F32)
        delta, nm, nv = _adam_math(w_ref[0], gv, m_ref[0], v_ref[0])
        go_ref[0] = gv
        d_ref[0] = delta
        nm_ref[0] = nm
        nv_ref[0] = nv

    lspec = pl.BlockSpec((1, tr, c), lambda i: (layer, i, 0))
    sd = jax.ShapeDtypeStruct((nl, r, c), F32)
    extra = list(prev) if prev is not None else []
    return _call(
        body, name=name, grid=(r // tr,),
        in_specs=[lspec, pl.BlockSpec((count, tr, c), lambda i: (0, i, 0)), lspec, lspec] + [ANY] * len(extra),
        out_specs=[lspec] * 4, out_shape=[sd] * 4,
        input_output_aliases={4 + t: t for t in range(len(extra))},
        compiler_params=_params("parallel"),
    )(w, parts, m, v, *extra)


def _adam(w, g, m, v, name):
    r, c = w.shape
    tr = _row_tile(r)
    count = g.shape[0] if g.ndim == 3 else 0

    def body(w_ref, g_ref, m_ref, v_ref, go_ref, d_ref, nm_ref, nv_ref):
        if count:
            gv = g_ref[0].astype(F32)
            for q in range(1, count):
                gv = gv + g_ref[q].astype(F32)
        else:
            gv = g_ref[...]
        delta, nm, nv = _adam_math(w_ref[...], gv, m_ref[...], v_ref[...])
        go_ref[...] = gv
        d_ref[...] = delta
        nm_ref[...] = nm
        nv_ref[...] = nv

    spec = pl.BlockSpec((tr, c), lambda i: (i, 0))
    gspec = pl.BlockSpec((count, tr, c), lambda i: (0, i, 0)) if count else spec
    sd = jax.ShapeDtypeStruct((r, c), F32)
    return _call(
        body, name=name, grid=(r // tr,), in_specs=[spec, gspec, spec, spec], out_specs=[spec] * 4,
        out_shape=[sd] * 4, compiler_params=_params("parallel"),
    )(w, g, m, v)


def _pad16(rows):
    return jnp.pad(rows, ((0, 16 - rows.shape[0]), (0, 0)))


def kernel(x, norm_mix_g, norm_ffn_g, pool_w, pool_b, pool_scale, kv_norm_g, w_k, w_v, k_norm_g, w_q, q_norm_g, rel_bias, w_o, w_gate, w_up, w_down, loss_target, m_norm_mix_g, m_norm_ffn_g, m_pool_w, m_pool_b, m_pool_scale, m_kv_norm_g, m_w_k, m_w_v, m_k_norm_g, m_w_q, m_q_norm_g, m_rel_bias, m_w_o, m_w_gate, m_w_up, m_w_down, v_norm_mix_g, v_norm_ffn_g, v_pool_w, v_pool_b, v_pool_scale, v_kv_norm_g, v_w_k, v_w_v, v_k_norm_g, v_w_q, v_q_norm_g, v_rel_bias, v_w_o, v_w_gate, v_w_up, v_w_down):
    assert x.shape == (1, S, D) and w_gate.shape == (2, D, F_SHARD) and w_k.shape == (D_SHARD, D)
    xin, target = x[0], loss_target[0]

    ffn_shards = [[w_gate[layer].T.astype(BF16), w_up[layer].T.astype(BF16), w_down[layer].astype(BF16)]
                  for layer in range(2)]
    att_shards = [w_k.astype(BF16), w_v.astype(BF16), w_q[0].astype(BF16), w_o[0].astype(BF16)]
    pool_shard = pool_w[0].astype(BF16).reshape(N_GROUPS * POOL_SHARD, GROUP)
    small = jnp.concatenate([pool_b[0].reshape(1, N_GROUPS * POOL_SHARD), pool_scale], axis=1)

    full0 = _run_exchange(_relay_gather_exchange(ffn_shards[0] + [pool_shard, _pad16(small)]), "gather_layer0")
    ffn_w0 = [a.reshape(F, D) for a in full0[:3]]
    pw_f = full0[3].reshape(N_DEV, N_GROUPS, POOL_SHARD, GROUP).transpose(1, 0, 2, 3).reshape(N_GROUPS, GROUP, GROUP)
    small_f = full0[4][:, 0, :]
    pb_f = small_f[:, :N_GROUPS * POOL_SHARD].reshape(N_DEV, N_GROUPS, POOL_SHARD).transpose(1, 0, 2).reshape(1, D)
    ps_f = small_f[:, N_GROUPS * POOL_SHARD:].reshape(1, D)

    g_mix0, g_mix1 = norm_mix_g[0:1], norm_mix_g[1:2]
    g_ffn0, g_ffn1 = norm_ffn_g[0:1], norm_ffn_g[1:2]
    g_kv = kv_norm_g.reshape(1, D)
    gk_t = jnp.tile(k_norm_g.reshape(1, HEAD), (1, N_HEADS))
    gq_t = jnp.tile(q_norm_g.reshape(1, HEAD), (1, N_HEADS))

    x1, diff = _pool_fwd(xin, g_mix0, pw_f, pb_f, ps_f)
    (x2, hf0, gg0, uu0), full_att = _ffn_fwd(x1, g_ffn0, *ffn_w0, name="ffn_fwd", ex=_relay_gather_exchange(att_shards))
    wk_f, wv_f, wq_f, wo_f = [a.reshape(D, D) for a in full_att]
    (hkv, h1, kpre, qpre, kp, vp, qq), full1_gate = _kvq_proj(x2, g_kv, g_mix1, wk_f, wv_f, wq_f, gk_t, gq_t,
                                                                ex=_relay_gather_exchange(ffn_shards[1][:1]))
    bias = _bias_band(_toeplitz_from_table(rel_bias[0]))
    (att,), full1_rest = _attn_fwd(qq, kp, vp, bias, ex=_relay_gather_exchange(ffn_shards[1][1:]))
    ffn_w1 = [a.reshape(F, D) for a in full1_gate + full1_rest]
    x3 = _mm(att, wo_f, "nn", F32, "attn_out", add=x2)
    (dx4, loss_rows, hf1, gg1, uu1), _ = _ffn_fwd(x3, g_ffn1, *ffn_w1, name="ffn_fwd_loss", target=target)

    def blocks(dw):
        return dw.reshape(N_DEV, dw.shape[0] // N_DEV, dw.shape[1])

    (dgg1, duu1, dwg1, dwu1, dwd1), _ = _ffn_bwd_weights(dx4, hf1, gg1, uu1, ffn_w1[2], name="ffn_bwd1")
    (dx3, dg_ffn1), _ = _ffn_bwd_input(dx4, x3, g_ffn1, [dgg1, duu1], ffn_w1[:2], "ffn_dx1")
    datt = _mm(dx3, wo_f, "nt", BF16, "d_attn")
    dwo = _mm(att, dx3, "tn", BF16, "d_wo")
    parts1 = [blocks(dw) for dw in (dwg1, dwu1, dwd1, dwo)]
    (dq, dkp, dvp, dband), stage1 = _attn_bwd(qq, kp, vp, bias, datt, ex=_pair_exchange(parts1))
    both1 = _pair_add(parts1, stage1, "pair_add_ffn1", copies=2)
    flight1, chip1, land1, _ = _chip_exchange_start(both1[:4], both1[4:], dq, "scatter_ffn1_start")
    (dx2, dqpre, dkpre, dg_mix1, dg_kv, dgq, dgk), _ = _qkv_input_bwd(
        dq, qpre, gq_t, dkp, kpre, gk_t, dvp, wq_f, wk_f, wv_f, dx3, x2, g_mix1, g_kv)
    dwq = _mm(h1, dqpre, "tn", BF16, "d_wq")
    dwk = _mm(hkv, dkpre, "tn", BF16, "d_wk")
    dwv = _mm(hkv, dvp, "tn", BF16, "d_wv", skip_rows=PADK)
    parts_att = [blocks(dw) for dw in (dwk, dwv, dwq)]
    (dgg0, duu0, *dw0), stage_att = _ffn_bwd_weights(
        dx2, hf0, gg0, uu0, ffn_w0[2], name="ffn_bwd0", ex=_pair_exchange(parts_att))
    both_att = _pair_add(parts_att, stage_att, "pair_add_att", copies=2)
    flight_att, chip_att, land_att, _ = _chip_exchange_start(both_att[:3], both_att[3:], dx2, "scatter_att_start")
    parts0 = [blocks(dw) for dw in dw0]
    (dx1, dg_ffn0), stage0 = _ffn_bwd_input(dx2, x1, g_ffn0, [dgg0, duu0], ffn_w0[:2], "ffn_dx0",
                                            ex=_pair_exchange(parts0))
    both = _pair_add(parts0, stage0, "pair_add_ffn0", copies=2)
    flight, chip0, land0, token = _chip_exchange_start(both[:3], both[3:], dx1, "scatter_ffn0_start")
    (grad_x, dpw, db_rows, ds_rows, dg_mix0), _ = _pool_bwd(dx1, xin, diff, g_mix0 + token[0:1, 0:1], pw_f, pb_f, ps_f)
    dtp, seg = _bias_grad(dband, dpw)

    weights = dict(norm_mix_g=norm_mix_g, norm_ffn_g=norm_ffn_g, pool_w=pool_w, pool_b=pool_b,
                   pool_scale=pool_scale, kv_norm_g=kv_norm_g, w_k=w_k, w_v=w_v, k_norm_g=k_norm_g, w_q=w_q,
                   q_norm_g=q_norm_g, rel_bias=rel_bias, w_o=w_o, w_gate=w_gate, w_up=w_up, w_down=w_down)
    mom1 = dict(norm_mix_g=m_norm_mix_g, norm_ffn_g=m_norm_ffn_g, pool_w=m_pool_w, pool_b=m_pool_b,
                pool_scale=m_pool_scale, kv_norm_g=m_kv_norm_g, w_k=m_w_k, w_v=m_w_v, k_norm_g=m_k_norm_g,
                w_q=m_w_q, q_norm_g=m_q_norm_g, rel_bias=m_rel_bias, w_o=m_w_o, w_gate=m_w_gate, w_up=m_w_up,
                w_down=m_w_down)
    mom2 = dict(norm_mix_g=v_norm_mix_g, norm_ffn_g=v_norm_ffn_g, pool_w=v_pool_w, pool_b=v_pool_b,
                pool_scale=v_pool_scale, kv_norm_g=v_kv_norm_g, w_k=v_w_k, w_v=v_w_v, k_norm_g=v_k_norm_g,
                w_q=v_w_q, q_norm_g=v_q_norm_g, rel_bias=v_rel_bias, w_o=v_w_o, w_gate=v_w_gate, w_up=v_w_up,
                w_down=v_w_down)
    names = list(weights)
    grads, deltas, new_m, new_v = {}, {}, {}, {}

    def adam_flat(nm, parts):
        shape = weights[nm].shape
        flat = lambda a: a.reshape(-1, shape[-1])
        done = _adam(flat(weights[nm]), parts, flat(mom1[nm]), flat(mom2[nm]), "adam_" + nm)
        grads[nm], deltas[nm], new_m[nm], new_v[nm] = [a.reshape(shape) for a in done]

    ffn_names = ("w_gate", "w_up", "w_down")
    def ffn_view(nm, a):
        return a if nm == "w_down" else a.transpose(0, 2, 1)

    dpw_blocks = dpw.reshape(N_GROUPS, N_DEV, POOL_SHARD, GROUP).transpose(1, 0, 2, 3)
    dpw_blocks = dpw_blocks.reshape(N_DEV, N_GROUPS * POOL_SHARD, GROUP).astype(BF16)
    misc = jnp.concatenate([dgk[0:1, 0:HEAD], dgq[0:1, 0:HEAD], loss_rows[0:1, 0:1],
                            seg[:, 0, 0].reshape(1, N_HEADS), seg[:, 0, 1].reshape(1, N_HEADS)], axis=1)
    misc = jnp.pad(misc, ((0, 0), (0, D - misc.shape[1])))
    vec_rows = jnp.concatenate([dg_mix0[0:1], dg_mix1[0:1], dg_ffn0[0:1], dg_ffn1[0:1], dg_kv[0:1],
                                db_rows[0:1], ds_rows[0:1], misc], axis=0)
    pack = jnp.concatenate([vec_rows, dtp.reshape(N_HEADS, TOEP)], axis=0)
    tot, recv_pool = _small_exchanges(pack, dpw_blocks, "small_exchanges", after=dtp)

    recv1 = _chip_exchange_wait(flight1, chip1, land1, tot, "scatter_ffn1_wait")
    layer1 = {nm: _adam_layer(ffn_view(nm, weights[nm]), g, ffn_view(nm, mom1[nm]), ffn_view(nm, mom2[nm]), 1,
                              "adam1_" + nm)
              for nm, g in zip(ffn_names, recv1)}
    adam_flat("w_o", recv1[3])
    recv_att = _chip_exchange_wait(flight_att, chip_att, land_att, deltas["w_o"], "scatter_att_wait")
    for nm, parts in zip(("w_k", "w_v", "w_q"), recv_att):
        adam_flat(nm, parts)

    loss = tot[7, 2 * HEAD]
    seg_tot = jnp.stack([tot[7, 2 * HEAD + 1:2 * HEAD + 1 + N_HEADS],
                         tot[7, 2 * HEAD + 1 + N_HEADS:2 * HEAD + 1 + 2 * N_HEADS]], axis=1)
    me = 4 * lax.axis_index("x") + 2 * lax.axis_index("y") + lax.axis_index("c")
    g_pool_b = lax.dynamic_slice_in_dim(tot[5].reshape(N_GROUPS, GROUP), me * POOL_SHARD, POOL_SHARD, axis=1)
    grads.update(
        norm_mix_g=tot[0:2], norm_ffn_g=tot[2:4], kv_norm_g=tot[4], k_norm_g=tot[7, 0:HEAD],
        q_norm_g=tot[7, HEAD:2 * HEAD].reshape(1, HEAD),
        rel_bias=_table_grad_from_toeplitz(tot[8:8 + N_HEADS], seg_tot).reshape(1, N_HEADS, N_REL),
        pool_b=g_pool_b.reshape(1, N_GROUPS, POOL_SHARD),
        pool_scale=lax.dynamic_slice_in_dim(tot[6:7], me * D_SHARD, D_SHARD, axis=1))
    adam_flat("pool_w", recv_pool)
    small_names = [nm for nm in names if nm not in ffn_names + ("w_k", "w_v", "w_q", "w_o", "pool_w")]

    def pack_small(tree):
        cols = []
        for nm in small_names:
            flat = tree[nm].reshape(-1)
            cols.append(jnp.pad(flat, (0, -flat.shape[0] % 1024)))
        return jnp.concatenate(cols).reshape(-1, 128)

    _, dl, m1, m2 = _adam(pack_small(weights), pack_small(grads), pack_small(mom1), pack_small(mom2), "adam_small")

    def unpack_small(packed, out):
        flat, off = packed.reshape(-1), 0
        for nm in small_names:
            size = weights[nm].size
            out[nm] = flat[off:off + size].reshape(weights[nm].shape)
            off += size + (-size % 1024)

    unpack_small(dl, deltas)
    unpack_small(m1, new_m)
    unpack_small(m2, new_v)

    recv0 = _chip_exchange_wait(flight, chip0, land0, deltas["w_q"], "scatter_ffn0_wait")
    for nm, g in zip(ffn_names, recv0):
        done = _adam_layer(ffn_view(nm, weights[nm]), g, ffn_view(nm, mom1[nm]), ffn_view(nm, mom2[nm]), 0,
                           "adam0_" + nm, prev=layer1[nm])
        grads[nm], deltas[nm], new_m[nm], new_v[nm] = [ffn_view(nm, a) for a in done]

    return (loss, grad_x[None], *[grads[nm] for nm in names], *[deltas[nm] for nm in names],
            *[new_m[nm] for nm in names], *[new_v[nm] for nm in names])
```

```python
import functools

import jax
import jax.numpy as jnp
from jax import lax
from jax.experimental import pallas as pl
from jax.experimental.pallas import tpu as pltpu

F32 = jnp.float32
BF16 = jnp.bfloat16
MESH_ID = pl.DeviceIdType.MESH

N_DEV = 8
S = 2048
D = 1024
F = 2816
F_SHARD = F // N_DEV
D_SHARD = D // N_DEV
N_GROUPS = 4
GROUP = D // N_GROUPS
POOL_SHARD = GROUP // N_DEV
MAX_WIN = 16
HEAD = 64
N_HEADS = D // HEAD
CHUNK = 64
LEFT = 8
QB = 4 * CHUNK
KB = QB + LEFT * CHUNK
PADK = LEFT * CHUNK
TOEP = 1024
N_REL = 257
MAX_REL = 128
EPS = 1e-6
NEG_INF = -1e30
ATTN_SCALE = HEAD ** -0.5

ADAM_LR = 0.001
ADAM_B1 = 0.9
ADAM_B2 = 0.999
ADAM_EPS = 1e-08
ADAM_WD = 0.01
ADAM_STEP = 10

VMEM_LIMIT = 52 * 1024 * 1024

ANY = pl.BlockSpec(memory_space=pl.ANY)
VMEM = pl.BlockSpec(memory_space=pltpu.VMEM)


def _call(body, **kw):
    return pl.pallas_call(body, **kw)


def _params(*sem):
    return pltpu.CompilerParams(dimension_semantics=sem, vmem_limit_bytes=VMEM_LIMIT)


def _dot(a, b, dims):
    return lax.dot_general(a, b, (dims, ((), ())), preferred_element_type=F32)


def _nn(a, b):
    return _dot(a, b, ((1,), (0,)))


def _nt(a, b):
    return _dot(a, b, ((1,), (1,)))


def _tn(a, b):
    return _dot(a, b, ((0,), (0,)))


def _rstd(x):
    return lax.rsqrt(jnp.mean(x * x, axis=-1, keepdims=True) + EPS)


def _rms_bwd(dh, x, r, g):
    gd = dh * g
    return r * gd - x * (r * r * r) * jnp.mean(gd * x, axis=-1, keepdims=True)


def _colsum8(v):
    return jnp.broadcast_to(jnp.sum(v, axis=0, keepdims=True), (8, v.shape[1]))


def _seg_sum(v):
    r = lax.broadcasted_iota(jnp.int32, (128, 128), 0) // HEAD
    c = lax.broadcasted_iota(jnp.int32, (128, 128), 1) // HEAD
    ones = jnp.where(r == c, 1.0, 0.0).astype(BF16)
    out = []
    for blk in range(v.shape[1] // 128):
        part = v[:, blk * 128:(blk + 1) * 128]
        hi = part.astype(BF16)
        rest = part - hi.astype(F32)
        mid = rest.astype(BF16)
        lo = (rest - mid.astype(F32)).astype(BF16)
        out.append(_nn(hi, ones) + _nn(mid, ones) + _nn(lo, ones))
    return jnp.concatenate(out, axis=1)


def _place():
    return lax.axis_index("x"), lax.axis_index("y"), lax.axis_index("c")


class _Exchange:
    def __init__(self, ins, out_shape, sems, start, mid, finish, early=None):
        self.ins, self.out_shape, self.sems = list(ins), list(out_shape), list(sems)
        self.start, self.mid, self.finish = start, mid, finish
        self.early = early if early is not None else (lambda ins, outs, sems: None)


def _relay_gather_exchange(shards):
    n = len(shards)

    def tools(ins, outs, sems):
        send_sems, recv_sems, local_sems = sems
        x, y, c = _place()
        me, sib = (x, y, c), (x, y, 1 - c)
        xn, yn, dg = (1 - x, y, c), (x, 1 - y, c), (1 - x, 1 - y, c)
        other = lambda dev: (dev[0], dev[1], 1 - c)

        def piece(k, ref, half):
            rows = shards[k].shape[0] // 2
            return ref if half is None else ref.at[pl.ds(half * rows, rows)]

        def held(k, dev, half):
            return piece(k, outs[k].at[4 * dev[0] + 2 * dev[1] + dev[2]], half)

        def copy(k, s, dev, half, to, own=False):
            return pltpu.make_async_remote_copy(
                src_ref=piece(k, ins[k], half) if own else held(k, dev, half), dst_ref=held(k, dev, half),
                send_sem=send_sems.at[k, s], recv_sem=recv_sems.at[k, s], device_id=to, device_id_type=MESH_ID)

        def mine(k):
            return pltpu.make_async_copy(ins[k], held(k, me, None), local_sems.at[k])

        def sent(k):
            return [copy(k, 0, me, None, sib, own=True), copy(k, 1, me, 0, xn, own=True), copy(k, 2, me, 1, yn, own=True),
                    copy(k, 3, me, 0, yn, own=True), copy(k, 4, me, 1, xn, own=True)]

        relays = [(1, xn, 0, [(5, yn), (7, sib)]), (2, yn, 1, [(6, xn), (9, sib)]), (3, yn, 0, [(10, sib)]),
                  (4, xn, 1, [(8, sib)]), (6, dg, 1, [(11, sib)]), (5, dg, 0, [(12, sib)])]
        from_sibling = [(0, other(me), None), (7, other(xn), 0), (8, other(xn), 1), (9, other(yn), 1),
                        (10, other(yn), 0), (11, other(dg), 1), (12, other(dg), 0)]
        return mine, sent, copy, relays, from_sibling, me

    def start(ins, outs, sems):
        mine, sent, _, _, _, _ = tools(ins, outs, sems)
        for k in range(n):
            mine(k).start()
        for k in range(n):
            for cp in sent(k):
                cp.start()

    def pass_on(ins, outs, sems, which):
        _, _, copy, relays, _, me = tools(ins, outs, sems)
        for k in range(n):
            for s, dev, half, onward in relays[which]:
                copy(k, s, dev, half, me).wait_recv()
                for s2, to in onward:
                    copy(k, s2, dev, half, to).start()

    def early(ins, outs, sems):
        pass_on(ins, outs, sems, slice(0, 2))

    def mid(ins, outs, sems):
        pass_on(ins, outs, sems, slice(2, None))

    def finish(ins, outs, sems):
        mine, sent, copy, relays, from_sibling, me = tools(ins, outs, sems)
        for k in range(n):
            for s, dev, half in from_sibling:
                copy(k, s, dev, half, me).wait_recv()
        for k in range(n):
            for cp in sent(k):
                cp.wait_send()
            for _, dev, half, onward in relays:
                for s2, to in onward:
                    copy(k, s2, dev, half, to).wait_send()
            mine(k).wait()

    return _Exchange(
        shards, [jax.ShapeDtypeStruct((N_DEV,) + a.shape, a.dtype) for a in shards],
        [pltpu.SemaphoreType.DMA((n, 13)), pltpu.SemaphoreType.DMA((n, 13)), pltpu.SemaphoreType.DMA((n,))],
        start, mid, finish, early)


def _peer(x, y, c, m):
    px = 1 - x if m & 4 else x
    py = 1 - y if m & 2 else y
    pc = 1 - c if m & 1 else c
    return px, py, pc


N_CHIPS = N_DEV // 2


def _pair_exchange(parts):
    n = len(parts)

    def copies(ins, outs, sems):
        send_sems, recv_sems = sems
        x, y, c = _place()
        return [pltpu.make_async_remote_copy(
            src_ref=ins[k].at[2 * q + 1 - c], dst_ref=outs[k].at[q], send_sem=send_sems.at[k, q],
            recv_sem=recv_sems.at[k, q], device_id=(x, y, 1 - c), device_id_type=MESH_ID)
            for k in range(n) for q in range(N_CHIPS)]

    def start(ins, outs, sems):
        for cp in copies(ins, outs, sems):
            cp.start()

    def mid(ins, outs, sems):
        pass

    def finish(ins, outs, sems):
        for cp in copies(ins, outs, sems):
            cp.wait_recv()
        for cp in copies(ins, outs, sems):
            cp.wait_send()

    return _Exchange(
        parts, [jax.ShapeDtypeStruct((N_CHIPS,) + a.shape[1:], a.dtype) for a in parts],
        [pltpu.SemaphoreType.DMA((n, N_CHIPS)), pltpu.SemaphoreType.DMA((n, N_CHIPS))], start, mid, finish)


def _pair_add(parts, stage, name):
    n = len(parts)
    place = jnp.stack([lax.axis_index("c"), 2 * lax.axis_index("x") + lax.axis_index("y")]).astype(jnp.int32)

    def body(place_ref, *refs):
        for k in range(n):
            mine, theirs = refs[k], refs[n + k]
            total = (mine[0, 0].astype(F32) + theirs[0].astype(F32)).astype(BF16)
            refs[2 * n + k][0] = total

            @pl.when(pl.program_id(1) == place_ref[1])
            def _():
                refs[3 * n + k][0] = total

    in_specs, out_specs, land_specs = [], [], []
    for a in parts:
        _, r, cdim = a.shape
        in_specs.append(pl.BlockSpec((1, 1, r // 2, cdim), lambda i, q, place_ref: (q, place_ref[0], i, 0)))
    for a in parts:
        _, r, cdim = a.shape
        in_specs.append(pl.BlockSpec((1, r // 2, cdim), lambda i, q, place_ref: (q, i, 0)))
        out_specs.append(pl.BlockSpec((1, r // 2, cdim), lambda i, q, place_ref: (q, i, 0)))
        land_specs.append(pl.BlockSpec((1, r // 2, cdim), lambda i, q, place_ref: (place_ref[1], i, 0)))
    return list(_call(
        body, name=name,
        grid_spec=pltpu.PrefetchScalarGridSpec(num_scalar_prefetch=1, grid=(2, N_CHIPS), in_specs=in_specs,
                                               out_specs=out_specs + land_specs),
        out_shape=[jax.ShapeDtypeStruct(s.shape, BF16) for s in stage] * 2,
        compiler_params=_params("arbitrary", "arbitrary"),
    )(place, *[a.reshape((N_CHIPS, 2) + a.shape[1:]) for a in parts], *stage))


HBM = pl.BlockSpec(memory_space=pltpu.HBM)
SEMAPHORES = pl.BlockSpec(memory_space=pltpu.SEMAPHORE)


def _chip_copies(srcs, lands, send_sems, recv_sems, mine_is_dst):
    x, y, c = _place()
    me = 2 * x + y
    out = []
    for m in range(1, N_CHIPS):
        px, py, _ = _peer(x, y, c, 2 * m)
        peer = 2 * px + py
        for k in range(len(srcs)):
            pair = k * (N_CHIPS - 1) + m - 1
            out.append(pltpu.make_async_remote_copy(
                src_ref=srcs[k].at[peer], dst_ref=lands[k].at[me if mine_is_dst else peer],
                send_sem=send_sems[pair], recv_sem=recv_sems[pair],
                device_id=(px, py, c), device_id_type=MESH_ID))
    return out


def _chip_exchange_start(chip_parts, lands, after, name):
    n = len(chip_parts)
    pairs = n * (N_CHIPS - 1)

    def body(*refs):
        srcs, zones = refs[:n], refs[n:2 * n]
        sems = refs[2 * n + 1:2 * n + 1 + 2 * pairs]
        token = refs[-1]
        for cp in _chip_copies(srcs, zones, sems[:pairs], sems[pairs:], True):
            cp.start()
        token[...] = jnp.zeros(token.shape, F32)

    thru = [pltpu.HBM(a.shape, a.dtype) for a in chip_parts + lands]
    hbm = [pltpu.with_memory_space_constraint(a, pltpu.HBM) for a in chip_parts + lands]
    res = _call(
        body, name=name,
        out_shape=[pltpu.SemaphoreType.DMA(())] * (2 * pairs) + thru + [jax.ShapeDtypeStruct((8, 128), F32)],
        in_specs=[HBM] * (2 * n) + [ANY], out_specs=[SEMAPHORES] * (2 * pairs) + [HBM] * (2 * n) + [VMEM],
        input_output_aliases={i: 2 * pairs + i for i in range(2 * n)},
        compiler_params=pltpu.CompilerParams(has_side_effects=pltpu.SideEffectType.DATAFLOW_SIDE_EFFECTING),
    )(*hbm, after)
    sems, rest = list(res[:2 * pairs]), res[2 * pairs:]
    return sems, list(rest[:n]), list(rest[n:2 * n]), rest[-1]


def _chip_exchange_wait(sems, chip_parts, lands, after, name):
    n = len(chip_parts)
    pairs = n * (N_CHIPS - 1)

    def body(*refs):
        srcs, zones = refs[:n], refs[n:2 * n]
        sem_refs = refs[2 * n:2 * n + 2 * pairs]
        for cp in _chip_copies(srcs, zones, sem_refs[:pairs], sem_refs[pairs:], False):
            cp.wait_send()
            cp.wait_recv()

    thru = [pltpu.HBM(a.shape, a.dtype) for a in chip_parts + lands]
    res = _call(
        body, name=name, out_shape=thru,
        in_specs=[HBM] * (2 * n) + [SEMAPHORES] * (2 * pairs) + [ANY], out_specs=[HBM] * (2 * n),
        input_output_aliases={i: i for i in range(2 * n)},
        compiler_params=pltpu.CompilerParams(has_side_effects=pltpu.SideEffectType.DATAFLOW_SIDE_EFFECTING),
    )(*chip_parts, *lands, *sems, after)
    return list(res[n:])


def _run_exchange(ex, name, after=None):
    n_in, n_out = len(ex.ins), len(ex.out_shape)
    order = [] if after is None else [after]

    def body(*refs):
        ins, outs = refs[:n_in], refs[n_in + len(order):n_in + len(order) + n_out]
        sems = refs[n_in + len(order) + n_out:]
        ex.start(ins, outs, sems)
        ex.early(ins, outs, sems)
        ex.mid(ins, outs, sems)
        ex.finish(ins, outs, sems)

    return list(_call(body, name=name, out_shape=ex.out_shape, in_specs=[ANY] * (n_in + len(order)),
                      out_specs=[ANY] * n_out, scratch_shapes=ex.sems)(*ex.ins, *order))


def _call_hosting(body, ex, phases, *, in_specs, out_specs, out_shape, scratch_shapes, args, **kw):
    n_in, n_out, n_scr = len(in_specs), len(out_specs), len(scratch_shapes)
    if ex is None:
        res = _call(body, in_specs=in_specs, out_specs=out_specs, out_shape=out_shape,
                    scratch_shapes=scratch_shapes, **kw)(*args)
        return list(res), []
    n_xin, n_xout = len(ex.ins), len(ex.out_shape)

    def hosting(*refs):
        a, b = n_in, n_in + n_xin
        c, d = b + n_out, b + n_out + n_xout
        ins, xins, outs, xouts = refs[:a], refs[a:b], refs[b:c], refs[c:d]
        scr, sems = refs[d:d + n_scr], refs[d + n_scr:]
        first, early, mid, last = phases()

        @pl.when(first)
        def _():
            ex.start(xins, xouts, sems)

        body(*ins, *outs, *scr)

        @pl.when(early)
        def _():
            ex.early(xins, xouts, sems)

        @pl.when(mid)
        def _():
            ex.mid(xins, xouts, sems)

        @pl.when(last)
        def _():
            ex.finish(xins, xouts, sems)

    res = _call(hosting, in_specs=list(in_specs) + [ANY] * n_xin, out_specs=list(out_specs) + [ANY] * n_xout,
                out_shape=list(out_shape) + ex.out_shape, scratch_shapes=list(scratch_shapes) + ex.sems,
                **kw)(*args, *ex.ins)
    return list(res[:n_out]), list(res[n_out:])


def _grid_phases(dims, mid_fraction=0.8, early_fraction=0.4):
    total = 1
    for d in dims:
        total *= d
    mid = min(max(int(total * mid_fraction), 1), total - 1)
    early = min(int(total * early_fraction), mid)

    def phases():
        step = pl.program_id(0)
        for axis in range(1, len(dims)):
            step = step * dims[axis] + pl.program_id(axis)
        return step == 0, step == early, step == mid, step == total - 1
    return phases


def _small_exchanges(pack, blocks, name, after):
    rows = pack.shape[0]

    def body(in_ref, blk_ref, _, out_ref, got_ref, recv, send_sems, recv_sems, blk_send, blk_recv, blk_local):
        x, y, c = _place()
        me = 4 * x + 2 * y + c
        peers = [_peer(x, y, c, m) for m in range(1, N_DEV)]
        index = [4 * px + 2 * py + pc for px, py, pc in peers]

        def block_copy(m, landing):
            return pltpu.make_async_remote_copy(
                src_ref=blk_ref.at[index[m]], dst_ref=got_ref.at[index[m] if landing else me],
                send_sem=blk_send.at[m], recv_sem=blk_recv.at[m], device_id=peers[m], device_id_type=MESH_ID)

        def pack_copy(m, landing):
            return pltpu.make_async_remote_copy(
                src_ref=in_ref, dst_ref=recv.at[index[m] if landing else me], send_sem=send_sems.at[m],
                recv_sem=recv_sems.at[m], device_id=peers[m], device_id_type=MESH_ID)

        own = pltpu.make_async_copy(blk_ref.at[me], got_ref.at[me], blk_local)
        own.start()
        recv[me] = in_ref[...]
        for m in range(N_DEV - 1):
            pack_copy(m, False).start()
            block_copy(m, False).start()
        for m in range(N_DEV - 1):
            pack_copy(m, True).wait_recv()
        acc = recv[0]
        for d in range(1, N_DEV):
            acc = acc + recv[d]
        out_ref[...] = acc
        for m in range(N_DEV - 1):
            block_copy(m, True).wait_recv()
        for m in range(N_DEV - 1):
            pack_copy(m, False).wait_send()
            block_copy(m, False).wait_send()
        own.wait()

    return _call(
        body, name=name,
        out_shape=[jax.ShapeDtypeStruct(pack.shape, F32), jax.ShapeDtypeStruct(blocks.shape, blocks.dtype)],
        in_specs=[VMEM, ANY, ANY], out_specs=[VMEM, ANY],
        scratch_shapes=[pltpu.VMEM((N_DEV, rows, pack.shape[1]), F32)] + [pltpu.SemaphoreType.DMA((7,))] * 4
        + [pltpu.SemaphoreType.DMA],
        compiler_params=pltpu.CompilerParams(vmem_limit_bytes=VMEM_LIMIT),
    )(pack, blocks, after)


POOL_TS = 256


def _pool_counts(first_row, rows, win):
    t = first_row + lax.broadcasted_iota(jnp.int32, (rows, 1), 0)
    return jnp.minimum(t + 1, win).astype(F32)


def _pool_fwd(x, g, w, b, scale):
    nt = S // POOL_TS

    def body(x_ref, g_ref, w_ref, b_ref, s_ref, out_ref, diff_ref, ext):
        i = pl.program_id(0)

        @pl.when(i == 0)
        def _():
            ext[0:MAX_WIN, :] = jnp.zeros((MAX_WIN, D), F32)

        @pl.when(i > 0)
        def _():
            ext[0:MAX_WIN, :] = ext[POOL_TS:POOL_TS + MAX_WIN, :]

        xv = x_ref[...]
        h = xv * _rstd(xv) * g_ref[...]
        ext[MAX_WIN:, :] = h
        for gi in range(N_GROUPS):
            win = 2 << gi
            cols = slice(gi * GROUP, (gi + 1) * GROUP)
            sm = ext[:, cols]
            k = 1
            while k < win:
                sm = sm + pltpu.roll(sm, k, axis=0)
                k *= 2
            pooled = sm[MAX_WIN:, :] / _pool_counts(i * POOL_TS, POOL_TS, win)
            diff = (pooled - h[:, cols]).astype(BF16)
            yv = (_nn(diff, w_ref[gi]) + b_ref[:, cols]) * s_ref[:, cols]
            out_ref[:, cols] = xv[:, cols] + yv
            diff_ref[:, cols] = diff

    row = pl.BlockSpec((1, D), lambda i: (0, 0))
    tile = pl.BlockSpec((POOL_TS, D), lambda i: (i, 0))
    return _call(
        body, name="pool_fwd", grid=(nt,),
        in_specs=[tile, row, pl.BlockSpec((N_GROUPS, GROUP, GROUP), lambda i: (0, 0, 0)), row, row],
        out_specs=[tile, tile],
        out_shape=[jax.ShapeDtypeStruct((S, D), F32), jax.ShapeDtypeStruct((S, D), BF16)],
        scratch_shapes=[pltpu.VMEM((POOL_TS + MAX_WIN, D), F32)],
        compiler_params=_params("arbitrary"),
    )(x, g, w, b, scale)


def _pool_bwd(dy, x, diff, g, w, b, scale, ex=None):
    nt = S // POOL_TS

    def body(dy_ref, x_ref, diff_ref, g_ref, w_ref, b_ref, s_ref, gx_ref, dw_ref, db_ref, ds_ref, dg_ref, ext, dh):
        i = pl.program_id(0)
        first_row = (nt - 1 - i) * POOL_TS

        @pl.when(i == 0)
        def _():
            ext[POOL_TS:, :] = jnp.zeros((MAX_WIN, D), F32)
            dw_ref[...] = jnp.zeros(dw_ref.shape, F32)
            db_ref[...] = jnp.zeros(db_ref.shape, F32)
            ds_ref[...] = jnp.zeros(ds_ref.shape, F32)
            dg_ref[...] = jnp.zeros(dg_ref.shape, F32)

        @pl.when(i > 0)
        def _():
            ext[POOL_TS:, :] = ext[0:MAX_WIN, :]

        dyv = dy_ref[...]
        for gi in range(N_GROUPS):
            win = 2 << gi
            cols = slice(gi * GROUP, (gi + 1) * GROUP)
            dfb = diff_ref[:, cols]
            z = _nn(dfb, w_ref[gi]) + b_ref[:, cols]
            dyg = dyv[:, cols]
            ds_ref[:, cols] += _colsum8(dyg * z)
            dz = dyg * s_ref[:, cols]
            db_ref[:, cols] += _colsum8(dz)
            dzb = dz.astype(BF16)
            dw_ref[gi] += _tn(dfb, dzb)
            ddiff = _nt(dzb, w_ref[gi])
            ext[0:POOL_TS, cols] = ddiff / _pool_counts(first_row, POOL_TS, win)
            sm = ext[:, cols]
            k = 1
            while k < win:
                sm = sm + pltpu.roll(sm, POOL_TS + MAX_WIN - k, axis=0)
                k *= 2
            dh[:, cols] = sm[0:POOL_TS, :] - ddiff
        xv = x_ref[...]
        r = _rstd(xv)
        gv = g_ref[...]
        dhv = dh[...]
        dg_ref[...] += _colsum8(dhv * xv * r)
        gx_ref[...] = dyv + _rms_bwd(dhv, xv, r, gv)

    row = pl.BlockSpec((1, D), lambda i: (0, 0))
    tile = pl.BlockSpec((POOL_TS, D), lambda i: (nt - 1 - i, 0))
    acc = pl.BlockSpec((8, D), lambda i: (0, 0))
    wspec = pl.BlockSpec((N_GROUPS, GROUP, GROUP), lambda i: (0, 0, 0))
    return _call_hosting(
        body, ex, _grid_phases((nt,)), name="pool_bwd", grid=(nt,),
        in_specs=[tile, tile, tile, row, wspec, row, row],
        out_specs=[tile, wspec, acc, acc, acc],
        out_shape=[jax.ShapeDtypeStruct((S, D), F32), jax.ShapeDtypeStruct((N_GROUPS, GROUP, GROUP), F32),
                   jax.ShapeDtypeStruct((8, D), F32), jax.ShapeDtypeStruct((8, D), F32),
                   jax.ShapeDtypeStruct((8, D), F32)],
        scratch_shapes=[pltpu.VMEM((POOL_TS + MAX_WIN, D), F32), pltpu.VMEM((POOL_TS, D), F32)],
        args=(dy, x, diff, g, w, b, scale), compiler_params=_params("arbitrary"),
    )


FFN_TS = min(S, 1024)
FFN_TF = 256


def _ffn_fwd(x, g, wg_t, wu_t, wd, name, target=None, ex=None):
    width = wd.shape[0]
    ni, nj = S // FFN_TS, width // FFN_TF
    with_loss = target is not None
    extra = [target] if with_loss else []

    def body(*refs):
        x_ref, g_ref, wg_ref, wu_ref, wd_ref = refs[:5]
        if with_loss:
            t_ref, out_ref, loss_ref, h_ref, gg_ref, uu_ref, hs, acc = refs[5:]
        else:
            out_ref, h_ref, gg_ref, uu_ref, hs, acc = refs[5:]
        i, j = pl.program_id(0), pl.program_id(1)

        @pl.when(j == 0)
        def _():
            xv = x_ref[...]
            hb = (xv * _rstd(xv) * g_ref[...]).astype(BF16)
            hs[...] = hb
            h_ref[...] = hb
            acc[...] = jnp.zeros(acc.shape, F32)

        hb = hs[...]
        gg = _nt(hb, wg_ref[...])
        uu = _nt(hb, wu_ref[...])
        gg_ref[...] = gg
        uu_ref[...] = uu
        a = (gg * jax.nn.sigmoid(gg) * uu).astype(BF16)
        acc[...] += _nn(a, wd_ref[...])

        @pl.when(j == nj - 1)
        def _():
            yv = x_ref[...] + acc[...]
            if with_loss:
                err = yv - t_ref[...]
                out_ref[...] = err * (1.0 / D)
                part = jnp.sum(err * err) * (0.5 / D)

                @pl.when(i == 0)
                def _():
                    loss_ref[...] = jnp.zeros(loss_ref.shape, F32)

                loss_ref[...] += jnp.broadcast_to(part, loss_ref.shape)
            else:
                out_ref[...] = yv

    xt = pl.BlockSpec((FFN_TS, D), lambda i, j: (i, 0))
    row = pl.BlockSpec((1, D), lambda i, j: (0, 0))
    wt = pl.BlockSpec((FFN_TF, D), lambda i, j: (j, 0))
    gt = pl.BlockSpec((FFN_TS, FFN_TF), lambda i, j: (i, j))
    in_specs = [xt, row, wt, wt, wt] + [xt] * len(extra)
    out_specs = [xt] + ([pl.BlockSpec((8, 128), lambda i, j: (0, 0))] if with_loss else []) + [xt, gt, gt]
    out_shape = ([jax.ShapeDtypeStruct((S, D), F32)] + ([jax.ShapeDtypeStruct((8, 128), F32)] if with_loss else [])
                 + [jax.ShapeDtypeStruct((S, D), BF16), jax.ShapeDtypeStruct((S, width), F32),
                    jax.ShapeDtypeStruct((S, width), F32)])
    args = (x, g, wg_t, wu_t, wd, *extra)
    return _call_hosting(
        body, ex, _grid_phases((ni, nj), 0.8), name=name, grid=(ni, nj),
        in_specs=in_specs, out_specs=out_specs, out_shape=out_shape,
        scratch_shapes=[pltpu.VMEM((FFN_TS, D), BF16), pltpu.VMEM((FFN_TS, D), F32)], args=args,
        compiler_params=_params("arbitrary", "arbitrary"),
    )


def _ffn_bwd_weights(dout, h, gg, uu, wd, name, ex=None):
    width = wd.shape[0]
    nj = width // FFN_TF

    def body(do_ref, h_ref, gg_ref, uu_ref, wd_ref, dg_ref, du_ref, dwg_ref, dwu_ref, dwd_ref, dob, kept):
        j = pl.program_id(0)

        @pl.when(j == 0)
        def _():
            dob[...] = do_ref[...].astype(BF16)
            kept[1] = jnp.zeros(kept.shape[1:], BF16)

        prev, cur = (j + 1) % 2, j % 2
        hb, dov = h_ref[...], dob[...]
        dwg_ref[...] = _tn(kept[prev, 0], hb).astype(BF16)
        dwu_ref[...] = _tn(kept[prev, 1], hb).astype(BF16)
        dwd_ref[...] = _tn(kept[prev, 2], dov).astype(BF16)
        gv, uv = gg_ref[...], uu_ref[...]
        da = _nt(dov, wd_ref[...])
        sg = jax.nn.sigmoid(gv)
        sl = gv * sg
        dub = (da * sl).astype(BF16)
        dgb = (da * uv * (sg * (1.0 + gv * (1.0 - sg)))).astype(BF16)
        dg_ref[...] = dgb
        du_ref[...] = dub
        kept[cur, 0] = dgb
        kept[cur, 1] = dub
        kept[cur, 2] = (sl * uv).astype(BF16)

    once = pl.Buffered(1)
    whole = lambda: pl.BlockSpec((S, D), lambda j, _: (0, 0), pipeline_mode=once)
    this = lambda j: jnp.minimum(j, nj - 1)
    last = lambda j: jnp.maximum(j - 1, 0)
    wt_in = pl.BlockSpec((FFN_TF, D), lambda j, _: (this(j), 0))
    wt_out = pl.BlockSpec((FFN_TF, D), lambda j, _: (last(j), 0))
    gt = pl.BlockSpec((S, FFN_TF), lambda j, _: (0, this(j)))
    return _call_hosting(
        body, ex, _grid_phases((nj + 1, 1)), name=name, grid=(nj + 1, 1),
        in_specs=[whole(), whole(), gt, gt, wt_in], out_specs=[gt, gt, wt_out, wt_out, wt_out],
        out_shape=[jax.ShapeDtypeStruct((S, width), BF16)] * 2 + [jax.ShapeDtypeStruct((width, D), BF16)] * 3,
        scratch_shapes=[pltpu.VMEM((S, D), BF16), pltpu.VMEM((2, 3, S, FFN_TF), BF16)], args=(dout, h, gg, uu, wd),
        compiler_params=_params("arbitrary", "arbitrary"),
    )


BWD_TS = 512


def _ffn_bwd_input(dres, x, g, grads, weights, name, ex=None):
    nt = S // BWD_TS
    n = len(grads)

    def body(*refs):
        dres_ref, x_ref, g_ref = refs[:3]
        grad_refs, w_refs = refs[3:3 + n], refs[3 + n:3 + 2 * n]
        dx_ref, dgam_ref = refs[3 + 2 * n:]
        dh = _nn(grad_refs[0][...], w_refs[0][...])
        for k in range(1, n):
            dh = dh + _nn(grad_refs[k][...], w_refs[k][...])
        xv = x_ref[...]
        r = _rstd(xv)
        dx_ref[...] = dres_ref[...] + _rms_bwd(dh, xv, r, g_ref[...])

        @pl.when(pl.program_id(0) == 0)
        def _():
            dgam_ref[...] = jnp.zeros(dgam_ref.shape, F32)

        dgam_ref[...] += _colsum8(dh * xv * r)

    tile = pl.BlockSpec((BWD_TS, D), lambda i: (i, 0))
    ftiles = [pl.BlockSpec((BWD_TS, a.shape[1]), lambda i: (i, 0)) for a in grads]
    wspecs = [pl.BlockSpec(w.shape, lambda i: (0, 0), pipeline_mode=pl.Buffered(1)) for w in weights]
    return _call_hosting(
        body, ex, _grid_phases((nt,)), name=name, grid=(nt,),
        in_specs=[tile, tile, pl.BlockSpec((1, D), lambda i: (0, 0))] + ftiles + wspecs,
        out_specs=[tile, pl.BlockSpec((8, D), lambda i: (0, 0))],
        out_shape=[jax.ShapeDtypeStruct((S, D), F32), jax.ShapeDtypeStruct((8, D), F32)],
        scratch_shapes=[], args=(dres, x, g, *grads, *weights), compiler_params=_params("arbitrary"),
    )


def _mm(a, b, mode, out_dtype, name, add=None, skip_rows=0):
    if mode == "nn":
        (m, kd), n = (a.shape[0] - skip_rows, a.shape[1]), b.shape[1]
    elif mode == "nt":
        (m, kd), n = (a.shape[0] - skip_rows, a.shape[1]), b.shape[0]
    else:
        (kd, m), n = a.shape, b.shape[1]
    tm, tn, tk = min(m, 1024), min(n, 1024), min(kd, 1024)
    if skip_rows and mode == "tn":
        tk = min(tk, skip_rows)
    elif skip_rows:
        tm = min(tm, skip_rows)
    skip_a = skip_rows // tm if mode != "tn" else 0
    skip_b = skip_rows // tk if mode == "tn" else 0
    assert skip_rows == skip_a * tm + skip_b * tk
    nk = kd // tk
    dot = {"nn": _nn, "nt": _nt, "tn": _tn}[mode]

    def body(*refs):
        if add is None:
            a_ref, b_ref, o_ref, acc = refs
        else:
            a_ref, b_ref, add_ref, o_ref, acc = refs
        k = pl.program_id(2)

        @pl.when(k == 0)
        def _():
            acc[...] = jnp.zeros(acc.shape, F32)

        acc[...] += dot(a_ref[...].astype(BF16), b_ref[...].astype(BF16))

        @pl.when(k == nk - 1)
        def _():
            res = acc[...]
            if add is not None:
                res = res + add_ref[...]
            o_ref[...] = res.astype(out_dtype)

    if mode == "tn":
        a_spec = pl.BlockSpec((tk, tm), lambda i, j, k: (k, i))
        b_spec = pl.BlockSpec((tk, tn), lambda i, j, k: (k + skip_b, j))
    else:
        a_spec = pl.BlockSpec((tm, tk), lambda i, j, k: (i + skip_a, k))
        b_spec = (pl.BlockSpec((tk, tn), lambda i, j, k: (k, j)) if mode == "nn"
                  else pl.BlockSpec((tn, tk), lambda i, j, k: (j, k)))
    o_spec = pl.BlockSpec((tm, tn), lambda i, j, k: (i, j))
    in_specs = [a_spec, b_spec] + ([o_spec] if add is not None else [])
    args = (a, b) + ((add,) if add is not None else ())
    return _call(
        body, name=name, grid=(m // tm, n // tn, nk), in_specs=in_specs, out_specs=o_spec,
        out_shape=jax.ShapeDtypeStruct((m, n), out_dtype), scratch_shapes=[pltpu.VMEM((tm, tn), F32)],
        compiler_params=_params("parallel", "parallel", "arbitrary"),
    )(*args)


PROJ_TS = 256


def _kvq_proj(x, g_kv, g_mix, wk, wv, wq, gk, gq, ex=None):
    lead = PADK // PROJ_TS

    def body(x_ref, gkv_ref, gmix_ref, wk_ref, wv_ref, wq_ref, gk_ref, gq_ref,
             hkv_ref, h1_ref, kpre_ref, qpre_ref, k_ref, v_ref, q_ref):
        i = pl.program_id(0)

        @pl.when(i < lead)
        def _():
            k_ref[...] = jnp.zeros(k_ref.shape, BF16)
            v_ref[...] = jnp.zeros(v_ref.shape, BF16)

        @pl.when(i >= lead)
        def _():
            xv = x_ref[...]
            xr = xv * _rstd(xv)
            hkv = (xr * gkv_ref[...]).astype(BF16)
            h1 = (xr * gmix_ref[...]).astype(BF16)
            hkv_ref[...] = hkv
            h1_ref[...] = h1
            kpre = _nn(hkv, wk_ref[...])
            qpre = _nn(h1, wq_ref[...])
            kpre_ref[...] = kpre
            qpre_ref[...] = qpre
            v_ref[...] = _nn(hkv, wv_ref[...]).astype(BF16)
            rk = lax.rsqrt(_seg_sum(kpre * kpre) * (1.0 / HEAD) + EPS)
            k_ref[...] = (kpre * rk * gk_ref[...]).astype(BF16)
            rq = lax.rsqrt(_seg_sum(qpre * qpre) * (1.0 / HEAD) + EPS)
            q_ref[...] = (qpre * rq * gq_ref[...]).astype(BF16)

    tile = pl.BlockSpec((PROJ_TS, D), lambda i: (jnp.maximum(i - lead, 0), 0))
    padded = pl.BlockSpec((PROJ_TS, D), lambda i: (i, 0))
    row = pl.BlockSpec((1, D), lambda i: (0, 0))
    wspec = pl.BlockSpec((D, D), lambda i: (0, 0))
    bf = jax.ShapeDtypeStruct((S, D), BF16)
    ff = jax.ShapeDtypeStruct((S, D), F32)
    bp = jax.ShapeDtypeStruct((PADK + S, D), BF16)
    return _call_hosting(
        body, ex, _grid_phases((lead + S // PROJ_TS,), 0.85), name="kvq_proj", grid=(lead + S // PROJ_TS,),
        in_specs=[tile, row, row, wspec, wspec, wspec, row, row],
        out_specs=[tile, tile, tile, tile, padded, padded, tile], out_shape=[bf, bf, ff, ff, bp, bp, bf],
        scratch_shapes=[], args=(x, g_kv, g_mix, wk, wv, wq, gk, gq), compiler_params=_params("arbitrary"),
    )


def _qkv_input_bwd(dq, qpre, gq, dkp, kpre, gk, dvp, wq, wk, wv, dres, x, g_mix, g_kv, ex=None):
    nt = S // PROJ_TS
    lead = PADK // PROJ_TS

    def head_bwd(dov, pv, hgv):
        r = lax.rsqrt(_seg_sum(pv * pv) * (1.0 / HEAD) + EPS)
        gd = dov * hgv
        dpre = r * gd - pv * (r * r * r) * (_seg_sum(gd * pv) * (1.0 / HEAD))
        return dpre.astype(BF16), _colsum8(dov * pv * r)

    def fold_heads(full):
        fold = full[:, 0:128]
        for blk in range(1, D // 128):
            fold = fold + full[:, blk * 128:(blk + 1) * 128]
        return fold + pltpu.roll(fold, HEAD, axis=1)

    def body(dq_ref, qpre_ref, gq_ref, dk_ref, kpre_ref, gk_ref, dv_ref, wq_ref, wk_ref, wv_ref, dres_ref, x_ref,
             gmix_ref, gkv_ref, dx_ref, dqpre_ref, dkpre_ref, dgmix_ref, dgkv_ref, dgq_ref, dgk_ref, accq, acck):
        i = pl.program_id(0)

        @pl.when(i == 0)
        def _():
            accq[...] = jnp.zeros(accq.shape, F32)
            acck[...] = jnp.zeros(acck.shape, F32)
            dgmix_ref[...] = jnp.zeros(dgmix_ref.shape, F32)
            dgkv_ref[...] = jnp.zeros(dgkv_ref.shape, F32)

        dqb, cq = head_bwd(dq_ref[...], qpre_ref[...], gq_ref[...])
        dkb, ck = head_bwd(dk_ref[...], kpre_ref[...], gk_ref[...])
        dqpre_ref[...] = dqb
        dkpre_ref[...] = dkb
        accq[...] += cq
        acck[...] += ck
        dh1 = _nt(dqb, wq_ref[...])
        dhkv = _nt(dkb, wk_ref[...]) + _nt(dv_ref[...].astype(BF16), wv_ref[...])
        xv = x_ref[...]
        r = _rstd(xv)
        dx_ref[...] = dres_ref[...] + _rms_bwd(dh1, xv, r, gmix_ref[...]) + _rms_bwd(dhkv, xv, r, gkv_ref[...])
        dgmix_ref[...] += _colsum8(dh1 * xv * r)
        dgkv_ref[...] += _colsum8(dhkv * xv * r)

        @pl.when(i == nt - 1)
        def _():
            dgq_ref[...] = fold_heads(accq[...])
            dgk_ref[...] = fold_heads(acck[...])

    tile = pl.BlockSpec((PROJ_TS, D), lambda i: (i, 0))
    behind = pl.BlockSpec((PROJ_TS, D), lambda i: (i + lead, 0))
    row = pl.BlockSpec((1, D), lambda i: (0, 0))
    wspec = lambda: pl.BlockSpec((D, D), lambda i: (0, 0), pipeline_mode=pl.Buffered(1))
    acc = pl.BlockSpec((8, D), lambda i: (0, 0))
    small = pl.BlockSpec((8, 128), lambda i: (0, 0))
    bf = jax.ShapeDtypeStruct((S, D), BF16)
    return _call_hosting(
        body, ex, _grid_phases((nt,)), name="qkv_input_bwd", grid=(nt,),
        in_specs=[tile, tile, row, behind, tile, row, behind, wspec(), wspec(), wspec(), tile, tile, row, row],
        out_specs=[tile, tile, tile, acc, acc, small, small],
        out_shape=[jax.ShapeDtypeStruct((S, D), F32), bf, bf, jax.ShapeDtypeStruct((8, D), F32),
                   jax.ShapeDtypeStruct((8, D), F32), jax.ShapeDtypeStruct((8, 128), F32),
                   jax.ShapeDtypeStruct((8, 128), F32)],
        scratch_shapes=[pltpu.VMEM((8, D), F32), pltpu.VMEM((8, D), F32)],
        args=(dq, qpre, gq, dkp, kpre, gk, dvp, wq, wk, wv, dres, x, g_mix, g_kv),
        compiler_params=_params("arbitrary"),
    )


def _toeplitz_from_table(table):
    far = jnp.broadcast_to(table[:, N_REL - 1:], (N_HEADS, PADK - MAX_REL + 1))
    near = table[:, N_REL - 2::-1]
    past = jnp.broadcast_to(table[:, 0:1], (N_HEADS, MAX_REL))
    wrap = jnp.broadcast_to(table[:, N_REL - 1:], (N_HEADS, TOEP - (PADK + 2 * MAX_REL + 1)))
    return jnp.concatenate([far, near, past, wrap], axis=1).reshape(N_HEADS, 1, TOEP)


def _table_grad_from_toeplitz(dtp, seg):
    lo = PADK - MAX_REL + 1
    near = dtp[:, lo + N_REL - 3:lo - 1:-1]
    return jnp.concatenate([seg[:, 1:2], near, seg[:, 0:1]], axis=1)


def _bias_band(tp):
    def body(tp_ref, out_ref):
        bv = pltpu.roll(jnp.broadcast_to(tp_ref[0], (QB, TOEP)), 0, axis=1, stride=1, stride_axis=0)
        out_ref[0] = jnp.where(_band_mask(), bv[:, 0:KB], NEG_INF)

    return _call(
        body, name="bias_band", grid=(N_HEADS,),
        in_specs=[pl.BlockSpec((1, 1, TOEP), lambda h: (h, 0, 0))],
        out_specs=pl.BlockSpec((1, QB, KB), lambda h: (h, 0, 0)),
        out_shape=jax.ShapeDtypeStruct((N_HEADS, QB, KB), F32),
        compiler_params=_params("parallel"),
    )(tp)


def _bias_grad(dband, after):
    lo, hi = PADK - MAX_REL + 1, PADK + MAX_REL

    def body(db_ref, _, dtp_ref, seg_ref):
        bv = jnp.concatenate([db_ref[0], jnp.zeros((QB, TOEP - KB), F32)], axis=1)
        row = lax.broadcasted_iota(jnp.int32, (QB, TOEP), 0)
        k = 1
        while k < QB:
            bv = jnp.where((row & k) != 0, pltpu.roll(bv, TOEP - k, axis=1), bv)
            k *= 2
        col = jnp.sum(bv, axis=0, keepdims=True)
        dtp_ref[0] = col
        u = lax.broadcasted_iota(jnp.int32, (1, TOEP), 1)
        far = jnp.sum(jnp.where((u < lo) | (u > hi + MAX_REL), col, 0.0))
        past = jnp.sum(jnp.where((u >= hi) & (u <= hi + MAX_REL), col, 0.0))
        lane = lax.broadcasted_iota(jnp.int32, (1, 128), 1)
        seg_ref[0] = jnp.where(lane == 0, far, jnp.where(lane == 1, past, 0.0))

    return _call(
        body, name="bias_grad", grid=(N_HEADS,),
        in_specs=[pl.BlockSpec((1, QB, KB), lambda h: (h, 0, 0)), ANY],
        out_specs=[pl.BlockSpec((1, 1, TOEP), lambda h: (h, 0, 0)), pl.BlockSpec((1, 1, 128), lambda h: (h, 0, 0))],
        out_shape=[jax.ShapeDtypeStruct((N_HEADS, 1, TOEP), F32), jax.ShapeDtypeStruct((N_HEADS, 1, 128), F32)],
        compiler_params=_params("parallel"),
    )(dband, after)


N_QB = S // QB
HEADS_PER_STEP = 4
ATT_LANES = HEADS_PER_STEP * HEAD
N_HG = D // ATT_LANES


def _band_mask():
    qc = lax.broadcasted_iota(jnp.int32, (QB, KB), 0) // CHUNK
    kc = lax.broadcasted_iota(jnp.int32, (QB, KB), 1) // CHUNK
    return (kc >= qc) & (kc <= qc + LEFT)


def _half_scale(hh, scale):
    lane = lax.broadcasted_iota(jnp.int32, (1, 128), 1)
    return jnp.where((lane < HEAD) == (hh == 0), scale, 0.0).astype(BF16)


def _probs(qh, kb, bias, first_key):
    sc = _nt(qh, kb) + bias
    if first_key is not None:
        sc = jnp.where(lax.broadcasted_iota(jnp.int32, (QB, KB), 1) >= first_key, sc, NEG_INF)
    e = jnp.exp(sc - jnp.max(sc, axis=-1, keepdims=True))
    return e * (1.0 / jnp.sum(e, axis=-1, keepdims=True))


def _by_padding(cb, compute):
    @pl.when(cb < PADK // QB)
    def _():
        compute(PADK - cb * QB)

    @pl.when(cb >= PADK // QB)
    def _():
        compute(None)


def _attn_fwd(q, kp, vp, bias, ex=None):
    def body(q_ref, k_ref, v_ref, b_ref, o_ref):
        cb = pl.program_id(1)
        band = pl.ds(pl.multiple_of(cb * QB, QB), KB)
        low = lax.broadcasted_iota(jnp.int32, (QB, 128), 1) < HEAD

        def compute(first_key):
            for pair in range(HEADS_PER_STEP // 2):
                lanes = pl.ds(pair * 128, 128)
                kb, vb, qv = k_ref[band, lanes], v_ref[band, lanes], q_ref[:, lanes]
                outs = []
                for hh in range(2):
                    pb = _probs(qv * _half_scale(hh, ATTN_SCALE), kb, b_ref[2 * pair + hh], first_key).astype(BF16)
                    outs.append(_nn(pb, vb))
                o_ref[:, lanes] = jnp.where(low, outs[0], outs[1]).astype(BF16)

        _by_padding(cb, compute)

    qspec = pl.BlockSpec((QB, ATT_LANES), lambda hg, cb: (cb, hg))
    kspec = pl.BlockSpec((PADK + S, ATT_LANES), lambda hg, cb: (0, hg))
    return _call_hosting(
        body, ex, _grid_phases((N_HG, N_QB), 0.85), name="attn_fwd", grid=(N_HG, N_QB),
        in_specs=[qspec, kspec, kspec, pl.BlockSpec((HEADS_PER_STEP, QB, KB), lambda hg, cb: (hg, 0, 0))],
        out_specs=[qspec], out_shape=[jax.ShapeDtypeStruct((S, D), BF16)], scratch_shapes=[],
        args=(q, kp, vp, bias), compiler_params=_params("arbitrary", "arbitrary"),
    )


def _attn_bwd(q, kp, vp, bias, do, ex=None):
    def body(q_ref, k_ref, v_ref, b_ref, do_ref, dq_ref, dk_ref, dv_ref, db_ref):
        cb = pl.program_id(1)

        @pl.when(cb == 0)
        def _():
            dk_ref[...] = jnp.zeros(dk_ref.shape, F32)
            dv_ref[...] = jnp.zeros(dv_ref.shape, F32)
            db_ref[...] = jnp.zeros(db_ref.shape, F32)

        band = pl.ds(pl.multiple_of(cb * QB, QB), KB)
        low = lax.broadcasted_iota(jnp.int32, (QB, 128), 1) < HEAD

        def compute(first_key):
            for pair in range(HEADS_PER_STEP // 2):
                lanes = pl.ds(pair * 128, 128)
                kb, vb = k_ref[band, lanes], v_ref[band, lanes]
                qv, dov = q_ref[:, lanes], do_ref[:, lanes]
                dq = jnp.zeros((QB, 128), F32)
                dkb = jnp.zeros((KB, 128), F32)
                dvb = jnp.zeros((KB, 128), F32)
                for hh in range(2):
                    sel = low if hh == 0 else jnp.logical_not(low)
                    doh = dov * _half_scale(hh, 1.0)
                    p = _probs(qv * _half_scale(hh, ATTN_SCALE), kb, b_ref[2 * pair + hh], first_key)
                    dp = _nt(doh, vb)
                    dvb = dvb + _tn(p.astype(BF16), doh)
                    ds = p * (dp - jnp.sum(dp * p, axis=-1, keepdims=True))
                    db_ref[2 * pair + hh] += ds
                    dsb = (ds * ATTN_SCALE).astype(BF16)
                    dq = dq + jnp.where(sel, _nn(dsb, kb), 0.0)
                    dkb = dkb + _tn(dsb, qv * _half_scale(hh, 1.0))
                dq_ref[:, lanes] = dq
                dk_ref[band, lanes] += dkb
                dv_ref[band, lanes] += dvb

        _by_padding(cb, compute)

    qspec = pl.BlockSpec((QB, ATT_LANES), lambda hg, cb: (cb, hg))
    kspec = pl.BlockSpec((PADK + S, ATT_LANES), lambda hg, cb: (0, hg))
    bspec = pl.BlockSpec((HEADS_PER_STEP, QB, KB), lambda hg, cb: (hg, 0, 0))
    kf = jax.ShapeDtypeStruct((PADK + S, D), F32)
    return _call_hosting(
        body, ex, _grid_phases((N_HG, N_QB)), name="attn_bwd", grid=(N_HG, N_QB),
        in_specs=[qspec, kspec, kspec, bspec, qspec],
        out_specs=[qspec, kspec, kspec, bspec],
        out_shape=[jax.ShapeDtypeStruct((S, D), F32), kf, kf, jax.ShapeDtypeStruct((N_HEADS, QB, KB), F32)],
        scratch_shapes=[], args=(q, kp, vp, bias, do), compiler_params=_params("arbitrary", "arbitrary"),
    )


def _adam_math(w, g, m, v):
    m = ADAM_B1 * m + (1.0 - ADAM_B1) * g
    v = ADAM_B2 * v + (1.0 - ADAM_B2) * (g * g)
    m_hat = m / (1.0 - ADAM_B1 ** ADAM_STEP)
    v_hat = v / (1.0 - ADAM_B2 ** ADAM_STEP)
    delta = -ADAM_LR * (m_hat / (jnp.sqrt(v_hat) + ADAM_EPS) + ADAM_WD * w)
    return delta, m, v


def _row_tile(r):
    for cand in (256, 176, 128, 64, 32, 16, 8):
        if r % cand == 0:
            return cand
    return r


def _adam_layer(w, parts, m, v, layer, name, prev=None):
    nl, r, c = w.shape
    count = parts.shape[0]
    tr = _row_tile(r)

    def body(w_ref, g_ref, m_ref, v_ref, *rest):
        go_ref, d_ref, nm_ref, nv_ref = rest[-4:]
        gv = g_ref[0].astype(F32)
        for q in range(1, count):
            gv = gv + g_ref[q].astype(F32)
        delta, nm, nv = _adam_math(w_ref[0], gv, m_ref[0], v_ref[0])
        go_ref[0] = gv
        d_ref[0] = delta
        nm_ref[0] = nm
        nv_ref[0] = nv

    lspec = pl.BlockSpec((1, tr, c), lambda i: (layer, i, 0))
    sd = jax.ShapeDtypeStruct((nl, r, c), F32)
    extra = list(prev) if prev is not None else []
    return _call(
        body, name=name, grid=(r // tr,),
        in_specs=[lspec, pl.BlockSpec((count, tr, c), lambda i: (0, i, 0)), lspec, lspec] + [ANY] * len(extra),
        out_specs=[lspec] * 4, out_shape=[sd] * 4,
        input_output_aliases={4 + t: t for t in range(len(extra))},
        compiler_params=_params("parallel"),
    )(w, parts, m, v, *extra)


def _adam(w, g, m, v, name):
    r, c = w.shape
    tr = _row_tile(r)
    count = g.shape[0] if g.ndim == 3 else 0

    def body(w_ref, g_ref, m_ref, v_ref, go_ref, d_ref, nm_ref, nv_ref):
        if count:
            gv = g_ref[0].astype(F32)
            for q in range(1, count):
                gv = gv + g_ref[q].astype(F32)
        else:
            gv = g_ref[...]
        delta, nm, nv = _adam_math(w_ref[...], gv, m_ref[...], v_ref[...])
        go_ref[...] = gv
        d_ref[...] = delta
        nm_ref[...] = nm
        nv_ref[...] = nv

    spec = pl.BlockSpec((tr, c), lambda i: (i, 0))
    gspec = pl.BlockSpec((count, tr, c), lambda i: (0, i, 0)) if count else spec
    sd = jax.ShapeDtypeStruct((r, c), F32)
    return _call(
        body, name=name, grid=(r // tr,), in_specs=[spec, gspec, spec, spec], out_specs=[spec] * 4,
        out_shape=[sd] * 4, compiler_params=_params("parallel"),
    )(w, g, m, v)


def _pad16(rows):
    return jnp.pad(rows, ((0, 16 - rows.shape[0]), (0, 0)))


def kernel(x, norm_mix_g, norm_ffn_g, pool_w, pool_b, pool_scale, kv_norm_g, w_k, w_v, k_norm_g, w_q, q_norm_g, rel_bias, w_o, w_gate, w_up, w_down, loss_target, m_norm_mix_g, m_norm_ffn_g, m_pool_w, m_pool_b, m_pool_scale, m_kv_norm_g, m_w_k, m_w_v, m_k_norm_g, m_w_q, m_q_norm_g, m_rel_bias, m_w_o, m_w_gate, m_w_up, m_w_down, v_norm_mix_g, v_norm_ffn_g, v_pool_w, v_pool_b, v_pool_scale, v_kv_norm_g, v_w_k, v_w_v, v_k_norm_g, v_w_q, v_q_norm_g, v_rel_bias, v_w_o, v_w_gate, v_w_up, v_w_down):
    assert x.shape == (1, S, D) and w_gate.shape == (2, D, F_SHARD) and w_k.shape == (D_SHARD, D)
    xin, target = x[0], loss_target[0]

    ffn_shards = [[w_gate[layer].T.astype(BF16), w_up[layer].T.astype(BF16), w_down[layer].astype(BF16)]
                  for layer in range(2)]
    att_shards = [w_k.astype(BF16), w_v.astype(BF16), w_q[0].astype(BF16), w_o[0].astype(BF16)]
    pool_shard = pool_w[0].astype(BF16).reshape(N_GROUPS * POOL_SHARD, GROUP)
    small = jnp.concatenate([pool_b[0].reshape(1, N_GROUPS * POOL_SHARD), pool_scale], axis=1)

    full0 = _run_exchange(_relay_gather_exchange(ffn_shards[0] + [pool_shard, _pad16(small)]), "gather_layer0")
    ffn_w0 = [a.reshape(F, D) for a in full0[:3]]
    pw_f = full0[3].reshape(N_DEV, N_GROUPS, POOL_SHARD, GROUP).transpose(1, 0, 2, 3).reshape(N_GROUPS, GROUP, GROUP)
    small_f = full0[4][:, 0, :]
    pb_f = small_f[:, :N_GROUPS * POOL_SHARD].reshape(N_DEV, N_GROUPS, POOL_SHARD).transpose(1, 0, 2).reshape(1, D)
    ps_f = small_f[:, N_GROUPS * POOL_SHARD:].reshape(1, D)

    g_mix0, g_mix1 = norm_mix_g[0:1], norm_mix_g[1:2]
    g_ffn0, g_ffn1 = norm_ffn_g[0:1], norm_ffn_g[1:2]
    g_kv = kv_norm_g.reshape(1, D)
    gk_t = jnp.tile(k_norm_g.reshape(1, HEAD), (1, N_HEADS))
    gq_t = jnp.tile(q_norm_g.reshape(1, HEAD), (1, N_HEADS))

    x1, diff = _pool_fwd(xin, g_mix0, pw_f, pb_f, ps_f)
    (x2, hf0, gg0, uu0), full_att = _ffn_fwd(x1, g_ffn0, *ffn_w0, name="ffn_fwd", ex=_relay_gather_exchange(att_shards))
    wk_f, wv_f, wq_f, wo_f = [a.reshape(D, D) for a in full_att]
    (hkv, h1, kpre, qpre, kp, vp, qq), full1_gate = _kvq_proj(x2, g_kv, g_mix1, wk_f, wv_f, wq_f, gk_t, gq_t,
                                                                ex=_relay_gather_exchange(ffn_shards[1][:1]))
    bias = _bias_band(_toeplitz_from_table(rel_bias[0]))
    (att,), full1_rest = _attn_fwd(qq, kp, vp, bias, ex=_relay_gather_exchange(ffn_shards[1][1:]))
    ffn_w1 = [a.reshape(F, D) for a in full1_gate + full1_rest]
    x3 = _mm(att, wo_f, "nn", F32, "attn_out", add=x2)
    (dx4, loss_rows, hf1, gg1, uu1), _ = _ffn_fwd(x3, g_ffn1, *ffn_w1, name="ffn_fwd_loss", target=target)

    def blocks(dw):
        return dw.reshape(N_DEV, dw.shape[0] // N_DEV, dw.shape[1])

    (dgg1, duu1, dwg1, dwu1, dwd1), _ = _ffn_bwd_weights(dx4, hf1, gg1, uu1, ffn_w1[2], name="ffn_bwd1")
    (dx3, dg_ffn1), _ = _ffn_bwd_input(dx4, x3, g_ffn1, [dgg1, duu1], ffn_w1[:2], "ffn_dx1")
    datt = _mm(dx3, wo_f, "nt", BF16, "d_attn")
    dwo = _mm(att, dx3, "tn", BF16, "d_wo")
    parts1 = [blocks(dw) for dw in (dwg1, dwu1, dwd1, dwo)]
    (dq, dkp, dvp, dband), stage1 = _attn_bwd(qq, kp, vp, bias, datt, ex=_pair_exchange(parts1))
    both1 = _pair_add(parts1, stage1, "pair_add_ffn1")
    flight1, chip1, land1, _ = _chip_exchange_start(both1[:4], both1[4:], dq, "scatter_ffn1_start")
    (dx2, dqpre, dkpre, dg_mix1, dg_kv, dgq, dgk), _ = _qkv_input_bwd(
        dq, qpre, gq_t, dkp, kpre, gk_t, dvp, wq_f, wk_f, wv_f, dx3, x2, g_mix1, g_kv)
    dwq = _mm(h1, dqpre, "tn", BF16, "d_wq")
    dwk = _mm(hkv, dkpre, "tn", BF16, "d_wk")
    dwv = _mm(hkv, dvp, "tn", BF16, "d_wv", skip_rows=PADK)
    parts_att = [blocks(dw) for dw in (dwk, dwv, dwq)]
    (dgg0, duu0, *dw0), stage_att = _ffn_bwd_weights(
        dx2, hf0, gg0, uu0, ffn_w0[2], name="ffn_bwd0", ex=_pair_exchange(parts_att))
    both_att = _pair_add(parts_att, stage_att, "pair_add_att")
    flight_att, chip_att, land_att, _ = _chip_exchange_start(both_att[:3], both_att[3:], dx2, "scatter_att_start")
    parts0 = [blocks(dw) for dw in dw0]
    (dx1, dg_ffn0), stage0 = _ffn_bwd_input(dx2, x1, g_ffn0, [dgg0, duu0], ffn_w0[:2], "ffn_dx0",
                                            ex=_pair_exchange(parts0))
    both = _pair_add(parts0, stage0, "pair_add_ffn0")
    flight, chip0, land0, token = _chip_exchange_start(both[:3], both[3:], dx1, "scatter_ffn0_start")
    (grad_x, dpw, db_rows, ds_rows, dg_mix0), _ = _pool_bwd(dx1, xin, diff, g_mix0 + token[0:1, 0:1], pw_f, pb_f, ps_f)
    dtp, seg = _bias_grad(dband, dpw)

    weights = dict(norm_mix_g=norm_mix_g, norm_ffn_g=norm_ffn_g, pool_w=pool_w, pool_b=pool_b,
                   pool_scale=pool_scale, kv_norm_g=kv_norm_g, w_k=w_k, w_v=w_v, k_norm_g=k_norm_g, w_q=w_q,
                   q_norm_g=q_norm_g, rel_bias=rel_bias, w_o=w_o, w_gate=w_gate, w_up=w_up, w_down=w_down)
    mom1 = dict(norm_mix_g=m_norm_mix_g, norm_ffn_g=m_norm_ffn_g, pool_w=m_pool_w, pool_b=m_pool_b,
                pool_scale=m_pool_scale, kv_norm_g=m_kv_norm_g, w_k=m_w_k, w_v=m_w_v, k_norm_g=m_k_norm_g,
                w_q=m_w_q, q_norm_g=m_q_norm_g, rel_bias=m_rel_bias, w_o=m_w_o, w_gate=m_w_gate, w_up=m_w_up,
                w_down=m_w_down)
    mom2 = dict(norm_mix_g=v_norm_mix_g, norm_ffn_g=v_norm_ffn_g, pool_w=v_pool_w, pool_b=v_pool_b,
                pool_scale=v_pool_scale, kv_norm_g=v_kv_norm_g, w_k=v_w_k, w_v=v_w_v, k_norm_g=v_k_norm_g,
                w_q=v_w_q, q_norm_g=v_q_norm_g, rel_bias=v_rel_bias, w_o=v_w_o, w_gate=v_w_gate, w_up=v_w_up,
                w_down=v_w_down)
    names = list(weights)
    grads, deltas, new_m, new_v = {}, {}, {}, {}

    def adam_flat(nm, parts):
        shape = weights[nm].shape
        flat = lambda a: a.reshape(-1, shape[-1])
        done = _adam(flat(weights[nm]), parts, flat(mom1[nm]), flat(mom2[nm]), "adam_" + nm)
        grads[nm], deltas[nm], new_m[nm], new_v[nm] = [a.reshape(shape) for a in done]

    ffn_names = ("w_gate", "w_up", "w_down")
    def ffn_view(nm, a):
        return a if nm == "w_down" else a.transpose(0, 2, 1)

    recv1 = _chip_exchange_wait(flight1, chip1, land1, dtp, "scatter_ffn1_wait")
    layer1 = {nm: _adam_layer(ffn_view(nm, weights[nm]), g, ffn_view(nm, mom1[nm]), ffn_view(nm, mom2[nm]), 1,
                              "adam1_" + nm)
              for nm, g in zip(ffn_names, recv1)}
    adam_flat("w_o", recv1[3])
    recv_att = _chip_exchange_wait(flight_att, chip_att, land_att, deltas["w_o"], "scatter_att_wait")
    for nm, parts in zip(("w_k", "w_v", "w_q"), recv_att):
        adam_flat(nm, parts)

    dpw_blocks = dpw.reshape(N_GROUPS, N_DEV, POOL_SHARD, GROUP).transpose(1, 0, 2, 3)
    dpw_blocks = dpw_blocks.reshape(N_DEV, N_GROUPS * POOL_SHARD, GROUP).astype(BF16)
    misc = jnp.concatenate([dgk[0:1, 0:HEAD], dgq[0:1, 0:HEAD], loss_rows[0:1, 0:1],
                            seg[:, 0, 0].reshape(1, N_HEADS), seg[:, 0, 1].reshape(1, N_HEADS)], axis=1)
    misc = jnp.pad(misc, ((0, 0), (0, D - misc.shape[1])))
    vec_rows = jnp.concatenate([dg_mix0[0:1], dg_mix1[0:1], dg_ffn0[0:1], dg_ffn1[0:1], dg_kv[0:1],
                                db_rows[0:1], ds_rows[0:1], misc], axis=0)
    pack = jnp.concatenate([vec_rows, dtp.reshape(N_HEADS, TOEP)], axis=0)
    tot, recv_pool = _small_exchanges(pack, dpw_blocks, "small_exchanges", after=deltas["w_q"])

    loss = tot[7, 2 * HEAD]
    seg_tot = jnp.stack([tot[7, 2 * HEAD + 1:2 * HEAD + 1 + N_HEADS],
                         tot[7, 2 * HEAD + 1 + N_HEADS:2 * HEAD + 1 + 2 * N_HEADS]], axis=1)
    me = 4 * lax.axis_index("x") + 2 * lax.axis_index("y") + lax.axis_index("c")
    g_pool_b = lax.dynamic_slice_in_dim(tot[5].reshape(N_GROUPS, GROUP), me * POOL_SHARD, POOL_SHARD, axis=1)
    grads.update(
        norm_mix_g=tot[0:2], norm_ffn_g=tot[2:4], kv_norm_g=tot[4], k_norm_g=tot[7, 0:HEAD],
        q_norm_g=tot[7, HEAD:2 * HEAD].reshape(1, HEAD),
        rel_bias=_table_grad_from_toeplitz(tot[8:8 + N_HEADS], seg_tot).reshape(1, N_HEADS, N_REL),
        pool_b=g_pool_b.reshape(1, N_GROUPS, POOL_SHARD),
        pool_scale=lax.dynamic_slice_in_dim(tot[6:7], me * D_SHARD, D_SHARD, axis=1))
    adam_flat("pool_w", recv_pool)
    small_names = [nm for nm in names if nm not in ffn_names + ("w_k", "w_v", "w_q", "w_o", "pool_w")]

    def pack_small(tree):
        cols = []
        for nm in small_names:
            flat = tree[nm].reshape(-1)
            cols.append(jnp.pad(flat, (0, -flat.shape[0] % 1024)))
        return jnp.concatenate(cols).reshape(-1, 128)

    _, dl, m1, m2 = _adam(pack_small(weights), pack_small(grads), pack_small(mom1), pack_small(mom2), "adam_small")

    def unpack_small(packed, out):
        flat, off = packed.reshape(-1), 0
        for nm in small_names:
            size = weights[nm].size
            out[nm] = flat[off:off + size].reshape(weights[nm].shape)
            off += size + (-size % 1024)

    unpack_small(dl, deltas)
    unpack_small(m1, new_m)
    unpack_small(m2, new_v)

    recv0 = _chip_exchange_wait(flight, chip0, land0, dl, "scatter_ffn0_wait")
    for nm, g in zip(ffn_names, recv0):
        done = _adam_layer(ffn_view(nm, weights[nm]), g, ffn_view(nm, mom1[nm]), ffn_view(nm, mom2[nm]), 0,
                           "adam0_" + nm, prev=layer1[nm])
        grads[nm], deltas[nm], new_m[nm], new_v[nm] = [ffn_view(nm, a) for a in done]

    return (loss, grad_x[None], *[grads[nm] for nm in names], *[deltas[nm] for nm in names],
            *[new_m[nm] for nm in names], *[new_v[nm] for nm in names])
```

```python
import functools

import jax
import jax.numpy as jnp
from jax import lax
from jax.experimental import pallas as pl
from jax.experimental.pallas import tpu as pltpu

F32 = jnp.float32
BF16 = jnp.bfloat16
MESH_ID = pl.DeviceIdType.MESH

N_DEV = 8
S = 2048
D = 1024
F = 2816
F_SHARD = F // N_DEV
D_SHARD = D // N_DEV
N_GROUPS = 4
GROUP = D // N_GROUPS
POOL_SHARD = GROUP // N_DEV
MAX_WIN = 16
HEAD = 64
N_HEADS = D // HEAD
CHUNK = 64
LEFT = 8
QB = 4 * CHUNK
KB = QB + LEFT * CHUNK
PADK = LEFT * CHUNK
TOEP = 1024
N_REL = 257
MAX_REL = 128
EPS = 1e-6
NEG_INF = -1e30
ATTN_SCALE = HEAD ** -0.5

ADAM_LR = 0.001
ADAM_B1 = 0.9
ADAM_B2 = 0.999
ADAM_EPS = 1e-08
ADAM_WD = 0.01
ADAM_STEP = 10

VMEM_LIMIT = 52 * 1024 * 1024

ANY = pl.BlockSpec(memory_space=pl.ANY)
VMEM = pl.BlockSpec(memory_space=pltpu.VMEM)


def _call(body, **kw):
    return pl.pallas_call(body, **kw)


def _params(*sem):
    return pltpu.CompilerParams(dimension_semantics=sem, vmem_limit_bytes=VMEM_LIMIT)


def _dot(a, b, dims):
    return lax.dot_general(a, b, (dims, ((), ())), preferred_element_type=F32)


def _nn(a, b):
    return _dot(a, b, ((1,), (0,)))


def _nt(a, b):
    return _dot(a, b, ((1,), (1,)))


def _tn(a, b):
    return _dot(a, b, ((0,), (0,)))


def _rstd(x):
    return lax.rsqrt(jnp.mean(x * x, axis=-1, keepdims=True) + EPS)


def _rms_bwd(dh, x, r, g):
    gd = dh * g
    return r * gd - x * (r * r * r) * jnp.mean(gd * x, axis=-1, keepdims=True)


def _colsum8(v):
    return jnp.broadcast_to(jnp.sum(v, axis=0, keepdims=True), (8, v.shape[1]))


def _seg_sum(v):
    r = lax.broadcasted_iota(jnp.int32, (128, 128), 0) // HEAD
    c = lax.broadcasted_iota(jnp.int32, (128, 128), 1) // HEAD
    ones = jnp.where(r == c, 1.0, 0.0).astype(BF16)
    out = []
    for blk in range(v.shape[1] // 128):
        part = v[:, blk * 128:(blk + 1) * 128]
        hi = part.astype(BF16)
        rest = part - hi.astype(F32)
        mid = rest.astype(BF16)
        lo = (rest - mid.astype(F32)).astype(BF16)
        out.append(_nn(hi, ones) + _nn(mid, ones) + _nn(lo, ones))
    return jnp.concatenate(out, axis=1)


def _place():
    return lax.axis_index("x"), lax.axis_index("y"), lax.axis_index("c")


class _Exchange:
    def __init__(self, ins, out_shape, sems, start, mid, finish, early=None):
        self.ins, self.out_shape, self.sems = list(ins), list(out_shape), list(sems)
        self.start, self.mid, self.finish = start, mid, finish
        self.early = early if early is not None else (lambda ins, outs, sems: None)


def _relay_gather_exchange(shards):
    n = len(shards)

    def tools(ins, outs, sems):
        send_sems, recv_sems, local_sems = sems
        x, y, c = _place()
        me, sib = (x, y, c), (x, y, 1 - c)
        xn, yn, dg = (1 - x, y, c), (x, 1 - y, c), (1 - x, 1 - y, c)
        other = lambda dev: (dev[0], dev[1], 1 - c)

        def piece(k, ref, half):
            rows = shards[k].shape[0] // 2
            return ref if half is None else ref.at[pl.ds(half * rows, rows)]

        def held(k, dev, half):
            return piece(k, outs[k].at[4 * dev[0] + 2 * dev[1] + dev[2]], half)

        def copy(k, s, dev, half, to, own=False):
            return pltpu.make_async_remote_copy(
                src_ref=piece(k, ins[k], half) if own else held(k, dev, half), dst_ref=held(k, dev, half),
                send_sem=send_sems.at[k, s], recv_sem=recv_sems.at[k, s], device_id=to, device_id_type=MESH_ID)

        def mine(k):
            return pltpu.make_async_copy(ins[k], held(k, me, None), local_sems.at[k])

        def sent(k):
            return [copy(k, 0, me, None, sib, own=True), copy(k, 1, me, 0, xn, own=True), copy(k, 2, me, 1, yn, own=True),
                    copy(k, 3, me, 0, yn, own=True), copy(k, 4, me, 1, xn, own=True)]

        relays = [(1, xn, 0, [(5, yn), (7, sib)]), (2, yn, 1, [(6, xn), (9, sib)]), (3, yn, 0, [(10, sib)]),
                  (4, xn, 1, [(8, sib)]), (6, dg, 1, [(11, sib)]), (5, dg, 0, [(12, sib)])]
        from_sibling = [(0, other(me), None), (7, other(xn), 0), (8, other(xn), 1), (9, other(yn), 1),
                        (10, other(yn), 0), (11, other(dg), 1), (12, other(dg), 0)]
        return mine, sent, copy, relays, from_sibling, me

    def start(ins, outs, sems):
        mine, sent, _, _, _, _ = tools(ins, outs, sems)
        for k in range(n):
            mine(k).start()
        for k in range(n):
            for cp in sent(k):
                cp.start()

    def pass_on(ins, outs, sems, which):
        _, _, copy, relays, _, me = tools(ins, outs, sems)
        for k in range(n):
            for s, dev, half, onward in relays[which]:
                copy(k, s, dev, half, me).wait_recv()
                for s2, to in onward:
                    copy(k, s2, dev, half, to).start()

    def early(ins, outs, sems):
        pass_on(ins, outs, sems, slice(0, 2))

    def mid(ins, outs, sems):
        pass_on(ins, outs, sems, slice(2, None))

    def finish(ins, outs, sems):
        mine, sent, copy, relays, from_sibling, me = tools(ins, outs, sems)
        for k in range(n):
            for s, dev, half in from_sibling:
                copy(k, s, dev, half, me).wait_recv()
        for k in range(n):
            for cp in sent(k):
                cp.wait_send()
            for _, dev, half, onward in relays:
                for s2, to in onward:
                    copy(k, s2, dev, half, to).wait_send()
            mine(k).wait()

    return _Exchange(
        shards, [jax.ShapeDtypeStruct((N_DEV,) + a.shape, a.dtype) for a in shards],
        [pltpu.SemaphoreType.DMA((n, 13)), pltpu.SemaphoreType.DMA((n, 13)), pltpu.SemaphoreType.DMA((n,))],
        start, mid, finish, early)


def _peer(x, y, c, m):
    px = 1 - x if m & 4 else x
    py = 1 - y if m & 2 else y
    pc = 1 - c if m & 1 else c
    return px, py, pc


N_CHIPS = N_DEV // 2


def _pair_exchange(parts):
    n = len(parts)

    def copies(ins, outs, sems):
        send_sems, recv_sems = sems
        x, y, c = _place()
        return [pltpu.make_async_remote_copy(
            src_ref=ins[k].at[2 * q + 1 - c], dst_ref=outs[k].at[q], send_sem=send_sems.at[k, q],
            recv_sem=recv_sems.at[k, q], device_id=(x, y, 1 - c), device_id_type=MESH_ID)
            for k in range(n) for q in range(N_CHIPS)]

    def start(ins, outs, sems):
        for cp in copies(ins, outs, sems):
            cp.start()

    def mid(ins, outs, sems):
        pass

    def finish(ins, outs, sems):
        for cp in copies(ins, outs, sems):
            cp.wait_recv()
        for cp in copies(ins, outs, sems):
            cp.wait_send()

    return _Exchange(
        parts, [jax.ShapeDtypeStruct((N_CHIPS,) + a.shape[1:], a.dtype) for a in parts],
        [pltpu.SemaphoreType.DMA((n, N_CHIPS)), pltpu.SemaphoreType.DMA((n, N_CHIPS))], start, mid, finish)


def _pair_add(parts, stage, name):
    n = len(parts)
    place = jnp.stack([lax.axis_index("c"), 2 * lax.axis_index("x") + lax.axis_index("y")]).astype(jnp.int32)

    def body(place_ref, *refs):
        for k in range(n):
            mine, theirs = refs[k], refs[n + k]
            total = (mine[0, 0].astype(F32) + theirs[0].astype(F32)).astype(BF16)
            refs[2 * n + k][0] = total

            @pl.when(pl.program_id(1) == place_ref[1])
            def _():
                refs[3 * n + k][0] = total

    in_specs, out_specs, land_specs = [], [], []
    for a in parts:
        _, r, cdim = a.shape
        in_specs.append(pl.BlockSpec((1, 1, r // 2, cdim), lambda i, q, place_ref: (q, place_ref[0], i, 0)))
    for a in parts:
        _, r, cdim = a.shape
        in_specs.append(pl.BlockSpec((1, r // 2, cdim), lambda i, q, place_ref: (q, i, 0)))
        out_specs.append(pl.BlockSpec((1, r // 2, cdim), lambda i, q, place_ref: (q, i, 0)))
        land_specs.append(pl.BlockSpec((1, r // 2, cdim), lambda i, q, place_ref: (place_ref[1], i, 0)))
    return list(_call(
        body, name=name,
        grid_spec=pltpu.PrefetchScalarGridSpec(num_scalar_prefetch=1, grid=(2, N_CHIPS), in_specs=in_specs,
                                               out_specs=out_specs + land_specs),
        out_shape=[jax.ShapeDtypeStruct(s.shape, BF16) for s in stage] * 2,
        compiler_params=_params("arbitrary", "arbitrary"),
    )(place, *[a.reshape((N_CHIPS, 2) + a.shape[1:]) for a in parts], *stage))


HBM = pl.BlockSpec(memory_space=pltpu.HBM)
SEMAPHORES = pl.BlockSpec(memory_space=pltpu.SEMAPHORE)


def _chip_copies(srcs, lands, send_sems, recv_sems, mine_is_dst):
    x, y, c = _place()
    me = 2 * x + y
    out = []
    for m in range(1, N_CHIPS):
        px, py, _ = _peer(x, y, c, 2 * m)
        peer = 2 * px + py
        for k in range(len(srcs)):
            pair = k * (N_CHIPS - 1) + m - 1
            out.append(pltpu.make_async_remote_copy(
                src_ref=srcs[k].at[peer], dst_ref=lands[k].at[me if mine_is_dst else peer],
                send_sem=send_sems[pair], recv_sem=recv_sems[pair],
                device_id=(px, py, c), device_id_type=MESH_ID))
    return out


def _chip_exchange_start(chip_parts, lands, after, name):
    n = len(chip_parts)
    pairs = n * (N_CHIPS - 1)

    def body(*refs):
        srcs, zones = refs[:n], refs[n:2 * n]
        sems = refs[2 * n + 1:2 * n + 1 + 2 * pairs]
        token = refs[-1]
        for cp in _chip_copies(srcs, zones, sems[:pairs], sems[pairs:], True):
            cp.start()
        token[...] = jnp.zeros(token.shape, F32)

    thru = [pltpu.HBM(a.shape, a.dtype) for a in chip_parts + lands]
    hbm = [pltpu.with_memory_space_constraint(a, pltpu.HBM) for a in chip_parts + lands]
    res = _call(
        body, name=name,
        out_shape=[pltpu.SemaphoreType.DMA(())] * (2 * pairs) + thru + [jax.ShapeDtypeStruct((8, 128), F32)],
        in_specs=[HBM] * (2 * n) + [ANY], out_specs=[SEMAPHORES] * (2 * pairs) + [HBM] * (2 * n) + [VMEM],
        input_output_aliases={i: 2 * pairs + i for i in range(2 * n)},
        compiler_params=pltpu.CompilerParams(has_side_effects=pltpu.SideEffectType.DATAFLOW_SIDE_EFFECTING),
    )(*hbm, after)
    sems, rest = list(res[:2 * pairs]), res[2 * pairs:]
    return sems, list(rest[:n]), list(rest[n:2 * n]), rest[-1]


def _chip_exchange_wait(sems, chip_parts, lands, after, name):
    n = len(chip_parts)
    pairs = n * (N_CHIPS - 1)
    after = list(after) if isinstance(after, (list, tuple)) else [after]

    def body(*refs):
        srcs, zones = refs[:n], refs[n:2 * n]
        sem_refs = refs[2 * n:2 * n + 2 * pairs]
        for cp in _chip_copies(srcs, zones, sem_refs[:pairs], sem_refs[pairs:], False):
            cp.wait_send()
            cp.wait_recv()

    thru = [pltpu.HBM(a.shape, a.dtype) for a in chip_parts + lands]
    res = _call(
        body, name=name, out_shape=thru,
        in_specs=[HBM] * (2 * n) + [SEMAPHORES] * (2 * pairs) + [ANY] * len(after), out_specs=[HBM] * (2 * n),
        input_output_aliases={i: i for i in range(2 * n)},
        compiler_params=pltpu.CompilerParams(has_side_effects=pltpu.SideEffectType.DATAFLOW_SIDE_EFFECTING),
    )(*chip_parts, *lands, *sems, *after)
    return list(res[n:])


def _run_exchange(ex, name, after=None):
    n_in, n_out = len(ex.ins), len(ex.out_shape)
    order = [] if after is None else [after]

    def body(*refs):
        ins, outs = refs[:n_in], refs[n_in + len(order):n_in + len(order) + n_out]
        sems = refs[n_in + len(order) + n_out:]
        ex.start(ins, outs, sems)
        ex.early(ins, outs, sems)
        ex.mid(ins, outs, sems)
        ex.finish(ins, outs, sems)

    return list(_call(body, name=name, out_shape=ex.out_shape, in_specs=[ANY] * (n_in + len(order)),
                      out_specs=[ANY] * n_out, scratch_shapes=ex.sems)(*ex.ins, *order))


def _call_hosting(body, ex, phases, *, in_specs, out_specs, out_shape, scratch_shapes, args, **kw):
    n_in, n_out, n_scr = len(in_specs), len(out_specs), len(scratch_shapes)
    if ex is None:
        res = _call(body, in_specs=in_specs, out_specs=out_specs, out_shape=out_shape,
                    scratch_shapes=scratch_shapes, **kw)(*args)
        return list(res), []
    n_xin, n_xout = len(ex.ins), len(ex.out_shape)

    def hosting(*refs):
        a, b = n_in, n_in + n_xin
        c, d = b + n_out, b + n_out + n_xout
        ins, xins, outs, xouts = refs[:a], refs[a:b], refs[b:c], refs[c:d]
        scr, sems = refs[d:d + n_scr], refs[d + n_scr:]
        first, early, mid, last = phases()

        @pl.when(first)
        def _():
            ex.start(xins, xouts, sems)

        body(*ins, *outs, *scr)

        @pl.when(early)
        def _():
            ex.early(xins, xouts, sems)

        @pl.when(mid)
        def _():
            ex.mid(xins, xouts, sems)

        @pl.when(last)
        def _():
            ex.finish(xins, xouts, sems)

    res = _call(hosting, in_specs=list(in_specs) + [ANY] * n_xin, out_specs=list(out_specs) + [ANY] * n_xout,
                out_shape=list(out_shape) + ex.out_shape, scratch_shapes=list(scratch_shapes) + ex.sems,
                **kw)(*args, *ex.ins)
    return list(res[:n_out]), list(res[n_out:])


def _grid_phases(dims, mid_fraction=0.8, early_fraction=0.4):
    total = 1
    for d in dims:
        total *= d
    mid = min(max(int(total * mid_fraction), 1), total - 1)
    early = min(int(total * early_fraction), mid)

    def phases():
        step = pl.program_id(0)
        for axis in range(1, len(dims)):
            step = step * dims[axis] + pl.program_id(axis)
        return step == 0, step == early, step == mid, step == total - 1
    return phases


def _small_exchanges(pack, blocks, name, after):
    rows = pack.shape[0]

    def body(in_ref, blk_ref, _, out_ref, got_ref, recv, send_sems, recv_sems, blk_send, blk_recv, blk_local):
        x, y, c = _place()
        me = 4 * x + 2 * y + c
        peers = [_peer(x, y, c, m) for m in range(1, N_DEV)]
        index = [4 * px + 2 * py + pc for px, py, pc in peers]

        def block_copy(m, landing):
            return pltpu.make_async_remote_copy(
                src_ref=blk_ref.at[index[m]], dst_ref=got_ref.at[index[m] if landing else me],
                send_sem=blk_send.at[m], recv_sem=blk_recv.at[m], device_id=peers[m], device_id_type=MESH_ID)

        def pack_copy(m, landing):
            return pltpu.make_async_remote_copy(
                src_ref=in_ref, dst_ref=recv.at[index[m] if landing else me], send_sem=send_sems.at[m],
                recv_sem=recv_sems.at[m], device_id=peers[m], device_id_type=MESH_ID)

        own = pltpu.make_async_copy(blk_ref.at[me], got_ref.at[me], blk_local)
        own.start()
        recv[me] = in_ref[...]
        for m in range(N_DEV - 1):
            pack_copy(m, False).start()
            block_copy(m, False).start()
        for m in range(N_DEV - 1):
            pack_copy(m, True).wait_recv()
        acc = recv[0]
        for d in range(1, N_DEV):
            acc = acc + recv[d]
        out_ref[...] = acc
        for m in range(N_DEV - 1):
            block_copy(m, True).wait_recv()
        for m in range(N_DEV - 1):
            pack_copy(m, False).wait_send()
            block_copy(m, False).wait_send()
        own.wait()

    return _call(
        body, name=name,
        out_shape=[jax.ShapeDtypeStruct(pack.shape, F32), jax.ShapeDtypeStruct(blocks.shape, blocks.dtype)],
        in_specs=[VMEM, ANY, ANY], out_specs=[VMEM, ANY],
        scratch_shapes=[pltpu.VMEM((N_DEV, rows, pack.shape[1]), F32)] + [pltpu.SemaphoreType.DMA((7,))] * 4
        + [pltpu.SemaphoreType.DMA],
        compiler_params=pltpu.CompilerParams(vmem_limit_bytes=VMEM_LIMIT),
    )(pack, blocks, after)


POOL_TS = 256


def _pool_counts(first_row, rows, win):
    t = first_row + lax.broadcasted_iota(jnp.int32, (rows, 1), 0)
    return jnp.minimum(t + 1, win).astype(F32)


def _pool_fwd(x, g, w, b, scale):
    nt = S // POOL_TS

    def body(x_ref, g_ref, w_ref, b_ref, s_ref, out_ref, diff_ref, ext):
        i = pl.program_id(0)

        @pl.when(i == 0)
        def _():
            ext[0:MAX_WIN, :] = jnp.zeros((MAX_WIN, D), F32)

        @pl.when(i > 0)
        def _():
            ext[0:MAX_WIN, :] = ext[POOL_TS:POOL_TS + MAX_WIN, :]

        xv = x_ref[...]
        h = xv * _rstd(xv) * g_ref[...]
        ext[MAX_WIN:, :] = h
        for gi in range(N_GROUPS):
            win = 2 << gi
            cols = slice(gi * GROUP, (gi + 1) * GROUP)
            sm = ext[:, cols]
            k = 1
            while k < win:
                sm = sm + pltpu.roll(sm, k, axis=0)
                k *= 2
            pooled = sm[MAX_WIN:, :] / _pool_counts(i * POOL_TS, POOL_TS, win)
            diff = (pooled - h[:, cols]).astype(BF16)
            yv = (_nn(diff, w_ref[gi]) + b_ref[:, cols]) * s_ref[:, cols]
            out_ref[:, cols] = xv[:, cols] + yv
            diff_ref[:, cols] = diff

    row = pl.BlockSpec((1, D), lambda i: (0, 0))
    tile = pl.BlockSpec((POOL_TS, D), lambda i: (i, 0))
    return _call(
        body, name="pool_fwd", grid=(nt,),
        in_specs=[tile, row, pl.BlockSpec((N_GROUPS, GROUP, GROUP), lambda i: (0, 0, 0)), row, row],
        out_specs=[tile, tile],
        out_shape=[jax.ShapeDtypeStruct((S, D), F32), jax.ShapeDtypeStruct((S, D), BF16)],
        scratch_shapes=[pltpu.VMEM((POOL_TS + MAX_WIN, D), F32)],
        compiler_params=_params("arbitrary"),
    )(x, g, w, b, scale)


def _pool_bwd(dy, x, diff, g, w, b, scale, ex=None):
    nt = S // POOL_TS

    def body(dy_ref, x_ref, diff_ref, g_ref, w_ref, b_ref, s_ref, gx_ref, dw_ref, db_ref, ds_ref, dg_ref, ext, dh):
        i = pl.program_id(0)
        first_row = (nt - 1 - i) * POOL_TS

        @pl.when(i == 0)
        def _():
            ext[POOL_TS:, :] = jnp.zeros((MAX_WIN, D), F32)
            dw_ref[...] = jnp.zeros(dw_ref.shape, F32)
            db_ref[...] = jnp.zeros(db_ref.shape, F32)
            ds_ref[...] = jnp.zeros(ds_ref.shape, F32)
            dg_ref[...] = jnp.zeros(dg_ref.shape, F32)

        @pl.when(i > 0)
        def _():
            ext[POOL_TS:, :] = ext[0:MAX_WIN, :]

        dyv = dy_ref[...]
        for gi in range(N_GROUPS):
            win = 2 << gi
            cols = slice(gi * GROUP, (gi + 1) * GROUP)
            dfb = diff_ref[:, cols]
            z = _nn(dfb, w_ref[gi]) + b_ref[:, cols]
            dyg = dyv[:, cols]
            ds_ref[:, cols] += _colsum8(dyg * z)
            dz = dyg * s_ref[:, cols]
            db_ref[:, cols] += _colsum8(dz)
            dzb = dz.astype(BF16)
            dw_ref[gi] += _tn(dfb, dzb)
            ddiff = _nt(dzb, w_ref[gi])
            ext[0:POOL_TS, cols] = ddiff / _pool_counts(first_row, POOL_TS, win)
            sm = ext[:, cols]
            k = 1
            while k < win:
                sm = sm + pltpu.roll(sm, POOL_TS + MAX_WIN - k, axis=0)
                k *= 2
            dh[:, cols] = sm[0:POOL_TS, :] - ddiff
        xv = x_ref[...]
        r = _rstd(xv)
        gv = g_ref[...]
        dhv = dh[...]
        dg_ref[...] += _colsum8(dhv * xv * r)
        gx_ref[...] = dyv + _rms_bwd(dhv, xv, r, gv)

    row = pl.BlockSpec((1, D), lambda i: (0, 0))
    tile = pl.BlockSpec((POOL_TS, D), lambda i: (nt - 1 - i, 0))
    acc = pl.BlockSpec((8, D), lambda i: (0, 0))
    wspec = pl.BlockSpec((N_GROUPS, GROUP, GROUP), lambda i: (0, 0, 0))
    return _call_hosting(
        body, ex, _grid_phases((nt,)), name="pool_bwd", grid=(nt,),
        in_specs=[tile, tile, tile, row, wspec, row, row],
        out_specs=[tile, wspec, acc, acc, acc],
        out_shape=[jax.ShapeDtypeStruct((S, D), F32), jax.ShapeDtypeStruct((N_GROUPS, GROUP, GROUP), F32),
                   jax.ShapeDtypeStruct((8, D), F32), jax.ShapeDtypeStruct((8, D), F32),
                   jax.ShapeDtypeStruct((8, D), F32)],
        scratch_shapes=[pltpu.VMEM((POOL_TS + MAX_WIN, D), F32), pltpu.VMEM((POOL_TS, D), F32)],
        args=(dy, x, diff, g, w, b, scale), compiler_params=_params("arbitrary"),
    )


FFN_TS = min(S, 1024)
FFN_TF = 256


def _ffn_fwd(x, g, wg_t, wu_t, wd, name, target=None, ex=None):
    width = wd.shape[0]
    ni, nj = S // FFN_TS, width // FFN_TF
    with_loss = target is not None
    extra = [target] if with_loss else []

    def body(*refs):
        x_ref, g_ref, wg_ref, wu_ref, wd_ref = refs[:5]
        if with_loss:
            t_ref, out_ref, loss_ref, h_ref, gg_ref, uu_ref, hs, acc = refs[5:]
        else:
            out_ref, h_ref, gg_ref, uu_ref, hs, acc = refs[5:]
        i, j = pl.program_id(0), pl.program_id(1)

        @pl.when(j == 0)
        def _():
            xv = x_ref[...]
            hb = (xv * _rstd(xv) * g_ref[...]).astype(BF16)
            hs[...] = hb
            h_ref[...] = hb
            acc[...] = jnp.zeros(acc.shape, F32)

        hb = hs[...]
        gg = _nt(hb, wg_ref[...])
        uu = _nt(hb, wu_ref[...])
        gg_ref[...] = gg
        uu_ref[...] = uu
        a = (gg * jax.nn.sigmoid(gg) * uu).astype(BF16)
        acc[...] += _nn(a, wd_ref[...])

        @pl.when(j == nj - 1)
        def _():
            yv = x_ref[...] + acc[...]
            if with_loss:
                err = yv - t_ref[...]
                out_ref[...] = err * (1.0 / D)
                part = jnp.sum(err * err) * (0.5 / D)

                @pl.when(i == 0)
                def _():
                    loss_ref[...] = jnp.zeros(loss_ref.shape, F32)

                loss_ref[...] += jnp.broadcast_to(part, loss_ref.shape)
            else:
                out_ref[...] = yv

    xt = pl.BlockSpec((FFN_TS, D), lambda i, j: (i, 0))
    row = pl.BlockSpec((1, D), lambda i, j: (0, 0))
    wt = pl.BlockSpec((FFN_TF, D), lambda i, j: (j, 0))
    gt = pl.BlockSpec((FFN_TS, FFN_TF), lambda i, j: (i, j))
    in_specs = [xt, row, wt, wt, wt] + [xt] * len(extra)
    out_specs = [xt] + ([pl.BlockSpec((8, 128), lambda i, j: (0, 0))] if with_loss else []) + [xt, gt, gt]
    out_shape = ([jax.ShapeDtypeStruct((S, D), F32)] + ([jax.ShapeDtypeStruct((8, 128), F32)] if with_loss else [])
                 + [jax.ShapeDtypeStruct((S, D), BF16), jax.ShapeDtypeStruct((S, width), F32),
                    jax.ShapeDtypeStruct((S, width), F32)])
    args = (x, g, wg_t, wu_t, wd, *extra)
    return _call_hosting(
        body, ex, _grid_phases((ni, nj), 0.8), name=name, grid=(ni, nj),
        in_specs=in_specs, out_specs=out_specs, out_shape=out_shape,
        scratch_shapes=[pltpu.VMEM((FFN_TS, D), BF16), pltpu.VMEM((FFN_TS, D), F32)], args=args,
        compiler_params=_params("arbitrary", "arbitrary"),
    )


def _ffn_bwd_weights(dout, h, gg, uu, wd, name, ex=None):
    width = wd.shape[0]
    nj = width // FFN_TF

    def body(do_ref, h_ref, gg_ref, uu_ref, wd_ref, dg_ref, du_ref, dwg_ref, dwu_ref, dwd_ref, dob, kept):
        j = pl.program_id(0)

        @pl.when(j == 0)
        def _():
            dob[...] = do_ref[...].astype(BF16)
            kept[1] = jnp.zeros(kept.shape[1:], BF16)

        prev, cur = (j + 1) % 2, j % 2
        hb, dov = h_ref[...], dob[...]
        dwg_ref[...] = _tn(kept[prev, 0], hb).astype(BF16)
        dwu_ref[...] = _tn(kept[prev, 1], hb).astype(BF16)
        dwd_ref[...] = _tn(kept[prev, 2], dov).astype(BF16)
        gv, uv = gg_ref[...], uu_ref[...]
        da = _nt(dov, wd_ref[...])
        sg = jax.nn.sigmoid(gv)
        sl = gv * sg
        dub = (da * sl).astype(BF16)
        dgb = (da * uv * (sg * (1.0 + gv * (1.0 - sg)))).astype(BF16)
        dg_ref[...] = dgb
        du_ref[...] = dub
        kept[cur, 0] = dgb
        kept[cur, 1] = dub
        kept[cur, 2] = (sl * uv).astype(BF16)

    once = pl.Buffered(1)
    whole = lambda: pl.BlockSpec((S, D), lambda j, _: (0, 0), pipeline_mode=once)
    this = lambda j: jnp.minimum(j, nj - 1)
    last = lambda j: jnp.maximum(j - 1, 0)
    wt_in = pl.BlockSpec((FFN_TF, D), lambda j, _: (this(j), 0))
    wt_out = pl.BlockSpec((FFN_TF, D), lambda j, _: (last(j), 0))
    gt = pl.BlockSpec((S, FFN_TF), lambda j, _: (0, this(j)))
    return _call_hosting(
        body, ex, _grid_phases((nj + 1, 1)), name=name, grid=(nj + 1, 1),
        in_specs=[whole(), whole(), gt, gt, wt_in], out_specs=[gt, gt, wt_out, wt_out, wt_out],
        out_shape=[jax.ShapeDtypeStruct((S, width), BF16)] * 2 + [jax.ShapeDtypeStruct((width, D), BF16)] * 3,
        scratch_shapes=[pltpu.VMEM((S, D), BF16), pltpu.VMEM((2, 3, S, FFN_TF), BF16)], args=(dout, h, gg, uu, wd),
        compiler_params=_params("arbitrary", "arbitrary"),
    )


BWD_TS = 512


def _ffn_bwd_input(dres, x, g, grads, weights, name, ex=None):
    nt = S // BWD_TS
    n = len(grads)

    def body(*refs):
        dres_ref, x_ref, g_ref = refs[:3]
        grad_refs, w_refs = refs[3:3 + n], refs[3 + n:3 + 2 * n]
        dx_ref, dgam_ref = refs[3 + 2 * n:]
        dh = _nn(grad_refs[0][...], w_refs[0][...])
        for k in range(1, n):
            dh = dh + _nn(grad_refs[k][...], w_refs[k][...])
        xv = x_ref[...]
        r = _rstd(xv)
        dx_ref[...] = dres_ref[...] + _rms_bwd(dh, xv, r, g_ref[...])

        @pl.when(pl.program_id(0) == 0)
        def _():
            dgam_ref[...] = jnp.zeros(dgam_ref.shape, F32)

        dgam_ref[...] += _colsum8(dh * xv * r)

    tile = pl.BlockSpec((BWD_TS, D), lambda i: (i, 0))
    ftiles = [pl.BlockSpec((BWD_TS, a.shape[1]), lambda i: (i, 0)) for a in grads]
    wspecs = [pl.BlockSpec(w.shape, lambda i: (0, 0), pipeline_mode=pl.Buffered(1)) for w in weights]
    return _call_hosting(
        body, ex, _grid_phases((nt,)), name=name, grid=(nt,),
        in_specs=[tile, tile, pl.BlockSpec((1, D), lambda i: (0, 0))] + ftiles + wspecs,
        out_specs=[tile, pl.BlockSpec((8, D), lambda i: (0, 0))],
        out_shape=[jax.ShapeDtypeStruct((S, D), F32), jax.ShapeDtypeStruct((8, D), F32)],
        scratch_shapes=[], args=(dres, x, g, *grads, *weights), compiler_params=_params("arbitrary"),
    )


def _mm(a, b, mode, out_dtype, name, add=None, skip_rows=0):
    if mode == "nn":
        (m, kd), n = (a.shape[0] - skip_rows, a.shape[1]), b.shape[1]
    elif mode == "nt":
        (m, kd), n = (a.shape[0] - skip_rows, a.shape[1]), b.shape[0]
    else:
        (kd, m), n = a.shape, b.shape[1]
    tm, tn, tk = min(m, 1024), min(n, 1024), min(kd, 1024)
    if skip_rows and mode == "tn":
        tk = min(tk, skip_rows)
    elif skip_rows:
        tm = min(tm, skip_rows)
    skip_a = skip_rows // tm if mode != "tn" else 0
    skip_b = skip_rows // tk if mode == "tn" else 0
    assert skip_rows == skip_a * tm + skip_b * tk
    nk = kd // tk
    dot = {"nn": _nn, "nt": _nt, "tn": _tn}[mode]

    def body(*refs):
        if add is None:
            a_ref, b_ref, o_ref, acc = refs
        else:
            a_ref, b_ref, add_ref, o_ref, acc = refs
        k = pl.program_id(2)

        @pl.when(k == 0)
        def _():
            acc[...] = jnp.zeros(acc.shape, F32)

        acc[...] += dot(a_ref[...].astype(BF16), b_ref[...].astype(BF16))

        @pl.when(k == nk - 1)
        def _():
            res = acc[...]
            if add is not None:
                res = res + add_ref[...]
            o_ref[...] = res.astype(out_dtype)

    if mode == "tn":
        a_spec = pl.BlockSpec((tk, tm), lambda i, j, k: (k, i))
        b_spec = pl.BlockSpec((tk, tn), lambda i, j, k: (k + skip_b, j))
    else:
        a_spec = pl.BlockSpec((tm, tk), lambda i, j, k: (i + skip_a, k))
        b_spec = (pl.BlockSpec((tk, tn), lambda i, j, k: (k, j)) if mode == "nn"
                  else pl.BlockSpec((tn, tk), lambda i, j, k: (j, k)))
    o_spec = pl.BlockSpec((tm, tn), lambda i, j, k: (i, j))
    in_specs = [a_spec, b_spec] + ([o_spec] if add is not None else [])
    args = (a, b) + ((add,) if add is not None else ())
    return _call(
        body, name=name, grid=(m // tm, n // tn, nk), in_specs=in_specs, out_specs=o_spec,
        out_shape=jax.ShapeDtypeStruct((m, n), out_dtype), scratch_shapes=[pltpu.VMEM((tm, tn), F32)],
        compiler_params=_params("parallel", "parallel", "arbitrary"),
    )(*args)


PROJ_TS = 256


def _kvq_proj(x, g_kv, g_mix, wk, wv, wq, gk, gq, ex=None):
    lead = PADK // PROJ_TS

    def body(x_ref, gkv_ref, gmix_ref, wk_ref, wv_ref, wq_ref, gk_ref, gq_ref,
             hkv_ref, h1_ref, kpre_ref, qpre_ref, k_ref, v_ref, q_ref):
        i = pl.program_id(0)

        @pl.when(i < lead)
        def _():
            k_ref[...] = jnp.zeros(k_ref.shape, BF16)
            v_ref[...] = jnp.zeros(v_ref.shape, BF16)

        @pl.when(i >= lead)
        def _():
            xv = x_ref[...]
            xr = xv * _rstd(xv)
            hkv = (xr * gkv_ref[...]).astype(BF16)
            h1 = (xr * gmix_ref[...]).astype(BF16)
            hkv_ref[...] = hkv
            h1_ref[...] = h1
            kpre = _nn(hkv, wk_ref[...])
            qpre = _nn(h1, wq_ref[...])
            kpre_ref[...] = kpre
            qpre_ref[...] = qpre
            v_ref[...] = _nn(hkv, wv_ref[...]).astype(BF16)
            rk = lax.rsqrt(_seg_sum(kpre * kpre) * (1.0 / HEAD) + EPS)
            k_ref[...] = (kpre * rk * gk_ref[...]).astype(BF16)
            rq = lax.rsqrt(_seg_sum(qpre * qpre) * (1.0 / HEAD) + EPS)
            q_ref[...] = (qpre * rq * gq_ref[...]).astype(BF16)

    tile = pl.BlockSpec((PROJ_TS, D), lambda i: (jnp.maximum(i - lead, 0), 0))
    padded = pl.BlockSpec((PROJ_TS, D), lambda i: (i, 0))
    row = pl.BlockSpec((1, D), lambda i: (0, 0))
    wspec = pl.BlockSpec((D, D), lambda i: (0, 0))
    bf = jax.ShapeDtypeStruct((S, D), BF16)
    ff = jax.ShapeDtypeStruct((S, D), F32)
    bp = jax.ShapeDtypeStruct((PADK + S, D), BF16)
    return _call_hosting(
        body, ex, _grid_phases((lead + S // PROJ_TS,), 0.85), name="kvq_proj", grid=(lead + S // PROJ_TS,),
        in_specs=[tile, row, row, wspec, wspec, wspec, row, row],
        out_specs=[tile, tile, tile, tile, padded, padded, tile], out_shape=[bf, bf, ff, ff, bp, bp, bf],
        scratch_shapes=[], args=(x, g_kv, g_mix, wk, wv, wq, gk, gq), compiler_params=_params("arbitrary"),
    )


def _qkv_input_bwd(dq, qpre, gq, dkp, kpre, gk, dvp, wq, wk, wv, dres, x, g_mix, g_kv, ex=None):
    nt = S // PROJ_TS
    lead = PADK // PROJ_TS

    def head_bwd(dov, pv, hgv):
        r = lax.rsqrt(_seg_sum(pv * pv) * (1.0 / HEAD) + EPS)
        gd = dov * hgv
        dpre = r * gd - pv * (r * r * r) * (_seg_sum(gd * pv) * (1.0 / HEAD))
        return dpre.astype(BF16), _colsum8(dov * pv * r)

    def fold_heads(full):
        fold = full[:, 0:128]
        for blk in range(1, D // 128):
            fold = fold + full[:, blk * 128:(blk + 1) * 128]
        return fold + pltpu.roll(fold, HEAD, axis=1)

    def body(dq_ref, qpre_ref, gq_ref, dk_ref, kpre_ref, gk_ref, dv_ref, wq_ref, wk_ref, wv_ref, dres_ref, x_ref,
             gmix_ref, gkv_ref, dx_ref, dqpre_ref, dkpre_ref, dgmix_ref, dgkv_ref, dgq_ref, dgk_ref, accq, acck):
        i = pl.program_id(0)

        @pl.when(i == 0)
        def _():
            accq[...] = jnp.zeros(accq.shape, F32)
            acck[...] = jnp.zeros(acck.shape, F32)
            dgmix_ref[...] = jnp.zeros(dgmix_ref.shape, F32)
            dgkv_ref[...] = jnp.zeros(dgkv_ref.shape, F32)

        dqb, cq = head_bwd(dq_ref[...], qpre_ref[...], gq_ref[...])
        dkb, ck = head_bwd(dk_ref[...], kpre_ref[...], gk_ref[...])
        dqpre_ref[...] = dqb
        dkpre_ref[...] = dkb
        accq[...] += cq
        acck[...] += ck
        dh1 = _nt(dqb, wq_ref[...])
        dhkv = _nt(dkb, wk_ref[...]) + _nt(dv_ref[...].astype(BF16), wv_ref[...])
        xv = x_ref[...]
        r = _rstd(xv)
        dx_ref[...] = dres_ref[...] + _rms_bwd(dh1, xv, r, gmix_ref[...]) + _rms_bwd(dhkv, xv, r, gkv_ref[...])
        dgmix_ref[...] += _colsum8(dh1 * xv * r)
        dgkv_ref[...] += _colsum8(dhkv * xv * r)

        @pl.when(i == nt - 1)
        def _():
            dgq_ref[...] = fold_heads(accq[...])
            dgk_ref[...] = fold_heads(acck[...])

    tile = pl.BlockSpec((PROJ_TS, D), lambda i: (i, 0))
    behind = pl.BlockSpec((PROJ_TS, D), lambda i: (i + lead, 0))
    row = pl.BlockSpec((1, D), lambda i: (0, 0))
    wspec = lambda: pl.BlockSpec((D, D), lambda i: (0, 0), pipeline_mode=pl.Buffered(1))
    acc = pl.BlockSpec((8, D), lambda i: (0, 0))
    small = pl.BlockSpec((8, 128), lambda i: (0, 0))
    bf = jax.ShapeDtypeStruct((S, D), BF16)
    return _call_hosting(
        body, ex, _grid_phases((nt,)), name="qkv_input_bwd", grid=(nt,),
        in_specs=[tile, tile, row, behind, tile, row, behind, wspec(), wspec(), wspec(), tile, tile, row, row],
        out_specs=[tile, tile, tile, acc, acc, small, small],
        out_shape=[jax.ShapeDtypeStruct((S, D), F32), bf, bf, jax.ShapeDtypeStruct((8, D), F32),
                   jax.ShapeDtypeStruct((8, D), F32), jax.ShapeDtypeStruct((8, 128), F32),
                   jax.ShapeDtypeStruct((8, 128), F32)],
        scratch_shapes=[pltpu.VMEM((8, D), F32), pltpu.VMEM((8, D), F32)],
        args=(dq, qpre, gq, dkp, kpre, gk, dvp, wq, wk, wv, dres, x, g_mix, g_kv),
        compiler_params=_params("arbitrary"),
    )


def _toeplitz_from_table(table):
    far = jnp.broadcast_to(table[:, N_REL - 1:], (N_HEADS, PADK - MAX_REL + 1))
    near = table[:, N_REL - 2::-1]
    past = jnp.broadcast_to(table[:, 0:1], (N_HEADS, MAX_REL))
    wrap = jnp.broadcast_to(table[:, N_REL - 1:], (N_HEADS, TOEP - (PADK + 2 * MAX_REL + 1)))
    return jnp.concatenate([far, near, past, wrap], axis=1).reshape(N_HEADS, 1, TOEP)


def _table_grad_from_toeplitz(dtp, seg):
    lo = PADK - MAX_REL + 1
    near = dtp[:, lo + N_REL - 3:lo - 1:-1]
    return jnp.concatenate([seg[:, 1:2], near, seg[:, 0:1]], axis=1)


def _bias_band(tp):
    def body(tp_ref, out_ref):
        bv = pltpu.roll(jnp.broadcast_to(tp_ref[0], (QB, TOEP)), 0, axis=1, stride=1, stride_axis=0)
        out_ref[0] = jnp.where(_band_mask(), bv[:, 0:KB], NEG_INF)

    return _call(
        body, name="bias_band", grid=(N_HEADS,),
        in_specs=[pl.BlockSpec((1, 1, TOEP), lambda h: (h, 0, 0))],
        out_specs=pl.BlockSpec((1, QB, KB), lambda h: (h, 0, 0)),
        out_shape=jax.ShapeDtypeStruct((N_HEADS, QB, KB), F32),
        compiler_params=_params("parallel"),
    )(tp)


def _bias_grad(dband, after):
    lo, hi = PADK - MAX_REL + 1, PADK + MAX_REL

    def body(db_ref, _, dtp_ref, seg_ref):
        bv = jnp.concatenate([db_ref[0], jnp.zeros((QB, TOEP - KB), F32)], axis=1)
        row = lax.broadcasted_iota(jnp.int32, (QB, TOEP), 0)
        k = 1
        while k < QB:
            bv = jnp.where((row & k) != 0, pltpu.roll(bv, TOEP - k, axis=1), bv)
            k *= 2
        col = jnp.sum(bv, axis=0, keepdims=True)
        dtp_ref[0] = col
        u = lax.broadcasted_iota(jnp.int32, (1, TOEP), 1)
        far = jnp.sum(jnp.where((u < lo) | (u > hi + MAX_REL), col, 0.0))
        past = jnp.sum(jnp.where((u >= hi) & (u <= hi + MAX_REL), col, 0.0))
        lane = lax.broadcasted_iota(jnp.int32, (1, 128), 1)
        seg_ref[0] = jnp.where(lane == 0, far, jnp.where(lane == 1, past, 0.0))

    return _call(
        body, name="bias_grad", grid=(N_HEADS,),
        in_specs=[pl.BlockSpec((1, QB, KB), lambda h: (h, 0, 0)), ANY],
        out_specs=[pl.BlockSpec((1, 1, TOEP), lambda h: (h, 0, 0)), pl.BlockSpec((1, 1, 128), lambda h: (h, 0, 0))],
        out_shape=[jax.ShapeDtypeStruct((N_HEADS, 1, TOEP), F32), jax.ShapeDtypeStruct((N_HEADS, 1, 128), F32)],
        compiler_params=_params("parallel"),
    )(dband, after)


N_QB = S // QB
HEADS_PER_STEP = 4
ATT_LANES = HEADS_PER_STEP * HEAD
N_HG = D // ATT_LANES


def _band_mask():
    qc = lax.broadcasted_iota(jnp.int32, (QB, KB), 0) // CHUNK
    kc = lax.broadcasted_iota(jnp.int32, (QB, KB), 1) // CHUNK
    return (kc >= qc) & (kc <= qc + LEFT)


def _half_scale(hh, scale):
    lane = lax.broadcasted_iota(jnp.int32, (1, 128), 1)
    return jnp.where((lane < HEAD) == (hh == 0), scale, 0.0).astype(BF16)


def _probs(qh, kb, bias, first_key):
    sc = _nt(qh, kb) + bias
    if first_key is not None:
        sc = jnp.where(lax.broadcasted_iota(jnp.int32, (QB, KB), 1) >= first_key, sc, NEG_INF)
    e = jnp.exp(sc - jnp.max(sc, axis=-1, keepdims=True))
    return e * (1.0 / jnp.sum(e, axis=-1, keepdims=True))


def _by_padding(cb, compute):
    @pl.when(cb < PADK // QB)
    def _():
        compute(PADK - cb * QB)

    @pl.when(cb >= PADK // QB)
    def _():
        compute(None)


def _attn_fwd(q, kp, vp, bias, ex=None):
    def body(q_ref, k_ref, v_ref, b_ref, o_ref):
        cb = pl.program_id(1)
        band = pl.ds(pl.multiple_of(cb * QB, QB), KB)
        low = lax.broadcasted_iota(jnp.int32, (QB, 128), 1) < HEAD

        def compute(first_key):
            for pair in range(HEADS_PER_STEP // 2):
                lanes = pl.ds(pair * 128, 128)
                kb, vb, qv = k_ref[band, lanes], v_ref[band, lanes], q_ref[:, lanes]
                outs = []
                for hh in range(2):
                    pb = _probs(qv * _half_scale(hh, ATTN_SCALE), kb, b_ref[2 * pair + hh], first_key).astype(BF16)
                    outs.append(_nn(pb, vb))
                o_ref[:, lanes] = jnp.where(low, outs[0], outs[1]).astype(BF16)

        _by_padding(cb, compute)

    qspec = pl.BlockSpec((QB, ATT_LANES), lambda hg, cb: (cb, hg))
    kspec = pl.BlockSpec((PADK + S, ATT_LANES), lambda hg, cb: (0, hg))
    return _call_hosting(
        body, ex, _grid_phases((N_HG, N_QB), 0.85), name="attn_fwd", grid=(N_HG, N_QB),
        in_specs=[qspec, kspec, kspec, pl.BlockSpec((HEADS_PER_STEP, QB, KB), lambda hg, cb: (hg, 0, 0))],
        out_specs=[qspec], out_shape=[jax.ShapeDtypeStruct((S, D), BF16)], scratch_shapes=[],
        args=(q, kp, vp, bias), compiler_params=_params("arbitrary", "arbitrary"),
    )


def _attn_bwd(q, kp, vp, bias, do, ex=None):
    def body(q_ref, k_ref, v_ref, b_ref, do_ref, dq_ref, dk_ref, dv_ref, db_ref):
        cb = pl.program_id(1)

        @pl.when(cb == 0)
        def _():
            dk_ref[...] = jnp.zeros(dk_ref.shape, F32)
            dv_ref[...] = jnp.zeros(dv_ref.shape, F32)
            db_ref[...] = jnp.zeros(db_ref.shape, F32)

        band = pl.ds(pl.multiple_of(cb * QB, QB), KB)
        low = lax.broadcasted_iota(jnp.int32, (QB, 128), 1) < HEAD

        def compute(first_key):
            for pair in range(HEADS_PER_STEP // 2):
                lanes = pl.ds(pair * 128, 128)
                kb, vb = k_ref[band, lanes], v_ref[band, lanes]
                qv, dov = q_ref[:, lanes], do_ref[:, lanes]
                dq = jnp.zeros((QB, 128), F32)
                dkb = jnp.zeros((KB, 128), F32)
                dvb = jnp.zeros((KB, 128), F32)
                for hh in range(2):
                    sel = low if hh == 0 else jnp.logical_not(low)
                    doh = dov * _half_scale(hh, 1.0)
                    p = _probs(qv * _half_scale(hh, ATTN_SCALE), kb, b_ref[2 * pair + hh], first_key)
                    dp = _nt(doh, vb)
                    dvb = dvb + _tn(p.astype(BF16), doh)
                    ds = p * (dp - jnp.sum(dp * p, axis=-1, keepdims=True))
                    db_ref[2 * pair + hh] += ds
                    dsb = (ds * ATTN_SCALE).astype(BF16)
                    dq = dq + jnp.where(sel, _nn(dsb, kb), 0.0)
                    dkb = dkb + _tn(dsb, qv * _half_scale(hh, 1.0))
                dq_ref[:, lanes] = dq
                dk_ref[band, lanes] += dkb
                dv_ref[band, lanes] += dvb

        _by_padding(cb, compute)

    qspec = pl.BlockSpec((QB, ATT_LANES), lambda hg, cb: (cb, hg))
    kspec = pl.BlockSpec((PADK + S, ATT_LANES), lambda hg, cb: (0, hg))
    bspec = pl.BlockSpec((HEADS_PER_STEP, QB, KB), lambda hg, cb: (hg, 0, 0))
    kf = jax.ShapeDtypeStruct((PADK + S, D), F32)
    return _call_hosting(
        body, ex, _grid_phases((N_HG, N_QB)), name="attn_bwd", grid=(N_HG, N_QB),
        in_specs=[qspec, kspec, kspec, bspec, qspec],
        out_specs=[qspec, kspec, kspec, bspec],
        out_shape=[jax.ShapeDtypeStruct((S, D), F32), kf, kf, jax.ShapeDtypeStruct((N_HEADS, QB, KB), F32)],
        scratch_shapes=[], args=(q, kp, vp, bias, do), compiler_params=_params("arbitrary", "arbitrary"),
    )


def _adam_math(w, g, m, v):
    m = ADAM_B1 * m + (1.0 - ADAM_B1) * g
    v = ADAM_B2 * v + (1.0 - ADAM_B2) * (g * g)
    m_hat = m / (1.0 - ADAM_B1 ** ADAM_STEP)
    v_hat = v / (1.0 - ADAM_B2 ** ADAM_STEP)
    delta = -ADAM_LR * (m_hat / (jnp.sqrt(v_hat) + ADAM_EPS) + ADAM_WD * w)
    return delta, m, v


def _row_tile(r):
    for cand in (256, 176, 128, 64, 32, 16, 8):
        if r % cand == 0:
            return cand
    return r


def _adam_layer(w, parts, m, v, layer, name, prev=None):
    nl, r, c = w.shape
    count = parts.shape[0]
    tr = _row_tile(r)

    def body(w_ref, g_ref, m_ref, v_ref, *rest):
        go_ref, d_ref, nm_ref, nv_ref = rest[-4:]
        gv = g_ref[0].astype(F32)
        for q in range(1, count):
            gv = gv + g_ref[q].astype(F32)
        delta, nm, nv = _adam_math(w_ref[0], gv, m_ref[0], v_ref[0])
        go_ref[0] = gv
        d_ref[0] = delta
        nm_ref[0] = nm
        nv_ref[0] = nv

    lspec = pl.BlockSpec((1, tr, c), lambda i: (layer, i, 0))
    sd = jax.ShapeDtypeStruct((nl, r, c), F32)
    extra = list(prev) if prev is not None else []
    return _call(
        body, name=name, grid=(r // tr,),
        in_specs=[lspec, pl.BlockSpec((count, tr, c), lambda i: (0, i, 0)), lspec, lspec] + [ANY] * len(extra),
        out_specs=[lspec] * 4, out_shape=[sd] * 4,
        input_output_aliases={4 + t: t for t in range(len(extra))},
        compiler_params=_params("parallel"),
    )(w, parts, m, v, *extra)


def _adam(w, g, m, v, name):
    r, c = w.shape
    tr = _row_tile(r)
    count = g.shape[0] if g.ndim == 3 else 0

    def body(w_ref, g_ref, m_ref, v_ref, go_ref, d_ref, nm_ref, nv_ref):
        if count:
            gv = g_ref[0].astype(F32)
            for q in range(1, count):
                gv = gv + g_ref[q].astype(F32)
        else:
            gv = g_ref[...]
        delta, nm, nv = _adam_math(w_ref[...], gv, m_ref[...], v_ref[...])
        go_ref[...] = gv
        d_ref[...] = delta
        nm_ref[...] = nm
        nv_ref[...] = nv

    spec = pl.BlockSpec((tr, c), lambda i: (i, 0))
    gspec = pl.BlockSpec((count, tr, c), lambda i: (0, i, 0)) if count else spec
    sd = jax.ShapeDtypeStruct((r, c), F32)
    return _call(
        body, name=name, grid=(r // tr,), in_specs=[spec, gspec, spec, spec], out_specs=[spec] * 4,
        out_shape=[sd] * 4, compiler_params=_params("parallel"),
    )(w, g, m, v)


def _pad16(rows):
    return jnp.pad(rows, ((0, 16 - rows.shape[0]), (0, 0)))


def kernel(x, norm_mix_g, norm_ffn_g, pool_w, pool_b, pool_scale, kv_norm_g, w_k, w_v, k_norm_g, w_q, q_norm_g, rel_bias, w_o, w_gate, w_up, w_down, loss_target, m_norm_mix_g, m_norm_ffn_g, m_pool_w, m_pool_b, m_pool_scale, m_kv_norm_g, m_w_k, m_w_v, m_k_norm_g, m_w_q, m_q_norm_g, m_rel_bias, m_w_o, m_w_gate, m_w_up, m_w_down, v_norm_mix_g, v_norm_ffn_g, v_pool_w, v_pool_b, v_pool_scale, v_kv_norm_g, v_w_k, v_w_v, v_k_norm_g, v_w_q, v_q_norm_g, v_rel_bias, v_w_o, v_w_gate, v_w_up, v_w_down):
    assert x.shape == (1, S, D) and w_gate.shape == (2, D, F_SHARD) and w_k.shape == (D_SHARD, D)
    xin, target = x[0], loss_target[0]

    ffn_shards = [[w_gate[layer].T.astype(BF16), w_up[layer].T.astype(BF16), w_down[layer].astype(BF16)]
                  for layer in range(2)]
    att_shards = [w_k.astype(BF16), w_v.astype(BF16), w_q[0].astype(BF16), w_o[0].astype(BF16)]
    pool_shard = pool_w[0].astype(BF16).reshape(N_GROUPS * POOL_SHARD, GROUP)
    small = jnp.concatenate([pool_b[0].reshape(1, N_GROUPS * POOL_SHARD), pool_scale], axis=1)

    full0 = _run_exchange(_relay_gather_exchange(ffn_shards[0] + [pool_shard, _pad16(small)]), "gather_layer0")
    ffn_w0 = [a.reshape(F, D) for a in full0[:3]]
    pw_f = full0[3].reshape(N_DEV, N_GROUPS, POOL_SHARD, GROUP).transpose(1, 0, 2, 3).reshape(N_GROUPS, GROUP, GROUP)
    small_f = full0[4][:, 0, :]
    pb_f = small_f[:, :N_GROUPS * POOL_SHARD].reshape(N_DEV, N_GROUPS, POOL_SHARD).transpose(1, 0, 2).reshape(1, D)
    ps_f = small_f[:, N_GROUPS * POOL_SHARD:].reshape(1, D)

    g_mix0, g_mix1 = norm_mix_g[0:1], norm_mix_g[1:2]
    g_ffn0, g_ffn1 = norm_ffn_g[0:1], norm_ffn_g[1:2]
    g_kv = kv_norm_g.reshape(1, D)
    gk_t = jnp.tile(k_norm_g.reshape(1, HEAD), (1, N_HEADS))
    gq_t = jnp.tile(q_norm_g.reshape(1, HEAD), (1, N_HEADS))

    x1, diff = _pool_fwd(xin, g_mix0, pw_f, pb_f, ps_f)
    (x2, hf0, gg0, uu0), full_att = _ffn_fwd(x1, g_ffn0, *ffn_w0, name="ffn_fwd", ex=_relay_gather_exchange(att_shards))
    wk_f, wv_f, wq_f, wo_f = [a.reshape(D, D) for a in full_att]
    (hkv, h1, kpre, qpre, kp, vp, qq), full1_gate = _kvq_proj(x2, g_kv, g_mix1, wk_f, wv_f, wq_f, gk_t, gq_t,
                                                                ex=_relay_gather_exchange(ffn_shards[1][:1]))
    bias = _bias_band(_toeplitz_from_table(rel_bias[0]))
    (att,), full1_rest = _attn_fwd(qq, kp, vp, bias, ex=_relay_gather_exchange(ffn_shards[1][1:]))
    ffn_w1 = [a.reshape(F, D) for a in full1_gate + full1_rest]
    x3 = _mm(att, wo_f, "nn", F32, "attn_out", add=x2)
    (dx4, loss_rows, hf1, gg1, uu1), _ = _ffn_fwd(x3, g_ffn1, *ffn_w1, name="ffn_fwd_loss", target=target)

    def blocks(dw):
        return dw.reshape(N_DEV, dw.shape[0] // N_DEV, dw.shape[1])

    (dgg1, duu1, dwg1, dwu1, dwd1), _ = _ffn_bwd_weights(dx4, hf1, gg1, uu1, ffn_w1[2], name="ffn_bwd1")
    (dx3, dg_ffn1), _ = _ffn_bwd_input(dx4, x3, g_ffn1, [dgg1, duu1], ffn_w1[:2], "ffn_dx1")
    datt = _mm(dx3, wo_f, "nt", BF16, "d_attn")
    dwo = _mm(att, dx3, "tn", BF16, "d_wo")
    parts1 = [blocks(dw) for dw in (dwg1, dwu1, dwd1, dwo)]
    (dq, dkp, dvp, dband), stage1 = _attn_bwd(qq, kp, vp, bias, datt, ex=_pair_exchange(parts1))
    both1 = _pair_add(parts1, stage1, "pair_add_ffn1")
    flight1, chip1, land1, _ = _chip_exchange_start(both1[:4], both1[4:], dq, "scatter_ffn1_start")
    (dx2, dqpre, dkpre, dg_mix1, dg_kv, dgq, dgk), _ = _qkv_input_bwd(
        dq, qpre, gq_t, dkp, kpre, gk_t, dvp, wq_f, wk_f, wv_f, dx3, x2, g_mix1, g_kv)
    dwq = _mm(h1, dqpre, "tn", BF16, "d_wq")
    dwk = _mm(hkv, dkpre, "tn", BF16, "d_wk")
    dwv = _mm(hkv, dvp, "tn", BF16, "d_wv", skip_rows=PADK)
    parts_att = [blocks(dw) for dw in (dwk, dwv, dwq)]
    (dgg0, duu0, *dw0), stage_att = _ffn_bwd_weights(
        dx2, hf0, gg0, uu0, ffn_w0[2], name="ffn_bwd0", ex=_pair_exchange(parts_att))
    both_att = _pair_add(parts_att, stage_att, "pair_add_att")
    flight_att, chip_att, land_att, token_att = _chip_exchange_start(both_att[:3], both_att[3:], dx2,
                                                                     "scatter_att_start")
    parts0 = [blocks(dw) for dw in dw0]
    (dx1, dg_ffn0), stage0 = _ffn_bwd_input(dx2, x1, g_ffn0 + token_att[0:1, 0:1], [dgg0, duu0], ffn_w0[:2],
                                            "ffn_dx0", ex=_pair_exchange(parts0))
    both = _pair_add(parts0, stage0, "pair_add_ffn0")
    flight, chip0, land0, token = _chip_exchange_start(both[:3], both[3:], dx1, "scatter_ffn0_start")
    (grad_x, dpw, db_rows, ds_rows, dg_mix0), _ = _pool_bwd(dx1, xin, diff, g_mix0 + token[0:1, 0:1], pw_f, pb_f, ps_f)
    dtp, seg = _bias_grad(dband, dpw)

    weights = dict(norm_mix_g=norm_mix_g, norm_ffn_g=norm_ffn_g, pool_w=pool_w, pool_b=pool_b,
                   pool_scale=pool_scale, kv_norm_g=kv_norm_g, w_k=w_k, w_v=w_v, k_norm_g=k_norm_g, w_q=w_q,
                   q_norm_g=q_norm_g, rel_bias=rel_bias, w_o=w_o, w_gate=w_gate, w_up=w_up, w_down=w_down)
    mom1 = dict(norm_mix_g=m_norm_mix_g, norm_ffn_g=m_norm_ffn_g, pool_w=m_pool_w, pool_b=m_pool_b,
                pool_scale=m_pool_scale, kv_norm_g=m_kv_norm_g, w_k=m_w_k, w_v=m_w_v, k_norm_g=m_k_norm_g,
                w_q=m_w_q, q_norm_g=m_q_norm_g, rel_bias=m_rel_bias, w_o=m_w_o, w_gate=m_w_gate, w_up=m_w_up,
                w_down=m_w_down)
    mom2 = dict(norm_mix_g=v_norm_mix_g, norm_ffn_g=v_norm_ffn_g, pool_w=v_pool_w, pool_b=v_pool_b,
                pool_scale=v_pool_scale, kv_norm_g=v_kv_norm_g, w_k=v_w_k, w_v=v_w_v, k_norm_g=v_k_norm_g,
                w_q=v_w_q, q_norm_g=v_q_norm_g, rel_bias=v_rel_bias, w_o=v_w_o, w_gate=v_w_gate, w_up=v_w_up,
                w_down=v_w_down)
    names = list(weights)
    grads, deltas, new_m, new_v = {}, {}, {}, {}

    def adam_flat(nm, parts):
        shape = weights[nm].shape
        flat = lambda a: a.reshape(-1, shape[-1])
        done = _adam(flat(weights[nm]), parts, flat(mom1[nm]), flat(mom2[nm]), "adam_" + nm)
        grads[nm], deltas[nm], new_m[nm], new_v[nm] = [a.reshape(shape) for a in done]

    ffn_names = ("w_gate", "w_up", "w_down")
    def ffn_view(nm, a):
        return a if nm == "w_down" else a.transpose(0, 2, 1)

    recv1 = _chip_exchange_wait(flight1, chip1, land1, dtp, "scatter_ffn1_wait")
    layer1 = {nm: _adam_layer(ffn_view(nm, weights[nm]), g, ffn_view(nm, mom1[nm]), ffn_view(nm, mom2[nm]), 1,
                              "adam1_" + nm)
              for nm, g in zip(ffn_names, recv1)}
    adam_flat("w_o", recv1[3])
    done1 = [layer1[nm][1] for nm in ffn_names] + [deltas["w_o"]]
    recv_att = _chip_exchange_wait(flight_att, chip_att, land_att, done1, "scatter_att_wait")
    for nm, parts in zip(("w_k", "w_v", "w_q"), recv_att):
        adam_flat(nm, parts)

    dpw_blocks = dpw.reshape(N_GROUPS, N_DEV, POOL_SHARD, GROUP).transpose(1, 0, 2, 3)
    dpw_blocks = dpw_blocks.reshape(N_DEV, N_GROUPS * POOL_SHARD, GROUP).astype(BF16)
    misc = jnp.concatenate([dgk[0:1, 0:HEAD], dgq[0:1, 0:HEAD], loss_rows[0:1, 0:1],
                            seg[:, 0, 0].reshape(1, N_HEADS), seg[:, 0, 1].reshape(1, N_HEADS)], axis=1)
    misc = jnp.pad(misc, ((0, 0), (0, D - misc.shape[1])))
    vec_rows = jnp.concatenate([dg_mix0[0:1], dg_mix1[0:1], dg_ffn0[0:1], dg_ffn1[0:1], dg_kv[0:1],
                                db_rows[0:1], ds_rows[0:1], misc], axis=0)
    pack = jnp.concatenate([vec_rows, dtp.reshape(N_HEADS, TOEP)], axis=0)
    tot, recv_pool = _small_exchanges(pack, dpw_blocks, "small_exchanges", after=deltas["w_q"])

    loss = tot[7, 2 * HEAD]
    seg_tot = jnp.stack([tot[7, 2 * HEAD + 1:2 * HEAD + 1 + N_HEADS],
                         tot[7, 2 * HEAD + 1 + N_HEADS:2 * HEAD + 1 + 2 * N_HEADS]], axis=1)
    me = 4 * lax.axis_index("x") + 2 * lax.axis_index("y") + lax.axis_index("c")
    g_pool_b = lax.dynamic_slice_in_dim(tot[5].reshape(N_GROUPS, GROUP), me * POOL_SHARD, POOL_SHARD, axis=1)
    grads.update(
        norm_mix_g=tot[0:2], norm_ffn_g=tot[2:4], kv_norm_g=tot[4], k_norm_g=tot[7, 0:HEAD],
        q_norm_g=tot[7, HEAD:2 * HEAD].reshape(1, HEAD),
        rel_bias=_table_grad_from_toeplitz(tot[8:8 + N_HEADS], seg_tot).reshape(1, N_HEADS, N_REL),
        pool_b=g_pool_b.reshape(1, N_GROUPS, POOL_SHARD),
        pool_scale=lax.dynamic_slice_in_dim(tot[6:7], me * D_SHARD, D_SHARD, axis=1))
    adam_flat("pool_w", recv_pool)
    small_names = [nm for nm in names if nm not in ffn_names + ("w_k", "w_v", "w_q", "w_o", "pool_w")]

    def pack_small(tree):
        cols = []
        for nm in small_names:
            flat = tree[nm].reshape(-1)
            cols.append(jnp.pad(flat, (0, -flat.shape[0] % 1024)))
        return jnp.concatenate(cols).reshape(-1, 128)

    _, dl, m1, m2 = _adam(pack_small(weights), pack_small(grads), pack_small(mom1), pack_small(mom2), "adam_small")

    def unpack_small(packed, out):
        flat, off = packed.reshape(-1), 0
        for nm in small_names:
            size = weights[nm].size
            out[nm] = flat[off:off + size].reshape(weights[nm].shape)
            off += size + (-size % 1024)

    unpack_small(dl, deltas)
    unpack_small(m1, new_m)
    unpack_small(m2, new_v)

    recv0 = _chip_exchange_wait(flight, chip0, land0, dl, "scatter_ffn0_wait")
    for nm, g in zip(ffn_names, recv0):
        done = _adam_layer(ffn_view(nm, weights[nm]), g, ffn_view(nm, mom1[nm]), ffn_view(nm, mom2[nm]), 0,
                           "adam0_" + nm, prev=layer1[nm])
        grads[nm], deltas[nm], new_m[nm], new_v[nm] = [ffn_view(nm, a) for a in done]

    return (loss, grad_x[None], *[grads[nm] for nm in names], *[deltas[nm] for nm in names],
            *[new_m[nm] for nm in names], *[new_v[nm] for nm in names])
```

```python
import functools

import jax
import jax.numpy as jnp
from jax import lax
from jax.experimental import pallas as pl
from jax.experimental.pallas import tpu as pltpu

F32 = jnp.float32
BF16 = jnp.bfloat16
MESH_ID = pl.DeviceIdType.MESH

N_DEV = 8
S = 2048
D = 1024
F = 2816
F_SHARD = F // N_DEV
D_SHARD = D // N_DEV
N_GROUPS = 4
GROUP = D // N_GROUPS
POOL_SHARD = GROUP // N_DEV
MAX_WIN = 16
HEAD = 64
N_HEADS = D // HEAD
CHUNK = 64
LEFT = 8
QB = 4 * CHUNK
KB = QB + LEFT * CHUNK
PADK = LEFT * CHUNK
TOEP = 1024
N_REL = 257
MAX_REL = 128
EPS = 1e-6
NEG_INF = -1e30
ATTN_SCALE = HEAD ** -0.5

ADAM_LR = 0.001
ADAM_B1 = 0.9
ADAM_B2 = 0.999
ADAM_EPS = 1e-08
ADAM_WD = 0.01
ADAM_STEP = 10

VMEM_LIMIT = 52 * 1024 * 1024

ANY = pl.BlockSpec(memory_space=pl.ANY)
VMEM = pl.BlockSpec(memory_space=pltpu.VMEM)


def _call(body, **kw):
    return pl.pallas_call(body, **kw)


def _params(*sem):
    return pltpu.CompilerParams(dimension_semantics=sem, vmem_limit_bytes=VMEM_LIMIT)


def _dot(a, b, dims):
    return lax.dot_general(a, b, (dims, ((), ())), preferred_element_type=F32)


def _nn(a, b):
    return _dot(a, b, ((1,), (0,)))


def _nt(a, b):
    return _dot(a, b, ((1,), (1,)))


def _tn(a, b):
    return _dot(a, b, ((0,), (0,)))


def _rstd(x):
    return lax.rsqrt(jnp.mean(x * x, axis=-1, keepdims=True) + EPS)


def _rms_bwd(dh, x, r, g):
    gd = dh * g
    return r * gd - x * (r * r * r) * jnp.mean(gd * x, axis=-1, keepdims=True)


def _colsum8(v):
    return jnp.broadcast_to(jnp.sum(v, axis=0, keepdims=True), (8, v.shape[1]))


def _seg_sum(v):
    r = lax.broadcasted_iota(jnp.int32, (128, 128), 0) // HEAD
    c = lax.broadcasted_iota(jnp.int32, (128, 128), 1) // HEAD
    ones = jnp.where(r == c, 1.0, 0.0).astype(BF16)
    out = []
    for blk in range(v.shape[1] // 128):
        part = v[:, blk * 128:(blk + 1) * 128]
        hi = part.astype(BF16)
        rest = part - hi.astype(F32)
        mid = rest.astype(BF16)
        lo = (rest - mid.astype(F32)).astype(BF16)
        out.append(_nn(hi, ones) + _nn(mid, ones) + _nn(lo, ones))
    return jnp.concatenate(out, axis=1)


def _place():
    return lax.axis_index("x"), lax.axis_index("y"), lax.axis_index("c")


class _Exchange:
    def __init__(self, ins, out_shape, sems, start, mid, finish, early=None):
        self.ins, self.out_shape, self.sems = list(ins), list(out_shape), list(sems)
        self.start, self.mid, self.finish = start, mid, finish
        self.early = early if early is not None else (lambda ins, outs, sems: None)


def _relay_gather_exchange(shards):
    n = len(shards)

    def tools(ins, outs, sems):
        send_sems, recv_sems, local_sems = sems
        x, y, c = _place()
        me, sib = (x, y, c), (x, y, 1 - c)
        xn, yn, dg = (1 - x, y, c), (x, 1 - y, c), (1 - x, 1 - y, c)
        other = lambda dev: (dev[0], dev[1], 1 - c)

        def piece(k, ref, half):
            rows = shards[k].shape[0] // 2
            return ref if half is None else ref.at[pl.ds(half * rows, rows)]

        def held(k, dev, half):
            return piece(k, outs[k].at[4 * dev[0] + 2 * dev[1] + dev[2]], half)

        def copy(k, s, dev, half, to, own=False):
            return pltpu.make_async_remote_copy(
                src_ref=piece(k, ins[k], half) if own else held(k, dev, half), dst_ref=held(k, dev, half),
                send_sem=send_sems.at[k, s], recv_sem=recv_sems.at[k, s], device_id=to, device_id_type=MESH_ID)

        def mine(k):
            return pltpu.make_async_copy(ins[k], held(k, me, None), local_sems.at[k])

        def sent(k):
            return [copy(k, 0, me, None, sib, own=True), copy(k, 1, me, 0, xn, own=True), copy(k, 2, me, 1, yn, own=True),
                    copy(k, 3, me, 0, yn, own=True), copy(k, 4, me, 1, xn, own=True)]

        relays = [(1, xn, 0, [(5, yn), (7, sib)]), (2, yn, 1, [(6, xn), (9, sib)]), (3, yn, 0, [(10, sib)]),
                  (4, xn, 1, [(8, sib)]), (6, dg, 1, [(11, sib)]), (5, dg, 0, [(12, sib)])]
        from_sibling = [(0, other(me), None), (7, other(xn), 0), (8, other(xn), 1), (9, other(yn), 1),
                        (10, other(yn), 0), (11, other(dg), 1), (12, other(dg), 0)]
        return mine, sent, copy, relays, from_sibling, me

    def start(ins, outs, sems):
        mine, sent, _, _, _, _ = tools(ins, outs, sems)
        for k in range(n):
            mine(k).start()
        for k in range(n):
            for cp in sent(k):
                cp.start()

    def pass_on(ins, outs, sems, which):
        _, _, copy, relays, _, me = tools(ins, outs, sems)
        for k in range(n):
            for s, dev, half, onward in relays[which]:
                copy(k, s, dev, half, me).wait_recv()
                for s2, to in onward:
                    copy(k, s2, dev, half, to).start()

    def early(ins, outs, sems):
        pass_on(ins, outs, sems, slice(0, 2))

    def mid(ins, outs, sems):
        pass_on(ins, outs, sems, slice(2, None))

    def finish(ins, outs, sems):
        mine, sent, copy, relays, from_sibling, me = tools(ins, outs, sems)
        for k in range(n):
            for s, dev, half in from_sibling:
                copy(k, s, dev, half, me).wait_recv()
        for k in range(n):
            for cp in sent(k):
                cp.wait_send()
            for _, dev, half, onward in relays:
                for s2, to in onward:
                    copy(k, s2, dev, half, to).wait_send()
            mine(k).wait()

    return _Exchange(
        shards, [jax.ShapeDtypeStruct((N_DEV,) + a.shape, a.dtype) for a in shards],
        [pltpu.SemaphoreType.DMA((n, 13)), pltpu.SemaphoreType.DMA((n, 13)), pltpu.SemaphoreType.DMA((n,))],
        start, mid, finish, early)


def _peer(x, y, c, m):
    px = 1 - x if m & 4 else x
    py = 1 - y if m & 2 else y
    pc = 1 - c if m & 1 else c
    return px, py, pc


N_CHIPS = N_DEV // 2


def _pair_exchange(parts):
    n = len(parts)

    def copies(ins, outs, sems):
        send_sems, recv_sems = sems
        x, y, c = _place()
        return [pltpu.make_async_remote_copy(
            src_ref=ins[k].at[2 * q + 1 - c], dst_ref=outs[k].at[q], send_sem=send_sems.at[k, q],
            recv_sem=recv_sems.at[k, q], device_id=(x, y, 1 - c), device_id_type=MESH_ID)
            for k in range(n) for q in range(N_CHIPS)]

    def start(ins, outs, sems):
        for cp in copies(ins, outs, sems):
            cp.start()

    def mid(ins, outs, sems):
        pass

    def finish(ins, outs, sems):
        for cp in copies(ins, outs, sems):
            cp.wait_recv()
        for cp in copies(ins, outs, sems):
            cp.wait_send()

    return _Exchange(
        parts, [jax.ShapeDtypeStruct((N_CHIPS,) + a.shape[1:], a.dtype) for a in parts],
        [pltpu.SemaphoreType.DMA((n, N_CHIPS)), pltpu.SemaphoreType.DMA((n, N_CHIPS))], start, mid, finish)


def _pair_add(parts, stage, name):
    n = len(parts)
    place = jnp.stack([lax.axis_index("c"), 2 * lax.axis_index("x") + lax.axis_index("y")]).astype(jnp.int32)

    def body(place_ref, *refs):
        for k in range(n):
            mine, theirs = refs[k], refs[n + k]
            total = (mine[0, 0].astype(F32) + theirs[0].astype(F32)).astype(BF16)
            refs[2 * n + k][0] = total

            @pl.when(pl.program_id(1) == place_ref[1])
            def _():
                refs[3 * n + k][0] = total

    in_specs, out_specs, land_specs = [], [], []
    for a in parts:
        _, r, cdim = a.shape
        in_specs.append(pl.BlockSpec((1, 1, r // 2, cdim), lambda i, q, place_ref: (q, place_ref[0], i, 0)))
    for a in parts:
        _, r, cdim = a.shape
        in_specs.append(pl.BlockSpec((1, r // 2, cdim), lambda i, q, place_ref: (q, i, 0)))
        out_specs.append(pl.BlockSpec((1, r // 2, cdim), lambda i, q, place_ref: (q, i, 0)))
        land_specs.append(pl.BlockSpec((1, r // 2, cdim), lambda i, q, place_ref: (place_ref[1], i, 0)))
    return list(_call(
        body, name=name,
        grid_spec=pltpu.PrefetchScalarGridSpec(num_scalar_prefetch=1, grid=(2, N_CHIPS), in_specs=in_specs,
                                               out_specs=out_specs + land_specs),
        out_shape=[jax.ShapeDtypeStruct(s.shape, BF16) for s in stage] * 2,
        compiler_params=_params("arbitrary", "arbitrary"),
    )(place, *[a.reshape((N_CHIPS, 2) + a.shape[1:]) for a in parts], *stage))


HBM = pl.BlockSpec(memory_space=pltpu.HBM)
SEMAPHORES = pl.BlockSpec(memory_space=pltpu.SEMAPHORE)


def _chip_copies(srcs, lands, send_sems, recv_sems, mine_is_dst):
    x, y, c = _place()
    me = 2 * x + y
    out = []
    for m in range(1, N_CHIPS):
        px, py, _ = _peer(x, y, c, 2 * m)
        peer = 2 * px + py
        for k in range(len(srcs)):
            pair = k * (N_CHIPS - 1) + m - 1
            out.append(pltpu.make_async_remote_copy(
                src_ref=srcs[k].at[peer], dst_ref=lands[k].at[me if mine_is_dst else peer],
                send_sem=send_sems[pair], recv_sem=recv_sems[pair],
                device_id=(px, py, c), device_id_type=MESH_ID))
    return out


def _chip_exchange_start(chip_parts, lands, after, name):
    n = len(chip_parts)
    pairs = n * (N_CHIPS - 1)

    def body(*refs):
        srcs, zones = refs[:n], refs[n:2 * n]
        sems = refs[2 * n + 1:2 * n + 1 + 2 * pairs]
        token = refs[-1]
        for cp in _chip_copies(srcs, zones, sems[:pairs], sems[pairs:], True):
            cp.start()
        token[...] = jnp.zeros(token.shape, F32)

    thru = [pltpu.HBM(a.shape, a.dtype) for a in chip_parts + lands]
    hbm = [pltpu.with_memory_space_constraint(a, pltpu.HBM) for a in chip_parts + lands]
    res = _call(
        body, name=name,
        out_shape=[pltpu.SemaphoreType.DMA(())] * (2 * pairs) + thru + [jax.ShapeDtypeStruct((8, 128), F32)],
        in_specs=[HBM] * (2 * n) + [ANY], out_specs=[SEMAPHORES] * (2 * pairs) + [HBM] * (2 * n) + [VMEM],
        input_output_aliases={i: 2 * pairs + i for i in range(2 * n)},
        compiler_params=pltpu.CompilerParams(has_side_effects=pltpu.SideEffectType.DATAFLOW_SIDE_EFFECTING),
    )(*hbm, after)
    sems, rest = list(res[:2 * pairs]), res[2 * pairs:]
    return sems, list(rest[:n]), list(rest[n:2 * n]), rest[-1]


def _chip_exchange_wait(sems, chip_parts, lands, after, name):
    n = len(chip_parts)
    pairs = n * (N_CHIPS - 1)
    after = list(after) if isinstance(after, (list, tuple)) else [after]

    def body(*refs):
        srcs, zones = refs[:n], refs[n:2 * n]
        sem_refs = refs[2 * n:2 * n + 2 * pairs]
        for cp in _chip_copies(srcs, zones, sem_refs[:pairs], sem_refs[pairs:], False):
            cp.wait_send()
            cp.wait_recv()

    thru = [pltpu.HBM(a.shape, a.dtype) for a in chip_parts + lands]
    res = _call(
        body, name=name, out_shape=thru,
        in_specs=[HBM] * (2 * n) + [SEMAPHORES] * (2 * pairs) + [ANY] * len(after), out_specs=[HBM] * (2 * n),
        input_output_aliases={i: i for i in range(2 * n)},
        compiler_params=pltpu.CompilerParams(has_side_effects=pltpu.SideEffectType.DATAFLOW_SIDE_EFFECTING),
    )(*chip_parts, *lands, *sems, *after)
    return list(res[n:])


def _run_exchange(ex, name, after=None):
    n_in, n_out = len(ex.ins), len(ex.out_shape)
    order = [] if after is None else [after]

    def body(*refs):
        ins, outs = refs[:n_in], refs[n_in + len(order):n_in + len(order) + n_out]
        sems = refs[n_in + len(order) + n_out:]
        ex.start(ins, outs, sems)
        ex.early(ins, outs, sems)
        ex.mid(ins, outs, sems)
        ex.finish(ins, outs, sems)

    return list(_call(body, name=name, out_shape=ex.out_shape, in_specs=[ANY] * (n_in + len(order)),
                      out_specs=[ANY] * n_out, scratch_shapes=ex.sems)(*ex.ins, *order))


def _call_hosting(body, ex, phases, *, in_specs, out_specs, out_shape, scratch_shapes, args, **kw):
    n_in, n_out, n_scr = len(in_specs), len(out_specs), len(scratch_shapes)
    if ex is None:
        res = _call(body, in_specs=in_specs, out_specs=out_specs, out_shape=out_shape,
                    scratch_shapes=scratch_shapes, **kw)(*args)
        return list(res), []
    n_xin, n_xout = len(ex.ins), len(ex.out_shape)

    def hosting(*refs):
        a, b = n_in, n_in + n_xin
        c, d = b + n_out, b + n_out + n_xout
        ins, xins, outs, xouts = refs[:a], refs[a:b], refs[b:c], refs[c:d]
        scr, sems = refs[d:d + n_scr], refs[d + n_scr:]
        first, early, mid, last = phases()

        @pl.when(first)
        def _():
            ex.start(xins, xouts, sems)

        body(*ins, *outs, *scr)

        @pl.when(early)
        def _():
            ex.early(xins, xouts, sems)

        @pl.when(mid)
        def _():
            ex.mid(xins, xouts, sems)

        @pl.when(last)
        def _():
            ex.finish(xins, xouts, sems)

    res = _call(hosting, in_specs=list(in_specs) + [ANY] * n_xin, out_specs=list(out_specs) + [ANY] * n_xout,
                out_shape=list(out_shape) + ex.out_shape, scratch_shapes=list(scratch_shapes) + ex.sems,
                **kw)(*args, *ex.ins)
    return list(res[:n_out]), list(res[n_out:])


def _grid_phases(dims, mid_fraction=0.8, early_fraction=0.4):
    total = 1
    for d in dims:
        total *= d
    mid = min(max(int(total * mid_fraction), 1), total - 1)
    early = min(int(total * early_fraction), mid)

    def phases():
        step = pl.program_id(0)
        for axis in range(1, len(dims)):
            step = step * dims[axis] + pl.program_id(axis)
        return step == 0, step == early, step == mid, step == total - 1
    return phases


def _small_exchanges(pack, blocks, name, after):
    rows = pack.shape[0]

    def body(in_ref, blk_ref, *rest):
        out_ref, got_ref, recv, send_sems, recv_sems, blk_send, blk_recv, blk_local = rest[len(after):]
        x, y, c = _place()
        me = 4 * x + 2 * y + c
        peers = [_peer(x, y, c, m) for m in range(1, N_DEV)]
        index = [4 * px + 2 * py + pc for px, py, pc in peers]

        def block_copy(m, landing):
            return pltpu.make_async_remote_copy(
                src_ref=blk_ref.at[index[m]], dst_ref=got_ref.at[index[m] if landing else me],
                send_sem=blk_send.at[m], recv_sem=blk_recv.at[m], device_id=peers[m], device_id_type=MESH_ID)

        def pack_copy(m, landing):
            return pltpu.make_async_remote_copy(
                src_ref=in_ref, dst_ref=recv.at[index[m] if landing else me], send_sem=send_sems.at[m],
                recv_sem=recv_sems.at[m], device_id=peers[m], device_id_type=MESH_ID)

        own = pltpu.make_async_copy(blk_ref.at[me], got_ref.at[me], blk_local)
        own.start()
        recv[me] = in_ref[...]
        for m in range(N_DEV - 1):
            pack_copy(m, False).start()
            block_copy(m, False).start()
        for m in range(N_DEV - 1):
            pack_copy(m, True).wait_recv()
        acc = recv[0]
        for d in range(1, N_DEV):
            acc = acc + recv[d]
        out_ref[...] = acc
        for m in range(N_DEV - 1):
            block_copy(m, True).wait_recv()
        for m in range(N_DEV - 1):
            pack_copy(m, False).wait_send()
            block_copy(m, False).wait_send()
        own.wait()

    return _call(
        body, name=name,
        out_shape=[jax.ShapeDtypeStruct(pack.shape, F32), jax.ShapeDtypeStruct(blocks.shape, blocks.dtype)],
        in_specs=[VMEM, ANY] + [ANY] * len(after), out_specs=[VMEM, ANY],
        scratch_shapes=[pltpu.VMEM((N_DEV, rows, pack.shape[1]), F32)] + [pltpu.SemaphoreType.DMA((7,))] * 4
        + [pltpu.SemaphoreType.DMA],
        compiler_params=pltpu.CompilerParams(vmem_limit_bytes=VMEM_LIMIT),
    )(pack, blocks, *after)


POOL_TS = 256


def _pool_counts(first_row, rows, win):
    t = first_row + lax.broadcasted_iota(jnp.int32, (rows, 1), 0)
    return jnp.minimum(t + 1, win).astype(F32)


def _pool_fwd(x, g, w, b, scale):
    nt = S // POOL_TS

    def body(x_ref, g_ref, w_ref, b_ref, s_ref, out_ref, diff_ref, ext):
        i = pl.program_id(0)

        @pl.when(i == 0)
        def _():
            ext[0:MAX_WIN, :] = jnp.zeros((MAX_WIN, D), F32)

        @pl.when(i > 0)
        def _():
            ext[0:MAX_WIN, :] = ext[POOL_TS:POOL_TS + MAX_WIN, :]

        xv = x_ref[...]
        h = xv * _rstd(xv) * g_ref[...]
        ext[MAX_WIN:, :] = h
        for gi in range(N_GROUPS):
            win = 2 << gi
            cols = slice(gi * GROUP, (gi + 1) * GROUP)
            sm = ext[:, cols]
            k = 1
            while k < win:
                sm = sm + pltpu.roll(sm, k, axis=0)
                k *= 2
            pooled = sm[MAX_WIN:, :] / _pool_counts(i * POOL_TS, POOL_TS, win)
            diff = (pooled - h[:, cols]).astype(BF16)
            yv = (_nn(diff, w_ref[gi]) + b_ref[:, cols]) * s_ref[:, cols]
            out_ref[:, cols] = xv[:, cols] + yv
            diff_ref[:, cols] = diff

    row = pl.BlockSpec((1, D), lambda i: (0, 0))
    tile = pl.BlockSpec((POOL_TS, D), lambda i: (i, 0))
    return _call(
        body, name="pool_fwd", grid=(nt,),
        in_specs=[tile, row, pl.BlockSpec((N_GROUPS, GROUP, GROUP), lambda i: (0, 0, 0)), row, row],
        out_specs=[tile, tile],
        out_shape=[jax.ShapeDtypeStruct((S, D), F32), jax.ShapeDtypeStruct((S, D), BF16)],
        scratch_shapes=[pltpu.VMEM((POOL_TS + MAX_WIN, D), F32)],
        compiler_params=_params("arbitrary"),
    )(x, g, w, b, scale)


def _pool_bwd(dy, x, diff, g, w, b, scale, ex=None):
    nt = S // POOL_TS

    def body(dy_ref, x_ref, diff_ref, g_ref, w_ref, b_ref, s_ref, gx_ref, dw_ref, db_ref, ds_ref, dg_ref, ext, dh):
        i = pl.program_id(0)
        first_row = (nt - 1 - i) * POOL_TS

        @pl.when(i == 0)
        def _():
            ext[POOL_TS:, :] = jnp.zeros((MAX_WIN, D), F32)
            dw_ref[...] = jnp.zeros(dw_ref.shape, F32)
            db_ref[...] = jnp.zeros(db_ref.shape, F32)
            ds_ref[...] = jnp.zeros(ds_ref.shape, F32)
            dg_ref[...] = jnp.zeros(dg_ref.shape, F32)

        @pl.when(i > 0)
        def _():
            ext[POOL_TS:, :] = ext[0:MAX_WIN, :]

        dyv = dy_ref[...]
        for gi in range(N_GROUPS):
            win = 2 << gi
            cols = slice(gi * GROUP, (gi + 1) * GROUP)
            dfb = diff_ref[:, cols]
            z = _nn(dfb, w_ref[gi]) + b_ref[:, cols]
            dyg = dyv[:, cols]
            ds_ref[:, cols] += _colsum8(dyg * z)
            dz = dyg * s_ref[:, cols]
            db_ref[:, cols] += _colsum8(dz)
            dzb = dz.astype(BF16)
            dw_ref[gi] += _tn(dfb, dzb)
            ddiff = _nt(dzb, w_ref[gi])
            ext[0:POOL_TS, cols] = ddiff / _pool_counts(first_row, POOL_TS, win)
            sm = ext[:, cols]
            k = 1
            while k < win:
                sm = sm + pltpu.roll(sm, POOL_TS + MAX_WIN - k, axis=0)
                k *= 2
            dh[:, cols] = sm[0:POOL_TS, :] - ddiff
        xv = x_ref[...]
        r = _rstd(xv)
        gv = g_ref[...]
        dhv = dh[...]
        dg_ref[...] += _colsum8(dhv * xv * r)
        gx_ref[...] = dyv + _rms_bwd(dhv, xv, r, gv)

    row = pl.BlockSpec((1, D), lambda i: (0, 0))
    tile = pl.BlockSpec((POOL_TS, D), lambda i: (nt - 1 - i, 0))
    acc = pl.BlockSpec((8, D), lambda i: (0, 0))
    wspec = pl.BlockSpec((N_GROUPS, GROUP, GROUP), lambda i: (0, 0, 0))
    return _call_hosting(
        body, ex, _grid_phases((nt,)), name="pool_bwd", grid=(nt,),
        in_specs=[tile, tile, tile, row, wspec, row, row],
        out_specs=[tile, wspec, acc, acc, acc],
        out_shape=[jax.ShapeDtypeStruct((S, D), F32), jax.ShapeDtypeStruct((N_GROUPS, GROUP, GROUP), F32),
                   jax.ShapeDtypeStruct((8, D), F32), jax.ShapeDtypeStruct((8, D), F32),
                   jax.ShapeDtypeStruct((8, D), F32)],
        scratch_shapes=[pltpu.VMEM((POOL_TS + MAX_WIN, D), F32), pltpu.VMEM((POOL_TS, D), F32)],
        args=(dy, x, diff, g, w, b, scale), compiler_params=_params("arbitrary"),
    )


FFN_TS = min(S, 1024)
FFN_TF = 256


def _ffn_fwd(x, g, wg_t, wu_t, wd, name, target=None, ex=None):
    width = wd.shape[0]
    ni, nj = S // FFN_TS, width // FFN_TF
    with_loss = target is not None
    extra = [target] if with_loss else []

    def body(*refs):
        x_ref, g_ref, wg_ref, wu_ref, wd_ref = refs[:5]
        if with_loss:
            t_ref, out_ref, loss_ref, h_ref, gg_ref, uu_ref, hs, acc = refs[5:]
        else:
            out_ref, h_ref, gg_ref, uu_ref, hs, acc = refs[5:]
        i, j = pl.program_id(0), pl.program_id(1)

        @pl.when(j == 0)
        def _():
            xv = x_ref[...]
            hb = (xv * _rstd(xv) * g_ref[...]).astype(BF16)
            hs[...] = hb
            h_ref[...] = hb
            acc[...] = jnp.zeros(acc.shape, F32)

        hb = hs[...]
        gg = _nt(hb, wg_ref[...])
        uu = _nt(hb, wu_ref[...])
        gg_ref[...] = gg
        uu_ref[...] = uu
        a = (gg * jax.nn.sigmoid(gg) * uu).astype(BF16)
        acc[...] += _nn(a, wd_ref[...])

        @pl.when(j == nj - 1)
        def _():
            yv = x_ref[...] + acc[...]
            if with_loss:
                err = yv - t_ref[...]
                out_ref[...] = err * (1.0 / D)
                part = jnp.sum(err * err) * (0.5 / D)

                @pl.when(i == 0)
                def _():
                    loss_ref[...] = jnp.zeros(loss_ref.shape, F32)

                loss_ref[...] += jnp.broadcast_to(part, loss_ref.shape)
            else:
                out_ref[...] = yv

    xt = pl.BlockSpec((FFN_TS, D), lambda i, j: (i, 0))
    row = pl.BlockSpec((1, D), lambda i, j: (0, 0))
    wt = pl.BlockSpec((FFN_TF, D), lambda i, j: (j, 0))
    gt = pl.BlockSpec((FFN_TS, FFN_TF), lambda i, j: (i, j))
    in_specs = [xt, row, wt, wt, wt] + [xt] * len(extra)
    out_specs = [xt] + ([pl.BlockSpec((8, 128), lambda i, j: (0, 0))] if with_loss else []) + [xt, gt, gt]
    out_shape = ([jax.ShapeDtypeStruct((S, D), F32)] + ([jax.ShapeDtypeStruct((8, 128), F32)] if with_loss else [])
                 + [jax.ShapeDtypeStruct((S, D), BF16), jax.ShapeDtypeStruct((S, width), F32),
                    jax.ShapeDtypeStruct((S, width), F32)])
    args = (x, g, wg_t, wu_t, wd, *extra)
    return _call_hosting(
        body, ex, _grid_phases((ni, nj), 0.8), name=name, grid=(ni, nj),
        in_specs=in_specs, out_specs=out_specs, out_shape=out_shape,
        scratch_shapes=[pltpu.VMEM((FFN_TS, D), BF16), pltpu.VMEM((FFN_TS, D), F32)], args=args,
        compiler_params=_params("arbitrary", "arbitrary"),
    )


def _ffn_bwd_weights(dout, h, gg, uu, wd, name, ex=None):
    width = wd.shape[0]
    nj = width // FFN_TF

    def body(do_ref, h_ref, gg_ref, uu_ref, wd_ref, dg_ref, du_ref, dwg_ref, dwu_ref, dwd_ref, dob, kept):
        j = pl.program_id(0)

        @pl.when(j == 0)
        def _():
            dob[...] = do_ref[...].astype(BF16)
            kept[1] = jnp.zeros(kept.shape[1:], BF16)

        prev, cur = (j + 1) % 2, j % 2
        hb, dov = h_ref[...], dob[...]
        dwg_ref[...] = _tn(kept[prev, 0], hb).astype(BF16)
        dwu_ref[...] = _tn(kept[prev, 1], hb).astype(BF16)
        dwd_ref[...] = _tn(kept[prev, 2], dov).astype(BF16)
        gv, uv = gg_ref[...], uu_ref[...]
        da = _nt(dov, wd_ref[...])
        sg = jax.nn.sigmoid(gv)
        sl = gv * sg
        dub = (da * sl).astype(BF16)
        dgb = (da * uv * (sg * (1.0 + gv * (1.0 - sg)))).astype(BF16)
        dg_ref[...] = dgb
        du_ref[...] = dub
        kept[cur, 0] = dgb
        kept[cur, 1] = dub
        kept[cur, 2] = (sl * uv).astype(BF16)

    once = pl.Buffered(1)
    whole = lambda: pl.BlockSpec((S, D), lambda j, _: (0, 0), pipeline_mode=once)
    this = lambda j: jnp.minimum(j, nj - 1)
    last = lambda j: jnp.maximum(j - 1, 0)
    wt_in = pl.BlockSpec((FFN_TF, D), lambda j, _: (this(j), 0))
    wt_out = pl.BlockSpec((FFN_TF, D), lambda j, _: (last(j), 0))
    gt = pl.BlockSpec((S, FFN_TF), lambda j, _: (0, this(j)))
    return _call_hosting(
        body, ex, _grid_phases((nj + 1, 1)), name=name, grid=(nj + 1, 1),
        in_specs=[whole(), whole(), gt, gt, wt_in], out_specs=[gt, gt, wt_out, wt_out, wt_out],
        out_shape=[jax.ShapeDtypeStruct((S, width), BF16)] * 2 + [jax.ShapeDtypeStruct((width, D), BF16)] * 3,
        scratch_shapes=[pltpu.VMEM((S, D), BF16), pltpu.VMEM((2, 3, S, FFN_TF), BF16)], args=(dout, h, gg, uu, wd),
        compiler_params=_params("arbitrary", "arbitrary"),
    )


BWD_TS = 512


def _ffn_bwd_input(dres, x, g, grads, weights, name, ex=None):
    nt = S // BWD_TS
    n = len(grads)

    def body(*refs):
        dres_ref, x_ref, g_ref = refs[:3]
        grad_refs, w_refs = refs[3:3 + n], refs[3 + n:3 + 2 * n]
        dx_ref, dgam_ref = refs[3 + 2 * n:]
        dh = _nn(grad_refs[0][...], w_refs[0][...])
        for k in range(1, n):
            dh = dh + _nn(grad_refs[k][...], w_refs[k][...])
        xv = x_ref[...]
        r = _rstd(xv)
        dx_ref[...] = dres_ref[...] + _rms_bwd(dh, xv, r, g_ref[...])

        @pl.when(pl.program_id(0) == 0)
        def _():
            dgam_ref[...] = jnp.zeros(dgam_ref.shape, F32)

        dgam_ref[...] += _colsum8(dh * xv * r)

    tile = pl.BlockSpec((BWD_TS, D), lambda i: (i, 0))
    ftiles = [pl.BlockSpec((BWD_TS, a.shape[1]), lambda i: (i, 0)) for a in grads]
    wspecs = [pl.BlockSpec(w.shape, lambda i: (0, 0), pipeline_mode=pl.Buffered(1)) for w in weights]
    return _call_hosting(
        body, ex, _grid_phases((nt,)), name=name, grid=(nt,),
        in_specs=[tile, tile, pl.BlockSpec((1, D), lambda i: (0, 0))] + ftiles + wspecs,
        out_specs=[tile, pl.BlockSpec((8, D), lambda i: (0, 0))],
        out_shape=[jax.ShapeDtypeStruct((S, D), F32), jax.ShapeDtypeStruct((8, D), F32)],
        scratch_shapes=[], args=(dres, x, g, *grads, *weights), compiler_params=_params("arbitrary"),
    )


def _mm(a, b, mode, out_dtype, name, add=None, skip_rows=0):
    if mode == "nn":
        (m, kd), n = (a.shape[0] - skip_rows, a.shape[1]), b.shape[1]
    elif mode == "nt":
        (m, kd), n = (a.shape[0] - skip_rows, a.shape[1]), b.shape[0]
    else:
        (kd, m), n = a.shape, b.shape[1]
    tm, tn, tk = min(m, 1024), min(n, 1024), min(kd, 1024)
    if skip_rows and mode == "tn":
        tk = min(tk, skip_rows)
    elif skip_rows:
        tm = min(tm, skip_rows)
    skip_a = skip_rows // tm if mode != "tn" else 0
    skip_b = skip_rows // tk if mode == "tn" else 0
    assert skip_rows == skip_a * tm + skip_b * tk
    nk = kd // tk
    dot = {"nn": _nn, "nt": _nt, "tn": _tn}[mode]

    def body(*refs):
        if add is None:
            a_ref, b_ref, o_ref, acc = refs
        else:
            a_ref, b_ref, add_ref, o_ref, acc = refs
        k = pl.program_id(2)

        @pl.when(k == 0)
        def _():
            acc[...] = jnp.zeros(acc.shape, F32)

        acc[...] += dot(a_ref[...].astype(BF16), b_ref[...].astype(BF16))

        @pl.when(k == nk - 1)
        def _():
            res = acc[...]
            if add is not None:
                res = res + add_ref[...]
            o_ref[...] = res.astype(out_dtype)

    if mode == "tn":
        a_spec = pl.BlockSpec((tk, tm), lambda i, j, k: (k, i))
        b_spec = pl.BlockSpec((tk, tn), lambda i, j, k: (k + skip_b, j))
    else:
        a_spec = pl.BlockSpec((tm, tk), lambda i, j, k: (i + skip_a, k))
        b_spec = (pl.BlockSpec((tk, tn), lambda i, j, k: (k, j)) if mode == "nn"
                  else pl.BlockSpec((tn, tk), lambda i, j, k: (j, k)))
    o_spec = pl.BlockSpec((tm, tn), lambda i, j, k: (i, j))
    in_specs = [a_spec, b_spec] + ([o_spec] if add is not None else [])
    args = (a, b) + ((add,) if add is not None else ())
    return _call(
        body, name=name, grid=(m // tm, n // tn, nk), in_specs=in_specs, out_specs=o_spec,
        out_shape=jax.ShapeDtypeStruct((m, n), out_dtype), scratch_shapes=[pltpu.VMEM((tm, tn), F32)],
        compiler_params=_params("parallel", "parallel", "arbitrary"),
    )(*args)


PROJ_TS = 256


def _kvq_proj(x, g_kv, g_mix, wk, wv, wq, gk, gq, ex=None):
    lead = PADK // PROJ_TS

    def body(x_ref, gkv_ref, gmix_ref, wk_ref, wv_ref, wq_ref, gk_ref, gq_ref,
             hkv_ref, h1_ref, kpre_ref, qpre_ref, k_ref, v_ref, q_ref):
        i = pl.program_id(0)

        @pl.when(i < lead)
        def _():
            k_ref[...] = jnp.zeros(k_ref.shape, BF16)
            v_ref[...] = jnp.zeros(v_ref.shape, BF16)

        @pl.when(i >= lead)
        def _():
            xv = x_ref[...]
            xr = xv * _rstd(xv)
            hkv = (xr * gkv_ref[...]).astype(BF16)
            h1 = (xr * gmix_ref[...]).astype(BF16)
            hkv_ref[...] = hkv
            h1_ref[...] = h1
            kpre = _nn(hkv, wk_ref[...])
            qpre = _nn(h1, wq_ref[...])
            kpre_ref[...] = kpre
            qpre_ref[...] = qpre
            v_ref[...] = _nn(hkv, wv_ref[...]).astype(BF16)
            rk = lax.rsqrt(_seg_sum(kpre * kpre) * (1.0 / HEAD) + EPS)
            k_ref[...] = (kpre * rk * gk_ref[...]).astype(BF16)
            rq = lax.rsqrt(_seg_sum(qpre * qpre) * (1.0 / HEAD) + EPS)
            q_ref[...] = (qpre * rq * gq_ref[...]).astype(BF16)

    tile = pl.BlockSpec((PROJ_TS, D), lambda i: (jnp.maximum(i - lead, 0), 0))
    padded = pl.BlockSpec((PROJ_TS, D), lambda i: (i, 0))
    row = pl.BlockSpec((1, D), lambda i: (0, 0))
    wspec = pl.BlockSpec((D, D), lambda i: (0, 0))
    bf = jax.ShapeDtypeStruct((S, D), BF16)
    ff = jax.ShapeDtypeStruct((S, D), F32)
    bp = jax.ShapeDtypeStruct((PADK + S, D), BF16)
    return _call_hosting(
        body, ex, _grid_phases((lead + S // PROJ_TS,), 0.85), name="kvq_proj", grid=(lead + S // PROJ_TS,),
        in_specs=[tile, row, row, wspec, wspec, wspec, row, row],
        out_specs=[tile, tile, tile, tile, padded, padded, tile], out_shape=[bf, bf, ff, ff, bp, bp, bf],
        scratch_shapes=[], args=(x, g_kv, g_mix, wk, wv, wq, gk, gq), compiler_params=_params("arbitrary"),
    )


def _qkv_input_bwd(dq, qpre, gq, dkp, kpre, gk, dvp, wq, wk, wv, dres, x, g_mix, g_kv, ex=None):
    nt = S // PROJ_TS
    lead = PADK // PROJ_TS

    def head_bwd(dov, pv, hgv):
        r = lax.rsqrt(_seg_sum(pv * pv) * (1.0 / HEAD) + EPS)
        gd = dov * hgv
        dpre = r * gd - pv * (r * r * r) * (_seg_sum(gd * pv) * (1.0 / HEAD))
        return dpre.astype(BF16), _colsum8(dov * pv * r)

    def fold_heads(full):
        fold = full[:, 0:128]
        for blk in range(1, D // 128):
            fold = fold + full[:, blk * 128:(blk + 1) * 128]
        return fold + pltpu.roll(fold, HEAD, axis=1)

    def body(dq_ref, qpre_ref, gq_ref, dk_ref, kpre_ref, gk_ref, dv_ref, wq_ref, wk_ref, wv_ref, dres_ref, x_ref,
             gmix_ref, gkv_ref, dx_ref, dqpre_ref, dkpre_ref, dgmix_ref, dgkv_ref, dgq_ref, dgk_ref, accq, acck):
        i = pl.program_id(0)

        @pl.when(i == 0)
        def _():
            accq[...] = jnp.zeros(accq.shape, F32)
            acck[...] = jnp.zeros(acck.shape, F32)
            dgmix_ref[...] = jnp.zeros(dgmix_ref.shape, F32)
            dgkv_ref[...] = jnp.zeros(dgkv_ref.shape, F32)

        dqb, cq = head_bwd(dq_ref[...], qpre_ref[...], gq_ref[...])
        dkb, ck = head_bwd(dk_ref[...], kpre_ref[...], gk_ref[...])
        dqpre_ref[...] = dqb
        dkpre_ref[...] = dkb
        accq[...] += cq
        acck[...] += ck
        dh1 = _nt(dqb, wq_ref[...])
        dhkv = _nt(dkb, wk_ref[...]) + _nt(dv_ref[...].astype(BF16), wv_ref[...])
        xv = x_ref[...]
        r = _rstd(xv)
        dx_ref[...] = dres_ref[...] + _rms_bwd(dh1, xv, r, gmix_ref[...]) + _rms_bwd(dhkv, xv, r, gkv_ref[...])
        dgmix_ref[...] += _colsum8(dh1 * xv * r)
        dgkv_ref[...] += _colsum8(dhkv * xv * r)

        @pl.when(i == nt - 1)
        def _():
            dgq_ref[...] = fold_heads(accq[...])
            dgk_ref[...] = fold_heads(acck[...])

    tile = pl.BlockSpec((PROJ_TS, D), lambda i: (i, 0))
    behind = pl.BlockSpec((PROJ_TS, D), lambda i: (i + lead, 0))
    row = pl.BlockSpec((1, D), lambda i: (0, 0))
    wspec = lambda: pl.BlockSpec((D, D), lambda i: (0, 0), pipeline_mode=pl.Buffered(1))
    acc = pl.BlockSpec((8, D), lambda i: (0, 0))
    small = pl.BlockSpec((8, 128), lambda i: (0, 0))
    bf = jax.ShapeDtypeStruct((S, D), BF16)
    return _call_hosting(
        body, ex, _grid_phases((nt,)), name="qkv_input_bwd", grid=(nt,),
        in_specs=[tile, tile, row, behind, tile, row, behind, wspec(), wspec(), wspec(), tile, tile, row, row],
        out_specs=[tile, tile, tile, acc, acc, small, small],
        out_shape=[jax.ShapeDtypeStruct((S, D), F32), bf, bf, jax.ShapeDtypeStruct((8, D), F32),
                   jax.ShapeDtypeStruct((8, D), F32), jax.ShapeDtypeStruct((8, 128), F32),
                   jax.ShapeDtypeStruct((8, 128), F32)],
        scratch_shapes=[pltpu.VMEM((8, D), F32), pltpu.VMEM((8, D), F32)],
        args=(dq, qpre, gq, dkp, kpre, gk, dvp, wq, wk, wv, dres, x, g_mix, g_kv),
        compiler_params=_params("arbitrary"),
    )


def _toeplitz_from_table(table):
    far = jnp.broadcast_to(table[:, N_REL - 1:], (N_HEADS, PADK - MAX_REL + 1))
    near = table[:, N_REL - 2::-1]
    past = jnp.broadcast_to(table[:, 0:1], (N_HEADS, MAX_REL))
    wrap = jnp.broadcast_to(table[:, N_REL - 1:], (N_HEADS, TOEP - (PADK + 2 * MAX_REL + 1)))
    return jnp.concatenate([far, near, past, wrap], axis=1).reshape(N_HEADS, 1, TOEP)


def _table_grad_from_toeplitz(dtp, seg):
    lo = PADK - MAX_REL + 1
    near = dtp[:, lo + N_REL - 3:lo - 1:-1]
    return jnp.concatenate([seg[:, 1:2], near, seg[:, 0:1]], axis=1)


def _bias_band(tp):
    def body(tp_ref, out_ref):
        bv = pltpu.roll(jnp.broadcast_to(tp_ref[0], (QB, TOEP)), 0, axis=1, stride=1, stride_axis=0)
        out_ref[0] = jnp.where(_band_mask(), bv[:, 0:KB], NEG_INF)

    return _call(
        body, name="bias_band", grid=(N_HEADS,),
        in_specs=[pl.BlockSpec((1, 1, TOEP), lambda h: (h, 0, 0))],
        out_specs=pl.BlockSpec((1, QB, KB), lambda h: (h, 0, 0)),
        out_shape=jax.ShapeDtypeStruct((N_HEADS, QB, KB), F32),
        compiler_params=_params("parallel"),
    )(tp)


def _bias_grad(dband, after):
    lo, hi = PADK - MAX_REL + 1, PADK + MAX_REL

    def body(db_ref, _, dtp_ref, seg_ref):
        bv = jnp.concatenate([db_ref[0], jnp.zeros((QB, TOEP - KB), F32)], axis=1)
        row = lax.broadcasted_iota(jnp.int32, (QB, TOEP), 0)
        k = 1
        while k < QB:
            bv = jnp.where((row & k) != 0, pltpu.roll(bv, TOEP - k, axis=1), bv)
            k *= 2
        col = jnp.sum(bv, axis=0, keepdims=True)
        dtp_ref[0] = col
        u = lax.broadcasted_iota(jnp.int32, (1, TOEP), 1)
        far = jnp.sum(jnp.where((u < lo) | (u > hi + MAX_REL), col, 0.0))
        past = jnp.sum(jnp.where((u >= hi) & (u <= hi + MAX_REL), col, 0.0))
        lane = lax.broadcasted_iota(jnp.int32, (1, 128), 1)
        seg_ref[0] = jnp.where(lane == 0, far, jnp.where(lane == 1, past, 0.0))

    return _call(
        body, name="bias_grad", grid=(N_HEADS,),
        in_specs=[pl.BlockSpec((1, QB, KB), lambda h: (h, 0, 0)), ANY],
        out_specs=[pl.BlockSpec((1, 1, TOEP), lambda h: (h, 0, 0)), pl.BlockSpec((1, 1, 128), lambda h: (h, 0, 0))],
        out_shape=[jax.ShapeDtypeStruct((N_HEADS, 1, TOEP), F32), jax.ShapeDtypeStruct((N_HEADS, 1, 128), F32)],
        compiler_params=_params("parallel"),
    )(dband, after)


N_QB = S // QB
HEADS_PER_STEP = 4
ATT_LANES = HEADS_PER_STEP * HEAD
N_HG = D // ATT_LANES


def _band_mask():
    qc = lax.broadcasted_iota(jnp.int32, (QB, KB), 0) // CHUNK
    kc = lax.broadcasted_iota(jnp.int32, (QB, KB), 1) // CHUNK
    return (kc >= qc) & (kc <= qc + LEFT)


def _half_scale(hh, scale):
    lane = lax.broadcasted_iota(jnp.int32, (1, 128), 1)
    return jnp.where((lane < HEAD) == (hh == 0), scale, 0.0).astype(BF16)


def _probs(qh, kb, bias, first_key):
    sc = _nt(qh, kb) + bias
    if first_key is not None:
        sc = jnp.where(lax.broadcasted_iota(jnp.int32, (QB, KB), 1) >= first_key, sc, NEG_INF)
    e = jnp.exp(sc - jnp.max(sc, axis=-1, keepdims=True))
    return e * (1.0 / jnp.sum(e, axis=-1, keepdims=True))


def _by_padding(cb, compute):
    @pl.when(cb < PADK // QB)
    def _():
        compute(PADK - cb * QB)

    @pl.when(cb >= PADK // QB)
    def _():
        compute(None)


def _attn_fwd(q, kp, vp, bias, ex=None):
    def body(q_ref, k_ref, v_ref, b_ref, o_ref):
        cb = pl.program_id(1)
        band = pl.ds(pl.multiple_of(cb * QB, QB), KB)
        low = lax.broadcasted_iota(jnp.int32, (QB, 128), 1) < HEAD

        def compute(first_key):
            for pair in range(HEADS_PER_STEP // 2):
                lanes = pl.ds(pair * 128, 128)
                kb, vb, qv = k_ref[band, lanes], v_ref[band, lanes], q_ref[:, lanes]
                outs = []
                for hh in range(2):
                    pb = _probs(qv * _half_scale(hh, ATTN_SCALE), kb, b_ref[2 * pair + hh], first_key).astype(BF16)
                    outs.append(_nn(pb, vb))
                o_ref[:, lanes] = jnp.where(low, outs[0], outs[1]).astype(BF16)

        _by_padding(cb, compute)

    qspec = pl.BlockSpec((QB, ATT_LANES), lambda hg, cb: (cb, hg))
    kspec = pl.BlockSpec((PADK + S, ATT_LANES), lambda hg, cb: (0, hg))
    return _call_hosting(
        body, ex, _grid_phases((N_HG, N_QB), 0.85), name="attn_fwd", grid=(N_HG, N_QB),
        in_specs=[qspec, kspec, kspec, pl.BlockSpec((HEADS_PER_STEP, QB, KB), lambda hg, cb: (hg, 0, 0))],
        out_specs=[qspec], out_shape=[jax.ShapeDtypeStruct((S, D), BF16)], scratch_shapes=[],
        args=(q, kp, vp, bias), compiler_params=_params("arbitrary", "arbitrary"),
    )


def _attn_bwd(q, kp, vp, bias, do, ex=None):
    def body(q_ref, k_ref, v_ref, b_ref, do_ref, dq_ref, dk_ref, dv_ref, db_ref):
        cb = pl.program_id(1)

        @pl.when(cb == 0)
        def _():
            dk_ref[...] = jnp.zeros(dk_ref.shape, F32)
            dv_ref[...] = jnp.zeros(dv_ref.shape, F32)
            db_ref[...] = jnp.zeros(db_ref.shape, F32)

        band = pl.ds(pl.multiple_of(cb * QB, QB), KB)
        low = lax.broadcasted_iota(jnp.int32, (QB, 128), 1) < HEAD

        def compute(first_key):
            for pair in range(HEADS_PER_STEP // 2):
                lanes = pl.ds(pair * 128, 128)
                kb, vb = k_ref[band, lanes], v_ref[band, lanes]
                qv, dov = q_ref[:, lanes], do_ref[:, lanes]
                dq = jnp.zeros((QB, 128), F32)
                dkb = jnp.zeros((KB, 128), F32)
                dvb = jnp.zeros((KB, 128), F32)
                for hh in range(2):
                    sel = low if hh == 0 else jnp.logical_not(low)
                    doh = dov * _half_scale(hh, 1.0)
                    p = _probs(qv * _half_scale(hh, ATTN_SCALE), kb, b_ref[2 * pair + hh], first_key)
                    dp = _nt(doh, vb)
                    dvb = dvb + _tn(p.astype(BF16), doh)
                    ds = p * (dp - jnp.sum(dp * p, axis=-1, keepdims=True))
                    db_ref[2 * pair + hh] += ds
                    dsb = (ds * ATTN_SCALE).astype(BF16)
                    dq = dq + jnp.where(sel, _nn(dsb, kb), 0.0)
                    dkb = dkb + _tn(dsb, qv * _half_scale(hh, 1.0))
                dq_ref[:, lanes] = dq
                dk_ref[band, lanes] += dkb
                dv_ref[band, lanes] += dvb

        _by_padding(cb, compute)

    qspec = pl.BlockSpec((QB, ATT_LANES), lambda hg, cb: (cb, hg))
    kspec = pl.BlockSpec((PADK + S, ATT_LANES), lambda hg, cb: (0, hg))
    bspec = pl.BlockSpec((HEADS_PER_STEP, QB, KB), lambda hg, cb: (hg, 0, 0))
    kf = jax.ShapeDtypeStruct((PADK + S, D), F32)
    return _call_hosting(
        body, ex, _grid_phases((N_HG, N_QB)), name="attn_bwd", grid=(N_HG, N_QB),
        in_specs=[qspec, kspec, kspec, bspec, qspec],
        out_specs=[qspec, kspec, kspec, bspec],
        out_shape=[jax.ShapeDtypeStruct((S, D), F32), kf, kf, jax.ShapeDtypeStruct((N_HEADS, QB, KB), F32)],
        scratch_shapes=[], args=(q, kp, vp, bias, do), compiler_params=_params("arbitrary", "arbitrary"),
    )


def _adam_math(w, g, m, v):
    m = ADAM_B1 * m + (1.0 - ADAM_B1) * g
    v = ADAM_B2 * v + (1.0 - ADAM_B2) * (g * g)
    m_hat = m / (1.0 - ADAM_B1 ** ADAM_STEP)
    v_hat = v / (1.0 - ADAM_B2 ** ADAM_STEP)
    delta = -ADAM_LR * (m_hat / (jnp.sqrt(v_hat) + ADAM_EPS) + ADAM_WD * w)
    return delta, m, v


def _row_tile(r):
    for cand in (256, 176, 128, 64, 32, 16, 8):
        if r % cand == 0:
            return cand
    return r


def _adam_layer(w, parts, m, v, layer, name, prev=None):
    nl, r, c = w.shape
    count = parts.shape[0]
    tr = _row_tile(r)

    def body(w_ref, g_ref, m_ref, v_ref, *rest):
        go_ref, d_ref, nm_ref, nv_ref = rest[-4:]
        gv = g_ref[0].astype(F32)
        for q in range(1, count):
            gv = gv + g_ref[q].astype(F32)
        delta, nm, nv = _adam_math(w_ref[0], gv, m_ref[0], v_ref[0])
        go_ref[0] = gv
        d_ref[0] = delta
        nm_ref[0] = nm
        nv_ref[0] = nv

    lspec = pl.BlockSpec((1, tr, c), lambda i: (layer, i, 0))
    sd = jax.ShapeDtypeStruct((nl, r, c), F32)
    extra = list(prev) if prev is not None else []
    return _call(
        body, name=name, grid=(r // tr,),
        in_specs=[lspec, pl.BlockSpec((count, tr, c), lambda i: (0, i, 0)), lspec, lspec] + [ANY] * len(extra),
        out_specs=[lspec] * 4, out_shape=[sd] * 4,
        input_output_aliases={4 + t: t for t in range(len(extra))},
        compiler_params=_params("parallel"),
    )(w, parts, m, v, *extra)


def _adam(w, g, m, v, name):
    r, c = w.shape
    tr = _row_tile(r)
    count = g.shape[0] if g.ndim == 3 else 0

    def body(w_ref, g_ref, m_ref, v_ref, go_ref, d_ref, nm_ref, nv_ref):
        if count:
            gv = g_ref[0].astype(F32)
            for q in range(1, count):
                gv = gv + g_ref[q].astype(F32)
        else:
            gv = g_ref[...]
        delta, nm, nv = _adam_math(w_ref[...], gv, m_ref[...], v_ref[...])
        go_ref[...] = gv
        d_ref[...] = delta
        nm_ref[...] = nm
        nv_ref[...] = nv

    spec = pl.BlockSpec((tr, c), lambda i: (i, 0))
    gspec = pl.BlockSpec((count, tr, c), lambda i: (0, i, 0)) if count else spec
    sd = jax.ShapeDtypeStruct((r, c), F32)
    return _call(
        body, name=name, grid=(r // tr,), in_specs=[spec, gspec, spec, spec], out_specs=[spec] * 4,
        out_shape=[sd] * 4, compiler_params=_params("parallel"),
    )(w, g, m, v)


def _pad16(rows):
    return jnp.pad(rows, ((0, 16 - rows.shape[0]), (0, 0)))


def kernel(x, norm_mix_g, norm_ffn_g, pool_w, pool_b, pool_scale, kv_norm_g, w_k, w_v, k_norm_g, w_q, q_norm_g, rel_bias, w_o, w_gate, w_up, w_down, loss_target, m_norm_mix_g, m_norm_ffn_g, m_pool_w, m_pool_b, m_pool_scale, m_kv_norm_g, m_w_k, m_w_v, m_k_norm_g, m_w_q, m_q_norm_g, m_rel_bias, m_w_o, m_w_gate, m_w_up, m_w_down, v_norm_mix_g, v_norm_ffn_g, v_pool_w, v_pool_b, v_pool_scale, v_kv_norm_g, v_w_k, v_w_v, v_k_norm_g, v_w_q, v_q_norm_g, v_rel_bias, v_w_o, v_w_gate, v_w_up, v_w_down):
    assert x.shape == (1, S, D) and w_gate.shape == (2, D, F_SHARD) and w_k.shape == (D_SHARD, D)
    xin, target = x[0], loss_target[0]

    ffn_shards = [[w_gate[layer].T.astype(BF16), w_up[layer].T.astype(BF16), w_down[layer].astype(BF16)]
                  for layer in range(2)]
    att_shards = [w_k.astype(BF16), w_v.astype(BF16), w_q[0].astype(BF16), w_o[0].astype(BF16)]
    pool_shard = pool_w[0].astype(BF16).reshape(N_GROUPS * POOL_SHARD, GROUP)
    small = jnp.concatenate([pool_b[0].reshape(1, N_GROUPS * POOL_SHARD), pool_scale], axis=1)

    full0 = _run_exchange(_relay_gather_exchange(ffn_shards[0] + [pool_shard, _pad16(small)]), "gather_layer0")
    ffn_w0 = [a.reshape(F, D) for a in full0[:3]]
    pw_f = full0[3].reshape(N_DEV, N_GROUPS, POOL_SHARD, GROUP).transpose(1, 0, 2, 3).reshape(N_GROUPS, GROUP, GROUP)
    small_f = full0[4][:, 0, :]
    pb_f = small_f[:, :N_GROUPS * POOL_SHARD].reshape(N_DEV, N_GROUPS, POOL_SHARD).transpose(1, 0, 2).reshape(1, D)
    ps_f = small_f[:, N_GROUPS * POOL_SHARD:].reshape(1, D)

    g_mix0, g_mix1 = norm_mix_g[0:1], norm_mix_g[1:2]
    g_ffn0, g_ffn1 = norm_ffn_g[0:1], norm_ffn_g[1:2]
    g_kv = kv_norm_g.reshape(1, D)
    gk_t = jnp.tile(k_norm_g.reshape(1, HEAD), (1, N_HEADS))
    gq_t = jnp.tile(q_norm_g.reshape(1, HEAD), (1, N_HEADS))

    x1, diff = _pool_fwd(xin, g_mix0, pw_f, pb_f, ps_f)
    (x2, hf0, gg0, uu0), full_att = _ffn_fwd(x1, g_ffn0, *ffn_w0, name="ffn_fwd", ex=_relay_gather_exchange(att_shards))
    wk_f, wv_f, wq_f, wo_f = [a.reshape(D, D) for a in full_att]
    (hkv, h1, kpre, qpre, kp, vp, qq), full1_gate = _kvq_proj(x2, g_kv, g_mix1, wk_f, wv_f, wq_f, gk_t, gq_t,
                                                                ex=_relay_gather_exchange(ffn_shards[1][:1]))
    bias = _bias_band(_toeplitz_from_table(rel_bias[0]))
    (att,), full1_rest = _attn_fwd(qq, kp, vp, bias, ex=_relay_gather_exchange(ffn_shards[1][1:]))
    ffn_w1 = [a.reshape(F, D) for a in full1_gate + full1_rest]
    x3 = _mm(att, wo_f, "nn", F32, "attn_out", add=x2)
    (dx4, loss_rows, hf1, gg1, uu1), _ = _ffn_fwd(x3, g_ffn1, *ffn_w1, name="ffn_fwd_loss", target=target)

    def blocks(dw):
        return dw.reshape(N_DEV, dw.shape[0] // N_DEV, dw.shape[1])

    (dgg1, duu1, dwg1, dwu1, dwd1), _ = _ffn_bwd_weights(dx4, hf1, gg1, uu1, ffn_w1[2], name="ffn_bwd1")
    (dx3, dg_ffn1), _ = _ffn_bwd_input(dx4, x3, g_ffn1, [dgg1, duu1], ffn_w1[:2], "ffn_dx1")
    datt = _mm(dx3, wo_f, "nt", BF16, "d_attn")
    dwo = _mm(att, dx3, "tn", BF16, "d_wo")
    parts1 = [blocks(dw) for dw in (dwg1, dwu1, dwd1, dwo)]
    (dq, dkp, dvp, dband), stage1 = _attn_bwd(qq, kp, vp, bias, datt, ex=_pair_exchange(parts1))
    both1 = _pair_add(parts1, stage1, "pair_add_ffn1")
    flight1, chip1, land1, _ = _chip_exchange_start(both1[:4], both1[4:], dq, "scatter_ffn1_start")
    (dx2, dqpre, dkpre, dg_mix1, dg_kv, dgq, dgk), _ = _qkv_input_bwd(
        dq, qpre, gq_t, dkp, kpre, gk_t, dvp, wq_f, wk_f, wv_f, dx3, x2, g_mix1, g_kv)
    dwq = _mm(h1, dqpre, "tn", BF16, "d_wq")
    dwk = _mm(hkv, dkpre, "tn", BF16, "d_wk")
    dwv = _mm(hkv, dvp, "tn", BF16, "d_wv", skip_rows=PADK)
    parts_att = [blocks(dw) for dw in (dwk, dwv, dwq)]
    (dgg0, duu0, *dw0), stage_att = _ffn_bwd_weights(
        dx2, hf0, gg0, uu0, ffn_w0[2], name="ffn_bwd0", ex=_pair_exchange(parts_att))
    both_att = _pair_add(parts_att, stage_att, "pair_add_att")
    flight_att, chip_att, land_att, token_att = _chip_exchange_start(both_att[:3], both_att[3:], dx2,
                                                                     "scatter_att_start")
    parts0 = [blocks(dw) for dw in dw0]
    (dx1, dg_ffn0), stage0 = _ffn_bwd_input(dx2, x1, g_ffn0 + token_att[0:1, 0:1], [dgg0, duu0], ffn_w0[:2],
                                            "ffn_dx0", ex=_pair_exchange(parts0))
    both = _pair_add(parts0, stage0, "pair_add_ffn0")
    flight, chip0, land0, token = _chip_exchange_start(both[:3], both[3:], dx1, "scatter_ffn0_start")
    (grad_x, dpw, db_rows, ds_rows, dg_mix0), _ = _pool_bwd(dx1, xin, diff, g_mix0 + token[0:1, 0:1], pw_f, pb_f, ps_f)
    dtp, seg = _bias_grad(dband, dpw)

    weights = dict(norm_mix_g=norm_mix_g, norm_ffn_g=norm_ffn_g, pool_w=pool_w, pool_b=pool_b,
                   pool_scale=pool_scale, kv_norm_g=kv_norm_g, w_k=w_k, w_v=w_v, k_norm_g=k_norm_g, w_q=w_q,
                   q_norm_g=q_norm_g, rel_bias=rel_bias, w_o=w_o, w_gate=w_gate, w_up=w_up, w_down=w_down)
    mom1 = dict(norm_mix_g=m_norm_mix_g, norm_ffn_g=m_norm_ffn_g, pool_w=m_pool_w, pool_b=m_pool_b,
                pool_scale=m_pool_scale, kv_norm_g=m_kv_norm_g, w_k=m_w_k, w_v=m_w_v, k_norm_g=m_k_norm_g,
                w_q=m_w_q, q_norm_g=m_q_norm_g, rel_bias=m_rel_bias, w_o=m_w_o, w_gate=m_w_gate, w_up=m_w_up,
                w_down=m_w_down)
    mom2 = dict(norm_mix_g=v_norm_mix_g, norm_ffn_g=v_norm_ffn_g, pool_w=v_pool_w, pool_b=v_pool_b,
                pool_scale=v_pool_scale, kv_norm_g=v_kv_norm_g, w_k=v_w_k, w_v=v_w_v, k_norm_g=v_k_norm_g,
                w_q=v_w_q, q_norm_g=v_q_norm_g, rel_bias=v_rel_bias, w_o=v_w_o, w_gate=v_w_gate, w_up=v_w_up,
                w_down=v_w_down)
    names = list(weights)
    grads, deltas, new_m, new_v = {}, {}, {}, {}

    def adam_flat(nm, parts):
        shape = weights[nm].shape
        flat = lambda a: a.reshape(-1, shape[-1])
        done = _adam(flat(weights[nm]), parts, flat(mom1[nm]), flat(mom2[nm]), "adam_" + nm)
        grads[nm], deltas[nm], new_m[nm], new_v[nm] = [a.reshape(shape) for a in done]

    ffn_names = ("w_gate", "w_up", "w_down")
    def ffn_view(nm, a):
        return a if nm == "w_down" else a.transpose(0, 2, 1)

    recv1 = _chip_exchange_wait(flight1, chip1, land1, dtp, "scatter_ffn1_wait")
    layer1 = {nm: _adam_layer(ffn_view(nm, weights[nm]), g, ffn_view(nm, mom1[nm]), ffn_view(nm, mom2[nm]), 1,
                              "adam1_" + nm)
              for nm, g in zip(ffn_names, recv1)}
    adam_flat("w_o", recv1[3])
    done1 = [layer1[nm][1] for nm in ffn_names] + [deltas["w_o"]]
    recv_att = _chip_exchange_wait(flight_att, chip_att, land_att, done1, "scatter_att_wait")
    for nm, parts in zip(("w_k", "w_v", "w_q"), recv_att):
        adam_flat(nm, parts)

    dpw_blocks = dpw.reshape(N_GROUPS, N_DEV, POOL_SHARD, GROUP).transpose(1, 0, 2, 3)
    dpw_blocks = dpw_blocks.reshape(N_DEV, N_GROUPS * POOL_SHARD, GROUP).astype(BF16)
    misc = jnp.concatenate([dgk[0:1, 0:HEAD], dgq[0:1, 0:HEAD], loss_rows[0:1, 0:1],
                            seg[:, 0, 0].reshape(1, N_HEADS), seg[:, 0, 1].reshape(1, N_HEADS)], axis=1)
    misc = jnp.pad(misc, ((0, 0), (0, D - misc.shape[1])))
    vec_rows = jnp.concatenate([dg_mix0[0:1], dg_mix1[0:1], dg_ffn0[0:1], dg_ffn1[0:1], dg_kv[0:1],
                                db_rows[0:1], ds_rows[0:1], misc], axis=0)
    pack = jnp.concatenate([vec_rows, dtp.reshape(N_HEADS, TOEP)], axis=0)
    tot, recv_pool = _small_exchanges(pack, dpw_blocks, "small_exchanges",
                                      after=[deltas[nm] for nm in ("w_k", "w_v", "w_q")])

    loss = tot[7, 2 * HEAD]
    seg_tot = jnp.stack([tot[7, 2 * HEAD + 1:2 * HEAD + 1 + N_HEADS],
                         tot[7, 2 * HEAD + 1 + N_HEADS:2 * HEAD + 1 + 2 * N_HEADS]], axis=1)
    me = 4 * lax.axis_index("x") + 2 * lax.axis_index("y") + lax.axis_index("c")
    g_pool_b = lax.dynamic_slice_in_dim(tot[5].reshape(N_GROUPS, GROUP), me * POOL_SHARD, POOL_SHARD, axis=1)
    grads.update(
        norm_mix_g=tot[0:2], norm_ffn_g=tot[2:4], kv_norm_g=tot[4], k_norm_g=tot[7, 0:HEAD],
        q_norm_g=tot[7, HEAD:2 * HEAD].reshape(1, HEAD),
        rel_bias=_table_grad_from_toeplitz(tot[8:8 + N_HEADS], seg_tot).reshape(1, N_HEADS, N_REL),
        pool_b=g_pool_b.reshape(1, N_GROUPS, POOL_SHARD),
        pool_scale=lax.dynamic_slice_in_dim(tot[6:7], me * D_SHARD, D_SHARD, axis=1))
    adam_flat("pool_w", recv_pool)
    small_names = [nm for nm in names if nm not in ffn_names + ("w_k", "w_v", "w_q", "w_o", "pool_w")]

    def pack_small(tree):
        cols = []
        for nm in small_names:
            flat = tree[nm].reshape(-1)
            cols.append(jnp.pad(flat, (0, -flat.shape[0] % 1024)))
        return jnp.concatenate(cols).reshape(-1, 128)

    _, dl, m1, m2 = _adam(pack_small(weights), pack_small(grads), pack_small(mom1), pack_small(mom2), "adam_small")

    def unpack_small(packed, out):
        flat, off = packed.reshape(-1), 0
        for nm in small_names:
            size = weights[nm].size
            out[nm] = flat[off:off + size].reshape(weights[nm].shape)
            off += size + (-size % 1024)

    unpack_small(dl, deltas)
    unpack_small(m1, new_m)
    unpack_small(m2, new_v)

    recv0 = _chip_exchange_wait(flight, chip0, land0, dl, "scatter_ffn0_wait")
    for nm, g in zip(ffn_names, recv0):
        done = _adam_layer(ffn_view(nm, weights[nm]), g, ffn_view(nm, mom1[nm]), ffn_view(nm, mom2[nm]), 0,
                           "adam0_" + nm, prev=layer1[nm])
        grads[nm], deltas[nm], new_m[nm], new_v[nm] = [ffn_view(nm, a) for a in done]

    return (loss, grad_x[None], *[grads[nm] for nm in names], *[deltas[nm] for nm in names],
            *[new_m[nm] for nm in names], *[new_v[nm] for nm in names])
```

```python
import functools

import jax
import jax.numpy as jnp
from jax import lax
from jax.experimental import pallas as pl
from jax.experimental.pallas import tpu as pltpu

F32 = jnp.float32
BF16 = jnp.bfloat16
MESH_ID = pl.DeviceIdType.MESH

N_DEV = 8
S = 2048
D = 1024
F = 2816
F_SHARD = F // N_DEV
D_SHARD = D // N_DEV
N_GROUPS = 4
GROUP = D // N_GROUPS
POOL_SHARD = GROUP // N_DEV
MAX_WIN = 16
HEAD = 64
N_HEADS = D // HEAD
CHUNK = 64
LEFT = 8
QB = 4 * CHUNK
KB = QB + LEFT * CHUNK
PADK = LEFT * CHUNK
TOEP = 1024
N_REL = 257
MAX_REL = 128
EPS = 1e-6
NEG_INF = -1e30
ATTN_SCALE = HEAD ** -0.5

ADAM_LR = 0.001
ADAM_B1 = 0.9
ADAM_B2 = 0.999
ADAM_EPS = 1e-08
ADAM_WD = 0.01
ADAM_STEP = 10

VMEM_LIMIT = 52 * 1024 * 1024

ANY = pl.BlockSpec(memory_space=pl.ANY)
VMEM = pl.BlockSpec(memory_space=pltpu.VMEM)


def _call(body, **kw):
    return pl.pallas_call(body, **kw)


def _params(*sem):
    return pltpu.CompilerParams(dimension_semantics=sem, vmem_limit_bytes=VMEM_LIMIT)


def _dot(a, b, dims):
    return lax.dot_general(a, b, (dims, ((), ())), preferred_element_type=F32)


def _nn(a, b):
    return _dot(a, b, ((1,), (0,)))


def _nt(a, b):
    return _dot(a, b, ((1,), (1,)))


def _tn(a, b):
    return _dot(a, b, ((0,), (0,)))


def _rstd(x):
    return lax.rsqrt(jnp.mean(x * x, axis=-1, keepdims=True) + EPS)


def _rms_bwd(dh, x, r, g):
    gd = dh * g
    return r * gd - x * (r * r * r) * jnp.mean(gd * x, axis=-1, keepdims=True)


def _colsum8(v):
    return jnp.broadcast_to(jnp.sum(v, axis=0, keepdims=True), (8, v.shape[1]))


def _seg_sum(v):
    r = lax.broadcasted_iota(jnp.int32, (128, 128), 0) // HEAD
    c = lax.broadcasted_iota(jnp.int32, (128, 128), 1) // HEAD
    ones = jnp.where(r == c, 1.0, 0.0).astype(BF16)
    out = []
    for blk in range(v.shape[1] // 128):
        part = v[:, blk * 128:(blk + 1) * 128]
        hi = part.astype(BF16)
        rest = part - hi.astype(F32)
        mid = rest.astype(BF16)
        lo = (rest - mid.astype(F32)).astype(BF16)
        out.append(_nn(hi, ones) + _nn(mid, ones) + _nn(lo, ones))
    return jnp.concatenate(out, axis=1)


def _place():
    return lax.axis_index("x"), lax.axis_index("y"), lax.axis_index("c")


class _Exchange:
    def __init__(self, ins, out_shape, sems, start, mid, finish, early=None):
        self.ins, self.out_shape, self.sems = list(ins), list(out_shape), list(sems)
        self.start, self.mid, self.finish = start, mid, finish
        self.early = early if early is not None else (lambda ins, outs, sems: None)


def _relay_gather_exchange(shards):
    n = len(shards)

    def tools(ins, outs, sems):
        send_sems, recv_sems, local_sems = sems
        x, y, c = _place()
        me, sib = (x, y, c), (x, y, 1 - c)
        xn, yn, dg = (1 - x, y, c), (x, 1 - y, c), (1 - x, 1 - y, c)
        other = lambda dev: (dev[0], dev[1], 1 - c)

        def piece(k, ref, half):
            rows = shards[k].shape[0] // 2
            return ref if half is None else ref.at[pl.ds(half * rows, rows)]

        def held(k, dev, half):
            return piece(k, outs[k].at[4 * dev[0] + 2 * dev[1] + dev[2]], half)

        def copy(k, s, dev, half, to, own=False):
            return pltpu.make_async_remote_copy(
                src_ref=piece(k, ins[k], half) if own else held(k, dev, half), dst_ref=held(k, dev, half),
                send_sem=send_sems.at[k, s], recv_sem=recv_sems.at[k, s], device_id=to, device_id_type=MESH_ID)

        def mine(k):
            return pltpu.make_async_copy(ins[k], held(k, me, None), local_sems.at[k])

        def sent(k):
            return [copy(k, 0, me, None, sib, own=True), copy(k, 1, me, 0, xn, own=True), copy(k, 2, me, 1, yn, own=True),
                    copy(k, 3, me, 0, yn, own=True), copy(k, 4, me, 1, xn, own=True)]

        relays = [(1, xn, 0, [(5, yn), (7, sib)]), (2, yn, 1, [(6, xn), (9, sib)]), (3, yn, 0, [(10, sib)]),
                  (4, xn, 1, [(8, sib)]), (6, dg, 1, [(11, sib)]), (5, dg, 0, [(12, sib)])]
        from_sibling = [(0, other(me), None), (7, other(xn), 0), (8, other(xn), 1), (9, other(yn), 1),
                        (10, other(yn), 0), (11, other(dg), 1), (12, other(dg), 0)]
        return mine, sent, copy, relays, from_sibling, me

    def start(ins, outs, sems):
        mine, sent, _, _, _, _ = tools(ins, outs, sems)
        for k in range(n):
            mine(k).start()
        for k in range(n):
            for cp in sent(k):
                cp.start()

    def pass_on(ins, outs, sems, which):
        _, _, copy, relays, _, me = tools(ins, outs, sems)
        for k in range(n):
            for s, dev, half, onward in relays[which]:
                copy(k, s, dev, half, me).wait_recv()
                for s2, to in onward:
                    copy(k, s2, dev, half, to).start()

    def early(ins, outs, sems):
        pass_on(ins, outs, sems, slice(0, 2))

    def mid(ins, outs, sems):
        pass_on(ins, outs, sems, slice(2, None))

    def finish(ins, outs, sems):
        mine, sent, copy, relays, from_sibling, me = tools(ins, outs, sems)
        for k in range(n):
            for s, dev, half in from_sibling:
                copy(k, s, dev, half, me).wait_recv()
        for k in range(n):
            for cp in sent(k):
                cp.wait_send()
            for _, dev, half, onward in relays:
                for s2, to in onward:
                    copy(k, s2, dev, half, to).wait_send()
            mine(k).wait()

    return _Exchange(
        shards, [jax.ShapeDtypeStruct((N_DEV,) + a.shape, a.dtype) for a in shards],
        [pltpu.SemaphoreType.DMA((n, 13)), pltpu.SemaphoreType.DMA((n, 13)), pltpu.SemaphoreType.DMA((n,))],
        start, mid, finish, early)


def _peer(x, y, c, m):
    px = 1 - x if m & 4 else x
    py = 1 - y if m & 2 else y
    pc = 1 - c if m & 1 else c
    return px, py, pc


N_CHIPS = N_DEV // 2


def _pair_exchange(parts):
    n = len(parts)

    def copies(ins, outs, sems):
        send_sems, recv_sems = sems
        x, y, c = _place()
        return [pltpu.make_async_remote_copy(
            src_ref=ins[k].at[2 * q + 1 - c], dst_ref=outs[k].at[q], send_sem=send_sems.at[k, q],
            recv_sem=recv_sems.at[k, q], device_id=(x, y, 1 - c), device_id_type=MESH_ID)
            for k in range(n) for q in range(N_CHIPS)]

    def start(ins, outs, sems):
        for cp in copies(ins, outs, sems):
            cp.start()

    def mid(ins, outs, sems):
        pass

    def finish(ins, outs, sems):
        for cp in copies(ins, outs, sems):
            cp.wait_recv()
        for cp in copies(ins, outs, sems):
            cp.wait_send()

    return _Exchange(
        parts, [jax.ShapeDtypeStruct((N_CHIPS,) + a.shape[1:], a.dtype) for a in parts],
        [pltpu.SemaphoreType.DMA((n, N_CHIPS)), pltpu.SemaphoreType.DMA((n, N_CHIPS))], start, mid, finish)


def _pair_add(parts, stage, name):
    n = len(parts)
    place = jnp.stack([lax.axis_index("c"), 2 * lax.axis_index("x") + lax.axis_index("y")]).astype(jnp.int32)

    def body(place_ref, *refs):
        for k in range(n):
            mine, theirs = refs[k], refs[n + k]
            total = (mine[0, 0].astype(F32) + theirs[0].astype(F32)).astype(BF16)
            refs[2 * n + k][0] = total

            @pl.when(pl.program_id(1) == place_ref[1])
            def _():
                refs[3 * n + k][0] = total

    in_specs, out_specs, land_specs = [], [], []
    for a in parts:
        _, r, cdim = a.shape
        in_specs.append(pl.BlockSpec((1, 1, r // 2, cdim), lambda i, q, place_ref: (q, place_ref[0], i, 0)))
    for a in parts:
        _, r, cdim = a.shape
        in_specs.append(pl.BlockSpec((1, r // 2, cdim), lambda i, q, place_ref: (q, i, 0)))
        out_specs.append(pl.BlockSpec((1, r // 2, cdim), lambda i, q, place_ref: (q, i, 0)))
        land_specs.append(pl.BlockSpec((1, r // 2, cdim), lambda i, q, place_ref: (place_ref[1], i, 0)))
    return list(_call(
        body, name=name,
        grid_spec=pltpu.PrefetchScalarGridSpec(num_scalar_prefetch=1, grid=(2, N_CHIPS), in_specs=in_specs,
                                               out_specs=out_specs + land_specs),
        out_shape=[jax.ShapeDtypeStruct(s.shape, BF16) for s in stage] * 2,
        compiler_params=_params("arbitrary", "arbitrary"),
    )(place, *[a.reshape((N_CHIPS, 2) + a.shape[1:]) for a in parts], *stage))


HBM = pl.BlockSpec(memory_space=pltpu.HBM)
SEMAPHORES = pl.BlockSpec(memory_space=pltpu.SEMAPHORE)


def _chip_copies(srcs, lands, send_sems, recv_sems, mine_is_dst):
    x, y, c = _place()
    me = 2 * x + y
    out = []
    for m in range(1, N_CHIPS):
        px, py, _ = _peer(x, y, c, 2 * m)
        peer = 2 * px + py
        for k in range(len(srcs)):
            pair = k * (N_CHIPS - 1) + m - 1
            out.append(pltpu.make_async_remote_copy(
                src_ref=srcs[k].at[peer], dst_ref=lands[k].at[me if mine_is_dst else peer],
                send_sem=send_sems[pair], recv_sem=recv_sems[pair],
                device_id=(px, py, c), device_id_type=MESH_ID))
    return out


def _chip_exchange_start(chip_parts, lands, after, name):
    n = len(chip_parts)
    pairs = n * (N_CHIPS - 1)

    def body(*refs):
        srcs, zones = refs[:n], refs[n:2 * n]
        sems = refs[2 * n + 1:2 * n + 1 + 2 * pairs]
        token = refs[-1]
        for cp in _chip_copies(srcs, zones, sems[:pairs], sems[pairs:], True):
            cp.start()
        token[...] = jnp.zeros(token.shape, F32)

    thru = [pltpu.HBM(a.shape, a.dtype) for a in chip_parts + lands]
    hbm = [pltpu.with_memory_space_constraint(a, pltpu.HBM) for a in chip_parts + lands]
    res = _call(
        body, name=name,
        out_shape=[pltpu.SemaphoreType.DMA(())] * (2 * pairs) + thru + [jax.ShapeDtypeStruct((8, 128), F32)],
        in_specs=[HBM] * (2 * n) + [ANY], out_specs=[SEMAPHORES] * (2 * pairs) + [HBM] * (2 * n) + [VMEM],
        input_output_aliases={i: 2 * pairs + i for i in range(2 * n)},
        compiler_params=pltpu.CompilerParams(has_side_effects=pltpu.SideEffectType.DATAFLOW_SIDE_EFFECTING),
    )(*hbm, after)
    sems, rest = list(res[:2 * pairs]), res[2 * pairs:]
    return sems, list(rest[:n]), list(rest[n:2 * n]), rest[-1]


def _chip_exchange_wait(sems, chip_parts, lands, after, name):
    n = len(chip_parts)
    pairs = n * (N_CHIPS - 1)
    after = list(after) if isinstance(after, (list, tuple)) else [after]

    def body(*refs):
        srcs, zones = refs[:n], refs[n:2 * n]
        sem_refs = refs[2 * n:2 * n + 2 * pairs]
        for cp in _chip_copies(srcs, zones, sem_refs[:pairs], sem_refs[pairs:], False):
            cp.wait_send()
            cp.wait_recv()

    thru = [pltpu.HBM(a.shape, a.dtype) for a in chip_parts + lands]
    res = _call(
        body, name=name, out_shape=thru,
        in_specs=[HBM] * (2 * n) + [SEMAPHORES] * (2 * pairs) + [ANY] * len(after), out_specs=[HBM] * (2 * n),
        input_output_aliases={i: i for i in range(2 * n)},
        compiler_params=pltpu.CompilerParams(has_side_effects=pltpu.SideEffectType.DATAFLOW_SIDE_EFFECTING),
    )(*chip_parts, *lands, *sems, *after)
    return list(res[n:])


def _run_exchange(ex, name, after=None):
    n_in, n_out = len(ex.ins), len(ex.out_shape)
    order = [] if after is None else [after]

    def body(*refs):
        ins, outs = refs[:n_in], refs[n_in + len(order):n_in + len(order) + n_out]
        sems = refs[n_in + len(order) + n_out:]
        ex.start(ins, outs, sems)
        ex.early(ins, outs, sems)
        ex.mid(ins, outs, sems)
        ex.finish(ins, outs, sems)

    return list(_call(body, name=name, out_shape=ex.out_shape, in_specs=[ANY] * (n_in + len(order)),
                      out_specs=[ANY] * n_out, scratch_shapes=ex.sems)(*ex.ins, *order))


def _call_hosting(body, ex, phases, *, in_specs, out_specs, out_shape, scratch_shapes, args, **kw):
    n_in, n_out, n_scr = len(in_specs), len(out_specs), len(scratch_shapes)
    if ex is None:
        res = _call(body, in_specs=in_specs, out_specs=out_specs, out_shape=out_shape,
                    scratch_shapes=scratch_shapes, **kw)(*args)
        return list(res), []
    n_xin, n_xout = len(ex.ins), len(ex.out_shape)

    def hosting(*refs):
        a, b = n_in, n_in + n_xin
        c, d = b + n_out, b + n_out + n_xout
        ins, xins, outs, xouts = refs[:a], refs[a:b], refs[b:c], refs[c:d]
        scr, sems = refs[d:d + n_scr], refs[d + n_scr:]
        first, early, mid, last = phases()

        @pl.when(first)
        def _():
            ex.start(xins, xouts, sems)

        body(*ins, *outs, *scr)

        @pl.when(early)
        def _():
            ex.early(xins, xouts, sems)

        @pl.when(mid)
        def _():
            ex.mid(xins, xouts, sems)

        @pl.when(last)
        def _():
            ex.finish(xins, xouts, sems)

    res = _call(hosting, in_specs=list(in_specs) + [ANY] * n_xin, out_specs=list(out_specs) + [ANY] * n_xout,
                out_shape=list(out_shape) + ex.out_shape, scratch_shapes=list(scratch_shapes) + ex.sems,
                **kw)(*args, *ex.ins)
    return list(res[:n_out]), list(res[n_out:])


def _grid_phases(dims, mid_fraction=0.8, early_fraction=0.4):
    total = 1
    for d in dims:
        total *= d
    mid = min(max(int(total * mid_fraction), 1), total - 1)
    early = min(int(total * early_fraction), mid)

    def phases():
        step = pl.program_id(0)
        for axis in range(1, len(dims)):
            step = step * dims[axis] + pl.program_id(axis)
        return step == 0, step == early, step == mid, step == total - 1
    return phases


def _small_exchanges(pack, blocks, name, after):
    rows = pack.shape[0]

    def body(in_ref, blk_ref, *rest):
        out_ref, got_ref, recv, send_sems, recv_sems, blk_send, blk_recv, blk_local = rest[len(after):]
        x, y, c = _place()
        me = 4 * x + 2 * y + c
        peers = [_peer(x, y, c, m) for m in range(1, N_DEV)]
        index = [4 * px + 2 * py + pc for px, py, pc in peers]

        def block_copy(m, landing):
            return pltpu.make_async_remote_copy(
                src_ref=blk_ref.at[index[m]], dst_ref=got_ref.at[index[m] if landing else me],
                send_sem=blk_send.at[m], recv_sem=blk_recv.at[m], device_id=peers[m], device_id_type=MESH_ID)

        def pack_copy(m, landing):
            return pltpu.make_async_remote_copy(
                src_ref=in_ref, dst_ref=recv.at[index[m] if landing else me], send_sem=send_sems.at[m],
                recv_sem=recv_sems.at[m], device_id=peers[m], device_id_type=MESH_ID)

        own = pltpu.make_async_copy(blk_ref.at[me], got_ref.at[me], blk_local)
        own.start()
        recv[me] = in_ref[...]
        for m in range(N_DEV - 1):
            pack_copy(m, False).start()
            block_copy(m, False).start()
        for m in range(N_DEV - 1):
            pack_copy(m, True).wait_recv()
        acc = recv[0]
        for d in range(1, N_DEV):
            acc = acc + recv[d]
        out_ref[...] = acc
        for m in range(N_DEV - 1):
            block_copy(m, True).wait_recv()
        for m in range(N_DEV - 1):
            pack_copy(m, False).wait_send()
            block_copy(m, False).wait_send()
        own.wait()

    return _call(
        body, name=name,
        out_shape=[jax.ShapeDtypeStruct(pack.shape, F32), jax.ShapeDtypeStruct(blocks.shape, blocks.dtype)],
        in_specs=[VMEM, ANY] + [ANY] * len(after), out_specs=[VMEM, ANY],
        scratch_shapes=[pltpu.VMEM((N_DEV, rows, pack.shape[1]), F32)] + [pltpu.SemaphoreType.DMA((7,))] * 4
        + [pltpu.SemaphoreType.DMA],
        compiler_params=pltpu.CompilerParams(vmem_limit_bytes=VMEM_LIMIT),
    )(pack, blocks, *after)


POOL_TS = 256


def _pool_counts(first_row, rows, win):
    t = first_row + lax.broadcasted_iota(jnp.int32, (rows, 1), 0)
    return jnp.minimum(t + 1, win).astype(F32)


def _pool_fwd(x, g, w, b, scale):
    nt = S // POOL_TS

    def body(x_ref, g_ref, w_ref, b_ref, s_ref, out_ref, diff_ref, ext):
        i = pl.program_id(0)

        @pl.when(i == 0)
        def _():
            ext[0:MAX_WIN, :] = jnp.zeros((MAX_WIN, D), F32)

        @pl.when(i > 0)
        def _():
            ext[0:MAX_WIN, :] = ext[POOL_TS:POOL_TS + MAX_WIN, :]

        xv = x_ref[...]
        h = xv * _rstd(xv) * g_ref[...]
        ext[MAX_WIN:, :] = h
        for gi in range(N_GROUPS):
            win = 2 << gi
            cols = slice(gi * GROUP, (gi + 1) * GROUP)
            sm = ext[:, cols]
            k = 1
            while k < win:
                sm = sm + pltpu.roll(sm, k, axis=0)
                k *= 2
            pooled = sm[MAX_WIN:, :] / _pool_counts(i * POOL_TS, POOL_TS, win)
            diff = (pooled - h[:, cols]).astype(BF16)
            yv = (_nn(diff, w_ref[gi]) + b_ref[:, cols]) * s_ref[:, cols]
            out_ref[:, cols] = xv[:, cols] + yv
            diff_ref[:, cols] = diff

    row = pl.BlockSpec((1, D), lambda i: (0, 0))
    tile = pl.BlockSpec((POOL_TS, D), lambda i: (i, 0))
    return _call(
        body, name="pool_fwd", grid=(nt,),
        in_specs=[tile, row, pl.BlockSpec((N_GROUPS, GROUP, GROUP), lambda i: (0, 0, 0)), row, row],
        out_specs=[tile, tile],
        out_shape=[jax.ShapeDtypeStruct((S, D), F32), jax.ShapeDtypeStruct((S, D), BF16)],
        scratch_shapes=[pltpu.VMEM((POOL_TS + MAX_WIN, D), F32)],
        compiler_params=_params("arbitrary"),
    )(x, g, w, b, scale)


def _pool_bwd(dy, x, diff, g, w, b, scale, ex=None, after=()):
    nt = S // POOL_TS

    def body(dy_ref, x_ref, diff_ref, g_ref, w_ref, b_ref, s_ref, *rest):
        gx_ref, dw_ref, db_ref, ds_ref, dg_ref, ext, dh = rest[len(after):]
        i = pl.program_id(0)
        first_row = (nt - 1 - i) * POOL_TS

        @pl.when(i == 0)
        def _():
            ext[POOL_TS:, :] = jnp.zeros((MAX_WIN, D), F32)
            dw_ref[...] = jnp.zeros(dw_ref.shape, F32)
            db_ref[...] = jnp.zeros(db_ref.shape, F32)
            ds_ref[...] = jnp.zeros(ds_ref.shape, F32)
            dg_ref[...] = jnp.zeros(dg_ref.shape, F32)

        @pl.when(i > 0)
        def _():
            ext[POOL_TS:, :] = ext[0:MAX_WIN, :]

        dyv = dy_ref[...]
        for gi in range(N_GROUPS):
            win = 2 << gi
            cols = slice(gi * GROUP, (gi + 1) * GROUP)
            dfb = diff_ref[:, cols]
            z = _nn(dfb, w_ref[gi]) + b_ref[:, cols]
            dyg = dyv[:, cols]
            ds_ref[:, cols] += _colsum8(dyg * z)
            dz = dyg * s_ref[:, cols]
            db_ref[:, cols] += _colsum8(dz)
            dzb = dz.astype(BF16)
            dw_ref[gi] += _tn(dfb, dzb)
            ddiff = _nt(dzb, w_ref[gi])
            ext[0:POOL_TS, cols] = ddiff / _pool_counts(first_row, POOL_TS, win)
            sm = ext[:, cols]
            k = 1
            while k < win:
                sm = sm + pltpu.roll(sm, POOL_TS + MAX_WIN - k, axis=0)
                k *= 2
            dh[:, cols] = sm[0:POOL_TS, :] - ddiff
        xv = x_ref[...]
        r = _rstd(xv)
        gv = g_ref[...]
        dhv = dh[...]
        dg_ref[...] += _colsum8(dhv * xv * r)
        gx_ref[...] = dyv + _rms_bwd(dhv, xv, r, gv)

    row = pl.BlockSpec((1, D), lambda i: (0, 0))
    tile = pl.BlockSpec((POOL_TS, D), lambda i: (nt - 1 - i, 0))
    acc = pl.BlockSpec((8, D), lambda i: (0, 0))
    wspec = pl.BlockSpec((N_GROUPS, GROUP, GROUP), lambda i: (0, 0, 0))
    return _call_hosting(
        body, ex, _grid_phases((nt,)), name="pool_bwd", grid=(nt,),
        in_specs=[tile, tile, tile, row, wspec, row, row] + [ANY] * len(after),
        out_specs=[tile, wspec, acc, acc, acc],
        out_shape=[jax.ShapeDtypeStruct((S, D), F32), jax.ShapeDtypeStruct((N_GROUPS, GROUP, GROUP), F32),
                   jax.ShapeDtypeStruct((8, D), F32), jax.ShapeDtypeStruct((8, D), F32),
                   jax.ShapeDtypeStruct((8, D), F32)],
        scratch_shapes=[pltpu.VMEM((POOL_TS + MAX_WIN, D), F32), pltpu.VMEM((POOL_TS, D), F32)],
        args=(dy, x, diff, g, w, b, scale, *after), compiler_params=_params("arbitrary"),
    )


FFN_TS = min(S, 1024)
FFN_TF = 256


def _ffn_fwd(x, g, wg_t, wu_t, wd, name, target=None, ex=None):
    width = wd.shape[0]
    ni, nj = S // FFN_TS, width // FFN_TF
    with_loss = target is not None
    extra = [target] if with_loss else []

    def body(*refs):
        x_ref, g_ref, wg_ref, wu_ref, wd_ref = refs[:5]
        if with_loss:
            t_ref, out_ref, loss_ref, h_ref, gg_ref, uu_ref, hs, acc = refs[5:]
        else:
            out_ref, h_ref, gg_ref, uu_ref, hs, acc = refs[5:]
        i, j = pl.program_id(0), pl.program_id(1)

        @pl.when(j == 0)
        def _():
            xv = x_ref[...]
            hb = (xv * _rstd(xv) * g_ref[...]).astype(BF16)
            hs[...] = hb
            h_ref[...] = hb
            acc[...] = jnp.zeros(acc.shape, F32)

        hb = hs[...]
        gg = _nt(hb, wg_ref[...])
        uu = _nt(hb, wu_ref[...])
        gg_ref[...] = gg
        uu_ref[...] = uu
        a = (gg * jax.nn.sigmoid(gg) * uu).astype(BF16)
        acc[...] += _nn(a, wd_ref[...])

        @pl.when(j == nj - 1)
        def _():
            yv = x_ref[...] + acc[...]
            if with_loss:
                err = yv - t_ref[...]
                out_ref[...] = err * (1.0 / D)
                part = jnp.sum(err * err) * (0.5 / D)

                @pl.when(i == 0)
                def _():
                    loss_ref[...] = jnp.zeros(loss_ref.shape, F32)

                loss_ref[...] += jnp.broadcast_to(part, loss_ref.shape)
            else:
                out_ref[...] = yv

    xt = pl.BlockSpec((FFN_TS, D), lambda i, j: (i, 0))
    row = pl.BlockSpec((1, D), lambda i, j: (0, 0))
    wt = pl.BlockSpec((FFN_TF, D), lambda i, j: (j, 0))
    gt = pl.BlockSpec((FFN_TS, FFN_TF), lambda i, j: (i, j))
    in_specs = [xt, row, wt, wt, wt] + [xt] * len(extra)
    out_specs = [xt] + ([pl.BlockSpec((8, 128), lambda i, j: (0, 0))] if with_loss else []) + [xt, gt, gt]
    out_shape = ([jax.ShapeDtypeStruct((S, D), F32)] + ([jax.ShapeDtypeStruct((8, 128), F32)] if with_loss else [])
                 + [jax.ShapeDtypeStruct((S, D), BF16), jax.ShapeDtypeStruct((S, width), F32),
                    jax.ShapeDtypeStruct((S, width), F32)])
    args = (x, g, wg_t, wu_t, wd, *extra)
    return _call_hosting(
        body, ex, _grid_phases((ni, nj), 0.8), name=name, grid=(ni, nj),
        in_specs=in_specs, out_specs=out_specs, out_shape=out_shape,
        scratch_shapes=[pltpu.VMEM((FFN_TS, D), BF16), pltpu.VMEM((FFN_TS, D), F32)], args=args,
        compiler_params=_params("arbitrary", "arbitrary"),
    )


def _ffn_bwd_weights(dout, h, gg, uu, wd, name, ex=None):
    width = wd.shape[0]
    nj = width // FFN_TF

    def body(do_ref, h_ref, gg_ref, uu_ref, wd_ref, dg_ref, du_ref, dwg_ref, dwu_ref, dwd_ref, dob, kept):
        j = pl.program_id(0)

        @pl.when(j == 0)
        def _():
            dob[...] = do_ref[...].astype(BF16)
            kept[1] = jnp.zeros(kept.shape[1:], BF16)

        prev, cur = (j + 1) % 2, j % 2
        hb, dov = h_ref[...], dob[...]
        dwg_ref[...] = _tn(kept[prev, 0], hb).astype(BF16)
        dwu_ref[...] = _tn(kept[prev, 1], hb).astype(BF16)
        dwd_ref[...] = _tn(kept[prev, 2], dov).astype(BF16)
        gv, uv = gg_ref[...], uu_ref[...]
        da = _nt(dov, wd_ref[...])
        sg = jax.nn.sigmoid(gv)
        sl = gv * sg
        dub = (da * sl).astype(BF16)
        dgb = (da * uv * (sg * (1.0 + gv * (1.0 - sg)))).astype(BF16)
        dg_ref[...] = dgb
        du_ref[...] = dub
        kept[cur, 0] = dgb
        kept[cur, 1] = dub
        kept[cur, 2] = (sl * uv).astype(BF16)

    once = pl.Buffered(1)
    whole = lambda: pl.BlockSpec((S, D), lambda j, _: (0, 0), pipeline_mode=once)
    this = lambda j: jnp.minimum(j, nj - 1)
    last = lambda j: jnp.maximum(j - 1, 0)
    wt_in = pl.BlockSpec((FFN_TF, D), lambda j, _: (this(j), 0))
    wt_out = pl.BlockSpec((FFN_TF, D), lambda j, _: (last(j), 0))
    gt = pl.BlockSpec((S, FFN_TF), lambda j, _: (0, this(j)))
    return _call_hosting(
        body, ex, _grid_phases((nj + 1, 1)), name=name, grid=(nj + 1, 1),
        in_specs=[whole(), whole(), gt, gt, wt_in], out_specs=[gt, gt, wt_out, wt_out, wt_out],
        out_shape=[jax.ShapeDtypeStruct((S, width), BF16)] * 2 + [jax.ShapeDtypeStruct((width, D), BF16)] * 3,
        scratch_shapes=[pltpu.VMEM((S, D), BF16), pltpu.VMEM((2, 3, S, FFN_TF), BF16)], args=(dout, h, gg, uu, wd),
        compiler_params=_params("arbitrary", "arbitrary"),
    )


BWD_TS = 512


def _ffn_bwd_input(dres, x, g, grads, weights, name, ex=None, after=()):
    nt = S // BWD_TS
    n = len(grads)

    def body(*refs):
        dres_ref, x_ref, g_ref = refs[:3]
        grad_refs, w_refs = refs[3:3 + n], refs[3 + n:3 + 2 * n]
        dx_ref, dgam_ref = refs[3 + 2 * n + len(after):]
        dh = _nn(grad_refs[0][...], w_refs[0][...])
        for k in range(1, n):
            dh = dh + _nn(grad_refs[k][...], w_refs[k][...])
        xv = x_ref[...]
        r = _rstd(xv)
        dx_ref[...] = dres_ref[...] + _rms_bwd(dh, xv, r, g_ref[...])

        @pl.when(pl.program_id(0) == 0)
        def _():
            dgam_ref[...] = jnp.zeros(dgam_ref.shape, F32)

        dgam_ref[...] += _colsum8(dh * xv * r)

    tile = pl.BlockSpec((BWD_TS, D), lambda i: (i, 0))
    ftiles = [pl.BlockSpec((BWD_TS, a.shape[1]), lambda i: (i, 0)) for a in grads]
    wspecs = [pl.BlockSpec(w.shape, lambda i: (0, 0), pipeline_mode=pl.Buffered(1)) for w in weights]
    return _call_hosting(
        body, ex, _grid_phases((nt,)), name=name, grid=(nt,),
        in_specs=[tile, tile, pl.BlockSpec((1, D), lambda i: (0, 0))] + ftiles + wspecs + [ANY] * len(after),
        out_specs=[tile, pl.BlockSpec((8, D), lambda i: (0, 0))],
        out_shape=[jax.ShapeDtypeStruct((S, D), F32), jax.ShapeDtypeStruct((8, D), F32)],
        scratch_shapes=[], args=(dres, x, g, *grads, *weights, *after), compiler_params=_params("arbitrary"),
    )


def _mm(a, b, mode, out_dtype, name, add=None, skip_rows=0):
    if mode == "nn":
        (m, kd), n = (a.shape[0] - skip_rows, a.shape[1]), b.shape[1]
    elif mode == "nt":
        (m, kd), n = (a.shape[0] - skip_rows, a.shape[1]), b.shape[0]
    else:
        (kd, m), n = a.shape, b.shape[1]
    tm, tn, tk = min(m, 1024), min(n, 1024), min(kd, 1024)
    if skip_rows and mode == "tn":
        tk = min(tk, skip_rows)
    elif skip_rows:
        tm = min(tm, skip_rows)
    skip_a = skip_rows // tm if mode != "tn" else 0
    skip_b = skip_rows // tk if mode == "tn" else 0
    assert skip_rows == skip_a * tm + skip_b * tk
    nk = kd // tk
    dot = {"nn": _nn, "nt": _nt, "tn": _tn}[mode]

    def body(*refs):
        if add is None:
            a_ref, b_ref, o_ref, acc = refs
        else:
            a_ref, b_ref, add_ref, o_ref, acc = refs
        k = pl.program_id(2)

        @pl.when(k == 0)
        def _():
            acc[...] = jnp.zeros(acc.shape, F32)

        acc[...] += dot(a_ref[...].astype(BF16), b_ref[...].astype(BF16))

        @pl.when(k == nk - 1)
        def _():
            res = acc[...]
            if add is not None:
                res = res + add_ref[...]
            o_ref[...] = res.astype(out_dtype)

    if mode == "tn":
        a_spec = pl.BlockSpec((tk, tm), lambda i, j, k: (k, i))
        b_spec = pl.BlockSpec((tk, tn), lambda i, j, k: (k + skip_b, j))
    else:
        a_spec = pl.BlockSpec((tm, tk), lambda i, j, k: (i + skip_a, k))
        b_spec = (pl.BlockSpec((tk, tn), lambda i, j, k: (k, j)) if mode == "nn"
                  else pl.BlockSpec((tn, tk), lambda i, j, k: (j, k)))
    o_spec = pl.BlockSpec((tm, tn), lambda i, j, k: (i, j))
    in_specs = [a_spec, b_spec] + ([o_spec] if add is not None else [])
    args = (a, b) + ((add,) if add is not None else ())
    return _call(
        body, name=name, grid=(m // tm, n // tn, nk), in_specs=in_specs, out_specs=o_spec,
        out_shape=jax.ShapeDtypeStruct((m, n), out_dtype), scratch_shapes=[pltpu.VMEM((tm, tn), F32)],
        compiler_params=_params("parallel", "parallel", "arbitrary"),
    )(*args)


PROJ_TS = 256


def _kvq_proj(x, g_kv, g_mix, wk, wv, wq, gk, gq, ex=None):
    lead = PADK // PROJ_TS

    def body(x_ref, gkv_ref, gmix_ref, wk_ref, wv_ref, wq_ref, gk_ref, gq_ref,
             hkv_ref, h1_ref, kpre_ref, qpre_ref, k_ref, v_ref, q_ref):
        i = pl.program_id(0)

        @pl.when(i < lead)
        def _():
            k_ref[...] = jnp.zeros(k_ref.shape, BF16)
            v_ref[...] = jnp.zeros(v_ref.shape, BF16)

        @pl.when(i >= lead)
        def _():
            xv = x_ref[...]
            xr = xv * _rstd(xv)
            hkv = (xr * gkv_ref[...]).astype(BF16)
            h1 = (xr * gmix_ref[...]).astype(BF16)
            hkv_ref[...] = hkv
            h1_ref[...] = h1
            kpre = _nn(hkv, wk_ref[...])
            qpre = _nn(h1, wq_ref[...])
            kpre_ref[...] = kpre
            qpre_ref[...] = qpre
            v_ref[...] = _nn(hkv, wv_ref[...]).astype(BF16)
            rk = lax.rsqrt(_seg_sum(kpre * kpre) * (1.0 / HEAD) + EPS)
            k_ref[...] = (kpre * rk * gk_ref[...]).astype(BF16)
            rq = lax.rsqrt(_seg_sum(qpre * qpre) * (1.0 / HEAD) + EPS)
            q_ref[...] = (qpre * rq * gq_ref[...]).astype(BF16)

    tile = pl.BlockSpec((PROJ_TS, D), lambda i: (jnp.maximum(i - lead, 0), 0))
    padded = pl.BlockSpec((PROJ_TS, D), lambda i: (i, 0))
    row = pl.BlockSpec((1, D), lambda i: (0, 0))
    wspec = pl.BlockSpec((D, D), lambda i: (0, 0))
    bf = jax.ShapeDtypeStruct((S, D), BF16)
    ff = jax.ShapeDtypeStruct((S, D), F32)
    bp = jax.ShapeDtypeStruct((PADK + S, D), BF16)
    return _call_hosting(
        body, ex, _grid_phases((lead + S // PROJ_TS,), 0.85), name="kvq_proj", grid=(lead + S // PROJ_TS,),
        in_specs=[tile, row, row, wspec, wspec, wspec, row, row],
        out_specs=[tile, tile, tile, tile, padded, padded, tile], out_shape=[bf, bf, ff, ff, bp, bp, bf],
        scratch_shapes=[], args=(x, g_kv, g_mix, wk, wv, wq, gk, gq), compiler_params=_params("arbitrary"),
    )


def _qkv_input_bwd(dq, qpre, gq, dkp, kpre, gk, dvp, wq, wk, wv, dres, x, g_mix, g_kv, ex=None):
    nt = S // PROJ_TS
    lead = PADK // PROJ_TS

    def head_bwd(dov, pv, hgv):
        r = lax.rsqrt(_seg_sum(pv * pv) * (1.0 / HEAD) + EPS)
        gd = dov * hgv
        dpre = r * gd - pv * (r * r * r) * (_seg_sum(gd * pv) * (1.0 / HEAD))
        return dpre.astype(BF16), _colsum8(dov * pv * r)

    def fold_heads(full):
        fold = full[:, 0:128]
        for blk in range(1, D // 128):
            fold = fold + full[:, blk * 128:(blk + 1) * 128]
        return fold + pltpu.roll(fold, HEAD, axis=1)

    def body(dq_ref, qpre_ref, gq_ref, dk_ref, kpre_ref, gk_ref, dv_ref, wq_ref, wk_ref, wv_ref, dres_ref, x_ref,
             gmix_ref, gkv_ref, dx_ref, dqpre_ref, dkpre_ref, dgmix_ref, dgkv_ref, dgq_ref, dgk_ref, accq, acck):
        i = pl.program_id(0)

        @pl.when(i == 0)
        def _():
            accq[...] = jnp.zeros(accq.shape, F32)
            acck[...] = jnp.zeros(acck.shape, F32)
            dgmix_ref[...] = jnp.zeros(dgmix_ref.shape, F32)
            dgkv_ref[...] = jnp.zeros(dgkv_ref.shape, F32)

        dqb, cq = head_bwd(dq_ref[...], qpre_ref[...], gq_ref[...])
        dkb, ck = head_bwd(dk_ref[...], kpre_ref[...], gk_ref[...])
        dqpre_ref[...] = dqb
        dkpre_ref[...] = dkb
        accq[...] += cq
        acck[...] += ck
        dh1 = _nt(dqb, wq_ref[...])
        dhkv = _nt(dkb, wk_ref[...]) + _nt(dv_ref[...].astype(BF16), wv_ref[...])
        xv = x_ref[...]
        r = _rstd(xv)
        dx_ref[...] = dres_ref[...] + _rms_bwd(dh1, xv, r, gmix_ref[...]) + _rms_bwd(dhkv, xv, r, gkv_ref[...])
        dgmix_ref[...] += _colsum8(dh1 * xv * r)
        dgkv_ref[...] += _colsum8(dhkv * xv * r)

        @pl.when(i == nt - 1)
        def _():
            dgq_ref[...] = fold_heads(accq[...])
            dgk_ref[...] = fold_heads(acck[...])

    tile = pl.BlockSpec((PROJ_TS, D), lambda i: (i, 0))
    behind = pl.BlockSpec((PROJ_TS, D), lambda i: (i + lead, 0))
    row = pl.BlockSpec((1, D), lambda i: (0, 0))
    wspec = lambda: pl.BlockSpec((D, D), lambda i: (0, 0), pipeline_mode=pl.Buffered(1))
    acc = pl.BlockSpec((8, D), lambda i: (0, 0))
    small = pl.BlockSpec((8, 128), lambda i: (0, 0))
    bf = jax.ShapeDtypeStruct((S, D), BF16)
    return _call_hosting(
        body, ex, _grid_phases((nt,)), name="qkv_input_bwd", grid=(nt,),
        in_specs=[tile, tile, row, behind, tile, row, behind, wspec(), wspec(), wspec(), tile, tile, row, row],
        out_specs=[tile, tile, tile, acc, acc, small, small],
        out_shape=[jax.ShapeDtypeStruct((S, D), F32), bf, bf, jax.ShapeDtypeStruct((8, D), F32),
                   jax.ShapeDtypeStruct((8, D), F32), jax.ShapeDtypeStruct((8, 128), F32),
                   jax.ShapeDtypeStruct((8, 128), F32)],
        scratch_shapes=[pltpu.VMEM((8, D), F32), pltpu.VMEM((8, D), F32)],
        args=(dq, qpre, gq, dkp, kpre, gk, dvp, wq, wk, wv, dres, x, g_mix, g_kv),
        compiler_params=_params("arbitrary"),
    )


def _toeplitz_from_table(table):
    far = jnp.broadcast_to(table[:, N_REL - 1:], (N_HEADS, PADK - MAX_REL + 1))
    near = table[:, N_REL - 2::-1]
    past = jnp.broadcast_to(table[:, 0:1], (N_HEADS, MAX_REL))
    wrap = jnp.broadcast_to(table[:, N_REL - 1:], (N_HEADS, TOEP - (PADK + 2 * MAX_REL + 1)))
    return jnp.concatenate([far, near, past, wrap], axis=1).reshape(N_HEADS, 1, TOEP)


def _table_grad_from_toeplitz(dtp, seg):
    lo = PADK - MAX_REL + 1
    near = dtp[:, lo + N_REL - 3:lo - 1:-1]
    return jnp.concatenate([seg[:, 1:2], near, seg[:, 0:1]], axis=1)


def _bias_band(tp):
    def body(tp_ref, out_ref):
        bv = pltpu.roll(jnp.broadcast_to(tp_ref[0], (QB, TOEP)), 0, axis=1, stride=1, stride_axis=0)
        out_ref[0] = jnp.where(_band_mask(), bv[:, 0:KB], NEG_INF)

    return _call(
        body, name="bias_band", grid=(N_HEADS,),
        in_specs=[pl.BlockSpec((1, 1, TOEP), lambda h: (h, 0, 0))],
        out_specs=pl.BlockSpec((1, QB, KB), lambda h: (h, 0, 0)),
        out_shape=jax.ShapeDtypeStruct((N_HEADS, QB, KB), F32),
        compiler_params=_params("parallel"),
    )(tp)


def _bias_grad(dband, after):
    lo, hi = PADK - MAX_REL + 1, PADK + MAX_REL

    def body(db_ref, _, dtp_ref, seg_ref):
        bv = jnp.concatenate([db_ref[0], jnp.zeros((QB, TOEP - KB), F32)], axis=1)
        row = lax.broadcasted_iota(jnp.int32, (QB, TOEP), 0)
        k = 1
        while k < QB:
            bv = jnp.where((row & k) != 0, pltpu.roll(bv, TOEP - k, axis=1), bv)
            k *= 2
        col = jnp.sum(bv, axis=0, keepdims=True)
        dtp_ref[0] = col
        u = lax.broadcasted_iota(jnp.int32, (1, TOEP), 1)
        far = jnp.sum(jnp.where((u < lo) | (u > hi + MAX_REL), col, 0.0))
        past = jnp.sum(jnp.where((u >= hi) & (u <= hi + MAX_REL), col, 0.0))
        lane = lax.broadcasted_iota(jnp.int32, (1, 128), 1)
        seg_ref[0] = jnp.where(lane == 0, far, jnp.where(lane == 1, past, 0.0))

    return _call(
        body, name="bias_grad", grid=(N_HEADS,),
        in_specs=[pl.BlockSpec((1, QB, KB), lambda h: (h, 0, 0)), ANY],
        out_specs=[pl.BlockSpec((1, 1, TOEP), lambda h: (h, 0, 0)), pl.BlockSpec((1, 1, 128), lambda h: (h, 0, 0))],
        out_shape=[jax.ShapeDtypeStruct((N_HEADS, 1, TOEP), F32), jax.ShapeDtypeStruct((N_HEADS, 1, 128), F32)],
        compiler_params=_params("parallel"),
    )(dband, after)


N_QB = S // QB
HEADS_PER_STEP = 4
ATT_LANES = HEADS_PER_STEP * HEAD
N_HG = D // ATT_LANES


def _band_mask():
    qc = lax.broadcasted_iota(jnp.int32, (QB, KB), 0) // CHUNK
    kc = lax.broadcasted_iota(jnp.int32, (QB, KB), 1) // CHUNK
    return (kc >= qc) & (kc <= qc + LEFT)


def _half_scale(hh, scale):
    lane = lax.broadcasted_iota(jnp.int32, (1, 128), 1)
    return jnp.where((lane < HEAD) == (hh == 0), scale, 0.0).astype(BF16)


def _probs(qh, kb, bias, first_key):
    sc = _nt(qh, kb) + bias
    if first_key is not None:
        sc = jnp.where(lax.broadcasted_iota(jnp.int32, (QB, KB), 1) >= first_key, sc, NEG_INF)
    e = jnp.exp(sc - jnp.max(sc, axis=-1, keepdims=True))
    return e * (1.0 / jnp.sum(e, axis=-1, keepdims=True))


def _by_padding(cb, compute):
    @pl.when(cb < PADK // QB)
    def _():
        compute(PADK - cb * QB)

    @pl.when(cb >= PADK // QB)
    def _():
        compute(None)


def _attn_fwd(q, kp, vp, bias, ex=None):
    def body(q_ref, k_ref, v_ref, b_ref, o_ref):
        cb = pl.program_id(1)
        band = pl.ds(pl.multiple_of(cb * QB, QB), KB)
        low = lax.broadcasted_iota(jnp.int32, (QB, 128), 1) < HEAD

        def compute(first_key):
            for pair in range(HEADS_PER_STEP // 2):
                lanes = pl.ds(pair * 128, 128)
                kb, vb, qv = k_ref[band, lanes], v_ref[band, lanes], q_ref[:, lanes]
                outs = []
                for hh in range(2):
                    pb = _probs(qv * _half_scale(hh, ATTN_SCALE), kb, b_ref[2 * pair + hh], first_key).astype(BF16)
                    outs.append(_nn(pb, vb))
                o_ref[:, lanes] = jnp.where(low, outs[0], outs[1]).astype(BF16)

        _by_padding(cb, compute)

    qspec = pl.BlockSpec((QB, ATT_LANES), lambda hg, cb: (cb, hg))
    kspec = pl.BlockSpec((PADK + S, ATT_LANES), lambda hg, cb: (0, hg))
    return _call_hosting(
        body, ex, _grid_phases((N_HG, N_QB), 0.85), name="attn_fwd", grid=(N_HG, N_QB),
        in_specs=[qspec, kspec, kspec, pl.BlockSpec((HEADS_PER_STEP, QB, KB), lambda hg, cb: (hg, 0, 0))],
        out_specs=[qspec], out_shape=[jax.ShapeDtypeStruct((S, D), BF16)], scratch_shapes=[],
        args=(q, kp, vp, bias), compiler_params=_params("arbitrary", "arbitrary"),
    )


def _attn_bwd(q, kp, vp, bias, do, ex=None):
    def body(q_ref, k_ref, v_ref, b_ref, do_ref, dq_ref, dk_ref, dv_ref, db_ref):
        cb = pl.program_id(1)

        @pl.when(cb == 0)
        def _():
            dk_ref[...] = jnp.zeros(dk_ref.shape, F32)
            dv_ref[...] = jnp.zeros(dv_ref.shape, F32)
            db_ref[...] = jnp.zeros(db_ref.shape, F32)

        band = pl.ds(pl.multiple_of(cb * QB, QB), KB)
        low = lax.broadcasted_iota(jnp.int32, (QB, 128), 1) < HEAD

        def compute(first_key):
            for pair in range(HEADS_PER_STEP // 2):
                lanes = pl.ds(pair * 128, 128)
                kb, vb = k_ref[band, lanes], v_ref[band, lanes]
                qv, dov = q_ref[:, lanes], do_ref[:, lanes]
                dq = jnp.zeros((QB, 128), F32)
                dkb = jnp.zeros((KB, 128), F32)
                dvb = jnp.zeros((KB, 128), F32)
                for hh in range(2):
                    sel = low if hh == 0 else jnp.logical_not(low)
                    doh = dov * _half_scale(hh, 1.0)
                    p = _probs(qv * _half_scale(hh, ATTN_SCALE), kb, b_ref[2 * pair + hh], first_key)
                    dp = _nt(doh, vb)
                    dvb = dvb + _tn(p.astype(BF16), doh)
                    ds = p * (dp - jnp.sum(dp * p, axis=-1, keepdims=True))
                    db_ref[2 * pair + hh] += ds
                    dsb = (ds * ATTN_SCALE).astype(BF16)
                    dq = dq + jnp.where(sel, _nn(dsb, kb), 0.0)
                    dkb = dkb + _tn(dsb, qv * _half_scale(hh, 1.0))
                dq_ref[:, lanes] = dq
                dk_ref[band, lanes] += dkb
                dv_ref[band, lanes] += dvb

        _by_padding(cb, compute)

    qspec = pl.BlockSpec((QB, ATT_LANES), lambda hg, cb: (cb, hg))
    kspec = pl.BlockSpec((PADK + S, ATT_LANES), lambda hg, cb: (0, hg))
    bspec = pl.BlockSpec((HEADS_PER_STEP, QB, KB), lambda hg, cb: (hg, 0, 0))
    kf = jax.ShapeDtypeStruct((PADK + S, D), F32)
    return _call_hosting(
        body, ex, _grid_phases((N_HG, N_QB)), name="attn_bwd", grid=(N_HG, N_QB),
        in_specs=[qspec, kspec, kspec, bspec, qspec],
        out_specs=[qspec, kspec, kspec, bspec],
        out_shape=[jax.ShapeDtypeStruct((S, D), F32), kf, kf, jax.ShapeDtypeStruct((N_HEADS, QB, KB), F32)],
        scratch_shapes=[], args=(q, kp, vp, bias, do), compiler_params=_params("arbitrary", "arbitrary"),
    )


def _adam_math(w, g, m, v):
    m = ADAM_B1 * m + (1.0 - ADAM_B1) * g
    v = ADAM_B2 * v + (1.0 - ADAM_B2) * (g * g)
    m_hat = m / (1.0 - ADAM_B1 ** ADAM_STEP)
    v_hat = v / (1.0 - ADAM_B2 ** ADAM_STEP)
    delta = -ADAM_LR * (m_hat / (jnp.sqrt(v_hat) + ADAM_EPS) + ADAM_WD * w)
    return delta, m, v


def _row_tile(r):
    for cand in (256, 176, 128, 64, 32, 16, 8):
        if r % cand == 0:
            return cand
    return r


def _adam_layer(w, parts, m, v, layer, name, prev=None):
    nl, r, c = w.shape
    count = parts.shape[0]
    tr = _row_tile(r)

    def body(w_ref, g_ref, m_ref, v_ref, *rest):
        go_ref, d_ref, nm_ref, nv_ref = rest[-4:]
        gv = g_ref[0].astype(F32)
        for q in range(1, count):
            gv = gv + g_ref[q].astype(F32)
        delta, nm, nv = _adam_math(w_ref[0], gv, m_ref[0], v_ref[0])
        go_ref[0] = gv
        d_ref[0] = delta
        nm_ref[0] = nm
        nv_ref[0] = nv

    lspec = pl.BlockSpec((1, tr, c), lambda i: (layer, i, 0))
    sd = jax.ShapeDtypeStruct((nl, r, c), F32)
    extra = list(prev) if prev is not None else []
    return _call(
        body, name=name, grid=(r // tr,),
        in_specs=[lspec, pl.BlockSpec((count, tr, c), lambda i: (0, i, 0)), lspec, lspec] + [ANY] * len(extra),
        out_specs=[lspec] * 4, out_shape=[sd] * 4,
        input_output_aliases={4 + t: t for t in range(len(extra))},
        compiler_params=_params("parallel"),
    )(w, parts, m, v, *extra)


def _adam(w, g, m, v, name):
    r, c = w.shape
    tr = _row_tile(r)
    count = g.shape[0] if g.ndim == 3 else 0

    def body(w_ref, g_ref, m_ref, v_ref, go_ref, d_ref, nm_ref, nv_ref):
        if count:
            gv = g_ref[0].astype(F32)
            for q in range(1, count):
                gv = gv + g_ref[q].astype(F32)
        else:
            gv = g_ref[...]
        delta, nm, nv = _adam_math(w_ref[...], gv, m_ref[...], v_ref[...])
        go_ref[...] = gv
        d_ref[...] = delta
        nm_ref[...] = nm
        nv_ref[...] = nv

    spec = pl.BlockSpec((tr, c), lambda i: (i, 0))
    gspec = pl.BlockSpec((count, tr, c), lambda i: (0, i, 0)) if count else spec
    sd = jax.ShapeDtypeStruct((r, c), F32)
    return _call(
        body, name=name, grid=(r // tr,), in_specs=[spec, gspec, spec, spec], out_specs=[spec] * 4,
        out_shape=[sd] * 4, compiler_params=_params("parallel"),
    )(w, g, m, v)


def _pad16(rows):
    return jnp.pad(rows, ((0, 16 - rows.shape[0]), (0, 0)))


def kernel(x, norm_mix_g, norm_ffn_g, pool_w, pool_b, pool_scale, kv_norm_g, w_k, w_v, k_norm_g, w_q, q_norm_g, rel_bias, w_o, w_gate, w_up, w_down, loss_target, m_norm_mix_g, m_norm_ffn_g, m_pool_w, m_pool_b, m_pool_scale, m_kv_norm_g, m_w_k, m_w_v, m_k_norm_g, m_w_q, m_q_norm_g, m_rel_bias, m_w_o, m_w_gate, m_w_up, m_w_down, v_norm_mix_g, v_norm_ffn_g, v_pool_w, v_pool_b, v_pool_scale, v_kv_norm_g, v_w_k, v_w_v, v_k_norm_g, v_w_q, v_q_norm_g, v_rel_bias, v_w_o, v_w_gate, v_w_up, v_w_down):
    assert x.shape == (1, S, D) and w_gate.shape == (2, D, F_SHARD) and w_k.shape == (D_SHARD, D)
    xin, target = x[0], loss_target[0]

    ffn_shards = [[w_gate[layer].T.astype(BF16), w_up[layer].T.astype(BF16), w_down[layer].astype(BF16)]
                  for layer in range(2)]
    att_shards = [w_k.astype(BF16), w_v.astype(BF16), w_q[0].astype(BF16), w_o[0].astype(BF16)]
    pool_shard = pool_w[0].astype(BF16).reshape(N_GROUPS * POOL_SHARD, GROUP)
    small = jnp.concatenate([pool_b[0].reshape(1, N_GROUPS * POOL_SHARD), pool_scale], axis=1)

    full0 = _run_exchange(_relay_gather_exchange(ffn_shards[0] + [pool_shard, _pad16(small)]), "gather_layer0")
    ffn_w0 = [a.reshape(F, D) for a in full0[:3]]
    pw_f = full0[3].reshape(N_DEV, N_GROUPS, POOL_SHARD, GROUP).transpose(1, 0, 2, 3).reshape(N_GROUPS, GROUP, GROUP)
    small_f = full0[4][:, 0, :]
    pb_f = small_f[:, :N_GROUPS * POOL_SHARD].reshape(N_DEV, N_GROUPS, POOL_SHARD).transpose(1, 0, 2).reshape(1, D)
    ps_f = small_f[:, N_GROUPS * POOL_SHARD:].reshape(1, D)

    g_mix0, g_mix1 = norm_mix_g[0:1], norm_mix_g[1:2]
    g_ffn0, g_ffn1 = norm_ffn_g[0:1], norm_ffn_g[1:2]
    g_kv = kv_norm_g.reshape(1, D)
    gk_t = jnp.tile(k_norm_g.reshape(1, HEAD), (1, N_HEADS))
    gq_t = jnp.tile(q_norm_g.reshape(1, HEAD), (1, N_HEADS))

    x1, diff = _pool_fwd(xin, g_mix0, pw_f, pb_f, ps_f)
    (x2, hf0, gg0, uu0), full_att = _ffn_fwd(x1, g_ffn0, *ffn_w0, name="ffn_fwd", ex=_relay_gather_exchange(att_shards))
    wk_f, wv_f, wq_f, wo_f = [a.reshape(D, D) for a in full_att]
    (hkv, h1, kpre, qpre, kp, vp, qq), full1_gate = _kvq_proj(x2, g_kv, g_mix1, wk_f, wv_f, wq_f, gk_t, gq_t,
                                                                ex=_relay_gather_exchange(ffn_shards[1][:1]))
    bias = _bias_band(_toeplitz_from_table(rel_bias[0]))
    (att,), full1_rest = _attn_fwd(qq, kp, vp, bias, ex=_relay_gather_exchange(ffn_shards[1][1:]))
    ffn_w1 = [a.reshape(F, D) for a in full1_gate + full1_rest]
    x3 = _mm(att, wo_f, "nn", F32, "attn_out", add=x2)
    (dx4, loss_rows, hf1, gg1, uu1), _ = _ffn_fwd(x3, g_ffn1, *ffn_w1, name="ffn_fwd_loss", target=target)

    def blocks(dw):
        return dw.reshape(N_DEV, dw.shape[0] // N_DEV, dw.shape[1])

    (dgg1, duu1, dwg1, dwu1, dwd1), _ = _ffn_bwd_weights(dx4, hf1, gg1, uu1, ffn_w1[2], name="ffn_bwd1")
    (dx3, dg_ffn1), _ = _ffn_bwd_input(dx4, x3, g_ffn1, [dgg1, duu1], ffn_w1[:2], "ffn_dx1")
    datt = _mm(dx3, wo_f, "nt", BF16, "d_attn")
    dwo = _mm(att, dx3, "tn", BF16, "d_wo")
    parts1 = [blocks(dw) for dw in (dwg1, dwu1, dwd1, dwo)]
    (dq, dkp, dvp, dband), stage1 = _attn_bwd(qq, kp, vp, bias, datt, ex=_pair_exchange(parts1))
    both1 = _pair_add(parts1, stage1, "pair_add_ffn1")
    flight1, chip1, land1, _ = _chip_exchange_start(both1[:4], both1[4:], dq, "scatter_ffn1_start")
    (dx2, dqpre, dkpre, dg_mix1, dg_kv, dgq, dgk), _ = _qkv_input_bwd(
        dq, qpre, gq_t, dkp, kpre, gk_t, dvp, wq_f, wk_f, wv_f, dx3, x2, g_mix1, g_kv)
    dwq = _mm(h1, dqpre, "tn", BF16, "d_wq")
    dwk = _mm(hkv, dkpre, "tn", BF16, "d_wk")
    dwv = _mm(hkv, dvp, "tn", BF16, "d_wv", skip_rows=PADK)
    parts_att = [blocks(dw) for dw in (dwk, dwv, dwq)]
    (dgg0, duu0, *dw0), stage_att = _ffn_bwd_weights(
        dx2, hf0, gg0, uu0, ffn_w0[2], name="ffn_bwd0", ex=_pair_exchange(parts_att))
    both_att = _pair_add(parts_att, stage_att, "pair_add_att")
    flight_att, chip_att, land_att, token_att = _chip_exchange_start(both_att[:3], both_att[3:], dx2,
                                                                     "scatter_att_start")
    parts0 = [blocks(dw) for dw in dw0]
    (dx1, dg_ffn0), stage0 = _ffn_bwd_input(dx2, x1, g_ffn0, [dgg0, duu0], ffn_w0[:2], "ffn_dx0",
                                            ex=_pair_exchange(parts0), after=[token_att])
    both = _pair_add(parts0, stage0, "pair_add_ffn0")
    flight, chip0, land0, token = _chip_exchange_start(both[:3], both[3:], dx1, "scatter_ffn0_start")
    (grad_x, dpw, db_rows, ds_rows, dg_mix0), _ = _pool_bwd(dx1, xin, diff, g_mix0, pw_f, pb_f, ps_f,
                                                            after=[token])
    dtp, seg = _bias_grad(dband, dpw)

    weights = dict(norm_mix_g=norm_mix_g, norm_ffn_g=norm_ffn_g, pool_w=pool_w, pool_b=pool_b,
                   pool_scale=pool_scale, kv_norm_g=kv_norm_g, w_k=w_k, w_v=w_v, k_norm_g=k_norm_g, w_q=w_q,
                   q_norm_g=q_norm_g, rel_bias=rel_bias, w_o=w_o, w_gate=w_gate, w_up=w_up, w_down=w_down)
    mom1 = dict(norm_mix_g=m_norm_mix_g, norm_ffn_g=m_norm_ffn_g, pool_w=m_pool_w, pool_b=m_pool_b,
                pool_scale=m_pool_scale, kv_norm_g=m_kv_norm_g, w_k=m_w_k, w_v=m_w_v, k_norm_g=m_k_norm_g,
                w_q=m_w_q, q_norm_g=m_q_norm_g, rel_bias=m_rel_bias, w_o=m_w_o, w_gate=m_w_gate, w_up=m_w_up,
                w_down=m_w_down)
    mom2 = dict(norm_mix_g=v_norm_mix_g, norm_ffn_g=v_norm_ffn_g, pool_w=v_pool_w, pool_b=v_pool_b,
                pool_scale=v_pool_scale, kv_norm_g=v_kv_norm_g, w_k=v_w_k, w_v=v_w_v, k_norm_g=v_k_norm_g,
                w_q=v_w_q, q_norm_g=v_q_norm_g, rel_bias=v_rel_bias, w_o=v_w_o, w_gate=v_w_gate, w_up=v_w_up,
                w_down=v_w_down)
    names = list(weights)
    grads, deltas, new_m, new_v = {}, {}, {}, {}

    def adam_flat(nm, parts):
        shape = weights[nm].shape
        flat = lambda a: a.reshape(-1, shape[-1])
        done = _adam(flat(weights[nm]), parts, flat(mom1[nm]), flat(mom2[nm]), "adam_" + nm)
        grads[nm], deltas[nm], new_m[nm], new_v[nm] = [a.reshape(shape) for a in done]

    ffn_names = ("w_gate", "w_up", "w_down")
    def ffn_view(nm, a):
        return a if nm == "w_down" else a.transpose(0, 2, 1)

    recv1 = _chip_exchange_wait(flight1, chip1, land1, dtp, "scatter_ffn1_wait")
    layer1 = {nm: _adam_layer(ffn_view(nm, weights[nm]), g, ffn_view(nm, mom1[nm]), ffn_view(nm, mom2[nm]), 1,
                              "adam1_" + nm)
              for nm, g in zip(ffn_names, recv1)}
    adam_flat("w_o", recv1[3])
    done1 = [layer1[nm][1] for nm in ffn_names] + [deltas["w_o"]]
    recv_att = _chip_exchange_wait(flight_att, chip_att, land_att, done1, "scatter_att_wait")
    for nm, parts in zip(("w_k", "w_v", "w_q"), recv_att):
        adam_flat(nm, parts)

    dpw_blocks = dpw.reshape(N_GROUPS, N_DEV, POOL_SHARD, GROUP).transpose(1, 0, 2, 3)
    dpw_blocks = dpw_blocks.reshape(N_DEV, N_GROUPS * POOL_SHARD, GROUP).astype(BF16)
    misc = jnp.concatenate([dgk[0:1, 0:HEAD], dgq[0:1, 0:HEAD], loss_rows[0:1, 0:1],
                            seg[:, 0, 0].reshape(1, N_HEADS), seg[:, 0, 1].reshape(1, N_HEADS)], axis=1)
    misc = jnp.pad(misc, ((0, 0), (0, D - misc.shape[1])))
    vec_rows = jnp.concatenate([dg_mix0[0:1], dg_mix1[0:1], dg_ffn0[0:1], dg_ffn1[0:1], dg_kv[0:1],
                                db_rows[0:1], ds_rows[0:1], misc], axis=0)
    pack = jnp.concatenate([vec_rows, dtp.reshape(N_HEADS, TOEP)], axis=0)
    tot, recv_pool = _small_exchanges(pack, dpw_blocks, "small_exchanges",
                                      after=[deltas[nm] for nm in ("w_k", "w_v", "w_q")])

    loss = tot[7, 2 * HEAD]
    seg_tot = jnp.stack([tot[7, 2 * HEAD + 1:2 * HEAD + 1 + N_HEADS],
                         tot[7, 2 * HEAD + 1 + N_HEADS:2 * HEAD + 1 + 2 * N_HEADS]], axis=1)
    me = 4 * lax.axis_index("x") + 2 * lax.axis_index("y") + lax.axis_index("c")
    g_pool_b = lax.dynamic_slice_in_dim(tot[5].reshape(N_GROUPS, GROUP), me * POOL_SHARD, POOL_SHARD, axis=1)
    grads.update(
        norm_mix_g=tot[0:2], norm_ffn_g=tot[2:4], kv_norm_g=tot[4], k_norm_g=tot[7, 0:HEAD],
        q_norm_g=tot[7, HEAD:2 * HEAD].reshape(1, HEAD),
        rel_bias=_table_grad_from_toeplitz(tot[8:8 + N_HEADS], seg_tot).reshape(1, N_HEADS, N_REL),
        pool_b=g_pool_b.reshape(1, N_GROUPS, POOL_SHARD),
        pool_scale=lax.dynamic_slice_in_dim(tot[6:7], me * D_SHARD, D_SHARD, axis=1))
    adam_flat("pool_w", recv_pool)
    small_names = [nm for nm in names if nm not in ffn_names + ("w_k", "w_v", "w_q", "w_o", "pool_w")]

    def pack_small(tree):
        cols = []
        for nm in small_names:
            flat = tree[nm].reshape(-1)
            cols.append(jnp.pad(flat, (0, -flat.shape[0] % 1024)))
        return jnp.concatenate(cols).reshape(-1, 128)

    _, dl, m1, m2 = _adam(pack_small(weights), pack_small(grads), pack_small(mom1), pack_small(mom2), "adam_small")

    def unpack_small(packed, out):
        flat, off = packed.reshape(-1), 0
        for nm in small_names:
            size = weights[nm].size
            out[nm] = flat[off:off + size].reshape(weights[nm].shape)
            off += size + (-size % 1024)

    unpack_small(dl, deltas)
    unpack_small(m1, new_m)
    unpack_small(m2, new_v)

    recv0 = _chip_exchange_wait(flight, chip0, land0, dl, "scatter_ffn0_wait")
    for nm, g in zip(ffn_names, recv0):
        done = _adam_layer(ffn_view(nm, weights[nm]), g, ffn_view(nm, mom1[nm]), ffn_view(nm, mom2[nm]), 0,
                           "adam0_" + nm, prev=layer1[nm])
        grads[nm], deltas[nm], new_m[nm], new_v[nm] = [ffn_view(nm, a) for a in done]

    return (loss, grad_x[None], *[grads[nm] for nm in names], *[deltas[nm] for nm in names],
            *[new_m[nm] for nm in names], *[new_v[nm] for nm in names])
```

```python
import functools

import jax
import jax.numpy as jnp
from jax import lax
from jax.experimental import pallas as pl
from jax.experimental.pallas import tpu as pltpu

F32 = jnp.float32
BF16 = jnp.bfloat16
MESH_ID = pl.DeviceIdType.MESH

N_DEV = 8
S = 2048
D = 1024
F = 2816
F_SHARD = F // N_DEV
D_SHARD = D // N_DEV
N_GROUPS = 4
GROUP = D // N_GROUPS
POOL_SHARD = GROUP // N_DEV
MAX_WIN = 16
HEAD = 64
N_HEADS = D // HEAD
CHUNK = 64
LEFT = 8
QB = 4 * CHUNK
KB = QB + LEFT * CHUNK
PADK = LEFT * CHUNK
TOEP = 1024
N_REL = 257
MAX_REL = 128
EPS = 1e-6
NEG_INF = -1e30
ATTN_SCALE = HEAD ** -0.5

ADAM_LR = 0.001
ADAM_B1 = 0.9
ADAM_B2 = 0.999
ADAM_EPS = 1e-08
ADAM_WD = 0.01
ADAM_STEP = 10

VMEM_LIMIT = 52 * 1024 * 1024

ANY = pl.BlockSpec(memory_space=pl.ANY)
VMEM = pl.BlockSpec(memory_space=pltpu.VMEM)


def _call(body, **kw):
    return pl.pallas_call(body, **kw)


def _params(*sem):
    return pltpu.CompilerParams(dimension_semantics=sem, vmem_limit_bytes=VMEM_LIMIT)


def _dot(a, b, dims):
    return lax.dot_general(a, b, (dims, ((), ())), preferred_element_type=F32)


def _nn(a, b):
    return _dot(a, b, ((1,), (0,)))


def _nt(a, b):
    return _dot(a, b, ((1,), (1,)))


def _tn(a, b):
    return _dot(a, b, ((0,), (0,)))


def _rstd(x):
    return lax.rsqrt(jnp.mean(x * x, axis=-1, keepdims=True) + EPS)


def _rms_bwd(dh, x, r, g):
    gd = dh * g
    return r * gd - x * (r * r * r) * jnp.mean(gd * x, axis=-1, keepdims=True)


def _colsum8(v):
    return jnp.broadcast_to(jnp.sum(v, axis=0, keepdims=True), (8, v.shape[1]))


def _seg_sum(v):
    r = lax.broadcasted_iota(jnp.int32, (128, 128), 0) // HEAD
    c = lax.broadcasted_iota(jnp.int32, (128, 128), 1) // HEAD
    ones = jnp.where(r == c, 1.0, 0.0).astype(BF16)
    out = []
    for blk in range(v.shape[1] // 128):
        part = v[:, blk * 128:(blk + 1) * 128]
        hi = part.astype(BF16)
        rest = part - hi.astype(F32)
        mid = rest.astype(BF16)
        lo = (rest - mid.astype(F32)).astype(BF16)
        out.append(_nn(hi, ones) + _nn(mid, ones) + _nn(lo, ones))
    return jnp.concatenate(out, axis=1)


def _place():
    return lax.axis_index("x"), lax.axis_index("y"), lax.axis_index("c")


class _Exchange:
    def __init__(self, ins, out_shape, sems, start, mid, finish, early=None):
        self.ins, self.out_shape, self.sems = list(ins), list(out_shape), list(sems)
        self.start, self.mid, self.finish = start, mid, finish
        self.early = early if early is not None else (lambda ins, outs, sems: None)


def _relay_gather_exchange(shards):
    n = len(shards)

    def tools(ins, outs, sems):
        send_sems, recv_sems, local_sems = sems
        x, y, c = _place()
        me, sib = (x, y, c), (x, y, 1 - c)
        xn, yn, dg = (1 - x, y, c), (x, 1 - y, c), (1 - x, 1 - y, c)
        other = lambda dev: (dev[0], dev[1], 1 - c)

        def piece(k, ref, half):
            rows = shards[k].shape[0] // 2
            return ref if half is None else ref.at[pl.ds(half * rows, rows)]

        def held(k, dev, half):
            return piece(k, outs[k].at[4 * dev[0] + 2 * dev[1] + dev[2]], half)

        def copy(k, s, dev, half, to, own=False):
            return pltpu.make_async_remote_copy(
                src_ref=piece(k, ins[k], half) if own else held(k, dev, half), dst_ref=held(k, dev, half),
                send_sem=send_sems.at[k, s], recv_sem=recv_sems.at[k, s], device_id=to, device_id_type=MESH_ID)

        def mine(k):
            return pltpu.make_async_copy(ins[k], held(k, me, None), local_sems.at[k])

        def sent(k):
            return [copy(k, 0, me, None, sib, own=True), copy(k, 1, me, 0, xn, own=True), copy(k, 2, me, 1, yn, own=True),
                    copy(k, 3, me, 0, yn, own=True), copy(k, 4, me, 1, xn, own=True)]

        relays = [(1, xn, 0, [(5, yn), (7, sib)]), (2, yn, 1, [(6, xn), (9, sib)]), (3, yn, 0, [(10, sib)]),
                  (4, xn, 1, [(8, sib)]), (6, dg, 1, [(11, sib)]), (5, dg, 0, [(12, sib)])]
        from_sibling = [(0, other(me), None), (7, other(xn), 0), (8, other(xn), 1), (9, other(yn), 1),
                        (10, other(yn), 0), (11, other(dg), 1), (12, other(dg), 0)]
        return mine, sent, copy, relays, from_sibling, me

    def start(ins, outs, sems):
        mine, sent, _, _, _, _ = tools(ins, outs, sems)
        for k in range(n):
            mine(k).start()
        for k in range(n):
            for cp in sent(k):
                cp.start()

    def pass_on(ins, outs, sems, which):
        _, _, copy, relays, _, me = tools(ins, outs, sems)
        for k in range(n):
            for s, dev, half, onward in relays[which]:
                copy(k, s, dev, half, me).wait_recv()
                for s2, to in onward:
                    copy(k, s2, dev, half, to).start()

    def early(ins, outs, sems):
        pass_on(ins, outs, sems, slice(0, 2))

    def mid(ins, outs, sems):
        pass_on(ins, outs, sems, slice(2, None))

    def finish(ins, outs, sems):
        mine, sent, copy, relays, from_sibling, me = tools(ins, outs, sems)
        for k in range(n):
            for s, dev, half in from_sibling:
                copy(k, s, dev, half, me).wait_recv()
        for k in range(n):
            for cp in sent(k):
                cp.wait_send()
            for _, dev, half, onward in relays:
                for s2, to in onward:
                    copy(k, s2, dev, half, to).wait_send()
            mine(k).wait()

    return _Exchange(
        shards, [jax.ShapeDtypeStruct((N_DEV,) + a.shape, a.dtype) for a in shards],
        [pltpu.SemaphoreType.DMA((n, 13)), pltpu.SemaphoreType.DMA((n, 13)), pltpu.SemaphoreType.DMA((n,))],
        start, mid, finish, early)


def _peer(x, y, c, m):
    px = 1 - x if m & 4 else x
    py = 1 - y if m & 2 else y
    pc = 1 - c if m & 1 else c
    return px, py, pc


N_CHIPS = N_DEV // 2


def _pair_exchange(parts):
    n = len(parts)

    def copies(ins, outs, sems):
        send_sems, recv_sems = sems
        x, y, c = _place()
        return [pltpu.make_async_remote_copy(
            src_ref=ins[k].at[2 * q + 1 - c], dst_ref=outs[k].at[q], send_sem=send_sems.at[k, q],
            recv_sem=recv_sems.at[k, q], device_id=(x, y, 1 - c), device_id_type=MESH_ID)
            for k in range(n) for q in range(N_CHIPS)]

    def start(ins, outs, sems):
        for cp in copies(ins, outs, sems):
            cp.start()

    def mid(ins, outs, sems):
        pass

    def finish(ins, outs, sems):
        for cp in copies(ins, outs, sems):
            cp.wait_recv()
        for cp in copies(ins, outs, sems):
            cp.wait_send()

    return _Exchange(
        parts, [jax.ShapeDtypeStruct((N_CHIPS,) + a.shape[1:], a.dtype) for a in parts],
        [pltpu.SemaphoreType.DMA((n, N_CHIPS)), pltpu.SemaphoreType.DMA((n, N_CHIPS))], start, mid, finish)


def _pair_add(parts, stage, name):
    n = len(parts)
    place = jnp.stack([lax.axis_index("c"), 2 * lax.axis_index("x") + lax.axis_index("y")]).astype(jnp.int32)

    def body(place_ref, *refs):
        for k in range(n):
            mine, theirs = refs[k], refs[n + k]
            total = (mine[0, 0].astype(F32) + theirs[0].astype(F32)).astype(BF16)
            refs[2 * n + k][0] = total

            @pl.when(pl.program_id(1) == place_ref[1])
            def _():
                refs[3 * n + k][0] = total

    in_specs, out_specs, land_specs = [], [], []
    for a in parts:
        _, r, cdim = a.shape
        in_specs.append(pl.BlockSpec((1, 1, r // 2, cdim), lambda i, q, place_ref: (q, place_ref[0], i, 0)))
    for a in parts:
        _, r, cdim = a.shape
        in_specs.append(pl.BlockSpec((1, r // 2, cdim), lambda i, q, place_ref: (q, i, 0)))
        out_specs.append(pl.BlockSpec((1, r // 2, cdim), lambda i, q, place_ref: (q, i, 0)))
        land_specs.append(pl.BlockSpec((1, r // 2, cdim), lambda i, q, place_ref: (place_ref[1], i, 0)))
    return list(_call(
        body, name=name,
        grid_spec=pltpu.PrefetchScalarGridSpec(num_scalar_prefetch=1, grid=(2, N_CHIPS), in_specs=in_specs,
                                               out_specs=out_specs + land_specs),
        out_shape=[jax.ShapeDtypeStruct(s.shape, BF16) for s in stage] * 2,
        compiler_params=_params("arbitrary", "arbitrary"),
    )(place, *[a.reshape((N_CHIPS, 2) + a.shape[1:]) for a in parts], *stage))


HBM = pl.BlockSpec(memory_space=pltpu.HBM)
SEMAPHORES = pl.BlockSpec(memory_space=pltpu.SEMAPHORE)


def _chip_copies(srcs, lands, send_sems, recv_sems, mine_is_dst):
    x, y, c = _place()
    me = 2 * x + y
    out = []
    for m in range(1, N_CHIPS):
        px, py, _ = _peer(x, y, c, 2 * m)
        peer = 2 * px + py
        for k in range(len(srcs)):
            pair = k * (N_CHIPS - 1) + m - 1
            out.append(pltpu.make_async_remote_copy(
                src_ref=srcs[k].at[peer], dst_ref=lands[k].at[me if mine_is_dst else peer],
                send_sem=send_sems[pair], recv_sem=recv_sems[pair],
                device_id=(px, py, c), device_id_type=MESH_ID))
    return out


def _chip_exchange_start(chip_parts, lands, after, name):
    n = len(chip_parts)
    pairs = n * (N_CHIPS - 1)

    def body(*refs):
        srcs, zones = refs[:n], refs[n:2 * n]
        sems = refs[2 * n + 1:2 * n + 1 + 2 * pairs]
        token = refs[-1]
        for cp in _chip_copies(srcs, zones, sems[:pairs], sems[pairs:], True):
            cp.start()
        token[...] = jnp.zeros(token.shape, F32)

    thru = [pltpu.HBM(a.shape, a.dtype) for a in chip_parts + lands]
    hbm = [pltpu.with_memory_space_constraint(a, pltpu.HBM) for a in chip_parts + lands]
    res = _call(
        body, name=name,
        out_shape=[pltpu.SemaphoreType.DMA(())] * (2 * pairs) + thru + [jax.ShapeDtypeStruct((8, 128), F32)],
        in_specs=[HBM] * (2 * n) + [ANY], out_specs=[SEMAPHORES] * (2 * pairs) + [HBM] * (2 * n) + [VMEM],
        input_output_aliases={i: 2 * pairs + i for i in range(2 * n)},
        compiler_params=pltpu.CompilerParams(has_side_effects=pltpu.SideEffectType.DATAFLOW_SIDE_EFFECTING),
    )(*hbm, after)
    sems, rest = list(res[:2 * pairs]), res[2 * pairs:]
    return sems, list(rest[:n]), list(rest[n:2 * n]), rest[-1]


def _chip_exchange_wait(sems, chip_parts, lands, after, name):
    n = len(chip_parts)
    pairs = n * (N_CHIPS - 1)
    after = list(after) if isinstance(after, (list, tuple)) else [after]

    def body(*refs):
        srcs, zones = refs[:n], refs[n:2 * n]
        sem_refs = refs[2 * n:2 * n + 2 * pairs]
        for cp in _chip_copies(srcs, zones, sem_refs[:pairs], sem_refs[pairs:], False):
            cp.wait_send()
            cp.wait_recv()

    thru = [pltpu.HBM(a.shape, a.dtype) for a in chip_parts + lands]
    res = _call(
        body, name=name, out_shape=thru,
        in_specs=[HBM] * (2 * n) + [SEMAPHORES] * (2 * pairs) + [ANY] * len(after), out_specs=[HBM] * (2 * n),
        input_output_aliases={i: i for i in range(2 * n)},
        compiler_params=pltpu.CompilerParams(has_side_effects=pltpu.SideEffectType.DATAFLOW_SIDE_EFFECTING),
    )(*chip_parts, *lands, *sems, *after)
    return list(res[n:])


def _run_exchange(ex, name, after=None):
    n_in, n_out = len(ex.ins), len(ex.out_shape)
    order = [] if after is None else [after]

    def body(*refs):
        ins, outs = refs[:n_in], refs[n_in + len(order):n_in + len(order) + n_out]
        sems = refs[n_in + len(order) + n_out:]
        ex.start(ins, outs, sems)
        ex.early(ins, outs, sems)
        ex.mid(ins, outs, sems)
        ex.finish(ins, outs, sems)

    return list(_call(body, name=name, out_shape=ex.out_shape, in_specs=[ANY] * (n_in + len(order)),
                      out_specs=[ANY] * n_out, scratch_shapes=ex.sems)(*ex.ins, *order))


def _call_hosting(body, ex, phases, *, in_specs, out_specs, out_shape, scratch_shapes, args, **kw):
    n_in, n_out, n_scr = len(in_specs), len(out_specs), len(scratch_shapes)
    if ex is None:
        res = _call(body, in_specs=in_specs, out_specs=out_specs, out_shape=out_shape,
                    scratch_shapes=scratch_shapes, **kw)(*args)
        return list(res), []
    n_xin, n_xout = len(ex.ins), len(ex.out_shape)

    def hosting(*refs):
        a, b = n_in, n_in + n_xin
        c, d = b + n_out, b + n_out + n_xout
        ins, xins, outs, xouts = refs[:a], refs[a:b], refs[b:c], refs[c:d]
        scr, sems = refs[d:d + n_scr], refs[d + n_scr:]
        first, early, mid, last = phases()

        @pl.when(first)
        def _():
            ex.start(xins, xouts, sems)

        body(*ins, *outs, *scr)

        @pl.when(early)
        def _():
            ex.early(xins, xouts, sems)

        @pl.when(mid)
        def _():
            ex.mid(xins, xouts, sems)

        @pl.when(last)
        def _():
            ex.finish(xins, xouts, sems)

    res = _call(hosting, in_specs=list(in_specs) + [ANY] * n_xin, out_specs=list(out_specs) + [ANY] * n_xout,
                out_shape=list(out_shape) + ex.out_shape, scratch_shapes=list(scratch_shapes) + ex.sems,
                **kw)(*args, *ex.ins)
    return list(res[:n_out]), list(res[n_out:])


def _grid_phases(dims, mid_fraction=0.8, early_fraction=0.4):
    total = 1
    for d in dims:
        total *= d
    mid = min(max(int(total * mid_fraction), 1), total - 1)
    early = min(int(total * early_fraction), mid)

    def phases():
        step = pl.program_id(0)
        for axis in range(1, len(dims)):
            step = step * dims[axis] + pl.program_id(axis)
        return step == 0, step == early, step == mid, step == total - 1
    return phases


def _small_exchanges(pack, blocks, name, after):
    rows = pack.shape[0]

    def body(in_ref, blk_ref, *rest):
        out_ref, got_ref, recv, send_sems, recv_sems, blk_send, blk_recv, blk_local = rest[len(after):]
        x, y, c = _place()
        me = 4 * x + 2 * y + c
        peers = [_peer(x, y, c, m) for m in range(1, N_DEV)]
        index = [4 * px + 2 * py + pc for px, py, pc in peers]

        def block_copy(m, landing):
            return pltpu.make_async_remote_copy(
                src_ref=blk_ref.at[index[m]], dst_ref=got_ref.at[index[m] if landing else me],
                send_sem=blk_send.at[m], recv_sem=blk_recv.at[m], device_id=peers[m], device_id_type=MESH_ID)

        def pack_copy(m, landing):
            return pltpu.make_async_remote_copy(
                src_ref=in_ref, dst_ref=recv.at[index[m] if landing else me], send_sem=send_sems.at[m],
                recv_sem=recv_sems.at[m], device_id=peers[m], device_id_type=MESH_ID)

        own = pltpu.make_async_copy(blk_ref.at[me], got_ref.at[me], blk_local)
        own.start()
        recv[me] = in_ref[...]
        for m in range(N_DEV - 1):
            pack_copy(m, False).start()
            block_copy(m, False).start()
        for m in range(N_DEV - 1):
            pack_copy(m, True).wait_recv()
        acc = recv[0]
        for d in range(1, N_DEV):
            acc = acc + recv[d]
        out_ref[...] = acc
        for m in range(N_DEV - 1):
            block_copy(m, True).wait_recv()
        for m in range(N_DEV - 1):
            pack_copy(m, False).wait_send()
            block_copy(m, False).wait_send()
        own.wait()

    return _call(
        body, name=name,
        out_shape=[jax.ShapeDtypeStruct(pack.shape, F32), jax.ShapeDtypeStruct(blocks.shape, blocks.dtype)],
        in_specs=[VMEM, ANY] + [ANY] * len(after), out_specs=[VMEM, ANY],
        scratch_shapes=[pltpu.VMEM((N_DEV, rows, pack.shape[1]), F32)] + [pltpu.SemaphoreType.DMA((7,))] * 4
        + [pltpu.SemaphoreType.DMA],
        compiler_params=pltpu.CompilerParams(vmem_limit_bytes=VMEM_LIMIT),
    )(pack, blocks, *after)


POOL_TS = 256


def _pool_counts(first_row, rows, win):
    t = first_row + lax.broadcasted_iota(jnp.int32, (rows, 1), 0)
    return jnp.minimum(t + 1, win).astype(F32)


def _pool_fwd(x, g, w, b, scale):
    nt = S // POOL_TS

    def body(x_ref, g_ref, w_ref, b_ref, s_ref, out_ref, diff_ref, ext):
        i = pl.program_id(0)

        @pl.when(i == 0)
        def _():
            ext[0:MAX_WIN, :] = jnp.zeros((MAX_WIN, D), F32)

        @pl.when(i > 0)
        def _():
            ext[0:MAX_WIN, :] = ext[POOL_TS:POOL_TS + MAX_WIN, :]

        xv = x_ref[...]
        h = xv * _rstd(xv) * g_ref[...]
        ext[MAX_WIN:, :] = h
        for gi in range(N_GROUPS):
            win = 2 << gi
            cols = slice(gi * GROUP, (gi + 1) * GROUP)
            sm = ext[:, cols]
            k = 1
            while k < win:
                sm = sm + pltpu.roll(sm, k, axis=0)
                k *= 2
            pooled = sm[MAX_WIN:, :] / _pool_counts(i * POOL_TS, POOL_TS, win)
            diff = (pooled - h[:, cols]).astype(BF16)
            yv = (_nn(diff, w_ref[gi]) + b_ref[:, cols]) * s_ref[:, cols]
            out_ref[:, cols] = xv[:, cols] + yv
            diff_ref[:, cols] = diff

    row = pl.BlockSpec((1, D), lambda i: (0, 0))
    tile = pl.BlockSpec((POOL_TS, D), lambda i: (i, 0))
    return _call(
        body, name="pool_fwd", grid=(nt,),
        in_specs=[tile, row, pl.BlockSpec((N_GROUPS, GROUP, GROUP), lambda i: (0, 0, 0)), row, row],
        out_specs=[tile, tile],
        out_shape=[jax.ShapeDtypeStruct((S, D), F32), jax.ShapeDtypeStruct((S, D), BF16)],
        scratch_shapes=[pltpu.VMEM((POOL_TS + MAX_WIN, D), F32)],
        compiler_params=_params("arbitrary"),
    )(x, g, w, b, scale)


def _pool_bwd(dy, x, diff, g, w, b, scale, ex=None, after=()):
    nt = S // POOL_TS

    def body(dy_ref, x_ref, diff_ref, g_ref, w_ref, b_ref, s_ref, *rest):
        gx_ref, dw_ref, db_ref, ds_ref, dg_ref, ext, dh = rest[len(after):]
        i = pl.program_id(0)
        first_row = (nt - 1 - i) * POOL_TS

        @pl.when(i == 0)
        def _():
            ext[POOL_TS:, :] = jnp.zeros((MAX_WIN, D), F32)
            dw_ref[...] = jnp.zeros(dw_ref.shape, F32)
            db_ref[...] = jnp.zeros(db_ref.shape, F32)
            ds_ref[...] = jnp.zeros(ds_ref.shape, F32)
            dg_ref[...] = jnp.zeros(dg_ref.shape, F32)

        @pl.when(i > 0)
        def _():
            ext[POOL_TS:, :] = ext[0:MAX_WIN, :]

        dyv = dy_ref[...]
        for gi in range(N_GROUPS):
            win = 2 << gi
            cols = slice(gi * GROUP, (gi + 1) * GROUP)
            dfb = diff_ref[:, cols]
            z = _nn(dfb, w_ref[gi]) + b_ref[:, cols]
            dyg = dyv[:, cols]
            ds_ref[:, cols] += _colsum8(dyg * z)
            dz = dyg * s_ref[:, cols]
            db_ref[:, cols] += _colsum8(dz)
            dzb = dz.astype(BF16)
            dw_ref[gi] += _tn(dfb, dzb)
            ddiff = _nt(dzb, w_ref[gi])
            ext[0:POOL_TS, cols] = ddiff / _pool_counts(first_row, POOL_TS, win)
            sm = ext[:, cols]
            k = 1
            while k < win:
                sm = sm + pltpu.roll(sm, POOL_TS + MAX_WIN - k, axis=0)
                k *= 2
            dh[:, cols] = sm[0:POOL_TS, :] - ddiff
        xv = x_ref[...]
        r = _rstd(xv)
        gv = g_ref[...]
        dhv = dh[...]
        dg_ref[...] += _colsum8(dhv * xv * r)
        gx_ref[...] = dyv + _rms_bwd(dhv, xv, r, gv)

    row = pl.BlockSpec((1, D), lambda i: (0, 0))
    tile = pl.BlockSpec((POOL_TS, D), lambda i: (nt - 1 - i, 0))
    acc = pl.BlockSpec((8, D), lambda i: (0, 0))
    wspec = pl.BlockSpec((N_GROUPS, GROUP, GROUP), lambda i: (0, 0, 0))
    return _call_hosting(
        body, ex, _grid_phases((nt,)), name="pool_bwd", grid=(nt,),
        in_specs=[tile, tile, tile, row, wspec, row, row] + [ANY] * len(after),
        out_specs=[tile, wspec, acc, acc, acc],
        out_shape=[jax.ShapeDtypeStruct((S, D), F32), jax.ShapeDtypeStruct((N_GROUPS, GROUP, GROUP), F32),
                   jax.ShapeDtypeStruct((8, D), F32), jax.ShapeDtypeStruct((8, D), F32),
                   jax.ShapeDtypeStruct((8, D), F32)],
        scratch_shapes=[pltpu.VMEM((POOL_TS + MAX_WIN, D), F32), pltpu.VMEM((POOL_TS, D), F32)],
        args=(dy, x, diff, g, w, b, scale, *after), compiler_params=_params("arbitrary"),
    )


FFN_TS = min(S, 1024)
FFN_TF = 256
WEIGHT_SLOTS = 3


def _ffn_fwd(x, g, wg_t, wu_t, wd, name, target=None, ex=None):
    width = wd.shape[0]
    ni, nj = S // FFN_TS, width // FFN_TF
    with_loss = target is not None
    extra = [target] if with_loss else []

    def body(*refs):
        x_ref, g_ref, wg_ref, wu_ref, wd_ref = refs[:5]
        if with_loss:
            t_ref, out_ref, loss_ref, h_ref, gg_ref, uu_ref, hs, acc, wbuf, wsem = refs[5:]
        else:
            out_ref, h_ref, gg_ref, uu_ref, hs, acc, wbuf, wsem = refs[5:]
        i, j = pl.program_id(0), pl.program_id(1)
        step = i * nj + j

        def tile_copies(s):
            rows = pl.ds(pl.multiple_of((s % nj) * FFN_TF, FFN_TF), FFN_TF)
            return [pltpu.make_async_copy(w.at[rows], wbuf.at[s % WEIGHT_SLOTS, k], wsem.at[s % WEIGHT_SLOTS, k])
                    for k, w in enumerate((wg_ref, wu_ref, wd_ref))]

        @pl.when(step == 0)
        def _():
            for s in range(WEIGHT_SLOTS - 1):
                for cp in tile_copies(s):
                    cp.start()

        @pl.when(step + WEIGHT_SLOTS - 1 < ni * nj)
        def _():
            for cp in tile_copies(step + WEIGHT_SLOTS - 1):
                cp.start()

        @pl.when(j == 0)
        def _():
            xv = x_ref[...]
            hb = (xv * _rstd(xv) * g_ref[...]).astype(BF16)
            hs[...] = hb
            h_ref[...] = hb
            acc[...] = jnp.zeros(acc.shape, F32)

        for cp in tile_copies(step):
            cp.wait()
        slot = step % WEIGHT_SLOTS
        hb = hs[...]
        gg = _nt(hb, wbuf[slot, 0])
        uu = _nt(hb, wbuf[slot, 1])
        gg_ref[...] = gg
        uu_ref[...] = uu
        a = (gg * jax.nn.sigmoid(gg) * uu).astype(BF16)
        acc[...] += _nn(a, wbuf[slot, 2])

        @pl.when(j == nj - 1)
        def _():
            yv = x_ref[...] + acc[...]
            if with_loss:
                err = yv - t_ref[...]
                out_ref[...] = err * (1.0 / D)
                part = jnp.sum(err * err) * (0.5 / D)

                @pl.when(i == 0)
                def _():
                    loss_ref[...] = jnp.zeros(loss_ref.shape, F32)

                loss_ref[...] += jnp.broadcast_to(part, loss_ref.shape)
            else:
                out_ref[...] = yv

    xt = pl.BlockSpec((FFN_TS, D), lambda i, j: (i, 0))
    row = pl.BlockSpec((1, D), lambda i, j: (0, 0))
    wt = ANY
    gt = pl.BlockSpec((FFN_TS, FFN_TF), lambda i, j: (i, j))
    in_specs = [xt, row, wt, wt, wt] + [xt] * len(extra)
    out_specs = [xt] + ([pl.BlockSpec((8, 128), lambda i, j: (0, 0))] if with_loss else []) + [xt, gt, gt]
    out_shape = ([jax.ShapeDtypeStruct((S, D), F32)] + ([jax.ShapeDtypeStruct((8, 128), F32)] if with_loss else [])
                 + [jax.ShapeDtypeStruct((S, D), BF16), jax.ShapeDtypeStruct((S, width), F32),
                    jax.ShapeDtypeStruct((S, width), F32)])
    args = (x, g, wg_t, wu_t, wd, *extra)
    return _call_hosting(
        body, ex, _grid_phases((ni, nj), 0.8), name=name, grid=(ni, nj),
        in_specs=in_specs, out_specs=out_specs, out_shape=out_shape,
        scratch_shapes=[pltpu.VMEM((FFN_TS, D), BF16), pltpu.VMEM((FFN_TS, D), F32),
                        pltpu.VMEM((WEIGHT_SLOTS, 3, FFN_TF, D), BF16), pltpu.SemaphoreType.DMA((WEIGHT_SLOTS, 3))],
        args=args,
        compiler_params=_params("arbitrary", "arbitrary"),
    )


def _ffn_bwd_weights(dout, h, gg, uu, wd, name, ex=None):
    width = wd.shape[0]
    nj = width // FFN_TF

    def body(do_ref, h_ref, gg_ref, uu_ref, wd_ref, dg_ref, du_ref, dwg_ref, dwu_ref, dwd_ref, dob, kept):
        j = pl.program_id(0)

        @pl.when(j == 0)
        def _():
            dob[...] = do_ref[...].astype(BF16)
            kept[1] = jnp.zeros(kept.shape[1:], BF16)

        prev, cur = (j + 1) % 2, j % 2
        hb, dov = h_ref[...], dob[...]
        dwg_ref[...] = _tn(kept[prev, 0], hb).astype(BF16)
        dwu_ref[...] = _tn(kept[prev, 1], hb).astype(BF16)
        dwd_ref[...] = _tn(kept[prev, 2], dov).astype(BF16)
        gv, uv = gg_ref[...], uu_ref[...]
        da = _nt(dov, wd_ref[...])
        sg = jax.nn.sigmoid(gv)
        sl = gv * sg
        dub = (da * sl).astype(BF16)
        dgb = (da * uv * (sg * (1.0 + gv * (1.0 - sg)))).astype(BF16)
        dg_ref[...] = dgb
        du_ref[...] = dub
        kept[cur, 0] = dgb
        kept[cur, 1] = dub
        kept[cur, 2] = (sl * uv).astype(BF16)

    once = pl.Buffered(1)
    whole = lambda: pl.BlockSpec((S, D), lambda j, _: (0, 0), pipeline_mode=once)
    this = lambda j: jnp.minimum(j, nj - 1)
    last = lambda j: jnp.maximum(j - 1, 0)
    wt_in = pl.BlockSpec((FFN_TF, D), lambda j, _: (this(j), 0))
    wt_out = pl.BlockSpec((FFN_TF, D), lambda j, _: (last(j), 0))
    gt = pl.BlockSpec((S, FFN_TF), lambda j, _: (0, this(j)))
    return _call_hosting(
        body, ex, _grid_phases((nj + 1, 1)), name=name, grid=(nj + 1, 1),
        in_specs=[whole(), whole(), gt, gt, wt_in], out_specs=[gt, gt, wt_out, wt_out, wt_out],
        out_shape=[jax.ShapeDtypeStruct((S, width), BF16)] * 2 + [jax.ShapeDtypeStruct((width, D), BF16)] * 3,
        scratch_shapes=[pltpu.VMEM((S, D), BF16), pltpu.VMEM((2, 3, S, FFN_TF), BF16)], args=(dout, h, gg, uu, wd),
        compiler_params=_params("arbitrary", "arbitrary"),
    )


BWD_TS = 512


def _ffn_bwd_input(dres, x, g, grads, weights, name, ex=None, after=()):
    nt = S // BWD_TS
    n = len(grads)

    def body(*refs):
        dres_ref, x_ref, g_ref = refs[:3]
        grad_refs, w_refs = refs[3:3 + n], refs[3 + n:3 + 2 * n]
        dx_ref, dgam_ref = refs[3 + 2 * n + len(after):]
        dh = _nn(grad_refs[0][...], w_refs[0][...])
        for k in range(1, n):
            dh = dh + _nn(grad_refs[k][...], w_refs[k][...])
        xv = x_ref[...]
        r = _rstd(xv)
        dx_ref[...] = dres_ref[...] + _rms_bwd(dh, xv, r, g_ref[...])

        @pl.when(pl.program_id(0) == 0)
        def _():
            dgam_ref[...] = jnp.zeros(dgam_ref.shape, F32)

        dgam_ref[...] += _colsum8(dh * xv * r)

    tile = pl.BlockSpec((BWD_TS, D), lambda i: (i, 0))
    ftiles = [pl.BlockSpec((BWD_TS, a.shape[1]), lambda i: (i, 0)) for a in grads]
    wspecs = [pl.BlockSpec(w.shape, lambda i: (0, 0), pipeline_mode=pl.Buffered(1)) for w in weights]
    return _call_hosting(
        body, ex, _grid_phases((nt,)), name=name, grid=(nt,),
        in_specs=[tile, tile, pl.BlockSpec((1, D), lambda i: (0, 0))] + ftiles + wspecs + [ANY] * len(after),
        out_specs=[tile, pl.BlockSpec((8, D), lambda i: (0, 0))],
        out_shape=[jax.ShapeDtypeStruct((S, D), F32), jax.ShapeDtypeStruct((8, D), F32)],
        scratch_shapes=[], args=(dres, x, g, *grads, *weights, *after), compiler_params=_params("arbitrary"),
    )


def _mm(a, b, mode, out_dtype, name, add=None, skip_rows=0):
    if mode == "nn":
        (m, kd), n = (a.shape[0] - skip_rows, a.shape[1]), b.shape[1]
    elif mode == "nt":
        (m, kd), n = (a.shape[0] - skip_rows, a.shape[1]), b.shape[0]
    else:
        (kd, m), n = a.shape, b.shape[1]
    tm, tn, tk = min(m, 1024), min(n, 1024), min(kd, 1024)
    if skip_rows and mode == "tn":
        tk = min(tk, skip_rows)
    elif skip_rows:
        tm = min(tm, skip_rows)
    skip_a = skip_rows // tm if mode != "tn" else 0
    skip_b = skip_rows // tk if mode == "tn" else 0
    assert skip_rows == skip_a * tm + skip_b * tk
    nk = kd // tk
    dot = {"nn": _nn, "nt": _nt, "tn": _tn}[mode]

    def body(*refs):
        if add is None:
            a_ref, b_ref, o_ref, acc = refs
        else:
            a_ref, b_ref, add_ref, o_ref, acc = refs
        k = pl.program_id(2)

        @pl.when(k == 0)
        def _():
            acc[...] = jnp.zeros(acc.shape, F32)

        acc[...] += dot(a_ref[...].astype(BF16), b_ref[...].astype(BF16))

        @pl.when(k == nk - 1)
        def _():
            res = acc[...]
            if add is not None:
                res = res + add_ref[...]
            o_ref[...] = res.astype(out_dtype)

    if mode == "tn":
        a_spec = pl.BlockSpec((tk, tm), lambda i, j, k: (k, i))
        b_spec = pl.BlockSpec((tk, tn), lambda i, j, k: (k + skip_b, j))
    else:
        a_spec = pl.BlockSpec((tm, tk), lambda i, j, k: (i + skip_a, k))
        b_spec = (pl.BlockSpec((tk, tn), lambda i, j, k: (k, j)) if mode == "nn"
                  else pl.BlockSpec((tn, tk), lambda i, j, k: (j, k)))
    o_spec = pl.BlockSpec((tm, tn), lambda i, j, k: (i, j))
    in_specs = [a_spec, b_spec] + ([o_spec] if add is not None else [])
    args = (a, b) + ((add,) if add is not None else ())
    return _call(
        body, name=name, grid=(m // tm, n // tn, nk), in_specs=in_specs, out_specs=o_spec,
        out_shape=jax.ShapeDtypeStruct((m, n), out_dtype), scratch_shapes=[pltpu.VMEM((tm, tn), F32)],
        compiler_params=_params("parallel", "parallel", "arbitrary"),
    )(*args)


PROJ_TS = 256


def _kvq_proj(x, g_kv, g_mix, wk, wv, wq, gk, gq, ex=None):
    lead = PADK // PROJ_TS

    def body(x_ref, gkv_ref, gmix_ref, wk_ref, wv_ref, wq_ref, gk_ref, gq_ref,
             hkv_ref, h1_ref, kpre_ref, qpre_ref, k_ref, v_ref, q_ref):
        i = pl.program_id(0)

        @pl.when(i < lead)
        def _():
            k_ref[...] = jnp.zeros(k_ref.shape, BF16)
            v_ref[...] = jnp.zeros(v_ref.shape, BF16)

        @pl.when(i >= lead)
        def _():
            xv = x_ref[...]
            xr = xv * _rstd(xv)
            hkv = (xr * gkv_ref[...]).astype(BF16)
            h1 = (xr * gmix_ref[...]).astype(BF16)
            hkv_ref[...] = hkv
            h1_ref[...] = h1
            kpre = _nn(hkv, wk_ref[...])
            qpre = _nn(h1, wq_ref[...])
            kpre_ref[...] = kpre
            qpre_ref[...] = qpre
            v_ref[...] = _nn(hkv, wv_ref[...]).astype(BF16)
            rk = lax.rsqrt(_seg_sum(kpre * kpre) * (1.0 / HEAD) + EPS)
            k_ref[...] = (kpre * rk * gk_ref[...]).astype(BF16)
            rq = lax.rsqrt(_seg_sum(qpre * qpre) * (1.0 / HEAD) + EPS)
            q_ref[...] = (qpre * rq * gq_ref[...]).astype(BF16)

    tile = pl.BlockSpec((PROJ_TS, D), lambda i: (jnp.maximum(i - lead, 0), 0))
    padded = pl.BlockSpec((PROJ_TS, D), lambda i: (i, 0))
    row = pl.BlockSpec((1, D), lambda i: (0, 0))
    wspec = pl.BlockSpec((D, D), lambda i: (0, 0))
    bf = jax.ShapeDtypeStruct((S, D), BF16)
    ff = jax.ShapeDtypeStruct((S, D), F32)
    bp = jax.ShapeDtypeStruct((PADK + S, D), BF16)
    return _call_hosting(
        body, ex, _grid_phases((lead + S // PROJ_TS,), 0.85), name="kvq_proj", grid=(lead + S // PROJ_TS,),
        in_specs=[tile, row, row, wspec, wspec, wspec, row, row],
        out_specs=[tile, tile, tile, tile, padded, padded, tile], out_shape=[bf, bf, ff, ff, bp, bp, bf],
        scratch_shapes=[], args=(x, g_kv, g_mix, wk, wv, wq, gk, gq), compiler_params=_params("arbitrary"),
    )


def _qkv_input_bwd(dq, qpre, gq, dkp, kpre, gk, dvp, wq, wk, wv, dres, x, g_mix, g_kv, ex=None):
    nt = S // PROJ_TS
    lead = PADK // PROJ_TS

    def head_bwd(dov, pv, hgv):
        r = lax.rsqrt(_seg_sum(pv * pv) * (1.0 / HEAD) + EPS)
        gd = dov * hgv
        dpre = r * gd - pv * (r * r * r) * (_seg_sum(gd * pv) * (1.0 / HEAD))
        return dpre.astype(BF16), _colsum8(dov * pv * r)

    def fold_heads(full):
        fold = full[:, 0:128]
        for blk in range(1, D // 128):
            fold = fold + full[:, blk * 128:(blk + 1) * 128]
        return fold + pltpu.roll(fold, HEAD, axis=1)

    def body(dq_ref, qpre_ref, gq_ref, dk_ref, kpre_ref, gk_ref, dv_ref, wq_ref, wk_ref, wv_ref, dres_ref, x_ref,
             gmix_ref, gkv_ref, dx_ref, dqpre_ref, dkpre_ref, dgmix_ref, dgkv_ref, dgq_ref, dgk_ref, accq, acck):
        i = pl.program_id(0)

        @pl.when(i == 0)
        def _():
            accq[...] = jnp.zeros(accq.shape, F32)
            acck[...] = jnp.zeros(acck.shape, F32)
            dgmix_ref[...] = jnp.zeros(dgmix_ref.shape, F32)
            dgkv_ref[...] = jnp.zeros(dgkv_ref.shape, F32)

        dqb, cq = head_bwd(dq_ref[...], qpre_ref[...], gq_ref[...])
        dkb, ck = head_bwd(dk_ref[...], kpre_ref[...], gk_ref[...])
        dqpre_ref[...] = dqb
        dkpre_ref[...] = dkb
        accq[...] += cq
        acck[...] += ck
        dh1 = _nt(dqb, wq_ref[...])
        dhkv = _nt(dkb, wk_ref[...]) + _nt(dv_ref[...].astype(BF16), wv_ref[...])
        xv = x_ref[...]
        r = _rstd(xv)
        dx_ref[...] = dres_ref[...] + _rms_bwd(dh1, xv, r, gmix_ref[...]) + _rms_bwd(dhkv, xv, r, gkv_ref[...])
        dgmix_ref[...] += _colsum8(dh1 * xv * r)
        dgkv_ref[...] += _colsum8(dhkv * xv * r)

        @pl.when(i == nt - 1)
        def _():
            dgq_ref[...] = fold_heads(accq[...])
            dgk_ref[...] = fold_heads(acck[...])

    tile = pl.BlockSpec((PROJ_TS, D), lambda i: (i, 0))
    behind = pl.BlockSpec((PROJ_TS, D), lambda i: (i + lead, 0))
    row = pl.BlockSpec((1, D), lambda i: (0, 0))
    wspec = lambda: pl.BlockSpec((D, D), lambda i: (0, 0), pipeline_mode=pl.Buffered(1))
    acc = pl.BlockSpec((8, D), lambda i: (0, 0))
    small = pl.BlockSpec((8, 128), lambda i: (0, 0))
    bf = jax.ShapeDtypeStruct((S, D), BF16)
    return _call_hosting(
        body, ex, _grid_phases((nt,)), name="qkv_input_bwd", grid=(nt,),
        in_specs=[tile, tile, row, behind, tile, row, behind, wspec(), wspec(), wspec(), tile, tile, row, row],
        out_specs=[tile, tile, tile, acc, acc, small, small],
        out_shape=[jax.ShapeDtypeStruct((S, D), F32), bf, bf, jax.ShapeDtypeStruct((8, D), F32),
                   jax.ShapeDtypeStruct((8, D), F32), jax.ShapeDtypeStruct((8, 128), F32),
                   jax.ShapeDtypeStruct((8, 128), F32)],
        scratch_shapes=[pltpu.VMEM((8, D), F32), pltpu.VMEM((8, D), F32)],
        args=(dq, qpre, gq, dkp, kpre, gk, dvp, wq, wk, wv, dres, x, g_mix, g_kv),
        compiler_params=_params("arbitrary"),
    )


def _toeplitz_from_table(table):
    far = jnp.broadcast_to(table[:, N_REL - 1:], (N_HEADS, PADK - MAX_REL + 1))
    near = table[:, N_REL - 2::-1]
    past = jnp.broadcast_to(table[:, 0:1], (N_HEADS, MAX_REL))
    wrap = jnp.broadcast_to(table[:, N_REL - 1:], (N_HEADS, TOEP - (PADK + 2 * MAX_REL + 1)))
    return jnp.concatenate([far, near, past, wrap], axis=1).reshape(N_HEADS, 1, TOEP)


def _table_grad_from_toeplitz(dtp, seg):
    lo = PADK - MAX_REL + 1
    near = dtp[:, lo + N_REL - 3:lo - 1:-1]
    return jnp.concatenate([seg[:, 1:2], near, seg[:, 0:1]], axis=1)


def _bias_band(tp):
    def body(tp_ref, out_ref):
        bv = pltpu.roll(jnp.broadcast_to(tp_ref[0], (QB, TOEP)), 0, axis=1, stride=1, stride_axis=0)
        out_ref[0] = jnp.where(_band_mask(), bv[:, 0:KB], NEG_INF)

    return _call(
        body, name="bias_band", grid=(N_HEADS,),
        in_specs=[pl.BlockSpec((1, 1, TOEP), lambda h: (h, 0, 0))],
        out_specs=pl.BlockSpec((1, QB, KB), lambda h: (h, 0, 0)),
        out_shape=jax.ShapeDtypeStruct((N_HEADS, QB, KB), F32),
        compiler_params=_params("parallel"),
    )(tp)


def _bias_grad(dband, after):
    lo, hi = PADK - MAX_REL + 1, PADK + MAX_REL

    def body(db_ref, _, dtp_ref, seg_ref):
        bv = jnp.concatenate([db_ref[0], jnp.zeros((QB, TOEP - KB), F32)], axis=1)
        row = lax.broadcasted_iota(jnp.int32, (QB, TOEP), 0)
        k = 1
        while k < QB:
            bv = jnp.where((row & k) != 0, pltpu.roll(bv, TOEP - k, axis=1), bv)
            k *= 2
        col = jnp.sum(bv, axis=0, keepdims=True)
        dtp_ref[0] = col
        u = lax.broadcasted_iota(jnp.int32, (1, TOEP), 1)
        far = jnp.sum(jnp.where((u < lo) | (u > hi + MAX_REL), col, 0.0))
        past = jnp.sum(jnp.where((u >= hi) & (u <= hi + MAX_REL), col, 0.0))
        lane = lax.broadcasted_iota(jnp.int32, (1, 128), 1)
        seg_ref[0] = jnp.where(lane == 0, far, jnp.where(lane == 1, past, 0.0))

    return _call(
        body, name="bias_grad", grid=(N_HEADS,),
        in_specs=[pl.BlockSpec((1, QB, KB), lambda h: (h, 0, 0)), ANY],
        out_specs=[pl.BlockSpec((1, 1, TOEP), lambda h: (h, 0, 0)), pl.BlockSpec((1, 1, 128), lambda h: (h, 0, 0))],
        out_shape=[jax.ShapeDtypeStruct((N_HEADS, 1, TOEP), F32), jax.ShapeDtypeStruct((N_HEADS, 1, 128), F32)],
        compiler_params=_params("parallel"),
    )(dband, after)


N_QB = S // QB
HEADS_PER_STEP = 4
ATT_LANES = HEADS_PER_STEP * HEAD
N_HG = D // ATT_LANES


def _band_mask():
    qc = lax.broadcasted_iota(jnp.int32, (QB, KB), 0) // CHUNK
    kc = lax.broadcasted_iota(jnp.int32, (QB, KB), 1) // CHUNK
    return (kc >= qc) & (kc <= qc + LEFT)


def _half_scale(hh, scale):
    lane = lax.broadcasted_iota(jnp.int32, (1, 128), 1)
    return jnp.where((lane < HEAD) == (hh == 0), scale, 0.0).astype(BF16)


def _probs(qh, kb, bias, first_key):
    sc = _nt(qh, kb) + bias
    if first_key is not None:
        sc = jnp.where(lax.broadcasted_iota(jnp.int32, (QB, KB), 1) >= first_key, sc, NEG_INF)
    e = jnp.exp(sc - jnp.max(sc, axis=-1, keepdims=True))
    return e * (1.0 / jnp.sum(e, axis=-1, keepdims=True))


def _by_padding(cb, compute):
    @pl.when(cb < PADK // QB)
    def _():
        compute(PADK - cb * QB)

    @pl.when(cb >= PADK // QB)
    def _():
        compute(None)


def _attn_fwd(q, kp, vp, bias, ex=None):
    def body(q_ref, k_ref, v_ref, b_ref, o_ref):
        cb = pl.program_id(1)
        band = pl.ds(pl.multiple_of(cb * QB, QB), KB)
        low = lax.broadcasted_iota(jnp.int32, (QB, 128), 1) < HEAD

        def compute(first_key):
            for pair in range(HEADS_PER_STEP // 2):
                lanes = pl.ds(pair * 128, 128)
                kb, vb, qv = k_ref[band, lanes], v_ref[band, lanes], q_ref[:, lanes]
                outs = []
                for hh in range(2):
                    pb = _probs(qv * _half_scale(hh, ATTN_SCALE), kb, b_ref[2 * pair + hh], first_key).astype(BF16)
                    outs.append(_nn(pb, vb))
                o_ref[:, lanes] = jnp.where(low, outs[0], outs[1]).astype(BF16)

        _by_padding(cb, compute)

    qspec = pl.BlockSpec((QB, ATT_LANES), lambda hg, cb: (cb, hg))
    kspec = pl.BlockSpec((PADK + S, ATT_LANES), lambda hg, cb: (0, hg))
    return _call_hosting(
        body, ex, _grid_phases((N_HG, N_QB), 0.85), name="attn_fwd", grid=(N_HG, N_QB),
        in_specs=[qspec, kspec, kspec, pl.BlockSpec((HEADS_PER_STEP, QB, KB), lambda hg, cb: (hg, 0, 0))],
        out_specs=[qspec], out_shape=[jax.ShapeDtypeStruct((S, D), BF16)], scratch_shapes=[],
        args=(q, kp, vp, bias), compiler_params=_params("arbitrary", "arbitrary"),
    )


def _attn_bwd(q, kp, vp, bias, do, ex=None):
    def body(q_ref, k_ref, v_ref, b_ref, do_ref, dq_ref, dk_ref, dv_ref, db_ref):
        cb = pl.program_id(1)

        @pl.when(cb == 0)
        def _():
            dk_ref[...] = jnp.zeros(dk_ref.shape, F32)
            dv_ref[...] = jnp.zeros(dv_ref.shape, F32)
            db_ref[...] = jnp.zeros(db_ref.shape, F32)

        band = pl.ds(pl.multiple_of(cb * QB, QB), KB)
        low = lax.broadcasted_iota(jnp.int32, (QB, 128), 1) < HEAD

        def compute(first_key):
            for pair in range(HEADS_PER_STEP // 2):
                lanes = pl.ds(pair * 128, 128)
                kb, vb = k_ref[band, lanes], v_ref[band, lanes]
                qv, dov = q_ref[:, lanes], do_ref[:, lanes]
                dq = jnp.zeros((QB, 128), F32)
                dkb = jnp.zeros((KB, 128), F32)
                dvb = jnp.zeros((KB, 128), F32)
                for hh in range(2):
                    sel = low if hh == 0 else jnp.logical_not(low)
                    doh = dov * _half_scale(hh, 1.0)
                    p = _probs(qv * _half_scale(hh, ATTN_SCALE), kb, b_ref[2 * pair + hh], first_key)
                    dp = _nt(doh, vb)
                    dvb = dvb + _tn(p.astype(BF16), doh)
                    ds = p * (dp - jnp.sum(dp * p, axis=-1, keepdims=True))
                    db_ref[2 * pair + hh] += ds
                    dsb = (ds * ATTN_SCALE).astype(BF16)
                    dq = dq + jnp.where(sel, _nn(dsb, kb), 0.0)
                    dkb = dkb + _tn(dsb, qv * _half_scale(hh, 1.0))
                dq_ref[:, lanes] = dq
                dk_ref[band, lanes] += dkb
                dv_ref[band, lanes] += dvb

        _by_padding(cb, compute)

    qspec = pl.BlockSpec((QB, ATT_LANES), lambda hg, cb: (cb, hg))
    kspec = pl.BlockSpec((PADK + S, ATT_LANES), lambda hg, cb: (0, hg))
    bspec = pl.BlockSpec((HEADS_PER_STEP, QB, KB), lambda hg, cb: (hg, 0, 0))
    kf = jax.ShapeDtypeStruct((PADK + S, D), F32)
    return _call_hosting(
        body, ex, _grid_phases((N_HG, N_QB)), name="attn_bwd", grid=(N_HG, N_QB),
        in_specs=[qspec, kspec, kspec, bspec, qspec],
        out_specs=[qspec, kspec, kspec, bspec],
        out_shape=[jax.ShapeDtypeStruct((S, D), F32), kf, kf, jax.ShapeDtypeStruct((N_HEADS, QB, KB), F32)],
        scratch_shapes=[], args=(q, kp, vp, bias, do), compiler_params=_params("arbitrary", "arbitrary"),
    )


def _adam_math(w, g, m, v):
    m = ADAM_B1 * m + (1.0 - ADAM_B1) * g
    v = ADAM_B2 * v + (1.0 - ADAM_B2) * (g * g)
    m_hat = m / (1.0 - ADAM_B1 ** ADAM_STEP)
    v_hat = v / (1.0 - ADAM_B2 ** ADAM_STEP)
    delta = -ADAM_LR * (m_hat / (jnp.sqrt(v_hat) + ADAM_EPS) + ADAM_WD * w)
    return delta, m, v


def _row_tile(r):
    for cand in (256, 176, 128, 64, 32, 16, 8):
        if r % cand == 0:
            return cand
    return r


def _adam_layer(w, parts, m, v, layer, name, prev=None):
    nl, r, c = w.shape
    count = parts.shape[0]
    tr = _row_tile(r)

    def body(w_ref, g_ref, m_ref, v_ref, *rest):
        go_ref, d_ref, nm_ref, nv_ref = rest[-4:]
        gv = g_ref[0].astype(F32)
        for q in range(1, count):
            gv = gv + g_ref[q].astype(F32)
        delta, nm, nv = _adam_math(w_ref[0], gv, m_ref[0], v_ref[0])
        go_ref[0] = gv
        d_ref[0] = delta
        nm_ref[0] = nm
        nv_ref[0] = nv

    lspec = pl.BlockSpec((1, tr, c), lambda i: (layer, i, 0))
    sd = jax.ShapeDtypeStruct((nl, r, c), F32)
    extra = list(prev) if prev is not None else []
    return _call(
        body, name=name, grid=(r // tr,),
        in_specs=[lspec, pl.BlockSpec((count, tr, c), lambda i: (0, i, 0)), lspec, lspec] + [ANY] * len(extra),
        out_specs=[lspec] * 4, out_shape=[sd] * 4,
        input_output_aliases={4 + t: t for t in range(len(extra))},
        compiler_params=_params("parallel"),
    )(w, parts, m, v, *extra)


def _adam(w, g, m, v, name):
    r, c = w.shape
    tr = _row_tile(r)
    count = g.shape[0] if g.ndim == 3 else 0

    def body(w_ref, g_ref, m_ref, v_ref, go_ref, d_ref, nm_ref, nv_ref):
        if count:
            gv = g_ref[0].astype(F32)
            for q in range(1, count):
                gv = gv + g_ref[q].astype(F32)
        else:
            gv = g_ref[...]
        delta, nm, nv = _adam_math(w_ref[...], gv, m_ref[...], v_ref[...])
        go_ref[...] = gv
        d_ref[...] = delta
        nm_ref[...] = nm
        nv_ref[...] = nv

    spec = pl.BlockSpec((tr, c), lambda i: (i, 0))
    gspec = pl.BlockSpec((count, tr, c), lambda i: (0, i, 0)) if count else spec
    sd = jax.ShapeDtypeStruct((r, c), F32)
    return _call(
        body, name=name, grid=(r // tr,), in_specs=[spec, gspec, spec, spec], out_specs=[spec] * 4,
        out_shape=[sd] * 4, compiler_params=_params("parallel"),
    )(w, g, m, v)


def _pad16(rows):
    return jnp.pad(rows, ((0, 16 - rows.shape[0]), (0, 0)))


def kernel(x, norm_mix_g, norm_ffn_g, pool_w, pool_b, pool_scale, kv_norm_g, w_k, w_v, k_norm_g, w_q, q_norm_g, rel_bias, w_o, w_gate, w_up, w_down, loss_target, m_norm_mix_g, m_norm_ffn_g, m_pool_w, m_pool_b, m_pool_scale, m_kv_norm_g, m_w_k, m_w_v, m_k_norm_g, m_w_q, m_q_norm_g, m_rel_bias, m_w_o, m_w_gate, m_w_up, m_w_down, v_norm_mix_g, v_norm_ffn_g, v_pool_w, v_pool_b, v_pool_scale, v_kv_norm_g, v_w_k, v_w_v, v_k_norm_g, v_w_q, v_q_norm_g, v_rel_bias, v_w_o, v_w_gate, v_w_up, v_w_down):
    assert x.shape == (1, S, D) and w_gate.shape == (2, D, F_SHARD) and w_k.shape == (D_SHARD, D)
    xin, target = x[0], loss_target[0]

    ffn_shards = [[w_gate[layer].T.astype(BF16), w_up[layer].T.astype(BF16), w_down[layer].astype(BF16)]
                  for layer in range(2)]
    att_shards = [w_k.astype(BF16), w_v.astype(BF16), w_q[0].astype(BF16), w_o[0].astype(BF16)]
    pool_shard = pool_w[0].astype(BF16).reshape(N_GROUPS * POOL_SHARD, GROUP)
    small = jnp.concatenate([pool_b[0].reshape(1, N_GROUPS * POOL_SHARD), pool_scale], axis=1)

    full0 = _run_exchange(_relay_gather_exchange(ffn_shards[0] + [pool_shard, _pad16(small)]), "gather_layer0")
    ffn_w0 = [a.reshape(F, D) for a in full0[:3]]
    pw_f = full0[3].reshape(N_DEV, N_GROUPS, POOL_SHARD, GROUP).transpose(1, 0, 2, 3).reshape(N_GROUPS, GROUP, GROUP)
    small_f = full0[4][:, 0, :]
    pb_f = small_f[:, :N_GROUPS * POOL_SHARD].reshape(N_DEV, N_GROUPS, POOL_SHARD).transpose(1, 0, 2).reshape(1, D)
    ps_f = small_f[:, N_GROUPS * POOL_SHARD:].reshape(1, D)

    g_mix0, g_mix1 = norm_mix_g[0:1], norm_mix_g[1:2]
    g_ffn0, g_ffn1 = norm_ffn_g[0:1], norm_ffn_g[1:2]
    g_kv = kv_norm_g.reshape(1, D)
    gk_t = jnp.tile(k_norm_g.reshape(1, HEAD), (1, N_HEADS))
    gq_t = jnp.tile(q_norm_g.reshape(1, HEAD), (1, N_HEADS))

    x1, diff = _pool_fwd(xin, g_mix0, pw_f, pb_f, ps_f)
    (x2, hf0, gg0, uu0), full_att = _ffn_fwd(x1, g_ffn0, *ffn_w0, name="ffn_fwd", ex=_relay_gather_exchange(att_shards))
    wk_f, wv_f, wq_f, wo_f = [a.reshape(D, D) for a in full_att]
    (hkv, h1, kpre, qpre, kp, vp, qq), full1_gate = _kvq_proj(x2, g_kv, g_mix1, wk_f, wv_f, wq_f, gk_t, gq_t,
                                                                ex=_relay_gather_exchange(ffn_shards[1][:1]))
    bias = _bias_band(_toeplitz_from_table(rel_bias[0]))
    (att,), full1_rest = _attn_fwd(qq, kp, vp, bias, ex=_relay_gather_exchange(ffn_shards[1][1:]))
    ffn_w1 = [a.reshape(F, D) for a in full1_gate + full1_rest]
    x3 = _mm(att, wo_f, "nn", F32, "attn_out", add=x2)
    (dx4, loss_rows, hf1, gg1, uu1), _ = _ffn_fwd(x3, g_ffn1, *ffn_w1, name="ffn_fwd_loss", target=target)

    def blocks(dw):
        return dw.reshape(N_DEV, dw.shape[0] // N_DEV, dw.shape[1])

    (dgg1, duu1, dwg1, dwu1, dwd1), _ = _ffn_bwd_weights(dx4, hf1, gg1, uu1, ffn_w1[2], name="ffn_bwd1")
    (dx3, dg_ffn1), _ = _ffn_bwd_input(dx4, x3, g_ffn1, [dgg1, duu1], ffn_w1[:2], "ffn_dx1")
    datt = _mm(dx3, wo_f, "nt", BF16, "d_attn")
    dwo = _mm(att, dx3, "tn", BF16, "d_wo")
    parts1 = [blocks(dw) for dw in (dwg1, dwu1, dwd1, dwo)]
    (dq, dkp, dvp, dband), stage1 = _attn_bwd(qq, kp, vp, bias, datt, ex=_pair_exchange(parts1))
    both1 = _pair_add(parts1, stage1, "pair_add_ffn1")
    flight1, chip1, land1, _ = _chip_exchange_start(both1[:4], both1[4:], dq, "scatter_ffn1_start")
    (dx2, dqpre, dkpre, dg_mix1, dg_kv, dgq, dgk), _ = _qkv_input_bwd(
        dq, qpre, gq_t, dkp, kpre, gk_t, dvp, wq_f, wk_f, wv_f, dx3, x2, g_mix1, g_kv)
    dwq = _mm(h1, dqpre, "tn", BF16, "d_wq")
    dwk = _mm(hkv, dkpre, "tn", BF16, "d_wk")
    dwv = _mm(hkv, dvp, "tn", BF16, "d_wv", skip_rows=PADK)
    parts_att = [blocks(dw) for dw in (dwk, dwv, dwq)]
    (dgg0, duu0, *dw0), stage_att = _ffn_bwd_weights(
        dx2, hf0, gg0, uu0, ffn_w0[2], name="ffn_bwd0", ex=_pair_exchange(parts_att))
    both_att = _pair_add(parts_att, stage_att, "pair_add_att")
    flight_att, chip_att, land_att, token_att = _chip_exchange_start(both_att[:3], both_att[3:], dx2,
                                                                     "scatter_att_start")
    parts0 = [blocks(dw) for dw in dw0]
    (dx1, dg_ffn0), stage0 = _ffn_bwd_input(dx2, x1, g_ffn0, [dgg0, duu0], ffn_w0[:2], "ffn_dx0",
                                            ex=_pair_exchange(parts0), after=[token_att])
    both = _pair_add(parts0, stage0, "pair_add_ffn0")
    flight, chip0, land0, token = _chip_exchange_start(both[:3], both[3:], dx1, "scatter_ffn0_start")
    (grad_x, dpw, db_rows, ds_rows, dg_mix0), _ = _pool_bwd(dx1, xin, diff, g_mix0, pw_f, pb_f, ps_f,
                                                            after=[token])
    dtp, seg = _bias_grad(dband, dpw)

    weights = dict(norm_mix_g=norm_mix_g, norm_ffn_g=norm_ffn_g, pool_w=pool_w, pool_b=pool_b,
                   pool_scale=pool_scale, kv_norm_g=kv_norm_g, w_k=w_k, w_v=w_v, k_norm_g=k_norm_g, w_q=w_q,
                   q_norm_g=q_norm_g, rel_bias=rel_bias, w_o=w_o, w_gate=w_gate, w_up=w_up, w_down=w_down)
    mom1 = dict(norm_mix_g=m_norm_mix_g, norm_ffn_g=m_norm_ffn_g, pool_w=m_pool_w, pool_b=m_pool_b,
                pool_scale=m_pool_scale, kv_norm_g=m_kv_norm_g, w_k=m_w_k, w_v=m_w_v, k_norm_g=m_k_norm_g,
                w_q=m_w_q, q_norm_g=m_q_norm_g, rel_bias=m_rel_bias, w_o=m_w_o, w_gate=m_w_gate, w_up=m_w_up,
                w_down=m_w_down)
    mom2 = dict(norm_mix_g=v_norm_mix_g, norm_ffn_g=v_norm_ffn_g, pool_w=v_pool_w, pool_b=v_pool_b,
                pool_scale=v_pool_scale, kv_norm_g=v_kv_norm_g, w_k=v_w_k, w_v=v_w_v, k_norm_g=v_k_norm_g,
                w_q=v_w_q, q_norm_g=v_q_norm_g, rel_bias=v_rel_bias, w_o=v_w_o, w_gate=v_w_gate, w_up=v_w_up,
                w_down=v_w_down)
    names = list(weights)
    grads, deltas, new_m, new_v = {}, {}, {}, {}

    def adam_flat(nm, parts):
        shape = weights[nm].shape
        flat = lambda a: a.reshape(-1, shape[-1])
        done = _adam(flat(weights[nm]), parts, flat(mom1[nm]), flat(mom2[nm]), "adam_" + nm)
        grads[nm], deltas[nm], new_m[nm], new_v[nm] = [a.reshape(shape) for a in done]

    ffn_names = ("w_gate", "w_up", "w_down")
    def ffn_view(nm, a):
        return a if nm == "w_down" else a.transpose(0, 2, 1)

    recv1 = _chip_exchange_wait(flight1, chip1, land1, dtp, "scatter_ffn1_wait")
    layer1 = {nm: _adam_layer(ffn_view(nm, weights[nm]), g, ffn_view(nm, mom1[nm]), ffn_view(nm, mom2[nm]), 1,
                              "adam1_" + nm)
              for nm, g in zip(ffn_names, recv1)}
    adam_flat("w_o", recv1[3])
    done1 = [layer1[nm][1] for nm in ffn_names] + [deltas["w_o"]]
    recv_att = _chip_exchange_wait(flight_att, chip_att, land_att, done1, "scatter_att_wait")
    for nm, parts in zip(("w_k", "w_v", "w_q"), recv_att):
        adam_flat(nm, parts)

    dpw_blocks = dpw.reshape(N_GROUPS, N_DEV, POOL_SHARD, GROUP).transpose(1, 0, 2, 3)
    dpw_blocks = dpw_blocks.reshape(N_DEV, N_GROUPS * POOL_SHARD, GROUP).astype(BF16)
    misc = jnp.concatenate([dgk[0:1, 0:HEAD], dgq[0:1, 0:HEAD], loss_rows[0:1, 0:1],
                            seg[:, 0, 0].reshape(1, N_HEADS), seg[:, 0, 1].reshape(1, N_HEADS)], axis=1)
    misc = jnp.pad(misc, ((0, 0), (0, D - misc.shape[1])))
    vec_rows = jnp.concatenate([dg_mix0[0:1], dg_mix1[0:1], dg_ffn0[0:1], dg_ffn1[0:1], dg_kv[0:1],
                                db_rows[0:1], ds_rows[0:1], misc], axis=0)
    pack = jnp.concatenate([vec_rows, dtp.reshape(N_HEADS, TOEP)], axis=0)
    tot, recv_pool = _small_exchanges(pack, dpw_blocks, "small_exchanges",
                                      after=[deltas[nm] for nm in ("w_k", "w_v", "w_q")])

    loss = tot[7, 2 * HEAD]
    seg_tot = jnp.stack([tot[7, 2 * HEAD + 1:2 * HEAD + 1 + N_HEADS],
                         tot[7, 2 * HEAD + 1 + N_HEADS:2 * HEAD + 1 + 2 * N_HEADS]], axis=1)
    me = 4 * lax.axis_index("x") + 2 * lax.axis_index("y") + lax.axis_index("c")
    g_pool_b = lax.dynamic_slice_in_dim(tot[5].reshape(N_GROUPS, GROUP), me * POOL_SHARD, POOL_SHARD, axis=1)
    grads.update(
        norm_mix_g=tot[0:2], norm_ffn_g=tot[2:4], kv_norm_g=tot[4], k_norm_g=tot[7, 0:HEAD],
        q_norm_g=tot[7, HEAD:2 * HEAD].reshape(1, HEAD),
        rel_bias=_table_grad_from_toeplitz(tot[8:8 + N_HEADS], seg_tot).reshape(1, N_HEADS, N_REL),
        pool_b=g_pool_b.reshape(1, N_GROUPS, POOL_SHARD),
        pool_scale=lax.dynamic_slice_in_dim(tot[6:7], me * D_SHARD, D_SHARD, axis=1))
    adam_flat("pool_w", recv_pool)
    small_names = [nm for nm in names if nm not in ffn_names + ("w_k", "w_v", "w_q", "w_o", "pool_w")]

    def pack_small(tree):
        cols = []
        for nm in small_names:
            flat = tree[nm].reshape(-1)
            cols.append(jnp.pad(flat, (0, -flat.shape[0] % 1024)))
        return jnp.concatenate(cols).reshape(-1, 128)

    _, dl, m1, m2 = _adam(pack_small(weights), pack_small(grads), pack_small(mom1), pack_small(mom2), "adam_small")

    def unpack_small(packed, out):
        flat, off = packed.reshape(-1), 0
        for nm in small_names:
            size = weights[nm].size
            out[nm] = flat[off:off + size].reshape(weights[nm].shape)
            off += size + (-size % 1024)

    unpack_small(dl, deltas)
    unpack_small(m1, new_m)
    unpack_small(m2, new_v)

    recv0 = _chip_exchange_wait(flight, chip0, land0, dl, "scatter_ffn0_wait")
    for nm, g in zip(ffn_names, recv0):
        done = _adam_layer(ffn_view(nm, weights[nm]), g, ffn_view(nm, mom1[nm]), ffn_view(nm, mom2[nm]), 0,
                           "adam0_" + nm, prev=layer1[nm])
        grads[nm], deltas[nm], new_m[nm], new_v[nm] = [ffn_view(nm, a) for a in done]

    return (loss, grad_x[None], *[grads[nm] for nm in names], *[deltas[nm] for nm in names],
            *[new_m[nm] for nm in names], *[new_v[nm] for nm in names])
```
